```python
import jax, jax.numpy as jnp
from jax import lax
import numpy as np

D_MODEL = 1024
BATCH = 16
SEQ = 2048
DEPTH = 1

N_META = 16
CHUNK = 64
HG_HEADS = 4
HG_DK = 128
HG_DV = 128
GDN_HEADS = 4
GDN_DK = 128
GDN_DV = 128
CONV_W = 4
HG_WIDTH = HG_HEADS * HG_DK
GDN_WIDTH = GDN_HEADS * GDN_DV
D_MIX = HG_WIDTH + GDN_WIDTH
IN_COLS = 4 * HG_WIDTH + 4 * GDN_WIDTH + 2 * GDN_HEADS
EPS = 1e-6

kernel_name = "hymba_hgrn2_gated_deltanet_layer"


def rmsnorm(x, w):
    xf = x.astype(jnp.float32)
    y = xf * lax.rsqrt(jnp.mean(xf * xf, axis=-1, keepdims=True) + EPS)
    return (y * w.astype(jnp.float32)).astype(x.dtype)


def gated_rmsnorm(o, z, w):
    of = jnp.transpose(o, (0, 2, 1, 3)).astype(jnp.float32)
    y = of * lax.rsqrt(jnp.mean(of * of, axis=-1, keepdims=True) + EPS) * w.astype(jnp.float32)
    B, T = y.shape[0], y.shape[1]
    return (y.reshape(B, T, -1) * jax.nn.silu(z.astype(jnp.float32))).astype(z.dtype)


def heads(a, H):
    B, T, _ = a.shape
    return jnp.transpose(a.reshape(B, T, H, -1), (0, 2, 1, 3))


def causal_conv(x, w):
    C = x.shape[-1]
    return lax.conv_general_dilated(x, w[:, None, :].astype(x.dtype), window_strides=(1,),
                                    padding=((CONV_W - 1, 0),),
                                    dimension_numbers=('NWC', 'WIO', 'NWC'),
                                    feature_group_count=C)


def to_chunks(a, C):
    B, H, T = a.shape[:3]
    return jnp.moveaxis(a.reshape((B, H, T // C, C) + a.shape[3:]), 2, 0)


def from_chunks(o):
    nc, B, H, C, d = o.shape
    return jnp.moveaxis(o, 0, 2).reshape(B, H, nc * C, d)


def run_chunked(step, inputs, S0, C):
    xs = tuple(to_chunks(a, C) for a in inputs)
    S, o = lax.scan(step, S0, xs)
    return S, from_chunks(o)


def causal_mixer(step, inputs, S0):
    meta = tuple(a[:, :, :N_META] for a in inputs)
    real = tuple(a[:, :, N_META:] for a in inputs)
    S, o_meta = run_chunked(step, meta, S0, N_META)
    _, o_real = run_chunked(step, real, S, CHUNK)
    return jnp.concatenate([o_meta, o_real], axis=2)


def hgrn2_chunk(S, inp):
    q, k, v, logf = inp
    C = q.shape[2]
    G = jnp.cumsum(logf, axis=2)
    causal = jnp.tril(jnp.ones((C, C), dtype=bool))
    diff = G[:, :, :, None, :] - G[:, :, None, :, :]
    decay = jnp.exp(jnp.where(causal[None, None, :, :, None], diff, -jnp.inf))
    A = jnp.einsum('bhtd,bhtsd,bhsd->bhts', q, decay, k)
    o = jnp.einsum('bhts,bhsv->bhtv', A, v) + jnp.einsum('bhtd,bhdv->bhtv', q * jnp.exp(G), S)
    G_last = G[:, :, -1]
    S_new = jnp.exp(G_last)[..., None] * S + jnp.einsum(
        'bhsd,bhsv->bhdv', k * jnp.exp(G_last[:, :, None, :] - G), v)
    return S_new, o


def gdn_chunk(S, inp):
    q, k, v, g, beta = inp
    C = q.shape[2]
    dv = v.shape[-1]
    gam = jnp.cumsum(g, axis=-1)
    diff = gam[..., :, None] - gam[..., None, :]
    strict = jnp.tril(jnp.ones((C, C), dtype=bool), -1)
    incl = jnp.tril(jnp.ones((C, C), dtype=bool))
    dec_strict = jnp.exp(jnp.where(strict, diff, -jnp.inf))
    dec_incl = jnp.exp(jnp.where(incl, diff, -jnp.inf))
    A = beta[..., None] * jnp.einsum('bhtd,bhsd->bhts', k, k) * dec_strict
    lhs = jnp.eye(C, dtype=A.dtype) + A
    rhs = jnp.concatenate([beta[..., None] * v, (beta * jnp.exp(gam))[..., None] * k], axis=-1)
    sol = lax.linalg.triangular_solve(lhs, rhs, left_side=True, lower=True)
    U, W = sol[..., :dv], sol[..., dv:]
    u = U - jnp.einsum('bhtd,bhdv->bhtv', W, S)
    qk = jnp.einsum('bhtd,bhsd->bhts', q, k) * dec_incl
    o = jnp.einsum('bhtd,bhdv->bhtv', q * jnp.exp(gam)[..., None], S) + jnp.einsum('bhts,bhsv->bhtv', qk, u)
    g_last = gam[..., -1]
    S_new = jnp.exp(g_last)[..., None, None] * S + jnp.einsum(
        'bhsd,bhsv->bhdv', k * jnp.exp(g_last[..., None] - gam)[..., None], u)
    return S_new, o


def hybrid_layer(h, norm_w, w_in, conv_w, lower_bound, hg_norm_w, gdn_A_log, gdn_dt_bias, gdn_norm_w, w_out):
    B, T, _ = h.shape
    f32 = jnp.float32
    u = rmsnorm(h, norm_w)
    proj = jnp.einsum('btd,dc->btc', u, w_in)
    cuts = np.cumsum([HG_WIDTH, HG_WIDTH, HG_WIDTH, HG_WIDTH, 3 * GDN_WIDTH, GDN_WIDTH, GDN_HEADS])
    hg_q, hg_f, hg_i, hg_z, gd_qkv, gd_z, gd_a, gd_b = jnp.split(proj, [int(c) for c in cuts], axis=-1)

    q = heads(jax.nn.silu(hg_q.astype(f32)), HG_HEADS)
    lb = lower_bound.astype(f32)
    f = lb + (1.0 - lb) * jax.nn.sigmoid(hg_f.astype(f32))
    k = heads(1.0 - f, HG_HEADS)
    logf = heads(jnp.log(f), HG_HEADS)
    v = heads(hg_i.astype(f32), HG_HEADS)
    S0 = jnp.zeros((B, HG_HEADS, HG_DK, HG_DV), f32)
    o_hg = causal_mixer(hgrn2_chunk, (q, k, v, logf), S0)
    y_hg = gated_rmsnorm(o_hg, hg_z, hg_norm_w)

    qkv = jax.nn.silu(causal_conv(gd_qkv, conv_w).astype(f32))
    gq, gk, gv = jnp.split(qkv, 3, axis=-1)
    gq = heads(gq, GDN_HEADS)
    gk = heads(gk, GDN_HEADS)
    gv = heads(gv, GDN_HEADS)
    gq = gq * lax.rsqrt(jnp.sum(gq * gq, -1, keepdims=True) + EPS) * (GDN_DK ** -0.5)
    gk = gk * lax.rsqrt(jnp.sum(gk * gk, -1, keepdims=True) + EPS)
    g = -jnp.exp(gdn_A_log.astype(f32)) * jax.nn.softplus(gd_a.astype(f32) + gdn_dt_bias.astype(f32))
    g = jnp.transpose(g, (0, 2, 1))
    beta = jnp.transpose(jax.nn.sigmoid(gd_b.astype(f32)), (0, 2, 1))
    S0g = jnp.zeros((B, GDN_HEADS, GDN_DK, GDN_DV), f32)
    o_gd = causal_mixer(gdn_chunk, (gq, gk, gv, g, beta), S0g)
    y_gd = gated_rmsnorm(o_gd, gd_z, gdn_norm_w)

    y = jnp.concatenate([y_hg, y_gd], axis=-1)
    return jnp.einsum('btc,cd->btd', y, w_out).astype(h.dtype)


def _fwd_setup_inputs(seed: int = 0) -> dict:
    key = jax.random.key(seed)
    ks = jax.random.split(key, 14)
    f32 = jnp.float32
    x = jax.random.normal(ks[0], (BATCH, SEQ, D_MODEL), f32)
    meta_tokens = jax.random.normal(ks[1], (N_META, D_MODEL), f32)
    norm_w = 1.0 + 0.02 * jax.random.normal(ks[2], (DEPTH, D_MODEL), f32)
    w_in = jax.random.normal(ks[3], (DEPTH, D_MODEL, IN_COLS), f32) * D_MODEL ** -0.5
    conv_w = jax.random.normal(ks[4], (DEPTH, CONV_W, 3 * GDN_WIDTH), f32) * CONV_W ** -0.5
    hg_lb_logits = 0.5 * jax.random.normal(ks[5], (DEPTH + 1, HG_WIDTH), f32)
    hg_norm_w = 1.0 + 0.02 * jax.random.normal(ks[6], (DEPTH, HG_DV), f32)
    gdn_A_log = jnp.log(jax.random.uniform(ks[7], (DEPTH, GDN_HEADS), f32, 1.0, 16.0))
    dt = jnp.exp(jax.random.uniform(ks[8], (DEPTH, GDN_HEADS), f32, jnp.log(0.001), jnp.log(0.1)))
    gdn_dt_bias = dt + jnp.log(-jnp.expm1(-dt))
    gdn_norm_w = 1.0 + 0.02 * jax.random.normal(ks[9], (DEPTH, GDN_DV), f32)
    w_out = jax.random.normal(ks[10], (DEPTH, D_MIX, D_MODEL), f32) * D_MIX ** -0.5
    final_norm_w = 1.0 + 0.02 * jax.random.normal(ks[11], (D_MODEL,), f32)
    return {"x": x, "meta_tokens": meta_tokens, "norm_w": norm_w, "w_in": w_in, "conv_w": conv_w,
            "hg_lb_logits": hg_lb_logits, "hg_norm_w": hg_norm_w, "gdn_A_log": gdn_A_log,
            "gdn_dt_bias": gdn_dt_bias, "gdn_norm_w": gdn_norm_w, "w_out": w_out,
            "final_norm_w": final_norm_w}


def _fwd_reference(x, meta_tokens, norm_w, w_in, conv_w, hg_lb_logits, hg_norm_w, gdn_A_log,
              gdn_dt_bias, gdn_norm_w, w_out, final_norm_w):
    B = x.shape[0]
    meta = jnp.broadcast_to(meta_tokens[None].astype(x.dtype), (B, N_META, D_MODEL))
    h = jnp.concatenate([meta, x], axis=1)
    lower_bounds = jnp.cumsum(jax.nn.softmax(hg_lb_logits.astype(jnp.float32), axis=0), axis=0)
    for l in range(DEPTH):
        h = h + hybrid_layer(h, norm_w[l], w_in[l], conv_w[l], lower_bounds[l], hg_norm_w[l],
                             gdn_A_log[l], gdn_dt_bias[l], gdn_norm_w[l], w_out[l])
    y = rmsnorm(h, final_norm_w)
    return y[:, N_META:]


import jax as _jax
import jax.numpy as _jnp

TWIN_FORMAT = 'train_step'
FWD_PARAMS = ['x', 'meta_tokens', 'norm_w', 'w_in', 'conv_w', 'hg_lb_logits', 'hg_norm_w', 'gdn_A_log', 'gdn_dt_bias', 'gdn_norm_w', 'w_out', 'final_norm_w']
TWIN_WEIGHTS = ['meta_tokens', 'norm_w', 'w_in', 'conv_w', 'hg_lb_logits', 'hg_norm_w', 'gdn_A_log', 'gdn_dt_bias', 'gdn_norm_w', 'w_out', 'final_norm_w']
TWIN_DIFF_INPUT = 'x'
TWIN_INPUTS = ['x', 'meta_tokens', 'norm_w', 'w_in', 'conv_w', 'hg_lb_logits', 'hg_norm_w', 'gdn_A_log', 'gdn_dt_bias', 'gdn_norm_w', 'w_out', 'final_norm_w', 'loss_target', 'm_meta_tokens', 'm_norm_w', 'm_w_in', 'm_conv_w', 'm_hg_lb_logits', 'm_hg_norm_w', 'm_gdn_A_log', 'm_gdn_dt_bias', 'm_gdn_norm_w', 'm_w_out', 'm_final_norm_w', 'v_meta_tokens', 'v_norm_w', 'v_w_in', 'v_conv_w', 'v_hg_lb_logits', 'v_hg_norm_w', 'v_gdn_A_log', 'v_gdn_dt_bias', 'v_gdn_norm_w', 'v_w_out', 'v_final_norm_w']
TWIN_OUTPUTS = ['loss', 'grad_x', 'grad_meta_tokens', 'grad_norm_w', 'grad_w_in', 'grad_conv_w', 'grad_hg_lb_logits', 'grad_hg_norm_w', 'grad_gdn_A_log', 'grad_gdn_dt_bias', 'grad_gdn_norm_w', 'grad_w_out', 'grad_final_norm_w', 'delta_meta_tokens', 'delta_norm_w', 'delta_w_in', 'delta_conv_w', 'delta_hg_lb_logits', 'delta_hg_norm_w', 'delta_gdn_A_log', 'delta_gdn_dt_bias', 'delta_gdn_norm_w', 'delta_w_out', 'delta_final_norm_w', 'new_m_meta_tokens', 'new_m_norm_w', 'new_m_w_in', 'new_m_conv_w', 'new_m_hg_lb_logits', 'new_m_hg_norm_w', 'new_m_gdn_A_log', 'new_m_gdn_dt_bias', 'new_m_gdn_norm_w', 'new_m_w_out', 'new_m_final_norm_w', 'new_v_meta_tokens', 'new_v_norm_w', 'new_v_w_in', 'new_v_conv_w', 'new_v_hg_lb_logits', 'new_v_hg_norm_w', 'new_v_gdn_A_log', 'new_v_gdn_dt_bias', 'new_v_gdn_norm_w', 'new_v_w_out', 'new_v_final_norm_w']
TWIN_LEAF_KINDS = {'loss': 'loss', 'grad_x': 'grad_x', 'grad_meta_tokens': 'grad_w', 'grad_norm_w': 'grad_w', 'grad_w_in': 'grad_w', 'grad_conv_w': 'grad_w', 'grad_hg_lb_logits': 'grad_w', 'grad_hg_norm_w': 'grad_w', 'grad_gdn_A_log': 'grad_w', 'grad_gdn_dt_bias': 'grad_w', 'grad_gdn_norm_w': 'grad_w', 'grad_w_out': 'grad_w', 'grad_final_norm_w': 'grad_w', 'delta_meta_tokens': 'delta_w', 'delta_norm_w': 'delta_w', 'delta_w_in': 'delta_w', 'delta_conv_w': 'delta_w', 'delta_hg_lb_logits': 'delta_w', 'delta_hg_norm_w': 'delta_w', 'delta_gdn_A_log': 'delta_w', 'delta_gdn_dt_bias': 'delta_w', 'delta_gdn_norm_w': 'delta_w', 'delta_w_out': 'delta_w', 'delta_final_norm_w': 'delta_w', 'new_m_meta_tokens': 'new_m', 'new_m_norm_w': 'new_m', 'new_m_w_in': 'new_m', 'new_m_conv_w': 'new_m', 'new_m_hg_lb_logits': 'new_m', 'new_m_hg_norm_w': 'new_m', 'new_m_gdn_A_log': 'new_m', 'new_m_gdn_dt_bias': 'new_m', 'new_m_gdn_norm_w': 'new_m', 'new_m_w_out': 'new_m', 'new_m_final_norm_w': 'new_m', 'new_v_meta_tokens': 'new_v', 'new_v_norm_w': 'new_v', 'new_v_w_in': 'new_v', 'new_v_conv_w': 'new_v', 'new_v_hg_lb_logits': 'new_v', 'new_v_hg_norm_w': 'new_v', 'new_v_gdn_A_log': 'new_v', 'new_v_gdn_dt_bias': 'new_v', 'new_v_gdn_norm_w': 'new_v', 'new_v_w_out': 'new_v', 'new_v_final_norm_w': 'new_v'}


def _forward(args):
    return _fwd_reference(*[args[k] for k in FWD_PARAMS])


def _output_shape():
    out = _jax.eval_shape(lambda: _forward(_fwd_setup_inputs(0)))
    return out.shape, out.dtype

N_MICROBATCH = 1
ADAM_LR = 0.001
ADAM_B1 = 0.9
ADAM_B2 = 0.999
ADAM_EPS = 1e-08
ADAM_WD = 0.01
ADAM_STEP = 10
PER_EXAMPLE_BATCH_AXIS = {'x': 0, 'loss_target': 0}
SHARED_INPUTS = []
_WEIGHT_DTYPES = {'meta_tokens': _jnp.float32, 'norm_w': _jnp.float32, 'w_in': _jnp.float32, 'conv_w': _jnp.float32, 'hg_lb_logits': _jnp.float32, 'hg_norm_w': _jnp.float32, 'gdn_A_log': _jnp.float32, 'gdn_dt_bias': _jnp.float32, 'gdn_norm_w': _jnp.float32, 'w_out': _jnp.float32, 'final_norm_w': _jnp.float32}
MOMENT_SCALE = {'meta_tokens': 4.436917e-03, 'norm_w': 1.452731e-01, 'w_in': 7.242770e-02, 'conv_w': 7.549213e-02, 'hg_lb_logits': 9.142069e-03, 'hg_norm_w': 1.948264e-01, 'gdn_A_log': 8.687794e-02, 'gdn_dt_bias': 8.767929e-02, 'gdn_norm_w': 2.446253e-01, 'w_out': 9.063652e-02, 'final_norm_w': 3.201140e+01}


def _to_microbatches(a, axis):
    t = _jnp.moveaxis(a, axis, 0)
    t = t.reshape((N_MICROBATCH, t.shape[0] // N_MICROBATCH) + t.shape[1:])
    return _jnp.moveaxis(t, 1, axis + 1)


def setup_inputs(seed: int = 0) -> dict:
    inp = _fwd_setup_inputs(seed)
    key = _jax.random.fold_in(_jax.random.key(seed), 7919)
    shape, _ = _output_shape()
    out = dict(inp)
    out["loss_target"] = _jax.random.normal(_jax.random.fold_in(key, 0), shape, _jnp.float32)
    for i, name in enumerate(TWIN_WEIGHTS):
        w = inp[name].astype(_jnp.float32)
        if MOMENT_SCALE is None:
            s = _jnp.sqrt(_jnp.mean(_jnp.square(w)) + 1e-30)
        else:
            s = MOMENT_SCALE[name]
        km, kv = _jax.random.split(_jax.random.fold_in(key, i + 1))
        out[name] = w
        out["m_" + name] = s * _jax.random.normal(km, w.shape, _jnp.float32)
        out["v_" + name] = (s * s) * _jax.random.uniform(kv, w.shape, _jnp.float32, 0.5, 1.5)
    if N_MICROBATCH > 1:
        for name, axis in PER_EXAMPLE_BATCH_AXIS.items():
            out[name] = _to_microbatches(out[name], axis)
    return {'x': out['x'], 'meta_tokens': out['meta_tokens'], 'norm_w': out['norm_w'], 'w_in': out['w_in'], 'conv_w': out['conv_w'], 'hg_lb_logits': out['hg_lb_logits'], 'hg_norm_w': out['hg_norm_w'], 'gdn_A_log': out['gdn_A_log'], 'gdn_dt_bias': out['gdn_dt_bias'], 'gdn_norm_w': out['gdn_norm_w'], 'w_out': out['w_out'], 'final_norm_w': out['final_norm_w'], 'loss_target': out['loss_target'], 'm_meta_tokens': out['m_meta_tokens'], 'm_norm_w': out['m_norm_w'], 'm_w_in': out['m_w_in'], 'm_conv_w': out['m_conv_w'], 'm_hg_lb_logits': out['m_hg_lb_logits'], 'm_hg_norm_w': out['m_hg_norm_w'], 'm_gdn_A_log': out['m_gdn_A_log'], 'm_gdn_dt_bias': out['m_gdn_dt_bias'], 'm_gdn_norm_w': out['m_gdn_norm_w'], 'm_w_out': out['m_w_out'], 'm_final_norm_w': out['m_final_norm_w'], 'v_meta_tokens': out['v_meta_tokens'], 'v_norm_w': out['v_norm_w'], 'v_w_in': out['v_w_in'], 'v_conv_w': out['v_conv_w'], 'v_hg_lb_logits': out['v_hg_lb_logits'], 'v_hg_norm_w': out['v_hg_norm_w'], 'v_gdn_A_log': out['v_gdn_A_log'], 'v_gdn_dt_bias': out['v_gdn_dt_bias'], 'v_gdn_norm_w': out['v_gdn_norm_w'], 'v_w_out': out['v_w_out'], 'v_final_norm_w': out['v_final_norm_w']}


def _loss(weights, diff, rest, loss_target):
    with _jax.named_scope("forward"):
        args = {**rest, TWIN_DIFF_INPUT: diff, **{k: w.astype(_WEIGHT_DTYPES[k]) for k, w in weights.items()}}
        y = _forward(args)
    with _jax.named_scope("loss_head"):
        err = _jnp.square(y.astype(_jnp.float32) - loss_target)
        return 0.5 * _jnp.sum(_jnp.mean(err, axis=-1)) if err.ndim else 0.5 * err


def _adamw(w, g, m, v):
    m = ADAM_B1 * m + (1.0 - ADAM_B1) * g
    v = ADAM_B2 * v + (1.0 - ADAM_B2) * _jnp.square(g)
    m_hat = m / (1.0 - ADAM_B1 ** ADAM_STEP)
    v_hat = v / (1.0 - ADAM_B2 ** ADAM_STEP)
    delta = -ADAM_LR * (m_hat / (_jnp.sqrt(v_hat) + ADAM_EPS) + ADAM_WD * w)
    return delta, m, v


def reference(x, meta_tokens, norm_w, w_in, conv_w, hg_lb_logits, hg_norm_w, gdn_A_log, gdn_dt_bias, gdn_norm_w, w_out, final_norm_w, loss_target, m_meta_tokens, m_norm_w, m_w_in, m_conv_w, m_hg_lb_logits, m_hg_norm_w, m_gdn_A_log, m_gdn_dt_bias, m_gdn_norm_w, m_w_out, m_final_norm_w, v_meta_tokens, v_norm_w, v_w_in, v_conv_w, v_hg_lb_logits, v_hg_norm_w, v_gdn_A_log, v_gdn_dt_bias, v_gdn_norm_w, v_w_out, v_final_norm_w):
    given = dict(x=x, meta_tokens=meta_tokens, norm_w=norm_w, w_in=w_in, conv_w=conv_w, hg_lb_logits=hg_lb_logits, hg_norm_w=hg_norm_w, gdn_A_log=gdn_A_log, gdn_dt_bias=gdn_dt_bias, gdn_norm_w=gdn_norm_w, w_out=w_out, final_norm_w=final_norm_w, loss_target=loss_target, m_meta_tokens=m_meta_tokens, m_norm_w=m_norm_w, m_w_in=m_w_in, m_conv_w=m_conv_w, m_hg_lb_logits=m_hg_lb_logits, m_hg_norm_w=m_hg_norm_w, m_gdn_A_log=m_gdn_A_log, m_gdn_dt_bias=m_gdn_dt_bias, m_gdn_norm_w=m_gdn_norm_w, m_w_out=m_w_out, m_final_norm_w=m_final_norm_w, v_meta_tokens=v_meta_tokens, v_norm_w=v_norm_w, v_w_in=v_w_in, v_conv_w=v_conv_w, v_hg_lb_logits=v_hg_lb_logits, v_hg_norm_w=v_hg_norm_w, v_gdn_A_log=v_gdn_A_log, v_gdn_dt_bias=v_gdn_dt_bias, v_gdn_norm_w=v_gdn_norm_w, v_w_out=v_w_out, v_final_norm_w=v_final_norm_w)
    weights = {n: given[n] for n in TWIN_WEIGHTS}
    shared = {n: given[n] for n in SHARED_INPUTS}
    per_example = {n: given[n] for n in ['x']}
    grad_fn = _jax.value_and_grad(_loss, argnums=(0, 1))

    def one_microbatch(ex, loss_target):
        ex = dict(ex)
        diff = ex.pop(TWIN_DIFF_INPUT)
        return grad_fn(weights, diff, {**shared, **ex}, loss_target)

    if N_MICROBATCH == 1:
        loss, (grad_w, grad_x) = one_microbatch(per_example, given["loss_target"])
    else:
        def body(carry, xs):
            loss_sum, grad_sum = carry
            l_k, (gw_k, gx_k) = one_microbatch(xs[0], xs[1])
            with _jax.named_scope("update"):
                return (loss_sum + l_k, _jax.tree.map(_jnp.add, grad_sum, gw_k)), gx_k

        init = (_jnp.zeros((), _jnp.float32), _jax.tree.map(_jnp.zeros_like, weights))
        (loss, grad_w), grad_x = _jax.lax.scan(body, init, (per_example, given["loss_target"]))
    with _jax.named_scope("update"):
        delta_w, new_m, new_v = {}, {}, {}
        for n in TWIN_WEIGHTS:
            delta_w[n], new_m[n], new_v[n] = _adamw(weights[n], grad_w[n], given["m_" + n], given["v_" + n])
    return (loss, grad_x, *[grad_w[n] for n in TWIN_WEIGHTS], *[delta_w[n] for n in TWIN_WEIGHTS],
            *[new_m[n] for n in TWIN_WEIGHTS], *[new_v[n] for n in TWIN_WEIGHTS])
```

```python
import functools

import jax
import jax.numpy as jnp
from jax import lax
from jax.experimental import pallas as pl
from jax.experimental.pallas import tpu as pltpu

f32 = jnp.float32
bf16 = jnp.bfloat16
HI = lax.Precision.HIGHEST
MESH = pl.DeviceIdType.MESH
ANY = pl.BlockSpec(memory_space=pl.ANY)

D = 1024
NB = 2
N_META = 16
SEQ = 2048
PAD = 48
TP = PAD + N_META + SEQ
C = 64
NCH = TP // C
N = NB * TP
H = 4
DK = 128
HW = 3 * DK
PC = 4224
IN_COLS = 4104
SHARD_COLS = IN_COLS // 4
EPS = 1e-6
ADAM_LR, ADAM_B1, ADAM_B2, ADAM_EPS, ADAM_WD, ADAM_STEP = 0.001, 0.9, 0.999, 1e-08, 0.01, 10
VMEM_LIMIT = 56 * 1024 * 1024


def _cp(sem=None, **kw):
    return pltpu.CompilerParams(dimension_semantics=sem, vmem_limit_bytes=VMEM_LIMIT, **kw)


def mm(a, b):
    return jnp.dot(a, b, precision=HI, preferred_element_type=f32)


def mm_nt(a, b):
    return lax.dot_general(a, b, (((1,), (1,)), ((), ())), precision=HI, preferred_element_type=f32)


def mm_tn(a, b):
    return lax.dot_general(a, b, (((0,), (0,)), ((), ())), precision=HI, preferred_element_type=f32)


def bdot(a, b):
    return jnp.dot(a.astype(bf16), b.astype(bf16), preferred_element_type=f32)


def bdot_nt(a, b):
    return lax.dot_general(a.astype(bf16), b.astype(bf16), (((1,), (1,)), ((), ())), preferred_element_type=f32)


def bdot_tn(a, b):
    return lax.dot_general(a.astype(bf16), b.astype(bf16), (((0,), (0,)), ((), ())), preferred_element_type=f32)


def _iota2(n, m):
    return lax.broadcasted_iota(jnp.int32, (n, m), 0), lax.broadcasted_iota(jnp.int32, (n, m), 1)


def silu(x):
    return x * jax.nn.sigmoid(x)


def rmsnorm(x, w):
    return x * lax.rsqrt(jnp.mean(x * x, axis=-1, keepdims=True) + EPS) * w


def hg_masks():
    t, r = _iota2(C, C)
    mats = [r <= t, r > t]
    lvl = []
    for l in range(1, 7):
        sz = 1 << l
        half = sz >> 1
        seg_t = t >> l
        upper_t = (t & (sz - 1)) >= half
        mid_t = seg_t * sz + half - 1
        mats.append(upper_t & (r > mid_t) & (r <= t))
        mats.append((~upper_t) & (r > t) & (r <= mid_t))
        lvl.append(((seg_t == (r >> l)) & upper_t & ((r & (sz - 1)) < half)).astype(f32))
    stk = jnp.concatenate([m.astype(f32) for m in mats], axis=0)
    return stk, lvl, (t == r).astype(f32)


def hg_chunk(St, p, l0, l1):
    pq, pf, v = p[:, 0:DK], p[:, DK:2 * DK], p[:, 2 * DK:3 * DK]
    m = jnp.maximum(l0, l1)
    e0 = jnp.exp(l0 - m)
    e1 = jnp.exp(l1 - m)
    lb = e0 / (e0 + e1)
    q = silu(pq)
    f = lb + (1.0 - lb) * jax.nn.sigmoid(pf)
    k = 1.0 - f
    logf = jnp.log(f)
    stk, lvl, eye = hg_masks()
    Dm = mm(stk, logf)
    G = Dm[0:C]
    Grev = Dm[C:2 * C]
    A = eye * mm_nt(q, k)
    for i in range(6):
        Du = Dm[(2 + 2 * i) * C:(3 + 2 * i) * C]
        Dl = Dm[(3 + 2 * i) * C:(4 + 2 * i) * C]
        A = A + lvl[i] * mm_nt(q * jnp.exp(Du), k * jnp.exp(Dl))
    o = mm(A, v) + mm_nt(q * jnp.exp(G), St)
    Glast = jnp.sum(logf, axis=0, keepdims=True)
    St_new = St * jnp.exp(Glast) + mm_tn(v, k * jnp.exp(Grev))
    return St_new, o


def softplus(x):
    return jnp.maximum(x, 0.0) + jnp.log(1.0 + jnp.exp(-jnp.abs(x)))


def gd_chunk(S, c, ab, alog, dtb, h):
    q = silu(c[:, 0:DK])
    k = silu(c[:, DK:2 * DK])
    v = silu(c[:, 2 * DK:3 * DK])
    q = q * lax.rsqrt(jnp.sum(q * q, -1, keepdims=True) + EPS) * (DK ** -0.5)
    k = k * lax.rsqrt(jnp.sum(k * k, -1, keepdims=True) + EPS)
    lane = lax.broadcasted_iota(jnp.int32, (1, 128), 1)
    sel_a = (lane == h).astype(f32)
    sel_b = (lane == h + 4).astype(f32)
    a = jnp.sum(ab * sel_a, -1, keepdims=True)
    b = jnp.sum(ab * sel_b, -1, keepdims=True)
    al = jnp.sum(alog * sel_a, -1, keepdims=True)
    db = jnp.sum(dtb * sel_a, -1, keepdims=True)
    g = -jnp.exp(al) * softplus(a + db)
    beta = jax.nn.sigmoid(b)
    t, r = _iota2(C, C)
    gb = g * jnp.ones((1, C), f32)
    gam_c = mm((r <= t).astype(f32), gb)
    gam_r = mm_tn(gb, (t <= r).astype(f32))
    grev = mm((r > t).astype(f32), gb)
    dec_incl = jnp.exp(jnp.where(r <= t, gam_c - gam_r, -1e30))
    egam = jnp.exp(gam_c[:, 0:1])
    A = beta * mm_nt(k, k) * dec_incl * (r < t).astype(f32)
    eye = (t == r).astype(f32)
    Dg = A * ((t >> 4) == (r >> 4)).astype(f32)
    L = A - Dg
    D2 = mm(Dg, Dg)
    D4 = mm(D2, D2)
    D8 = mm(D4, D4)
    M = mm(mm(mm(eye - Dg, eye + D2), eye + D4), eye + D8)
    Nn = mm(M, L)
    Tinv = mm(mm(eye - Nn, eye + mm(Nn, Nn)), M)
    U = mm(Tinv, beta * v)
    W = mm(Tinv, (beta * egam) * k)
    u = U - mm(W, S)
    o = mm(q * egam, S) + mm(mm_nt(q, k) * dec_incl, u)
    glast = jnp.sum(g, axis=0, keepdims=True)
    S_new = jnp.exp(glast) * S + mm_tn(k * jnp.exp(grev[:, 0:1]), u)
    return S_new, o


def _in_proj(hflat, norm_w, wbig):
    tm = 384

    def body(h_ref, nw_ref, w_ref, p_ref, u_ref):
        u = rmsnorm(h_ref[...], nw_ref[...]).astype(bf16)
        u_ref[...] = u
        p_ref[...] = jnp.dot(u, w_ref[...], preferred_element_type=f32)

    return pl.pallas_call(
        body, name="in_proj", grid=(N // tm,),
        in_specs=[pl.BlockSpec((tm, D), lambda i: (i, 0)), pl.BlockSpec((1, D), lambda i: (0, 0)),
                  pl.BlockSpec((D, PC), lambda i: (0, 0))],
        out_specs=[pl.BlockSpec((tm, PC), lambda i: (i, 0)), pl.BlockSpec((tm, D), lambda i: (i, 0))],
        out_shape=[jax.ShapeDtypeStruct((N, PC), f32), jax.ShapeDtypeStruct((N, D), bf16)],
        compiler_params=_cp(("arbitrary",)),
    )(hflat, norm_w, wbig)


def _hg_fwd(proj3, l0, l1):
    def body(p_ref, l0_ref, l1_ref, o_ref, s_ref, st):
        ch = pl.program_id(1)
        h = pl.program_id(2)

        @pl.when(ch == 0)
        def _():
            st[h] = jnp.zeros((DK, DK), f32)

        S = st[h]
        s_ref[0, 0, 0] = S
        Sn, o = hg_chunk(S, p_ref[0], l0_ref[...], l1_ref[...])
        st[h] = Sn
        o_ref[0] = o

    return pl.pallas_call(
        body, name="hg_fwd", grid=(NB, NCH, H),
        in_specs=[pl.BlockSpec((1, C, HW), lambda b, c, h: (b, c, h)),
                  pl.BlockSpec((1, DK), lambda b, c, h: (0, h)), pl.BlockSpec((1, DK), lambda b, c, h: (0, h))],
        out_specs=[pl.BlockSpec((1, C, DK), lambda b, c, h: (b, c, h)),
                   pl.BlockSpec((1, 1, 1, DK, DK), lambda b, c, h: (b, c, h, 0, 0))],
        out_shape=[jax.ShapeDtypeStruct((NB, TP, H * DK), f32), jax.ShapeDtypeStruct((NB, NCH, H, DK, DK), f32)],
        scratch_shapes=[pltpu.VMEM((H, DK, DK), f32)],
        compiler_params=_cp(("arbitrary", "arbitrary", "arbitrary")),
    )(proj3, l0, l1)


def _hg_bwd(proj3, l0, l1, s_saved, do):
    def body(p_ref, l0_ref, l1_ref, s_ref, do_ref, dp_ref, dl0_ref, dl1_ref, dst):
        b = pl.program_id(0)
        ch = pl.program_id(1)
        h = pl.program_id(2)

        @pl.when(ch == 0)
        def _():
            dst[h] = jnp.zeros((DK, DK), f32)

        @pl.when((b == 0) & (ch == 0) & (h == 0))
        def _():
            dl0_ref[...] = jnp.zeros_like(dl0_ref)
            dl1_ref[...] = jnp.zeros_like(dl1_ref)

        _, vjp = jax.vjp(hg_chunk, s_ref[0, 0, 0], p_ref[0], l0_ref[...], l1_ref[...])
        dS, dp, dl0, dl1 = vjp((dst[h], do_ref[0]))
        dst[h] = dS
        dp_ref[0] = dp
        dl0_ref[h] += jnp.broadcast_to(dl0, (8, DK))
        dl1_ref[h] += jnp.broadcast_to(dl1, (8, DK))

    rev = lambda c: NCH - 1 - c
    return pl.pallas_call(
        body, name="hg_bwd", grid=(NB, NCH, H),
        in_specs=[pl.BlockSpec((1, C, HW), lambda b, c, h: (b, rev(c), h)),
                  pl.BlockSpec((1, DK), lambda b, c, h: (0, h)), pl.BlockSpec((1, DK), lambda b, c, h: (0, h)),
                  pl.BlockSpec((1, 1, 1, DK, DK), lambda b, c, h: (b, rev(c), h, 0, 0)),
                  pl.BlockSpec((1, C, DK), lambda b, c, h: (b, rev(c), h))],
        out_specs=[pl.BlockSpec((1, C, HW), lambda b, c, h: (b, rev(c), h)),
                   pl.BlockSpec((H, 8, DK), lambda b, c, h: (0, 0, 0)), pl.BlockSpec((H, 8, DK), lambda b, c, h: (0, 0, 0))],
        out_shape=[jax.ShapeDtypeStruct((NB, TP, H * HW), f32), jax.ShapeDtypeStruct((H, 8, DK), f32),
                   jax.ShapeDtypeStruct((H, 8, DK), f32)],
        scratch_shapes=[pltpu.VMEM((H, DK, DK), f32)],
        compiler_params=_cp(("arbitrary", "arbitrary", "arbitrary")),
    )(proj3, l0, l1, s_saved, do)


def _conv_fwd(proj3, convp):
    def body(x_ref, w_ref, y_ref):
        x = x_ref[0]
        row = lax.broadcasted_iota(jnp.int32, (TP, 1), 0)
        y = w_ref[3] * x
        for s in (1, 2, 3):
            y = y + w_ref[3 - s] * jnp.where(row >= s, pltpu.roll(x, s, 0), 0.0)
        y_ref[0] = y

    return pl.pallas_call(
        body, name="conv_fwd", grid=(NB, H),
        in_specs=[pl.BlockSpec((1, TP, HW), lambda b, j: (b, 0, H + j)), pl.BlockSpec((4, 1, HW), lambda b, j: (0, 0, j))],
        out_specs=pl.BlockSpec((1, TP, HW), lambda b, j: (b, 0, j)),
        out_shape=jax.ShapeDtypeStruct((NB, TP, H * HW), f32),
        compiler_params=_cp(("arbitrary", "arbitrary")),
    )(proj3, convp)


def _conv_bwd(proj3, convp, dy):
    def body(x_ref, w_ref, dy_ref, dx_ref, dw_ref):
        b = pl.program_id(1)

        @pl.when(b == 0)
        def _():
            dw_ref[...] = jnp.zeros_like(dw_ref)

        x = x_ref[0]
        g = dy_ref[0]
        row = lax.broadcasted_iota(jnp.int32, (TP, 1), 0)
        dx = w_ref[3] * g
        dw_ref[3] += jnp.broadcast_to(jnp.sum(x * g, axis=0, keepdims=True), (8, HW))
        for s in (1, 2, 3):
            dx = dx + w_ref[3 - s] * jnp.where(row < TP - s, pltpu.roll(g, TP - s, 0), 0.0)
            xs = jnp.where(row >= s, pltpu.roll(x, s, 0), 0.0)
            dw_ref[3 - s] += jnp.broadcast_to(jnp.sum(xs * g, axis=0, keepdims=True), (8, HW))
        dx_ref[0] = dx

    return pl.pallas_call(
        body, name="conv_bwd", grid=(H, NB),
        in_specs=[pl.BlockSpec((1, TP, HW), lambda j, b: (b, 0, H + j)), pl.BlockSpec((4, 1, HW), lambda j, b: (0, 0, j)),
                  pl.BlockSpec((1, TP, HW), lambda j, b: (b, 0, j))],
        out_specs=[pl.BlockSpec((1, TP, HW), lambda j, b: (b, 0, j)), pl.BlockSpec((4, 8, HW), lambda j, b: (0, 0, j))],
        out_shape=[jax.ShapeDtypeStruct((NB, TP, H * HW), f32), jax.ShapeDtypeStruct((4, 8, H * HW), f32)],
        compiler_params=_cp(("arbitrary", "arbitrary")),
    )(proj3, convp, dy)


def _gd_fwd(cv, proj3, alog, dtb):
    def body(c_ref, ab_ref, al_ref, db_ref, o_ref, s_ref, st):
        ch = pl.program_id(1)
        h = pl.program_id(2)

        @pl.when(ch == 0)
        def _():
            st[h] = jnp.zeros((DK, DK), f32)

        S = st[h]
        s_ref[0, 0, 0] = S
        Sn, o = gd_chunk(S, c_ref[0], ab_ref[0], al_ref[...], db_ref[...], h)
        st[h] = Sn
        o_ref[0] = o

    return pl.pallas_call(
        body, name="gd_fwd", grid=(NB, NCH, H),
        in_specs=[pl.BlockSpec((1, C, HW), lambda b, c, h: (b, c, h)),
                  pl.BlockSpec((1, C, DK), lambda b, c, h: (b, c, (PC - DK) // DK)),
                  pl.BlockSpec((1, DK), lambda b, c, h: (0, 0)), pl.BlockSpec((1, DK), lambda b, c, h: (0, 0))],
        out_specs=[pl.BlockSpec((1, C, DK), lambda b, c, h: (b, c, h)),
                   pl.BlockSpec((1, 1, 1, DK, DK), lambda b, c, h: (b, c, h, 0, 0))],
        out_shape=[jax.ShapeDtypeStruct((NB, TP, H * DK), f32), jax.ShapeDtypeStruct((NB, NCH, H, DK, DK), f32)],
        scratch_shapes=[pltpu.VMEM((H, DK, DK), f32)],
        compiler_params=_cp(("arbitrary", "arbitrary", "arbitrary")),
    )(cv, proj3, alog, dtb)


def _gd_bwd(cv, proj3, alog, dtb, s_saved, do):
    def body(c_ref, ab_ref, al_ref, db_ref, s_ref, do_ref, dc_ref, dab_ref, dal_ref, ddb_ref, dst):
        b = pl.program_id(0)
        ch = pl.program_id(1)
        h = pl.program_id(2)

        @pl.when(ch == 0)
        def _():
            dst[h] = jnp.zeros((DK, DK), f32)

        @pl.when((b == 0) & (ch == 0) & (h == 0))
        def _():
            dal_ref[...] = jnp.zeros_like(dal_ref)
            ddb_ref[...] = jnp.zeros_like(ddb_ref)

        fn = functools.partial(gd_chunk, h=h)
        _, vjp = jax.vjp(fn, s_ref[0, 0, 0], c_ref[0], ab_ref[0], al_ref[...], db_ref[...])
        dS, dc, dab, dal, ddb = vjp((dst[h], do_ref[0]))
        dst[h] = dS
        dc_ref[0] = dc

        @pl.when(h == 0)
        def _():
            dab_ref[0] = dab

        @pl.when(h != 0)
        def _():
            dab_ref[0] += dab

        dal_ref[...] += jnp.broadcast_to(dal, (8, DK))
        ddb_ref[...] += jnp.broadcast_to(ddb, (8, DK))

    rev = lambda c: NCH - 1 - c
    return pl.pallas_call(
        body, name="gd_bwd", grid=(NB, NCH, H),
        in_specs=[pl.BlockSpec((1, C, HW), lambda b, c, h: (b, rev(c), h)),
                  pl.BlockSpec((1, C, DK), lambda b, c, h: (b, rev(c), (PC - DK) // DK)),
                  pl.BlockSpec((1, DK), lambda b, c, h: (0, 0)), pl.BlockSpec((1, DK), lambda b, c, h: (0, 0)),
                  pl.BlockSpec((1, 1, 1, DK, DK), lambda b, c, h: (b, rev(c), h, 0, 0)),
                  pl.BlockSpec((1, C, DK), lambda b, c, h: (b, rev(c), h))],
        out_specs=[pl.BlockSpec((1, C, HW), lambda b, c, h: (b, rev(c), h)),
                   pl.BlockSpec((1, C, DK), lambda b, c, h: (b, rev(c), 0)),
                   pl.BlockSpec((8, DK), lambda b, c, h: (0, 0)), pl.BlockSpec((8, DK), lambda b, c, h: (0, 0))],
        out_shape=[jax.ShapeDtypeStruct((NB, TP, H * HW), f32), jax.ShapeDtypeStruct((NB, TP, DK), f32),
                   jax.ShapeDtypeStruct((8, DK), f32), jax.ShapeDtypeStruct((8, DK), f32)],
        scratch_shapes=[pltpu.VMEM((H, DK, DK), f32)],
        compiler_params=_cp(("arbitrary", "arbitrary", "arbitrary")),
    )(cv, proj3, alog, dtb, s_saved, do)


def _gated(o, z, w):
    outs = []
    for hh in range(H):
        sl = slice(hh * DK, (hh + 1) * DK)
        outs.append(rmsnorm(o[:, sl], w) * silu(z[:, sl]))
    return jnp.concatenate(outs, axis=-1)


def _out_loss(o_hg, o_gd, proj, hgw, gdw, wout, hflat, fw, target):
    tm = 384

    def body(ohg_ref, ogd_ref, z_ref, hgw_ref, gdw_ref, wo_ref, h_ref, fw_ref, tg_ref,
             loss_ref, dohg_ref, dogd_ref, dz_ref, dh_ref, dwo_ref, dhgw_ref, dgdw_ref, dfw_ref):
        i = pl.program_id(0)

        @pl.when(i == 0)
        def _():
            for r in (loss_ref, dwo_ref, dhgw_ref, dgdw_ref, dfw_ref):
                r[...] = jnp.zeros_like(r)

        row = i * tm + lax.broadcasted_iota(jnp.int32, (tm, 1), 0)
        tok = jnp.where(row >= TP, row - TP, row)
        valid = (tok >= PAD + N_META).astype(f32)
        hval = h_ref[...]
        tgt = tg_ref[...]

        def mix(ohg, ogd, z, w1, w2):
            return jnp.concatenate([_gated(ohg, z[:, 0:H * DK], w1), _gated(ogd, z[:, H * DK:], w2)], axis=-1)

        y, vjp_mix = jax.vjp(mix, ohg_ref[...], ogd_ref[...], z_ref[...], hgw_ref[...], gdw_ref[...])
        out = bdot(y, wo_ref[...])

        def head(out, fwv):
            err = (rmsnorm(hval + out, fwv) - tgt) * valid
            return 0.5 * jnp.sum(jnp.mean(err * err, axis=-1, keepdims=True))

        loss, vjp_head = jax.vjp(head, out, fw_ref[...])
        dout, dfw = vjp_head(jnp.ones((), f32))
        dh_ref[...] = dout
        dy = bdot_nt(dout, wo_ref[...])
        dwo_ref[...] += bdot_tn(y, dout)
        dohg, dogd, dz, dw1, dw2 = vjp_mix(dy)
        dohg_ref[...] = dohg
        dogd_ref[...] = dogd
        dz_ref[...] = dz
        loss_ref[...] += jnp.broadcast_to(loss, (8, DK))
        dhgw_ref[...] += jnp.broadcast_to(dw1, (8, DK))
        dgdw_ref[...] += jnp.broadcast_to(dw2, (8, DK))
        dfw_ref[...] += jnp.broadcast_to(dfw, (8, D))

    row = lambda w: pl.BlockSpec((tm, w), lambda i: (i, 0))
    whole = lambda r, w: pl.BlockSpec((r, w), lambda i: (0, 0))
    return pl.pallas_call(
        body, name="out_loss", grid=(N // tm,),
        in_specs=[row(H * DK), row(H * DK), pl.BlockSpec((tm, 2 * H * DK), lambda i: (i, 3)),
                  whole(1, DK), whole(1, DK), whole(D, D), row(D), whole(1, D), row(D)],
        out_specs=[whole(8, DK), row(H * DK), row(H * DK), row(2 * H * DK), row(D), whole(D, D),
                   whole(8, DK), whole(8, DK), whole(8, D)],
        out_shape=[jax.ShapeDtypeStruct((8, DK), f32), jax.ShapeDtypeStruct((N, H * DK), f32),
                   jax.ShapeDtypeStruct((N, H * DK), f32), jax.ShapeDtypeStruct((N, 2 * H * DK), f32),
                   jax.ShapeDtypeStruct((N, D), f32), jax.ShapeDtypeStruct((D, D), f32),
                   jax.ShapeDtypeStruct((8, DK), f32), jax.ShapeDtypeStruct((8, DK), f32),
                   jax.ShapeDtypeStruct((8, D), f32)],
        compiler_params=_cp(("arbitrary",)),
    )(o_hg, o_gd, proj, hgw, gdw, wout, hflat, fw, target)


def _in_bwd(d_hg, d_qkv, d_z, d_ab, wbig, hflat, norm_w, dh_res):
    tm = 192
    widths = (H * HW, H * HW, 2 * H * DK, DK)
    offs = (0, H * HW, 2 * H * HW, 2 * H * HW + 2 * H * DK)

    def body(d0, d1, d2, d3, w_ref, h_ref, nw_ref, dhr_ref, dh_ref, dnw_ref):
        @pl.when(pl.program_id(0) == 0)
        def _():
            dnw_ref[...] = jnp.zeros_like(dnw_ref)

        du = jnp.zeros((tm, D), f32)
        for d_ref, off, wd in zip((d0, d1, d2, d3), offs, widths):
            du = du + bdot_nt(d_ref[...], w_ref[:, off:off + wd])
        _, vjp = jax.vjp(rmsnorm, h_ref[...], nw_ref[...])
        dh, dnw = vjp(du)
        dh_ref[...] = dh + dhr_ref[...]
        dnw_ref[...] += jnp.broadcast_to(dnw, (8, D))

    row = lambda w: pl.BlockSpec((tm, w), lambda i: (i, 0))
    return pl.pallas_call(
        body, name="in_bwd", grid=(N // tm,),
        in_specs=[row(widths[0]), row(widths[1]), row(widths[2]), row(widths[3]),
                  pl.BlockSpec((D, PC), lambda i: (0, 0)), row(D), pl.BlockSpec((1, D), lambda i: (0, 0)), row(D)],
        out_specs=[row(D), pl.BlockSpec((8, D), lambda i: (0, 0))],
        out_shape=[jax.ShapeDtypeStruct((N, D), f32), jax.ShapeDtypeStruct((8, D), f32)],
        compiler_params=_cp(("arbitrary",)),
    )(d_hg, d_qkv, d_z, d_ab, wbig, hflat, norm_w, dh_res)


def _w_grad(ub, d, name):
    n = d.shape[1]
    tn = min(n, 512)
    tk = N // 4

    def body(u_ref, d_ref, o_ref):
        @pl.when(pl.program_id(1) == 0)
        def _():
            o_ref[...] = jnp.zeros_like(o_ref)

        o_ref[...] += bdot_tn(u_ref[...], d_ref[...])

    return pl.pallas_call(
        body, name=name, grid=(n // tn, N // tk),
        in_specs=[pl.BlockSpec((tk, D), lambda j, k: (k, 0)), pl.BlockSpec((tk, tn), lambda j, k: (k, j))],
        out_specs=pl.BlockSpec((D, tn), lambda j, k: (0, j)),
        out_shape=jax.ShapeDtypeStruct((D, n), f32),
        compiler_params=_cp(("arbitrary", "arbitrary")),
    )(ub, d)


def _adam_math(g, w, m, v):
    m2 = ADAM_B1 * m + (1.0 - ADAM_B1) * g
    v2 = ADAM_B2 * v + (1.0 - ADAM_B2) * (g * g)
    m_hat = m2 / (1.0 - ADAM_B1 ** ADAM_STEP)
    v_hat = v2 / (1.0 - ADAM_B2 ** ADAM_STEP)
    delta = -ADAM_LR * (m_hat / (jnp.sqrt(v_hat) + ADAM_EPS) + ADAM_WD * w)
    return delta, m2, v2


def _adamw(gs, w, m, v, name):
    R, Cc = w.shape
    tr = 256 if R % 256 == 0 else R
    ng = len(gs)

    def body(*refs):
        g = refs[0][...]
        for r in refs[1:ng]:
            g = g + r[...]
        w_ref, m_ref, v_ref, g_ref, d_ref, m2_ref, v2_ref = refs[ng:]
        delta, m2, v2 = _adam_math(g, w_ref[...], m_ref[...], v_ref[...])
        g_ref[...] = g
        d_ref[...] = delta
        m2_ref[...] = m2
        v2_ref[...] = v2

    spec = pl.BlockSpec((tr, Cc), lambda i: (i, 0))
    return pl.pallas_call(
        body, name=name, grid=(R // tr,),
        in_specs=[spec] * (ng + 3), out_specs=[spec] * 4,
        out_shape=[jax.ShapeDtypeStruct((R, Cc), f32)] * 4,
        compiler_params=_cp(("arbitrary",)),
    )(*gs, w, m, v)


def _sum_slots(r, name):
    S, R, Cc = r.shape
    tr = 256 if R % 256 == 0 else R

    def body(r_ref, o_ref):
        acc = r_ref[0]
        for s in range(1, S):
            acc = acc + r_ref[s]
        o_ref[...] = acc

    return pl.pallas_call(
        body, name=name, grid=(R // tr,),
        in_specs=[pl.BlockSpec((S, tr, Cc), lambda i: (0, i, 0))], out_specs=pl.BlockSpec((tr, Cc), lambda i: (i, 0)),
        out_shape=jax.ShapeDtypeStruct((R, Cc), f32),
        compiler_params=_cp(("arbitrary",)),
    )(r)


def _place():
    x, y, c = lax.axis_index("x"), lax.axis_index("y"), lax.axis_index("c")
    return x, y, c, [(1 - x, y), (x, 1 - y), (1 - x, 1 - y)]


def _gather_weights(shards):
    na = len(shards)

    def body(*refs):
        srcs, dsts = refs[:na], refs[na:2 * na]
        send_sems, recv_sems, loc_sems = refs[2 * na:]
        x, y, c, chips = _place()
        me = 2 * x + y
        locs = [pltpu.make_async_copy(s, d.at[me], loc_sems.at[i]) for i, (s, d) in enumerate(zip(srcs, dsts))]
        for cp in locs:
            cp.start()

        def copy(j, i, slot):
            px, py = chips[j]
            return pltpu.make_async_remote_copy(
                src_ref=srcs[i], dst_ref=dsts[i].at[slot], send_sem=send_sems.at[na * j + i],
                recv_sem=recv_sems.at[na * j + i], device_id=(px, py, c), device_id_type=MESH)

        sends = [copy(j, i, me) for j in range(3) for i in range(na)]
        for cp in sends:
            cp.start()
        for j, (px, py) in enumerate(chips):
            for i in range(na):
                copy(j, i, 2 * px + py).wait_recv()
        for cp in sends:
            cp.wait_send()
        for cp in locs:
            cp.wait()

    return pl.pallas_call(
        body, name="gather_weights",
        in_specs=[ANY] * na, out_specs=[ANY] * na,
        out_shape=[jax.ShapeDtypeStruct((4,) + s.shape, s.dtype) for s in shards],
        scratch_shapes=[pltpu.SemaphoreType.DMA((3 * na,)), pltpu.SemaphoreType.DMA((3 * na,)),
                        pltpu.SemaphoreType.DMA((na,))],
        compiler_params=pltpu.CompilerParams(has_side_effects=True),
    )(*shards)


def _scatter_grads(g_in, g_out, pack):
    R = pack.shape[0]

    def body(gi, go, pk, ri, ro, rp, send_sems, recv_sems, loc_sems):
        x, y, c, chips = _place()
        me = 2 * x + y
        me8 = 4 * x + 2 * y + c
        locs = [pltpu.make_async_copy(gi.at[me], ri.at[me], loc_sems.at[0]),
                pltpu.make_async_copy(go.at[me], ro.at[me], loc_sems.at[1]),
                pltpu.make_async_copy(pk, rp.at[me8], loc_sems.at[2])]
        for cp in locs:
            cp.start()

        def big(j, i, src_slot, dst_slot):
            px, py = chips[j]
            s, d = ((gi, ri), (go, ro))[i]
            return pltpu.make_async_remote_copy(
                src_ref=s.at[src_slot], dst_ref=d.at[dst_slot], send_sem=send_sems.at[2 * j + i],
                recv_sem=recv_sems.at[2 * j + i], device_id=(px, py, c), device_id_type=MESH)

        flips = [(fx, fy, fc) for fx in (0, 1) for fy in (0, 1) for fc in (0, 1)][1:]

        def small(k, slot):
            fx, fy, fc = flips[k]
            return pltpu.make_async_remote_copy(
                src_ref=pk, dst_ref=rp.at[slot], send_sem=send_sems.at[6 + k], recv_sem=recv_sems.at[6 + k],
                device_id=(x ^ fx, y ^ fy, c ^ fc), device_id_type=MESH)

        sends = [big(j, i, 2 * chips[j][0] + chips[j][1], me) for j in range(3) for i in range(2)]
        sends += [small(k, me8) for k in range(7)]
        for cp in sends:
            cp.start()
        for j, (px, py) in enumerate(chips):
            for i in range(2):
                big(j, i, me, 2 * px + py).wait_recv()
        for k, (fx, fy, fc) in enumerate(flips):
            small(k, 4 * (x ^ fx) + 2 * (y ^ fy) + (c ^ fc)).wait_recv()
        for cp in sends:
            cp.wait_send()
        for cp in locs:
            cp.wait()

    return pl.pallas_call(
        body, name="scatter_grads",
        in_specs=[ANY] * 3, out_specs=[ANY] * 3,
        out_shape=[jax.ShapeDtypeStruct(g_in.shape, f32), jax.ShapeDtypeStruct(g_out.shape, f32),
                   jax.ShapeDtypeStruct((8, R, 128), f32)],
        scratch_shapes=[pltpu.SemaphoreType.DMA((13,)), pltpu.SemaphoreType.DMA((13,)), pltpu.SemaphoreType.DMA((3,))],
        compiler_params=pltpu.CompilerParams(has_side_effects=True),
    )(g_in, g_out, pack)


def _swap_sibling(a, b):
    def body(a_ref, b_ref, pa_ref, pb_ref, send_sems, recv_sems):
        x, y, c, _ = _place()
        cps = [pltpu.make_async_remote_copy(src_ref=s, dst_ref=d, send_sem=send_sems.at[i], recv_sem=recv_sems.at[i],
                                            device_id=(x, y, 1 - c), device_id_type=MESH)
               for i, (s, d) in enumerate(((a_ref, pa_ref), (b_ref, pb_ref)))]
        for cp in cps:
            cp.start()
        for cp in cps:
            cp.wait()

    return pl.pallas_call(
        body, name="swap_sibling",
        in_specs=[ANY] * 2, out_specs=[ANY] * 2,
        out_shape=[jax.ShapeDtypeStruct(a.shape, a.dtype), jax.ShapeDtypeStruct(b.shape, b.dtype)],
        scratch_shapes=[pltpu.SemaphoreType.DMA((2,)), pltpu.SemaphoreType.DMA((2,))],
        compiler_params=pltpu.CompilerParams(has_side_effects=True),
    )(a, b)


def _big_from_full(w):
    parts = []
    for h in range(H):
        parts += [w[..., 512 * j + DK * h: 512 * j + DK * (h + 1)] for j in range(3)]
    for h in range(H):
        parts += [w[..., 2048 + 512 * j + DK * h: 2048 + 512 * j + DK * (h + 1)] for j in range(3)]
    parts += [w[..., 1536:2048], w[..., 3584:4096], w[..., 4096:4104],
              jnp.zeros(w.shape[:-1] + (PC - 4104,), w.dtype)]
    return jnp.concatenate(parts, axis=-1)


def _full_from_big(g_hg, g_qkv, g_z, g_ab):
    hg = [[g_hg[..., HW * h + DK * j: HW * h + DK * (j + 1)] for h in range(H)] for j in range(3)]
    gd = [[g_qkv[..., HW * h + DK * j: HW * h + DK * (j + 1)] for h in range(H)] for j in range(3)]
    parts = hg[0] + hg[1] + hg[2] + [g_z[..., 0:512]] + gd[0] + gd[1] + gd[2] + [g_z[..., 512:1024], g_ab[..., 0:8]]
    return jnp.concatenate(parts, axis=-1)


def _conv_perm(w):
    return jnp.concatenate([w[..., 512 * j + DK * h: 512 * j + DK * (h + 1)] for h in range(H) for j in range(3)], axis=-1)


def _conv_unperm(w):
    return jnp.concatenate([w[..., HW * h + DK * j: HW * h + DK * (j + 1)] for j in range(3) for h in range(H)], axis=-1)


def _rows8(a):
    flat = a.reshape(-1)
    n = flat.shape[0]
    rows = -(-n // 1024) * 8
    return jnp.pad(flat, (0, rows * 128 - n)).reshape(rows, 128)


def kernel(x, meta_tokens, norm_w, w_in, conv_w, hg_lb_logits, hg_norm_w, gdn_A_log, gdn_dt_bias, gdn_norm_w, w_out, final_norm_w, loss_target, m_meta_tokens, m_norm_w, m_w_in, m_conv_w, m_hg_lb_logits, m_hg_norm_w, m_gdn_A_log, m_gdn_dt_bias, m_gdn_norm_w, m_w_out, m_final_norm_w, v_meta_tokens, v_norm_w, v_w_in, v_conv_w, v_hg_lb_logits, v_hg_norm_w, v_gdn_A_log, v_gdn_dt_bias, v_gdn_norm_w, v_w_out, v_final_norm_w):
    me = 2 * lax.axis_index("x") + lax.axis_index("y")

    g_win, g_wout, g_conv, g_meta = _gather_weights(
        [w_in[0].astype(bf16), w_out[0].astype(bf16), conv_w[0], meta_tokens])
    w_full = jnp.transpose(g_win, (1, 0, 2)).reshape(D, IN_COLS)
    wbig = _big_from_full(w_full)
    wout_full = g_wout.reshape(D, D)
    conv_full = jnp.transpose(g_conv, (1, 0, 2)).reshape(4, 1536)
    convp = _conv_perm(conv_full).reshape(4, 1, H * HW)
    meta_full = jnp.transpose(g_meta, (1, 0, 2)).reshape(N_META, D)

    (loss8, grad_x, d_meta, d_nw, gw_full, d_conv, d_lb, d_hgw, d_alog, d_dtb, d_gdw, g_wout_part, d_fw) = _local_step(
        x, loss_target, wbig, wout_full, convp, meta_full, norm_w, hg_lb_logits, hg_norm_w, gdn_A_log, gdn_dt_bias,
        gdn_norm_w, final_norm_w)

    g_in_blocks = jnp.transpose(gw_full.reshape(D, 4, SHARD_COLS), (1, 0, 2))
    g_out_blocks = g_wout_part.reshape(4, D // 4, D)
    pack = jnp.concatenate([
        loss8, d_nw[0].reshape(8, 128), d_lb.reshape(8, 128), d_hgw, _rows8(d_alog[0, :H]), _rows8(d_dtb[0, :H]),
        d_gdw, d_fw[0].reshape(8, 128), d_meta.reshape(128, 128), d_conv.reshape(48, 128)], axis=0)
    return _reduce_and_update(
        me, grad_x, g_in_blocks, g_out_blocks, pack, meta_tokens, norm_w, w_in, conv_w, hg_lb_logits, hg_norm_w, gdn_A_log,
        gdn_dt_bias, gdn_norm_w, w_out, final_norm_w, m_meta_tokens, m_norm_w, m_w_in, m_conv_w, m_hg_lb_logits,
        m_hg_norm_w, m_gdn_A_log, m_gdn_dt_bias, m_gdn_norm_w, m_w_out, m_final_norm_w, v_meta_tokens, v_norm_w, v_w_in,
        v_conv_w, v_hg_lb_logits, v_hg_norm_w, v_gdn_A_log, v_gdn_dt_bias, v_gdn_norm_w, v_w_out, v_final_norm_w)


def _local_step(x, loss_target, wbig, wout_full, convp, meta_full, norm_w, hg_lb_logits, hg_norm_w, gdn_A_log, gdn_dt_bias,
                gdn_norm_w, final_norm_w):
    h3 = jnp.concatenate([jnp.zeros((NB, PAD, D), f32), jnp.broadcast_to(meta_full[None], (NB, N_META, D)), x], axis=1)
    hflat = h3.reshape(N, D)
    target = jnp.pad(loss_target, ((0, 0), (PAD + N_META, 0), (0, 0))).reshape(N, D)
    l0, l1 = hg_lb_logits[0:1], hg_lb_logits[1:2]
    alog = jnp.pad(gdn_A_log, ((0, 0), (0, DK - H)))
    dtb = jnp.pad(gdn_dt_bias, ((0, 0), (0, DK - H)))
    fw = final_norm_w.reshape(1, D)

    proj, ub = _in_proj(hflat, norm_w, wbig)
    proj3 = proj.reshape(NB, TP, PC)
    o_hg, s_hg = _hg_fwd(proj3, l0, l1)
    cv = _conv_fwd(proj3, convp)
    o_gd, s_gd = _gd_fwd(cv, proj3, alog, dtb)
    (loss8, d_ohg, d_ogd, d_z, dh_res, g_wout_part, d_hgw, d_gdw, d_fw) = _out_loss(
        o_hg.reshape(N, H * DK), o_gd.reshape(N, H * DK), proj, hg_norm_w, gdn_norm_w, wout_full, hflat, fw, target)
    d_hg, d_l0, d_l1 = _hg_bwd(proj3, l0, l1, s_hg, d_ohg.reshape(NB, TP, H * DK))
    d_cv, d_ab, d_alog, d_dtb = _gd_bwd(cv, proj3, alog, dtb, s_gd, d_ogd.reshape(NB, TP, H * DK))
    d_qkv, d_convp = _conv_bwd(proj3, convp, d_cv)
    d_hg2, d_qkv2, d_ab2 = d_hg.reshape(N, H * HW), d_qkv.reshape(N, H * HW), d_ab.reshape(N, DK)
    dh, d_nw = _in_bwd(d_hg2, d_qkv2, d_z, d_ab2, wbig, hflat, norm_w, dh_res)
    gw_hg = _w_grad(ub, d_hg2, "w_grad_hg")
    gw_qkv = _w_grad(ub, d_qkv2, "w_grad_qkv")
    gw_z = _w_grad(ub, d_z, "w_grad_z")
    gw_ab = _w_grad(ub, d_ab2, "w_grad_ab")

    dh3 = dh.reshape(NB, TP, D)
    grad_x = dh3[:, PAD + N_META:, :]
    d_meta = jnp.sum(dh3[:, PAD:PAD + N_META, :], axis=0)
    d_conv = _conv_unperm(d_convp[:, 0, :])
    d_lb = jnp.stack([d_l0[:, 0, :].reshape(H * DK), d_l1[:, 0, :].reshape(H * DK)])
    gw_full = _full_from_big(gw_hg, gw_qkv, gw_z, gw_ab)
    return loss8, grad_x, d_meta, d_nw, gw_full, d_conv, d_lb, d_hgw, d_alog, d_dtb, d_gdw, g_wout_part, d_fw


def _reduce_and_update(me, grad_x, g_in_blocks, g_out_blocks, pack, meta_tokens, norm_w, w_in, conv_w, hg_lb_logits, hg_norm_w,
                       gdn_A_log, gdn_dt_bias, gdn_norm_w, w_out, final_norm_w, m_meta_tokens, m_norm_w, m_w_in, m_conv_w,
                       m_hg_lb_logits, m_hg_norm_w, m_gdn_A_log, m_gdn_dt_bias, m_gdn_norm_w, m_w_out, m_final_norm_w,
                       v_meta_tokens, v_norm_w, v_w_in, v_conv_w, v_hg_lb_logits, v_hg_norm_w, v_gdn_A_log, v_gdn_dt_bias,
                       v_gdn_norm_w, v_w_out, v_final_norm_w):
    r_in, r_out, r_pack = _scatter_grads(g_in_blocks, g_out_blocks, pack)
    s_in = _sum_slots(r_in, "sum_w_in")
    s_out = _sum_slots(r_out, "sum_w_out")
    p_in, p_out = _swap_sibling(s_in, s_out)
    small = _sum_slots(r_pack, "sum_small")

    gi, di, mi, vi = _adamw([s_in, p_in], w_in[0], m_w_in[0], v_w_in[0], "adamw_w_in")
    go, do_, mo, vo = _adamw([s_out, p_out], w_out[0], m_w_out[0], v_w_out[0], "adamw_w_out")

    g_meta_full = small[64:192].reshape(N_META, D)
    g_meta_loc = lax.dynamic_slice(g_meta_full, (0, me * 256), (N_META, 256))
    gm, dm, mm_, vm = _adamw([g_meta_loc], meta_tokens, m_meta_tokens, v_meta_tokens, "adamw_meta")
    g_conv_full = small[192:240].reshape(4, 1536)
    g_conv_loc = lax.dynamic_slice(g_conv_full, (0, me * 384), (4, 384))
    gc, dc, mc, vc = _adamw([g_conv_loc], conv_w[0], m_conv_w[0], v_conv_w[0], "adamw_conv")

    reps = [(norm_w, m_norm_w, v_norm_w), (hg_lb_logits, m_hg_lb_logits, v_hg_lb_logits),
            (hg_norm_w, m_hg_norm_w, v_hg_norm_w), (gdn_A_log, m_gdn_A_log, v_gdn_A_log),
            (gdn_dt_bias, m_gdn_dt_bias, v_gdn_dt_bias), (gdn_norm_w, m_gdn_norm_w, v_gdn_norm_w),
            (final_norm_w, m_final_norm_w, v_final_norm_w)]
    wp = jnp.concatenate([_rows8(t[0]) for t in reps], axis=0)
    mp = jnp.concatenate([_rows8(t[1]) for t in reps], axis=0)
    vp = jnp.concatenate([_rows8(t[2]) for t in reps], axis=0)
    gr, dr, mr, vr = _adamw([small[8:64]], wp, mp, vp, "adamw_small")

    def unpack(p):
        outs = []
        for i, t in enumerate(reps):
            n = t[0].size
            outs.append(p[8 * i:8 * i + 8].reshape(-1)[:n].reshape(t[0].shape))
        return outs

    def leaves(meta_v, conv_v, in_v, out_v, rep_p):
        nw, lb, hgw, al, db, gdw, fwv = unpack(rep_p)
        return [meta_v, nw, in_v[None], conv_v[None], lb, hgw, al, db, gdw, out_v[None], fwv]

    loss = small[0, 0]
    return (loss, grad_x, *leaves(gm, gc, gi, go, gr), *leaves(dm, dc, di, do_, dr),
            *leaves(mm_, mc, mi, mo, mr), *leaves(vm, vc, vi, vo, vr))
```

```python
import functools

import jax
import jax.numpy as jnp
from jax import lax
from jax.experimental import pallas as pl
from jax.experimental.pallas import tpu as pltpu

f32 = jnp.float32
bf16 = jnp.bfloat16
MESH = pl.DeviceIdType.MESH
ANY = pl.BlockSpec(memory_space=pl.ANY)

D = 1024
NB = 2
N_META = 16
SEQ = 2048
PAD = 48
TP = PAD + N_META + SEQ
C = 64
NCH = TP // C
N = NB * TP
H = 4
DK = 128
HD = H * DK
PC = 4224
IN_COLS = 4104
SHARD_COLS = IN_COLS // 4
COL_HG, COL_ZHG, COL_QKV, COL_ZGD, COL_AB = 0, 3 * HD, 4 * HD, 7 * HD, 8 * HD
EPS = 1e-6
ADAM_LR, ADAM_B1, ADAM_B2, ADAM_EPS, ADAM_WD, ADAM_STEP = 0.001, 0.9, 0.999, 1e-08, 0.01, 10
VMEM_LIMIT = 56 * 1024 * 1024

P_HG = dict(lvl=1, av=1, qs=1, su=1)
P_GD = dict(kk=1, inv=3, sol=3, ws=1, qk=1, o=1, su=1)


def _cp(sem=None, **kw):
    return pltpu.CompilerParams(dimension_semantics=sem, vmem_limit_bytes=VMEM_LIMIT, **kw)


_DIMS = {"nn": (((1,), (0,)), ((), ())), "nt": (((1,), (1,)), ((), ())), "tn": (((0,), (0,)), ((), ()))}


def _split(x):
    hi = x.astype(bf16)
    return hi, (x - hi.astype(f32)).astype(bf16)


def _dg(a, b, kind, passes):
    d = lambda x, y: lax.dot_general(x, y, _DIMS[kind], preferred_element_type=f32)
    if passes == 1:
        return d(a.astype(bf16), b.astype(bf16))
    ah, al = _split(a)
    bh, bl = _split(b)
    return d(ah, bh) + d(ah, bl) + d(al, bh)


@functools.partial(jax.custom_vjp, nondiff_argnums=(2, 3))
def mmx(a, b, kind, passes):
    return _dg(a, b, kind, passes)


def _mmx_fwd(a, b, kind, passes):
    return _dg(a, b, kind, passes), (a, b)


def _mmx_bwd(kind, passes, res, g):
    a, b = res
    if kind == "nn":
        return _dg(g, b, "nt", passes), _dg(a, g, "tn", passes)
    if kind == "nt":
        return _dg(g, b, "nn", passes), _dg(g, a, "tn", passes)
    return _dg(b, g, "nt", passes), _dg(a, g, "nn", passes)


mmx.defvjp(_mmx_fwd, _mmx_bwd)


def _mask_dg(mask, x, kind):
    n = x.shape[1]
    xh, xl = _split(x)
    r = lax.dot_general(mask, jnp.concatenate([xh, xl], axis=1), _DIMS[kind], preferred_element_type=f32)
    return r[:, :n] + r[:, n:]


@jax.custom_vjp
def mask_mm(mask, x):
    return _mask_dg(mask, x, "nn")


def _mask_fwd(mask, x):
    return _mask_dg(mask, x, "nn"), mask


def _mask_bwd(mask, g):
    return None, _mask_dg(mask, g, "tn")


mask_mm.defvjp(_mask_fwd, _mask_bwd)


def bdot(a, b):
    return jnp.dot(a.astype(bf16), b.astype(bf16), preferred_element_type=f32)


def bdot_nt(a, b):
    return lax.dot_general(a.astype(bf16), b.astype(bf16), _DIMS["nt"], preferred_element_type=f32)


def bdot_tn(a, b):
    return lax.dot_general(a.astype(bf16), b.astype(bf16), _DIMS["tn"], preferred_element_type=f32)


def _iota2(n, m):
    return lax.broadcasted_iota(jnp.int32, (n, m), 0), lax.broadcasted_iota(jnp.int32, (n, m), 1)


def silu(x):
    return x * jax.nn.sigmoid(x)


def softplus(x):
    return jnp.maximum(x, 0.0) + jnp.log(1.0 + jnp.exp(-jnp.abs(x)))


def rmsnorm(x, w):
    return x * lax.rsqrt(jnp.mean(x * x, axis=-1, keepdims=True) + EPS) * w


def hg_masks():
    t, r = _iota2(C, C)
    mats = [r <= t, r > t]
    lvl = []
    for l in range(1, 7):
        sz = 1 << l
        half = sz >> 1
        seg_t = t >> l
        upper_t = (t & (sz - 1)) >= half
        mid_t = seg_t * sz + half - 1
        mats.append(upper_t & (r > mid_t) & (r <= t))
        mats.append((~upper_t) & (r > t) & (r <= mid_t))
        lvl.append(((seg_t == (r >> l)) & upper_t & ((r & (sz - 1)) < half)).astype(f32))
    stk = jnp.concatenate([m.astype(bf16) for m in mats], axis=0)
    return stk, lvl, (t == r).astype(f32)


def hg_chunk(St, p, l0, l1):
    pq, pf, v = p[:, 0:HD], p[:, HD:2 * HD], p[:, 2 * HD:3 * HD]
    m = jnp.maximum(l0, l1)
    e0 = jnp.exp(l0 - m)
    e1 = jnp.exp(l1 - m)
    lb = e0 / (e0 + e1)
    q = silu(pq)
    f = lb + (1.0 - lb) * jax.nn.sigmoid(pf)
    k = 1.0 - f
    logf = jnp.log(f)
    stk, lvl, eye = hg_masks()
    Dm = mask_mm(stk, logf)
    qs = [q] + [q * jnp.exp(Dm[(2 + 2 * i) * C:(3 + 2 * i) * C]) for i in range(6)]
    ks = [k] + [k * jnp.exp(Dm[(3 + 2 * i) * C:(4 + 2 * i) * C]) for i in range(6)]
    msk = [eye] + lvl
    qG = q * jnp.exp(Dm[0:C])
    kR = k * jnp.exp(Dm[C:2 * C])
    eGl = jnp.exp(jnp.sum(logf, axis=0, keepdims=True))
    outs, Sn = [], []
    for h in range(H):
        sl = slice(h * DK, (h + 1) * DK)
        A = None
        for i in range(7):
            a = msk[i] * mmx(qs[i][:, sl], ks[i][:, sl], "nt", P_HG["lvl"])
            A = a if A is None else A + a
        outs.append(mmx(A, v[:, sl], "nn", P_HG["av"]) + mmx(qG[:, sl], St[h], "nt", P_HG["qs"]))
        Sn.append(St[h] * eGl[:, sl] + mmx(v[:, sl], kR[:, sl], "tn", P_HG["su"]))
    return tuple(Sn), jnp.concatenate(outs, axis=1)


def gd_chunk(S, c, ab, alog, dtb):
    qa, ka, va = silu(c[:, 0:HD]), silu(c[:, HD:2 * HD]), silu(c[:, 2 * HD:3 * HD])
    g4 = -jnp.exp(alog) * softplus(ab + dtb)
    b4 = jax.nn.sigmoid(ab)
    t, r = _iota2(C, C)
    gam4 = mask_mm((r <= t).astype(bf16), g4)
    grev4 = mask_mm((r > t).astype(bf16), g4)
    gam4T = gam4.T
    glast4 = jnp.sum(g4, axis=0, keepdims=True)
    lane = lax.broadcasted_iota(jnp.int32, (1, DK), 1)
    subl = lax.broadcasted_iota(jnp.int32, (8, 1), 0)
    eye = (t == r).astype(f32)
    strict = (r < t).astype(f32)
    bd = ((t >> 4) == (r >> 4)).astype(f32)
    inv = lambda a, b: mmx(a, b, "nn", P_GD["inv"])
    outs, Sn = [], []
    for h in range(H):
        sl = slice(h * DK, (h + 1) * DK)
        q, k, v = qa[:, sl], ka[:, sl], va[:, sl]
        q = q * lax.rsqrt(jnp.sum(q * q, -1, keepdims=True) + EPS) * (DK ** -0.5)
        k = k * lax.rsqrt(jnp.sum(k * k, -1, keepdims=True) + EPS)
        oh = (lane == h).astype(f32)
        gam_c = jnp.sum(gam4 * oh, -1, keepdims=True)
        grev_c = jnp.sum(grev4 * oh, -1, keepdims=True)
        beta = jnp.sum(b4 * (lane == h + H).astype(f32), -1, keepdims=True)
        glast = jnp.sum(glast4 * oh, -1, keepdims=True)
        gam_r = jnp.sum(gam4T[0:8, :] * (subl == h).astype(f32), axis=0, keepdims=True)
        dec = jnp.exp(jnp.where(r <= t, gam_c - gam_r, -1e30))
        egam = jnp.exp(gam_c)
        A = beta * mmx(k, k, "nt", P_GD["kk"]) * dec * strict
        Dg = A * bd
        L = A - Dg
        D2 = inv(Dg, Dg)
        D4 = inv(D2, D2)
        D8 = inv(D4, D4)
        M = inv(inv(inv(eye - Dg, eye + D2), eye + D4), eye + D8)
        Nn = inv(M, L)
        Tinv = inv(inv(eye - Nn, eye + inv(Nn, Nn)), M)
        rhs = jnp.concatenate([beta * v, (beta * egam) * k], axis=1)
        sol = mmx(Tinv, rhs, "nn", P_GD["sol"])
        u = sol[:, 0:DK] - mmx(sol[:, DK:2 * DK], S[h], "nn", P_GD["ws"])
        outs.append(mmx(q * egam, S[h], "nn", P_GD["o"])
                    + mmx(mmx(q, k, "nt", P_GD["qk"]) * dec, u, "nn", P_GD["o"]))
        Sn.append(jnp.exp(glast) * S[h] + mmx(k * jnp.exp(grev_c), u, "tn", P_GD["su"]))
    return tuple(Sn), jnp.concatenate(outs, axis=1)


def _in_proj(hflat, norm_w, wbig):
    tm = 384

    def body(h_ref, nw_ref, w_ref, p_ref, u_ref):
        u = rmsnorm(h_ref[...], nw_ref[...]).astype(bf16)
        u_ref[...] = u
        p_ref[...] = jnp.dot(u, w_ref[...], preferred_element_type=f32)

    return pl.pallas_call(
        body, name="in_proj", grid=(N // tm,),
        in_specs=[pl.BlockSpec((tm, D), lambda i: (i, 0)), pl.BlockSpec((1, D), lambda i: (0, 0)),
                  pl.BlockSpec((D, PC), lambda i: (0, 0))],
        out_specs=[pl.BlockSpec((tm, PC), lambda i: (i, 0)), pl.BlockSpec((tm, D), lambda i: (i, 0))],
        out_shape=[jax.ShapeDtypeStruct((N, PC), f32), jax.ShapeDtypeStruct((N, D), bf16)],
        compiler_params=_cp(("arbitrary",)),
    )(hflat, norm_w, wbig)


_STATE_SPEC = lambda ix: pl.BlockSpec((1, 1, H, DK, DK), lambda b, c: (b, ix(c), 0, 0, 0))
_REV = lambda c: NCH - 1 - c
_FWD = lambda c: c


def _scan_fwd(chunk_fn, name, x3, xcol, extra, extra_specs):
    ne = len(extra)

    def body(*refs):
        x_ref = refs[0]
        e_refs = refs[1:1 + ne]
        o_ref, s_ref, st = refs[1 + ne:]

        @pl.when(pl.program_id(1) == 0)
        def _():
            st[...] = jnp.zeros_like(st)

        S = tuple(st[h] for h in range(H))
        s_ref[0, 0] = st[...]
        Sn, o = chunk_fn(S, x_ref[0], *[e[0] if len(e.shape) == 3 else e[...] for e in e_refs])
        for h in range(H):
            st[h] = Sn[h]
        o_ref[0] = o

    return pl.pallas_call(
        body, name=name, grid=(NB, NCH),
        in_specs=[pl.BlockSpec((1, C, 3 * HD), lambda b, c: (b, c, xcol))] + extra_specs(_FWD),
        out_specs=[pl.BlockSpec((1, C, HD), lambda b, c: (b, c, 0)), _STATE_SPEC(_FWD)],
        out_shape=[jax.ShapeDtypeStruct((NB, TP, HD), f32), jax.ShapeDtypeStruct((NB, NCH, H, DK, DK), f32)],
        scratch_shapes=[pltpu.VMEM((H, DK, DK), f32)],
        compiler_params=_cp(("arbitrary", "arbitrary")),
    )(x3, *extra)


def _hg_extra_specs(ix):
    return [pl.BlockSpec((1, HD), lambda b, c: (0, 0)), pl.BlockSpec((1, HD), lambda b, c: (0, 0))]


def _gd_extra_specs(ix):
    return [pl.BlockSpec((1, C, DK), lambda b, c: (b, ix(c), COL_AB // DK)),
            pl.BlockSpec((1, DK), lambda b, c: (0, 0)), pl.BlockSpec((1, DK), lambda b, c: (0, 0))]


def _hg_fwd(proj3, l0, l1):
    return _scan_fwd(hg_chunk, "hg_fwd", proj3, 0, [l0, l1], _hg_extra_specs)


def _gd_fwd(cv, proj3, alog, dtb):
    return _scan_fwd(gd_chunk, "gd_fwd", cv, 0, [proj3, alog, dtb], _gd_extra_specs)


def _hg_bwd(proj3, l0, l1, s_saved, do):
    def body(p_ref, l0_ref, l1_ref, s_ref, do_ref, dp_ref, dl0_ref, dl1_ref, dst):
        @pl.when(pl.program_id(1) == 0)
        def _():
            dst[...] = jnp.zeros_like(dst)

        @pl.when((pl.program_id(0) == 0) & (pl.program_id(1) == 0))
        def _():
            dl0_ref[...] = jnp.zeros_like(dl0_ref)
            dl1_ref[...] = jnp.zeros_like(dl1_ref)

        S = tuple(s_ref[0, 0, h] for h in range(H))
        _, vjp = jax.vjp(hg_chunk, S, p_ref[0], l0_ref[...], l1_ref[...])
        dS, dp, dl0, dl1 = vjp((tuple(dst[h] for h in range(H)), do_ref[0]))
        for h in range(H):
            dst[h] = dS[h]
        dp_ref[0] = dp
        dl0_ref[...] += jnp.broadcast_to(dl0, (8, HD))
        dl1_ref[...] += jnp.broadcast_to(dl1, (8, HD))

    acc = pl.BlockSpec((8, HD), lambda b, c: (0, 0))
    return pl.pallas_call(
        body, name="hg_bwd", grid=(NB, NCH),
        in_specs=[pl.BlockSpec((1, C, 3 * HD), lambda b, c: (b, _REV(c), 0))] + _hg_extra_specs(_REV)
        + [_STATE_SPEC(_REV), pl.BlockSpec((1, C, HD), lambda b, c: (b, _REV(c), 0))],
        out_specs=[pl.BlockSpec((1, C, 3 * HD), lambda b, c: (b, _REV(c), 0)), acc, acc],
        out_shape=[jax.ShapeDtypeStruct((NB, TP, 3 * HD), f32), jax.ShapeDtypeStruct((8, HD), f32),
                   jax.ShapeDtypeStruct((8, HD), f32)],
        scratch_shapes=[pltpu.VMEM((H, DK, DK), f32)],
        compiler_params=_cp(("arbitrary", "arbitrary")),
    )(proj3, l0, l1, s_saved, do)


def _gd_bwd(cv, proj3, alog, dtb, s_saved, do):
    def body(c_ref, ab_ref, al_ref, db_ref, s_ref, do_ref, dc_ref, dab_ref, dal_ref, ddb_ref, dst):
        @pl.when(pl.program_id(1) == 0)
        def _():
            dst[...] = jnp.zeros_like(dst)

        @pl.when((pl.program_id(0) == 0) & (pl.program_id(1) == 0))
        def _():
            dal_ref[...] = jnp.zeros_like(dal_ref)
            ddb_ref[...] = jnp.zeros_like(ddb_ref)

        S = tuple(s_ref[0, 0, h] for h in range(H))
        _, vjp = jax.vjp(gd_chunk, S, c_ref[0], ab_ref[0], al_ref[...], db_ref[...])
        dS, dc, dab, dal, ddb = vjp((tuple(dst[h] for h in range(H)), do_ref[0]))
        for h in range(H):
            dst[h] = dS[h]
        dc_ref[0] = dc
        dab_ref[0] = dab
        dal_ref[...] += jnp.broadcast_to(dal, (8, DK))
        ddb_ref[...] += jnp.broadcast_to(ddb, (8, DK))

    acc = pl.BlockSpec((8, DK), lambda b, c: (0, 0))
    return pl.pallas_call(
        body, name="gd_bwd", grid=(NB, NCH),
        in_specs=[pl.BlockSpec((1, C, 3 * HD), lambda b, c: (b, _REV(c), 0))] + _gd_extra_specs(_REV)
        + [_STATE_SPEC(_REV), pl.BlockSpec((1, C, HD), lambda b, c: (b, _REV(c), 0))],
        out_specs=[pl.BlockSpec((1, C, 3 * HD), lambda b, c: (b, _REV(c), 0)),
                   pl.BlockSpec((1, C, DK), lambda b, c: (b, _REV(c), 0)), acc, acc],
        out_shape=[jax.ShapeDtypeStruct((NB, TP, 3 * HD), f32), jax.ShapeDtypeStruct((NB, TP, DK), f32),
                   jax.ShapeDtypeStruct((8, DK), f32), jax.ShapeDtypeStruct((8, DK), f32)],
        scratch_shapes=[pltpu.VMEM((H, DK, DK), f32)],
        compiler_params=_cp(("arbitrary", "arbitrary")),
    )(cv, proj3, alog, dtb, s_saved, do)


def _conv_fwd(proj3, conv4):
    def body(x_ref, w_ref, y_ref):
        x = x_ref[0]
        row = lax.broadcasted_iota(jnp.int32, (TP, 1), 0)
        y = w_ref[3] * x
        for s in (1, 2, 3):
            y = y + w_ref[3 - s] * jnp.where(row >= s, pltpu.roll(x, s, 0), 0.0)
        y_ref[0] = y

    return pl.pallas_call(
        body, name="conv_fwd", grid=(NB, 3),
        in_specs=[pl.BlockSpec((1, TP, HD), lambda b, j: (b, 0, COL_QKV // HD + j)),
                  pl.BlockSpec((4, 1, HD), lambda b, j: (0, 0, j))],
        out_specs=pl.BlockSpec((1, TP, HD), lambda b, j: (b, 0, j)),
        out_shape=jax.ShapeDtypeStruct((NB, TP, 3 * HD), f32),
        compiler_params=_cp(("arbitrary", "arbitrary")),
    )(proj3, conv4)


def _conv_bwd(proj3, conv4, dy):
    def body(x_ref, w_ref, dy_ref, dx_ref, dw_ref):
        @pl.when(pl.program_id(1) == 0)
        def _():
            dw_ref[...] = jnp.zeros_like(dw_ref)

        x = x_ref[0]
        g = dy_ref[0]
        row = lax.broadcasted_iota(jnp.int32, (TP, 1), 0)
        dx = w_ref[3] * g
        dw_ref[3] += jnp.broadcast_to(jnp.sum(x * g, axis=0, keepdims=True), (8, HD))
        for s in (1, 2, 3):
            dx = dx + w_ref[3 - s] * jnp.where(row < TP - s, pltpu.roll(g, TP - s, 0), 0.0)
            xs = jnp.where(row >= s, pltpu.roll(x, s, 0), 0.0)
            dw_ref[3 - s] += jnp.broadcast_to(jnp.sum(xs * g, axis=0, keepdims=True), (8, HD))
        dx_ref[0] = dx

    return pl.pallas_call(
        body, name="conv_bwd", grid=(3, NB),
        in_specs=[pl.BlockSpec((1, TP, HD), lambda j, b: (b, 0, COL_QKV // HD + j)),
                  pl.BlockSpec((4, 1, HD), lambda j, b: (0, 0, j)), pl.BlockSpec((1, TP, HD), lambda j, b: (b, 0, j))],
        out_specs=[pl.BlockSpec((1, TP, HD), lambda j, b: (b, 0, j)), pl.BlockSpec((4, 8, HD), lambda j, b: (0, 0, j))],
        out_shape=[jax.ShapeDtypeStruct((NB, TP, 3 * HD), f32), jax.ShapeDtypeStruct((4, 8, 3 * HD), f32)],
        compiler_params=_cp(("arbitrary", "arbitrary")),
    )(proj3, conv4, dy)


def _gated(o, z, w):
    outs = []
    for hh in range(H):
        sl = slice(hh * DK, (hh + 1) * DK)
        outs.append(rmsnorm(o[:, sl], w) * silu(z[:, sl]))
    return jnp.concatenate(outs, axis=-1)


def _out_loss(o_hg, o_gd, proj, hgw, gdw, wout, hflat, fw, target):
    tm = 384

    def body(ohg_ref, ogd_ref, zhg_ref, zgd_ref, hgw_ref, gdw_ref, wo_ref, h_ref, fw_ref, tg_ref,
             loss_ref, dohg_ref, dogd_ref, dzhg_ref, dzgd_ref, dh_ref, dwo_ref, dhgw_ref, dgdw_ref, dfw_ref):
        i = pl.program_id(0)

        @pl.when(i == 0)
        def _():
            for r in (loss_ref, dwo_ref, dhgw_ref, dgdw_ref, dfw_ref):
                r[...] = jnp.zeros_like(r)

        row = i * tm + lax.broadcasted_iota(jnp.int32, (tm, 1), 0)
        tok = jnp.where(row >= TP, row - TP, row)
        valid = (tok >= PAD + N_META).astype(f32)
        hval = h_ref[...]
        tgt = tg_ref[...]

        def mix(ohg, ogd, zhg, zgd, w1, w2):
            return jnp.concatenate([_gated(ohg, zhg, w1), _gated(ogd, zgd, w2)], axis=-1)

        y, vjp_mix = jax.vjp(mix, ohg_ref[...], ogd_ref[...], zhg_ref[...], zgd_ref[...], hgw_ref[...], gdw_ref[...])
        out = bdot(y, wo_ref[...])

        def head(out, fwv):
            err = (rmsnorm(hval + out, fwv) - tgt) * valid
            return 0.5 * jnp.sum(jnp.mean(err * err, axis=-1, keepdims=True))

        loss, vjp_head = jax.vjp(head, out, fw_ref[...])
        dout, dfw = vjp_head(jnp.ones((), f32))
        dh_ref[...] = dout
        dy = bdot_nt(dout, wo_ref[...])
        dwo_ref[...] += bdot_tn(y, dout)
        dohg, dogd, dzhg, dzgd, dw1, dw2 = vjp_mix(dy)
        dohg_ref[...] = dohg
        dogd_ref[...] = dogd
        dzhg_ref[...] = dzhg
        dzgd_ref[...] = dzgd
        loss_ref[...] += jnp.broadcast_to(loss, (8, DK))
        dhgw_ref[...] += jnp.broadcast_to(dw1, (8, DK))
        dgdw_ref[...] += jnp.broadcast_to(dw2, (8, DK))
        dfw_ref[...] += jnp.broadcast_to(dfw, (8, D))

    row = lambda w: pl.BlockSpec((tm, w), lambda i: (i, 0))
    whole = lambda r, w: pl.BlockSpec((r, w), lambda i: (0, 0))
    col = lambda c0: pl.BlockSpec((tm, HD), lambda i: (i, c0 // HD))
    return pl.pallas_call(
        body, name="out_loss", grid=(N // tm,),
        in_specs=[row(HD), row(HD), col(COL_ZHG), col(COL_ZGD),
                  whole(1, DK), whole(1, DK), whole(D, D), row(D), whole(1, D), row(D)],
        out_specs=[whole(8, DK), row(HD), row(HD), row(HD), row(HD), row(D), whole(D, D),
                   whole(8, DK), whole(8, DK), whole(8, D)],
        out_shape=[jax.ShapeDtypeStruct((8, DK), f32)] + [jax.ShapeDtypeStruct((N, HD), f32)] * 4
        + [jax.ShapeDtypeStruct((N, D), f32), jax.ShapeDtypeStruct((D, D), f32),
           jax.ShapeDtypeStruct((8, DK), f32), jax.ShapeDtypeStruct((8, DK), f32), jax.ShapeDtypeStruct((8, D), f32)],
        compiler_params=_cp(("arbitrary",)),
    )(o_hg, o_gd, proj, proj, hgw, gdw, wout, hflat, fw, target)


def _in_bwd(pieces, wbig, hflat, norm_w, dh_res):
    tm = 192
    np_ = len(pieces)
    offs = [c0 for _, c0 in pieces]
    widths = [d.shape[1] for d, _ in pieces]

    def body(*refs):
        d_refs = refs[:np_]
        w_ref, h_ref, nw_ref, dhr_ref, dh_ref, dnw_ref = refs[np_:]

        @pl.when(pl.program_id(0) == 0)
        def _():
            dnw_ref[...] = jnp.zeros_like(dnw_ref)

        du = jnp.zeros((tm, D), f32)
        for d_ref, off, wd in zip(d_refs, offs, widths):
            du = du + bdot_nt(d_ref[...], w_ref[:, off:off + wd])
        _, vjp = jax.vjp(rmsnorm, h_ref[...], nw_ref[...])
        dh, dnw = vjp(du)
        dh_ref[...] = dh + dhr_ref[...]
        dnw_ref[...] += jnp.broadcast_to(dnw, (8, D))

    row = lambda w: pl.BlockSpec((tm, w), lambda i: (i, 0))
    return pl.pallas_call(
        body, name="in_bwd", grid=(N // tm,),
        in_specs=[row(w) for w in widths]
        + [pl.BlockSpec((D, PC), lambda i: (0, 0)), row(D), pl.BlockSpec((1, D), lambda i: (0, 0)), row(D)],
        out_specs=[row(D), pl.BlockSpec((8, D), lambda i: (0, 0))],
        out_shape=[jax.ShapeDtypeStruct((N, D), f32), jax.ShapeDtypeStruct((8, D), f32)],
        compiler_params=_cp(("arbitrary",)),
    )(*[d for d, _ in pieces], wbig, hflat, norm_w, dh_res)


def _w_grad(ub, d, name):
    n = d.shape[1]
    tn = min(n, 512)
    tk = N // 4

    def body(u_ref, d_ref, o_ref):
        @pl.when(pl.program_id(1) == 0)
        def _():
            o_ref[...] = jnp.zeros_like(o_ref)

        o_ref[...] += bdot_tn(u_ref[...], d_ref[...])

    return pl.pallas_call(
        body, name=name, grid=(n // tn, N // tk),
        in_specs=[pl.BlockSpec((tk, D), lambda j, k: (k, 0)), pl.BlockSpec((tk, tn), lambda j, k: (k, j))],
        out_specs=pl.BlockSpec((D, tn), lambda j, k: (0, j)),
        out_shape=jax.ShapeDtypeStruct((D, n), f32),
        compiler_params=_cp(("arbitrary", "arbitrary")),
    )(ub, d)


def _adam_math(g, w, m, v):
    m2 = ADAM_B1 * m + (1.0 - ADAM_B1) * g
    v2 = ADAM_B2 * v + (1.0 - ADAM_B2) * (g * g)
    m_hat = m2 / (1.0 - ADAM_B1 ** ADAM_STEP)
    v_hat = v2 / (1.0 - ADAM_B2 ** ADAM_STEP)
    delta = -ADAM_LR * (m_hat / (jnp.sqrt(v_hat) + ADAM_EPS) + ADAM_WD * w)
    return delta, m2, v2


def _adamw(gs, w, m, v, name):
    R, Cc = w.shape
    tr = 256 if R % 256 == 0 else R
    ng = len(gs)

    def body(*refs):
        g = refs[0][...]
        for r in refs[1:ng]:
            g = g + r[...]
        w_ref, m_ref, v_ref, g_ref, d_ref, m2_ref, v2_ref = refs[ng:]
        delta, m2, v2 = _adam_math(g, w_ref[...], m_ref[...], v_ref[...])
        g_ref[...] = g
        d_ref[...] = delta
        m2_ref[...] = m2
        v2_ref[...] = v2

    spec = pl.BlockSpec((tr, Cc), lambda i: (i, 0))
    return pl.pallas_call(
        body, name=name, grid=(R // tr,),
        in_specs=[spec] * (ng + 3), out_specs=[spec] * 4,
        out_shape=[jax.ShapeDtypeStruct((R, Cc), f32)] * 4,
        compiler_params=_cp(("arbitrary",)),
    )(*gs, w, m, v)


def _sum_slots(r, name):
    S, R, Cc = r.shape
    tr = 256 if R % 256 == 0 else R

    def body(r_ref, o_ref):
        acc = r_ref[0]
        for s in range(1, S):
            acc = acc + r_ref[s]
        o_ref[...] = acc

    return pl.pallas_call(
        body, name=name, grid=(R // tr,),
        in_specs=[pl.BlockSpec((S, tr, Cc), lambda i: (0, i, 0))], out_specs=pl.BlockSpec((tr, Cc), lambda i: (i, 0)),
        out_shape=jax.ShapeDtypeStruct((R, Cc), f32),
        compiler_params=_cp(("arbitrary",)),
    )(r)


def _place():
    x, y, c = lax.axis_index("x"), lax.axis_index("y"), lax.axis_index("c")
    return x, y, c, [(1 - x, y), (x, 1 - y), (1 - x, 1 - y)]


def _gather_weights(shards):
    na = len(shards)

    def body(*refs):
        srcs, dsts = refs[:na], refs[na:2 * na]
        send_sems, recv_sems, loc_sems = refs[2 * na:]
        x, y, c, chips = _place()
        me = 2 * x + y
        locs = [pltpu.make_async_copy(s, d.at[me], loc_sems.at[i]) for i, (s, d) in enumerate(zip(srcs, dsts))]
        for cp in locs:
            cp.start()

        def copy(j, i, slot):
            px, py = chips[j]
            return pltpu.make_async_remote_copy(
                src_ref=srcs[i], dst_ref=dsts[i].at[slot], send_sem=send_sems.at[na * j + i],
                recv_sem=recv_sems.at[na * j + i], device_id=(px, py, c), device_id_type=MESH)

        sends = [copy(j, i, me) for j in range(3) for i in range(na)]
        for cp in sends:
            cp.start()
        for j, (px, py) in enumerate(chips):
            for i in range(na):
                copy(j, i, 2 * px + py).wait_recv()
        for cp in sends:
            cp.wait_send()
        for cp in locs:
            cp.wait()

    return pl.pallas_call(
        body, name="gather_weights",
        in_specs=[ANY] * na, out_specs=[ANY] * na,
        out_shape=[jax.ShapeDtypeStruct((4,) + s.shape, s.dtype) for s in shards],
        scratch_shapes=[pltpu.SemaphoreType.DMA((3 * na,)), pltpu.SemaphoreType.DMA((3 * na,)),
                        pltpu.SemaphoreType.DMA((na,))],
        compiler_params=pltpu.CompilerParams(has_side_effects=True),
    )(*shards)


def _scatter_grads(g_in, g_out, pack):
    R = pack.shape[0]

    def body(gi, go, pk, ri, ro, rp, send_sems, recv_sems, loc_sems):
        x, y, c, chips = _place()
        me = 2 * x + y
        me8 = 4 * x + 2 * y + c
        locs = [pltpu.make_async_copy(gi.at[me], ri.at[me], loc_sems.at[0]),
                pltpu.make_async_copy(go.at[me], ro.at[me], loc_sems.at[1]),
                pltpu.make_async_copy(pk, rp.at[me8], loc_sems.at[2])]
        for cp in locs:
            cp.start()

        def big(j, i, src_slot, dst_slot):
            px, py = chips[j]
            s, d = ((gi, ri), (go, ro))[i]
            return pltpu.make_async_remote_copy(
                src_ref=s.at[src_slot], dst_ref=d.at[dst_slot], send_sem=send_sems.at[2 * j + i],
                recv_sem=recv_sems.at[2 * j + i], device_id=(px, py, c), device_id_type=MESH)

        flips = [(fx, fy, fc) for fx in (0, 1) for fy in (0, 1) for fc in (0, 1)][1:]

        def small(k, slot):
            fx, fy, fc = flips[k]
            return pltpu.make_async_remote_copy(
                src_ref=pk, dst_ref=rp.at[slot], send_sem=send_sems.at[6 + k], recv_sem=recv_sems.at[6 + k],
                device_id=(x ^ fx, y ^ fy, c ^ fc), device_id_type=MESH)

        sends = [big(j, i, 2 * chips[j][0] + chips[j][1], me) for j in range(3) for i in range(2)]
        sends += [small(k, me8) for k in range(7)]
        for cp in sends:
            cp.start()
        for j, (px, py) in enumerate(chips):
            for i in range(2):
                big(j, i, me, 2 * px + py).wait_recv()
        for k, (fx, fy, fc) in enumerate(flips):
            small(k, 4 * (x ^ fx) + 2 * (y ^ fy) + (c ^ fc)).wait_recv()
        for cp in sends:
            cp.wait_send()
        for cp in locs:
            cp.wait()

    return pl.pallas_call(
        body, name="scatter_grads",
        in_specs=[ANY] * 3, out_specs=[ANY] * 3,
        out_shape=[jax.ShapeDtypeStruct(g_in.shape, f32), jax.ShapeDtypeStruct(g_out.shape, f32),
                   jax.ShapeDtypeStruct((8, R, 128), f32)],
        scratch_shapes=[pltpu.SemaphoreType.DMA((13,)), pltpu.SemaphoreType.DMA((13,)), pltpu.SemaphoreType.DMA((3,))],
        compiler_params=pltpu.CompilerParams(has_side_effects=True),
    )(g_in, g_out, pack)


def _swap_sibling(a, b):
    def body(a_ref, b_ref, pa_ref, pb_ref, send_sems, recv_sems):
        x, y, c, _ = _place()
        cps = [pltpu.make_async_remote_copy(src_ref=s, dst_ref=d, send_sem=send_sems.at[i], recv_sem=recv_sems.at[i],
                                            device_id=(x, y, 1 - c), device_id_type=MESH)
               for i, (s, d) in enumerate(((a_ref, pa_ref), (b_ref, pb_ref)))]
        for cp in cps:
            cp.start()
        for cp in cps:
            cp.wait()

    return pl.pallas_call(
        body, name="swap_sibling",
        in_specs=[ANY] * 2, out_specs=[ANY] * 2,
        out_shape=[jax.ShapeDtypeStruct(a.shape, a.dtype), jax.ShapeDtypeStruct(b.shape, b.dtype)],
        scratch_shapes=[pltpu.SemaphoreType.DMA((2,)), pltpu.SemaphoreType.DMA((2,))],
        compiler_params=pltpu.CompilerParams(has_side_effects=True),
    )(a, b)


def _rows8(a):
    flat = a.reshape(-1)
    n = flat.shape[0]
    rows = -(-n // 1024) * 8
    return jnp.pad(flat, (0, rows * 128 - n)).reshape(rows, 128)


def kernel(x, meta_tokens, norm_w, w_in, conv_w, hg_lb_logits, hg_norm_w, gdn_A_log, gdn_dt_bias, gdn_norm_w, w_out, final_norm_w, loss_target, m_meta_tokens, m_norm_w, m_w_in, m_conv_w, m_hg_lb_logits, m_hg_norm_w, m_gdn_A_log, m_gdn_dt_bias, m_gdn_norm_w, m_w_out, m_final_norm_w, v_meta_tokens, v_norm_w, v_w_in, v_conv_w, v_hg_lb_logits, v_hg_norm_w, v_gdn_A_log, v_gdn_dt_bias, v_gdn_norm_w, v_w_out, v_final_norm_w):
    me = 2 * lax.axis_index("x") + lax.axis_index("y")

    g_win, g_wout, g_conv, g_meta = _gather_weights(
        [w_in[0].astype(bf16), w_out[0].astype(bf16), conv_w[0], meta_tokens])
    w_full = jnp.transpose(g_win, (1, 0, 2)).reshape(D, IN_COLS)
    wbig = jnp.pad(w_full, ((0, 0), (0, PC - IN_COLS)))
    wout_full = g_wout.reshape(D, D)
    conv4 = jnp.transpose(g_conv, (1, 0, 2)).reshape(4, 1, 3 * HD)
    meta_full = jnp.transpose(g_meta, (1, 0, 2)).reshape(N_META, D)

    (loss8, grad_x, d_meta, d_nw, gw_full, d_conv, d_lb, d_hgw, d_alog, d_dtb, d_gdw, g_wout_part, d_fw) = _local_step(
        x, loss_target, wbig, wout_full, conv4, meta_full, norm_w, hg_lb_logits, hg_norm_w, gdn_A_log, gdn_dt_bias,
        gdn_norm_w, final_norm_w)

    g_in_blocks = jnp.transpose(gw_full.reshape(D, 4, SHARD_COLS), (1, 0, 2))
    g_out_blocks = g_wout_part.reshape(4, D // 4, D)
    pack = jnp.concatenate([
        loss8, d_nw[0].reshape(8, 128), d_lb.reshape(8, 128), d_hgw, _rows8(d_alog[0, :H]), _rows8(d_dtb[0, :H]),
        d_gdw, d_fw[0].reshape(8, 128), d_meta.reshape(128, 128), d_conv.reshape(48, 128)], axis=0)
    return _reduce_and_update(
        me, grad_x, g_in_blocks, g_out_blocks, pack, meta_tokens, norm_w, w_in, conv_w, hg_lb_logits, hg_norm_w, gdn_A_log,
        gdn_dt_bias, gdn_norm_w, w_out, final_norm_w, m_meta_tokens, m_norm_w, m_w_in, m_conv_w, m_hg_lb_logits,
        m_hg_norm_w, m_gdn_A_log, m_gdn_dt_bias, m_gdn_norm_w, m_w_out, m_final_norm_w, v_meta_tokens, v_norm_w, v_w_in,
        v_conv_w, v_hg_lb_logits, v_hg_norm_w, v_gdn_A_log, v_gdn_dt_bias, v_gdn_norm_w, v_w_out, v_final_norm_w)


def _local_step(x, loss_target, wbig, wout_full, conv4, meta_full, norm_w, hg_lb_logits, hg_norm_w, gdn_A_log, gdn_dt_bias,
                gdn_norm_w, final_norm_w):
    h3 = jnp.concatenate([jnp.zeros((NB, PAD, D), f32), jnp.broadcast_to(meta_full[None], (NB, N_META, D)), x], axis=1)
    hflat = h3.reshape(N, D)
    target = jnp.pad(loss_target, ((0, 0), (PAD + N_META, 0), (0, 0))).reshape(N, D)
    l0, l1 = hg_lb_logits[0:1], hg_lb_logits[1:2]
    alog = jnp.pad(gdn_A_log, ((0, 0), (0, DK - H)))
    dtb = jnp.pad(gdn_dt_bias, ((0, 0), (0, DK - H)))
    fw = final_norm_w.reshape(1, D)

    proj, ub = _in_proj(hflat, norm_w, wbig)
    proj3 = proj.reshape(NB, TP, PC)
    o_hg, s_hg = _hg_fwd(proj3, l0, l1)
    cv = _conv_fwd(proj3, conv4)
    o_gd, s_gd = _gd_fwd(cv, proj3, alog, dtb)
    (loss8, d_ohg, d_ogd, d_zhg, d_zgd, dh_res, g_wout_part, d_hgw, d_gdw, d_fw) = _out_loss(
        o_hg.reshape(N, HD), o_gd.reshape(N, HD), proj, hg_norm_w, gdn_norm_w, wout_full, hflat, fw, target)
    d_hg, d_l0, d_l1 = _hg_bwd(proj3, l0, l1, s_hg, d_ohg.reshape(NB, TP, HD))
    d_cv, d_ab, d_alog, d_dtb = _gd_bwd(cv, proj3, alog, dtb, s_gd, d_ogd.reshape(NB, TP, HD))
    d_qkv, d_conv4 = _conv_bwd(proj3, conv4, d_cv)
    d_hg2, d_qkv2, d_ab2 = d_hg.reshape(N, 3 * HD), d_qkv.reshape(N, 3 * HD), d_ab.reshape(N, DK)
    pieces = [(d_hg2, COL_HG), (d_zhg, COL_ZHG), (d_qkv2, COL_QKV), (d_zgd, COL_ZGD), (d_ab2, COL_AB)]
    dh, d_nw = _in_bwd(pieces, wbig, hflat, norm_w, dh_res)
    gws = [_w_grad(ub, d, "w_grad_%d" % c0) for d, c0 in pieces]
    gw_full = jnp.concatenate(gws[:4] + [gws[4][:, 0:IN_COLS - COL_AB]], axis=1)

    dh3 = dh.reshape(NB, TP, D)
    grad_x = dh3[:, PAD + N_META:, :]
    d_meta = jnp.sum(dh3[:, PAD:PAD + N_META, :], axis=0)
    d_conv = d_conv4[:, 0, :]
    d_lb = jnp.concatenate([d_l0[0:1], d_l1[0:1]], axis=0)
    return loss8, grad_x, d_meta, d_nw, gw_full, d_conv, d_lb, d_hgw, d_alog, d_dtb, d_gdw, g_wout_part, d_fw


def _reduce_and_update(me, grad_x, g_in_blocks, g_out_blocks, pack, meta_tokens, norm_w, w_in, conv_w, hg_lb_logits, hg_norm_w,
                       gdn_A_log, gdn_dt_bias, gdn_norm_w, w_out, final_norm_w, m_meta_tokens, m_norm_w, m_w_in, m_conv_w,
                       m_hg_lb_logits, m_hg_norm_w, m_gdn_A_log, m_gdn_dt_bias, m_gdn_norm_w, m_w_out, m_final_norm_w,
                       v_meta_tokens, v_norm_w, v_w_in, v_conv_w, v_hg_lb_logits, v_hg_norm_w, v_gdn_A_log, v_gdn_dt_bias,
                       v_gdn_norm_w, v_w_out, v_final_norm_w):
    r_in, r_out, r_pack = _scatter_grads(g_in_blocks, g_out_blocks, pack)
    s_in = _sum_slots(r_in, "sum_w_in")
    s_out = _sum_slots(r_out, "sum_w_out")
    p_in, p_out = _swap_sibling(s_in, s_out)
    small = _sum_slots(r_pack, "sum_small")

    gi, di, mi, vi = _adamw([s_in, p_in], w_in[0], m_w_in[0], v_w_in[0], "adamw_w_in")
    go, do_, mo, vo = _adamw([s_out, p_out], w_out[0], m_w_out[0], v_w_out[0], "adamw_w_out")

    g_meta_full = small[64:192].reshape(N_META, D)
    g_meta_loc = lax.dynamic_slice(g_meta_full, (0, me * 256), (N_META, 256))
    gm, dm, mm_, vm = _adamw([g_meta_loc], meta_tokens, m_meta_tokens, v_meta_tokens, "adamw_meta")
    g_conv_full = small[192:240].reshape(4, 1536)
    g_conv_loc = lax.dynamic_slice(g_conv_full, (0, me * 384), (4, 384))
    gc, dc, mc, vc = _adamw([g_conv_loc], conv_w[0], m_conv_w[0], v_conv_w[0], "adamw_conv")

    reps = [(norm_w, m_norm_w, v_norm_w), (hg_lb_logits, m_hg_lb_logits, v_hg_lb_logits),
            (hg_norm_w, m_hg_norm_w, v_hg_norm_w), (gdn_A_log, m_gdn_A_log, v_gdn_A_log),
            (gdn_dt_bias, m_gdn_dt_bias, v_gdn_dt_bias), (gdn_norm_w, m_gdn_norm_w, v_gdn_norm_w),
            (final_norm_w, m_final_norm_w, v_final_norm_w)]
    wp = jnp.concatenate([_rows8(t[0]) for t in reps], axis=0)
    mp = jnp.concatenate([_rows8(t[1]) for t in reps], axis=0)
    vp = jnp.concatenate([_rows8(t[2]) for t in reps], axis=0)
    gr, dr, mr, vr = _adamw([small[8:64]], wp, mp, vp, "adamw_small")

    def unpack(p):
        outs = []
        for i, t in enumerate(reps):
            n = t[0].size
            outs.append(p[8 * i:8 * i + 8].reshape(-1)[:n].reshape(t[0].shape))
        return outs

    def leaves(meta_v, conv_v, in_v, out_v, rep_p):
        nw, lb, hgw, al, db, gdw, fwv = unpack(rep_p)
        return [meta_v, nw, in_v[None], conv_v[None], lb, hgw, al, db, gdw, out_v[None], fwv]

    loss = small[0, 0]
    return (loss, grad_x, *leaves(gm, gc, gi, go, gr), *leaves(dm, dc, di, do_, dr),
            *leaves(mm_, mc, mi, mo, mr), *leaves(vm, vc, vi, vo, vr))
```

```python
import functools

import jax
import jax.numpy as jnp
from jax import lax
from jax.experimental import pallas as pl
from jax.experimental.pallas import tpu as pltpu

f32 = jnp.float32
bf16 = jnp.bfloat16
MESH = pl.DeviceIdType.MESH
ANY = pl.BlockSpec(memory_space=pl.ANY)

D = 1024
NB = 2
N_META = 16
SEQ = 2048
PAD = 48
TP = PAD + N_META + SEQ
C = 64
NCH = TP // C
N = NB * TP
H = 4
DK = 128
HD = H * DK
PC = 4224
IN_COLS = 4104
SHARD_COLS = IN_COLS // 4
COL_HG, COL_ZHG, COL_QKV, COL_ZGD, COL_AB = 0, 3 * HD, 4 * HD, 7 * HD, 8 * HD
EPS = 1e-6
ADAM_LR, ADAM_B1, ADAM_B2, ADAM_EPS, ADAM_WD, ADAM_STEP = 0.001, 0.9, 0.999, 1e-08, 0.01, 10
VMEM_LIMIT = 56 * 1024 * 1024

P_HG = dict(lvl=1, av=1, qs=1, su=1)
P_GD = dict(kk=1, inv=3, sol=3, ws=1, qk=1, o=1, su=1)


def _cp(sem=None, **kw):
    return pltpu.CompilerParams(dimension_semantics=sem, vmem_limit_bytes=VMEM_LIMIT, **kw)


_DIMS = {"nn": (((1,), (0,)), ((), ())), "nt": (((1,), (1,)), ((), ())), "tn": (((0,), (0,)), ((), ()))}


def _split(x):
    hi = x.astype(bf16)
    return hi, (x - hi.astype(f32)).astype(bf16)


def _dg(a, b, kind, passes):
    d = lambda x, y: lax.dot_general(x, y, _DIMS[kind], preferred_element_type=f32)
    if passes == 1:
        return d(a.astype(bf16), b.astype(bf16))
    ah, al = _split(a)
    bh, bl = _split(b)
    return d(ah, bh) + d(ah, bl) + d(al, bh)


@functools.partial(jax.custom_vjp, nondiff_argnums=(2, 3))
def mmx(a, b, kind, passes):
    return _dg(a, b, kind, passes)


def _mmx_fwd(a, b, kind, passes):
    return _dg(a, b, kind, passes), (a, b)


def _mmx_bwd(kind, passes, res, g):
    a, b = res
    if kind == "nn":
        return _dg(g, b, "nt", passes), _dg(a, g, "tn", passes)
    if kind == "nt":
        return _dg(g, b, "nn", passes), _dg(g, a, "tn", passes)
    return _dg(b, g, "nt", passes), _dg(a, g, "nn", passes)


mmx.defvjp(_mmx_fwd, _mmx_bwd)


def _mask_dg(mask, x, kind):
    n = x.shape[1]
    xh, xl = _split(x)
    r = lax.dot_general(mask, jnp.concatenate([xh, xl], axis=1), _DIMS[kind], preferred_element_type=f32)
    return r[:, :n] + r[:, n:]


@jax.custom_vjp
def mask_mm(mask, x):
    return _mask_dg(mask, x, "nn")


def _mask_fwd(mask, x):
    return _mask_dg(mask, x, "nn"), mask


def _mask_bwd(mask, g):
    return None, _mask_dg(mask, g, "tn")


mask_mm.defvjp(_mask_fwd, _mask_bwd)


def bdot(a, b):
    return jnp.dot(a.astype(bf16), b.astype(bf16), preferred_element_type=f32)


def bdot_nt(a, b):
    return lax.dot_general(a.astype(bf16), b.astype(bf16), _DIMS["nt"], preferred_element_type=f32)


def bdot_tn(a, b):
    return lax.dot_general(a.astype(bf16), b.astype(bf16), _DIMS["tn"], preferred_element_type=f32)


def _iota2(n, m):
    return lax.broadcasted_iota(jnp.int32, (n, m), 0), lax.broadcasted_iota(jnp.int32, (n, m), 1)


def silu(x):
    return x * jax.nn.sigmoid(x)


def softplus(x):
    return jnp.maximum(x, 0.0) + jnp.log(1.0 + jnp.exp(-jnp.abs(x)))


def rmsnorm(x, w):
    return x * lax.rsqrt(jnp.mean(x * x, axis=-1, keepdims=True) + EPS) * w


def hg_masks():
    t, r = _iota2(C, C)
    mats = [r <= t, r > t]
    lvl = []
    for l in range(1, 7):
        sz = 1 << l
        half = sz >> 1
        seg_t = t >> l
        upper_t = (t & (sz - 1)) >= half
        mid_t = seg_t * sz + half - 1
        mats.append(upper_t & (r > mid_t) & (r <= t))
        mats.append((~upper_t) & (r > t) & (r <= mid_t))
        lvl.append(((seg_t == (r >> l)) & upper_t & ((r & (sz - 1)) < half)).astype(f32))
    stk = jnp.concatenate([m.astype(bf16) for m in mats], axis=0)
    return stk, lvl, (t == r).astype(f32)


def hg_chunk(St, p, l0, l1):
    pq, pf, v = p[:, 0:HD], p[:, HD:2 * HD], p[:, 2 * HD:3 * HD]
    m = jnp.maximum(l0, l1)
    e0 = jnp.exp(l0 - m)
    e1 = jnp.exp(l1 - m)
    lb = e0 / (e0 + e1)
    q = silu(pq)
    f = lb + (1.0 - lb) * jax.nn.sigmoid(pf)
    k = 1.0 - f
    logf = jnp.log(f)
    stk, lvl, eye = hg_masks()
    Dm = mask_mm(stk, logf)
    qs = [q] + [q * jnp.exp(Dm[(2 + 2 * i) * C:(3 + 2 * i) * C]) for i in range(6)]
    ks = [k] + [k * jnp.exp(Dm[(3 + 2 * i) * C:(4 + 2 * i) * C]) for i in range(6)]
    msk = [eye] + lvl
    qG = q * jnp.exp(Dm[0:C])
    kR = k * jnp.exp(Dm[C:2 * C])
    eGl = jnp.exp(jnp.sum(logf, axis=0, keepdims=True))
    heads = range(H)
    sls = [slice(h * DK, (h + 1) * DK) for h in heads]
    parts = [[msk[i] * mmx(qs[i][:, sl], ks[i][:, sl], "nt", P_HG["lvl"]) for sl in sls] for i in range(7)]
    A = [sum(parts[i][h] for i in range(1, 7)) + parts[0][h] for h in heads]
    qS = [mmx(qG[:, sls[h]], St[h], "nt", P_HG["qs"]) for h in heads]
    Sn = [St[h] * eGl[:, sls[h]] + mmx(v[:, sls[h]], kR[:, sls[h]], "tn", P_HG["su"]) for h in heads]
    outs = [mmx(A[h], v[:, sls[h]], "nn", P_HG["av"]) + qS[h] for h in heads]
    return tuple(Sn), jnp.concatenate(outs, axis=1)


def gd_chunk(S, c, ab, alog, dtb):
    qa, ka, va = silu(c[:, 0:HD]), silu(c[:, HD:2 * HD]), silu(c[:, 2 * HD:3 * HD])
    g4 = -jnp.exp(alog) * softplus(ab + dtb)
    b4 = jax.nn.sigmoid(ab)
    t, r = _iota2(C, C)
    gam4 = mask_mm((r <= t).astype(bf16), g4)
    grev4 = mask_mm((r > t).astype(bf16), g4)
    gam4T = gam4.T
    glast4 = jnp.sum(g4, axis=0, keepdims=True)
    lane = lax.broadcasted_iota(jnp.int32, (1, DK), 1)
    subl = lax.broadcasted_iota(jnp.int32, (8, 1), 0)
    eye = (t == r).astype(f32)
    strict = (r < t).astype(f32)
    bd = ((t >> 4) == (r >> 4)).astype(f32)
    heads = range(H)
    inv = lambda a, b: [mmx(a[h], b[h], "nn", P_GD["inv"]) for h in heads]
    sls = [slice(h * DK, (h + 1) * DK) for h in heads]
    v = [va[:, sl] for sl in sls]
    q = [qa[:, sl] for sl in sls]
    k = [ka[:, sl] for sl in sls]
    q = [x * lax.rsqrt(jnp.sum(x * x, -1, keepdims=True) + EPS) * (DK ** -0.5) for x in q]
    k = [x * lax.rsqrt(jnp.sum(x * x, -1, keepdims=True) + EPS) for x in k]
    oh = [(lane == h).astype(f32) for h in heads]
    gam_c = [jnp.sum(gam4 * oh[h], -1, keepdims=True) for h in heads]
    grev_c = [jnp.sum(grev4 * oh[h], -1, keepdims=True) for h in heads]
    beta = [jnp.sum(b4 * (lane == h + H).astype(f32), -1, keepdims=True) for h in heads]
    glast = [jnp.sum(glast4 * oh[h], -1, keepdims=True) for h in heads]
    gam_r = [jnp.sum(gam4T[0:8, :] * (subl == h).astype(f32), axis=0, keepdims=True) for h in heads]
    dec = [jnp.exp(jnp.where(r <= t, gam_c[h] - gam_r[h], -1e30)) for h in heads]
    egam = [jnp.exp(gam_c[h]) for h in heads]
    kk = [mmx(k[h], k[h], "nt", P_GD["kk"]) for h in heads]
    qk = [mmx(q[h], k[h], "nt", P_GD["qk"]) * dec[h] for h in heads]
    A = [beta[h] * kk[h] * dec[h] * strict for h in heads]
    Dg = [A[h] * bd for h in heads]
    L = [A[h] - Dg[h] for h in heads]
    ImD = [eye - Dg[h] for h in heads]
    D2 = inv(Dg, Dg)
    P1 = inv(ImD, [eye + x for x in D2])
    D4 = inv(D2, D2)
    P2 = inv(P1, [eye + x for x in D4])
    D8 = inv(D4, D4)
    M = inv(P2, [eye + x for x in D8])
    Nn = inv(M, L)
    N2 = inv(Nn, Nn)
    T1 = inv([eye - x for x in Nn], [eye + x for x in N2])
    Tinv = inv(T1, M)
    rhs = [jnp.concatenate([beta[h] * v[h], (beta[h] * egam[h]) * k[h]], axis=1) for h in heads]
    sol = [mmx(Tinv[h], rhs[h], "nn", P_GD["sol"]) for h in heads]
    qS = [mmx(q[h] * egam[h], S[h], "nn", P_GD["o"]) for h in heads]
    wS = [mmx(sol[h][:, DK:2 * DK], S[h], "nn", P_GD["ws"]) for h in heads]
    u = [sol[h][:, 0:DK] - wS[h] for h in heads]
    outs = [qS[h] + mmx(qk[h], u[h], "nn", P_GD["o"]) for h in heads]
    Sn = [jnp.exp(glast[h]) * S[h] + mmx(k[h] * jnp.exp(grev_c[h]), u[h], "tn", P_GD["su"]) for h in heads]
    return tuple(Sn), jnp.concatenate(outs, axis=1)


def _in_proj(hflat, norm_w, wbig):
    tm = 384

    def body(h_ref, nw_ref, w_ref, p_ref, u_ref):
        u = rmsnorm(h_ref[...], nw_ref[...]).astype(bf16)
        u_ref[...] = u
        p_ref[...] = jnp.dot(u, w_ref[...], preferred_element_type=f32)

    return pl.pallas_call(
        body, name="in_proj", grid=(N // tm,),
        in_specs=[pl.BlockSpec((tm, D), lambda i: (i, 0)), pl.BlockSpec((1, D), lambda i: (0, 0)),
                  pl.BlockSpec((D, PC), lambda i: (0, 0))],
        out_specs=[pl.BlockSpec((tm, PC), lambda i: (i, 0)), pl.BlockSpec((tm, D), lambda i: (i, 0))],
        out_shape=[jax.ShapeDtypeStruct((N, PC), f32), jax.ShapeDtypeStruct((N, D), bf16)],
        compiler_params=_cp(("arbitrary",)),
    )(hflat, norm_w, wbig)


_STATE_SPEC = lambda ix: pl.BlockSpec((1, 1, H, DK, DK), lambda b, c: (b, ix(c), 0, 0, 0))
_REV = lambda c: NCH - 1 - c
_FWD = lambda c: c


def _scan_fwd(chunk_fn, name, x3, xcol, extra, extra_specs):
    ne = len(extra)

    def body(*refs):
        x_ref = refs[0]
        e_refs = refs[1:1 + ne]
        o_ref, s_ref, st = refs[1 + ne:]

        @pl.when(pl.program_id(1) == 0)
        def _():
            st[...] = jnp.zeros_like(st)

        S = tuple(st[h] for h in range(H))
        s_ref[0, 0] = st[...]
        Sn, o = chunk_fn(S, x_ref[0], *[e[0] if len(e.shape) == 3 else e[...] for e in e_refs])
        for h in range(H):
            st[h] = Sn[h]
        o_ref[0] = o

    return pl.pallas_call(
        body, name=name, grid=(NB, NCH),
        in_specs=[pl.BlockSpec((1, C, 3 * HD), lambda b, c: (b, c, xcol))] + extra_specs(_FWD),
        out_specs=[pl.BlockSpec((1, C, HD), lambda b, c: (b, c, 0)), _STATE_SPEC(_FWD)],
        out_shape=[jax.ShapeDtypeStruct((NB, TP, HD), f32), jax.ShapeDtypeStruct((NB, NCH, H, DK, DK), f32)],
        scratch_shapes=[pltpu.VMEM((H, DK, DK), f32)],
        compiler_params=_cp(("arbitrary", "arbitrary")),
    )(x3, *extra)


def _hg_extra_specs(ix):
    return [pl.BlockSpec((1, HD), lambda b, c: (0, 0)), pl.BlockSpec((1, HD), lambda b, c: (0, 0))]


def _gd_extra_specs(ix):
    return [pl.BlockSpec((1, C, DK), lambda b, c: (b, ix(c), COL_AB // DK)),
            pl.BlockSpec((1, DK), lambda b, c: (0, 0)), pl.BlockSpec((1, DK), lambda b, c: (0, 0))]


def _hg_fwd(proj3, l0, l1):
    return _scan_fwd(hg_chunk, "hg_fwd", proj3, 0, [l0, l1], _hg_extra_specs)


def _gd_fwd(cv, proj3, alog, dtb):
    return _scan_fwd(gd_chunk, "gd_fwd", cv, 0, [proj3, alog, dtb], _gd_extra_specs)


def _hg_bwd(proj3, l0, l1, s_saved, do):
    def body(p_ref, l0_ref, l1_ref, s_ref, do_ref, dp_ref, dl0_ref, dl1_ref, dst):
        @pl.when(pl.program_id(1) == 0)
        def _():
            dst[...] = jnp.zeros_like(dst)

        @pl.when((pl.program_id(0) == 0) & (pl.program_id(1) == 0))
        def _():
            dl0_ref[...] = jnp.zeros_like(dl0_ref)
            dl1_ref[...] = jnp.zeros_like(dl1_ref)

        S = tuple(s_ref[0, 0, h] for h in range(H))
        _, vjp = jax.vjp(hg_chunk, S, p_ref[0], l0_ref[...], l1_ref[...])
        dS, dp, dl0, dl1 = vjp((tuple(dst[h] for h in range(H)), do_ref[0]))
        for h in range(H):
            dst[h] = dS[h]
        dp_ref[0] = dp
        dl0_ref[...] += jnp.broadcast_to(dl0, (8, HD))
        dl1_ref[...] += jnp.broadcast_to(dl1, (8, HD))

    acc = pl.BlockSpec((8, HD), lambda b, c: (0, 0))
    return pl.pallas_call(
        body, name="hg_bwd", grid=(NB, NCH),
        in_specs=[pl.BlockSpec((1, C, 3 * HD), lambda b, c: (b, _REV(c), 0))] + _hg_extra_specs(_REV)
        + [_STATE_SPEC(_REV), pl.BlockSpec((1, C, HD), lambda b, c: (b, _REV(c), 0))],
        out_specs=[pl.BlockSpec((1, C, 3 * HD), lambda b, c: (b, _REV(c), 0)), acc, acc],
        out_shape=[jax.ShapeDtypeStruct((NB, TP, 3 * HD), f32), jax.ShapeDtypeStruct((8, HD), f32),
                   jax.ShapeDtypeStruct((8, HD), f32)],
        scratch_shapes=[pltpu.VMEM((H, DK, DK), f32)],
        compiler_params=_cp(("arbitrary", "arbitrary")),
    )(proj3, l0, l1, s_saved, do)


def _gd_bwd(cv, proj3, alog, dtb, s_saved, do):
    def body(c_ref, ab_ref, al_ref, db_ref, s_ref, do_ref, dc_ref, dab_ref, dal_ref, ddb_ref, dst):
        @pl.when(pl.program_id(1) == 0)
        def _():
            dst[...] = jnp.zeros_like(dst)

        @pl.when((pl.program_id(0) == 0) & (pl.program_id(1) == 0))
        def _():
            dal_ref[...] = jnp.zeros_like(dal_ref)
            ddb_ref[...] = jnp.zeros_like(ddb_ref)

        S = tuple(s_ref[0, 0, h] for h in range(H))
        _, vjp = jax.vjp(gd_chunk, S, c_ref[0], ab_ref[0], al_ref[...], db_ref[...])
        dS, dc, dab, dal, ddb = vjp((tuple(dst[h] for h in range(H)), do_ref[0]))
        for h in range(H):
            dst[h] = dS[h]
        dc_ref[0] = dc
        dab_ref[0] = dab
        dal_ref[...] += jnp.broadcast_to(dal, (8, DK))
        ddb_ref[...] += jnp.broadcast_to(ddb, (8, DK))

    acc = pl.BlockSpec((8, DK), lambda b, c: (0, 0))
    return pl.pallas_call(
        body, name="gd_bwd", grid=(NB, NCH),
        in_specs=[pl.BlockSpec((1, C, 3 * HD), lambda b, c: (b, _REV(c), 0))] + _gd_extra_specs(_REV)
        + [_STATE_SPEC(_REV), pl.BlockSpec((1, C, HD), lambda b, c: (b, _REV(c), 0))],
        out_specs=[pl.BlockSpec((1, C, 3 * HD), lambda b, c: (b, _REV(c), 0)),
                   pl.BlockSpec((1, C, DK), lambda b, c: (b, _REV(c), 0)), acc, acc],
        out_shape=[jax.ShapeDtypeStruct((NB, TP, 3 * HD), f32), jax.ShapeDtypeStruct((NB, TP, DK), f32),
                   jax.ShapeDtypeStruct((8, DK), f32), jax.ShapeDtypeStruct((8, DK), f32)],
        scratch_shapes=[pltpu.VMEM((H, DK, DK), f32)],
        compiler_params=_cp(("arbitrary", "arbitrary")),
    )(cv, proj3, alog, dtb, s_saved, do)


def _conv_fwd(proj3, conv4):
    def body(x_ref, w_ref, y_ref):
        x = x_ref[0]
        row = lax.broadcasted_iota(jnp.int32, (TP, 1), 0)
        y = w_ref[3] * x
        for s in (1, 2, 3):
            y = y + w_ref[3 - s] * jnp.where(row >= s, pltpu.roll(x, s, 0), 0.0)
        y_ref[0] = y

    return pl.pallas_call(
        body, name="conv_fwd", grid=(NB, 3),
        in_specs=[pl.BlockSpec((1, TP, HD), lambda b, j: (b, 0, COL_QKV // HD + j)),
                  pl.BlockSpec((4, 1, HD), lambda b, j: (0, 0, j))],
        out_specs=pl.BlockSpec((1, TP, HD), lambda b, j: (b, 0, j)),
        out_shape=jax.ShapeDtypeStruct((NB, TP, 3 * HD), f32),
        compiler_params=_cp(("arbitrary", "arbitrary")),
    )(proj3, conv4)


def _conv_bwd(proj3, conv4, dy):
    def body(x_ref, w_ref, dy_ref, dx_ref, dw_ref):
        @pl.when(pl.program_id(1) == 0)
        def _():
            dw_ref[...] = jnp.zeros_like(dw_ref)

        x = x_ref[0]
        g = dy_ref[0]
        row = lax.broadcasted_iota(jnp.int32, (TP, 1), 0)
        dx = w_ref[3] * g
        dw_ref[3] += jnp.broadcast_to(jnp.sum(x * g, axis=0, keepdims=True), (8, HD))
        for s in (1, 2, 3):
            dx = dx + w_ref[3 - s] * jnp.where(row < TP - s, pltpu.roll(g, TP - s, 0), 0.0)
            xs = jnp.where(row >= s, pltpu.roll(x, s, 0), 0.0)
            dw_ref[3 - s] += jnp.broadcast_to(jnp.sum(xs * g, axis=0, keepdims=True), (8, HD))
        dx_ref[0] = dx

    return pl.pallas_call(
        body, name="conv_bwd", grid=(3, NB),
        in_specs=[pl.BlockSpec((1, TP, HD), lambda j, b: (b, 0, COL_QKV // HD + j)),
                  pl.BlockSpec((4, 1, HD), lambda j, b: (0, 0, j)), pl.BlockSpec((1, TP, HD), lambda j, b: (b, 0, j))],
        out_specs=[pl.BlockSpec((1, TP, HD), lambda j, b: (b, 0, j)), pl.BlockSpec((4, 8, HD), lambda j, b: (0, 0, j))],
        out_shape=[jax.ShapeDtypeStruct((NB, TP, 3 * HD), f32), jax.ShapeDtypeStruct((4, 8, 3 * HD), f32)],
        compiler_params=_cp(("arbitrary", "arbitrary")),
    )(proj3, conv4, dy)


def _gated(o, z, w):
    outs = []
    for hh in range(H):
        sl = slice(hh * DK, (hh + 1) * DK)
        outs.append(rmsnorm(o[:, sl], w) * silu(z[:, sl]))
    return jnp.concatenate(outs, axis=-1)


def _out_loss(o_hg, o_gd, proj, hgw, gdw, wout, hflat, fw, target):
    tm = 384

    def body(ohg_ref, ogd_ref, zhg_ref, zgd_ref, hgw_ref, gdw_ref, wo_ref, h_ref, fw_ref, tg_ref,
             loss_ref, dohg_ref, dogd_ref, dzhg_ref, dzgd_ref, dh_ref, dwo_ref, dhgw_ref, dgdw_ref, dfw_ref):
        i = pl.program_id(0)

        @pl.when(i == 0)
        def _():
            for r in (loss_ref, dwo_ref, dhgw_ref, dgdw_ref, dfw_ref):
                r[...] = jnp.zeros_like(r)

        row = i * tm + lax.broadcasted_iota(jnp.int32, (tm, 1), 0)
        tok = jnp.where(row >= TP, row - TP, row)
        valid = (tok >= PAD + N_META).astype(f32)
        hval = h_ref[...]
        tgt = tg_ref[...]

        def mix(ohg, ogd, zhg, zgd, w1, w2):
            return jnp.concatenate([_gated(ohg, zhg, w1), _gated(ogd, zgd, w2)], axis=-1)

        y, vjp_mix = jax.vjp(mix, ohg_ref[...], ogd_ref[...], zhg_ref[...], zgd_ref[...], hgw_ref[...], gdw_ref[...])
        out = bdot(y, wo_ref[...])

        def head(out, fwv):
            err = (rmsnorm(hval + out, fwv) - tgt) * valid
            return 0.5 * jnp.sum(jnp.mean(err * err, axis=-1, keepdims=True))

        loss, vjp_head = jax.vjp(head, out, fw_ref[...])
        dout, dfw = vjp_head(jnp.ones((), f32))
        dh_ref[...] = dout
        dy = bdot_nt(dout, wo_ref[...])
        dwo_ref[...] += bdot_tn(y, dout)
        dohg, dogd, dzhg, dzgd, dw1, dw2 = vjp_mix(dy)
        dohg_ref[...] = dohg
        dogd_ref[...] = dogd
        dzhg_ref[...] = dzhg
        dzgd_ref[...] = dzgd
        loss_ref[...] += jnp.broadcast_to(loss, (8, DK))
        dhgw_ref[...] += jnp.broadcast_to(dw1, (8, DK))
        dgdw_ref[...] += jnp.broadcast_to(dw2, (8, DK))
        dfw_ref[...] += jnp.broadcast_to(dfw, (8, D))

    row = lambda w: pl.BlockSpec((tm, w), lambda i: (i, 0))
    whole = lambda r, w: pl.BlockSpec((r, w), lambda i: (0, 0))
    col = lambda c0: pl.BlockSpec((tm, HD), lambda i: (i, c0 // HD))
    return pl.pallas_call(
        body, name="out_loss", grid=(N // tm,),
        in_specs=[row(HD), row(HD), col(COL_ZHG), col(COL_ZGD),
                  whole(1, DK), whole(1, DK), whole(D, D), row(D), whole(1, D), row(D)],
        out_specs=[whole(8, DK), row(HD), row(HD), row(HD), row(HD), row(D), whole(D, D),
                   whole(8, DK), whole(8, DK), whole(8, D)],
        out_shape=[jax.ShapeDtypeStruct((8, DK), f32)] + [jax.ShapeDtypeStruct((N, HD), f32)] * 4
        + [jax.ShapeDtypeStruct((N, D), f32), jax.ShapeDtypeStruct((D, D), f32),
           jax.ShapeDtypeStruct((8, DK), f32), jax.ShapeDtypeStruct((8, DK), f32), jax.ShapeDtypeStruct((8, D), f32)],
        compiler_params=_cp(("arbitrary",)),
    )(o_hg, o_gd, proj, proj, hgw, gdw, wout, hflat, fw, target)


def _in_bwd(pieces, wbig, hflat, norm_w, dh_res):
    tm = 192
    np_ = len(pieces)
    offs = [c0 for _, c0 in pieces]
    widths = [d.shape[1] for d, _ in pieces]

    def body(*refs):
        d_refs = refs[:np_]
        w_ref, h_ref, nw_ref, dhr_ref, dh_ref, dnw_ref = refs[np_:]

        @pl.when(pl.program_id(0) == 0)
        def _():
            dnw_ref[...] = jnp.zeros_like(dnw_ref)

        du = jnp.zeros((tm, D), f32)
        for d_ref, off, wd in zip(d_refs, offs, widths):
            du = du + bdot_nt(d_ref[...], w_ref[:, off:off + wd])
        _, vjp = jax.vjp(rmsnorm, h_ref[...], nw_ref[...])
        dh, dnw = vjp(du)
        dh_ref[...] = dh + dhr_ref[...]
        dnw_ref[...] += jnp.broadcast_to(dnw, (8, D))

    row = lambda w: pl.BlockSpec((tm, w), lambda i: (i, 0))
    return pl.pallas_call(
        body, name="in_bwd", grid=(N // tm,),
        in_specs=[row(w) for w in widths]
        + [pl.BlockSpec((D, PC), lambda i: (0, 0)), row(D), pl.BlockSpec((1, D), lambda i: (0, 0)), row(D)],
        out_specs=[row(D), pl.BlockSpec((8, D), lambda i: (0, 0))],
        out_shape=[jax.ShapeDtypeStruct((N, D), f32), jax.ShapeDtypeStruct((8, D), f32)],
        compiler_params=_cp(("arbitrary",)),
    )(*[d for d, _ in pieces], wbig, hflat, norm_w, dh_res)


def _w_grad(ub, d, name):
    n = d.shape[1]
    tn = min(n, 512)
    tk = N // 4

    def body(u_ref, d_ref, o_ref):
        @pl.when(pl.program_id(1) == 0)
        def _():
            o_ref[...] = jnp.zeros_like(o_ref)

        o_ref[...] += bdot_tn(u_ref[...], d_ref[...])

    return pl.pallas_call(
        body, name=name, grid=(n // tn, N // tk),
        in_specs=[pl.BlockSpec((tk, D), lambda j, k: (k, 0)), pl.BlockSpec((tk, tn), lambda j, k: (k, j))],
        out_specs=pl.BlockSpec((D, tn), lambda j, k: (0, j)),
        out_shape=jax.ShapeDtypeStruct((D, n), f32),
        compiler_params=_cp(("arbitrary", "arbitrary")),
    )(ub, d)


def _adam_math(g, w, m, v):
    m2 = ADAM_B1 * m + (1.0 - ADAM_B1) * g
    v2 = ADAM_B2 * v + (1.0 - ADAM_B2) * (g * g)
    m_hat = m2 / (1.0 - ADAM_B1 ** ADAM_STEP)
    v_hat = v2 / (1.0 - ADAM_B2 ** ADAM_STEP)
    delta = -ADAM_LR * (m_hat / (jnp.sqrt(v_hat) + ADAM_EPS) + ADAM_WD * w)
    return delta, m2, v2


def _adamw(gs, w, m, v, name):
    R, Cc = w.shape
    tr = 256 if R % 256 == 0 else R
    ng = len(gs)

    def body(*refs):
        g = refs[0][...]
        for r in refs[1:ng]:
            g = g + r[...]
        w_ref, m_ref, v_ref, g_ref, d_ref, m2_ref, v2_ref = refs[ng:]
        delta, m2, v2 = _adam_math(g, w_ref[...], m_ref[...], v_ref[...])
        g_ref[...] = g
        d_ref[...] = delta
        m2_ref[...] = m2
        v2_ref[...] = v2

    spec = pl.BlockSpec((tr, Cc), lambda i: (i, 0))
    return pl.pallas_call(
        body, name=name, grid=(R // tr,),
        in_specs=[spec] * (ng + 3), out_specs=[spec] * 4,
        out_shape=[jax.ShapeDtypeStruct((R, Cc), f32)] * 4,
        compiler_params=_cp(("arbitrary",)),
    )(*gs, w, m, v)


def _sum_slots(r, name):
    S, R, Cc = r.shape
    tr = 256 if R % 256 == 0 else R

    def body(r_ref, o_ref):
        acc = r_ref[0]
        for s in range(1, S):
            acc = acc + r_ref[s]
        o_ref[...] = acc

    return pl.pallas_call(
        body, name=name, grid=(R // tr,),
        in_specs=[pl.BlockSpec((S, tr, Cc), lambda i: (0, i, 0))], out_specs=pl.BlockSpec((tr, Cc), lambda i: (i, 0)),
        out_shape=jax.ShapeDtypeStruct((R, Cc), f32),
        compiler_params=_cp(("arbitrary",)),
    )(r)


def _place():
    x, y, c = lax.axis_index("x"), lax.axis_index("y"), lax.axis_index("c")
    return x, y, c, [(1 - x, y), (x, 1 - y), (1 - x, 1 - y)]


def _gather_weights(shards):
    na = len(shards)

    def body(*refs):
        srcs, dsts = refs[:na], refs[na:2 * na]
        send_sems, recv_sems, loc_sems = refs[2 * na:]
        x, y, c, chips = _place()
        me = 2 * x + y
        locs = [pltpu.make_async_copy(s, d.at[me], loc_sems.at[i]) for i, (s, d) in enumerate(zip(srcs, dsts))]
        for cp in locs:
            cp.start()

        def copy(j, i, slot):
            px, py = chips[j]
            return pltpu.make_async_remote_copy(
                src_ref=srcs[i], dst_ref=dsts[i].at[slot], send_sem=send_sems.at[na * j + i],
                recv_sem=recv_sems.at[na * j + i], device_id=(px, py, c), device_id_type=MESH)

        sends = [copy(j, i, me) for j in range(3) for i in range(na)]
        for cp in sends:
            cp.start()
        for j, (px, py) in enumerate(chips):
            for i in range(na):
                copy(j, i, 2 * px + py).wait_recv()
        for cp in sends:
            cp.wait_send()
        for cp in locs:
            cp.wait()

    return pl.pallas_call(
        body, name="gather_weights",
        in_specs=[ANY] * na, out_specs=[ANY] * na,
        out_shape=[jax.ShapeDtypeStruct((4,) + s.shape, s.dtype) for s in shards],
        scratch_shapes=[pltpu.SemaphoreType.DMA((3 * na,)), pltpu.SemaphoreType.DMA((3 * na,)),
                        pltpu.SemaphoreType.DMA((na,))],
        compiler_params=pltpu.CompilerParams(has_side_effects=True),
    )(*shards)


def _scatter_grads(g_in, g_out, pack):
    R = pack.shape[0]

    def body(gi, go, pk, ri, ro, rp, send_sems, recv_sems, loc_sems):
        x, y, c, chips = _place()
        me = 2 * x + y
        me8 = 4 * x + 2 * y + c
        locs = [pltpu.make_async_copy(gi.at[me], ri.at[me], loc_sems.at[0]),
                pltpu.make_async_copy(go.at[me], ro.at[me], loc_sems.at[1]),
                pltpu.make_async_copy(pk, rp.at[me8], loc_sems.at[2])]
        for cp in locs:
            cp.start()

        def big(j, i, src_slot, dst_slot):
            px, py = chips[j]
            s, d = ((gi, ri), (go, ro))[i]
            return pltpu.make_async_remote_copy(
                src_ref=s.at[src_slot], dst_ref=d.at[dst_slot], send_sem=send_sems.at[2 * j + i],
                recv_sem=recv_sems.at[2 * j + i], device_id=(px, py, c), device_id_type=MESH)

        flips = [(fx, fy, fc) for fx in (0, 1) for fy in (0, 1) for fc in (0, 1)][1:]

        def small(k, slot):
            fx, fy, fc = flips[k]
            return pltpu.make_async_remote_copy(
                src_ref=pk, dst_ref=rp.at[slot], send_sem=send_sems.at[6 + k], recv_sem=recv_sems.at[6 + k],
                device_id=(x ^ fx, y ^ fy, c ^ fc), device_id_type=MESH)

        sends = [big(j, i, 2 * chips[j][0] + chips[j][1], me) for j in range(3) for i in range(2)]
        sends += [small(k, me8) for k in range(7)]
        for cp in sends:
            cp.start()
        for j, (px, py) in enumerate(chips):
            for i in range(2):
                big(j, i, me, 2 * px + py).wait_recv()
        for k, (fx, fy, fc) in enumerate(flips):
            small(k, 4 * (x ^ fx) + 2 * (y ^ fy) + (c ^ fc)).wait_recv()
        for cp in sends:
            cp.wait_send()
        for cp in locs:
            cp.wait()

    return pl.pallas_call(
        body, name="scatter_grads",
        in_specs=[ANY] * 3, out_specs=[ANY] * 3,
        out_shape=[jax.ShapeDtypeStruct(g_in.shape, f32), jax.ShapeDtypeStruct(g_out.shape, f32),
                   jax.ShapeDtypeStruct((8, R, 128), f32)],
        scratch_shapes=[pltpu.SemaphoreType.DMA((13,)), pltpu.SemaphoreType.DMA((13,)), pltpu.SemaphoreType.DMA((3,))],
        compiler_params=pltpu.CompilerParams(has_side_effects=True),
    )(g_in, g_out, pack)


def _swap_sibling(a, b):
    def body(a_ref, b_ref, pa_ref, pb_ref, send_sems, recv_sems):
        x, y, c, _ = _place()
        cps = [pltpu.make_async_remote_copy(src_ref=s, dst_ref=d, send_sem=send_sems.at[i], recv_sem=recv_sems.at[i],
                                            device_id=(x, y, 1 - c), device_id_type=MESH)
               for i, (s, d) in enumerate(((a_ref, pa_ref), (b_ref, pb_ref)))]
        for cp in cps:
            cp.start()
        for cp in cps:
            cp.wait()

    return pl.pallas_call(
        body, name="swap_sibling",
        in_specs=[ANY] * 2, out_specs=[ANY] * 2,
        out_shape=[jax.ShapeDtypeStruct(a.shape, a.dtype), jax.ShapeDtypeStruct(b.shape, b.dtype)],
        scratch_shapes=[pltpu.SemaphoreType.DMA((2,)), pltpu.SemaphoreType.DMA((2,))],
        compiler_params=pltpu.CompilerParams(has_side_effects=True),
    )(a, b)


def _rows8(a):
    flat = a.reshape(-1)
    n = flat.shape[0]
    rows = -(-n // 1024) * 8
    return jnp.pad(flat, (0, rows * 128 - n)).reshape(rows, 128)


def kernel(x, meta_tokens, norm_w, w_in, conv_w, hg_lb_logits, hg_norm_w, gdn_A_log, gdn_dt_bias, gdn_norm_w, w_out, final_norm_w, loss_target, m_meta_tokens, m_norm_w, m_w_in, m_conv_w, m_hg_lb_logits, m_hg_norm_w, m_gdn_A_log, m_gdn_dt_bias, m_gdn_norm_w, m_w_out, m_final_norm_w, v_meta_tokens, v_norm_w, v_w_in, v_conv_w, v_hg_lb_logits, v_hg_norm_w, v_gdn_A_log, v_gdn_dt_bias, v_gdn_norm_w, v_w_out, v_final_norm_w):
    me = 2 * lax.axis_index("x") + lax.axis_index("y")

    g_win, g_wout, g_conv, g_meta = _gather_weights(
        [w_in[0].astype(bf16), w_out[0].astype(bf16), conv_w[0], meta_tokens])
    w_full = jnp.transpose(g_win, (1, 0, 2)).reshape(D, IN_COLS)
    wbig = jnp.pad(w_full, ((0, 0), (0, PC - IN_COLS)))
    wout_full = g_wout.reshape(D, D)
    conv4 = jnp.transpose(g_conv, (1, 0, 2)).reshape(4, 1, 3 * HD)
    meta_full = jnp.transpose(g_meta, (1, 0, 2)).reshape(N_META, D)

    (loss8, grad_x, d_meta, d_nw, gw_full, d_conv, d_lb, d_hgw, d_alog, d_dtb, d_gdw, g_wout_part, d_fw) = _local_step(
        x, loss_target, wbig, wout_full, conv4, meta_full, norm_w, hg_lb_logits, hg_norm_w, gdn_A_log, gdn_dt_bias,
        gdn_norm_w, final_norm_w)

    g_in_blocks = jnp.transpose(gw_full.reshape(D, 4, SHARD_COLS), (1, 0, 2))
    g_out_blocks = g_wout_part.reshape(4, D // 4, D)
    pack = jnp.concatenate([
        loss8, d_nw[0].reshape(8, 128), d_lb.reshape(8, 128), d_hgw, _rows8(d_alog[0, :H]), _rows8(d_dtb[0, :H]),
        d_gdw, d_fw[0].reshape(8, 128), d_meta.reshape(128, 128), d_conv.reshape(48, 128)], axis=0)
    return _reduce_and_update(
        me, grad_x, g_in_blocks, g_out_blocks, pack, meta_tokens, norm_w, w_in, conv_w, hg_lb_logits, hg_norm_w, gdn_A_log,
        gdn_dt_bias, gdn_norm_w, w_out, final_norm_w, m_meta_tokens, m_norm_w, m_w_in, m_conv_w, m_hg_lb_logits,
        m_hg_norm_w, m_gdn_A_log, m_gdn_dt_bias, m_gdn_norm_w, m_w_out, m_final_norm_w, v_meta_tokens, v_norm_w, v_w_in,
        v_conv_w, v_hg_lb_logits, v_hg_norm_w, v_gdn_A_log, v_gdn_dt_bias, v_gdn_norm_w, v_w_out, v_final_norm_w)


def _local_step(x, loss_target, wbig, wout_full, conv4, meta_full, norm_w, hg_lb_logits, hg_norm_w, gdn_A_log, gdn_dt_bias,
                gdn_norm_w, final_norm_w):
    h3 = jnp.concatenate([jnp.zeros((NB, PAD, D), f32), jnp.broadcast_to(meta_full[None], (NB, N_META, D)), x], axis=1)
    hflat = h3.reshape(N, D)
    target = jnp.pad(loss_target, ((0, 0), (PAD + N_META, 0), (0, 0))).reshape(N, D)
    l0, l1 = hg_lb_logits[0:1], hg_lb_logits[1:2]
    alog = jnp.pad(gdn_A_log, ((0, 0), (0, DK - H)))
    dtb = jnp.pad(gdn_dt_bias, ((0, 0), (0, DK - H)))
    fw = final_norm_w.reshape(1, D)

    proj, ub = _in_proj(hflat, norm_w, wbig)
    proj3 = proj.reshape(NB, TP, PC)
    o_hg, s_hg = _hg_fwd(proj3, l0, l1)
    cv = _conv_fwd(proj3, conv4)
    o_gd, s_gd = _gd_fwd(cv, proj3, alog, dtb)
    (loss8, d_ohg, d_ogd, d_zhg, d_zgd, dh_res, g_wout_part, d_hgw, d_gdw, d_fw) = _out_loss(
        o_hg.reshape(N, HD), o_gd.reshape(N, HD), proj, hg_norm_w, gdn_norm_w, wout_full, hflat, fw, target)
    d_hg, d_l0, d_l1 = _hg_bwd(proj3, l0, l1, s_hg, d_ohg.reshape(NB, TP, HD))
    d_cv, d_ab, d_alog, d_dtb = _gd_bwd(cv, proj3, alog, dtb, s_gd, d_ogd.reshape(NB, TP, HD))
    d_qkv, d_conv4 = _conv_bwd(proj3, conv4, d_cv)
    d_hg2, d_qkv2, d_ab2 = d_hg.reshape(N, 3 * HD), d_qkv.reshape(N, 3 * HD), d_ab.reshape(N, DK)
    pieces = [(d_hg2, COL_HG), (d_zhg, COL_ZHG), (d_qkv2, COL_QKV), (d_zgd, COL_ZGD), (d_ab2, COL_AB)]
    dh, d_nw = _in_bwd(pieces, wbig, hflat, norm_w, dh_res)
    gws = [_w_grad(ub, d, "w_grad_%d" % c0) for d, c0 in pieces]
    gw_full = jnp.concatenate(gws[:4] + [gws[4][:, 0:IN_COLS - COL_AB]], axis=1)

    dh3 = dh.reshape(NB, TP, D)
    grad_x = dh3[:, PAD + N_META:, :]
    d_meta = jnp.sum(dh3[:, PAD:PAD + N_META, :], axis=0)
    d_conv = d_conv4[:, 0, :]
    d_lb = jnp.concatenate([d_l0[0:1], d_l1[0:1]], axis=0)
    return loss8, grad_x, d_meta, d_nw, gw_full, d_conv, d_lb, d_hgw, d_alog, d_dtb, d_gdw, g_wout_part, d_fw


def _reduce_and_update(me, grad_x, g_in_blocks, g_out_blocks, pack, meta_tokens, norm_w, w_in, conv_w, hg_lb_logits, hg_norm_w,
                       gdn_A_log, gdn_dt_bias, gdn_norm_w, w_out, final_norm_w, m_meta_tokens, m_norm_w, m_w_in, m_conv_w,
                       m_hg_lb_logits, m_hg_norm_w, m_gdn_A_log, m_gdn_dt_bias, m_gdn_norm_w, m_w_out, m_final_norm_w,
                       v_meta_tokens, v_norm_w, v_w_in, v_conv_w, v_hg_lb_logits, v_hg_norm_w, v_gdn_A_log, v_gdn_dt_bias,
                       v_gdn_norm_w, v_w_out, v_final_norm_w):
    r_in, r_out, r_pack = _scatter_grads(g_in_blocks, g_out_blocks, pack)
    s_in = _sum_slots(r_in, "sum_w_in")
    s_out = _sum_slots(r_out, "sum_w_out")
    p_in, p_out = _swap_sibling(s_in, s_out)
    small = _sum_slots(r_pack, "sum_small")

    gi, di, mi, vi = _adamw([s_in, p_in], w_in[0], m_w_in[0], v_w_in[0], "adamw_w_in")
    go, do_, mo, vo = _adamw([s_out, p_out], w_out[0], m_w_out[0], v_w_out[0], "adamw_w_out")

    g_meta_full = small[64:192].reshape(N_META, D)
    g_meta_loc = lax.dynamic_slice(g_meta_full, (0, me * 256), (N_META, 256))
    gm, dm, mm_, vm = _adamw([g_meta_loc], meta_tokens, m_meta_tokens, v_meta_tokens, "adamw_meta")
    g_conv_full = small[192:240].reshape(4, 1536)
    g_conv_loc = lax.dynamic_slice(g_conv_full, (0, me * 384), (4, 384))
    gc, dc, mc, vc = _adamw([g_conv_loc], conv_w[0], m_conv_w[0], v_conv_w[0], "adamw_conv")

    reps = [(norm_w, m_norm_w, v_norm_w), (hg_lb_logits, m_hg_lb_logits, v_hg_lb_logits),
            (hg_norm_w, m_hg_norm_w, v_hg_norm_w), (gdn_A_log, m_gdn_A_log, v_gdn_A_log),
            (gdn_dt_bias, m_gdn_dt_bias, v_gdn_dt_bias), (gdn_norm_w, m_gdn_norm_w, v_gdn_norm_w),
            (final_norm_w, m_final_norm_w, v_final_norm_w)]
    wp = jnp.concatenate([_rows8(t[0]) for t in reps], axis=0)
    mp = jnp.concatenate([_rows8(t[1]) for t in reps], axis=0)
    vp = jnp.concatenate([_rows8(t[2]) for t in reps], axis=0)
    gr, dr, mr, vr = _adamw([small[8:64]], wp, mp, vp, "adamw_small")

    def unpack(p):
        outs = []
        for i, t in enumerate(reps):
            n = t[0].size
            outs.append(p[8 * i:8 * i + 8].reshape(-1)[:n].reshape(t[0].shape))
        return outs

    def leaves(meta_v, conv_v, in_v, out_v, rep_p):
        nw, lb, hgw, al, db, gdw, fwv = unpack(rep_p)
        return [meta_v, nw, in_v[None], conv_v[None], lb, hgw, al, db, gdw, out_v[None], fwv]

    loss = small[0, 0]
    return (loss, grad_x, *leaves(gm, gc, gi, go, gr), *leaves(dm, dc, di, do_, dr),
            *leaves(mm_, mc, mi, mo, mr), *leaves(vm, vc, vi, vo, vr))
```

```python
import functools

import jax
import jax.numpy as jnp
from jax import lax
from jax.experimental import pallas as pl
from jax.experimental.pallas import tpu as pltpu

f32 = jnp.float32
bf16 = jnp.bfloat16
MESH = pl.DeviceIdType.MESH
ANY = pl.BlockSpec(memory_space=pl.ANY)

D = 1024
NB = 2
N_META = 16
SEQ = 2048
PAD = 48
TP = PAD + N_META + SEQ
C = 64
NCH = TP // C
N = NB * TP
H = 4
DK = 128
HD = H * DK
PC = 4224
IN_COLS = 4104
SHARD_COLS = IN_COLS // 4
COL_HG, COL_ZHG, COL_QKV, COL_ZGD, COL_AB = 0, 3 * HD, 4 * HD, 7 * HD, 8 * HD
EPS = 1e-6
ADAM_LR, ADAM_B1, ADAM_B2, ADAM_EPS, ADAM_WD, ADAM_STEP = 0.001, 0.9, 0.999, 1e-08, 0.01, 10
VMEM_LIMIT = 56 * 1024 * 1024

P_HG = dict(lvl=1, av=1, qs=1, su=1)
P_GD = dict(kk=1, inv=3, sol=3, ws=1, qk=1, o=1, su=1)


def _cp(sem=None, **kw):
    return pltpu.CompilerParams(dimension_semantics=sem, vmem_limit_bytes=VMEM_LIMIT, **kw)


_DIMS = {"nn": (((1,), (0,)), ((), ())), "nt": (((1,), (1,)), ((), ())), "tn": (((0,), (0,)), ((), ()))}


def _split(x):
    hi = x.astype(bf16)
    return hi, (x - hi.astype(f32)).astype(bf16)


def _dg(a, b, kind, passes):
    d = lambda x, y: lax.dot_general(x, y, _DIMS[kind], preferred_element_type=f32)
    if passes == 1:
        return d(a.astype(bf16), b.astype(bf16))
    ah, al = _split(a)
    bh, bl = _split(b)
    return d(ah, bh) + d(ah, bl) + d(al, bh)


@functools.partial(jax.custom_vjp, nondiff_argnums=(2, 3))
def mmx(a, b, kind, passes):
    return _dg(a, b, kind, passes)


def _mmx_fwd(a, b, kind, passes):
    return _dg(a, b, kind, passes), (a, b)


def _mmx_bwd(kind, passes, res, g):
    a, b = res
    if kind == "nn":
        return _dg(g, b, "nt", passes), _dg(a, g, "tn", passes)
    if kind == "nt":
        return _dg(g, b, "nn", passes), _dg(g, a, "tn", passes)
    return _dg(b, g, "nt", passes), _dg(a, g, "nn", passes)


mmx.defvjp(_mmx_fwd, _mmx_bwd)


def _mask_dg(mask, x, kind):
    n = x.shape[1]
    xh, xl = _split(x)
    r = lax.dot_general(mask, jnp.concatenate([xh, xl], axis=1), _DIMS[kind], preferred_element_type=f32)
    return r[:, :n] + r[:, n:]


@jax.custom_vjp
def mask_mm(mask, x):
    return _mask_dg(mask, x, "nn")


def _mask_fwd(mask, x):
    return _mask_dg(mask, x, "nn"), mask


def _mask_bwd(mask, g):
    return None, _mask_dg(mask, g, "tn")


mask_mm.defvjp(_mask_fwd, _mask_bwd)


def bdot(a, b):
    return jnp.dot(a.astype(bf16), b.astype(bf16), preferred_element_type=f32)


def bdot_nt(a, b):
    return lax.dot_general(a.astype(bf16), b.astype(bf16), _DIMS["nt"], preferred_element_type=f32)


def bdot_tn(a, b):
    return lax.dot_general(a.astype(bf16), b.astype(bf16), _DIMS["tn"], preferred_element_type=f32)


def _iota2(n, m):
    return lax.broadcasted_iota(jnp.int32, (n, m), 0), lax.broadcasted_iota(jnp.int32, (n, m), 1)


def silu(x):
    return x * jax.nn.sigmoid(x)


def softplus(x):
    return jnp.maximum(x, 0.0) + jnp.log(1.0 + jnp.exp(-jnp.abs(x)))


def rmsnorm(x, w):
    return x * lax.rsqrt(jnp.mean(x * x, axis=-1, keepdims=True) + EPS) * w


def hg_masks():
    t, r = _iota2(C, C)
    mats = [r <= t, r > t]
    lvl = []
    for l in range(1, 7):
        sz = 1 << l
        half = sz >> 1
        seg_t = t >> l
        upper_t = (t & (sz - 1)) >= half
        mid_t = seg_t * sz + half - 1
        mats.append(upper_t & (r > mid_t) & (r <= t))
        mats.append((~upper_t) & (r > t) & (r <= mid_t))
        lvl.append(((seg_t == (r >> l)) & upper_t & ((r & (sz - 1)) < half)).astype(f32))
    stk = jnp.concatenate([m.astype(bf16) for m in mats], axis=0)
    return stk, lvl, (t == r).astype(f32)


def hg_chunk(St, p, l0, l1):
    pq, pf, v = p[:, 0:HD], p[:, HD:2 * HD], p[:, 2 * HD:3 * HD]
    m = jnp.maximum(l0, l1)
    e0 = jnp.exp(l0 - m)
    e1 = jnp.exp(l1 - m)
    lb = e0 / (e0 + e1)
    q = silu(pq)
    f = lb + (1.0 - lb) * jax.nn.sigmoid(pf)
    k = 1.0 - f
    logf = jnp.log(f)
    stk, lvl, eye = hg_masks()
    Dm = mask_mm(stk, logf)
    qs = [q] + [q * jnp.exp(Dm[(2 + 2 * i) * C:(3 + 2 * i) * C]) for i in range(6)]
    ks = [k] + [k * jnp.exp(Dm[(3 + 2 * i) * C:(4 + 2 * i) * C]) for i in range(6)]
    msk = [eye] + lvl
    qG = q * jnp.exp(Dm[0:C])
    kR = k * jnp.exp(Dm[C:2 * C])
    eGl = jnp.exp(jnp.sum(logf, axis=0, keepdims=True))
    heads = range(H)
    sls = [slice(h * DK, (h + 1) * DK) for h in heads]
    parts = [[msk[i] * mmx(qs[i][:, sl], ks[i][:, sl], "nt", P_HG["lvl"]) for sl in sls] for i in range(7)]
    A = [sum(parts[i][h] for i in range(1, 7)) + parts[0][h] for h in heads]
    qS = [mmx(qG[:, sls[h]], St[h], "nt", P_HG["qs"]) for h in heads]
    Sn = [St[h] * eGl[:, sls[h]] + mmx(v[:, sls[h]], kR[:, sls[h]], "tn", P_HG["su"]) for h in heads]
    outs = [mmx(A[h], v[:, sls[h]], "nn", P_HG["av"]) + qS[h] for h in heads]
    return tuple(Sn), jnp.concatenate(outs, axis=1)


def gd_chunk(S, c, ab, alog, dtb):
    qa, ka, va = silu(c[:, 0:HD]), silu(c[:, HD:2 * HD]), silu(c[:, 2 * HD:3 * HD])
    g4 = -jnp.exp(alog) * softplus(ab + dtb)
    b4 = jax.nn.sigmoid(ab)
    t, r = _iota2(C, C)
    gam4 = mask_mm((r <= t).astype(bf16), g4)
    grev4 = mask_mm((r > t).astype(bf16), g4)
    gam4T = gam4.T
    glast4 = jnp.sum(g4, axis=0, keepdims=True)
    lane = lax.broadcasted_iota(jnp.int32, (1, DK), 1)
    subl = lax.broadcasted_iota(jnp.int32, (8, 1), 0)
    eye = (t == r).astype(f32)
    strict = (r < t).astype(f32)
    bd = ((t >> 4) == (r >> 4)).astype(f32)
    heads = range(H)
    inv = lambda a, b: [mmx(a[h], b[h], "nn", P_GD["inv"]) for h in heads]
    sls = [slice(h * DK, (h + 1) * DK) for h in heads]
    v = [va[:, sl] for sl in sls]
    q = [qa[:, sl] for sl in sls]
    k = [ka[:, sl] for sl in sls]
    q = [x * lax.rsqrt(jnp.sum(x * x, -1, keepdims=True) + EPS) * (DK ** -0.5) for x in q]
    k = [x * lax.rsqrt(jnp.sum(x * x, -1, keepdims=True) + EPS) for x in k]
    oh = [(lane == h).astype(f32) for h in heads]
    gam_c = [jnp.sum(gam4 * oh[h], -1, keepdims=True) for h in heads]
    grev_c = [jnp.sum(grev4 * oh[h], -1, keepdims=True) for h in heads]
    beta = [jnp.sum(b4 * (lane == h + H).astype(f32), -1, keepdims=True) for h in heads]
    glast = [jnp.sum(glast4 * oh[h], -1, keepdims=True) for h in heads]
    gam_r = [jnp.sum(gam4T[0:8, :] * (subl == h).astype(f32), axis=0, keepdims=True) for h in heads]
    dec = [jnp.exp(jnp.where(r <= t, gam_c[h] - gam_r[h], -1e30)) for h in heads]
    egam = [jnp.exp(gam_c[h]) for h in heads]
    kk = [mmx(k[h], k[h], "nt", P_GD["kk"]) for h in heads]
    qk = [mmx(q[h], k[h], "nt", P_GD["qk"]) * dec[h] for h in heads]
    A = [beta[h] * kk[h] * dec[h] * strict for h in heads]
    Dg = [A[h] * bd for h in heads]
    L = [A[h] - Dg[h] for h in heads]
    ImD = [eye - Dg[h] for h in heads]
    D2 = inv(Dg, Dg)
    P1 = inv(ImD, [eye + x for x in D2])
    D4 = inv(D2, D2)
    P2 = inv(P1, [eye + x for x in D4])
    D8 = inv(D4, D4)
    M = inv(P2, [eye + x for x in D8])
    Nn = inv(M, L)
    N2 = inv(Nn, Nn)
    T1 = inv([eye - x for x in Nn], [eye + x for x in N2])
    Tinv = inv(T1, M)
    rhs = [jnp.concatenate([beta[h] * v[h], (beta[h] * egam[h]) * k[h]], axis=1) for h in heads]
    sol = [mmx(Tinv[h], rhs[h], "nn", P_GD["sol"]) for h in heads]
    qS = [mmx(q[h] * egam[h], S[h], "nn", P_GD["o"]) for h in heads]
    wS = [mmx(sol[h][:, DK:2 * DK], S[h], "nn", P_GD["ws"]) for h in heads]
    u = [sol[h][:, 0:DK] - wS[h] for h in heads]
    outs = [qS[h] + mmx(qk[h], u[h], "nn", P_GD["o"]) for h in heads]
    Sn = [jnp.exp(glast[h]) * S[h] + mmx(k[h] * jnp.exp(grev_c[h]), u[h], "tn", P_GD["su"]) for h in heads]
    return tuple(Sn), jnp.concatenate(outs, axis=1)


def _in_proj(hflat, norm_w, wbig):
    tm = 384

    def body(h_ref, nw_ref, w_ref, p_ref, u_ref):
        u = rmsnorm(h_ref[...], nw_ref[...]).astype(bf16)
        u_ref[...] = u
        p_ref[...] = jnp.dot(u, w_ref[...], preferred_element_type=f32)

    return pl.pallas_call(
        body, name="in_proj", grid=(N // tm,),
        in_specs=[pl.BlockSpec((tm, D), lambda i: (i, 0)), pl.BlockSpec((1, D), lambda i: (0, 0)),
                  pl.BlockSpec((D, PC), lambda i: (0, 0))],
        out_specs=[pl.BlockSpec((tm, PC), lambda i: (i, 0)), pl.BlockSpec((tm, D), lambda i: (i, 0))],
        out_shape=[jax.ShapeDtypeStruct((N, PC), f32), jax.ShapeDtypeStruct((N, D), bf16)],
        compiler_params=_cp(("arbitrary",)),
    )(hflat, norm_w, wbig)


_STATE_SPEC = lambda ix: pl.BlockSpec((1, 1, H, DK, DK), lambda b, c: (b, ix(c), 0, 0, 0))
_REV = lambda c: NCH - 1 - c
_FWD = lambda c: c


def _scan_fwd(chunk_fn, name, x3, xcol, extra, extra_specs):
    ne = len(extra)

    def body(*refs):
        x_ref = refs[0]
        e_refs = refs[1:1 + ne]
        o_ref, s_ref, st = refs[1 + ne:]

        @pl.when(pl.program_id(1) == 0)
        def _():
            st[...] = jnp.zeros_like(st)

        S = tuple(st[h] for h in range(H))
        s_ref[0, 0] = st[...]
        Sn, o = chunk_fn(S, x_ref[0], *[e[0] if len(e.shape) == 3 else e[...] for e in e_refs])
        for h in range(H):
            st[h] = Sn[h]
        o_ref[0] = o

    return pl.pallas_call(
        body, name=name, grid=(NB, NCH),
        in_specs=[pl.BlockSpec((1, C, 3 * HD), lambda b, c: (b, c, xcol))] + extra_specs(_FWD),
        out_specs=[pl.BlockSpec((1, C, HD), lambda b, c: (b, c, 0)), _STATE_SPEC(_FWD)],
        out_shape=[jax.ShapeDtypeStruct((NB, TP, HD), f32), jax.ShapeDtypeStruct((NB, NCH, H, DK, DK), f32)],
        scratch_shapes=[pltpu.VMEM((H, DK, DK), f32)],
        compiler_params=_cp(("arbitrary", "arbitrary")),
    )(x3, *extra)


def _hg_extra_specs(ix):
    return [pl.BlockSpec((1, HD), lambda b, c: (0, 0)), pl.BlockSpec((1, HD), lambda b, c: (0, 0))]


def _gd_extra_specs(ix):
    return [pl.BlockSpec((1, C, DK), lambda b, c: (b, ix(c), COL_AB // DK)),
            pl.BlockSpec((1, DK), lambda b, c: (0, 0)), pl.BlockSpec((1, DK), lambda b, c: (0, 0))]


def _hg_fwd(proj3, l0, l1):
    return _scan_fwd(hg_chunk, "hg_fwd", proj3, 0, [l0, l1], _hg_extra_specs)


def _gd_fwd(cv, proj3, alog, dtb):
    return _scan_fwd(gd_chunk, "gd_fwd", cv, 0, [proj3, alog, dtb], _gd_extra_specs)


def _hg_bwd(proj3, l0, l1, s_saved, do):
    def body(p_ref, l0_ref, l1_ref, s_ref, do_ref, dp_ref, dl0_ref, dl1_ref, dst):
        @pl.when(pl.program_id(1) == 0)
        def _():
            dst[...] = jnp.zeros_like(dst)

        @pl.when((pl.program_id(0) == 0) & (pl.program_id(1) == 0))
        def _():
            dl0_ref[...] = jnp.zeros_like(dl0_ref)
            dl1_ref[...] = jnp.zeros_like(dl1_ref)

        S = tuple(s_ref[0, 0, h] for h in range(H))
        _, vjp = jax.vjp(hg_chunk, S, p_ref[0], l0_ref[...], l1_ref[...])
        dS, dp, dl0, dl1 = vjp((tuple(dst[h] for h in range(H)), do_ref[0]))
        for h in range(H):
            dst[h] = dS[h]
        dp_ref[0] = dp
        dl0_ref[...] += jnp.broadcast_to(dl0, (8, HD))
        dl1_ref[...] += jnp.broadcast_to(dl1, (8, HD))

    acc = pl.BlockSpec((8, HD), lambda b, c: (0, 0))
    return pl.pallas_call(
        body, name="hg_bwd", grid=(NB, NCH),
        in_specs=[pl.BlockSpec((1, C, 3 * HD), lambda b, c: (b, _REV(c), 0))] + _hg_extra_specs(_REV)
        + [_STATE_SPEC(_REV), pl.BlockSpec((1, C, HD), lambda b, c: (b, _REV(c), 0))],
        out_specs=[pl.BlockSpec((1, C, 3 * HD), lambda b, c: (b, _REV(c), 0)), acc, acc],
        out_shape=[jax.ShapeDtypeStruct((NB, TP, 3 * HD), f32), jax.ShapeDtypeStruct((8, HD), f32),
                   jax.ShapeDtypeStruct((8, HD), f32)],
        scratch_shapes=[pltpu.VMEM((H, DK, DK), f32)],
        compiler_params=_cp(("arbitrary", "arbitrary")),
    )(proj3, l0, l1, s_saved, do)


def _gd_bwd(cv, proj3, alog, dtb, s_saved, do):
    def body(c_ref, ab_ref, al_ref, db_ref, s_ref, do_ref, dc_ref, dab_ref, dal_ref, ddb_ref, dst):
        @pl.when(pl.program_id(1) == 0)
        def _():
            dst[...] = jnp.zeros_like(dst)

        @pl.when((pl.program_id(0) == 0) & (pl.program_id(1) == 0))
        def _():
            dal_ref[...] = jnp.zeros_like(dal_ref)
            ddb_ref[...] = jnp.zeros_like(ddb_ref)

        S = tuple(s_ref[0, 0, h] for h in range(H))
        _, vjp = jax.vjp(gd_chunk, S, c_ref[0], ab_ref[0], al_ref[...], db_ref[...])
        dS, dc, dab, dal, ddb = vjp((tuple(dst[h] for h in range(H)), do_ref[0]))
        for h in range(H):
            dst[h] = dS[h]
        dc_ref[0] = dc
        dab_ref[0] = dab
        dal_ref[...] += jnp.broadcast_to(dal, (8, DK))
        ddb_ref[...] += jnp.broadcast_to(ddb, (8, DK))

    acc = pl.BlockSpec((8, DK), lambda b, c: (0, 0))
    return pl.pallas_call(
        body, name="gd_bwd", grid=(NB, NCH),
        in_specs=[pl.BlockSpec((1, C, 3 * HD), lambda b, c: (b, _REV(c), 0))] + _gd_extra_specs(_REV)
        + [_STATE_SPEC(_REV), pl.BlockSpec((1, C, HD), lambda b, c: (b, _REV(c), 0))],
        out_specs=[pl.BlockSpec((1, C, 3 * HD), lambda b, c: (b, _REV(c), 0)),
                   pl.BlockSpec((1, C, DK), lambda b, c: (b, _REV(c), 0)), acc, acc],
        out_shape=[jax.ShapeDtypeStruct((NB, TP, 3 * HD), f32), jax.ShapeDtypeStruct((NB, TP, DK), f32),
                   jax.ShapeDtypeStruct((8, DK), f32), jax.ShapeDtypeStruct((8, DK), f32)],
        scratch_shapes=[pltpu.VMEM((H, DK, DK), f32)],
        compiler_params=_cp(("arbitrary", "arbitrary")),
    )(cv, proj3, alog, dtb, s_saved, do)


def _conv_fwd(proj3, conv4):
    def body(x_ref, w_ref, y_ref):
        x = x_ref[0]
        row = lax.broadcasted_iota(jnp.int32, (TP, 1), 0)
        y = w_ref[3] * x
        for s in (1, 2, 3):
            y = y + w_ref[3 - s] * jnp.where(row >= s, pltpu.roll(x, s, 0), 0.0)
        y_ref[0] = y

    return pl.pallas_call(
        body, name="conv_fwd", grid=(NB, 3),
        in_specs=[pl.BlockSpec((1, TP, HD), lambda b, j: (b, 0, COL_QKV // HD + j)),
                  pl.BlockSpec((4, 1, HD), lambda b, j: (0, 0, j))],
        out_specs=pl.BlockSpec((1, TP, HD), lambda b, j: (b, 0, j)),
        out_shape=jax.ShapeDtypeStruct((NB, TP, 3 * HD), f32),
        compiler_params=_cp(("arbitrary", "arbitrary")),
    )(proj3, conv4)


def _conv_bwd(proj3, conv4, dy):
    def body(x_ref, w_ref, dy_ref, dx_ref, dw_ref):
        @pl.when(pl.program_id(1) == 0)
        def _():
            dw_ref[...] = jnp.zeros_like(dw_ref)

        x = x_ref[0]
        g = dy_ref[0]
        row = lax.broadcasted_iota(jnp.int32, (TP, 1), 0)
        dx = w_ref[3] * g
        dw_ref[3] += jnp.broadcast_to(jnp.sum(x * g, axis=0, keepdims=True), (8, HD))
        for s in (1, 2, 3):
            dx = dx + w_ref[3 - s] * jnp.where(row < TP - s, pltpu.roll(g, TP - s, 0), 0.0)
            xs = jnp.where(row >= s, pltpu.roll(x, s, 0), 0.0)
            dw_ref[3 - s] += jnp.broadcast_to(jnp.sum(xs * g, axis=0, keepdims=True), (8, HD))
        dx_ref[0] = dx

    return pl.pallas_call(
        body, name="conv_bwd", grid=(3, NB),
        in_specs=[pl.BlockSpec((1, TP, HD), lambda j, b: (b, 0, COL_QKV // HD + j)),
                  pl.BlockSpec((4, 1, HD), lambda j, b: (0, 0, j)), pl.BlockSpec((1, TP, HD), lambda j, b: (b, 0, j))],
        out_specs=[pl.BlockSpec((1, TP, HD), lambda j, b: (b, 0, j)), pl.BlockSpec((4, 8, HD), lambda j, b: (0, 0, j))],
        out_shape=[jax.ShapeDtypeStruct((NB, TP, 3 * HD), f32), jax.ShapeDtypeStruct((4, 8, 3 * HD), f32)],
        compiler_params=_cp(("arbitrary", "arbitrary")),
    )(proj3, conv4, dy)


def _gated(o, z, w):
    outs = []
    for hh in range(H):
        sl = slice(hh * DK, (hh + 1) * DK)
        outs.append(rmsnorm(o[:, sl], w) * silu(z[:, sl]))
    return jnp.concatenate(outs, axis=-1)


def _out_loss(o_hg, o_gd, proj, hgw, gdw, wout, hflat, fw, target):
    tm = 384

    def body(ohg_ref, ogd_ref, zhg_ref, zgd_ref, hgw_ref, gdw_ref, wo_ref, h_ref, fw_ref, tg_ref,
             loss_ref, dohg_ref, dogd_ref, dzhg_ref, dzgd_ref, dh_ref, dwo_ref, dhgw_ref, dgdw_ref, dfw_ref):
        i = pl.program_id(0)

        @pl.when(i == 0)
        def _():
            for r in (loss_ref, dwo_ref, dhgw_ref, dgdw_ref, dfw_ref):
                r[...] = jnp.zeros_like(r)

        row = i * tm + lax.broadcasted_iota(jnp.int32, (tm, 1), 0)
        tok = jnp.where(row >= TP, row - TP, row)
        valid = (tok >= PAD + N_META).astype(f32)
        hval = h_ref[...]
        tgt = tg_ref[...]

        def mix(ohg, ogd, zhg, zgd, w1, w2):
            return jnp.concatenate([_gated(ohg, zhg, w1), _gated(ogd, zgd, w2)], axis=-1)

        y, vjp_mix = jax.vjp(mix, ohg_ref[...], ogd_ref[...], zhg_ref[...], zgd_ref[...], hgw_ref[...], gdw_ref[...])
        out = bdot(y, wo_ref[...])

        def head(out, fwv):
            err = (rmsnorm(hval + out, fwv) - tgt) * valid
            return 0.5 * jnp.sum(jnp.mean(err * err, axis=-1, keepdims=True))

        loss, vjp_head = jax.vjp(head, out, fw_ref[...])
        dout, dfw = vjp_head(jnp.ones((), f32))
        dh_ref[...] = dout
        dy = bdot_nt(dout, wo_ref[...])
        dwo_ref[...] += bdot_tn(y, dout)
        dohg, dogd, dzhg, dzgd, dw1, dw2 = vjp_mix(dy)
        dohg_ref[...] = dohg
        dogd_ref[...] = dogd
        dzhg_ref[...] = dzhg
        dzgd_ref[...] = dzgd
        loss_ref[...] += jnp.broadcast_to(loss, (8, DK))
        dhgw_ref[...] += jnp.broadcast_to(dw1, (8, DK))
        dgdw_ref[...] += jnp.broadcast_to(dw2, (8, DK))
        dfw_ref[...] += jnp.broadcast_to(dfw, (8, D))

    row = lambda w: pl.BlockSpec((tm, w), lambda i: (i, 0))
    whole = lambda r, w: pl.BlockSpec((r, w), lambda i: (0, 0))
    col = lambda c0: pl.BlockSpec((tm, HD), lambda i: (i, c0 // HD))
    return pl.pallas_call(
        body, name="out_loss", grid=(N // tm,),
        in_specs=[row(HD), row(HD), col(COL_ZHG), col(COL_ZGD),
                  whole(1, DK), whole(1, DK), whole(D, D), row(D), whole(1, D), row(D)],
        out_specs=[whole(8, DK), row(HD), row(HD), row(HD), row(HD), row(D), whole(D, D),
                   whole(8, DK), whole(8, DK), whole(8, D)],
        out_shape=[jax.ShapeDtypeStruct((8, DK), f32)] + [jax.ShapeDtypeStruct((N, HD), f32)] * 4
        + [jax.ShapeDtypeStruct((N, D), f32), jax.ShapeDtypeStruct((D, D), f32),
           jax.ShapeDtypeStruct((8, DK), f32), jax.ShapeDtypeStruct((8, DK), f32), jax.ShapeDtypeStruct((8, D), f32)],
        compiler_params=_cp(("arbitrary",)),
    )(o_hg, o_gd, proj, proj, hgw, gdw, wout, hflat, fw, target)


def _in_bwd(pieces, wbig, hflat, norm_w, dh_res):
    tm = 192
    np_ = len(pieces)
    offs = [c0 for _, c0 in pieces]
    widths = [d.shape[1] for d, _ in pieces]

    def body(*refs):
        d_refs = refs[:np_]
        w_ref, h_ref, nw_ref, dhr_ref, dh_ref, dnw_ref = refs[np_:]

        @pl.when(pl.program_id(0) == 0)
        def _():
            dnw_ref[...] = jnp.zeros_like(dnw_ref)

        du = jnp.zeros((tm, D), f32)
        for d_ref, off, wd in zip(d_refs, offs, widths):
            du = du + bdot_nt(d_ref[...], w_ref[:, off:off + wd])
        _, vjp = jax.vjp(rmsnorm, h_ref[...], nw_ref[...])
        dh, dnw = vjp(du)
        dh_ref[...] = dh + dhr_ref[...]
        dnw_ref[...] += jnp.broadcast_to(dnw, (8, D))

    row = lambda w: pl.BlockSpec((tm, w), lambda i: (i, 0))
    return pl.pallas_call(
        body, name="in_bwd", grid=(N // tm,),
        in_specs=[row(w) for w in widths]
        + [pl.BlockSpec((D, PC), lambda i: (0, 0)), row(D), pl.BlockSpec((1, D), lambda i: (0, 0)), row(D)],
        out_specs=[row(D), pl.BlockSpec((8, D), lambda i: (0, 0))],
        out_shape=[jax.ShapeDtypeStruct((N, D), f32), jax.ShapeDtypeStruct((8, D), f32)],
        compiler_params=_cp(("arbitrary",)),
    )(*[d for d, _ in pieces], wbig, hflat, norm_w, dh_res)


def _w_grad(ub, d, name):
    n = d.shape[1]
    tn = min(n, 512)
    tk = N // 4

    def body(u_ref, d_ref, o_ref):
        @pl.when(pl.program_id(1) == 0)
        def _():
            o_ref[...] = jnp.zeros_like(o_ref)

        o_ref[...] += bdot_tn(u_ref[...], d_ref[...])

    return pl.pallas_call(
        body, name=name, grid=(n // tn, N // tk),
        in_specs=[pl.BlockSpec((tk, D), lambda j, k: (k, 0)), pl.BlockSpec((tk, tn), lambda j, k: (k, j))],
        out_specs=pl.BlockSpec((D, tn), lambda j, k: (0, j)),
        out_shape=jax.ShapeDtypeStruct((D, n), f32),
        compiler_params=_cp(("arbitrary", "arbitrary")),
    )(ub, d)


def _adam_math(g, w, m, v):
    m2 = ADAM_B1 * m + (1.0 - ADAM_B1) * g
    v2 = ADAM_B2 * v + (1.0 - ADAM_B2) * (g * g)
    m_hat = m2 / (1.0 - ADAM_B1 ** ADAM_STEP)
    v_hat = v2 / (1.0 - ADAM_B2 ** ADAM_STEP)
    delta = -ADAM_LR * (m_hat / (jnp.sqrt(v_hat) + ADAM_EPS) + ADAM_WD * w)
    return delta, m2, v2


def _adamw(gs, w, m, v, name):
    R, Cc = w.shape
    tr = 256 if R % 256 == 0 else R
    ng = len(gs)

    def body(*refs):
        g = refs[0][...]
        for r in refs[1:ng]:
            g = g + r[...]
        w_ref, m_ref, v_ref, g_ref, d_ref, m2_ref, v2_ref = refs[ng:]
        delta, m2, v2 = _adam_math(g, w_ref[...], m_ref[...], v_ref[...])
        g_ref[...] = g
        d_ref[...] = delta
        m2_ref[...] = m2
        v2_ref[...] = v2

    spec = pl.BlockSpec((tr, Cc), lambda i: (i, 0))
    return pl.pallas_call(
        body, name=name, grid=(R // tr,),
        in_specs=[spec] * (ng + 3), out_specs=[spec] * 4,
        out_shape=[jax.ShapeDtypeStruct((R, Cc), f32)] * 4,
        compiler_params=_cp(("arbitrary",)),
    )(*gs, w, m, v)


def _sum_slots(r, name):
    S, R, Cc = r.shape
    tr = 256 if R % 256 == 0 else R

    def body(r_ref, o_ref):
        acc = r_ref[0]
        for s in range(1, S):
            acc = acc + r_ref[s]
        o_ref[...] = acc

    return pl.pallas_call(
        body, name=name, grid=(R // tr,),
        in_specs=[pl.BlockSpec((S, tr, Cc), lambda i: (0, i, 0))], out_specs=pl.BlockSpec((tr, Cc), lambda i: (i, 0)),
        out_shape=jax.ShapeDtypeStruct((R, Cc), f32),
        compiler_params=_cp(("arbitrary",)),
    )(r)


def _place():
    x, y, c = lax.axis_index("x"), lax.axis_index("y"), lax.axis_index("c")
    return x, y, c, [(1 - x, y), (x, 1 - y), (1 - x, 1 - y)]


def _gather_weights(halved, whole):
    nh, nw = len(halved), len(whole)
    na = nh + nw

    def body(*refs):
        srcs, dsts = refs[:na], refs[na:2 * na]
        send_sems, recv_sems, loc_sems = refs[2 * na:]
        x, y, c, chips = _place()
        me = 2 * x + y
        locs = [pltpu.make_async_copy(s, d.at[me], loc_sems.at[i]) for i, (s, d) in enumerate(zip(srcs, dsts))]
        for cp in locs:
            cp.start()

        def ici(j, i, slot):
            px, py = chips[j]
            src = srcs[i].at[c] if i < nh else srcs[i]
            dst = dsts[i].at[slot, c] if i < nh else dsts[i].at[slot]
            return pltpu.make_async_remote_copy(
                src_ref=src, dst_ref=dst, send_sem=send_sems.at[na * j + i], recv_sem=recv_sems.at[na * j + i],
                device_id=(px, py, c), device_id_type=MESH)

        def d2d(j, i, half):
            px, py = chips[j]
            blk = dsts[i].at[2 * px + py, half]
            return pltpu.make_async_remote_copy(
                src_ref=blk, dst_ref=blk, send_sem=send_sems.at[3 * na + nh * j + i],
                recv_sem=recv_sems.at[3 * na + nh * j + i], device_id=(x, y, 1 - c), device_id_type=MESH)

        sends = [ici(j, i, me) for j in range(3) for i in range(na)]
        for cp in sends:
            cp.start()
        for j, (px, py) in enumerate(chips):
            for i in range(na):
                ici(j, i, 2 * px + py).wait_recv()
                if i < nh:
                    fwd = d2d(j, i, c)
                    fwd.start()
                    sends.append(fwd)
        for j in range(3):
            for i in range(nh):
                d2d(j, i, 1 - c).wait_recv()
        for cp in sends:
            cp.wait_send()
        for cp in locs:
            cp.wait()

    nsem = 3 * na + 3 * nh
    return pl.pallas_call(
        body, name="gather_weights",
        in_specs=[ANY] * na, out_specs=[ANY] * na,
        out_shape=[jax.ShapeDtypeStruct((4,) + s.shape, s.dtype) for s in list(halved) + list(whole)],
        scratch_shapes=[pltpu.SemaphoreType.DMA((nsem,)), pltpu.SemaphoreType.DMA((nsem,)),
                        pltpu.SemaphoreType.DMA((na,))],
        compiler_params=pltpu.CompilerParams(has_side_effects=True),
    )(*halved, *whole)


def _swap_halves(gs):
    na = len(gs)

    def body(*refs):
        srcs, dsts = refs[:na], refs[na:2 * na]
        send_sems, recv_sems = refs[2 * na:]
        x, y, c, _ = _place()
        cps = [pltpu.make_async_remote_copy(
            src_ref=srcs[i].at[q, 1 - c], dst_ref=dsts[i].at[q], send_sem=send_sems.at[4 * i + q],
            recv_sem=recv_sems.at[4 * i + q], device_id=(x, y, 1 - c), device_id_type=MESH)
            for i in range(na) for q in range(4)]
        for cp in cps:
            cp.start()
        for cp in cps:
            cp.wait()

    return pl.pallas_call(
        body, name="swap_halves",
        in_specs=[ANY] * na, out_specs=[ANY] * na,
        out_shape=[jax.ShapeDtypeStruct((4,) + g.shape[2:], g.dtype) for g in gs],
        scratch_shapes=[pltpu.SemaphoreType.DMA((4 * na,)), pltpu.SemaphoreType.DMA((4 * na,))],
        compiler_params=pltpu.CompilerParams(has_side_effects=True),
    )(*gs)


def _add_halves(c_arr, g, s, name):
    _, _, R, Cc = g.shape
    tr = min(R, 256)

    def body(c_ref, g_ref, s_ref, b_ref, f_ref):
        p = g_ref[0, 0] + s_ref[0]
        f_ref[0] = p
        b_ref[0] = p.astype(bf16)

    blk = pl.BlockSpec((1, tr, Cc), lambda q, i, cr: (q, i, 0))
    return pl.pallas_call(
        body, name=name,
        grid_spec=pltpu.PrefetchScalarGridSpec(
            num_scalar_prefetch=1, grid=(4, R // tr),
            in_specs=[pl.BlockSpec((1, 1, tr, Cc), lambda q, i, cr: (q, cr[0], i, 0)), blk], out_specs=[blk, blk]),
        out_shape=[jax.ShapeDtypeStruct((4, R, Cc), bf16), jax.ShapeDtypeStruct((4, R, Cc), f32)],
        compiler_params=_cp(("arbitrary", "arbitrary")),
    )(c_arr, g, s)


def _scatter_blocks(pbs, pack):
    na = len(pbs)
    R = pack.shape[0]

    def body(*refs):
        srcs, pk = refs[:na], refs[na]
        dsts, rp = refs[na + 1:2 * na + 1], refs[2 * na + 1]
        send_sems, recv_sems, loc_sems = refs[2 * na + 2:]
        x, y, c, chips = _place()
        me = 2 * x + y
        me8 = 4 * x + 2 * y + c
        locs = [pltpu.make_async_copy(srcs[i].at[me], dsts[i].at[me], loc_sems.at[i]) for i in range(na)]
        locs.append(pltpu.make_async_copy(pk, rp.at[me8], loc_sems.at[na]))
        for cp in locs:
            cp.start()

        def big(j, i, src_slot, dst_slot):
            px, py = chips[j]
            return pltpu.make_async_remote_copy(
                src_ref=srcs[i].at[src_slot], dst_ref=dsts[i].at[dst_slot], send_sem=send_sems.at[na * j + i],
                recv_sem=recv_sems.at[na * j + i], device_id=(px, py, c), device_id_type=MESH)

        flips = [(fx, fy, fc) for fx in (0, 1) for fy in (0, 1) for fc in (0, 1)][1:]

        def small(k, slot):
            fx, fy, fc = flips[k]
            return pltpu.make_async_remote_copy(
                src_ref=pk, dst_ref=rp.at[slot], send_sem=send_sems.at[3 * na + k], recv_sem=recv_sems.at[3 * na + k],
                device_id=(x ^ fx, y ^ fy, c ^ fc), device_id_type=MESH)

        sends = [big(j, i, 2 * chips[j][0] + chips[j][1], me) for j in range(3) for i in range(na)]
        sends += [small(k, me8) for k in range(7)]
        for cp in sends:
            cp.start()
        for j, (px, py) in enumerate(chips):
            for i in range(na):
                big(j, i, me, 2 * px + py).wait_recv()
        for k, (fx, fy, fc) in enumerate(flips):
            small(k, 4 * (x ^ fx) + 2 * (y ^ fy) + (c ^ fc)).wait_recv()
        for cp in sends:
            cp.wait_send()
        for cp in locs:
            cp.wait()

    nsem = 3 * na + 7
    return pl.pallas_call(
        body, name="scatter_blocks",
        in_specs=[ANY] * (na + 1), out_specs=[ANY] * (na + 1),
        out_shape=[jax.ShapeDtypeStruct(p.shape, p.dtype) for p in pbs] + [jax.ShapeDtypeStruct((8, R, 128), f32)],
        scratch_shapes=[pltpu.SemaphoreType.DMA((nsem,)), pltpu.SemaphoreType.DMA((nsem,)),
                        pltpu.SemaphoreType.DMA((na + 1,))],
        compiler_params=pltpu.CompilerParams(has_side_effects=True),
    )(*pbs, pack)


def _sum_blocks(me_arr, pf, r, name):
    _, R, Cc = pf.shape
    tr = min(R, 256)

    def body(me_ref, pf_ref, r_ref, o_ref):
        me = me_ref[0]
        acc = None
        for q in range(4):
            own = (q == me).astype(f32)
            term = own * pf_ref[0] + (1.0 - own) * r_ref[q].astype(f32)
            acc = term if acc is None else acc + term
        o_ref[...] = acc

    return pl.pallas_call(
        body, name=name,
        grid_spec=pltpu.PrefetchScalarGridSpec(
            num_scalar_prefetch=1, grid=(R // tr,),
            in_specs=[pl.BlockSpec((1, tr, Cc), lambda i, mr: (mr[0], i, 0)),
                      pl.BlockSpec((4, tr, Cc), lambda i, mr: (0, i, 0))],
            out_specs=pl.BlockSpec((tr, Cc), lambda i, mr: (i, 0))),
        out_shape=jax.ShapeDtypeStruct((R, Cc), f32),
        compiler_params=_cp(("arbitrary",)),
    )(me_arr, pf, r)


def _join_halves(fs):
    na = len(fs)

    def body(*refs):
        srcs, dsts = refs[:na], refs[na:2 * na]
        send_sems, recv_sems, loc_sems = refs[2 * na:]
        x, y, c, _ = _place()
        locs = [pltpu.make_async_copy(srcs[i], dsts[i].at[c], loc_sems.at[i]) for i in range(na)]
        cps = [pltpu.make_async_remote_copy(
            src_ref=srcs[i], dst_ref=dsts[i].at[c], send_sem=send_sems.at[i], recv_sem=recv_sems.at[i],
            device_id=(x, y, 1 - c), device_id_type=MESH) for i in range(na)]
        for cp in locs + cps:
            cp.start()
        for i in range(na):
            pltpu.make_async_remote_copy(
                src_ref=srcs[i], dst_ref=dsts[i].at[1 - c], send_sem=send_sems.at[i], recv_sem=recv_sems.at[i],
                device_id=(x, y, 1 - c), device_id_type=MESH).wait_recv()
        for cp in cps:
            cp.wait_send()
        for cp in locs:
            cp.wait()

    return pl.pallas_call(
        body, name="join_halves",
        in_specs=[ANY] * na, out_specs=[ANY] * na,
        out_shape=[jax.ShapeDtypeStruct((2,) + f.shape, f.dtype) for f in fs],
        scratch_shapes=[pltpu.SemaphoreType.DMA((na,)), pltpu.SemaphoreType.DMA((na,)), pltpu.SemaphoreType.DMA((na,))],
        compiler_params=pltpu.CompilerParams(has_side_effects=True),
    )(*fs)


def _rows8(a):
    flat = a.reshape(-1)
    n = flat.shape[0]
    rows = -(-n // 1024) * 8
    return jnp.pad(flat, (0, rows * 128 - n)).reshape(rows, 128)


def kernel(x, meta_tokens, norm_w, w_in, conv_w, hg_lb_logits, hg_norm_w, gdn_A_log, gdn_dt_bias, gdn_norm_w, w_out, final_norm_w, loss_target, m_meta_tokens, m_norm_w, m_w_in, m_conv_w, m_hg_lb_logits, m_hg_norm_w, m_gdn_A_log, m_gdn_dt_bias, m_gdn_norm_w, m_w_out, m_final_norm_w, v_meta_tokens, v_norm_w, v_w_in, v_conv_w, v_hg_lb_logits, v_hg_norm_w, v_gdn_A_log, v_gdn_dt_bias, v_gdn_norm_w, v_w_out, v_final_norm_w):
    me = 2 * lax.axis_index("x") + lax.axis_index("y")

    g_win, g_wout, g_conv, g_meta = _gather_weights(
        [w_in[0].astype(bf16).reshape(2, D // 2, SHARD_COLS), w_out[0].astype(bf16).reshape(2, D // 8, D)],
        [conv_w[0], meta_tokens])
    w_full = jnp.transpose(g_win.reshape(4, D, SHARD_COLS), (1, 0, 2)).reshape(D, IN_COLS)
    wbig = jnp.pad(w_full, ((0, 0), (0, PC - IN_COLS)))
    wout_full = g_wout.reshape(D, D)
    conv4 = jnp.transpose(g_conv, (1, 0, 2)).reshape(4, 1, 3 * HD)
    meta_full = jnp.transpose(g_meta, (1, 0, 2)).reshape(N_META, D)

    (loss8, grad_x, d_meta, d_nw, gw_full, d_conv, d_lb, d_hgw, d_alog, d_dtb, d_gdw, g_wout_part, d_fw) = _local_step(
        x, loss_target, wbig, wout_full, conv4, meta_full, norm_w, hg_lb_logits, hg_norm_w, gdn_A_log, gdn_dt_bias,
        gdn_norm_w, final_norm_w)

    g_in_blocks = jnp.transpose(gw_full.reshape(D, 4, SHARD_COLS), (1, 0, 2))
    g_out_blocks = g_wout_part.reshape(4, D // 4, D)
    pack = jnp.concatenate([
        loss8, d_nw[0].reshape(8, 128), d_lb.reshape(8, 128), d_hgw, _rows8(d_alog[0, :H]), _rows8(d_dtb[0, :H]),
        d_gdw, d_fw[0].reshape(8, 128), d_meta.reshape(128, 128), d_conv.reshape(48, 128)], axis=0)
    return _reduce_and_update(
        me, grad_x, g_in_blocks, g_out_blocks, pack, meta_tokens, norm_w, w_in, conv_w, hg_lb_logits, hg_norm_w, gdn_A_log,
        gdn_dt_bias, gdn_norm_w, w_out, final_norm_w, m_meta_tokens, m_norm_w, m_w_in, m_conv_w, m_hg_lb_logits,
        m_hg_norm_w, m_gdn_A_log, m_gdn_dt_bias, m_gdn_norm_w, m_w_out, m_final_norm_w, v_meta_tokens, v_norm_w, v_w_in,
        v_conv_w, v_hg_lb_logits, v_hg_norm_w, v_gdn_A_log, v_gdn_dt_bias, v_gdn_norm_w, v_w_out, v_final_norm_w)


def _local_step(x, loss_target, wbig, wout_full, conv4, meta_full, norm_w, hg_lb_logits, hg_norm_w, gdn_A_log, gdn_dt_bias,
                gdn_norm_w, final_norm_w):
    h3 = jnp.concatenate([jnp.zeros((NB, PAD, D), f32), jnp.broadcast_to(meta_full[None], (NB, N_META, D)), x], axis=1)
    hflat = h3.reshape(N, D)
    target = jnp.pad(loss_target, ((0, 0), (PAD + N_META, 0), (0, 0))).reshape(N, D)
    l0, l1 = hg_lb_logits[0:1], hg_lb_logits[1:2]
    alog = jnp.pad(gdn_A_log, ((0, 0), (0, DK - H)))
    dtb = jnp.pad(gdn_dt_bias, ((0, 0), (0, DK - H)))
    fw = final_norm_w.reshape(1, D)

    proj, ub = _in_proj(hflat, norm_w, wbig)
    proj3 = proj.reshape(NB, TP, PC)
    o_hg, s_hg = _hg_fwd(proj3, l0, l1)
    cv = _conv_fwd(proj3, conv4)
    o_gd, s_gd = _gd_fwd(cv, proj3, alog, dtb)
    (loss8, d_ohg, d_ogd, d_zhg, d_zgd, dh_res, g_wout_part, d_hgw, d_gdw, d_fw) = _out_loss(
        o_hg.reshape(N, HD), o_gd.reshape(N, HD), proj, hg_norm_w, gdn_norm_w, wout_full, hflat, fw, target)
    d_hg, d_l0, d_l1 = _hg_bwd(proj3, l0, l1, s_hg, d_ohg.reshape(NB, TP, HD))
    d_cv, d_ab, d_alog, d_dtb = _gd_bwd(cv, proj3, alog, dtb, s_gd, d_ogd.reshape(NB, TP, HD))
    d_qkv, d_conv4 = _conv_bwd(proj3, conv4, d_cv)
    d_hg2, d_qkv2, d_ab2 = d_hg.reshape(N, 3 * HD), d_qkv.reshape(N, 3 * HD), d_ab.reshape(N, DK)
    pieces = [(d_hg2, COL_HG), (d_zhg, COL_ZHG), (d_qkv2, COL_QKV), (d_zgd, COL_ZGD), (d_ab2, COL_AB)]
    dh, d_nw = _in_bwd(pieces, wbig, hflat, norm_w, dh_res)
    gws = [_w_grad(ub, d, "w_grad_%d" % c0) for d, c0 in pieces]
    gw_full = jnp.concatenate(gws[:4] + [gws[4][:, 0:IN_COLS - COL_AB]], axis=1)

    dh3 = dh.reshape(NB, TP, D)
    grad_x = dh3[:, PAD + N_META:, :]
    d_meta = jnp.sum(dh3[:, PAD:PAD + N_META, :], axis=0)
    d_conv = d_conv4[:, 0, :]
    d_lb = jnp.concatenate([d_l0[0:1], d_l1[0:1]], axis=0)
    return loss8, grad_x, d_meta, d_nw, gw_full, d_conv, d_lb, d_hgw, d_alog, d_dtb, d_gdw, g_wout_part, d_fw


def _reduce_and_update(me, grad_x, g_in_blocks, g_out_blocks, pack, meta_tokens, norm_w, w_in, conv_w, hg_lb_logits, hg_norm_w,
                       gdn_A_log, gdn_dt_bias, gdn_norm_w, w_out, final_norm_w, m_meta_tokens, m_norm_w, m_w_in, m_conv_w,
                       m_hg_lb_logits, m_hg_norm_w, m_gdn_A_log, m_gdn_dt_bias, m_gdn_norm_w, m_w_out, m_final_norm_w,
                       v_meta_tokens, v_norm_w, v_w_in, v_conv_w, v_hg_lb_logits, v_hg_norm_w, v_gdn_A_log, v_gdn_dt_bias,
                       v_gdn_norm_w, v_w_out, v_final_norm_w):
    c_arr = lax.axis_index("c").reshape(1).astype(jnp.int32)
    me_arr = me.reshape(1).astype(jnp.int32)
    g_in4 = g_in_blocks.reshape(4, 2, D // 2, SHARD_COLS)
    g_out4 = g_out_blocks.reshape(4, 2, D // 8, D)
    s_in, s_out = _swap_halves([g_in4, g_out4])
    pb_in, pf_in = _add_halves(c_arr, g_in4, s_in, "add_w_in")
    pb_out, pf_out = _add_halves(c_arr, g_out4, s_out, "add_w_out")
    r_in, r_out, r_pack = _scatter_blocks([pb_in, pb_out], pack)
    f_in = _sum_blocks(me_arr, pf_in, r_in, "sum_w_in")
    f_out = _sum_blocks(me_arr, pf_out, r_out, "sum_w_out")
    j_in, j_out = _join_halves([f_in, f_out])
    small = _sum_slots(r_pack, "sum_small")

    gi, di, mi, vi = _adamw([j_in.reshape(D, SHARD_COLS)], w_in[0], m_w_in[0], v_w_in[0], "adamw_w_in")
    go, do_, mo, vo = _adamw([j_out.reshape(D // 4, D)], w_out[0], m_w_out[0], v_w_out[0], "adamw_w_out")

    g_meta_full = small[64:192].reshape(N_META, D)
    g_meta_loc = lax.dynamic_slice(g_meta_full, (0, me * 256), (N_META, 256))
    gm, dm, mm_, vm = _adamw([g_meta_loc], meta_tokens, m_meta_tokens, v_meta_tokens, "adamw_meta")
    g_conv_full = small[192:240].reshape(4, 1536)
    g_conv_loc = lax.dynamic_slice(g_conv_full, (0, me * 384), (4, 384))
    gc, dc, mc, vc = _adamw([g_conv_loc], conv_w[0], m_conv_w[0], v_conv_w[0], "adamw_conv")

    reps = [(norm_w, m_norm_w, v_norm_w), (hg_lb_logits, m_hg_lb_logits, v_hg_lb_logits),
            (hg_norm_w, m_hg_norm_w, v_hg_norm_w), (gdn_A_log, m_gdn_A_log, v_gdn_A_log),
            (gdn_dt_bias, m_gdn_dt_bias, v_gdn_dt_bias), (gdn_norm_w, m_gdn_norm_w, v_gdn_norm_w),
            (final_norm_w, m_final_norm_w, v_final_norm_w)]
    wp = jnp.concatenate([_rows8(t[0]) for t in reps], axis=0)
    mp = jnp.concatenate([_rows8(t[1]) for t in reps], axis=0)
    vp = jnp.concatenate([_rows8(t[2]) for t in reps], axis=0)
    gr, dr, mr, vr = _adamw([small[8:64]], wp, mp, vp, "adamw_small")

    def unpack(p):
        outs = []
        for i, t in enumerate(reps):
            n = t[0].size
            outs.append(p[8 * i:8 * i + 8].reshape(-1)[:n].reshape(t[0].shape))
        return outs

    def leaves(meta_v, conv_v, in_v, out_v, rep_p):
        nw, lb, hgw, al, db, gdw, fwv = unpack(rep_p)
        return [meta_v, nw, in_v[None], conv_v[None], lb, hgw, al, db, gdw, out_v[None], fwv]

    loss = small[0, 0]
    return (loss, grad_x, *leaves(gm, gc, gi, go, gr), *leaves(dm, dc, di, do_, dr),
            *leaves(mm_, mc, mi, mo, mr), *leaves(vm, vc, vi, vo, vr))
```

```python
import functools

import jax
import jax.numpy as jnp
from jax import lax
from jax.experimental import pallas as pl
from jax.experimental.pallas import tpu as pltpu

f32 = jnp.float32
bf16 = jnp.bfloat16
MESH = pl.DeviceIdType.MESH
ANY = pl.BlockSpec(memory_space=pl.ANY)

D = 1024
NB = 2
N_META = 16
SEQ = 2048
PAD = 48
TP = PAD + N_META + SEQ
C = 64
NCH = TP // C
N = NB * TP
H = 4
DK = 128
HD = H * DK
PC = 4224
IN_COLS = 4104
SHARD_COLS = IN_COLS // 4
COL_HG, COL_ZHG, COL_QKV, COL_ZGD, COL_AB = 0, 3 * HD, 4 * HD, 7 * HD, 8 * HD
EPS = 1e-6
ADAM_LR, ADAM_B1, ADAM_B2, ADAM_EPS, ADAM_WD, ADAM_STEP = 0.001, 0.9, 0.999, 1e-08, 0.01, 10
VMEM_LIMIT = 56 * 1024 * 1024

P_HG = dict(lvl=1, av=1, qs=1, su=1)
P_GD = dict(kk=1, inv=3, sol=3, ws=1, qk=1, o=1, su=1)


def _cp(sem=None, **kw):
    return pltpu.CompilerParams(dimension_semantics=sem, vmem_limit_bytes=VMEM_LIMIT, **kw)


_DIMS = {"nn": (((1,), (0,)), ((), ())), "nt": (((1,), (1,)), ((), ())), "tn": (((0,), (0,)), ((), ()))}


def _split(x):
    hi = x.astype(bf16)
    return hi, (x - hi.astype(f32)).astype(bf16)


def _dg(a, b, kind, passes):
    d = lambda x, y: lax.dot_general(x, y, _DIMS[kind], preferred_element_type=f32)
    if passes == 1:
        return d(a.astype(bf16), b.astype(bf16))
    ah, al = _split(a)
    bh, bl = _split(b)
    return d(ah, bh) + d(ah, bl) + d(al, bh)


@functools.partial(jax.custom_vjp, nondiff_argnums=(2, 3))
def mmx(a, b, kind, passes):
    return _dg(a, b, kind, passes)


def _mmx_fwd(a, b, kind, passes):
    return _dg(a, b, kind, passes), (a, b)


def _mmx_bwd(kind, passes, res, g):
    a, b = res
    if kind == "nn":
        return _dg(g, b, "nt", passes), _dg(a, g, "tn", passes)
    if kind == "nt":
        return _dg(g, b, "nn", passes), _dg(g, a, "tn", passes)
    return _dg(b, g, "nt", passes), _dg(a, g, "nn", passes)


mmx.defvjp(_mmx_fwd, _mmx_bwd)


def _mask_dg(mask, x, kind):
    n = x.shape[1]
    xh, xl = _split(x)
    r = lax.dot_general(mask, jnp.concatenate([xh, xl], axis=1), _DIMS[kind], preferred_element_type=f32)
    return r[:, :n] + r[:, n:]


@jax.custom_vjp
def mask_mm(mask, x):
    return _mask_dg(mask, x, "nn")


def _mask_fwd(mask, x):
    return _mask_dg(mask, x, "nn"), mask


def _mask_bwd(mask, g):
    return None, _mask_dg(mask, g, "tn")


mask_mm.defvjp(_mask_fwd, _mask_bwd)


def bdot(a, b):
    return jnp.dot(a.astype(bf16), b.astype(bf16), preferred_element_type=f32)


def bdot_nt(a, b):
    return lax.dot_general(a.astype(bf16), b.astype(bf16), _DIMS["nt"], preferred_element_type=f32)


def bdot_tn(a, b):
    return lax.dot_general(a.astype(bf16), b.astype(bf16), _DIMS["tn"], preferred_element_type=f32)


def _iota2(n, m):
    return lax.broadcasted_iota(jnp.int32, (n, m), 0), lax.broadcasted_iota(jnp.int32, (n, m), 1)


def silu(x):
    return x * jax.nn.sigmoid(x)


def softplus(x):
    return jnp.maximum(x, 0.0) + jnp.log(1.0 + jnp.exp(-jnp.abs(x)))


def rmsnorm(x, w):
    return x * lax.rsqrt(jnp.mean(x * x, axis=-1, keepdims=True) + EPS) * w


def hg_masks():
    t, r = _iota2(C, C)
    mats = [r <= t, r > t]
    lvl = []
    for l in range(1, 7):
        sz = 1 << l
        half = sz >> 1
        seg_t = t >> l
        upper_t = (t & (sz - 1)) >= half
        mid_t = seg_t * sz + half - 1
        mats.append(upper_t & (r > mid_t) & (r <= t))
        mats.append((~upper_t) & (r > t) & (r <= mid_t))
        lvl.append(((seg_t == (r >> l)) & upper_t & ((r & (sz - 1)) < half)).astype(f32))
    stk = jnp.concatenate([m.astype(bf16) for m in mats], axis=0)
    return stk, lvl, (t == r).astype(f32)


def hg_chunk(St, p, l0, l1):
    pq, pf, v = p[:, 0:HD], p[:, HD:2 * HD], p[:, 2 * HD:3 * HD]
    m = jnp.maximum(l0, l1)
    e0 = jnp.exp(l0 - m)
    e1 = jnp.exp(l1 - m)
    lb = e0 / (e0 + e1)
    q = silu(pq)
    f = lb + (1.0 - lb) * jax.nn.sigmoid(pf)
    k = 1.0 - f
    logf = jnp.log(f)
    stk, lvl, eye = hg_masks()
    Dm = mask_mm(stk, logf)
    qs = [q] + [q * jnp.exp(Dm[(2 + 2 * i) * C:(3 + 2 * i) * C]) for i in range(6)]
    ks = [k] + [k * jnp.exp(Dm[(3 + 2 * i) * C:(4 + 2 * i) * C]) for i in range(6)]
    msk = [eye] + lvl
    qG = q * jnp.exp(Dm[0:C])
    kR = k * jnp.exp(Dm[C:2 * C])
    eGl = jnp.exp(jnp.sum(logf, axis=0, keepdims=True))
    heads = range(H)
    sls = [slice(h * DK, (h + 1) * DK) for h in heads]
    parts = [[msk[i] * mmx(qs[i][:, sl], ks[i][:, sl], "nt", P_HG["lvl"]) for sl in sls] for i in range(7)]
    A = [sum(parts[i][h] for i in range(1, 7)) + parts[0][h] for h in heads]
    qS = [mmx(qG[:, sls[h]], St[h], "nt", P_HG["qs"]) for h in heads]
    Sn = [St[h] * eGl[:, sls[h]] + mmx(v[:, sls[h]], kR[:, sls[h]], "tn", P_HG["su"]) for h in heads]
    outs = [mmx(A[h], v[:, sls[h]], "nn", P_HG["av"]) + qS[h] for h in heads]
    return tuple(Sn), jnp.concatenate(outs, axis=1)


def gd_chunk(S, c, ab, alog, dtb):
    qa, ka, va = silu(c[:, 0:HD]), silu(c[:, HD:2 * HD]), silu(c[:, 2 * HD:3 * HD])
    g4 = -jnp.exp(alog) * softplus(ab + dtb)
    b4 = jax.nn.sigmoid(ab)
    t, r = _iota2(C, C)
    gam4 = mask_mm((r <= t).astype(bf16), g4)
    grev4 = mask_mm((r > t).astype(bf16), g4)
    gam4T = gam4.T
    glast4 = jnp.sum(g4, axis=0, keepdims=True)
    lane = lax.broadcasted_iota(jnp.int32, (1, DK), 1)
    subl = lax.broadcasted_iota(jnp.int32, (8, 1), 0)
    eye = (t == r).astype(f32)
    strict = (r < t).astype(f32)
    bd = ((t >> 4) == (r >> 4)).astype(f32)
    heads = range(H)
    inv = lambda a, b: [mmx(a[h], b[h], "nn", P_GD["inv"]) for h in heads]
    sls = [slice(h * DK, (h + 1) * DK) for h in heads]
    v = [va[:, sl] for sl in sls]
    q = [qa[:, sl] for sl in sls]
    k = [ka[:, sl] for sl in sls]
    q = [x * lax.rsqrt(jnp.sum(x * x, -1, keepdims=True) + EPS) * (DK ** -0.5) for x in q]
    k = [x * lax.rsqrt(jnp.sum(x * x, -1, keepdims=True) + EPS) for x in k]
    oh = [(lane == h).astype(f32) for h in heads]
    gam_c = [jnp.sum(gam4 * oh[h], -1, keepdims=True) for h in heads]
    grev_c = [jnp.sum(grev4 * oh[h], -1, keepdims=True) for h in heads]
    beta = [jnp.sum(b4 * (lane == h + H).astype(f32), -1, keepdims=True) for h in heads]
    glast = [jnp.sum(glast4 * oh[h], -1, keepdims=True) for h in heads]
    gam_r = [jnp.sum(gam4T[0:8, :] * (subl == h).astype(f32), axis=0, keepdims=True) for h in heads]
    dec = [jnp.exp(jnp.where(r <= t, gam_c[h] - gam_r[h], -1e30)) for h in heads]
    egam = [jnp.exp(gam_c[h]) for h in heads]
    kk = [mmx(k[h], k[h], "nt", P_GD["kk"]) for h in heads]
    qk = [mmx(q[h], k[h], "nt", P_GD["qk"]) * dec[h] for h in heads]
    A = [beta[h] * kk[h] * dec[h] * strict for h in heads]
    Dg = [A[h] * bd for h in heads]
    L = [A[h] - Dg[h] for h in heads]
    ImD = [eye - Dg[h] for h in heads]
    D2 = inv(Dg, Dg)
    P1 = inv(ImD, [eye + x for x in D2])
    D4 = inv(D2, D2)
    P2 = inv(P1, [eye + x for x in D4])
    D8 = inv(D4, D4)
    M = inv(P2, [eye + x for x in D8])
    Nn = inv(M, L)
    N2 = inv(Nn, Nn)
    T1 = inv([eye - x for x in Nn], [eye + x for x in N2])
    Tinv = inv(T1, M)
    rhs = [jnp.concatenate([beta[h] * v[h], (beta[h] * egam[h]) * k[h]], axis=1) for h in heads]
    sol = [mmx(Tinv[h], rhs[h], "nn", P_GD["sol"]) for h in heads]
    qS = [mmx(q[h] * egam[h], S[h], "nn", P_GD["o"]) for h in heads]
    wS = [mmx(sol[h][:, DK:2 * DK], S[h], "nn", P_GD["ws"]) for h in heads]
    u = [sol[h][:, 0:DK] - wS[h] for h in heads]
    outs = [qS[h] + mmx(qk[h], u[h], "nn", P_GD["o"]) for h in heads]
    Sn = [jnp.exp(glast[h]) * S[h] + mmx(k[h] * jnp.exp(grev_c[h]), u[h], "tn", P_GD["su"]) for h in heads]
    return tuple(Sn), jnp.concatenate(outs, axis=1)


def _in_proj(hflat, norm_w, wbig):
    tm = 384

    def body(h_ref, nw_ref, w_ref, p_ref, u_ref):
        u = rmsnorm(h_ref[...], nw_ref[...]).astype(bf16)
        u_ref[...] = u
        p_ref[...] = jnp.dot(u, w_ref[...], preferred_element_type=f32)

    return pl.pallas_call(
        body, name="in_proj", grid=(N // tm,),
        in_specs=[pl.BlockSpec((tm, D), lambda i: (i, 0)), pl.BlockSpec((1, D), lambda i: (0, 0)),
                  pl.BlockSpec((D, PC), lambda i: (0, 0))],
        out_specs=[pl.BlockSpec((tm, PC), lambda i: (i, 0)), pl.BlockSpec((tm, D), lambda i: (i, 0))],
        out_shape=[jax.ShapeDtypeStruct((N, PC), f32), jax.ShapeDtypeStruct((N, D), bf16)],
        compiler_params=_cp(("arbitrary",)),
    )(hflat, norm_w, wbig)


_STATE_SPEC = lambda ix: pl.BlockSpec((1, 1, H, DK, DK), lambda b, c: (b, ix(c), 0, 0, 0))
_REV = lambda c: NCH - 1 - c
_FWD = lambda c: c


def _scan_fwd(chunk_fn, name, x3, xcol, extra, extra_specs):
    ne = len(extra)

    def body(*refs):
        x_ref = refs[0]
        e_refs = refs[1:1 + ne]
        o_ref, s_ref, st = refs[1 + ne:]

        @pl.when(pl.program_id(1) == 0)
        def _():
            st[...] = jnp.zeros_like(st)

        S = tuple(st[h] for h in range(H))
        s_ref[0, 0] = st[...]
        Sn, o = chunk_fn(S, x_ref[0], *[e[0] if len(e.shape) == 3 else e[...] for e in e_refs])
        for h in range(H):
            st[h] = Sn[h]
        o_ref[0] = o

    return pl.pallas_call(
        body, name=name, grid=(NB, NCH),
        in_specs=[pl.BlockSpec((1, C, 3 * HD), lambda b, c: (b, c, xcol))] + extra_specs(_FWD),
        out_specs=[pl.BlockSpec((1, C, HD), lambda b, c: (b, c, 0)), _STATE_SPEC(_FWD)],
        out_shape=[jax.ShapeDtypeStruct((NB, TP, HD), f32), jax.ShapeDtypeStruct((NB, NCH, H, DK, DK), f32)],
        scratch_shapes=[pltpu.VMEM((H, DK, DK), f32)],
        compiler_params=_cp(("arbitrary", "arbitrary")),
    )(x3, *extra)


def _hg_extra_specs(ix):
    return [pl.BlockSpec((1, HD), lambda b, c: (0, 0)), pl.BlockSpec((1, HD), lambda b, c: (0, 0))]


def _gd_extra_specs(ix):
    return [pl.BlockSpec((1, C, DK), lambda b, c: (b, ix(c), COL_AB // DK)),
            pl.BlockSpec((1, DK), lambda b, c: (0, 0)), pl.BlockSpec((1, DK), lambda b, c: (0, 0))]


def _hg_fwd(proj3, l0, l1):
    return _scan_fwd(hg_chunk, "hg_fwd", proj3, 0, [l0, l1], _hg_extra_specs)


def _gd_fwd(cv, proj3, alog, dtb):
    return _scan_fwd(gd_chunk, "gd_fwd", cv, 0, [proj3, alog, dtb], _gd_extra_specs)


def _hg_bwd(proj3, l0, l1, s_saved, do):
    def body(p_ref, l0_ref, l1_ref, s_ref, do_ref, dp_ref, dl0_ref, dl1_ref, dst):
        @pl.when(pl.program_id(1) == 0)
        def _():
            dst[...] = jnp.zeros_like(dst)

        @pl.when((pl.program_id(0) == 0) & (pl.program_id(1) == 0))
        def _():
            dl0_ref[...] = jnp.zeros_like(dl0_ref)
            dl1_ref[...] = jnp.zeros_like(dl1_ref)

        S = tuple(s_ref[0, 0, h] for h in range(H))
        _, vjp = jax.vjp(hg_chunk, S, p_ref[0], l0_ref[...], l1_ref[...])
        dS, dp, dl0, dl1 = vjp((tuple(dst[h] for h in range(H)), do_ref[0]))
        for h in range(H):
            dst[h] = dS[h]
        dp_ref[0] = dp
        dl0_ref[...] += jnp.broadcast_to(dl0, (8, HD))
        dl1_ref[...] += jnp.broadcast_to(dl1, (8, HD))

    acc = pl.BlockSpec((8, HD), lambda b, c: (0, 0))
    return pl.pallas_call(
        body, name="hg_bwd", grid=(NB, NCH),
        in_specs=[pl.BlockSpec((1, C, 3 * HD), lambda b, c: (b, _REV(c), 0))] + _hg_extra_specs(_REV)
        + [_STATE_SPEC(_REV), pl.BlockSpec((1, C, HD), lambda b, c: (b, _REV(c), 0))],
        out_specs=[pl.BlockSpec((1, C, 3 * HD), lambda b, c: (b, _REV(c), 0)), acc, acc],
        out_shape=[jax.ShapeDtypeStruct((NB, TP, 3 * HD), f32), jax.ShapeDtypeStruct((8, HD), f32),
                   jax.ShapeDtypeStruct((8, HD), f32)],
        scratch_shapes=[pltpu.VMEM((H, DK, DK), f32)],
        compiler_params=_cp(("arbitrary", "arbitrary")),
    )(proj3, l0, l1, s_saved, do)


def _gd_bwd(cv, proj3, alog, dtb, s_saved, do):
    def body(c_ref, ab_ref, al_ref, db_ref, s_ref, do_ref, dc_ref, dab_ref, dal_ref, ddb_ref, dst):
        @pl.when(pl.program_id(1) == 0)
        def _():
            dst[...] = jnp.zeros_like(dst)

        @pl.when((pl.program_id(0) == 0) & (pl.program_id(1) == 0))
        def _():
            dal_ref[...] = jnp.zeros_like(dal_ref)
            ddb_ref[...] = jnp.zeros_like(ddb_ref)

        S = tuple(s_ref[0, 0, h] for h in range(H))
        _, vjp = jax.vjp(gd_chunk, S, c_ref[0], ab_ref[0], al_ref[...], db_ref[...])
        dS, dc, dab, dal, ddb = vjp((tuple(dst[h] for h in range(H)), do_ref[0]))
        for h in range(H):
            dst[h] = dS[h]
        dc_ref[0] = dc
        dab_ref[0] = dab
        dal_ref[...] += jnp.broadcast_to(dal, (8, DK))
        ddb_ref[...] += jnp.broadcast_to(ddb, (8, DK))

    acc = pl.BlockSpec((8, DK), lambda b, c: (0, 0))
    return pl.pallas_call(
        body, name="gd_bwd", grid=(NB, NCH),
        in_specs=[pl.BlockSpec((1, C, 3 * HD), lambda b, c: (b, _REV(c), 0))] + _gd_extra_specs(_REV)
        + [_STATE_SPEC(_REV), pl.BlockSpec((1, C, HD), lambda b, c: (b, _REV(c), 0))],
        out_specs=[pl.BlockSpec((1, C, 3 * HD), lambda b, c: (b, _REV(c), 0)),
                   pl.BlockSpec((1, C, DK), lambda b, c: (b, _REV(c), 0)), acc, acc],
        out_shape=[jax.ShapeDtypeStruct((NB, TP, 3 * HD), f32), jax.ShapeDtypeStruct((NB, TP, DK), f32),
                   jax.ShapeDtypeStruct((8, DK), f32), jax.ShapeDtypeStruct((8, DK), f32)],
        scratch_shapes=[pltpu.VMEM((H, DK, DK), f32)],
        compiler_params=_cp(("arbitrary", "arbitrary")),
    )(cv, proj3, alog, dtb, s_saved, do)


def _conv_fwd(proj3, conv4):
    def body(x_ref, w_ref, y_ref):
        x = x_ref[0]
        row = lax.broadcasted_iota(jnp.int32, (TP, 1), 0)
        y = w_ref[3] * x
        for s in (1, 2, 3):
            y = y + w_ref[3 - s] * jnp.where(row >= s, pltpu.roll(x, s, 0), 0.0)
        y_ref[0] = y

    return pl.pallas_call(
        body, name="conv_fwd", grid=(NB, 3),
        in_specs=[pl.BlockSpec((1, TP, HD), lambda b, j: (b, 0, COL_QKV // HD + j)),
                  pl.BlockSpec((4, 1, HD), lambda b, j: (0, 0, j))],
        out_specs=pl.BlockSpec((1, TP, HD), lambda b, j: (b, 0, j)),
        out_shape=jax.ShapeDtypeStruct((NB, TP, 3 * HD), f32),
        compiler_params=_cp(("arbitrary", "arbitrary")),
    )(proj3, conv4)


def _conv_bwd(proj3, conv4, dy):
    def body(x_ref, w_ref, dy_ref, dx_ref, dw_ref):
        @pl.when(pl.program_id(1) == 0)
        def _():
            dw_ref[...] = jnp.zeros_like(dw_ref)

        x = x_ref[0]
        g = dy_ref[0]
        row = lax.broadcasted_iota(jnp.int32, (TP, 1), 0)
        dx = w_ref[3] * g
        dw_ref[3] += jnp.broadcast_to(jnp.sum(x * g, axis=0, keepdims=True), (8, HD))
        for s in (1, 2, 3):
            dx = dx + w_ref[3 - s] * jnp.where(row < TP - s, pltpu.roll(g, TP - s, 0), 0.0)
            xs = jnp.where(row >= s, pltpu.roll(x, s, 0), 0.0)
            dw_ref[3 - s] += jnp.broadcast_to(jnp.sum(xs * g, axis=0, keepdims=True), (8, HD))
        dx_ref[0] = dx

    return pl.pallas_call(
        body, name="conv_bwd", grid=(3, NB),
        in_specs=[pl.BlockSpec((1, TP, HD), lambda j, b: (b, 0, COL_QKV // HD + j)),
                  pl.BlockSpec((4, 1, HD), lambda j, b: (0, 0, j)), pl.BlockSpec((1, TP, HD), lambda j, b: (b, 0, j))],
        out_specs=[pl.BlockSpec((1, TP, HD), lambda j, b: (b, 0, j)), pl.BlockSpec((4, 8, HD), lambda j, b: (0, 0, j))],
        out_shape=[jax.ShapeDtypeStruct((NB, TP, 3 * HD), f32), jax.ShapeDtypeStruct((4, 8, 3 * HD), f32)],
        compiler_params=_cp(("arbitrary", "arbitrary")),
    )(proj3, conv4, dy)


def _gated(o, z, w):
    outs = []
    for hh in range(H):
        sl = slice(hh * DK, (hh + 1) * DK)
        outs.append(rmsnorm(o[:, sl], w) * silu(z[:, sl]))
    return jnp.concatenate(outs, axis=-1)


def _out_loss(o_hg, o_gd, proj, hgw, gdw, wout, hflat, fw, target):
    tm = 384

    def body(ohg_ref, ogd_ref, zhg_ref, zgd_ref, hgw_ref, gdw_ref, wo_ref, h_ref, fw_ref, tg_ref,
             loss_ref, dohg_ref, dogd_ref, dzhg_ref, dzgd_ref, dh_ref, dwo_ref, dhgw_ref, dgdw_ref, dfw_ref):
        i = pl.program_id(0)

        @pl.when(i == 0)
        def _():
            for r in (loss_ref, dwo_ref, dhgw_ref, dgdw_ref, dfw_ref):
                r[...] = jnp.zeros_like(r)

        row = i * tm + lax.broadcasted_iota(jnp.int32, (tm, 1), 0)
        tok = jnp.where(row >= TP, row - TP, row)
        valid = (tok >= PAD + N_META).astype(f32)
        hval = h_ref[...]
        tgt = tg_ref[...]

        def mix(ohg, ogd, zhg, zgd, w1, w2):
            return jnp.concatenate([_gated(ohg, zhg, w1), _gated(ogd, zgd, w2)], axis=-1)

        y, vjp_mix = jax.vjp(mix, ohg_ref[...], ogd_ref[...], zhg_ref[...], zgd_ref[...], hgw_ref[...], gdw_ref[...])
        out = bdot(y, wo_ref[...])

        def head(out, fwv):
            err = (rmsnorm(hval + out, fwv) - tgt) * valid
            return 0.5 * jnp.sum(jnp.mean(err * err, axis=-1, keepdims=True))

        loss, vjp_head = jax.vjp(head, out, fw_ref[...])
        dout, dfw = vjp_head(jnp.ones((), f32))
        dh_ref[...] = dout
        dy = bdot_nt(dout, wo_ref[...])
        dwo_ref[...] += bdot_tn(y, dout)
        dohg, dogd, dzhg, dzgd, dw1, dw2 = vjp_mix(dy)
        dohg_ref[...] = dohg
        dogd_ref[...] = dogd
        dzhg_ref[...] = dzhg
        dzgd_ref[...] = dzgd
        loss_ref[...] += jnp.broadcast_to(loss, (8, DK))
        dhgw_ref[...] += jnp.broadcast_to(dw1, (8, DK))
        dgdw_ref[...] += jnp.broadcast_to(dw2, (8, DK))
        dfw_ref[...] += jnp.broadcast_to(dfw, (8, D))

    row = lambda w: pl.BlockSpec((tm, w), lambda i: (i, 0))
    whole = lambda r, w: pl.BlockSpec((r, w), lambda i: (0, 0))
    col = lambda c0: pl.BlockSpec((tm, HD), lambda i: (i, c0 // HD))
    return pl.pallas_call(
        body, name="out_loss", grid=(N // tm,),
        in_specs=[row(HD), row(HD), col(COL_ZHG), col(COL_ZGD),
                  whole(1, DK), whole(1, DK), whole(D, D), row(D), whole(1, D), row(D)],
        out_specs=[whole(8, DK), row(HD), row(HD), row(HD), row(HD), row(D), whole(D, D),
                   whole(8, DK), whole(8, DK), whole(8, D)],
        out_shape=[jax.ShapeDtypeStruct((8, DK), f32)] + [jax.ShapeDtypeStruct((N, HD), f32)] * 4
        + [jax.ShapeDtypeStruct((N, D), f32), jax.ShapeDtypeStruct((D, D), f32),
           jax.ShapeDtypeStruct((8, DK), f32), jax.ShapeDtypeStruct((8, DK), f32), jax.ShapeDtypeStruct((8, D), f32)],
        compiler_params=_cp(("arbitrary",)),
    )(o_hg, o_gd, proj, proj, hgw, gdw, wout, hflat, fw, target)


def _in_bwd(pieces, wbig, hflat, norm_w, dh_res):
    tm = 192
    np_ = len(pieces)
    offs = [c0 for _, c0 in pieces]
    widths = [d.shape[1] for d, _ in pieces]

    def body(*refs):
        d_refs = refs[:np_]
        w_ref, h_ref, nw_ref, dhr_ref, dh_ref, dnw_ref = refs[np_:]

        @pl.when(pl.program_id(0) == 0)
        def _():
            dnw_ref[...] = jnp.zeros_like(dnw_ref)

        du = jnp.zeros((tm, D), f32)
        for d_ref, off, wd in zip(d_refs, offs, widths):
            du = du + bdot_nt(d_ref[...], w_ref[:, off:off + wd])
        _, vjp = jax.vjp(rmsnorm, h_ref[...], nw_ref[...])
        dh, dnw = vjp(du)
        dh_ref[...] = dh + dhr_ref[...]
        dnw_ref[...] += jnp.broadcast_to(dnw, (8, D))

    row = lambda w: pl.BlockSpec((tm, w), lambda i: (i, 0))
    return pl.pallas_call(
        body, name="in_bwd", grid=(N // tm,),
        in_specs=[row(w) for w in widths]
        + [pl.BlockSpec((D, PC), lambda i: (0, 0)), row(D), pl.BlockSpec((1, D), lambda i: (0, 0)), row(D)],
        out_specs=[row(D), pl.BlockSpec((8, D), lambda i: (0, 0))],
        out_shape=[jax.ShapeDtypeStruct((N, D), f32), jax.ShapeDtypeStruct((8, D), f32)],
        compiler_params=_cp(("arbitrary",)),
    )(*[d for d, _ in pieces], wbig, hflat, norm_w, dh_res)


def _w_grad(ub, d, name):
    n = d.shape[1]
    tn = min(n, 512)
    tk = N // 4

    def body(u_ref, d_ref, o_ref):
        @pl.when(pl.program_id(1) == 0)
        def _():
            o_ref[...] = jnp.zeros_like(o_ref)

        o_ref[...] += bdot_tn(u_ref[...], d_ref[...])

    return pl.pallas_call(
        body, name=name, grid=(n // tn, N // tk),
        in_specs=[pl.BlockSpec((tk, D), lambda j, k: (k, 0)), pl.BlockSpec((tk, tn), lambda j, k: (k, j))],
        out_specs=pl.BlockSpec((D, tn), lambda j, k: (0, j)),
        out_shape=jax.ShapeDtypeStruct((D, n), f32),
        compiler_params=_cp(("arbitrary", "arbitrary")),
    )(ub, d)


def _adam_math(g, w, m, v):
    m2 = ADAM_B1 * m + (1.0 - ADAM_B1) * g
    v2 = ADAM_B2 * v + (1.0 - ADAM_B2) * (g * g)
    m_hat = m2 / (1.0 - ADAM_B1 ** ADAM_STEP)
    v_hat = v2 / (1.0 - ADAM_B2 ** ADAM_STEP)
    delta = -ADAM_LR * (m_hat / (jnp.sqrt(v_hat) + ADAM_EPS) + ADAM_WD * w)
    return delta, m2, v2


def _adamw(gs, w, m, v, name):
    R, Cc = w.shape
    tr = 256 if R % 256 == 0 else R
    ng = len(gs)

    def body(*refs):
        g = refs[0][...]
        for r in refs[1:ng]:
            g = g + r[...]
        w_ref, m_ref, v_ref, g_ref, d_ref, m2_ref, v2_ref = refs[ng:]
        delta, m2, v2 = _adam_math(g, w_ref[...], m_ref[...], v_ref[...])
        g_ref[...] = g
        d_ref[...] = delta
        m2_ref[...] = m2
        v2_ref[...] = v2

    spec = pl.BlockSpec((tr, Cc), lambda i: (i, 0))
    return pl.pallas_call(
        body, name=name, grid=(R // tr,),
        in_specs=[spec] * (ng + 3), out_specs=[spec] * 4,
        out_shape=[jax.ShapeDtypeStruct((R, Cc), f32)] * 4,
        compiler_params=_cp(("arbitrary",)),
    )(*gs, w, m, v)


def _sum_slots(r, name):
    S, R, Cc = r.shape
    tr = 256 if R % 256 == 0 else R

    def body(r_ref, o_ref):
        acc = r_ref[0]
        for s in range(1, S):
            acc = acc + r_ref[s]
        o_ref[...] = acc

    return pl.pallas_call(
        body, name=name, grid=(R // tr,),
        in_specs=[pl.BlockSpec((S, tr, Cc), lambda i: (0, i, 0))], out_specs=pl.BlockSpec((tr, Cc), lambda i: (i, 0)),
        out_shape=jax.ShapeDtypeStruct((R, Cc), f32),
        compiler_params=_cp(("arbitrary",)),
    )(r)


def _place():
    x, y, c = lax.axis_index("x"), lax.axis_index("y"), lax.axis_index("c")
    return x, y, c, [(1 - x, y), (x, 1 - y), (1 - x, 1 - y)]


def _gather_weights(halved, whole):
    nh, nw = len(halved), len(whole)
    na = nh + nw

    def body(*refs):
        srcs, dsts = refs[:na], refs[na:2 * na]
        send_sems, recv_sems, loc_sems = refs[2 * na:2 * na + 3]
        stage = refs[2 * na + 3:]
        x, y, c, chips = _place()
        me = 2 * x + y
        loads = [pltpu.make_async_copy(s, v, loc_sems.at[i]) for i, (s, v) in enumerate(zip(srcs, stage))]
        locs = [pltpu.make_async_copy(v, d.at[me], loc_sems.at[i]) for i, (v, d) in enumerate(zip(stage, dsts))]
        for cp in loads:
            cp.start()

        def ici(j, i, slot):
            px, py = chips[j]
            src = srcs[i].at[c] if i < nh else srcs[i]
            dst = dsts[i].at[slot, c] if i < nh else dsts[i].at[slot]
            return pltpu.make_async_remote_copy(
                src_ref=src, dst_ref=dst, send_sem=send_sems.at[na * j + i], recv_sem=recv_sems.at[na * j + i],
                device_id=(px, py, c), device_id_type=MESH)

        def d2d(j, i, half):
            px, py = chips[j]
            blk = dsts[i].at[2 * px + py, half]
            return pltpu.make_async_remote_copy(
                src_ref=blk, dst_ref=blk, send_sem=send_sems.at[3 * na + nh * j + i],
                recv_sem=recv_sems.at[3 * na + nh * j + i], device_id=(x, y, 1 - c), device_id_type=MESH)

        sends = [ici(j, i, me) for j in range(3) for i in range(na)]
        for cp in sends:
            cp.start()
        for ld, st in zip(loads, locs):
            ld.wait()
            st.start()
        for j, (px, py) in enumerate(chips):
            for i in range(na):
                ici(j, i, 2 * px + py).wait_recv()
                if i < nh:
                    fwd = d2d(j, i, c)
                    fwd.start()
                    sends.append(fwd)
        for j in range(3):
            for i in range(nh):
                d2d(j, i, 1 - c).wait_recv()
        for cp in sends:
            cp.wait_send()
        for cp in locs:
            cp.wait()

    nsem = 3 * na + 3 * nh
    return pl.pallas_call(
        body, name="gather_weights",
        in_specs=[ANY] * na, out_specs=[ANY] * na,
        out_shape=[jax.ShapeDtypeStruct((4,) + s.shape, s.dtype) for s in list(halved) + list(whole)],
        scratch_shapes=[pltpu.SemaphoreType.DMA((nsem,)), pltpu.SemaphoreType.DMA((nsem,)),
                        pltpu.SemaphoreType.DMA((na,))] + [pltpu.VMEM(s.shape, s.dtype) for s in list(halved) + list(whole)],
        compiler_params=pltpu.CompilerParams(has_side_effects=True, vmem_limit_bytes=VMEM_LIMIT),
    )(*halved, *whole)


def _swap_halves(gs):
    na = len(gs)

    def body(*refs):
        srcs, dsts = refs[:na], refs[na:2 * na]
        send_sems, recv_sems = refs[2 * na:]
        x, y, c, _ = _place()
        cps = [pltpu.make_async_remote_copy(
            src_ref=srcs[i].at[q, 1 - c], dst_ref=dsts[i].at[q], send_sem=send_sems.at[4 * i + q],
            recv_sem=recv_sems.at[4 * i + q], device_id=(x, y, 1 - c), device_id_type=MESH)
            for i in range(na) for q in range(4)]
        for cp in cps:
            cp.start()
        for cp in cps:
            cp.wait()

    return pl.pallas_call(
        body, name="swap_halves",
        in_specs=[ANY] * na, out_specs=[ANY] * na,
        out_shape=[jax.ShapeDtypeStruct((4,) + g.shape[2:], g.dtype) for g in gs],
        scratch_shapes=[pltpu.SemaphoreType.DMA((4 * na,)), pltpu.SemaphoreType.DMA((4 * na,))],
        compiler_params=pltpu.CompilerParams(has_side_effects=True),
    )(*gs)


def _add_halves(c_arr, g, s, name):
    _, _, R, Cc = g.shape
    tr = min(R, 256)

    def body(c_ref, g_ref, s_ref, b_ref, f_ref):
        p = g_ref[0, 0] + s_ref[0]
        f_ref[0] = p
        b_ref[0] = p.astype(bf16)

    blk = pl.BlockSpec((1, tr, Cc), lambda q, i, cr: (q, i, 0))
    return pl.pallas_call(
        body, name=name,
        grid_spec=pltpu.PrefetchScalarGridSpec(
            num_scalar_prefetch=1, grid=(4, R // tr),
            in_specs=[pl.BlockSpec((1, 1, tr, Cc), lambda q, i, cr: (q, cr[0], i, 0)), blk], out_specs=[blk, blk]),
        out_shape=[jax.ShapeDtypeStruct((4, R, Cc), bf16), jax.ShapeDtypeStruct((4, R, Cc), f32)],
        compiler_params=_cp(("arbitrary", "arbitrary")),
    )(c_arr, g, s)


_FLIPS = [(fx, fy, fc) for fx in (0, 1) for fy in (0, 1) for fc in (0, 1)][1:]


def _scatter_blocks(pbs, pack):
    na = len(pbs)
    R = pack.shape[0]

    def body(*refs):
        srcs, pk = refs[:na], refs[na]
        dsts, rp = refs[na + 1:2 * na + 1], refs[2 * na + 1]
        send_sems, recv_sems = refs[2 * na + 2:]
        x, y, c, chips = _place()

        def big(j, i):
            px, py = chips[j]
            return pltpu.make_async_remote_copy(
                src_ref=srcs[i].at[2 * px + py], dst_ref=dsts[i].at[j], send_sem=send_sems.at[na * j + i],
                recv_sem=recv_sems.at[na * j + i], device_id=(px, py, c), device_id_type=MESH)

        def small(k):
            fx, fy, fc = _FLIPS[k]
            return pltpu.make_async_remote_copy(
                src_ref=pk, dst_ref=rp.at[k], send_sem=send_sems.at[3 * na + k], recv_sem=recv_sems.at[3 * na + k],
                device_id=(x ^ fx, y ^ fy, c ^ fc), device_id_type=MESH)

        cps = [big(j, i) for j in range(3) for i in range(na)] + [small(k) for k in range(7)]
        for cp in cps:
            cp.start()
        for cp in cps:
            cp.wait()

    nsem = 3 * na + 7
    return pl.pallas_call(
        body, name="scatter_blocks",
        in_specs=[ANY] * (na + 1), out_specs=[ANY] * (na + 1),
        out_shape=[jax.ShapeDtypeStruct((3,) + p.shape[1:], p.dtype) for p in pbs]
        + [jax.ShapeDtypeStruct((7, R, 128), f32)],
        scratch_shapes=[pltpu.SemaphoreType.DMA((nsem,)), pltpu.SemaphoreType.DMA((nsem,))],
        compiler_params=pltpu.CompilerParams(has_side_effects=True),
    )(*pbs, pack)


def _sum_blocks(me_arr, pf, r, name):
    _, R, Cc = pf.shape
    tr = min(R, 256)

    def body(me_ref, pf_ref, r_ref, o_ref):
        acc = pf_ref[0]
        for j in range(3):
            acc = acc + r_ref[j].astype(f32)
        o_ref[...] = acc

    return pl.pallas_call(
        body, name=name,
        grid_spec=pltpu.PrefetchScalarGridSpec(
            num_scalar_prefetch=1, grid=(R // tr,),
            in_specs=[pl.BlockSpec((1, tr, Cc), lambda i, mr: (mr[0], i, 0)),
                      pl.BlockSpec((3, tr, Cc), lambda i, mr: (0, i, 0))],
            out_specs=pl.BlockSpec((tr, Cc), lambda i, mr: (i, 0))),
        out_shape=jax.ShapeDtypeStruct((R, Cc), f32),
        compiler_params=_cp(("arbitrary",)),
    )(me_arr, pf, r)


def _sum_packs(me8_arr, pack, rp):
    R = pack.shape[0]

    def body(me_ref, pk_ref, rp_ref, o_ref):
        me8 = me_ref[0]
        acc = None
        for d in range(8):
            rel = d ^ me8
            term = jnp.where(rel == 0, pk_ref[...], rp_ref[jnp.maximum(rel - 1, 0)])
            acc = term if acc is None else acc + term
        o_ref[...] = acc

    return pl.pallas_call(
        body, name="sum_packs",
        grid_spec=pltpu.PrefetchScalarGridSpec(
            num_scalar_prefetch=1, grid=(1,),
            in_specs=[pl.BlockSpec((R, 128), lambda i, mr: (0, 0)), pl.BlockSpec((7, R, 128), lambda i, mr: (0, 0, 0))],
            out_specs=pl.BlockSpec((R, 128), lambda i, mr: (0, 0))),
        out_shape=jax.ShapeDtypeStruct((R, 128), f32),
        compiler_params=_cp(("arbitrary",)),
    )(me8_arr, pack, rp)


def _swap_finished(fs):
    na = len(fs)

    def body(*refs):
        srcs, dsts = refs[:na], refs[na:2 * na]
        send_sems, recv_sems = refs[2 * na:]
        x, y, c, _ = _place()
        cps = [pltpu.make_async_remote_copy(
            src_ref=srcs[i], dst_ref=dsts[i], send_sem=send_sems.at[i], recv_sem=recv_sems.at[i],
            device_id=(x, y, 1 - c), device_id_type=MESH) for i in range(na)]
        for cp in cps:
            cp.start()
        for cp in cps:
            cp.wait()

    return pl.pallas_call(
        body, name="swap_finished",
        in_specs=[ANY] * na, out_specs=[ANY] * na,
        out_shape=[jax.ShapeDtypeStruct(f.shape, f.dtype) for f in fs],
        scratch_shapes=[pltpu.SemaphoreType.DMA((na,)), pltpu.SemaphoreType.DMA((na,))],
        compiler_params=pltpu.CompilerParams(has_side_effects=True),
    )(*fs)


def _adamw_halves(c_arr, mine, peer, w, m, v, name):
    _, R, Cc = w.shape
    tr = min(R, 256)

    def body(c_ref, mine_ref, peer_ref, w_ref, m_ref, v_ref, g_ref, d_ref, m2_ref, v2_ref):
        g = jnp.where(pl.program_id(0) == c_ref[0], mine_ref[...], peer_ref[...])
        delta, m2, v2 = _adam_math(g, w_ref[0], m_ref[0], v_ref[0])
        g_ref[0] = g
        d_ref[0] = delta
        m2_ref[0] = m2
        v2_ref[0] = v2

    half = pl.BlockSpec((tr, Cc), lambda hh, i, cr: (i, 0))
    full = pl.BlockSpec((1, tr, Cc), lambda hh, i, cr: (hh, i, 0))
    return pl.pallas_call(
        body, name=name,
        grid_spec=pltpu.PrefetchScalarGridSpec(
            num_scalar_prefetch=1, grid=(2, R // tr), in_specs=[half, half, full, full, full], out_specs=[full] * 4),
        out_shape=[jax.ShapeDtypeStruct((2, R, Cc), f32)] * 4,
        compiler_params=_cp(("arbitrary", "arbitrary")),
    )(c_arr, mine, peer, w, m, v)


def _rows8(a):
    flat = a.reshape(-1)
    n = flat.shape[0]
    rows = -(-n // 1024) * 8
    return jnp.pad(flat, (0, rows * 128 - n)).reshape(rows, 128)


def kernel(x, meta_tokens, norm_w, w_in, conv_w, hg_lb_logits, hg_norm_w, gdn_A_log, gdn_dt_bias, gdn_norm_w, w_out, final_norm_w, loss_target, m_meta_tokens, m_norm_w, m_w_in, m_conv_w, m_hg_lb_logits, m_hg_norm_w, m_gdn_A_log, m_gdn_dt_bias, m_gdn_norm_w, m_w_out, m_final_norm_w, v_meta_tokens, v_norm_w, v_w_in, v_conv_w, v_hg_lb_logits, v_hg_norm_w, v_gdn_A_log, v_gdn_dt_bias, v_gdn_norm_w, v_w_out, v_final_norm_w):
    me = 2 * lax.axis_index("x") + lax.axis_index("y")

    g_win, g_wout, g_conv, g_meta = _gather_weights(
        [w_in[0].astype(bf16).reshape(2, D // 2, SHARD_COLS), w_out[0].astype(bf16).reshape(2, D // 8, D)],
        [conv_w[0], meta_tokens])
    w_full = jnp.transpose(g_win.reshape(4, D, SHARD_COLS), (1, 0, 2)).reshape(D, IN_COLS)
    wbig = jnp.pad(w_full, ((0, 0), (0, PC - IN_COLS)))
    wout_full = g_wout.reshape(D, D)
    conv4 = jnp.transpose(g_conv, (1, 0, 2)).reshape(4, 1, 3 * HD)
    meta_full = jnp.transpose(g_meta, (1, 0, 2)).reshape(N_META, D)

    (loss8, grad_x, d_meta, d_nw, gw_full, d_conv, d_lb, d_hgw, d_alog, d_dtb, d_gdw, g_wout_part, d_fw) = _local_step(
        x, loss_target, wbig, wout_full, conv4, meta_full, norm_w, hg_lb_logits, hg_norm_w, gdn_A_log, gdn_dt_bias,
        gdn_norm_w, final_norm_w)

    g_in_blocks = jnp.transpose(gw_full.reshape(D, 4, SHARD_COLS), (1, 0, 2))
    g_out_blocks = g_wout_part.reshape(4, D // 4, D)
    pack = jnp.concatenate([
        loss8, d_nw[0].reshape(8, 128), d_lb.reshape(8, 128), d_hgw, _rows8(d_alog[0, :H]), _rows8(d_dtb[0, :H]),
        d_gdw, d_fw[0].reshape(8, 128), d_meta.reshape(128, 128), d_conv.reshape(48, 128)], axis=0)
    return _reduce_and_update(
        me, grad_x, g_in_blocks, g_out_blocks, pack, meta_tokens, norm_w, w_in, conv_w, hg_lb_logits, hg_norm_w, gdn_A_log,
        gdn_dt_bias, gdn_norm_w, w_out, final_norm_w, m_meta_tokens, m_norm_w, m_w_in, m_conv_w, m_hg_lb_logits,
        m_hg_norm_w, m_gdn_A_log, m_gdn_dt_bias, m_gdn_norm_w, m_w_out, m_final_norm_w, v_meta_tokens, v_norm_w, v_w_in,
        v_conv_w, v_hg_lb_logits, v_hg_norm_w, v_gdn_A_log, v_gdn_dt_bias, v_gdn_norm_w, v_w_out, v_final_norm_w)


def _local_step(x, loss_target, wbig, wout_full, conv4, meta_full, norm_w, hg_lb_logits, hg_norm_w, gdn_A_log, gdn_dt_bias,
                gdn_norm_w, final_norm_w):
    h3 = jnp.concatenate([jnp.zeros((NB, PAD, D), f32), jnp.broadcast_to(meta_full[None], (NB, N_META, D)), x], axis=1)
    hflat = h3.reshape(N, D)
    target = jnp.pad(loss_target, ((0, 0), (PAD + N_META, 0), (0, 0))).reshape(N, D)
    l0, l1 = hg_lb_logits[0:1], hg_lb_logits[1:2]
    alog = jnp.pad(gdn_A_log, ((0, 0), (0, DK - H)))
    dtb = jnp.pad(gdn_dt_bias, ((0, 0), (0, DK - H)))
    fw = final_norm_w.reshape(1, D)

    proj, ub = _in_proj(hflat, norm_w, wbig)
    proj3 = proj.reshape(NB, TP, PC)
    o_hg, s_hg = _hg_fwd(proj3, l0, l1)
    cv = _conv_fwd(proj3, conv4)
    o_gd, s_gd = _gd_fwd(cv, proj3, alog, dtb)
    (loss8, d_ohg, d_ogd, d_zhg, d_zgd, dh_res, g_wout_part, d_hgw, d_gdw, d_fw) = _out_loss(
        o_hg.reshape(N, HD), o_gd.reshape(N, HD), proj, hg_norm_w, gdn_norm_w, wout_full, hflat, fw, target)
    d_hg, d_l0, d_l1 = _hg_bwd(proj3, l0, l1, s_hg, d_ohg.reshape(NB, TP, HD))
    d_cv, d_ab, d_alog, d_dtb = _gd_bwd(cv, proj3, alog, dtb, s_gd, d_ogd.reshape(NB, TP, HD))
    d_qkv, d_conv4 = _conv_bwd(proj3, conv4, d_cv)
    d_hg2, d_qkv2, d_ab2 = d_hg.reshape(N, 3 * HD), d_qkv.reshape(N, 3 * HD), d_ab.reshape(N, DK)
    pieces = [(d_hg2, COL_HG), (d_zhg, COL_ZHG), (d_qkv2, COL_QKV), (d_zgd, COL_ZGD), (d_ab2, COL_AB)]
    dh, d_nw = _in_bwd(pieces, wbig, hflat, norm_w, dh_res)
    gws = [_w_grad(ub, d, "w_grad_%d" % c0) for d, c0 in pieces]
    gw_full = jnp.concatenate(gws[:4] + [gws[4][:, 0:IN_COLS - COL_AB]], axis=1)

    dh3 = dh.reshape(NB, TP, D)
    grad_x = dh3[:, PAD + N_META:, :]
    d_meta = jnp.sum(dh3[:, PAD:PAD + N_META, :], axis=0)
    d_conv = d_conv4[:, 0, :]
    d_lb = jnp.concatenate([d_l0[0:1], d_l1[0:1]], axis=0)
    return loss8, grad_x, d_meta, d_nw, gw_full, d_conv, d_lb, d_hgw, d_alog, d_dtb, d_gdw, g_wout_part, d_fw


def _reduce_and_update(me, grad_x, g_in_blocks, g_out_blocks, pack, meta_tokens, norm_w, w_in, conv_w, hg_lb_logits, hg_norm_w,
                       gdn_A_log, gdn_dt_bias, gdn_norm_w, w_out, final_norm_w, m_meta_tokens, m_norm_w, m_w_in, m_conv_w,
                       m_hg_lb_logits, m_hg_norm_w, m_gdn_A_log, m_gdn_dt_bias, m_gdn_norm_w, m_w_out, m_final_norm_w,
                       v_meta_tokens, v_norm_w, v_w_in, v_conv_w, v_hg_lb_logits, v_hg_norm_w, v_gdn_A_log, v_gdn_dt_bias,
                       v_gdn_norm_w, v_w_out, v_final_norm_w):
    c_arr = lax.axis_index("c").reshape(1).astype(jnp.int32)
    me_arr = me.reshape(1).astype(jnp.int32)
    g_in4 = g_in_blocks.reshape(4, 2, D // 2, SHARD_COLS)
    g_out4 = g_out_blocks.reshape(4, 2, D // 8, D)
    s_in, s_out = _swap_halves([g_in4, g_out4])
    pb_in, pf_in = _add_halves(c_arr, g_in4, s_in, "add_w_in")
    pb_out, pf_out = _add_halves(c_arr, g_out4, s_out, "add_w_out")
    r_in, r_out, r_pack = _scatter_blocks([pb_in, pb_out], pack)
    f_in = _sum_blocks(me_arr, pf_in, r_in, "sum_w_in")
    f_out = _sum_blocks(me_arr, pf_out, r_out, "sum_w_out")
    o_in, o_out = _swap_finished([f_in, f_out])
    me8_arr = (2 * me + lax.axis_index("c")).reshape(1).astype(jnp.int32)
    small = _sum_packs(me8_arr, pack, r_pack)

    half_in = lambda a: a[0].reshape(2, D // 2, SHARD_COLS)
    half_out = lambda a: a[0].reshape(2, D // 8, D)
    gi, di, mi, vi = [a.reshape(D, SHARD_COLS) for a in _adamw_halves(
        c_arr, f_in, o_in, half_in(w_in), half_in(m_w_in), half_in(v_w_in), "adamw_w_in")]
    go, do_, mo, vo = [a.reshape(D // 4, D) for a in _adamw_halves(
        c_arr, f_out, o_out, half_out(w_out), half_out(m_w_out), half_out(v_w_out), "adamw_w_out")]

    g_meta_full = small[64:192].reshape(N_META, D)
    g_meta_loc = lax.dynamic_slice(g_meta_full, (0, me * 256), (N_META, 256))
    gm, dm, mm_, vm = _adamw([g_meta_loc], meta_tokens, m_meta_tokens, v_meta_tokens, "adamw_meta")
    g_conv_full = small[192:240].reshape(4, 1536)
    g_conv_loc = lax.dynamic_slice(g_conv_full, (0, me * 384), (4, 384))
    gc, dc, mc, vc = _adamw([g_conv_loc], conv_w[0], m_conv_w[0], v_conv_w[0], "adamw_conv")

    reps = [(norm_w, m_norm_w, v_norm_w), (hg_lb_logits, m_hg_lb_logits, v_hg_lb_logits),
            (hg_norm_w, m_hg_norm_w, v_hg_norm_w), (gdn_A_log, m_gdn_A_log, v_gdn_A_log),
            (gdn_dt_bias, m_gdn_dt_bias, v_gdn_dt_bias), (gdn_norm_w, m_gdn_norm_w, v_gdn_norm_w),
            (final_norm_w, m_final_norm_w, v_final_norm_w)]
    wp = jnp.concatenate([_rows8(t[0]) for t in reps], axis=0)
    mp = jnp.concatenate([_rows8(t[1]) for t in reps], axis=0)
    vp = jnp.concatenate([_rows8(t[2]) for t in reps], axis=0)
    gr, dr, mr, vr = _adamw([small[8:64]], wp, mp, vp, "adamw_small")

    def unpack(p):
        outs = []
        for i, t in enumerate(reps):
            n = t[0].size
            outs.append(p[8 * i:8 * i + 8].reshape(-1)[:n].reshape(t[0].shape))
        return outs

    def leaves(meta_v, conv_v, in_v, out_v, rep_p):
        nw, lb, hgw, al, db, gdw, fwv = unpack(rep_p)
        return [meta_v, nw, in_v[None], conv_v[None], lb, hgw, al, db, gdw, out_v[None], fwv]

    loss = small[0, 0]
    return (loss, grad_x, *leaves(gm, gc, gi, go, gr), *leaves(dm, dc, di, do_, dr),
            *leaves(mm_, mc, mi, mo, mr), *leaves(vm, vc, vi, vo, vr))
```

```python
import functools

import jax
import jax.numpy as jnp
from jax import lax
from jax.experimental import pallas as pl
from jax.experimental.pallas import tpu as pltpu

f32 = jnp.float32
bf16 = jnp.bfloat16
MESH = pl.DeviceIdType.MESH
ANY = pl.BlockSpec(memory_space=pl.ANY)

D = 1024
NB = 2
N_META = 16
SEQ = 2048
PAD = 48
TP = PAD + N_META + SEQ
C = 64
NCH = TP // C
N = NB * TP
H = 4
DK = 128
HD = H * DK
PC = 4224
IN_COLS = 4104
SHARD_COLS = IN_COLS // 4
COL_HG, COL_ZHG, COL_QKV, COL_ZGD, COL_AB = 0, 3 * HD, 4 * HD, 7 * HD, 8 * HD
EPS = 1e-6
ADAM_LR, ADAM_B1, ADAM_B2, ADAM_EPS, ADAM_WD, ADAM_STEP = 0.001, 0.9, 0.999, 1e-08, 0.01, 10
VMEM_LIMIT = 56 * 1024 * 1024

P_HG = dict(lvl=1, av=1, qs=1, su=1)
P_GD = dict(kk=1, inv=1, sol=1, ws=1, qk=1, o=1, su=1)


def _cp(sem=None, **kw):
    return pltpu.CompilerParams(dimension_semantics=sem, vmem_limit_bytes=VMEM_LIMIT, **kw)


_DIMS = {"nn": (((1,), (0,)), ((), ())), "nt": (((1,), (1,)), ((), ())), "tn": (((0,), (0,)), ((), ()))}


def _split(x):
    hi = x.astype(bf16)
    return hi, (x - hi.astype(f32)).astype(bf16)


def _dg(a, b, kind, passes):
    d = lambda x, y: lax.dot_general(x, y, _DIMS[kind], preferred_element_type=f32)
    if passes == 1:
        return d(a.astype(bf16), b.astype(bf16))
    ah, al = _split(a)
    bh, bl = _split(b)
    return d(ah, bh) + d(ah, bl) + d(al, bh)


@functools.partial(jax.custom_vjp, nondiff_argnums=(2, 3))
def mmx(a, b, kind, passes):
    return _dg(a, b, kind, passes)


def _mmx_fwd(a, b, kind, passes):
    return _dg(a, b, kind, passes), (a, b)


def _mmx_bwd(kind, passes, res, g):
    a, b = res
    if kind == "nn":
        return _dg(g, b, "nt", passes), _dg(a, g, "tn", passes)
    if kind == "nt":
        return _dg(g, b, "nn", passes), _dg(g, a, "tn", passes)
    return _dg(b, g, "nt", passes), _dg(a, g, "nn", passes)


mmx.defvjp(_mmx_fwd, _mmx_bwd)


def _mask_dg(mask, x, kind):
    n = x.shape[1]
    xh, xl = _split(x)
    r = lax.dot_general(mask, jnp.concatenate([xh, xl], axis=1), _DIMS[kind], preferred_element_type=f32)
    return r[:, :n] + r[:, n:]


@jax.custom_vjp
def mask_mm(mask, x):
    return _mask_dg(mask, x, "nn")


def _mask_fwd(mask, x):
    return _mask_dg(mask, x, "nn"), mask


def _mask_bwd(mask, g):
    return None, _mask_dg(mask, g, "tn")


mask_mm.defvjp(_mask_fwd, _mask_bwd)


def bdot(a, b):
    return jnp.dot(a.astype(bf16), b.astype(bf16), preferred_element_type=f32)


def bdot_nt(a, b):
    return lax.dot_general(a.astype(bf16), b.astype(bf16), _DIMS["nt"], preferred_element_type=f32)


def bdot_tn(a, b):
    return lax.dot_general(a.astype(bf16), b.astype(bf16), _DIMS["tn"], preferred_element_type=f32)


def _iota2(n, m):
    return lax.broadcasted_iota(jnp.int32, (n, m), 0), lax.broadcasted_iota(jnp.int32, (n, m), 1)


def silu(x):
    return x * jax.nn.sigmoid(x)


def softplus(x):
    return jnp.maximum(x, 0.0) + jnp.log(1.0 + jnp.exp(-jnp.abs(x)))


def rmsnorm(x, w):
    return x * lax.rsqrt(jnp.mean(x * x, axis=-1, keepdims=True) + EPS) * w


def hg_masks():
    t, r = _iota2(C, C)
    mats = [r <= t, r > t]
    lvl = []
    for l in range(1, 7):
        sz = 1 << l
        half = sz >> 1
        seg_t = t >> l
        upper_t = (t & (sz - 1)) >= half
        mid_t = seg_t * sz + half - 1
        mats.append(upper_t & (r > mid_t) & (r <= t))
        mats.append((~upper_t) & (r > t) & (r <= mid_t))
        lvl.append(((seg_t == (r >> l)) & upper_t & ((r & (sz - 1)) < half)).astype(f32))
    stk = jnp.concatenate([m.astype(bf16) for m in mats], axis=0)
    return stk, lvl, (t == r).astype(f32)


def _head(a, h):
    return a[:, h * DK:(h + 1) * DK]


def hg_chunk(St, ps, l0, l1):
    m = jnp.maximum(l0, l1)
    e0 = jnp.exp(l0 - m)
    e1 = jnp.exp(l1 - m)
    lb = e0 / (e0 + e1)
    stk, lvl, eye = hg_masks()
    msk = [eye] + lvl
    qs, ks, vs, qG, kR, eGl = [], [], [], [], [], []
    for p in ps:
        pq, pf, v = p[:, 0:HD], p[:, HD:2 * HD], p[:, 2 * HD:3 * HD]
        q = silu(pq)
        f = lb + (1.0 - lb) * jax.nn.sigmoid(pf)
        k = 1.0 - f
        logf = jnp.log(f)
        Dm = mask_mm(stk, logf)
        qs.append([q] + [q * jnp.exp(Dm[(2 + 2 * i) * C:(3 + 2 * i) * C]) for i in range(6)])
        ks.append([k] + [k * jnp.exp(Dm[(3 + 2 * i) * C:(4 + 2 * i) * C]) for i in range(6)])
        vs.append(v)
        qG.append(q * jnp.exp(Dm[0:C]))
        kR.append(k * jnp.exp(Dm[C:2 * C]))
        eGl.append(jnp.exp(jnp.sum(logf, axis=0, keepdims=True)))
    units = [(b, h) for b in range(len(ps)) for h in range(H)]
    parts = [[msk[i] * mmx(_head(qs[b][i], h), _head(ks[b][i], h), "nt", P_HG["lvl"]) for b, h in units]
             for i in range(7)]
    A = [functools.reduce(lambda x, y: x + y, [parts[i][n] for i in range(7)]) for n in range(len(units))]
    qS = [mmx(_head(qG[b], h), St[n], "nt", P_HG["qs"]) for n, (b, h) in enumerate(units)]
    Sn = [St[n] * _head(eGl[b], h) + mmx(_head(vs[b], h), _head(kR[b], h), "tn", P_HG["su"])
          for n, (b, h) in enumerate(units)]
    outs = [mmx(A[n], _head(vs[b], h), "nn", P_HG["av"]) + qS[n] for n, (b, h) in enumerate(units)]
    return tuple(Sn), tuple(jnp.concatenate(outs[b * H:(b + 1) * H], axis=1) for b in range(len(ps)))


def gd_chunk(S, cs, abs_, alog, dtb):
    t, r = _iota2(C, C)
    tri = (r <= t).astype(bf16)
    ups = (r > t).astype(bf16)
    lane = lax.broadcasted_iota(jnp.int32, (1, DK), 1)
    subl = lax.broadcasted_iota(jnp.int32, (8, 1), 0)
    eye = (t == r).astype(f32)
    strict = (r < t).astype(f32)
    bd = ((t >> 4) == (r >> 4)).astype(f32)
    qa, ka, va, b4, gam4, grev4, gam4T, glast4 = [], [], [], [], [], [], [], []
    for c, ab in zip(cs, abs_):
        qa.append(silu(c[:, 0:HD]))
        ka.append(silu(c[:, HD:2 * HD]))
        va.append(silu(c[:, 2 * HD:3 * HD]))
        g4 = -jnp.exp(alog) * softplus(ab + dtb)
        b4.append(jax.nn.sigmoid(ab))
        gam4.append(mask_mm(tri, g4))
        grev4.append(mask_mm(ups, g4))
        gam4T.append(gam4[-1].T)
        glast4.append(jnp.sum(g4, axis=0, keepdims=True))
    units = [(b, h) for b in range(len(cs)) for h in range(H)]
    nu = range(len(units))
    inv = lambda a, b: [mmx(a[n], b[n], "nn", P_GD["inv"]) for n in nu]
    v = [_head(va[b], h) for b, h in units]
    q = [_head(qa[b], h) for b, h in units]
    k = [_head(ka[b], h) for b, h in units]
    q = [x * lax.rsqrt(jnp.sum(x * x, -1, keepdims=True) + EPS) * (DK ** -0.5) for x in q]
    k = [x * lax.rsqrt(jnp.sum(x * x, -1, keepdims=True) + EPS) for x in k]
    oh = [(lane == h).astype(f32) for h in range(H)]
    gam_c = [jnp.sum(gam4[b] * oh[h], -1, keepdims=True) for b, h in units]
    grev_c = [jnp.sum(grev4[b] * oh[h], -1, keepdims=True) for b, h in units]
    beta = [jnp.sum(b4[b] * (lane == h + H).astype(f32), -1, keepdims=True) for b, h in units]
    glast = [jnp.sum(glast4[b] * oh[h], -1, keepdims=True) for b, h in units]
    gam_r = [jnp.sum(gam4T[b][0:8, :] * (subl == h).astype(f32), axis=0, keepdims=True) for b, h in units]
    dec = [jnp.exp(jnp.where(r <= t, gam_c[n] - gam_r[n], -1e30)) for n in nu]
    egam = [jnp.exp(gam_c[n]) for n in nu]
    kk = [mmx(k[n], k[n], "nt", P_GD["kk"]) for n in nu]
    qk = [mmx(q[n], k[n], "nt", P_GD["qk"]) * dec[n] for n in nu]
    A = [beta[n] * kk[n] * dec[n] * strict for n in nu]
    Dg = [A[n] * bd for n in nu]
    L = [A[n] - Dg[n] for n in nu]
    ImD = [eye - Dg[n] for n in nu]
    D2 = inv(Dg, Dg)
    P1 = inv(ImD, [eye + x for x in D2])
    D4 = inv(D2, D2)
    P2 = inv(P1, [eye + x for x in D4])
    D8 = inv(D4, D4)
    M = inv(P2, [eye + x for x in D8])
    Nn = inv(M, L)
    N2 = inv(Nn, Nn)
    T1 = inv([eye - x for x in Nn], [eye + x for x in N2])
    Tinv = inv(T1, M)
    rhs = [jnp.concatenate([beta[n] * v[n], (beta[n] * egam[n]) * k[n]], axis=1) for n in nu]
    sol = [mmx(Tinv[n], rhs[n], "nn", P_GD["sol"]) for n in nu]
    qwS = [mmx(jnp.concatenate([q[n] * egam[n], sol[n][:, DK:2 * DK]], axis=0), S[n], "nn", P_GD["ws"]) for n in nu]
    u = [sol[n][:, 0:DK] - qwS[n][C:2 * C] for n in nu]
    outs = [qwS[n][0:C] + mmx(qk[n], u[n], "nn", P_GD["o"]) for n in nu]
    Sn = [jnp.exp(glast[n]) * S[n] + mmx(k[n] * jnp.exp(grev_c[n]), u[n], "tn", P_GD["su"]) for n in nu]
    return tuple(Sn), tuple(jnp.concatenate(outs[b * H:(b + 1) * H], axis=1) for b in range(len(cs)))


def _in_proj(hflat, norm_w, wbig):
    tm = 384

    def body(h_ref, nw_ref, w_ref, p_ref, u_ref):
        u = rmsnorm(h_ref[...], nw_ref[...]).astype(bf16)
        u_ref[...] = u
        p_ref[...] = jnp.dot(u, w_ref[...], preferred_element_type=f32)

    return pl.pallas_call(
        body, name="in_proj", grid=(N // tm,),
        in_specs=[pl.BlockSpec((tm, D), lambda i: (i, 0)), pl.BlockSpec((1, D), lambda i: (0, 0)),
                  pl.BlockSpec((D, PC), lambda i: (0, 0))],
        out_specs=[pl.BlockSpec((tm, PC), lambda i: (i, 0)), pl.BlockSpec((tm, D), lambda i: (i, 0))],
        out_shape=[jax.ShapeDtypeStruct((N, PC), f32), jax.ShapeDtypeStruct((N, D), bf16)],
        compiler_params=_cp(("arbitrary",)),
    )(hflat, norm_w, wbig)


NU = NB * H
_REV = lambda c: NCH - 1 - c
_FWD = lambda c: c


def _tok_spec(w, ix, col=0):
    return pl.BlockSpec((NB, C, w), lambda c: (0, ix(c), col))


def _state_spec(ix):
    return pl.BlockSpec((NB, 1, H, DK, DK), lambda c: (0, ix(c), 0, 0, 0))


def _row_spec(w):
    return pl.BlockSpec((1, w), lambda c: (0, 0))


def _rows(ref):
    return tuple(ref[b] for b in range(NB))


def _scan_fwd(chunk_fn, name, x3, extra, extra_specs, n_tok):
    ne = len(extra)

    def body(*refs):
        x_ref = refs[0]
        e_refs = refs[1:1 + ne]
        o_ref, s_ref, st = refs[1 + ne:]

        @pl.when(pl.program_id(0) == 0)
        def _():
            st[...] = jnp.zeros_like(st)

        S = tuple(st[n] for n in range(NU))
        for n in range(NU):
            s_ref[n // H, 0, n % H] = S[n]
        args = [_rows(e) if i < n_tok else e[...] for i, e in enumerate(e_refs)]
        Sn, o = chunk_fn(S, _rows(x_ref), *args)
        for n in range(NU):
            st[n] = Sn[n]
        for b in range(NB):
            o_ref[b] = o[b]

    return pl.pallas_call(
        body, name=name, grid=(NCH,),
        in_specs=[_tok_spec(3 * HD, _FWD)] + extra_specs(_FWD),
        out_specs=[_tok_spec(HD, _FWD), _state_spec(_FWD)],
        out_shape=[jax.ShapeDtypeStruct((NB, TP, HD), f32), jax.ShapeDtypeStruct((NB, NCH, H, DK, DK), f32)],
        scratch_shapes=[pltpu.VMEM((NU, DK, DK), f32)],
        compiler_params=_cp(("arbitrary",)),
    )(x3, *extra)


def _hg_extra_specs(ix):
    return [_row_spec(HD), _row_spec(HD)]


def _gd_extra_specs(ix):
    return [_tok_spec(DK, ix, COL_AB // DK), _row_spec(DK), _row_spec(DK)]


def _hg_fwd(proj3, l0, l1):
    return _scan_fwd(hg_chunk, "hg_fwd", proj3, [l0, l1], _hg_extra_specs, 0)


def _gd_fwd(cv, proj3, alog, dtb):
    return _scan_fwd(gd_chunk, "gd_fwd", cv, [proj3, alog, dtb], _gd_extra_specs, 1)


def _hg_bwd(proj3, l0, l1, s_saved, do):
    def body(p_ref, l0_ref, l1_ref, s_ref, do_ref, dp_ref, dl0_ref, dl1_ref, dst):
        @pl.when(pl.program_id(0) == 0)
        def _():
            dst[...] = jnp.zeros_like(dst)
            dl0_ref[...] = jnp.zeros_like(dl0_ref)
            dl1_ref[...] = jnp.zeros_like(dl1_ref)

        S = tuple(s_ref[n // H, 0, n % H] for n in range(NU))
        _, vjp = jax.vjp(hg_chunk, S, _rows(p_ref), l0_ref[...], l1_ref[...])
        dS, dp, dl0, dl1 = vjp((tuple(dst[n] for n in range(NU)), _rows(do_ref)))
        for n in range(NU):
            dst[n] = dS[n]
        for b in range(NB):
            dp_ref[b] = dp[b]
        dl0_ref[...] += jnp.broadcast_to(dl0, (8, HD))
        dl1_ref[...] += jnp.broadcast_to(dl1, (8, HD))

    acc = pl.BlockSpec((8, HD), lambda c: (0, 0))
    return pl.pallas_call(
        body, name="hg_bwd", grid=(NCH,),
        in_specs=[_tok_spec(3 * HD, _REV)] + _hg_extra_specs(_REV) + [_state_spec(_REV), _tok_spec(HD, _REV)],
        out_specs=[_tok_spec(3 * HD, _REV), acc, acc],
        out_shape=[jax.ShapeDtypeStruct((NB, TP, 3 * HD), f32), jax.ShapeDtypeStruct((8, HD), f32),
                   jax.ShapeDtypeStruct((8, HD), f32)],
        scratch_shapes=[pltpu.VMEM((NU, DK, DK), f32)],
        compiler_params=_cp(("arbitrary",)),
    )(proj3, l0, l1, s_saved, do)


def _gd_bwd(cv, proj3, alog, dtb, s_saved, do):
    def body(c_ref, ab_ref, al_ref, db_ref, s_ref, do_ref, dc_ref, dab_ref, dal_ref, ddb_ref, dst):
        @pl.when(pl.program_id(0) == 0)
        def _():
            dst[...] = jnp.zeros_like(dst)
            dal_ref[...] = jnp.zeros_like(dal_ref)
            ddb_ref[...] = jnp.zeros_like(ddb_ref)

        S = tuple(s_ref[n // H, 0, n % H] for n in range(NU))
        _, vjp = jax.vjp(gd_chunk, S, _rows(c_ref), _rows(ab_ref), al_ref[...], db_ref[...])
        dS, dc, dab, dal, ddb = vjp((tuple(dst[n] for n in range(NU)), _rows(do_ref)))
        for n in range(NU):
            dst[n] = dS[n]
        for b in range(NB):
            dc_ref[b] = dc[b]
            dab_ref[b] = dab[b]
        dal_ref[...] += jnp.broadcast_to(dal, (8, DK))
        ddb_ref[...] += jnp.broadcast_to(ddb, (8, DK))

    acc = pl.BlockSpec((8, DK), lambda c: (0, 0))
    return pl.pallas_call(
        body, name="gd_bwd", grid=(NCH,),
        in_specs=[_tok_spec(3 * HD, _REV)] + _gd_extra_specs(_REV) + [_state_spec(_REV), _tok_spec(HD, _REV)],
        out_specs=[_tok_spec(3 * HD, _REV), _tok_spec(DK, _REV), acc, acc],
        out_shape=[jax.ShapeDtypeStruct((NB, TP, 3 * HD), f32), jax.ShapeDtypeStruct((NB, TP, DK), f32),
                   jax.ShapeDtypeStruct((8, DK), f32), jax.ShapeDtypeStruct((8, DK), f32)],
        scratch_shapes=[pltpu.VMEM((NU, DK, DK), f32)],
        compiler_params=_cp(("arbitrary",)),
    )(cv, proj3, alog, dtb, s_saved, do)


def _conv_fwd(proj3, conv4):
    def body(x_ref, w_ref, y_ref):
        x = x_ref[0]
        row = lax.broadcasted_iota(jnp.int32, (TP, 1), 0)
        y = w_ref[3] * x
        for s in (1, 2, 3):
            y = y + w_ref[3 - s] * jnp.where(row >= s, pltpu.roll(x, s, 0), 0.0)
        y_ref[0] = y

    return pl.pallas_call(
        body, name="conv_fwd", grid=(NB, 3),
        in_specs=[pl.BlockSpec((1, TP, HD), lambda b, j: (b, 0, COL_QKV // HD + j)),
                  pl.BlockSpec((4, 1, HD), lambda b, j: (0, 0, j))],
        out_specs=pl.BlockSpec((1, TP, HD), lambda b, j: (b, 0, j)),
        out_shape=jax.ShapeDtypeStruct((NB, TP, 3 * HD), f32),
        compiler_params=_cp(("arbitrary", "arbitrary")),
    )(proj3, conv4)


def _conv_bwd(proj3, conv4, dy):
    def body(x_ref, w_ref, dy_ref, dx_ref, dw_ref):
        @pl.when(pl.program_id(1) == 0)
        def _():
            dw_ref[...] = jnp.zeros_like(dw_ref)

        x = x_ref[0]
        g = dy_ref[0]
        row = lax.broadcasted_iota(jnp.int32, (TP, 1), 0)
        dx = w_ref[3] * g
        dw_ref[3] += jnp.broadcast_to(jnp.sum(x * g, axis=0, keepdims=True), (8, HD))
        for s in (1, 2, 3):
            dx = dx + w_ref[3 - s] * jnp.where(row < TP - s, pltpu.roll(g, TP - s, 0), 0.0)
            xs = jnp.where(row >= s, pltpu.roll(x, s, 0), 0.0)
            dw_ref[3 - s] += jnp.broadcast_to(jnp.sum(xs * g, axis=0, keepdims=True), (8, HD))
        dx_ref[0] = dx

    return pl.pallas_call(
        body, name="conv_bwd", grid=(3, NB),
        in_specs=[pl.BlockSpec((1, TP, HD), lambda j, b: (b, 0, COL_QKV // HD + j)),
                  pl.BlockSpec((4, 1, HD), lambda j, b: (0, 0, j)), pl.BlockSpec((1, TP, HD), lambda j, b: (b, 0, j))],
        out_specs=[pl.BlockSpec((1, TP, HD), lambda j, b: (b, 0, j)), pl.BlockSpec((4, 8, HD), lambda j, b: (0, 0, j))],
        out_shape=[jax.ShapeDtypeStruct((NB, TP, 3 * HD), f32), jax.ShapeDtypeStruct((4, 8, 3 * HD), f32)],
        compiler_params=_cp(("arbitrary", "arbitrary")),
    )(proj3, conv4, dy)


def _gated(o, z, w):
    outs = []
    for hh in range(H):
        sl = slice(hh * DK, (hh + 1) * DK)
        outs.append(rmsnorm(o[:, sl], w) * silu(z[:, sl]))
    return jnp.concatenate(outs, axis=-1)


def _out_loss(o_hg, o_gd, proj, hgw, gdw, wout, hflat, fw, target):
    tm = 384

    def body(ohg_ref, ogd_ref, zhg_ref, zgd_ref, hgw_ref, gdw_ref, wo_ref, h_ref, fw_ref, tg_ref,
             loss_ref, dohg_ref, dogd_ref, dzhg_ref, dzgd_ref, dh_ref, dwo_ref, dhgw_ref, dgdw_ref, dfw_ref):
        i = pl.program_id(0)

        @pl.when(i == 0)
        def _():
            for r in (loss_ref, dwo_ref, dhgw_ref, dgdw_ref, dfw_ref):
                r[...] = jnp.zeros_like(r)

        row = i * tm + lax.broadcasted_iota(jnp.int32, (tm, 1), 0)
        tok = jnp.where(row >= TP, row - TP, row)
        valid = (tok >= PAD + N_META).astype(f32)
        hval = h_ref[...]
        tgt = tg_ref[...]

        def mix(ohg, ogd, zhg, zgd, w1, w2):
            return jnp.concatenate([_gated(ohg, zhg, w1), _gated(ogd, zgd, w2)], axis=-1)

        y, vjp_mix = jax.vjp(mix, ohg_ref[...], ogd_ref[...], zhg_ref[...], zgd_ref[...], hgw_ref[...], gdw_ref[...])
        out = bdot(y, wo_ref[...])

        def head(out, fwv):
            err = (rmsnorm(hval + out, fwv) - tgt) * valid
            return 0.5 * jnp.sum(jnp.mean(err * err, axis=-1, keepdims=True))

        loss, vjp_head = jax.vjp(head, out, fw_ref[...])
        dout, dfw = vjp_head(jnp.ones((), f32))
        dh_ref[...] = dout
        dy = bdot_nt(dout, wo_ref[...])
        dwo_ref[...] += bdot_tn(y, dout)
        dohg, dogd, dzhg, dzgd, dw1, dw2 = vjp_mix(dy)
        dohg_ref[...] = dohg
        dogd_ref[...] = dogd
        dzhg_ref[...] = dzhg
        dzgd_ref[...] = dzgd
        loss_ref[...] += jnp.broadcast_to(loss, (8, DK))
        dhgw_ref[...] += jnp.broadcast_to(dw1, (8, DK))
        dgdw_ref[...] += jnp.broadcast_to(dw2, (8, DK))
        dfw_ref[...] += jnp.broadcast_to(dfw, (8, D))

    row = lambda w: pl.BlockSpec((tm, w), lambda i: (i, 0))
    whole = lambda r, w: pl.BlockSpec((r, w), lambda i: (0, 0))
    col = lambda c0: pl.BlockSpec((tm, HD), lambda i: (i, c0 // HD))
    return pl.pallas_call(
        body, name="out_loss", grid=(N // tm,),
        in_specs=[row(HD), row(HD), col(COL_ZHG), col(COL_ZGD),
                  whole(1, DK), whole(1, DK), whole(D, D), row(D), whole(1, D), row(D)],
        out_specs=[whole(8, DK), row(HD), row(HD), row(HD), row(HD), row(D), whole(D, D),
                   whole(8, DK), whole(8, DK), whole(8, D)],
        out_shape=[jax.ShapeDtypeStruct((8, DK), f32)] + [jax.ShapeDtypeStruct((N, HD), f32)] * 4
        + [jax.ShapeDtypeStruct((N, D), f32), jax.ShapeDtypeStruct((D, D), f32),
           jax.ShapeDtypeStruct((8, DK), f32), jax.ShapeDtypeStruct((8, DK), f32), jax.ShapeDtypeStruct((8, D), f32)],
        compiler_params=_cp(("arbitrary",)),
    )(o_hg, o_gd, proj, proj, hgw, gdw, wout, hflat, fw, target)


def _in_bwd(pieces, wbig, hflat, norm_w, dh_res):
    tm = 192
    np_ = len(pieces)
    offs = [c0 for _, c0 in pieces]
    widths = [d.shape[1] for d, _ in pieces]

    def body(*refs):
        d_refs = refs[:np_]
        w_ref, h_ref, nw_ref, dhr_ref, dh_ref, dnw_ref = refs[np_:]

        @pl.when(pl.program_id(0) == 0)
        def _():
            dnw_ref[...] = jnp.zeros_like(dnw_ref)

        du = jnp.zeros((tm, D), f32)
        for d_ref, off, wd in zip(d_refs, offs, widths):
            du = du + bdot_nt(d_ref[...], w_ref[:, off:off + wd])
        _, vjp = jax.vjp(rmsnorm, h_ref[...], nw_ref[...])
        dh, dnw = vjp(du)
        dh_ref[...] = dh + dhr_ref[...]
        dnw_ref[...] += jnp.broadcast_to(dnw, (8, D))

    row = lambda w: pl.BlockSpec((tm, w), lambda i: (i, 0))
    return pl.pallas_call(
        body, name="in_bwd", grid=(N // tm,),
        in_specs=[row(w) for w in widths]
        + [pl.BlockSpec((D, PC), lambda i: (0, 0)), row(D), pl.BlockSpec((1, D), lambda i: (0, 0)), row(D)],
        out_specs=[row(D), pl.BlockSpec((8, D), lambda i: (0, 0))],
        out_shape=[jax.ShapeDtypeStruct((N, D), f32), jax.ShapeDtypeStruct((8, D), f32)],
        compiler_params=_cp(("arbitrary",)),
    )(*[d for d, _ in pieces], wbig, hflat, norm_w, dh_res)


def _w_grad(ub, d, name):
    n = d.shape[1]
    tn = min(n, 512)
    tk = N // 4

    def body(u_ref, d_ref, o_ref):
        @pl.when(pl.program_id(1) == 0)
        def _():
            o_ref[...] = jnp.zeros_like(o_ref)

        o_ref[...] += bdot_tn(u_ref[...], d_ref[...])

    return pl.pallas_call(
        body, name=name, grid=(n // tn, N // tk),
        in_specs=[pl.BlockSpec((tk, D), lambda j, k: (k, 0)), pl.BlockSpec((tk, tn), lambda j, k: (k, j))],
        out_specs=pl.BlockSpec((D, tn), lambda j, k: (0, j)),
        out_shape=jax.ShapeDtypeStruct((D, n), f32),
        compiler_params=_cp(("arbitrary", "arbitrary")),
    )(ub, d)


def _adam_math(g, w, m, v):
    m2 = ADAM_B1 * m + (1.0 - ADAM_B1) * g
    v2 = ADAM_B2 * v + (1.0 - ADAM_B2) * (g * g)
    m_hat = m2 / (1.0 - ADAM_B1 ** ADAM_STEP)
    v_hat = v2 / (1.0 - ADAM_B2 ** ADAM_STEP)
    delta = -ADAM_LR * (m_hat / (jnp.sqrt(v_hat) + ADAM_EPS) + ADAM_WD * w)
    return delta, m2, v2


def _adamw(gs, w, m, v, name):
    R, Cc = w.shape
    tr = 256 if R % 256 == 0 else R
    ng = len(gs)

    def body(*refs):
        g = refs[0][...]
        for r in refs[1:ng]:
            g = g + r[...]
        w_ref, m_ref, v_ref, g_ref, d_ref, m2_ref, v2_ref = refs[ng:]
        delta, m2, v2 = _adam_math(g, w_ref[...], m_ref[...], v_ref[...])
        g_ref[...] = g
        d_ref[...] = delta
        m2_ref[...] = m2
        v2_ref[...] = v2

    spec = pl.BlockSpec((tr, Cc), lambda i: (i, 0))
    return pl.pallas_call(
        body, name=name, grid=(R // tr,),
        in_specs=[spec] * (ng + 3), out_specs=[spec] * 4,
        out_shape=[jax.ShapeDtypeStruct((R, Cc), f32)] * 4,
        compiler_params=_cp(("arbitrary",)),
    )(*gs, w, m, v)


def _sum_slots(r, name):
    S, R, Cc = r.shape
    tr = 256 if R % 256 == 0 else R

    def body(r_ref, o_ref):
        acc = r_ref[0]
        for s in range(1, S):
            acc = acc + r_ref[s]
        o_ref[...] = acc

    return pl.pallas_call(
        body, name=name, grid=(R // tr,),
        in_specs=[pl.BlockSpec((S, tr, Cc), lambda i: (0, i, 0))], out_specs=pl.BlockSpec((tr, Cc), lambda i: (i, 0)),
        out_shape=jax.ShapeDtypeStruct((R, Cc), f32),
        compiler_params=_cp(("arbitrary",)),
    )(r)


def _place():
    x, y, c = lax.axis_index("x"), lax.axis_index("y"), lax.axis_index("c")
    return x, y, c, [(1 - x, y), (x, 1 - y), (1 - x, 1 - y)]


def _gather_weights(halved, whole):
    nh, nw = len(halved), len(whole)
    na = nh + nw

    def body(*refs):
        srcs, dsts = refs[:na], refs[na:2 * na]
        send_sems, recv_sems, loc_sems = refs[2 * na:2 * na + 3]
        stage = refs[2 * na + 3:]
        x, y, c, chips = _place()
        me = 2 * x + y
        loads = [pltpu.make_async_copy(s, v, loc_sems.at[i]) for i, (s, v) in enumerate(zip(srcs, stage))]
        locs = [pltpu.make_async_copy(v, d.at[me], loc_sems.at[i]) for i, (v, d) in enumerate(zip(stage, dsts))]
        for cp in loads:
            cp.start()

        def ici(j, i, slot):
            px, py = chips[j]
            src = srcs[i].at[c] if i < nh else srcs[i]
            dst = dsts[i].at[slot, c] if i < nh else dsts[i].at[slot]
            return pltpu.make_async_remote_copy(
                src_ref=src, dst_ref=dst, send_sem=send_sems.at[na * j + i], recv_sem=recv_sems.at[na * j + i],
                device_id=(px, py, c), device_id_type=MESH)

        def d2d(j, i, half):
            px, py = chips[j]
            blk = dsts[i].at[2 * px + py, half]
            return pltpu.make_async_remote_copy(
                src_ref=blk, dst_ref=blk, send_sem=send_sems.at[3 * na + nh * j + i],
                recv_sem=recv_sems.at[3 * na + nh * j + i], device_id=(x, y, 1 - c), device_id_type=MESH)

        sends = [ici(j, i, me) for j in range(3) for i in range(na)]
        for cp in sends:
            cp.start()
        for ld, st in zip(loads, locs):
            ld.wait()
            st.start()
        for j, (px, py) in enumerate(chips):
            for i in range(na):
                ici(j, i, 2 * px + py).wait_recv()
                if i < nh:
                    fwd = d2d(j, i, c)
                    fwd.start()
                    sends.append(fwd)
        for j in range(3):
            for i in range(nh):
                d2d(j, i, 1 - c).wait_recv()
        for cp in sends:
            cp.wait_send()
        for cp in locs:
            cp.wait()

    nsem = 3 * na + 3 * nh
    return pl.pallas_call(
        body, name="gather_weights",
        in_specs=[ANY] * na, out_specs=[ANY] * na,
        out_shape=[jax.ShapeDtypeStruct((4,) + s.shape, s.dtype) for s in list(halved) + list(whole)],
        scratch_shapes=[pltpu.SemaphoreType.DMA((nsem,)), pltpu.SemaphoreType.DMA((nsem,)),
                        pltpu.SemaphoreType.DMA((na,))] + [pltpu.VMEM(s.shape, s.dtype) for s in list(halved) + list(whole)],
        compiler_params=pltpu.CompilerParams(has_side_effects=True, vmem_limit_bytes=VMEM_LIMIT),
    )(*halved, *whole)


def _swap_halves(gs):
    na = len(gs)

    def body(*refs):
        srcs, dsts = refs[:na], refs[na:2 * na]
        send_sems, recv_sems = refs[2 * na:]
        x, y, c, _ = _place()
        cps = [pltpu.make_async_remote_copy(
            src_ref=srcs[i].at[q, 1 - c], dst_ref=dsts[i].at[q], send_sem=send_sems.at[4 * i + q],
            recv_sem=recv_sems.at[4 * i + q], device_id=(x, y, 1 - c), device_id_type=MESH)
            for i in range(na) for q in range(4)]
        for cp in cps:
            cp.start()
        for cp in cps:
            cp.wait()

    return pl.pallas_call(
        body, name="swap_halves",
        in_specs=[ANY] * na, out_specs=[ANY] * na,
        out_shape=[jax.ShapeDtypeStruct((4,) + g.shape[2:], g.dtype) for g in gs],
        scratch_shapes=[pltpu.SemaphoreType.DMA((4 * na,)), pltpu.SemaphoreType.DMA((4 * na,))],
        compiler_params=pltpu.CompilerParams(has_side_effects=True),
    )(*gs)


def _add_halves(c_arr, g, s, name):
    _, _, R, Cc = g.shape
    tr = min(R, 256)

    def body(c_ref, g_ref, s_ref, b_ref, f_ref):
        p = g_ref[0, 0] + s_ref[0]
        f_ref[0] = p
        b_ref[0] = p.astype(bf16)

    blk = pl.BlockSpec((1, tr, Cc), lambda q, i, cr: (q, i, 0))
    return pl.pallas_call(
        body, name=name,
        grid_spec=pltpu.PrefetchScalarGridSpec(
            num_scalar_prefetch=1, grid=(4, R // tr),
            in_specs=[pl.BlockSpec((1, 1, tr, Cc), lambda q, i, cr: (q, cr[0], i, 0)), blk], out_specs=[blk, blk]),
        out_shape=[jax.ShapeDtypeStruct((4, R, Cc), bf16), jax.ShapeDtypeStruct((4, R, Cc), f32)],
        compiler_params=_cp(("arbitrary", "arbitrary")),
    )(c_arr, g, s)


_FLIPS = [(fx, fy, fc) for fx in (0, 1) for fy in (0, 1) for fc in (0, 1)][1:]


def _scatter_blocks(pbs, pack):
    na = len(pbs)
    R = pack.shape[0]

    def body(*refs):
        srcs, pk = refs[:na], refs[na]
        dsts, rp = refs[na + 1:2 * na + 1], refs[2 * na + 1]
        send_sems, recv_sems = refs[2 * na + 2:]
        x, y, c, chips = _place()

        def big(j, i):
            px, py = chips[j]
            return pltpu.make_async_remote_copy(
                src_ref=srcs[i].at[2 * px + py], dst_ref=dsts[i].at[j], send_sem=send_sems.at[na * j + i],
                recv_sem=recv_sems.at[na * j + i], device_id=(px, py, c), device_id_type=MESH)

        def small(k):
            fx, fy, fc = _FLIPS[k]
            return pltpu.make_async_remote_copy(
                src_ref=pk, dst_ref=rp.at[k], send_sem=send_sems.at[3 * na + k], recv_sem=recv_sems.at[3 * na + k],
                device_id=(x ^ fx, y ^ fy, c ^ fc), device_id_type=MESH)

        cps = [big(j, i) for j in range(3) for i in range(na)] + [small(k) for k in range(7)]
        for cp in cps:
            cp.start()
        for cp in cps:
            cp.wait()

    nsem = 3 * na + 7
    return pl.pallas_call(
        body, name="scatter_blocks",
        in_specs=[ANY] * (na + 1), out_specs=[ANY] * (na + 1),
        out_shape=[jax.ShapeDtypeStruct((3,) + p.shape[1:], p.dtype) for p in pbs]
        + [jax.ShapeDtypeStruct((7, R, 128), f32)],
        scratch_shapes=[pltpu.SemaphoreType.DMA((nsem,)), pltpu.SemaphoreType.DMA((nsem,))],
        compiler_params=pltpu.CompilerParams(has_side_effects=True),
    )(*pbs, pack)


def _sum_blocks(me_arr, pf, r, name):
    _, R, Cc = pf.shape
    tr = min(R, 256)

    def body(me_ref, pf_ref, r_ref, o_ref):
        acc = pf_ref[0]
        for j in range(3):
            acc = acc + r_ref[j].astype(f32)
        o_ref[...] = acc

    return pl.pallas_call(
        body, name=name,
        grid_spec=pltpu.PrefetchScalarGridSpec(
            num_scalar_prefetch=1, grid=(R // tr,),
            in_specs=[pl.BlockSpec((1, tr, Cc), lambda i, mr: (mr[0], i, 0)),
                      pl.BlockSpec((3, tr, Cc), lambda i, mr: (0, i, 0))],
            out_specs=pl.BlockSpec((tr, Cc), lambda i, mr: (i, 0))),
        out_shape=jax.ShapeDtypeStruct((R, Cc), f32),
        compiler_params=_cp(("arbitrary",)),
    )(me_arr, pf, r)


def _sum_packs(me8_arr, pack, rp):
    R = pack.shape[0]

    def body(me_ref, pk_ref, rp_ref, o_ref):
        me8 = me_ref[0]
        acc = None
        for d in range(8):
            rel = d ^ me8
            term = jnp.where(rel == 0, pk_ref[...], rp_ref[jnp.maximum(rel - 1, 0)])
            acc = term if acc is None else acc + term
        o_ref[...] = acc

    return pl.pallas_call(
        body, name="sum_packs",
        grid_spec=pltpu.PrefetchScalarGridSpec(
            num_scalar_prefetch=1, grid=(1,),
            in_specs=[pl.BlockSpec((R, 128), lambda i, mr: (0, 0)), pl.BlockSpec((7, R, 128), lambda i, mr: (0, 0, 0))],
            out_specs=pl.BlockSpec((R, 128), lambda i, mr: (0, 0))),
        out_shape=jax.ShapeDtypeStruct((R, 128), f32),
        compiler_params=_cp(("arbitrary",)),
    )(me8_arr, pack, rp)


def _swap_finished(fs):
    na = len(fs)

    def body(*refs):
        srcs, dsts = refs[:na], refs[na:2 * na]
        send_sems, recv_sems = refs[2 * na:]
        x, y, c, _ = _place()
        cps = [pltpu.make_async_remote_copy(
            src_ref=srcs[i], dst_ref=dsts[i], send_sem=send_sems.at[i], recv_sem=recv_sems.at[i],
            device_id=(x, y, 1 - c), device_id_type=MESH) for i in range(na)]
        for cp in cps:
            cp.start()
        for cp in cps:
            cp.wait()

    return pl.pallas_call(
        body, name="swap_finished",
        in_specs=[ANY] * na, out_specs=[ANY] * na,
        out_shape=[jax.ShapeDtypeStruct(f.shape, f.dtype) for f in fs],
        scratch_shapes=[pltpu.SemaphoreType.DMA((na,)), pltpu.SemaphoreType.DMA((na,))],
        compiler_params=pltpu.CompilerParams(has_side_effects=True),
    )(*fs)


def _adamw_halves(c_arr, mine, peer, w, m, v, name):
    _, R, Cc = w.shape
    tr = min(R, 256)

    def body(c_ref, mine_ref, peer_ref, w_ref, m_ref, v_ref, g_ref, d_ref, m2_ref, v2_ref):
        g = jnp.where(pl.program_id(0) == c_ref[0], mine_ref[...], peer_ref[...])
        delta, m2, v2 = _adam_math(g, w_ref[0], m_ref[0], v_ref[0])
        g_ref[0] = g
        d_ref[0] = delta
        m2_ref[0] = m2
        v2_ref[0] = v2

    half = pl.BlockSpec((tr, Cc), lambda hh, i, cr: (i, 0))
    full = pl.BlockSpec((1, tr, Cc), lambda hh, i, cr: (hh, i, 0))
    return pl.pallas_call(
        body, name=name,
        grid_spec=pltpu.PrefetchScalarGridSpec(
            num_scalar_prefetch=1, grid=(2, R // tr), in_specs=[half, half, full, full, full], out_specs=[full] * 4),
        out_shape=[jax.ShapeDtypeStruct((2, R, Cc), f32)] * 4,
        compiler_params=_cp(("arbitrary", "arbitrary")),
    )(c_arr, mine, peer, w, m, v)


def _rows8(a):
    flat = a.reshape(-1)
    n = flat.shape[0]
    rows = -(-n // 1024) * 8
    return jnp.pad(flat, (0, rows * 128 - n)).reshape(rows, 128)


def kernel(x, meta_tokens, norm_w, w_in, conv_w, hg_lb_logits, hg_norm_w, gdn_A_log, gdn_dt_bias, gdn_norm_w, w_out, final_norm_w, loss_target, m_meta_tokens, m_norm_w, m_w_in, m_conv_w, m_hg_lb_logits, m_hg_norm_w, m_gdn_A_log, m_gdn_dt_bias, m_gdn_norm_w, m_w_out, m_final_norm_w, v_meta_tokens, v_norm_w, v_w_in, v_conv_w, v_hg_lb_logits, v_hg_norm_w, v_gdn_A_log, v_gdn_dt_bias, v_gdn_norm_w, v_w_out, v_final_norm_w):
    me = 2 * lax.axis_index("x") + lax.axis_index("y")

    g_win, g_wout, g_conv, g_meta = _gather_weights(
        [w_in[0].astype(bf16).reshape(2, D // 2, SHARD_COLS), w_out[0].astype(bf16).reshape(2, D // 8, D)],
        [conv_w[0], meta_tokens])
    w_full = jnp.transpose(g_win.reshape(4, D, SHARD_COLS), (1, 0, 2)).reshape(D, IN_COLS)
    wbig = jnp.pad(w_full, ((0, 0), (0, PC - IN_COLS)))
    wout_full = g_wout.reshape(D, D)
    conv4 = jnp.transpose(g_conv, (1, 0, 2)).reshape(4, 1, 3 * HD)
    meta_full = jnp.transpose(g_meta, (1, 0, 2)).reshape(N_META, D)

    (loss8, grad_x, d_meta, d_nw, gw_full, d_conv, d_lb, d_hgw, d_alog, d_dtb, d_gdw, g_wout_part, d_fw) = _local_step(
        x, loss_target, wbig, wout_full, conv4, meta_full, norm_w, hg_lb_logits, hg_norm_w, gdn_A_log, gdn_dt_bias,
        gdn_norm_w, final_norm_w)

    g_in_blocks = jnp.transpose(gw_full.reshape(D, 4, SHARD_COLS), (1, 0, 2))
    g_out_blocks = g_wout_part.reshape(4, D // 4, D)
    pack = jnp.concatenate([
        loss8, d_nw[0].reshape(8, 128), d_lb.reshape(8, 128), d_hgw, _rows8(d_alog[0, :H]), _rows8(d_dtb[0, :H]),
        d_gdw, d_fw[0].reshape(8, 128), d_meta.reshape(128, 128), d_conv.reshape(48, 128)], axis=0)
    return _reduce_and_update(
        me, grad_x, g_in_blocks, g_out_blocks, pack, meta_tokens, norm_w, w_in, conv_w, hg_lb_logits, hg_norm_w, gdn_A_log,
        gdn_dt_bias, gdn_norm_w, w_out, final_norm_w, m_meta_tokens, m_norm_w, m_w_in, m_conv_w, m_hg_lb_logits,
        m_hg_norm_w, m_gdn_A_log, m_gdn_dt_bias, m_gdn_norm_w, m_w_out, m_final_norm_w, v_meta_tokens, v_norm_w, v_w_in,
        v_conv_w, v_hg_lb_logits, v_hg_norm_w, v_gdn_A_log, v_gdn_dt_bias, v_gdn_norm_w, v_w_out, v_final_norm_w)


def _local_step(x, loss_target, wbig, wout_full, conv4, meta_full, norm_w, hg_lb_logits, hg_norm_w, gdn_A_log, gdn_dt_bias,
                gdn_norm_w, final_norm_w):
    h3 = jnp.concatenate([jnp.zeros((NB, PAD, D), f32), jnp.broadcast_to(meta_full[None], (NB, N_META, D)), x], axis=1)
    hflat = h3.reshape(N, D)
    target = jnp.pad(loss_target, ((0, 0), (PAD + N_META, 0), (0, 0))).reshape(N, D)
    l0, l1 = hg_lb_logits[0:1], hg_lb_logits[1:2]
    alog = jnp.pad(gdn_A_log, ((0, 0), (0, DK - H)))
    dtb = jnp.pad(gdn_dt_bias, ((0, 0), (0, DK - H)))
    fw = final_norm_w.reshape(1, D)

    proj, ub = _in_proj(hflat, norm_w, wbig)
    proj3 = proj.reshape(NB, TP, PC)
    o_hg, s_hg = _hg_fwd(proj3, l0, l1)
    cv = _conv_fwd(proj3, conv4)
    o_gd, s_gd = _gd_fwd(cv, proj3, alog, dtb)
    (loss8, d_ohg, d_ogd, d_zhg, d_zgd, dh_res, g_wout_part, d_hgw, d_gdw, d_fw) = _out_loss(
        o_hg.reshape(N, HD), o_gd.reshape(N, HD), proj, hg_norm_w, gdn_norm_w, wout_full, hflat, fw, target)
    d_hg, d_l0, d_l1 = _hg_bwd(proj3, l0, l1, s_hg, d_ohg.reshape(NB, TP, HD))
    d_cv, d_ab, d_alog, d_dtb = _gd_bwd(cv, proj3, alog, dtb, s_gd, d_ogd.reshape(NB, TP, HD))
    d_qkv, d_conv4 = _conv_bwd(proj3, conv4, d_cv)
    d_hg2, d_qkv2, d_ab2 = d_hg.reshape(N, 3 * HD), d_qkv.reshape(N, 3 * HD), d_ab.reshape(N, DK)
    pieces = [(d_hg2, COL_HG), (d_zhg, COL_ZHG), (d_qkv2, COL_QKV), (d_zgd, COL_ZGD), (d_ab2, COL_AB)]
    dh, d_nw = _in_bwd(pieces, wbig, hflat, norm_w, dh_res)
    gws = [_w_grad(ub, d, "w_grad_%d" % c0) for d, c0 in pieces]
    gw_full = jnp.concatenate(gws[:4] + [gws[4][:, 0:IN_COLS - COL_AB]], axis=1)

    dh3 = dh.reshape(NB, TP, D)
    grad_x = dh3[:, PAD + N_META:, :]
    d_meta = jnp.sum(dh3[:, PAD:PAD + N_META, :], axis=0)
    d_conv = d_conv4[:, 0, :]
    d_lb = jnp.concatenate([d_l0[0:1], d_l1[0:1]], axis=0)
    return loss8, grad_x, d_meta, d_nw, gw_full, d_conv, d_lb, d_hgw, d_alog, d_dtb, d_gdw, g_wout_part, d_fw


def _reduce_and_update(me, grad_x, g_in_blocks, g_out_blocks, pack, meta_tokens, norm_w, w_in, conv_w, hg_lb_logits, hg_norm_w,
                       gdn_A_log, gdn_dt_bias, gdn_norm_w, w_out, final_norm_w, m_meta_tokens, m_norm_w, m_w_in, m_conv_w,
                       m_hg_lb_logits, m_hg_norm_w, m_gdn_A_log, m_gdn_dt_bias, m_gdn_norm_w, m_w_out, m_final_norm_w,
                       v_meta_tokens, v_norm_w, v_w_in, v_conv_w, v_hg_lb_logits, v_hg_norm_w, v_gdn_A_log, v_gdn_dt_bias,
                       v_gdn_norm_w, v_w_out, v_final_norm_w):
    c_arr = lax.axis_index("c").reshape(1).astype(jnp.int32)
    me_arr = me.reshape(1).astype(jnp.int32)
    g_in4 = g_in_blocks.reshape(4, 2, D // 2, SHARD_COLS)
    g_out4 = g_out_blocks.reshape(4, 2, D // 8, D)
    s_in, s_out = _swap_halves([g_in4, g_out4])
    pb_in, pf_in = _add_halves(c_arr, g_in4, s_in, "add_w_in")
    pb_out, pf_out = _add_halves(c_arr, g_out4, s_out, "add_w_out")
    r_in, r_out, r_pack = _scatter_blocks([pb_in, pb_out], pack)
    f_in = _sum_blocks(me_arr, pf_in, r_in, "sum_w_in")
    f_out = _sum_blocks(me_arr, pf_out, r_out, "sum_w_out")
    o_in, o_out = _swap_finished([f_in, f_out])
    me8_arr = (2 * me + lax.axis_index("c")).reshape(1).astype(jnp.int32)
    small = _sum_packs(me8_arr, pack, r_pack)

    half_in = lambda a: a[0].reshape(2, D // 2, SHARD_COLS)
    half_out = lambda a: a[0].reshape(2, D // 8, D)
    gi, di, mi, vi = [a.reshape(D, SHARD_COLS) for a in _adamw_halves(
        c_arr, f_in, o_in, half_in(w_in), half_in(m_w_in), half_in(v_w_in), "adamw_w_in")]
    go, do_, mo, vo = [a.reshape(D // 4, D) for a in _adamw_halves(
        c_arr, f_out, o_out, half_out(w_out), half_out(m_w_out), half_out(v_w_out), "adamw_w_out")]

    g_meta_full = small[64:192].reshape(N_META, D)
    g_meta_loc = lax.dynamic_slice(g_meta_full, (0, me * 256), (N_META, 256))
    gm, dm, mm_, vm = _adamw([g_meta_loc], meta_tokens, m_meta_tokens, v_meta_tokens, "adamw_meta")
    g_conv_full = small[192:240].reshape(4, 1536)
    g_conv_loc = lax.dynamic_slice(g_conv_full, (0, me * 384), (4, 384))
    gc, dc, mc, vc = _adamw([g_conv_loc], conv_w[0], m_conv_w[0], v_conv_w[0], "adamw_conv")

    reps = [(norm_w, m_norm_w, v_norm_w), (hg_lb_logits, m_hg_lb_logits, v_hg_lb_logits),
            (hg_norm_w, m_hg_norm_w, v_hg_norm_w), (gdn_A_log, m_gdn_A_log, v_gdn_A_log),
            (gdn_dt_bias, m_gdn_dt_bias, v_gdn_dt_bias), (gdn_norm_w, m_gdn_norm_w, v_gdn_norm_w),
            (final_norm_w, m_final_norm_w, v_final_norm_w)]
    wp = jnp.concatenate([_rows8(t[0]) for t in reps], axis=0)
    mp = jnp.concatenate([_rows8(t[1]) for t in reps], axis=0)
    vp = jnp.concatenate([_rows8(t[2]) for t in reps], axis=0)
    gr, dr, mr, vr = _adamw([small[8:64]], wp, mp, vp, "adamw_small")

    def unpack(p):
        outs = []
        for i, t in enumerate(reps):
            n = t[0].size
            outs.append(p[8 * i:8 * i + 8].reshape(-1)[:n].reshape(t[0].shape))
        return outs

    def leaves(meta_v, conv_v, in_v, out_v, rep_p):
        nw, lb, hgw, al, db, gdw, fwv = unpack(rep_p)
        return [meta_v, nw, in_v[None], conv_v[None], lb, hgw, al, db, gdw, out_v[None], fwv]

    loss = small[0, 0]
    return (loss, grad_x, *leaves(gm, gc, gi, go, gr), *leaves(dm, dc, di, do_, dr),
            *leaves(mm_, mc, mi, mo, mr), *leaves(vm, vc, vi, vo, vr))
```

```python
import functools

import jax
import jax.numpy as jnp
from jax import lax
from jax.experimental import pallas as pl
from jax.experimental.pallas import tpu as pltpu

f32 = jnp.float32
bf16 = jnp.bfloat16
MESH = pl.DeviceIdType.MESH
ANY = pl.BlockSpec(memory_space=pl.ANY)

D = 1024
NB = 2
N_META = 16
SEQ = 2048
PAD = 48
TP = PAD + N_META + SEQ
C = 64
NCH = TP // C
N = NB * TP
H = 4
DK = 128
HD = H * DK
PC = 4224
IN_COLS = 4104
SHARD_COLS = IN_COLS // 4
COL_HG, COL_ZHG, COL_QKV, COL_ZGD, COL_AB = 0, 3 * HD, 4 * HD, 7 * HD, 8 * HD
EPS = 1e-6
ADAM_LR, ADAM_B1, ADAM_B2, ADAM_EPS, ADAM_WD, ADAM_STEP = 0.001, 0.9, 0.999, 1e-08, 0.01, 10
VMEM_LIMIT = 56 * 1024 * 1024

P_HG = dict(lvl=1, av=1, qs=1, su=1)
P_GD = dict(kk=1, inv=1, sol=1, ws=1, qk=1, o=1, su=1)


def _cp(sem=None, **kw):
    return pltpu.CompilerParams(dimension_semantics=sem, vmem_limit_bytes=VMEM_LIMIT, **kw)


_DIMS = {"nn": (((1,), (0,)), ((), ())), "nt": (((1,), (1,)), ((), ())), "tn": (((0,), (0,)), ((), ()))}


def _split(x):
    hi = x.astype(bf16)
    return hi, (x - hi.astype(f32)).astype(bf16)


def _dg(a, b, kind, passes):
    d = lambda x, y: lax.dot_general(x, y, _DIMS[kind], preferred_element_type=f32)
    if passes == 1:
        return d(a.astype(bf16), b.astype(bf16))
    ah, al = _split(a)
    bh, bl = _split(b)
    return d(ah, bh) + d(ah, bl) + d(al, bh)


@functools.partial(jax.custom_vjp, nondiff_argnums=(2, 3))
def mmx(a, b, kind, passes):
    return _dg(a, b, kind, passes)


def _mmx_fwd(a, b, kind, passes):
    return _dg(a, b, kind, passes), (a, b)


def _mmx_bwd(kind, passes, res, g):
    a, b = res
    if kind == "nn":
        return _dg(g, b, "nt", passes), _dg(a, g, "tn", passes)
    if kind == "nt":
        return _dg(g, b, "nn", passes), _dg(g, a, "tn", passes)
    return _dg(b, g, "nt", passes), _dg(a, g, "nn", passes)


mmx.defvjp(_mmx_fwd, _mmx_bwd)


def _mask_dg(mask, x, kind):
    n = x.shape[1]
    xh, xl = _split(x)
    r = lax.dot_general(mask, jnp.concatenate([xh, xl], axis=1), _DIMS[kind], preferred_element_type=f32)
    return r[:, :n] + r[:, n:]


@jax.custom_vjp
def mask_mm(mask, x):
    return _mask_dg(mask, x, "nn")


def _mask_fwd(mask, x):
    return _mask_dg(mask, x, "nn"), mask


def _mask_bwd(mask, g):
    return None, _mask_dg(mask, g, "tn")


mask_mm.defvjp(_mask_fwd, _mask_bwd)


def bdot(a, b):
    return jnp.dot(a.astype(bf16), b.astype(bf16), preferred_element_type=f32)


def bdot_nt(a, b):
    return lax.dot_general(a.astype(bf16), b.astype(bf16), _DIMS["nt"], preferred_element_type=f32)


def bdot_tn(a, b):
    return lax.dot_general(a.astype(bf16), b.astype(bf16), _DIMS["tn"], preferred_element_type=f32)


def _iota2(n, m):
    return lax.broadcasted_iota(jnp.int32, (n, m), 0), lax.broadcasted_iota(jnp.int32, (n, m), 1)


def silu(x):
    return x * jax.nn.sigmoid(x)


def softplus(x):
    return jnp.maximum(x, 0.0) + jnp.log(1.0 + jnp.exp(-jnp.abs(x)))


def rmsnorm(x, w):
    return x * lax.rsqrt(jnp.mean(x * x, axis=-1, keepdims=True) + EPS) * w


def hg_masks():
    t, r = _iota2(C, C)
    mats = [r <= t, r > t]
    lvl = []
    for l in range(1, 7):
        sz = 1 << l
        half = sz >> 1
        seg_t = t >> l
        upper_t = (t & (sz - 1)) >= half
        mid_t = seg_t * sz + half - 1
        mats.append((upper_t & (r > mid_t) & (r <= t)) | ((~upper_t) & (r > t) & (r <= mid_t)))
        lvl.append(((seg_t == (r >> l)) & upper_t & ((r & (sz - 1)) < half)).astype(f32))
    stk = jnp.concatenate([m.astype(bf16) for m in mats], axis=0)
    return stk, lvl, (t == r).astype(f32)


def _head(a, h):
    return a[:, h * DK:(h + 1) * DK]


def hg_chunk(St, ps, l0, l1):
    m = jnp.maximum(l0, l1)
    e0 = jnp.exp(l0 - m)
    e1 = jnp.exp(l1 - m)
    lb = e0 / (e0 + e1)
    stk, lvl, eye = hg_masks()
    msk = [eye] + lvl
    qs, ks, vs, qG, kR, eGl = [], [], [], [], [], []
    for p in ps:
        pq, pf, v = p[:, 0:HD], p[:, HD:2 * HD], p[:, 2 * HD:3 * HD]
        q = silu(pq)
        f = lb + (1.0 - lb) * jax.nn.sigmoid(pf)
        k = 1.0 - f
        logf = jnp.log(f)
        Dm = mask_mm(stk, logf)
        ex = [jnp.exp(Dm[(2 + i) * C:(3 + i) * C]) for i in range(6)]
        qs.append([q] + [q * e for e in ex])
        ks.append([k] + [k * e for e in ex])
        vs.append(v)
        qG.append(q * jnp.exp(Dm[0:C]))
        kR.append(k * jnp.exp(Dm[C:2 * C]))
        eGl.append(jnp.exp(jnp.sum(logf, axis=0, keepdims=True)))
    units = [(b, h) for b in range(len(ps)) for h in range(H)]
    parts = [[msk[i] * mmx(_head(qs[b][i], h), _head(ks[b][i], h), "nt", P_HG["lvl"]) for b, h in units]
             for i in range(7)]
    A = [functools.reduce(lambda x, y: x + y, [parts[i][n] for i in range(7)]) for n in range(len(units))]
    qS = [mmx(_head(qG[b], h), St[n], "nt", P_HG["qs"]) for n, (b, h) in enumerate(units)]
    Sn = [St[n] * _head(eGl[b], h) + mmx(_head(vs[b], h), _head(kR[b], h), "tn", P_HG["su"])
          for n, (b, h) in enumerate(units)]
    outs = [mmx(A[n], _head(vs[b], h), "nn", P_HG["av"]) + qS[n] for n, (b, h) in enumerate(units)]
    return tuple(Sn), tuple(jnp.concatenate(outs[b * H:(b + 1) * H], axis=1) for b in range(len(ps)))


def gd_chunk(S, cs, abs_, alog, dtb):
    t, r = _iota2(C, C)
    tri = (r <= t).astype(bf16)
    ups = (r > t).astype(bf16)
    lane = lax.broadcasted_iota(jnp.int32, (1, DK), 1)
    subl = lax.broadcasted_iota(jnp.int32, (8, 1), 0)
    eye = (t == r).astype(f32)
    strict = (r < t).astype(f32)
    bd = ((t >> 4) == (r >> 4)).astype(f32)
    qa, ka, va, b4, gam4, grev4, gam4T, glast4 = [], [], [], [], [], [], [], []
    for c, ab in zip(cs, abs_):
        qa.append(silu(c[:, 0:HD]))
        ka.append(silu(c[:, HD:2 * HD]))
        va.append(silu(c[:, 2 * HD:3 * HD]))
        g4 = -jnp.exp(alog) * softplus(ab + dtb)
        b4.append(jax.nn.sigmoid(ab))
        gam4.append(mask_mm(tri, g4))
        grev4.append(mask_mm(ups, g4))
        gam4T.append(gam4[-1].T)
        glast4.append(jnp.sum(g4, axis=0, keepdims=True))
    units = [(b, h) for b in range(len(cs)) for h in range(H)]
    nu = range(len(units))
    inv = lambda a, b: [mmx(a[n], b[n], "nn", P_GD["inv"]) for n in nu]
    v = [_head(va[b], h) for b, h in units]
    q = [_head(qa[b], h) for b, h in units]
    k = [_head(ka[b], h) for b, h in units]
    q = [x * lax.rsqrt(jnp.sum(x * x, -1, keepdims=True) + EPS) * (DK ** -0.5) for x in q]
    k = [x * lax.rsqrt(jnp.sum(x * x, -1, keepdims=True) + EPS) for x in k]
    oh = [(lane == h).astype(f32) for h in range(H)]
    gam_c = [jnp.sum(gam4[b] * oh[h], -1, keepdims=True) for b, h in units]
    grev_c = [jnp.sum(grev4[b] * oh[h], -1, keepdims=True) for b, h in units]
    beta = [jnp.sum(b4[b] * (lane == h + H).astype(f32), -1, keepdims=True) for b, h in units]
    glast = [jnp.sum(glast4[b] * oh[h], -1, keepdims=True) for b, h in units]
    gam_r = [jnp.sum(gam4T[b][0:8, :] * (subl == h).astype(f32), axis=0, keepdims=True) for b, h in units]
    dec = [jnp.exp(jnp.where(r <= t, gam_c[n] - gam_r[n], -1e30)) for n in nu]
    egam = [jnp.exp(gam_c[n]) for n in nu]
    kk = [mmx(k[n], k[n], "nt", P_GD["kk"]) for n in nu]
    qk = [mmx(q[n], k[n], "nt", P_GD["qk"]) * dec[n] for n in nu]
    A = [beta[n] * kk[n] * dec[n] * strict for n in nu]
    Dg = [A[n] * bd for n in nu]
    L = [A[n] - Dg[n] for n in nu]
    ImD = [eye - Dg[n] for n in nu]
    D2 = inv(Dg, Dg)
    P1 = inv(ImD, [eye + x for x in D2])
    D4 = inv(D2, D2)
    P2 = inv(P1, [eye + x for x in D4])
    D8 = inv(D4, D4)
    M = inv(P2, [eye + x for x in D8])
    Nn = inv(M, L)
    N2 = inv(Nn, Nn)
    T1 = inv([eye - x for x in Nn], [eye + x for x in N2])
    Tinv = inv(T1, M)
    rhs = [jnp.concatenate([beta[n] * v[n], (beta[n] * egam[n]) * k[n]], axis=1) for n in nu]
    sol = [mmx(Tinv[n], rhs[n], "nn", P_GD["sol"]) for n in nu]
    qwS = [mmx(jnp.concatenate([q[n] * egam[n], sol[n][:, DK:2 * DK]], axis=0), S[n], "nn", P_GD["ws"]) for n in nu]
    u = [sol[n][:, 0:DK] - qwS[n][C:2 * C] for n in nu]
    outs = [qwS[n][0:C] + mmx(qk[n], u[n], "nn", P_GD["o"]) for n in nu]
    Sn = [jnp.exp(glast[n]) * S[n] + mmx(k[n] * jnp.exp(grev_c[n]), u[n], "tn", P_GD["su"]) for n in nu]
    return tuple(Sn), tuple(jnp.concatenate(outs[b * H:(b + 1) * H], axis=1) for b in range(len(cs)))


def _in_proj(hflat, norm_w, wbig):
    tm = 384

    def body(h_ref, nw_ref, w_ref, p_ref, ut_ref):
        u = rmsnorm(h_ref[...], nw_ref[...])
        ut_ref[...] = u.T.astype(bf16)
        p_ref[...] = jnp.dot(u.astype(bf16), w_ref[...], preferred_element_type=f32)

    return pl.pallas_call(
        body, name="in_proj", grid=(N // tm,),
        in_specs=[pl.BlockSpec((tm, D), lambda i: (i, 0)), pl.BlockSpec((1, D), lambda i: (0, 0)),
                  pl.BlockSpec((D, PC), lambda i: (0, 0))],
        out_specs=[pl.BlockSpec((tm, PC), lambda i: (i, 0)), pl.BlockSpec((D, tm), lambda i: (0, i))],
        out_shape=[jax.ShapeDtypeStruct((N, PC), f32), jax.ShapeDtypeStruct((D, N), bf16)],
        compiler_params=_cp(("arbitrary",)),
    )(hflat, norm_w, wbig)


NU = NB * H
_REV = lambda c: NCH - 1 - c
_FWD = lambda c: c


def _tok_spec(w, ix, col=0):
    return pl.BlockSpec((NB, C, w), lambda c: (0, ix(c), col))


def _state_spec(ix):
    return pl.BlockSpec((NB, 1, H, DK, DK), lambda c: (0, ix(c), 0, 0, 0))


def _row_spec(w):
    return pl.BlockSpec((1, w), lambda c: (0, 0))


def _rows(ref):
    return tuple(ref[b] for b in range(NB))


def _scan_fwd(chunk_fn, name, x3, extra, extra_specs, n_tok):
    ne = len(extra)

    def body(*refs):
        x_ref = refs[0]
        e_refs = refs[1:1 + ne]
        o_ref, s_ref, st = refs[1 + ne:]

        @pl.when(pl.program_id(0) == 0)
        def _():
            st[...] = jnp.zeros_like(st)

        S = tuple(st[n] for n in range(NU))
        for n in range(NU):
            s_ref[n // H, 0, n % H] = S[n]
        args = [_rows(e) if i < n_tok else e[...] for i, e in enumerate(e_refs)]
        Sn, o = chunk_fn(S, _rows(x_ref), *args)
        for n in range(NU):
            st[n] = Sn[n]
        for b in range(NB):
            o_ref[b] = o[b]

    return pl.pallas_call(
        body, name=name, grid=(NCH,),
        in_specs=[_tok_spec(3 * HD, _FWD)] + extra_specs(_FWD),
        out_specs=[_tok_spec(HD, _FWD), _state_spec(_FWD)],
        out_shape=[jax.ShapeDtypeStruct((NB, TP, HD), f32), jax.ShapeDtypeStruct((NB, NCH, H, DK, DK), f32)],
        scratch_shapes=[pltpu.VMEM((NU, DK, DK), f32)],
        compiler_params=_cp(("arbitrary",)),
    )(x3, *extra)


def _hg_extra_specs(ix):
    return [_row_spec(HD), _row_spec(HD)]


def _gd_extra_specs(ix):
    return [_tok_spec(DK, ix, COL_AB // DK), _row_spec(DK), _row_spec(DK)]


def _hg_fwd(proj3, l0, l1):
    return _scan_fwd(hg_chunk, "hg_fwd", proj3, [l0, l1], _hg_extra_specs, 0)


def _gd_fwd(cv, proj3, alog, dtb):
    return _scan_fwd(gd_chunk, "gd_fwd", cv, [proj3, alog, dtb], _gd_extra_specs, 1)


def _hg_bwd(proj3, l0, l1, s_saved, do):
    def body(p_ref, l0_ref, l1_ref, s_ref, do_ref, dp_ref, dl0_ref, dl1_ref, dst):
        @pl.when(pl.program_id(0) == 0)
        def _():
            dst[...] = jnp.zeros_like(dst)
            dl0_ref[...] = jnp.zeros_like(dl0_ref)
            dl1_ref[...] = jnp.zeros_like(dl1_ref)

        S = tuple(s_ref[n // H, 0, n % H] for n in range(NU))
        _, vjp = jax.vjp(hg_chunk, S, _rows(p_ref), l0_ref[...], l1_ref[...])
        dS, dp, dl0, dl1 = vjp((tuple(dst[n] for n in range(NU)), _rows(do_ref)))
        for n in range(NU):
            dst[n] = dS[n]
        for b in range(NB):
            dp_ref[b] = dp[b].astype(bf16)
        dl0_ref[...] += jnp.broadcast_to(dl0, (8, HD))
        dl1_ref[...] += jnp.broadcast_to(dl1, (8, HD))

    acc = pl.BlockSpec((8, HD), lambda c: (0, 0))
    return pl.pallas_call(
        body, name="hg_bwd", grid=(NCH,),
        in_specs=[_tok_spec(3 * HD, _REV)] + _hg_extra_specs(_REV) + [_state_spec(_REV), _tok_spec(HD, _REV)],
        out_specs=[_tok_spec(3 * HD, _REV), acc, acc],
        out_shape=[jax.ShapeDtypeStruct((NB, TP, 3 * HD), bf16), jax.ShapeDtypeStruct((8, HD), f32),
                   jax.ShapeDtypeStruct((8, HD), f32)],
        scratch_shapes=[pltpu.VMEM((NU, DK, DK), f32)],
        compiler_params=_cp(("arbitrary",)),
    )(proj3, l0, l1, s_saved, do)


def _gd_bwd(cv, proj3, alog, dtb, s_saved, do):
    def body(c_ref, ab_ref, al_ref, db_ref, s_ref, do_ref, dc_ref, dab_ref, dal_ref, ddb_ref, dst):
        @pl.when(pl.program_id(0) == 0)
        def _():
            dst[...] = jnp.zeros_like(dst)
            dal_ref[...] = jnp.zeros_like(dal_ref)
            ddb_ref[...] = jnp.zeros_like(ddb_ref)

        S = tuple(s_ref[n // H, 0, n % H] for n in range(NU))
        _, vjp = jax.vjp(gd_chunk, S, _rows(c_ref), _rows(ab_ref), al_ref[...], db_ref[...])
        dS, dc, dab, dal, ddb = vjp((tuple(dst[n] for n in range(NU)), _rows(do_ref)))
        for n in range(NU):
            dst[n] = dS[n]
        for b in range(NB):
            dc_ref[b] = dc[b]
            dab_ref[b] = dab[b].astype(bf16)
        dal_ref[...] += jnp.broadcast_to(dal, (8, DK))
        ddb_ref[...] += jnp.broadcast_to(ddb, (8, DK))

    acc = pl.BlockSpec((8, DK), lambda c: (0, 0))
    return pl.pallas_call(
        body, name="gd_bwd", grid=(NCH,),
        in_specs=[_tok_spec(3 * HD, _REV)] + _gd_extra_specs(_REV) + [_state_spec(_REV), _tok_spec(HD, _REV)],
        out_specs=[_tok_spec(3 * HD, _REV), _tok_spec(DK, _REV), acc, acc],
        out_shape=[jax.ShapeDtypeStruct((NB, TP, 3 * HD), f32), jax.ShapeDtypeStruct((NB, TP, DK), bf16),
                   jax.ShapeDtypeStruct((8, DK), f32), jax.ShapeDtypeStruct((8, DK), f32)],
        scratch_shapes=[pltpu.VMEM((NU, DK, DK), f32)],
        compiler_params=_cp(("arbitrary",)),
    )(cv, proj3, alog, dtb, s_saved, do)


def _conv_fwd(proj3, conv4):
    def body(x_ref, w_ref, y_ref):
        x = x_ref[0]
        row = lax.broadcasted_iota(jnp.int32, (TP, 1), 0)
        y = w_ref[3] * x
        for s in (1, 2, 3):
            y = y + w_ref[3 - s] * jnp.where(row >= s, pltpu.roll(x, s, 0), 0.0)
        y_ref[0] = y

    return pl.pallas_call(
        body, name="conv_fwd", grid=(NB, 3),
        in_specs=[pl.BlockSpec((1, TP, HD), lambda b, j: (b, 0, COL_QKV // HD + j)),
                  pl.BlockSpec((4, 1, HD), lambda b, j: (0, 0, j))],
        out_specs=pl.BlockSpec((1, TP, HD), lambda b, j: (b, 0, j)),
        out_shape=jax.ShapeDtypeStruct((NB, TP, 3 * HD), f32),
        compiler_params=_cp(("arbitrary", "arbitrary")),
    )(proj3, conv4)


def _conv_bwd(proj3, conv4, dy):
    def body(x_ref, w_ref, dy_ref, dx_ref, dw_ref):
        @pl.when(pl.program_id(1) == 0)
        def _():
            dw_ref[...] = jnp.zeros_like(dw_ref)

        x = x_ref[0]
        g = dy_ref[0]
        row = lax.broadcasted_iota(jnp.int32, (TP, 1), 0)
        dx = w_ref[3] * g
        dw_ref[3] += jnp.broadcast_to(jnp.sum(x * g, axis=0, keepdims=True), (8, HD))
        for s in (1, 2, 3):
            dx = dx + w_ref[3 - s] * jnp.where(row < TP - s, pltpu.roll(g, TP - s, 0), 0.0)
            xs = jnp.where(row >= s, pltpu.roll(x, s, 0), 0.0)
            dw_ref[3 - s] += jnp.broadcast_to(jnp.sum(xs * g, axis=0, keepdims=True), (8, HD))
        dx_ref[0] = dx.astype(bf16)

    return pl.pallas_call(
        body, name="conv_bwd", grid=(3, NB),
        in_specs=[pl.BlockSpec((1, TP, HD), lambda j, b: (b, 0, COL_QKV // HD + j)),
                  pl.BlockSpec((4, 1, HD), lambda j, b: (0, 0, j)), pl.BlockSpec((1, TP, HD), lambda j, b: (b, 0, j))],
        out_specs=[pl.BlockSpec((1, TP, HD), lambda j, b: (b, 0, j)), pl.BlockSpec((4, 8, HD), lambda j, b: (0, 0, j))],
        out_shape=[jax.ShapeDtypeStruct((NB, TP, 3 * HD), bf16), jax.ShapeDtypeStruct((4, 8, 3 * HD), f32)],
        compiler_params=_cp(("arbitrary", "arbitrary")),
    )(proj3, conv4, dy)


def _gated(o, z, w):
    outs = []
    for hh in range(H):
        sl = slice(hh * DK, (hh + 1) * DK)
        outs.append(rmsnorm(o[:, sl], w) * silu(z[:, sl]))
    return jnp.concatenate(outs, axis=-1)


def _out_loss(o_hg, o_gd, proj, hgw, gdw, wout, hflat, fw, target):
    tm = 384

    def body(ohg_ref, ogd_ref, zhg_ref, zgd_ref, hgw_ref, gdw_ref, wo_ref, h_ref, fw_ref, tg_ref,
             loss_ref, dohg_ref, dogd_ref, dzhg_ref, dzgd_ref, dh_ref, dwo_ref, dhgw_ref, dgdw_ref, dfw_ref):
        i = pl.program_id(0)

        @pl.when(i == 0)
        def _():
            for r in (loss_ref, dwo_ref, dhgw_ref, dgdw_ref, dfw_ref):
                r[...] = jnp.zeros_like(r)

        row = i * tm + lax.broadcasted_iota(jnp.int32, (tm, 1), 0)
        tok = jnp.where(row >= TP, row - TP, row)
        valid = (tok >= PAD + N_META).astype(f32)
        hval = h_ref[...]
        tgt = tg_ref[...]

        def mix(ohg, ogd, zhg, zgd, w1, w2):
            return jnp.concatenate([_gated(ohg, zhg, w1), _gated(ogd, zgd, w2)], axis=-1)

        y, vjp_mix = jax.vjp(mix, ohg_ref[...], ogd_ref[...], zhg_ref[...], zgd_ref[...], hgw_ref[...], gdw_ref[...])
        out = bdot(y, wo_ref[...])

        def head(out, fwv):
            err = (rmsnorm(hval + out, fwv) - tgt) * valid
            return 0.5 * jnp.sum(jnp.mean(err * err, axis=-1, keepdims=True))

        loss, vjp_head = jax.vjp(head, out, fw_ref[...])
        dout, dfw = vjp_head(jnp.ones((), f32))
        dh_ref[...] = dout
        dy = bdot_nt(dout, wo_ref[...])
        dwo_ref[...] += bdot_tn(y, dout)
        dohg, dogd, dzhg, dzgd, dw1, dw2 = vjp_mix(dy)
        dohg_ref[...] = dohg
        dogd_ref[...] = dogd
        dzhg_ref[...] = dzhg.astype(bf16)
        dzgd_ref[...] = dzgd.astype(bf16)
        loss_ref[...] += jnp.broadcast_to(loss, (8, DK))
        dhgw_ref[...] += jnp.broadcast_to(dw1, (8, DK))
        dgdw_ref[...] += jnp.broadcast_to(dw2, (8, DK))
        dfw_ref[...] += jnp.broadcast_to(dfw, (8, D))

    row = lambda w: pl.BlockSpec((tm, w), lambda i: (i, 0))
    whole = lambda r, w: pl.BlockSpec((r, w), lambda i: (0, 0))
    col = lambda c0: pl.BlockSpec((tm, HD), lambda i: (i, c0 // HD))
    return pl.pallas_call(
        body, name="out_loss", grid=(N // tm,),
        in_specs=[row(HD), row(HD), col(COL_ZHG), col(COL_ZGD),
                  whole(1, DK), whole(1, DK), whole(D, D), row(D), whole(1, D), row(D)],
        out_specs=[whole(8, DK), row(HD), row(HD), row(HD), row(HD), row(D), whole(D, D),
                   whole(8, DK), whole(8, DK), whole(8, D)],
        out_shape=[jax.ShapeDtypeStruct((8, DK), f32)] + [jax.ShapeDtypeStruct((N, HD), f32)] * 2
        + [jax.ShapeDtypeStruct((N, HD), bf16)] * 2
        + [jax.ShapeDtypeStruct((N, D), f32), jax.ShapeDtypeStruct((D, D), f32),
           jax.ShapeDtypeStruct((8, DK), f32), jax.ShapeDtypeStruct((8, DK), f32), jax.ShapeDtypeStruct((8, D), f32)],
        compiler_params=_cp(("arbitrary",)),
    )(o_hg, o_gd, proj, proj, hgw, gdw, wout, hflat, fw, target)


def _in_bwd(pieces, wbig, hflat, norm_w, dh_res):
    tm = 384
    np_ = len(pieces)
    offs = [c0 for _, c0 in pieces]
    widths = [d.shape[1] for d, _ in pieces]

    def body(*refs):
        d_refs = refs[:np_]
        w_ref, h_ref, nw_ref, dhr_ref, dh_ref, dnw_ref = refs[np_:]

        @pl.when(pl.program_id(0) == 0)
        def _():
            dnw_ref[...] = jnp.zeros_like(dnw_ref)

        du = jnp.zeros((tm, D), f32)
        for d_ref, off, wd in zip(d_refs, offs, widths):
            du = du + bdot_nt(d_ref[...], w_ref[:, off:off + wd])
        _, vjp = jax.vjp(rmsnorm, h_ref[...], nw_ref[...])
        dh, dnw = vjp(du)
        dh_ref[...] = dh + dhr_ref[...]
        dnw_ref[...] += jnp.broadcast_to(dnw, (8, D))

    row = lambda w: pl.BlockSpec((tm, w), lambda i: (i, 0))
    return pl.pallas_call(
        body, name="in_bwd", grid=(N // tm,),
        in_specs=[row(w) for w in widths]
        + [pl.BlockSpec((D, PC), lambda i: (0, 0)), row(D), pl.BlockSpec((1, D), lambda i: (0, 0)), row(D)],
        out_specs=[row(D), pl.BlockSpec((8, D), lambda i: (0, 0))],
        out_shape=[jax.ShapeDtypeStruct((N, D), f32), jax.ShapeDtypeStruct((8, D), f32)],
        compiler_params=_cp(("arbitrary",)),
    )(*[d for d, _ in pieces], wbig, hflat, norm_w, dh_res)


def _w_grad(ut, d, name):
    n = d.shape[1]
    tk = N // 3

    def body(u_ref, d_ref, o_ref):
        @pl.when(pl.program_id(0) == 0)
        def _():
            o_ref[...] = jnp.zeros_like(o_ref)

        o_ref[...] += jnp.dot(u_ref[...], d_ref[...], preferred_element_type=f32)

    return pl.pallas_call(
        body, name=name, grid=(N // tk,),
        in_specs=[pl.BlockSpec((D, tk), lambda k: (0, k)), pl.BlockSpec((tk, n), lambda k: (k, 0))],
        out_specs=pl.BlockSpec((D, n), lambda k: (0, 0)),
        out_shape=jax.ShapeDtypeStruct((D, n), f32),
        compiler_params=_cp(("arbitrary",)),
    )(ut, d)


def _adam_math(g, w, m, v):
    m2 = ADAM_B1 * m + (1.0 - ADAM_B1) * g
    v2 = ADAM_B2 * v + (1.0 - ADAM_B2) * (g * g)
    m_hat = m2 / (1.0 - ADAM_B1 ** ADAM_STEP)
    v_hat = v2 / (1.0 - ADAM_B2 ** ADAM_STEP)
    delta = -ADAM_LR * (m_hat / (jnp.sqrt(v_hat) + ADAM_EPS) + ADAM_WD * w)
    return delta, m2, v2


def _adamw(gs, w, m, v, name):
    R, Cc = w.shape
    tr = 256 if R % 256 == 0 else R
    ng = len(gs)

    def body(*refs):
        g = refs[0][...]
        for r in refs[1:ng]:
            g = g + r[...]
        w_ref, m_ref, v_ref, g_ref, d_ref, m2_ref, v2_ref = refs[ng:]
        delta, m2, v2 = _adam_math(g, w_ref[...], m_ref[...], v_ref[...])
        g_ref[...] = g
        d_ref[...] = delta
        m2_ref[...] = m2
        v2_ref[...] = v2

    spec = pl.BlockSpec((tr, Cc), lambda i: (i, 0))
    return pl.pallas_call(
        body, name=name, grid=(R // tr,),
        in_specs=[spec] * (ng + 3), out_specs=[spec] * 4,
        out_shape=[jax.ShapeDtypeStruct((R, Cc), f32)] * 4,
        compiler_params=_cp(("arbitrary",)),
    )(*gs, w, m, v)


def _sum_slots(r, name):
    S, R, Cc = r.shape
    tr = 256 if R % 256 == 0 else R

    def body(r_ref, o_ref):
        acc = r_ref[0]
        for s in range(1, S):
            acc = acc + r_ref[s]
        o_ref[...] = acc

    return pl.pallas_call(
        body, name=name, grid=(R // tr,),
        in_specs=[pl.BlockSpec((S, tr, Cc), lambda i: (0, i, 0))], out_specs=pl.BlockSpec((tr, Cc), lambda i: (i, 0)),
        out_shape=jax.ShapeDtypeStruct((R, Cc), f32),
        compiler_params=_cp(("arbitrary",)),
    )(r)


def _place():
    x, y, c = lax.axis_index("x"), lax.axis_index("y"), lax.axis_index("c")
    return x, y, c, [(1 - x, y), (x, 1 - y), (1 - x, 1 - y)]


def _gather_weights(halved, whole):
    nh, nw = len(halved), len(whole)
    na = nh + nw

    def body(*refs):
        srcs, dsts = refs[:na], refs[na:2 * na]
        send_sems, recv_sems, loc_sems = refs[2 * na:2 * na + 3]
        stage = refs[2 * na + 3:]
        x, y, c, chips = _place()
        me = 2 * x + y
        loads = [pltpu.make_async_copy(s, v, loc_sems.at[i]) for i, (s, v) in enumerate(zip(srcs, stage))]
        locs = [pltpu.make_async_copy(v, d.at[me], loc_sems.at[i]) for i, (v, d) in enumerate(zip(stage, dsts))]
        for cp in loads:
            cp.start()

        def ici(j, i, slot):
            px, py = chips[j]
            src = srcs[i].at[c] if i < nh else srcs[i]
            dst = dsts[i].at[slot, c] if i < nh else dsts[i].at[slot]
            return pltpu.make_async_remote_copy(
                src_ref=src, dst_ref=dst, send_sem=send_sems.at[na * j + i], recv_sem=recv_sems.at[na * j + i],
                device_id=(px, py, c), device_id_type=MESH)

        def d2d(j, i, half):
            px, py = chips[j]
            blk = dsts[i].at[2 * px + py, half]
            return pltpu.make_async_remote_copy(
                src_ref=blk, dst_ref=blk, send_sem=send_sems.at[3 * na + nh * j + i],
                recv_sem=recv_sems.at[3 * na + nh * j + i], device_id=(x, y, 1 - c), device_id_type=MESH)

        sends = [ici(j, i, me) for j in range(3) for i in range(na)]
        for cp in sends:
            cp.start()
        for ld, st in zip(loads, locs):
            ld.wait()
            st.start()
        for j, (px, py) in enumerate(chips):
            for i in range(na):
                ici(j, i, 2 * px + py).wait_recv()
                if i < nh:
                    fwd = d2d(j, i, c)
                    fwd.start()
                    sends.append(fwd)
        for j in range(3):
            for i in range(nh):
                d2d(j, i, 1 - c).wait_recv()
        for cp in sends:
            cp.wait_send()
        for cp in locs:
            cp.wait()

    nsem = 3 * na + 3 * nh
    return pl.pallas_call(
        body, name="gather_weights",
        in_specs=[ANY] * na, out_specs=[ANY] * na,
        out_shape=[jax.ShapeDtypeStruct((4,) + s.shape, s.dtype) for s in list(halved) + list(whole)],
        scratch_shapes=[pltpu.SemaphoreType.DMA((nsem,)), pltpu.SemaphoreType.DMA((nsem,)),
                        pltpu.SemaphoreType.DMA((na,))] + [pltpu.VMEM(s.shape, s.dtype) for s in list(halved) + list(whole)],
        compiler_params=pltpu.CompilerParams(has_side_effects=True, vmem_limit_bytes=VMEM_LIMIT),
    )(*halved, *whole)


def _swap_halves(gs):
    na = len(gs)

    def body(*refs):
        srcs, dsts = refs[:na], refs[na:2 * na]
        send_sems, recv_sems = refs[2 * na:]
        x, y, c, _ = _place()
        cps = [pltpu.make_async_remote_copy(
            src_ref=srcs[i].at[q, 1 - c], dst_ref=dsts[i].at[q], send_sem=send_sems.at[4 * i + q],
            recv_sem=recv_sems.at[4 * i + q], device_id=(x, y, 1 - c), device_id_type=MESH)
            for i in range(na) for q in range(4)]
        for cp in cps:
            cp.start()
        for cp in cps:
            cp.wait()

    return pl.pallas_call(
        body, name="swap_halves",
        in_specs=[ANY] * na, out_specs=[ANY] * na,
        out_shape=[jax.ShapeDtypeStruct((4,) + g.shape[2:], g.dtype) for g in gs],
        scratch_shapes=[pltpu.SemaphoreType.DMA((4 * na,)), pltpu.SemaphoreType.DMA((4 * na,))],
        compiler_params=pltpu.CompilerParams(has_side_effects=True),
    )(*gs)


def _add_halves(c_arr, g, s, name):
    _, _, R, Cc = g.shape
    tr = min(R, 256)

    def body(c_ref, g_ref, s_ref, b_ref, f_ref):
        p = g_ref[0, 0] + s_ref[0]
        f_ref[0] = p
        b_ref[0] = p.astype(bf16)

    blk = pl.BlockSpec((1, tr, Cc), lambda q, i, cr: (q, i, 0))
    return pl.pallas_call(
        body, name=name,
        grid_spec=pltpu.PrefetchScalarGridSpec(
            num_scalar_prefetch=1, grid=(4, R // tr),
            in_specs=[pl.BlockSpec((1, 1, tr, Cc), lambda q, i, cr: (q, cr[0], i, 0)), blk], out_specs=[blk, blk]),
        out_shape=[jax.ShapeDtypeStruct((4, R, Cc), bf16), jax.ShapeDtypeStruct((4, R, Cc), f32)],
        compiler_params=_cp(("arbitrary", "arbitrary")),
    )(c_arr, g, s)


_FLIPS = [(fx, fy, fc) for fx in (0, 1) for fy in (0, 1) for fc in (0, 1)][1:]


def _scatter_blocks(pbs, pack):
    na = len(pbs)
    R = pack.shape[0]

    def body(*refs):
        srcs, pk = refs[:na], refs[na]
        dsts, rp = refs[na + 1:2 * na + 1], refs[2 * na + 1]
        send_sems, recv_sems = refs[2 * na + 2:]
        x, y, c, chips = _place()

        def big(j, i):
            px, py = chips[j]
            return pltpu.make_async_remote_copy(
                src_ref=srcs[i].at[2 * px + py], dst_ref=dsts[i].at[j], send_sem=send_sems.at[na * j + i],
                recv_sem=recv_sems.at[na * j + i], device_id=(px, py, c), device_id_type=MESH)

        def small(k):
            fx, fy, fc = _FLIPS[k]
            return pltpu.make_async_remote_copy(
                src_ref=pk, dst_ref=rp.at[k], send_sem=send_sems.at[3 * na + k], recv_sem=recv_sems.at[3 * na + k],
                device_id=(x ^ fx, y ^ fy, c ^ fc), device_id_type=MESH)

        cps = [big(j, i) for j in range(3) for i in range(na)] + [small(k) for k in range(7)]
        for cp in cps:
            cp.start()
        for cp in cps:
            cp.wait()

    nsem = 3 * na + 7
    return pl.pallas_call(
        body, name="scatter_blocks",
        in_specs=[ANY] * (na + 1), out_specs=[ANY] * (na + 1),
        out_shape=[jax.ShapeDtypeStruct((3,) + p.shape[1:], p.dtype) for p in pbs]
        + [jax.ShapeDtypeStruct((7, R, 128), f32)],
        scratch_shapes=[pltpu.SemaphoreType.DMA((nsem,)), pltpu.SemaphoreType.DMA((nsem,))],
        compiler_params=pltpu.CompilerParams(has_side_effects=True),
    )(*pbs, pack)


def _sum_blocks(me_arr, pf, r, name):
    _, R, Cc = pf.shape
    tr = min(R, 256)

    def body(me_ref, pf_ref, r_ref, o_ref):
        acc = pf_ref[0]
        for j in range(3):
            acc = acc + r_ref[j].astype(f32)
        o_ref[...] = acc

    return pl.pallas_call(
        body, name=name,
        grid_spec=pltpu.PrefetchScalarGridSpec(
            num_scalar_prefetch=1, grid=(R // tr,),
            in_specs=[pl.BlockSpec((1, tr, Cc), lambda i, mr: (mr[0], i, 0)),
                      pl.BlockSpec((3, tr, Cc), lambda i, mr: (0, i, 0))],
            out_specs=pl.BlockSpec((tr, Cc), lambda i, mr: (i, 0))),
        out_shape=jax.ShapeDtypeStruct((R, Cc), f32),
        compiler_params=_cp(("arbitrary",)),
    )(me_arr, pf, r)


def _sum_packs(me8_arr, pack, rp):
    R = pack.shape[0]

    def body(me_ref, pk_ref, rp_ref, o_ref):
        me8 = me_ref[0]
        acc = None
        for d in range(8):
            rel = d ^ me8
            term = jnp.where(rel == 0, pk_ref[...], rp_ref[jnp.maximum(rel - 1, 0)])
            acc = term if acc is None else acc + term
        o_ref[...] = acc

    return pl.pallas_call(
        body, name="sum_packs",
        grid_spec=pltpu.PrefetchScalarGridSpec(
            num_scalar_prefetch=1, grid=(1,),
            in_specs=[pl.BlockSpec((R, 128), lambda i, mr: (0, 0)), pl.BlockSpec((7, R, 128), lambda i, mr: (0, 0, 0))],
            out_specs=pl.BlockSpec((R, 128), lambda i, mr: (0, 0))),
        out_shape=jax.ShapeDtypeStruct((R, 128), f32),
        compiler_params=_cp(("arbitrary",)),
    )(me8_arr, pack, rp)


def _swap_finished(fs):
    na = len(fs)

    def body(*refs):
        srcs, dsts = refs[:na], refs[na:2 * na]
        send_sems, recv_sems = refs[2 * na:]
        x, y, c, _ = _place()
        cps = [pltpu.make_async_remote_copy(
            src_ref=srcs[i], dst_ref=dsts[i], send_sem=send_sems.at[i], recv_sem=recv_sems.at[i],
            device_id=(x, y, 1 - c), device_id_type=MESH) for i in range(na)]
        for cp in cps:
            cp.start()
        for cp in cps:
            cp.wait()

    return pl.pallas_call(
        body, name="swap_finished",
        in_specs=[ANY] * na, out_specs=[ANY] * na,
        out_shape=[jax.ShapeDtypeStruct(f.shape, f.dtype) for f in fs],
        scratch_shapes=[pltpu.SemaphoreType.DMA((na,)), pltpu.SemaphoreType.DMA((na,))],
        compiler_params=pltpu.CompilerParams(has_side_effects=True),
    )(*fs)


def _adamw_halves(c_arr, mine, peer, w, m, v, name):
    _, R, Cc = w.shape
    tr = min(R, 256)

    def body(c_ref, mine_ref, peer_ref, w_ref, m_ref, v_ref, g_ref, d_ref, m2_ref, v2_ref):
        g = jnp.where(pl.program_id(0) == c_ref[0], mine_ref[...], peer_ref[...])
        delta, m2, v2 = _adam_math(g, w_ref[0], m_ref[0], v_ref[0])
        g_ref[0] = g
        d_ref[0] = delta
        m2_ref[0] = m2
        v2_ref[0] = v2

    half = pl.BlockSpec((tr, Cc), lambda hh, i, cr: (i, 0))
    full = pl.BlockSpec((1, tr, Cc), lambda hh, i, cr: (hh, i, 0))
    return pl.pallas_call(
        body, name=name,
        grid_spec=pltpu.PrefetchScalarGridSpec(
            num_scalar_prefetch=1, grid=(2, R // tr), in_specs=[half, half, full, full, full], out_specs=[full] * 4),
        out_shape=[jax.ShapeDtypeStruct((2, R, Cc), f32)] * 4,
        compiler_params=_cp(("arbitrary", "arbitrary")),
    )(c_arr, mine, peer, w, m, v)


def _rows8(a):
    flat = a.reshape(-1)
    n = flat.shape[0]
    rows = -(-n // 1024) * 8
    return jnp.pad(flat, (0, rows * 128 - n)).reshape(rows, 128)


def kernel(x, meta_tokens, norm_w, w_in, conv_w, hg_lb_logits, hg_norm_w, gdn_A_log, gdn_dt_bias, gdn_norm_w, w_out, final_norm_w, loss_target, m_meta_tokens, m_norm_w, m_w_in, m_conv_w, m_hg_lb_logits, m_hg_norm_w, m_gdn_A_log, m_gdn_dt_bias, m_gdn_norm_w, m_w_out, m_final_norm_w, v_meta_tokens, v_norm_w, v_w_in, v_conv_w, v_hg_lb_logits, v_hg_norm_w, v_gdn_A_log, v_gdn_dt_bias, v_gdn_norm_w, v_w_out, v_final_norm_w):
    me = 2 * lax.axis_index("x") + lax.axis_index("y")

    g_win, g_wout, g_conv, g_meta = _gather_weights(
        [w_in[0].astype(bf16).reshape(2, D // 2, SHARD_COLS), w_out[0].astype(bf16).reshape(2, D // 8, D)],
        [conv_w[0], meta_tokens])
    w_full = jnp.transpose(g_win.reshape(4, D, SHARD_COLS), (1, 0, 2)).reshape(D, IN_COLS)
    wbig = jnp.pad(w_full, ((0, 0), (0, PC - IN_COLS)))
    wout_full = g_wout.reshape(D, D)
    conv4 = jnp.transpose(g_conv, (1, 0, 2)).reshape(4, 1, 3 * HD)
    meta_full = jnp.transpose(g_meta, (1, 0, 2)).reshape(N_META, D)

    (loss8, grad_x, d_meta, d_nw, gw_full, d_conv, d_lb, d_hgw, d_alog, d_dtb, d_gdw, g_wout_part, d_fw) = _local_step(
        x, loss_target, wbig, wout_full, conv4, meta_full, norm_w, hg_lb_logits, hg_norm_w, gdn_A_log, gdn_dt_bias,
        gdn_norm_w, final_norm_w)

    g_in_blocks = jnp.transpose(gw_full.reshape(D, 4, SHARD_COLS), (1, 0, 2))
    g_out_blocks = g_wout_part.reshape(4, D // 4, D)
    pack = jnp.concatenate([
        loss8, d_nw[0].reshape(8, 128), d_lb.reshape(8, 128), d_hgw, _rows8(d_alog[0, :H]), _rows8(d_dtb[0, :H]),
        d_gdw, d_fw[0].reshape(8, 128), d_meta.reshape(128, 128), d_conv.reshape(48, 128)], axis=0)
    return _reduce_and_update(
        me, grad_x, g_in_blocks, g_out_blocks, pack, meta_tokens, norm_w, w_in, conv_w, hg_lb_logits, hg_norm_w, gdn_A_log,
        gdn_dt_bias, gdn_norm_w, w_out, final_norm_w, m_meta_tokens, m_norm_w, m_w_in, m_conv_w, m_hg_lb_logits,
        m_hg_norm_w, m_gdn_A_log, m_gdn_dt_bias, m_gdn_norm_w, m_w_out, m_final_norm_w, v_meta_tokens, v_norm_w, v_w_in,
        v_conv_w, v_hg_lb_logits, v_hg_norm_w, v_gdn_A_log, v_gdn_dt_bias, v_gdn_norm_w, v_w_out, v_final_norm_w)


def _local_step(x, loss_target, wbig, wout_full, conv4, meta_full, norm_w, hg_lb_logits, hg_norm_w, gdn_A_log, gdn_dt_bias,
                gdn_norm_w, final_norm_w):
    h3 = jnp.concatenate([jnp.zeros((NB, PAD, D), f32), jnp.broadcast_to(meta_full[None], (NB, N_META, D)), x], axis=1)
    hflat = h3.reshape(N, D)
    target = jnp.pad(loss_target, ((0, 0), (PAD + N_META, 0), (0, 0))).reshape(N, D)
    l0, l1 = hg_lb_logits[0:1], hg_lb_logits[1:2]
    alog = jnp.pad(gdn_A_log, ((0, 0), (0, DK - H)))
    dtb = jnp.pad(gdn_dt_bias, ((0, 0), (0, DK - H)))
    fw = final_norm_w.reshape(1, D)

    proj, ut = _in_proj(hflat, norm_w, wbig)
    proj3 = proj.reshape(NB, TP, PC)
    o_hg, s_hg = _hg_fwd(proj3, l0, l1)
    cv = _conv_fwd(proj3, conv4)
    o_gd, s_gd = _gd_fwd(cv, proj3, alog, dtb)
    (loss8, d_ohg, d_ogd, d_zhg, d_zgd, dh_res, g_wout_part, d_hgw, d_gdw, d_fw) = _out_loss(
        o_hg.reshape(N, HD), o_gd.reshape(N, HD), proj, hg_norm_w, gdn_norm_w, wout_full, hflat, fw, target)
    d_hg, d_l0, d_l1 = _hg_bwd(proj3, l0, l1, s_hg, d_ohg.reshape(NB, TP, HD))
    d_cv, d_ab, d_alog, d_dtb = _gd_bwd(cv, proj3, alog, dtb, s_gd, d_ogd.reshape(NB, TP, HD))
    d_qkv, d_conv4 = _conv_bwd(proj3, conv4, d_cv)
    d_hg2, d_qkv2, d_ab2 = d_hg.reshape(N, 3 * HD), d_qkv.reshape(N, 3 * HD), d_ab.reshape(N, DK)
    pieces = [(d_hg2, COL_HG), (d_zhg, COL_ZHG), (d_qkv2, COL_QKV), (d_zgd, COL_ZGD), (d_ab2, COL_AB)]
    dh, d_nw = _in_bwd(pieces, wbig, hflat, norm_w, dh_res)
    gws = [_w_grad(ut, d, "w_grad_%d" % c0) for d, c0 in pieces]
    gw_full = jnp.concatenate(gws[:4] + [gws[4][:, 0:IN_COLS - COL_AB]], axis=1)

    dh3 = dh.reshape(NB, TP, D)
    grad_x = dh3[:, PAD + N_META:, :]
    d_meta = jnp.sum(dh3[:, PAD:PAD + N_META, :], axis=0)
    d_conv = d_conv4[:, 0, :]
    d_lb = jnp.concatenate([d_l0[0:1], d_l1[0:1]], axis=0)
    return loss8, grad_x, d_meta, d_nw, gw_full, d_conv, d_lb, d_hgw, d_alog, d_dtb, d_gdw, g_wout_part, d_fw


def _reduce_and_update(me, grad_x, g_in_blocks, g_out_blocks, pack, meta_tokens, norm_w, w_in, conv_w, hg_lb_logits, hg_norm_w,
                       gdn_A_log, gdn_dt_bias, gdn_norm_w, w_out, final_norm_w, m_meta_tokens, m_norm_w, m_w_in, m_conv_w,
                       m_hg_lb_logits, m_hg_norm_w, m_gdn_A_log, m_gdn_dt_bias, m_gdn_norm_w, m_w_out, m_final_norm_w,
                       v_meta_tokens, v_norm_w, v_w_in, v_conv_w, v_hg_lb_logits, v_hg_norm_w, v_gdn_A_log, v_gdn_dt_bias,
                       v_gdn_norm_w, v_w_out, v_final_norm_w):
    c_arr = lax.axis_index("c").reshape(1).astype(jnp.int32)
    me_arr = me.reshape(1).astype(jnp.int32)
    g_in4 = g_in_blocks.reshape(4, 2, D // 2, SHARD_COLS)
    g_out4 = g_out_blocks.reshape(4, 2, D // 8, D)
    s_in, s_out = _swap_halves([g_in4, g_out4])
    pb_in, pf_in = _add_halves(c_arr, g_in4, s_in, "add_w_in")
    pb_out, pf_out = _add_halves(c_arr, g_out4, s_out, "add_w_out")
    r_in, r_out, r_pack = _scatter_blocks([pb_in, pb_out], pack)
    f_in = _sum_blocks(me_arr, pf_in, r_in, "sum_w_in")
    f_out = _sum_blocks(me_arr, pf_out, r_out, "sum_w_out")
    o_in, o_out = _swap_finished([f_in, f_out])
    me8_arr = (2 * me + lax.axis_index("c")).reshape(1).astype(jnp.int32)
    small = _sum_packs(me8_arr, pack, r_pack)

    half_in = lambda a: a[0].reshape(2, D // 2, SHARD_COLS)
    half_out = lambda a: a[0].reshape(2, D // 8, D)
    gi, di, mi, vi = [a.reshape(D, SHARD_COLS) for a in _adamw_halves(
        c_arr, f_in, o_in, half_in(w_in), half_in(m_w_in), half_in(v_w_in), "adamw_w_in")]
    go, do_, mo, vo = [a.reshape(D // 4, D) for a in _adamw_halves(
        c_arr, f_out, o_out, half_out(w_out), half_out(m_w_out), half_out(v_w_out), "adamw_w_out")]

    g_meta_full = small[64:192].reshape(N_META, D)
    g_meta_loc = lax.dynamic_slice(g_meta_full, (0, me * 256), (N_META, 256))
    gm, dm, mm_, vm = _adamw([g_meta_loc], meta_tokens, m_meta_tokens, v_meta_tokens, "adamw_meta")
    g_conv_full = small[192:240].reshape(4, 1536)
    g_conv_loc = lax.dynamic_slice(g_conv_full, (0, me * 384), (4, 384))
    gc, dc, mc, vc = _adamw([g_conv_loc], conv_w[0], m_conv_w[0], v_conv_w[0], "adamw_conv")

    reps = [(norm_w, m_norm_w, v_norm_w), (hg_lb_logits, m_hg_lb_logits, v_hg_lb_logits),
            (hg_norm_w, m_hg_norm_w, v_hg_norm_w), (gdn_A_log, m_gdn_A_log, v_gdn_A_log),
            (gdn_dt_bias, m_gdn_dt_bias, v_gdn_dt_bias), (gdn_norm_w, m_gdn_norm_w, v_gdn_norm_w),
            (final_norm_w, m_final_norm_w, v_final_norm_w)]
    wp = jnp.concatenate([_rows8(t[0]) for t in reps], axis=0)
    mp = jnp.concatenate([_rows8(t[1]) for t in reps], axis=0)
    vp = jnp.concatenate([_rows8(t[2]) for t in reps], axis=0)
    gr, dr, mr, vr = _adamw([small[8:64]], wp, mp, vp, "adamw_small")

    def unpack(p):
        outs = []
        for i, t in enumerate(reps):
            n = t[0].size
            outs.append(p[8 * i:8 * i + 8].reshape(-1)[:n].reshape(t[0].shape))
        return outs

    def leaves(meta_v, conv_v, in_v, out_v, rep_p):
        nw, lb, hgw, al, db, gdw, fwv = unpack(rep_p)
        return [meta_v, nw, in_v[None], conv_v[None], lb, hgw, al, db, gdw, out_v[None], fwv]

    loss = small[0, 0]
    return (loss, grad_x, *leaves(gm, gc, gi, go, gr), *leaves(dm, dc, di, do_, dr),
            *leaves(mm_, mc, mi, mo, mr), *leaves(vm, vc, vi, vo, vr))
```

```python
import functools

import jax
import jax.numpy as jnp
from jax import lax
from jax.experimental import pallas as pl
from jax.experimental.pallas import tpu as pltpu

f32 = jnp.float32
bf16 = jnp.bfloat16
MESH = pl.DeviceIdType.MESH
ANY = pl.BlockSpec(memory_space=pl.ANY)

D = 1024
NB = 2
N_META = 16
SEQ = 2048
PAD = 48
TP = PAD + N_META + SEQ
C = 64
NCH = TP // C
N = NB * TP
H = 4
DK = 128
HD = H * DK
PC = 4224
IN_COLS = 4104
SHARD_COLS = IN_COLS // 4
COL_HG, COL_ZHG, COL_QKV, COL_ZGD, COL_AB = 0, 3 * HD, 4 * HD, 7 * HD, 8 * HD
EPS = 1e-6
ADAM_LR, ADAM_B1, ADAM_B2, ADAM_EPS, ADAM_WD, ADAM_STEP = 0.001, 0.9, 0.999, 1e-08, 0.01, 10
VMEM_LIMIT = 56 * 1024 * 1024

P_HG = dict(lvl=1, av=1, qs=1, su=1)
P_GD = dict(kk=1, inv=1, sol=1, ws=1, qk=1, o=1, su=1)


def _cp(sem=None, **kw):
    return pltpu.CompilerParams(dimension_semantics=sem, vmem_limit_bytes=VMEM_LIMIT, **kw)


_DIMS = {"nn": (((1,), (0,)), ((), ())), "nt": (((1,), (1,)), ((), ())), "tn": (((0,), (0,)), ((), ()))}


def _split(x):
    hi = x.astype(bf16)
    return hi, (x - hi.astype(f32)).astype(bf16)


def _dg(a, b, kind, passes):
    d = lambda x, y: lax.dot_general(x, y, _DIMS[kind], preferred_element_type=f32)
    if passes == 1:
        return d(a.astype(bf16), b.astype(bf16))
    ah, al = _split(a)
    bh, bl = _split(b)
    return d(ah, bh) + d(ah, bl) + d(al, bh)


@functools.partial(jax.custom_vjp, nondiff_argnums=(2, 3))
def mmx(a, b, kind, passes):
    return _dg(a, b, kind, passes)


def _mmx_fwd(a, b, kind, passes):
    return _dg(a, b, kind, passes), (a, b)


def _mmx_bwd(kind, passes, res, g):
    a, b = res
    if kind == "nn":
        return _dg(g, b, "nt", passes), _dg(a, g, "tn", passes)
    if kind == "nt":
        return _dg(g, b, "nn", passes), _dg(g, a, "tn", passes)
    return _dg(b, g, "nt", passes), _dg(a, g, "nn", passes)


mmx.defvjp(_mmx_fwd, _mmx_bwd)


def _mask_dg(mask, x, kind):
    n = x.shape[1]
    xh, xl = _split(x)
    r = lax.dot_general(mask, jnp.concatenate([xh, xl], axis=1), _DIMS[kind], preferred_element_type=f32)
    return r[:, :n] + r[:, n:]


@jax.custom_vjp
def mask_mm(mask, x):
    return _mask_dg(mask, x, "nn")


def _mask_fwd(mask, x):
    return _mask_dg(mask, x, "nn"), mask


def _mask_bwd(mask, g):
    return None, _mask_dg(mask, g, "tn")


mask_mm.defvjp(_mask_fwd, _mask_bwd)


def bdot(a, b):
    return jnp.dot(a.astype(bf16), b.astype(bf16), preferred_element_type=f32)


def bdot_nt(a, b):
    return lax.dot_general(a.astype(bf16), b.astype(bf16), _DIMS["nt"], preferred_element_type=f32)


def bdot_tn(a, b):
    return lax.dot_general(a.astype(bf16), b.astype(bf16), _DIMS["tn"], preferred_element_type=f32)


def _iota2(n, m):
    return lax.broadcasted_iota(jnp.int32, (n, m), 0), lax.broadcasted_iota(jnp.int32, (n, m), 1)


def silu(x):
    return x * jax.nn.sigmoid(x)


def softplus(x):
    return jnp.maximum(x, 0.0) + jnp.log(1.0 + jnp.exp(-jnp.abs(x)))


def rmsnorm(x, w):
    return x * lax.rsqrt(jnp.mean(x * x, axis=-1, keepdims=True) + EPS) * w


def hg_masks():
    t, r = _iota2(C, C)
    mats = [r <= t, r > t]
    lvl = []
    for l in range(1, 7):
        sz = 1 << l
        half = sz >> 1
        seg_t = t >> l
        upper_t = (t & (sz - 1)) >= half
        mid_t = seg_t * sz + half - 1
        mats.append((upper_t & (r > mid_t) & (r <= t)) | ((~upper_t) & (r > t) & (r <= mid_t)))
        lvl.append(((seg_t == (r >> l)) & upper_t & ((r & (sz - 1)) < half)).astype(f32))
    stk = jnp.concatenate([m.astype(bf16) for m in mats], axis=0)
    return stk, lvl, (t == r).astype(f32)


def _head(a, h):
    return a[:, h * DK:(h + 1) * DK]


def hg_chunk(St, ps, l0, l1):
    m = jnp.maximum(l0, l1)
    e0 = jnp.exp(l0 - m)
    e1 = jnp.exp(l1 - m)
    lb = e0 / (e0 + e1)
    stk, lvl, eye = hg_masks()
    msk = [eye] + lvl
    qs, ks, vs, qG, kR, eGl = [], [], [], [], [], []
    for p in ps:
        pq, pf, v = p[:, 0:HD], p[:, HD:2 * HD], p[:, 2 * HD:3 * HD]
        q = silu(pq)
        f = lb + (1.0 - lb) * jax.nn.sigmoid(pf)
        k = 1.0 - f
        logf = jnp.log(f)
        Dm = mask_mm(stk, logf)
        ex = [jnp.exp(Dm[(2 + i) * C:(3 + i) * C]) for i in range(6)]
        qs.append([q] + [q * e for e in ex])
        ks.append([k] + [k * e for e in ex])
        vs.append(v)
        qG.append(q * jnp.exp(Dm[0:C]))
        kR.append(k * jnp.exp(Dm[C:2 * C]))
        eGl.append(jnp.exp(jnp.sum(logf, axis=0, keepdims=True)))
    units = [(b, h) for b in range(len(ps)) for h in range(H)]
    parts = [[msk[i] * mmx(_head(qs[b][i], h), _head(ks[b][i], h), "nt", P_HG["lvl"]) for b, h in units]
             for i in range(7)]
    A = [functools.reduce(lambda x, y: x + y, [parts[i][n] for i in range(7)]) for n in range(len(units))]
    qS = [mmx(_head(qG[b], h), St[n], "nt", P_HG["qs"]) for n, (b, h) in enumerate(units)]
    Sn = [St[n] * _head(eGl[b], h) + mmx(_head(vs[b], h), _head(kR[b], h), "tn", P_HG["su"])
          for n, (b, h) in enumerate(units)]
    outs = [mmx(A[n], _head(vs[b], h), "nn", P_HG["av"]) + qS[n] for n, (b, h) in enumerate(units)]
    return tuple(Sn), tuple(jnp.concatenate(outs[b * H:(b + 1) * H], axis=1) for b in range(len(ps)))


def gd_chunk(S, cs, abs_, alog, dtb):
    t, r = _iota2(C, C)
    tri = (r <= t).astype(bf16)
    ups = (r > t).astype(bf16)
    lane = lax.broadcasted_iota(jnp.int32, (1, DK), 1)
    subl = lax.broadcasted_iota(jnp.int32, (8, 1), 0)
    eye = (t == r).astype(f32)
    strict = (r < t).astype(f32)
    bd = ((t >> 4) == (r >> 4)).astype(f32)
    qa, ka, va, b4, gam4, grev4, gam4T, glast4 = [], [], [], [], [], [], [], []
    for c, ab in zip(cs, abs_):
        qa.append(silu(c[:, 0:HD]))
        ka.append(silu(c[:, HD:2 * HD]))
        va.append(silu(c[:, 2 * HD:3 * HD]))
        g4 = -jnp.exp(alog) * softplus(ab + dtb)
        b4.append(jax.nn.sigmoid(ab))
        gam4.append(mask_mm(tri, g4))
        grev4.append(mask_mm(ups, g4))
        gam4T.append(gam4[-1].T)
        glast4.append(jnp.sum(g4, axis=0, keepdims=True))
    units = [(b, h) for b in range(len(cs)) for h in range(H)]
    nu = range(len(units))
    inv = lambda a, b: [mmx(a[n], b[n], "nn", P_GD["inv"]) for n in nu]
    v = [_head(va[b], h) for b, h in units]
    q = [_head(qa[b], h) for b, h in units]
    k = [_head(ka[b], h) for b, h in units]
    q = [x * lax.rsqrt(jnp.sum(x * x, -1, keepdims=True) + EPS) * (DK ** -0.5) for x in q]
    k = [x * lax.rsqrt(jnp.sum(x * x, -1, keepdims=True) + EPS) for x in k]
    oh = [(lane == h).astype(f32) for h in range(H)]
    gam_c = [jnp.sum(gam4[b] * oh[h], -1, keepdims=True) for b, h in units]
    grev_c = [jnp.sum(grev4[b] * oh[h], -1, keepdims=True) for b, h in units]
    beta = [jnp.sum(b4[b] * (lane == h + H).astype(f32), -1, keepdims=True) for b, h in units]
    glast = [jnp.sum(glast4[b] * oh[h], -1, keepdims=True) for b, h in units]
    gam_r = [jnp.sum(gam4T[b][0:8, :] * (subl == h).astype(f32), axis=0, keepdims=True) for b, h in units]
    dec = [jnp.exp(jnp.where(r <= t, gam_c[n] - gam_r[n], -1e30)) for n in nu]
    egam = [jnp.exp(gam_c[n]) for n in nu]
    kk = [mmx(k[n], k[n], "nt", P_GD["kk"]) for n in nu]
    qk = [mmx(q[n], k[n], "nt", P_GD["qk"]) * dec[n] for n in nu]
    A = [beta[n] * kk[n] * dec[n] * strict for n in nu]
    Dg = [A[n] * bd for n in nu]
    L = [A[n] - Dg[n] for n in nu]
    ImD = [eye - Dg[n] for n in nu]
    D2 = inv(Dg, Dg)
    P1 = inv(ImD, [eye + x for x in D2])
    D4 = inv(D2, D2)
    P2 = inv(P1, [eye + x for x in D4])
    D8 = inv(D4, D4)
    M = inv(P2, [eye + x for x in D8])
    Nn = inv(M, L)
    N2 = inv(Nn, Nn)
    T1 = inv([eye - x for x in Nn], [eye + x for x in N2])
    Tinv = inv(T1, M)
    rhs = [jnp.concatenate([beta[n] * v[n], (beta[n] * egam[n]) * k[n]], axis=1) for n in nu]
    sol = [mmx(Tinv[n], rhs[n], "nn", P_GD["sol"]) for n in nu]
    qwS = [mmx(jnp.concatenate([q[n] * egam[n], sol[n][:, DK:2 * DK]], axis=0), S[n], "nn", P_GD["ws"]) for n in nu]
    u = [sol[n][:, 0:DK] - qwS[n][C:2 * C] for n in nu]
    outs = [qwS[n][0:C] + mmx(qk[n], u[n], "nn", P_GD["o"]) for n in nu]
    Sn = [jnp.exp(glast[n]) * S[n] + mmx(k[n] * jnp.exp(grev_c[n]), u[n], "tn", P_GD["su"]) for n in nu]
    return tuple(Sn), tuple(jnp.concatenate(outs[b * H:(b + 1) * H], axis=1) for b in range(len(cs)))


def _in_proj(hflat, norm_w, wbig):
    tm = 384

    def body(h_ref, nw_ref, w_ref, p_ref, ut_ref):
        u = rmsnorm(h_ref[...], nw_ref[...])
        ut_ref[...] = u.T.astype(bf16)
        p_ref[...] = jnp.dot(u.astype(bf16), w_ref[...], preferred_element_type=f32)

    return pl.pallas_call(
        body, name="in_proj", grid=(N // tm,),
        in_specs=[pl.BlockSpec((tm, D), lambda i: (i, 0)), pl.BlockSpec((1, D), lambda i: (0, 0)),
                  pl.BlockSpec((D, PC), lambda i: (0, 0))],
        out_specs=[pl.BlockSpec((tm, PC), lambda i: (i, 0)), pl.BlockSpec((D, tm), lambda i: (0, i))],
        out_shape=[jax.ShapeDtypeStruct((N, PC), f32), jax.ShapeDtypeStruct((D, N), bf16)],
        compiler_params=_cp(("arbitrary",)),
    )(hflat, norm_w, wbig)


NU = NB * H
_REV = lambda c: NCH - 1 - c
_FWD = lambda c: c


def _tok_spec(w, ix, col=0):
    return pl.BlockSpec((NB, C, w), lambda c: (0, ix(c), col))


def _state_spec(ix):
    return pl.BlockSpec((NB, 1, H, DK, DK), lambda c: (0, ix(c), 0, 0, 0))


def _row_spec(w):
    return pl.BlockSpec((1, w), lambda c: (0, 0))


def _rows(ref):
    return tuple(ref[b] for b in range(NB))


def _scan_fwd(chunk_fn, name, x3, extra, extra_specs, n_tok):
    ne = len(extra)

    def body(*refs):
        x_ref = refs[0]
        e_refs = refs[1:1 + ne]
        o_ref, s_ref, st = refs[1 + ne:]

        @pl.when(pl.program_id(0) == 0)
        def _():
            st[...] = jnp.zeros_like(st)

        S = tuple(st[n] for n in range(NU))
        for n in range(NU):
            s_ref[n // H, 0, n % H] = S[n]
        args = [_rows(e) if i < n_tok else e[...] for i, e in enumerate(e_refs)]
        Sn, o = chunk_fn(S, _rows(x_ref), *args)
        for n in range(NU):
            st[n] = Sn[n]
        for b in range(NB):
            o_ref[b] = o[b]

    return pl.pallas_call(
        body, name=name, grid=(NCH,),
        in_specs=[_tok_spec(3 * HD, _FWD)] + extra_specs(_FWD),
        out_specs=[_tok_spec(HD, _FWD), _state_spec(_FWD)],
        out_shape=[jax.ShapeDtypeStruct((NB, TP, HD), f32), jax.ShapeDtypeStruct((NB, NCH, H, DK, DK), f32)],
        scratch_shapes=[pltpu.VMEM((NU, DK, DK), f32)],
        compiler_params=_cp(("arbitrary",)),
    )(x3, *extra)


def _hg_extra_specs(ix):
    return [_row_spec(HD), _row_spec(HD)]


def _gd_extra_specs(ix):
    return [_tok_spec(DK, ix, COL_AB // DK), _row_spec(DK), _row_spec(DK)]


def _hg_fwd(proj3, l0, l1):
    return _scan_fwd(hg_chunk, "hg_fwd", proj3, [l0, l1], _hg_extra_specs, 0)


def _gd_fwd(cv, proj3, alog, dtb):
    return _scan_fwd(gd_chunk, "gd_fwd", cv, [proj3, alog, dtb], _gd_extra_specs, 1)


def _hg_bwd(proj3, l0, l1, s_saved, do):
    def body(p_ref, l0_ref, l1_ref, s_ref, do_ref, dp_ref, dl0_ref, dl1_ref, dst):
        @pl.when(pl.program_id(0) == 0)
        def _():
            dst[...] = jnp.zeros_like(dst)
            dl0_ref[...] = jnp.zeros_like(dl0_ref)
            dl1_ref[...] = jnp.zeros_like(dl1_ref)

        S = tuple(s_ref[n // H, 0, n % H] for n in range(NU))
        _, vjp = jax.vjp(hg_chunk, S, _rows(p_ref), l0_ref[...], l1_ref[...])
        dS, dp, dl0, dl1 = vjp((tuple(dst[n] for n in range(NU)), _rows(do_ref)))
        for n in range(NU):
            dst[n] = dS[n]
        for b in range(NB):
            dp_ref[b] = dp[b].astype(bf16)
        dl0_ref[...] += jnp.broadcast_to(dl0, (8, HD))
        dl1_ref[...] += jnp.broadcast_to(dl1, (8, HD))

    acc = pl.BlockSpec((8, HD), lambda c: (0, 0))
    return pl.pallas_call(
        body, name="hg_bwd", grid=(NCH,),
        in_specs=[_tok_spec(3 * HD, _REV)] + _hg_extra_specs(_REV) + [_state_spec(_REV), _tok_spec(HD, _REV)],
        out_specs=[_tok_spec(3 * HD, _REV), acc, acc],
        out_shape=[jax.ShapeDtypeStruct((NB, TP, 3 * HD), bf16), jax.ShapeDtypeStruct((8, HD), f32),
                   jax.ShapeDtypeStruct((8, HD), f32)],
        scratch_shapes=[pltpu.VMEM((NU, DK, DK), f32)],
        compiler_params=_cp(("arbitrary",)),
    )(proj3, l0, l1, s_saved, do)


def _gd_bwd(cv, proj3, alog, dtb, s_saved, do):
    def body(c_ref, ab_ref, al_ref, db_ref, s_ref, do_ref, dc_ref, dab_ref, dal_ref, ddb_ref, dst):
        @pl.when(pl.program_id(0) == 0)
        def _():
            dst[...] = jnp.zeros_like(dst)
            dal_ref[...] = jnp.zeros_like(dal_ref)
            ddb_ref[...] = jnp.zeros_like(ddb_ref)

        S = tuple(s_ref[n // H, 0, n % H] for n in range(NU))
        _, vjp = jax.vjp(gd_chunk, S, _rows(c_ref), _rows(ab_ref), al_ref[...], db_ref[...])
        dS, dc, dab, dal, ddb = vjp((tuple(dst[n] for n in range(NU)), _rows(do_ref)))
        for n in range(NU):
            dst[n] = dS[n]
        for b in range(NB):
            dc_ref[b] = dc[b]
            dab_ref[b] = dab[b].astype(bf16)
        dal_ref[...] += jnp.broadcast_to(dal, (8, DK))
        ddb_ref[...] += jnp.broadcast_to(ddb, (8, DK))

    acc = pl.BlockSpec((8, DK), lambda c: (0, 0))
    return pl.pallas_call(
        body, name="gd_bwd", grid=(NCH,),
        in_specs=[_tok_spec(3 * HD, _REV)] + _gd_extra_specs(_REV) + [_state_spec(_REV), _tok_spec(HD, _REV)],
        out_specs=[_tok_spec(3 * HD, _REV), _tok_spec(DK, _REV), acc, acc],
        out_shape=[jax.ShapeDtypeStruct((NB, TP, 3 * HD), f32), jax.ShapeDtypeStruct((NB, TP, DK), bf16),
                   jax.ShapeDtypeStruct((8, DK), f32), jax.ShapeDtypeStruct((8, DK), f32)],
        scratch_shapes=[pltpu.VMEM((NU, DK, DK), f32)],
        compiler_params=_cp(("arbitrary",)),
    )(cv, proj3, alog, dtb, s_saved, do)


def _conv_fwd(proj3, conv4):
    def body(x_ref, w_ref, y_ref):
        x = x_ref[0]
        row = lax.broadcasted_iota(jnp.int32, (TP, 1), 0)
        y = w_ref[3] * x
        for s in (1, 2, 3):
            y = y + w_ref[3 - s] * jnp.where(row >= s, pltpu.roll(x, s, 0), 0.0)
        y_ref[0] = y

    return pl.pallas_call(
        body, name="conv_fwd", grid=(NB, 3),
        in_specs=[pl.BlockSpec((1, TP, HD), lambda b, j: (b, 0, COL_QKV // HD + j)),
                  pl.BlockSpec((4, 1, HD), lambda b, j: (0, 0, j))],
        out_specs=pl.BlockSpec((1, TP, HD), lambda b, j: (b, 0, j)),
        out_shape=jax.ShapeDtypeStruct((NB, TP, 3 * HD), f32),
        compiler_params=_cp(("arbitrary", "arbitrary")),
    )(proj3, conv4)


def _conv_bwd(proj3, conv4, dy):
    def body(x_ref, w_ref, dy_ref, dx_ref, dw_ref):
        @pl.when(pl.program_id(1) == 0)
        def _():
            dw_ref[...] = jnp.zeros_like(dw_ref)

        x = x_ref[0]
        g = dy_ref[0]
        row = lax.broadcasted_iota(jnp.int32, (TP, 1), 0)
        dx = w_ref[3] * g
        dw_ref[3] += jnp.broadcast_to(jnp.sum(x * g, axis=0, keepdims=True), (8, HD))
        for s in (1, 2, 3):
            dx = dx + w_ref[3 - s] * jnp.where(row < TP - s, pltpu.roll(g, TP - s, 0), 0.0)
            xs = jnp.where(row >= s, pltpu.roll(x, s, 0), 0.0)
            dw_ref[3 - s] += jnp.broadcast_to(jnp.sum(xs * g, axis=0, keepdims=True), (8, HD))
        dx_ref[0] = dx.astype(bf16)

    return pl.pallas_call(
        body, name="conv_bwd", grid=(3, NB),
        in_specs=[pl.BlockSpec((1, TP, HD), lambda j, b: (b, 0, COL_QKV // HD + j)),
                  pl.BlockSpec((4, 1, HD), lambda j, b: (0, 0, j)), pl.BlockSpec((1, TP, HD), lambda j, b: (b, 0, j))],
        out_specs=[pl.BlockSpec((1, TP, HD), lambda j, b: (b, 0, j)), pl.BlockSpec((4, 8, HD), lambda j, b: (0, 0, j))],
        out_shape=[jax.ShapeDtypeStruct((NB, TP, 3 * HD), bf16), jax.ShapeDtypeStruct((4, 8, 3 * HD), f32)],
        compiler_params=_cp(("arbitrary", "arbitrary")),
    )(proj3, conv4, dy)


def _gated(o, z, w):
    outs = []
    for hh in range(H):
        sl = slice(hh * DK, (hh + 1) * DK)
        outs.append(rmsnorm(o[:, sl], w) * silu(z[:, sl]))
    return jnp.concatenate(outs, axis=-1)


def _out_loss(o_hg, o_gd, proj, hgw, gdw, wout, hflat, fw, target):
    tm = 384

    def body(ohg_ref, ogd_ref, zhg_ref, zgd_ref, hgw_ref, gdw_ref, wo_ref, h_ref, fw_ref, tg_ref,
             loss_ref, dohg_ref, dogd_ref, dzhg_ref, dzgd_ref, dh_ref, dwo_ref, dhgw_ref, dgdw_ref, dfw_ref):
        i = pl.program_id(0)

        @pl.when(i == 0)
        def _():
            for r in (loss_ref, dwo_ref, dhgw_ref, dgdw_ref, dfw_ref):
                r[...] = jnp.zeros_like(r)

        row = i * tm + lax.broadcasted_iota(jnp.int32, (tm, 1), 0)
        tok = jnp.where(row >= TP, row - TP, row)
        valid = (tok >= PAD + N_META).astype(f32)
        hval = h_ref[...]
        tgt = tg_ref[...]

        def mix(ohg, ogd, zhg, zgd, w1, w2):
            return jnp.concatenate([_gated(ohg, zhg, w1), _gated(ogd, zgd, w2)], axis=-1)

        y, vjp_mix = jax.vjp(mix, ohg_ref[...], ogd_ref[...], zhg_ref[...], zgd_ref[...], hgw_ref[...], gdw_ref[...])
        out = bdot(y, wo_ref[...])

        def head(out, fwv):
            err = (rmsnorm(hval + out, fwv) - tgt) * valid
            return 0.5 * jnp.sum(jnp.mean(err * err, axis=-1, keepdims=True))

        loss, vjp_head = jax.vjp(head, out, fw_ref[...])
        dout, dfw = vjp_head(jnp.ones((), f32))
        dh_ref[...] = dout
        dy = bdot_nt(dout, wo_ref[...])
        dwo_ref[...] += bdot_tn(y, dout)
        dohg, dogd, dzhg, dzgd, dw1, dw2 = vjp_mix(dy)
        dohg_ref[...] = dohg
        dogd_ref[...] = dogd
        dzhg_ref[...] = dzhg.astype(bf16)
        dzgd_ref[...] = dzgd.astype(bf16)
        loss_ref[...] += jnp.broadcast_to(loss, (8, DK))
        dhgw_ref[...] += jnp.broadcast_to(dw1, (8, DK))
        dgdw_ref[...] += jnp.broadcast_to(dw2, (8, DK))
        dfw_ref[...] += jnp.broadcast_to(dfw, (8, D))

    row = lambda w: pl.BlockSpec((tm, w), lambda i: (i, 0))
    whole = lambda r, w: pl.BlockSpec((r, w), lambda i: (0, 0))
    col = lambda c0: pl.BlockSpec((tm, HD), lambda i: (i, c0 // HD))
    return pl.pallas_call(
        body, name="out_loss", grid=(N // tm,),
        in_specs=[row(HD), row(HD), col(COL_ZHG), col(COL_ZGD),
                  whole(1, DK), whole(1, DK), whole(D, D), row(D), whole(1, D), row(D)],
        out_specs=[whole(8, DK), row(HD), row(HD), row(HD), row(HD), row(D), whole(D, D),
                   whole(8, DK), whole(8, DK), whole(8, D)],
        out_shape=[jax.ShapeDtypeStruct((8, DK), f32)] + [jax.ShapeDtypeStruct((N, HD), f32)] * 2
        + [jax.ShapeDtypeStruct((N, HD), bf16)] * 2
        + [jax.ShapeDtypeStruct((N, D), f32), jax.ShapeDtypeStruct((D, D), f32),
           jax.ShapeDtypeStruct((8, DK), f32), jax.ShapeDtypeStruct((8, DK), f32), jax.ShapeDtypeStruct((8, D), f32)],
        compiler_params=_cp(("arbitrary",)),
    )(o_hg, o_gd, proj, proj, hgw, gdw, wout, hflat, fw, target)


def _in_bwd(pieces, wbig, hflat, norm_w, dh_res, pbs):
    tm = 384
    nsteps = N // tm
    np_ = len(pieces)
    na = len(pbs)
    offs = [c0 for _, c0 in pieces]
    widths = [d.shape[1] for d, _ in pieces]

    def body(*refs):
        d_refs = refs[:np_]
        w_ref, h_ref, nw_ref, dhr_ref = refs[np_:np_ + 4]
        srcs = refs[np_ + 4:np_ + 4 + na]
        dh_ref, dnw_ref = refs[np_ + 4 + na:np_ + 6 + na]
        dsts = refs[np_ + 6 + na:np_ + 6 + 2 * na]
        sems = refs[np_ + 6 + 2 * na:]
        i = pl.program_id(0)

        def copies():
            if not na:
                return []
            x, y, c, chips = _place()
            return [pltpu.make_async_remote_copy(
                src_ref=srcs[a].at[2 * px + py], dst_ref=dsts[a].at[j], send_sem=sems[0].at[na * j + a],
                recv_sem=sems[1].at[na * j + a], device_id=(px, py, c), device_id_type=MESH)
                for j, (px, py) in enumerate(chips) for a in range(na)]

        @pl.when(i == 0)
        def _():
            dnw_ref[...] = jnp.zeros_like(dnw_ref)
            for cp in copies():
                cp.start()

        du = jnp.zeros((tm, D), f32)
        for d_ref, off, wd in zip(d_refs, offs, widths):
            du = du + bdot_nt(d_ref[...], w_ref[:, off:off + wd])
        _, vjp = jax.vjp(rmsnorm, h_ref[...], nw_ref[...])
        dh, dnw = vjp(du)
        dh_ref[...] = dh + dhr_ref[...]
        dnw_ref[...] += jnp.broadcast_to(dnw, (8, D))

        @pl.when(i == nsteps - 1)
        def _():
            for cp in copies():
                cp.wait()

    row = lambda w: pl.BlockSpec((tm, w), lambda i: (i, 0))
    return pl.pallas_call(
        body, name="in_bwd", grid=(nsteps,),
        in_specs=[row(w) for w in widths]
        + [pl.BlockSpec((D, PC), lambda i: (0, 0)), row(D), pl.BlockSpec((1, D), lambda i: (0, 0)), row(D)] + [ANY] * na,
        out_specs=[row(D), pl.BlockSpec((8, D), lambda i: (0, 0))] + [ANY] * na,
        out_shape=[jax.ShapeDtypeStruct((N, D), f32), jax.ShapeDtypeStruct((8, D), f32)]
        + [jax.ShapeDtypeStruct((3,) + p.shape[1:], p.dtype) for p in pbs],
        scratch_shapes=[pltpu.SemaphoreType.DMA((3 * na,)), pltpu.SemaphoreType.DMA((3 * na,))] if na else [],
        compiler_params=_cp(("arbitrary",)),
    )(*[d for d, _ in pieces], wbig, hflat, norm_w, dh_res, *pbs)


def _w_grad(ut, d, name):
    n = d.shape[1]
    tk = N // 3

    def body(u_ref, d_ref, o_ref):
        @pl.when(pl.program_id(0) == 0)
        def _():
            o_ref[...] = jnp.zeros_like(o_ref)

        o_ref[...] += jnp.dot(u_ref[...], d_ref[...], preferred_element_type=f32)

    return pl.pallas_call(
        body, name=name, grid=(N // tk,),
        in_specs=[pl.BlockSpec((D, tk), lambda k: (0, k)), pl.BlockSpec((tk, n), lambda k: (k, 0))],
        out_specs=pl.BlockSpec((D, n), lambda k: (0, 0)),
        out_shape=jax.ShapeDtypeStruct((D, n), f32),
        compiler_params=_cp(("arbitrary",)),
    )(ut, d)


def _adam_math(g, w, m, v):
    m2 = ADAM_B1 * m + (1.0 - ADAM_B1) * g
    v2 = ADAM_B2 * v + (1.0 - ADAM_B2) * (g * g)
    m_hat = m2 / (1.0 - ADAM_B1 ** ADAM_STEP)
    v_hat = v2 / (1.0 - ADAM_B2 ** ADAM_STEP)
    delta = -ADAM_LR * (m_hat / (jnp.sqrt(v_hat) + ADAM_EPS) + ADAM_WD * w)
    return delta, m2, v2


def _adamw(gs, w, m, v, name):
    R, Cc = w.shape
    tr = 256 if R % 256 == 0 else R
    ng = len(gs)

    def body(*refs):
        g = refs[0][...]
        for r in refs[1:ng]:
            g = g + r[...]
        w_ref, m_ref, v_ref, g_ref, d_ref, m2_ref, v2_ref = refs[ng:]
        delta, m2, v2 = _adam_math(g, w_ref[...], m_ref[...], v_ref[...])
        g_ref[...] = g
        d_ref[...] = delta
        m2_ref[...] = m2
        v2_ref[...] = v2

    spec = pl.BlockSpec((tr, Cc), lambda i: (i, 0))
    return pl.pallas_call(
        body, name=name, grid=(R // tr,),
        in_specs=[spec] * (ng + 3), out_specs=[spec] * 4,
        out_shape=[jax.ShapeDtypeStruct((R, Cc), f32)] * 4,
        compiler_params=_cp(("arbitrary",)),
    )(*gs, w, m, v)


def _sum_slots(r, name):
    S, R, Cc = r.shape
    tr = 256 if R % 256 == 0 else R

    def body(r_ref, o_ref):
        acc = r_ref[0]
        for s in range(1, S):
            acc = acc + r_ref[s]
        o_ref[...] = acc

    return pl.pallas_call(
        body, name=name, grid=(R // tr,),
        in_specs=[pl.BlockSpec((S, tr, Cc), lambda i: (0, i, 0))], out_specs=pl.BlockSpec((tr, Cc), lambda i: (i, 0)),
        out_shape=jax.ShapeDtypeStruct((R, Cc), f32),
        compiler_params=_cp(("arbitrary",)),
    )(r)


def _place():
    x, y, c = lax.axis_index("x"), lax.axis_index("y"), lax.axis_index("c")
    return x, y, c, [(1 - x, y), (x, 1 - y), (1 - x, 1 - y)]


def _gather_weights(halved, whole):
    nh, nw = len(halved), len(whole)
    na = nh + nw

    def body(*refs):
        srcs, dsts = refs[:na], refs[na:2 * na]
        send_sems, recv_sems, loc_sems = refs[2 * na:2 * na + 3]
        stage = refs[2 * na + 3:]
        x, y, c, chips = _place()
        me = 2 * x + y
        loads = [pltpu.make_async_copy(s, v, loc_sems.at[i]) for i, (s, v) in enumerate(zip(srcs, stage))]
        locs = [pltpu.make_async_copy(v, d.at[me], loc_sems.at[i]) for i, (v, d) in enumerate(zip(stage, dsts))]
        for cp in loads:
            cp.start()

        def ici(j, i, slot):
            px, py = chips[j]
            src = srcs[i].at[c] if i < nh else srcs[i]
            dst = dsts[i].at[slot, c] if i < nh else dsts[i].at[slot]
            return pltpu.make_async_remote_copy(
                src_ref=src, dst_ref=dst, send_sem=send_sems.at[na * j + i], recv_sem=recv_sems.at[na * j + i],
                device_id=(px, py, c), device_id_type=MESH)

        def d2d(j, i, half):
            px, py = chips[j]
            blk = dsts[i].at[2 * px + py, half]
            return pltpu.make_async_remote_copy(
                src_ref=blk, dst_ref=blk, send_sem=send_sems.at[3 * na + nh * j + i],
                recv_sem=recv_sems.at[3 * na + nh * j + i], device_id=(x, y, 1 - c), device_id_type=MESH)

        sends = [ici(j, i, me) for j in range(3) for i in range(na)]
        for cp in sends:
            cp.start()
        for ld, st in zip(loads, locs):
            ld.wait()
            st.start()
        for j, (px, py) in enumerate(chips):
            for i in range(na):
                ici(j, i, 2 * px + py).wait_recv()
                if i < nh:
                    fwd = d2d(j, i, c)
                    fwd.start()
                    sends.append(fwd)
        for j in range(3):
            for i in range(nh):
                d2d(j, i, 1 - c).wait_recv()
        for cp in sends:
            cp.wait_send()
        for cp in locs:
            cp.wait()

    nsem = 3 * na + 3 * nh
    return pl.pallas_call(
        body, name="gather_weights",
        in_specs=[ANY] * na, out_specs=[ANY] * na,
        out_shape=[jax.ShapeDtypeStruct((4,) + s.shape, s.dtype) for s in list(halved) + list(whole)],
        scratch_shapes=[pltpu.SemaphoreType.DMA((nsem,)), pltpu.SemaphoreType.DMA((nsem,)),
                        pltpu.SemaphoreType.DMA((na,))] + [pltpu.VMEM(s.shape, s.dtype) for s in list(halved) + list(whole)],
        compiler_params=pltpu.CompilerParams(has_side_effects=True, vmem_limit_bytes=VMEM_LIMIT),
    )(*halved, *whole)


def _swap_halves(gs):
    na = len(gs)

    def body(*refs):
        srcs, dsts = refs[:na], refs[na:2 * na]
        send_sems, recv_sems = refs[2 * na:]
        x, y, c, _ = _place()
        cps = [pltpu.make_async_remote_copy(
            src_ref=srcs[i].at[q, 1 - c], dst_ref=dsts[i].at[q], send_sem=send_sems.at[4 * i + q],
            recv_sem=recv_sems.at[4 * i + q], device_id=(x, y, 1 - c), device_id_type=MESH)
            for i in range(na) for q in range(4)]
        for cp in cps:
            cp.start()
        for cp in cps:
            cp.wait()

    return pl.pallas_call(
        body, name="swap_halves",
        in_specs=[ANY] * na, out_specs=[ANY] * na,
        out_shape=[jax.ShapeDtypeStruct((4,) + g.shape[2:], g.dtype) for g in gs],
        scratch_shapes=[pltpu.SemaphoreType.DMA((4 * na,)), pltpu.SemaphoreType.DMA((4 * na,))],
        compiler_params=pltpu.CompilerParams(has_side_effects=True),
    )(*gs)


def _add_halves(c_arr, g, s, name):
    _, _, R, Cc = g.shape
    tr = min(R, 256)

    def body(c_ref, g_ref, s_ref, b_ref, f_ref):
        p = g_ref[0, 0] + s_ref[0]
        f_ref[0] = p
        b_ref[0] = p.astype(bf16)

    blk = pl.BlockSpec((1, tr, Cc), lambda q, i, cr: (q, i, 0))
    return pl.pallas_call(
        body, name=name,
        grid_spec=pltpu.PrefetchScalarGridSpec(
            num_scalar_prefetch=1, grid=(4, R // tr),
            in_specs=[pl.BlockSpec((1, 1, tr, Cc), lambda q, i, cr: (q, cr[0], i, 0)), blk], out_specs=[blk, blk]),
        out_shape=[jax.ShapeDtypeStruct((4, R, Cc), bf16), jax.ShapeDtypeStruct((4, R, Cc), f32)],
        compiler_params=_cp(("arbitrary", "arbitrary")),
    )(c_arr, g, s)


_FLIPS = [(fx, fy, fc) for fx in (0, 1) for fy in (0, 1) for fc in (0, 1)][1:]


def _scatter_blocks(pbs, pack):
    na = len(pbs)
    R = pack.shape[0]

    def body(*refs):
        srcs, pk = refs[:na], refs[na]
        dsts, rp = refs[na + 1:2 * na + 1], refs[2 * na + 1]
        send_sems, recv_sems = refs[2 * na + 2:]
        x, y, c, chips = _place()

        def big(j, i):
            px, py = chips[j]
            return pltpu.make_async_remote_copy(
                src_ref=srcs[i].at[2 * px + py], dst_ref=dsts[i].at[j], send_sem=send_sems.at[na * j + i],
                recv_sem=recv_sems.at[na * j + i], device_id=(px, py, c), device_id_type=MESH)

        def small(k):
            fx, fy, fc = _FLIPS[k]
            return pltpu.make_async_remote_copy(
                src_ref=pk, dst_ref=rp.at[k], send_sem=send_sems.at[3 * na + k], recv_sem=recv_sems.at[3 * na + k],
                device_id=(x ^ fx, y ^ fy, c ^ fc), device_id_type=MESH)

        cps = [big(j, i) for j in range(3) for i in range(na)] + [small(k) for k in range(7)]
        for cp in cps:
            cp.start()
        for cp in cps:
            cp.wait()

    nsem = 3 * na + 7
    return pl.pallas_call(
        body, name="scatter_blocks",
        in_specs=[ANY] * (na + 1), out_specs=[ANY] * (na + 1),
        out_shape=[jax.ShapeDtypeStruct((3,) + p.shape[1:], p.dtype) for p in pbs]
        + [jax.ShapeDtypeStruct((7, R, 128), f32)],
        scratch_shapes=[pltpu.SemaphoreType.DMA((nsem,)), pltpu.SemaphoreType.DMA((nsem,))],
        compiler_params=pltpu.CompilerParams(has_side_effects=True),
    )(*pbs, pack)


def _sum_blocks(me_arr, pf, r, name):
    _, R, Cc = pf.shape
    tr = min(R, 256)

    def body(me_ref, pf_ref, r_ref, o_ref):
        acc = pf_ref[0]
        for j in range(3):
            acc = acc + r_ref[j].astype(f32)
        o_ref[...] = acc

    return pl.pallas_call(
        body, name=name,
        grid_spec=pltpu.PrefetchScalarGridSpec(
            num_scalar_prefetch=1, grid=(R // tr,),
            in_specs=[pl.BlockSpec((1, tr, Cc), lambda i, mr: (mr[0], i, 0)),
                      pl.BlockSpec((3, tr, Cc), lambda i, mr: (0, i, 0))],
            out_specs=pl.BlockSpec((tr, Cc), lambda i, mr: (i, 0))),
        out_shape=jax.ShapeDtypeStruct((R, Cc), f32),
        compiler_params=_cp(("arbitrary",)),
    )(me_arr, pf, r)


def _sum_packs(me8_arr, pack, rp):
    R = pack.shape[0]

    def body(me_ref, pk_ref, rp_ref, o_ref):
        me8 = me_ref[0]
        acc = None
        for d in range(8):
            rel = d ^ me8
            term = jnp.where(rel == 0, pk_ref[...], rp_ref[jnp.maximum(rel - 1, 0)])
            acc = term if acc is None else acc + term
        o_ref[...] = acc

    return pl.pallas_call(
        body, name="sum_packs",
        grid_spec=pltpu.PrefetchScalarGridSpec(
            num_scalar_prefetch=1, grid=(1,),
            in_specs=[pl.BlockSpec((R, 128), lambda i, mr: (0, 0)), pl.BlockSpec((7, R, 128), lambda i, mr: (0, 0, 0))],
            out_specs=pl.BlockSpec((R, 128), lambda i, mr: (0, 0))),
        out_shape=jax.ShapeDtypeStruct((R, 128), f32),
        compiler_params=_cp(("arbitrary",)),
    )(me8_arr, pack, rp)


def _swap_finished(fs):
    na = len(fs)

    def body(*refs):
        srcs, dsts = refs[:na], refs[na:2 * na]
        send_sems, recv_sems = refs[2 * na:]
        x, y, c, _ = _place()
        cps = [pltpu.make_async_remote_copy(
            src_ref=srcs[i], dst_ref=dsts[i], send_sem=send_sems.at[i], recv_sem=recv_sems.at[i],
            device_id=(x, y, 1 - c), device_id_type=MESH) for i in range(na)]
        for cp in cps:
            cp.start()
        for cp in cps:
            cp.wait()

    return pl.pallas_call(
        body, name="swap_finished",
        in_specs=[ANY] * na, out_specs=[ANY] * na,
        out_shape=[jax.ShapeDtypeStruct(f.shape, f.dtype) for f in fs],
        scratch_shapes=[pltpu.SemaphoreType.DMA((na,)), pltpu.SemaphoreType.DMA((na,))],
        compiler_params=pltpu.CompilerParams(has_side_effects=True),
    )(*fs)


def _adamw_halves(c_arr, mine, peer, w, m, v, name):
    _, R, Cc = w.shape
    tr = min(R, 256)

    def body(c_ref, mine_ref, peer_ref, w_ref, m_ref, v_ref, g_ref, d_ref, m2_ref, v2_ref):
        g = jnp.where(pl.program_id(0) == c_ref[0], mine_ref[...], peer_ref[...])
        delta, m2, v2 = _adam_math(g, w_ref[0], m_ref[0], v_ref[0])
        g_ref[0] = g
        d_ref[0] = delta
        m2_ref[0] = m2
        v2_ref[0] = v2

    half = pl.BlockSpec((tr, Cc), lambda hh, i, cr: (i, 0))
    full = pl.BlockSpec((1, tr, Cc), lambda hh, i, cr: (hh, i, 0))
    return pl.pallas_call(
        body, name=name,
        grid_spec=pltpu.PrefetchScalarGridSpec(
            num_scalar_prefetch=1, grid=(2, R // tr), in_specs=[half, half, full, full, full], out_specs=[full] * 4),
        out_shape=[jax.ShapeDtypeStruct((2, R, Cc), f32)] * 4,
        compiler_params=_cp(("arbitrary", "arbitrary")),
    )(c_arr, mine, peer, w, m, v)


def _rows8(a):
    flat = a.reshape(-1)
    n = flat.shape[0]
    rows = -(-n // 1024) * 8
    return jnp.pad(flat, (0, rows * 128 - n)).reshape(rows, 128)


def kernel(x, meta_tokens, norm_w, w_in, conv_w, hg_lb_logits, hg_norm_w, gdn_A_log, gdn_dt_bias, gdn_norm_w, w_out, final_norm_w, loss_target, m_meta_tokens, m_norm_w, m_w_in, m_conv_w, m_hg_lb_logits, m_hg_norm_w, m_gdn_A_log, m_gdn_dt_bias, m_gdn_norm_w, m_w_out, m_final_norm_w, v_meta_tokens, v_norm_w, v_w_in, v_conv_w, v_hg_lb_logits, v_hg_norm_w, v_gdn_A_log, v_gdn_dt_bias, v_gdn_norm_w, v_w_out, v_final_norm_w):
    me = 2 * lax.axis_index("x") + lax.axis_index("y")

    g_win, g_wout, g_conv, g_meta = _gather_weights(
        [w_in[0].astype(bf16).reshape(2, D // 2, SHARD_COLS), w_out[0].astype(bf16).reshape(2, D // 8, D)],
        [conv_w[0], meta_tokens])
    w_full = jnp.transpose(g_win.reshape(4, D, SHARD_COLS), (1, 0, 2)).reshape(D, IN_COLS)
    wbig = jnp.pad(w_full, ((0, 0), (0, PC - IN_COLS)))
    wout_full = g_wout.reshape(D, D)
    conv4 = jnp.transpose(g_conv, (1, 0, 2)).reshape(4, 1, 3 * HD)
    meta_full = jnp.transpose(g_meta, (1, 0, 2)).reshape(N_META, D)

    c_arr = lax.axis_index("c").reshape(1).astype(jnp.int32)

    def chip_partials(gw_full, g_wout_part):
        g_in4 = jnp.transpose(gw_full.reshape(D, 4, SHARD_COLS), (1, 0, 2)).reshape(4, 2, D // 2, SHARD_COLS)
        g_out4 = g_wout_part.reshape(4, 2, D // 8, D)
        s_in, s_out = _swap_halves([g_in4, g_out4])
        pb_in, pf_in = _add_halves(c_arr, g_in4, s_in, "add_w_in")
        pb_out, pf_out = _add_halves(c_arr, g_out4, s_out, "add_w_out")
        return [pb_in, pb_out], [pf_in, pf_out]

    (loss8, grad_x, d_meta, d_nw, d_conv, d_lb, d_hgw, d_alog, d_dtb, d_gdw, d_fw, pfs, rs) = _local_step(
        x, loss_target, wbig, wout_full, conv4, meta_full, norm_w, hg_lb_logits, hg_norm_w, gdn_A_log, gdn_dt_bias,
        gdn_norm_w, final_norm_w, chip_partials)

    pack = jnp.concatenate([
        loss8, d_nw[0].reshape(8, 128), d_lb.reshape(8, 128), d_hgw, _rows8(d_alog[0, :H]), _rows8(d_dtb[0, :H]),
        d_gdw, d_fw[0].reshape(8, 128), d_meta.reshape(128, 128), d_conv.reshape(48, 128)], axis=0)
    return _reduce_and_update(
        me, c_arr, grad_x, pfs, rs, pack, meta_tokens, norm_w, w_in, conv_w, hg_lb_logits, hg_norm_w, gdn_A_log,
        gdn_dt_bias, gdn_norm_w, w_out, final_norm_w, m_meta_tokens, m_norm_w, m_w_in, m_conv_w, m_hg_lb_logits,
        m_hg_norm_w, m_gdn_A_log, m_gdn_dt_bias, m_gdn_norm_w, m_w_out, m_final_norm_w, v_meta_tokens, v_norm_w, v_w_in,
        v_conv_w, v_hg_lb_logits, v_hg_norm_w, v_gdn_A_log, v_gdn_dt_bias, v_gdn_norm_w, v_w_out, v_final_norm_w)


def _local_step(x, loss_target, wbig, wout_full, conv4, meta_full, norm_w, hg_lb_logits, hg_norm_w, gdn_A_log, gdn_dt_bias,
                gdn_norm_w, final_norm_w, chip_partials):
    h3 = jnp.concatenate([jnp.zeros((NB, PAD, D), f32), jnp.broadcast_to(meta_full[None], (NB, N_META, D)), x], axis=1)
    hflat = h3.reshape(N, D)
    target = jnp.pad(loss_target, ((0, 0), (PAD + N_META, 0), (0, 0))).reshape(N, D)
    l0, l1 = hg_lb_logits[0:1], hg_lb_logits[1:2]
    alog = jnp.pad(gdn_A_log, ((0, 0), (0, DK - H)))
    dtb = jnp.pad(gdn_dt_bias, ((0, 0), (0, DK - H)))
    fw = final_norm_w.reshape(1, D)

    proj, ut = _in_proj(hflat, norm_w, wbig)
    proj3 = proj.reshape(NB, TP, PC)
    o_hg, s_hg = _hg_fwd(proj3, l0, l1)
    cv = _conv_fwd(proj3, conv4)
    o_gd, s_gd = _gd_fwd(cv, proj3, alog, dtb)
    (loss8, d_ohg, d_ogd, d_zhg, d_zgd, dh_res, g_wout_part, d_hgw, d_gdw, d_fw) = _out_loss(
        o_hg.reshape(N, HD), o_gd.reshape(N, HD), proj, hg_norm_w, gdn_norm_w, wout_full, hflat, fw, target)
    d_hg, d_l0, d_l1 = _hg_bwd(proj3, l0, l1, s_hg, d_ohg.reshape(NB, TP, HD))
    d_cv, d_ab, d_alog, d_dtb = _gd_bwd(cv, proj3, alog, dtb, s_gd, d_ogd.reshape(NB, TP, HD))
    d_qkv, d_conv4 = _conv_bwd(proj3, conv4, d_cv)
    d_hg2, d_qkv2, d_ab2 = d_hg.reshape(N, 3 * HD), d_qkv.reshape(N, 3 * HD), d_ab.reshape(N, DK)
    pieces = [(d_hg2, COL_HG), (d_zhg, COL_ZHG), (d_qkv2, COL_QKV), (d_zgd, COL_ZGD), (d_ab2, COL_AB)]
    gws = [_w_grad(ut, d, "w_grad_%d" % c0) for d, c0 in pieces]
    gw_full = jnp.concatenate(gws[:4] + [gws[4][:, 0:IN_COLS - COL_AB]], axis=1)
    pbs, pfs = chip_partials(gw_full, g_wout_part) if chip_partials else ([], [gw_full, g_wout_part])
    dh, d_nw, *rs = _in_bwd(pieces, wbig, hflat, norm_w, dh_res, pbs)

    dh3 = dh.reshape(NB, TP, D)
    grad_x = dh3[:, PAD + N_META:, :]
    d_meta = jnp.sum(dh3[:, PAD:PAD + N_META, :], axis=0)
    d_conv = d_conv4[:, 0, :]
    d_lb = jnp.concatenate([d_l0[0:1], d_l1[0:1]], axis=0)
    return loss8, grad_x, d_meta, d_nw, d_conv, d_lb, d_hgw, d_alog, d_dtb, d_gdw, d_fw, pfs, rs


def _reduce_and_update(me, c_arr, grad_x, pfs, rs, pack, meta_tokens, norm_w, w_in, conv_w, hg_lb_logits, hg_norm_w,
                       gdn_A_log, gdn_dt_bias, gdn_norm_w, w_out, final_norm_w, m_meta_tokens, m_norm_w, m_w_in, m_conv_w,
                       m_hg_lb_logits, m_hg_norm_w, m_gdn_A_log, m_gdn_dt_bias, m_gdn_norm_w, m_w_out, m_final_norm_w,
                       v_meta_tokens, v_norm_w, v_w_in, v_conv_w, v_hg_lb_logits, v_hg_norm_w, v_gdn_A_log, v_gdn_dt_bias,
                       v_gdn_norm_w, v_w_out, v_final_norm_w):
    me_arr = me.reshape(1).astype(jnp.int32)
    (pf_in, pf_out), (r_in, r_out) = pfs, rs
    (r_pack,) = _scatter_blocks([], pack)
    f_in = _sum_blocks(me_arr, pf_in, r_in, "sum_w_in")
    f_out = _sum_blocks(me_arr, pf_out, r_out, "sum_w_out")
    o_in, o_out = _swap_finished([f_in, f_out])
    me8_arr = (2 * me + lax.axis_index("c")).reshape(1).astype(jnp.int32)
    small = _sum_packs(me8_arr, pack, r_pack)

    half_in = lambda a: a[0].reshape(2, D // 2, SHARD_COLS)
    half_out = lambda a: a[0].reshape(2, D // 8, D)
    gi, di, mi, vi = [a.reshape(D, SHARD_COLS) for a in _adamw_halves(
        c_arr, f_in, o_in, half_in(w_in), half_in(m_w_in), half_in(v_w_in), "adamw_w_in")]
    go, do_, mo, vo = [a.reshape(D // 4, D) for a in _adamw_halves(
        c_arr, f_out, o_out, half_out(w_out), half_out(m_w_out), half_out(v_w_out), "adamw_w_out")]

    g_meta_full = small[64:192].reshape(N_META, D)
    g_meta_loc = lax.dynamic_slice(g_meta_full, (0, me * 256), (N_META, 256))
    gm, dm, mm_, vm = _adamw([g_meta_loc], meta_tokens, m_meta_tokens, v_meta_tokens, "adamw_meta")
    g_conv_full = small[192:240].reshape(4, 1536)
    g_conv_loc = lax.dynamic_slice(g_conv_full, (0, me * 384), (4, 384))
    gc, dc, mc, vc = _adamw([g_conv_loc], conv_w[0], m_conv_w[0], v_conv_w[0], "adamw_conv")

    reps = [(norm_w, m_norm_w, v_norm_w), (hg_lb_logits, m_hg_lb_logits, v_hg_lb_logits),
            (hg_norm_w, m_hg_norm_w, v_hg_norm_w), (gdn_A_log, m_gdn_A_log, v_gdn_A_log),
            (gdn_dt_bias, m_gdn_dt_bias, v_gdn_dt_bias), (gdn_norm_w, m_gdn_norm_w, v_gdn_norm_w),
            (final_norm_w, m_final_norm_w, v_final_norm_w)]
    wp = jnp.concatenate([_rows8(t[0]) for t in reps], axis=0)
    mp = jnp.concatenate([_rows8(t[1]) for t in reps], axis=0)
    vp = jnp.concatenate([_rows8(t[2]) for t in reps], axis=0)
    gr, dr, mr, vr = _adamw([small[8:64]], wp, mp, vp, "adamw_small")

    def unpack(p):
        outs = []
        for i, t in enumerate(reps):
            n = t[0].size
            outs.append(p[8 * i:8 * i + 8].reshape(-1)[:n].reshape(t[0].shape))
        return outs

    def leaves(meta_v, conv_v, in_v, out_v, rep_p):
        nw, lb, hgw, al, db, gdw, fwv = unpack(rep_p)
        return [meta_v, nw, in_v[None], conv_v[None], lb, hgw, al, db, gdw, out_v[None], fwv]

    loss = small[0, 0]
    return (loss, grad_x, *leaves(gm, gc, gi, go, gr), *leaves(dm, dc, di, do_, dr),
            *leaves(mm_, mc, mi, mo, mr), *leaves(vm, vc, vi, vo, vr))
```

```python
import functools

import jax
import jax.numpy as jnp
from jax import lax
from jax.experimental import pallas as pl
from jax.experimental.pallas import tpu as pltpu

f32 = jnp.float32
bf16 = jnp.bfloat16
MESH = pl.DeviceIdType.MESH
ANY = pl.BlockSpec(memory_space=pl.ANY)

D = 1024
NB = 2
N_META = 16
SEQ = 2048
PAD = 48
TP = PAD + N_META + SEQ
C = 64
NCH = TP // C
N = NB * TP
H = 4
DK = 128
HD = H * DK
PC = 4224
IN_COLS = 4104
SHARD_COLS = IN_COLS // 4
COL_HG, COL_ZHG, COL_QKV, COL_ZGD, COL_AB = 0, 3 * HD, 4 * HD, 7 * HD, 8 * HD
EPS = 1e-6
ADAM_LR, ADAM_B1, ADAM_B2, ADAM_EPS, ADAM_WD, ADAM_STEP = 0.001, 0.9, 0.999, 1e-08, 0.01, 10
VMEM_LIMIT = 56 * 1024 * 1024

P_HG = dict(lvl=1, av=1, qs=1, su=1)
P_GD = dict(kk=1, inv=1, sol=1, ws=1, qk=1, o=1, su=1)


def _cp(sem=None, **kw):
    return pltpu.CompilerParams(dimension_semantics=sem, vmem_limit_bytes=VMEM_LIMIT, **kw)


_DIMS = {"nn": (((1,), (0,)), ((), ())), "nt": (((1,), (1,)), ((), ())), "tn": (((0,), (0,)), ((), ()))}


def _split(x):
    hi = x.astype(bf16)
    return hi, (x - hi.astype(f32)).astype(bf16)


def _dg(a, b, kind, passes):
    d = lambda x, y: lax.dot_general(x, y, _DIMS[kind], preferred_element_type=f32)
    if passes == 1:
        return d(a.astype(bf16), b.astype(bf16))
    ah, al = _split(a)
    bh, bl = _split(b)
    return d(ah, bh) + d(ah, bl) + d(al, bh)


@functools.partial(jax.custom_vjp, nondiff_argnums=(2, 3))
def mmx(a, b, kind, passes):
    return _dg(a, b, kind, passes)


def _mmx_fwd(a, b, kind, passes):
    return _dg(a, b, kind, passes), (a, b)


def _mmx_bwd(kind, passes, res, g):
    a, b = res
    if kind == "nn":
        return _dg(g, b, "nt", passes), _dg(a, g, "tn", passes)
    if kind == "nt":
        return _dg(g, b, "nn", passes), _dg(g, a, "tn", passes)
    return _dg(b, g, "nt", passes), _dg(a, g, "nn", passes)


mmx.defvjp(_mmx_fwd, _mmx_bwd)


def _mask_dg(mask, x, kind):
    n = x.shape[1]
    xh, xl = _split(x)
    r = lax.dot_general(mask, jnp.concatenate([xh, xl], axis=1), _DIMS[kind], preferred_element_type=f32)
    return r[:, :n] + r[:, n:]


@jax.custom_vjp
def mask_mm(mask, x):
    return _mask_dg(mask, x, "nn")


def _mask_fwd(mask, x):
    return _mask_dg(mask, x, "nn"), mask


def _mask_bwd(mask, g):
    return None, _mask_dg(mask, g, "tn")


mask_mm.defvjp(_mask_fwd, _mask_bwd)


def bdot(a, b):
    return jnp.dot(a.astype(bf16), b.astype(bf16), preferred_element_type=f32)


def bdot_nt(a, b):
    return lax.dot_general(a.astype(bf16), b.astype(bf16), _DIMS["nt"], preferred_element_type=f32)


def bdot_tn(a, b):
    return lax.dot_general(a.astype(bf16), b.astype(bf16), _DIMS["tn"], preferred_element_type=f32)


def _iota2(n, m):
    return lax.broadcasted_iota(jnp.int32, (n, m), 0), lax.broadcasted_iota(jnp.int32, (n, m), 1)


def silu(x):
    return x * jax.nn.sigmoid(x)


def softplus(x):
    return jnp.maximum(x, 0.0) + jnp.log(1.0 + jnp.exp(-jnp.abs(x)))


def rmsnorm(x, w):
    return x * lax.rsqrt(jnp.mean(x * x, axis=-1, keepdims=True) + EPS) * w


def hg_masks():
    t, r = _iota2(C, C)
    mats = [r <= t, r > t]
    lvl = []
    for l in range(1, 7):
        sz = 1 << l
        half = sz >> 1
        seg_t = t >> l
        upper_t = (t & (sz - 1)) >= half
        mid_t = seg_t * sz + half - 1
        mats.append((upper_t & (r > mid_t) & (r <= t)) | ((~upper_t) & (r > t) & (r <= mid_t)))
        lvl.append(((seg_t == (r >> l)) & upper_t & ((r & (sz - 1)) < half)).astype(f32))
    stk = jnp.concatenate([m.astype(bf16) for m in mats], axis=0)
    return stk, lvl, (t == r).astype(f32)


def _head(a, h):
    return a[:, h * DK:(h + 1) * DK]


def hg_chunk(St, ps, l0, l1):
    m = jnp.maximum(l0, l1)
    e0 = jnp.exp(l0 - m)
    e1 = jnp.exp(l1 - m)
    lb = e0 / (e0 + e1)
    stk, lvl, eye = hg_masks()
    msk = [eye] + lvl
    qs, ks, vs, qG, kR, eGl = [], [], [], [], [], []
    for p in ps:
        pq, pf, v = p[:, 0:HD], p[:, HD:2 * HD], p[:, 2 * HD:3 * HD]
        q = silu(pq)
        f = lb + (1.0 - lb) * jax.nn.sigmoid(pf)
        k = 1.0 - f
        logf = jnp.log(f)
        Dm = mask_mm(stk, logf)
        ex = [jnp.exp(Dm[(2 + i) * C:(3 + i) * C]) for i in range(6)]
        qs.append([q] + [q * e for e in ex])
        ks.append([k] + [k * e for e in ex])
        vs.append(v)
        qG.append(q * jnp.exp(Dm[0:C]))
        kR.append(k * jnp.exp(Dm[C:2 * C]))
        eGl.append(jnp.exp(jnp.sum(logf, axis=0, keepdims=True)))
    units = [(b, h) for b in range(len(ps)) for h in range(H)]
    parts = [[msk[i] * mmx(_head(qs[b][i], h), _head(ks[b][i], h), "nt", P_HG["lvl"]) for b, h in units]
             for i in range(7)]
    A = [functools.reduce(lambda x, y: x + y, [parts[i][n] for i in range(7)]) for n in range(len(units))]
    qS = [mmx(_head(qG[b], h), St[n], "nt", P_HG["qs"]) for n, (b, h) in enumerate(units)]
    Sn = [St[n] * _head(eGl[b], h) + mmx(_head(vs[b], h), _head(kR[b], h), "tn", P_HG["su"])
          for n, (b, h) in enumerate(units)]
    outs = [mmx(A[n], _head(vs[b], h), "nn", P_HG["av"]) + qS[n] for n, (b, h) in enumerate(units)]
    return tuple(Sn), tuple(jnp.concatenate(outs[b * H:(b + 1) * H], axis=1) for b in range(len(ps)))


@jax.custom_vjp
def use_inverse(A, T):
    return T


def _use_inverse_fwd(A, T):
    return T, T


def _use_inverse_bwd(T, g):
    return -_dg(T, _dg(g, T, "nt", P_GD["inv"]), "tn", P_GD["inv"]), jnp.zeros_like(T)


use_inverse.defvjp(_use_inverse_fwd, _use_inverse_bwd)


def gd_chunk(S, cs, abs_, alog, dtb, t_saved=None):
    t, r = _iota2(C, C)
    tri = (r <= t).astype(bf16)
    ups = (r > t).astype(bf16)
    lane = lax.broadcasted_iota(jnp.int32, (1, DK), 1)
    subl = lax.broadcasted_iota(jnp.int32, (8, 1), 0)
    eye = (t == r).astype(f32)
    strict = (r < t).astype(f32)
    bd = ((t >> 4) == (r >> 4)).astype(f32)
    qa, ka, va, b4, gam4, grev4, gam4T, glast4 = [], [], [], [], [], [], [], []
    for c, ab in zip(cs, abs_):
        qa.append(silu(c[:, 0:HD]))
        ka.append(silu(c[:, HD:2 * HD]))
        va.append(silu(c[:, 2 * HD:3 * HD]))
        g4 = -jnp.exp(alog) * softplus(ab + dtb)
        b4.append(jax.nn.sigmoid(ab))
        gam4.append(mask_mm(tri, g4))
        grev4.append(mask_mm(ups, g4))
        gam4T.append(gam4[-1].T)
        glast4.append(jnp.sum(g4, axis=0, keepdims=True))
    units = [(b, h) for b in range(len(cs)) for h in range(H)]
    nu = range(len(units))
    inv = lambda a, b: [mmx(a[n], b[n], "nn", P_GD["inv"]) for n in nu]
    v = [_head(va[b], h) for b, h in units]
    q = [_head(qa[b], h) for b, h in units]
    k = [_head(ka[b], h) for b, h in units]
    q = [x * lax.rsqrt(jnp.sum(x * x, -1, keepdims=True) + EPS) * (DK ** -0.5) for x in q]
    k = [x * lax.rsqrt(jnp.sum(x * x, -1, keepdims=True) + EPS) for x in k]
    oh = [(lane == h).astype(f32) for h in range(H)]
    gam_c = [jnp.sum(gam4[b] * oh[h], -1, keepdims=True) for b, h in units]
    grev_c = [jnp.sum(grev4[b] * oh[h], -1, keepdims=True) for b, h in units]
    beta = [jnp.sum(b4[b] * (lane == h + H).astype(f32), -1, keepdims=True) for b, h in units]
    glast = [jnp.sum(glast4[b] * oh[h], -1, keepdims=True) for b, h in units]
    gam_r = [jnp.sum(gam4T[b][0:8, :] * (subl == h).astype(f32), axis=0, keepdims=True) for b, h in units]
    dec = [jnp.exp(jnp.where(r <= t, gam_c[n] - gam_r[n], -1e30)) for n in nu]
    egam = [jnp.exp(gam_c[n]) for n in nu]
    kk = [mmx(k[n], k[n], "nt", P_GD["kk"]) for n in nu]
    qk = [mmx(q[n], k[n], "nt", P_GD["qk"]) * dec[n] for n in nu]
    A = [beta[n] * kk[n] * dec[n] * strict for n in nu]
    Dg = [A[n] * bd for n in nu]
    L = [A[n] - Dg[n] for n in nu]
    if t_saved is None:
        ImD = [eye - Dg[n] for n in nu]
        D2 = inv(Dg, Dg)
        P1 = inv(ImD, [eye + x for x in D2])
        D4 = inv(D2, D2)
        P2 = inv(P1, [eye + x for x in D4])
        D8 = inv(D4, D4)
        M = inv(P2, [eye + x for x in D8])
        Nn = inv(M, L)
        N2 = inv(Nn, Nn)
        T1 = inv([eye - x for x in Nn], [eye + x for x in N2])
        Tinv = inv(T1, M)
    else:
        Tinv = [use_inverse(A[n], t_saved[b][:, h * DK:h * DK + C]) for n, (b, h) in enumerate(units)]
    rhs = [jnp.concatenate([beta[n] * v[n], (beta[n] * egam[n]) * k[n]], axis=1) for n in nu]
    sol = [mmx(Tinv[n], rhs[n], "nn", P_GD["sol"]) for n in nu]
    qwS = [mmx(jnp.concatenate([q[n] * egam[n], sol[n][:, DK:2 * DK]], axis=0), S[n], "nn", P_GD["ws"]) for n in nu]
    u = [sol[n][:, 0:DK] - qwS[n][C:2 * C] for n in nu]
    outs = [qwS[n][0:C] + mmx(qk[n], u[n], "nn", P_GD["o"]) for n in nu]
    Sn = [jnp.exp(glast[n]) * S[n] + mmx(k[n] * jnp.exp(grev_c[n]), u[n], "tn", P_GD["su"]) for n in nu]
    zpad = jnp.zeros((C, DK - C), f32)
    t_pack = tuple(jnp.concatenate([x for n in range(b * H, (b + 1) * H) for x in (lax.stop_gradient(Tinv[n]), zpad)],
                                   axis=1) for b in range(len(cs)))
    return tuple(Sn), tuple(jnp.concatenate(outs[b * H:(b + 1) * H], axis=1) for b in range(len(cs))), t_pack


def _in_proj(hflat, norm_w, wbig):
    tm = 384

    def body(h_ref, nw_ref, w_ref, p_ref, ut_ref):
        u = rmsnorm(h_ref[...], nw_ref[...])
        ut_ref[...] = u.T.astype(bf16)
        p_ref[...] = jnp.dot(u.astype(bf16), w_ref[...], preferred_element_type=f32)

    return pl.pallas_call(
        body, name="in_proj", grid=(N // tm,),
        in_specs=[pl.BlockSpec((tm, D), lambda i: (i, 0)), pl.BlockSpec((1, D), lambda i: (0, 0)),
                  pl.BlockSpec((D, PC), lambda i: (0, 0))],
        out_specs=[pl.BlockSpec((tm, PC), lambda i: (i, 0)), pl.BlockSpec((D, tm), lambda i: (0, i))],
        out_shape=[jax.ShapeDtypeStruct((N, PC), f32), jax.ShapeDtypeStruct((D, N), bf16)],
        compiler_params=_cp(("arbitrary",)),
    )(hflat, norm_w, wbig)


NU = NB * H
_REV = lambda c: NCH - 1 - c
_FWD = lambda c: c


def _tok_spec(w, ix, col=0):
    return pl.BlockSpec((NB, C, w), lambda c: (0, ix(c), col))


def _state_spec(ix):
    return pl.BlockSpec((NB, 1, H, DK, DK), lambda c: (0, ix(c), 0, 0, 0))


def _row_spec(w):
    return pl.BlockSpec((1, w), lambda c: (0, 0))


def _rows(ref):
    return tuple(ref[b] for b in range(NB))


def _scan_fwd(chunk_fn, name, x3, extra, extra_specs, n_tok, aux_w):
    ne = len(extra)
    na = 1 if aux_w else 0

    def body(*refs):
        x_ref = refs[0]
        e_refs = refs[1:1 + ne]
        o_ref, s_ref = refs[1 + ne:3 + ne]
        aux_refs = refs[3 + ne:3 + ne + na]
        st = refs[3 + ne + na]

        @pl.when(pl.program_id(0) == 0)
        def _():
            st[...] = jnp.zeros_like(st)

        S = tuple(st[n] for n in range(NU))
        for n in range(NU):
            s_ref[n // H, 0, n % H] = S[n]
        args = [_rows(e) if i < n_tok else e[...] for i, e in enumerate(e_refs)]
        Sn, o, *aux = chunk_fn(S, _rows(x_ref), *args)
        for n in range(NU):
            st[n] = Sn[n]
        for b in range(NB):
            o_ref[b] = o[b]
            for a_ref, a in zip(aux_refs, aux):
                a_ref[b] = a[b]

    return pl.pallas_call(
        body, name=name, grid=(NCH,),
        in_specs=[_tok_spec(3 * HD, _FWD)] + extra_specs(_FWD),
        out_specs=[_tok_spec(HD, _FWD), _state_spec(_FWD)] + [_tok_spec(aux_w, _FWD)] * na,
        out_shape=[jax.ShapeDtypeStruct((NB, TP, HD), f32), jax.ShapeDtypeStruct((NB, NCH, H, DK, DK), f32)]
        + [jax.ShapeDtypeStruct((NB, TP, aux_w), f32)] * na,
        scratch_shapes=[pltpu.VMEM((NU, DK, DK), f32)],
        compiler_params=_cp(("arbitrary",)),
    )(x3, *extra)


def _hg_extra_specs(ix):
    return [_row_spec(HD), _row_spec(HD)]


def _gd_extra_specs(ix):
    return [_tok_spec(DK, ix, COL_AB // DK), _row_spec(DK), _row_spec(DK)]


def _hg_fwd(proj3, l0, l1):
    return _scan_fwd(hg_chunk, "hg_fwd", proj3, [l0, l1], _hg_extra_specs, 0, 0)


def _gd_fwd(cv, proj3, alog, dtb):
    return _scan_fwd(gd_chunk, "gd_fwd", cv, [proj3, alog, dtb], _gd_extra_specs, 1, H * DK)


def _hg_bwd(proj3, l0, l1, s_saved, do):
    def body(p_ref, l0_ref, l1_ref, s_ref, do_ref, dp_ref, dl0_ref, dl1_ref, dst):
        @pl.when(pl.program_id(0) == 0)
        def _():
            dst[...] = jnp.zeros_like(dst)
            dl0_ref[...] = jnp.zeros_like(dl0_ref)
            dl1_ref[...] = jnp.zeros_like(dl1_ref)

        S = tuple(s_ref[n // H, 0, n % H] for n in range(NU))
        _, vjp = jax.vjp(hg_chunk, S, _rows(p_ref), l0_ref[...], l1_ref[...])
        dS, dp, dl0, dl1 = vjp((tuple(dst[n] for n in range(NU)), _rows(do_ref)))
        for n in range(NU):
            dst[n] = dS[n]
        for b in range(NB):
            dp_ref[b] = dp[b].astype(bf16)
        dl0_ref[...] += jnp.broadcast_to(dl0, (8, HD))
        dl1_ref[...] += jnp.broadcast_to(dl1, (8, HD))

    acc = pl.BlockSpec((8, HD), lambda c: (0, 0))
    return pl.pallas_call(
        body, name="hg_bwd", grid=(NCH,),
        in_specs=[_tok_spec(3 * HD, _REV)] + _hg_extra_specs(_REV) + [_state_spec(_REV), _tok_spec(HD, _REV)],
        out_specs=[_tok_spec(3 * HD, _REV), acc, acc],
        out_shape=[jax.ShapeDtypeStruct((NB, TP, 3 * HD), bf16), jax.ShapeDtypeStruct((8, HD), f32),
                   jax.ShapeDtypeStruct((8, HD), f32)],
        scratch_shapes=[pltpu.VMEM((NU, DK, DK), f32)],
        compiler_params=_cp(("arbitrary",)),
    )(proj3, l0, l1, s_saved, do)


def _gd_bwd(cv, proj3, alog, dtb, s_saved, t_saved, do):
    def body(c_ref, ab_ref, al_ref, db_ref, s_ref, t_ref, do_ref, dc_ref, dab_ref, dal_ref, ddb_ref, dst):
        @pl.when(pl.program_id(0) == 0)
        def _():
            dst[...] = jnp.zeros_like(dst)
            dal_ref[...] = jnp.zeros_like(dal_ref)
            ddb_ref[...] = jnp.zeros_like(ddb_ref)

        S = tuple(s_ref[n // H, 0, n % H] for n in range(NU))
        t_rows = _rows(t_ref)
        fn = lambda *a: gd_chunk(*a, t_saved=t_rows)[0:2]
        _, vjp = jax.vjp(fn, S, _rows(c_ref), _rows(ab_ref), al_ref[...], db_ref[...])
        dS, dc, dab, dal, ddb = vjp((tuple(dst[n] for n in range(NU)), _rows(do_ref)))
        for n in range(NU):
            dst[n] = dS[n]
        for b in range(NB):
            dc_ref[b] = dc[b]
            dab_ref[b] = dab[b].astype(bf16)
        dal_ref[...] += jnp.broadcast_to(dal, (8, DK))
        ddb_ref[...] += jnp.broadcast_to(ddb, (8, DK))

    acc = pl.BlockSpec((8, DK), lambda c: (0, 0))
    return pl.pallas_call(
        body, name="gd_bwd", grid=(NCH,),
        in_specs=[_tok_spec(3 * HD, _REV)] + _gd_extra_specs(_REV)
        + [_state_spec(_REV), _tok_spec(HD, _REV), _tok_spec(HD, _REV)],
        out_specs=[_tok_spec(3 * HD, _REV), _tok_spec(DK, _REV), acc, acc],
        out_shape=[jax.ShapeDtypeStruct((NB, TP, 3 * HD), f32), jax.ShapeDtypeStruct((NB, TP, DK), bf16),
                   jax.ShapeDtypeStruct((8, DK), f32), jax.ShapeDtypeStruct((8, DK), f32)],
        scratch_shapes=[pltpu.VMEM((NU, DK, DK), f32)],
        compiler_params=_cp(("arbitrary",)),
    )(cv, proj3, alog, dtb, s_saved, t_saved, do)


def _conv_fwd(proj3, conv4):
    def body(x_ref, w_ref, y_ref):
        x = x_ref[0]
        row = lax.broadcasted_iota(jnp.int32, (TP, 1), 0)
        y = w_ref[3] * x
        for s in (1, 2, 3):
            y = y + w_ref[3 - s] * jnp.where(row >= s, pltpu.roll(x, s, 0), 0.0)
        y_ref[0] = y

    return pl.pallas_call(
        body, name="conv_fwd", grid=(NB, 3),
        in_specs=[pl.BlockSpec((1, TP, HD), lambda b, j: (b, 0, COL_QKV // HD + j)),
                  pl.BlockSpec((4, 1, HD), lambda b, j: (0, 0, j))],
        out_specs=pl.BlockSpec((1, TP, HD), lambda b, j: (b, 0, j)),
        out_shape=jax.ShapeDtypeStruct((NB, TP, 3 * HD), f32),
        compiler_params=_cp(("arbitrary", "arbitrary")),
    )(proj3, conv4)


def _conv_bwd(proj3, conv4, dy):
    def body(x_ref, w_ref, dy_ref, dx_ref, dw_ref):
        @pl.when(pl.program_id(1) == 0)
        def _():
            dw_ref[...] = jnp.zeros_like(dw_ref)

        x = x_ref[0]
        g = dy_ref[0]
        row = lax.broadcasted_iota(jnp.int32, (TP, 1), 0)
        dx = w_ref[3] * g
        dw_ref[3] += jnp.broadcast_to(jnp.sum(x * g, axis=0, keepdims=True), (8, HD))
        for s in (1, 2, 3):
            dx = dx + w_ref[3 - s] * jnp.where(row < TP - s, pltpu.roll(g, TP - s, 0), 0.0)
            xs = jnp.where(row >= s, pltpu.roll(x, s, 0), 0.0)
            dw_ref[3 - s] += jnp.broadcast_to(jnp.sum(xs * g, axis=0, keepdims=True), (8, HD))
        dx_ref[0] = dx.astype(bf16)

    return pl.pallas_call(
        body, name="conv_bwd", grid=(3, NB),
        in_specs=[pl.BlockSpec((1, TP, HD), lambda j, b: (b, 0, COL_QKV // HD + j)),
                  pl.BlockSpec((4, 1, HD), lambda j, b: (0, 0, j)), pl.BlockSpec((1, TP, HD), lambda j, b: (b, 0, j))],
        out_specs=[pl.BlockSpec((1, TP, HD), lambda j, b: (b, 0, j)), pl.BlockSpec((4, 8, HD), lambda j, b: (0, 0, j))],
        out_shape=[jax.ShapeDtypeStruct((NB, TP, 3 * HD), bf16), jax.ShapeDtypeStruct((4, 8, 3 * HD), f32)],
        compiler_params=_cp(("arbitrary", "arbitrary")),
    )(proj3, conv4, dy)


def _gated(o, z, w):
    outs = []
    for hh in range(H):
        sl = slice(hh * DK, (hh + 1) * DK)
        outs.append(rmsnorm(o[:, sl], w) * silu(z[:, sl]))
    return jnp.concatenate(outs, axis=-1)


def _out_loss(o_hg, o_gd, proj, hgw, gdw, wout, hflat, fw, target):
    tm = 384

    def body(ohg_ref, ogd_ref, zhg_ref, zgd_ref, hgw_ref, gdw_ref, wo_ref, h_ref, fw_ref, tg_ref,
             loss_ref, dohg_ref, dogd_ref, dzhg_ref, dzgd_ref, dh_ref, dwo_ref, dhgw_ref, dgdw_ref, dfw_ref):
        i = pl.program_id(0)

        @pl.when(i == 0)
        def _():
            for r in (loss_ref, dwo_ref, dhgw_ref, dgdw_ref, dfw_ref):
                r[...] = jnp.zeros_like(r)

        row = i * tm + lax.broadcasted_iota(jnp.int32, (tm, 1), 0)
        tok = jnp.where(row >= TP, row - TP, row)
        valid = (tok >= PAD + N_META).astype(f32)
        hval = h_ref[...]
        tgt = tg_ref[...]

        def mix(ohg, ogd, zhg, zgd, w1, w2):
            return jnp.concatenate([_gated(ohg, zhg, w1), _gated(ogd, zgd, w2)], axis=-1)

        y, vjp_mix = jax.vjp(mix, ohg_ref[...], ogd_ref[...], zhg_ref[...], zgd_ref[...], hgw_ref[...], gdw_ref[...])
        out = bdot(y, wo_ref[...])

        def head(out, fwv):
            err = (rmsnorm(hval + out, fwv) - tgt) * valid
            return 0.5 * jnp.sum(jnp.mean(err * err, axis=-1, keepdims=True))

        loss, vjp_head = jax.vjp(head, out, fw_ref[...])
        dout, dfw = vjp_head(jnp.ones((), f32))
        dh_ref[...] = dout
        dy = bdot_nt(dout, wo_ref[...])
        dwo_ref[...] += bdot_tn(y, dout)
        dohg, dogd, dzhg, dzgd, dw1, dw2 = vjp_mix(dy)
        dohg_ref[...] = dohg
        dogd_ref[...] = dogd
        dzhg_ref[...] = dzhg.astype(bf16)
        dzgd_ref[...] = dzgd.astype(bf16)
        loss_ref[...] += jnp.broadcast_to(loss, (8, DK))
        dhgw_ref[...] += jnp.broadcast_to(dw1, (8, DK))
        dgdw_ref[...] += jnp.broadcast_to(dw2, (8, DK))
        dfw_ref[...] += jnp.broadcast_to(dfw, (8, D))

    row = lambda w: pl.BlockSpec((tm, w), lambda i: (i, 0))
    whole = lambda r, w: pl.BlockSpec((r, w), lambda i: (0, 0))
    col = lambda c0: pl.BlockSpec((tm, HD), lambda i: (i, c0 // HD))
    return pl.pallas_call(
        body, name="out_loss", grid=(N // tm,),
        in_specs=[row(HD), row(HD), col(COL_ZHG), col(COL_ZGD),
                  whole(1, DK), whole(1, DK), whole(D, D), row(D), whole(1, D), row(D)],
        out_specs=[whole(8, DK), row(HD), row(HD), row(HD), row(HD), row(D), whole(D, D),
                   whole(8, DK), whole(8, DK), whole(8, D)],
        out_shape=[jax.ShapeDtypeStruct((8, DK), f32)] + [jax.ShapeDtypeStruct((N, HD), f32)] * 2
        + [jax.ShapeDtypeStruct((N, HD), bf16)] * 2
        + [jax.ShapeDtypeStruct((N, D), f32), jax.ShapeDtypeStruct((D, D), f32),
           jax.ShapeDtypeStruct((8, DK), f32), jax.ShapeDtypeStruct((8, DK), f32), jax.ShapeDtypeStruct((8, D), f32)],
        compiler_params=_cp(("arbitrary",)),
    )(o_hg, o_gd, proj, proj, hgw, gdw, wout, hflat, fw, target)


def _in_bwd(pieces, wbig, hflat, norm_w, dh_res, pbs):
    tm = 384
    nsteps = N // tm
    np_ = len(pieces)
    na = len(pbs)
    offs = [c0 for _, c0 in pieces]
    widths = [d.shape[1] for d, _ in pieces]

    def body(*refs):
        d_refs = refs[:np_]
        w_ref, h_ref, nw_ref, dhr_ref = refs[np_:np_ + 4]
        srcs = refs[np_ + 4:np_ + 4 + na]
        dh_ref, dnw_ref = refs[np_ + 4 + na:np_ + 6 + na]
        dsts = refs[np_ + 6 + na:np_ + 6 + 2 * na]
        sems = refs[np_ + 6 + 2 * na:]
        i = pl.program_id(0)

        def copies():
            if not na:
                return []
            x, y, c, chips = _place()
            return [pltpu.make_async_remote_copy(
                src_ref=srcs[a].at[2 * px + py], dst_ref=dsts[a].at[j], send_sem=sems[0].at[na * j + a],
                recv_sem=sems[1].at[na * j + a], device_id=(px, py, c), device_id_type=MESH)
                for j, (px, py) in enumerate(chips) for a in range(na)]

        @pl.when(i == 0)
        def _():
            dnw_ref[...] = jnp.zeros_like(dnw_ref)
            for cp in copies():
                cp.start()

        du = jnp.zeros((tm, D), f32)
        for d_ref, off, wd in zip(d_refs, offs, widths):
            du = du + bdot_nt(d_ref[...], w_ref[:, off:off + wd])
        _, vjp = jax.vjp(rmsnorm, h_ref[...], nw_ref[...])
        dh, dnw = vjp(du)
        dh_ref[...] = dh + dhr_ref[...]
        dnw_ref[...] += jnp.broadcast_to(dnw, (8, D))

        @pl.when(i == nsteps - 1)
        def _():
            for cp in copies():
                cp.wait()

    row = lambda w: pl.BlockSpec((tm, w), lambda i: (i, 0))
    return pl.pallas_call(
        body, name="in_bwd", grid=(nsteps,),
        in_specs=[row(w) for w in widths]
        + [pl.BlockSpec((D, PC), lambda i: (0, 0)), row(D), pl.BlockSpec((1, D), lambda i: (0, 0)), row(D)] + [ANY] * na,
        out_specs=[row(D), pl.BlockSpec((8, D), lambda i: (0, 0))] + [ANY] * na,
        out_shape=[jax.ShapeDtypeStruct((N, D), f32), jax.ShapeDtypeStruct((8, D), f32)]
        + [jax.ShapeDtypeStruct((3,) + p.shape[1:], p.dtype) for p in pbs],
        scratch_shapes=[pltpu.SemaphoreType.DMA((3 * na,)), pltpu.SemaphoreType.DMA((3 * na,))] if na else [],
        compiler_params=_cp(("arbitrary",)),
    )(*[d for d, _ in pieces], wbig, hflat, norm_w, dh_res, *pbs)


def _w_grad(ut, d, name):
    n = d.shape[1]
    tk = N // 3

    def body(u_ref, d_ref, o_ref):
        @pl.when(pl.program_id(0) == 0)
        def _():
            o_ref[...] = jnp.zeros_like(o_ref)

        o_ref[...] += jnp.dot(u_ref[...], d_ref[...], preferred_element_type=f32)

    return pl.pallas_call(
        body, name=name, grid=(N // tk,),
        in_specs=[pl.BlockSpec((D, tk), lambda k: (0, k)), pl.BlockSpec((tk, n), lambda k: (k, 0))],
        out_specs=pl.BlockSpec((D, n), lambda k: (0, 0)),
        out_shape=jax.ShapeDtypeStruct((D, n), f32),
        compiler_params=_cp(("arbitrary",)),
    )(ut, d)


def _adam_math(g, w, m, v):
    m2 = ADAM_B1 * m + (1.0 - ADAM_B1) * g
    v2 = ADAM_B2 * v + (1.0 - ADAM_B2) * (g * g)
    m_hat = m2 / (1.0 - ADAM_B1 ** ADAM_STEP)
    v_hat = v2 / (1.0 - ADAM_B2 ** ADAM_STEP)
    delta = -ADAM_LR * (m_hat / (jnp.sqrt(v_hat) + ADAM_EPS) + ADAM_WD * w)
    return delta, m2, v2


def _adamw(gs, w, m, v, name):
    R, Cc = w.shape
    tr = 256 if R % 256 == 0 else R
    ng = len(gs)

    def body(*refs):
        g = refs[0][...]
        for r in refs[1:ng]:
            g = g + r[...]
        w_ref, m_ref, v_ref, g_ref, d_ref, m2_ref, v2_ref = refs[ng:]
        delta, m2, v2 = _adam_math(g, w_ref[...], m_ref[...], v_ref[...])
        g_ref[...] = g
        d_ref[...] = delta
        m2_ref[...] = m2
        v2_ref[...] = v2

    spec = pl.BlockSpec((tr, Cc), lambda i: (i, 0))
    return pl.pallas_call(
        body, name=name, grid=(R // tr,),
        in_specs=[spec] * (ng + 3), out_specs=[spec] * 4,
        out_shape=[jax.ShapeDtypeStruct((R, Cc), f32)] * 4,
        compiler_params=_cp(("arbitrary",)),
    )(*gs, w, m, v)


def _sum_slots(r, name):
    S, R, Cc = r.shape
    tr = 256 if R % 256 == 0 else R

    def body(r_ref, o_ref):
        acc = r_ref[0]
        for s in range(1, S):
            acc = acc + r_ref[s]
        o_ref[...] = acc

    return pl.pallas_call(
        body, name=name, grid=(R // tr,),
        in_specs=[pl.BlockSpec((S, tr, Cc), lambda i: (0, i, 0))], out_specs=pl.BlockSpec((tr, Cc), lambda i: (i, 0)),
        out_shape=jax.ShapeDtypeStruct((R, Cc), f32),
        compiler_params=_cp(("arbitrary",)),
    )(r)


def _place():
    x, y, c = lax.axis_index("x"), lax.axis_index("y"), lax.axis_index("c")
    return x, y, c, [(1 - x, y), (x, 1 - y), (1 - x, 1 - y)]


def _gather_weights(halved, whole):
    nh, nw = len(halved), len(whole)
    na = nh + nw

    def body(*refs):
        srcs, dsts = refs[:na], refs[na:2 * na]
        send_sems, recv_sems, loc_sems = refs[2 * na:2 * na + 3]
        stage = refs[2 * na + 3:]
        x, y, c, chips = _place()
        me = 2 * x + y
        loads = [pltpu.make_async_copy(s, v, loc_sems.at[i]) for i, (s, v) in enumerate(zip(srcs, stage))]
        locs = [pltpu.make_async_copy(v, d.at[me], loc_sems.at[i]) for i, (v, d) in enumerate(zip(stage, dsts))]
        for cp in loads:
            cp.start()

        def ici(j, i, slot):
            px, py = chips[j]
            src = srcs[i].at[c] if i < nh else srcs[i]
            dst = dsts[i].at[slot, c] if i < nh else dsts[i].at[slot]
            return pltpu.make_async_remote_copy(
                src_ref=src, dst_ref=dst, send_sem=send_sems.at[na * j + i], recv_sem=recv_sems.at[na * j + i],
                device_id=(px, py, c), device_id_type=MESH)

        def d2d(j, i, half):
            px, py = chips[j]
            blk = dsts[i].at[2 * px + py, half]
            return pltpu.make_async_remote_copy(
                src_ref=blk, dst_ref=blk, send_sem=send_sems.at[3 * na + nh * j + i],
                recv_sem=recv_sems.at[3 * na + nh * j + i], device_id=(x, y, 1 - c), device_id_type=MESH)

        sends = [ici(j, i, me) for j in range(3) for i in range(na)]
        for cp in sends:
            cp.start()
        for ld, st in zip(loads, locs):
            ld.wait()
            st.start()
        for j, (px, py) in enumerate(chips):
            for i in range(na):
                ici(j, i, 2 * px + py).wait_recv()
                if i < nh:
                    fwd = d2d(j, i, c)
                    fwd.start()
                    sends.append(fwd)
        for j in range(3):
            for i in range(nh):
                d2d(j, i, 1 - c).wait_recv()
        for cp in sends:
            cp.wait_send()
        for cp in locs:
            cp.wait()

    nsem = 3 * na + 3 * nh
    return pl.pallas_call(
        body, name="gather_weights",
        in_specs=[ANY] * na, out_specs=[ANY] * na,
        out_shape=[jax.ShapeDtypeStruct((4,) + s.shape, s.dtype) for s in list(halved) + list(whole)],
        scratch_shapes=[pltpu.SemaphoreType.DMA((nsem,)), pltpu.SemaphoreType.DMA((nsem,)),
                        pltpu.SemaphoreType.DMA((na,))] + [pltpu.VMEM(s.shape, s.dtype) for s in list(halved) + list(whole)],
        compiler_params=pltpu.CompilerParams(has_side_effects=True, vmem_limit_bytes=VMEM_LIMIT),
    )(*halved, *whole)


def _swap_halves(gs):
    na = len(gs)

    def body(*refs):
        srcs, dsts = refs[:na], refs[na:2 * na]
        send_sems, recv_sems = refs[2 * na:]
        x, y, c, _ = _place()
        cps = [pltpu.make_async_remote_copy(
            src_ref=srcs[i].at[q, 1 - c], dst_ref=dsts[i].at[q], send_sem=send_sems.at[4 * i + q],
            recv_sem=recv_sems.at[4 * i + q], device_id=(x, y, 1 - c), device_id_type=MESH)
            for i in range(na) for q in range(4)]
        for cp in cps:
            cp.start()
        for cp in cps:
            cp.wait()

    return pl.pallas_call(
        body, name="swap_halves",
        in_specs=[ANY] * na, out_specs=[ANY] * na,
        out_shape=[jax.ShapeDtypeStruct((4,) + g.shape[2:], g.dtype) for g in gs],
        scratch_shapes=[pltpu.SemaphoreType.DMA((4 * na,)), pltpu.SemaphoreType.DMA((4 * na,))],
        compiler_params=pltpu.CompilerParams(has_side_effects=True),
    )(*gs)


def _add_halves(c_arr, g, s, name):
    _, _, R, Cc = g.shape
    tr = min(R, 256)

    def body(c_ref, g_ref, s_ref, b_ref, f_ref):
        p = g_ref[0, 0] + s_ref[0]
        f_ref[0] = p
        b_ref[0] = p.astype(bf16)

    blk = pl.BlockSpec((1, tr, Cc), lambda q, i, cr: (q, i, 0))
    return pl.pallas_call(
        body, name=name,
        grid_spec=pltpu.PrefetchScalarGridSpec(
            num_scalar_prefetch=1, grid=(4, R // tr),
            in_specs=[pl.BlockSpec((1, 1, tr, Cc), lambda q, i, cr: (q, cr[0], i, 0)), blk], out_specs=[blk, blk]),
        out_shape=[jax.ShapeDtypeStruct((4, R, Cc), bf16), jax.ShapeDtypeStruct((4, R, Cc), f32)],
        compiler_params=_cp(("arbitrary", "arbitrary")),
    )(c_arr, g, s)


_FLIPS = [(fx, fy, fc) for fx in (0, 1) for fy in (0, 1) for fc in (0, 1)][1:]


def _scatter_blocks(pbs, pack):
    na = len(pbs)
    R = pack.shape[0]

    def body(*refs):
        srcs, pk = refs[:na], refs[na]
        dsts, rp = refs[na + 1:2 * na + 1], refs[2 * na + 1]
        send_sems, recv_sems = refs[2 * na + 2:]
        x, y, c, chips = _place()

        def big(j, i):
            px, py = chips[j]
            return pltpu.make_async_remote_copy(
                src_ref=srcs[i].at[2 * px + py], dst_ref=dsts[i].at[j], send_sem=send_sems.at[na * j + i],
                recv_sem=recv_sems.at[na * j + i], device_id=(px, py, c), device_id_type=MESH)

        def small(k):
            fx, fy, fc = _FLIPS[k]
            return pltpu.make_async_remote_copy(
                src_ref=pk, dst_ref=rp.at[k], send_sem=send_sems.at[3 * na + k], recv_sem=recv_sems.at[3 * na + k],
                device_id=(x ^ fx, y ^ fy, c ^ fc), device_id_type=MESH)

        cps = [big(j, i) for j in range(3) for i in range(na)] + [small(k) for k in range(7)]
        for cp in cps:
            cp.start()
        for cp in cps:
            cp.wait()

    nsem = 3 * na + 7
    return pl.pallas_call(
        body, name="scatter_blocks",
        in_specs=[ANY] * (na + 1), out_specs=[ANY] * (na + 1),
        out_shape=[jax.ShapeDtypeStruct((3,) + p.shape[1:], p.dtype) for p in pbs]
        + [jax.ShapeDtypeStruct((7, R, 128), f32)],
        scratch_shapes=[pltpu.SemaphoreType.DMA((nsem,)), pltpu.SemaphoreType.DMA((nsem,))],
        compiler_params=pltpu.CompilerParams(has_side_effects=True),
    )(*pbs, pack)


def _sum_blocks(me_arr, pf, r, name):
    _, R, Cc = pf.shape
    tr = min(R, 256)

    def body(me_ref, pf_ref, r_ref, o_ref):
        acc = pf_ref[0]
        for j in range(3):
            acc = acc + r_ref[j].astype(f32)
        o_ref[...] = acc

    return pl.pallas_call(
        body, name=name,
        grid_spec=pltpu.PrefetchScalarGridSpec(
            num_scalar_prefetch=1, grid=(R // tr,),
            in_specs=[pl.BlockSpec((1, tr, Cc), lambda i, mr: (mr[0], i, 0)),
                      pl.BlockSpec((3, tr, Cc), lambda i, mr: (0, i, 0))],
            out_specs=pl.BlockSpec((tr, Cc), lambda i, mr: (i, 0))),
        out_shape=jax.ShapeDtypeStruct((R, Cc), f32),
        compiler_params=_cp(("arbitrary",)),
    )(me_arr, pf, r)


def _sum_packs(me8_arr, pack, rp):
    R = pack.shape[0]

    def body(me_ref, pk_ref, rp_ref, o_ref):
        me8 = me_ref[0]
        acc = None
        for d in range(8):
            rel = d ^ me8
            term = jnp.where(rel == 0, pk_ref[...], rp_ref[jnp.maximum(rel - 1, 0)])
            acc = term if acc is None else acc + term
        o_ref[...] = acc

    return pl.pallas_call(
        body, name="sum_packs",
        grid_spec=pltpu.PrefetchScalarGridSpec(
            num_scalar_prefetch=1, grid=(1,),
            in_specs=[pl.BlockSpec((R, 128), lambda i, mr: (0, 0)), pl.BlockSpec((7, R, 128), lambda i, mr: (0, 0, 0))],
            out_specs=pl.BlockSpec((R, 128), lambda i, mr: (0, 0))),
        out_shape=jax.ShapeDtypeStruct((R, 128), f32),
        compiler_params=_cp(("arbitrary",)),
    )(me8_arr, pack, rp)


def _swap_finished(fs):
    na = len(fs)

    def body(*refs):
        srcs, dsts = refs[:na], refs[na:2 * na]
        send_sems, recv_sems = refs[2 * na:]
        x, y, c, _ = _place()
        cps = [pltpu.make_async_remote_copy(
            src_ref=srcs[i], dst_ref=dsts[i], send_sem=send_sems.at[i], recv_sem=recv_sems.at[i],
            device_id=(x, y, 1 - c), device_id_type=MESH) for i in range(na)]
        for cp in cps:
            cp.start()
        for cp in cps:
            cp.wait()

    return pl.pallas_call(
        body, name="swap_finished",
        in_specs=[ANY] * na, out_specs=[ANY] * na,
        out_shape=[jax.ShapeDtypeStruct(f.shape, f.dtype) for f in fs],
        scratch_shapes=[pltpu.SemaphoreType.DMA((na,)), pltpu.SemaphoreType.DMA((na,))],
        compiler_params=pltpu.CompilerParams(has_side_effects=True),
    )(*fs)


def _adamw_halves(c_arr, mine, peer, w, m, v, name):
    _, R, Cc = w.shape
    tr = min(R, 256)

    def body(c_ref, mine_ref, peer_ref, w_ref, m_ref, v_ref, g_ref, d_ref, m2_ref, v2_ref):
        g = jnp.where(pl.program_id(0) == c_ref[0], mine_ref[...], peer_ref[...])
        delta, m2, v2 = _adam_math(g, w_ref[0], m_ref[0], v_ref[0])
        g_ref[0] = g
        d_ref[0] = delta
        m2_ref[0] = m2
        v2_ref[0] = v2

    half = pl.BlockSpec((tr, Cc), lambda hh, i, cr: (i, 0))
    full = pl.BlockSpec((1, tr, Cc), lambda hh, i, cr: (hh, i, 0))
    return pl.pallas_call(
        body, name=name,
        grid_spec=pltpu.PrefetchScalarGridSpec(
            num_scalar_prefetch=1, grid=(2, R // tr), in_specs=[half, half, full, full, full], out_specs=[full] * 4),
        out_shape=[jax.ShapeDtypeStruct((2, R, Cc), f32)] * 4,
        compiler_params=_cp(("arbitrary", "arbitrary")),
    )(c_arr, mine, peer, w, m, v)


def _rows8(a):
    flat = a.reshape(-1)
    n = flat.shape[0]
    rows = -(-n // 1024) * 8
    return jnp.pad(flat, (0, rows * 128 - n)).reshape(rows, 128)


def kernel(x, meta_tokens, norm_w, w_in, conv_w, hg_lb_logits, hg_norm_w, gdn_A_log, gdn_dt_bias, gdn_norm_w, w_out, final_norm_w, loss_target, m_meta_tokens, m_norm_w, m_w_in, m_conv_w, m_hg_lb_logits, m_hg_norm_w, m_gdn_A_log, m_gdn_dt_bias, m_gdn_norm_w, m_w_out, m_final_norm_w, v_meta_tokens, v_norm_w, v_w_in, v_conv_w, v_hg_lb_logits, v_hg_norm_w, v_gdn_A_log, v_gdn_dt_bias, v_gdn_norm_w, v_w_out, v_final_norm_w):
    me = 2 * lax.axis_index("x") + lax.axis_index("y")

    g_win, g_wout, g_conv, g_meta = _gather_weights(
        [w_in[0].astype(bf16).reshape(2, D // 2, SHARD_COLS), w_out[0].astype(bf16).reshape(2, D // 8, D)],
        [conv_w[0], meta_tokens])
    w_full = jnp.transpose(g_win.reshape(4, D, SHARD_COLS), (1, 0, 2)).reshape(D, IN_COLS)
    wbig = jnp.pad(w_full, ((0, 0), (0, PC - IN_COLS)))
    wout_full = g_wout.reshape(D, D)
    conv4 = jnp.transpose(g_conv, (1, 0, 2)).reshape(4, 1, 3 * HD)
    meta_full = jnp.transpose(g_meta, (1, 0, 2)).reshape(N_META, D)

    c_arr = lax.axis_index("c").reshape(1).astype(jnp.int32)

    def chip_partials(gw_full, g_wout_part):
        g_in4 = jnp.transpose(gw_full.reshape(D, 4, SHARD_COLS), (1, 0, 2)).reshape(4, 2, D // 2, SHARD_COLS)
        g_out4 = g_wout_part.reshape(4, 2, D // 8, D)
        s_in, s_out = _swap_halves([g_in4, g_out4])
        pb_in, pf_in = _add_halves(c_arr, g_in4, s_in, "add_w_in")
        pb_out, pf_out = _add_halves(c_arr, g_out4, s_out, "add_w_out")
        return [pb_in, pb_out], [pf_in, pf_out]

    (loss8, grad_x, d_meta, d_nw, d_conv, d_lb, d_hgw, d_alog, d_dtb, d_gdw, d_fw, pfs, rs) = _local_step(
        x, loss_target, wbig, wout_full, conv4, meta_full, norm_w, hg_lb_logits, hg_norm_w, gdn_A_log, gdn_dt_bias,
        gdn_norm_w, final_norm_w, chip_partials)

    pack = jnp.concatenate([
        loss8, d_nw[0].reshape(8, 128), d_lb.reshape(8, 128), d_hgw, _rows8(d_alog[0, :H]), _rows8(d_dtb[0, :H]),
        d_gdw, d_fw[0].reshape(8, 128), d_meta.reshape(128, 128), d_conv.reshape(48, 128)], axis=0)
    return _reduce_and_update(
        me, c_arr, grad_x, pfs, rs, pack, meta_tokens, norm_w, w_in, conv_w, hg_lb_logits, hg_norm_w, gdn_A_log,
        gdn_dt_bias, gdn_norm_w, w_out, final_norm_w, m_meta_tokens, m_norm_w, m_w_in, m_conv_w, m_hg_lb_logits,
        m_hg_norm_w, m_gdn_A_log, m_gdn_dt_bias, m_gdn_norm_w, m_w_out, m_final_norm_w, v_meta_tokens, v_norm_w, v_w_in,
        v_conv_w, v_hg_lb_logits, v_hg_norm_w, v_gdn_A_log, v_gdn_dt_bias, v_gdn_norm_w, v_w_out, v_final_norm_w)


def _local_step(x, loss_target, wbig, wout_full, conv4, meta_full, norm_w, hg_lb_logits, hg_norm_w, gdn_A_log, gdn_dt_bias,
                gdn_norm_w, final_norm_w, chip_partials):
    h3 = jnp.concatenate([jnp.zeros((NB, PAD, D), f32), jnp.broadcast_to(meta_full[None], (NB, N_META, D)), x], axis=1)
    hflat = h3.reshape(N, D)
    target = jnp.pad(loss_target, ((0, 0), (PAD + N_META, 0), (0, 0))).reshape(N, D)
    l0, l1 = hg_lb_logits[0:1], hg_lb_logits[1:2]
    alog = jnp.pad(gdn_A_log, ((0, 0), (0, DK - H)))
    dtb = jnp.pad(gdn_dt_bias, ((0, 0), (0, DK - H)))
    fw = final_norm_w.reshape(1, D)

    proj, ut = _in_proj(hflat, norm_w, wbig)
    proj3 = proj.reshape(NB, TP, PC)
    o_hg, s_hg = _hg_fwd(proj3, l0, l1)
    cv = _conv_fwd(proj3, conv4)
    o_gd, s_gd, t_gd = _gd_fwd(cv, proj3, alog, dtb)
    (loss8, d_ohg, d_ogd, d_zhg, d_zgd, dh_res, g_wout_part, d_hgw, d_gdw, d_fw) = _out_loss(
        o_hg.reshape(N, HD), o_gd.reshape(N, HD), proj, hg_norm_w, gdn_norm_w, wout_full, hflat, fw, target)
    d_hg, d_l0, d_l1 = _hg_bwd(proj3, l0, l1, s_hg, d_ohg.reshape(NB, TP, HD))
    d_cv, d_ab, d_alog, d_dtb = _gd_bwd(cv, proj3, alog, dtb, s_gd, t_gd, d_ogd.reshape(NB, TP, HD))
    d_qkv, d_conv4 = _conv_bwd(proj3, conv4, d_cv)
    d_hg2, d_qkv2, d_ab2 = d_hg.reshape(N, 3 * HD), d_qkv.reshape(N, 3 * HD), d_ab.reshape(N, DK)
    pieces = [(d_hg2, COL_HG), (d_zhg, COL_ZHG), (d_qkv2, COL_QKV), (d_zgd, COL_ZGD), (d_ab2, COL_AB)]
    gws = [_w_grad(ut, d, "w_grad_%d" % c0) for d, c0 in pieces]
    gw_full = jnp.concatenate(gws[:4] + [gws[4][:, 0:IN_COLS - COL_AB]], axis=1)
    pbs, pfs = chip_partials(gw_full, g_wout_part) if chip_partials else ([], [gw_full, g_wout_part])
    dh, d_nw, *rs = _in_bwd(pieces, wbig, hflat, norm_w, dh_res, pbs)

    dh3 = dh.reshape(NB, TP, D)
    grad_x = dh3[:, PAD + N_META:, :]
    d_meta = jnp.sum(dh3[:, PAD:PAD + N_META, :], axis=0)
    d_conv = d_conv4[:, 0, :]
    d_lb = jnp.concatenate([d_l0[0:1], d_l1[0:1]], axis=0)
    return loss8, grad_x, d_meta, d_nw, d_conv, d_lb, d_hgw, d_alog, d_dtb, d_gdw, d_fw, pfs, rs


def _reduce_and_update(me, c_arr, grad_x, pfs, rs, pack, meta_tokens, norm_w, w_in, conv_w, hg_lb_logits, hg_norm_w,
                       gdn_A_log, gdn_dt_bias, gdn_norm_w, w_out, final_norm_w, m_meta_tokens, m_norm_w, m_w_in, m_conv_w,
                       m_hg_lb_logits, m_hg_norm_w, m_gdn_A_log, m_gdn_dt_bias, m_gdn_norm_w, m_w_out, m_final_norm_w,
                       v_meta_tokens, v_norm_w, v_w_in, v_conv_w, v_hg_lb_logits, v_hg_norm_w, v_gdn_A_log, v_gdn_dt_bias,
                       v_gdn_norm_w, v_w_out, v_final_norm_w):
    me_arr = me.reshape(1).astype(jnp.int32)
    (pf_in, pf_out), (r_in, r_out) = pfs, rs
    (r_pack,) = _scatter_blocks([], pack)
    f_in = _sum_blocks(me_arr, pf_in, r_in, "sum_w_in")
    f_out = _sum_blocks(me_arr, pf_out, r_out, "sum_w_out")
    o_in, o_out = _swap_finished([f_in, f_out])
    me8_arr = (2 * me + lax.axis_index("c")).reshape(1).astype(jnp.int32)
    small = _sum_packs(me8_arr, pack, r_pack)

    half_in = lambda a: a[0].reshape(2, D // 2, SHARD_COLS)
    half_out = lambda a: a[0].reshape(2, D // 8, D)
    gi, di, mi, vi = [a.reshape(D, SHARD_COLS) for a in _adamw_halves(
        c_arr, f_in, o_in, half_in(w_in), half_in(m_w_in), half_in(v_w_in), "adamw_w_in")]
    go, do_, mo, vo = [a.reshape(D // 4, D) for a in _adamw_halves(
        c_arr, f_out, o_out, half_out(w_out), half_out(m_w_out), half_out(v_w_out), "adamw_w_out")]

    g_meta_full = small[64:192].reshape(N_META, D)
    g_meta_loc = lax.dynamic_slice(g_meta_full, (0, me * 256), (N_META, 256))
    gm, dm, mm_, vm = _adamw([g_meta_loc], meta_tokens, m_meta_tokens, v_meta_tokens, "adamw_meta")
    g_conv_full = small[192:240].reshape(4, 1536)
    g_conv_loc = lax.dynamic_slice(g_conv_full, (0, me * 384), (4, 384))
    gc, dc, mc, vc = _adamw([g_conv_loc], conv_w[0], m_conv_w[0], v_conv_w[0], "adamw_conv")

    reps = [(norm_w, m_norm_w, v_norm_w), (hg_lb_logits, m_hg_lb_logits, v_hg_lb_logits),
            (hg_norm_w, m_hg_norm_w, v_hg_norm_w), (gdn_A_log, m_gdn_A_log, v_gdn_A_log),
            (gdn_dt_bias, m_gdn_dt_bias, v_gdn_dt_bias), (gdn_norm_w, m_gdn_norm_w, v_gdn_norm_w),
            (final_norm_w, m_final_norm_w, v_final_norm_w)]
    wp = jnp.concatenate([_rows8(t[0]) for t in reps], axis=0)
    mp = jnp.concatenate([_rows8(t[1]) for t in reps], axis=0)
    vp = jnp.concatenate([_rows8(t[2]) for t in reps], axis=0)
    gr, dr, mr, vr = _adamw([small[8:64]], wp, mp, vp, "adamw_small")

    def unpack(p):
        outs = []
        for i, t in enumerate(reps):
            n = t[0].size
            outs.append(p[8 * i:8 * i + 8].reshape(-1)[:n].reshape(t[0].shape))
        return outs

    def leaves(meta_v, conv_v, in_v, out_v, rep_p):
        nw, lb, hgw, al, db, gdw, fwv = unpack(rep_p)
        return [meta_v, nw, in_v[None], conv_v[None], lb, hgw, al, db, gdw, out_v[None], fwv]

    loss = small[0, 0]
    return (loss, grad_x, *leaves(gm, gc, gi, go, gr), *leaves(dm, dc, di, do_, dr),
            *leaves(mm_, mc, mi, mo, mr), *leaves(vm, vc, vi, vo, vr))
```

```python
import functools

import jax
import jax.numpy as jnp
from jax import lax
from jax.experimental import pallas as pl
from jax.experimental.pallas import tpu as pltpu

f32 = jnp.float32
bf16 = jnp.bfloat16
MESH = pl.DeviceIdType.MESH
ANY = pl.BlockSpec(memory_space=pl.ANY)

D = 1024
NB = 2
N_META = 16
SEQ = 2048
PAD = 48
TP = PAD + N_META + SEQ
C = 64
NCH = TP // C
N = NB * TP
H = 4
DK = 128
HD = H * DK
PC = 4224
IN_COLS = 4104
SHARD_COLS = IN_COLS // 4
COL_HG, COL_ZHG, COL_QKV, COL_ZGD, COL_AB = 0, 3 * HD, 4 * HD, 7 * HD, 8 * HD
EPS = 1e-6
ADAM_LR, ADAM_B1, ADAM_B2, ADAM_EPS, ADAM_WD, ADAM_STEP = 0.001, 0.9, 0.999, 1e-08, 0.01, 10
VMEM_LIMIT = 56 * 1024 * 1024

P_HG = dict(lvl=1, av=1, qs=1, su=1)
P_GD = dict(kk=1, inv=1, sol=1, ws=1, qk=1, o=1, su=1)


def _cp(sem=None, **kw):
    return pltpu.CompilerParams(dimension_semantics=sem, vmem_limit_bytes=VMEM_LIMIT, **kw)


_DIMS = {"nn": (((1,), (0,)), ((), ())), "nt": (((1,), (1,)), ((), ())), "tn": (((0,), (0,)), ((), ()))}


def _split(x):
    hi = x.astype(bf16)
    return hi, (x - hi.astype(f32)).astype(bf16)


def _dg(a, b, kind, passes):
    d = lambda x, y: lax.dot_general(x, y, _DIMS[kind], preferred_element_type=f32)
    if passes == 1:
        return d(a.astype(bf16), b.astype(bf16))
    ah, al = _split(a)
    bh, bl = _split(b)
    return d(ah, bh) + d(ah, bl) + d(al, bh)


@functools.partial(jax.custom_vjp, nondiff_argnums=(2, 3))
def mmx(a, b, kind, passes):
    return _dg(a, b, kind, passes)


def _mmx_fwd(a, b, kind, passes):
    return _dg(a, b, kind, passes), (a, b)


def _mmx_bwd(kind, passes, res, g):
    a, b = res
    if kind == "nn":
        return _dg(g, b, "nt", passes), _dg(a, g, "tn", passes)
    if kind == "nt":
        return _dg(g, b, "nn", passes), _dg(g, a, "tn", passes)
    return _dg(b, g, "nt", passes), _dg(a, g, "nn", passes)


mmx.defvjp(_mmx_fwd, _mmx_bwd)


def _mask_dg(mask, x, kind):
    n = x.shape[1]
    xh, xl = _split(x)
    r = lax.dot_general(mask, jnp.concatenate([xh, xl], axis=1), _DIMS[kind], preferred_element_type=f32)
    return r[:, :n] + r[:, n:]


@jax.custom_vjp
def mask_mm(mask, x):
    return _mask_dg(mask, x, "nn")


def _mask_fwd(mask, x):
    return _mask_dg(mask, x, "nn"), mask


def _mask_bwd(mask, g):
    return None, _mask_dg(mask, g, "tn")


mask_mm.defvjp(_mask_fwd, _mask_bwd)


def bdot(a, b):
    return jnp.dot(a.astype(bf16), b.astype(bf16), preferred_element_type=f32)


def bdot_nt(a, b):
    return lax.dot_general(a.astype(bf16), b.astype(bf16), _DIMS["nt"], preferred_element_type=f32)


def bdot_tn(a, b):
    return lax.dot_general(a.astype(bf16), b.astype(bf16), _DIMS["tn"], preferred_element_type=f32)


def _iota2(n, m):
    return lax.broadcasted_iota(jnp.int32, (n, m), 0), lax.broadcasted_iota(jnp.int32, (n, m), 1)


def silu(x):
    return x * jax.nn.sigmoid(x)


def softplus(x):
    return jnp.maximum(x, 0.0) + jnp.log(1.0 + jnp.exp(-jnp.abs(x)))


def rmsnorm(x, w):
    return x * lax.rsqrt(jnp.mean(x * x, axis=-1, keepdims=True) + EPS) * w


def hg_masks():
    t, r = _iota2(C, C)
    mats = [r <= t, r > t]
    lvl = []
    for l in range(1, 7):
        sz = 1 << l
        half = sz >> 1
        seg_t = t >> l
        upper_t = (t & (sz - 1)) >= half
        mid_t = seg_t * sz + half - 1
        mats.append((upper_t & (r > mid_t) & (r <= t)) | ((~upper_t) & (r > t) & (r <= mid_t)))
        lvl.append(((seg_t == (r >> l)) & upper_t & ((r & (sz - 1)) < half)).astype(f32))
    stk = jnp.concatenate([m.astype(bf16) for m in mats], axis=0)
    return stk, lvl, (t == r).astype(f32)


def _head(a, h):
    return a[:, h * DK:(h + 1) * DK]


def hg_chunk(St, ps, l0, l1):
    m = jnp.maximum(l0, l1)
    e0 = jnp.exp(l0 - m)
    e1 = jnp.exp(l1 - m)
    lb = e0 / (e0 + e1)
    stk, lvl, eye = hg_masks()
    msk = [eye] + lvl
    qs, ks, vs, qG, kR, eGl = [], [], [], [], [], []
    for p in ps:
        pq, pf, v = p[:, 0:HD], p[:, HD:2 * HD], p[:, 2 * HD:3 * HD]
        q = silu(pq)
        f = lb + (1.0 - lb) * jax.nn.sigmoid(pf)
        k = 1.0 - f
        logf = jnp.log(f)
        Dm = mask_mm(stk, logf)
        ex = [jnp.exp(Dm[(2 + i) * C:(3 + i) * C]) for i in range(6)]
        qs.append([q] + [q * e for e in ex])
        ks.append([k] + [k * e for e in ex])
        vs.append(v)
        qG.append(q * jnp.exp(Dm[0:C]))
        kR.append(k * jnp.exp(Dm[C:2 * C]))
        eGl.append(jnp.exp(jnp.sum(logf, axis=0, keepdims=True)))
    units = [(b, h) for b in range(len(ps)) for h in range(H)]
    parts = [[msk[i] * mmx(_head(qs[b][i], h), _head(ks[b][i], h), "nt", P_HG["lvl"]) for b, h in units]
             for i in range(7)]
    A = [functools.reduce(lambda x, y: x + y, [parts[i][n] for i in range(7)]) for n in range(len(units))]
    qS = [mmx(_head(qG[b], h), St[n], "nt", P_HG["qs"]) for n, (b, h) in enumerate(units)]
    Sn = [St[n] * _head(eGl[b], h) + mmx(_head(vs[b], h), _head(kR[b], h), "tn", P_HG["su"])
          for n, (b, h) in enumerate(units)]
    outs = [mmx(A[n], _head(vs[b], h), "nn", P_HG["av"]) + qS[n] for n, (b, h) in enumerate(units)]
    return tuple(Sn), tuple(jnp.concatenate(outs[b * H:(b + 1) * H], axis=1) for b in range(len(ps)))


@jax.custom_vjp
def use_inverse(A, T):
    return T


def _use_inverse_fwd(A, T):
    return T, T


def _use_inverse_bwd(T, g):
    return -_dg(T, _dg(g, T, "nt", P_GD["inv"]), "tn", P_GD["inv"]), jnp.zeros_like(T)


use_inverse.defvjp(_use_inverse_fwd, _use_inverse_bwd)


def gd_chunk(S, cs, abs_, alog, dtb, t_saved=None):
    t, r = _iota2(C, C)
    tri = (r <= t).astype(bf16)
    ups = (r > t).astype(bf16)
    lane = lax.broadcasted_iota(jnp.int32, (1, DK), 1)
    subl = lax.broadcasted_iota(jnp.int32, (8, 1), 0)
    eye = (t == r).astype(f32)
    strict = (r < t).astype(f32)
    bd = ((t >> 4) == (r >> 4)).astype(f32)
    qa, ka, va, b4, gam4, grev4, gam4T, glast4 = [], [], [], [], [], [], [], []
    for c, ab in zip(cs, abs_):
        qa.append(silu(c[:, 0:HD]))
        ka.append(silu(c[:, HD:2 * HD]))
        va.append(silu(c[:, 2 * HD:3 * HD]))
        g4 = -jnp.exp(alog) * softplus(ab + dtb)
        b4.append(jax.nn.sigmoid(ab))
        gam4.append(mask_mm(tri, g4))
        grev4.append(mask_mm(ups, g4))
        gam4T.append(gam4[-1].T)
        glast4.append(jnp.sum(g4, axis=0, keepdims=True))
    units = [(b, h) for b in range(len(cs)) for h in range(H)]
    nu = range(len(units))
    inv = lambda a, b: [mmx(a[n], b[n], "nn", P_GD["inv"]) for n in nu]
    v = [_head(va[b], h) for b, h in units]
    q = [_head(qa[b], h) for b, h in units]
    k = [_head(ka[b], h) for b, h in units]
    q = [x * lax.rsqrt(jnp.sum(x * x, -1, keepdims=True) + EPS) * (DK ** -0.5) for x in q]
    k = [x * lax.rsqrt(jnp.sum(x * x, -1, keepdims=True) + EPS) for x in k]
    oh = [(lane == h).astype(f32) for h in range(H)]
    gam_c = [jnp.sum(gam4[b] * oh[h], -1, keepdims=True) for b, h in units]
    grev_c = [jnp.sum(grev4[b] * oh[h], -1, keepdims=True) for b, h in units]
    beta = [jnp.sum(b4[b] * (lane == h + H).astype(f32), -1, keepdims=True) for b, h in units]
    glast = [jnp.sum(glast4[b] * oh[h], -1, keepdims=True) for b, h in units]
    gam_r = [jnp.sum(gam4T[b][0:8, :] * (subl == h).astype(f32), axis=0, keepdims=True) for b, h in units]
    dec = [jnp.exp(jnp.where(r <= t, gam_c[n] - gam_r[n], -1e30)) for n in nu]
    egam = [jnp.exp(gam_c[n]) for n in nu]
    kk = [mmx(k[n], k[n], "nt", P_GD["kk"]) for n in nu]
    qk = [mmx(q[n], k[n], "nt", P_GD["qk"]) * dec[n] for n in nu]
    A = [beta[n] * kk[n] * dec[n] * strict for n in nu]
    Dg = [A[n] * bd for n in nu]
    L = [A[n] - Dg[n] for n in nu]
    if t_saved is None:
        ImD = [eye - Dg[n] for n in nu]
        D2 = inv(Dg, Dg)
        P1 = inv(ImD, [eye + x for x in D2])
        D4 = inv(D2, D2)
        P2 = inv(P1, [eye + x for x in D4])
        D8 = inv(D4, D4)
        M = inv(P2, [eye + x for x in D8])
        Nn = inv(M, L)
        N2 = inv(Nn, Nn)
        T1 = inv([eye - x for x in Nn], [eye + x for x in N2])
        Tinv = inv(T1, M)
    else:
        Tinv = [use_inverse(A[n], t_saved[b][:, h * DK:h * DK + C]) for n, (b, h) in enumerate(units)]
    rhs = [jnp.concatenate([beta[n] * v[n], (beta[n] * egam[n]) * k[n]], axis=1) for n in nu]
    sol = [mmx(Tinv[n], rhs[n], "nn", P_GD["sol"]) for n in nu]
    qwS = [mmx(jnp.concatenate([q[n] * egam[n], sol[n][:, DK:2 * DK]], axis=0), S[n], "nn", P_GD["ws"]) for n in nu]
    u = [sol[n][:, 0:DK] - qwS[n][C:2 * C] for n in nu]
    outs = [qwS[n][0:C] + mmx(qk[n], u[n], "nn", P_GD["o"]) for n in nu]
    Sn = [jnp.exp(glast[n]) * S[n] + mmx(k[n] * jnp.exp(grev_c[n]), u[n], "tn", P_GD["su"]) for n in nu]
    zpad = jnp.zeros((C, DK - C), f32)
    t_pack = tuple(jnp.concatenate([x for n in range(b * H, (b + 1) * H) for x in (lax.stop_gradient(Tinv[n]), zpad)],
                                   axis=1) for b in range(len(cs)))
    return tuple(Sn), tuple(jnp.concatenate(outs[b * H:(b + 1) * H], axis=1) for b in range(len(cs))), t_pack


def _in_proj(hflat, norm_w, wbig):
    tm = 384

    def body(h_ref, nw_ref, w_ref, p_ref, ut_ref):
        u = rmsnorm(h_ref[...], nw_ref[...])
        ut_ref[...] = u.T.astype(bf16)
        p_ref[...] = jnp.dot(u.astype(bf16), w_ref[...], preferred_element_type=f32)

    return pl.pallas_call(
        body, name="in_proj", grid=(N // tm,),
        in_specs=[pl.BlockSpec((tm, D), lambda i: (i, 0)), pl.BlockSpec((1, D), lambda i: (0, 0)),
                  pl.BlockSpec((D, PC), lambda i: (0, 0))],
        out_specs=[pl.BlockSpec((tm, PC), lambda i: (i, 0)), pl.BlockSpec((D, tm), lambda i: (0, i))],
        out_shape=[jax.ShapeDtypeStruct((N, PC), f32), jax.ShapeDtypeStruct((D, N), bf16)],
        compiler_params=_cp(("arbitrary",)),
    )(hflat, norm_w, wbig)


NU = NB * H
_REV = lambda c: NCH - 1 - c
_FWD = lambda c: c


def _tok_spec(w, ix, col=0):
    return pl.BlockSpec((NB, C, w), lambda c: (0, ix(c), col))


def _state_spec(ix):
    return pl.BlockSpec((NB, 1, H, DK, DK), lambda c: (0, ix(c), 0, 0, 0))


def _row_spec(w):
    return pl.BlockSpec((1, w), lambda c: (0, 0))


def _rows(ref):
    return tuple(ref[b] for b in range(NB))


def _scan_fwd(chunk_fn, name, x3, extra, extra_specs, n_tok, aux_w):
    ne = len(extra)
    na = 1 if aux_w else 0

    def body(*refs):
        x_ref = refs[0]
        e_refs = refs[1:1 + ne]
        o_ref, s_ref = refs[1 + ne:3 + ne]
        aux_refs = refs[3 + ne:3 + ne + na]
        st = refs[3 + ne + na]

        @pl.when(pl.program_id(0) == 0)
        def _():
            st[...] = jnp.zeros_like(st)

        S = tuple(st[n] for n in range(NU))
        for n in range(NU):
            s_ref[n // H, 0, n % H] = S[n]
        args = [_rows(e) if i < n_tok else e[...] for i, e in enumerate(e_refs)]
        Sn, o, *aux = chunk_fn(S, _rows(x_ref), *args)
        for n in range(NU):
            st[n] = Sn[n]
        for b in range(NB):
            o_ref[b] = o[b]
            for a_ref, a in zip(aux_refs, aux):
                a_ref[b] = a[b]

    return pl.pallas_call(
        body, name=name, grid=(NCH,),
        in_specs=[_tok_spec(3 * HD, _FWD)] + extra_specs(_FWD),
        out_specs=[_tok_spec(HD, _FWD), _state_spec(_FWD)] + [_tok_spec(aux_w, _FWD)] * na,
        out_shape=[jax.ShapeDtypeStruct((NB, TP, HD), f32), jax.ShapeDtypeStruct((NB, NCH, H, DK, DK), f32)]
        + [jax.ShapeDtypeStruct((NB, TP, aux_w), f32)] * na,
        scratch_shapes=[pltpu.VMEM((NU, DK, DK), f32)],
        compiler_params=_cp(("arbitrary",)),
    )(x3, *extra)


def _hg_extra_specs(ix):
    return [_row_spec(HD), _row_spec(HD)]


def _gd_extra_specs(ix):
    return [_tok_spec(DK, ix, COL_AB // DK), _row_spec(DK), _row_spec(DK)]


def _hg_fwd(proj3, l0, l1):
    return _scan_fwd(hg_chunk, "hg_fwd", proj3, [l0, l1], _hg_extra_specs, 0, 0)


def _gd_fwd(cv, proj3, alog, dtb):
    return _scan_fwd(gd_chunk, "gd_fwd", cv, [proj3, alog, dtb], _gd_extra_specs, 1, H * DK)


def _hg_bwd(proj3, l0, l1, s_saved, do):
    def body(p_ref, l0_ref, l1_ref, s_ref, do_ref, dp_ref, dl0_ref, dl1_ref, dst):
        @pl.when(pl.program_id(0) == 0)
        def _():
            dst[...] = jnp.zeros_like(dst)
            dl0_ref[...] = jnp.zeros_like(dl0_ref)
            dl1_ref[...] = jnp.zeros_like(dl1_ref)

        S = tuple(s_ref[n // H, 0, n % H] for n in range(NU))
        _, vjp = jax.vjp(hg_chunk, S, _rows(p_ref), l0_ref[...], l1_ref[...])
        dS, dp, dl0, dl1 = vjp((tuple(dst[n] for n in range(NU)), _rows(do_ref)))
        for n in range(NU):
            dst[n] = dS[n]
        for b in range(NB):
            dp_ref[b] = dp[b].astype(bf16)
        dl0_ref[...] += jnp.broadcast_to(dl0, (8, HD))
        dl1_ref[...] += jnp.broadcast_to(dl1, (8, HD))

    acc = pl.BlockSpec((8, HD), lambda c: (0, 0))
    return pl.pallas_call(
        body, name="hg_bwd", grid=(NCH,),
        in_specs=[_tok_spec(3 * HD, _REV)] + _hg_extra_specs(_REV) + [_state_spec(_REV), _tok_spec(HD, _REV)],
        out_specs=[_tok_spec(3 * HD, _REV), acc, acc],
        out_shape=[jax.ShapeDtypeStruct((NB, TP, 3 * HD), bf16), jax.ShapeDtypeStruct((8, HD), f32),
                   jax.ShapeDtypeStruct((8, HD), f32)],
        scratch_shapes=[pltpu.VMEM((NU, DK, DK), f32)],
        compiler_params=_cp(("arbitrary",)),
    )(proj3, l0, l1, s_saved, do)


def _gd_bwd(cv, proj3, alog, dtb, s_saved, t_saved, do):
    def body(c_ref, ab_ref, al_ref, db_ref, s_ref, t_ref, do_ref, dc_ref, dab_ref, dal_ref, ddb_ref, dst):
        @pl.when(pl.program_id(0) == 0)
        def _():
            dst[...] = jnp.zeros_like(dst)
            dal_ref[...] = jnp.zeros_like(dal_ref)
            ddb_ref[...] = jnp.zeros_like(ddb_ref)

        S = tuple(s_ref[n // H, 0, n % H] for n in range(NU))
        t_rows = _rows(t_ref)
        fn = lambda *a: gd_chunk(*a, t_saved=t_rows)[0:2]
        _, vjp = jax.vjp(fn, S, _rows(c_ref), _rows(ab_ref), al_ref[...], db_ref[...])
        dS, dc, dab, dal, ddb = vjp((tuple(dst[n] for n in range(NU)), _rows(do_ref)))
        for n in range(NU):
            dst[n] = dS[n]
        for b in range(NB):
            dc_ref[b] = dc[b]
            dab_ref[b] = dab[b].astype(bf16)
        dal_ref[...] += jnp.broadcast_to(dal, (8, DK))
        ddb_ref[...] += jnp.broadcast_to(ddb, (8, DK))

    acc = pl.BlockSpec((8, DK), lambda c: (0, 0))
    return pl.pallas_call(
        body, name="gd_bwd", grid=(NCH,),
        in_specs=[_tok_spec(3 * HD, _REV)] + _gd_extra_specs(_REV)
        + [_state_spec(_REV), _tok_spec(HD, _REV), _tok_spec(HD, _REV)],
        out_specs=[_tok_spec(3 * HD, _REV), _tok_spec(DK, _REV), acc, acc],
        out_shape=[jax.ShapeDtypeStruct((NB, TP, 3 * HD), f32), jax.ShapeDtypeStruct((NB, TP, DK), bf16),
                   jax.ShapeDtypeStruct((8, DK), f32), jax.ShapeDtypeStruct((8, DK), f32)],
        scratch_shapes=[pltpu.VMEM((NU, DK, DK), f32)],
        compiler_params=_cp(("arbitrary",)),
    )(cv, proj3, alog, dtb, s_saved, t_saved, do)


def _conv_fwd(proj3, conv4):
    def body(x_ref, w_ref, y_ref):
        x = x_ref[0]
        row = lax.broadcasted_iota(jnp.int32, (TP, 1), 0)
        y = w_ref[3] * x
        for s in (1, 2, 3):
            y = y + w_ref[3 - s] * jnp.where(row >= s, pltpu.roll(x, s, 0), 0.0)
        y_ref[0] = y

    return pl.pallas_call(
        body, name="conv_fwd", grid=(NB, 3),
        in_specs=[pl.BlockSpec((1, TP, HD), lambda b, j: (b, 0, COL_QKV // HD + j)),
                  pl.BlockSpec((4, 1, HD), lambda b, j: (0, 0, j))],
        out_specs=pl.BlockSpec((1, TP, HD), lambda b, j: (b, 0, j)),
        out_shape=jax.ShapeDtypeStruct((NB, TP, 3 * HD), f32),
        compiler_params=_cp(("arbitrary", "arbitrary")),
    )(proj3, conv4)


def _conv_bwd(proj3, conv4, dy):
    def body(x_ref, w_ref, dy_ref, dx_ref, dw_ref):
        @pl.when(pl.program_id(1) == 0)
        def _():
            dw_ref[...] = jnp.zeros_like(dw_ref)

        x = x_ref[0]
        g = dy_ref[0]
        row = lax.broadcasted_iota(jnp.int32, (TP, 1), 0)
        dx = w_ref[3] * g
        dw_ref[3] += jnp.broadcast_to(jnp.sum(x * g, axis=0, keepdims=True), (8, HD))
        for s in (1, 2, 3):
            dx = dx + w_ref[3 - s] * jnp.where(row < TP - s, pltpu.roll(g, TP - s, 0), 0.0)
            xs = jnp.where(row >= s, pltpu.roll(x, s, 0), 0.0)
            dw_ref[3 - s] += jnp.broadcast_to(jnp.sum(xs * g, axis=0, keepdims=True), (8, HD))
        dx_ref[0] = dx.astype(bf16)

    return pl.pallas_call(
        body, name="conv_bwd", grid=(3, NB),
        in_specs=[pl.BlockSpec((1, TP, HD), lambda j, b: (b, 0, COL_QKV // HD + j)),
                  pl.BlockSpec((4, 1, HD), lambda j, b: (0, 0, j)), pl.BlockSpec((1, TP, HD), lambda j, b: (b, 0, j))],
        out_specs=[pl.BlockSpec((1, TP, HD), lambda j, b: (b, 0, j)), pl.BlockSpec((4, 8, HD), lambda j, b: (0, 0, j))],
        out_shape=[jax.ShapeDtypeStruct((NB, TP, 3 * HD), bf16), jax.ShapeDtypeStruct((4, 8, 3 * HD), f32)],
        compiler_params=_cp(("arbitrary", "arbitrary")),
    )(proj3, conv4, dy)


def _gated(o, z, w):
    outs = []
    for hh in range(H):
        sl = slice(hh * DK, (hh + 1) * DK)
        outs.append(rmsnorm(o[:, sl], w) * silu(z[:, sl]))
    return jnp.concatenate(outs, axis=-1)


def _out_loss(o_hg, o_gd, proj, hgw, gdw, wout, hflat, fw, target):
    tm = 384

    def body(ohg_ref, ogd_ref, zhg_ref, zgd_ref, hgw_ref, gdw_ref, wo_ref, h_ref, fw_ref, tg_ref,
             loss_ref, dohg_ref, dogd_ref, dzhg_ref, dzgd_ref, dh_ref, dwo_ref, dhgw_ref, dgdw_ref, dfw_ref):
        i = pl.program_id(0)

        @pl.when(i == 0)
        def _():
            for r in (loss_ref, dwo_ref, dhgw_ref, dgdw_ref, dfw_ref):
                r[...] = jnp.zeros_like(r)

        row = i * tm + lax.broadcasted_iota(jnp.int32, (tm, 1), 0)
        tok = jnp.where(row >= TP, row - TP, row)
        valid = (tok >= PAD + N_META).astype(f32)
        hval = h_ref[...]
        tgt = tg_ref[...]

        def mix(ohg, ogd, zhg, zgd, w1, w2):
            return jnp.concatenate([_gated(ohg, zhg, w1), _gated(ogd, zgd, w2)], axis=-1)

        y, vjp_mix = jax.vjp(mix, ohg_ref[...], ogd_ref[...], zhg_ref[...], zgd_ref[...], hgw_ref[...], gdw_ref[...])
        out = bdot(y, wo_ref[...])

        def head(out, fwv):
            err = (rmsnorm(hval + out, fwv) - tgt) * valid
            return 0.5 * jnp.sum(jnp.mean(err * err, axis=-1, keepdims=True))

        loss, vjp_head = jax.vjp(head, out, fw_ref[...])
        dout, dfw = vjp_head(jnp.ones((), f32))
        dh_ref[...] = dout
        dy = bdot_nt(dout, wo_ref[...])
        dwo_ref[...] += bdot_tn(y, dout)
        dohg, dogd, dzhg, dzgd, dw1, dw2 = vjp_mix(dy)
        dohg_ref[...] = dohg
        dogd_ref[...] = dogd
        dzhg_ref[...] = dzhg.astype(bf16)
        dzgd_ref[...] = dzgd.astype(bf16)
        loss_ref[...] += jnp.broadcast_to(loss, (8, DK))
        dhgw_ref[...] += jnp.broadcast_to(dw1, (8, DK))
        dgdw_ref[...] += jnp.broadcast_to(dw2, (8, DK))
        dfw_ref[...] += jnp.broadcast_to(dfw, (8, D))

    row = lambda w: pl.BlockSpec((tm, w), lambda i: (i, 0))
    whole = lambda r, w: pl.BlockSpec((r, w), lambda i: (0, 0))
    col = lambda c0: pl.BlockSpec((tm, HD), lambda i: (i, c0 // HD))
    return pl.pallas_call(
        body, name="out_loss", grid=(N // tm,),
        in_specs=[row(HD), row(HD), col(COL_ZHG), col(COL_ZGD),
                  whole(1, DK), whole(1, DK), whole(D, D), row(D), whole(1, D), row(D)],
        out_specs=[whole(8, DK), row(HD), row(HD), row(HD), row(HD), row(D), whole(D, D),
                   whole(8, DK), whole(8, DK), whole(8, D)],
        out_shape=[jax.ShapeDtypeStruct((8, DK), f32)] + [jax.ShapeDtypeStruct((N, HD), f32)] * 2
        + [jax.ShapeDtypeStruct((N, HD), bf16)] * 2
        + [jax.ShapeDtypeStruct((N, D), f32), jax.ShapeDtypeStruct((D, D), f32),
           jax.ShapeDtypeStruct((8, DK), f32), jax.ShapeDtypeStruct((8, DK), f32), jax.ShapeDtypeStruct((8, D), f32)],
        compiler_params=_cp(("arbitrary",)),
    )(o_hg, o_gd, proj, proj, hgw, gdw, wout, hflat, fw, target)


def _in_bwd(pieces, wbig, hflat, norm_w, dh_res, pbs):
    tm = 384
    nsteps = N // tm
    np_ = len(pieces)
    na = len(pbs)
    offs = [c0 for _, c0 in pieces]
    widths = [d.shape[1] for d, _ in pieces]

    def body(*refs):
        d_refs = refs[:np_]
        w_ref, h_ref, nw_ref, dhr_ref = refs[np_:np_ + 4]
        srcs = refs[np_ + 4:np_ + 4 + na]
        dh_ref, dnw_ref = refs[np_ + 4 + na:np_ + 6 + na]
        dsts = refs[np_ + 6 + na:np_ + 6 + 2 * na]
        sems = refs[np_ + 6 + 2 * na:]
        i = pl.program_id(0)

        def copies():
            if not na:
                return []
            x, y, c, chips = _place()
            return [pltpu.make_async_remote_copy(
                src_ref=srcs[a].at[2 * px + py], dst_ref=dsts[a].at[j], send_sem=sems[0].at[na * j + a],
                recv_sem=sems[1].at[na * j + a], device_id=(px, py, c), device_id_type=MESH)
                for j, (px, py) in enumerate(chips) for a in range(na)]

        @pl.when(i == 0)
        def _():
            dnw_ref[...] = jnp.zeros_like(dnw_ref)
            for cp in copies():
                cp.start()

        du = jnp.zeros((tm, D), f32)
        for d_ref, off, wd in zip(d_refs, offs, widths):
            du = du + bdot_nt(d_ref[...], w_ref[:, off:off + wd])
        _, vjp = jax.vjp(rmsnorm, h_ref[...], nw_ref[...])
        dh, dnw = vjp(du)
        dh_ref[...] = dh + dhr_ref[...]
        dnw_ref[...] += jnp.broadcast_to(dnw, (8, D))

        @pl.when(i == nsteps - 1)
        def _():
            for cp in copies():
                cp.wait()

    row = lambda w: pl.BlockSpec((tm, w), lambda i: (i, 0))
    return pl.pallas_call(
        body, name="in_bwd", grid=(nsteps,),
        in_specs=[row(w) for w in widths]
        + [pl.BlockSpec((D, PC), lambda i: (0, 0)), row(D), pl.BlockSpec((1, D), lambda i: (0, 0)), row(D)] + [ANY] * na,
        out_specs=[row(D), pl.BlockSpec((8, D), lambda i: (0, 0))] + [ANY] * na,
        out_shape=[jax.ShapeDtypeStruct((N, D), f32), jax.ShapeDtypeStruct((8, D), f32)]
        + [jax.ShapeDtypeStruct((3,) + p.shape[1:], p.dtype) for p in pbs],
        scratch_shapes=[pltpu.SemaphoreType.DMA((3 * na,)), pltpu.SemaphoreType.DMA((3 * na,))] if na else [],
        compiler_params=_cp(("arbitrary",)),
    )(*[d for d, _ in pieces], wbig, hflat, norm_w, dh_res, *pbs)


def _w_grad(ut, d, name):
    n = d.shape[1]
    tk = N // 3

    def body(u_ref, d_ref, o_ref):
        @pl.when(pl.program_id(0) == 0)
        def _():
            o_ref[...] = jnp.zeros_like(o_ref)

        o_ref[...] += jnp.dot(u_ref[...], d_ref[...], preferred_element_type=f32)

    return pl.pallas_call(
        body, name=name, grid=(N // tk,),
        in_specs=[pl.BlockSpec((D, tk), lambda k: (0, k)), pl.BlockSpec((tk, n), lambda k: (k, 0))],
        out_specs=pl.BlockSpec((D, n), lambda k: (0, 0)),
        out_shape=jax.ShapeDtypeStruct((D, n), f32),
        compiler_params=_cp(("arbitrary",)),
    )(ut, d)


def _adam_math(g, w, m, v):
    m2 = ADAM_B1 * m + (1.0 - ADAM_B1) * g
    v2 = ADAM_B2 * v + (1.0 - ADAM_B2) * (g * g)
    m_hat = m2 / (1.0 - ADAM_B1 ** ADAM_STEP)
    v_hat = v2 / (1.0 - ADAM_B2 ** ADAM_STEP)
    delta = -ADAM_LR * (m_hat / (jnp.sqrt(v_hat) + ADAM_EPS) + ADAM_WD * w)
    return delta, m2, v2


def _adamw(gs, w, m, v, name):
    R, Cc = w.shape
    tr = 256 if R % 256 == 0 else R
    ng = len(gs)

    def body(*refs):
        g = refs[0][...]
        for r in refs[1:ng]:
            g = g + r[...]
        w_ref, m_ref, v_ref, g_ref, d_ref, m2_ref, v2_ref = refs[ng:]
        delta, m2, v2 = _adam_math(g, w_ref[...], m_ref[...], v_ref[...])
        g_ref[...] = g
        d_ref[...] = delta
        m2_ref[...] = m2
        v2_ref[...] = v2

    spec = pl.BlockSpec((tr, Cc), lambda i: (i, 0))
    return pl.pallas_call(
        body, name=name, grid=(R // tr,),
        in_specs=[spec] * (ng + 3), out_specs=[spec] * 4,
        out_shape=[jax.ShapeDtypeStruct((R, Cc), f32)] * 4,
        compiler_params=_cp(("arbitrary",)),
    )(*gs, w, m, v)


def _adamw_rows(g, w, m, v, name):
    R, _, Cc = w.shape
    tr = R // 9

    def body(g_ref, w_ref, m_ref, v_ref, go_ref, d_ref, m2_ref, v2_ref):
        g = g_ref[...]
        delta, m2, v2 = _adam_math(g, w_ref[...], m_ref[...], v_ref[...])
        go_ref[...] = g
        d_ref[...] = delta
        m2_ref[...] = m2
        v2_ref[...] = v2

    spec = pl.BlockSpec((tr, 1, Cc), lambda i: (i, 0, 0))
    return pl.pallas_call(
        body, name=name, grid=(R // tr,),
        in_specs=[spec] * 4, out_specs=[spec] * 4,
        out_shape=[jax.ShapeDtypeStruct((R, 1, Cc), f32)] * 4,
        compiler_params=_cp(("arbitrary",)),
    )(g, w, m, v)


def _sum_slots(r, name):
    S, R, Cc = r.shape
    tr = 256 if R % 256 == 0 else R

    def body(r_ref, o_ref):
        acc = r_ref[0]
        for s in range(1, S):
            acc = acc + r_ref[s]
        o_ref[...] = acc

    return pl.pallas_call(
        body, name=name, grid=(R // tr,),
        in_specs=[pl.BlockSpec((S, tr, Cc), lambda i: (0, i, 0))], out_specs=pl.BlockSpec((tr, Cc), lambda i: (i, 0)),
        out_shape=jax.ShapeDtypeStruct((R, Cc), f32),
        compiler_params=_cp(("arbitrary",)),
    )(r)


def _place():
    x, y, c = lax.axis_index("x"), lax.axis_index("y"), lax.axis_index("c")
    return x, y, c, [(1 - x, y), (x, 1 - y), (1 - x, 1 - y)]


def _gather_weights(halved, whole):
    nh, nw = len(halved), len(whole)
    na = nh + nw

    def body(*refs):
        srcs, dsts = refs[:na], refs[na:2 * na]
        send_sems, recv_sems, loc_sems = refs[2 * na:2 * na + 3]
        stage = refs[2 * na + 3:]
        x, y, c, chips = _place()
        me = 2 * x + y
        loads = [pltpu.make_async_copy(s, v, loc_sems.at[i]) for i, (s, v) in enumerate(zip(srcs, stage))]
        locs = [pltpu.make_async_copy(v, d.at[me], loc_sems.at[i]) for i, (v, d) in enumerate(zip(stage, dsts))]
        for cp in loads:
            cp.start()

        def ici(j, i, slot):
            px, py = chips[j]
            src = srcs[i].at[c] if i < nh else srcs[i]
            dst = dsts[i].at[slot, c] if i < nh else dsts[i].at[slot]
            return pltpu.make_async_remote_copy(
                src_ref=src, dst_ref=dst, send_sem=send_sems.at[na * j + i], recv_sem=recv_sems.at[na * j + i],
                device_id=(px, py, c), device_id_type=MESH)

        def d2d(j, i, half):
            px, py = chips[j]
            blk = dsts[i].at[2 * px + py, half]
            return pltpu.make_async_remote_copy(
                src_ref=blk, dst_ref=blk, send_sem=send_sems.at[3 * na + nh * j + i],
                recv_sem=recv_sems.at[3 * na + nh * j + i], device_id=(x, y, 1 - c), device_id_type=MESH)

        sends = [ici(j, i, me) for j in range(3) for i in range(na)]
        for cp in sends:
            cp.start()
        for ld, st in zip(loads, locs):
            ld.wait()
            st.start()
        for j, (px, py) in enumerate(chips):
            for i in range(na):
                ici(j, i, 2 * px + py).wait_recv()
                if i < nh:
                    fwd = d2d(j, i, c)
                    fwd.start()
                    sends.append(fwd)
        for j in range(3):
            for i in range(nh):
                d2d(j, i, 1 - c).wait_recv()
        for cp in sends:
            cp.wait_send()
        for cp in locs:
            cp.wait()

    nsem = 3 * na + 3 * nh
    return pl.pallas_call(
        body, name="gather_weights",
        in_specs=[ANY] * na, out_specs=[ANY] * na,
        out_shape=[jax.ShapeDtypeStruct((4,) + s.shape, s.dtype) for s in list(halved) + list(whole)],
        scratch_shapes=[pltpu.SemaphoreType.DMA((nsem,)), pltpu.SemaphoreType.DMA((nsem,)),
                        pltpu.SemaphoreType.DMA((na,))] + [pltpu.VMEM(s.shape, s.dtype) for s in list(halved) + list(whole)],
        compiler_params=pltpu.CompilerParams(has_side_effects=True, vmem_limit_bytes=VMEM_LIMIT),
    )(*halved, *whole)


def _swap_halves(gs):
    na = len(gs)

    def body(*refs):
        srcs, dsts = refs[:na], refs[na:2 * na]
        send_sems, recv_sems = refs[2 * na:]
        x, y, c, _ = _place()
        cps = [pltpu.make_async_remote_copy(
            src_ref=srcs[i].at[q, 1 - c], dst_ref=dsts[i].at[q], send_sem=send_sems.at[4 * i + q],
            recv_sem=recv_sems.at[4 * i + q], device_id=(x, y, 1 - c), device_id_type=MESH)
            for i in range(na) for q in range(4)]
        for cp in cps:
            cp.start()
        for cp in cps:
            cp.wait()

    return pl.pallas_call(
        body, name="swap_halves",
        in_specs=[ANY] * na, out_specs=[ANY] * na,
        out_shape=[jax.ShapeDtypeStruct((4,) + g.shape[2:], g.dtype) for g in gs],
        scratch_shapes=[pltpu.SemaphoreType.DMA((4 * na,)), pltpu.SemaphoreType.DMA((4 * na,))],
        compiler_params=pltpu.CompilerParams(has_side_effects=True),
    )(*gs)


def _add_halves(c_arr, g, s, name):
    _, _, R, Cc = g.shape
    tr = min(R, 256)

    def body(c_ref, g_ref, s_ref, b_ref, f_ref):
        p = g_ref[0, 0] + s_ref[0]
        f_ref[0] = p
        b_ref[0] = p.astype(bf16)

    blk = pl.BlockSpec((1, tr, Cc), lambda q, i, cr: (q, i, 0))
    return pl.pallas_call(
        body, name=name,
        grid_spec=pltpu.PrefetchScalarGridSpec(
            num_scalar_prefetch=1, grid=(4, R // tr),
            in_specs=[pl.BlockSpec((1, 1, tr, Cc), lambda q, i, cr: (q, cr[0], i, 0)), blk], out_specs=[blk, blk]),
        out_shape=[jax.ShapeDtypeStruct((4, R, Cc), bf16), jax.ShapeDtypeStruct((4, R, Cc), f32)],
        compiler_params=_cp(("arbitrary", "arbitrary")),
    )(c_arr, g, s)


_FLIPS = [(fx, fy, fc) for fx in (0, 1) for fy in (0, 1) for fc in (0, 1)][1:]


def _scatter_blocks(pbs, pack):
    na = len(pbs)
    R = pack.shape[0]

    def body(*refs):
        srcs, pk = refs[:na], refs[na]
        dsts, rp = refs[na + 1:2 * na + 1], refs[2 * na + 1]
        send_sems, recv_sems = refs[2 * na + 2:]
        x, y, c, chips = _place()

        def big(j, i):
            px, py = chips[j]
            return pltpu.make_async_remote_copy(
                src_ref=srcs[i].at[2 * px + py], dst_ref=dsts[i].at[j], send_sem=send_sems.at[na * j + i],
                recv_sem=recv_sems.at[na * j + i], device_id=(px, py, c), device_id_type=MESH)

        def small(k):
            fx, fy, fc = _FLIPS[k]
            return pltpu.make_async_remote_copy(
                src_ref=pk, dst_ref=rp.at[k], send_sem=send_sems.at[3 * na + k], recv_sem=recv_sems.at[3 * na + k],
                device_id=(x ^ fx, y ^ fy, c ^ fc), device_id_type=MESH)

        cps = [big(j, i) for j in range(3) for i in range(na)] + [small(k) for k in range(7)]
        for cp in cps:
            cp.start()
        for cp in cps:
            cp.wait()

    nsem = 3 * na + 7
    return pl.pallas_call(
        body, name="scatter_blocks",
        in_specs=[ANY] * (na + 1), out_specs=[ANY] * (na + 1),
        out_shape=[jax.ShapeDtypeStruct((3,) + p.shape[1:], p.dtype) for p in pbs]
        + [jax.ShapeDtypeStruct((7, R, 128), f32)],
        scratch_shapes=[pltpu.SemaphoreType.DMA((nsem,)), pltpu.SemaphoreType.DMA((nsem,))],
        compiler_params=pltpu.CompilerParams(has_side_effects=True),
    )(*pbs, pack)


def _sum_blocks(me_arr, pf, r, name):
    _, R, Cc = pf.shape
    tr = min(R, 256)

    def body(me_ref, pf_ref, r_ref, o_ref):
        acc = pf_ref[0]
        for j in range(3):
            acc = acc + r_ref[j].astype(f32)
        o_ref[...] = acc

    return pl.pallas_call(
        body, name=name,
        grid_spec=pltpu.PrefetchScalarGridSpec(
            num_scalar_prefetch=1, grid=(R // tr,),
            in_specs=[pl.BlockSpec((1, tr, Cc), lambda i, mr: (mr[0], i, 0)),
                      pl.BlockSpec((3, tr, Cc), lambda i, mr: (0, i, 0))],
            out_specs=pl.BlockSpec((tr, Cc), lambda i, mr: (i, 0))),
        out_shape=jax.ShapeDtypeStruct((R, Cc), f32),
        compiler_params=_cp(("arbitrary",)),
    )(me_arr, pf, r)


def _sum_packs(me8_arr, pack, rp):
    R = pack.shape[0]

    def body(me_ref, pk_ref, rp_ref, o_ref):
        me8 = me_ref[0]
        acc = None
        for d in range(8):
            rel = d ^ me8
            term = jnp.where(rel == 0, pk_ref[...], rp_ref[jnp.maximum(rel - 1, 0)])
            acc = term if acc is None else acc + term
        o_ref[...] = acc

    return pl.pallas_call(
        body, name="sum_packs",
        grid_spec=pltpu.PrefetchScalarGridSpec(
            num_scalar_prefetch=1, grid=(1,),
            in_specs=[pl.BlockSpec((R, 128), lambda i, mr: (0, 0)), pl.BlockSpec((7, R, 128), lambda i, mr: (0, 0, 0))],
            out_specs=pl.BlockSpec((R, 128), lambda i, mr: (0, 0))),
        out_shape=jax.ShapeDtypeStruct((R, 128), f32),
        compiler_params=_cp(("arbitrary",)),
    )(me8_arr, pack, rp)


def _swap_finished(fs):
    na = len(fs)

    def body(*refs):
        srcs, dsts = refs[:na], refs[na:2 * na]
        send_sems, recv_sems = refs[2 * na:]
        x, y, c, _ = _place()
        cps = [pltpu.make_async_remote_copy(
            src_ref=srcs[i], dst_ref=dsts[i], send_sem=send_sems.at[i], recv_sem=recv_sems.at[i],
            device_id=(x, y, 1 - c), device_id_type=MESH) for i in range(na)]
        for cp in cps:
            cp.start()
        for cp in cps:
            cp.wait()

    return pl.pallas_call(
        body, name="swap_finished",
        in_specs=[ANY] * na, out_specs=[ANY] * na,
        out_shape=[jax.ShapeDtypeStruct(f.shape, f.dtype) for f in fs],
        scratch_shapes=[pltpu.SemaphoreType.DMA((na,)), pltpu.SemaphoreType.DMA((na,))],
        compiler_params=pltpu.CompilerParams(has_side_effects=True),
    )(*fs)


def _adamw_halves(c_arr, mine, peer, w, m, v, name):
    _, R, Cc = w.shape
    tr = min(R, 256)

    def body(c_ref, mine_ref, peer_ref, w_ref, m_ref, v_ref, g_ref, d_ref, m2_ref, v2_ref):
        g = jnp.where(pl.program_id(0) == c_ref[0], mine_ref[...], peer_ref[...])
        delta, m2, v2 = _adam_math(g, w_ref[0], m_ref[0], v_ref[0])
        g_ref[0] = g
        d_ref[0] = delta
        m2_ref[0] = m2
        v2_ref[0] = v2

    half = pl.BlockSpec((tr, Cc), lambda hh, i, cr: (i, 0))
    full = pl.BlockSpec((1, tr, Cc), lambda hh, i, cr: (hh, i, 0))
    return pl.pallas_call(
        body, name=name,
        grid_spec=pltpu.PrefetchScalarGridSpec(
            num_scalar_prefetch=1, grid=(2, R // tr), in_specs=[half, half, full, full, full], out_specs=[full] * 4),
        out_shape=[jax.ShapeDtypeStruct((2, R, Cc), f32)] * 4,
        compiler_params=_cp(("arbitrary", "arbitrary")),
    )(c_arr, mine, peer, w, m, v)


def _rows8(a):
    flat = a.reshape(-1)
    n = flat.shape[0]
    rows = -(-n // 1024) * 8
    return jnp.pad(flat, (0, rows * 128 - n)).reshape(rows, 128)


def kernel(x, meta_tokens, norm_w, w_in, conv_w, hg_lb_logits, hg_norm_w, gdn_A_log, gdn_dt_bias, gdn_norm_w, w_out, final_norm_w, loss_target, m_meta_tokens, m_norm_w, m_w_in, m_conv_w, m_hg_lb_logits, m_hg_norm_w, m_gdn_A_log, m_gdn_dt_bias, m_gdn_norm_w, m_w_out, m_final_norm_w, v_meta_tokens, v_norm_w, v_w_in, v_conv_w, v_hg_lb_logits, v_hg_norm_w, v_gdn_A_log, v_gdn_dt_bias, v_gdn_norm_w, v_w_out, v_final_norm_w):
    me = 2 * lax.axis_index("x") + lax.axis_index("y")

    g_win, g_wout, g_conv, g_meta = _gather_weights(
        [w_in[0].astype(bf16).reshape(2, D // 2, SHARD_COLS), w_out[0].astype(bf16).reshape(2, D // 8, D)],
        [conv_w[0], meta_tokens])
    w_full = jnp.transpose(g_win.reshape(4, D, SHARD_COLS), (1, 0, 2)).reshape(D, IN_COLS)
    wbig = jnp.pad(w_full, ((0, 0), (0, PC - IN_COLS)))
    wout_full = g_wout.reshape(D, D)
    conv4 = jnp.transpose(g_conv, (1, 0, 2)).reshape(4, 1, 3 * HD)
    meta_full = jnp.transpose(g_meta, (1, 0, 2)).reshape(N_META, D)

    c_arr = lax.axis_index("c").reshape(1).astype(jnp.int32)

    def chip_partials(gw_full, g_wout_part):
        g_in4 = jnp.transpose(gw_full.reshape(D, 4, SHARD_COLS), (1, 0, 2)).reshape(4, 2, D // 2, SHARD_COLS)
        g_out4 = g_wout_part.reshape(4, 2, D // 8, D)
        s_in, s_out = _swap_halves([g_in4, g_out4])
        pb_in, pf_in = _add_halves(c_arr, g_in4, s_in, "add_w_in")
        pb_out, pf_out = _add_halves(c_arr, g_out4, s_out, "add_w_out")
        return [pb_in, pb_out], [pf_in, pf_out]

    (loss8, grad_x, d_meta, d_nw, d_conv, d_lb, d_hgw, d_alog, d_dtb, d_gdw, d_fw, pfs, rs) = _local_step(
        x, loss_target, wbig, wout_full, conv4, meta_full, norm_w, hg_lb_logits, hg_norm_w, gdn_A_log, gdn_dt_bias,
        gdn_norm_w, final_norm_w, chip_partials)

    pack = jnp.concatenate([
        loss8, d_nw[0].reshape(8, 128), d_lb.reshape(8, 128), d_hgw, _rows8(d_alog[0, :H]), _rows8(d_dtb[0, :H]),
        d_gdw, d_fw[0].reshape(8, 128), d_meta.reshape(128, 128), d_conv.reshape(48, 128)], axis=0)
    return _reduce_and_update(
        me, c_arr, grad_x, pfs, rs, pack, meta_tokens, norm_w, w_in, conv_w, hg_lb_logits, hg_norm_w, gdn_A_log,
        gdn_dt_bias, gdn_norm_w, w_out, final_norm_w, m_meta_tokens, m_norm_w, m_w_in, m_conv_w, m_hg_lb_logits,
        m_hg_norm_w, m_gdn_A_log, m_gdn_dt_bias, m_gdn_norm_w, m_w_out, m_final_norm_w, v_meta_tokens, v_norm_w, v_w_in,
        v_conv_w, v_hg_lb_logits, v_hg_norm_w, v_gdn_A_log, v_gdn_dt_bias, v_gdn_norm_w, v_w_out, v_final_norm_w)


def _local_step(x, loss_target, wbig, wout_full, conv4, meta_full, norm_w, hg_lb_logits, hg_norm_w, gdn_A_log, gdn_dt_bias,
                gdn_norm_w, final_norm_w, chip_partials):
    h3 = jnp.concatenate([jnp.zeros((NB, PAD, D), f32), jnp.broadcast_to(meta_full[None], (NB, N_META, D)), x], axis=1)
    hflat = h3.reshape(N, D)
    target = jnp.pad(loss_target, ((0, 0), (PAD + N_META, 0), (0, 0))).reshape(N, D)
    l0, l1 = hg_lb_logits[0:1], hg_lb_logits[1:2]
    alog = jnp.pad(gdn_A_log, ((0, 0), (0, DK - H)))
    dtb = jnp.pad(gdn_dt_bias, ((0, 0), (0, DK - H)))
    fw = final_norm_w.reshape(1, D)

    proj, ut = _in_proj(hflat, norm_w, wbig)
    proj3 = proj.reshape(NB, TP, PC)
    o_hg, s_hg = _hg_fwd(proj3, l0, l1)
    cv = _conv_fwd(proj3, conv4)
    o_gd, s_gd, t_gd = _gd_fwd(cv, proj3, alog, dtb)
    (loss8, d_ohg, d_ogd, d_zhg, d_zgd, dh_res, g_wout_part, d_hgw, d_gdw, d_fw) = _out_loss(
        o_hg.reshape(N, HD), o_gd.reshape(N, HD), proj, hg_norm_w, gdn_norm_w, wout_full, hflat, fw, target)
    d_hg, d_l0, d_l1 = _hg_bwd(proj3, l0, l1, s_hg, d_ohg.reshape(NB, TP, HD))
    d_cv, d_ab, d_alog, d_dtb = _gd_bwd(cv, proj3, alog, dtb, s_gd, t_gd, d_ogd.reshape(NB, TP, HD))
    d_qkv, d_conv4 = _conv_bwd(proj3, conv4, d_cv)
    d_hg2, d_qkv2, d_ab2 = d_hg.reshape(N, 3 * HD), d_qkv.reshape(N, 3 * HD), d_ab.reshape(N, DK)
    pieces = [(d_hg2, COL_HG), (d_zhg, COL_ZHG), (d_qkv2, COL_QKV), (d_zgd, COL_ZGD), (d_ab2, COL_AB)]
    gws = [_w_grad(ut, d, "w_grad_%d" % c0) for d, c0 in pieces]
    gw_full = jnp.concatenate(gws[:4] + [gws[4][:, 0:IN_COLS - COL_AB]], axis=1)
    pbs, pfs = chip_partials(gw_full, g_wout_part) if chip_partials else ([], [gw_full, g_wout_part])
    dh, d_nw, *rs = _in_bwd(pieces, wbig, hflat, norm_w, dh_res, pbs)

    dh3 = dh.reshape(NB, TP, D)
    grad_x = dh3[:, PAD + N_META:, :]
    d_meta = jnp.sum(dh3[:, PAD:PAD + N_META, :], axis=0)
    d_conv = d_conv4[:, 0, :]
    d_lb = jnp.concatenate([d_l0[0:1], d_l1[0:1]], axis=0)
    return loss8, grad_x, d_meta, d_nw, d_conv, d_lb, d_hgw, d_alog, d_dtb, d_gdw, d_fw, pfs, rs


def _reduce_and_update(me, c_arr, grad_x, pfs, rs, pack, meta_tokens, norm_w, w_in, conv_w, hg_lb_logits, hg_norm_w,
                       gdn_A_log, gdn_dt_bias, gdn_norm_w, w_out, final_norm_w, m_meta_tokens, m_norm_w, m_w_in, m_conv_w,
                       m_hg_lb_logits, m_hg_norm_w, m_gdn_A_log, m_gdn_dt_bias, m_gdn_norm_w, m_w_out, m_final_norm_w,
                       v_meta_tokens, v_norm_w, v_w_in, v_conv_w, v_hg_lb_logits, v_hg_norm_w, v_gdn_A_log, v_gdn_dt_bias,
                       v_gdn_norm_w, v_w_out, v_final_norm_w):
    me_arr = me.reshape(1).astype(jnp.int32)
    (pf_in, pf_out), (r_in, r_out) = pfs, rs
    (r_pack,) = _scatter_blocks([], pack)
    f_in = _sum_blocks(me_arr, pf_in, r_in, "sum_w_in")
    f_out = _sum_blocks(me_arr, pf_out, r_out, "sum_w_out")
    o_in, o_out = _swap_finished([f_in, f_out])
    me8_arr = (2 * me + lax.axis_index("c")).reshape(1).astype(jnp.int32)
    small = _sum_packs(me8_arr, pack, r_pack)

    half_out = lambda a: a[0].reshape(2, D // 8, D)
    is0 = lax.axis_index("c") == 0
    g_in = jnp.concatenate([jnp.where(is0, f_in, o_in), jnp.where(is0, o_in, f_in)], axis=0)
    to_cm = lambda a: jnp.transpose(a, (2, 0, 1))
    gi, di, mi, vi = [jnp.transpose(a, (1, 2, 0))[0] for a in _adamw_rows(
        g_in.T.reshape(SHARD_COLS, 1, D), to_cm(w_in), to_cm(m_w_in), to_cm(v_w_in), "adamw_w_in")]
    go, do_, mo, vo = [a.reshape(D // 4, D) for a in _adamw_halves(
        c_arr, f_out, o_out, half_out(w_out), half_out(m_w_out), half_out(v_w_out), "adamw_w_out")]

    g_meta_full = small[64:192].reshape(N_META, D)
    g_meta_loc = lax.dynamic_slice(g_meta_full, (0, me * 256), (N_META, 256))
    gm, dm, mm_, vm = _adamw([g_meta_loc], meta_tokens, m_meta_tokens, v_meta_tokens, "adamw_meta")
    g_conv_full = small[192:240].reshape(4, 1536)
    g_conv_loc = lax.dynamic_slice(g_conv_full, (0, me * 384), (4, 384))
    gc, dc, mc, vc = _adamw([g_conv_loc], conv_w[0], m_conv_w[0], v_conv_w[0], "adamw_conv")

    reps = [(norm_w, m_norm_w, v_norm_w), (hg_lb_logits, m_hg_lb_logits, v_hg_lb_logits),
            (hg_norm_w, m_hg_norm_w, v_hg_norm_w), (gdn_A_log, m_gdn_A_log, v_gdn_A_log),
            (gdn_dt_bias, m_gdn_dt_bias, v_gdn_dt_bias), (gdn_norm_w, m_gdn_norm_w, v_gdn_norm_w),
            (final_norm_w, m_final_norm_w, v_final_norm_w)]
    wp = jnp.concatenate([_rows8(t[0]) for t in reps], axis=0)
    mp = jnp.concatenate([_rows8(t[1]) for t in reps], axis=0)
    vp = jnp.concatenate([_rows8(t[2]) for t in reps], axis=0)
    gr, dr, mr, vr = _adamw([small[8:64]], wp, mp, vp, "adamw_small")

    def unpack(p):
        outs = []
        for i, t in enumerate(reps):
            n = t[0].size
            outs.append(p[8 * i:8 * i + 8].reshape(-1)[:n].reshape(t[0].shape))
        return outs

    def leaves(meta_v, conv_v, in_v, out_v, rep_p):
        nw, lb, hgw, al, db, gdw, fwv = unpack(rep_p)
        return [meta_v, nw, in_v[None], conv_v[None], lb, hgw, al, db, gdw, out_v[None], fwv]

    loss = small[0, 0]
    return (loss, grad_x, *leaves(gm, gc, gi, go, gr), *leaves(dm, dc, di, do_, dr),
            *leaves(mm_, mc, mi, mo, mr), *leaves(vm, vc, vi, vo, vr))
```

```python
import functools

import jax
import jax.numpy as jnp
from jax import lax
from jax.experimental import pallas as pl
from jax.experimental.pallas import tpu as pltpu

f32 = jnp.float32
bf16 = jnp.bfloat16
MESH = pl.DeviceIdType.MESH
ANY = pl.BlockSpec(memory_space=pl.ANY)

D = 1024
NB = 2
N_META = 16
SEQ = 2048
PAD = 48
TP = PAD + N_META + SEQ
C = 64
NCH = TP // C
N = NB * TP
H = 4
DK = 128
HD = H * DK
PC = 4224
IN_COLS = 4104
SHARD_COLS = IN_COLS // 4
COL_HG, COL_ZHG, COL_QKV, COL_ZGD, COL_AB = 0, 3 * HD, 4 * HD, 7 * HD, 8 * HD
EPS = 1e-6
ADAM_LR, ADAM_B1, ADAM_B2, ADAM_EPS, ADAM_WD, ADAM_STEP = 0.001, 0.9, 0.999, 1e-08, 0.01, 10
VMEM_LIMIT = 56 * 1024 * 1024

P_HG = dict(lvl=1, av=1, qs=1, su=1)
P_GD = dict(kk=1, inv=1, sol=1, ws=1, qk=1, o=1, su=1)


def _cp(sem=None, **kw):
    return pltpu.CompilerParams(dimension_semantics=sem, vmem_limit_bytes=VMEM_LIMIT, **kw)


_DIMS = {"nn": (((1,), (0,)), ((), ())), "nt": (((1,), (1,)), ((), ())), "tn": (((0,), (0,)), ((), ()))}


def _split(x):
    hi = x.astype(bf16)
    return hi, (x - hi.astype(f32)).astype(bf16)


def _dg(a, b, kind, passes):
    d = lambda x, y: lax.dot_general(x, y, _DIMS[kind], preferred_element_type=f32)
    if passes == 1:
        return d(a.astype(bf16), b.astype(bf16))
    ah, al = _split(a)
    bh, bl = _split(b)
    return d(ah, bh) + d(ah, bl) + d(al, bh)


@functools.partial(jax.custom_vjp, nondiff_argnums=(2, 3))
def mmx(a, b, kind, passes):
    return _dg(a, b, kind, passes)


def _mmx_fwd(a, b, kind, passes):
    return _dg(a, b, kind, passes), (a, b)


def _mmx_bwd(kind, passes, res, g):
    a, b = res
    if kind == "nn":
        return _dg(g, b, "nt", passes), _dg(a, g, "tn", passes)
    if kind == "nt":
        return _dg(g, b, "nn", passes), _dg(g, a, "tn", passes)
    return _dg(b, g, "nt", passes), _dg(a, g, "nn", passes)


mmx.defvjp(_mmx_fwd, _mmx_bwd)


def _mask_dg(mask, x, kind):
    n = x.shape[1]
    xh, xl = _split(x)
    r = lax.dot_general(mask, jnp.concatenate([xh, xl], axis=1), _DIMS[kind], preferred_element_type=f32)
    return r[:, :n] + r[:, n:]


@jax.custom_vjp
def mask_mm(mask, x):
    return _mask_dg(mask, x, "nn")


def _mask_fwd(mask, x):
    return _mask_dg(mask, x, "nn"), mask


def _mask_bwd(mask, g):
    return None, _mask_dg(mask, g, "tn")


mask_mm.defvjp(_mask_fwd, _mask_bwd)


def bdot(a, b):
    return jnp.dot(a.astype(bf16), b.astype(bf16), preferred_element_type=f32)


def bdot_nt(a, b):
    return lax.dot_general(a.astype(bf16), b.astype(bf16), _DIMS["nt"], preferred_element_type=f32)


def bdot_tn(a, b):
    return lax.dot_general(a.astype(bf16), b.astype(bf16), _DIMS["tn"], preferred_element_type=f32)


def _iota2(n, m):
    return lax.broadcasted_iota(jnp.int32, (n, m), 0), lax.broadcasted_iota(jnp.int32, (n, m), 1)


sigmoid = jax.nn.sigmoid


def silu(x):
    return x * sigmoid(x)


def softplus(x):
    return jnp.maximum(x, 0.0) + jnp.log(1.0 + jnp.exp(-jnp.abs(x)))


def rmsnorm(x, w):
    return x * lax.rsqrt(jnp.mean(x * x, axis=-1, keepdims=True) + EPS) * w


def hg_masks():
    t, r = _iota2(C, C)
    mats = [r <= t, r > t]
    lvl = []
    for l in range(1, 7):
        sz = 1 << l
        half = sz >> 1
        seg_t = t >> l
        upper_t = (t & (sz - 1)) >= half
        mid_t = seg_t * sz + half - 1
        mats.append((upper_t & (r > mid_t) & (r <= t)) | ((~upper_t) & (r > t) & (r <= mid_t)))
        lvl.append(((seg_t == (r >> l)) & upper_t & ((r & (sz - 1)) < half)).astype(f32))
    stk = jnp.concatenate([m.astype(bf16) for m in mats], axis=0)
    return stk, lvl, (t == r).astype(f32)


def _head(a, h):
    return a[:, h * DK:(h + 1) * DK]


def hg_chunk(St, ps, l0, l1):
    m = jnp.maximum(l0, l1)
    e0 = jnp.exp(l0 - m)
    e1 = jnp.exp(l1 - m)
    lb = e0 / (e0 + e1)
    stk, lvl, eye = hg_masks()
    msk = [eye] + lvl
    qs, ks, vs, qG, kR, eGl = [], [], [], [], [], []
    for p in ps:
        pq, pf, v = p[:, 0:HD], p[:, HD:2 * HD], p[:, 2 * HD:3 * HD]
        q = silu(pq)
        f = lb + (1.0 - lb) * sigmoid(pf)
        k = 1.0 - f
        logf = jnp.log(f)
        Dm = mask_mm(stk, logf)
        ex = [jnp.exp(Dm[(2 + i) * C:(3 + i) * C]) for i in range(6)]
        qs.append([q] + [q * e for e in ex])
        ks.append([k] + [k * e for e in ex])
        vs.append(v)
        qG.append(q * jnp.exp(Dm[0:C]))
        kR.append(k * jnp.exp(Dm[C:2 * C]))
        eGl.append(jnp.exp(jnp.sum(logf, axis=0, keepdims=True)))
    units = [(b, h) for b in range(len(ps)) for h in range(H)]
    parts = [[msk[i] * mmx(_head(qs[b][i], h), _head(ks[b][i], h), "nt", P_HG["lvl"]) for b, h in units]
             for i in range(7)]
    A = [functools.reduce(lambda x, y: x + y, [parts[i][n] for i in range(7)]) for n in range(len(units))]
    qS = [mmx(_head(qG[b], h), St[n], "nt", P_HG["qs"]) for n, (b, h) in enumerate(units)]
    Sn = [St[n] * _head(eGl[b], h) + mmx(_head(vs[b], h), _head(kR[b], h), "tn", P_HG["su"])
          for n, (b, h) in enumerate(units)]
    outs = [mmx(A[n], _head(vs[b], h), "nn", P_HG["av"]) + qS[n] for n, (b, h) in enumerate(units)]
    return tuple(Sn), tuple(jnp.concatenate(outs[b * H:(b + 1) * H], axis=1) for b in range(len(ps)))


@jax.custom_vjp
def use_inverse(A, T):
    return T


def _use_inverse_fwd(A, T):
    return T, T


def _use_inverse_bwd(T, g):
    return -_dg(T, _dg(g, T, "nt", P_GD["inv"]), "tn", P_GD["inv"]), jnp.zeros_like(T)


use_inverse.defvjp(_use_inverse_fwd, _use_inverse_bwd)


def gd_chunk(S, cs, abs_, alog, dtb, t_saved=None):
    t, r = _iota2(C, C)
    tri = (r <= t).astype(bf16)
    ups = (r > t).astype(bf16)
    lane = lax.broadcasted_iota(jnp.int32, (1, DK), 1)
    subl = lax.broadcasted_iota(jnp.int32, (8, 1), 0)
    eye = (t == r).astype(f32)
    strict = (r < t).astype(f32)
    bd = ((t >> 4) == (r >> 4)).astype(f32)
    qa, ka, va, b4, gam4, grev4, gam4T, glast4 = [], [], [], [], [], [], [], []
    for c, ab in zip(cs, abs_):
        qa.append(silu(c[:, 0:HD]))
        ka.append(silu(c[:, HD:2 * HD]))
        va.append(silu(c[:, 2 * HD:3 * HD]))
        g4 = -jnp.exp(alog) * softplus(ab + dtb)
        b4.append(sigmoid(ab))
        gam4.append(mask_mm(tri, g4))
        grev4.append(mask_mm(ups, g4))
        gam4T.append(gam4[-1].T)
        glast4.append(jnp.sum(g4, axis=0, keepdims=True))
    units = [(b, h) for b in range(len(cs)) for h in range(H)]
    nu = range(len(units))
    inv = lambda a, b: [mmx(a[n], b[n], "nn", P_GD["inv"]) for n in nu]
    v = [_head(va[b], h) for b, h in units]
    q = [_head(qa[b], h) for b, h in units]
    k = [_head(ka[b], h) for b, h in units]
    q = [x * lax.rsqrt(jnp.sum(x * x, -1, keepdims=True) + EPS) * (DK ** -0.5) for x in q]
    k = [x * lax.rsqrt(jnp.sum(x * x, -1, keepdims=True) + EPS) for x in k]
    oh = [(lane == h).astype(f32) for h in range(H)]
    gam_c = [jnp.sum(gam4[b] * oh[h], -1, keepdims=True) for b, h in units]
    grev_c = [jnp.sum(grev4[b] * oh[h], -1, keepdims=True) for b, h in units]
    beta = [jnp.sum(b4[b] * (lane == h + H).astype(f32), -1, keepdims=True) for b, h in units]
    glast = [jnp.sum(glast4[b] * oh[h], -1, keepdims=True) for b, h in units]
    gam_r = [jnp.sum(gam4T[b][0:8, :] * (subl == h).astype(f32), axis=0, keepdims=True) for b, h in units]
    dec = [jnp.exp(jnp.where(r <= t, gam_c[n] - gam_r[n], -1e30)) for n in nu]
    egam = [jnp.exp(gam_c[n]) for n in nu]
    kk = [mmx(k[n], k[n], "nt", P_GD["kk"]) for n in nu]
    qk = [mmx(q[n], k[n], "nt", P_GD["qk"]) * dec[n] for n in nu]
    A = [beta[n] * kk[n] * dec[n] * strict for n in nu]
    Dg = [A[n] * bd for n in nu]
    L = [A[n] - Dg[n] for n in nu]
    if t_saved is None:
        ImD = [eye - Dg[n] for n in nu]
        D2 = inv(Dg, Dg)
        P1 = inv(ImD, [eye + x for x in D2])
        D4 = inv(D2, D2)
        P2 = inv(P1, [eye + x for x in D4])
        D8 = inv(D4, D4)
        M = inv(P2, [eye + x for x in D8])
        Nn = inv(M, L)
        N2 = inv(Nn, Nn)
        T1 = inv([eye - x for x in Nn], [eye + x for x in N2])
        Tinv = inv(T1, M)
    else:
        Tinv = [use_inverse(A[n], t_saved[b][:, h * DK:h * DK + C]) for n, (b, h) in enumerate(units)]
    rhs = [jnp.concatenate([beta[n] * v[n], (beta[n] * egam[n]) * k[n]], axis=1) for n in nu]
    sol = [mmx(Tinv[n], rhs[n], "nn", P_GD["sol"]) for n in nu]
    qwS = [mmx(jnp.concatenate([q[n] * egam[n], sol[n][:, DK:2 * DK]], axis=0), S[n], "nn", P_GD["ws"]) for n in nu]
    u = [sol[n][:, 0:DK] - qwS[n][C:2 * C] for n in nu]
    outs = [qwS[n][0:C] + mmx(qk[n], u[n], "nn", P_GD["o"]) for n in nu]
    Sn = [jnp.exp(glast[n]) * S[n] + mmx(k[n] * jnp.exp(grev_c[n]), u[n], "tn", P_GD["su"]) for n in nu]
    zpad = jnp.zeros((C, DK - C), f32)
    t_pack = tuple(jnp.concatenate([x for n in range(b * H, (b + 1) * H) for x in (lax.stop_gradient(Tinv[n]), zpad)],
                                   axis=1) for b in range(len(cs)))
    return tuple(Sn), tuple(jnp.concatenate(outs[b * H:(b + 1) * H], axis=1) for b in range(len(cs))), t_pack


def _in_proj(hflat, norm_w, wbig):
    tm = 384

    def body(h_ref, nw_ref, w_ref, p_ref, ut_ref):
        u = rmsnorm(h_ref[...], nw_ref[...])
        ut_ref[...] = u.T.astype(bf16)
        p_ref[...] = jnp.dot(u.astype(bf16), w_ref[...], preferred_element_type=f32)

    return pl.pallas_call(
        body, name="in_proj", grid=(N // tm,),
        in_specs=[pl.BlockSpec((tm, D), lambda i: (i, 0)), pl.BlockSpec((1, D), lambda i: (0, 0)),
                  pl.BlockSpec((D, PC), lambda i: (0, 0))],
        out_specs=[pl.BlockSpec((tm, PC), lambda i: (i, 0)), pl.BlockSpec((D, tm), lambda i: (0, i))],
        out_shape=[jax.ShapeDtypeStruct((N, PC), f32), jax.ShapeDtypeStruct((D, N), bf16)],
        compiler_params=_cp(("arbitrary",)),
    )(hflat, norm_w, wbig)


NU = NB * H
_REV = lambda c: NCH - 1 - c
_FWD = lambda c: c


def _tok_spec(w, ix, col=0):
    return pl.BlockSpec((NB, C, w), lambda c: (0, ix(c), col))


def _state_spec(ix):
    return pl.BlockSpec((NB, 1, H, DK, DK), lambda c: (0, ix(c), 0, 0, 0))


def _row_spec(w):
    return pl.BlockSpec((1, w), lambda c: (0, 0))


def _rows(ref):
    return tuple(ref[b] for b in range(NB))


def _scan_fwd(chunk_fn, name, x3, extra, extra_specs, n_tok, aux_w):
    ne = len(extra)
    na = 1 if aux_w else 0

    def body(*refs):
        x_ref = refs[0]
        e_refs = refs[1:1 + ne]
        o_ref, s_ref = refs[1 + ne:3 + ne]
        aux_refs = refs[3 + ne:3 + ne + na]
        st = refs[3 + ne + na]

        @pl.when(pl.program_id(0) == 0)
        def _():
            st[...] = jnp.zeros_like(st)

        S = tuple(st[n] for n in range(NU))
        for n in range(NU):
            s_ref[n // H, 0, n % H] = S[n]
        args = [_rows(e) if i < n_tok else e[...] for i, e in enumerate(e_refs)]
        Sn, o, *aux = chunk_fn(S, _rows(x_ref), *args)
        for n in range(NU):
            st[n] = Sn[n]
        for b in range(NB):
            o_ref[b] = o[b]
            for a_ref, a in zip(aux_refs, aux):
                a_ref[b] = a[b]

    return pl.pallas_call(
        body, name=name, grid=(NCH,),
        in_specs=[_tok_spec(3 * HD, _FWD)] + extra_specs(_FWD),
        out_specs=[_tok_spec(HD, _FWD), _state_spec(_FWD)] + [_tok_spec(aux_w, _FWD)] * na,
        out_shape=[jax.ShapeDtypeStruct((NB, TP, HD), f32), jax.ShapeDtypeStruct((NB, NCH, H, DK, DK), f32)]
        + [jax.ShapeDtypeStruct((NB, TP, aux_w), f32)] * na,
        scratch_shapes=[pltpu.VMEM((NU, DK, DK), f32)],
        compiler_params=_cp(("arbitrary",)),
    )(x3, *extra)


def _hg_extra_specs(ix):
    return [_row_spec(HD), _row_spec(HD)]


def _gd_extra_specs(ix):
    return [_tok_spec(DK, ix, COL_AB // DK), _row_spec(DK), _row_spec(DK)]


def _hg_fwd(proj3, l0, l1):
    return _scan_fwd(hg_chunk, "hg_fwd", proj3, [l0, l1], _hg_extra_specs, 0, 0)


def _gd_fwd(cv, proj3, alog, dtb):
    return _scan_fwd(gd_chunk, "gd_fwd", cv, [proj3, alog, dtb], _gd_extra_specs, 1, H * DK)


def _hg_bwd(proj3, l0, l1, s_saved, do):
    def body(p_ref, l0_ref, l1_ref, s_ref, do_ref, dp_ref, dl0_ref, dl1_ref, dst):
        @pl.when(pl.program_id(0) == 0)
        def _():
            dst[...] = jnp.zeros_like(dst)
            dl0_ref[...] = jnp.zeros_like(dl0_ref)
            dl1_ref[...] = jnp.zeros_like(dl1_ref)

        S = tuple(s_ref[n // H, 0, n % H] for n in range(NU))
        _, vjp = jax.vjp(hg_chunk, S, _rows(p_ref), l0_ref[...], l1_ref[...])
        dS, dp, dl0, dl1 = vjp((tuple(dst[n] for n in range(NU)), _rows(do_ref)))
        for n in range(NU):
            dst[n] = dS[n]
        for b in range(NB):
            dp_ref[b] = dp[b].astype(bf16)
        dl0_ref[...] += jnp.broadcast_to(dl0, (8, HD))
        dl1_ref[...] += jnp.broadcast_to(dl1, (8, HD))

    acc = pl.BlockSpec((8, HD), lambda c: (0, 0))
    return pl.pallas_call(
        body, name="hg_bwd", grid=(NCH,),
        in_specs=[_tok_spec(3 * HD, _REV)] + _hg_extra_specs(_REV) + [_state_spec(_REV), _tok_spec(HD, _REV)],
        out_specs=[_tok_spec(3 * HD, _REV), acc, acc],
        out_shape=[jax.ShapeDtypeStruct((NB, TP, 3 * HD), bf16), jax.ShapeDtypeStruct((8, HD), f32),
                   jax.ShapeDtypeStruct((8, HD), f32)],
        scratch_shapes=[pltpu.VMEM((NU, DK, DK), f32)],
        compiler_params=_cp(("arbitrary",)),
    )(proj3, l0, l1, s_saved, do)


def _gd_bwd(cv, proj3, alog, dtb, s_saved, t_saved, do):
    def body(c_ref, ab_ref, al_ref, db_ref, s_ref, t_ref, do_ref, dc_ref, dab_ref, dal_ref, ddb_ref, dst):
        @pl.when(pl.program_id(0) == 0)
        def _():
            dst[...] = jnp.zeros_like(dst)
            dal_ref[...] = jnp.zeros_like(dal_ref)
            ddb_ref[...] = jnp.zeros_like(ddb_ref)

        S = tuple(s_ref[n // H, 0, n % H] for n in range(NU))
        t_rows = _rows(t_ref)
        fn = lambda *a: gd_chunk(*a, t_saved=t_rows)[0:2]
        _, vjp = jax.vjp(fn, S, _rows(c_ref), _rows(ab_ref), al_ref[...], db_ref[...])
        dS, dc, dab, dal, ddb = vjp((tuple(dst[n] for n in range(NU)), _rows(do_ref)))
        for n in range(NU):
            dst[n] = dS[n]
        for b in range(NB):
            dc_ref[b] = dc[b]
            dab_ref[b] = dab[b].astype(bf16)
        dal_ref[...] += jnp.broadcast_to(dal, (8, DK))
        ddb_ref[...] += jnp.broadcast_to(ddb, (8, DK))

    acc = pl.BlockSpec((8, DK), lambda c: (0, 0))
    return pl.pallas_call(
        body, name="gd_bwd", grid=(NCH,),
        in_specs=[_tok_spec(3 * HD, _REV)] + _gd_extra_specs(_REV)
        + [_state_spec(_REV), _tok_spec(HD, _REV), _tok_spec(HD, _REV)],
        out_specs=[_tok_spec(3 * HD, _REV), _tok_spec(DK, _REV), acc, acc],
        out_shape=[jax.ShapeDtypeStruct((NB, TP, 3 * HD), f32), jax.ShapeDtypeStruct((NB, TP, DK), bf16),
                   jax.ShapeDtypeStruct((8, DK), f32), jax.ShapeDtypeStruct((8, DK), f32)],
        scratch_shapes=[pltpu.VMEM((NU, DK, DK), f32)],
        compiler_params=_cp(("arbitrary",)),
    )(cv, proj3, alog, dtb, s_saved, t_saved, do)


def _conv_fwd(proj3, conv4):
    def body(x_ref, w_ref, y_ref):
        x = x_ref[0]
        row = lax.broadcasted_iota(jnp.int32, (TP, 1), 0)
        y = w_ref[3] * x
        for s in (1, 2, 3):
            y = y + w_ref[3 - s] * jnp.where(row >= s, pltpu.roll(x, s, 0), 0.0)
        y_ref[0] = y

    return pl.pallas_call(
        body, name="conv_fwd", grid=(NB, 3),
        in_specs=[pl.BlockSpec((1, TP, HD), lambda b, j: (b, 0, COL_QKV // HD + j)),
                  pl.BlockSpec((4, 1, HD), lambda b, j: (0, 0, j))],
        out_specs=pl.BlockSpec((1, TP, HD), lambda b, j: (b, 0, j)),
        out_shape=jax.ShapeDtypeStruct((NB, TP, 3 * HD), f32),
        compiler_params=_cp(("arbitrary", "arbitrary")),
    )(proj3, conv4)


def _conv_bwd(proj3, conv4, dy):
    def body(x_ref, w_ref, dy_ref, dx_ref, dw_ref):
        @pl.when(pl.program_id(1) == 0)
        def _():
            dw_ref[...] = jnp.zeros_like(dw_ref)

        x = x_ref[0]
        g = dy_ref[0]
        row = lax.broadcasted_iota(jnp.int32, (TP, 1), 0)
        dx = w_ref[3] * g
        dw_ref[3] += jnp.broadcast_to(jnp.sum(x * g, axis=0, keepdims=True), (8, HD))
        for s in (1, 2, 3):
            dx = dx + w_ref[3 - s] * jnp.where(row < TP - s, pltpu.roll(g, TP - s, 0), 0.0)
            xs = jnp.where(row >= s, pltpu.roll(x, s, 0), 0.0)
            dw_ref[3 - s] += jnp.broadcast_to(jnp.sum(xs * g, axis=0, keepdims=True), (8, HD))
        dx_ref[0] = dx.astype(bf16)

    return pl.pallas_call(
        body, name="conv_bwd", grid=(3, NB),
        in_specs=[pl.BlockSpec((1, TP, HD), lambda j, b: (b, 0, COL_QKV // HD + j)),
                  pl.BlockSpec((4, 1, HD), lambda j, b: (0, 0, j)), pl.BlockSpec((1, TP, HD), lambda j, b: (b, 0, j))],
        out_specs=[pl.BlockSpec((1, TP, HD), lambda j, b: (b, 0, j)), pl.BlockSpec((4, 8, HD), lambda j, b: (0, 0, j))],
        out_shape=[jax.ShapeDtypeStruct((NB, TP, 3 * HD), bf16), jax.ShapeDtypeStruct((4, 8, 3 * HD), f32)],
        compiler_params=_cp(("arbitrary", "arbitrary")),
    )(proj3, conv4, dy)


def _gated(o, z, w):
    outs = []
    for hh in range(H):
        sl = slice(hh * DK, (hh + 1) * DK)
        outs.append(rmsnorm(o[:, sl], w) * silu(z[:, sl]))
    return jnp.concatenate(outs, axis=-1)


def _out_loss(o_hg, o_gd, proj, hgw, gdw, wout, hflat, fw, target):
    tm = 384

    def body(ohg_ref, ogd_ref, zhg_ref, zgd_ref, hgw_ref, gdw_ref, wo_ref, h_ref, fw_ref, tg_ref,
             loss_ref, dohg_ref, dogd_ref, dzhg_ref, dzgd_ref, dh_ref, dwo_ref, dhgw_ref, dgdw_ref, dfw_ref):
        i = pl.program_id(0)

        @pl.when(i == 0)
        def _():
            for r in (loss_ref, dwo_ref, dhgw_ref, dgdw_ref, dfw_ref):
                r[...] = jnp.zeros_like(r)

        row = i * tm + lax.broadcasted_iota(jnp.int32, (tm, 1), 0)
        tok = jnp.where(row >= TP, row - TP, row)
        valid = (tok >= PAD + N_META).astype(f32)
        hval = h_ref[...]
        tgt = tg_ref[...]

        def mix(ohg, ogd, zhg, zgd, w1, w2):
            return jnp.concatenate([_gated(ohg, zhg, w1), _gated(ogd, zgd, w2)], axis=-1)

        y, vjp_mix = jax.vjp(mix, ohg_ref[...], ogd_ref[...], zhg_ref[...], zgd_ref[...], hgw_ref[...], gdw_ref[...])
        out = bdot(y, wo_ref[...])

        def head(out, fwv):
            err = (rmsnorm(hval + out, fwv) - tgt) * valid
            return 0.5 * jnp.sum(jnp.mean(err * err, axis=-1, keepdims=True))

        loss, vjp_head = jax.vjp(head, out, fw_ref[...])
        dout, dfw = vjp_head(jnp.ones((), f32))
        dh_ref[...] = dout
        dy = bdot_nt(dout, wo_ref[...])
        dwo_ref[...] += bdot_tn(y, dout)
        dohg, dogd, dzhg, dzgd, dw1, dw2 = vjp_mix(dy)
        dohg_ref[...] = dohg
        dogd_ref[...] = dogd
        dzhg_ref[...] = dzhg.astype(bf16)
        dzgd_ref[...] = dzgd.astype(bf16)
        loss_ref[...] += jnp.broadcast_to(loss, (8, DK))
        dhgw_ref[...] += jnp.broadcast_to(dw1, (8, DK))
        dgdw_ref[...] += jnp.broadcast_to(dw2, (8, DK))
        dfw_ref[...] += jnp.broadcast_to(dfw, (8, D))

    row = lambda w: pl.BlockSpec((tm, w), lambda i: (i, 0))
    whole = lambda r, w: pl.BlockSpec((r, w), lambda i: (0, 0))
    col = lambda c0: pl.BlockSpec((tm, HD), lambda i: (i, c0 // HD))
    return pl.pallas_call(
        body, name="out_loss", grid=(N // tm,),
        in_specs=[row(HD), row(HD), col(COL_ZHG), col(COL_ZGD),
                  whole(1, DK), whole(1, DK), whole(D, D), row(D), whole(1, D), row(D)],
        out_specs=[whole(8, DK), row(HD), row(HD), row(HD), row(HD), row(D), whole(D, D),
                   whole(8, DK), whole(8, DK), whole(8, D)],
        out_shape=[jax.ShapeDtypeStruct((8, DK), f32)] + [jax.ShapeDtypeStruct((N, HD), f32)] * 2
        + [jax.ShapeDtypeStruct((N, HD), bf16)] * 2
        + [jax.ShapeDtypeStruct((N, D), f32), jax.ShapeDtypeStruct((D, D), f32),
           jax.ShapeDtypeStruct((8, DK), f32), jax.ShapeDtypeStruct((8, DK), f32), jax.ShapeDtypeStruct((8, D), f32)],
        compiler_params=_cp(("arbitrary",)),
    )(o_hg, o_gd, proj, proj, hgw, gdw, wout, hflat, fw, target)


def _in_bwd(pieces, wbig, hflat, norm_w, dh_res, pbs):
    tm = 384
    nsteps = N // tm
    np_ = len(pieces)
    na = len(pbs)
    offs = [c0 for _, c0 in pieces]
    widths = [d.shape[1] for d, _ in pieces]

    def body(*refs):
        d_refs = refs[:np_]
        w_ref, h_ref, nw_ref, dhr_ref = refs[np_:np_ + 4]
        srcs = refs[np_ + 4:np_ + 4 + na]
        dh_ref, dnw_ref = refs[np_ + 4 + na:np_ + 6 + na]
        dsts = refs[np_ + 6 + na:np_ + 6 + 2 * na]
        sems = refs[np_ + 6 + 2 * na:]
        i = pl.program_id(0)

        def copies():
            if not na:
                return []
            x, y, c, chips = _place()
            return [pltpu.make_async_remote_copy(
                src_ref=srcs[a].at[2 * px + py], dst_ref=dsts[a].at[j], send_sem=sems[0].at[na * j + a],
                recv_sem=sems[1].at[na * j + a], device_id=(px, py, c), device_id_type=MESH)
                for j, (px, py) in enumerate(chips) for a in range(na)]

        @pl.when(i == 0)
        def _():
            dnw_ref[...] = jnp.zeros_like(dnw_ref)
            for cp in copies():
                cp.start()

        du = jnp.zeros((tm, D), f32)
        for d_ref, off, wd in zip(d_refs, offs, widths):
            du = du + bdot_nt(d_ref[...], w_ref[:, off:off + wd])
        _, vjp = jax.vjp(rmsnorm, h_ref[...], nw_ref[...])
        dh, dnw = vjp(du)
        dh_ref[...] = dh + dhr_ref[...]
        dnw_ref[...] += jnp.broadcast_to(dnw, (8, D))

        @pl.when(i == nsteps - 1)
        def _():
            for cp in copies():
                cp.wait()

    row = lambda w: pl.BlockSpec((tm, w), lambda i: (i, 0))
    return pl.pallas_call(
        body, name="in_bwd", grid=(nsteps,),
        in_specs=[row(w) for w in widths]
        + [pl.BlockSpec((D, PC), lambda i: (0, 0)), row(D), pl.BlockSpec((1, D), lambda i: (0, 0)), row(D)] + [ANY] * na,
        out_specs=[row(D), pl.BlockSpec((8, D), lambda i: (0, 0))] + [ANY] * na,
        out_shape=[jax.ShapeDtypeStruct((N, D), f32), jax.ShapeDtypeStruct((8, D), f32)]
        + [jax.ShapeDtypeStruct((3,) + p.shape[1:], p.dtype) for p in pbs],
        scratch_shapes=[pltpu.SemaphoreType.DMA((3 * na,)), pltpu.SemaphoreType.DMA((3 * na,))] if na else [],
        compiler_params=_cp(("arbitrary",)),
    )(*[d for d, _ in pieces], wbig, hflat, norm_w, dh_res, *pbs)


def _w_grad(ut, pieces):
    tk = 384
    offs = [c0 for _, c0 in pieces]
    widths = [d.shape[1] for d, _ in pieces]

    def body(u_ref, *refs):
        d_refs, o_ref = refs[:-1], refs[-1]

        @pl.when(pl.program_id(0) == 0)
        def _():
            o_ref[...] = jnp.zeros_like(o_ref)

        u = u_ref[...]
        for d_ref, off, wd in zip(d_refs, offs, widths):
            o_ref[:, off:off + wd] += jnp.dot(u, d_ref[...], preferred_element_type=f32)

    return pl.pallas_call(
        body, name="w_grad", grid=(N // tk,),
        in_specs=[pl.BlockSpec((D, tk), lambda k: (0, k))] + [pl.BlockSpec((tk, w), lambda k: (k, 0)) for w in widths],
        out_specs=pl.BlockSpec((D, PC), lambda k: (0, 0)),
        out_shape=jax.ShapeDtypeStruct((D, PC), f32),
        compiler_params=_cp(("arbitrary",)),
    )(ut, *[d for d, _ in pieces])


def _adam_math(g, w, m, v):
    m2 = ADAM_B1 * m + (1.0 - ADAM_B1) * g
    v2 = ADAM_B2 * v + (1.0 - ADAM_B2) * (g * g)
    m_hat = m2 / (1.0 - ADAM_B1 ** ADAM_STEP)
    v_hat = v2 / (1.0 - ADAM_B2 ** ADAM_STEP)
    delta = -ADAM_LR * (m_hat / (jnp.sqrt(v_hat) + ADAM_EPS) + ADAM_WD * w)
    return delta, m2, v2


def _adamw(gs, w, m, v, name):
    R, Cc = w.shape
    tr = 256 if R % 256 == 0 else R
    ng = len(gs)

    def body(*refs):
        g = refs[0][...]
        for r in refs[1:ng]:
            g = g + r[...]
        w_ref, m_ref, v_ref, g_ref, d_ref, m2_ref, v2_ref = refs[ng:]
        delta, m2, v2 = _adam_math(g, w_ref[...], m_ref[...], v_ref[...])
        g_ref[...] = g
        d_ref[...] = delta
        m2_ref[...] = m2
        v2_ref[...] = v2

    spec = pl.BlockSpec((tr, Cc), lambda i: (i, 0))
    return pl.pallas_call(
        body, name=name, grid=(R // tr,),
        in_specs=[spec] * (ng + 3), out_specs=[spec] * 4,
        out_shape=[jax.ShapeDtypeStruct((R, Cc), f32)] * 4,
        compiler_params=_cp(("arbitrary",)),
    )(*gs, w, m, v)


def _adamw_rows(g, w, m, v, name):
    R, _, Cc = w.shape
    tr = R // 9

    def body(g_ref, w_ref, m_ref, v_ref, go_ref, d_ref, m2_ref, v2_ref):
        g = g_ref[...]
        delta, m2, v2 = _adam_math(g, w_ref[...], m_ref[...], v_ref[...])
        go_ref[...] = g
        d_ref[...] = delta
        m2_ref[...] = m2
        v2_ref[...] = v2

    spec = pl.BlockSpec((tr, 1, Cc), lambda i: (i, 0, 0))
    return pl.pallas_call(
        body, name=name, grid=(R // tr,),
        in_specs=[spec] * 4, out_specs=[spec] * 4,
        out_shape=[jax.ShapeDtypeStruct((R, 1, Cc), f32)] * 4,
        compiler_params=_cp(("arbitrary",)),
    )(g, w, m, v)


def _sum_slots(r, name):
    S, R, Cc = r.shape
    tr = 256 if R % 256 == 0 else R

    def body(r_ref, o_ref):
        acc = r_ref[0]
        for s in range(1, S):
            acc = acc + r_ref[s]
        o_ref[...] = acc

    return pl.pallas_call(
        body, name=name, grid=(R // tr,),
        in_specs=[pl.BlockSpec((S, tr, Cc), lambda i: (0, i, 0))], out_specs=pl.BlockSpec((tr, Cc), lambda i: (i, 0)),
        out_shape=jax.ShapeDtypeStruct((R, Cc), f32),
        compiler_params=_cp(("arbitrary",)),
    )(r)


def _place():
    x, y, c = lax.axis_index("x"), lax.axis_index("y"), lax.axis_index("c")
    return x, y, c, [(1 - x, y), (x, 1 - y), (1 - x, 1 - y)]


def _gather_weights(halved, whole):
    nh, nw = len(halved), len(whole)
    na = nh + nw

    def body(*refs):
        srcs, dsts = refs[:na], refs[na:2 * na]
        send_sems, recv_sems, loc_sems = refs[2 * na:2 * na + 3]
        stage = refs[2 * na + 3:]
        x, y, c, chips = _place()
        me = 2 * x + y
        loads = [pltpu.make_async_copy(s, v, loc_sems.at[i]) for i, (s, v) in enumerate(zip(srcs, stage))]
        locs = [pltpu.make_async_copy(v, d.at[me], loc_sems.at[i]) for i, (v, d) in enumerate(zip(stage, dsts))]
        for cp in loads:
            cp.start()

        def ici(j, i, slot):
            px, py = chips[j]
            src = srcs[i].at[c] if i < nh else srcs[i]
            dst = dsts[i].at[slot, c] if i < nh else dsts[i].at[slot]
            return pltpu.make_async_remote_copy(
                src_ref=src, dst_ref=dst, send_sem=send_sems.at[na * j + i], recv_sem=recv_sems.at[na * j + i],
                device_id=(px, py, c), device_id_type=MESH)

        def d2d(j, i, half):
            px, py = chips[j]
            blk = dsts[i].at[2 * px + py, half]
            return pltpu.make_async_remote_copy(
                src_ref=blk, dst_ref=blk, send_sem=send_sems.at[3 * na + nh * j + i],
                recv_sem=recv_sems.at[3 * na + nh * j + i], device_id=(x, y, 1 - c), device_id_type=MESH)

        sends = [ici(j, i, me) for j in range(3) for i in range(na)]
        for cp in sends:
            cp.start()
        for ld, st in zip(loads, locs):
            ld.wait()
            st.start()
        for j, (px, py) in enumerate(chips):
            for i in range(na):
                ici(j, i, 2 * px + py).wait_recv()
                if i < nh:
                    fwd = d2d(j, i, c)
                    fwd.start()
                    sends.append(fwd)
        for j in range(3):
            for i in range(nh):
                d2d(j, i, 1 - c).wait_recv()
        for cp in sends:
            cp.wait_send()
        for cp in locs:
            cp.wait()

    nsem = 3 * na + 3 * nh
    return pl.pallas_call(
        body, name="gather_weights",
        in_specs=[ANY] * na, out_specs=[ANY] * na,
        out_shape=[jax.ShapeDtypeStruct((4,) + s.shape, s.dtype) for s in list(halved) + list(whole)],
        scratch_shapes=[pltpu.SemaphoreType.DMA((nsem,)), pltpu.SemaphoreType.DMA((nsem,)),
                        pltpu.SemaphoreType.DMA((na,))] + [pltpu.VMEM(s.shape, s.dtype) for s in list(halved) + list(whole)],
        compiler_params=pltpu.CompilerParams(has_side_effects=True, vmem_limit_bytes=VMEM_LIMIT),
    )(*halved, *whole)


def _swap_halves(gs):
    na = len(gs)
    jobs = [(i, q) for i in range(na) for q in range(gs[i].shape[0])]

    def body(*refs):
        srcs, dsts = refs[:na], refs[na:2 * na]
        send_sems, recv_sems = refs[2 * na:]
        x, y, c, _ = _place()
        cps = [pltpu.make_async_remote_copy(
            src_ref=srcs[i].at[q, 1 - c], dst_ref=dsts[i].at[q], send_sem=send_sems.at[k],
            recv_sem=recv_sems.at[k], device_id=(x, y, 1 - c), device_id_type=MESH)
            for k, (i, q) in enumerate(jobs)]
        for cp in cps:
            cp.start()
        for cp in cps:
            cp.wait()

    return pl.pallas_call(
        body, name="swap_halves",
        in_specs=[ANY] * na, out_specs=[ANY] * na,
        out_shape=[jax.ShapeDtypeStruct(g.shape[0:1] + g.shape[2:], g.dtype) for g in gs],
        scratch_shapes=[pltpu.SemaphoreType.DMA((len(jobs),)), pltpu.SemaphoreType.DMA((len(jobs),))],
        compiler_params=pltpu.CompilerParams(has_side_effects=True),
    )(*gs)


def _add_halves(c_arr, g, s, name):
    Q, _, R, Cc = g.shape
    tr = min(R, 128)

    def body(c_ref, g_ref, s_ref, b_ref, f_ref):
        p = g_ref[0, 0] + s_ref[0]
        f_ref[0] = p
        b_ref[0] = p.astype(bf16)

    blk = pl.BlockSpec((1, tr, Cc), lambda q, i, cr: (q, i, 0))
    return pl.pallas_call(
        body, name=name,
        grid_spec=pltpu.PrefetchScalarGridSpec(
            num_scalar_prefetch=1, grid=(Q, R // tr),
            in_specs=[pl.BlockSpec((1, 1, tr, Cc), lambda q, i, cr: (q, cr[0], i, 0)), blk], out_specs=[blk, blk]),
        out_shape=[jax.ShapeDtypeStruct((Q, R, Cc), bf16), jax.ShapeDtypeStruct((Q, R, Cc), f32)],
        compiler_params=_cp(("arbitrary", "arbitrary")),
    )(c_arr, g, s)


_FLIPS = [(fx, fy, fc) for fx in (0, 1) for fy in (0, 1) for fc in (0, 1)][1:]


def _scatter_blocks(pbs, pack):
    na = len(pbs)
    R = pack.shape[0]

    def body(*refs):
        srcs, pk = refs[:na], refs[na]
        dsts, rp = refs[na + 1:2 * na + 1], refs[2 * na + 1]
        send_sems, recv_sems = refs[2 * na + 2:]
        x, y, c, chips = _place()

        def big(j, i):
            px, py = chips[j]
            return pltpu.make_async_remote_copy(
                src_ref=srcs[i].at[2 * px + py], dst_ref=dsts[i].at[j], send_sem=send_sems.at[na * j + i],
                recv_sem=recv_sems.at[na * j + i], device_id=(px, py, c), device_id_type=MESH)

        def small(k):
            fx, fy, fc = _FLIPS[k]
            return pltpu.make_async_remote_copy(
                src_ref=pk, dst_ref=rp.at[k], send_sem=send_sems.at[3 * na + k], recv_sem=recv_sems.at[3 * na + k],
                device_id=(x ^ fx, y ^ fy, c ^ fc), device_id_type=MESH)

        cps = [big(j, i) for j in range(3) for i in range(na)] + [small(k) for k in range(7)]
        for cp in cps:
            cp.start()
        for cp in cps:
            cp.wait()

    nsem = 3 * na + 7
    return pl.pallas_call(
        body, name="scatter_blocks",
        in_specs=[ANY] * (na + 1), out_specs=[ANY] * (na + 1),
        out_shape=[jax.ShapeDtypeStruct((3,) + p.shape[1:], p.dtype) for p in pbs]
        + [jax.ShapeDtypeStruct((7, R, 128), f32)],
        scratch_shapes=[pltpu.SemaphoreType.DMA((nsem,)), pltpu.SemaphoreType.DMA((nsem,))],
        compiler_params=pltpu.CompilerParams(has_side_effects=True),
    )(*pbs, pack)


def _sum_blocks(own, r, name):
    R, Cc = own.shape
    tr = min(R, 256)

    def body(own_ref, r_ref, o_ref):
        acc = own_ref[...]
        for j in range(3):
            acc = acc + r_ref[j].astype(f32)
        o_ref[...] = acc

    return pl.pallas_call(
        body, name=name, grid=(R // tr,),
        in_specs=[pl.BlockSpec((tr, Cc), lambda i: (i, 0)), pl.BlockSpec((3, tr, Cc), lambda i: (0, i, 0))],
        out_specs=pl.BlockSpec((tr, Cc), lambda i: (i, 0)),
        out_shape=jax.ShapeDtypeStruct((R, Cc), f32),
        compiler_params=_cp(("arbitrary",)),
    )(own, r)


def _sum_packs(me8_arr, pack, rp):
    R = pack.shape[0]

    def body(me_ref, pk_ref, rp_ref, o_ref):
        me8 = me_ref[0]
        acc = None
        for d in range(8):
            rel = d ^ me8
            term = jnp.where(rel == 0, pk_ref[...], rp_ref[jnp.maximum(rel - 1, 0)])
            acc = term if acc is None else acc + term
        o_ref[...] = acc

    return pl.pallas_call(
        body, name="sum_packs",
        grid_spec=pltpu.PrefetchScalarGridSpec(
            num_scalar_prefetch=1, grid=(1,),
            in_specs=[pl.BlockSpec((R, 128), lambda i, mr: (0, 0)), pl.BlockSpec((7, R, 128), lambda i, mr: (0, 0, 0))],
            out_specs=pl.BlockSpec((R, 128), lambda i, mr: (0, 0))),
        out_shape=jax.ShapeDtypeStruct((R, 128), f32),
        compiler_params=_cp(("arbitrary",)),
    )(me8_arr, pack, rp)


def _swap_finished(fs):
    na = len(fs)

    def body(*refs):
        srcs, dsts = refs[:na], refs[na:2 * na]
        send_sems, recv_sems = refs[2 * na:]
        x, y, c, _ = _place()
        cps = [pltpu.make_async_remote_copy(
            src_ref=srcs[i], dst_ref=dsts[i], send_sem=send_sems.at[i], recv_sem=recv_sems.at[i],
            device_id=(x, y, 1 - c), device_id_type=MESH) for i in range(na)]
        for cp in cps:
            cp.start()
        for cp in cps:
            cp.wait()

    return pl.pallas_call(
        body, name="swap_finished",
        in_specs=[ANY] * na, out_specs=[ANY] * na,
        out_shape=[jax.ShapeDtypeStruct(f.shape, f.dtype) for f in fs],
        scratch_shapes=[pltpu.SemaphoreType.DMA((na,)), pltpu.SemaphoreType.DMA((na,))],
        compiler_params=pltpu.CompilerParams(has_side_effects=True),
    )(*fs)


def _adamw_halves(c_arr, mine, peer, w, m, v, name):
    _, R, Cc = w.shape
    tr = min(R, 256)

    def body(c_ref, mine_ref, peer_ref, w_ref, m_ref, v_ref, g_ref, d_ref, m2_ref, v2_ref):
        g = jnp.where(pl.program_id(0) == c_ref[0], mine_ref[...], peer_ref[...])
        delta, m2, v2 = _adam_math(g, w_ref[0], m_ref[0], v_ref[0])
        g_ref[0] = g
        d_ref[0] = delta
        m2_ref[0] = m2
        v2_ref[0] = v2

    half = pl.BlockSpec((tr, Cc), lambda hh, i, cr: (i, 0))
    full = pl.BlockSpec((1, tr, Cc), lambda hh, i, cr: (hh, i, 0))
    return pl.pallas_call(
        body, name=name,
        grid_spec=pltpu.PrefetchScalarGridSpec(
            num_scalar_prefetch=1, grid=(2, R // tr), in_specs=[half, half, full, full, full], out_specs=[full] * 4),
        out_shape=[jax.ShapeDtypeStruct((2, R, Cc), f32)] * 4,
        compiler_params=_cp(("arbitrary", "arbitrary")),
    )(c_arr, mine, peer, w, m, v)


def _rows8(a):
    flat = a.reshape(-1)
    n = flat.shape[0]
    rows = -(-n // 1024) * 8
    return jnp.pad(flat, (0, rows * 128 - n)).reshape(rows, 128)


def kernel(x, meta_tokens, norm_w, w_in, conv_w, hg_lb_logits, hg_norm_w, gdn_A_log, gdn_dt_bias, gdn_norm_w, w_out, final_norm_w, loss_target, m_meta_tokens, m_norm_w, m_w_in, m_conv_w, m_hg_lb_logits, m_hg_norm_w, m_gdn_A_log, m_gdn_dt_bias, m_gdn_norm_w, m_w_out, m_final_norm_w, v_meta_tokens, v_norm_w, v_w_in, v_conv_w, v_hg_lb_logits, v_hg_norm_w, v_gdn_A_log, v_gdn_dt_bias, v_gdn_norm_w, v_w_out, v_final_norm_w):
    me = 2 * lax.axis_index("x") + lax.axis_index("y")

    g_win, g_wout, g_conv, g_meta = _gather_weights(
        [w_in[0].astype(bf16).reshape(2, D // 2, SHARD_COLS), w_out[0].astype(bf16).reshape(2, D // 8, D)],
        [conv_w[0], meta_tokens])
    w_full = jnp.transpose(g_win.reshape(4, D, SHARD_COLS), (1, 0, 2)).reshape(D, IN_COLS)
    wbig = jnp.pad(w_full, ((0, 0), (0, PC - IN_COLS)))
    wout_full = g_wout.reshape(D, D)
    conv4 = jnp.transpose(g_conv, (1, 0, 2)).reshape(4, 1, 3 * HD)
    meta_full = jnp.transpose(g_meta, (1, 0, 2)).reshape(N_META, D)

    c_arr = lax.axis_index("c").reshape(1).astype(jnp.int32)

    def chip_partials(gw, g_wout_part):
        g_in2 = gw.reshape(1, 2, D // 2, PC)
        g_out4 = g_wout_part.reshape(4, 2, D // 8, D)
        s_in, s_out = _swap_halves([g_in2, g_out4])
        pb_in, pf_in = _add_halves(c_arr, g_in2, s_in, "add_w_in")
        pb_out, pf_out = _add_halves(c_arr, g_out4, s_out, "add_w_out")
        pb_blocks = jnp.transpose(pb_in[0, :, 0:IN_COLS].reshape(D // 2, 4, SHARD_COLS), (1, 0, 2))
        own_in = lax.dynamic_slice(pf_in[0], (0, me * SHARD_COLS), (D // 2, SHARD_COLS))
        own_out = lax.dynamic_index_in_dim(pf_out, me, axis=0, keepdims=False)
        return [pb_blocks, pb_out], [own_in, own_out]

    (loss8, grad_x, d_meta, d_nw, d_conv, d_lb, d_hgw, d_alog, d_dtb, d_gdw, d_fw, pfs, rs) = _local_step(
        x, loss_target, wbig, wout_full, conv4, meta_full, norm_w, hg_lb_logits, hg_norm_w, gdn_A_log, gdn_dt_bias,
        gdn_norm_w, final_norm_w, chip_partials)

    pack = jnp.concatenate([
        loss8, d_nw[0].reshape(8, 128), d_lb.reshape(8, 128), d_hgw, _rows8(d_alog[0, :H]), _rows8(d_dtb[0, :H]),
        d_gdw, d_fw[0].reshape(8, 128), d_meta.reshape(128, 128), d_conv.reshape(48, 128)], axis=0)
    return _reduce_and_update(
        me, c_arr, grad_x, pfs, rs, pack, meta_tokens, norm_w, w_in, conv_w, hg_lb_logits, hg_norm_w, gdn_A_log,
        gdn_dt_bias, gdn_norm_w, w_out, final_norm_w, m_meta_tokens, m_norm_w, m_w_in, m_conv_w, m_hg_lb_logits,
        m_hg_norm_w, m_gdn_A_log, m_gdn_dt_bias, m_gdn_norm_w, m_w_out, m_final_norm_w, v_meta_tokens, v_norm_w, v_w_in,
        v_conv_w, v_hg_lb_logits, v_hg_norm_w, v_gdn_A_log, v_gdn_dt_bias, v_gdn_norm_w, v_w_out, v_final_norm_w)


def _local_step(x, loss_target, wbig, wout_full, conv4, meta_full, norm_w, hg_lb_logits, hg_norm_w, gdn_A_log, gdn_dt_bias,
                gdn_norm_w, final_norm_w, chip_partials):
    h3 = jnp.concatenate([jnp.zeros((NB, PAD, D), f32), jnp.broadcast_to(meta_full[None], (NB, N_META, D)), x], axis=1)
    hflat = h3.reshape(N, D)
    target = jnp.pad(loss_target, ((0, 0), (PAD + N_META, 0), (0, 0))).reshape(N, D)
    l0, l1 = hg_lb_logits[0:1], hg_lb_logits[1:2]
    alog = jnp.pad(gdn_A_log, ((0, 0), (0, DK - H)))
    dtb = jnp.pad(gdn_dt_bias, ((0, 0), (0, DK - H)))
    fw = final_norm_w.reshape(1, D)

    proj, ut = _in_proj(hflat, norm_w, wbig)
    proj3 = proj.reshape(NB, TP, PC)
    o_hg, s_hg = _hg_fwd(proj3, l0, l1)
    cv = _conv_fwd(proj3, conv4)
    o_gd, s_gd, t_gd = _gd_fwd(cv, proj3, alog, dtb)
    (loss8, d_ohg, d_ogd, d_zhg, d_zgd, dh_res, g_wout_part, d_hgw, d_gdw, d_fw) = _out_loss(
        o_hg.reshape(N, HD), o_gd.reshape(N, HD), proj, hg_norm_w, gdn_norm_w, wout_full, hflat, fw, target)
    d_hg, d_l0, d_l1 = _hg_bwd(proj3, l0, l1, s_hg, d_ohg.reshape(NB, TP, HD))
    d_cv, d_ab, d_alog, d_dtb = _gd_bwd(cv, proj3, alog, dtb, s_gd, t_gd, d_ogd.reshape(NB, TP, HD))
    d_qkv, d_conv4 = _conv_bwd(proj3, conv4, d_cv)
    d_hg2, d_qkv2, d_ab2 = d_hg.reshape(N, 3 * HD), d_qkv.reshape(N, 3 * HD), d_ab.reshape(N, DK)
    pieces = [(d_hg2, COL_HG), (d_zhg, COL_ZHG), (d_qkv2, COL_QKV), (d_zgd, COL_ZGD), (d_ab2, COL_AB)]
    gw = _w_grad(ut, pieces)
    pbs, pfs = chip_partials(gw, g_wout_part) if chip_partials else ([], [gw, g_wout_part])
    dh, d_nw, *rs = _in_bwd(pieces, wbig, hflat, norm_w, dh_res, pbs)

    dh3 = dh.reshape(NB, TP, D)
    grad_x = dh3[:, PAD + N_META:, :]
    d_meta = jnp.sum(dh3[:, PAD:PAD + N_META, :], axis=0)
    d_conv = d_conv4[:, 0, :]
    d_lb = jnp.concatenate([d_l0[0:1], d_l1[0:1]], axis=0)
    return loss8, grad_x, d_meta, d_nw, d_conv, d_lb, d_hgw, d_alog, d_dtb, d_gdw, d_fw, pfs, rs


def _reduce_and_update(me, c_arr, grad_x, pfs, rs, pack, meta_tokens, norm_w, w_in, conv_w, hg_lb_logits, hg_norm_w,
                       gdn_A_log, gdn_dt_bias, gdn_norm_w, w_out, final_norm_w, m_meta_tokens, m_norm_w, m_w_in, m_conv_w,
                       m_hg_lb_logits, m_hg_norm_w, m_gdn_A_log, m_gdn_dt_bias, m_gdn_norm_w, m_w_out, m_final_norm_w,
                       v_meta_tokens, v_norm_w, v_w_in, v_conv_w, v_hg_lb_logits, v_hg_norm_w, v_gdn_A_log, v_gdn_dt_bias,
                       v_gdn_norm_w, v_w_out, v_final_norm_w):
    (own_in, own_out), (r_in, r_out) = pfs, rs
    (r_pack,) = _scatter_blocks([], pack)
    f_in = _sum_blocks(own_in, r_in, "sum_w_in")
    f_out = _sum_blocks(own_out, r_out, "sum_w_out")
    o_in, o_out = _swap_finished([f_in, f_out])
    me8_arr = (2 * me + lax.axis_index("c")).reshape(1).astype(jnp.int32)
    small = _sum_packs(me8_arr, pack, r_pack)

    half_out = lambda a: a[0].reshape(2, D // 8, D)
    is0 = lax.axis_index("c") == 0
    g_in = jnp.concatenate([jnp.where(is0, f_in, o_in), jnp.where(is0, o_in, f_in)], axis=0)
    to_cm = lambda a: jnp.transpose(a, (2, 0, 1))
    gi, di, mi, vi = [jnp.transpose(a, (1, 2, 0))[0] for a in _adamw_rows(
        g_in.T.reshape(SHARD_COLS, 1, D), to_cm(w_in), to_cm(m_w_in), to_cm(v_w_in), "adamw_w_in")]
    go, do_, mo, vo = [a.reshape(D // 4, D) for a in _adamw_halves(
        c_arr, f_out, o_out, half_out(w_out), half_out(m_w_out), half_out(v_w_out), "adamw_w_out")]

    g_meta_full = small[64:192].reshape(N_META, D)
    g_meta_loc = lax.dynamic_slice(g_meta_full, (0, me * 256), (N_META, 256))
    gm, dm, mm_, vm = _adamw([g_meta_loc], meta_tokens, m_meta_tokens, v_meta_tokens, "adamw_meta")
    g_conv_full = small[192:240].reshape(4, 1536)
    g_conv_loc = lax.dynamic_slice(g_conv_full, (0, me * 384), (4, 384))
    gc, dc, mc, vc = _adamw([g_conv_loc], conv_w[0], m_conv_w[0], v_conv_w[0], "adamw_conv")

    reps = [(norm_w, m_norm_w, v_norm_w), (hg_lb_logits, m_hg_lb_logits, v_hg_lb_logits),
            (hg_norm_w, m_hg_norm_w, v_hg_norm_w), (gdn_A_log, m_gdn_A_log, v_gdn_A_log),
            (gdn_dt_bias, m_gdn_dt_bias, v_gdn_dt_bias), (gdn_norm_w, m_gdn_norm_w, v_gdn_norm_w),
            (final_norm_w, m_final_norm_w, v_final_norm_w)]
    wp = jnp.concatenate([_rows8(t[0]) for t in reps], axis=0)
    mp = jnp.concatenate([_rows8(t[1]) for t in reps], axis=0)
    vp = jnp.concatenate([_rows8(t[2]) for t in reps], axis=0)
    gr, dr, mr, vr = _adamw([small[8:64]], wp, mp, vp, "adamw_small")

    def unpack(p):
        outs = []
        for i, t in enumerate(reps):
            n = t[0].size
            outs.append(p[8 * i:8 * i + 8].reshape(-1)[:n].reshape(t[0].shape))
        return outs

    def leaves(meta_v, conv_v, in_v, out_v, rep_p):
        nw, lb, hgw, al, db, gdw, fwv = unpack(rep_p)
        return [meta_v, nw, in_v[None], conv_v[None], lb, hgw, al, db, gdw, out_v[None], fwv]

    loss = small[0, 0]
    return (loss, grad_x, *leaves(gm, gc, gi, go, gr), *leaves(dm, dc, di, do_, dr),
            *leaves(mm_, mc, mi, mo, mr), *leaves(vm, vc, vi, vo, vr))
```

```python
import functools

import jax
import jax.numpy as jnp
from jax import lax
from jax.experimental import pallas as pl
from jax.experimental.pallas import tpu as pltpu

f32 = jnp.float32
bf16 = jnp.bfloat16
MESH = pl.DeviceIdType.MESH
ANY = pl.BlockSpec(memory_space=pl.ANY)

D = 1024
NB = 2
N_META = 16
SEQ = 2048
PAD = 48
TP = PAD + N_META + SEQ
C = 64
NCH = TP // C
N = NB * TP
H = 4
DK = 128
HD = H * DK
PC = 4224
IN_COLS = 4104
SHARD_COLS = IN_COLS // 4
COL_HG, COL_ZHG, COL_QKV, COL_ZGD, COL_AB = 0, 3 * HD, 4 * HD, 7 * HD, 8 * HD
EPS = 1e-6
ADAM_LR, ADAM_B1, ADAM_B2, ADAM_EPS, ADAM_WD, ADAM_STEP = 0.001, 0.9, 0.999, 1e-08, 0.01, 10
VMEM_LIMIT = 56 * 1024 * 1024

P_HG = dict(lvl=1, av=1, qs=1, su=1)
P_GD = dict(kk=1, inv=1, sol=1, ws=1, qk=1, o=1, su=1)


def _cp(sem=None, **kw):
    return pltpu.CompilerParams(dimension_semantics=sem, vmem_limit_bytes=VMEM_LIMIT, **kw)


_DIMS = {"nn": (((1,), (0,)), ((), ())), "nt": (((1,), (1,)), ((), ())), "tn": (((0,), (0,)), ((), ()))}


def _split(x):
    hi = x.astype(bf16)
    return hi, (x - hi.astype(f32)).astype(bf16)


def _dg(a, b, kind, passes):
    d = lambda x, y: lax.dot_general(x, y, _DIMS[kind], preferred_element_type=f32)
    if passes == 1:
        return d(a.astype(bf16), b.astype(bf16))
    ah, al = _split(a)
    bh, bl = _split(b)
    return d(ah, bh) + d(ah, bl) + d(al, bh)


@functools.partial(jax.custom_vjp, nondiff_argnums=(2, 3))
def mmx(a, b, kind, passes):
    return _dg(a, b, kind, passes)


def _mmx_fwd(a, b, kind, passes):
    return _dg(a, b, kind, passes), (a, b)


def _mmx_bwd(kind, passes, res, g):
    a, b = res
    if kind == "nn":
        return _dg(g, b, "nt", passes), _dg(a, g, "tn", passes)
    if kind == "nt":
        return _dg(g, b, "nn", passes), _dg(g, a, "tn", passes)
    return _dg(b, g, "nt", passes), _dg(a, g, "nn", passes)


mmx.defvjp(_mmx_fwd, _mmx_bwd)


def _mask_dg(mask, x):
    xh, xl = _split(x)
    return jnp.dot(jnp.concatenate([mask, mask], axis=1), jnp.concatenate([xh, xl], axis=0), preferred_element_type=f32)


@functools.partial(jax.custom_vjp, nondiff_argnums=(2,))
def mask_mm(mask, x, bwd_passes):
    return _mask_dg(mask, x)


def _mask_fwd(mask, x, bwd_passes):
    return _mask_dg(mask, x), mask


def _mask_bwd(bwd_passes, mask, g):
    d = lambda y: lax.dot_general(mask, y, _DIMS["tn"], preferred_element_type=f32)
    if bwd_passes == 1:
        return None, d(g.astype(bf16))
    gh, gl = _split(g)
    return None, d(gh) + d(gl)


mask_mm.defvjp(_mask_fwd, _mask_bwd)


def bdot(a, b):
    return jnp.dot(a.astype(bf16), b.astype(bf16), preferred_element_type=f32)


def bdot_nt(a, b):
    return lax.dot_general(a.astype(bf16), b.astype(bf16), _DIMS["nt"], preferred_element_type=f32)


def bdot_tn(a, b):
    return lax.dot_general(a.astype(bf16), b.astype(bf16), _DIMS["tn"], preferred_element_type=f32)


def _iota2(n, m):
    return lax.broadcasted_iota(jnp.int32, (n, m), 0), lax.broadcasted_iota(jnp.int32, (n, m), 1)


sigmoid = jax.nn.sigmoid


def silu(x):
    return x * sigmoid(x)


def softplus(x):
    return jnp.maximum(x, 0.0) + jnp.log(1.0 + jnp.exp(-jnp.abs(x)))


def rmsnorm(x, w):
    return x * lax.rsqrt(jnp.mean(x * x, axis=-1, keepdims=True) + EPS) * w


def hg_masks():
    t, r = _iota2(C, C)
    mats = [r <= t, r > t]
    lvl = []
    for l in range(1, 7):
        sz = 1 << l
        half = sz >> 1
        seg_t = t >> l
        upper_t = (t & (sz - 1)) >= half
        mid_t = seg_t * sz + half - 1
        mats.append((upper_t & (r > mid_t) & (r <= t)) | ((~upper_t) & (r > t) & (r <= mid_t)))
        lvl.append(((seg_t == (r >> l)) & upper_t & ((r & (sz - 1)) < half)).astype(f32))
    stk = jnp.concatenate([m.astype(bf16) for m in mats], axis=0)
    return stk, lvl, (t == r).astype(f32)


def _head(a, h):
    return a[:, h * DK:(h + 1) * DK]


def hg_chunk(St, ps, l0, l1):
    m = jnp.maximum(l0, l1)
    e0 = jnp.exp(l0 - m)
    e1 = jnp.exp(l1 - m)
    lb = e0 / (e0 + e1)
    stk, lvl, eye = hg_masks()
    msk = [eye] + lvl
    qs, ks, vs, qG, kR, eGl = [], [], [], [], [], []
    for p in ps:
        pq, pf, v = p[:, 0:HD], p[:, HD:2 * HD], p[:, 2 * HD:3 * HD]
        q = silu(pq)
        f = lb + (1.0 - lb) * sigmoid(pf)
        k = 1.0 - f
        logf = jnp.log(f)
        Dm = mask_mm(stk, logf, 1)
        ex = [jnp.exp(Dm[(2 + i) * C:(3 + i) * C]) for i in range(6)]
        qs.append([q] + [q * e for e in ex])
        ks.append([k] + [k * e for e in ex])
        vs.append(v)
        qG.append(q * jnp.exp(Dm[0:C]))
        kR.append(k * jnp.exp(Dm[C:2 * C]))
        eGl.append(jnp.exp(jnp.sum(logf, axis=0, keepdims=True)))
    units = [(b, h) for b in range(len(ps)) for h in range(H)]
    parts = [[msk[i] * mmx(_head(qs[b][i], h), _head(ks[b][i], h), "nt", P_HG["lvl"]) for b, h in units]
             for i in range(7)]
    A = [functools.reduce(lambda x, y: x + y, [parts[i][n] for i in range(7)]) for n in range(len(units))]
    qS = [mmx(_head(qG[b], h), St[n], "nt", P_HG["qs"]) for n, (b, h) in enumerate(units)]
    Sn = [St[n] * _head(eGl[b], h) + mmx(_head(vs[b], h), _head(kR[b], h), "tn", P_HG["su"])
          for n, (b, h) in enumerate(units)]
    outs = [mmx(A[n], _head(vs[b], h), "nn", P_HG["av"]) + qS[n] for n, (b, h) in enumerate(units)]
    return tuple(Sn), tuple(jnp.concatenate(outs[b * H:(b + 1) * H], axis=1) for b in range(len(ps)))


@jax.custom_vjp
def use_inverse(A, T):
    return T


def _use_inverse_fwd(A, T):
    return T, T


def _use_inverse_bwd(T, g):
    return -_dg(T, _dg(g, T, "nt", P_GD["inv"]), "tn", P_GD["inv"]), jnp.zeros_like(T)


use_inverse.defvjp(_use_inverse_fwd, _use_inverse_bwd)


def gd_chunk(S, cs, abs_, alog, dtb, t_saved=None):
    t, r = _iota2(C, C)
    tri = (r <= t).astype(bf16)
    ups = (r > t).astype(bf16)
    lane = lax.broadcasted_iota(jnp.int32, (1, DK), 1)
    subl = lax.broadcasted_iota(jnp.int32, (8, 1), 0)
    eye = (t == r).astype(f32)
    strict = (r < t).astype(f32)
    bd = ((t >> 4) == (r >> 4)).astype(f32)
    qa, ka, va, b4, gam4, grev4, gam4T, glast4 = [], [], [], [], [], [], [], []
    for c, ab in zip(cs, abs_):
        qa.append(silu(c[:, 0:HD]))
        ka.append(silu(c[:, HD:2 * HD]))
        va.append(silu(c[:, 2 * HD:3 * HD]))
        g4 = -jnp.exp(alog) * softplus(ab + dtb)
        b4.append(sigmoid(ab))
        gam4.append(mask_mm(tri, g4, 2))
        grev4.append(mask_mm(ups, g4, 2))
        gam4T.append(gam4[-1].T)
        glast4.append(jnp.sum(g4, axis=0, keepdims=True))
    units = [(b, h) for b in range(len(cs)) for h in range(H)]
    nu = range(len(units))
    inv = lambda a, b: [mmx(a[n], b[n], "nn", P_GD["inv"]) for n in nu]
    v = [_head(va[b], h) for b, h in units]
    q = [_head(qa[b], h) for b, h in units]
    k = [_head(ka[b], h) for b, h in units]
    q = [x * lax.rsqrt(jnp.sum(x * x, -1, keepdims=True) + EPS) * (DK ** -0.5) for x in q]
    k = [x * lax.rsqrt(jnp.sum(x * x, -1, keepdims=True) + EPS) for x in k]
    oh = [(lane == h).astype(f32) for h in range(H)]
    gam_c = [jnp.sum(gam4[b] * oh[h], -1, keepdims=True) for b, h in units]
    grev_c = [jnp.sum(grev4[b] * oh[h], -1, keepdims=True) for b, h in units]
    beta = [jnp.sum(b4[b] * (lane == h + H).astype(f32), -1, keepdims=True) for b, h in units]
    glast = [jnp.sum(glast4[b] * oh[h], -1, keepdims=True) for b, h in units]
    gam_r = [jnp.sum(gam4T[b][0:8, :] * (subl == h).astype(f32), axis=0, keepdims=True) for b, h in units]
    dec = [jnp.exp(jnp.where(r <= t, gam_c[n] - gam_r[n], -1e30)) for n in nu]
    egam = [jnp.exp(gam_c[n]) for n in nu]
    kk = [mmx(k[n], k[n], "nt", P_GD["kk"]) for n in nu]
    qk = [mmx(q[n], k[n], "nt", P_GD["qk"]) * dec[n] for n in nu]
    A = [beta[n] * kk[n] * dec[n] * strict for n in nu]
    Dg = [A[n] * bd for n in nu]
    L = [A[n] - Dg[n] for n in nu]
    if t_saved is None:
        ImD = [eye - Dg[n] for n in nu]
        D2 = inv(Dg, Dg)
        P1 = inv(ImD, [eye + x for x in D2])
        D4 = inv(D2, D2)
        P2 = inv(P1, [eye + x for x in D4])
        D8 = inv(D4, D4)
        M = inv(P2, [eye + x for x in D8])
        Nn = inv(M, L)
        N2 = inv(Nn, Nn)
        T1 = inv([eye - x for x in Nn], [eye + x for x in N2])
        Tinv = inv(T1, M)
    else:
        Tinv = [use_inverse(A[n], t_saved[b][:, h * DK:h * DK + C]) for n, (b, h) in enumerate(units)]
    rhs = [jnp.concatenate([beta[n] * v[n], (beta[n] * egam[n]) * k[n]], axis=1) for n in nu]
    sol = [mmx(Tinv[n], rhs[n], "nn", P_GD["sol"]) for n in nu]
    qwS = [mmx(jnp.concatenate([q[n] * egam[n], sol[n][:, DK:2 * DK]], axis=0), S[n], "nn", P_GD["ws"]) for n in nu]
    u = [sol[n][:, 0:DK] - qwS[n][C:2 * C] for n in nu]
    outs = [qwS[n][0:C] + mmx(qk[n], u[n], "nn", P_GD["o"]) for n in nu]
    Sn = [jnp.exp(glast[n]) * S[n] + mmx(k[n] * jnp.exp(grev_c[n]), u[n], "tn", P_GD["su"]) for n in nu]
    zpad = jnp.zeros((C, DK - C), f32)
    t_pack = tuple(jnp.concatenate([x for n in range(b * H, (b + 1) * H) for x in (lax.stop_gradient(Tinv[n]), zpad)],
                                   axis=1) for b in range(len(cs)))
    return tuple(Sn), tuple(jnp.concatenate(outs[b * H:(b + 1) * H], axis=1) for b in range(len(cs))), t_pack


def _in_proj(hflat, norm_w, wbig):
    tm = 384

    def body(h_ref, nw_ref, w_ref, p_ref, ut_ref):
        u = rmsnorm(h_ref[...], nw_ref[...])
        ut_ref[...] = u.T.astype(bf16)
        p_ref[...] = jnp.dot(u.astype(bf16), w_ref[...], preferred_element_type=f32)

    return pl.pallas_call(
        body, name="in_proj", grid=(N // tm,),
        in_specs=[pl.BlockSpec((tm, D), lambda i: (i, 0)), pl.BlockSpec((1, D), lambda i: (0, 0)),
                  pl.BlockSpec((D, PC), lambda i: (0, 0))],
        out_specs=[pl.BlockSpec((tm, PC), lambda i: (i, 0)), pl.BlockSpec((D, tm), lambda i: (0, i))],
        out_shape=[jax.ShapeDtypeStruct((N, PC), f32), jax.ShapeDtypeStruct((D, N), bf16)],
        compiler_params=_cp(("arbitrary",)),
    )(hflat, norm_w, wbig)


NU = NB * H
_REV = lambda c: NCH - 1 - c
_FWD = lambda c: c


def _tok_spec(w, ix, col=0):
    return pl.BlockSpec((NB, C, w), lambda c: (0, ix(c), col))


def _state_spec(ix):
    return pl.BlockSpec((NB, 1, H, DK, DK), lambda c: (0, ix(c), 0, 0, 0))


def _row_spec(w):
    return pl.BlockSpec((1, w), lambda c: (0, 0))


def _rows(ref):
    return tuple(ref[b] for b in range(NB))


def _scan_fwd(chunk_fn, name, x3, extra, extra_specs, n_tok, aux_w):
    ne = len(extra)
    na = 1 if aux_w else 0

    def body(*refs):
        x_ref = refs[0]
        e_refs = refs[1:1 + ne]
        o_ref, s_ref = refs[1 + ne:3 + ne]
        aux_refs = refs[3 + ne:3 + ne + na]
        st = refs[3 + ne + na]

        @pl.when(pl.program_id(0) == 0)
        def _():
            st[...] = jnp.zeros_like(st)

        S = tuple(st[n] for n in range(NU))
        for n in range(NU):
            s_ref[n // H, 0, n % H] = S[n]
        args = [_rows(e) if i < n_tok else e[...] for i, e in enumerate(e_refs)]
        Sn, o, *aux = chunk_fn(S, _rows(x_ref), *args)
        for n in range(NU):
            st[n] = Sn[n]
        for b in range(NB):
            o_ref[b] = o[b]
            for a_ref, a in zip(aux_refs, aux):
                a_ref[b] = a[b]

    return pl.pallas_call(
        body, name=name, grid=(NCH,),
        in_specs=[_tok_spec(3 * HD, _FWD)] + extra_specs(_FWD),
        out_specs=[_tok_spec(HD, _FWD), _state_spec(_FWD)] + [_tok_spec(aux_w, _FWD)] * na,
        out_shape=[jax.ShapeDtypeStruct((NB, TP, HD), f32), jax.ShapeDtypeStruct((NB, NCH, H, DK, DK), f32)]
        + [jax.ShapeDtypeStruct((NB, TP, aux_w), f32)] * na,
        scratch_shapes=[pltpu.VMEM((NU, DK, DK), f32)],
        compiler_params=_cp(("arbitrary",)),
    )(x3, *extra)


def _hg_extra_specs(ix):
    return [_row_spec(HD), _row_spec(HD)]


def _gd_extra_specs(ix):
    return [_tok_spec(DK, ix, COL_AB // DK), _row_spec(DK), _row_spec(DK)]


def _hg_fwd(proj3, l0, l1):
    return _scan_fwd(hg_chunk, "hg_fwd", proj3, [l0, l1], _hg_extra_specs, 0, 0)


def _gd_fwd(cv, proj3, alog, dtb):
    return _scan_fwd(gd_chunk, "gd_fwd", cv, [proj3, alog, dtb], _gd_extra_specs, 1, H * DK)


def _hg_bwd(proj3, l0, l1, s_saved, do):
    def body(p_ref, l0_ref, l1_ref, s_ref, do_ref, dp_ref, dl0_ref, dl1_ref, dst):
        @pl.when(pl.program_id(0) == 0)
        def _():
            dst[...] = jnp.zeros_like(dst)
            dl0_ref[...] = jnp.zeros_like(dl0_ref)
            dl1_ref[...] = jnp.zeros_like(dl1_ref)

        S = tuple(s_ref[n // H, 0, n % H] for n in range(NU))
        _, vjp = jax.vjp(hg_chunk, S, _rows(p_ref), l0_ref[...], l1_ref[...])
        dS, dp, dl0, dl1 = vjp((tuple(dst[n] for n in range(NU)), _rows(do_ref)))
        for n in range(NU):
            dst[n] = dS[n]
        for b in range(NB):
            dp_ref[b] = dp[b].astype(bf16)
        dl0_ref[...] += jnp.broadcast_to(dl0, (8, HD))
        dl1_ref[...] += jnp.broadcast_to(dl1, (8, HD))

    acc = pl.BlockSpec((8, HD), lambda c: (0, 0))
    return pl.pallas_call(
        body, name="hg_bwd", grid=(NCH,),
        in_specs=[_tok_spec(3 * HD, _REV)] + _hg_extra_specs(_REV) + [_state_spec(_REV), _tok_spec(HD, _REV)],
        out_specs=[_tok_spec(3 * HD, _REV), acc, acc],
        out_shape=[jax.ShapeDtypeStruct((NB, TP, 3 * HD), bf16), jax.ShapeDtypeStruct((8, HD), f32),
                   jax.ShapeDtypeStruct((8, HD), f32)],
        scratch_shapes=[pltpu.VMEM((NU, DK, DK), f32)],
        compiler_params=_cp(("arbitrary",)),
    )(proj3, l0, l1, s_saved, do)


def _gd_bwd(cv, proj3, alog, dtb, s_saved, t_saved, do):
    def body(c_ref, ab_ref, al_ref, db_ref, s_ref, t_ref, do_ref, dc_ref, dab_ref, dal_ref, ddb_ref, dst):
        @pl.when(pl.program_id(0) == 0)
        def _():
            dst[...] = jnp.zeros_like(dst)
            dal_ref[...] = jnp.zeros_like(dal_ref)
            ddb_ref[...] = jnp.zeros_like(ddb_ref)

        S = tuple(s_ref[n // H, 0, n % H] for n in range(NU))
        t_rows = _rows(t_ref)
        fn = lambda *a: gd_chunk(*a, t_saved=t_rows)[0:2]
        _, vjp = jax.vjp(fn, S, _rows(c_ref), _rows(ab_ref), al_ref[...], db_ref[...])
        dS, dc, dab, dal, ddb = vjp((tuple(dst[n] for n in range(NU)), _rows(do_ref)))
        for n in range(NU):
            dst[n] = dS[n]
        for b in range(NB):
            dc_ref[b] = dc[b]
            dab_ref[b] = dab[b].astype(bf16)
        dal_ref[...] += jnp.broadcast_to(dal, (8, DK))
        ddb_ref[...] += jnp.broadcast_to(ddb, (8, DK))

    acc = pl.BlockSpec((8, DK), lambda c: (0, 0))
    return pl.pallas_call(
        body, name="gd_bwd", grid=(NCH,),
        in_specs=[_tok_spec(3 * HD, _REV)] + _gd_extra_specs(_REV)
        + [_state_spec(_REV), _tok_spec(HD, _REV), _tok_spec(HD, _REV)],
        out_specs=[_tok_spec(3 * HD, _REV), _tok_spec(DK, _REV), acc, acc],
        out_shape=[jax.ShapeDtypeStruct((NB, TP, 3 * HD), f32), jax.ShapeDtypeStruct((NB, TP, DK), bf16),
                   jax.ShapeDtypeStruct((8, DK), f32), jax.ShapeDtypeStruct((8, DK), f32)],
        scratch_shapes=[pltpu.VMEM((NU, DK, DK), f32)],
        compiler_params=_cp(("arbitrary",)),
    )(cv, proj3, alog, dtb, s_saved, t_saved, do)


def _conv_fwd(proj3, conv4):
    def body(x_ref, w_ref, y_ref):
        x = x_ref[0]
        row = lax.broadcasted_iota(jnp.int32, (TP, 1), 0)
        y = w_ref[3] * x
        for s in (1, 2, 3):
            y = y + w_ref[3 - s] * jnp.where(row >= s, pltpu.roll(x, s, 0), 0.0)
        y_ref[0] = y

    return pl.pallas_call(
        body, name="conv_fwd", grid=(NB, 3),
        in_specs=[pl.BlockSpec((1, TP, HD), lambda b, j: (b, 0, COL_QKV // HD + j)),
                  pl.BlockSpec((4, 1, HD), lambda b, j: (0, 0, j))],
        out_specs=pl.BlockSpec((1, TP, HD), lambda b, j: (b, 0, j)),
        out_shape=jax.ShapeDtypeStruct((NB, TP, 3 * HD), f32),
        compiler_params=_cp(("arbitrary", "arbitrary")),
    )(proj3, conv4)


def _conv_bwd(proj3, conv4, dy):
    def body(x_ref, w_ref, dy_ref, dx_ref, dw_ref):
        @pl.when(pl.program_id(1) == 0)
        def _():
            dw_ref[...] = jnp.zeros_like(dw_ref)

        x = x_ref[0]
        g = dy_ref[0]
        row = lax.broadcasted_iota(jnp.int32, (TP, 1), 0)
        dx = w_ref[3] * g
        dw_ref[3] += jnp.broadcast_to(jnp.sum(x * g, axis=0, keepdims=True), (8, HD))
        for s in (1, 2, 3):
            dx = dx + w_ref[3 - s] * jnp.where(row < TP - s, pltpu.roll(g, TP - s, 0), 0.0)
            xs = jnp.where(row >= s, pltpu.roll(x, s, 0), 0.0)
            dw_ref[3 - s] += jnp.broadcast_to(jnp.sum(xs * g, axis=0, keepdims=True), (8, HD))
        dx_ref[0] = dx.astype(bf16)

    return pl.pallas_call(
        body, name="conv_bwd", grid=(3, NB),
        in_specs=[pl.BlockSpec((1, TP, HD), lambda j, b: (b, 0, COL_QKV // HD + j)),
                  pl.BlockSpec((4, 1, HD), lambda j, b: (0, 0, j)), pl.BlockSpec((1, TP, HD), lambda j, b: (b, 0, j))],
        out_specs=[pl.BlockSpec((1, TP, HD), lambda j, b: (b, 0, j)), pl.BlockSpec((4, 8, HD), lambda j, b: (0, 0, j))],
        out_shape=[jax.ShapeDtypeStruct((NB, TP, 3 * HD), bf16), jax.ShapeDtypeStruct((4, 8, 3 * HD), f32)],
        compiler_params=_cp(("arbitrary", "arbitrary")),
    )(proj3, conv4, dy)


def _gated(o, z, w):
    outs = []
    for hh in range(H):
        sl = slice(hh * DK, (hh + 1) * DK)
        outs.append(rmsnorm(o[:, sl], w) * silu(z[:, sl]))
    return jnp.concatenate(outs, axis=-1)


def _out_loss(o_hg, o_gd, proj, hgw, gdw, wout, hflat, fw, target):
    tm = 384

    def body(ohg_ref, ogd_ref, zhg_ref, zgd_ref, hgw_ref, gdw_ref, wo_ref, h_ref, fw_ref, tg_ref,
             loss_ref, dohg_ref, dogd_ref, dzhg_ref, dzgd_ref, dh_ref, dwo_ref, dhgw_ref, dgdw_ref, dfw_ref):
        i = pl.program_id(0)

        @pl.when(i == 0)
        def _():
            for r in (loss_ref, dwo_ref, dhgw_ref, dgdw_ref, dfw_ref):
                r[...] = jnp.zeros_like(r)

        row = i * tm + lax.broadcasted_iota(jnp.int32, (tm, 1), 0)
        tok = jnp.where(row >= TP, row - TP, row)
        valid = (tok >= PAD + N_META).astype(f32)
        hval = h_ref[...]
        tgt = tg_ref[...]

        def mix(ohg, ogd, zhg, zgd, w1, w2):
            return jnp.concatenate([_gated(ohg, zhg, w1), _gated(ogd, zgd, w2)], axis=-1)

        y, vjp_mix = jax.vjp(mix, ohg_ref[...], ogd_ref[...], zhg_ref[...], zgd_ref[...], hgw_ref[...], gdw_ref[...])
        out = bdot(y, wo_ref[...])

        def head(out, fwv):
            err = (rmsnorm(hval + out, fwv) - tgt) * valid
            return 0.5 * jnp.sum(jnp.mean(err * err, axis=-1, keepdims=True))

        loss, vjp_head = jax.vjp(head, out, fw_ref[...])
        dout, dfw = vjp_head(jnp.ones((), f32))
        dh_ref[...] = dout
        dy = bdot_nt(dout, wo_ref[...])
        dwo_ref[...] += bdot_tn(y, dout)
        dohg, dogd, dzhg, dzgd, dw1, dw2 = vjp_mix(dy)
        dohg_ref[...] = dohg
        dogd_ref[...] = dogd
        dzhg_ref[...] = dzhg.astype(bf16)
        dzgd_ref[...] = dzgd.astype(bf16)
        loss_ref[...] += jnp.broadcast_to(loss, (8, DK))
        dhgw_ref[...] += jnp.broadcast_to(dw1, (8, DK))
        dgdw_ref[...] += jnp.broadcast_to(dw2, (8, DK))
        dfw_ref[...] += jnp.broadcast_to(dfw, (8, D))

    row = lambda w: pl.BlockSpec((tm, w), lambda i: (i, 0))
    whole = lambda r, w: pl.BlockSpec((r, w), lambda i: (0, 0))
    col = lambda c0: pl.BlockSpec((tm, HD), lambda i: (i, c0 // HD))
    return pl.pallas_call(
        body, name="out_loss", grid=(N // tm,),
        in_specs=[row(HD), row(HD), col(COL_ZHG), col(COL_ZGD),
                  whole(1, DK), whole(1, DK), whole(D, D), row(D), whole(1, D), row(D)],
        out_specs=[whole(8, DK), row(HD), row(HD), row(HD), row(HD), row(D), whole(D, D),
                   whole(8, DK), whole(8, DK), whole(8, D)],
        out_shape=[jax.ShapeDtypeStruct((8, DK), f32)] + [jax.ShapeDtypeStruct((N, HD), f32)] * 2
        + [jax.ShapeDtypeStruct((N, HD), bf16)] * 2
        + [jax.ShapeDtypeStruct((N, D), f32), jax.ShapeDtypeStruct((D, D), f32),
           jax.ShapeDtypeStruct((8, DK), f32), jax.ShapeDtypeStruct((8, DK), f32), jax.ShapeDtypeStruct((8, D), f32)],
        compiler_params=_cp(("arbitrary",)),
    )(o_hg, o_gd, proj, proj, hgw, gdw, wout, hflat, fw, target)


def _in_bwd(pieces, wbig, hflat, norm_w, dh_res, pbs):
    tm = 384
    nsteps = N // tm
    np_ = len(pieces)
    na = len(pbs)
    offs = [c0 for _, c0 in pieces]
    widths = [d.shape[1] for d, _ in pieces]

    def body(*refs):
        d_refs = refs[:np_]
        w_ref, h_ref, nw_ref, dhr_ref = refs[np_:np_ + 4]
        srcs = refs[np_ + 4:np_ + 4 + na]
        dh_ref, dnw_ref = refs[np_ + 4 + na:np_ + 6 + na]
        dsts = refs[np_ + 6 + na:np_ + 6 + 2 * na]
        sems = refs[np_ + 6 + 2 * na:]
        i = pl.program_id(0)

        def copies():
            if not na:
                return []
            x, y, c, chips = _place()
            return [pltpu.make_async_remote_copy(
                src_ref=srcs[a].at[2 * px + py], dst_ref=dsts[a].at[j], send_sem=sems[0].at[na * j + a],
                recv_sem=sems[1].at[na * j + a], device_id=(px, py, c), device_id_type=MESH)
                for j, (px, py) in enumerate(chips) for a in range(na)]

        @pl.when(i == 0)
        def _():
            dnw_ref[...] = jnp.zeros_like(dnw_ref)
            for cp in copies():
                cp.start()

        du = jnp.zeros((tm, D), f32)
        for d_ref, off, wd in zip(d_refs, offs, widths):
            du = du + bdot_nt(d_ref[...], w_ref[:, off:off + wd])
        _, vjp = jax.vjp(rmsnorm, h_ref[...], nw_ref[...])
        dh, dnw = vjp(du)
        dh_ref[...] = dh + dhr_ref[...]
        dnw_ref[...] += jnp.broadcast_to(dnw, (8, D))

        @pl.when(i == nsteps - 1)
        def _():
            for cp in copies():
                cp.wait()

    row = lambda w: pl.BlockSpec((tm, w), lambda i: (i, 0))
    return pl.pallas_call(
        body, name="in_bwd", grid=(nsteps,),
        in_specs=[row(w) for w in widths]
        + [pl.BlockSpec((D, PC), lambda i: (0, 0)), row(D), pl.BlockSpec((1, D), lambda i: (0, 0)), row(D)] + [ANY] * na,
        out_specs=[row(D), pl.BlockSpec((8, D), lambda i: (0, 0))] + [ANY] * na,
        out_shape=[jax.ShapeDtypeStruct((N, D), f32), jax.ShapeDtypeStruct((8, D), f32)]
        + [jax.ShapeDtypeStruct((3,) + p.shape[1:], p.dtype) for p in pbs],
        scratch_shapes=[pltpu.SemaphoreType.DMA((3 * na,)), pltpu.SemaphoreType.DMA((3 * na,))] if na else [],
        compiler_params=_cp(("arbitrary",)),
    )(*[d for d, _ in pieces], wbig, hflat, norm_w, dh_res, *pbs)


def _w_grad(ut, pieces):
    tk = 384
    offs = [c0 for _, c0 in pieces]
    widths = [d.shape[1] for d, _ in pieces]

    def body(u_ref, *refs):
        d_refs, o_ref = refs[:-1], refs[-1]

        @pl.when(pl.program_id(0) == 0)
        def _():
            o_ref[...] = jnp.zeros_like(o_ref)

        u = u_ref[...]
        for d_ref, off, wd in zip(d_refs, offs, widths):
            o_ref[:, off:off + wd] += jnp.dot(u, d_ref[...], preferred_element_type=f32)

    return pl.pallas_call(
        body, name="w_grad", grid=(N // tk,),
        in_specs=[pl.BlockSpec((D, tk), lambda k: (0, k))] + [pl.BlockSpec((tk, w), lambda k: (k, 0)) for w in widths],
        out_specs=pl.BlockSpec((D, PC), lambda k: (0, 0)),
        out_shape=jax.ShapeDtypeStruct((D, PC), f32),
        compiler_params=_cp(("arbitrary",)),
    )(ut, *[d for d, _ in pieces])


def _adam_math(g, w, m, v):
    m2 = ADAM_B1 * m + (1.0 - ADAM_B1) * g
    v2 = ADAM_B2 * v + (1.0 - ADAM_B2) * (g * g)
    m_hat = m2 / (1.0 - ADAM_B1 ** ADAM_STEP)
    v_hat = v2 / (1.0 - ADAM_B2 ** ADAM_STEP)
    delta = -ADAM_LR * (m_hat / (jnp.sqrt(v_hat) + ADAM_EPS) + ADAM_WD * w)
    return delta, m2, v2


def _adamw(gs, w, m, v, name):
    R, Cc = w.shape
    tr = 256 if R % 256 == 0 else R
    ng = len(gs)

    def body(*refs):
        g = refs[0][...]
        for r in refs[1:ng]:
            g = g + r[...]
        w_ref, m_ref, v_ref, g_ref, d_ref, m2_ref, v2_ref = refs[ng:]
        delta, m2, v2 = _adam_math(g, w_ref[...], m_ref[...], v_ref[...])
        g_ref[...] = g
        d_ref[...] = delta
        m2_ref[...] = m2
        v2_ref[...] = v2

    spec = pl.BlockSpec((tr, Cc), lambda i: (i, 0))
    return pl.pallas_call(
        body, name=name, grid=(R // tr,),
        in_specs=[spec] * (ng + 3), out_specs=[spec] * 4,
        out_shape=[jax.ShapeDtypeStruct((R, Cc), f32)] * 4,
        compiler_params=_cp(("arbitrary",)),
    )(*gs, w, m, v)


def _adamw_rows(g, w, m, v, name):
    R, _, Cc = w.shape
    tr = R // 9

    def body(g_ref, w_ref, m_ref, v_ref, go_ref, d_ref, m2_ref, v2_ref):
        g = g_ref[...]
        delta, m2, v2 = _adam_math(g, w_ref[...], m_ref[...], v_ref[...])
        go_ref[...] = g
        d_ref[...] = delta
        m2_ref[...] = m2
        v2_ref[...] = v2

    spec = pl.BlockSpec((tr, 1, Cc), lambda i: (i, 0, 0))
    return pl.pallas_call(
        body, name=name, grid=(R // tr,),
        in_specs=[spec] * 4, out_specs=[spec] * 4,
        out_shape=[jax.ShapeDtypeStruct((R, 1, Cc), f32)] * 4,
        compiler_params=_cp(("arbitrary",)),
    )(g, w, m, v)


def _sum_slots(r, name):
    S, R, Cc = r.shape
    tr = 256 if R % 256 == 0 else R

    def body(r_ref, o_ref):
        acc = r_ref[0]
        for s in range(1, S):
            acc = acc + r_ref[s]
        o_ref[...] = acc

    return pl.pallas_call(
        body, name=name, grid=(R // tr,),
        in_specs=[pl.BlockSpec((S, tr, Cc), lambda i: (0, i, 0))], out_specs=pl.BlockSpec((tr, Cc), lambda i: (i, 0)),
        out_shape=jax.ShapeDtypeStruct((R, Cc), f32),
        compiler_params=_cp(("arbitrary",)),
    )(r)


def _place():
    x, y, c = lax.axis_index("x"), lax.axis_index("y"), lax.axis_index("c")
    return x, y, c, [(1 - x, y), (x, 1 - y), (1 - x, 1 - y)]


def _gather_weights(halved, whole):
    nh, nw = len(halved), len(whole)
    na = nh + nw

    def body(*refs):
        srcs, dsts = refs[:na], refs[na:2 * na]
        send_sems, recv_sems, loc_sems = refs[2 * na:2 * na + 3]
        stage = refs[2 * na + 3:]
        x, y, c, chips = _place()
        me = 2 * x + y
        loads = [pltpu.make_async_copy(s, v, loc_sems.at[i]) for i, (s, v) in enumerate(zip(srcs, stage))]
        locs = [pltpu.make_async_copy(v, d.at[me], loc_sems.at[i]) for i, (v, d) in enumerate(zip(stage, dsts))]
        for cp in loads:
            cp.start()

        def ici(j, i, slot):
            px, py = chips[j]
            src = srcs[i].at[c] if i < nh else srcs[i]
            dst = dsts[i].at[slot, c] if i < nh else dsts[i].at[slot]
            return pltpu.make_async_remote_copy(
                src_ref=src, dst_ref=dst, send_sem=send_sems.at[na * j + i], recv_sem=recv_sems.at[na * j + i],
                device_id=(px, py, c), device_id_type=MESH)

        def d2d(j, i, half):
            px, py = chips[j]
            blk = dsts[i].at[2 * px + py, half]
            return pltpu.make_async_remote_copy(
                src_ref=blk, dst_ref=blk, send_sem=send_sems.at[3 * na + nh * j + i],
                recv_sem=recv_sems.at[3 * na + nh * j + i], device_id=(x, y, 1 - c), device_id_type=MESH)

        sends = [ici(j, i, me) for j in range(3) for i in range(na)]
        for cp in sends:
            cp.start()
        for ld, st in zip(loads, locs):
            ld.wait()
            st.start()
        for j, (px, py) in enumerate(chips):
            for i in range(na):
                ici(j, i, 2 * px + py).wait_recv()
                if i < nh:
                    fwd = d2d(j, i, c)
                    fwd.start()
                    sends.append(fwd)
        for j in range(3):
            for i in range(nh):
                d2d(j, i, 1 - c).wait_recv()
        for cp in sends:
            cp.wait_send()
        for cp in locs:
            cp.wait()

    nsem = 3 * na + 3 * nh
    return pl.pallas_call(
        body, name="gather_weights",
        in_specs=[ANY] * na, out_specs=[ANY] * na,
        out_shape=[jax.ShapeDtypeStruct((4,) + s.shape, s.dtype) for s in list(halved) + list(whole)],
        scratch_shapes=[pltpu.SemaphoreType.DMA((nsem,)), pltpu.SemaphoreType.DMA((nsem,)),
                        pltpu.SemaphoreType.DMA((na,))] + [pltpu.VMEM(s.shape, s.dtype) for s in list(halved) + list(whole)],
        compiler_params=pltpu.CompilerParams(has_side_effects=True, vmem_limit_bytes=VMEM_LIMIT),
    )(*halved, *whole)


def _swap_halves(gs):
    na = len(gs)
    jobs = [(i, q) for i in range(na) for q in range(gs[i].shape[0])]

    def body(*refs):
        srcs, dsts = refs[:na], refs[na:2 * na]
        send_sems, recv_sems = refs[2 * na:]
        x, y, c, _ = _place()
        cps = [pltpu.make_async_remote_copy(
            src_ref=srcs[i].at[q, 1 - c], dst_ref=dsts[i].at[q], send_sem=send_sems.at[k],
            recv_sem=recv_sems.at[k], device_id=(x, y, 1 - c), device_id_type=MESH)
            for k, (i, q) in enumerate(jobs)]
        for cp in cps:
            cp.start()
        for cp in cps:
            cp.wait()

    return pl.pallas_call(
        body, name="swap_halves",
        in_specs=[ANY] * na, out_specs=[ANY] * na,
        out_shape=[jax.ShapeDtypeStruct(g.shape[0:1] + g.shape[2:], g.dtype) for g in gs],
        scratch_shapes=[pltpu.SemaphoreType.DMA((len(jobs),)), pltpu.SemaphoreType.DMA((len(jobs),))],
        compiler_params=pltpu.CompilerParams(has_side_effects=True),
    )(*gs)


def _add_halves(c_arr, g, s, name):
    Q, _, R, Cc = g.shape
    tr = min(R, 128)

    def body(c_ref, g_ref, s_ref, b_ref, f_ref):
        p = g_ref[0, 0] + s_ref[0]
        f_ref[0] = p
        b_ref[0] = p.astype(bf16)

    blk = pl.BlockSpec((1, tr, Cc), lambda q, i, cr: (q, i, 0))
    return pl.pallas_call(
        body, name=name,
        grid_spec=pltpu.PrefetchScalarGridSpec(
            num_scalar_prefetch=1, grid=(Q, R // tr),
            in_specs=[pl.BlockSpec((1, 1, tr, Cc), lambda q, i, cr: (q, cr[0], i, 0)), blk], out_specs=[blk, blk]),
        out_shape=[jax.ShapeDtypeStruct((Q, R, Cc), bf16), jax.ShapeDtypeStruct((Q, R, Cc), f32)],
        compiler_params=_cp(("arbitrary", "arbitrary")),
    )(c_arr, g, s)


_FLIPS = [(fx, fy, fc) for fx in (0, 1) for fy in (0, 1) for fc in (0, 1)][1:]


def _scatter_blocks(pbs, pack):
    na = len(pbs)
    R = pack.shape[0]

    def body(*refs):
        srcs, pk = refs[:na], refs[na]
        dsts, rp = refs[na + 1:2 * na + 1], refs[2 * na + 1]
        send_sems, recv_sems = refs[2 * na + 2:]
        x, y, c, chips = _place()

        def big(j, i):
            px, py = chips[j]
            return pltpu.make_async_remote_copy(
                src_ref=srcs[i].at[2 * px + py], dst_ref=dsts[i].at[j], send_sem=send_sems.at[na * j + i],
                recv_sem=recv_sems.at[na * j + i], device_id=(px, py, c), device_id_type=MESH)

        def small(k):
            fx, fy, fc = _FLIPS[k]
            return pltpu.make_async_remote_copy(
                src_ref=pk, dst_ref=rp.at[k], send_sem=send_sems.at[3 * na + k], recv_sem=recv_sems.at[3 * na + k],
                device_id=(x ^ fx, y ^ fy, c ^ fc), device_id_type=MESH)

        cps = [big(j, i) for j in range(3) for i in range(na)] + [small(k) for k in range(7)]
        for cp in cps:
            cp.start()
        for cp in cps:
            cp.wait()

    nsem = 3 * na + 7
    return pl.pallas_call(
        body, name="scatter_blocks",
        in_specs=[ANY] * (na + 1), out_specs=[ANY] * (na + 1),
        out_shape=[jax.ShapeDtypeStruct((3,) + p.shape[1:], p.dtype) for p in pbs]
        + [jax.ShapeDtypeStruct((7, R, 128), f32)],
        scratch_shapes=[pltpu.SemaphoreType.DMA((nsem,)), pltpu.SemaphoreType.DMA((nsem,))],
        compiler_params=pltpu.CompilerParams(has_side_effects=True),
    )(*pbs, pack)


def _sum_blocks(own, r, name):
    R, Cc = own.shape
    tr = min(R, 256)

    def body(own_ref, r_ref, o_ref):
        acc = own_ref[...]
        for j in range(3):
            acc = acc + r_ref[j].astype(f32)
        o_ref[...] = acc

    return pl.pallas_call(
        body, name=name, grid=(R // tr,),
        in_specs=[pl.BlockSpec((tr, Cc), lambda i: (i, 0)), pl.BlockSpec((3, tr, Cc), lambda i: (0, i, 0))],
        out_specs=pl.BlockSpec((tr, Cc), lambda i: (i, 0)),
        out_shape=jax.ShapeDtypeStruct((R, Cc), f32),
        compiler_params=_cp(("arbitrary",)),
    )(own, r)


def _sum_packs(me8_arr, pack, rp):
    R = pack.shape[0]

    def body(me_ref, pk_ref, rp_ref, o_ref):
        me8 = me_ref[0]
        acc = None
        for d in range(8):
            rel = d ^ me8
            term = jnp.where(rel == 0, pk_ref[...], rp_ref[jnp.maximum(rel - 1, 0)])
            acc = term if acc is None else acc + term
        o_ref[...] = acc

    return pl.pallas_call(
        body, name="sum_packs",
        grid_spec=pltpu.PrefetchScalarGridSpec(
            num_scalar_prefetch=1, grid=(1,),
            in_specs=[pl.BlockSpec((R, 128), lambda i, mr: (0, 0)), pl.BlockSpec((7, R, 128), lambda i, mr: (0, 0, 0))],
            out_specs=pl.BlockSpec((R, 128), lambda i, mr: (0, 0))),
        out_shape=jax.ShapeDtypeStruct((R, 128), f32),
        compiler_params=_cp(("arbitrary",)),
    )(me8_arr, pack, rp)


def _swap_finished(fs):
    na = len(fs)

    def body(*refs):
        srcs, dsts = refs[:na], refs[na:2 * na]
        send_sems, recv_sems = refs[2 * na:]
        x, y, c, _ = _place()
        cps = [pltpu.make_async_remote_copy(
            src_ref=srcs[i], dst_ref=dsts[i], send_sem=send_sems.at[i], recv_sem=recv_sems.at[i],
            device_id=(x, y, 1 - c), device_id_type=MESH) for i in range(na)]
        for cp in cps:
            cp.start()
        for cp in cps:
            cp.wait()

    return pl.pallas_call(
        body, name="swap_finished",
        in_specs=[ANY] * na, out_specs=[ANY] * na,
        out_shape=[jax.ShapeDtypeStruct(f.shape, f.dtype) for f in fs],
        scratch_shapes=[pltpu.SemaphoreType.DMA((na,)), pltpu.SemaphoreType.DMA((na,))],
        compiler_params=pltpu.CompilerParams(has_side_effects=True),
    )(*fs)


def _adamw_halves(c_arr, mine, peer, w, m, v, name):
    _, R, Cc = w.shape
    tr = min(R, 256)

    def body(c_ref, mine_ref, peer_ref, w_ref, m_ref, v_ref, g_ref, d_ref, m2_ref, v2_ref):
        g = jnp.where(pl.program_id(0) == c_ref[0], mine_ref[...], peer_ref[...])
        delta, m2, v2 = _adam_math(g, w_ref[0], m_ref[0], v_ref[0])
        g_ref[0] = g
        d_ref[0] = delta
        m2_ref[0] = m2
        v2_ref[0] = v2

    half = pl.BlockSpec((tr, Cc), lambda hh, i, cr: (i, 0))
    full = pl.BlockSpec((1, tr, Cc), lambda hh, i, cr: (hh, i, 0))
    return pl.pallas_call(
        body, name=name,
        grid_spec=pltpu.PrefetchScalarGridSpec(
            num_scalar_prefetch=1, grid=(2, R // tr), in_specs=[half, half, full, full, full], out_specs=[full] * 4),
        out_shape=[jax.ShapeDtypeStruct((2, R, Cc), f32)] * 4,
        compiler_params=_cp(("arbitrary", "arbitrary")),
    )(c_arr, mine, peer, w, m, v)


def _rows8(a):
    flat = a.reshape(-1)
    n = flat.shape[0]
    rows = -(-n // 1024) * 8
    return jnp.pad(flat, (0, rows * 128 - n)).reshape(rows, 128)


def kernel(x, meta_tokens, norm_w, w_in, conv_w, hg_lb_logits, hg_norm_w, gdn_A_log, gdn_dt_bias, gdn_norm_w, w_out, final_norm_w, loss_target, m_meta_tokens, m_norm_w, m_w_in, m_conv_w, m_hg_lb_logits, m_hg_norm_w, m_gdn_A_log, m_gdn_dt_bias, m_gdn_norm_w, m_w_out, m_final_norm_w, v_meta_tokens, v_norm_w, v_w_in, v_conv_w, v_hg_lb_logits, v_hg_norm_w, v_gdn_A_log, v_gdn_dt_bias, v_gdn_norm_w, v_w_out, v_final_norm_w):
    me = 2 * lax.axis_index("x") + lax.axis_index("y")

    g_win, g_wout, g_conv, g_meta = _gather_weights(
        [w_in[0].astype(bf16).reshape(2, D // 2, SHARD_COLS), w_out[0].astype(bf16).reshape(2, D // 8, D)],
        [conv_w[0], meta_tokens])
    w_full = jnp.transpose(g_win.reshape(4, D, SHARD_COLS), (1, 0, 2)).reshape(D, IN_COLS)
    wbig = jnp.pad(w_full, ((0, 0), (0, PC - IN_COLS)))
    wout_full = g_wout.reshape(D, D)
    conv4 = jnp.transpose(g_conv, (1, 0, 2)).reshape(4, 1, 3 * HD)
    meta_full = jnp.transpose(g_meta, (1, 0, 2)).reshape(N_META, D)

    c_arr = lax.axis_index("c").reshape(1).astype(jnp.int32)

    def chip_partials(gw, g_wout_part):
        g_in2 = gw.reshape(1, 2, D // 2, PC)
        g_out4 = g_wout_part.reshape(4, 2, D // 8, D)
        s_in, s_out = _swap_halves([g_in2, g_out4])
        pb_in, pf_in = _add_halves(c_arr, g_in2, s_in, "add_w_in")
        pb_out, pf_out = _add_halves(c_arr, g_out4, s_out, "add_w_out")
        pb_blocks = jnp.transpose(pb_in[0, :, 0:IN_COLS].reshape(D // 2, 4, SHARD_COLS), (1, 0, 2))
        own_in = lax.dynamic_slice(pf_in[0], (0, me * SHARD_COLS), (D // 2, SHARD_COLS))
        own_out = lax.dynamic_index_in_dim(pf_out, me, axis=0, keepdims=False)
        return [pb_blocks, pb_out], [own_in, own_out]

    (loss8, grad_x, d_meta, d_nw, d_conv, d_lb, d_hgw, d_alog, d_dtb, d_gdw, d_fw, pfs, rs) = _local_step(
        x, loss_target, wbig, wout_full, conv4, meta_full, norm_w, hg_lb_logits, hg_norm_w, gdn_A_log, gdn_dt_bias,
        gdn_norm_w, final_norm_w, chip_partials)

    pack = jnp.concatenate([
        loss8, d_nw[0].reshape(8, 128), d_lb.reshape(8, 128), d_hgw, _rows8(d_alog[0, :H]), _rows8(d_dtb[0, :H]),
        d_gdw, d_fw[0].reshape(8, 128), d_meta.reshape(128, 128), d_conv.reshape(48, 128)], axis=0)
    return _reduce_and_update(
        me, c_arr, grad_x, pfs, rs, pack, meta_tokens, norm_w, w_in, conv_w, hg_lb_logits, hg_norm_w, gdn_A_log,
        gdn_dt_bias, gdn_norm_w, w_out, final_norm_w, m_meta_tokens, m_norm_w, m_w_in, m_conv_w, m_hg_lb_logits,
        m_hg_norm_w, m_gdn_A_log, m_gdn_dt_bias, m_gdn_norm_w, m_w_out, m_final_norm_w, v_meta_tokens, v_norm_w, v_w_in,
        v_conv_w, v_hg_lb_logits, v_hg_norm_w, v_gdn_A_log, v_gdn_dt_bias, v_gdn_norm_w, v_w_out, v_final_norm_w)


def _local_step(x, loss_target, wbig, wout_full, conv4, meta_full, norm_w, hg_lb_logits, hg_norm_w, gdn_A_log, gdn_dt_bias,
                gdn_norm_w, final_norm_w, chip_partials):
    h3 = jnp.concatenate([jnp.zeros((NB, PAD, D), f32), jnp.broadcast_to(meta_full[None], (NB, N_META, D)), x], axis=1)
    hflat = h3.reshape(N, D)
    target = jnp.pad(loss_target, ((0, 0), (PAD + N_META, 0), (0, 0))).reshape(N, D)
    l0, l1 = hg_lb_logits[0:1], hg_lb_logits[1:2]
    alog = jnp.pad(gdn_A_log, ((0, 0), (0, DK - H)))
    dtb = jnp.pad(gdn_dt_bias, ((0, 0), (0, DK - H)))
    fw = final_norm_w.reshape(1, D)

    proj, ut = _in_proj(hflat, norm_w, wbig)
    proj3 = proj.reshape(NB, TP, PC)
    o_hg, s_hg = _hg_fwd(proj3, l0, l1)
    cv = _conv_fwd(proj3, conv4)
    o_gd, s_gd, t_gd = _gd_fwd(cv, proj3, alog, dtb)
    (loss8, d_ohg, d_ogd, d_zhg, d_zgd, dh_res, g_wout_part, d_hgw, d_gdw, d_fw) = _out_loss(
        o_hg.reshape(N, HD), o_gd.reshape(N, HD), proj, hg_norm_w, gdn_norm_w, wout_full, hflat, fw, target)
    d_hg, d_l0, d_l1 = _hg_bwd(proj3, l0, l1, s_hg, d_ohg.reshape(NB, TP, HD))
    d_cv, d_ab, d_alog, d_dtb = _gd_bwd(cv, proj3, alog, dtb, s_gd, t_gd, d_ogd.reshape(NB, TP, HD))
    d_qkv, d_conv4 = _conv_bwd(proj3, conv4, d_cv)
    d_hg2, d_qkv2, d_ab2 = d_hg.reshape(N, 3 * HD), d_qkv.reshape(N, 3 * HD), d_ab.reshape(N, DK)
    pieces = [(d_hg2, COL_HG), (d_zhg, COL_ZHG), (d_qkv2, COL_QKV), (d_zgd, COL_ZGD), (d_ab2, COL_AB)]
    gw = _w_grad(ut, pieces)
    pbs, pfs = chip_partials(gw, g_wout_part) if chip_partials else ([], [gw, g_wout_part])
    dh, d_nw, *rs = _in_bwd(pieces, wbig, hflat, norm_w, dh_res, pbs)

    dh3 = dh.reshape(NB, TP, D)
    grad_x = dh3[:, PAD + N_META:, :]
    d_meta = jnp.sum(dh3[:, PAD:PAD + N_META, :], axis=0)
    d_conv = d_conv4[:, 0, :]
    d_lb = jnp.concatenate([d_l0[0:1], d_l1[0:1]], axis=0)
    return loss8, grad_x, d_meta, d_nw, d_conv, d_lb, d_hgw, d_alog, d_dtb, d_gdw, d_fw, pfs, rs


def _reduce_and_update(me, c_arr, grad_x, pfs, rs, pack, meta_tokens, norm_w, w_in, conv_w, hg_lb_logits, hg_norm_w,
                       gdn_A_log, gdn_dt_bias, gdn_norm_w, w_out, final_norm_w, m_meta_tokens, m_norm_w, m_w_in, m_conv_w,
                       m_hg_lb_logits, m_hg_norm_w, m_gdn_A_log, m_gdn_dt_bias, m_gdn_norm_w, m_w_out, m_final_norm_w,
                       v_meta_tokens, v_norm_w, v_w_in, v_conv_w, v_hg_lb_logits, v_hg_norm_w, v_gdn_A_log, v_gdn_dt_bias,
                       v_gdn_norm_w, v_w_out, v_final_norm_w):
    (own_in, own_out), (r_in, r_out) = pfs, rs
    (r_pack,) = _scatter_blocks([], pack)
    f_in = _sum_blocks(own_in, r_in, "sum_w_in")
    f_out = _sum_blocks(own_out, r_out, "sum_w_out")
    o_in, o_out = _swap_finished([f_in, f_out])
    me8_arr = (2 * me + lax.axis_index("c")).reshape(1).astype(jnp.int32)
    small = _sum_packs(me8_arr, pack, r_pack)

    half_out = lambda a: a[0].reshape(2, D // 8, D)
    is0 = lax.axis_index("c") == 0
    g_in = jnp.concatenate([jnp.where(is0, f_in, o_in), jnp.where(is0, o_in, f_in)], axis=0)
    to_cm = lambda a: jnp.transpose(a, (2, 0, 1))
    gi, di, mi, vi = [jnp.transpose(a, (1, 2, 0))[0] for a in _adamw_rows(
        g_in.T.reshape(SHARD_COLS, 1, D), to_cm(w_in), to_cm(m_w_in), to_cm(v_w_in), "adamw_w_in")]
    go, do_, mo, vo = [a.reshape(D // 4, D) for a in _adamw_halves(
        c_arr, f_out, o_out, half_out(w_out), half_out(m_w_out), half_out(v_w_out), "adamw_w_out")]

    g_meta_full = small[64:192].reshape(N_META, D)
    g_meta_loc = lax.dynamic_slice(g_meta_full, (0, me * 256), (N_META, 256))
    gm, dm, mm_, vm = _adamw([g_meta_loc], meta_tokens, m_meta_tokens, v_meta_tokens, "adamw_meta")
    g_conv_full = small[192:240].reshape(4, 1536)
    g_conv_loc = lax.dynamic_slice(g_conv_full, (0, me * 384), (4, 384))
    gc, dc, mc, vc = _adamw([g_conv_loc], conv_w[0], m_conv_w[0], v_conv_w[0], "adamw_conv")

    reps = [(norm_w, m_norm_w, v_norm_w), (hg_lb_logits, m_hg_lb_logits, v_hg_lb_logits),
            (hg_norm_w, m_hg_norm_w, v_hg_norm_w), (gdn_A_log, m_gdn_A_log, v_gdn_A_log),
            (gdn_dt_bias, m_gdn_dt_bias, v_gdn_dt_bias), (gdn_norm_w, m_gdn_norm_w, v_gdn_norm_w),
            (final_norm_w, m_final_norm_w, v_final_norm_w)]
    wp = jnp.concatenate([_rows8(t[0]) for t in reps], axis=0)
    mp = jnp.concatenate([_rows8(t[1]) for t in reps], axis=0)
    vp = jnp.concatenate([_rows8(t[2]) for t in reps], axis=0)
    gr, dr, mr, vr = _adamw([small[8:64]], wp, mp, vp, "adamw_small")

    def unpack(p):
        outs = []
        for i, t in enumerate(reps):
            n = t[0].size
            outs.append(p[8 * i:8 * i + 8].reshape(-1)[:n].reshape(t[0].shape))
        return outs

    def leaves(meta_v, conv_v, in_v, out_v, rep_p):
        nw, lb, hgw, al, db, gdw, fwv = unpack(rep_p)
        return [meta_v, nw, in_v[None], conv_v[None], lb, hgw, al, db, gdw, out_v[None], fwv]

    loss = small[0, 0]
    return (loss, grad_x, *leaves(gm, gc, gi, go, gr), *leaves(dm, dc, di, do_, dr),
            *leaves(mm_, mc, mi, mo, mr), *leaves(vm, vc, vi, vo, vr))
```

```python
import functools

import jax
import jax.numpy as jnp
from jax import lax
from jax.experimental import pallas as pl
from jax.experimental.pallas import tpu as pltpu

f32 = jnp.float32
bf16 = jnp.bfloat16
MESH = pl.DeviceIdType.MESH
ANY = pl.BlockSpec(memory_space=pl.ANY)

D = 1024
NB = 2
N_META = 16
SEQ = 2048
PAD = 48
TP = PAD + N_META + SEQ
C = 64
NCH = TP // C
N = NB * TP
H = 4
DK = 128
HD = H * DK
PC = 4224
IN_COLS = 4104
SHARD_COLS = IN_COLS // 4
COL_HG, COL_ZHG, COL_QKV, COL_ZGD, COL_AB = 0, 3 * HD, 4 * HD, 7 * HD, 8 * HD
EPS = 1e-6
ADAM_LR, ADAM_B1, ADAM_B2, ADAM_EPS, ADAM_WD, ADAM_STEP = 0.001, 0.9, 0.999, 1e-08, 0.01, 10
VMEM_LIMIT = 56 * 1024 * 1024

P_HG = dict(lvl=1, av=1, qs=1, su=1)
P_GD = dict(kk=1, inv=1, sol=1, ws=1, qk=1, o=1, su=1)


def _cp(sem=None, **kw):
    return pltpu.CompilerParams(dimension_semantics=sem, vmem_limit_bytes=VMEM_LIMIT, **kw)


_DIMS = {"nn": (((1,), (0,)), ((), ())), "nt": (((1,), (1,)), ((), ())), "tn": (((0,), (0,)), ((), ()))}


def _split(x):
    hi = x.astype(bf16)
    return hi, (x - hi.astype(f32)).astype(bf16)


def _dg(a, b, kind, passes):
    d = lambda x, y: lax.dot_general(x, y, _DIMS[kind], preferred_element_type=f32)
    if passes == 1:
        return d(a.astype(bf16), b.astype(bf16))
    ah, al = _split(a)
    bh, bl = _split(b)
    return d(ah, bh) + d(ah, bl) + d(al, bh)


@functools.partial(jax.custom_vjp, nondiff_argnums=(2, 3))
def mmx(a, b, kind, passes):
    return _dg(a, b, kind, passes)


def _mmx_fwd(a, b, kind, passes):
    return _dg(a, b, kind, passes), (a, b)


def _mmx_bwd(kind, passes, res, g):
    a, b = res
    if kind == "nn":
        return _dg(g, b, "nt", passes), _dg(a, g, "tn", passes)
    if kind == "nt":
        return _dg(g, b, "nn", passes), _dg(g, a, "tn", passes)
    return _dg(b, g, "nt", passes), _dg(a, g, "nn", passes)


mmx.defvjp(_mmx_fwd, _mmx_bwd)


def _mask_dg(mask, x):
    xh, xl = _split(x)
    return jnp.dot(jnp.concatenate([mask, mask], axis=1), jnp.concatenate([xh, xl], axis=0), preferred_element_type=f32)


@functools.partial(jax.custom_vjp, nondiff_argnums=(2,))
def mask_mm(mask, x, bwd_passes):
    return _mask_dg(mask, x)


def _mask_fwd(mask, x, bwd_passes):
    return _mask_dg(mask, x), mask


def _mask_bwd(bwd_passes, mask, g):
    d = lambda y: lax.dot_general(mask, y, _DIMS["tn"], preferred_element_type=f32)
    if bwd_passes == 1:
        return None, d(g.astype(bf16))
    gh, gl = _split(g)
    return None, d(gh) + d(gl)


mask_mm.defvjp(_mask_fwd, _mask_bwd)


def bdot(a, b):
    return jnp.dot(a.astype(bf16), b.astype(bf16), preferred_element_type=f32)


def bdot_nt(a, b):
    return lax.dot_general(a.astype(bf16), b.astype(bf16), _DIMS["nt"], preferred_element_type=f32)


def bdot_tn(a, b):
    return lax.dot_general(a.astype(bf16), b.astype(bf16), _DIMS["tn"], preferred_element_type=f32)


def _iota2(n, m):
    return lax.broadcasted_iota(jnp.int32, (n, m), 0), lax.broadcasted_iota(jnp.int32, (n, m), 1)


sigmoid = jax.nn.sigmoid


def silu(x):
    return x * sigmoid(x)


def softplus(x):
    return jnp.maximum(x, 0.0) + jnp.log(1.0 + jnp.exp(-jnp.abs(x)))


def rmsnorm(x, w):
    return x * lax.rsqrt(jnp.mean(x * x, axis=-1, keepdims=True) + EPS) * w


def hg_masks():
    t, r = _iota2(C, C)
    mats = [r <= t, r > t]
    lvl = []
    for l in range(1, 7):
        sz = 1 << l
        half = sz >> 1
        seg_t = t >> l
        upper_t = (t & (sz - 1)) >= half
        mid_t = seg_t * sz + half - 1
        mats.append((upper_t & (r > mid_t) & (r <= t)) | ((~upper_t) & (r > t) & (r <= mid_t)))
        lvl.append(((seg_t == (r >> l)) & upper_t & ((r & (sz - 1)) < half)).astype(f32))
    stk = jnp.concatenate([m.astype(bf16) for m in mats], axis=0)
    return stk, lvl, (t == r).astype(f32)


def _head(a, h):
    return a[:, h * DK:(h + 1) * DK]


def _run(*gens):
    results = [None] * len(gens)
    live = list(range(len(gens)))
    while live:
        for i in list(live):
            try:
                next(gens[i])
            except StopIteration as e:
                results[i] = e.value
                live.remove(i)
    return results


def hg_chunk(St, ps, l0, l1):
    return _run(hg_stages(St, ps, l0, l1))[0]


def gd_chunk(S, cs, abs_, alog, dtb, t_saved=None):
    return _run(gd_stages(S, cs, abs_, alog, dtb, t_saved))[0]


def mix_chunk(St, ps, l0, l1, S, cs, abs_, alog, dtb):
    (sn_h, o_h), (sn_g, o_g, t_pack) = _run(hg_stages(St, ps, l0, l1), gd_stages(S, cs, abs_, alog, dtb))
    return sn_h, o_h, sn_g, o_g, t_pack


def hg_stages(St, ps, l0, l1):
    m = jnp.maximum(l0, l1)
    e0 = jnp.exp(l0 - m)
    e1 = jnp.exp(l1 - m)
    lb = e0 / (e0 + e1)
    stk, lvl, eye = hg_masks()
    msk = [eye] + lvl
    qs, ks, vs, qG, kR, eGl = [], [], [], [], [], []
    for p in ps:
        pq, pf, v = p[:, 0:HD], p[:, HD:2 * HD], p[:, 2 * HD:3 * HD]
        q = silu(pq)
        f = lb + (1.0 - lb) * sigmoid(pf)
        k = 1.0 - f
        logf = jnp.log(f)
        Dm = mask_mm(stk, logf, 1)
        ex = [jnp.exp(Dm[(2 + i) * C:(3 + i) * C]) for i in range(6)]
        qs.append([q] + [q * e for e in ex])
        ks.append([k] + [k * e for e in ex])
        vs.append(v)
        qG.append(q * jnp.exp(Dm[0:C]))
        kR.append(k * jnp.exp(Dm[C:2 * C]))
        eGl.append(jnp.exp(jnp.sum(logf, axis=0, keepdims=True)))
    yield
    units = [(b, h) for b in range(len(ps)) for h in range(H)]
    parts = []
    for i in range(7):
        parts.append([msk[i] * mmx(_head(qs[b][i], h), _head(ks[b][i], h), "nt", P_HG["lvl"]) for b, h in units])
        yield
    A = [functools.reduce(lambda x, y: x + y, [parts[i][n] for i in range(7)]) for n in range(len(units))]
    qS = [mmx(_head(qG[b], h), St[n], "nt", P_HG["qs"]) for n, (b, h) in enumerate(units)]
    Sn = [St[n] * _head(eGl[b], h) + mmx(_head(vs[b], h), _head(kR[b], h), "tn", P_HG["su"])
          for n, (b, h) in enumerate(units)]
    yield
    outs = [mmx(A[n], _head(vs[b], h), "nn", P_HG["av"]) + qS[n] for n, (b, h) in enumerate(units)]
    return tuple(Sn), tuple(jnp.concatenate(outs[b * H:(b + 1) * H], axis=1) for b in range(len(ps)))


@jax.custom_vjp
def use_inverse(A, T):
    return T


def _use_inverse_fwd(A, T):
    return T, T


def _use_inverse_bwd(T, g):
    return -_dg(T, _dg(g, T, "nt", P_GD["inv"]), "tn", P_GD["inv"]), jnp.zeros_like(T)


use_inverse.defvjp(_use_inverse_fwd, _use_inverse_bwd)


def gd_stages(S, cs, abs_, alog, dtb, t_saved=None):
    t, r = _iota2(C, C)
    tri = (r <= t).astype(bf16)
    ups = (r > t).astype(bf16)
    lane = lax.broadcasted_iota(jnp.int32, (1, DK), 1)
    subl = lax.broadcasted_iota(jnp.int32, (8, 1), 0)
    eye = (t == r).astype(f32)
    strict = (r < t).astype(f32)
    bd = ((t >> 4) == (r >> 4)).astype(f32)
    qa, ka, va, b4, gam4, grev4, gam4T, glast4 = [], [], [], [], [], [], [], []
    for c, ab in zip(cs, abs_):
        qa.append(silu(c[:, 0:HD]))
        ka.append(silu(c[:, HD:2 * HD]))
        va.append(silu(c[:, 2 * HD:3 * HD]))
        g4 = -jnp.exp(alog) * softplus(ab + dtb)
        b4.append(sigmoid(ab))
        gam4.append(mask_mm(tri, g4, 2))
        grev4.append(mask_mm(ups, g4, 2))
        gam4T.append(gam4[-1].T)
        glast4.append(jnp.sum(g4, axis=0, keepdims=True))
    yield
    units = [(b, h) for b in range(len(cs)) for h in range(H)]
    nu = range(len(units))
    inv = lambda a, b: [mmx(a[n], b[n], "nn", P_GD["inv"]) for n in nu]
    v = [_head(va[b], h) for b, h in units]
    q = [_head(qa[b], h) for b, h in units]
    k = [_head(ka[b], h) for b, h in units]
    q = [x * lax.rsqrt(jnp.sum(x * x, -1, keepdims=True) + EPS) * (DK ** -0.5) for x in q]
    k = [x * lax.rsqrt(jnp.sum(x * x, -1, keepdims=True) + EPS) for x in k]
    oh = [(lane == h).astype(f32) for h in range(H)]
    gam_c = [jnp.sum(gam4[b] * oh[h], -1, keepdims=True) for b, h in units]
    grev_c = [jnp.sum(grev4[b] * oh[h], -1, keepdims=True) for b, h in units]
    beta = [jnp.sum(b4[b] * (lane == h + H).astype(f32), -1, keepdims=True) for b, h in units]
    glast = [jnp.sum(glast4[b] * oh[h], -1, keepdims=True) for b, h in units]
    gam_r = [jnp.sum(gam4T[b][0:8, :] * (subl == h).astype(f32), axis=0, keepdims=True) for b, h in units]
    dec = [jnp.exp(jnp.where(r <= t, gam_c[n] - gam_r[n], -1e30)) for n in nu]
    egam = [jnp.exp(gam_c[n]) for n in nu]
    kk = [mmx(k[n], k[n], "nt", P_GD["kk"]) for n in nu]
    qk = [mmx(q[n], k[n], "nt", P_GD["qk"]) * dec[n] for n in nu]
    yield
    A = [beta[n] * kk[n] * dec[n] * strict for n in nu]
    Dg = [A[n] * bd for n in nu]
    L = [A[n] - Dg[n] for n in nu]
    if t_saved is None:
        ImD = [eye - Dg[n] for n in nu]
        D2 = inv(Dg, Dg)
        yield
        P1 = inv(ImD, [eye + x for x in D2])
        D4 = inv(D2, D2)
        yield
        P2 = inv(P1, [eye + x for x in D4])
        D8 = inv(D4, D4)
        yield
        M = inv(P2, [eye + x for x in D8])
        yield
        Nn = inv(M, L)
        yield
        N2 = inv(Nn, Nn)
        yield
        T1 = inv([eye - x for x in Nn], [eye + x for x in N2])
        yield
        Tinv = inv(T1, M)
        yield
    else:
        Tinv = [use_inverse(A[n], t_saved[b][:, h * DK:h * DK + C]) for n, (b, h) in enumerate(units)]
    rhs = [jnp.concatenate([beta[n] * v[n], (beta[n] * egam[n]) * k[n]], axis=1) for n in nu]
    sol = [mmx(Tinv[n], rhs[n], "nn", P_GD["sol"]) for n in nu]
    yield
    qwS = [mmx(jnp.concatenate([q[n] * egam[n], sol[n][:, DK:2 * DK]], axis=0), S[n], "nn", P_GD["ws"]) for n in nu]
    yield
    u = [sol[n][:, 0:DK] - qwS[n][C:2 * C] for n in nu]
    outs = [qwS[n][0:C] + mmx(qk[n], u[n], "nn", P_GD["o"]) for n in nu]
    Sn = [jnp.exp(glast[n]) * S[n] + mmx(k[n] * jnp.exp(grev_c[n]), u[n], "tn", P_GD["su"]) for n in nu]
    zpad = jnp.zeros((C, DK - C), f32)
    t_pack = tuple(jnp.concatenate([x for n in range(b * H, (b + 1) * H) for x in (lax.stop_gradient(Tinv[n]), zpad)],
                                   axis=1) for b in range(len(cs)))
    return tuple(Sn), tuple(jnp.concatenate(outs[b * H:(b + 1) * H], axis=1) for b in range(len(cs))), t_pack


def _in_proj(hflat, norm_w, wbig):
    tm = 384

    def body(h_ref, nw_ref, w_ref, p_ref, ut_ref):
        u = rmsnorm(h_ref[...], nw_ref[...])
        ut_ref[...] = u.T.astype(bf16)
        p_ref[...] = jnp.dot(u.astype(bf16), w_ref[...], preferred_element_type=f32)

    return pl.pallas_call(
        body, name="in_proj", grid=(N // tm,),
        in_specs=[pl.BlockSpec((tm, D), lambda i: (i, 0)), pl.BlockSpec((1, D), lambda i: (0, 0)),
                  pl.BlockSpec((D, PC), lambda i: (0, 0))],
        out_specs=[pl.BlockSpec((tm, PC), lambda i: (i, 0)), pl.BlockSpec((D, tm), lambda i: (0, i))],
        out_shape=[jax.ShapeDtypeStruct((N, PC), f32), jax.ShapeDtypeStruct((D, N), bf16)],
        compiler_params=_cp(("arbitrary",)),
    )(hflat, norm_w, wbig)


NU = NB * H
_REV = lambda c: NCH - 1 - c
_FWD = lambda c: c


def _tok_spec(w, ix, col=0):
    return pl.BlockSpec((NB, C, w), lambda c: (0, ix(c), col))


def _state_spec(ix):
    return pl.BlockSpec((NB, 1, H, DK, DK), lambda c: (0, ix(c), 0, 0, 0))


def _row_spec(w):
    return pl.BlockSpec((1, w), lambda c: (0, 0))


def _rows(ref):
    return tuple(ref[b] for b in range(NB))


def _hg_extra_specs(ix):
    return [_row_spec(HD), _row_spec(HD)]


def _gd_extra_specs(ix):
    return [_tok_spec(DK, ix, COL_AB // DK), _row_spec(DK), _row_spec(DK)]


def _mix_fwd(proj3, cv, l0, l1, alog, dtb):
    def body(p_ref, c_ref, ab_ref, l0_ref, l1_ref, al_ref, db_ref, oh_ref, sh_ref, og_ref, sg_ref, t_ref, sth, stg):
        @pl.when(pl.program_id(0) == 0)
        def _():
            sth[...] = jnp.zeros_like(sth)
            stg[...] = jnp.zeros_like(stg)

        Sh = tuple(sth[n] for n in range(NU))
        Sg = tuple(stg[n] for n in range(NU))
        for n in range(NU):
            sh_ref[n // H, 0, n % H] = Sh[n]
            sg_ref[n // H, 0, n % H] = Sg[n]
        snh, oh, sng, og, tp = mix_chunk(Sh, _rows(p_ref), l0_ref[...], l1_ref[...],
                                         Sg, _rows(c_ref), _rows(ab_ref), al_ref[...], db_ref[...])
        for n in range(NU):
            sth[n] = snh[n]
            stg[n] = sng[n]
        for b in range(NB):
            oh_ref[b] = oh[b]
            og_ref[b] = og[b]
            t_ref[b] = tp[b]

    tok = jax.ShapeDtypeStruct((NB, TP, HD), f32)
    st = jax.ShapeDtypeStruct((NB, NCH, H, DK, DK), f32)
    return pl.pallas_call(
        body, name="mix_fwd", grid=(NCH,),
        in_specs=[_tok_spec(3 * HD, _FWD), _tok_spec(3 * HD, _FWD), _tok_spec(DK, _FWD, COL_AB // DK),
                  _row_spec(HD), _row_spec(HD), _row_spec(DK), _row_spec(DK)],
        out_specs=[_tok_spec(HD, _FWD), _state_spec(_FWD), _tok_spec(HD, _FWD), _state_spec(_FWD), _tok_spec(HD, _FWD)],
        out_shape=[tok, st, tok, st, tok],
        scratch_shapes=[pltpu.VMEM((NU, DK, DK), f32), pltpu.VMEM((NU, DK, DK), f32)],
        compiler_params=_cp(("arbitrary",)),
    )(proj3, cv, proj3, l0, l1, alog, dtb)


def _hg_bwd(proj3, l0, l1, s_saved, do):
    def body(p_ref, l0_ref, l1_ref, s_ref, do_ref, dp_ref, dl0_ref, dl1_ref, dst):
        @pl.when(pl.program_id(0) == 0)
        def _():
            dst[...] = jnp.zeros_like(dst)
            dl0_ref[...] = jnp.zeros_like(dl0_ref)
            dl1_ref[...] = jnp.zeros_like(dl1_ref)

        S = tuple(s_ref[n // H, 0, n % H] for n in range(NU))
        _, vjp = jax.vjp(hg_chunk, S, _rows(p_ref), l0_ref[...], l1_ref[...])
        dS, dp, dl0, dl1 = vjp((tuple(dst[n] for n in range(NU)), _rows(do_ref)))
        for n in range(NU):
            dst[n] = dS[n]
        for b in range(NB):
            dp_ref[b] = dp[b].astype(bf16)
        dl0_ref[...] += jnp.broadcast_to(dl0, (8, HD))
        dl1_ref[...] += jnp.broadcast_to(dl1, (8, HD))

    acc = pl.BlockSpec((8, HD), lambda c: (0, 0))
    return pl.pallas_call(
        body, name="hg_bwd", grid=(NCH,),
        in_specs=[_tok_spec(3 * HD, _REV)] + _hg_extra_specs(_REV) + [_state_spec(_REV), _tok_spec(HD, _REV)],
        out_specs=[_tok_spec(3 * HD, _REV), acc, acc],
        out_shape=[jax.ShapeDtypeStruct((NB, TP, 3 * HD), bf16), jax.ShapeDtypeStruct((8, HD), f32),
                   jax.ShapeDtypeStruct((8, HD), f32)],
        scratch_shapes=[pltpu.VMEM((NU, DK, DK), f32)],
        compiler_params=_cp(("arbitrary",)),
    )(proj3, l0, l1, s_saved, do)


def _gd_bwd(cv, proj3, alog, dtb, s_saved, t_saved, do):
    def body(c_ref, ab_ref, al_ref, db_ref, s_ref, t_ref, do_ref, dc_ref, dab_ref, dal_ref, ddb_ref, dst):
        @pl.when(pl.program_id(0) == 0)
        def _():
            dst[...] = jnp.zeros_like(dst)
            dal_ref[...] = jnp.zeros_like(dal_ref)
            ddb_ref[...] = jnp.zeros_like(ddb_ref)

        S = tuple(s_ref[n // H, 0, n % H] for n in range(NU))
        t_rows = _rows(t_ref)
        fn = lambda *a: gd_chunk(*a, t_saved=t_rows)[0:2]
        _, vjp = jax.vjp(fn, S, _rows(c_ref), _rows(ab_ref), al_ref[...], db_ref[...])
        dS, dc, dab, dal, ddb = vjp((tuple(dst[n] for n in range(NU)), _rows(do_ref)))
        for n in range(NU):
            dst[n] = dS[n]
        for b in range(NB):
            dc_ref[b] = dc[b]
            dab_ref[b] = dab[b].astype(bf16)
        dal_ref[...] += jnp.broadcast_to(dal, (8, DK))
        ddb_ref[...] += jnp.broadcast_to(ddb, (8, DK))

    acc = pl.BlockSpec((8, DK), lambda c: (0, 0))
    return pl.pallas_call(
        body, name="gd_bwd", grid=(NCH,),
        in_specs=[_tok_spec(3 * HD, _REV)] + _gd_extra_specs(_REV)
        + [_state_spec(_REV), _tok_spec(HD, _REV), _tok_spec(HD, _REV)],
        out_specs=[_tok_spec(3 * HD, _REV), _tok_spec(DK, _REV), acc, acc],
        out_shape=[jax.ShapeDtypeStruct((NB, TP, 3 * HD), f32), jax.ShapeDtypeStruct((NB, TP, DK), bf16),
                   jax.ShapeDtypeStruct((8, DK), f32), jax.ShapeDtypeStruct((8, DK), f32)],
        scratch_shapes=[pltpu.VMEM((NU, DK, DK), f32)],
        compiler_params=_cp(("arbitrary",)),
    )(cv, proj3, alog, dtb, s_saved, t_saved, do)


def _conv_fwd(proj3, conv4):
    def body(x_ref, w_ref, y_ref):
        x = x_ref[0]
        row = lax.broadcasted_iota(jnp.int32, (TP, 1), 0)
        y = w_ref[3] * x
        for s in (1, 2, 3):
            y = y + w_ref[3 - s] * jnp.where(row >= s, pltpu.roll(x, s, 0), 0.0)
        y_ref[0] = y

    return pl.pallas_call(
        body, name="conv_fwd", grid=(NB, 3),
        in_specs=[pl.BlockSpec((1, TP, HD), lambda b, j: (b, 0, COL_QKV // HD + j)),
                  pl.BlockSpec((4, 1, HD), lambda b, j: (0, 0, j))],
        out_specs=pl.BlockSpec((1, TP, HD), lambda b, j: (b, 0, j)),
        out_shape=jax.ShapeDtypeStruct((NB, TP, 3 * HD), f32),
        compiler_params=_cp(("arbitrary", "arbitrary")),
    )(proj3, conv4)


def _conv_bwd(proj3, conv4, dy):
    def body(x_ref, w_ref, dy_ref, dx_ref, dw_ref):
        @pl.when(pl.program_id(1) == 0)
        def _():
            dw_ref[...] = jnp.zeros_like(dw_ref)

        x = x_ref[0]
        g = dy_ref[0]
        row = lax.broadcasted_iota(jnp.int32, (TP, 1), 0)
        dx = w_ref[3] * g
        dw_ref[3] += jnp.broadcast_to(jnp.sum(x * g, axis=0, keepdims=True), (8, HD))
        for s in (1, 2, 3):
            dx = dx + w_ref[3 - s] * jnp.where(row < TP - s, pltpu.roll(g, TP - s, 0), 0.0)
            xs = jnp.where(row >= s, pltpu.roll(x, s, 0), 0.0)
            dw_ref[3 - s] += jnp.broadcast_to(jnp.sum(xs * g, axis=0, keepdims=True), (8, HD))
        dx_ref[0] = dx.astype(bf16)

    return pl.pallas_call(
        body, name="conv_bwd", grid=(3, NB),
        in_specs=[pl.BlockSpec((1, TP, HD), lambda j, b: (b, 0, COL_QKV // HD + j)),
                  pl.BlockSpec((4, 1, HD), lambda j, b: (0, 0, j)), pl.BlockSpec((1, TP, HD), lambda j, b: (b, 0, j))],
        out_specs=[pl.BlockSpec((1, TP, HD), lambda j, b: (b, 0, j)), pl.BlockSpec((4, 8, HD), lambda j, b: (0, 0, j))],
        out_shape=[jax.ShapeDtypeStruct((NB, TP, 3 * HD), bf16), jax.ShapeDtypeStruct((4, 8, 3 * HD), f32)],
        compiler_params=_cp(("arbitrary", "arbitrary")),
    )(proj3, conv4, dy)


def _gated(o, z, w):
    outs = []
    for hh in range(H):
        sl = slice(hh * DK, (hh + 1) * DK)
        outs.append(rmsnorm(o[:, sl], w) * silu(z[:, sl]))
    return jnp.concatenate(outs, axis=-1)


def _out_loss(o_hg, o_gd, proj, hgw, gdw, wout, hflat, fw, target):
    tm = 384

    def body(ohg_ref, ogd_ref, zhg_ref, zgd_ref, hgw_ref, gdw_ref, wo_ref, h_ref, fw_ref, tg_ref,
             loss_ref, dohg_ref, dogd_ref, dzhg_ref, dzgd_ref, dh_ref, dwo_ref, dhgw_ref, dgdw_ref, dfw_ref):
        i = pl.program_id(0)

        @pl.when(i == 0)
        def _():
            for r in (loss_ref, dwo_ref, dhgw_ref, dgdw_ref, dfw_ref):
                r[...] = jnp.zeros_like(r)

        row = i * tm + lax.broadcasted_iota(jnp.int32, (tm, 1), 0)
        tok = jnp.where(row >= TP, row - TP, row)
        valid = (tok >= PAD + N_META).astype(f32)
        hval = h_ref[...]
        tgt = tg_ref[...]

        def mix(ohg, ogd, zhg, zgd, w1, w2):
            return jnp.concatenate([_gated(ohg, zhg, w1), _gated(ogd, zgd, w2)], axis=-1)

        y, vjp_mix = jax.vjp(mix, ohg_ref[...], ogd_ref[...], zhg_ref[...], zgd_ref[...], hgw_ref[...], gdw_ref[...])
        out = bdot(y, wo_ref[...])

        def head(out, fwv):
            err = (rmsnorm(hval + out, fwv) - tgt) * valid
            return 0.5 * jnp.sum(jnp.mean(err * err, axis=-1, keepdims=True))

        loss, vjp_head = jax.vjp(head, out, fw_ref[...])
        dout, dfw = vjp_head(jnp.ones((), f32))
        dh_ref[...] = dout
        dy = bdot_nt(dout, wo_ref[...])
        dwo_ref[...] += bdot_tn(y, dout)
        dohg, dogd, dzhg, dzgd, dw1, dw2 = vjp_mix(dy)
        dohg_ref[...] = dohg
        dogd_ref[...] = dogd
        dzhg_ref[...] = dzhg.astype(bf16)
        dzgd_ref[...] = dzgd.astype(bf16)
        loss_ref[...] += jnp.broadcast_to(loss, (8, DK))
        dhgw_ref[...] += jnp.broadcast_to(dw1, (8, DK))
        dgdw_ref[...] += jnp.broadcast_to(dw2, (8, DK))
        dfw_ref[...] += jnp.broadcast_to(dfw, (8, D))

    row = lambda w: pl.BlockSpec((tm, w), lambda i: (i, 0))
    whole = lambda r, w: pl.BlockSpec((r, w), lambda i: (0, 0))
    col = lambda c0: pl.BlockSpec((tm, HD), lambda i: (i, c0 // HD))
    return pl.pallas_call(
        body, name="out_loss", grid=(N // tm,),
        in_specs=[row(HD), row(HD), col(COL_ZHG), col(COL_ZGD),
                  whole(1, DK), whole(1, DK), whole(D, D), row(D), whole(1, D), row(D)],
        out_specs=[whole(8, DK), row(HD), row(HD), row(HD), row(HD), row(D), whole(D, D),
                   whole(8, DK), whole(8, DK), whole(8, D)],
        out_shape=[jax.ShapeDtypeStruct((8, DK), f32)] + [jax.ShapeDtypeStruct((N, HD), f32)] * 2
        + [jax.ShapeDtypeStruct((N, HD), bf16)] * 2
        + [jax.ShapeDtypeStruct((N, D), f32), jax.ShapeDtypeStruct((D, D), f32),
           jax.ShapeDtypeStruct((8, DK), f32), jax.ShapeDtypeStruct((8, DK), f32), jax.ShapeDtypeStruct((8, D), f32)],
        compiler_params=_cp(("arbitrary",)),
    )(o_hg, o_gd, proj, proj, hgw, gdw, wout, hflat, fw, target)


def _in_bwd(pieces, wbig, hflat, norm_w, dh_res, pbs):
    tm = 384
    nsteps = N // tm
    np_ = len(pieces)
    na = len(pbs)
    offs = [c0 for _, c0 in pieces]
    widths = [d.shape[1] for d, _ in pieces]

    def body(*refs):
        d_refs = refs[:np_]
        w_ref, h_ref, nw_ref, dhr_ref = refs[np_:np_ + 4]
        srcs = refs[np_ + 4:np_ + 4 + na]
        dh_ref, dnw_ref = refs[np_ + 4 + na:np_ + 6 + na]
        dsts = refs[np_ + 6 + na:np_ + 6 + 2 * na]
        sems = refs[np_ + 6 + 2 * na:]
        i = pl.program_id(0)

        def copies():
            if not na:
                return []
            x, y, c, chips = _place()
            return [pltpu.make_async_remote_copy(
                src_ref=srcs[a].at[2 * px + py], dst_ref=dsts[a].at[j], send_sem=sems[0].at[na * j + a],
                recv_sem=sems[1].at[na * j + a], device_id=(px, py, c), device_id_type=MESH)
                for j, (px, py) in enumerate(chips) for a in range(na)]

        @pl.when(i == 0)
        def _():
            dnw_ref[...] = jnp.zeros_like(dnw_ref)
            for cp in copies():
                cp.start()

        du = jnp.zeros((tm, D), f32)
        for d_ref, off, wd in zip(d_refs, offs, widths):
            du = du + bdot_nt(d_ref[...], w_ref[:, off:off + wd])
        _, vjp = jax.vjp(rmsnorm, h_ref[...], nw_ref[...])
        dh, dnw = vjp(du)
        dh_ref[...] = dh + dhr_ref[...]
        dnw_ref[...] += jnp.broadcast_to(dnw, (8, D))

        @pl.when(i == nsteps - 1)
        def _():
            for cp in copies():
                cp.wait()

    row = lambda w: pl.BlockSpec((tm, w), lambda i: (i, 0))
    return pl.pallas_call(
        body, name="in_bwd", grid=(nsteps,),
        in_specs=[row(w) for w in widths]
        + [pl.BlockSpec((D, PC), lambda i: (0, 0)), row(D), pl.BlockSpec((1, D), lambda i: (0, 0)), row(D)] + [ANY] * na,
        out_specs=[row(D), pl.BlockSpec((8, D), lambda i: (0, 0))] + [ANY] * na,
        out_shape=[jax.ShapeDtypeStruct((N, D), f32), jax.ShapeDtypeStruct((8, D), f32)]
        + [jax.ShapeDtypeStruct((3,) + p.shape[1:], p.dtype) for p in pbs],
        scratch_shapes=[pltpu.SemaphoreType.DMA((3 * na,)), pltpu.SemaphoreType.DMA((3 * na,))] if na else [],
        compiler_params=_cp(("arbitrary",)),
    )(*[d for d, _ in pieces], wbig, hflat, norm_w, dh_res, *pbs)


def _w_grad(ut, pieces):
    tk = 384
    offs = [c0 for _, c0 in pieces]
    widths = [d.shape[1] for d, _ in pieces]

    def body(u_ref, *refs):
        d_refs, o_ref = refs[:-1], refs[-1]

        @pl.when(pl.program_id(0) == 0)
        def _():
            o_ref[...] = jnp.zeros_like(o_ref)

        u = u_ref[...]
        for d_ref, off, wd in zip(d_refs, offs, widths):
            o_ref[:, off:off + wd] += jnp.dot(u, d_ref[...], preferred_element_type=f32)

    return pl.pallas_call(
        body, name="w_grad", grid=(N // tk,),
        in_specs=[pl.BlockSpec((D, tk), lambda k: (0, k))] + [pl.BlockSpec((tk, w), lambda k: (k, 0)) for w in widths],
        out_specs=pl.BlockSpec((D, PC), lambda k: (0, 0)),
        out_shape=jax.ShapeDtypeStruct((D, PC), f32),
        compiler_params=_cp(("arbitrary",)),
    )(ut, *[d for d, _ in pieces])


def _adam_math(g, w, m, v):
    m2 = ADAM_B1 * m + (1.0 - ADAM_B1) * g
    v2 = ADAM_B2 * v + (1.0 - ADAM_B2) * (g * g)
    m_hat = m2 / (1.0 - ADAM_B1 ** ADAM_STEP)
    v_hat = v2 / (1.0 - ADAM_B2 ** ADAM_STEP)
    delta = -ADAM_LR * (m_hat / (jnp.sqrt(v_hat) + ADAM_EPS) + ADAM_WD * w)
    return delta, m2, v2


def _adamw(gs, w, m, v, name):
    R, Cc = w.shape
    tr = 256 if R % 256 == 0 else R
    ng = len(gs)

    def body(*refs):
        g = refs[0][...]
        for r in refs[1:ng]:
            g = g + r[...]
        w_ref, m_ref, v_ref, g_ref, d_ref, m2_ref, v2_ref = refs[ng:]
        delta, m2, v2 = _adam_math(g, w_ref[...], m_ref[...], v_ref[...])
        g_ref[...] = g
        d_ref[...] = delta
        m2_ref[...] = m2
        v2_ref[...] = v2

    spec = pl.BlockSpec((tr, Cc), lambda i: (i, 0))
    return pl.pallas_call(
        body, name=name, grid=(R // tr,),
        in_specs=[spec] * (ng + 3), out_specs=[spec] * 4,
        out_shape=[jax.ShapeDtypeStruct((R, Cc), f32)] * 4,
        compiler_params=_cp(("arbitrary",)),
    )(*gs, w, m, v)


def _adamw_rows(g, w, m, v, name):
    R, _, Cc = w.shape
    tr = R // 9

    def body(g_ref, w_ref, m_ref, v_ref, go_ref, d_ref, m2_ref, v2_ref):
        g = g_ref[...]
        delta, m2, v2 = _adam_math(g, w_ref[...], m_ref[...], v_ref[...])
        go_ref[...] = g
        d_ref[...] = delta
        m2_ref[...] = m2
        v2_ref[...] = v2

    spec = pl.BlockSpec((tr, 1, Cc), lambda i: (i, 0, 0))
    return pl.pallas_call(
        body, name=name, grid=(R // tr,),
        in_specs=[spec] * 4, out_specs=[spec] * 4,
        out_shape=[jax.ShapeDtypeStruct((R, 1, Cc), f32)] * 4,
        compiler_params=_cp(("arbitrary",)),
    )(g, w, m, v)


def _sum_slots(r, name):
    S, R, Cc = r.shape
    tr = 256 if R % 256 == 0 else R

    def body(r_ref, o_ref):
        acc = r_ref[0]
        for s in range(1, S):
            acc = acc + r_ref[s]
        o_ref[...] = acc

    return pl.pallas_call(
        body, name=name, grid=(R // tr,),
        in_specs=[pl.BlockSpec((S, tr, Cc), lambda i: (0, i, 0))], out_specs=pl.BlockSpec((tr, Cc), lambda i: (i, 0)),
        out_shape=jax.ShapeDtypeStruct((R, Cc), f32),
        compiler_params=_cp(("arbitrary",)),
    )(r)


def _place():
    x, y, c = lax.axis_index("x"), lax.axis_index("y"), lax.axis_index("c")
    return x, y, c, [(1 - x, y), (x, 1 - y), (1 - x, 1 - y)]


def _gather_weights(halved, whole):
    nh, nw = len(halved), len(whole)
    na = nh + nw

    def body(*refs):
        srcs, dsts = refs[:na], refs[na:2 * na]
        send_sems, recv_sems, loc_sems = refs[2 * na:2 * na + 3]
        stage = refs[2 * na + 3:]
        x, y, c, chips = _place()
        me = 2 * x + y
        loads = [pltpu.make_async_copy(s, v, loc_sems.at[i]) for i, (s, v) in enumerate(zip(srcs, stage))]
        locs = [pltpu.make_async_copy(v, d.at[me], loc_sems.at[i]) for i, (v, d) in enumerate(zip(stage, dsts))]
        for cp in loads:
            cp.start()

        def ici(j, i, slot):
            px, py = chips[j]
            src = srcs[i].at[c] if i < nh else srcs[i]
            dst = dsts[i].at[slot, c] if i < nh else dsts[i].at[slot]
            return pltpu.make_async_remote_copy(
                src_ref=src, dst_ref=dst, send_sem=send_sems.at[na * j + i], recv_sem=recv_sems.at[na * j + i],
                device_id=(px, py, c), device_id_type=MESH)

        def d2d(j, i, half):
            px, py = chips[j]
            blk = dsts[i].at[2 * px + py, half]
            return pltpu.make_async_remote_copy(
                src_ref=blk, dst_ref=blk, send_sem=send_sems.at[3 * na + nh * j + i],
                recv_sem=recv_sems.at[3 * na + nh * j + i], device_id=(x, y, 1 - c), device_id_type=MESH)

        sends = [ici(j, i, me) for j in range(3) for i in range(na)]
        for cp in sends:
            cp.start()
        for ld, st in zip(loads, locs):
            ld.wait()
            st.start()
        for j, (px, py) in enumerate(chips):
            for i in range(na):
                ici(j, i, 2 * px + py).wait_recv()
                if i < nh:
                    fwd = d2d(j, i, c)
                    fwd.start()
                    sends.append(fwd)
        for j in range(3):
            for i in range(nh):
                d2d(j, i, 1 - c).wait_recv()
        for cp in sends:
            cp.wait_send()
        for cp in locs:
            cp.wait()

    nsem = 3 * na + 3 * nh
    return pl.pallas_call(
        body, name="gather_weights",
        in_specs=[ANY] * na, out_specs=[ANY] * na,
        out_shape=[jax.ShapeDtypeStruct((4,) + s.shape, s.dtype) for s in list(halved) + list(whole)],
        scratch_shapes=[pltpu.SemaphoreType.DMA((nsem,)), pltpu.SemaphoreType.DMA((nsem,)),
                        pltpu.SemaphoreType.DMA((na,))] + [pltpu.VMEM(s.shape, s.dtype) for s in list(halved) + list(whole)],
        compiler_params=pltpu.CompilerParams(has_side_effects=True, vmem_limit_bytes=VMEM_LIMIT),
    )(*halved, *whole)


def _swap_halves(gs):
    na = len(gs)
    jobs = [(i, q) for i in range(na) for q in range(gs[i].shape[0])]

    def body(*refs):
        srcs, dsts = refs[:na], refs[na:2 * na]
        send_sems, recv_sems = refs[2 * na:]
        x, y, c, _ = _place()
        cps = [pltpu.make_async_remote_copy(
            src_ref=srcs[i].at[q, 1 - c], dst_ref=dsts[i].at[q], send_sem=send_sems.at[k],
            recv_sem=recv_sems.at[k], device_id=(x, y, 1 - c), device_id_type=MESH)
            for k, (i, q) in enumerate(jobs)]
        for cp in cps:
            cp.start()
        for cp in cps:
            cp.wait()

    return pl.pallas_call(
        body, name="swap_halves",
        in_specs=[ANY] * na, out_specs=[ANY] * na,
        out_shape=[jax.ShapeDtypeStruct(g.shape[0:1] + g.shape[2:], g.dtype) for g in gs],
        scratch_shapes=[pltpu.SemaphoreType.DMA((len(jobs),)), pltpu.SemaphoreType.DMA((len(jobs),))],
        compiler_params=pltpu.CompilerParams(has_side_effects=True),
    )(*gs)


def _add_halves(c_arr, g, s, name):
    Q, _, R, Cc = g.shape
    tr = min(R, 128)

    def body(c_ref, g_ref, s_ref, b_ref, f_ref):
        p = g_ref[0, 0] + s_ref[0]
        f_ref[0] = p
        b_ref[0] = p.astype(bf16)

    blk = pl.BlockSpec((1, tr, Cc), lambda q, i, cr: (q, i, 0))
    return pl.pallas_call(
        body, name=name,
        grid_spec=pltpu.PrefetchScalarGridSpec(
            num_scalar_prefetch=1, grid=(Q, R // tr),
            in_specs=[pl.BlockSpec((1, 1, tr, Cc), lambda q, i, cr: (q, cr[0], i, 0)), blk], out_specs=[blk, blk]),
        out_shape=[jax.ShapeDtypeStruct((Q, R, Cc), bf16), jax.ShapeDtypeStruct((Q, R, Cc), f32)],
        compiler_params=_cp(("arbitrary", "arbitrary")),
    )(c_arr, g, s)


_FLIPS = [(fx, fy, fc) for fx in (0, 1) for fy in (0, 1) for fc in (0, 1)][1:]


def _scatter_blocks(pbs, pack):
    na = len(pbs)
    R = pack.shape[0]

    def body(*refs):
        srcs, pk = refs[:na], refs[na]
        dsts, rp = refs[na + 1:2 * na + 1], refs[2 * na + 1]
        send_sems, recv_sems = refs[2 * na + 2:]
        x, y, c, chips = _place()

        def big(j, i):
            px, py = chips[j]
            return pltpu.make_async_remote_copy(
                src_ref=srcs[i].at[2 * px + py], dst_ref=dsts[i].at[j], send_sem=send_sems.at[na * j + i],
                recv_sem=recv_sems.at[na * j + i], device_id=(px, py, c), device_id_type=MESH)

        def small(k):
            fx, fy, fc = _FLIPS[k]
            return pltpu.make_async_remote_copy(
                src_ref=pk, dst_ref=rp.at[k], send_sem=send_sems.at[3 * na + k], recv_sem=recv_sems.at[3 * na + k],
                device_id=(x ^ fx, y ^ fy, c ^ fc), device_id_type=MESH)

        cps = [big(j, i) for j in range(3) for i in range(na)] + [small(k) for k in range(7)]
        for cp in cps:
            cp.start()
        for cp in cps:
            cp.wait()

    nsem = 3 * na + 7
    return pl.pallas_call(
        body, name="scatter_blocks",
        in_specs=[ANY] * (na + 1), out_specs=[ANY] * (na + 1),
        out_shape=[jax.ShapeDtypeStruct((3,) + p.shape[1:], p.dtype) for p in pbs]
        + [jax.ShapeDtypeStruct((7, R, 128), f32)],
        scratch_shapes=[pltpu.SemaphoreType.DMA((nsem,)), pltpu.SemaphoreType.DMA((nsem,))],
        compiler_params=pltpu.CompilerParams(has_side_effects=True),
    )(*pbs, pack)


def _sum_blocks(own, r, name):
    R, Cc = own.shape
    tr = min(R, 256)

    def body(own_ref, r_ref, o_ref):
        acc = own_ref[...]
        for j in range(3):
            acc = acc + r_ref[j].astype(f32)
        o_ref[...] = acc

    return pl.pallas_call(
        body, name=name, grid=(R // tr,),
        in_specs=[pl.BlockSpec((tr, Cc), lambda i: (i, 0)), pl.BlockSpec((3, tr, Cc), lambda i: (0, i, 0))],
        out_specs=pl.BlockSpec((tr, Cc), lambda i: (i, 0)),
        out_shape=jax.ShapeDtypeStruct((R, Cc), f32),
        compiler_params=_cp(("arbitrary",)),
    )(own, r)


def _sum_packs(me8_arr, pack, rp):
    R = pack.shape[0]

    def body(me_ref, pk_ref, rp_ref, o_ref):
        me8 = me_ref[0]
        acc = None
        for d in range(8):
            rel = d ^ me8
            term = jnp.where(rel == 0, pk_ref[...], rp_ref[jnp.maximum(rel - 1, 0)])
            acc = term if acc is None else acc + term
        o_ref[...] = acc

    return pl.pallas_call(
        body, name="sum_packs",
        grid_spec=pltpu.PrefetchScalarGridSpec(
            num_scalar_prefetch=1, grid=(1,),
            in_specs=[pl.BlockSpec((R, 128), lambda i, mr: (0, 0)), pl.BlockSpec((7, R, 128), lambda i, mr: (0, 0, 0))],
            out_specs=pl.BlockSpec((R, 128), lambda i, mr: (0, 0))),
        out_shape=jax.ShapeDtypeStruct((R, 128), f32),
        compiler_params=_cp(("arbitrary",)),
    )(me8_arr, pack, rp)


def _swap_finished(fs):
    na = len(fs)

    def body(*refs):
        srcs, dsts = refs[:na], refs[na:2 * na]
        send_sems, recv_sems = refs[2 * na:]
        x, y, c, _ = _place()
        cps = [pltpu.make_async_remote_copy(
            src_ref=srcs[i], dst_ref=dsts[i], send_sem=send_sems.at[i], recv_sem=recv_sems.at[i],
            device_id=(x, y, 1 - c), device_id_type=MESH) for i in range(na)]
        for cp in cps:
            cp.start()
        for cp in cps:
            cp.wait()

    return pl.pallas_call(
        body, name="swap_finished",
        in_specs=[ANY] * na, out_specs=[ANY] * na,
        out_shape=[jax.ShapeDtypeStruct(f.shape, f.dtype) for f in fs],
        scratch_shapes=[pltpu.SemaphoreType.DMA((na,)), pltpu.SemaphoreType.DMA((na,))],
        compiler_params=pltpu.CompilerParams(has_side_effects=True),
    )(*fs)


def _adamw_halves(c_arr, mine, peer, w, m, v, name):
    _, R, Cc = w.shape
    tr = min(R, 256)

    def body(c_ref, mine_ref, peer_ref, w_ref, m_ref, v_ref, g_ref, d_ref, m2_ref, v2_ref):
        g = jnp.where(pl.program_id(0) == c_ref[0], mine_ref[...], peer_ref[...])
        delta, m2, v2 = _adam_math(g, w_ref[0], m_ref[0], v_ref[0])
        g_ref[0] = g
        d_ref[0] = delta
        m2_ref[0] = m2
        v2_ref[0] = v2

    half = pl.BlockSpec((tr, Cc), lambda hh, i, cr: (i, 0))
    full = pl.BlockSpec((1, tr, Cc), lambda hh, i, cr: (hh, i, 0))
    return pl.pallas_call(
        body, name=name,
        grid_spec=pltpu.PrefetchScalarGridSpec(
            num_scalar_prefetch=1, grid=(2, R // tr), in_specs=[half, half, full, full, full], out_specs=[full] * 4),
        out_shape=[jax.ShapeDtypeStruct((2, R, Cc), f32)] * 4,
        compiler_params=_cp(("arbitrary", "arbitrary")),
    )(c_arr, mine, peer, w, m, v)


def _rows8(a):
    flat = a.reshape(-1)
    n = flat.shape[0]
    rows = -(-n // 1024) * 8
    return jnp.pad(flat, (0, rows * 128 - n)).reshape(rows, 128)


def kernel(x, meta_tokens, norm_w, w_in, conv_w, hg_lb_logits, hg_norm_w, gdn_A_log, gdn_dt_bias, gdn_norm_w, w_out, final_norm_w, loss_target, m_meta_tokens, m_norm_w, m_w_in, m_conv_w, m_hg_lb_logits, m_hg_norm_w, m_gdn_A_log, m_gdn_dt_bias, m_gdn_norm_w, m_w_out, m_final_norm_w, v_meta_tokens, v_norm_w, v_w_in, v_conv_w, v_hg_lb_logits, v_hg_norm_w, v_gdn_A_log, v_gdn_dt_bias, v_gdn_norm_w, v_w_out, v_final_norm_w):
    me = 2 * lax.axis_index("x") + lax.axis_index("y")

    g_win, g_wout, g_conv, g_meta = _gather_weights(
        [w_in[0].astype(bf16).reshape(2, D // 2, SHARD_COLS), w_out[0].astype(bf16).reshape(2, D // 8, D)],
        [conv_w[0], meta_tokens])
    w_full = jnp.transpose(g_win.reshape(4, D, SHARD_COLS), (1, 0, 2)).reshape(D, IN_COLS)
    wbig = jnp.pad(w_full, ((0, 0), (0, PC - IN_COLS)))
    wout_full = g_wout.reshape(D, D)
    conv4 = jnp.transpose(g_conv, (1, 0, 2)).reshape(4, 1, 3 * HD)
    meta_full = jnp.transpose(g_meta, (1, 0, 2)).reshape(N_META, D)

    c_arr = lax.axis_index("c").reshape(1).astype(jnp.int32)

    def chip_partials(gw, g_wout_part):
        g_in2 = gw.reshape(1, 2, D // 2, PC)
        g_out4 = g_wout_part.reshape(4, 2, D // 8, D)
        s_in, s_out = _swap_halves([g_in2, g_out4])
        pb_in, pf_in = _add_halves(c_arr, g_in2, s_in, "add_w_in")
        pb_out, pf_out = _add_halves(c_arr, g_out4, s_out, "add_w_out")
        pb_blocks = jnp.transpose(pb_in[0, :, 0:IN_COLS].reshape(D // 2, 4, SHARD_COLS), (1, 0, 2))
        own_in = lax.dynamic_slice(pf_in[0], (0, me * SHARD_COLS), (D // 2, SHARD_COLS))
        own_out = lax.dynamic_index_in_dim(pf_out, me, axis=0, keepdims=False)
        return [pb_blocks, pb_out], [own_in, own_out]

    (loss8, grad_x, d_meta, d_nw, d_conv, d_lb, d_hgw, d_alog, d_dtb, d_gdw, d_fw, pfs, rs) = _local_step(
        x, loss_target, wbig, wout_full, conv4, meta_full, norm_w, hg_lb_logits, hg_norm_w, gdn_A_log, gdn_dt_bias,
        gdn_norm_w, final_norm_w, chip_partials)

    pack = jnp.concatenate([
        loss8, d_nw[0].reshape(8, 128), d_lb.reshape(8, 128), d_hgw, _rows8(d_alog[0, :H]), _rows8(d_dtb[0, :H]),
        d_gdw, d_fw[0].reshape(8, 128), d_meta.reshape(128, 128), d_conv.reshape(48, 128)], axis=0)
    return _reduce_and_update(
        me, c_arr, grad_x, pfs, rs, pack, meta_tokens, norm_w, w_in, conv_w, hg_lb_logits, hg_norm_w, gdn_A_log,
        gdn_dt_bias, gdn_norm_w, w_out, final_norm_w, m_meta_tokens, m_norm_w, m_w_in, m_conv_w, m_hg_lb_logits,
        m_hg_norm_w, m_gdn_A_log, m_gdn_dt_bias, m_gdn_norm_w, m_w_out, m_final_norm_w, v_meta_tokens, v_norm_w, v_w_in,
        v_conv_w, v_hg_lb_logits, v_hg_norm_w, v_gdn_A_log, v_gdn_dt_bias, v_gdn_norm_w, v_w_out, v_final_norm_w)


def _local_step(x, loss_target, wbig, wout_full, conv4, meta_full, norm_w, hg_lb_logits, hg_norm_w, gdn_A_log, gdn_dt_bias,
                gdn_norm_w, final_norm_w, chip_partials):
    h3 = jnp.concatenate([jnp.zeros((NB, PAD, D), f32), jnp.broadcast_to(meta_full[None], (NB, N_META, D)), x], axis=1)
    hflat = h3.reshape(N, D)
    target = jnp.pad(loss_target, ((0, 0), (PAD + N_META, 0), (0, 0))).reshape(N, D)
    l0, l1 = hg_lb_logits[0:1], hg_lb_logits[1:2]
    alog = jnp.pad(gdn_A_log, ((0, 0), (0, DK - H)))
    dtb = jnp.pad(gdn_dt_bias, ((0, 0), (0, DK - H)))
    fw = final_norm_w.reshape(1, D)

    proj, ut = _in_proj(hflat, norm_w, wbig)
    proj3 = proj.reshape(NB, TP, PC)
    cv = _conv_fwd(proj3, conv4)
    o_hg, s_hg, o_gd, s_gd, t_gd = _mix_fwd(proj3, cv, l0, l1, alog, dtb)
    (loss8, d_ohg, d_ogd, d_zhg, d_zgd, dh_res, g_wout_part, d_hgw, d_gdw, d_fw) = _out_loss(
        o_hg.reshape(N, HD), o_gd.reshape(N, HD), proj, hg_norm_w, gdn_norm_w, wout_full, hflat, fw, target)
    d_hg, d_l0, d_l1 = _hg_bwd(proj3, l0, l1, s_hg, d_ohg.reshape(NB, TP, HD))
    d_cv, d_ab, d_alog, d_dtb = _gd_bwd(cv, proj3, alog, dtb, s_gd, t_gd, d_ogd.reshape(NB, TP, HD))
    d_qkv, d_conv4 = _conv_bwd(proj3, conv4, d_cv)
    d_hg2, d_qkv2, d_ab2 = d_hg.reshape(N, 3 * HD), d_qkv.reshape(N, 3 * HD), d_ab.reshape(N, DK)
    pieces = [(d_hg2, COL_HG), (d_zhg, COL_ZHG), (d_qkv2, COL_QKV), (d_zgd, COL_ZGD), (d_ab2, COL_AB)]
    gw = _w_grad(ut, pieces)
    pbs, pfs = chip_partials(gw, g_wout_part) if chip_partials else ([], [gw, g_wout_part])
    dh, d_nw, *rs = _in_bwd(pieces, wbig, hflat, norm_w, dh_res, pbs)

    dh3 = dh.reshape(NB, TP, D)
    grad_x = dh3[:, PAD + N_META:, :]
    d_meta = jnp.sum(dh3[:, PAD:PAD + N_META, :], axis=0)
    d_conv = d_conv4[:, 0, :]
    d_lb = jnp.concatenate([d_l0[0:1], d_l1[0:1]], axis=0)
    return loss8, grad_x, d_meta, d_nw, d_conv, d_lb, d_hgw, d_alog, d_dtb, d_gdw, d_fw, pfs, rs


def _reduce_and_update(me, c_arr, grad_x, pfs, rs, pack, meta_tokens, norm_w, w_in, conv_w, hg_lb_logits, hg_norm_w,
                       gdn_A_log, gdn_dt_bias, gdn_norm_w, w_out, final_norm_w, m_meta_tokens, m_norm_w, m_w_in, m_conv_w,
                       m_hg_lb_logits, m_hg_norm_w, m_gdn_A_log, m_gdn_dt_bias, m_gdn_norm_w, m_w_out, m_final_norm_w,
                       v_meta_tokens, v_norm_w, v_w_in, v_conv_w, v_hg_lb_logits, v_hg_norm_w, v_gdn_A_log, v_gdn_dt_bias,
                       v_gdn_norm_w, v_w_out, v_final_norm_w):
    (own_in, own_out), (r_in, r_out) = pfs, rs
    (r_pack,) = _scatter_blocks([], pack)
    f_in = _sum_blocks(own_in, r_in, "sum_w_in")
    f_out = _sum_blocks(own_out, r_out, "sum_w_out")
    o_in, o_out = _swap_finished([f_in, f_out])
    me8_arr = (2 * me + lax.axis_index("c")).reshape(1).astype(jnp.int32)
    small = _sum_packs(me8_arr, pack, r_pack)

    half_out = lambda a: a[0].reshape(2, D // 8, D)
    is0 = lax.axis_index("c") == 0
    g_in = jnp.concatenate([jnp.where(is0, f_in, o_in), jnp.where(is0, o_in, f_in)], axis=0)
    to_cm = lambda a: jnp.transpose(a, (2, 0, 1))
    gi, di, mi, vi = [jnp.transpose(a, (1, 2, 0))[0] for a in _adamw_rows(
        g_in.T.reshape(SHARD_COLS, 1, D), to_cm(w_in), to_cm(m_w_in), to_cm(v_w_in), "adamw_w_in")]
    go, do_, mo, vo = [a.reshape(D // 4, D) for a in _adamw_halves(
        c_arr, f_out, o_out, half_out(w_out), half_out(m_w_out), half_out(v_w_out), "adamw_w_out")]

    g_meta_full = small[64:192].reshape(N_META, D)
    g_meta_loc = lax.dynamic_slice(g_meta_full, (0, me * 256), (N_META, 256))
    gm, dm, mm_, vm = _adamw([g_meta_loc], meta_tokens, m_meta_tokens, v_meta_tokens, "adamw_meta")
    g_conv_full = small[192:240].reshape(4, 1536)
    g_conv_loc = lax.dynamic_slice(g_conv_full, (0, me * 384), (4, 384))
    gc, dc, mc, vc = _adamw([g_conv_loc], conv_w[0], m_conv_w[0], v_conv_w[0], "adamw_conv")

    reps = [(norm_w, m_norm_w, v_norm_w), (hg_lb_logits, m_hg_lb_logits, v_hg_lb_logits),
            (hg_norm_w, m_hg_norm_w, v_hg_norm_w), (gdn_A_log, m_gdn_A_log, v_gdn_A_log),
            (gdn_dt_bias, m_gdn_dt_bias, v_gdn_dt_bias), (gdn_norm_w, m_gdn_norm_w, v_gdn_norm_w),
            (final_norm_w, m_final_norm_w, v_final_norm_w)]
    wp = jnp.concatenate([_rows8(t[0]) for t in reps], axis=0)
    mp = jnp.concatenate([_rows8(t[1]) for t in reps], axis=0)
    vp = jnp.concatenate([_rows8(t[2]) for t in reps], axis=0)
    gr, dr, mr, vr = _adamw([small[8:64]], wp, mp, vp, "adamw_small")

    def unpack(p):
        outs = []
        for i, t in enumerate(reps):
            n = t[0].size
            outs.append(p[8 * i:8 * i + 8].reshape(-1)[:n].reshape(t[0].shape))
        return outs

    def leaves(meta_v, conv_v, in_v, out_v, rep_p):
        nw, lb, hgw, al, db, gdw, fwv = unpack(rep_p)
        return [meta_v, nw, in_v[None], conv_v[None], lb, hgw, al, db, gdw, out_v[None], fwv]

    loss = small[0, 0]
    return (loss, grad_x, *leaves(gm, gc, gi, go, gr), *leaves(dm, dc, di, do_, dr),
            *leaves(mm_, mc, mi, mo, mr), *leaves(vm, vc, vi, vo, vr))
```

```python
import functools

import jax
import jax.numpy as jnp
from jax import lax
from jax.experimental import pallas as pl
from jax.experimental.pallas import tpu as pltpu

f32 = jnp.float32
bf16 = jnp.bfloat16
MESH = pl.DeviceIdType.MESH
ANY = pl.BlockSpec(memory_space=pl.ANY)

D = 1024
NB = 2
N_META = 16
SEQ = 2048
PAD = 48
TP = PAD + N_META + SEQ
C = 64
NCH = TP // C
N = NB * TP
H = 4
DK = 128
HD = H * DK
PC = 4224
IN_COLS = 4104
SHARD_COLS = IN_COLS // 4
COL_HG, COL_ZHG, COL_QKV, COL_ZGD, COL_AB = 0, 3 * HD, 4 * HD, 7 * HD, 8 * HD
EPS = 1e-6
ADAM_LR, ADAM_B1, ADAM_B2, ADAM_EPS, ADAM_WD, ADAM_STEP = 0.001, 0.9, 0.999, 1e-08, 0.01, 10
VMEM_LIMIT = 56 * 1024 * 1024

P_HG = dict(lvl=1, av=1, qs=1, su=1)
P_GD = dict(kk=1, inv=1, sol=1, ws=1, qk=1, o=1, su=1)


def _cp(sem=None, **kw):
    return pltpu.CompilerParams(dimension_semantics=sem, vmem_limit_bytes=VMEM_LIMIT, **kw)


_DIMS = {"nn": (((1,), (0,)), ((), ())), "nt": (((1,), (1,)), ((), ())), "tn": (((0,), (0,)), ((), ()))}


def _split(x):
    hi = x.astype(bf16)
    return hi, (x - hi.astype(f32)).astype(bf16)


def _dg(a, b, kind, passes):
    d = lambda x, y: lax.dot_general(x, y, _DIMS[kind], preferred_element_type=f32)
    if passes == 1:
        return d(a.astype(bf16), b.astype(bf16))
    ah, al = _split(a)
    bh, bl = _split(b)
    return d(ah, bh) + d(ah, bl) + d(al, bh)


@functools.partial(jax.custom_vjp, nondiff_argnums=(2, 3))
def mmx(a, b, kind, passes):
    return _dg(a, b, kind, passes)


def _mmx_fwd(a, b, kind, passes):
    return _dg(a, b, kind, passes), (a, b)


def _mmx_bwd(kind, passes, res, g):
    a, b = res
    if kind == "nn":
        return _dg(g, b, "nt", passes), _dg(a, g, "tn", passes)
    if kind == "nt":
        return _dg(g, b, "nn", passes), _dg(g, a, "tn", passes)
    return _dg(b, g, "nt", passes), _dg(a, g, "nn", passes)


mmx.defvjp(_mmx_fwd, _mmx_bwd)


def _mask_dg(mask, x):
    xh, xl = _split(x)
    return jnp.dot(jnp.concatenate([mask, mask], axis=1), jnp.concatenate([xh, xl], axis=0), preferred_element_type=f32)


@functools.partial(jax.custom_vjp, nondiff_argnums=(2,))
def mask_mm(mask, x, bwd_passes):
    return _mask_dg(mask, x)


def _mask_fwd(mask, x, bwd_passes):
    return _mask_dg(mask, x), mask


def _mask_bwd(bwd_passes, mask, g):
    d = lambda y: lax.dot_general(mask, y, _DIMS["tn"], preferred_element_type=f32)
    if bwd_passes == 1:
        return None, d(g.astype(bf16))
    gh, gl = _split(g)
    return None, d(gh) + d(gl)


mask_mm.defvjp(_mask_fwd, _mask_bwd)


def bdot(a, b):
    return jnp.dot(a.astype(bf16), b.astype(bf16), preferred_element_type=f32)


def bdot_nt(a, b):
    return lax.dot_general(a.astype(bf16), b.astype(bf16), _DIMS["nt"], preferred_element_type=f32)


def bdot_tn(a, b):
    return lax.dot_general(a.astype(bf16), b.astype(bf16), _DIMS["tn"], preferred_element_type=f32)


def _iota2(n, m):
    return lax.broadcasted_iota(jnp.int32, (n, m), 0), lax.broadcasted_iota(jnp.int32, (n, m), 1)


sigmoid = jax.nn.sigmoid


def silu(x):
    return x * sigmoid(x)


def softplus(x):
    return jnp.maximum(x, 0.0) + jnp.log(1.0 + jnp.exp(-jnp.abs(x)))


def rmsnorm(x, w):
    return x * lax.rsqrt(jnp.mean(x * x, axis=-1, keepdims=True) + EPS) * w


def hg_masks():
    t, r = _iota2(C, C)
    mats = [r <= t, r > t]
    lvl = []
    for l in range(1, 7):
        sz = 1 << l
        half = sz >> 1
        seg_t = t >> l
        upper_t = (t & (sz - 1)) >= half
        mid_t = seg_t * sz + half - 1
        mats.append((upper_t & (r > mid_t) & (r <= t)) | ((~upper_t) & (r > t) & (r <= mid_t)))
        lvl.append(((seg_t == (r >> l)) & upper_t & ((r & (sz - 1)) < half)).astype(f32))
    stk = jnp.concatenate([m.astype(bf16) for m in mats], axis=0)
    return stk, lvl, (t == r).astype(f32)


def _head(a, h):
    return a[:, h * DK:(h + 1) * DK]


def _run(*gens):
    results = [None] * len(gens)
    live = list(range(len(gens)))
    while live:
        for i in list(live):
            try:
                next(gens[i])
            except StopIteration as e:
                results[i] = e.value
                live.remove(i)
    return results


def hg_chunk(St, ps, l0, l1):
    return _run(hg_stages(St, ps, l0, l1))[0]


def gd_chunk(S, cs, abs_, alog, dtb, t_saved=None):
    return _run(gd_stages(S, cs, abs_, alog, dtb, t_saved))[0]


def mix_chunk(St, ps, l0, l1, S, cs, abs_, alog, dtb):
    (sn_h, o_h), (sn_g, o_g, t_pack) = _run(hg_stages(St, ps, l0, l1), gd_stages(S, cs, abs_, alog, dtb))
    return sn_h, o_h, sn_g, o_g, t_pack


def hg_stages(St, ps, l0, l1):
    m = jnp.maximum(l0, l1)
    e0 = jnp.exp(l0 - m)
    e1 = jnp.exp(l1 - m)
    lb = e0 / (e0 + e1)
    stk, lvl, eye = hg_masks()
    msk = [eye] + lvl
    qs, ks, vs, qG, kR, eGl = [], [], [], [], [], []
    for p in ps:
        pq, pf, v = p[:, 0:HD], p[:, HD:2 * HD], p[:, 2 * HD:3 * HD]
        q = silu(pq)
        f = lb + (1.0 - lb) * sigmoid(pf)
        k = 1.0 - f
        logf = jnp.log(f)
        Dm = mask_mm(stk, logf, 1)
        ex = [jnp.exp(Dm[(2 + i) * C:(3 + i) * C]) for i in range(6)]
        qs.append([q] + [q * e for e in ex])
        ks.append([k] + [k * e for e in ex])
        vs.append(v)
        qG.append(q * jnp.exp(Dm[0:C]))
        kR.append(k * jnp.exp(Dm[C:2 * C]))
        eGl.append(jnp.exp(jnp.sum(logf, axis=0, keepdims=True)))
    yield
    units = [(b, h) for b in range(len(ps)) for h in range(H)]
    parts = []
    for i in range(7):
        parts.append([msk[i] * mmx(_head(qs[b][i], h), _head(ks[b][i], h), "nt", P_HG["lvl"]) for b, h in units])
        yield
    A = [functools.reduce(lambda x, y: x + y, [parts[i][n] for i in range(7)]) for n in range(len(units))]
    qS = [mmx(_head(qG[b], h), St[n], "nt", P_HG["qs"]) for n, (b, h) in enumerate(units)]
    Sn = [St[n] * _head(eGl[b], h) + mmx(_head(vs[b], h), _head(kR[b], h), "tn", P_HG["su"])
          for n, (b, h) in enumerate(units)]
    yield
    outs = [mmx(A[n], _head(vs[b], h), "nn", P_HG["av"]) + qS[n] for n, (b, h) in enumerate(units)]
    return tuple(Sn), tuple(jnp.concatenate(outs[b * H:(b + 1) * H], axis=1) for b in range(len(ps)))


@jax.custom_vjp
def use_inverse(A, T):
    return T


def _use_inverse_fwd(A, T):
    return T, T


def _use_inverse_bwd(T, g):
    return -_dg(T, _dg(g, T, "nt", P_GD["inv"]), "tn", P_GD["inv"]), jnp.zeros_like(T)


use_inverse.defvjp(_use_inverse_fwd, _use_inverse_bwd)


def gd_stages(S, cs, abs_, alog, dtb, t_saved=None):
    t, r = _iota2(C, C)
    tri = (r <= t).astype(bf16)
    ups = (r > t).astype(bf16)
    lane = lax.broadcasted_iota(jnp.int32, (1, DK), 1)
    subl = lax.broadcasted_iota(jnp.int32, (8, 1), 0)
    eye = (t == r).astype(f32)
    strict = (r < t).astype(f32)
    bd = ((t >> 4) == (r >> 4)).astype(f32)
    qa, ka, va, b4, gam4, grev4, gam4T, glast4 = [], [], [], [], [], [], [], []
    for c, ab in zip(cs, abs_):
        qa.append(silu(c[:, 0:HD]))
        ka.append(silu(c[:, HD:2 * HD]))
        va.append(silu(c[:, 2 * HD:3 * HD]))
        g4 = -jnp.exp(alog) * softplus(ab + dtb)
        b4.append(sigmoid(ab))
        gam4.append(mask_mm(tri, g4, 2))
        grev4.append(mask_mm(ups, g4, 2))
        gam4T.append(gam4[-1].T)
        glast4.append(jnp.sum(g4, axis=0, keepdims=True))
    yield
    units = [(b, h) for b in range(len(cs)) for h in range(H)]
    nu = range(len(units))
    inv = lambda a, b: [mmx(a[n], b[n], "nn", P_GD["inv"]) for n in nu]
    v = [_head(va[b], h) for b, h in units]
    q = [_head(qa[b], h) for b, h in units]
    k = [_head(ka[b], h) for b, h in units]
    q = [x * lax.rsqrt(jnp.sum(x * x, -1, keepdims=True) + EPS) * (DK ** -0.5) for x in q]
    k = [x * lax.rsqrt(jnp.sum(x * x, -1, keepdims=True) + EPS) for x in k]
    oh = [(lane == h).astype(f32) for h in range(H)]
    gam_c = [jnp.sum(gam4[b] * oh[h], -1, keepdims=True) for b, h in units]
    grev_c = [jnp.sum(grev4[b] * oh[h], -1, keepdims=True) for b, h in units]
    beta = [jnp.sum(b4[b] * (lane == h + H).astype(f32), -1, keepdims=True) for b, h in units]
    glast = [jnp.sum(glast4[b] * oh[h], -1, keepdims=True) for b, h in units]
    gam_r = [jnp.sum(gam4T[b][0:8, :] * (subl == h).astype(f32), axis=0, keepdims=True) for b, h in units]
    dec = [jnp.exp(jnp.where(r <= t, gam_c[n] - gam_r[n], -1e30)) for n in nu]
    egam = [jnp.exp(gam_c[n]) for n in nu]
    kk = [mmx(k[n], k[n], "nt", P_GD["kk"]) for n in nu]
    qk = [mmx(q[n], k[n], "nt", P_GD["qk"]) * dec[n] for n in nu]
    yield
    A = [beta[n] * kk[n] * dec[n] * strict for n in nu]
    Dg = [A[n] * bd for n in nu]
    L = [A[n] - Dg[n] for n in nu]
    if t_saved is None:
        ImD = [eye - Dg[n] for n in nu]
        D2 = inv(Dg, Dg)
        yield
        P1 = inv(ImD, [eye + x for x in D2])
        D4 = inv(D2, D2)
        yield
        P2 = inv(P1, [eye + x for x in D4])
        D8 = inv(D4, D4)
        yield
        M = inv(P2, [eye + x for x in D8])
        yield
        Nn = inv(M, L)
        yield
        N2 = inv(Nn, Nn)
        yield
        T1 = inv([eye - x for x in Nn], [eye + x for x in N2])
        yield
        Tinv = inv(T1, M)
        yield
    else:
        Tinv = [use_inverse(A[n], t_saved[b][:, h * DK:h * DK + C]) for n, (b, h) in enumerate(units)]
    rhs = [jnp.concatenate([beta[n] * v[n], (beta[n] * egam[n]) * k[n]], axis=1) for n in nu]
    sol = [mmx(Tinv[n], rhs[n], "nn", P_GD["sol"]) for n in nu]
    yield
    qwS = [mmx(jnp.concatenate([q[n] * egam[n], sol[n][:, DK:2 * DK]], axis=0), S[n], "nn", P_GD["ws"]) for n in nu]
    yield
    u = [sol[n][:, 0:DK] - qwS[n][C:2 * C] for n in nu]
    outs = [qwS[n][0:C] + mmx(qk[n], u[n], "nn", P_GD["o"]) for n in nu]
    Sn = [jnp.exp(glast[n]) * S[n] + mmx(k[n] * jnp.exp(grev_c[n]), u[n], "tn", P_GD["su"]) for n in nu]
    zpad = jnp.zeros((C, DK - C), f32)
    t_pack = tuple(jnp.concatenate([x for n in range(b * H, (b + 1) * H) for x in (lax.stop_gradient(Tinv[n]), zpad)],
                                   axis=1) for b in range(len(cs)))
    return tuple(Sn), tuple(jnp.concatenate(outs[b * H:(b + 1) * H], axis=1) for b in range(len(cs))), t_pack


def _in_proj(hflat, norm_w, wbig):
    tm = 384

    def body(h_ref, nw_ref, w_ref, p_ref, ut_ref):
        u = rmsnorm(h_ref[...], nw_ref[...])
        ut_ref[...] = u.T.astype(bf16)
        p_ref[...] = jnp.dot(u.astype(bf16), w_ref[...], preferred_element_type=f32)

    return pl.pallas_call(
        body, name="in_proj", grid=(N // tm,),
        in_specs=[pl.BlockSpec((tm, D), lambda i: (i, 0)), pl.BlockSpec((1, D), lambda i: (0, 0)),
                  pl.BlockSpec((D, PC), lambda i: (0, 0))],
        out_specs=[pl.BlockSpec((tm, PC), lambda i: (i, 0)), pl.BlockSpec((D, tm), lambda i: (0, i))],
        out_shape=[jax.ShapeDtypeStruct((N, PC), f32), jax.ShapeDtypeStruct((D, N), bf16)],
        compiler_params=_cp(("arbitrary",)),
    )(hflat, norm_w, wbig)


NU = NB * H
_REV = lambda c: NCH - 1 - c
_FWD = lambda c: c


def _tok_spec(w, ix, col=0):
    return pl.BlockSpec((NB, C, w), lambda c: (0, ix(c), col))


def _state_spec(ix):
    return pl.BlockSpec((NB, 1, H, DK, DK), lambda c: (0, ix(c), 0, 0, 0))


def _row_spec(w):
    return pl.BlockSpec((1, w), lambda c: (0, 0))


def _rows(ref):
    return tuple(ref[b] for b in range(NB))


def _hg_extra_specs(ix):
    return [_row_spec(HD), _row_spec(HD)]


def _gd_extra_specs(ix):
    return [_tok_spec(DK, ix, COL_AB // DK), _row_spec(DK), _row_spec(DK)]


def _mix_fwd(proj3, cv, l0, l1, alog, dtb):
    def body(p_ref, c_ref, ab_ref, l0_ref, l1_ref, al_ref, db_ref, oh_ref, sh_ref, og_ref, sg_ref, t_ref, sth, stg):
        @pl.when(pl.program_id(0) == 0)
        def _():
            sth[...] = jnp.zeros_like(sth)
            stg[...] = jnp.zeros_like(stg)

        Sh = tuple(sth[n] for n in range(NU))
        Sg = tuple(stg[n] for n in range(NU))
        for n in range(NU):
            sh_ref[n // H, 0, n % H] = Sh[n]
            sg_ref[n // H, 0, n % H] = Sg[n]
        snh, oh, sng, og, tp = mix_chunk(Sh, _rows(p_ref), l0_ref[...], l1_ref[...],
                                         Sg, _rows(c_ref), _rows(ab_ref), al_ref[...], db_ref[...])
        for n in range(NU):
            sth[n] = snh[n]
            stg[n] = sng[n]
        for b in range(NB):
            oh_ref[b] = oh[b]
            og_ref[b] = og[b]
            t_ref[b] = tp[b]

    tok = jax.ShapeDtypeStruct((NB, TP, HD), f32)
    st = jax.ShapeDtypeStruct((NB, NCH, H, DK, DK), f32)
    return pl.pallas_call(
        body, name="mix_fwd", grid=(NCH,),
        in_specs=[_tok_spec(3 * HD, _FWD), _tok_spec(3 * HD, _FWD), _tok_spec(DK, _FWD, COL_AB // DK),
                  _row_spec(HD), _row_spec(HD), _row_spec(DK), _row_spec(DK)],
        out_specs=[_tok_spec(HD, _FWD), _state_spec(_FWD), _tok_spec(HD, _FWD), _state_spec(_FWD), _tok_spec(HD, _FWD)],
        out_shape=[tok, st, tok, st, tok],
        scratch_shapes=[pltpu.VMEM((NU, DK, DK), f32), pltpu.VMEM((NU, DK, DK), f32)],
        compiler_params=_cp(("arbitrary",)),
    )(proj3, cv, proj3, l0, l1, alog, dtb)


def _hg_bwd(proj3, l0, l1, s_saved, do):
    def body(p_ref, l0_ref, l1_ref, s_ref, do_ref, dp_ref, dl0_ref, dl1_ref, dst):
        @pl.when(pl.program_id(0) == 0)
        def _():
            dst[...] = jnp.zeros_like(dst)
            dl0_ref[...] = jnp.zeros_like(dl0_ref)
            dl1_ref[...] = jnp.zeros_like(dl1_ref)

        S = tuple(s_ref[n // H, 0, n % H] for n in range(NU))
        _, vjp = jax.vjp(hg_chunk, S, _rows(p_ref), l0_ref[...], l1_ref[...])
        dS, dp, dl0, dl1 = vjp((tuple(dst[n] for n in range(NU)), _rows(do_ref)))
        for n in range(NU):
            dst[n] = dS[n]
        for b in range(NB):
            dp_ref[b] = dp[b].astype(bf16)
        dl0_ref[...] += jnp.broadcast_to(dl0, (8, HD))
        dl1_ref[...] += jnp.broadcast_to(dl1, (8, HD))

    acc = pl.BlockSpec((8, HD), lambda c: (0, 0))
    return pl.pallas_call(
        body, name="hg_bwd", grid=(NCH,),
        in_specs=[_tok_spec(3 * HD, _REV)] + _hg_extra_specs(_REV) + [_state_spec(_REV), _tok_spec(HD, _REV)],
        out_specs=[_tok_spec(3 * HD, _REV), acc, acc],
        out_shape=[jax.ShapeDtypeStruct((NB, TP, 3 * HD), bf16), jax.ShapeDtypeStruct((8, HD), f32),
                   jax.ShapeDtypeStruct((8, HD), f32)],
        scratch_shapes=[pltpu.VMEM((NU, DK, DK), f32)],
        compiler_params=_cp(("arbitrary",)),
    )(proj3, l0, l1, s_saved, do)


def _gd_bwd(cv, proj3, alog, dtb, s_saved, t_saved, do):
    def body(c_ref, ab_ref, al_ref, db_ref, s_ref, t_ref, do_ref, dc_ref, dab_ref, dal_ref, ddb_ref, dst):
        @pl.when(pl.program_id(0) == 0)
        def _():
            dst[...] = jnp.zeros_like(dst)
            dal_ref[...] = jnp.zeros_like(dal_ref)
            ddb_ref[...] = jnp.zeros_like(ddb_ref)

        S = tuple(s_ref[n // H, 0, n % H] for n in range(NU))
        t_rows = _rows(t_ref)
        fn = lambda *a: gd_chunk(*a, t_saved=t_rows)[0:2]
        _, vjp = jax.vjp(fn, S, _rows(c_ref), _rows(ab_ref), al_ref[...], db_ref[...])
        dS, dc, dab, dal, ddb = vjp((tuple(dst[n] for n in range(NU)), _rows(do_ref)))
        for n in range(NU):
            dst[n] = dS[n]
        for b in range(NB):
            dc_ref[b] = dc[b]
            dab_ref[b] = dab[b].astype(bf16)
        dal_ref[...] += jnp.broadcast_to(dal, (8, DK))
        ddb_ref[...] += jnp.broadcast_to(ddb, (8, DK))

    acc = pl.BlockSpec((8, DK), lambda c: (0, 0))
    return pl.pallas_call(
        body, name="gd_bwd", grid=(NCH,),
        in_specs=[_tok_spec(3 * HD, _REV)] + _gd_extra_specs(_REV)
        + [_state_spec(_REV), _tok_spec(HD, _REV), _tok_spec(HD, _REV)],
        out_specs=[_tok_spec(3 * HD, _REV), _tok_spec(DK, _REV), acc, acc],
        out_shape=[jax.ShapeDtypeStruct((NB, TP, 3 * HD), f32), jax.ShapeDtypeStruct((NB, TP, DK), bf16),
                   jax.ShapeDtypeStruct((8, DK), f32), jax.ShapeDtypeStruct((8, DK), f32)],
        scratch_shapes=[pltpu.VMEM((NU, DK, DK), f32)],
        compiler_params=_cp(("arbitrary",)),
    )(cv, proj3, alog, dtb, s_saved, t_saved, do)


def _conv_fwd(proj3, conv4):
    def body(x_ref, w_ref, y_ref):
        x = x_ref[0]
        y = w_ref[3] * x
        for s in (1, 2, 3):
            y = y + w_ref[3 - s] * pltpu.roll(x, s, 0)
        y_ref[0] = y
        y_ref[0, 0:8, :] = jnp.zeros((8, HD), f32)

    return pl.pallas_call(
        body, name="conv_fwd", grid=(NB, 3),
        in_specs=[pl.BlockSpec((1, TP, HD), lambda b, j: (b, 0, COL_QKV // HD + j)),
                  pl.BlockSpec((4, 1, HD), lambda b, j: (0, 0, j))],
        out_specs=pl.BlockSpec((1, TP, HD), lambda b, j: (b, 0, j)),
        out_shape=jax.ShapeDtypeStruct((NB, TP, 3 * HD), f32),
        compiler_params=_cp(("arbitrary", "arbitrary")),
    )(proj3, conv4)


def _conv_bwd(proj3, conv4, dy):
    def body(x_ref, w_ref, dy_ref, dx_ref, dw_ref):
        @pl.when(pl.program_id(1) == 0)
        def _():
            dw_ref[...] = jnp.zeros_like(dw_ref)

        x = x_ref[0]
        row = lax.broadcasted_iota(jnp.int32, (TP, 1), 0)
        g = jnp.where(row >= 8, dy_ref[0], 0.0)
        dx = w_ref[3] * g
        dw_ref[3] += jnp.broadcast_to(jnp.sum(x * g, axis=0, keepdims=True), (8, HD))
        for s in (1, 2, 3):
            dx = dx + w_ref[3 - s] * pltpu.roll(g, TP - s, 0)
            dw_ref[3 - s] += jnp.broadcast_to(jnp.sum(pltpu.roll(x, s, 0) * g, axis=0, keepdims=True), (8, HD))
        dx_ref[0] = dx.astype(bf16)

    return pl.pallas_call(
        body, name="conv_bwd", grid=(3, NB),
        in_specs=[pl.BlockSpec((1, TP, HD), lambda j, b: (b, 0, COL_QKV // HD + j)),
                  pl.BlockSpec((4, 1, HD), lambda j, b: (0, 0, j)), pl.BlockSpec((1, TP, HD), lambda j, b: (b, 0, j))],
        out_specs=[pl.BlockSpec((1, TP, HD), lambda j, b: (b, 0, j)), pl.BlockSpec((4, 8, HD), lambda j, b: (0, 0, j))],
        out_shape=[jax.ShapeDtypeStruct((NB, TP, 3 * HD), bf16), jax.ShapeDtypeStruct((4, 8, 3 * HD), f32)],
        compiler_params=_cp(("arbitrary", "arbitrary")),
    )(proj3, conv4, dy)


def _out_loss(o_hg, o_gd, proj, hgw, gdw, wout, hflat, fw, target):
    tm = 384

    def body(ohg_ref, ogd_ref, zhg_ref, zgd_ref, hgw_ref, gdw_ref, wo_ref, h_ref, fw_ref, tg_ref,
             loss_ref, dohg_ref, dogd_ref, dzhg_ref, dzgd_ref, dh_ref, dwo_ref, dhgw_ref, dgdw_ref, dfw_ref):
        i = pl.program_id(0)

        @pl.when(i == 0)
        def _():
            for r in (loss_ref, dwo_ref, dhgw_ref, dgdw_ref, dfw_ref):
                r[...] = jnp.zeros_like(r)

        row = i * tm + lax.broadcasted_iota(jnp.int32, (tm, 1), 0)
        tok = jnp.where(row >= TP, row - TP, row)
        valid = (tok >= PAD + N_META).astype(f32)
        hval = h_ref[...]
        tgt = tg_ref[...]

        mixers = ((ohg_ref, zhg_ref, hgw_ref[...]), (ogd_ref, zgd_ref, gdw_ref[...]))
        saved, ys = [], []
        for o_ref, z_ref, w in mixers:
            for hh in range(H):
                sl = slice(hh * DK, (hh + 1) * DK)
                o, z = o_ref[:, sl], z_ref[:, sl]
                r = lax.rsqrt(jnp.mean(o * o, axis=-1, keepdims=True) + EPS)
                n = o * r
                sg = sigmoid(z)
                ws = w * (z * sg)
                saved.append((r, n, sg, z, ws, w))
                ys.append(n * ws)
        y = jnp.concatenate(ys, axis=-1)
        h2 = hval + bdot(y, wo_ref[...])
        r2 = lax.rsqrt(jnp.mean(h2 * h2, axis=-1, keepdims=True) + EPS)
        n2 = h2 * r2
        fwv = fw_ref[...]
        err = (n2 * fwv - tgt) * valid
        loss = (0.5 / D) * jnp.sum(err * err)
        dyf = err * (1.0 / D)
        dn2 = dyf * fwv
        dout = r2 * (dn2 - n2 * jnp.mean(dn2 * n2, axis=-1, keepdims=True))
        dh_ref[...] = dout
        dy = bdot_nt(dout, wo_ref[...])
        dwo_ref[...] += bdot_tn(y, dout)
        dws = []
        for mi, (do_ref, dz_ref) in enumerate(((dohg_ref, dzhg_ref), (dogd_ref, dzgd_ref))):
            dw = jnp.zeros((1, DK), f32)
            for hh in range(H):
                sl = slice(hh * DK, (hh + 1) * DK)
                r, n, sg, z, ws, w = saved[mi * H + hh]
                dyh = dy[:, mi * HD + hh * DK:mi * HD + (hh + 1) * DK]
                t = dyh * n
                dw = dw + jnp.sum(t * (z * sg), axis=0, keepdims=True)
                dz_ref[:, sl] = (t * w * (sg * (1.0 + z * (1.0 - sg)))).astype(bf16)
                dn = dyh * ws
                do_ref[:, sl] = r * (dn - n * jnp.mean(dn * n, axis=-1, keepdims=True))
            dws.append(dw)
        loss_ref[...] += jnp.broadcast_to(loss, (8, DK))
        dhgw_ref[...] += jnp.broadcast_to(dws[0], (8, DK))
        dgdw_ref[...] += jnp.broadcast_to(dws[1], (8, DK))
        dfw_ref[...] += jnp.broadcast_to(jnp.sum(dyf * n2, axis=0, keepdims=True), (8, D))

    row = lambda w: pl.BlockSpec((tm, w), lambda i: (i, 0))
    whole = lambda r, w: pl.BlockSpec((r, w), lambda i: (0, 0))
    col = lambda c0: pl.BlockSpec((tm, HD), lambda i: (i, c0 // HD))
    return pl.pallas_call(
        body, name="out_loss", grid=(N // tm,),
        in_specs=[row(HD), row(HD), col(COL_ZHG), col(COL_ZGD),
                  whole(1, DK), whole(1, DK), whole(D, D), row(D), whole(1, D), row(D)],
        out_specs=[whole(8, DK), row(HD), row(HD), row(HD), row(HD), row(D), whole(D, D),
                   whole(8, DK), whole(8, DK), whole(8, D)],
        out_shape=[jax.ShapeDtypeStruct((8, DK), f32)] + [jax.ShapeDtypeStruct((N, HD), f32)] * 2
        + [jax.ShapeDtypeStruct((N, HD), bf16)] * 2
        + [jax.ShapeDtypeStruct((N, D), f32), jax.ShapeDtypeStruct((D, D), f32),
           jax.ShapeDtypeStruct((8, DK), f32), jax.ShapeDtypeStruct((8, DK), f32), jax.ShapeDtypeStruct((8, D), f32)],
        compiler_params=_cp(("arbitrary",)),
    )(o_hg, o_gd, proj, proj, hgw, gdw, wout, hflat, fw, target)


def _in_bwd(pieces, wbig, hflat, norm_w, dh_res, pbs):
    tm = 384
    nsteps = N // tm
    np_ = len(pieces)
    na = len(pbs)
    offs = [c0 for _, c0 in pieces]
    widths = [d.shape[1] for d, _ in pieces]

    def body(*refs):
        d_refs = refs[:np_]
        w_ref, h_ref, nw_ref, dhr_ref = refs[np_:np_ + 4]
        srcs = refs[np_ + 4:np_ + 4 + na]
        dh_ref, dnw_ref = refs[np_ + 4 + na:np_ + 6 + na]
        dsts = refs[np_ + 6 + na:np_ + 6 + 2 * na]
        sems = refs[np_ + 6 + 2 * na:]
        i = pl.program_id(0)

        def copies():
            if not na:
                return []
            x, y, c, chips = _place()
            return [pltpu.make_async_remote_copy(
                src_ref=srcs[a].at[2 * px + py], dst_ref=dsts[a].at[j], send_sem=sems[0].at[na * j + a],
                recv_sem=sems[1].at[na * j + a], device_id=(px, py, c), device_id_type=MESH)
                for j, (px, py) in enumerate(chips) for a in range(na)]

        @pl.when(i == 0)
        def _():
            dnw_ref[...] = jnp.zeros_like(dnw_ref)
            for cp in copies():
                cp.start()

        du = jnp.zeros((tm, D), f32)
        for d_ref, off, wd in zip(d_refs, offs, widths):
            du = du + bdot_nt(d_ref[...], w_ref[:, off:off + wd])
        _, vjp = jax.vjp(rmsnorm, h_ref[...], nw_ref[...])
        dh, dnw = vjp(du)
        dh_ref[...] = dh + dhr_ref[...]
        dnw_ref[...] += jnp.broadcast_to(dnw, (8, D))

        @pl.when(i == nsteps - 1)
        def _():
            for cp in copies():
                cp.wait()

    row = lambda w: pl.BlockSpec((tm, w), lambda i: (i, 0))
    return pl.pallas_call(
        body, name="in_bwd", grid=(nsteps,),
        in_specs=[row(w) for w in widths]
        + [pl.BlockSpec((D, PC), lambda i: (0, 0)), row(D), pl.BlockSpec((1, D), lambda i: (0, 0)), row(D)] + [ANY] * na,
        out_specs=[row(D), pl.BlockSpec((8, D), lambda i: (0, 0))] + [ANY] * na,
        out_shape=[jax.ShapeDtypeStruct((N, D), f32), jax.ShapeDtypeStruct((8, D), f32)]
        + [jax.ShapeDtypeStruct((3,) + p.shape[1:], p.dtype) for p in pbs],
        scratch_shapes=[pltpu.SemaphoreType.DMA((3 * na,)), pltpu.SemaphoreType.DMA((3 * na,))] if na else [],
        compiler_params=_cp(("arbitrary",)),
    )(*[d for d, _ in pieces], wbig, hflat, norm_w, dh_res, *pbs)


def _w_grad(ut, pieces):
    tk = 384
    offs = [c0 for _, c0 in pieces]
    widths = [d.shape[1] for d, _ in pieces]

    def body(u_ref, *refs):
        d_refs, o_ref = refs[:-1], refs[-1]

        @pl.when(pl.program_id(0) == 0)
        def _():
            o_ref[...] = jnp.zeros_like(o_ref)

        u = u_ref[...]
        for d_ref, off, wd in zip(d_refs, offs, widths):
            o_ref[:, off:off + wd] += jnp.dot(u, d_ref[...], preferred_element_type=f32)

    return pl.pallas_call(
        body, name="w_grad", grid=(N // tk,),
        in_specs=[pl.BlockSpec((D, tk), lambda k: (0, k))] + [pl.BlockSpec((tk, w), lambda k: (k, 0)) for w in widths],
        out_specs=pl.BlockSpec((D, PC), lambda k: (0, 0)),
        out_shape=jax.ShapeDtypeStruct((D, PC), f32),
        compiler_params=_cp(("arbitrary",)),
    )(ut, *[d for d, _ in pieces])


def _adam_math(g, w, m, v):
    m2 = ADAM_B1 * m + (1.0 - ADAM_B1) * g
    v2 = ADAM_B2 * v + (1.0 - ADAM_B2) * (g * g)
    m_hat = m2 / (1.0 - ADAM_B1 ** ADAM_STEP)
    v_hat = v2 / (1.0 - ADAM_B2 ** ADAM_STEP)
    delta = -ADAM_LR * (m_hat / (jnp.sqrt(v_hat) + ADAM_EPS) + ADAM_WD * w)
    return delta, m2, v2


def _adamw(gs, w, m, v, name):
    R, Cc = w.shape
    tr = 256 if R % 256 == 0 else R
    ng = len(gs)

    def body(*refs):
        g = refs[0][...]
        for r in refs[1:ng]:
            g = g + r[...]
        w_ref, m_ref, v_ref, g_ref, d_ref, m2_ref, v2_ref = refs[ng:]
        delta, m2, v2 = _adam_math(g, w_ref[...], m_ref[...], v_ref[...])
        g_ref[...] = g
        d_ref[...] = delta
        m2_ref[...] = m2
        v2_ref[...] = v2

    spec = pl.BlockSpec((tr, Cc), lambda i: (i, 0))
    return pl.pallas_call(
        body, name=name, grid=(R // tr,),
        in_specs=[spec] * (ng + 3), out_specs=[spec] * 4,
        out_shape=[jax.ShapeDtypeStruct((R, Cc), f32)] * 4,
        compiler_params=_cp(("arbitrary",)),
    )(*gs, w, m, v)


def _adamw_rows(g, w, m, v, name):
    R, _, Cc = w.shape
    tr = R // 9

    def body(g_ref, w_ref, m_ref, v_ref, go_ref, d_ref, m2_ref, v2_ref):
        g = g_ref[...]
        delta, m2, v2 = _adam_math(g, w_ref[...], m_ref[...], v_ref[...])
        go_ref[...] = g
        d_ref[...] = delta
        m2_ref[...] = m2
        v2_ref[...] = v2

    spec = pl.BlockSpec((tr, 1, Cc), lambda i: (i, 0, 0))
    return pl.pallas_call(
        body, name=name, grid=(R // tr,),
        in_specs=[spec] * 4, out_specs=[spec] * 4,
        out_shape=[jax.ShapeDtypeStruct((R, 1, Cc), f32)] * 4,
        compiler_params=_cp(("arbitrary",)),
    )(g, w, m, v)


def _place():
    x, y, c = lax.axis_index("x"), lax.axis_index("y"), lax.axis_index("c")
    return x, y, c, [(1 - x, y), (x, 1 - y), (1 - x, 1 - y)]


def _gather_weights(halved, whole):
    nh, nw = len(halved), len(whole)
    na = nh + nw

    def body(*refs):
        srcs, dsts = refs[:na], refs[na:2 * na]
        send_sems, recv_sems, loc_sems = refs[2 * na:2 * na + 3]
        stage = refs[2 * na + 3:]
        x, y, c, chips = _place()
        me = 2 * x + y
        loads = [pltpu.make_async_copy(s, v, loc_sems.at[i]) for i, (s, v) in enumerate(zip(srcs, stage))]
        locs = [pltpu.make_async_copy(v, d.at[me], loc_sems.at[i]) for i, (v, d) in enumerate(zip(stage, dsts))]
        for cp in loads:
            cp.start()

        def ici(j, i, slot):
            px, py = chips[j]
            src = srcs[i].at[c] if i < nh else srcs[i]
            dst = dsts[i].at[slot, c] if i < nh else dsts[i].at[slot]
            return pltpu.make_async_remote_copy(
                src_ref=src, dst_ref=dst, send_sem=send_sems.at[na * j + i], recv_sem=recv_sems.at[na * j + i],
                device_id=(px, py, c), device_id_type=MESH)

        def d2d(j, i, half):
            px, py = chips[j]
            blk = dsts[i].at[2 * px + py, half]
            return pltpu.make_async_remote_copy(
                src_ref=blk, dst_ref=blk, send_sem=send_sems.at[3 * na + nh * j + i],
                recv_sem=recv_sems.at[3 * na + nh * j + i], device_id=(x, y, 1 - c), device_id_type=MESH)

        sends = [ici(j, i, me) for j in range(3) for i in range(na)]
        for cp in sends:
            cp.start()
        for ld, st in zip(loads, locs):
            ld.wait()
            st.start()
        for j, (px, py) in enumerate(chips):
            for i in range(na):
                ici(j, i, 2 * px + py).wait_recv()
                if i < nh:
                    fwd = d2d(j, i, c)
                    fwd.start()
                    sends.append(fwd)
        for j in range(3):
            for i in range(nh):
                d2d(j, i, 1 - c).wait_recv()
        for cp in sends:
            cp.wait_send()
        for cp in locs:
            cp.wait()

    nsem = 3 * na + 3 * nh
    return pl.pallas_call(
        body, name="gather_weights",
        in_specs=[ANY] * na, out_specs=[ANY] * na,
        out_shape=[jax.ShapeDtypeStruct((4,) + s.shape, s.dtype) for s in list(halved) + list(whole)],
        scratch_shapes=[pltpu.SemaphoreType.DMA((nsem,)), pltpu.SemaphoreType.DMA((nsem,)),
                        pltpu.SemaphoreType.DMA((na,))] + [pltpu.VMEM(s.shape, s.dtype) for s in list(halved) + list(whole)],
        compiler_params=pltpu.CompilerParams(has_side_effects=True, vmem_limit_bytes=VMEM_LIMIT),
    )(*halved, *whole)


def _swap_halves(gs):
    na = len(gs)
    jobs = [(i, q) for i in range(na) for q in range(gs[i].shape[0])]

    def body(*refs):
        srcs, dsts = refs[:na], refs[na:2 * na]
        send_sems, recv_sems = refs[2 * na:]
        x, y, c, _ = _place()
        cps = [pltpu.make_async_remote_copy(
            src_ref=srcs[i].at[q, 1 - c], dst_ref=dsts[i].at[q], send_sem=send_sems.at[k],
            recv_sem=recv_sems.at[k], device_id=(x, y, 1 - c), device_id_type=MESH)
            for k, (i, q) in enumerate(jobs)]
        for cp in cps:
            cp.start()
        for cp in cps:
            cp.wait()

    return pl.pallas_call(
        body, name="swap_halves",
        in_specs=[ANY] * na, out_specs=[ANY] * na,
        out_shape=[jax.ShapeDtypeStruct(g.shape[0:1] + g.shape[2:], g.dtype) for g in gs],
        scratch_shapes=[pltpu.SemaphoreType.DMA((len(jobs),)), pltpu.SemaphoreType.DMA((len(jobs),))],
        compiler_params=pltpu.CompilerParams(has_side_effects=True),
    )(*gs)


def _add_halves(c_arr, g, s, name):
    Q, _, R, Cc = g.shape
    tr = min(R, 128)

    def body(c_ref, g_ref, s_ref, b_ref, f_ref):
        p = g_ref[0, 0] + s_ref[0]
        f_ref[0] = p
        b_ref[0] = p.astype(bf16)

    blk = pl.BlockSpec((1, tr, Cc), lambda q, i, cr: (q, i, 0))
    return pl.pallas_call(
        body, name=name,
        grid_spec=pltpu.PrefetchScalarGridSpec(
            num_scalar_prefetch=1, grid=(Q, R // tr),
            in_specs=[pl.BlockSpec((1, 1, tr, Cc), lambda q, i, cr: (q, cr[0], i, 0)), blk], out_specs=[blk, blk]),
        out_shape=[jax.ShapeDtypeStruct((Q, R, Cc), bf16), jax.ShapeDtypeStruct((Q, R, Cc), f32)],
        compiler_params=_cp(("arbitrary", "arbitrary")),
    )(c_arr, g, s)


_FLIPS = [(fx, fy, fc) for fx in (0, 1) for fy in (0, 1) for fc in (0, 1)][1:]


def _sum_blocks(own, r, name):
    R, Cc = own.shape
    tr = min(R, 256)

    def body(own_ref, r_ref, o_ref):
        acc = own_ref[...]
        for j in range(3):
            acc = acc + r_ref[j].astype(f32)
        o_ref[...] = acc

    return pl.pallas_call(
        body, name=name, grid=(R // tr,),
        in_specs=[pl.BlockSpec((tr, Cc), lambda i: (i, 0)), pl.BlockSpec((3, tr, Cc), lambda i: (0, i, 0))],
        out_specs=pl.BlockSpec((tr, Cc), lambda i: (i, 0)),
        out_shape=jax.ShapeDtypeStruct((R, Cc), f32),
        compiler_params=_cp(("arbitrary",)),
    )(own, r)


def _sum_packs(me8_arr, pack, rp):
    R = pack.shape[0]

    def body(me_ref, pk_ref, rp_ref, o_ref):
        me8 = me_ref[0]
        acc = None
        for d in range(8):
            rel = d ^ me8
            term = jnp.where(rel == 0, pk_ref[...], rp_ref[jnp.maximum(rel - 1, 0)])
            acc = term if acc is None else acc + term
        o_ref[...] = acc

    return pl.pallas_call(
        body, name="sum_packs",
        grid_spec=pltpu.PrefetchScalarGridSpec(
            num_scalar_prefetch=1, grid=(1,),
            in_specs=[pl.BlockSpec((R, 128), lambda i, mr: (0, 0)), pl.BlockSpec((7, R, 128), lambda i, mr: (0, 0, 0))],
            out_specs=pl.BlockSpec((R, 128), lambda i, mr: (0, 0))),
        out_shape=jax.ShapeDtypeStruct((R, 128), f32),
        compiler_params=_cp(("arbitrary",)),
    )(me8_arr, pack, rp)


def _swap_finished(fs, pack):
    na = len(fs)
    R = pack.shape[0]

    def body(*refs):
        srcs, pk = refs[:na], refs[na]
        dsts, rp = refs[na + 1:2 * na + 1], refs[2 * na + 1]
        send_sems, recv_sems = refs[2 * na + 2:]
        x, y, c, _ = _place()
        cps = [pltpu.make_async_remote_copy(
            src_ref=srcs[i], dst_ref=dsts[i], send_sem=send_sems.at[i], recv_sem=recv_sems.at[i],
            device_id=(x, y, 1 - c), device_id_type=MESH) for i in range(na)]
        cps += [pltpu.make_async_remote_copy(
            src_ref=pk, dst_ref=rp.at[k], send_sem=send_sems.at[na + k], recv_sem=recv_sems.at[na + k],
            device_id=(x ^ fx, y ^ fy, c ^ fc), device_id_type=MESH) for k, (fx, fy, fc) in enumerate(_FLIPS)]
        for cp in cps:
            cp.start()
        for cp in cps:
            cp.wait()

    return pl.pallas_call(
        body, name="swap_finished",
        in_specs=[ANY] * (na + 1), out_specs=[ANY] * (na + 1),
        out_shape=[jax.ShapeDtypeStruct(f.shape, f.dtype) for f in fs] + [jax.ShapeDtypeStruct((7, R, 128), f32)],
        scratch_shapes=[pltpu.SemaphoreType.DMA((na + 7,)), pltpu.SemaphoreType.DMA((na + 7,))],
        compiler_params=pltpu.CompilerParams(has_side_effects=True),
    )(*fs, pack)


def _adamw_halves(c_arr, mine, peer, w, m, v, name):
    _, R, Cc = w.shape
    tr = min(R, 256)

    def body(c_ref, mine_ref, peer_ref, w_ref, m_ref, v_ref, g_ref, d_ref, m2_ref, v2_ref):
        g = jnp.where(pl.program_id(0) == c_ref[0], mine_ref[...], peer_ref[...])
        delta, m2, v2 = _adam_math(g, w_ref[0], m_ref[0], v_ref[0])
        g_ref[0] = g
        d_ref[0] = delta
        m2_ref[0] = m2
        v2_ref[0] = v2

    half = pl.BlockSpec((tr, Cc), lambda hh, i, cr: (i, 0))
    full = pl.BlockSpec((1, tr, Cc), lambda hh, i, cr: (hh, i, 0))
    return pl.pallas_call(
        body, name=name,
        grid_spec=pltpu.PrefetchScalarGridSpec(
            num_scalar_prefetch=1, grid=(2, R // tr), in_specs=[half, half, full, full, full], out_specs=[full] * 4),
        out_shape=[jax.ShapeDtypeStruct((2, R, Cc), f32)] * 4,
        compiler_params=_cp(("arbitrary", "arbitrary")),
    )(c_arr, mine, peer, w, m, v)


def _rows8(a):
    flat = a.reshape(-1)
    n = flat.shape[0]
    rows = -(-n // 1024) * 8
    return jnp.pad(flat, (0, rows * 128 - n)).reshape(rows, 128)


def kernel(x, meta_tokens, norm_w, w_in, conv_w, hg_lb_logits, hg_norm_w, gdn_A_log, gdn_dt_bias, gdn_norm_w, w_out, final_norm_w, loss_target, m_meta_tokens, m_norm_w, m_w_in, m_conv_w, m_hg_lb_logits, m_hg_norm_w, m_gdn_A_log, m_gdn_dt_bias, m_gdn_norm_w, m_w_out, m_final_norm_w, v_meta_tokens, v_norm_w, v_w_in, v_conv_w, v_hg_lb_logits, v_hg_norm_w, v_gdn_A_log, v_gdn_dt_bias, v_gdn_norm_w, v_w_out, v_final_norm_w):
    me = 2 * lax.axis_index("x") + lax.axis_index("y")

    g_win, g_wout, g_conv, g_meta = _gather_weights(
        [w_in[0].astype(bf16).reshape(2, D // 2, SHARD_COLS), w_out[0].astype(bf16).reshape(2, D // 8, D)],
        [conv_w[0], meta_tokens])
    w_full = jnp.transpose(g_win.reshape(4, D, SHARD_COLS), (1, 0, 2)).reshape(D, IN_COLS)
    wbig = jnp.pad(w_full, ((0, 0), (0, PC - IN_COLS)))
    wout_full = g_wout.reshape(D, D)
    conv4 = jnp.transpose(g_conv, (1, 0, 2)).reshape(4, 1, 3 * HD)
    meta_full = jnp.transpose(g_meta, (1, 0, 2)).reshape(N_META, D)

    c_arr = lax.axis_index("c").reshape(1).astype(jnp.int32)

    def chip_partials(gw, g_wout_part):
        g_in2 = gw.reshape(1, 2, D // 2, PC)
        g_out4 = g_wout_part.reshape(4, 2, D // 8, D)
        s_in, s_out = _swap_halves([g_in2, g_out4])
        pb_in, pf_in = _add_halves(c_arr, g_in2, s_in, "add_w_in")
        pb_out, pf_out = _add_halves(c_arr, g_out4, s_out, "add_w_out")
        pb_blocks = jnp.transpose(pb_in[0, :, 0:IN_COLS].reshape(D // 2, 4, SHARD_COLS), (1, 0, 2))
        own_in = lax.dynamic_slice(pf_in[0], (0, me * SHARD_COLS), (D // 2, SHARD_COLS))
        own_out = lax.dynamic_index_in_dim(pf_out, me, axis=0, keepdims=False)
        return [pb_blocks, pb_out], [own_in, own_out]

    (loss8, grad_x, d_meta, d_nw, d_conv, d_lb, d_hgw, d_alog, d_dtb, d_gdw, d_fw, pfs, rs) = _local_step(
        x, loss_target, wbig, wout_full, conv4, meta_full, norm_w, hg_lb_logits, hg_norm_w, gdn_A_log, gdn_dt_bias,
        gdn_norm_w, final_norm_w, chip_partials)

    pack = jnp.concatenate([
        loss8, d_nw[0].reshape(8, 128), d_lb.reshape(8, 128), d_hgw, _rows8(d_alog[0, :H]), _rows8(d_dtb[0, :H]),
        d_gdw, d_fw[0].reshape(8, 128), d_meta.reshape(128, 128), d_conv.reshape(48, 128)], axis=0)
    return _reduce_and_update(
        me, c_arr, grad_x, pfs, rs, pack, meta_tokens, norm_w, w_in, conv_w, hg_lb_logits, hg_norm_w, gdn_A_log,
        gdn_dt_bias, gdn_norm_w, w_out, final_norm_w, m_meta_tokens, m_norm_w, m_w_in, m_conv_w, m_hg_lb_logits,
        m_hg_norm_w, m_gdn_A_log, m_gdn_dt_bias, m_gdn_norm_w, m_w_out, m_final_norm_w, v_meta_tokens, v_norm_w, v_w_in,
        v_conv_w, v_hg_lb_logits, v_hg_norm_w, v_gdn_A_log, v_gdn_dt_bias, v_gdn_norm_w, v_w_out, v_final_norm_w)


def _local_step(x, loss_target, wbig, wout_full, conv4, meta_full, norm_w, hg_lb_logits, hg_norm_w, gdn_A_log, gdn_dt_bias,
                gdn_norm_w, final_norm_w, chip_partials):
    h3 = jnp.concatenate([jnp.zeros((NB, PAD, D), f32), jnp.broadcast_to(meta_full[None], (NB, N_META, D)), x], axis=1)
    hflat = h3.reshape(N, D)
    target = jnp.pad(loss_target, ((0, 0), (PAD + N_META, 0), (0, 0))).reshape(N, D)
    l0, l1 = hg_lb_logits[0:1], hg_lb_logits[1:2]
    alog = jnp.pad(gdn_A_log, ((0, 0), (0, DK - H)))
    dtb = jnp.pad(gdn_dt_bias, ((0, 0), (0, DK - H)))
    fw = final_norm_w.reshape(1, D)

    proj, ut = _in_proj(hflat, norm_w, wbig)
    proj3 = proj.reshape(NB, TP, PC)
    cv = _conv_fwd(proj3, conv4)
    o_hg, s_hg, o_gd, s_gd, t_gd = _mix_fwd(proj3, cv, l0, l1, alog, dtb)
    (loss8, d_ohg, d_ogd, d_zhg, d_zgd, dh_res, g_wout_part, d_hgw, d_gdw, d_fw) = _out_loss(
        o_hg.reshape(N, HD), o_gd.reshape(N, HD), proj, hg_norm_w, gdn_norm_w, wout_full, hflat, fw, target)
    d_hg, d_l0, d_l1 = _hg_bwd(proj3, l0, l1, s_hg, d_ohg.reshape(NB, TP, HD))
    d_cv, d_ab, d_alog, d_dtb = _gd_bwd(cv, proj3, alog, dtb, s_gd, t_gd, d_ogd.reshape(NB, TP, HD))
    d_qkv, d_conv4 = _conv_bwd(proj3, conv4, d_cv)
    d_hg2, d_qkv2, d_ab2 = d_hg.reshape(N, 3 * HD), d_qkv.reshape(N, 3 * HD), d_ab.reshape(N, DK)
    pieces = [(d_hg2, COL_HG), (d_zhg, COL_ZHG), (d_qkv2, COL_QKV), (d_zgd, COL_ZGD), (d_ab2, COL_AB)]
    gw = _w_grad(ut, pieces)
    pbs, pfs = chip_partials(gw, g_wout_part) if chip_partials else ([], [gw, g_wout_part])
    dh, d_nw, *rs = _in_bwd(pieces, wbig, hflat, norm_w, dh_res, pbs)

    dh3 = dh.reshape(NB, TP, D)
    grad_x = dh3[:, PAD + N_META:, :]
    d_meta = jnp.sum(dh3[:, PAD:PAD + N_META, :], axis=0)
    d_conv = d_conv4[:, 0, :]
    d_lb = jnp.concatenate([d_l0[0:1], d_l1[0:1]], axis=0)
    return loss8, grad_x, d_meta, d_nw, d_conv, d_lb, d_hgw, d_alog, d_dtb, d_gdw, d_fw, pfs, rs


def _reduce_and_update(me, c_arr, grad_x, pfs, rs, pack, meta_tokens, norm_w, w_in, conv_w, hg_lb_logits, hg_norm_w,
                       gdn_A_log, gdn_dt_bias, gdn_norm_w, w_out, final_norm_w, m_meta_tokens, m_norm_w, m_w_in, m_conv_w,
                       m_hg_lb_logits, m_hg_norm_w, m_gdn_A_log, m_gdn_dt_bias, m_gdn_norm_w, m_w_out, m_final_norm_w,
                       v_meta_tokens, v_norm_w, v_w_in, v_conv_w, v_hg_lb_logits, v_hg_norm_w, v_gdn_A_log, v_gdn_dt_bias,
                       v_gdn_norm_w, v_w_out, v_final_norm_w):
    (own_in, own_out), (r_in, r_out) = pfs, rs
    f_in = _sum_blocks(own_in, r_in, "sum_w_in")
    f_out = _sum_blocks(own_out, r_out, "sum_w_out")
    o_in, o_out, r_pack = _swap_finished([f_in, f_out], pack)
    me8_arr = (2 * me + lax.axis_index("c")).reshape(1).astype(jnp.int32)
    small = _sum_packs(me8_arr, pack, r_pack)

    half_out = lambda a: a[0].reshape(2, D // 8, D)
    is0 = lax.axis_index("c") == 0
    g_in = jnp.concatenate([jnp.where(is0, f_in, o_in), jnp.where(is0, o_in, f_in)], axis=0)
    to_cm = lambda a: jnp.transpose(a, (2, 0, 1))
    gi, di, mi, vi = [jnp.transpose(a, (1, 2, 0))[0] for a in _adamw_rows(
        g_in.T.reshape(SHARD_COLS, 1, D), to_cm(w_in), to_cm(m_w_in), to_cm(v_w_in), "adamw_w_in")]
    go, do_, mo, vo = [a.reshape(D // 4, D) for a in _adamw_halves(
        c_arr, f_out, o_out, half_out(w_out), half_out(m_w_out), half_out(v_w_out), "adamw_w_out")]

    g_meta_full = small[64:192].reshape(N_META, D)
    g_meta_loc = lax.dynamic_slice(g_meta_full, (0, me * 256), (N_META, 256))
    gm, dm, mm_, vm = _adamw([g_meta_loc], meta_tokens, m_meta_tokens, v_meta_tokens, "adamw_meta")
    g_conv_full = small[192:240].reshape(4, 1536)
    g_conv_loc = lax.dynamic_slice(g_conv_full, (0, me * 384), (4, 384))
    gc, dc, mc, vc = _adamw([g_conv_loc], conv_w[0], m_conv_w[0], v_conv_w[0], "adamw_conv")

    reps = [(norm_w, m_norm_w, v_norm_w), (hg_lb_logits, m_hg_lb_logits, v_hg_lb_logits),
            (hg_norm_w, m_hg_norm_w, v_hg_norm_w), (gdn_A_log, m_gdn_A_log, v_gdn_A_log),
            (gdn_dt_bias, m_gdn_dt_bias, v_gdn_dt_bias), (gdn_norm_w, m_gdn_norm_w, v_gdn_norm_w),
            (final_norm_w, m_final_norm_w, v_final_norm_w)]
    wp = jnp.concatenate([_rows8(t[0]) for t in reps], axis=0)
    mp = jnp.concatenate([_rows8(t[1]) for t in reps], axis=0)
    vp = jnp.concatenate([_rows8(t[2]) for t in reps], axis=0)
    gr, dr, mr, vr = _adamw([small[8:64]], wp, mp, vp, "adamw_small")

    def unpack(p):
        outs = []
        for i, t in enumerate(reps):
            n = t[0].size
            outs.append(p[8 * i:8 * i + 8].reshape(-1)[:n].reshape(t[0].shape))
        return outs

    def leaves(meta_v, conv_v, in_v, out_v, rep_p):
        nw, lb, hgw, al, db, gdw, fwv = unpack(rep_p)
        return [meta_v, nw, in_v[None], conv_v[None], lb, hgw, al, db, gdw, out_v[None], fwv]

    loss = small[0, 0]
    return (loss, grad_x, *leaves(gm, gc, gi, go, gr), *leaves(dm, dc, di, do_, dr),
            *leaves(mm_, mc, mi, mo, mr), *leaves(vm, vc, vi, vo, vr))
```

```python
import functools

import jax
import jax.numpy as jnp
from jax import lax
from jax.experimental import pallas as pl
from jax.experimental.pallas import tpu as pltpu

f32 = jnp.float32
bf16 = jnp.bfloat16
MESH = pl.DeviceIdType.MESH
ANY = pl.BlockSpec(memory_space=pl.ANY)

D = 1024
NB = 2
N_META = 16
SEQ = 2048
PAD = 48
TP = PAD + N_META + SEQ
C = 64
NCH = TP // C
N = NB * TP
H = 4
DK = 128
HD = H * DK
PC = 4224
IN_COLS = 4104
SHARD_COLS = IN_COLS // 4
COL_HG, COL_ZHG, COL_QKV, COL_ZGD, COL_AB = 0, 3 * HD, 4 * HD, 7 * HD, 8 * HD
EPS = 1e-6
ADAM_LR, ADAM_B1, ADAM_B2, ADAM_EPS, ADAM_WD, ADAM_STEP = 0.001, 0.9, 0.999, 1e-08, 0.01, 10
VMEM_LIMIT = 56 * 1024 * 1024

P_HG = dict(lvl=1, av=1, qs=1, su=1)
P_GD = dict(kk=1, inv=1, sol=1, ws=1, qk=1, o=1, su=1)


def _cp(sem=None, **kw):
    return pltpu.CompilerParams(dimension_semantics=sem, vmem_limit_bytes=VMEM_LIMIT, **kw)


_DIMS = {"nn": (((1,), (0,)), ((), ())), "nt": (((1,), (1,)), ((), ())), "tn": (((0,), (0,)), ((), ()))}


def _split(x):
    hi = x.astype(bf16)
    return hi, (x - hi.astype(f32)).astype(bf16)


def _dg(a, b, kind, passes):
    d = lambda x, y: lax.dot_general(x, y, _DIMS[kind], preferred_element_type=f32)
    if passes == 1:
        return d(a.astype(bf16), b.astype(bf16))
    ah, al = _split(a)
    bh, bl = _split(b)
    return d(ah, bh) + d(ah, bl) + d(al, bh)


@functools.partial(jax.custom_vjp, nondiff_argnums=(2, 3))
def mmx(a, b, kind, passes):
    return _dg(a, b, kind, passes)


def _mmx_fwd(a, b, kind, passes):
    return _dg(a, b, kind, passes), (a, b)


def _mmx_bwd(kind, passes, res, g):
    a, b = res
    if kind == "nn":
        return _dg(g, b, "nt", passes), _dg(a, g, "tn", passes)
    if kind == "nt":
        return _dg(g, b, "nn", passes), _dg(g, a, "tn", passes)
    return _dg(b, g, "nt", passes), _dg(a, g, "nn", passes)


mmx.defvjp(_mmx_fwd, _mmx_bwd)


def _mask_dg(mask, x):
    xh, xl = _split(x)
    return jnp.dot(jnp.concatenate([mask, mask], axis=1), jnp.concatenate([xh, xl], axis=0), preferred_element_type=f32)


@functools.partial(jax.custom_vjp, nondiff_argnums=(2,))
def mask_mm(mask, x, bwd_passes):
    return _mask_dg(mask, x)


def _mask_fwd(mask, x, bwd_passes):
    return _mask_dg(mask, x), mask


def _mask_bwd(bwd_passes, mask, g):
    d = lambda y: lax.dot_general(mask, y, _DIMS["tn"], preferred_element_type=f32)
    if bwd_passes == 1:
        return None, d(g.astype(bf16))
    gh, gl = _split(g)
    return None, d(gh) + d(gl)


mask_mm.defvjp(_mask_fwd, _mask_bwd)


def bdot(a, b):
    return jnp.dot(a.astype(bf16), b.astype(bf16), preferred_element_type=f32)


def bdot_nt(a, b):
    return lax.dot_general(a.astype(bf16), b.astype(bf16), _DIMS["nt"], preferred_element_type=f32)


def bdot_tn(a, b):
    return lax.dot_general(a.astype(bf16), b.astype(bf16), _DIMS["tn"], preferred_element_type=f32)


def _iota2(n, m):
    return lax.broadcasted_iota(jnp.int32, (n, m), 0), lax.broadcasted_iota(jnp.int32, (n, m), 1)


sigmoid = jax.nn.sigmoid


def silu(x):
    return x * sigmoid(x)


def softplus(x):
    return jnp.maximum(x, 0.0) + jnp.log(1.0 + jnp.exp(-jnp.abs(x)))


def rmsnorm(x, w):
    return x * lax.rsqrt(jnp.mean(x * x, axis=-1, keepdims=True) + EPS) * w


def hg_masks():
    t, r = _iota2(C, C)
    mats = [r <= t, r > t]
    lvl = []
    for l in range(1, 7):
        sz = 1 << l
        half = sz >> 1
        seg_t = t >> l
        upper_t = (t & (sz - 1)) >= half
        mid_t = seg_t * sz + half - 1
        mats.append((upper_t & (r > mid_t) & (r <= t)) | ((~upper_t) & (r > t) & (r <= mid_t)))
        lvl.append(((seg_t == (r >> l)) & upper_t & ((r & (sz - 1)) < half)).astype(f32))
    stk = jnp.concatenate([m.astype(bf16) for m in mats], axis=0)
    return stk, lvl, (t == r).astype(f32)


def _head(a, h):
    return a[:, h * DK:(h + 1) * DK]


def _run(*gens):
    results = [None] * len(gens)
    live = list(range(len(gens)))
    while live:
        for i in list(live):
            try:
                next(gens[i])
            except StopIteration as e:
                results[i] = e.value
                live.remove(i)
    return results


def hg_chunk(St, ps, l0, l1):
    return _run(hg_stages(St, ps, l0, l1))[0]


def gd_chunk(S, cs, abs_, alog, dtb, t_saved=None):
    return _run(gd_stages(S, cs, abs_, alog, dtb, t_saved))[0]


def mix_chunk(St, ps, l0, l1, S, cs, abs_, alog, dtb):
    (sn_h, o_h), (sn_g, o_g, t_pack) = _run(hg_stages(St, ps, l0, l1), gd_stages(S, cs, abs_, alog, dtb))
    return sn_h, o_h, sn_g, o_g, t_pack


def hg_stages(St, ps, l0, l1):
    m = jnp.maximum(l0, l1)
    e0 = jnp.exp(l0 - m)
    e1 = jnp.exp(l1 - m)
    lb = e0 / (e0 + e1)
    stk, lvl, eye = hg_masks()
    msk = [eye] + lvl
    qs, ks, vs, qG, kR, eGl = [], [], [], [], [], []
    for p in ps:
        pq, pf, v = p[:, 0:HD], p[:, HD:2 * HD], p[:, 2 * HD:3 * HD]
        q = silu(pq)
        f = lb + (1.0 - lb) * sigmoid(pf)
        k = 1.0 - f
        logf = jnp.log(f)
        Dm = mask_mm(stk, logf, 1)
        ex = [jnp.exp(Dm[(2 + i) * C:(3 + i) * C]) for i in range(6)]
        qs.append([q] + [q * e for e in ex])
        ks.append([k] + [k * e for e in ex])
        vs.append(v)
        qG.append(q * jnp.exp(Dm[0:C]))
        kR.append(k * jnp.exp(Dm[C:2 * C]))
        eGl.append(jnp.exp(jnp.sum(logf, axis=0, keepdims=True)))
    yield
    units = [(b, h) for b in range(len(ps)) for h in range(H)]
    parts = []
    for i in range(7):
        parts.append([msk[i] * mmx(_head(qs[b][i], h), _head(ks[b][i], h), "nt", P_HG["lvl"]) for b, h in units])
        yield
    A = [functools.reduce(lambda x, y: x + y, [parts[i][n] for i in range(7)]) for n in range(len(units))]
    qS = [mmx(_head(qG[b], h), St[n], "nt", P_HG["qs"]) for n, (b, h) in enumerate(units)]
    Sn = [St[n] * _head(eGl[b], h) + mmx(_head(vs[b], h), _head(kR[b], h), "tn", P_HG["su"])
          for n, (b, h) in enumerate(units)]
    yield
    outs = [mmx(A[n], _head(vs[b], h), "nn", P_HG["av"]) + qS[n] for n, (b, h) in enumerate(units)]
    return tuple(Sn), tuple(jnp.concatenate(outs[b * H:(b + 1) * H], axis=1) for b in range(len(ps)))


@jax.custom_vjp
def use_inverse(A, T):
    return T


def _use_inverse_fwd(A, T):
    return T, T


def _use_inverse_bwd(T, g):
    return -_dg(T, _dg(g, T, "nt", P_GD["inv"]), "tn", P_GD["inv"]), jnp.zeros_like(T)


use_inverse.defvjp(_use_inverse_fwd, _use_inverse_bwd)


def gd_stages(S, cs, abs_, alog, dtb, t_saved=None):
    t, r = _iota2(C, C)
    tri = (r <= t).astype(bf16)
    ups = (r > t).astype(bf16)
    lane = lax.broadcasted_iota(jnp.int32, (1, DK), 1)
    subl = lax.broadcasted_iota(jnp.int32, (8, 1), 0)
    eye = (t == r).astype(f32)
    strict = (r < t).astype(f32)
    bd = ((t >> 4) == (r >> 4)).astype(f32)
    qa, ka, va, b4, gam4, grev4, gam4T, glast4 = [], [], [], [], [], [], [], []
    for c, ab in zip(cs, abs_):
        qa.append(silu(c[:, 0:HD]))
        ka.append(silu(c[:, HD:2 * HD]))
        va.append(silu(c[:, 2 * HD:3 * HD]))
        g4 = -jnp.exp(alog) * softplus(ab + dtb)
        b4.append(sigmoid(ab))
        gam4.append(mask_mm(tri, g4, 2))
        grev4.append(mask_mm(ups, g4, 2))
        gam4T.append(gam4[-1].T)
        glast4.append(jnp.sum(g4, axis=0, keepdims=True))
    yield
    units = [(b, h) for b in range(len(cs)) for h in range(H)]
    nu = range(len(units))
    inv = lambda a, b: [mmx(a[n], b[n], "nn", P_GD["inv"]) for n in nu]
    v = [_head(va[b], h) for b, h in units]
    q = [_head(qa[b], h) for b, h in units]
    k = [_head(ka[b], h) for b, h in units]
    q = [x * lax.rsqrt(jnp.sum(x * x, -1, keepdims=True) + EPS) * (DK ** -0.5) for x in q]
    k = [x * lax.rsqrt(jnp.sum(x * x, -1, keepdims=True) + EPS) for x in k]
    oh = [(lane == h).astype(f32) for h in range(H)]
    gam_c = [jnp.sum(gam4[b] * oh[h], -1, keepdims=True) for b, h in units]
    grev_c = [jnp.sum(grev4[b] * oh[h], -1, keepdims=True) for b, h in units]
    beta = [jnp.sum(b4[b] * (lane == h + H).astype(f32), -1, keepdims=True) for b, h in units]
    glast = [jnp.sum(glast4[b] * oh[h], -1, keepdims=True) for b, h in units]
    gam_r = [jnp.sum(gam4T[b][0:8, :] * (subl == h).astype(f32), axis=0, keepdims=True) for b, h in units]
    dec = [jnp.exp(jnp.where(r <= t, gam_c[n] - gam_r[n], -1e30)) for n in nu]
    egam = [jnp.exp(gam_c[n]) for n in nu]
    kk = [mmx(k[n], k[n], "nt", P_GD["kk"]) for n in nu]
    qk = [mmx(q[n], k[n], "nt", P_GD["qk"]) * dec[n] for n in nu]
    yield
    A = [beta[n] * kk[n] * dec[n] * strict for n in nu]
    Dg = [A[n] * bd for n in nu]
    L = [A[n] - Dg[n] for n in nu]
    if t_saved is None:
        ImD = [eye - Dg[n] for n in nu]
        D2 = inv(Dg, Dg)
        yield
        P1 = inv(ImD, [eye + x for x in D2])
        D4 = inv(D2, D2)
        yield
        P2 = inv(P1, [eye + x for x in D4])
        D8 = inv(D4, D4)
        yield
        M = inv(P2, [eye + x for x in D8])
        yield
        Nn = inv(M, L)
        yield
        N2 = inv(Nn, Nn)
        yield
        T1 = inv([eye - x for x in Nn], [eye + x for x in N2])
        yield
        Tinv = inv(T1, M)
        yield
    else:
        Tinv = [use_inverse(A[n], t_saved[b][:, h * DK:h * DK + C]) for n, (b, h) in enumerate(units)]
    rhs = [jnp.concatenate([beta[n] * v[n], (beta[n] * egam[n]) * k[n]], axis=1) for n in nu]
    sol = [mmx(Tinv[n], rhs[n], "nn", P_GD["sol"]) for n in nu]
    yield
    qwS = [mmx(jnp.concatenate([q[n] * egam[n], sol[n][:, DK:2 * DK]], axis=0), S[n], "nn", P_GD["ws"]) for n in nu]
    yield
    u = [sol[n][:, 0:DK] - qwS[n][C:2 * C] for n in nu]
    outs = [qwS[n][0:C] + mmx(qk[n], u[n], "nn", P_GD["o"]) for n in nu]
    Sn = [jnp.exp(glast[n]) * S[n] + mmx(k[n] * jnp.exp(grev_c[n]), u[n], "tn", P_GD["su"]) for n in nu]
    zpad = jnp.zeros((C, DK - C), f32)
    t_pack = tuple(jnp.concatenate([x for n in range(b * H, (b + 1) * H) for x in (lax.stop_gradient(Tinv[n]), zpad)],
                                   axis=1) for b in range(len(cs)))
    return tuple(Sn), tuple(jnp.concatenate(outs[b * H:(b + 1) * H], axis=1) for b in range(len(cs))), t_pack


def _in_proj(hflat, norm_w, wbig):
    tm = 384

    def body(h_ref, nw_ref, w_ref, p_ref, ut_ref):
        u = rmsnorm(h_ref[...], nw_ref[...])
        ut_ref[...] = u.T.astype(bf16)
        p_ref[...] = bdot_nt(u, w_ref[...])

    return pl.pallas_call(
        body, name="in_proj", grid=(N // tm,),
        in_specs=[pl.BlockSpec((tm, D), lambda i: (i, 0)), pl.BlockSpec((1, D), lambda i: (0, 0)),
                  pl.BlockSpec((PC, D), lambda i: (0, 0))],
        out_specs=[pl.BlockSpec((tm, PC), lambda i: (i, 0)), pl.BlockSpec((D, tm), lambda i: (0, i))],
        out_shape=[jax.ShapeDtypeStruct((N, PC), f32), jax.ShapeDtypeStruct((D, N), bf16)],
        compiler_params=_cp(("arbitrary",)),
    )(hflat, norm_w, wbig)


NU = NB * H
_REV = lambda c: NCH - 1 - c
_FWD = lambda c: c


def _tok_spec(w, ix, col=0):
    return pl.BlockSpec((NB, C, w), lambda c: (0, ix(c), col))


def _state_spec(ix):
    return pl.BlockSpec((NB, 1, H, DK, DK), lambda c: (0, ix(c), 0, 0, 0))


def _row_spec(w):
    return pl.BlockSpec((1, w), lambda c: (0, 0))


def _rows(ref):
    return tuple(ref[b] for b in range(NB))


def _hg_extra_specs(ix):
    return [_row_spec(HD), _row_spec(HD)]


def _gd_extra_specs(ix):
    return [_tok_spec(DK, ix, COL_AB // DK), _row_spec(DK), _row_spec(DK)]


def _mix_fwd(proj3, cv, l0, l1, alog, dtb):
    def body(p_ref, c_ref, ab_ref, l0_ref, l1_ref, al_ref, db_ref, oh_ref, sh_ref, og_ref, sg_ref, t_ref, sth, stg):
        @pl.when(pl.program_id(0) == 0)
        def _():
            sth[...] = jnp.zeros_like(sth)
            stg[...] = jnp.zeros_like(stg)

        Sh = tuple(sth[n] for n in range(NU))
        Sg = tuple(stg[n] for n in range(NU))
        for n in range(NU):
            sh_ref[n // H, 0, n % H] = Sh[n]
            sg_ref[n // H, 0, n % H] = Sg[n]
        snh, oh, sng, og, tp = mix_chunk(Sh, _rows(p_ref), l0_ref[...], l1_ref[...],
                                         Sg, _rows(c_ref), _rows(ab_ref), al_ref[...], db_ref[...])
        for n in range(NU):
            sth[n] = snh[n]
            stg[n] = sng[n]
        for b in range(NB):
            oh_ref[b] = oh[b]
            og_ref[b] = og[b]
            t_ref[b] = tp[b]

    tok = jax.ShapeDtypeStruct((NB, TP, HD), f32)
    st = jax.ShapeDtypeStruct((NB, NCH, H, DK, DK), f32)
    return pl.pallas_call(
        body, name="mix_fwd", grid=(NCH,),
        in_specs=[_tok_spec(3 * HD, _FWD), _tok_spec(3 * HD, _FWD), _tok_spec(DK, _FWD, COL_AB // DK),
                  _row_spec(HD), _row_spec(HD), _row_spec(DK), _row_spec(DK)],
        out_specs=[_tok_spec(HD, _FWD), _state_spec(_FWD), _tok_spec(HD, _FWD), _state_spec(_FWD), _tok_spec(HD, _FWD)],
        out_shape=[tok, st, tok, st, tok],
        scratch_shapes=[pltpu.VMEM((NU, DK, DK), f32), pltpu.VMEM((NU, DK, DK), f32)],
        compiler_params=_cp(("arbitrary",)),
    )(proj3, cv, proj3, l0, l1, alog, dtb)


def _hg_bwd(proj3, l0, l1, s_saved, do):
    def body(p_ref, l0_ref, l1_ref, s_ref, do_ref, dp_ref, dl0_ref, dl1_ref, dst):
        @pl.when(pl.program_id(0) == 0)
        def _():
            dst[...] = jnp.zeros_like(dst)
            dl0_ref[...] = jnp.zeros_like(dl0_ref)
            dl1_ref[...] = jnp.zeros_like(dl1_ref)

        S = tuple(s_ref[n // H, 0, n % H] for n in range(NU))
        _, vjp = jax.vjp(hg_chunk, S, _rows(p_ref), l0_ref[...], l1_ref[...])
        dS, dp, dl0, dl1 = vjp((tuple(dst[n] for n in range(NU)), _rows(do_ref)))
        for n in range(NU):
            dst[n] = dS[n]
        for b in range(NB):
            dp_ref[b] = dp[b].astype(bf16)
        dl0_ref[...] += jnp.broadcast_to(dl0, (8, HD))
        dl1_ref[...] += jnp.broadcast_to(dl1, (8, HD))

    acc = pl.BlockSpec((8, HD), lambda c: (0, 0))
    return pl.pallas_call(
        body, name="hg_bwd", grid=(NCH,),
        in_specs=[_tok_spec(3 * HD, _REV)] + _hg_extra_specs(_REV) + [_state_spec(_REV), _tok_spec(HD, _REV)],
        out_specs=[_tok_spec(3 * HD, _REV), acc, acc],
        out_shape=[jax.ShapeDtypeStruct((NB, TP, 3 * HD), bf16), jax.ShapeDtypeStruct((8, HD), f32),
                   jax.ShapeDtypeStruct((8, HD), f32)],
        scratch_shapes=[pltpu.VMEM((NU, DK, DK), f32)],
        compiler_params=_cp(("arbitrary",)),
    )(proj3, l0, l1, s_saved, do)


def _gd_bwd(cv, proj3, alog, dtb, s_saved, t_saved, do):
    def body(c_ref, ab_ref, al_ref, db_ref, s_ref, t_ref, do_ref, dc_ref, dab_ref, dal_ref, ddb_ref, dst):
        @pl.when(pl.program_id(0) == 0)
        def _():
            dst[...] = jnp.zeros_like(dst)
            dal_ref[...] = jnp.zeros_like(dal_ref)
            ddb_ref[...] = jnp.zeros_like(ddb_ref)

        S = tuple(s_ref[n // H, 0, n % H] for n in range(NU))
        t_rows = _rows(t_ref)
        fn = lambda *a: gd_chunk(*a, t_saved=t_rows)[0:2]
        _, vjp = jax.vjp(fn, S, _rows(c_ref), _rows(ab_ref), al_ref[...], db_ref[...])
        dS, dc, dab, dal, ddb = vjp((tuple(dst[n] for n in range(NU)), _rows(do_ref)))
        for n in range(NU):
            dst[n] = dS[n]
        for b in range(NB):
            dc_ref[b] = dc[b]
            dab_ref[b] = dab[b].astype(bf16)
        dal_ref[...] += jnp.broadcast_to(dal, (8, DK))
        ddb_ref[...] += jnp.broadcast_to(ddb, (8, DK))

    acc = pl.BlockSpec((8, DK), lambda c: (0, 0))
    return pl.pallas_call(
        body, name="gd_bwd", grid=(NCH,),
        in_specs=[_tok_spec(3 * HD, _REV)] + _gd_extra_specs(_REV)
        + [_state_spec(_REV), _tok_spec(HD, _REV), _tok_spec(HD, _REV)],
        out_specs=[_tok_spec(3 * HD, _REV), _tok_spec(DK, _REV), acc, acc],
        out_shape=[jax.ShapeDtypeStruct((NB, TP, 3 * HD), f32), jax.ShapeDtypeStruct((NB, TP, DK), bf16),
                   jax.ShapeDtypeStruct((8, DK), f32), jax.ShapeDtypeStruct((8, DK), f32)],
        scratch_shapes=[pltpu.VMEM((NU, DK, DK), f32)],
        compiler_params=_cp(("arbitrary",)),
    )(cv, proj3, alog, dtb, s_saved, t_saved, do)


def _conv_fwd(proj3, conv4):
    def body(x_ref, w_ref, y_ref):
        x = x_ref[0]
        y = w_ref[3] * x
        for s in (1, 2, 3):
            y = y + w_ref[3 - s] * pltpu.roll(x, s, 0)
        y_ref[0] = y
        y_ref[0, 0:8, :] = jnp.zeros((8, HD), f32)

    return pl.pallas_call(
        body, name="conv_fwd", grid=(NB, 3),
        in_specs=[pl.BlockSpec((1, TP, HD), lambda b, j: (b, 0, COL_QKV // HD + j)),
                  pl.BlockSpec((4, 1, HD), lambda b, j: (0, 0, j))],
        out_specs=pl.BlockSpec((1, TP, HD), lambda b, j: (b, 0, j)),
        out_shape=jax.ShapeDtypeStruct((NB, TP, 3 * HD), f32),
        compiler_params=_cp(("arbitrary", "arbitrary")),
    )(proj3, conv4)


def _conv_bwd(proj3, conv4, dy):
    def body(x_ref, w_ref, dy_ref, dx_ref, dw_ref):
        @pl.when(pl.program_id(1) == 0)
        def _():
            dw_ref[...] = jnp.zeros_like(dw_ref)

        x = x_ref[0]
        row = lax.broadcasted_iota(jnp.int32, (TP, 1), 0)
        g = jnp.where(row >= 8, dy_ref[0], 0.0)
        dx = w_ref[3] * g
        dw_ref[3] += jnp.broadcast_to(jnp.sum(x * g, axis=0, keepdims=True), (8, HD))
        for s in (1, 2, 3):
            dx = dx + w_ref[3 - s] * pltpu.roll(g, TP - s, 0)
            dw_ref[3 - s] += jnp.broadcast_to(jnp.sum(pltpu.roll(x, s, 0) * g, axis=0, keepdims=True), (8, HD))
        dx_ref[0] = dx.astype(bf16)

    return pl.pallas_call(
        body, name="conv_bwd", grid=(3, NB),
        in_specs=[pl.BlockSpec((1, TP, HD), lambda j, b: (b, 0, COL_QKV // HD + j)),
                  pl.BlockSpec((4, 1, HD), lambda j, b: (0, 0, j)), pl.BlockSpec((1, TP, HD), lambda j, b: (b, 0, j))],
        out_specs=[pl.BlockSpec((1, TP, HD), lambda j, b: (b, 0, j)), pl.BlockSpec((4, 8, HD), lambda j, b: (0, 0, j))],
        out_shape=[jax.ShapeDtypeStruct((NB, TP, 3 * HD), bf16), jax.ShapeDtypeStruct((4, 8, 3 * HD), f32)],
        compiler_params=_cp(("arbitrary", "arbitrary")),
    )(proj3, conv4, dy)


def _out_loss(o_hg, o_gd, proj, hgw, gdw, wout, hflat, fw, target):
    tm = 384

    def body(ohg_ref, ogd_ref, zhg_ref, zgd_ref, hgw_ref, gdw_ref, wo_ref, h_ref, fw_ref, tg_ref,
             loss_ref, dohg_ref, dogd_ref, dzhg_ref, dzgd_ref, dh_ref, dwo_ref, dhgw_ref, dgdw_ref, dfw_ref):
        i = pl.program_id(0)

        @pl.when(i == 0)
        def _():
            for r in (loss_ref, dwo_ref, dhgw_ref, dgdw_ref, dfw_ref):
                r[...] = jnp.zeros_like(r)

        row = i * tm + lax.broadcasted_iota(jnp.int32, (tm, 1), 0)
        tok = jnp.where(row >= TP, row - TP, row)
        valid = (tok >= PAD + N_META).astype(f32)
        hval = h_ref[...]
        tgt = tg_ref[...]

        mixers = ((ohg_ref, zhg_ref, hgw_ref[...]), (ogd_ref, zgd_ref, gdw_ref[...]))
        saved, ys = [], []
        for o_ref, z_ref, w in mixers:
            for hh in range(H):
                sl = slice(hh * DK, (hh + 1) * DK)
                o, z = o_ref[:, sl], z_ref[:, sl]
                r = lax.rsqrt(jnp.mean(o * o, axis=-1, keepdims=True) + EPS)
                n = o * r
                sg = sigmoid(z)
                ws = w * (z * sg)
                saved.append((r, n, sg, z, ws, w))
                ys.append(n * ws)
        y = jnp.concatenate(ys, axis=-1)
        h2 = hval + bdot(y, wo_ref[...])
        r2 = lax.rsqrt(jnp.mean(h2 * h2, axis=-1, keepdims=True) + EPS)
        n2 = h2 * r2
        fwv = fw_ref[...]
        err = (n2 * fwv - tgt) * valid
        loss = (0.5 / D) * jnp.sum(err * err)
        dyf = err * (1.0 / D)
        dn2 = dyf * fwv
        dout = r2 * (dn2 - n2 * jnp.mean(dn2 * n2, axis=-1, keepdims=True))
        dh_ref[...] = dout
        dy = bdot_nt(dout, wo_ref[...])
        dwo_ref[...] += bdot_tn(y, dout)
        dws = []
        for mi, (do_ref, dz_ref) in enumerate(((dohg_ref, dzhg_ref), (dogd_ref, dzgd_ref))):
            dw = jnp.zeros((1, DK), f32)
            for hh in range(H):
                sl = slice(hh * DK, (hh + 1) * DK)
                r, n, sg, z, ws, w = saved[mi * H + hh]
                dyh = dy[:, mi * HD + hh * DK:mi * HD + (hh + 1) * DK]
                t = dyh * n
                dw = dw + jnp.sum(t * (z * sg), axis=0, keepdims=True)
                dz_ref[:, sl] = (t * w * (sg * (1.0 + z * (1.0 - sg)))).astype(bf16)
                dn = dyh * ws
                do_ref[:, sl] = r * (dn - n * jnp.mean(dn * n, axis=-1, keepdims=True))
            dws.append(dw)
        loss_ref[...] += jnp.broadcast_to(loss, (8, DK))
        dhgw_ref[...] += jnp.broadcast_to(dws[0], (8, DK))
        dgdw_ref[...] += jnp.broadcast_to(dws[1], (8, DK))
        dfw_ref[...] += jnp.broadcast_to(jnp.sum(dyf * n2, axis=0, keepdims=True), (8, D))

    row = lambda w: pl.BlockSpec((tm, w), lambda i: (i, 0))
    whole = lambda r, w: pl.BlockSpec((r, w), lambda i: (0, 0))
    col = lambda c0: pl.BlockSpec((tm, HD), lambda i: (i, c0 // HD))
    return pl.pallas_call(
        body, name="out_loss", grid=(N // tm,),
        in_specs=[row(HD), row(HD), col(COL_ZHG), col(COL_ZGD),
                  whole(1, DK), whole(1, DK), whole(D, D), row(D), whole(1, D), row(D)],
        out_specs=[whole(8, DK), row(HD), row(HD), row(HD), row(HD), row(D), whole(D, D),
                   whole(8, DK), whole(8, DK), whole(8, D)],
        out_shape=[jax.ShapeDtypeStruct((8, DK), f32)] + [jax.ShapeDtypeStruct((N, HD), f32)] * 2
        + [jax.ShapeDtypeStruct((N, HD), bf16)] * 2
        + [jax.ShapeDtypeStruct((N, D), f32), jax.ShapeDtypeStruct((D, D), f32),
           jax.ShapeDtypeStruct((8, DK), f32), jax.ShapeDtypeStruct((8, DK), f32), jax.ShapeDtypeStruct((8, D), f32)],
        compiler_params=_cp(("arbitrary",)),
    )(o_hg, o_gd, proj, proj, hgw, gdw, wout, hflat, fw, target)


def _in_bwd(pieces, wbig, hflat, norm_w, dh_res, pbs):
    tm = 384
    nsteps = N // tm
    np_ = len(pieces)
    na = len(pbs)
    offs = [c0 for _, c0 in pieces]
    widths = [d.shape[1] for d, _ in pieces]

    def body(*refs):
        d_refs = refs[:np_]
        w_ref, h_ref, nw_ref, dhr_ref = refs[np_:np_ + 4]
        srcs = refs[np_ + 4:np_ + 4 + na]
        dh_ref, dnw_ref = refs[np_ + 4 + na:np_ + 6 + na]
        dsts = refs[np_ + 6 + na:np_ + 6 + 2 * na]
        sems = refs[np_ + 6 + 2 * na:]
        i = pl.program_id(0)

        def copies():
            if not na:
                return []
            x, y, c, chips = _place()
            return [pltpu.make_async_remote_copy(
                src_ref=srcs[a].at[2 * px + py], dst_ref=dsts[a].at[j], send_sem=sems[0].at[na * j + a],
                recv_sem=sems[1].at[na * j + a], device_id=(px, py, c), device_id_type=MESH)
                for j, (px, py) in enumerate(chips) for a in range(na)]

        @pl.when(i == 0)
        def _():
            dnw_ref[...] = jnp.zeros_like(dnw_ref)
            for cp in copies():
                cp.start()

        du = jnp.zeros((tm, D), f32)
        for d_ref, off, wd in zip(d_refs, offs, widths):
            du = du + bdot(d_ref[...], w_ref[off:off + wd, :])
        _, vjp = jax.vjp(rmsnorm, h_ref[...], nw_ref[...])
        dh, dnw = vjp(du)
        dh_ref[...] = dh + dhr_ref[...]
        dnw_ref[...] += jnp.broadcast_to(dnw, (8, D))

        @pl.when(i == nsteps - 1)
        def _():
            for cp in copies():
                cp.wait()

    row = lambda w: pl.BlockSpec((tm, w), lambda i: (i, 0))
    return pl.pallas_call(
        body, name="in_bwd", grid=(nsteps,),
        in_specs=[row(w) for w in widths]
        + [pl.BlockSpec((PC, D), lambda i: (0, 0)), row(D), pl.BlockSpec((1, D), lambda i: (0, 0)), row(D)] + [ANY] * na,
        out_specs=[row(D), pl.BlockSpec((8, D), lambda i: (0, 0))] + [ANY] * na,
        out_shape=[jax.ShapeDtypeStruct((N, D), f32), jax.ShapeDtypeStruct((8, D), f32)]
        + [jax.ShapeDtypeStruct((3,) + p.shape[1:], p.dtype) for p in pbs],
        scratch_shapes=[pltpu.SemaphoreType.DMA((3 * na,)), pltpu.SemaphoreType.DMA((3 * na,))] if na else [],
        compiler_params=_cp(("arbitrary",)),
    )(*[d for d, _ in pieces], wbig, hflat, norm_w, dh_res, *pbs)


def _w_grad(ut, pieces):
    tk = 384
    offs = [c0 for _, c0 in pieces]
    widths = [d.shape[1] for d, _ in pieces]

    def body(u_ref, *refs):
        d_refs, o_ref = refs[:-1], refs[-1]

        @pl.when(pl.program_id(0) == 0)
        def _():
            o_ref[...] = jnp.zeros_like(o_ref)

        u = u_ref[...]
        for d_ref, off, wd in zip(d_refs, offs, widths):
            o_ref[:, off:off + wd] += jnp.dot(u, d_ref[...], preferred_element_type=f32)

    return pl.pallas_call(
        body, name="w_grad", grid=(N // tk,),
        in_specs=[pl.BlockSpec((D, tk), lambda k: (0, k))] + [pl.BlockSpec((tk, w), lambda k: (k, 0)) for w in widths],
        out_specs=pl.BlockSpec((D, PC), lambda k: (0, 0)),
        out_shape=jax.ShapeDtypeStruct((D, PC), f32),
        compiler_params=_cp(("arbitrary",)),
    )(ut, *[d for d, _ in pieces])


def _adam_math(g, w, m, v):
    m2 = ADAM_B1 * m + (1.0 - ADAM_B1) * g
    v2 = ADAM_B2 * v + (1.0 - ADAM_B2) * (g * g)
    m_hat = m2 / (1.0 - ADAM_B1 ** ADAM_STEP)
    v_hat = v2 / (1.0 - ADAM_B2 ** ADAM_STEP)
    delta = -ADAM_LR * (m_hat / (jnp.sqrt(v_hat) + ADAM_EPS) + ADAM_WD * w)
    return delta, m2, v2


def _adamw(gs, w, m, v, name):
    R, Cc = w.shape
    tr = 256 if R % 256 == 0 else R
    ng = len(gs)

    def body(*refs):
        g = refs[0][...]
        for r in refs[1:ng]:
            g = g + r[...]
        w_ref, m_ref, v_ref, g_ref, d_ref, m2_ref, v2_ref = refs[ng:]
        delta, m2, v2 = _adam_math(g, w_ref[...], m_ref[...], v_ref[...])
        g_ref[...] = g
        d_ref[...] = delta
        m2_ref[...] = m2
        v2_ref[...] = v2

    spec = pl.BlockSpec((tr, Cc), lambda i: (i, 0))
    return pl.pallas_call(
        body, name=name, grid=(R // tr,),
        in_specs=[spec] * (ng + 3), out_specs=[spec] * 4,
        out_shape=[jax.ShapeDtypeStruct((R, Cc), f32)] * 4,
        compiler_params=_cp(("arbitrary",)),
    )(*gs, w, m, v)


def _adamw_rows(g, w, m, v, name):
    R, _, Cc = w.shape
    tr = R // 9

    def body(g_ref, w_ref, m_ref, v_ref, go_ref, d_ref, m2_ref, v2_ref):
        g = g_ref[...]
        delta, m2, v2 = _adam_math(g, w_ref[...], m_ref[...], v_ref[...])
        go_ref[...] = g
        d_ref[...] = delta
        m2_ref[...] = m2
        v2_ref[...] = v2

    spec = pl.BlockSpec((tr, 1, Cc), lambda i: (i, 0, 0))
    return pl.pallas_call(
        body, name=name, grid=(R // tr,),
        in_specs=[spec] * 4, out_specs=[spec] * 4,
        out_shape=[jax.ShapeDtypeStruct((R, 1, Cc), f32)] * 4,
        compiler_params=_cp(("arbitrary",)),
    )(g, w, m, v)


def _place():
    x, y, c = lax.axis_index("x"), lax.axis_index("y"), lax.axis_index("c")
    return x, y, c, [(1 - x, y), (x, 1 - y), (1 - x, 1 - y)]


def _gather_weights(halved, whole):
    nh, nw = len(halved), len(whole)
    na = nh + nw

    def body(*refs):
        srcs, dsts = refs[:na], refs[na:2 * na]
        send_sems, recv_sems, loc_sems = refs[2 * na:2 * na + 3]
        stage = refs[2 * na + 3:]
        x, y, c, chips = _place()
        me = 2 * x + y
        loads = [pltpu.make_async_copy(s, v, loc_sems.at[i]) for i, (s, v) in enumerate(zip(srcs, stage))]
        locs = [pltpu.make_async_copy(v, d.at[me], loc_sems.at[i]) for i, (v, d) in enumerate(zip(stage, dsts))]
        for cp in loads:
            cp.start()

        def ici(j, i, slot):
            px, py = chips[j]
            src = srcs[i].at[c] if i < nh else srcs[i]
            dst = dsts[i].at[slot, c] if i < nh else dsts[i].at[slot]
            return pltpu.make_async_remote_copy(
                src_ref=src, dst_ref=dst, send_sem=send_sems.at[na * j + i], recv_sem=recv_sems.at[na * j + i],
                device_id=(px, py, c), device_id_type=MESH)

        def d2d(j, i, half):
            px, py = chips[j]
            blk = dsts[i].at[2 * px + py, half]
            return pltpu.make_async_remote_copy(
                src_ref=blk, dst_ref=blk, send_sem=send_sems.at[3 * na + nh * j + i],
                recv_sem=recv_sems.at[3 * na + nh * j + i], device_id=(x, y, 1 - c), device_id_type=MESH)

        sends = [ici(j, i, me) for j in range(3) for i in range(na)]
        for cp in sends:
            cp.start()
        for ld, st in zip(loads, locs):
            ld.wait()
            st.start()
        for j, (px, py) in enumerate(chips):
            for i in range(na):
                ici(j, i, 2 * px + py).wait_recv()
                if i < nh:
                    fwd = d2d(j, i, c)
                    fwd.start()
                    sends.append(fwd)
        for j in range(3):
            for i in range(nh):
                d2d(j, i, 1 - c).wait_recv()
        for cp in sends:
            cp.wait_send()
        for cp in locs:
            cp.wait()

    nsem = 3 * na + 3 * nh
    return pl.pallas_call(
        body, name="gather_weights",
        in_specs=[ANY] * na, out_specs=[ANY] * na,
        out_shape=[jax.ShapeDtypeStruct((4,) + s.shape, s.dtype) for s in list(halved) + list(whole)],
        scratch_shapes=[pltpu.SemaphoreType.DMA((nsem,)), pltpu.SemaphoreType.DMA((nsem,)),
                        pltpu.SemaphoreType.DMA((na,))] + [pltpu.VMEM(s.shape, s.dtype) for s in list(halved) + list(whole)],
        compiler_params=pltpu.CompilerParams(has_side_effects=True, vmem_limit_bytes=VMEM_LIMIT),
    )(*halved, *whole)


def _swap_halves(gs):
    na = len(gs)
    jobs = [(i, q) for i in range(na) for q in range(gs[i].shape[0])]

    def body(*refs):
        srcs, dsts = refs[:na], refs[na:2 * na]
        send_sems, recv_sems = refs[2 * na:]
        x, y, c, _ = _place()
        cps = [pltpu.make_async_remote_copy(
            src_ref=srcs[i].at[q, 1 - c], dst_ref=dsts[i].at[q], send_sem=send_sems.at[k],
            recv_sem=recv_sems.at[k], device_id=(x, y, 1 - c), device_id_type=MESH)
            for k, (i, q) in enumerate(jobs)]
        for cp in cps:
            cp.start()
        for cp in cps:
            cp.wait()

    return pl.pallas_call(
        body, name="swap_halves",
        in_specs=[ANY] * na, out_specs=[ANY] * na,
        out_shape=[jax.ShapeDtypeStruct(g.shape[0:1] + g.shape[2:], g.dtype) for g in gs],
        scratch_shapes=[pltpu.SemaphoreType.DMA((len(jobs),)), pltpu.SemaphoreType.DMA((len(jobs),))],
        compiler_params=pltpu.CompilerParams(has_side_effects=True),
    )(*gs)


def _add_halves(c_arr, g, s, name):
    Q, _, R, Cc = g.shape
    tr = min(R, 128)

    def body(c_ref, g_ref, s_ref, b_ref, f_ref):
        p = g_ref[0, 0] + s_ref[0]
        f_ref[0] = p
        b_ref[0] = p.astype(bf16)

    blk = pl.BlockSpec((1, tr, Cc), lambda q, i, cr: (q, i, 0))
    return pl.pallas_call(
        body, name=name,
        grid_spec=pltpu.PrefetchScalarGridSpec(
            num_scalar_prefetch=1, grid=(Q, R // tr),
            in_specs=[pl.BlockSpec((1, 1, tr, Cc), lambda q, i, cr: (q, cr[0], i, 0)), blk], out_specs=[blk, blk]),
        out_shape=[jax.ShapeDtypeStruct((Q, R, Cc), bf16), jax.ShapeDtypeStruct((Q, R, Cc), f32)],
        compiler_params=_cp(("arbitrary", "arbitrary")),
    )(c_arr, g, s)


_FLIPS = [(fx, fy, fc) for fx in (0, 1) for fy in (0, 1) for fc in (0, 1)][1:]


def _sum_blocks(own, r, name):
    R, Cc = own.shape
    tr = min(R, 256)

    def body(own_ref, r_ref, o_ref):
        acc = own_ref[...]
        for j in range(3):
            acc = acc + r_ref[j].astype(f32)
        o_ref[...] = acc

    return pl.pallas_call(
        body, name=name, grid=(R // tr,),
        in_specs=[pl.BlockSpec((tr, Cc), lambda i: (i, 0)), pl.BlockSpec((3, tr, Cc), lambda i: (0, i, 0))],
        out_specs=pl.BlockSpec((tr, Cc), lambda i: (i, 0)),
        out_shape=jax.ShapeDtypeStruct((R, Cc), f32),
        compiler_params=_cp(("arbitrary",)),
    )(own, r)


def _sum_packs(me8_arr, pack, rp):
    R = pack.shape[0]

    def body(me_ref, pk_ref, rp_ref, o_ref):
        me8 = me_ref[0]
        acc = None
        for d in range(8):
            rel = d ^ me8
            term = jnp.where(rel == 0, pk_ref[...], rp_ref[jnp.maximum(rel - 1, 0)])
            acc = term if acc is None else acc + term
        o_ref[...] = acc

    return pl.pallas_call(
        body, name="sum_packs",
        grid_spec=pltpu.PrefetchScalarGridSpec(
            num_scalar_prefetch=1, grid=(1,),
            in_specs=[pl.BlockSpec((R, 128), lambda i, mr: (0, 0)), pl.BlockSpec((7, R, 128), lambda i, mr: (0, 0, 0))],
            out_specs=pl.BlockSpec((R, 128), lambda i, mr: (0, 0))),
        out_shape=jax.ShapeDtypeStruct((R, 128), f32),
        compiler_params=_cp(("arbitrary",)),
    )(me8_arr, pack, rp)


def _swap_finished(fs, pack):
    na = len(fs)
    R = pack.shape[0]

    def body(*refs):
        srcs, pk = refs[:na], refs[na]
        dsts, rp = refs[na + 1:2 * na + 1], refs[2 * na + 1]
        send_sems, recv_sems = refs[2 * na + 2:]
        x, y, c, _ = _place()
        cps = [pltpu.make_async_remote_copy(
            src_ref=srcs[i], dst_ref=dsts[i], send_sem=send_sems.at[i], recv_sem=recv_sems.at[i],
            device_id=(x, y, 1 - c), device_id_type=MESH) for i in range(na)]
        cps += [pltpu.make_async_remote_copy(
            src_ref=pk, dst_ref=rp.at[k], send_sem=send_sems.at[na + k], recv_sem=recv_sems.at[na + k],
            device_id=(x ^ fx, y ^ fy, c ^ fc), device_id_type=MESH) for k, (fx, fy, fc) in enumerate(_FLIPS)]
        for cp in cps:
            cp.start()
        for cp in cps:
            cp.wait()

    return pl.pallas_call(
        body, name="swap_finished",
        in_specs=[ANY] * (na + 1), out_specs=[ANY] * (na + 1),
        out_shape=[jax.ShapeDtypeStruct(f.shape, f.dtype) for f in fs] + [jax.ShapeDtypeStruct((7, R, 128), f32)],
        scratch_shapes=[pltpu.SemaphoreType.DMA((na + 7,)), pltpu.SemaphoreType.DMA((na + 7,))],
        compiler_params=pltpu.CompilerParams(has_side_effects=True),
    )(*fs, pack)


def _adamw_halves(c_arr, mine, peer, w, m, v, name):
    _, R, Cc = w.shape
    tr = min(R, 256)

    def body(c_ref, mine_ref, peer_ref, w_ref, m_ref, v_ref, g_ref, d_ref, m2_ref, v2_ref):
        g = jnp.where(pl.program_id(0) == c_ref[0], mine_ref[...], peer_ref[...])
        delta, m2, v2 = _adam_math(g, w_ref[0], m_ref[0], v_ref[0])
        g_ref[0] = g
        d_ref[0] = delta
        m2_ref[0] = m2
        v2_ref[0] = v2

    half = pl.BlockSpec((tr, Cc), lambda hh, i, cr: (i, 0))
    full = pl.BlockSpec((1, tr, Cc), lambda hh, i, cr: (hh, i, 0))
    return pl.pallas_call(
        body, name=name,
        grid_spec=pltpu.PrefetchScalarGridSpec(
            num_scalar_prefetch=1, grid=(2, R // tr), in_specs=[half, half, full, full, full], out_specs=[full] * 4),
        out_shape=[jax.ShapeDtypeStruct((2, R, Cc), f32)] * 4,
        compiler_params=_cp(("arbitrary", "arbitrary")),
    )(c_arr, mine, peer, w, m, v)


def _rows8(a):
    flat = a.reshape(-1)
    n = flat.shape[0]
    rows = -(-n // 1024) * 8
    return jnp.pad(flat, (0, rows * 128 - n)).reshape(rows, 128)


def kernel(x, meta_tokens, norm_w, w_in, conv_w, hg_lb_logits, hg_norm_w, gdn_A_log, gdn_dt_bias, gdn_norm_w, w_out, final_norm_w, loss_target, m_meta_tokens, m_norm_w, m_w_in, m_conv_w, m_hg_lb_logits, m_hg_norm_w, m_gdn_A_log, m_gdn_dt_bias, m_gdn_norm_w, m_w_out, m_final_norm_w, v_meta_tokens, v_norm_w, v_w_in, v_conv_w, v_hg_lb_logits, v_hg_norm_w, v_gdn_A_log, v_gdn_dt_bias, v_gdn_norm_w, v_w_out, v_final_norm_w):
    me = 2 * lax.axis_index("x") + lax.axis_index("y")

    wt_halves = jnp.transpose(jnp.transpose(w_in, (2, 0, 1)).astype(bf16).reshape(SHARD_COLS, 2, D // 2), (1, 0, 2))
    g_win, g_wout, g_conv, g_meta = _gather_weights(
        [wt_halves, w_out[0].astype(bf16).reshape(2, D // 8, D)], [conv_w[0], meta_tokens])
    wt_full = jnp.transpose(g_win, (0, 2, 1, 3)).reshape(IN_COLS, D)
    wbig = jnp.pad(wt_full, ((0, PC - IN_COLS), (0, 0)))
    wout_full = g_wout.reshape(D, D)
    conv4 = jnp.transpose(g_conv, (1, 0, 2)).reshape(4, 1, 3 * HD)
    meta_full = jnp.transpose(g_meta, (1, 0, 2)).reshape(N_META, D)

    c_arr = lax.axis_index("c").reshape(1).astype(jnp.int32)

    def chip_partials(gw, g_wout_part):
        g_in2 = gw.reshape(1, 2, D // 2, PC)
        g_out4 = g_wout_part.reshape(4, 2, D // 8, D)
        s_in, s_out = _swap_halves([g_in2, g_out4])
        pb_in, pf_in = _add_halves(c_arr, g_in2, s_in, "add_w_in")
        pb_out, pf_out = _add_halves(c_arr, g_out4, s_out, "add_w_out")
        pb_blocks = jnp.transpose(pb_in[0, :, 0:IN_COLS].reshape(D // 2, 4, SHARD_COLS), (1, 0, 2))
        own_in = lax.dynamic_slice(pf_in[0], (0, me * SHARD_COLS), (D // 2, SHARD_COLS))
        own_out = lax.dynamic_index_in_dim(pf_out, me, axis=0, keepdims=False)
        return [pb_blocks, pb_out], [own_in, own_out]

    (loss8, grad_x, d_meta, d_nw, d_conv, d_lb, d_hgw, d_alog, d_dtb, d_gdw, d_fw, pfs, rs) = _local_step(
        x, loss_target, wbig, wout_full, conv4, meta_full, norm_w, hg_lb_logits, hg_norm_w, gdn_A_log, gdn_dt_bias,
        gdn_norm_w, final_norm_w, chip_partials)

    pack = jnp.concatenate([
        loss8, d_nw[0].reshape(8, 128), d_lb.reshape(8, 128), d_hgw, _rows8(d_alog[0, :H]), _rows8(d_dtb[0, :H]),
        d_gdw, d_fw[0].reshape(8, 128), d_meta.reshape(128, 128), d_conv.reshape(48, 128)], axis=0)
    return _reduce_and_update(
        me, c_arr, grad_x, pfs, rs, pack, meta_tokens, norm_w, w_in, conv_w, hg_lb_logits, hg_norm_w, gdn_A_log,
        gdn_dt_bias, gdn_norm_w, w_out, final_norm_w, m_meta_tokens, m_norm_w, m_w_in, m_conv_w, m_hg_lb_logits,
        m_hg_norm_w, m_gdn_A_log, m_gdn_dt_bias, m_gdn_norm_w, m_w_out, m_final_norm_w, v_meta_tokens, v_norm_w, v_w_in,
        v_conv_w, v_hg_lb_logits, v_hg_norm_w, v_gdn_A_log, v_gdn_dt_bias, v_gdn_norm_w, v_w_out, v_final_norm_w)


def _local_step(x, loss_target, wbig, wout_full, conv4, meta_full, norm_w, hg_lb_logits, hg_norm_w, gdn_A_log, gdn_dt_bias,
                gdn_norm_w, final_norm_w, chip_partials):
    h3 = jnp.concatenate([jnp.zeros((NB, PAD, D), f32), jnp.broadcast_to(meta_full[None], (NB, N_META, D)), x], axis=1)
    hflat = h3.reshape(N, D)
    target = jnp.pad(loss_target, ((0, 0), (PAD + N_META, 0), (0, 0))).reshape(N, D)
    l0, l1 = hg_lb_logits[0:1], hg_lb_logits[1:2]
    alog = jnp.pad(gdn_A_log, ((0, 0), (0, DK - H)))
    dtb = jnp.pad(gdn_dt_bias, ((0, 0), (0, DK - H)))
    fw = final_norm_w.reshape(1, D)

    proj, ut = _in_proj(hflat, norm_w, wbig)
    proj3 = proj.reshape(NB, TP, PC)
    cv = _conv_fwd(proj3, conv4)
    o_hg, s_hg, o_gd, s_gd, t_gd = _mix_fwd(proj3, cv, l0, l1, alog, dtb)
    (loss8, d_ohg, d_ogd, d_zhg, d_zgd, dh_res, g_wout_part, d_hgw, d_gdw, d_fw) = _out_loss(
        o_hg.reshape(N, HD), o_gd.reshape(N, HD), proj, hg_norm_w, gdn_norm_w, wout_full, hflat, fw, target)
    d_hg, d_l0, d_l1 = _hg_bwd(proj3, l0, l1, s_hg, d_ohg.reshape(NB, TP, HD))
    d_cv, d_ab, d_alog, d_dtb = _gd_bwd(cv, proj3, alog, dtb, s_gd, t_gd, d_ogd.reshape(NB, TP, HD))
    d_qkv, d_conv4 = _conv_bwd(proj3, conv4, d_cv)
    d_hg2, d_qkv2, d_ab2 = d_hg.reshape(N, 3 * HD), d_qkv.reshape(N, 3 * HD), d_ab.reshape(N, DK)
    pieces = [(d_hg2, COL_HG), (d_zhg, COL_ZHG), (d_qkv2, COL_QKV), (d_zgd, COL_ZGD), (d_ab2, COL_AB)]
    gw = _w_grad(ut, pieces)
    pbs, pfs = chip_partials(gw, g_wout_part) if chip_partials else ([], [gw, g_wout_part])
    dh, d_nw, *rs = _in_bwd(pieces, wbig, hflat, norm_w, dh_res, pbs)

    dh3 = dh.reshape(NB, TP, D)
    grad_x = dh3[:, PAD + N_META:, :]
    d_meta = jnp.sum(dh3[:, PAD:PAD + N_META, :], axis=0)
    d_conv = d_conv4[:, 0, :]
    d_lb = jnp.concatenate([d_l0[0:1], d_l1[0:1]], axis=0)
    return loss8, grad_x, d_meta, d_nw, d_conv, d_lb, d_hgw, d_alog, d_dtb, d_gdw, d_fw, pfs, rs


def _reduce_and_update(me, c_arr, grad_x, pfs, rs, pack, meta_tokens, norm_w, w_in, conv_w, hg_lb_logits, hg_norm_w,
                       gdn_A_log, gdn_dt_bias, gdn_norm_w, w_out, final_norm_w, m_meta_tokens, m_norm_w, m_w_in, m_conv_w,
                       m_hg_lb_logits, m_hg_norm_w, m_gdn_A_log, m_gdn_dt_bias, m_gdn_norm_w, m_w_out, m_final_norm_w,
                       v_meta_tokens, v_norm_w, v_w_in, v_conv_w, v_hg_lb_logits, v_hg_norm_w, v_gdn_A_log, v_gdn_dt_bias,
                       v_gdn_norm_w, v_w_out, v_final_norm_w):
    (own_in, own_out), (r_in, r_out) = pfs, rs
    f_in = _sum_blocks(own_in, r_in, "sum_w_in")
    f_out = _sum_blocks(own_out, r_out, "sum_w_out")
    o_in, o_out, r_pack = _swap_finished([f_in, f_out], pack)
    me8_arr = (2 * me + lax.axis_index("c")).reshape(1).astype(jnp.int32)
    small = _sum_packs(me8_arr, pack, r_pack)

    half_out = lambda a: a[0].reshape(2, D // 8, D)
    is0 = lax.axis_index("c") == 0
    g_in = jnp.concatenate([jnp.where(is0, f_in, o_in), jnp.where(is0, o_in, f_in)], axis=0)
    to_cm = lambda a: jnp.transpose(a, (2, 0, 1))
    gi, di, mi, vi = [jnp.transpose(a, (1, 2, 0))[0] for a in _adamw_rows(
        g_in.T.reshape(SHARD_COLS, 1, D), to_cm(w_in), to_cm(m_w_in), to_cm(v_w_in), "adamw_w_in")]
    go, do_, mo, vo = [a.reshape(D // 4, D) for a in _adamw_halves(
        c_arr, f_out, o_out, half_out(w_out), half_out(m_w_out), half_out(v_w_out), "adamw_w_out")]

    g_meta_full = small[64:192].reshape(N_META, D)
    g_meta_loc = lax.dynamic_slice(g_meta_full, (0, me * 256), (N_META, 256))
    gm, dm, mm_, vm = _adamw([g_meta_loc], meta_tokens, m_meta_tokens, v_meta_tokens, "adamw_meta")
    g_conv_full = small[192:240].reshape(4, 1536)
    g_conv_loc = lax.dynamic_slice(g_conv_full, (0, me * 384), (4, 384))
    gc, dc, mc, vc = _adamw([g_conv_loc], conv_w[0], m_conv_w[0], v_conv_w[0], "adamw_conv")

    reps = [(norm_w, m_norm_w, v_norm_w), (hg_lb_logits, m_hg_lb_logits, v_hg_lb_logits),
            (hg_norm_w, m_hg_norm_w, v_hg_norm_w), (gdn_A_log, m_gdn_A_log, v_gdn_A_log),
            (gdn_dt_bias, m_gdn_dt_bias, v_gdn_dt_bias), (gdn_norm_w, m_gdn_norm_w, v_gdn_norm_w),
            (final_norm_w, m_final_norm_w, v_final_norm_w)]
    wp = jnp.concatenate([_rows8(t[0]) for t in reps], axis=0)
    mp = jnp.concatenate([_rows8(t[1]) for t in reps], axis=0)
    vp = jnp.concatenate([_rows8(t[2]) for t in reps], axis=0)
    gr, dr, mr, vr = _adamw([small[8:64]], wp, mp, vp, "adamw_small")

    def unpack(p):
        outs = []
        for i, t in enumerate(reps):
            n = t[0].size
            outs.append(p[8 * i:8 * i + 8].reshape(-1)[:n].reshape(t[0].shape))
        return outs

    def leaves(meta_v, conv_v, in_v, out_v, rep_p):
        nw, lb, hgw, al, db, gdw, fwv = unpack(rep_p)
        return [meta_v, nw, in_v[None], conv_v[None], lb, hgw, al, db, gdw, out_v[None], fwv]

    loss = small[0, 0]
    return (loss, grad_x, *leaves(gm, gc, gi, go, gr), *leaves(dm, dc, di, do_, dr),
            *leaves(mm_, mc, mi, mo, mr), *leaves(vm, vc, vi, vo, vr))
```

```python
import functools

import jax
import jax.numpy as jnp
from jax import lax
from jax.experimental import pallas as pl
from jax.experimental.pallas import tpu as pltpu

f32 = jnp.float32
bf16 = jnp.bfloat16
MESH = pl.DeviceIdType.MESH
ANY = pl.BlockSpec(memory_space=pl.ANY)

D = 1024
NB = 2
N_META = 16
SEQ = 2048
PAD = 48
TP = PAD + N_META + SEQ
C = 64
NCH = TP // C
N = NB * TP
H = 4
DK = 128
HD = H * DK
PC = 4224
IN_COLS = 4104
SHARD_COLS = IN_COLS // 4
COL_HG, COL_ZHG, COL_QKV, COL_ZGD, COL_AB = 0, 3 * HD, 4 * HD, 7 * HD, 8 * HD
EPS = 1e-6
ADAM_LR, ADAM_B1, ADAM_B2, ADAM_EPS, ADAM_WD, ADAM_STEP = 0.001, 0.9, 0.999, 1e-08, 0.01, 10
VMEM_LIMIT = 56 * 1024 * 1024

P_HG = dict(lvl=1, av=1, qs=1, su=1)
P_GD = dict(kk=1, inv=1, sol=1, ws=1, qk=1, o=1, su=1)


def _cp(sem=None, **kw):
    return pltpu.CompilerParams(dimension_semantics=sem, vmem_limit_bytes=VMEM_LIMIT, **kw)


_DIMS = {"nn": (((1,), (0,)), ((), ())), "nt": (((1,), (1,)), ((), ())), "tn": (((0,), (0,)), ((), ()))}


def _split(x):
    hi = x.astype(bf16)
    return hi, (x - hi.astype(f32)).astype(bf16)


def _dg(a, b, kind, passes):
    d = lambda x, y: lax.dot_general(x, y, _DIMS[kind], preferred_element_type=f32)
    if passes == 1:
        return d(a.astype(bf16), b.astype(bf16))
    ah, al = _split(a)
    bh, bl = _split(b)
    return d(ah, bh) + d(ah, bl) + d(al, bh)


@functools.partial(jax.custom_vjp, nondiff_argnums=(2, 3))
def mmx(a, b, kind, passes):
    return _dg(a, b, kind, passes)


def _mmx_fwd(a, b, kind, passes):
    return _dg(a, b, kind, passes), (a, b)


def _mmx_bwd(kind, passes, res, g):
    a, b = res
    if kind == "nn":
        return _dg(g, b, "nt", passes), _dg(a, g, "tn", passes)
    if kind == "nt":
        return _dg(g, b, "nn", passes), _dg(g, a, "tn", passes)
    return _dg(b, g, "nt", passes), _dg(a, g, "nn", passes)


mmx.defvjp(_mmx_fwd, _mmx_bwd)


def _mask_dg(mask, x):
    xh, xl = _split(x)
    return jnp.dot(jnp.concatenate([mask, mask], axis=1), jnp.concatenate([xh, xl], axis=0), preferred_element_type=f32)


@functools.partial(jax.custom_vjp, nondiff_argnums=(2,))
def mask_mm(mask, x, bwd_passes):
    return _mask_dg(mask, x)


def _mask_fwd(mask, x, bwd_passes):
    return _mask_dg(mask, x), mask


def _mask_bwd(bwd_passes, mask, g):
    d = lambda y: lax.dot_general(mask, y, _DIMS["tn"], preferred_element_type=f32)
    if bwd_passes == 1:
        return None, d(g.astype(bf16))
    gh, gl = _split(g)
    return None, d(gh) + d(gl)


mask_mm.defvjp(_mask_fwd, _mask_bwd)


def bdot(a, b):
    return jnp.dot(a.astype(bf16), b.astype(bf16), preferred_element_type=f32)


def bdot_nt(a, b):
    return lax.dot_general(a.astype(bf16), b.astype(bf16), _DIMS["nt"], preferred_element_type=f32)


def bdot_tn(a, b):
    return lax.dot_general(a.astype(bf16), b.astype(bf16), _DIMS["tn"], preferred_element_type=f32)


def _iota2(n, m):
    return lax.broadcasted_iota(jnp.int32, (n, m), 0), lax.broadcasted_iota(jnp.int32, (n, m), 1)


sigmoid = jax.nn.sigmoid


def silu(x):
    return x * sigmoid(x)


def softplus(x):
    return jnp.maximum(x, 0.0) + jnp.log(1.0 + jnp.exp(-jnp.abs(x)))


def rmsnorm(x, w):
    return x * lax.rsqrt(jnp.mean(x * x, axis=-1, keepdims=True) + EPS) * w


def hg_masks():
    t, r = _iota2(C, C)
    mats = [r <= t, r > t]
    lvl = []
    for l in range(1, 7):
        sz = 1 << l
        half = sz >> 1
        seg_t = t >> l
        upper_t = (t & (sz - 1)) >= half
        mid_t = seg_t * sz + half - 1
        mats.append((upper_t & (r > mid_t) & (r <= t)) | ((~upper_t) & (r > t) & (r <= mid_t)))
        lvl.append(((seg_t == (r >> l)) & upper_t & ((r & (sz - 1)) < half)).astype(f32))
    stk = jnp.concatenate([m.astype(bf16) for m in mats], axis=0)
    return stk, lvl, (t == r).astype(f32)


def _head(a, h):
    return a[:, h * DK:(h + 1) * DK]


def _run(*gens):
    results = [None] * len(gens)
    live = list(range(len(gens)))
    while live:
        for i in list(live):
            try:
                next(gens[i])
            except StopIteration as e:
                results[i] = e.value
                live.remove(i)
    return results


def hg_chunk(St, ps, l0, l1):
    return _run(hg_stages(St, ps, l0, l1))[0]


def gd_chunk(S, cs, abs_, alog, dtb, t_saved=None):
    return _run(gd_stages(S, cs, abs_, alog, dtb, t_saved))[0]


def mix_chunk(St, ps, l0, l1, S, cs, abs_, alog, dtb):
    (sn_h, o_h), (sn_g, o_g, t_pack) = _run(hg_stages(St, ps, l0, l1), gd_stages(S, cs, abs_, alog, dtb))
    return sn_h, o_h, sn_g, o_g, t_pack


def hg_stages(St, ps, l0, l1):
    m = jnp.maximum(l0, l1)
    e0 = jnp.exp(l0 - m)
    e1 = jnp.exp(l1 - m)
    lb = e0 / (e0 + e1)
    stk, lvl, eye = hg_masks()
    msk = [eye] + lvl
    qs, ks, vs, qG, kR, eGl = [], [], [], [], [], []
    for p in ps:
        pq, pf, v = p[:, 0:HD], p[:, HD:2 * HD], p[:, 2 * HD:3 * HD]
        q = silu(pq)
        f = lb + (1.0 - lb) * sigmoid(pf)
        k = 1.0 - f
        logf = jnp.log(f)
        Dm = mask_mm(stk, logf, 1)
        ex = [jnp.exp(Dm[(2 + i) * C:(3 + i) * C]) for i in range(6)]
        qs.append([q] + [q * e for e in ex])
        ks.append([k] + [k * e for e in ex])
        vs.append(v)
        qG.append(q * jnp.exp(Dm[0:C]))
        kR.append(k * jnp.exp(Dm[C:2 * C]))
        eGl.append(jnp.exp(jnp.sum(logf, axis=0, keepdims=True)))
    yield
    units = [(b, h) for b in range(len(ps)) for h in range(H)]
    parts = []
    for i in range(7):
        parts.append([msk[i] * mmx(_head(qs[b][i], h), _head(ks[b][i], h), "nt", P_HG["lvl"]) for b, h in units])
        yield
    A = [functools.reduce(lambda x, y: x + y, [parts[i][n] for i in range(7)]) for n in range(len(units))]
    qS = [mmx(_head(qG[b], h), St[n], "nt", P_HG["qs"]) for n, (b, h) in enumerate(units)]
    Sn = [St[n] * _head(eGl[b], h) + mmx(_head(vs[b], h), _head(kR[b], h), "tn", P_HG["su"])
          for n, (b, h) in enumerate(units)]
    yield
    outs = [mmx(A[n], _head(vs[b], h), "nn", P_HG["av"]) + qS[n] for n, (b, h) in enumerate(units)]
    return tuple(Sn), tuple(jnp.concatenate(outs[b * H:(b + 1) * H], axis=1) for b in range(len(ps)))


@jax.custom_vjp
def use_inverse(A, T):
    return T


def _use_inverse_fwd(A, T):
    return T, T


def _use_inverse_bwd(T, g):
    return -_dg(T, _dg(g, T, "nt", P_GD["inv"]), "tn", P_GD["inv"]), jnp.zeros_like(T)


use_inverse.defvjp(_use_inverse_fwd, _use_inverse_bwd)


def gd_stages(S, cs, abs_, alog, dtb, t_saved=None):
    t, r = _iota2(C, C)
    tri = (r <= t).astype(bf16)
    ups = (r > t).astype(bf16)
    lane = lax.broadcasted_iota(jnp.int32, (1, DK), 1)
    subl = lax.broadcasted_iota(jnp.int32, (8, 1), 0)
    eye = (t == r).astype(f32)
    strict = (r < t).astype(f32)
    bd = ((t >> 4) == (r >> 4)).astype(f32)
    qa, ka, va, b4, gam4, grev4, gam4T, glast4 = [], [], [], [], [], [], [], []
    for c, ab in zip(cs, abs_):
        qa.append(silu(c[:, 0:HD]))
        ka.append(silu(c[:, HD:2 * HD]))
        va.append(silu(c[:, 2 * HD:3 * HD]))
        g4 = -jnp.exp(alog) * softplus(ab + dtb)
        b4.append(sigmoid(ab))
        gam4.append(mask_mm(tri, g4, 2))
        grev4.append(mask_mm(ups, g4, 2))
        gam4T.append(gam4[-1].T)
        glast4.append(jnp.sum(g4, axis=0, keepdims=True))
    yield
    units = [(b, h) for b in range(len(cs)) for h in range(H)]
    nu = range(len(units))
    inv = lambda a, b: [mmx(a[n], b[n], "nn", P_GD["inv"]) for n in nu]
    v = [_head(va[b], h) for b, h in units]
    q = [_head(qa[b], h) for b, h in units]
    k = [_head(ka[b], h) for b, h in units]
    q = [x * lax.rsqrt(jnp.sum(x * x, -1, keepdims=True) + EPS) * (DK ** -0.5) for x in q]
    k = [x * lax.rsqrt(jnp.sum(x * x, -1, keepdims=True) + EPS) for x in k]
    oh = [(lane == h).astype(f32) for h in range(H)]
    gam_c = [jnp.sum(gam4[b] * oh[h], -1, keepdims=True) for b, h in units]
    grev_c = [jnp.sum(grev4[b] * oh[h], -1, keepdims=True) for b, h in units]
    beta = [jnp.sum(b4[b] * (lane == h + H).astype(f32), -1, keepdims=True) for b, h in units]
    glast = [jnp.sum(glast4[b] * oh[h], -1, keepdims=True) for b, h in units]
    gam_r = [jnp.sum(gam4T[b][0:8, :] * (subl == h).astype(f32), axis=0, keepdims=True) for b, h in units]
    dec = [jnp.exp(jnp.where(r <= t, gam_c[n] - gam_r[n], -1e30)) for n in nu]
    egam = [jnp.exp(gam_c[n]) for n in nu]
    kk = [mmx(k[n], k[n], "nt", P_GD["kk"]) for n in nu]
    qk = [mmx(q[n], k[n], "nt", P_GD["qk"]) * dec[n] for n in nu]
    yield
    A = [beta[n] * kk[n] * dec[n] * strict for n in nu]
    Dg = [A[n] * bd for n in nu]
    L = [A[n] - Dg[n] for n in nu]
    if t_saved is None:
        ImD = [eye - Dg[n] for n in nu]
        D2 = inv(Dg, Dg)
        yield
        P1 = inv(ImD, [eye + x for x in D2])
        D4 = inv(D2, D2)
        yield
        P2 = inv(P1, [eye + x for x in D4])
        D8 = inv(D4, D4)
        yield
        M = inv(P2, [eye + x for x in D8])
        yield
        Nn = inv(M, L)
        yield
        N2 = inv(Nn, Nn)
        yield
        T1 = inv([eye - x for x in Nn], [eye + x for x in N2])
        yield
        Tinv = inv(T1, M)
        yield
    else:
        Tinv = [use_inverse(A[n], t_saved[b][:, h * DK:h * DK + C]) for n, (b, h) in enumerate(units)]
    rhs = [jnp.concatenate([beta[n] * v[n], (beta[n] * egam[n]) * k[n]], axis=1) for n in nu]
    sol = [mmx(Tinv[n], rhs[n], "nn", P_GD["sol"]) for n in nu]
    yield
    qwS = [mmx(jnp.concatenate([q[n] * egam[n], sol[n][:, DK:2 * DK]], axis=0), S[n], "nn", P_GD["ws"]) for n in nu]
    yield
    u = [sol[n][:, 0:DK] - qwS[n][C:2 * C] for n in nu]
    outs = [qwS[n][0:C] + mmx(qk[n], u[n], "nn", P_GD["o"]) for n in nu]
    Sn = [jnp.exp(glast[n]) * S[n] + mmx(k[n] * jnp.exp(grev_c[n]), u[n], "tn", P_GD["su"]) for n in nu]
    zpad = jnp.zeros((C, DK - C), f32)
    t_pack = tuple(jnp.concatenate([x for n in range(b * H, (b + 1) * H) for x in (lax.stop_gradient(Tinv[n]), zpad)],
                                   axis=1) for b in range(len(cs)))
    return tuple(Sn), tuple(jnp.concatenate(outs[b * H:(b + 1) * H], axis=1) for b in range(len(cs))), t_pack


def _in_proj(hflat, norm_w, wbig):
    tm = 384

    def body(h_ref, nw_ref, w_ref, p_ref, ut_ref):
        u = rmsnorm(h_ref[...], nw_ref[...])
        ut_ref[...] = u.T.astype(bf16)
        p_ref[...] = bdot_nt(u, w_ref[...])

    return pl.pallas_call(
        body, name="in_proj", grid=(N // tm,),
        in_specs=[pl.BlockSpec((tm, D), lambda i: (i, 0)), pl.BlockSpec((1, D), lambda i: (0, 0)),
                  pl.BlockSpec((PC, D), lambda i: (0, 0))],
        out_specs=[pl.BlockSpec((tm, PC), lambda i: (i, 0)), pl.BlockSpec((D, tm), lambda i: (0, i))],
        out_shape=[jax.ShapeDtypeStruct((N, PC), f32), jax.ShapeDtypeStruct((D, N), bf16)],
        compiler_params=_cp(("arbitrary",)),
    )(hflat, norm_w, wbig)


NU = NB * H
_REV = lambda c: NCH - 1 - c
_FWD = lambda c: c


def _tok_spec(w, ix, col=0):
    return pl.BlockSpec((NB, C, w), lambda c: (0, ix(c), col))


def _state_spec(ix):
    return pl.BlockSpec((NB, 1, H, DK, DK), lambda c: (0, ix(c), 0, 0, 0))


def _row_spec(w):
    return pl.BlockSpec((1, w), lambda c: (0, 0))


def _rows(ref):
    return tuple(ref[b] for b in range(NB))


def _hg_extra_specs(ix):
    return [_row_spec(HD), _row_spec(HD)]


def _gd_extra_specs(ix):
    return [_tok_spec(DK, ix, COL_AB // DK), _row_spec(DK), _row_spec(DK)]


def _mix_fwd(proj3, cv, l0, l1, alog, dtb):
    def body(p_ref, c_ref, ab_ref, l0_ref, l1_ref, al_ref, db_ref, oh_ref, sh_ref, og_ref, sg_ref, t_ref, sth, stg):
        @pl.when(pl.program_id(0) == 0)
        def _():
            sth[...] = jnp.zeros_like(sth)
            stg[...] = jnp.zeros_like(stg)

        Sh = tuple(sth[n] for n in range(NU))
        Sg = tuple(stg[n] for n in range(NU))
        for n in range(NU):
            sh_ref[n // H, 0, n % H] = Sh[n]
            sg_ref[n // H, 0, n % H] = Sg[n]
        snh, oh, sng, og, tp = mix_chunk(Sh, _rows(p_ref), l0_ref[...], l1_ref[...],
                                         Sg, _rows(c_ref), _rows(ab_ref), al_ref[...], db_ref[...])
        for n in range(NU):
            sth[n] = snh[n]
            stg[n] = sng[n]
        for b in range(NB):
            oh_ref[b] = oh[b]
            og_ref[b] = og[b]
            t_ref[b] = tp[b]

    tok = jax.ShapeDtypeStruct((NB, TP, HD), f32)
    st = jax.ShapeDtypeStruct((NB, NCH, H, DK, DK), f32)
    return pl.pallas_call(
        body, name="mix_fwd", grid=(NCH,),
        in_specs=[_tok_spec(3 * HD, _FWD), _tok_spec(3 * HD, _FWD), _tok_spec(DK, _FWD, COL_AB // DK),
                  _row_spec(HD), _row_spec(HD), _row_spec(DK), _row_spec(DK)],
        out_specs=[_tok_spec(HD, _FWD), _state_spec(_FWD), _tok_spec(HD, _FWD), _state_spec(_FWD), _tok_spec(HD, _FWD)],
        out_shape=[tok, st, tok, st, tok],
        scratch_shapes=[pltpu.VMEM((NU, DK, DK), f32), pltpu.VMEM((NU, DK, DK), f32)],
        compiler_params=_cp(("arbitrary",)),
    )(proj3, cv, proj3, l0, l1, alog, dtb)


def _hg_bwd(proj3, l0, l1, s_saved, do):
    def body(p_ref, l0_ref, l1_ref, s_ref, do_ref, dp_ref, dl0_ref, dl1_ref, dst):
        @pl.when(pl.program_id(0) == 0)
        def _():
            dst[...] = jnp.zeros_like(dst)
            dl0_ref[...] = jnp.zeros_like(dl0_ref)
            dl1_ref[...] = jnp.zeros_like(dl1_ref)

        S = tuple(s_ref[n // H, 0, n % H] for n in range(NU))
        _, vjp = jax.vjp(hg_chunk, S, _rows(p_ref), l0_ref[...], l1_ref[...])
        dS, dp, dl0, dl1 = vjp((tuple(dst[n] for n in range(NU)), _rows(do_ref)))
        for n in range(NU):
            dst[n] = dS[n]
        for b in range(NB):
            dp_ref[b] = dp[b].astype(bf16)
        dl0_ref[...] += jnp.broadcast_to(dl0, (8, HD))
        dl1_ref[...] += jnp.broadcast_to(dl1, (8, HD))

    acc = pl.BlockSpec((8, HD), lambda c: (0, 0))
    return pl.pallas_call(
        body, name="hg_bwd", grid=(NCH,),
        in_specs=[_tok_spec(3 * HD, _REV)] + _hg_extra_specs(_REV) + [_state_spec(_REV), _tok_spec(HD, _REV)],
        out_specs=[_tok_spec(3 * HD, _REV), acc, acc],
        out_shape=[jax.ShapeDtypeStruct((NB, TP, 3 * HD), bf16), jax.ShapeDtypeStruct((8, HD), f32),
                   jax.ShapeDtypeStruct((8, HD), f32)],
        scratch_shapes=[pltpu.VMEM((NU, DK, DK), f32)],
        compiler_params=_cp(("arbitrary",)),
    )(proj3, l0, l1, s_saved, do)


def _gd_bwd(cv, proj3, alog, dtb, s_saved, t_saved, do):
    def body(c_ref, ab_ref, al_ref, db_ref, s_ref, t_ref, do_ref, dc_ref, dab_ref, dal_ref, ddb_ref, dst):
        @pl.when(pl.program_id(0) == 0)
        def _():
            dst[...] = jnp.zeros_like(dst)
            dal_ref[...] = jnp.zeros_like(dal_ref)
            ddb_ref[...] = jnp.zeros_like(ddb_ref)

        S = tuple(s_ref[n // H, 0, n % H] for n in range(NU))
        t_rows = _rows(t_ref)
        fn = lambda *a: gd_chunk(*a, t_saved=t_rows)[0:2]
        _, vjp = jax.vjp(fn, S, _rows(c_ref), _rows(ab_ref), al_ref[...], db_ref[...])
        dS, dc, dab, dal, ddb = vjp((tuple(dst[n] for n in range(NU)), _rows(do_ref)))
        for n in range(NU):
            dst[n] = dS[n]
        for b in range(NB):
            dc_ref[b] = dc[b]
            dab_ref[b] = dab[b].astype(bf16)
        dal_ref[...] += jnp.broadcast_to(dal, (8, DK))
        ddb_ref[...] += jnp.broadcast_to(ddb, (8, DK))

    acc = pl.BlockSpec((8, DK), lambda c: (0, 0))
    return pl.pallas_call(
        body, name="gd_bwd", grid=(NCH,),
        in_specs=[_tok_spec(3 * HD, _REV)] + _gd_extra_specs(_REV)
        + [_state_spec(_REV), _tok_spec(HD, _REV), _tok_spec(HD, _REV)],
        out_specs=[_tok_spec(3 * HD, _REV), _tok_spec(DK, _REV), acc, acc],
        out_shape=[jax.ShapeDtypeStruct((NB, TP, 3 * HD), f32), jax.ShapeDtypeStruct((NB, TP, DK), bf16),
                   jax.ShapeDtypeStruct((8, DK), f32), jax.ShapeDtypeStruct((8, DK), f32)],
        scratch_shapes=[pltpu.VMEM((NU, DK, DK), f32)],
        compiler_params=_cp(("arbitrary",)),
    )(cv, proj3, alog, dtb, s_saved, t_saved, do)


def _conv_fwd(proj3, conv4):
    def body(x_ref, w_ref, y_ref):
        x = x_ref[0]
        y = w_ref[3] * x
        for s in (1, 2, 3):
            y = y + w_ref[3 - s] * pltpu.roll(x, s, 0)
        y_ref[0] = y
        y_ref[0, 0:8, :] = jnp.zeros((8, HD), f32)

    return pl.pallas_call(
        body, name="conv_fwd", grid=(NB, 3),
        in_specs=[pl.BlockSpec((1, TP, HD), lambda b, j: (b, 0, COL_QKV // HD + j)),
                  pl.BlockSpec((4, 1, HD), lambda b, j: (0, 0, j))],
        out_specs=pl.BlockSpec((1, TP, HD), lambda b, j: (b, 0, j)),
        out_shape=jax.ShapeDtypeStruct((NB, TP, 3 * HD), f32),
        compiler_params=_cp(("arbitrary", "arbitrary")),
    )(proj3, conv4)


def _conv_bwd(proj3, conv4, dy):
    def body(x_ref, w_ref, dy_ref, dx_ref, dw_ref):
        @pl.when(pl.program_id(1) == 0)
        def _():
            dw_ref[...] = jnp.zeros_like(dw_ref)

        x = x_ref[0]
        row = lax.broadcasted_iota(jnp.int32, (TP, 1), 0)
        g = jnp.where(row >= 8, dy_ref[0], 0.0)
        dx = w_ref[3] * g
        dw_ref[3] += jnp.broadcast_to(jnp.sum(x * g, axis=0, keepdims=True), (8, HD))
        for s in (1, 2, 3):
            dx = dx + w_ref[3 - s] * pltpu.roll(g, TP - s, 0)
            dw_ref[3 - s] += jnp.broadcast_to(jnp.sum(pltpu.roll(x, s, 0) * g, axis=0, keepdims=True), (8, HD))
        dx_ref[0] = dx.astype(bf16)

    return pl.pallas_call(
        body, name="conv_bwd", grid=(3, NB),
        in_specs=[pl.BlockSpec((1, TP, HD), lambda j, b: (b, 0, COL_QKV // HD + j)),
                  pl.BlockSpec((4, 1, HD), lambda j, b: (0, 0, j)), pl.BlockSpec((1, TP, HD), lambda j, b: (b, 0, j))],
        out_specs=[pl.BlockSpec((1, TP, HD), lambda j, b: (b, 0, j)), pl.BlockSpec((4, 8, HD), lambda j, b: (0, 0, j))],
        out_shape=[jax.ShapeDtypeStruct((NB, TP, 3 * HD), bf16), jax.ShapeDtypeStruct((4, 8, 3 * HD), f32)],
        compiler_params=_cp(("arbitrary", "arbitrary")),
    )(proj3, conv4, dy)


def _out_loss(o_hg, o_gd, proj, hgw, gdw, wout, hflat, fw, target):
    tm = 384

    def body(ohg_ref, ogd_ref, zhg_ref, zgd_ref, hgw_ref, gdw_ref, wo_ref, h_ref, fw_ref, tg_ref,
             loss_ref, dohg_ref, dogd_ref, dzhg_ref, dzgd_ref, dh_ref, dwo_ref, dhgw_ref, dgdw_ref, dfw_ref):
        i = pl.program_id(0)

        @pl.when(i == 0)
        def _():
            for r in (loss_ref, dwo_ref, dhgw_ref, dgdw_ref, dfw_ref):
                r[...] = jnp.zeros_like(r)

        row = i * tm + lax.broadcasted_iota(jnp.int32, (tm, 1), 0)
        tok = jnp.where(row >= TP, row - TP, row)
        valid = (tok >= PAD + N_META).astype(f32)
        hval = h_ref[...]
        tgt = tg_ref[...]

        mixers = ((ohg_ref, zhg_ref, hgw_ref[...]), (ogd_ref, zgd_ref, gdw_ref[...]))
        saved, ys = [], []
        for o_ref, z_ref, w in mixers:
            for hh in range(H):
                sl = slice(hh * DK, (hh + 1) * DK)
                o, z = o_ref[:, sl], z_ref[:, sl]
                r = lax.rsqrt(jnp.mean(o * o, axis=-1, keepdims=True) + EPS)
                n = o * r
                sg = sigmoid(z)
                ws = w * (z * sg)
                saved.append((r, n, sg, z, ws, w))
                ys.append(n * ws)
        y = jnp.concatenate(ys, axis=-1)
        h2 = hval + bdot(y, wo_ref[...])
        r2 = lax.rsqrt(jnp.mean(h2 * h2, axis=-1, keepdims=True) + EPS)
        n2 = h2 * r2
        fwv = fw_ref[...]
        err = (n2 * fwv - tgt) * valid
        loss = (0.5 / D) * jnp.sum(err * err)
        dyf = err * (1.0 / D)
        dn2 = dyf * fwv
        dout = r2 * (dn2 - n2 * jnp.mean(dn2 * n2, axis=-1, keepdims=True))
        dh_ref[...] = dout
        dy = bdot_nt(dout, wo_ref[...])
        dwo_ref[...] += bdot_tn(y, dout)
        dws = []
        for mi, (do_ref, dz_ref) in enumerate(((dohg_ref, dzhg_ref), (dogd_ref, dzgd_ref))):
            dw = jnp.zeros((1, DK), f32)
            for hh in range(H):
                sl = slice(hh * DK, (hh + 1) * DK)
                r, n, sg, z, ws, w = saved[mi * H + hh]
                dyh = dy[:, mi * HD + hh * DK:mi * HD + (hh + 1) * DK]
                t = dyh * n
                dw = dw + jnp.sum(t * (z * sg), axis=0, keepdims=True)
                dz_ref[:, sl] = (t * w * (sg * (1.0 + z * (1.0 - sg)))).astype(bf16)
                dn = dyh * ws
                do_ref[:, sl] = r * (dn - n * jnp.mean(dn * n, axis=-1, keepdims=True))
            dws.append(dw)
        loss_ref[...] += jnp.broadcast_to(loss, (8, DK))
        dhgw_ref[...] += jnp.broadcast_to(dws[0], (8, DK))
        dgdw_ref[...] += jnp.broadcast_to(dws[1], (8, DK))
        dfw_ref[...] += jnp.broadcast_to(jnp.sum(dyf * n2, axis=0, keepdims=True), (8, D))

    row = lambda w: pl.BlockSpec((tm, w), lambda i: (i, 0))
    whole = lambda r, w: pl.BlockSpec((r, w), lambda i: (0, 0))
    col = lambda c0: pl.BlockSpec((tm, HD), lambda i: (i, c0 // HD))
    return pl.pallas_call(
        body, name="out_loss", grid=(N // tm,),
        in_specs=[row(HD), row(HD), col(COL_ZHG), col(COL_ZGD),
                  whole(1, DK), whole(1, DK), whole(D, D), row(D), whole(1, D), row(D)],
        out_specs=[whole(8, DK), row(HD), row(HD), row(HD), row(HD), row(D), whole(D, D),
                   whole(8, DK), whole(8, DK), whole(8, D)],
        out_shape=[jax.ShapeDtypeStruct((8, DK), f32)] + [jax.ShapeDtypeStruct((N, HD), f32)] * 2
        + [jax.ShapeDtypeStruct((N, HD), bf16)] * 2
        + [jax.ShapeDtypeStruct((N, D), f32), jax.ShapeDtypeStruct((D, D), f32),
           jax.ShapeDtypeStruct((8, DK), f32), jax.ShapeDtypeStruct((8, DK), f32), jax.ShapeDtypeStruct((8, D), f32)],
        compiler_params=_cp(("arbitrary",)),
    )(o_hg, o_gd, proj, proj, hgw, gdw, wout, hflat, fw, target)


def _in_bwd(pieces, wbig, hflat, norm_w, dh_res, pbs):
    tm = 384
    nsteps = N // tm
    np_ = len(pieces)
    na = len(pbs)
    offs = [c0 for _, c0 in pieces]
    widths = [d.shape[1] for d, _ in pieces]

    def body(*refs):
        d_refs = refs[:np_]
        w_ref, h_ref, nw_ref, dhr_ref = refs[np_:np_ + 4]
        srcs = refs[np_ + 4:np_ + 4 + na]
        dh_ref, dnw_ref = refs[np_ + 4 + na:np_ + 6 + na]
        dsts = refs[np_ + 6 + na:np_ + 6 + 2 * na]
        sems = refs[np_ + 6 + 2 * na:]
        i = pl.program_id(0)

        def copies():
            if not na:
                return []
            x, y, c, chips = _place()
            return [pltpu.make_async_remote_copy(
                src_ref=srcs[a].at[2 * px + py], dst_ref=dsts[a].at[j], send_sem=sems[0].at[na * j + a],
                recv_sem=sems[1].at[na * j + a], device_id=(px, py, c), device_id_type=MESH)
                for j, (px, py) in enumerate(chips) for a in range(na)]

        @pl.when(i == 0)
        def _():
            dnw_ref[...] = jnp.zeros_like(dnw_ref)
            for cp in copies():
                cp.start()

        du = jnp.zeros((tm, D), f32)
        for d_ref, off, wd in zip(d_refs, offs, widths):
            du = du + bdot(d_ref[...], w_ref[off:off + wd, :])
        _, vjp = jax.vjp(rmsnorm, h_ref[...], nw_ref[...])
        dh, dnw = vjp(du)
        dh_ref[...] = dh + dhr_ref[...]
        dnw_ref[...] += jnp.broadcast_to(dnw, (8, D))

        @pl.when(i == nsteps - 1)
        def _():
            for cp in copies():
                cp.wait()

    row = lambda w: pl.BlockSpec((tm, w), lambda i: (i, 0))
    return pl.pallas_call(
        body, name="in_bwd", grid=(nsteps,),
        in_specs=[row(w) for w in widths]
        + [pl.BlockSpec((PC, D), lambda i: (0, 0)), row(D), pl.BlockSpec((1, D), lambda i: (0, 0)), row(D)] + [ANY] * na,
        out_specs=[row(D), pl.BlockSpec((8, D), lambda i: (0, 0))] + [ANY] * na,
        out_shape=[jax.ShapeDtypeStruct((N, D), f32), jax.ShapeDtypeStruct((8, D), f32)]
        + [jax.ShapeDtypeStruct((3,) + p.shape[1:], p.dtype) for p in pbs],
        scratch_shapes=[pltpu.SemaphoreType.DMA((3 * na,)), pltpu.SemaphoreType.DMA((3 * na,))] if na else [],
        compiler_params=_cp(("arbitrary",)),
    )(*[d for d, _ in pieces], wbig, hflat, norm_w, dh_res, *pbs)


def _w_grad(ut, pieces):
    tk = 384
    offs = [c0 for _, c0 in pieces]
    widths = [d.shape[1] for d, _ in pieces]

    def body(u_ref, *refs):
        d_refs, o_ref = refs[:-1], refs[-1]

        @pl.when(pl.program_id(0) == 0)
        def _():
            o_ref[...] = jnp.zeros_like(o_ref)

        u = u_ref[...]
        for d_ref, off, wd in zip(d_refs, offs, widths):
            o_ref[:, off:off + wd] += jnp.dot(u, d_ref[...], preferred_element_type=f32)

    return pl.pallas_call(
        body, name="w_grad", grid=(N // tk,),
        in_specs=[pl.BlockSpec((D, tk), lambda k: (0, k))] + [pl.BlockSpec((tk, w), lambda k: (k, 0)) for w in widths],
        out_specs=pl.BlockSpec((D, PC), lambda k: (0, 0)),
        out_shape=jax.ShapeDtypeStruct((D, PC), f32),
        compiler_params=_cp(("arbitrary",)),
    )(ut, *[d for d, _ in pieces])


def _adam_math(g, w, m, v):
    m2 = ADAM_B1 * m + (1.0 - ADAM_B1) * g
    v2 = ADAM_B2 * v + (1.0 - ADAM_B2) * (g * g)
    m_hat = m2 / (1.0 - ADAM_B1 ** ADAM_STEP)
    v_hat = v2 / (1.0 - ADAM_B2 ** ADAM_STEP)
    delta = -ADAM_LR * (m_hat / (jnp.sqrt(v_hat) + ADAM_EPS) + ADAM_WD * w)
    return delta, m2, v2


def _adamw(gs, w, m, v, name):
    R, Cc = w.shape
    tr = 256 if R % 256 == 0 else R
    ng = len(gs)

    def body(*refs):
        g = refs[0][...]
        for r in refs[1:ng]:
            g = g + r[...]
        w_ref, m_ref, v_ref, g_ref, d_ref, m2_ref, v2_ref = refs[ng:]
        delta, m2, v2 = _adam_math(g, w_ref[...], m_ref[...], v_ref[...])
        g_ref[...] = g
        d_ref[...] = delta
        m2_ref[...] = m2
        v2_ref[...] = v2

    spec = pl.BlockSpec((tr, Cc), lambda i: (i, 0))
    return pl.pallas_call(
        body, name=name, grid=(R // tr,),
        in_specs=[spec] * (ng + 3), out_specs=[spec] * 4,
        out_shape=[jax.ShapeDtypeStruct((R, Cc), f32)] * 4,
        compiler_params=_cp(("arbitrary",)),
    )(*gs, w, m, v)


def _adamw_rows(g, w, m, v, name):
    R, _, Cc = w.shape
    tr = R // 9

    def body(g_ref, w_ref, m_ref, v_ref, go_ref, d_ref, m2_ref, v2_ref):
        g = g_ref[...]
        delta, m2, v2 = _adam_math(g, w_ref[...], m_ref[...], v_ref[...])
        go_ref[...] = g
        d_ref[...] = delta
        m2_ref[...] = m2
        v2_ref[...] = v2

    spec = pl.BlockSpec((tr, 1, Cc), lambda i: (i, 0, 0))
    return pl.pallas_call(
        body, name=name, grid=(R // tr,),
        in_specs=[spec] * 4, out_specs=[spec] * 4,
        out_shape=[jax.ShapeDtypeStruct((R, 1, Cc), f32)] * 4,
        compiler_params=_cp(("arbitrary",)),
    )(g, w, m, v)


def _place():
    x, y, c = lax.axis_index("x"), lax.axis_index("y"), lax.axis_index("c")
    return x, y, c, [(1 - x, y), (x, 1 - y), (1 - x, 1 - y)]


def _gather_weights(cm, halved, whole):
    R, _, Cc = cm.shape
    shards = [jax.ShapeDtypeStruct((2, R, Cc // 2), bf16)] + list(halved) + list(whole)
    nh = 1 + len(halved)
    na = len(shards)

    def body(*refs):
        srcs, dsts = refs[:na], refs[na:2 * na]
        send_sems, recv_sems, loc_sems = refs[2 * na:2 * na + 3]
        stage = refs[2 * na + 3:3 * na + 3]
        raw = refs[3 * na + 3]
        x, y, c, chips = _place()
        me = 2 * x + y
        loads = [pltpu.make_async_copy(srcs[0], raw, loc_sems.at[0])]
        loads += [pltpu.make_async_copy(srcs[i], stage[i], loc_sems.at[i]) for i in range(1, na)]
        locs = [pltpu.make_async_copy(v, d.at[me], loc_sems.at[i]) for i, (v, d) in enumerate(zip(stage, dsts))]
        for cp in loads:
            cp.start()

        def ici(j, i, slot):
            px, py = chips[j]
            src = (stage[0] if i == 0 else srcs[i]).at[c] if i < nh else srcs[i]
            dst = dsts[i].at[slot, c] if i < nh else dsts[i].at[slot]
            return pltpu.make_async_remote_copy(
                src_ref=src, dst_ref=dst, send_sem=send_sems.at[na * j + i], recv_sem=recv_sems.at[na * j + i],
                device_id=(px, py, c), device_id_type=MESH)

        def d2d(j, i, half):
            px, py = chips[j]
            blk = dsts[i].at[2 * px + py, half]
            return pltpu.make_async_remote_copy(
                src_ref=blk, dst_ref=blk, send_sem=send_sems.at[3 * na + nh * j + i],
                recv_sem=recv_sems.at[3 * na + nh * j + i], device_id=(x, y, 1 - c), device_id_type=MESH)

        sends = [ici(j, i, me) for j in range(3) for i in range(1, na)]
        for cp in sends:
            cp.start()
        loads[0].wait()
        v = raw[:, 0, :]
        for hc in range(2):
            stage[0][hc] = v[:, hc * (Cc // 2):(hc + 1) * (Cc // 2)].astype(bf16)
        first = [ici(j, 0, me) for j in range(3)]
        for cp in first:
            cp.start()
        sends += first
        locs[0].start()
        for ld, st in zip(loads[1:], locs[1:]):
            ld.wait()
            st.start()
        for j, (px, py) in enumerate(chips):
            for i in range(na):
                ici(j, i, 2 * px + py).wait_recv()
                if i < nh:
                    fwd = d2d(j, i, c)
                    fwd.start()
                    sends.append(fwd)
        for j in range(3):
            for i in range(nh):
                d2d(j, i, 1 - c).wait_recv()
        for cp in sends:
            cp.wait_send()
        for cp in locs:
            cp.wait()

    nsem = 3 * na + 3 * nh
    return pl.pallas_call(
        body, name="gather_weights",
        in_specs=[ANY] * na, out_specs=[ANY] * na,
        out_shape=[jax.ShapeDtypeStruct((4,) + s.shape, s.dtype) for s in shards],
        scratch_shapes=[pltpu.SemaphoreType.DMA((nsem,)), pltpu.SemaphoreType.DMA((nsem,)),
                        pltpu.SemaphoreType.DMA((na,))] + [pltpu.VMEM(s.shape, s.dtype) for s in shards]
        + [pltpu.VMEM(cm.shape, cm.dtype)],
        compiler_params=pltpu.CompilerParams(has_side_effects=True, vmem_limit_bytes=VMEM_LIMIT),
    )(cm, *halved, *whole)


def _swap_halves(gs):
    na = len(gs)
    jobs = [(i, q) for i in range(na) for q in range(gs[i].shape[0])]

    def body(*refs):
        srcs, dsts = refs[:na], refs[na:2 * na]
        send_sems, recv_sems = refs[2 * na:]
        x, y, c, _ = _place()
        cps = [pltpu.make_async_remote_copy(
            src_ref=srcs[i].at[q, 1 - c], dst_ref=dsts[i].at[q], send_sem=send_sems.at[k],
            recv_sem=recv_sems.at[k], device_id=(x, y, 1 - c), device_id_type=MESH)
            for k, (i, q) in enumerate(jobs)]
        for cp in cps:
            cp.start()
        for cp in cps:
            cp.wait()

    return pl.pallas_call(
        body, name="swap_halves",
        in_specs=[ANY] * na, out_specs=[ANY] * na,
        out_shape=[jax.ShapeDtypeStruct(g.shape[0:1] + g.shape[2:], g.dtype) for g in gs],
        scratch_shapes=[pltpu.SemaphoreType.DMA((len(jobs),)), pltpu.SemaphoreType.DMA((len(jobs),))],
        compiler_params=pltpu.CompilerParams(has_side_effects=True),
    )(*gs)


def _add_halves(c_arr, g, s, name):
    Q, _, R, Cc = g.shape
    tr = min(R, 128)

    def body(c_ref, g_ref, s_ref, b_ref, f_ref):
        p = g_ref[0, 0] + s_ref[0]
        f_ref[0] = p
        b_ref[0] = p.astype(bf16)

    blk = pl.BlockSpec((1, tr, Cc), lambda q, i, cr: (q, i, 0))
    return pl.pallas_call(
        body, name=name,
        grid_spec=pltpu.PrefetchScalarGridSpec(
            num_scalar_prefetch=1, grid=(Q, R // tr),
            in_specs=[pl.BlockSpec((1, 1, tr, Cc), lambda q, i, cr: (q, cr[0], i, 0)), blk], out_specs=[blk, blk]),
        out_shape=[jax.ShapeDtypeStruct((Q, R, Cc), bf16), jax.ShapeDtypeStruct((Q, R, Cc), f32)],
        compiler_params=_cp(("arbitrary", "arbitrary")),
    )(c_arr, g, s)


_FLIPS = [(fx, fy, fc) for fx in (0, 1) for fy in (0, 1) for fc in (0, 1)][1:]


def _sum_blocks(own, r, name):
    R, Cc = own.shape
    tr = min(R, 256)

    def body(own_ref, r_ref, o_ref):
        acc = own_ref[...]
        for j in range(3):
            acc = acc + r_ref[j].astype(f32)
        o_ref[...] = acc

    return pl.pallas_call(
        body, name=name, grid=(R // tr,),
        in_specs=[pl.BlockSpec((tr, Cc), lambda i: (i, 0)), pl.BlockSpec((3, tr, Cc), lambda i: (0, i, 0))],
        out_specs=pl.BlockSpec((tr, Cc), lambda i: (i, 0)),
        out_shape=jax.ShapeDtypeStruct((R, Cc), f32),
        compiler_params=_cp(("arbitrary",)),
    )(own, r)


def _sum_packs(me8_arr, pack, rp):
    R = pack.shape[0]

    def body(me_ref, pk_ref, rp_ref, o_ref):
        me8 = me_ref[0]
        acc = None
        for d in range(8):
            rel = d ^ me8
            term = jnp.where(rel == 0, pk_ref[...], rp_ref[jnp.maximum(rel - 1, 0)])
            acc = term if acc is None else acc + term
        o_ref[...] = acc

    return pl.pallas_call(
        body, name="sum_packs",
        grid_spec=pltpu.PrefetchScalarGridSpec(
            num_scalar_prefetch=1, grid=(1,),
            in_specs=[pl.BlockSpec((R, 128), lambda i, mr: (0, 0)), pl.BlockSpec((7, R, 128), lambda i, mr: (0, 0, 0))],
            out_specs=pl.BlockSpec((R, 128), lambda i, mr: (0, 0))),
        out_shape=jax.ShapeDtypeStruct((R, 128), f32),
        compiler_params=_cp(("arbitrary",)),
    )(me8_arr, pack, rp)


def _swap_finished(fs, pack):
    na = len(fs)
    R = pack.shape[0]

    def body(*refs):
        srcs, pk = refs[:na], refs[na]
        dsts, rp = refs[na + 1:2 * na + 1], refs[2 * na + 1]
        send_sems, recv_sems = refs[2 * na + 2:]
        x, y, c, _ = _place()
        cps = [pltpu.make_async_remote_copy(
            src_ref=srcs[i], dst_ref=dsts[i], send_sem=send_sems.at[i], recv_sem=recv_sems.at[i],
            device_id=(x, y, 1 - c), device_id_type=MESH) for i in range(na)]
        cps += [pltpu.make_async_remote_copy(
            src_ref=pk, dst_ref=rp.at[k], send_sem=send_sems.at[na + k], recv_sem=recv_sems.at[na + k],
            device_id=(x ^ fx, y ^ fy, c ^ fc), device_id_type=MESH) for k, (fx, fy, fc) in enumerate(_FLIPS)]
        for cp in cps:
            cp.start()
        for cp in cps:
            cp.wait()

    return pl.pallas_call(
        body, name="swap_finished",
        in_specs=[ANY] * (na + 1), out_specs=[ANY] * (na + 1),
        out_shape=[jax.ShapeDtypeStruct(f.shape, f.dtype) for f in fs] + [jax.ShapeDtypeStruct((7, R, 128), f32)],
        scratch_shapes=[pltpu.SemaphoreType.DMA((na + 7,)), pltpu.SemaphoreType.DMA((na + 7,))],
        compiler_params=pltpu.CompilerParams(has_side_effects=True),
    )(*fs, pack)


def _adamw_halves(c_arr, mine, peer, w, m, v, name):
    _, R, Cc = w.shape
    tr = min(R, 256)

    def body(c_ref, mine_ref, peer_ref, w_ref, m_ref, v_ref, g_ref, d_ref, m2_ref, v2_ref):
        g = jnp.where(pl.program_id(0) == c_ref[0], mine_ref[...], peer_ref[...])
        delta, m2, v2 = _adam_math(g, w_ref[0], m_ref[0], v_ref[0])
        g_ref[0] = g
        d_ref[0] = delta
        m2_ref[0] = m2
        v2_ref[0] = v2

    half = pl.BlockSpec((tr, Cc), lambda hh, i, cr: (i, 0))
    full = pl.BlockSpec((1, tr, Cc), lambda hh, i, cr: (hh, i, 0))
    return pl.pallas_call(
        body, name=name,
        grid_spec=pltpu.PrefetchScalarGridSpec(
            num_scalar_prefetch=1, grid=(2, R // tr), in_specs=[half, half, full, full, full], out_specs=[full] * 4),
        out_shape=[jax.ShapeDtypeStruct((2, R, Cc), f32)] * 4,
        compiler_params=_cp(("arbitrary", "arbitrary")),
    )(c_arr, mine, peer, w, m, v)


def _rows8(a):
    flat = a.reshape(-1)
    n = flat.shape[0]
    rows = -(-n // 1024) * 8
    return jnp.pad(flat, (0, rows * 128 - n)).reshape(rows, 128)


def kernel(x, meta_tokens, norm_w, w_in, conv_w, hg_lb_logits, hg_norm_w, gdn_A_log, gdn_dt_bias, gdn_norm_w, w_out, final_norm_w, loss_target, m_meta_tokens, m_norm_w, m_w_in, m_conv_w, m_hg_lb_logits, m_hg_norm_w, m_gdn_A_log, m_gdn_dt_bias, m_gdn_norm_w, m_w_out, m_final_norm_w, v_meta_tokens, v_norm_w, v_w_in, v_conv_w, v_hg_lb_logits, v_hg_norm_w, v_gdn_A_log, v_gdn_dt_bias, v_gdn_norm_w, v_w_out, v_final_norm_w):
    me = 2 * lax.axis_index("x") + lax.axis_index("y")

    g_win, g_wout, g_conv, g_meta = _gather_weights(
        jnp.transpose(w_in, (2, 0, 1)), [w_out[0].astype(bf16).reshape(2, D // 8, D)], [conv_w[0], meta_tokens])
    wt_full = jnp.transpose(g_win, (0, 2, 1, 3)).reshape(IN_COLS, D)
    wbig = jnp.pad(wt_full, ((0, PC - IN_COLS), (0, 0)))
    wout_full = g_wout.reshape(D, D)
    conv4 = jnp.transpose(g_conv, (1, 0, 2)).reshape(4, 1, 3 * HD)
    meta_full = jnp.transpose(g_meta, (1, 0, 2)).reshape(N_META, D)

    c_arr = lax.axis_index("c").reshape(1).astype(jnp.int32)

    def chip_partials(gw, g_wout_part):
        g_in2 = gw.reshape(1, 2, D // 2, PC)
        g_out4 = g_wout_part.reshape(4, 2, D // 8, D)
        s_in, s_out = _swap_halves([g_in2, g_out4])
        pb_in, pf_in = _add_halves(c_arr, g_in2, s_in, "add_w_in")
        pb_out, pf_out = _add_halves(c_arr, g_out4, s_out, "add_w_out")
        pb_blocks = jnp.transpose(pb_in[0, :, 0:IN_COLS].reshape(D // 2, 4, SHARD_COLS), (1, 0, 2))
        own_in = lax.dynamic_slice(pf_in[0], (0, me * SHARD_COLS), (D // 2, SHARD_COLS))
        own_out = lax.dynamic_index_in_dim(pf_out, me, axis=0, keepdims=False)
        return [pb_blocks, pb_out], [own_in, own_out]

    (loss8, grad_x, d_meta, d_nw, d_conv, d_lb, d_hgw, d_alog, d_dtb, d_gdw, d_fw, pfs, rs) = _local_step(
        x, loss_target, wbig, wout_full, conv4, meta_full, norm_w, hg_lb_logits, hg_norm_w, gdn_A_log, gdn_dt_bias,
        gdn_norm_w, final_norm_w, chip_partials)

    pack = jnp.concatenate([
        loss8, d_nw[0].reshape(8, 128), d_lb.reshape(8, 128), d_hgw, _rows8(d_alog[0, :H]), _rows8(d_dtb[0, :H]),
        d_gdw, d_fw[0].reshape(8, 128), d_meta.reshape(128, 128), d_conv.reshape(48, 128)], axis=0)
    return _reduce_and_update(
        me, c_arr, grad_x, pfs, rs, pack, meta_tokens, norm_w, w_in, conv_w, hg_lb_logits, hg_norm_w, gdn_A_log,
        gdn_dt_bias, gdn_norm_w, w_out, final_norm_w, m_meta_tokens, m_norm_w, m_w_in, m_conv_w, m_hg_lb_logits,
        m_hg_norm_w, m_gdn_A_log, m_gdn_dt_bias, m_gdn_norm_w, m_w_out, m_final_norm_w, v_meta_tokens, v_norm_w, v_w_in,
        v_conv_w, v_hg_lb_logits, v_hg_norm_w, v_gdn_A_log, v_gdn_dt_bias, v_gdn_norm_w, v_w_out, v_final_norm_w)


def _local_step(x, loss_target, wbig, wout_full, conv4, meta_full, norm_w, hg_lb_logits, hg_norm_w, gdn_A_log, gdn_dt_bias,
                gdn_norm_w, final_norm_w, chip_partials):
    h3 = jnp.concatenate([jnp.zeros((NB, PAD, D), f32), jnp.broadcast_to(meta_full[None], (NB, N_META, D)), x], axis=1)
    hflat = h3.reshape(N, D)
    target = jnp.pad(loss_target, ((0, 0), (PAD + N_META, 0), (0, 0))).reshape(N, D)
    l0, l1 = hg_lb_logits[0:1], hg_lb_logits[1:2]
    alog = jnp.pad(gdn_A_log, ((0, 0), (0, DK - H)))
    dtb = jnp.pad(gdn_dt_bias, ((0, 0), (0, DK - H)))
    fw = final_norm_w.reshape(1, D)

    proj, ut = _in_proj(hflat, norm_w, wbig)
    proj3 = proj.reshape(NB, TP, PC)
    cv = _conv_fwd(proj3, conv4)
    o_hg, s_hg, o_gd, s_gd, t_gd = _mix_fwd(proj3, cv, l0, l1, alog, dtb)
    (loss8, d_ohg, d_ogd, d_zhg, d_zgd, dh_res, g_wout_part, d_hgw, d_gdw, d_fw) = _out_loss(
        o_hg.reshape(N, HD), o_gd.reshape(N, HD), proj, hg_norm_w, gdn_norm_w, wout_full, hflat, fw, target)
    d_hg, d_l0, d_l1 = _hg_bwd(proj3, l0, l1, s_hg, d_ohg.reshape(NB, TP, HD))
    d_cv, d_ab, d_alog, d_dtb = _gd_bwd(cv, proj3, alog, dtb, s_gd, t_gd, d_ogd.reshape(NB, TP, HD))
    d_qkv, d_conv4 = _conv_bwd(proj3, conv4, d_cv)
    d_hg2, d_qkv2, d_ab2 = d_hg.reshape(N, 3 * HD), d_qkv.reshape(N, 3 * HD), d_ab.reshape(N, DK)
    pieces = [(d_hg2, COL_HG), (d_zhg, COL_ZHG), (d_qkv2, COL_QKV), (d_zgd, COL_ZGD), (d_ab2, COL_AB)]
    gw = _w_grad(ut, pieces)
    pbs, pfs = chip_partials(gw, g_wout_part) if chip_partials else ([], [gw, g_wout_part])
    dh, d_nw, *rs = _in_bwd(pieces, wbig, hflat, norm_w, dh_res, pbs)

    dh3 = dh.reshape(NB, TP, D)
    grad_x = dh3[:, PAD + N_META:, :]
    d_meta = jnp.sum(dh3[:, PAD:PAD + N_META, :], axis=0)
    d_conv = d_conv4[:, 0, :]
    d_lb = jnp.concatenate([d_l0[0:1], d_l1[0:1]], axis=0)
    return loss8, grad_x, d_meta, d_nw, d_conv, d_lb, d_hgw, d_alog, d_dtb, d_gdw, d_fw, pfs, rs


def _reduce_and_update(me, c_arr, grad_x, pfs, rs, pack, meta_tokens, norm_w, w_in, conv_w, hg_lb_logits, hg_norm_w,
                       gdn_A_log, gdn_dt_bias, gdn_norm_w, w_out, final_norm_w, m_meta_tokens, m_norm_w, m_w_in, m_conv_w,
                       m_hg_lb_logits, m_hg_norm_w, m_gdn_A_log, m_gdn_dt_bias, m_gdn_norm_w, m_w_out, m_final_norm_w,
                       v_meta_tokens, v_norm_w, v_w_in, v_conv_w, v_hg_lb_logits, v_hg_norm_w, v_gdn_A_log, v_gdn_dt_bias,
                       v_gdn_norm_w, v_w_out, v_final_norm_w):
    (own_in, own_out), (r_in, r_out) = pfs, rs
    f_in = _sum_blocks(own_in, r_in, "sum_w_in")
    f_out = _sum_blocks(own_out, r_out, "sum_w_out")
    o_in, o_out, r_pack = _swap_finished([f_in, f_out], pack)
    me8_arr = (2 * me + lax.axis_index("c")).reshape(1).astype(jnp.int32)
    small = _sum_packs(me8_arr, pack, r_pack)

    half_out = lambda a: a[0].reshape(2, D // 8, D)
    is0 = lax.axis_index("c") == 0
    g_in = jnp.concatenate([jnp.where(is0, f_in, o_in), jnp.where(is0, o_in, f_in)], axis=0)
    to_cm = lambda a: jnp.transpose(a, (2, 0, 1))
    gi, di, mi, vi = [jnp.transpose(a, (1, 2, 0))[0] for a in _adamw_rows(
        g_in.T.reshape(SHARD_COLS, 1, D), to_cm(w_in), to_cm(m_w_in), to_cm(v_w_in), "adamw_w_in")]
    go, do_, mo, vo = [a.reshape(D // 4, D) for a in _adamw_halves(
        c_arr, f_out, o_out, half_out(w_out), half_out(m_w_out), half_out(v_w_out), "adamw_w_out")]

    g_meta_full = small[64:192].reshape(N_META, D)
    g_meta_loc = lax.dynamic_slice(g_meta_full, (0, me * 256), (N_META, 256))
    gm, dm, mm_, vm = _adamw([g_meta_loc], meta_tokens, m_meta_tokens, v_meta_tokens, "adamw_meta")
    g_conv_full = small[192:240].reshape(4, 1536)
    g_conv_loc = lax.dynamic_slice(g_conv_full, (0, me * 384), (4, 384))
    gc, dc, mc, vc = _adamw([g_conv_loc], conv_w[0], m_conv_w[0], v_conv_w[0], "adamw_conv")

    reps = [(norm_w, m_norm_w, v_norm_w), (hg_lb_logits, m_hg_lb_logits, v_hg_lb_logits),
            (hg_norm_w, m_hg_norm_w, v_hg_norm_w), (gdn_A_log, m_gdn_A_log, v_gdn_A_log),
            (gdn_dt_bias, m_gdn_dt_bias, v_gdn_dt_bias), (gdn_norm_w, m_gdn_norm_w, v_gdn_norm_w),
            (final_norm_w, m_final_norm_w, v_final_norm_w)]
    wp = jnp.concatenate([_rows8(t[0]) for t in reps], axis=0)
    mp = jnp.concatenate([_rows8(t[1]) for t in reps], axis=0)
    vp = jnp.concatenate([_rows8(t[2]) for t in reps], axis=0)
    gr, dr, mr, vr = _adamw([small[8:64]], wp, mp, vp, "adamw_small")

    def unpack(p):
        outs = []
        for i, t in enumerate(reps):
            n = t[0].size
            outs.append(p[8 * i:8 * i + 8].reshape(-1)[:n].reshape(t[0].shape))
        return outs

    def leaves(meta_v, conv_v, in_v, out_v, rep_p):
        nw, lb, hgw, al, db, gdw, fwv = unpack(rep_p)
        return [meta_v, nw, in_v[None], conv_v[None], lb, hgw, al, db, gdw, out_v[None], fwv]

    loss = small[0, 0]
    return (loss, grad_x, *leaves(gm, gc, gi, go, gr), *leaves(dm, dc, di, do_, dr),
            *leaves(mm_, mc, mi, mo, mr), *leaves(vm, vc, vi, vo, vr))
```

```python
import functools

import jax
import jax.numpy as jnp
from jax import lax
from jax.experimental import pallas as pl
from jax.experimental.pallas import tpu as pltpu

f32 = jnp.float32
bf16 = jnp.bfloat16
MESH = pl.DeviceIdType.MESH
ANY = pl.BlockSpec(memory_space=pl.ANY)

D = 1024
NB = 2
N_META = 16
SEQ = 2048
PAD = 48
TP = PAD + N_META + SEQ
C = 64
NCH = TP // C
N = NB * TP
H = 4
DK = 128
HD = H * DK
PC = 4224
IN_COLS = 4104
SHARD_COLS = IN_COLS // 4
COL_HG, COL_ZHG, COL_QKV, COL_ZGD, COL_AB = 0, 3 * HD, 4 * HD, 7 * HD, 8 * HD
EPS = 1e-6
ADAM_LR, ADAM_B1, ADAM_B2, ADAM_EPS, ADAM_WD, ADAM_STEP = 0.001, 0.9, 0.999, 1e-08, 0.01, 10
VMEM_LIMIT = 56 * 1024 * 1024

P_HG = dict(lvl=1, av=1, qs=1, su=1)
P_GD = dict(kk=1, inv=1, sol=1, ws=1, qk=1, o=1, su=1)


def _cp(sem=None, **kw):
    return pltpu.CompilerParams(dimension_semantics=sem, vmem_limit_bytes=VMEM_LIMIT, **kw)


_DIMS = {"nn": (((1,), (0,)), ((), ())), "nt": (((1,), (1,)), ((), ())), "tn": (((0,), (0,)), ((), ()))}


def _split(x):
    hi = x.astype(bf16)
    return hi, (x - hi.astype(f32)).astype(bf16)


def _dg(a, b, kind, passes):
    d = lambda x, y: lax.dot_general(x, y, _DIMS[kind], preferred_element_type=f32)
    if passes == 1:
        return d(a.astype(bf16), b.astype(bf16))
    ah, al = _split(a)
    bh, bl = _split(b)
    return d(ah, bh) + d(ah, bl) + d(al, bh)


@functools.partial(jax.custom_vjp, nondiff_argnums=(2, 3))
def mmx(a, b, kind, passes):
    return _dg(a, b, kind, passes)


def _mmx_fwd(a, b, kind, passes):
    return _dg(a, b, kind, passes), (a, b)


def _mmx_bwd(kind, passes, res, g):
    a, b = res
    if kind == "nn":
        return _dg(g, b, "nt", passes), _dg(a, g, "tn", passes)
    if kind == "nt":
        return _dg(g, b, "nn", passes), _dg(g, a, "tn", passes)
    return _dg(b, g, "nt", passes), _dg(a, g, "nn", passes)


mmx.defvjp(_mmx_fwd, _mmx_bwd)


def _mask_dg(mask, x):
    xh, xl = _split(x)
    return jnp.dot(jnp.concatenate([mask, mask], axis=1), jnp.concatenate([xh, xl], axis=0), preferred_element_type=f32)


@functools.partial(jax.custom_vjp, nondiff_argnums=(2,))
def mask_mm(mask, x, bwd_passes):
    return _mask_dg(mask, x)


def _mask_fwd(mask, x, bwd_passes):
    return _mask_dg(mask, x), mask


def _mask_bwd(bwd_passes, mask, g):
    d = lambda y: lax.dot_general(mask, y, _DIMS["tn"], preferred_element_type=f32)
    if bwd_passes == 1:
        return None, d(g.astype(bf16))
    gh, gl = _split(g)
    return None, d(gh) + d(gl)


mask_mm.defvjp(_mask_fwd, _mask_bwd)


def bdot(a, b):
    return jnp.dot(a.astype(bf16), b.astype(bf16), preferred_element_type=f32)


def bdot_nt(a, b):
    return lax.dot_general(a.astype(bf16), b.astype(bf16), _DIMS["nt"], preferred_element_type=f32)


def bdot_tn(a, b):
    return lax.dot_general(a.astype(bf16), b.astype(bf16), _DIMS["tn"], preferred_element_type=f32)


def _iota2(n, m):
    return lax.broadcasted_iota(jnp.int32, (n, m), 0), lax.broadcasted_iota(jnp.int32, (n, m), 1)


sigmoid = jax.nn.sigmoid


def silu(x):
    return x * sigmoid(x)


def softplus(x):
    return jnp.maximum(x, 0.0) + jnp.log(1.0 + jnp.exp(-jnp.abs(x)))


def rmsnorm(x, w):
    return x * lax.rsqrt(jnp.mean(x * x, axis=-1, keepdims=True) + EPS) * w


def hg_masks():
    t, r = _iota2(C, C)
    mats = [r <= t, r > t]
    lvl = []
    for l in range(1, 7):
        sz = 1 << l
        half = sz >> 1
        seg_t = t >> l
        upper_t = (t & (sz - 1)) >= half
        mid_t = seg_t * sz + half - 1
        mats.append((upper_t & (r > mid_t) & (r <= t)) | ((~upper_t) & (r > t) & (r <= mid_t)))
        lvl.append(((seg_t == (r >> l)) & upper_t & ((r & (sz - 1)) < half)).astype(f32))
    stk = jnp.concatenate([m.astype(bf16) for m in mats], axis=0)
    return stk, lvl, (t == r).astype(f32)


def _head(a, h):
    return a[:, h * DK:(h + 1) * DK]


def _run(*gens):
    results = [None] * len(gens)
    live = list(range(len(gens)))
    while live:
        for i in list(live):
            try:
                next(gens[i])
            except StopIteration as e:
                results[i] = e.value
                live.remove(i)
    return results


def hg_chunk(St, ps, l0, l1):
    return _run(hg_stages(St, ps, l0, l1))[0]


def gd_chunk(S, cs, abs_, alog, dtb, t_saved=None):
    return _run(gd_stages(S, cs, abs_, alog, dtb, t_saved))[0]


def mix_chunk(St, ps, l0, l1, S, cs, abs_, alog, dtb):
    (sn_h, o_h), (sn_g, o_g, t_pack) = _run(hg_stages(St, ps, l0, l1), gd_stages(S, cs, abs_, alog, dtb))
    return sn_h, o_h, sn_g, o_g, t_pack


def hg_stages(St, ps, l0, l1):
    m = jnp.maximum(l0, l1)
    e0 = jnp.exp(l0 - m)
    e1 = jnp.exp(l1 - m)
    lb = e0 / (e0 + e1)
    stk, lvl, eye = hg_masks()
    msk = [eye] + lvl
    qs, ks, vs, qG, kR, eGl = [], [], [], [], [], []
    for p in ps:
        pq, pf, v = p[:, 0:HD], p[:, HD:2 * HD], p[:, 2 * HD:3 * HD]
        q = silu(pq)
        f = lb + (1.0 - lb) * sigmoid(pf)
        k = 1.0 - f
        logf = jnp.log(f)
        Dm = mask_mm(stk, logf, 1)
        ex = [jnp.exp(Dm[(2 + i) * C:(3 + i) * C]) for i in range(6)]
        qs.append([q] + [q * e for e in ex])
        ks.append([k] + [k * e for e in ex])
        vs.append(v)
        qG.append(q * jnp.exp(Dm[0:C]))
        kR.append(k * jnp.exp(Dm[C:2 * C]))
        eGl.append(jnp.exp(jnp.sum(logf, axis=0, keepdims=True)))
    yield
    units = [(b, h) for b in range(len(ps)) for h in range(H)]
    parts = []
    for i in range(7):
        parts.append([msk[i] * mmx(_head(qs[b][i], h), _head(ks[b][i], h), "nt", P_HG["lvl"]) for b, h in units])
        yield
    A = [functools.reduce(lambda x, y: x + y, [parts[i][n] for i in range(7)]) for n in range(len(units))]
    qS = [mmx(_head(qG[b], h), St[n], "nt", P_HG["qs"]) for n, (b, h) in enumerate(units)]
    Sn = [St[n] * _head(eGl[b], h) + mmx(_head(vs[b], h), _head(kR[b], h), "tn", P_HG["su"])
          for n, (b, h) in enumerate(units)]
    yield
    outs = [mmx(A[n], _head(vs[b], h), "nn", P_HG["av"]) + qS[n] for n, (b, h) in enumerate(units)]
    return tuple(Sn), tuple(jnp.concatenate(outs[b * H:(b + 1) * H], axis=1) for b in range(len(ps)))


@jax.custom_vjp
def use_inverse(A, T):
    return T


def _use_inverse_fwd(A, T):
    return T, T


def _use_inverse_bwd(T, g):
    return -_dg(T, _dg(g, T, "nt", P_GD["inv"]), "tn", P_GD["inv"]), jnp.zeros_like(T)


use_inverse.defvjp(_use_inverse_fwd, _use_inverse_bwd)


def gd_stages(S, cs, abs_, alog, dtb, t_saved=None):
    t, r = _iota2(C, C)
    tri = (r <= t).astype(bf16)
    ups = (r > t).astype(bf16)
    lane = lax.broadcasted_iota(jnp.int32, (1, DK), 1)
    subl = lax.broadcasted_iota(jnp.int32, (8, 1), 0)
    eye = (t == r).astype(f32)
    strict = (r < t).astype(f32)
    bd = ((t >> 4) == (r >> 4)).astype(f32)
    qa, ka, va, b4, gam4, grev4, gam4T, glast4 = [], [], [], [], [], [], [], []
    for c, ab in zip(cs, abs_):
        qa.append(silu(c[:, 0:HD]))
        ka.append(silu(c[:, HD:2 * HD]))
        va.append(silu(c[:, 2 * HD:3 * HD]))
        g4 = -jnp.exp(alog) * softplus(ab + dtb)
        b4.append(sigmoid(ab))
        gam4.append(mask_mm(tri, g4, 2))
        grev4.append(mask_mm(ups, g4, 2))
        gam4T.append(gam4[-1].T)
        glast4.append(jnp.sum(g4, axis=0, keepdims=True))
    yield
    units = [(b, h) for b in range(len(cs)) for h in range(H)]
    nu = range(len(units))
    inv = lambda a, b: [mmx(a[n], b[n], "nn", P_GD["inv"]) for n in nu]
    v = [_head(va[b], h) for b, h in units]
    q = [_head(qa[b], h) for b, h in units]
    k = [_head(ka[b], h) for b, h in units]
    q = [x * lax.rsqrt(jnp.sum(x * x, -1, keepdims=True) + EPS) * (DK ** -0.5) for x in q]
    k = [x * lax.rsqrt(jnp.sum(x * x, -1, keepdims=True) + EPS) for x in k]
    oh = [(lane == h).astype(f32) for h in range(H)]
    gam_c = [jnp.sum(gam4[b] * oh[h], -1, keepdims=True) for b, h in units]
    grev_c = [jnp.sum(grev4[b] * oh[h], -1, keepdims=True) for b, h in units]
    beta = [jnp.sum(b4[b] * (lane == h + H).astype(f32), -1, keepdims=True) for b, h in units]
    glast = [jnp.sum(glast4[b] * oh[h], -1, keepdims=True) for b, h in units]
    gam_r = [jnp.sum(gam4T[b][0:8, :] * (subl == h).astype(f32), axis=0, keepdims=True) for b, h in units]
    dec = [jnp.exp(jnp.where(r <= t, gam_c[n] - gam_r[n], -1e30)) for n in nu]
    egam = [jnp.exp(gam_c[n]) for n in nu]
    kk = [mmx(k[n], k[n], "nt", P_GD["kk"]) for n in nu]
    qk = [mmx(q[n], k[n], "nt", P_GD["qk"]) * dec[n] for n in nu]
    yield
    A = [beta[n] * kk[n] * dec[n] * strict for n in nu]
    Dg = [A[n] * bd for n in nu]
    L = [A[n] - Dg[n] for n in nu]
    if t_saved is None:
        ImD = [eye - Dg[n] for n in nu]
        D2 = inv(Dg, Dg)
        yield
        P1 = inv(ImD, [eye + x for x in D2])
        D4 = inv(D2, D2)
        yield
        P2 = inv(P1, [eye + x for x in D4])
        D8 = inv(D4, D4)
        yield
        M = inv(P2, [eye + x for x in D8])
        yield
        Nn = inv(M, L)
        yield
        N2 = inv(Nn, Nn)
        yield
        T1 = inv([eye - x for x in Nn], [eye + x for x in N2])
        yield
        Tinv = inv(T1, M)
        yield
    else:
        Tinv = [use_inverse(A[n], t_saved[b][:, h * DK:h * DK + C]) for n, (b, h) in enumerate(units)]
    rhs = [jnp.concatenate([beta[n] * v[n], (beta[n] * egam[n]) * k[n]], axis=1) for n in nu]
    sol = [mmx(Tinv[n], rhs[n], "nn", P_GD["sol"]) for n in nu]
    yield
    qwS = [mmx(jnp.concatenate([q[n] * egam[n], sol[n][:, DK:2 * DK]], axis=0), S[n], "nn", P_GD["ws"]) for n in nu]
    yield
    u = [sol[n][:, 0:DK] - qwS[n][C:2 * C] for n in nu]
    outs = [qwS[n][0:C] + mmx(qk[n], u[n], "nn", P_GD["o"]) for n in nu]
    Sn = [jnp.exp(glast[n]) * S[n] + mmx(k[n] * jnp.exp(grev_c[n]), u[n], "tn", P_GD["su"]) for n in nu]
    zpad = jnp.zeros((C, DK - C), f32)
    t_pack = tuple(jnp.concatenate([x for n in range(b * H, (b + 1) * H) for x in (lax.stop_gradient(Tinv[n]), zpad)],
                                   axis=1) for b in range(len(cs)))
    return tuple(Sn), tuple(jnp.concatenate(outs[b * H:(b + 1) * H], axis=1) for b in range(len(cs))), t_pack


def _in_proj(hflat, norm_w, wbig, conv4):
    tm = 384
    W3 = 3 * HD

    def body(h_ref, nw_ref, w_ref, cw_ref, p_ref, ut_ref, cv_ref, prev):
        i = pl.program_id(0)

        @pl.when(i == 0)
        def _():
            prev[...] = jnp.zeros_like(prev)

        u = rmsnorm(h_ref[...], nw_ref[...])
        ut_ref[...] = u.T.astype(bf16)
        p = bdot_nt(u, w_ref[...])
        p_ref[...] = p
        x = p[:, COL_QKV:COL_QKV + W3]
        xx = jnp.concatenate([prev[...], x], axis=0)
        y = cw_ref[3] * x
        for s in (1, 2, 3):
            y = y + cw_ref[3 - s] * pltpu.roll(xx, s, 0)[8:]
        row = i * tm + lax.broadcasted_iota(jnp.int32, (tm, 1), 0)
        tok = jnp.where(row >= TP, row - TP, row)
        cv_ref[...] = jnp.where(tok >= 8, y, 0.0)
        prev[...] = x[tm - 8:tm]

    return pl.pallas_call(
        body, name="in_proj", grid=(N // tm,),
        in_specs=[pl.BlockSpec((tm, D), lambda i: (i, 0)), pl.BlockSpec((1, D), lambda i: (0, 0)),
                  pl.BlockSpec((PC, D), lambda i: (0, 0)), pl.BlockSpec((4, 1, W3), lambda i: (0, 0, 0))],
        out_specs=[pl.BlockSpec((tm, PC), lambda i: (i, 0)), pl.BlockSpec((D, tm), lambda i: (0, i)),
                   pl.BlockSpec((tm, W3), lambda i: (i, 0))],
        out_shape=[jax.ShapeDtypeStruct((N, PC), f32), jax.ShapeDtypeStruct((D, N), bf16),
                   jax.ShapeDtypeStruct((N, W3), f32)],
        scratch_shapes=[pltpu.VMEM((8, W3), f32)],
        compiler_params=_cp(("arbitrary",)),
    )(hflat, norm_w, wbig, conv4)


NU = NB * H
_REV = lambda c: NCH - 1 - c
_FWD = lambda c: c


def _tok_spec(w, ix, col=0):
    return pl.BlockSpec((NB, C, w), lambda c: (0, ix(c), col))


def _state_spec(ix):
    return pl.BlockSpec((NB, 1, H, DK, DK), lambda c: (0, ix(c), 0, 0, 0))


def _row_spec(w):
    return pl.BlockSpec((1, w), lambda c: (0, 0))


def _rows(ref):
    return tuple(ref[b] for b in range(NB))


def _hg_extra_specs(ix):
    return [_row_spec(HD), _row_spec(HD)]


def _gd_extra_specs(ix):
    return [_tok_spec(DK, ix, COL_AB // DK), _row_spec(DK), _row_spec(DK)]


def _mix_fwd(proj3, cv, l0, l1, alog, dtb):
    def body(p_ref, c_ref, ab_ref, l0_ref, l1_ref, al_ref, db_ref, oh_ref, sh_ref, og_ref, sg_ref, t_ref, sth, stg):
        @pl.when(pl.program_id(0) == 0)
        def _():
            sth[...] = jnp.zeros_like(sth)
            stg[...] = jnp.zeros_like(stg)

        Sh = tuple(sth[n] for n in range(NU))
        Sg = tuple(stg[n] for n in range(NU))
        for n in range(NU):
            sh_ref[n // H, 0, n % H] = Sh[n]
            sg_ref[n // H, 0, n % H] = Sg[n]
        snh, oh, sng, og, tp = mix_chunk(Sh, _rows(p_ref), l0_ref[...], l1_ref[...],
                                         Sg, _rows(c_ref), _rows(ab_ref), al_ref[...], db_ref[...])
        for n in range(NU):
            sth[n] = snh[n]
            stg[n] = sng[n]
        for b in range(NB):
            oh_ref[b] = oh[b]
            og_ref[b] = og[b]
            t_ref[b] = tp[b]

    tok = jax.ShapeDtypeStruct((NB, TP, HD), f32)
    st = jax.ShapeDtypeStruct((NB, NCH, H, DK, DK), f32)
    return pl.pallas_call(
        body, name="mix_fwd", grid=(NCH,),
        in_specs=[_tok_spec(3 * HD, _FWD), _tok_spec(3 * HD, _FWD), _tok_spec(DK, _FWD, COL_AB // DK),
                  _row_spec(HD), _row_spec(HD), _row_spec(DK), _row_spec(DK)],
        out_specs=[_tok_spec(HD, _FWD), _state_spec(_FWD), _tok_spec(HD, _FWD), _state_spec(_FWD), _tok_spec(HD, _FWD)],
        out_shape=[tok, st, tok, st, tok],
        scratch_shapes=[pltpu.VMEM((NU, DK, DK), f32), pltpu.VMEM((NU, DK, DK), f32)],
        compiler_params=_cp(("arbitrary",)),
    )(proj3, cv, proj3, l0, l1, alog, dtb)


def _hg_bwd(proj3, l0, l1, s_saved, do):
    def body(p_ref, l0_ref, l1_ref, s_ref, do_ref, dp_ref, dl0_ref, dl1_ref, dst):
        @pl.when(pl.program_id(0) == 0)
        def _():
            dst[...] = jnp.zeros_like(dst)
            dl0_ref[...] = jnp.zeros_like(dl0_ref)
            dl1_ref[...] = jnp.zeros_like(dl1_ref)

        S = tuple(s_ref[n // H, 0, n % H] for n in range(NU))
        _, vjp = jax.vjp(hg_chunk, S, _rows(p_ref), l0_ref[...], l1_ref[...])
        dS, dp, dl0, dl1 = vjp((tuple(dst[n] for n in range(NU)), _rows(do_ref)))
        for n in range(NU):
            dst[n] = dS[n]
        for b in range(NB):
            dp_ref[b] = dp[b].astype(bf16)
        dl0_ref[...] += jnp.broadcast_to(dl0, (8, HD))
        dl1_ref[...] += jnp.broadcast_to(dl1, (8, HD))

    acc = pl.BlockSpec((8, HD), lambda c: (0, 0))
    return pl.pallas_call(
        body, name="hg_bwd", grid=(NCH,),
        in_specs=[_tok_spec(3 * HD, _REV)] + _hg_extra_specs(_REV) + [_state_spec(_REV), _tok_spec(HD, _REV)],
        out_specs=[_tok_spec(3 * HD, _REV), acc, acc],
        out_shape=[jax.ShapeDtypeStruct((NB, TP, 3 * HD), bf16), jax.ShapeDtypeStruct((8, HD), f32),
                   jax.ShapeDtypeStruct((8, HD), f32)],
        scratch_shapes=[pltpu.VMEM((NU, DK, DK), f32)],
        compiler_params=_cp(("arbitrary",)),
    )(proj3, l0, l1, s_saved, do)


def _gd_bwd(cv, proj3, alog, dtb, s_saved, t_saved, do):
    def body(c_ref, ab_ref, al_ref, db_ref, s_ref, t_ref, do_ref, dc_ref, dab_ref, dal_ref, ddb_ref, dst):
        @pl.when(pl.program_id(0) == 0)
        def _():
            dst[...] = jnp.zeros_like(dst)
            dal_ref[...] = jnp.zeros_like(dal_ref)
            ddb_ref[...] = jnp.zeros_like(ddb_ref)

        S = tuple(s_ref[n // H, 0, n % H] for n in range(NU))
        t_rows = _rows(t_ref)
        fn = lambda *a: gd_chunk(*a, t_saved=t_rows)[0:2]
        _, vjp = jax.vjp(fn, S, _rows(c_ref), _rows(ab_ref), al_ref[...], db_ref[...])
        dS, dc, dab, dal, ddb = vjp((tuple(dst[n] for n in range(NU)), _rows(do_ref)))
        for n in range(NU):
            dst[n] = dS[n]
        for b in range(NB):
            dc_ref[b] = dc[b]
            dab_ref[b] = dab[b].astype(bf16)
        dal_ref[...] += jnp.broadcast_to(dal, (8, DK))
        ddb_ref[...] += jnp.broadcast_to(ddb, (8, DK))

    acc = pl.BlockSpec((8, DK), lambda c: (0, 0))
    return pl.pallas_call(
        body, name="gd_bwd", grid=(NCH,),
        in_specs=[_tok_spec(3 * HD, _REV)] + _gd_extra_specs(_REV)
        + [_state_spec(_REV), _tok_spec(HD, _REV), _tok_spec(HD, _REV)],
        out_specs=[_tok_spec(3 * HD, _REV), _tok_spec(DK, _REV), acc, acc],
        out_shape=[jax.ShapeDtypeStruct((NB, TP, 3 * HD), f32), jax.ShapeDtypeStruct((NB, TP, DK), bf16),
                   jax.ShapeDtypeStruct((8, DK), f32), jax.ShapeDtypeStruct((8, DK), f32)],
        scratch_shapes=[pltpu.VMEM((NU, DK, DK), f32)],
        compiler_params=_cp(("arbitrary",)),
    )(cv, proj3, alog, dtb, s_saved, t_saved, do)


def _conv_bwd(proj3, conv4, dy):
    def body(x_ref, w_ref, dy_ref, dx_ref, dw_ref):
        @pl.when(pl.program_id(1) == 0)
        def _():
            dw_ref[...] = jnp.zeros_like(dw_ref)

        x = x_ref[0]
        row = lax.broadcasted_iota(jnp.int32, (TP, 1), 0)
        g = jnp.where(row >= 8, dy_ref[0], 0.0)
        dx = w_ref[3] * g
        dw_ref[3] += jnp.broadcast_to(jnp.sum(x * g, axis=0, keepdims=True), (8, HD))
        for s in (1, 2, 3):
            dx = dx + w_ref[3 - s] * pltpu.roll(g, TP - s, 0)
            dw_ref[3 - s] += jnp.broadcast_to(jnp.sum(pltpu.roll(x, s, 0) * g, axis=0, keepdims=True), (8, HD))
        dx_ref[0] = dx.astype(bf16)

    return pl.pallas_call(
        body, name="conv_bwd", grid=(3, NB),
        in_specs=[pl.BlockSpec((1, TP, HD), lambda j, b: (b, 0, COL_QKV // HD + j)),
                  pl.BlockSpec((4, 1, HD), lambda j, b: (0, 0, j)), pl.BlockSpec((1, TP, HD), lambda j, b: (b, 0, j))],
        out_specs=[pl.BlockSpec((1, TP, HD), lambda j, b: (b, 0, j)), pl.BlockSpec((4, 8, HD), lambda j, b: (0, 0, j))],
        out_shape=[jax.ShapeDtypeStruct((NB, TP, 3 * HD), bf16), jax.ShapeDtypeStruct((4, 8, 3 * HD), f32)],
        compiler_params=_cp(("arbitrary", "arbitrary")),
    )(proj3, conv4, dy)


def _out_loss(o_hg, o_gd, proj, hgw, gdw, wout, hflat, fw, target):
    tm = 384

    def body(ohg_ref, ogd_ref, zhg_ref, zgd_ref, hgw_ref, gdw_ref, wo_ref, h_ref, fw_ref, tg_ref,
             loss_ref, dohg_ref, dogd_ref, dzhg_ref, dzgd_ref, dh_ref, dwo_ref, dhgw_ref, dgdw_ref, dfw_ref):
        i = pl.program_id(0)

        @pl.when(i == 0)
        def _():
            for r in (loss_ref, dwo_ref, dhgw_ref, dgdw_ref, dfw_ref):
                r[...] = jnp.zeros_like(r)

        row = i * tm + lax.broadcasted_iota(jnp.int32, (tm, 1), 0)
        tok = jnp.where(row >= TP, row - TP, row)
        valid = (tok >= PAD + N_META).astype(f32)
        hval = h_ref[...]
        tgt = tg_ref[...]

        mixers = ((ohg_ref, zhg_ref, hgw_ref[...]), (ogd_ref, zgd_ref, gdw_ref[...]))
        saved, ys = [], []
        for o_ref, z_ref, w in mixers:
            for hh in range(H):
                sl = slice(hh * DK, (hh + 1) * DK)
                o, z = o_ref[:, sl], z_ref[:, sl]
                r = lax.rsqrt(jnp.mean(o * o, axis=-1, keepdims=True) + EPS)
                n = o * r
                sg = sigmoid(z)
                ws = w * (z * sg)
                saved.append((r, n, sg, z, ws, w))
                ys.append(n * ws)
        y = jnp.concatenate(ys, axis=-1)
        h2 = hval + bdot(y, wo_ref[...])
        r2 = lax.rsqrt(jnp.mean(h2 * h2, axis=-1, keepdims=True) + EPS)
        n2 = h2 * r2
        fwv = fw_ref[...]
        err = (n2 * fwv - tgt) * valid
        loss = (0.5 / D) * jnp.sum(err * err)
        dyf = err * (1.0 / D)
        dn2 = dyf * fwv
        dout = r2 * (dn2 - n2 * jnp.mean(dn2 * n2, axis=-1, keepdims=True))
        dh_ref[...] = dout
        dy = bdot_nt(dout, wo_ref[...])
        dwo_ref[...] += bdot_tn(y, dout)
        dws = []
        for mi, (do_ref, dz_ref) in enumerate(((dohg_ref, dzhg_ref), (dogd_ref, dzgd_ref))):
            dw = jnp.zeros((1, DK), f32)
            for hh in range(H):
                sl = slice(hh * DK, (hh + 1) * DK)
                r, n, sg, z, ws, w = saved[mi * H + hh]
                dyh = dy[:, mi * HD + hh * DK:mi * HD + (hh + 1) * DK]
                t = dyh * n
                dw = dw + jnp.sum(t * (z * sg), axis=0, keepdims=True)
                dz_ref[:, sl] = (t * w * (sg * (1.0 + z * (1.0 - sg)))).astype(bf16)
                dn = dyh * ws
                do_ref[:, sl] = r * (dn - n * jnp.mean(dn * n, axis=-1, keepdims=True))
            dws.append(dw)
        loss_ref[...] += jnp.broadcast_to(loss, (8, DK))
        dhgw_ref[...] += jnp.broadcast_to(dws[0], (8, DK))
        dgdw_ref[...] += jnp.broadcast_to(dws[1], (8, DK))
        dfw_ref[...] += jnp.broadcast_to(jnp.sum(dyf * n2, axis=0, keepdims=True), (8, D))

    row = lambda w: pl.BlockSpec((tm, w), lambda i: (i, 0))
    whole = lambda r, w: pl.BlockSpec((r, w), lambda i: (0, 0))
    col = lambda c0: pl.BlockSpec((tm, HD), lambda i: (i, c0 // HD))
    return pl.pallas_call(
        body, name="out_loss", grid=(N // tm,),
        in_specs=[row(HD), row(HD), col(COL_ZHG), col(COL_ZGD),
                  whole(1, DK), whole(1, DK), whole(D, D), row(D), whole(1, D), row(D)],
        out_specs=[whole(8, DK), row(HD), row(HD), row(HD), row(HD), row(D), whole(D, D),
                   whole(8, DK), whole(8, DK), whole(8, D)],
        out_shape=[jax.ShapeDtypeStruct((8, DK), f32)] + [jax.ShapeDtypeStruct((N, HD), f32)] * 2
        + [jax.ShapeDtypeStruct((N, HD), bf16)] * 2
        + [jax.ShapeDtypeStruct((N, D), f32), jax.ShapeDtypeStruct((D, D), f32),
           jax.ShapeDtypeStruct((8, DK), f32), jax.ShapeDtypeStruct((8, DK), f32), jax.ShapeDtypeStruct((8, D), f32)],
        compiler_params=_cp(("arbitrary",)),
    )(o_hg, o_gd, proj, proj, hgw, gdw, wout, hflat, fw, target)


def _in_bwd(pieces, wbig, hflat, norm_w, dh_res, pbs):
    tm = 384
    nsteps = N // tm
    np_ = len(pieces)
    na = len(pbs)
    offs = [c0 for _, c0 in pieces]
    widths = [d.shape[1] for d, _ in pieces]

    def body(*refs):
        d_refs = refs[:np_]
        w_ref, h_ref, nw_ref, dhr_ref = refs[np_:np_ + 4]
        srcs = refs[np_ + 4:np_ + 4 + na]
        dh_ref, dnw_ref = refs[np_ + 4 + na:np_ + 6 + na]
        dsts = refs[np_ + 6 + na:np_ + 6 + 2 * na]
        sems = refs[np_ + 6 + 2 * na:]
        i = pl.program_id(0)

        def copies():
            if not na:
                return []
            x, y, c, chips = _place()
            return [pltpu.make_async_remote_copy(
                src_ref=srcs[a].at[2 * px + py], dst_ref=dsts[a].at[j], send_sem=sems[0].at[na * j + a],
                recv_sem=sems[1].at[na * j + a], device_id=(px, py, c), device_id_type=MESH)
                for j, (px, py) in enumerate(chips) for a in range(na)]

        @pl.when(i == 0)
        def _():
            dnw_ref[...] = jnp.zeros_like(dnw_ref)
            for cp in copies():
                cp.start()

        du = jnp.zeros((tm, D), f32)
        for d_ref, off, wd in zip(d_refs, offs, widths):
            du = du + bdot(d_ref[...], w_ref[off:off + wd, :])
        _, vjp = jax.vjp(rmsnorm, h_ref[...], nw_ref[...])
        dh, dnw = vjp(du)
        dh_ref[...] = dh + dhr_ref[...]
        dnw_ref[...] += jnp.broadcast_to(dnw, (8, D))

        @pl.when(i == nsteps - 1)
        def _():
            for cp in copies():
                cp.wait()

    row = lambda w: pl.BlockSpec((tm, w), lambda i: (i, 0))
    return pl.pallas_call(
        body, name="in_bwd", grid=(nsteps,),
        in_specs=[row(w) for w in widths]
        + [pl.BlockSpec((PC, D), lambda i: (0, 0)), row(D), pl.BlockSpec((1, D), lambda i: (0, 0)), row(D)] + [ANY] * na,
        out_specs=[row(D), pl.BlockSpec((8, D), lambda i: (0, 0))] + [ANY] * na,
        out_shape=[jax.ShapeDtypeStruct((N, D), f32), jax.ShapeDtypeStruct((8, D), f32)]
        + [jax.ShapeDtypeStruct((3,) + p.shape[1:], p.dtype) for p in pbs],
        scratch_shapes=[pltpu.SemaphoreType.DMA((3 * na,)), pltpu.SemaphoreType.DMA((3 * na,))] if na else [],
        compiler_params=_cp(("arbitrary",)),
    )(*[d for d, _ in pieces], wbig, hflat, norm_w, dh_res, *pbs)


def _w_grad(ut, pieces):
    tk = 384
    offs = [c0 for _, c0 in pieces]
    widths = [d.shape[1] for d, _ in pieces]

    def body(u_ref, *refs):
        d_refs, o_ref = refs[:-1], refs[-1]

        @pl.when(pl.program_id(0) == 0)
        def _():
            o_ref[...] = jnp.zeros_like(o_ref)

        u = u_ref[...]
        for d_ref, off, wd in zip(d_refs, offs, widths):
            o_ref[:, off:off + wd] += jnp.dot(u, d_ref[...], preferred_element_type=f32)

    return pl.pallas_call(
        body, name="w_grad", grid=(N // tk,),
        in_specs=[pl.BlockSpec((D, tk), lambda k: (0, k))] + [pl.BlockSpec((tk, w), lambda k: (k, 0)) for w in widths],
        out_specs=pl.BlockSpec((D, PC), lambda k: (0, 0)),
        out_shape=jax.ShapeDtypeStruct((D, PC), f32),
        compiler_params=_cp(("arbitrary",)),
    )(ut, *[d for d, _ in pieces])


def _adam_math(g, w, m, v):
    m2 = ADAM_B1 * m + (1.0 - ADAM_B1) * g
    v2 = ADAM_B2 * v + (1.0 - ADAM_B2) * (g * g)
    m_hat = m2 / (1.0 - ADAM_B1 ** ADAM_STEP)
    v_hat = v2 / (1.0 - ADAM_B2 ** ADAM_STEP)
    delta = -ADAM_LR * (m_hat / (jnp.sqrt(v_hat) + ADAM_EPS) + ADAM_WD * w)
    return delta, m2, v2


def _adamw(gs, w, m, v, name):
    R, Cc = w.shape
    tr = 256 if R % 256 == 0 else R
    ng = len(gs)

    def body(*refs):
        g = refs[0][...]
        for r in refs[1:ng]:
            g = g + r[...]
        w_ref, m_ref, v_ref, g_ref, d_ref, m2_ref, v2_ref = refs[ng:]
        delta, m2, v2 = _adam_math(g, w_ref[...], m_ref[...], v_ref[...])
        g_ref[...] = g
        d_ref[...] = delta
        m2_ref[...] = m2
        v2_ref[...] = v2

    spec = pl.BlockSpec((tr, Cc), lambda i: (i, 0))
    return pl.pallas_call(
        body, name=name, grid=(R // tr,),
        in_specs=[spec] * (ng + 3), out_specs=[spec] * 4,
        out_shape=[jax.ShapeDtypeStruct((R, Cc), f32)] * 4,
        compiler_params=_cp(("arbitrary",)),
    )(*gs, w, m, v)


def _adamw_rows(g, w, m, v, name):
    R, _, Cc = w.shape
    tr = R // 9

    def body(g_ref, w_ref, m_ref, v_ref, go_ref, d_ref, m2_ref, v2_ref):
        g = g_ref[...]
        delta, m2, v2 = _adam_math(g, w_ref[...], m_ref[...], v_ref[...])
        go_ref[...] = g
        d_ref[...] = delta
        m2_ref[...] = m2
        v2_ref[...] = v2

    spec = pl.BlockSpec((tr, 1, Cc), lambda i: (i, 0, 0))
    return pl.pallas_call(
        body, name=name, grid=(R // tr,),
        in_specs=[spec] * 4, out_specs=[spec] * 4,
        out_shape=[jax.ShapeDtypeStruct((R, 1, Cc), f32)] * 4,
        compiler_params=_cp(("arbitrary",)),
    )(g, w, m, v)


def _place():
    x, y, c = lax.axis_index("x"), lax.axis_index("y"), lax.axis_index("c")
    return x, y, c, [(1 - x, y), (x, 1 - y), (1 - x, 1 - y)]


def _gather_weights(cm, halved, whole):
    R, _, Cc = cm.shape
    shards = [jax.ShapeDtypeStruct((2, R, Cc // 2), bf16)] + list(halved) + list(whole)
    nh = 1 + len(halved)
    na = len(shards)

    def body(*refs):
        srcs, dsts = refs[:na], refs[na:2 * na]
        send_sems, recv_sems, loc_sems = refs[2 * na:2 * na + 3]
        stage = refs[2 * na + 3:3 * na + 3]
        raw = refs[3 * na + 3]
        x, y, c, chips = _place()
        me = 2 * x + y
        loads = [pltpu.make_async_copy(srcs[0], raw, loc_sems.at[0])]
        loads += [pltpu.make_async_copy(srcs[i], stage[i], loc_sems.at[i]) for i in range(1, na)]
        locs = [pltpu.make_async_copy(v, d.at[me], loc_sems.at[i]) for i, (v, d) in enumerate(zip(stage, dsts))]
        for cp in loads:
            cp.start()

        def ici(j, i, slot):
            px, py = chips[j]
            src = (stage[0] if i == 0 else srcs[i]).at[c] if i < nh else srcs[i]
            dst = dsts[i].at[slot, c] if i < nh else dsts[i].at[slot]
            return pltpu.make_async_remote_copy(
                src_ref=src, dst_ref=dst, send_sem=send_sems.at[na * j + i], recv_sem=recv_sems.at[na * j + i],
                device_id=(px, py, c), device_id_type=MESH)

        def d2d(j, i, half):
            px, py = chips[j]
            blk = dsts[i].at[2 * px + py, half]
            return pltpu.make_async_remote_copy(
                src_ref=blk, dst_ref=blk, send_sem=send_sems.at[3 * na + nh * j + i],
                recv_sem=recv_sems.at[3 * na + nh * j + i], device_id=(x, y, 1 - c), device_id_type=MESH)

        sends = [ici(j, i, me) for j in range(3) for i in range(1, na)]
        for cp in sends:
            cp.start()
        loads[0].wait()
        v = raw[:, 0, :]
        for hc in range(2):
            stage[0][hc] = v[:, hc * (Cc // 2):(hc + 1) * (Cc // 2)].astype(bf16)
        first = [ici(j, 0, me) for j in range(3)]
        for cp in first:
            cp.start()
        sends += first
        locs[0].start()
        for ld, st in zip(loads[1:], locs[1:]):
            ld.wait()
            st.start()
        for j, (px, py) in enumerate(chips):
            for i in range(na):
                ici(j, i, 2 * px + py).wait_recv()
                if i < nh:
                    fwd = d2d(j, i, c)
                    fwd.start()
                    sends.append(fwd)
        for j in range(3):
            for i in range(nh):
                d2d(j, i, 1 - c).wait_recv()
        for cp in sends:
            cp.wait_send()
        for cp in locs:
            cp.wait()

    nsem = 3 * na + 3 * nh
    return pl.pallas_call(
        body, name="gather_weights",
        in_specs=[ANY] * na, out_specs=[ANY] * na,
        out_shape=[jax.ShapeDtypeStruct((4,) + s.shape, s.dtype) for s in shards],
        scratch_shapes=[pltpu.SemaphoreType.DMA((nsem,)), pltpu.SemaphoreType.DMA((nsem,)),
                        pltpu.SemaphoreType.DMA((na,))] + [pltpu.VMEM(s.shape, s.dtype) for s in shards]
        + [pltpu.VMEM(cm.shape, cm.dtype)],
        compiler_params=pltpu.CompilerParams(has_side_effects=True, vmem_limit_bytes=VMEM_LIMIT),
    )(cm, *halved, *whole)


def _swap_halves(gs):
    na = len(gs)
    jobs = [(i, q) for i in range(na) for q in range(gs[i].shape[0])]

    def body(*refs):
        srcs, dsts = refs[:na], refs[na:2 * na]
        send_sems, recv_sems = refs[2 * na:]
        x, y, c, _ = _place()
        cps = [pltpu.make_async_remote_copy(
            src_ref=srcs[i].at[q, 1 - c], dst_ref=dsts[i].at[q], send_sem=send_sems.at[k],
            recv_sem=recv_sems.at[k], device_id=(x, y, 1 - c), device_id_type=MESH)
            for k, (i, q) in enumerate(jobs)]
        for cp in cps:
            cp.start()
        for cp in cps:
            cp.wait()

    return pl.pallas_call(
        body, name="swap_halves",
        in_specs=[ANY] * na, out_specs=[ANY] * na,
        out_shape=[jax.ShapeDtypeStruct(g.shape[0:1] + g.shape[2:], g.dtype) for g in gs],
        scratch_shapes=[pltpu.SemaphoreType.DMA((len(jobs),)), pltpu.SemaphoreType.DMA((len(jobs),))],
        compiler_params=pltpu.CompilerParams(has_side_effects=True),
    )(*gs)


def _add_halves(c_arr, g, s, name):
    Q, _, R, Cc = g.shape
    tr = min(R, 128)

    def body(c_ref, g_ref, s_ref, b_ref, f_ref):
        p = g_ref[0, 0] + s_ref[0]
        f_ref[0] = p
        b_ref[0] = p.astype(bf16)

    blk = pl.BlockSpec((1, tr, Cc), lambda q, i, cr: (q, i, 0))
    return pl.pallas_call(
        body, name=name,
        grid_spec=pltpu.PrefetchScalarGridSpec(
            num_scalar_prefetch=1, grid=(Q, R // tr),
            in_specs=[pl.BlockSpec((1, 1, tr, Cc), lambda q, i, cr: (q, cr[0], i, 0)), blk], out_specs=[blk, blk]),
        out_shape=[jax.ShapeDtypeStruct((Q, R, Cc), bf16), jax.ShapeDtypeStruct((Q, R, Cc), f32)],
        compiler_params=_cp(("arbitrary", "arbitrary")),
    )(c_arr, g, s)


_FLIPS = [(fx, fy, fc) for fx in (0, 1) for fy in (0, 1) for fc in (0, 1)][1:]


def _sum_blocks(own, r, name):
    R, Cc = own.shape
    tr = min(R, 256)

    def body(own_ref, r_ref, o_ref):
        acc = own_ref[...]
        for j in range(3):
            acc = acc + r_ref[j].astype(f32)
        o_ref[...] = acc

    return pl.pallas_call(
        body, name=name, grid=(R // tr,),
        in_specs=[pl.BlockSpec((tr, Cc), lambda i: (i, 0)), pl.BlockSpec((3, tr, Cc), lambda i: (0, i, 0))],
        out_specs=pl.BlockSpec((tr, Cc), lambda i: (i, 0)),
        out_shape=jax.ShapeDtypeStruct((R, Cc), f32),
        compiler_params=_cp(("arbitrary",)),
    )(own, r)


def _sum_packs(me8_arr, pack, rp):
    R = pack.shape[0]

    def body(me_ref, pk_ref, rp_ref, o_ref):
        me8 = me_ref[0]
        acc = None
        for d in range(8):
            rel = d ^ me8
            term = jnp.where(rel == 0, pk_ref[...], rp_ref[jnp.maximum(rel - 1, 0)])
            acc = term if acc is None else acc + term
        o_ref[...] = acc

    return pl.pallas_call(
        body, name="sum_packs",
        grid_spec=pltpu.PrefetchScalarGridSpec(
            num_scalar_prefetch=1, grid=(1,),
            in_specs=[pl.BlockSpec((R, 128), lambda i, mr: (0, 0)), pl.BlockSpec((7, R, 128), lambda i, mr: (0, 0, 0))],
            out_specs=pl.BlockSpec((R, 128), lambda i, mr: (0, 0))),
        out_shape=jax.ShapeDtypeStruct((R, 128), f32),
        compiler_params=_cp(("arbitrary",)),
    )(me8_arr, pack, rp)


def _swap_finished(fs, pack):
    na = len(fs)
    R = pack.shape[0]

    def body(*refs):
        srcs, pk = refs[:na], refs[na]
        dsts, rp = refs[na + 1:2 * na + 1], refs[2 * na + 1]
        send_sems, recv_sems = refs[2 * na + 2:]
        x, y, c, _ = _place()
        cps = [pltpu.make_async_remote_copy(
            src_ref=srcs[i], dst_ref=dsts[i], send_sem=send_sems.at[i], recv_sem=recv_sems.at[i],
            device_id=(x, y, 1 - c), device_id_type=MESH) for i in range(na)]
        cps += [pltpu.make_async_remote_copy(
            src_ref=pk, dst_ref=rp.at[k], send_sem=send_sems.at[na + k], recv_sem=recv_sems.at[na + k],
            device_id=(x ^ fx, y ^ fy, c ^ fc), device_id_type=MESH) for k, (fx, fy, fc) in enumerate(_FLIPS)]
        for cp in cps:
            cp.start()
        for cp in cps:
            cp.wait()

    return pl.pallas_call(
        body, name="swap_finished",
        in_specs=[ANY] * (na + 1), out_specs=[ANY] * (na + 1),
        out_shape=[jax.ShapeDtypeStruct(f.shape, f.dtype) for f in fs] + [jax.ShapeDtypeStruct((7, R, 128), f32)],
        scratch_shapes=[pltpu.SemaphoreType.DMA((na + 7,)), pltpu.SemaphoreType.DMA((na + 7,))],
        compiler_params=pltpu.CompilerParams(has_side_effects=True),
    )(*fs, pack)


def _adamw_halves(c_arr, mine, peer, w, m, v, name):
    _, R, Cc = w.shape
    tr = min(R, 256)

    def body(c_ref, mine_ref, peer_ref, w_ref, m_ref, v_ref, g_ref, d_ref, m2_ref, v2_ref):
        g = jnp.where(pl.program_id(0) == c_ref[0], mine_ref[...], peer_ref[...])
        delta, m2, v2 = _adam_math(g, w_ref[0], m_ref[0], v_ref[0])
        g_ref[0] = g
        d_ref[0] = delta
        m2_ref[0] = m2
        v2_ref[0] = v2

    half = pl.BlockSpec((tr, Cc), lambda hh, i, cr: (i, 0))
    full = pl.BlockSpec((1, tr, Cc), lambda hh, i, cr: (hh, i, 0))
    return pl.pallas_call(
        body, name=name,
        grid_spec=pltpu.PrefetchScalarGridSpec(
            num_scalar_prefetch=1, grid=(2, R // tr), in_specs=[half, half, full, full, full], out_specs=[full] * 4),
        out_shape=[jax.ShapeDtypeStruct((2, R, Cc), f32)] * 4,
        compiler_params=_cp(("arbitrary", "arbitrary")),
    )(c_arr, mine, peer, w, m, v)


def _rows8(a):
    flat = a.reshape(-1)
    n = flat.shape[0]
    rows = -(-n // 1024) * 8
    return jnp.pad(flat, (0, rows * 128 - n)).reshape(rows, 128)


def kernel(x, meta_tokens, norm_w, w_in, conv_w, hg_lb_logits, hg_norm_w, gdn_A_log, gdn_dt_bias, gdn_norm_w, w_out, final_norm_w, loss_target, m_meta_tokens, m_norm_w, m_w_in, m_conv_w, m_hg_lb_logits, m_hg_norm_w, m_gdn_A_log, m_gdn_dt_bias, m_gdn_norm_w, m_w_out, m_final_norm_w, v_meta_tokens, v_norm_w, v_w_in, v_conv_w, v_hg_lb_logits, v_hg_norm_w, v_gdn_A_log, v_gdn_dt_bias, v_gdn_norm_w, v_w_out, v_final_norm_w):
    me = 2 * lax.axis_index("x") + lax.axis_index("y")

    g_win, g_wout, g_conv, g_meta = _gather_weights(
        jnp.transpose(w_in, (2, 0, 1)), [w_out[0].astype(bf16).reshape(2, D // 8, D)], [conv_w[0], meta_tokens])
    wt_full = jnp.transpose(g_win, (0, 2, 1, 3)).reshape(IN_COLS, D)
    wbig = jnp.pad(wt_full, ((0, PC - IN_COLS), (0, 0)))
    wout_full = g_wout.reshape(D, D)
    conv4 = jnp.transpose(g_conv, (1, 0, 2)).reshape(4, 1, 3 * HD)
    meta_full = jnp.transpose(g_meta, (1, 0, 2)).reshape(N_META, D)

    c_arr = lax.axis_index("c").reshape(1).astype(jnp.int32)

    def chip_partials(gw, g_wout_part):
        g_in2 = gw.reshape(1, 2, D // 2, PC)
        g_out4 = g_wout_part.reshape(4, 2, D // 8, D)
        s_in, s_out = _swap_halves([g_in2, g_out4])
        pb_in, pf_in = _add_halves(c_arr, g_in2, s_in, "add_w_in")
        pb_out, pf_out = _add_halves(c_arr, g_out4, s_out, "add_w_out")
        pb_blocks = jnp.transpose(pb_in[0, :, 0:IN_COLS].reshape(D // 2, 4, SHARD_COLS), (1, 0, 2))
        own_in = lax.dynamic_slice(pf_in[0], (0, me * SHARD_COLS), (D // 2, SHARD_COLS))
        own_out = lax.dynamic_index_in_dim(pf_out, me, axis=0, keepdims=False)
        return [pb_blocks, pb_out], [own_in, own_out]

    (loss8, grad_x, d_meta, d_nw, d_conv, d_lb, d_hgw, d_alog, d_dtb, d_gdw, d_fw, pfs, rs) = _local_step(
        x, loss_target, wbig, wout_full, conv4, meta_full, norm_w, hg_lb_logits, hg_norm_w, gdn_A_log, gdn_dt_bias,
        gdn_norm_w, final_norm_w, chip_partials)

    pack = jnp.concatenate([
        loss8, d_nw[0].reshape(8, 128), d_lb.reshape(8, 128), d_hgw, _rows8(d_alog[0, :H]), _rows8(d_dtb[0, :H]),
        d_gdw, d_fw[0].reshape(8, 128), d_meta.reshape(128, 128), d_conv.reshape(48, 128)], axis=0)
    return _reduce_and_update(
        me, c_arr, grad_x, pfs, rs, pack, meta_tokens, norm_w, w_in, conv_w, hg_lb_logits, hg_norm_w, gdn_A_log,
        gdn_dt_bias, gdn_norm_w, w_out, final_norm_w, m_meta_tokens, m_norm_w, m_w_in, m_conv_w, m_hg_lb_logits,
        m_hg_norm_w, m_gdn_A_log, m_gdn_dt_bias, m_gdn_norm_w, m_w_out, m_final_norm_w, v_meta_tokens, v_norm_w, v_w_in,
        v_conv_w, v_hg_lb_logits, v_hg_norm_w, v_gdn_A_log, v_gdn_dt_bias, v_gdn_norm_w, v_w_out, v_final_norm_w)


def _local_step(x, loss_target, wbig, wout_full, conv4, meta_full, norm_w, hg_lb_logits, hg_norm_w, gdn_A_log, gdn_dt_bias,
                gdn_norm_w, final_norm_w, chip_partials):
    h3 = jnp.concatenate([jnp.zeros((NB, PAD, D), f32), jnp.broadcast_to(meta_full[None], (NB, N_META, D)), x], axis=1)
    hflat = h3.reshape(N, D)
    target = jnp.pad(loss_target, ((0, 0), (PAD + N_META, 0), (0, 0))).reshape(N, D)
    l0, l1 = hg_lb_logits[0:1], hg_lb_logits[1:2]
    alog = jnp.pad(gdn_A_log, ((0, 0), (0, DK - H)))
    dtb = jnp.pad(gdn_dt_bias, ((0, 0), (0, DK - H)))
    fw = final_norm_w.reshape(1, D)

    proj, ut, cv2 = _in_proj(hflat, norm_w, wbig, conv4)
    proj3 = proj.reshape(NB, TP, PC)
    cv = cv2.reshape(NB, TP, 3 * HD)
    o_hg, s_hg, o_gd, s_gd, t_gd = _mix_fwd(proj3, cv, l0, l1, alog, dtb)
    (loss8, d_ohg, d_ogd, d_zhg, d_zgd, dh_res, g_wout_part, d_hgw, d_gdw, d_fw) = _out_loss(
        o_hg.reshape(N, HD), o_gd.reshape(N, HD), proj, hg_norm_w, gdn_norm_w, wout_full, hflat, fw, target)
    d_hg, d_l0, d_l1 = _hg_bwd(proj3, l0, l1, s_hg, d_ohg.reshape(NB, TP, HD))
    d_cv, d_ab, d_alog, d_dtb = _gd_bwd(cv, proj3, alog, dtb, s_gd, t_gd, d_ogd.reshape(NB, TP, HD))
    d_qkv, d_conv4 = _conv_bwd(proj3, conv4, d_cv)
    d_hg2, d_qkv2, d_ab2 = d_hg.reshape(N, 3 * HD), d_qkv.reshape(N, 3 * HD), d_ab.reshape(N, DK)
    pieces = [(d_hg2, COL_HG), (d_zhg, COL_ZHG), (d_qkv2, COL_QKV), (d_zgd, COL_ZGD), (d_ab2, COL_AB)]
    gw = _w_grad(ut, pieces)
    pbs, pfs = chip_partials(gw, g_wout_part) if chip_partials else ([], [gw, g_wout_part])
    dh, d_nw, *rs = _in_bwd(pieces, wbig, hflat, norm_w, dh_res, pbs)

    dh3 = dh.reshape(NB, TP, D)
    grad_x = dh3[:, PAD + N_META:, :]
    d_meta = jnp.sum(dh3[:, PAD:PAD + N_META, :], axis=0)
    d_conv = d_conv4[:, 0, :]
    d_lb = jnp.concatenate([d_l0[0:1], d_l1[0:1]], axis=0)
    return loss8, grad_x, d_meta, d_nw, d_conv, d_lb, d_hgw, d_alog, d_dtb, d_gdw, d_fw, pfs, rs


def _reduce_and_update(me, c_arr, grad_x, pfs, rs, pack, meta_tokens, norm_w, w_in, conv_w, hg_lb_logits, hg_norm_w,
                       gdn_A_log, gdn_dt_bias, gdn_norm_w, w_out, final_norm_w, m_meta_tokens, m_norm_w, m_w_in, m_conv_w,
                       m_hg_lb_logits, m_hg_norm_w, m_gdn_A_log, m_gdn_dt_bias, m_gdn_norm_w, m_w_out, m_final_norm_w,
                       v_meta_tokens, v_norm_w, v_w_in, v_conv_w, v_hg_lb_logits, v_hg_norm_w, v_gdn_A_log, v_gdn_dt_bias,
                       v_gdn_norm_w, v_w_out, v_final_norm_w):
    (own_in, own_out), (r_in, r_out) = pfs, rs
    f_in = _sum_blocks(own_in, r_in, "sum_w_in")
    f_out = _sum_blocks(own_out, r_out, "sum_w_out")
    o_in, o_out, r_pack = _swap_finished([f_in, f_out], pack)
    me8_arr = (2 * me + lax.axis_index("c")).reshape(1).astype(jnp.int32)
    small = _sum_packs(me8_arr, pack, r_pack)

    half_out = lambda a: a[0].reshape(2, D // 8, D)
    is0 = lax.axis_index("c") == 0
    g_in = jnp.concatenate([jnp.where(is0, f_in, o_in), jnp.where(is0, o_in, f_in)], axis=0)
    to_cm = lambda a: jnp.transpose(a, (2, 0, 1))
    gi, di, mi, vi = [jnp.transpose(a, (1, 2, 0))[0] for a in _adamw_rows(
        g_in.T.reshape(SHARD_COLS, 1, D), to_cm(w_in), to_cm(m_w_in), to_cm(v_w_in), "adamw_w_in")]
    go, do_, mo, vo = [a.reshape(D // 4, D) for a in _adamw_halves(
        c_arr, f_out, o_out, half_out(w_out), half_out(m_w_out), half_out(v_w_out), "adamw_w_out")]

    g_meta_full = small[64:192].reshape(N_META, D)
    g_meta_loc = lax.dynamic_slice(g_meta_full, (0, me * 256), (N_META, 256))
    gm, dm, mm_, vm = _adamw([g_meta_loc], meta_tokens, m_meta_tokens, v_meta_tokens, "adamw_meta")
    g_conv_full = small[192:240].reshape(4, 1536)
    g_conv_loc = lax.dynamic_slice(g_conv_full, (0, me * 384), (4, 384))
    gc, dc, mc, vc = _adamw([g_conv_loc], conv_w[0], m_conv_w[0], v_conv_w[0], "adamw_conv")

    reps = [(norm_w, m_norm_w, v_norm_w), (hg_lb_logits, m_hg_lb_logits, v_hg_lb_logits),
            (hg_norm_w, m_hg_norm_w, v_hg_norm_w), (gdn_A_log, m_gdn_A_log, v_gdn_A_log),
            (gdn_dt_bias, m_gdn_dt_bias, v_gdn_dt_bias), (gdn_norm_w, m_gdn_norm_w, v_gdn_norm_w),
            (final_norm_w, m_final_norm_w, v_final_norm_w)]
    wp = jnp.concatenate([_rows8(t[0]) for t in reps], axis=0)
    mp = jnp.concatenate([_rows8(t[1]) for t in reps], axis=0)
    vp = jnp.concatenate([_rows8(t[2]) for t in reps], axis=0)
    gr, dr, mr, vr = _adamw([small[8:64]], wp, mp, vp, "adamw_small")

    def unpack(p):
        outs = []
        for i, t in enumerate(reps):
            n = t[0].size
            outs.append(p[8 * i:8 * i + 8].reshape(-1)[:n].reshape(t[0].shape))
        return outs

    def leaves(meta_v, conv_v, in_v, out_v, rep_p):
        nw, lb, hgw, al, db, gdw, fwv = unpack(rep_p)
        return [meta_v, nw, in_v[None], conv_v[None], lb, hgw, al, db, gdw, out_v[None], fwv]

    loss = small[0, 0]
    return (loss, grad_x, *leaves(gm, gc, gi, go, gr), *leaves(dm, dc, di, do_, dr),
            *leaves(mm_, mc, mi, mo, mr), *leaves(vm, vc, vi, vo, vr))
```

```python
import functools

import jax
import jax.numpy as jnp
from jax import lax
from jax.experimental import pallas as pl
from jax.experimental.pallas import tpu as pltpu

f32 = jnp.float32
bf16 = jnp.bfloat16
MESH = pl.DeviceIdType.MESH
ANY = pl.BlockSpec(memory_space=pl.ANY)

D = 1024
NB = 2
N_META = 16
SEQ = 2048
PAD = 48
TP = PAD + N_META + SEQ
C = 64
NCH = TP // C
N = NB * TP
H = 4
DK = 128
HD = H * DK
PC = 4224
IN_COLS = 4104
SHARD_COLS = IN_COLS // 4
COL_HG, COL_ZHG, COL_QKV, COL_ZGD, COL_AB = 0, 3 * HD, 4 * HD, 7 * HD, 8 * HD
EPS = 1e-6
ADAM_LR, ADAM_B1, ADAM_B2, ADAM_EPS, ADAM_WD, ADAM_STEP = 0.001, 0.9, 0.999, 1e-08, 0.01, 10
VMEM_LIMIT = 56 * 1024 * 1024

P_HG = dict(lvl=1, av=1, qs=1, su=1)
P_GD = dict(kk=1, inv=1, sol=1, ws=1, qk=1, o=1, su=1)


def _cp(sem=None, **kw):
    return pltpu.CompilerParams(dimension_semantics=sem, vmem_limit_bytes=VMEM_LIMIT, **kw)


_DIMS = {"nn": (((1,), (0,)), ((), ())), "nt": (((1,), (1,)), ((), ())), "tn": (((0,), (0,)), ((), ()))}


def _split(x):
    hi = x.astype(bf16)
    return hi, (x - hi.astype(f32)).astype(bf16)


def _dg(a, b, kind, passes):
    d = lambda x, y: lax.dot_general(x, y, _DIMS[kind], preferred_element_type=f32)
    if passes == 1:
        return d(a.astype(bf16), b.astype(bf16))
    ah, al = _split(a)
    bh, bl = _split(b)
    return d(ah, bh) + d(ah, bl) + d(al, bh)


@functools.partial(jax.custom_vjp, nondiff_argnums=(2, 3))
def mmx(a, b, kind, passes):
    return _dg(a, b, kind, passes)


def _mmx_fwd(a, b, kind, passes):
    return _dg(a, b, kind, passes), (a, b)


def _mmx_bwd(kind, passes, res, g):
    a, b = res
    if kind == "nn":
        return _dg(g, b, "nt", passes), _dg(a, g, "tn", passes)
    if kind == "nt":
        return _dg(g, b, "nn", passes), _dg(g, a, "tn", passes)
    return _dg(b, g, "nt", passes), _dg(a, g, "nn", passes)


mmx.defvjp(_mmx_fwd, _mmx_bwd)


def _mask_dg(mask, x):
    xh, xl = _split(x)
    return jnp.dot(jnp.concatenate([mask, mask], axis=1), jnp.concatenate([xh, xl], axis=0), preferred_element_type=f32)


@functools.partial(jax.custom_vjp, nondiff_argnums=(2,))
def mask_mm(mask, x, bwd_passes):
    return _mask_dg(mask, x)


def _mask_fwd(mask, x, bwd_passes):
    return _mask_dg(mask, x), mask


def _mask_bwd(bwd_passes, mask, g):
    d = lambda y: lax.dot_general(mask, y, _DIMS["tn"], preferred_element_type=f32)
    if bwd_passes == 1:
        return None, d(g.astype(bf16))
    gh, gl = _split(g)
    return None, d(gh) + d(gl)


mask_mm.defvjp(_mask_fwd, _mask_bwd)


def bdot(a, b):
    return jnp.dot(a.astype(bf16), b.astype(bf16), preferred_element_type=f32)


def bdot_nt(a, b):
    return lax.dot_general(a.astype(bf16), b.astype(bf16), _DIMS["nt"], preferred_element_type=f32)


def bdot_tn(a, b):
    return lax.dot_general(a.astype(bf16), b.astype(bf16), _DIMS["tn"], preferred_element_type=f32)


def _iota2(n, m):
    return lax.broadcasted_iota(jnp.int32, (n, m), 0), lax.broadcasted_iota(jnp.int32, (n, m), 1)


sigmoid = jax.nn.sigmoid


def silu(x):
    return x * sigmoid(x)


def softplus(x):
    return jnp.maximum(x, 0.0) + jnp.log(1.0 + jnp.exp(-jnp.abs(x)))


def rmsnorm(x, w):
    return x * lax.rsqrt(jnp.mean(x * x, axis=-1, keepdims=True) + EPS) * w


def hg_masks():
    t, r = _iota2(C, C)
    mats = [r <= t, r > t]
    lvl = []
    for l in range(1, 7):
        sz = 1 << l
        half = sz >> 1
        seg_t = t >> l
        upper_t = (t & (sz - 1)) >= half
        mid_t = seg_t * sz + half - 1
        mats.append((upper_t & (r > mid_t) & (r <= t)) | ((~upper_t) & (r > t) & (r <= mid_t)))
        lvl.append(((seg_t == (r >> l)) & upper_t & ((r & (sz - 1)) < half)).astype(f32))
    stk = jnp.concatenate([m.astype(bf16) for m in mats], axis=0)
    return stk, lvl, (t == r).astype(f32)


def _head(a, h):
    return a[:, h * DK:(h + 1) * DK]


def _run(*gens):
    results = [None] * len(gens)
    live = list(range(len(gens)))
    while live:
        for i in list(live):
            try:
                next(gens[i])
            except StopIteration as e:
                results[i] = e.value
                live.remove(i)
    return results


def hg_chunk(St, ps, l0, l1):
    return _run(hg_stages(St, ps, l0, l1))[0]


def gd_chunk(S, cs, abs_, alog, dtb, t_saved=None):
    return _run(gd_stages(S, cs, abs_, alog, dtb, t_saved))[0]


def mix_chunk(St, ps, l0, l1, S, cs, abs_, alog, dtb):
    (sn_h, o_h), (sn_g, o_g, t_pack) = _run(hg_stages(St, ps, l0, l1), gd_stages(S, cs, abs_, alog, dtb))
    return sn_h, o_h, sn_g, o_g, t_pack


def hg_stages(St, ps, l0, l1):
    m = jnp.maximum(l0, l1)
    e0 = jnp.exp(l0 - m)
    e1 = jnp.exp(l1 - m)
    lb = e0 / (e0 + e1)
    stk, lvl, eye = hg_masks()
    msk = [eye] + lvl
    qs, ks, vs, qG, kR, eGl = [], [], [], [], [], []
    for p in ps:
        pq, pf, v = p[:, 0:HD], p[:, HD:2 * HD], p[:, 2 * HD:3 * HD]
        q = silu(pq)
        f = lb + (1.0 - lb) * sigmoid(pf)
        k = 1.0 - f
        logf = jnp.log(f)
        Dm = mask_mm(stk, logf, 1)
        ex = [jnp.exp(Dm[(2 + i) * C:(3 + i) * C]) for i in range(6)]
        qs.append([q] + [q * e for e in ex])
        ks.append([k] + [k * e for e in ex])
        vs.append(v)
        qG.append(q * jnp.exp(Dm[0:C]))
        kR.append(k * jnp.exp(Dm[C:2 * C]))
        eGl.append(jnp.exp(jnp.sum(logf, axis=0, keepdims=True)))
    yield
    units = [(b, h) for b in range(len(ps)) for h in range(H)]
    parts = []
    for i in range(7):
        parts.append([msk[i] * mmx(_head(qs[b][i], h), _head(ks[b][i], h), "nt", P_HG["lvl"]) for b, h in units])
        yield
    A = [functools.reduce(lambda x, y: x + y, [parts[i][n] for i in range(7)]) for n in range(len(units))]
    qS = [mmx(_head(qG[b], h), St[n], "nt", P_HG["qs"]) for n, (b, h) in enumerate(units)]
    Sn = [St[n] * _head(eGl[b], h) + mmx(_head(vs[b], h), _head(kR[b], h), "tn", P_HG["su"])
          for n, (b, h) in enumerate(units)]
    yield
    outs = [mmx(A[n], _head(vs[b], h), "nn", P_HG["av"]) + qS[n] for n, (b, h) in enumerate(units)]
    return tuple(Sn), tuple(jnp.concatenate(outs[b * H:(b + 1) * H], axis=1) for b in range(len(ps)))


@jax.custom_vjp
def use_inverse(A, T):
    return T


def _use_inverse_fwd(A, T):
    return T, T


def _use_inverse_bwd(T, g):
    return -_dg(T, _dg(g, T, "nt", P_GD["inv"]), "tn", P_GD["inv"]), jnp.zeros_like(T)


use_inverse.defvjp(_use_inverse_fwd, _use_inverse_bwd)


def gd_stages(S, cs, abs_, alog, dtb, t_saved=None):
    t, r = _iota2(C, C)
    tri = (r <= t).astype(bf16)
    ups = (r > t).astype(bf16)
    lane = lax.broadcasted_iota(jnp.int32, (1, DK), 1)
    subl = lax.broadcasted_iota(jnp.int32, (8, 1), 0)
    eye = (t == r).astype(f32)
    strict = (r < t).astype(f32)
    bd = ((t >> 4) == (r >> 4)).astype(f32)
    qa, ka, va, b4, gam4, grev4, gam4T, glast4 = [], [], [], [], [], [], [], []
    for c, ab in zip(cs, abs_):
        qa.append(silu(c[:, 0:HD]))
        ka.append(silu(c[:, HD:2 * HD]))
        va.append(silu(c[:, 2 * HD:3 * HD]))
        g4 = -jnp.exp(alog) * softplus(ab + dtb)
        b4.append(sigmoid(ab))
        gam4.append(mask_mm(tri, g4, 2))
        grev4.append(mask_mm(ups, g4, 2))
        gam4T.append(gam4[-1].T)
        glast4.append(jnp.sum(g4, axis=0, keepdims=True))
    yield
    units = [(b, h) for b in range(len(cs)) for h in range(H)]
    nu = range(len(units))
    inv = lambda a, b: [mmx(a[n], b[n], "nn", P_GD["inv"]) for n in nu]
    v = [_head(va[b], h) for b, h in units]
    q = [_head(qa[b], h) for b, h in units]
    k = [_head(ka[b], h) for b, h in units]
    q = [x * lax.rsqrt(jnp.sum(x * x, -1, keepdims=True) + EPS) * (DK ** -0.5) for x in q]
    k = [x * lax.rsqrt(jnp.sum(x * x, -1, keepdims=True) + EPS) for x in k]
    oh = [(lane == h).astype(f32) for h in range(H)]
    gam_c = [jnp.sum(gam4[b] * oh[h], -1, keepdims=True) for b, h in units]
    grev_c = [jnp.sum(grev4[b] * oh[h], -1, keepdims=True) for b, h in units]
    beta = [jnp.sum(b4[b] * (lane == h + H).astype(f32), -1, keepdims=True) for b, h in units]
    glast = [jnp.sum(glast4[b] * oh[h], -1, keepdims=True) for b, h in units]
    gam_r = [jnp.sum(gam4T[b][0:8, :] * (subl == h).astype(f32), axis=0, keepdims=True) for b, h in units]
    dec = [jnp.exp(jnp.where(r <= t, gam_c[n] - gam_r[n], -1e30)) for n in nu]
    egam = [jnp.exp(gam_c[n]) for n in nu]
    kk = [mmx(k[n], k[n], "nt", P_GD["kk"]) for n in nu]
    qk = [mmx(q[n], k[n], "nt", P_GD["qk"]) * dec[n] for n in nu]
    yield
    A = [beta[n] * kk[n] * dec[n] * strict for n in nu]
    Dg = [A[n] * bd for n in nu]
    L = [A[n] - Dg[n] for n in nu]
    if t_saved is None:
        ImD = [eye - Dg[n] for n in nu]
        D2 = inv(Dg, Dg)
        yield
        P1 = inv(ImD, [eye + x for x in D2])
        D4 = inv(D2, D2)
        yield
        P2 = inv(P1, [eye + x for x in D4])
        D8 = inv(D4, D4)
        yield
        M = inv(P2, [eye + x for x in D8])
        yield
        Nn = inv(M, L)
        yield
        N2 = inv(Nn, Nn)
        yield
        T1 = inv([eye - x for x in Nn], [eye + x for x in N2])
        yield
        Tinv = inv(T1, M)
        yield
    else:
        Tinv = [use_inverse(A[n], t_saved[b][:, h * DK:h * DK + C]) for n, (b, h) in enumerate(units)]
    rhs = [jnp.concatenate([beta[n] * v[n], (beta[n] * egam[n]) * k[n]], axis=1) for n in nu]
    sol = [mmx(Tinv[n], rhs[n], "nn", P_GD["sol"]) for n in nu]
    yield
    qwS = [mmx(jnp.concatenate([q[n] * egam[n], sol[n][:, DK:2 * DK]], axis=0), S[n], "nn", P_GD["ws"]) for n in nu]
    yield
    u = [sol[n][:, 0:DK] - qwS[n][C:2 * C] for n in nu]
    outs = [qwS[n][0:C] + mmx(qk[n], u[n], "nn", P_GD["o"]) for n in nu]
    Sn = [jnp.exp(glast[n]) * S[n] + mmx(k[n] * jnp.exp(grev_c[n]), u[n], "tn", P_GD["su"]) for n in nu]
    zpad = jnp.zeros((C, DK - C), f32)
    t_pack = tuple(jnp.concatenate([x for n in range(b * H, (b + 1) * H) for x in (lax.stop_gradient(Tinv[n]), zpad)],
                                   axis=1) for b in range(len(cs)))
    return tuple(Sn), tuple(jnp.concatenate(outs[b * H:(b + 1) * H], axis=1) for b in range(len(cs))), t_pack


def _in_proj(hflat, norm_w, wbig, conv4):
    tm = 384
    W3 = 3 * HD

    def body(h_ref, nw_ref, w_ref, cw_ref, p_ref, ut_ref, cv_ref, prev):
        i = pl.program_id(0)

        @pl.when(i == 0)
        def _():
            prev[...] = jnp.zeros_like(prev)

        u = rmsnorm(h_ref[...], nw_ref[...])
        ut_ref[...] = u.T.astype(bf16)
        p = bdot_nt(u, w_ref[...])
        p_ref[...] = p
        x = p[:, COL_QKV:COL_QKV + W3]
        xx = jnp.concatenate([prev[...], x], axis=0)
        y = cw_ref[3] * x
        for s in (1, 2, 3):
            y = y + cw_ref[3 - s] * pltpu.roll(xx, s, 0)[8:]
        row = i * tm + lax.broadcasted_iota(jnp.int32, (tm, 1), 0)
        tok = jnp.where(row >= TP, row - TP, row)
        cv_ref[...] = jnp.where(tok >= 8, y, 0.0)
        prev[...] = x[tm - 8:tm]

    return pl.pallas_call(
        body, name="in_proj", grid=(N // tm,),
        in_specs=[pl.BlockSpec((tm, D), lambda i: (i, 0)), pl.BlockSpec((1, D), lambda i: (0, 0)),
                  pl.BlockSpec((PC, D), lambda i: (0, 0)), pl.BlockSpec((4, 1, W3), lambda i: (0, 0, 0))],
        out_specs=[pl.BlockSpec((tm, PC), lambda i: (i, 0)), pl.BlockSpec((D, tm), lambda i: (0, i)),
                   pl.BlockSpec((tm, W3), lambda i: (i, 0))],
        out_shape=[jax.ShapeDtypeStruct((N, PC), f32), jax.ShapeDtypeStruct((D, N), bf16),
                   jax.ShapeDtypeStruct((N, W3), f32)],
        scratch_shapes=[pltpu.VMEM((8, W3), f32)],
        compiler_params=_cp(("arbitrary",)),
    )(hflat, norm_w, wbig, conv4)


NU = NB * H
_REV = lambda c: NCH - 1 - c
_FWD = lambda c: c


def _tok_spec(w, ix, col=0):
    return pl.BlockSpec((NB, C, w), lambda c: (0, ix(c), col))


def _state_spec(ix):
    return pl.BlockSpec((NB, 1, H, DK, DK), lambda c: (0, ix(c), 0, 0, 0))


def _row_spec(w):
    return pl.BlockSpec((1, w), lambda c: (0, 0))


def _rows(ref):
    return tuple(ref[b] for b in range(NB))


def _hg_extra_specs(ix):
    return [_row_spec(HD), _row_spec(HD)]


def _gd_extra_specs(ix):
    return [_tok_spec(DK, ix, COL_AB // DK), _row_spec(DK), _row_spec(DK)]


def _mix_fwd(proj3, cv, l0, l1, alog, dtb):
    def body(p_ref, c_ref, ab_ref, l0_ref, l1_ref, al_ref, db_ref, oh_ref, sh_ref, og_ref, sg_ref, t_ref, sth, stg):
        @pl.when(pl.program_id(0) == 0)
        def _():
            sth[...] = jnp.zeros_like(sth)
            stg[...] = jnp.zeros_like(stg)

        Sh = tuple(sth[n] for n in range(NU))
        Sg = tuple(stg[n] for n in range(NU))
        for n in range(NU):
            sh_ref[n // H, 0, n % H] = Sh[n]
            sg_ref[n // H, 0, n % H] = Sg[n]
        snh, oh, sng, og, tp = mix_chunk(Sh, _rows(p_ref), l0_ref[...], l1_ref[...],
                                         Sg, _rows(c_ref), _rows(ab_ref), al_ref[...], db_ref[...])
        for n in range(NU):
            sth[n] = snh[n]
            stg[n] = sng[n]
        for b in range(NB):
            oh_ref[b] = oh[b]
            og_ref[b] = og[b]
            t_ref[b] = tp[b]

    tok = jax.ShapeDtypeStruct((NB, TP, HD), f32)
    st = jax.ShapeDtypeStruct((NB, NCH, H, DK, DK), f32)
    return pl.pallas_call(
        body, name="mix_fwd", grid=(NCH,),
        in_specs=[_tok_spec(3 * HD, _FWD), _tok_spec(3 * HD, _FWD), _tok_spec(DK, _FWD, COL_AB // DK),
                  _row_spec(HD), _row_spec(HD), _row_spec(DK), _row_spec(DK)],
        out_specs=[_tok_spec(HD, _FWD), _state_spec(_FWD), _tok_spec(HD, _FWD), _state_spec(_FWD), _tok_spec(HD, _FWD)],
        out_shape=[tok, st, tok, st, tok],
        scratch_shapes=[pltpu.VMEM((NU, DK, DK), f32), pltpu.VMEM((NU, DK, DK), f32)],
        compiler_params=_cp(("arbitrary",)),
    )(proj3, cv, proj3, l0, l1, alog, dtb)


def _hg_bwd(proj3, l0, l1, s_saved, do):
    def body(p_ref, l0_ref, l1_ref, s_ref, do_ref, dp_ref, dl0_ref, dl1_ref, dst):
        @pl.when(pl.program_id(0) == 0)
        def _():
            dst[...] = jnp.zeros_like(dst)
            dl0_ref[...] = jnp.zeros_like(dl0_ref)
            dl1_ref[...] = jnp.zeros_like(dl1_ref)

        S = tuple(s_ref[n // H, 0, n % H] for n in range(NU))
        _, vjp = jax.vjp(hg_chunk, S, _rows(p_ref), l0_ref[...], l1_ref[...])
        dS, dp, dl0, dl1 = vjp((tuple(dst[n] for n in range(NU)), _rows(do_ref)))
        for n in range(NU):
            dst[n] = dS[n]
        for b in range(NB):
            dp_ref[b] = dp[b].astype(bf16)
        dl0_ref[...] += jnp.broadcast_to(dl0, (8, HD))
        dl1_ref[...] += jnp.broadcast_to(dl1, (8, HD))

    acc = pl.BlockSpec((8, HD), lambda c: (0, 0))
    return pl.pallas_call(
        body, name="hg_bwd", grid=(NCH,),
        in_specs=[_tok_spec(3 * HD, _REV)] + _hg_extra_specs(_REV) + [_state_spec(_REV), _tok_spec(HD, _REV)],
        out_specs=[_tok_spec(3 * HD, _REV), acc, acc],
        out_shape=[jax.ShapeDtypeStruct((NB, TP, 3 * HD), bf16), jax.ShapeDtypeStruct((8, HD), f32),
                   jax.ShapeDtypeStruct((8, HD), f32)],
        scratch_shapes=[pltpu.VMEM((NU, DK, DK), f32)],
        compiler_params=_cp(("arbitrary",)),
    )(proj3, l0, l1, s_saved, do)


def _gd_bwd(cv, proj3, alog, dtb, s_saved, t_saved, do):
    def body(c_ref, ab_ref, al_ref, db_ref, s_ref, t_ref, do_ref, dc_ref, dab_ref, dal_ref, ddb_ref, dst):
        @pl.when(pl.program_id(0) == 0)
        def _():
            dst[...] = jnp.zeros_like(dst)
            dal_ref[...] = jnp.zeros_like(dal_ref)
            ddb_ref[...] = jnp.zeros_like(ddb_ref)

        S = tuple(s_ref[n // H, 0, n % H] for n in range(NU))
        t_rows = _rows(t_ref)
        fn = lambda *a: gd_chunk(*a, t_saved=t_rows)[0:2]
        _, vjp = jax.vjp(fn, S, _rows(c_ref), _rows(ab_ref), al_ref[...], db_ref[...])
        dS, dc, dab, dal, ddb = vjp((tuple(dst[n] for n in range(NU)), _rows(do_ref)))
        for n in range(NU):
            dst[n] = dS[n]
        for b in range(NB):
            dc_ref[b] = dc[b]
            dab_ref[b] = dab[b].astype(bf16)
        dal_ref[...] += jnp.broadcast_to(dal, (8, DK))
        ddb_ref[...] += jnp.broadcast_to(ddb, (8, DK))

    acc = pl.BlockSpec((8, DK), lambda c: (0, 0))
    return pl.pallas_call(
        body, name="gd_bwd", grid=(NCH,),
        in_specs=[_tok_spec(3 * HD, _REV)] + _gd_extra_specs(_REV)
        + [_state_spec(_REV), _tok_spec(HD, _REV), _tok_spec(HD, _REV)],
        out_specs=[_tok_spec(3 * HD, _REV), _tok_spec(DK, _REV), acc, acc],
        out_shape=[jax.ShapeDtypeStruct((NB, TP, 3 * HD), f32), jax.ShapeDtypeStruct((NB, TP, DK), bf16),
                   jax.ShapeDtypeStruct((8, DK), f32), jax.ShapeDtypeStruct((8, DK), f32)],
        scratch_shapes=[pltpu.VMEM((NU, DK, DK), f32)],
        compiler_params=_cp(("arbitrary",)),
    )(cv, proj3, alog, dtb, s_saved, t_saved, do)


def _conv_bwd(proj3, conv4, dy):
    def body(x_ref, w_ref, dy_ref, dx_ref, dw_ref):
        @pl.when(pl.program_id(1) == 0)
        def _():
            dw_ref[...] = jnp.zeros_like(dw_ref)

        x = x_ref[0]
        row = lax.broadcasted_iota(jnp.int32, (TP, 1), 0)
        g = jnp.where(row >= 8, dy_ref[0], 0.0)
        dx = w_ref[3] * g
        dw_ref[3] += jnp.broadcast_to(jnp.sum(x * g, axis=0, keepdims=True), (8, HD))
        for s in (1, 2, 3):
            dx = dx + w_ref[3 - s] * pltpu.roll(g, TP - s, 0)
            dw_ref[3 - s] += jnp.broadcast_to(jnp.sum(pltpu.roll(x, s, 0) * g, axis=0, keepdims=True), (8, HD))
        dx_ref[0] = dx.astype(bf16)

    return pl.pallas_call(
        body, name="conv_bwd", grid=(3, NB),
        in_specs=[pl.BlockSpec((1, TP, HD), lambda j, b: (b, 0, COL_QKV // HD + j)),
                  pl.BlockSpec((4, 1, HD), lambda j, b: (0, 0, j)), pl.BlockSpec((1, TP, HD), lambda j, b: (b, 0, j))],
        out_specs=[pl.BlockSpec((1, TP, HD), lambda j, b: (b, 0, j)), pl.BlockSpec((4, 8, HD), lambda j, b: (0, 0, j))],
        out_shape=[jax.ShapeDtypeStruct((NB, TP, 3 * HD), bf16), jax.ShapeDtypeStruct((4, 8, 3 * HD), f32)],
        compiler_params=_cp(("arbitrary", "arbitrary")),
    )(proj3, conv4, dy)


def _out_loss(o_hg, o_gd, proj, hgw, gdw, wout, hflat, fw, target):
    tm = 384

    def body(ohg_ref, ogd_ref, zhg_ref, zgd_ref, hgw_ref, gdw_ref, wo_ref, h_ref, fw_ref, tg_ref,
             loss_ref, dohg_ref, dogd_ref, dzhg_ref, dzgd_ref, dh_ref, dwo_ref, dhgw_ref, dgdw_ref, dfw_ref):
        i = pl.program_id(0)

        @pl.when(i == 0)
        def _():
            for r in (loss_ref, dwo_ref, dhgw_ref, dgdw_ref, dfw_ref):
                r[...] = jnp.zeros_like(r)

        row = i * tm + lax.broadcasted_iota(jnp.int32, (tm, 1), 0)
        tok = jnp.where(row >= TP, row - TP, row)
        valid = (tok >= PAD + N_META).astype(f32)
        hval = h_ref[...]
        tgt = tg_ref[...]

        mixers = ((ohg_ref, zhg_ref, hgw_ref[...]), (ogd_ref, zgd_ref, gdw_ref[...]))
        saved, ys = [], []
        for o_ref, z_ref, w in mixers:
            for hh in range(H):
                sl = slice(hh * DK, (hh + 1) * DK)
                o, z = o_ref[:, sl], z_ref[:, sl]
                r = lax.rsqrt(jnp.mean(o * o, axis=-1, keepdims=True) + EPS)
                n = o * r
                sg = sigmoid(z)
                ws = w * (z * sg)
                saved.append((r, n, sg, z, ws, w))
                ys.append(n * ws)
        y = jnp.concatenate(ys, axis=-1)
        h2 = hval + bdot(y, wo_ref[...])
        r2 = lax.rsqrt(jnp.mean(h2 * h2, axis=-1, keepdims=True) + EPS)
        n2 = h2 * r2
        fwv = fw_ref[...]
        err = (n2 * fwv - tgt) * valid
        loss = (0.5 / D) * jnp.sum(err * err)
        dyf = err * (1.0 / D)
        dn2 = dyf * fwv
        dout = r2 * (dn2 - n2 * jnp.mean(dn2 * n2, axis=-1, keepdims=True))
        dh_ref[...] = dout
        dy = bdot_nt(dout, wo_ref[...])
        dwo_ref[...] += bdot_tn(y, dout)
        dws = []
        for mi, (do_ref, dz_ref) in enumerate(((dohg_ref, dzhg_ref), (dogd_ref, dzgd_ref))):
            dw = jnp.zeros((1, DK), f32)
            for hh in range(H):
                sl = slice(hh * DK, (hh + 1) * DK)
                r, n, sg, z, ws, w = saved[mi * H + hh]
                dyh = dy[:, mi * HD + hh * DK:mi * HD + (hh + 1) * DK]
                t = dyh * n
                dw = dw + jnp.sum(t * (z * sg), axis=0, keepdims=True)
                dz_ref[:, sl] = (t * w * (sg * (1.0 + z * (1.0 - sg)))).astype(bf16)
                dn = dyh * ws
                do_ref[:, sl] = r * (dn - n * jnp.mean(dn * n, axis=-1, keepdims=True))
            dws.append(dw)
        loss_ref[...] += jnp.broadcast_to(loss, (8, DK))
        dhgw_ref[...] += jnp.broadcast_to(dws[0], (8, DK))
        dgdw_ref[...] += jnp.broadcast_to(dws[1], (8, DK))
        dfw_ref[...] += jnp.broadcast_to(jnp.sum(dyf * n2, axis=0, keepdims=True), (8, D))

    row = lambda w: pl.BlockSpec((tm, w), lambda i: (i, 0))
    whole = lambda r, w: pl.BlockSpec((r, w), lambda i: (0, 0))
    col = lambda c0: pl.BlockSpec((tm, HD), lambda i: (i, c0 // HD))
    return pl.pallas_call(
        body, name="out_loss", grid=(N // tm,),
        in_specs=[row(HD), row(HD), col(COL_ZHG), col(COL_ZGD),
                  whole(1, DK), whole(1, DK), whole(D, D), row(D), whole(1, D), row(D)],
        out_specs=[whole(8, DK), row(HD), row(HD), row(HD), row(HD), row(D), whole(D, D),
                   whole(8, DK), whole(8, DK), whole(8, D)],
        out_shape=[jax.ShapeDtypeStruct((8, DK), f32)] + [jax.ShapeDtypeStruct((N, HD), f32)] * 2
        + [jax.ShapeDtypeStruct((N, HD), bf16)] * 2
        + [jax.ShapeDtypeStruct((N, D), f32), jax.ShapeDtypeStruct((D, D), f32),
           jax.ShapeDtypeStruct((8, DK), f32), jax.ShapeDtypeStruct((8, DK), f32), jax.ShapeDtypeStruct((8, D), f32)],
        compiler_params=_cp(("arbitrary",)),
    )(o_hg, o_gd, proj, proj, hgw, gdw, wout, hflat, fw, target)


def _in_bwd(pieces, wbig, hflat, norm_w, dh_res, pbs):
    tm = 384
    nsteps = N // tm
    np_ = len(pieces)
    na = len(pbs)
    offs = [c0 for _, c0 in pieces]
    widths = [d.shape[1] for d, _ in pieces]

    def body(*refs):
        d_refs = refs[:np_]
        w_ref, h_ref, nw_ref, dhr_ref = refs[np_:np_ + 4]
        srcs = refs[np_ + 4:np_ + 4 + na]
        dh_ref, dnw_ref = refs[np_ + 4 + na:np_ + 6 + na]
        dsts = refs[np_ + 6 + na:np_ + 6 + 2 * na]
        sems = refs[np_ + 6 + 2 * na:]
        i = pl.program_id(0)

        def copies():
            if not na:
                return []
            x, y, c, chips = _place()
            return [pltpu.make_async_remote_copy(
                src_ref=srcs[a].at[2 * px + py], dst_ref=dsts[a].at[j], send_sem=sems[0].at[na * j + a],
                recv_sem=sems[1].at[na * j + a], device_id=(px, py, c), device_id_type=MESH)
                for j, (px, py) in enumerate(chips) for a in range(na)]

        @pl.when(i == 0)
        def _():
            dnw_ref[...] = jnp.zeros_like(dnw_ref)
            for cp in copies():
                cp.start()

        du = jnp.zeros((tm, D), f32)
        for d_ref, off, wd in zip(d_refs, offs, widths):
            du = du + bdot(d_ref[...], w_ref[off:off + wd, :])
        _, vjp = jax.vjp(rmsnorm, h_ref[...], nw_ref[...])
        dh, dnw = vjp(du)
        dh_ref[...] = dh + dhr_ref[...]
        dnw_ref[...] += jnp.broadcast_to(dnw, (8, D))

        @pl.when(i == nsteps - 1)
        def _():
            for cp in copies():
                cp.wait()

    row = lambda w: pl.BlockSpec((tm, w), lambda i: (i, 0))
    return pl.pallas_call(
        body, name="in_bwd", grid=(nsteps,),
        in_specs=[row(w) for w in widths]
        + [pl.BlockSpec((PC, D), lambda i: (0, 0)), row(D), pl.BlockSpec((1, D), lambda i: (0, 0)), row(D)] + [ANY] * na,
        out_specs=[row(D), pl.BlockSpec((8, D), lambda i: (0, 0))] + [ANY] * na,
        out_shape=[jax.ShapeDtypeStruct((N, D), f32), jax.ShapeDtypeStruct((8, D), f32)]
        + [jax.ShapeDtypeStruct((3,) + p.shape[1:], p.dtype) for p in pbs],
        scratch_shapes=[pltpu.SemaphoreType.DMA((3 * na,)), pltpu.SemaphoreType.DMA((3 * na,))] if na else [],
        compiler_params=_cp(("arbitrary",)),
    )(*[d for d, _ in pieces], wbig, hflat, norm_w, dh_res, *pbs)


def _w_grad(ut, pieces):
    tk = N // 3
    offs = [c0 for _, c0 in pieces]
    widths = [d.shape[1] for d, _ in pieces]

    def body(u_ref, *refs):
        d_refs, o_ref = refs[:-1], refs[-1]

        @pl.when(pl.program_id(0) == 0)
        def _():
            o_ref[...] = jnp.zeros_like(o_ref)

        u = u_ref[...]
        for d_ref, off, wd in zip(d_refs, offs, widths):
            o_ref[:, off:off + wd] += jnp.dot(u, d_ref[...], preferred_element_type=f32)

    return pl.pallas_call(
        body, name="w_grad", grid=(N // tk,),
        in_specs=[pl.BlockSpec((D, tk), lambda k: (0, k))] + [pl.BlockSpec((tk, w), lambda k: (k, 0)) for w in widths],
        out_specs=pl.BlockSpec((D, PC), lambda k: (0, 0), pipeline_mode=pl.Buffered(1)),
        out_shape=jax.ShapeDtypeStruct((D, PC), f32),
        compiler_params=_cp(("arbitrary",)),
    )(ut, *[d for d, _ in pieces])


def _adam_math(g, w, m, v):
    m2 = ADAM_B1 * m + (1.0 - ADAM_B1) * g
    v2 = ADAM_B2 * v + (1.0 - ADAM_B2) * (g * g)
    m_hat = m2 / (1.0 - ADAM_B1 ** ADAM_STEP)
    v_hat = v2 / (1.0 - ADAM_B2 ** ADAM_STEP)
    delta = -ADAM_LR * (m_hat / (jnp.sqrt(v_hat) + ADAM_EPS) + ADAM_WD * w)
    return delta, m2, v2


def _adamw(gs, w, m, v, name):
    R, Cc = w.shape
    tr = 256 if R % 256 == 0 else R
    ng = len(gs)

    def body(*refs):
        g = refs[0][...]
        for r in refs[1:ng]:
            g = g + r[...]
        w_ref, m_ref, v_ref, g_ref, d_ref, m2_ref, v2_ref = refs[ng:]
        delta, m2, v2 = _adam_math(g, w_ref[...], m_ref[...], v_ref[...])
        g_ref[...] = g
        d_ref[...] = delta
        m2_ref[...] = m2
        v2_ref[...] = v2

    spec = pl.BlockSpec((tr, Cc), lambda i: (i, 0))
    return pl.pallas_call(
        body, name=name, grid=(R // tr,),
        in_specs=[spec] * (ng + 3), out_specs=[spec] * 4,
        out_shape=[jax.ShapeDtypeStruct((R, Cc), f32)] * 4,
        compiler_params=_cp(("arbitrary",)),
    )(*gs, w, m, v)


def _adamw_rows(g, w, m, v, name):
    R, _, Cc = w.shape
    tr = R // 9

    def body(g_ref, w_ref, m_ref, v_ref, go_ref, d_ref, m2_ref, v2_ref):
        g = g_ref[...]
        delta, m2, v2 = _adam_math(g, w_ref[...], m_ref[...], v_ref[...])
        go_ref[...] = g
        d_ref[...] = delta
        m2_ref[...] = m2
        v2_ref[...] = v2

    spec = pl.BlockSpec((tr, 1, Cc), lambda i: (i, 0, 0))
    return pl.pallas_call(
        body, name=name, grid=(R // tr,),
        in_specs=[spec] * 4, out_specs=[spec] * 4,
        out_shape=[jax.ShapeDtypeStruct((R, 1, Cc), f32)] * 4,
        compiler_params=_cp(("arbitrary",)),
    )(g, w, m, v)


def _place():
    x, y, c = lax.axis_index("x"), lax.axis_index("y"), lax.axis_index("c")
    return x, y, c, [(1 - x, y), (x, 1 - y), (1 - x, 1 - y)]


def _gather_weights(cm, halved, whole):
    R, _, Cc = cm.shape
    shards = [jax.ShapeDtypeStruct((2, R, Cc // 2), bf16)] + list(halved) + list(whole)
    nh = 1 + len(halved)
    na = len(shards)

    def body(*refs):
        srcs, dsts = refs[:na], refs[na:2 * na]
        send_sems, recv_sems, loc_sems = refs[2 * na:2 * na + 3]
        stage = refs[2 * na + 3:3 * na + 3]
        raw = refs[3 * na + 3]
        x, y, c, chips = _place()
        me = 2 * x + y
        loads = [pltpu.make_async_copy(srcs[0], raw, loc_sems.at[0])]
        loads += [pltpu.make_async_copy(srcs[i], stage[i], loc_sems.at[i]) for i in range(1, na)]
        locs = [pltpu.make_async_copy(v, d.at[me], loc_sems.at[i]) for i, (v, d) in enumerate(zip(stage, dsts))]
        for cp in loads:
            cp.start()

        def ici(j, i, slot):
            px, py = chips[j]
            src = (stage[0] if i == 0 else srcs[i]).at[c] if i < nh else srcs[i]
            dst = dsts[i].at[slot, c] if i < nh else dsts[i].at[slot]
            return pltpu.make_async_remote_copy(
                src_ref=src, dst_ref=dst, send_sem=send_sems.at[na * j + i], recv_sem=recv_sems.at[na * j + i],
                device_id=(px, py, c), device_id_type=MESH)

        def d2d(j, i, half):
            px, py = chips[j]
            blk = dsts[i].at[2 * px + py, half]
            return pltpu.make_async_remote_copy(
                src_ref=blk, dst_ref=blk, send_sem=send_sems.at[3 * na + nh * j + i],
                recv_sem=recv_sems.at[3 * na + nh * j + i], device_id=(x, y, 1 - c), device_id_type=MESH)

        sends = [ici(j, i, me) for j in range(3) for i in range(1, na)]
        for cp in sends:
            cp.start()
        loads[0].wait()
        v = raw[:, 0, :]
        for hc in range(2):
            stage[0][hc] = v[:, hc * (Cc // 2):(hc + 1) * (Cc // 2)].astype(bf16)
        first = [ici(j, 0, me) for j in range(3)]
        for cp in first:
            cp.start()
        sends += first
        locs[0].start()
        for ld, st in zip(loads[1:], locs[1:]):
            ld.wait()
            st.start()
        for j, (px, py) in enumerate(chips):
            for i in range(na):
                ici(j, i, 2 * px + py).wait_recv()
                if i < nh:
                    fwd = d2d(j, i, c)
                    fwd.start()
                    sends.append(fwd)
        for j in range(3):
            for i in range(nh):
                d2d(j, i, 1 - c).wait_recv()
        for cp in sends:
            cp.wait_send()
        for cp in locs:
            cp.wait()

    nsem = 3 * na + 3 * nh
    return pl.pallas_call(
        body, name="gather_weights",
        in_specs=[ANY] * na, out_specs=[ANY] * na,
        out_shape=[jax.ShapeDtypeStruct((4,) + s.shape, s.dtype) for s in shards],
        scratch_shapes=[pltpu.SemaphoreType.DMA((nsem,)), pltpu.SemaphoreType.DMA((nsem,)),
                        pltpu.SemaphoreType.DMA((na,))] + [pltpu.VMEM(s.shape, s.dtype) for s in shards]
        + [pltpu.VMEM(cm.shape, cm.dtype)],
        compiler_params=pltpu.CompilerParams(has_side_effects=True, vmem_limit_bytes=VMEM_LIMIT),
    )(cm, *halved, *whole)


def _swap_halves(gs):
    na = len(gs)
    jobs = [(i, q) for i in range(na) for q in range(gs[i].shape[0])]

    def body(*refs):
        srcs, dsts = refs[:na], refs[na:2 * na]
        send_sems, recv_sems = refs[2 * na:]
        x, y, c, _ = _place()
        cps = [pltpu.make_async_remote_copy(
            src_ref=srcs[i].at[q, 1 - c], dst_ref=dsts[i].at[q], send_sem=send_sems.at[k],
            recv_sem=recv_sems.at[k], device_id=(x, y, 1 - c), device_id_type=MESH)
            for k, (i, q) in enumerate(jobs)]
        for cp in cps:
            cp.start()
        for cp in cps:
            cp.wait()

    return pl.pallas_call(
        body, name="swap_halves",
        in_specs=[ANY] * na, out_specs=[ANY] * na,
        out_shape=[jax.ShapeDtypeStruct(g.shape[0:1] + g.shape[2:], g.dtype) for g in gs],
        scratch_shapes=[pltpu.SemaphoreType.DMA((len(jobs),)), pltpu.SemaphoreType.DMA((len(jobs),))],
        compiler_params=pltpu.CompilerParams(has_side_effects=True),
    )(*gs)


def _add_halves(c_arr, g, s, name):
    Q, _, R, Cc = g.shape
    tr = min(R, 128)

    def body(c_ref, g_ref, s_ref, b_ref, f_ref):
        p = g_ref[0, 0] + s_ref[0]
        f_ref[0] = p
        b_ref[0] = p.astype(bf16)

    blk = pl.BlockSpec((1, tr, Cc), lambda q, i, cr: (q, i, 0))
    return pl.pallas_call(
        body, name=name,
        grid_spec=pltpu.PrefetchScalarGridSpec(
            num_scalar_prefetch=1, grid=(Q, R // tr),
            in_specs=[pl.BlockSpec((1, 1, tr, Cc), lambda q, i, cr: (q, cr[0], i, 0)), blk], out_specs=[blk, blk]),
        out_shape=[jax.ShapeDtypeStruct((Q, R, Cc), bf16), jax.ShapeDtypeStruct((Q, R, Cc), f32)],
        compiler_params=_cp(("arbitrary", "arbitrary")),
    )(c_arr, g, s)


_FLIPS = [(fx, fy, fc) for fx in (0, 1) for fy in (0, 1) for fc in (0, 1)][1:]


def _sum_blocks(own, r, name):
    R, Cc = own.shape
    tr = min(R, 256)

    def body(own_ref, r_ref, o_ref):
        acc = own_ref[...]
        for j in range(3):
            acc = acc + r_ref[j].astype(f32)
        o_ref[...] = acc

    return pl.pallas_call(
        body, name=name, grid=(R // tr,),
        in_specs=[pl.BlockSpec((tr, Cc), lambda i: (i, 0)), pl.BlockSpec((3, tr, Cc), lambda i: (0, i, 0))],
        out_specs=pl.BlockSpec((tr, Cc), lambda i: (i, 0)),
        out_shape=jax.ShapeDtypeStruct((R, Cc), f32),
        compiler_params=_cp(("arbitrary",)),
    )(own, r)


def _sum_packs(me8_arr, pack, rp):
    R = pack.shape[0]

    def body(me_ref, pk_ref, rp_ref, o_ref):
        me8 = me_ref[0]
        acc = None
        for d in range(8):
            rel = d ^ me8
            term = jnp.where(rel == 0, pk_ref[...], rp_ref[jnp.maximum(rel - 1, 0)])
            acc = term if acc is None else acc + term
        o_ref[...] = acc

    return pl.pallas_call(
        body, name="sum_packs",
        grid_spec=pltpu.PrefetchScalarGridSpec(
            num_scalar_prefetch=1, grid=(1,),
            in_specs=[pl.BlockSpec((R, 128), lambda i, mr: (0, 0)), pl.BlockSpec((7, R, 128), lambda i, mr: (0, 0, 0))],
            out_specs=pl.BlockSpec((R, 128), lambda i, mr: (0, 0))),
        out_shape=jax.ShapeDtypeStruct((R, 128), f32),
        compiler_params=_cp(("arbitrary",)),
    )(me8_arr, pack, rp)


def _swap_finished(fs, pack):
    na = len(fs)
    R = pack.shape[0]

    def body(*refs):
        srcs, pk = refs[:na], refs[na]
        dsts, rp = refs[na + 1:2 * na + 1], refs[2 * na + 1]
        send_sems, recv_sems = refs[2 * na + 2:]
        x, y, c, _ = _place()
        cps = [pltpu.make_async_remote_copy(
            src_ref=srcs[i], dst_ref=dsts[i], send_sem=send_sems.at[i], recv_sem=recv_sems.at[i],
            device_id=(x, y, 1 - c), device_id_type=MESH) for i in range(na)]
        cps += [pltpu.make_async_remote_copy(
            src_ref=pk, dst_ref=rp.at[k], send_sem=send_sems.at[na + k], recv_sem=recv_sems.at[na + k],
            device_id=(x ^ fx, y ^ fy, c ^ fc), device_id_type=MESH) for k, (fx, fy, fc) in enumerate(_FLIPS)]
        for cp in cps:
            cp.start()
        for cp in cps:
            cp.wait()

    return pl.pallas_call(
        body, name="swap_finished",
        in_specs=[ANY] * (na + 1), out_specs=[ANY] * (na + 1),
        out_shape=[jax.ShapeDtypeStruct(f.shape, f.dtype) for f in fs] + [jax.ShapeDtypeStruct((7, R, 128), f32)],
        scratch_shapes=[pltpu.SemaphoreType.DMA((na + 7,)), pltpu.SemaphoreType.DMA((na + 7,))],
        compiler_params=pltpu.CompilerParams(has_side_effects=True),
    )(*fs, pack)


def _adamw_halves(c_arr, mine, peer, w, m, v, name):
    _, R, Cc = w.shape
    tr = min(R, 256)

    def body(c_ref, mine_ref, peer_ref, w_ref, m_ref, v_ref, g_ref, d_ref, m2_ref, v2_ref):
        g = jnp.where(pl.program_id(0) == c_ref[0], mine_ref[...], peer_ref[...])
        delta, m2, v2 = _adam_math(g, w_ref[0], m_ref[0], v_ref[0])
        g_ref[0] = g
        d_ref[0] = delta
        m2_ref[0] = m2
        v2_ref[0] = v2

    half = pl.BlockSpec((tr, Cc), lambda hh, i, cr: (i, 0))
    full = pl.BlockSpec((1, tr, Cc), lambda hh, i, cr: (hh, i, 0))
    return pl.pallas_call(
        body, name=name,
        grid_spec=pltpu.PrefetchScalarGridSpec(
            num_scalar_prefetch=1, grid=(2, R // tr), in_specs=[half, half, full, full, full], out_specs=[full] * 4),
        out_shape=[jax.ShapeDtypeStruct((2, R, Cc), f32)] * 4,
        compiler_params=_cp(("arbitrary", "arbitrary")),
    )(c_arr, mine, peer, w, m, v)


def _rows8(a):
    flat = a.reshape(-1)
    n = flat.shape[0]
    rows = -(-n // 1024) * 8
    return jnp.pad(flat, (0, rows * 128 - n)).reshape(rows, 128)


def kernel(x, meta_tokens, norm_w, w_in, conv_w, hg_lb_logits, hg_norm_w, gdn_A_log, gdn_dt_bias, gdn_norm_w, w_out, final_norm_w, loss_target, m_meta_tokens, m_norm_w, m_w_in, m_conv_w, m_hg_lb_logits, m_hg_norm_w, m_gdn_A_log, m_gdn_dt_bias, m_gdn_norm_w, m_w_out, m_final_norm_w, v_meta_tokens, v_norm_w, v_w_in, v_conv_w, v_hg_lb_logits, v_hg_norm_w, v_gdn_A_log, v_gdn_dt_bias, v_gdn_norm_w, v_w_out, v_final_norm_w):
    me = 2 * lax.axis_index("x") + lax.axis_index("y")

    g_win, g_wout, g_conv, g_meta = _gather_weights(
        jnp.transpose(w_in, (2, 0, 1)), [w_out[0].astype(bf16).reshape(2, D // 8, D)], [conv_w[0], meta_tokens])
    wt_full = jnp.transpose(g_win, (0, 2, 1, 3)).reshape(IN_COLS, D)
    wbig = jnp.pad(wt_full, ((0, PC - IN_COLS), (0, 0)))
    wout_full = g_wout.reshape(D, D)
    conv4 = jnp.transpose(g_conv, (1, 0, 2)).reshape(4, 1, 3 * HD)
    meta_full = jnp.transpose(g_meta, (1, 0, 2)).reshape(N_META, D)

    c_arr = lax.axis_index("c").reshape(1).astype(jnp.int32)

    def chip_partials(gw, g_wout_part):
        g_in2 = gw.reshape(1, 2, D // 2, PC)
        g_out4 = g_wout_part.reshape(4, 2, D // 8, D)
        s_in, s_out = _swap_halves([g_in2, g_out4])
        pb_in, pf_in = _add_halves(c_arr, g_in2, s_in, "add_w_in")
        pb_out, pf_out = _add_halves(c_arr, g_out4, s_out, "add_w_out")
        pb_blocks = jnp.transpose(pb_in[0, :, 0:IN_COLS].reshape(D // 2, 4, SHARD_COLS), (1, 0, 2))
        own_in = lax.dynamic_slice(pf_in[0], (0, me * SHARD_COLS), (D // 2, SHARD_COLS))
        own_out = lax.dynamic_index_in_dim(pf_out, me, axis=0, keepdims=False)
        return [pb_blocks, pb_out], [own_in, own_out]

    (loss8, grad_x, d_meta, d_nw, d_conv, d_lb, d_hgw, d_alog, d_dtb, d_gdw, d_fw, pfs, rs) = _local_step(
        x, loss_target, wbig, wout_full, conv4, meta_full, norm_w, hg_lb_logits, hg_norm_w, gdn_A_log, gdn_dt_bias,
        gdn_norm_w, final_norm_w, chip_partials)

    pack = jnp.concatenate([
        loss8, d_nw[0].reshape(8, 128), d_lb.reshape(8, 128), d_hgw, _rows8(d_alog[0, :H]), _rows8(d_dtb[0, :H]),
        d_gdw, d_fw[0].reshape(8, 128), d_meta.reshape(128, 128), d_conv.reshape(48, 128)], axis=0)
    return _reduce_and_update(
        me, c_arr, grad_x, pfs, rs, pack, meta_tokens, norm_w, w_in, conv_w, hg_lb_logits, hg_norm_w, gdn_A_log,
        gdn_dt_bias, gdn_norm_w, w_out, final_norm_w, m_meta_tokens, m_norm_w, m_w_in, m_conv_w, m_hg_lb_logits,
        m_hg_norm_w, m_gdn_A_log, m_gdn_dt_bias, m_gdn_norm_w, m_w_out, m_final_norm_w, v_meta_tokens, v_norm_w, v_w_in,
        v_conv_w, v_hg_lb_logits, v_hg_norm_w, v_gdn_A_log, v_gdn_dt_bias, v_gdn_norm_w, v_w_out, v_final_norm_w)


def _local_step(x, loss_target, wbig, wout_full, conv4, meta_full, norm_w, hg_lb_logits, hg_norm_w, gdn_A_log, gdn_dt_bias,
                gdn_norm_w, final_norm_w, chip_partials):
    h3 = jnp.concatenate([jnp.zeros((NB, PAD, D), f32), jnp.broadcast_to(meta_full[None], (NB, N_META, D)), x], axis=1)
    hflat = h3.reshape(N, D)
    target = jnp.pad(loss_target, ((0, 0), (PAD + N_META, 0), (0, 0))).reshape(N, D)
    l0, l1 = hg_lb_logits[0:1], hg_lb_logits[1:2]
    alog = jnp.pad(gdn_A_log, ((0, 0), (0, DK - H)))
    dtb = jnp.pad(gdn_dt_bias, ((0, 0), (0, DK - H)))
    fw = final_norm_w.reshape(1, D)

    proj, ut, cv2 = _in_proj(hflat, norm_w, wbig, conv4)
    proj3 = proj.reshape(NB, TP, PC)
    cv = cv2.reshape(NB, TP, 3 * HD)
    o_hg, s_hg, o_gd, s_gd, t_gd = _mix_fwd(proj3, cv, l0, l1, alog, dtb)
    (loss8, d_ohg, d_ogd, d_zhg, d_zgd, dh_res, g_wout_part, d_hgw, d_gdw, d_fw) = _out_loss(
        o_hg.reshape(N, HD), o_gd.reshape(N, HD), proj, hg_norm_w, gdn_norm_w, wout_full, hflat, fw, target)
    d_hg, d_l0, d_l1 = _hg_bwd(proj3, l0, l1, s_hg, d_ohg.reshape(NB, TP, HD))
    d_cv, d_ab, d_alog, d_dtb = _gd_bwd(cv, proj3, alog, dtb, s_gd, t_gd, d_ogd.reshape(NB, TP, HD))
    d_qkv, d_conv4 = _conv_bwd(proj3, conv4, d_cv)
    d_hg2, d_qkv2, d_ab2 = d_hg.reshape(N, 3 * HD), d_qkv.reshape(N, 3 * HD), d_ab.reshape(N, DK)
    pieces = [(d_hg2, COL_HG), (d_zhg, COL_ZHG), (d_qkv2, COL_QKV), (d_zgd, COL_ZGD), (d_ab2, COL_AB)]
    gw = _w_grad(ut, pieces)
    pbs, pfs = chip_partials(gw, g_wout_part) if chip_partials else ([], [gw, g_wout_part])
    dh, d_nw, *rs = _in_bwd(pieces, wbig, hflat, norm_w, dh_res, pbs)

    dh3 = dh.reshape(NB, TP, D)
    grad_x = dh3[:, PAD + N_META:, :]
    d_meta = jnp.sum(dh3[:, PAD:PAD + N_META, :], axis=0)
    d_conv = d_conv4[:, 0, :]
    d_lb = jnp.concatenate([d_l0[0:1], d_l1[0:1]], axis=0)
    return loss8, grad_x, d_meta, d_nw, d_conv, d_lb, d_hgw, d_alog, d_dtb, d_gdw, d_fw, pfs, rs


def _reduce_and_update(me, c_arr, grad_x, pfs, rs, pack, meta_tokens, norm_w, w_in, conv_w, hg_lb_logits, hg_norm_w,
                       gdn_A_log, gdn_dt_bias, gdn_norm_w, w_out, final_norm_w, m_meta_tokens, m_norm_w, m_w_in, m_conv_w,
                       m_hg_lb_logits, m_hg_norm_w, m_gdn_A_log, m_gdn_dt_bias, m_gdn_norm_w, m_w_out, m_final_norm_w,
                       v_meta_tokens, v_norm_w, v_w_in, v_conv_w, v_hg_lb_logits, v_hg_norm_w, v_gdn_A_log, v_gdn_dt_bias,
                       v_gdn_norm_w, v_w_out, v_final_norm_w):
    (own_in, own_out), (r_in, r_out) = pfs, rs
    f_in = _sum_blocks(own_in, r_in, "sum_w_in")
    f_out = _sum_blocks(own_out, r_out, "sum_w_out")
    o_in, o_out, r_pack = _swap_finished([f_in, f_out], pack)
    me8_arr = (2 * me + lax.axis_index("c")).reshape(1).astype(jnp.int32)
    small = _sum_packs(me8_arr, pack, r_pack)

    half_out = lambda a: a[0].reshape(2, D // 8, D)
    is0 = lax.axis_index("c") == 0
    g_in = jnp.concatenate([jnp.where(is0, f_in, o_in), jnp.where(is0, o_in, f_in)], axis=0)
    to_cm = lambda a: jnp.transpose(a, (2, 0, 1))
    gi, di, mi, vi = [jnp.transpose(a, (1, 2, 0))[0] for a in _adamw_rows(
        g_in.T.reshape(SHARD_COLS, 1, D), to_cm(w_in), to_cm(m_w_in), to_cm(v_w_in), "adamw_w_in")]
    go, do_, mo, vo = [a.reshape(D // 4, D) for a in _adamw_halves(
        c_arr, f_out, o_out, half_out(w_out), half_out(m_w_out), half_out(v_w_out), "adamw_w_out")]

    g_meta_full = small[64:192].reshape(N_META, D)
    g_meta_loc = lax.dynamic_slice(g_meta_full, (0, me * 256), (N_META, 256))
    gm, dm, mm_, vm = _adamw([g_meta_loc], meta_tokens, m_meta_tokens, v_meta_tokens, "adamw_meta")
    g_conv_full = small[192:240].reshape(4, 1536)
    g_conv_loc = lax.dynamic_slice(g_conv_full, (0, me * 384), (4, 384))
    gc, dc, mc, vc = _adamw([g_conv_loc], conv_w[0], m_conv_w[0], v_conv_w[0], "adamw_conv")

    reps = [(norm_w, m_norm_w, v_norm_w), (hg_lb_logits, m_hg_lb_logits, v_hg_lb_logits),
            (hg_norm_w, m_hg_norm_w, v_hg_norm_w), (gdn_A_log, m_gdn_A_log, v_gdn_A_log),
            (gdn_dt_bias, m_gdn_dt_bias, v_gdn_dt_bias), (gdn_norm_w, m_gdn_norm_w, v_gdn_norm_w),
            (final_norm_w, m_final_norm_w, v_final_norm_w)]
    wp = jnp.concatenate([_rows8(t[0]) for t in reps], axis=0)
    mp = jnp.concatenate([_rows8(t[1]) for t in reps], axis=0)
    vp = jnp.concatenate([_rows8(t[2]) for t in reps], axis=0)
    gr, dr, mr, vr = _adamw([small[8:64]], wp, mp, vp, "adamw_small")

    def unpack(p):
        outs = []
        for i, t in enumerate(reps):
            n = t[0].size
            outs.append(p[8 * i:8 * i + 8].reshape(-1)[:n].reshape(t[0].shape))
        return outs

    def leaves(meta_v, conv_v, in_v, out_v, rep_p):
        nw, lb, hgw, al, db, gdw, fwv = unpack(rep_p)
        return [meta_v, nw, in_v[None], conv_v[None], lb, hgw, al, db, gdw, out_v[None], fwv]

    loss = small[0, 0]
    return (loss, grad_x, *leaves(gm, gc, gi, go, gr), *leaves(dm, dc, di, do_, dr),
            *leaves(mm_, mc, mi, mo, mr), *leaves(vm, vc, vi, vo, vr))
```

```python
import functools

import jax
import jax.numpy as jnp
from jax import lax
from jax.experimental import pallas as pl
from jax.experimental.pallas import tpu as pltpu

f32 = jnp.float32
bf16 = jnp.bfloat16
MESH = pl.DeviceIdType.MESH
ANY = pl.BlockSpec(memory_space=pl.ANY)

D = 1024
NB = 2
N_META = 16
SEQ = 2048
PAD = 48
TP = PAD + N_META + SEQ
C = 64
NCH = TP // C
N = NB * TP
H = 4
DK = 128
HD = H * DK
PC = 4224
IN_COLS = 4104
SHARD_COLS = IN_COLS // 4
COL_HG, COL_ZHG, COL_QKV, COL_ZGD, COL_AB = 0, 3 * HD, 4 * HD, 7 * HD, 8 * HD
EPS = 1e-6
ADAM_LR, ADAM_B1, ADAM_B2, ADAM_EPS, ADAM_WD, ADAM_STEP = 0.001, 0.9, 0.999, 1e-08, 0.01, 10
VMEM_LIMIT = 56 * 1024 * 1024

P_HG = dict(lvl=1, av=1, qs=1, su=1)
P_GD = dict(kk=1, inv=1, sol=1, ws=1, qk=1, o=1, su=1)


def _cp(sem=None, **kw):
    return pltpu.CompilerParams(dimension_semantics=sem, vmem_limit_bytes=VMEM_LIMIT, **kw)


_DIMS = {"nn": (((1,), (0,)), ((), ())), "nt": (((1,), (1,)), ((), ())), "tn": (((0,), (0,)), ((), ()))}


def _split(x):
    hi = x.astype(bf16)
    return hi, (x - hi.astype(f32)).astype(bf16)


def _dg(a, b, kind, passes):
    d = lambda x, y: lax.dot_general(x, y, _DIMS[kind], preferred_element_type=f32)
    if passes == 1:
        return d(a.astype(bf16), b.astype(bf16))
    ah, al = _split(a)
    bh, bl = _split(b)
    return d(ah, bh) + d(ah, bl) + d(al, bh)


@functools.partial(jax.custom_vjp, nondiff_argnums=(2, 3))
def mmx(a, b, kind, passes):
    return _dg(a, b, kind, passes)


def _mmx_fwd(a, b, kind, passes):
    return _dg(a, b, kind, passes), (a, b)


def _mmx_bwd(kind, passes, res, g):
    a, b = res
    if kind == "nn":
        return _dg(g, b, "nt", passes), _dg(a, g, "tn", passes)
    if kind == "nt":
        return _dg(g, b, "nn", passes), _dg(g, a, "tn", passes)
    return _dg(b, g, "nt", passes), _dg(a, g, "nn", passes)


mmx.defvjp(_mmx_fwd, _mmx_bwd)


def _mask_dg(mask, x):
    xh, xl = _split(x)
    return jnp.dot(jnp.concatenate([mask, mask], axis=1), jnp.concatenate([xh, xl], axis=0), preferred_element_type=f32)


@functools.partial(jax.custom_vjp, nondiff_argnums=(2,))
def mask_mm(mask, x, bwd_passes):
    return _mask_dg(mask, x)


def _mask_fwd(mask, x, bwd_passes):
    return _mask_dg(mask, x), mask


def _mask_bwd(bwd_passes, mask, g):
    d = lambda y: lax.dot_general(mask, y, _DIMS["tn"], preferred_element_type=f32)
    if bwd_passes == 1:
        return None, d(g.astype(bf16))
    gh, gl = _split(g)
    return None, d(gh) + d(gl)


mask_mm.defvjp(_mask_fwd, _mask_bwd)


def bdot(a, b):
    return jnp.dot(a.astype(bf16), b.astype(bf16), preferred_element_type=f32)


def bdot_nt(a, b):
    return lax.dot_general(a.astype(bf16), b.astype(bf16), _DIMS["nt"], preferred_element_type=f32)


def bdot_tn(a, b):
    return lax.dot_general(a.astype(bf16), b.astype(bf16), _DIMS["tn"], preferred_element_type=f32)


def _iota2(n, m):
    return lax.broadcasted_iota(jnp.int32, (n, m), 0), lax.broadcasted_iota(jnp.int32, (n, m), 1)


sigmoid = jax.nn.sigmoid


def silu(x):
    return x * sigmoid(x)


def softplus(x):
    return jnp.maximum(x, 0.0) + jnp.log(1.0 + jnp.exp(-jnp.abs(x)))


def rmsnorm(x, w):
    return x * lax.rsqrt(jnp.mean(x * x, axis=-1, keepdims=True) + EPS) * w


def hg_masks():
    t, r = _iota2(C, C)
    mats = [r <= t, r > t]
    lvl = []
    for l in range(1, 7):
        sz = 1 << l
        half = sz >> 1
        seg_t = t >> l
        upper_t = (t & (sz - 1)) >= half
        mid_t = seg_t * sz + half - 1
        mats.append((upper_t & (r > mid_t) & (r <= t)) | ((~upper_t) & (r > t) & (r <= mid_t)))
        lvl.append(((seg_t == (r >> l)) & upper_t & ((r & (sz - 1)) < half)).astype(f32))
    stk = jnp.concatenate([m.astype(bf16) for m in mats], axis=0)
    return stk, lvl, (t == r).astype(f32)


def _head(a, h):
    return a[:, h * DK:(h + 1) * DK]


def _run(*gens):
    results = [None] * len(gens)
    live = list(range(len(gens)))
    while live:
        for i in list(live):
            try:
                next(gens[i])
            except StopIteration as e:
                results[i] = e.value
                live.remove(i)
    return results


def hg_chunk(St, ps, l0, l1):
    return _run(hg_stages(St, ps, l0, l1))[0]


def gd_chunk(S, cs, abs_, alog, dtb, t_saved=None):
    return _run(gd_stages(S, cs, abs_, alog, dtb, t_saved))[0]


def mix_chunk(St, ps, l0, l1, S, cs, abs_, alog, dtb):
    (sn_h, o_h), (sn_g, o_g, t_pack) = _run(hg_stages(St, ps, l0, l1), gd_stages(S, cs, abs_, alog, dtb))
    return sn_h, o_h, sn_g, o_g, t_pack


def hg_stages(St, ps, l0, l1):
    m = jnp.maximum(l0, l1)
    e0 = jnp.exp(l0 - m)
    e1 = jnp.exp(l1 - m)
    lb = e0 / (e0 + e1)
    stk, lvl, eye = hg_masks()
    msk = [eye] + lvl
    trow = lax.broadcasted_iota(jnp.int32, (C, 1), 0)
    upper = [(trow & ((1 << l) - 1)) >= (1 << (l - 1)) for l in range(1, 7)]
    qs, ks, vs, qG, kR, eGl = [], [], [], [], [], []
    for p in ps:
        pq, pf, v = p[:, 0:HD], p[:, HD:2 * HD], p[:, 2 * HD:3 * HD]
        q = silu(pq)
        f = lb + (1.0 - lb) * sigmoid(pf)
        k = 1.0 - f
        logf = jnp.log(f)
        Dm = mask_mm(stk, logf, 1)
        z = [jnp.where(up, q, k) * jnp.exp(Dm[(2 + i) * C:(3 + i) * C]) for i, up in enumerate(upper)]
        qs.append([q] + z)
        ks.append([k] + z)
        vs.append(v)
        qG.append(q * jnp.exp(Dm[0:C]))
        kR.append(k * jnp.exp(Dm[C:2 * C]))
        eGl.append(jnp.exp(jnp.sum(logf, axis=0, keepdims=True)))
    yield
    units = [(b, h) for b in range(len(ps)) for h in range(H)]
    parts = []
    for i in range(7):
        parts.append([msk[i] * mmx(_head(qs[b][i], h), _head(ks[b][i], h), "nt", P_HG["lvl"]) for b, h in units])
        yield
    A = [functools.reduce(lambda x, y: x + y, [parts[i][n] for i in range(7)]) for n in range(len(units))]
    qS = [mmx(_head(qG[b], h), St[n], "nt", P_HG["qs"]) for n, (b, h) in enumerate(units)]
    Sn = [St[n] * _head(eGl[b], h) + mmx(_head(vs[b], h), _head(kR[b], h), "tn", P_HG["su"])
          for n, (b, h) in enumerate(units)]
    yield
    outs = [mmx(A[n], _head(vs[b], h), "nn", P_HG["av"]) + qS[n] for n, (b, h) in enumerate(units)]
    return tuple(Sn), tuple(jnp.concatenate(outs[b * H:(b + 1) * H], axis=1) for b in range(len(ps)))


@jax.custom_vjp
def use_inverse(A, T):
    return T


def _use_inverse_fwd(A, T):
    return T, T


def _use_inverse_bwd(T, g):
    return -_dg(T, _dg(g, T, "nt", P_GD["inv"]), "tn", P_GD["inv"]), jnp.zeros_like(T)


use_inverse.defvjp(_use_inverse_fwd, _use_inverse_bwd)


def gd_stages(S, cs, abs_, alog, dtb, t_saved=None):
    t, r = _iota2(C, C)
    tri = (r <= t).astype(bf16)
    ups = (r > t).astype(bf16)
    lane = lax.broadcasted_iota(jnp.int32, (1, DK), 1)
    subl = lax.broadcasted_iota(jnp.int32, (8, 1), 0)
    eye = (t == r).astype(f32)
    strict = (r < t).astype(f32)
    bd = ((t >> 4) == (r >> 4)).astype(f32)
    qa, ka, va, b4, gam4, grev4, gam4T, glast4 = [], [], [], [], [], [], [], []
    for c, ab in zip(cs, abs_):
        qa.append(silu(c[:, 0:HD]))
        ka.append(silu(c[:, HD:2 * HD]))
        va.append(silu(c[:, 2 * HD:3 * HD]))
        g4 = -jnp.exp(alog) * softplus(ab + dtb)
        b4.append(sigmoid(ab))
        gam4.append(mask_mm(tri, g4, 2))
        grev4.append(mask_mm(ups, g4, 2))
        gam4T.append(gam4[-1].T)
        glast4.append(jnp.sum(g4, axis=0, keepdims=True))
    yield
    units = [(b, h) for b in range(len(cs)) for h in range(H)]
    nu = range(len(units))
    inv = lambda a, b: [mmx(a[n], b[n], "nn", P_GD["inv"]) for n in nu]
    v = [_head(va[b], h) for b, h in units]
    q = [_head(qa[b], h) for b, h in units]
    k = [_head(ka[b], h) for b, h in units]
    q = [x * lax.rsqrt(jnp.sum(x * x, -1, keepdims=True) + EPS) * (DK ** -0.5) for x in q]
    k = [x * lax.rsqrt(jnp.sum(x * x, -1, keepdims=True) + EPS) for x in k]
    oh = [(lane == h).astype(f32) for h in range(H)]
    gam_c = [jnp.sum(gam4[b] * oh[h], -1, keepdims=True) for b, h in units]
    grev_c = [jnp.sum(grev4[b] * oh[h], -1, keepdims=True) for b, h in units]
    beta = [jnp.sum(b4[b] * (lane == h + H).astype(f32), -1, keepdims=True) for b, h in units]
    glast = [jnp.sum(glast4[b] * oh[h], -1, keepdims=True) for b, h in units]
    gam_r = [jnp.sum(gam4T[b][0:8, :] * (subl == h).astype(f32), axis=0, keepdims=True) for b, h in units]
    dec = [jnp.exp(jnp.where(r <= t, gam_c[n] - gam_r[n], -1e30)) for n in nu]
    egam = [jnp.exp(gam_c[n]) for n in nu]
    kk = [mmx(k[n], k[n], "nt", P_GD["kk"]) for n in nu]
    qk = [mmx(q[n], k[n], "nt", P_GD["qk"]) * dec[n] for n in nu]
    yield
    A = [beta[n] * kk[n] * dec[n] * strict for n in nu]
    Dg = [A[n] * bd for n in nu]
    L = [A[n] - Dg[n] for n in nu]
    if t_saved is None:
        ImD = [eye - Dg[n] for n in nu]
        D2 = inv(Dg, Dg)
        yield
        P1 = inv(ImD, [eye + x for x in D2])
        D4 = inv(D2, D2)
        yield
        P2 = inv(P1, [eye + x for x in D4])
        D8 = inv(D4, D4)
        yield
        M = inv(P2, [eye + x for x in D8])
        yield
        Nn = inv(M, L)
        yield
        N2 = inv(Nn, Nn)
        yield
        T1 = inv([eye - x for x in Nn], [eye + x for x in N2])
        yield
        Tinv = inv(T1, M)
        yield
    else:
        Tinv = [use_inverse(A[n], t_saved[b][:, h * DK:h * DK + C]) for n, (b, h) in enumerate(units)]
    rhs = [jnp.concatenate([beta[n] * v[n], (beta[n] * egam[n]) * k[n]], axis=1) for n in nu]
    sol = [mmx(Tinv[n], rhs[n], "nn", P_GD["sol"]) for n in nu]
    yield
    qwS = [mmx(jnp.concatenate([q[n] * egam[n], sol[n][:, DK:2 * DK]], axis=0), S[n], "nn", P_GD["ws"]) for n in nu]
    yield
    u = [sol[n][:, 0:DK] - qwS[n][C:2 * C] for n in nu]
    outs = [qwS[n][0:C] + mmx(qk[n], u[n], "nn", P_GD["o"]) for n in nu]
    Sn = [jnp.exp(glast[n]) * S[n] + mmx(k[n] * jnp.exp(grev_c[n]), u[n], "tn", P_GD["su"]) for n in nu]
    zpad = jnp.zeros((C, DK - C), f32)
    t_pack = tuple(jnp.concatenate([x for n in range(b * H, (b + 1) * H) for x in (lax.stop_gradient(Tinv[n]), zpad)],
                                   axis=1) for b in range(len(cs)))
    return tuple(Sn), tuple(jnp.concatenate(outs[b * H:(b + 1) * H], axis=1) for b in range(len(cs))), t_pack


def _in_proj(hflat, norm_w, wbig, conv4):
    tm = 384
    W3 = 3 * HD

    def body(h_ref, nw_ref, w_ref, cw_ref, p_ref, ut_ref, cv_ref, prev):
        i = pl.program_id(0)

        @pl.when(i == 0)
        def _():
            prev[...] = jnp.zeros_like(prev)

        u = rmsnorm(h_ref[...], nw_ref[...])
        ut_ref[...] = u.T.astype(bf16)
        p = bdot_nt(u, w_ref[...])
        p_ref[...] = p
        x = p[:, COL_QKV:COL_QKV + W3]
        xx = jnp.concatenate([prev[...], x], axis=0)
        y = cw_ref[3] * x
        for s in (1, 2, 3):
            y = y + cw_ref[3 - s] * pltpu.roll(xx, s, 0)[8:]
        row = i * tm + lax.broadcasted_iota(jnp.int32, (tm, 1), 0)
        tok = jnp.where(row >= TP, row - TP, row)
        cv_ref[...] = jnp.where(tok >= 8, y, 0.0)
        prev[...] = x[tm - 8:tm]

    return pl.pallas_call(
        body, name="in_proj", grid=(N // tm,),
        in_specs=[pl.BlockSpec((tm, D), lambda i: (i, 0)), pl.BlockSpec((1, D), lambda i: (0, 0)),
                  pl.BlockSpec((PC, D), lambda i: (0, 0)), pl.BlockSpec((4, 1, W3), lambda i: (0, 0, 0))],
        out_specs=[pl.BlockSpec((tm, PC), lambda i: (i, 0)), pl.BlockSpec((D, tm), lambda i: (0, i)),
                   pl.BlockSpec((tm, W3), lambda i: (i, 0))],
        out_shape=[jax.ShapeDtypeStruct((N, PC), f32), jax.ShapeDtypeStruct((D, N), bf16),
                   jax.ShapeDtypeStruct((N, W3), f32)],
        scratch_shapes=[pltpu.VMEM((8, W3), f32)],
        compiler_params=_cp(("arbitrary",)),
    )(hflat, norm_w, wbig, conv4)


NU = NB * H
_REV = lambda c: NCH - 1 - c
_FWD = lambda c: c


def _tok_spec(w, ix, col=0):
    return pl.BlockSpec((NB, C, w), lambda c: (0, ix(c), col))


def _state_spec(ix):
    return pl.BlockSpec((NB, 1, H, DK, DK), lambda c: (0, ix(c), 0, 0, 0))


def _row_spec(w):
    return pl.BlockSpec((1, w), lambda c: (0, 0))


def _rows(ref):
    return tuple(ref[b] for b in range(NB))


def _hg_extra_specs(ix):
    return [_row_spec(HD), _row_spec(HD)]


def _gd_extra_specs(ix):
    return [_tok_spec(DK, ix, COL_AB // DK), _row_spec(DK), _row_spec(DK)]


def _mix_fwd(proj3, cv, l0, l1, alog, dtb):
    def body(p_ref, c_ref, ab_ref, l0_ref, l1_ref, al_ref, db_ref, oh_ref, sh_ref, og_ref, sg_ref, t_ref, sth, stg):
        @pl.when(pl.program_id(0) == 0)
        def _():
            sth[...] = jnp.zeros_like(sth)
            stg[...] = jnp.zeros_like(stg)

        Sh = tuple(sth[n] for n in range(NU))
        Sg = tuple(stg[n] for n in range(NU))
        for n in range(NU):
            sh_ref[n // H, 0, n % H] = Sh[n]
            sg_ref[n // H, 0, n % H] = Sg[n]
        snh, oh, sng, og, tp = mix_chunk(Sh, _rows(p_ref), l0_ref[...], l1_ref[...],
                                         Sg, _rows(c_ref), _rows(ab_ref), al_ref[...], db_ref[...])
        for n in range(NU):
            sth[n] = snh[n]
            stg[n] = sng[n]
        for b in range(NB):
            oh_ref[b] = oh[b]
            og_ref[b] = og[b]
            t_ref[b] = tp[b]

    tok = jax.ShapeDtypeStruct((NB, TP, HD), f32)
    st = jax.ShapeDtypeStruct((NB, NCH, H, DK, DK), f32)
    return pl.pallas_call(
        body, name="mix_fwd", grid=(NCH,),
        in_specs=[_tok_spec(3 * HD, _FWD), _tok_spec(3 * HD, _FWD), _tok_spec(DK, _FWD, COL_AB // DK),
                  _row_spec(HD), _row_spec(HD), _row_spec(DK), _row_spec(DK)],
        out_specs=[_tok_spec(HD, _FWD), _state_spec(_FWD), _tok_spec(HD, _FWD), _state_spec(_FWD), _tok_spec(HD, _FWD)],
        out_shape=[tok, st, tok, st, tok],
        scratch_shapes=[pltpu.VMEM((NU, DK, DK), f32), pltpu.VMEM((NU, DK, DK), f32)],
        compiler_params=_cp(("arbitrary",)),
    )(proj3, cv, proj3, l0, l1, alog, dtb)


def _hg_bwd(proj3, l0, l1, s_saved, do):
    def body(p_ref, l0_ref, l1_ref, s_ref, do_ref, dp_ref, dl0_ref, dl1_ref, dst):
        @pl.when(pl.program_id(0) == 0)
        def _():
            dst[...] = jnp.zeros_like(dst)
            dl0_ref[...] = jnp.zeros_like(dl0_ref)
            dl1_ref[...] = jnp.zeros_like(dl1_ref)

        S = tuple(s_ref[n // H, 0, n % H] for n in range(NU))
        _, vjp = jax.vjp(hg_chunk, S, _rows(p_ref), l0_ref[...], l1_ref[...])
        dS, dp, dl0, dl1 = vjp((tuple(dst[n] for n in range(NU)), _rows(do_ref)))
        for n in range(NU):
            dst[n] = dS[n]
        for b in range(NB):
            dp_ref[b] = dp[b].astype(bf16)
        dl0_ref[...] += jnp.broadcast_to(dl0, (8, HD))
        dl1_ref[...] += jnp.broadcast_to(dl1, (8, HD))

    acc = pl.BlockSpec((8, HD), lambda c: (0, 0))
    return pl.pallas_call(
        body, name="hg_bwd", grid=(NCH,),
        in_specs=[_tok_spec(3 * HD, _REV)] + _hg_extra_specs(_REV) + [_state_spec(_REV), _tok_spec(HD, _REV)],
        out_specs=[_tok_spec(3 * HD, _REV), acc, acc],
        out_shape=[jax.ShapeDtypeStruct((NB, TP, 3 * HD), bf16), jax.ShapeDtypeStruct((8, HD), f32),
                   jax.ShapeDtypeStruct((8, HD), f32)],
        scratch_shapes=[pltpu.VMEM((NU, DK, DK), f32)],
        compiler_params=_cp(("arbitrary",)),
    )(proj3, l0, l1, s_saved, do)


def _gd_bwd(cv, proj3, alog, dtb, s_saved, t_saved, do):
    def body(c_ref, ab_ref, al_ref, db_ref, s_ref, t_ref, do_ref, dc_ref, dab_ref, dal_ref, ddb_ref, dst):
        @pl.when(pl.program_id(0) == 0)
        def _():
            dst[...] = jnp.zeros_like(dst)
            dal_ref[...] = jnp.zeros_like(dal_ref)
            ddb_ref[...] = jnp.zeros_like(ddb_ref)

        S = tuple(s_ref[n // H, 0, n % H] for n in range(NU))
        t_rows = _rows(t_ref)
        fn = lambda *a: gd_chunk(*a, t_saved=t_rows)[0:2]
        _, vjp = jax.vjp(fn, S, _rows(c_ref), _rows(ab_ref), al_ref[...], db_ref[...])
        dS, dc, dab, dal, ddb = vjp((tuple(dst[n] for n in range(NU)), _rows(do_ref)))
        for n in range(NU):
            dst[n] = dS[n]
        for b in range(NB):
            dc_ref[b] = dc[b]
            dab_ref[b] = dab[b].astype(bf16)
        dal_ref[...] += jnp.broadcast_to(dal, (8, DK))
        ddb_ref[...] += jnp.broadcast_to(ddb, (8, DK))

    acc = pl.BlockSpec((8, DK), lambda c: (0, 0))
    return pl.pallas_call(
        body, name="gd_bwd", grid=(NCH,),
        in_specs=[_tok_spec(3 * HD, _REV)] + _gd_extra_specs(_REV)
        + [_state_spec(_REV), _tok_spec(HD, _REV), _tok_spec(HD, _REV)],
        out_specs=[_tok_spec(3 * HD, _REV), _tok_spec(DK, _REV), acc, acc],
        out_shape=[jax.ShapeDtypeStruct((NB, TP, 3 * HD), f32), jax.ShapeDtypeStruct((NB, TP, DK), bf16),
                   jax.ShapeDtypeStruct((8, DK), f32), jax.ShapeDtypeStruct((8, DK), f32)],
        scratch_shapes=[pltpu.VMEM((NU, DK, DK), f32)],
        compiler_params=_cp(("arbitrary",)),
    )(cv, proj3, alog, dtb, s_saved, t_saved, do)


def _conv_bwd(proj3, conv4, dy):
    def body(x_ref, w_ref, dy_ref, dx_ref, dw_ref):
        @pl.when(pl.program_id(1) == 0)
        def _():
            dw_ref[...] = jnp.zeros_like(dw_ref)

        x = x_ref[0]
        row = lax.broadcasted_iota(jnp.int32, (TP, 1), 0)
        g = jnp.where(row >= 8, dy_ref[0], 0.0)
        dx = w_ref[3] * g
        dw_ref[3] += jnp.broadcast_to(jnp.sum(x * g, axis=0, keepdims=True), (8, HD))
        for s in (1, 2, 3):
            dx = dx + w_ref[3 - s] * pltpu.roll(g, TP - s, 0)
            dw_ref[3 - s] += jnp.broadcast_to(jnp.sum(pltpu.roll(x, s, 0) * g, axis=0, keepdims=True), (8, HD))
        dx_ref[0] = dx.astype(bf16)

    return pl.pallas_call(
        body, name="conv_bwd", grid=(3, NB),
        in_specs=[pl.BlockSpec((1, TP, HD), lambda j, b: (b, 0, COL_QKV // HD + j)),
                  pl.BlockSpec((4, 1, HD), lambda j, b: (0, 0, j)), pl.BlockSpec((1, TP, HD), lambda j, b: (b, 0, j))],
        out_specs=[pl.BlockSpec((1, TP, HD), lambda j, b: (b, 0, j)), pl.BlockSpec((4, 8, HD), lambda j, b: (0, 0, j))],
        out_shape=[jax.ShapeDtypeStruct((NB, TP, 3 * HD), bf16), jax.ShapeDtypeStruct((4, 8, 3 * HD), f32)],
        compiler_params=_cp(("arbitrary", "arbitrary")),
    )(proj3, conv4, dy)


def _out_loss(o_hg, o_gd, proj, hgw, gdw, wout, hflat, fw, target):
    tm = 384

    def body(ohg_ref, ogd_ref, zhg_ref, zgd_ref, hgw_ref, gdw_ref, wo_ref, h_ref, fw_ref, *refs):
        tg_refs = refs[:tm // C]
        (loss_ref, dohg_ref, dogd_ref, dzhg_ref, dzgd_ref, dh_ref, dwo_ref, dhgw_ref, dgdw_ref, dfw_ref) = refs[tm // C:]
        i = pl.program_id(0)

        @pl.when(i == 0)
        def _():
            for r in (loss_ref, dwo_ref, dhgw_ref, dgdw_ref, dfw_ref):
                r[...] = jnp.zeros_like(r)

        row = i * tm + lax.broadcasted_iota(jnp.int32, (tm, 1), 0)
        tok = jnp.where(row >= TP, row - TP, row)
        valid = (tok >= PAD + N_META).astype(f32)
        hval = h_ref[...]
        tgt = jnp.concatenate([r[...] for r in tg_refs], axis=0)

        mixers = ((ohg_ref, zhg_ref, hgw_ref[...]), (ogd_ref, zgd_ref, gdw_ref[...]))
        saved, ys = [], []
        for o_ref, z_ref, w in mixers:
            for hh in range(H):
                sl = slice(hh * DK, (hh + 1) * DK)
                o, z = o_ref[:, sl], z_ref[:, sl]
                r = lax.rsqrt(jnp.mean(o * o, axis=-1, keepdims=True) + EPS)
                n = o * r
                sg = sigmoid(z)
                ws = w * (z * sg)
                saved.append((r, n, sg, z, ws, w))
                ys.append(n * ws)
        y = jnp.concatenate(ys, axis=-1)
        h2 = hval + bdot(y, wo_ref[...])
        r2 = lax.rsqrt(jnp.mean(h2 * h2, axis=-1, keepdims=True) + EPS)
        n2 = h2 * r2
        fwv = fw_ref[...]
        err = (n2 * fwv - tgt) * valid
        loss = (0.5 / D) * jnp.sum(err * err)
        dyf = err * (1.0 / D)
        dn2 = dyf * fwv
        dout = r2 * (dn2 - n2 * jnp.mean(dn2 * n2, axis=-1, keepdims=True))
        dh_ref[...] = dout
        dy = bdot_nt(dout, wo_ref[...])
        dwo_ref[...] += bdot_tn(y, dout)
        dws = []
        for mi, (do_ref, dz_ref) in enumerate(((dohg_ref, dzhg_ref), (dogd_ref, dzgd_ref))):
            dw = jnp.zeros((1, DK), f32)
            for hh in range(H):
                sl = slice(hh * DK, (hh + 1) * DK)
                r, n, sg, z, ws, w = saved[mi * H + hh]
                dyh = dy[:, mi * HD + hh * DK:mi * HD + (hh + 1) * DK]
                t = dyh * n
                dw = dw + jnp.sum(t * (z * sg), axis=0, keepdims=True)
                dz_ref[:, sl] = (t * w * (sg * (1.0 + z * (1.0 - sg)))).astype(bf16)
                dn = dyh * ws
                do_ref[:, sl] = r * (dn - n * jnp.mean(dn * n, axis=-1, keepdims=True))
            dws.append(dw)
        loss_ref[...] += jnp.broadcast_to(loss, (8, DK))
        dhgw_ref[...] += jnp.broadcast_to(dws[0], (8, DK))
        dgdw_ref[...] += jnp.broadcast_to(dws[1], (8, DK))
        dfw_ref[...] += jnp.broadcast_to(jnp.sum(dyf * n2, axis=0, keepdims=True), (8, D))

    row = lambda w: pl.BlockSpec((tm, w), lambda i: (i, 0))
    whole = lambda r, w: pl.BlockSpec((r, w), lambda i: (0, 0))
    col = lambda c0: pl.BlockSpec((tm, HD), lambda i: (i, c0 // HD))

    def tgt_spec(k):
        def index(i):
            chunk = (tm // C) * i + k
            b = chunk // NCH
            return jnp.maximum((SEQ // C) * b + chunk - NCH * b - 1, 0), 0
        return pl.BlockSpec((C, D), index)
    return pl.pallas_call(
        body, name="out_loss", grid=(N // tm,),
        in_specs=[row(HD), row(HD), col(COL_ZHG), col(COL_ZGD),
                  whole(1, DK), whole(1, DK), whole(D, D), row(D), whole(1, D)] + [tgt_spec(k) for k in range(tm // C)],
        out_specs=[whole(8, DK), row(HD), row(HD), row(HD), row(HD), row(D), whole(D, D),
                   whole(8, DK), whole(8, DK), whole(8, D)],
        out_shape=[jax.ShapeDtypeStruct((8, DK), f32)] + [jax.ShapeDtypeStruct((N, HD), f32)] * 2
        + [jax.ShapeDtypeStruct((N, HD), bf16)] * 2
        + [jax.ShapeDtypeStruct((N, D), f32), jax.ShapeDtypeStruct((D, D), f32),
           jax.ShapeDtypeStruct((8, DK), f32), jax.ShapeDtypeStruct((8, DK), f32), jax.ShapeDtypeStruct((8, D), f32)],
        compiler_params=_cp(("arbitrary",)),
    )(o_hg, o_gd, proj, proj, hgw, gdw, wout, hflat, fw, *[target] * (tm // C))


def _in_bwd(pieces, wbig, hflat, norm_w, dh_res, pbs):
    tm = 384
    nsteps = N // tm
    np_ = len(pieces)
    na = len(pbs)
    offs = [c0 for _, c0 in pieces]
    widths = [d.shape[1] for d, _ in pieces]

    def body(*refs):
        d_refs = refs[:np_]
        w_ref, h_ref, nw_ref, dhr_ref = refs[np_:np_ + 4]
        srcs = refs[np_ + 4:np_ + 4 + na]
        dh_ref, dnw_ref = refs[np_ + 4 + na:np_ + 6 + na]
        dsts = refs[np_ + 6 + na:np_ + 6 + 2 * na]
        sems = refs[np_ + 6 + 2 * na:]
        i = pl.program_id(0)

        def copies():
            if not na:
                return []
            x, y, c, chips = _place()
            return [pltpu.make_async_remote_copy(
                src_ref=srcs[a].at[2 * px + py], dst_ref=dsts[a].at[j], send_sem=sems[0].at[na * j + a],
                recv_sem=sems[1].at[na * j + a], device_id=(px, py, c), device_id_type=MESH)
                for j, (px, py) in enumerate(chips) for a in range(na)]

        @pl.when(i == 0)
        def _():
            dnw_ref[...] = jnp.zeros_like(dnw_ref)
            for cp in copies():
                cp.start()

        du = jnp.zeros((tm, D), f32)
        for d_ref, off, wd in zip(d_refs, offs, widths):
            du = du + bdot(d_ref[...], w_ref[off:off + wd, :])
        _, vjp = jax.vjp(rmsnorm, h_ref[...], nw_ref[...])
        dh, dnw = vjp(du)
        dh_ref[...] = dh + dhr_ref[...]
        dnw_ref[...] += jnp.broadcast_to(dnw, (8, D))

        @pl.when(i == nsteps - 1)
        def _():
            for cp in copies():
                cp.wait()

    row = lambda w: pl.BlockSpec((tm, w), lambda i: (i, 0))
    return pl.pallas_call(
        body, name="in_bwd", grid=(nsteps,),
        in_specs=[row(w) for w in widths]
        + [pl.BlockSpec((PC, D), lambda i: (0, 0)), row(D), pl.BlockSpec((1, D), lambda i: (0, 0)), row(D)] + [ANY] * na,
        out_specs=[row(D), pl.BlockSpec((8, D), lambda i: (0, 0))] + [ANY] * na,
        out_shape=[jax.ShapeDtypeStruct((N, D), f32), jax.ShapeDtypeStruct((8, D), f32)]
        + [jax.ShapeDtypeStruct((3,) + p.shape[1:], p.dtype) for p in pbs],
        scratch_shapes=[pltpu.SemaphoreType.DMA((3 * na,)), pltpu.SemaphoreType.DMA((3 * na,))] if na else [],
        compiler_params=_cp(("arbitrary",)),
    )(*[d for d, _ in pieces], wbig, hflat, norm_w, dh_res, *pbs)


def _w_grad(ut, pieces):
    tk = N // 3
    offs = [c0 for _, c0 in pieces]
    widths = [d.shape[1] for d, _ in pieces]

    def body(u_ref, *refs):
        d_refs, o_ref = refs[:-1], refs[-1]

        @pl.when(pl.program_id(0) == 0)
        def _():
            o_ref[...] = jnp.zeros_like(o_ref)

        u = u_ref[...]
        for d_ref, off, wd in zip(d_refs, offs, widths):
            o_ref[:, off:off + wd] += jnp.dot(u, d_ref[...], preferred_element_type=f32)

    return pl.pallas_call(
        body, name="w_grad", grid=(N // tk,),
        in_specs=[pl.BlockSpec((D, tk), lambda k: (0, k))] + [pl.BlockSpec((tk, w), lambda k: (k, 0)) for w in widths],
        out_specs=pl.BlockSpec((D, PC), lambda k: (0, 0), pipeline_mode=pl.Buffered(1)),
        out_shape=jax.ShapeDtypeStruct((D, PC), f32),
        compiler_params=_cp(("arbitrary",)),
    )(ut, *[d for d, _ in pieces])


def _adam_math(g, w, m, v):
    m2 = ADAM_B1 * m + (1.0 - ADAM_B1) * g
    v2 = ADAM_B2 * v + (1.0 - ADAM_B2) * (g * g)
    m_hat = m2 / (1.0 - ADAM_B1 ** ADAM_STEP)
    v_hat = v2 / (1.0 - ADAM_B2 ** ADAM_STEP)
    delta = -ADAM_LR * (m_hat / (jnp.sqrt(v_hat) + ADAM_EPS) + ADAM_WD * w)
    return delta, m2, v2


def _adamw(gs, w, m, v, name):
    R, Cc = w.shape
    tr = 256 if R % 256 == 0 else R
    ng = len(gs)

    def body(*refs):
        g = refs[0][...]
        for r in refs[1:ng]:
            g = g + r[...]
        w_ref, m_ref, v_ref, g_ref, d_ref, m2_ref, v2_ref = refs[ng:]
        delta, m2, v2 = _adam_math(g, w_ref[...], m_ref[...], v_ref[...])
        g_ref[...] = g
        d_ref[...] = delta
        m2_ref[...] = m2
        v2_ref[...] = v2

    spec = pl.BlockSpec((tr, Cc), lambda i: (i, 0))
    return pl.pallas_call(
        body, name=name, grid=(R // tr,),
        in_specs=[spec] * (ng + 3), out_specs=[spec] * 4,
        out_shape=[jax.ShapeDtypeStruct((R, Cc), f32)] * 4,
        compiler_params=_cp(("arbitrary",)),
    )(*gs, w, m, v)


def _adamw_rows(g, w, m, v, name):
    R, _, Cc = w.shape
    tr = R // 9

    def body(g_ref, w_ref, m_ref, v_ref, go_ref, d_ref, m2_ref, v2_ref):
        g = g_ref[...]
        delta, m2, v2 = _adam_math(g, w_ref[...], m_ref[...], v_ref[...])
        go_ref[...] = g
        d_ref[...] = delta
        m2_ref[...] = m2
        v2_ref[...] = v2

    spec = pl.BlockSpec((tr, 1, Cc), lambda i: (i, 0, 0))
    return pl.pallas_call(
        body, name=name, grid=(R // tr,),
        in_specs=[spec] * 4, out_specs=[spec] * 4,
        out_shape=[jax.ShapeDtypeStruct((R, 1, Cc), f32)] * 4,
        compiler_params=_cp(("arbitrary",)),
    )(g, w, m, v)


def _place():
    x, y, c = lax.axis_index("x"), lax.axis_index("y"), lax.axis_index("c")
    return x, y, c, [(1 - x, y), (x, 1 - y), (1 - x, 1 - y)]


def _gather_weights(cm, halved, whole):
    R, _, Cc = cm.shape
    shards = [jax.ShapeDtypeStruct((2, R, Cc // 2), bf16)] + list(halved) + list(whole)
    nh = 1 + len(halved)
    na = len(shards)

    def body(*refs):
        srcs, dsts = refs[:na], refs[na:2 * na]
        send_sems, recv_sems, loc_sems = refs[2 * na:2 * na + 3]
        stage = refs[2 * na + 3:3 * na + 3]
        raw = refs[3 * na + 3]
        x, y, c, chips = _place()
        me = 2 * x + y
        loads = [pltpu.make_async_copy(srcs[0], raw, loc_sems.at[0])]
        loads += [pltpu.make_async_copy(srcs[i], stage[i], loc_sems.at[i]) for i in range(1, na)]
        locs = [pltpu.make_async_copy(v, d.at[me], loc_sems.at[i]) for i, (v, d) in enumerate(zip(stage, dsts))]
        for cp in loads:
            cp.start()

        def ici(j, i, slot):
            px, py = chips[j]
            src = (stage[0] if i == 0 else srcs[i]).at[c] if i < nh else srcs[i]
            dst = dsts[i].at[slot, c] if i < nh else dsts[i].at[slot]
            return pltpu.make_async_remote_copy(
                src_ref=src, dst_ref=dst, send_sem=send_sems.at[na * j + i], recv_sem=recv_sems.at[na * j + i],
                device_id=(px, py, c), device_id_type=MESH)

        def d2d(j, i, half):
            px, py = chips[j]
            blk = dsts[i].at[2 * px + py, half]
            return pltpu.make_async_remote_copy(
                src_ref=blk, dst_ref=blk, send_sem=send_sems.at[3 * na + nh * j + i],
                recv_sem=recv_sems.at[3 * na + nh * j + i], device_id=(x, y, 1 - c), device_id_type=MESH)

        sends = [ici(j, i, me) for j in range(3) for i in range(1, na)]
        for cp in sends:
            cp.start()
        loads[0].wait()
        v = raw[:, 0, :]
        for hc in range(2):
            stage[0][hc] = v[:, hc * (Cc // 2):(hc + 1) * (Cc // 2)].astype(bf16)
        first = [ici(j, 0, me) for j in range(3)]
        for cp in first:
            cp.start()
        sends += first
        locs[0].start()
        for ld, st in zip(loads[1:], locs[1:]):
            ld.wait()
            st.start()
        for j, (px, py) in enumerate(chips):
            for i in range(na):
                ici(j, i, 2 * px + py).wait_recv()
                if i < nh:
                    fwd = d2d(j, i, c)
                    fwd.start()
                    sends.append(fwd)
        for j in range(3):
            for i in range(nh):
                d2d(j, i, 1 - c).wait_recv()
        for cp in sends:
            cp.wait_send()
        for cp in locs:
            cp.wait()

    nsem = 3 * na + 3 * nh
    return pl.pallas_call(
        body, name="gather_weights",
        in_specs=[ANY] * na, out_specs=[ANY] * na,
        out_shape=[jax.ShapeDtypeStruct((4,) + s.shape, s.dtype) for s in shards],
        scratch_shapes=[pltpu.SemaphoreType.DMA((nsem,)), pltpu.SemaphoreType.DMA((nsem,)),
                        pltpu.SemaphoreType.DMA((na,))] + [pltpu.VMEM(s.shape, s.dtype) for s in shards]
        + [pltpu.VMEM(cm.shape, cm.dtype)],
        compiler_params=pltpu.CompilerParams(has_side_effects=True, vmem_limit_bytes=VMEM_LIMIT),
    )(cm, *halved, *whole)


def _swap_halves(gs):
    na = len(gs)
    jobs = [(i, q) for i in range(na) for q in range(gs[i].shape[0])]

    def body(*refs):
        srcs, dsts = refs[:na], refs[na:2 * na]
        send_sems, recv_sems = refs[2 * na:]
        x, y, c, _ = _place()
        cps = [pltpu.make_async_remote_copy(
            src_ref=srcs[i].at[q, 1 - c], dst_ref=dsts[i].at[q], send_sem=send_sems.at[k],
            recv_sem=recv_sems.at[k], device_id=(x, y, 1 - c), device_id_type=MESH)
            for k, (i, q) in enumerate(jobs)]
        for cp in cps:
            cp.start()
        for cp in cps:
            cp.wait()

    return pl.pallas_call(
        body, name="swap_halves",
        in_specs=[ANY] * na, out_specs=[ANY] * na,
        out_shape=[jax.ShapeDtypeStruct(g.shape[0:1] + g.shape[2:], g.dtype) for g in gs],
        scratch_shapes=[pltpu.SemaphoreType.DMA((len(jobs),)), pltpu.SemaphoreType.DMA((len(jobs),))],
        compiler_params=pltpu.CompilerParams(has_side_effects=True),
    )(*gs)


def _add_halves(c_arr, g, s, name):
    Q, _, R, Cc = g.shape
    tr = min(R, 128)

    def body(c_ref, g_ref, s_ref, b_ref, f_ref):
        p = g_ref[0, 0] + s_ref[0]
        f_ref[0] = p
        b_ref[0] = p.astype(bf16)

    blk = pl.BlockSpec((1, tr, Cc), lambda q, i, cr: (q, i, 0))
    return pl.pallas_call(
        body, name=name,
        grid_spec=pltpu.PrefetchScalarGridSpec(
            num_scalar_prefetch=1, grid=(Q, R // tr),
            in_specs=[pl.BlockSpec((1, 1, tr, Cc), lambda q, i, cr: (q, cr[0], i, 0)), blk], out_specs=[blk, blk]),
        out_shape=[jax.ShapeDtypeStruct((Q, R, Cc), bf16), jax.ShapeDtypeStruct((Q, R, Cc), f32)],
        compiler_params=_cp(("arbitrary", "arbitrary")),
    )(c_arr, g, s)


_FLIPS = [(fx, fy, fc) for fx in (0, 1) for fy in (0, 1) for fc in (0, 1)][1:]


def _sum_blocks(own, r, name):
    R, Cc = own.shape
    tr = min(R, 256)

    def body(own_ref, r_ref, o_ref):
        acc = own_ref[...]
        for j in range(3):
            acc = acc + r_ref[j].astype(f32)
        o_ref[...] = acc

    return pl.pallas_call(
        body, name=name, grid=(R // tr,),
        in_specs=[pl.BlockSpec((tr, Cc), lambda i: (i, 0)), pl.BlockSpec((3, tr, Cc), lambda i: (0, i, 0))],
        out_specs=pl.BlockSpec((tr, Cc), lambda i: (i, 0)),
        out_shape=jax.ShapeDtypeStruct((R, Cc), f32),
        compiler_params=_cp(("arbitrary",)),
    )(own, r)


def _sum_packs(me8_arr, pack, rp):
    R = pack.shape[0]

    def body(me_ref, pk_ref, rp_ref, o_ref):
        me8 = me_ref[0]
        acc = None
        for d in range(8):
            rel = d ^ me8
            term = jnp.where(rel == 0, pk_ref[...], rp_ref[jnp.maximum(rel - 1, 0)])
            acc = term if acc is None else acc + term
        o_ref[...] = acc

    return pl.pallas_call(
        body, name="sum_packs",
        grid_spec=pltpu.PrefetchScalarGridSpec(
            num_scalar_prefetch=1, grid=(1,),
            in_specs=[pl.BlockSpec((R, 128), lambda i, mr: (0, 0)), pl.BlockSpec((7, R, 128), lambda i, mr: (0, 0, 0))],
            out_specs=pl.BlockSpec((R, 128), lambda i, mr: (0, 0))),
        out_shape=jax.ShapeDtypeStruct((R, 128), f32),
        compiler_params=_cp(("arbitrary",)),
    )(me8_arr, pack, rp)


def _swap_finished(fs, pack):
    na = len(fs)
    R = pack.shape[0]

    def body(*refs):
        srcs, pk = refs[:na], refs[na]
        dsts, rp = refs[na + 1:2 * na + 1], refs[2 * na + 1]
        send_sems, recv_sems = refs[2 * na + 2:]
        x, y, c, _ = _place()
        cps = [pltpu.make_async_remote_copy(
            src_ref=srcs[i], dst_ref=dsts[i], send_sem=send_sems.at[i], recv_sem=recv_sems.at[i],
            device_id=(x, y, 1 - c), device_id_type=MESH) for i in range(na)]
        cps += [pltpu.make_async_remote_copy(
            src_ref=pk, dst_ref=rp.at[k], send_sem=send_sems.at[na + k], recv_sem=recv_sems.at[na + k],
            device_id=(x ^ fx, y ^ fy, c ^ fc), device_id_type=MESH) for k, (fx, fy, fc) in enumerate(_FLIPS)]
        for cp in cps:
            cp.start()
        for cp in cps:
            cp.wait()

    return pl.pallas_call(
        body, name="swap_finished",
        in_specs=[ANY] * (na + 1), out_specs=[ANY] * (na + 1),
        out_shape=[jax.ShapeDtypeStruct(f.shape, f.dtype) for f in fs] + [jax.ShapeDtypeStruct((7, R, 128), f32)],
        scratch_shapes=[pltpu.SemaphoreType.DMA((na + 7,)), pltpu.SemaphoreType.DMA((na + 7,))],
        compiler_params=pltpu.CompilerParams(has_side_effects=True),
    )(*fs, pack)


def _adamw_halves(c_arr, mine, peer, w, m, v, name):
    _, R, Cc = w.shape
    tr = min(R, 256)

    def body(c_ref, mine_ref, peer_ref, w_ref, m_ref, v_ref, g_ref, d_ref, m2_ref, v2_ref):
        g = jnp.where(pl.program_id(0) == c_ref[0], mine_ref[...], peer_ref[...])
        delta, m2, v2 = _adam_math(g, w_ref[0], m_ref[0], v_ref[0])
        g_ref[0] = g
        d_ref[0] = delta
        m2_ref[0] = m2
        v2_ref[0] = v2

    half = pl.BlockSpec((tr, Cc), lambda hh, i, cr: (i, 0))
    full = pl.BlockSpec((1, tr, Cc), lambda hh, i, cr: (hh, i, 0))
    return pl.pallas_call(
        body, name=name,
        grid_spec=pltpu.PrefetchScalarGridSpec(
            num_scalar_prefetch=1, grid=(2, R // tr), in_specs=[half, half, full, full, full], out_specs=[full] * 4),
        out_shape=[jax.ShapeDtypeStruct((2, R, Cc), f32)] * 4,
        compiler_params=_cp(("arbitrary", "arbitrary")),
    )(c_arr, mine, peer, w, m, v)


def _rows8(a):
    flat = a.reshape(-1)
    n = flat.shape[0]
    rows = -(-n // 1024) * 8
    return jnp.pad(flat, (0, rows * 128 - n)).reshape(rows, 128)


def kernel(x, meta_tokens, norm_w, w_in, conv_w, hg_lb_logits, hg_norm_w, gdn_A_log, gdn_dt_bias, gdn_norm_w, w_out, final_norm_w, loss_target, m_meta_tokens, m_norm_w, m_w_in, m_conv_w, m_hg_lb_logits, m_hg_norm_w, m_gdn_A_log, m_gdn_dt_bias, m_gdn_norm_w, m_w_out, m_final_norm_w, v_meta_tokens, v_norm_w, v_w_in, v_conv_w, v_hg_lb_logits, v_hg_norm_w, v_gdn_A_log, v_gdn_dt_bias, v_gdn_norm_w, v_w_out, v_final_norm_w):
    me = 2 * lax.axis_index("x") + lax.axis_index("y")

    g_win, g_wout, g_conv, g_meta = _gather_weights(
        jnp.transpose(w_in, (2, 0, 1)), [w_out[0].astype(bf16).reshape(2, D // 8, D)], [conv_w[0], meta_tokens])
    wt_full = jnp.transpose(g_win, (0, 2, 1, 3)).reshape(IN_COLS, D)
    wbig = jnp.pad(wt_full, ((0, PC - IN_COLS), (0, 0)))
    wout_full = g_wout.reshape(D, D)
    conv4 = jnp.transpose(g_conv, (1, 0, 2)).reshape(4, 1, 3 * HD)
    meta_full = jnp.transpose(g_meta, (1, 0, 2)).reshape(N_META, D)

    c_arr = lax.axis_index("c").reshape(1).astype(jnp.int32)

    def chip_partials(gw, g_wout_part):
        g_in2 = gw.reshape(1, 2, D // 2, PC)
        g_out4 = g_wout_part.reshape(4, 2, D // 8, D)
        s_in, s_out = _swap_halves([g_in2, g_out4])
        pb_in, pf_in = _add_halves(c_arr, g_in2, s_in, "add_w_in")
        pb_out, pf_out = _add_halves(c_arr, g_out4, s_out, "add_w_out")
        pb_blocks = jnp.transpose(pb_in[0, :, 0:IN_COLS].reshape(D // 2, 4, SHARD_COLS), (1, 0, 2))
        own_in = lax.dynamic_slice(pf_in[0], (0, me * SHARD_COLS), (D // 2, SHARD_COLS))
        own_out = lax.dynamic_index_in_dim(pf_out, me, axis=0, keepdims=False)
        return [pb_blocks, pb_out], [own_in, own_out]

    (loss8, grad_x, d_meta, d_nw, d_conv, d_lb, d_hgw, d_alog, d_dtb, d_gdw, d_fw, pfs, rs) = _local_step(
        x, loss_target, wbig, wout_full, conv4, meta_full, norm_w, hg_lb_logits, hg_norm_w, gdn_A_log, gdn_dt_bias,
        gdn_norm_w, final_norm_w, chip_partials)

    pack = jnp.concatenate([
        loss8, d_nw[0].reshape(8, 128), d_lb.reshape(8, 128), d_hgw, _rows8(d_alog[0, :H]), _rows8(d_dtb[0, :H]),
        d_gdw, d_fw[0].reshape(8, 128), d_meta.reshape(128, 128), d_conv.reshape(48, 128)], axis=0)
    return _reduce_and_update(
        me, c_arr, grad_x, pfs, rs, pack, meta_tokens, norm_w, w_in, conv_w, hg_lb_logits, hg_norm_w, gdn_A_log,
        gdn_dt_bias, gdn_norm_w, w_out, final_norm_w, m_meta_tokens, m_norm_w, m_w_in, m_conv_w, m_hg_lb_logits,
        m_hg_norm_w, m_gdn_A_log, m_gdn_dt_bias, m_gdn_norm_w, m_w_out, m_final_norm_w, v_meta_tokens, v_norm_w, v_w_in,
        v_conv_w, v_hg_lb_logits, v_hg_norm_w, v_gdn_A_log, v_gdn_dt_bias, v_gdn_norm_w, v_w_out, v_final_norm_w)


def _local_step(x, loss_target, wbig, wout_full, conv4, meta_full, norm_w, hg_lb_logits, hg_norm_w, gdn_A_log, gdn_dt_bias,
                gdn_norm_w, final_norm_w, chip_partials):
    h3 = jnp.concatenate([jnp.zeros((NB, PAD, D), f32), jnp.broadcast_to(meta_full[None], (NB, N_META, D)), x], axis=1)
    hflat = h3.reshape(N, D)
    target = loss_target.reshape(NB * SEQ, D)
    l0, l1 = hg_lb_logits[0:1], hg_lb_logits[1:2]
    alog = jnp.pad(gdn_A_log, ((0, 0), (0, DK - H)))
    dtb = jnp.pad(gdn_dt_bias, ((0, 0), (0, DK - H)))
    fw = final_norm_w.reshape(1, D)

    proj, ut, cv2 = _in_proj(hflat, norm_w, wbig, conv4)
    proj3 = proj.reshape(NB, TP, PC)
    cv = cv2.reshape(NB, TP, 3 * HD)
    o_hg, s_hg, o_gd, s_gd, t_gd = _mix_fwd(proj3, cv, l0, l1, alog, dtb)
    (loss8, d_ohg, d_ogd, d_zhg, d_zgd, dh_res, g_wout_part, d_hgw, d_gdw, d_fw) = _out_loss(
        o_hg.reshape(N, HD), o_gd.reshape(N, HD), proj, hg_norm_w, gdn_norm_w, wout_full, hflat, fw, target)
    d_hg, d_l0, d_l1 = _hg_bwd(proj3, l0, l1, s_hg, d_ohg.reshape(NB, TP, HD))
    d_cv, d_ab, d_alog, d_dtb = _gd_bwd(cv, proj3, alog, dtb, s_gd, t_gd, d_ogd.reshape(NB, TP, HD))
    d_qkv, d_conv4 = _conv_bwd(proj3, conv4, d_cv)
    d_hg2, d_qkv2, d_ab2 = d_hg.reshape(N, 3 * HD), d_qkv.reshape(N, 3 * HD), d_ab.reshape(N, DK)
    pieces = [(d_hg2, COL_HG), (d_zhg, COL_ZHG), (d_qkv2, COL_QKV), (d_zgd, COL_ZGD), (d_ab2, COL_AB)]
    gw = _w_grad(ut, pieces)
    pbs, pfs = chip_partials(gw, g_wout_part) if chip_partials else ([], [gw, g_wout_part])
    dh, d_nw, *rs = _in_bwd(pieces, wbig, hflat, norm_w, dh_res, pbs)

    dh3 = dh.reshape(NB, TP, D)
    grad_x = dh3[:, PAD + N_META:, :]
    d_meta = jnp.sum(dh3[:, PAD:PAD + N_META, :], axis=0)
    d_conv = d_conv4[:, 0, :]
    d_lb = jnp.concatenate([d_l0[0:1], d_l1[0:1]], axis=0)
    return loss8, grad_x, d_meta, d_nw, d_conv, d_lb, d_hgw, d_alog, d_dtb, d_gdw, d_fw, pfs, rs


def _reduce_and_update(me, c_arr, grad_x, pfs, rs, pack, meta_tokens, norm_w, w_in, conv_w, hg_lb_logits, hg_norm_w,
                       gdn_A_log, gdn_dt_bias, gdn_norm_w, w_out, final_norm_w, m_meta_tokens, m_norm_w, m_w_in, m_conv_w,
                       m_hg_lb_logits, m_hg_norm_w, m_gdn_A_log, m_gdn_dt_bias, m_gdn_norm_w, m_w_out, m_final_norm_w,
                       v_meta_tokens, v_norm_w, v_w_in, v_conv_w, v_hg_lb_logits, v_hg_norm_w, v_gdn_A_log, v_gdn_dt_bias,
                       v_gdn_norm_w, v_w_out, v_final_norm_w):
    (own_in, own_out), (r_in, r_out) = pfs, rs
    f_in = _sum_blocks(own_in, r_in, "sum_w_in")
    f_out = _sum_blocks(own_out, r_out, "sum_w_out")
    o_in, o_out, r_pack = _swap_finished([f_in, f_out], pack)
    me8_arr = (2 * me + lax.axis_index("c")).reshape(1).astype(jnp.int32)
    small = _sum_packs(me8_arr, pack, r_pack)

    half_out = lambda a: a[0].reshape(2, D // 8, D)
    is0 = lax.axis_index("c") == 0
    g_in = jnp.concatenate([jnp.where(is0, f_in, o_in), jnp.where(is0, o_in, f_in)], axis=0)
    to_cm = lambda a: jnp.transpose(a, (2, 0, 1))
    gi, di, mi, vi = [jnp.transpose(a, (1, 2, 0))[0] for a in _adamw_rows(
        g_in.T.reshape(SHARD_COLS, 1, D), to_cm(w_in), to_cm(m_w_in), to_cm(v_w_in), "adamw_w_in")]
    go, do_, mo, vo = [a.reshape(D // 4, D) for a in _adamw_halves(
        c_arr, f_out, o_out, half_out(w_out), half_out(m_w_out), half_out(v_w_out), "adamw_w_out")]

    g_meta_full = small[64:192].reshape(N_META, D)
    g_meta_loc = lax.dynamic_slice(g_meta_full, (0, me * 256), (N_META, 256))
    gm, dm, mm_, vm = _adamw([g_meta_loc], meta_tokens, m_meta_tokens, v_meta_tokens, "adamw_meta")
    g_conv_full = small[192:240].reshape(4, 1536)
    g_conv_loc = lax.dynamic_slice(g_conv_full, (0, me * 384), (4, 384))
    gc, dc, mc, vc = _adamw([g_conv_loc], conv_w[0], m_conv_w[0], v_conv_w[0], "adamw_conv")

    reps = [(norm_w, m_norm_w, v_norm_w), (hg_lb_logits, m_hg_lb_logits, v_hg_lb_logits),
            (hg_norm_w, m_hg_norm_w, v_hg_norm_w), (gdn_A_log, m_gdn_A_log, v_gdn_A_log),
            (gdn_dt_bias, m_gdn_dt_bias, v_gdn_dt_bias), (gdn_norm_w, m_gdn_norm_w, v_gdn_norm_w),
            (final_norm_w, m_final_norm_w, v_final_norm_w)]
    wp = jnp.concatenate([_rows8(t[0]) for t in reps], axis=0)
    mp = jnp.concatenate([_rows8(t[1]) for t in reps], axis=0)
    vp = jnp.concatenate([_rows8(t[2]) for t in reps], axis=0)
    gr, dr, mr, vr = _adamw([small[8:64]], wp, mp, vp, "adamw_small")

    def unpack(p):
        outs = []
        for i, t in enumerate(reps):
            n = t[0].size
            outs.append(p[8 * i:8 * i + 8].reshape(-1)[:n].reshape(t[0].shape))
        return outs

    def leaves(meta_v, conv_v, in_v, out_v, rep_p):
        nw, lb, hgw, al, db, gdw, fwv = unpack(rep_p)
        return [meta_v, nw, in_v[None], conv_v[None], lb, hgw, al, db, gdw, out_v[None], fwv]

    loss = small[0, 0]
    return (loss, grad_x, *leaves(gm, gc, gi, go, gr), *leaves(dm, dc, di, do_, dr),
            *leaves(mm_, mc, mi, mo, mr), *leaves(vm, vc, vi, vo, vr))
```

```python
import functools

import jax
import jax.numpy as jnp
from jax import lax
from jax.experimental import pallas as pl
from jax.experimental.pallas import tpu as pltpu

f32 = jnp.float32
bf16 = jnp.bfloat16
MESH = pl.DeviceIdType.MESH
ANY = pl.BlockSpec(memory_space=pl.ANY)

D = 1024
NB = 2
N_META = 16
SEQ = 2048
PAD = 48
TP = PAD + N_META + SEQ
C = 64
NCH = TP // C
N = NB * TP
H = 4
DK = 128
HD = H * DK
PC = 4224
IN_COLS = 4104
SHARD_COLS = IN_COLS // 4
COL_HG, COL_ZHG, COL_QKV, COL_ZGD, COL_AB = 0, 3 * HD, 4 * HD, 7 * HD, 8 * HD
EPS = 1e-6
ADAM_LR, ADAM_B1, ADAM_B2, ADAM_EPS, ADAM_WD, ADAM_STEP = 0.001, 0.9, 0.999, 1e-08, 0.01, 10
VMEM_LIMIT = 56 * 1024 * 1024

P_HG = dict(lvl=1, av=1, qs=1, su=1)
P_GD = dict(kk=1, inv=1, sol=1, ws=1, qk=1, o=1, su=1)


def _cp(sem=None, **kw):
    return pltpu.CompilerParams(dimension_semantics=sem, vmem_limit_bytes=VMEM_LIMIT, **kw)


_DIMS = {"nn": (((1,), (0,)), ((), ())), "nt": (((1,), (1,)), ((), ())), "tn": (((0,), (0,)), ((), ()))}


def _split(x):
    hi = x.astype(bf16)
    return hi, (x - hi.astype(f32)).astype(bf16)


def _dg(a, b, kind, passes):
    d = lambda x, y: lax.dot_general(x, y, _DIMS[kind], preferred_element_type=f32)
    if passes == 1:
        return d(a.astype(bf16), b.astype(bf16))
    ah, al = _split(a)
    bh, bl = _split(b)
    return d(ah, bh) + d(ah, bl) + d(al, bh)


@functools.partial(jax.custom_vjp, nondiff_argnums=(2, 3))
def mmx(a, b, kind, passes):
    return _dg(a, b, kind, passes)


def _mmx_fwd(a, b, kind, passes):
    return _dg(a, b, kind, passes), (a, b)


def _mmx_bwd(kind, passes, res, g):
    a, b = res
    if kind == "nn":
        return _dg(g, b, "nt", passes), _dg(a, g, "tn", passes)
    if kind == "nt":
        return _dg(g, b, "nn", passes), _dg(g, a, "tn", passes)
    return _dg(b, g, "nt", passes), _dg(a, g, "nn", passes)


mmx.defvjp(_mmx_fwd, _mmx_bwd)


def _mask_dg(mask, x):
    xh, xl = _split(x)
    return jnp.dot(jnp.concatenate([mask, mask], axis=1), jnp.concatenate([xh, xl], axis=0), preferred_element_type=f32)


@functools.partial(jax.custom_vjp, nondiff_argnums=(2,))
def mask_mm(mask, x, bwd_passes):
    return _mask_dg(mask, x)


def _mask_fwd(mask, x, bwd_passes):
    return _mask_dg(mask, x), mask


def _mask_bwd(bwd_passes, mask, g):
    d = lambda y: lax.dot_general(mask, y, _DIMS["tn"], preferred_element_type=f32)
    if bwd_passes == 1:
        return None, d(g.astype(bf16))
    gh, gl = _split(g)
    return None, d(gh) + d(gl)


mask_mm.defvjp(_mask_fwd, _mask_bwd)


def bdot(a, b):
    return jnp.dot(a.astype(bf16), b.astype(bf16), preferred_element_type=f32)


def bdot_nt(a, b):
    return lax.dot_general(a.astype(bf16), b.astype(bf16), _DIMS["nt"], preferred_element_type=f32)


def bdot_tn(a, b):
    return lax.dot_general(a.astype(bf16), b.astype(bf16), _DIMS["tn"], preferred_element_type=f32)


def _iota2(n, m):
    return lax.broadcasted_iota(jnp.int32, (n, m), 0), lax.broadcasted_iota(jnp.int32, (n, m), 1)


sigmoid = jax.nn.sigmoid


def silu(x):
    return x * sigmoid(x)


def softplus(x):
    return jnp.maximum(x, 0.0) + jnp.log(1.0 + jnp.exp(-jnp.abs(x)))


def rmsnorm(x, w):
    return x * lax.rsqrt(jnp.mean(x * x, axis=-1, keepdims=True) + EPS) * w


def hg_masks():
    t, r = _iota2(C, C)
    mats = [r <= t, r > t]
    lvl = []
    for l in range(1, 7):
        sz = 1 << l
        half = sz >> 1
        seg_t = t >> l
        upper_t = (t & (sz - 1)) >= half
        mid_t = seg_t * sz + half - 1
        mats.append((upper_t & (r > mid_t) & (r <= t)) | ((~upper_t) & (r > t) & (r <= mid_t)))
        lvl.append(((seg_t == (r >> l)) & upper_t & ((r & (sz - 1)) < half)).astype(f32))
    stk = jnp.concatenate([m.astype(bf16) for m in mats], axis=0)
    return stk, lvl, (t == r).astype(f32)


def _head(a, h):
    return a[:, h * DK:(h + 1) * DK]


def _run(*gens):
    results = [None] * len(gens)
    live = list(range(len(gens)))
    while live:
        for i in list(live):
            try:
                next(gens[i])
            except StopIteration as e:
                results[i] = e.value
                live.remove(i)
    return results


def hg_chunk(St, ps, l0, l1):
    return _run(hg_stages(St, ps, l0, l1))[0]


def gd_chunk(S, cs, abs_, alog, dtb, t_saved=None):
    return _run(gd_stages(S, cs, abs_, alog, dtb, t_saved))[0]


def mix_chunk(St, ps, l0, l1, S, cs, abs_, alog, dtb):
    (sn_h, o_h), (sn_g, o_g, t_pack) = _run(hg_stages(St, ps, l0, l1), gd_stages(S, cs, abs_, alog, dtb))
    return sn_h, o_h, sn_g, o_g, t_pack


def hg_stages(St, ps, l0, l1):
    m = jnp.maximum(l0, l1)
    e0 = jnp.exp(l0 - m)
    e1 = jnp.exp(l1 - m)
    lb = e0 / (e0 + e1)
    stk, lvl, eye = hg_masks()
    msk = [eye] + lvl
    trow = lax.broadcasted_iota(jnp.int32, (C, 1), 0)
    upper = [(trow & ((1 << l) - 1)) >= (1 << (l - 1)) for l in range(1, 7)]
    qs, ks, vs, qG, kR, eGl = [], [], [], [], [], []
    for p in ps:
        pq, pf, v = p[:, 0:HD], p[:, HD:2 * HD], p[:, 2 * HD:3 * HD]
        q = silu(pq)
        f = lb + (1.0 - lb) * sigmoid(pf)
        k = 1.0 - f
        logf = jnp.log(f)
        Dm = mask_mm(stk, logf, 1)
        z = [jnp.where(up, q, k) * jnp.exp(Dm[(2 + i) * C:(3 + i) * C]) for i, up in enumerate(upper)]
        qs.append([q] + z)
        ks.append([k] + z)
        vs.append(v)
        qG.append(q * jnp.exp(Dm[0:C]))
        kR.append(k * jnp.exp(Dm[C:2 * C]))
        eGl.append(jnp.exp(jnp.sum(logf, axis=0, keepdims=True)))
    yield
    units = [(b, h) for b in range(len(ps)) for h in range(H)]
    parts = []
    for i in range(7):
        parts.append([msk[i] * mmx(_head(qs[b][i], h), _head(ks[b][i], h), "nt", P_HG["lvl"]) for b, h in units])
        yield
    A = [functools.reduce(lambda x, y: x + y, [parts[i][n] for i in range(7)]) for n in range(len(units))]
    qS = [mmx(_head(qG[b], h), St[n], "nt", P_HG["qs"]) for n, (b, h) in enumerate(units)]
    Sn = [St[n] * _head(eGl[b], h) + mmx(_head(vs[b], h), _head(kR[b], h), "tn", P_HG["su"])
          for n, (b, h) in enumerate(units)]
    yield
    outs = [mmx(A[n], _head(vs[b], h), "nn", P_HG["av"]) + qS[n] for n, (b, h) in enumerate(units)]
    return tuple(Sn), tuple(jnp.concatenate(outs[b * H:(b + 1) * H], axis=1) for b in range(len(ps)))


@jax.custom_vjp
def use_inverse(A, T):
    return T


def _use_inverse_fwd(A, T):
    return T, T


def _use_inverse_bwd(T, g):
    return -_dg(T, _dg(g, T, "nt", P_GD["inv"]), "tn", P_GD["inv"]), jnp.zeros_like(T)


use_inverse.defvjp(_use_inverse_fwd, _use_inverse_bwd)


def gd_stages(S, cs, abs_, alog, dtb, t_saved=None):
    t, r = _iota2(C, C)
    tri = (r <= t).astype(bf16)
    ups = (r > t).astype(bf16)
    lane = lax.broadcasted_iota(jnp.int32, (1, DK), 1)
    subl = lax.broadcasted_iota(jnp.int32, (8, 1), 0)
    eye = (t == r).astype(f32)
    strict = (r < t).astype(f32)
    bd = ((t >> 4) == (r >> 4)).astype(f32)
    qa, ka, va, b4, gam4, grev4, gam4T, glast4 = [], [], [], [], [], [], [], []
    for c, ab in zip(cs, abs_):
        qa.append(silu(c[:, 0:HD]))
        ka.append(silu(c[:, HD:2 * HD]))
        va.append(silu(c[:, 2 * HD:3 * HD]))
        g4 = -jnp.exp(alog) * softplus(ab + dtb)
        b4.append(sigmoid(ab))
        gam4.append(mask_mm(tri, g4, 2))
        grev4.append(mask_mm(ups, g4, 2))
        gam4T.append(gam4[-1].T)
        glast4.append(jnp.sum(g4, axis=0, keepdims=True))
    yield
    units = [(b, h) for b in range(len(cs)) for h in range(H)]
    nu = range(len(units))
    inv = lambda a, b: [mmx(a[n], b[n], "nn", P_GD["inv"]) for n in nu]
    v = [_head(va[b], h) for b, h in units]
    q = [_head(qa[b], h) for b, h in units]
    k = [_head(ka[b], h) for b, h in units]
    q = [x * lax.rsqrt(jnp.sum(x * x, -1, keepdims=True) + EPS) * (DK ** -0.5) for x in q]
    k = [x * lax.rsqrt(jnp.sum(x * x, -1, keepdims=True) + EPS) for x in k]
    oh = [(lane == h).astype(f32) for h in range(H)]
    gam_c = [jnp.sum(gam4[b] * oh[h], -1, keepdims=True) for b, h in units]
    grev_c = [jnp.sum(grev4[b] * oh[h], -1, keepdims=True) for b, h in units]
    beta = [jnp.sum(b4[b] * (lane == h + H).astype(f32), -1, keepdims=True) for b, h in units]
    glast = [jnp.sum(glast4[b] * oh[h], -1, keepdims=True) for b, h in units]
    gam_r = [jnp.sum(gam4T[b][0:8, :] * (subl == h).astype(f32), axis=0, keepdims=True) for b, h in units]
    dec = [jnp.exp(jnp.where(r <= t, gam_c[n] - gam_r[n], -1e30)) for n in nu]
    egam = [jnp.exp(gam_c[n]) for n in nu]
    kk = [mmx(k[n], k[n], "nt", P_GD["kk"]) for n in nu]
    qk = [mmx(q[n], k[n], "nt", P_GD["qk"]) * dec[n] for n in nu]
    yield
    A = [beta[n] * kk[n] * dec[n] * strict for n in nu]
    Dg = [A[n] * bd for n in nu]
    L = [A[n] - Dg[n] for n in nu]
    if t_saved is None:
        ImD = [eye - Dg[n] for n in nu]
        D2 = inv(Dg, Dg)
        yield
        P1 = inv(ImD, [eye + x for x in D2])
        D4 = inv(D2, D2)
        yield
        P2 = inv(P1, [eye + x for x in D4])
        D8 = inv(D4, D4)
        yield
        M = inv(P2, [eye + x for x in D8])
        yield
        Nn = inv(M, L)
        yield
        N2 = inv(Nn, Nn)
        yield
        T1 = inv([eye - x for x in Nn], [eye + x for x in N2])
        yield
        Tinv = inv(T1, M)
        yield
    else:
        Tinv = [use_inverse(A[n], t_saved[b][:, h * DK:h * DK + C]) for n, (b, h) in enumerate(units)]
    rhs = [jnp.concatenate([beta[n] * v[n], (beta[n] * egam[n]) * k[n]], axis=1) for n in nu]
    sol = [mmx(Tinv[n], rhs[n], "nn", P_GD["sol"]) for n in nu]
    yield
    qwS = [mmx(jnp.concatenate([q[n] * egam[n], sol[n][:, DK:2 * DK]], axis=0), S[n], "nn", P_GD["ws"]) for n in nu]
    yield
    u = [sol[n][:, 0:DK] - qwS[n][C:2 * C] for n in nu]
    outs = [qwS[n][0:C] + mmx(qk[n], u[n], "nn", P_GD["o"]) for n in nu]
    Sn = [jnp.exp(glast[n]) * S[n] + mmx(k[n] * jnp.exp(grev_c[n]), u[n], "tn", P_GD["su"]) for n in nu]
    zpad = jnp.zeros((C, DK - C), f32)
    t_pack = tuple(jnp.concatenate([x for n in range(b * H, (b + 1) * H) for x in (lax.stop_gradient(Tinv[n]), zpad)],
                                   axis=1) for b in range(len(cs)))
    return tuple(Sn), tuple(jnp.concatenate(outs[b * H:(b + 1) * H], axis=1) for b in range(len(cs))), t_pack


def _chunk_index(tile_chunks, k):
    def index(i):
        chunk = tile_chunks * i + k
        b = chunk // NCH
        return jnp.maximum((SEQ // C) * b + chunk - NCH * b - 1, 0), 0
    return index


def _in_proj(xflat, head, norm_w, wbig, conv4):
    tm = 384
    nck = tm // C
    W3 = 3 * HD

    def body(*refs):
        x_refs = refs[:nck]
        head_ref, nw_ref, w_ref, cw_ref, h_ref, p_ref, ut_ref, cv_ref, prev = refs[nck:]
        i = pl.program_id(0)

        @pl.when(i == 0)
        def _():
            prev[...] = jnp.zeros_like(prev)

        blocks = []
        for k in range(nck):
            chunk = nck * i + k
            blocks.append(jnp.where(chunk - NCH * (chunk // NCH) == 0, head_ref[...], x_refs[k][...]))
        hval = jnp.concatenate(blocks, axis=0)
        h_ref[...] = hval
        u = rmsnorm(hval, nw_ref[...])
        ut_ref[...] = u.T.astype(bf16)
        p = bdot_nt(u, w_ref[...])
        p_ref[...] = p
        x = p[:, COL_QKV:COL_QKV + W3]
        xx = jnp.concatenate([prev[...], x], axis=0)
        y = cw_ref[3] * x
        for s in (1, 2, 3):
            y = y + cw_ref[3 - s] * pltpu.roll(xx, s, 0)[8:]
        row = i * tm + lax.broadcasted_iota(jnp.int32, (tm, 1), 0)
        tok = jnp.where(row >= TP, row - TP, row)
        cv_ref[...] = jnp.where(tok >= 8, y, 0.0)
        prev[...] = x[tm - 8:tm]

    return pl.pallas_call(
        body, name="in_proj", grid=(N // tm,),
        in_specs=[pl.BlockSpec((C, D), _chunk_index(nck, k)) for k in range(nck)]
        + [pl.BlockSpec((C, D), lambda i: (0, 0)), pl.BlockSpec((1, D), lambda i: (0, 0)),
           pl.BlockSpec((PC, D), lambda i: (0, 0)), pl.BlockSpec((4, 1, W3), lambda i: (0, 0, 0))],
        out_specs=[pl.BlockSpec((tm, D), lambda i: (i, 0)), pl.BlockSpec((tm, PC), lambda i: (i, 0)),
                   pl.BlockSpec((D, tm), lambda i: (0, i)), pl.BlockSpec((tm, W3), lambda i: (i, 0))],
        out_shape=[jax.ShapeDtypeStruct((N, D), f32), jax.ShapeDtypeStruct((N, PC), f32),
                   jax.ShapeDtypeStruct((D, N), bf16), jax.ShapeDtypeStruct((N, W3), f32)],
        scratch_shapes=[pltpu.VMEM((8, W3), f32)],
        compiler_params=_cp(("arbitrary",)),
    )(*[xflat] * nck, head, norm_w, wbig, conv4)


NU = NB * H
_REV = lambda c: NCH - 1 - c
_FWD = lambda c: c


def _tok_spec(w, ix, col=0):
    return pl.BlockSpec((NB, C, w), lambda c: (0, ix(c), col))


def _state_spec(ix):
    return pl.BlockSpec((NB, 1, H, DK, DK), lambda c: (0, ix(c), 0, 0, 0))


def _row_spec(w):
    return pl.BlockSpec((1, w), lambda c: (0, 0))


def _rows(ref):
    return tuple(ref[b] for b in range(NB))


def _hg_extra_specs(ix):
    return [_row_spec(HD), _row_spec(HD)]


def _gd_extra_specs(ix):
    return [_tok_spec(DK, ix, COL_AB // DK), _row_spec(DK), _row_spec(DK)]


def _mix_fwd(proj3, cv, l0, l1, alog, dtb):
    def body(p_ref, c_ref, ab_ref, l0_ref, l1_ref, al_ref, db_ref, oh_ref, sh_ref, og_ref, sg_ref, t_ref, sth, stg):
        @pl.when(pl.program_id(0) == 0)
        def _():
            sth[...] = jnp.zeros_like(sth)
            stg[...] = jnp.zeros_like(stg)

        Sh = tuple(sth[n] for n in range(NU))
        Sg = tuple(stg[n] for n in range(NU))
        for n in range(NU):
            sh_ref[n // H, 0, n % H] = Sh[n]
            sg_ref[n // H, 0, n % H] = Sg[n]
        snh, oh, sng, og, tp = mix_chunk(Sh, _rows(p_ref), l0_ref[...], l1_ref[...],
                                         Sg, _rows(c_ref), _rows(ab_ref), al_ref[...], db_ref[...])
        for n in range(NU):
            sth[n] = snh[n]
            stg[n] = sng[n]
        for b in range(NB):
            oh_ref[b] = oh[b]
            og_ref[b] = og[b]
            t_ref[b] = tp[b]

    tok = jax.ShapeDtypeStruct((NB, TP, HD), f32)
    st = jax.ShapeDtypeStruct((NB, NCH, H, DK, DK), f32)
    return pl.pallas_call(
        body, name="mix_fwd", grid=(NCH,),
        in_specs=[_tok_spec(3 * HD, _FWD), _tok_spec(3 * HD, _FWD), _tok_spec(DK, _FWD, COL_AB // DK),
                  _row_spec(HD), _row_spec(HD), _row_spec(DK), _row_spec(DK)],
        out_specs=[_tok_spec(HD, _FWD), _state_spec(_FWD), _tok_spec(HD, _FWD), _state_spec(_FWD), _tok_spec(HD, _FWD)],
        out_shape=[tok, st, tok, st, tok],
        scratch_shapes=[pltpu.VMEM((NU, DK, DK), f32), pltpu.VMEM((NU, DK, DK), f32)],
        compiler_params=_cp(("arbitrary",)),
    )(proj3, cv, proj3, l0, l1, alog, dtb)


def _hg_bwd(proj3, l0, l1, s_saved, do):
    def body(p_ref, l0_ref, l1_ref, s_ref, do_ref, dp_ref, dl0_ref, dl1_ref, dst):
        @pl.when(pl.program_id(0) == 0)
        def _():
            dst[...] = jnp.zeros_like(dst)
            dl0_ref[...] = jnp.zeros_like(dl0_ref)
            dl1_ref[...] = jnp.zeros_like(dl1_ref)

        S = tuple(s_ref[n // H, 0, n % H] for n in range(NU))
        _, vjp = jax.vjp(hg_chunk, S, _rows(p_ref), l0_ref[...], l1_ref[...])
        dS, dp, dl0, dl1 = vjp((tuple(dst[n] for n in range(NU)), _rows(do_ref)))
        for n in range(NU):
            dst[n] = dS[n]
        for b in range(NB):
            dp_ref[b] = dp[b].astype(bf16)
        dl0_ref[...] += jnp.broadcast_to(dl0, (8, HD))
        dl1_ref[...] += jnp.broadcast_to(dl1, (8, HD))

    acc = pl.BlockSpec((8, HD), lambda c: (0, 0))
    return pl.pallas_call(
        body, name="hg_bwd", grid=(NCH,),
        in_specs=[_tok_spec(3 * HD, _REV)] + _hg_extra_specs(_REV) + [_state_spec(_REV), _tok_spec(HD, _REV)],
        out_specs=[_tok_spec(3 * HD, _REV), acc, acc],
        out_shape=[jax.ShapeDtypeStruct((NB, TP, 3 * HD), bf16), jax.ShapeDtypeStruct((8, HD), f32),
                   jax.ShapeDtypeStruct((8, HD), f32)],
        scratch_shapes=[pltpu.VMEM((NU, DK, DK), f32)],
        compiler_params=_cp(("arbitrary",)),
    )(proj3, l0, l1, s_saved, do)


def _gd_bwd(cv, proj3, alog, dtb, s_saved, t_saved, do):
    def body(c_ref, ab_ref, al_ref, db_ref, s_ref, t_ref, do_ref, dc_ref, dab_ref, dal_ref, ddb_ref, dst):
        @pl.when(pl.program_id(0) == 0)
        def _():
            dst[...] = jnp.zeros_like(dst)
            dal_ref[...] = jnp.zeros_like(dal_ref)
            ddb_ref[...] = jnp.zeros_like(ddb_ref)

        S = tuple(s_ref[n // H, 0, n % H] for n in range(NU))
        t_rows = _rows(t_ref)
        fn = lambda *a: gd_chunk(*a, t_saved=t_rows)[0:2]
        _, vjp = jax.vjp(fn, S, _rows(c_ref), _rows(ab_ref), al_ref[...], db_ref[...])
        dS, dc, dab, dal, ddb = vjp((tuple(dst[n] for n in range(NU)), _rows(do_ref)))
        for n in range(NU):
            dst[n] = dS[n]
        for b in range(NB):
            dc_ref[b] = dc[b]
            dab_ref[b] = dab[b].astype(bf16)
        dal_ref[...] += jnp.broadcast_to(dal, (8, DK))
        ddb_ref[...] += jnp.broadcast_to(ddb, (8, DK))

    acc = pl.BlockSpec((8, DK), lambda c: (0, 0))
    return pl.pallas_call(
        body, name="gd_bwd", grid=(NCH,),
        in_specs=[_tok_spec(3 * HD, _REV)] + _gd_extra_specs(_REV)
        + [_state_spec(_REV), _tok_spec(HD, _REV), _tok_spec(HD, _REV)],
        out_specs=[_tok_spec(3 * HD, _REV), _tok_spec(DK, _REV), acc, acc],
        out_shape=[jax.ShapeDtypeStruct((NB, TP, 3 * HD), f32), jax.ShapeDtypeStruct((NB, TP, DK), bf16),
                   jax.ShapeDtypeStruct((8, DK), f32), jax.ShapeDtypeStruct((8, DK), f32)],
        scratch_shapes=[pltpu.VMEM((NU, DK, DK), f32)],
        compiler_params=_cp(("arbitrary",)),
    )(cv, proj3, alog, dtb, s_saved, t_saved, do)


def _conv_bwd(proj3, conv4, dy):
    def body(x_ref, w_ref, dy_ref, dx_ref, dw_ref):
        @pl.when(pl.program_id(1) == 0)
        def _():
            dw_ref[...] = jnp.zeros_like(dw_ref)

        x = x_ref[0]
        row = lax.broadcasted_iota(jnp.int32, (TP, 1), 0)
        g = jnp.where(row >= 8, dy_ref[0], 0.0)
        dx = w_ref[3] * g
        dw_ref[3] += jnp.broadcast_to(jnp.sum(x * g, axis=0, keepdims=True), (8, HD))
        for s in (1, 2, 3):
            dx = dx + w_ref[3 - s] * pltpu.roll(g, TP - s, 0)
            dw_ref[3 - s] += jnp.broadcast_to(jnp.sum(pltpu.roll(x, s, 0) * g, axis=0, keepdims=True), (8, HD))
        dx_ref[0] = dx.astype(bf16)

    return pl.pallas_call(
        body, name="conv_bwd", grid=(3, NB),
        in_specs=[pl.BlockSpec((1, TP, HD), lambda j, b: (b, 0, COL_QKV // HD + j)),
                  pl.BlockSpec((4, 1, HD), lambda j, b: (0, 0, j)), pl.BlockSpec((1, TP, HD), lambda j, b: (b, 0, j))],
        out_specs=[pl.BlockSpec((1, TP, HD), lambda j, b: (b, 0, j)), pl.BlockSpec((4, 8, HD), lambda j, b: (0, 0, j))],
        out_shape=[jax.ShapeDtypeStruct((NB, TP, 3 * HD), bf16), jax.ShapeDtypeStruct((4, 8, 3 * HD), f32)],
        compiler_params=_cp(("arbitrary", "arbitrary")),
    )(proj3, conv4, dy)


def _out_loss(o_hg, o_gd, proj, hgw, gdw, wout, hflat, fw, target):
    tm = 384

    def body(ohg_ref, ogd_ref, zhg_ref, zgd_ref, hgw_ref, gdw_ref, wo_ref, h_ref, fw_ref, *refs):
        tg_refs = refs[:tm // C]
        (loss_ref, dohg_ref, dogd_ref, dzhg_ref, dzgd_ref, dh_ref, dwo_ref, dhgw_ref, dgdw_ref, dfw_ref) = refs[tm // C:]
        i = pl.program_id(0)

        @pl.when(i == 0)
        def _():
            for r in (loss_ref, dwo_ref, dhgw_ref, dgdw_ref, dfw_ref):
                r[...] = jnp.zeros_like(r)

        row = i * tm + lax.broadcasted_iota(jnp.int32, (tm, 1), 0)
        tok = jnp.where(row >= TP, row - TP, row)
        valid = (tok >= PAD + N_META).astype(f32)
        hval = h_ref[...]
        tgt = jnp.concatenate([r[...] for r in tg_refs], axis=0)

        mixers = ((ohg_ref, zhg_ref, hgw_ref[...]), (ogd_ref, zgd_ref, gdw_ref[...]))
        saved, ys = [], []
        for o_ref, z_ref, w in mixers:
            for hh in range(H):
                sl = slice(hh * DK, (hh + 1) * DK)
                o, z = o_ref[:, sl], z_ref[:, sl]
                r = lax.rsqrt(jnp.mean(o * o, axis=-1, keepdims=True) + EPS)
                n = o * r
                sg = sigmoid(z)
                ws = w * (z * sg)
                saved.append((r, n, sg, z, ws, w))
                ys.append(n * ws)
        y = jnp.concatenate(ys, axis=-1)
        h2 = hval + bdot(y, wo_ref[...])
        r2 = lax.rsqrt(jnp.mean(h2 * h2, axis=-1, keepdims=True) + EPS)
        n2 = h2 * r2
        fwv = fw_ref[...]
        err = (n2 * fwv - tgt) * valid
        loss = (0.5 / D) * jnp.sum(err * err)
        dyf = err * (1.0 / D)
        dn2 = dyf * fwv
        dout = r2 * (dn2 - n2 * jnp.mean(dn2 * n2, axis=-1, keepdims=True))
        dh_ref[...] = dout
        dy = bdot_nt(dout, wo_ref[...])
        dwo_ref[...] += bdot_tn(y, dout)
        dws = []
        for mi, (do_ref, dz_ref) in enumerate(((dohg_ref, dzhg_ref), (dogd_ref, dzgd_ref))):
            dw = jnp.zeros((1, DK), f32)
            for hh in range(H):
                sl = slice(hh * DK, (hh + 1) * DK)
                r, n, sg, z, ws, w = saved[mi * H + hh]
                dyh = dy[:, mi * HD + hh * DK:mi * HD + (hh + 1) * DK]
                t = dyh * n
                dw = dw + jnp.sum(t * (z * sg), axis=0, keepdims=True)
                dz_ref[:, sl] = (t * w * (sg * (1.0 + z * (1.0 - sg)))).astype(bf16)
                dn = dyh * ws
                do_ref[:, sl] = r * (dn - n * jnp.mean(dn * n, axis=-1, keepdims=True))
            dws.append(dw)
        loss_ref[...] += jnp.broadcast_to(loss, (8, DK))
        dhgw_ref[...] += jnp.broadcast_to(dws[0], (8, DK))
        dgdw_ref[...] += jnp.broadcast_to(dws[1], (8, DK))
        dfw_ref[...] += jnp.broadcast_to(jnp.sum(dyf * n2, axis=0, keepdims=True), (8, D))

    row = lambda w: pl.BlockSpec((tm, w), lambda i: (i, 0))
    whole = lambda r, w: pl.BlockSpec((r, w), lambda i: (0, 0))
    col = lambda c0: pl.BlockSpec((tm, HD), lambda i: (i, c0 // HD))

    tgt_spec = lambda k: pl.BlockSpec((C, D), _chunk_index(tm // C, k))
    return pl.pallas_call(
        body, name="out_loss", grid=(N // tm,),
        in_specs=[row(HD), row(HD), col(COL_ZHG), col(COL_ZGD),
                  whole(1, DK), whole(1, DK), whole(D, D), row(D), whole(1, D)] + [tgt_spec(k) for k in range(tm // C)],
        out_specs=[whole(8, DK), row(HD), row(HD), row(HD), row(HD), row(D), whole(D, D),
                   whole(8, DK), whole(8, DK), whole(8, D)],
        out_shape=[jax.ShapeDtypeStruct((8, DK), f32)] + [jax.ShapeDtypeStruct((N, HD), f32)] * 2
        + [jax.ShapeDtypeStruct((N, HD), bf16)] * 2
        + [jax.ShapeDtypeStruct((N, D), f32), jax.ShapeDtypeStruct((D, D), f32),
           jax.ShapeDtypeStruct((8, DK), f32), jax.ShapeDtypeStruct((8, DK), f32), jax.ShapeDtypeStruct((8, D), f32)],
        compiler_params=_cp(("arbitrary",)),
    )(o_hg, o_gd, proj, proj, hgw, gdw, wout, hflat, fw, *[target] * (tm // C))


def _in_bwd(pieces, wbig, hflat, norm_w, dh_res, pbs):
    tm = 384
    nsteps = N // tm
    np_ = len(pieces)
    na = len(pbs)
    offs = [c0 for _, c0 in pieces]
    widths = [d.shape[1] for d, _ in pieces]

    def body(*refs):
        d_refs = refs[:np_]
        w_ref, h_ref, nw_ref, dhr_ref = refs[np_:np_ + 4]
        srcs = refs[np_ + 4:np_ + 4 + na]
        dh_ref, dnw_ref = refs[np_ + 4 + na:np_ + 6 + na]
        dsts = refs[np_ + 6 + na:np_ + 6 + 2 * na]
        sems = refs[np_ + 6 + 2 * na:]
        i = pl.program_id(0)

        def copies():
            if not na:
                return []
            x, y, c, chips = _place()
            return [pltpu.make_async_remote_copy(
                src_ref=srcs[a].at[2 * px + py], dst_ref=dsts[a].at[j], send_sem=sems[0].at[na * j + a],
                recv_sem=sems[1].at[na * j + a], device_id=(px, py, c), device_id_type=MESH)
                for j, (px, py) in enumerate(chips) for a in range(na)]

        @pl.when(i == 0)
        def _():
            dnw_ref[...] = jnp.zeros_like(dnw_ref)
            for cp in copies():
                cp.start()

        du = jnp.zeros((tm, D), f32)
        for d_ref, off, wd in zip(d_refs, offs, widths):
            du = du + bdot(d_ref[...], w_ref[off:off + wd, :])
        _, vjp = jax.vjp(rmsnorm, h_ref[...], nw_ref[...])
        dh, dnw = vjp(du)
        dh_ref[...] = dh + dhr_ref[...]
        dnw_ref[...] += jnp.broadcast_to(dnw, (8, D))

        @pl.when(i == nsteps - 1)
        def _():
            for cp in copies():
                cp.wait()

    row = lambda w: pl.BlockSpec((tm, w), lambda i: (i, 0))
    return pl.pallas_call(
        body, name="in_bwd", grid=(nsteps,),
        in_specs=[row(w) for w in widths]
        + [pl.BlockSpec((PC, D), lambda i: (0, 0)), row(D), pl.BlockSpec((1, D), lambda i: (0, 0)), row(D)] + [ANY] * na,
        out_specs=[row(D), pl.BlockSpec((8, D), lambda i: (0, 0))] + [ANY] * na,
        out_shape=[jax.ShapeDtypeStruct((N, D), f32), jax.ShapeDtypeStruct((8, D), f32)]
        + [jax.ShapeDtypeStruct((3,) + p.shape[1:], p.dtype) for p in pbs],
        scratch_shapes=[pltpu.SemaphoreType.DMA((3 * na,)), pltpu.SemaphoreType.DMA((3 * na,))] if na else [],
        compiler_params=_cp(("arbitrary",)),
    )(*[d for d, _ in pieces], wbig, hflat, norm_w, dh_res, *pbs)


def _w_grad(ut, pieces):
    tk = N // 3
    offs = [c0 for _, c0 in pieces]
    widths = [d.shape[1] for d, _ in pieces]

    def body(u_ref, *refs):
        d_refs, o_ref = refs[:-1], refs[-1]

        @pl.when(pl.program_id(0) == 0)
        def _():
            o_ref[...] = jnp.zeros_like(o_ref)

        u = u_ref[...]
        for d_ref, off, wd in zip(d_refs, offs, widths):
            o_ref[:, off:off + wd] += jnp.dot(u, d_ref[...], preferred_element_type=f32)

    return pl.pallas_call(
        body, name="w_grad", grid=(N // tk,),
        in_specs=[pl.BlockSpec((D, tk), lambda k: (0, k))] + [pl.BlockSpec((tk, w), lambda k: (k, 0)) for w in widths],
        out_specs=pl.BlockSpec((D, PC), lambda k: (0, 0), pipeline_mode=pl.Buffered(1)),
        out_shape=jax.ShapeDtypeStruct((D, PC), f32),
        compiler_params=_cp(("arbitrary",)),
    )(ut, *[d for d, _ in pieces])


def _adam_math(g, w, m, v):
    m2 = ADAM_B1 * m + (1.0 - ADAM_B1) * g
    v2 = ADAM_B2 * v + (1.0 - ADAM_B2) * (g * g)
    m_hat = m2 / (1.0 - ADAM_B1 ** ADAM_STEP)
    v_hat = v2 / (1.0 - ADAM_B2 ** ADAM_STEP)
    delta = -ADAM_LR * (m_hat / (jnp.sqrt(v_hat) + ADAM_EPS) + ADAM_WD * w)
    return delta, m2, v2


def _adamw(gs, w, m, v, name):
    R, Cc = w.shape
    tr = 256 if R % 256 == 0 else R
    ng = len(gs)

    def body(*refs):
        g = refs[0][...]
        for r in refs[1:ng]:
            g = g + r[...]
        w_ref, m_ref, v_ref, g_ref, d_ref, m2_ref, v2_ref = refs[ng:]
        delta, m2, v2 = _adam_math(g, w_ref[...], m_ref[...], v_ref[...])
        g_ref[...] = g
        d_ref[...] = delta
        m2_ref[...] = m2
        v2_ref[...] = v2

    spec = pl.BlockSpec((tr, Cc), lambda i: (i, 0))
    return pl.pallas_call(
        body, name=name, grid=(R // tr,),
        in_specs=[spec] * (ng + 3), out_specs=[spec] * 4,
        out_shape=[jax.ShapeDtypeStruct((R, Cc), f32)] * 4,
        compiler_params=_cp(("arbitrary",)),
    )(*gs, w, m, v)


def _adamw_rows(g, w, m, v, name):
    R, _, Cc = w.shape
    tr = R // 9

    def body(g_ref, w_ref, m_ref, v_ref, go_ref, d_ref, m2_ref, v2_ref):
        g = g_ref[...]
        delta, m2, v2 = _adam_math(g, w_ref[...], m_ref[...], v_ref[...])
        go_ref[...] = g
        d_ref[...] = delta
        m2_ref[...] = m2
        v2_ref[...] = v2

    spec = pl.BlockSpec((tr, 1, Cc), lambda i: (i, 0, 0))
    return pl.pallas_call(
        body, name=name, grid=(R // tr,),
        in_specs=[spec] * 4, out_specs=[spec] * 4,
        out_shape=[jax.ShapeDtypeStruct((R, 1, Cc), f32)] * 4,
        compiler_params=_cp(("arbitrary",)),
    )(g, w, m, v)


def _place():
    x, y, c = lax.axis_index("x"), lax.axis_index("y"), lax.axis_index("c")
    return x, y, c, [(1 - x, y), (x, 1 - y), (1 - x, 1 - y)]


def _gather_weights(cm, halved, whole):
    R, _, Cc = cm.shape
    shards = [jax.ShapeDtypeStruct((2, R, Cc // 2), bf16)] + list(halved) + list(whole)
    nh = 1 + len(halved)
    na = len(shards)

    def body(*refs):
        srcs, dsts = refs[:na], refs[na:2 * na]
        send_sems, recv_sems, loc_sems = refs[2 * na:2 * na + 3]
        stage = refs[2 * na + 3:3 * na + 3]
        raw = refs[3 * na + 3]
        x, y, c, chips = _place()
        me = 2 * x + y
        loads = [pltpu.make_async_copy(srcs[0], raw, loc_sems.at[0])]
        loads += [pltpu.make_async_copy(srcs[i], stage[i], loc_sems.at[i]) for i in range(1, na)]
        locs = [pltpu.make_async_copy(v, d.at[me], loc_sems.at[i]) for i, (v, d) in enumerate(zip(stage, dsts))]
        for cp in loads:
            cp.start()

        def ici(j, i, slot):
            px, py = chips[j]
            src = (stage[0] if i == 0 else srcs[i]).at[c] if i < nh else srcs[i]
            dst = dsts[i].at[slot, c] if i < nh else dsts[i].at[slot]
            return pltpu.make_async_remote_copy(
                src_ref=src, dst_ref=dst, send_sem=send_sems.at[na * j + i], recv_sem=recv_sems.at[na * j + i],
                device_id=(px, py, c), device_id_type=MESH)

        def d2d(j, i, half):
            px, py = chips[j]
            blk = dsts[i].at[2 * px + py, half]
            return pltpu.make_async_remote_copy(
                src_ref=blk, dst_ref=blk, send_sem=send_sems.at[3 * na + nh * j + i],
                recv_sem=recv_sems.at[3 * na + nh * j + i], device_id=(x, y, 1 - c), device_id_type=MESH)

        sends = [ici(j, i, me) for j in range(3) for i in range(1, na)]
        for cp in sends:
            cp.start()
        loads[0].wait()
        v = raw[:, 0, :]
        for hc in range(2):
            stage[0][hc] = v[:, hc * (Cc // 2):(hc + 1) * (Cc // 2)].astype(bf16)
        first = [ici(j, 0, me) for j in range(3)]
        for cp in first:
            cp.start()
        sends += first
        locs[0].start()
        for ld, st in zip(loads[1:], locs[1:]):
            ld.wait()
            st.start()
        for j, (px, py) in enumerate(chips):
            for i in range(na):
                ici(j, i, 2 * px + py).wait_recv()
                if i < nh:
                    fwd = d2d(j, i, c)
                    fwd.start()
                    sends.append(fwd)
        for j in range(3):
            for i in range(nh):
                d2d(j, i, 1 - c).wait_recv()
        for cp in sends:
            cp.wait_send()
        for cp in locs:
            cp.wait()

    nsem = 3 * na + 3 * nh
    return pl.pallas_call(
        body, name="gather_weights",
        in_specs=[ANY] * na, out_specs=[ANY] * na,
        out_shape=[jax.ShapeDtypeStruct((4,) + s.shape, s.dtype) for s in shards],
        scratch_shapes=[pltpu.SemaphoreType.DMA((nsem,)), pltpu.SemaphoreType.DMA((nsem,)),
                        pltpu.SemaphoreType.DMA((na,))] + [pltpu.VMEM(s.shape, s.dtype) for s in shards]
        + [pltpu.VMEM(cm.shape, cm.dtype)],
        compiler_params=pltpu.CompilerParams(has_side_effects=True, vmem_limit_bytes=VMEM_LIMIT),
    )(cm, *halved, *whole)


def _swap_halves(gs):
    na = len(gs)
    jobs = [(i, q) for i in range(na) for q in range(gs[i].shape[0])]

    def body(*refs):
        srcs, dsts = refs[:na], refs[na:2 * na]
        send_sems, recv_sems = refs[2 * na:]
        x, y, c, _ = _place()
        cps = [pltpu.make_async_remote_copy(
            src_ref=srcs[i].at[q, 1 - c], dst_ref=dsts[i].at[q], send_sem=send_sems.at[k],
            recv_sem=recv_sems.at[k], device_id=(x, y, 1 - c), device_id_type=MESH)
            for k, (i, q) in enumerate(jobs)]
        for cp in cps:
            cp.start()
        for cp in cps:
            cp.wait()

    return pl.pallas_call(
        body, name="swap_halves",
        in_specs=[ANY] * na, out_specs=[ANY] * na,
        out_shape=[jax.ShapeDtypeStruct(g.shape[0:1] + g.shape[2:], g.dtype) for g in gs],
        scratch_shapes=[pltpu.SemaphoreType.DMA((len(jobs),)), pltpu.SemaphoreType.DMA((len(jobs),))],
        compiler_params=pltpu.CompilerParams(has_side_effects=True),
    )(*gs)


def _add_halves(c_arr, g, s, name):
    Q, _, R, Cc = g.shape
    tr = min(R, 128)

    def body(c_ref, g_ref, s_ref, b_ref, f_ref):
        p = g_ref[0, 0] + s_ref[0]
        f_ref[0] = p
        b_ref[0] = p.astype(bf16)

    blk = pl.BlockSpec((1, tr, Cc), lambda q, i, cr: (q, i, 0))
    return pl.pallas_call(
        body, name=name,
        grid_spec=pltpu.PrefetchScalarGridSpec(
            num_scalar_prefetch=1, grid=(Q, R // tr),
            in_specs=[pl.BlockSpec((1, 1, tr, Cc), lambda q, i, cr: (q, cr[0], i, 0)), blk], out_specs=[blk, blk]),
        out_shape=[jax.ShapeDtypeStruct((Q, R, Cc), bf16), jax.ShapeDtypeStruct((Q, R, Cc), f32)],
        compiler_params=_cp(("arbitrary", "arbitrary")),
    )(c_arr, g, s)


_FLIPS = [(fx, fy, fc) for fx in (0, 1) for fy in (0, 1) for fc in (0, 1)][1:]


def _sum_blocks(own, r, name):
    R, Cc = own.shape
    tr = min(R, 256)

    def body(own_ref, r_ref, o_ref):
        acc = own_ref[...]
        for j in range(3):
            acc = acc + r_ref[j].astype(f32)
        o_ref[...] = acc

    return pl.pallas_call(
        body, name=name, grid=(R // tr,),
        in_specs=[pl.BlockSpec((tr, Cc), lambda i: (i, 0)), pl.BlockSpec((3, tr, Cc), lambda i: (0, i, 0))],
        out_specs=pl.BlockSpec((tr, Cc), lambda i: (i, 0)),
        out_shape=jax.ShapeDtypeStruct((R, Cc), f32),
        compiler_params=_cp(("arbitrary",)),
    )(own, r)


def _sum_packs(me8_arr, pack, rp):
    R = pack.shape[0]

    def body(me_ref, pk_ref, rp_ref, o_ref):
        me8 = me_ref[0]
        acc = None
        for d in range(8):
            rel = d ^ me8
            term = jnp.where(rel == 0, pk_ref[...], rp_ref[jnp.maximum(rel - 1, 0)])
            acc = term if acc is None else acc + term
        o_ref[...] = acc

    return pl.pallas_call(
        body, name="sum_packs",
        grid_spec=pltpu.PrefetchScalarGridSpec(
            num_scalar_prefetch=1, grid=(1,),
            in_specs=[pl.BlockSpec((R, 128), lambda i, mr: (0, 0)), pl.BlockSpec((7, R, 128), lambda i, mr: (0, 0, 0))],
            out_specs=pl.BlockSpec((R, 128), lambda i, mr: (0, 0))),
        out_shape=jax.ShapeDtypeStruct((R, 128), f32),
        compiler_params=_cp(("arbitrary",)),
    )(me8_arr, pack, rp)


def _swap_finished(fs, pack):
    na = len(fs)
    R = pack.shape[0]

    def body(*refs):
        srcs, pk = refs[:na], refs[na]
        dsts, rp = refs[na + 1:2 * na + 1], refs[2 * na + 1]
        send_sems, recv_sems = refs[2 * na + 2:]
        x, y, c, _ = _place()
        cps = [pltpu.make_async_remote_copy(
            src_ref=srcs[i], dst_ref=dsts[i], send_sem=send_sems.at[i], recv_sem=recv_sems.at[i],
            device_id=(x, y, 1 - c), device_id_type=MESH) for i in range(na)]
        cps += [pltpu.make_async_remote_copy(
            src_ref=pk, dst_ref=rp.at[k], send_sem=send_sems.at[na + k], recv_sem=recv_sems.at[na + k],
            device_id=(x ^ fx, y ^ fy, c ^ fc), device_id_type=MESH) for k, (fx, fy, fc) in enumerate(_FLIPS)]
        for cp in cps:
            cp.start()
        for cp in cps:
            cp.wait()

    return pl.pallas_call(
        body, name="swap_finished",
        in_specs=[ANY] * (na + 1), out_specs=[ANY] * (na + 1),
        out_shape=[jax.ShapeDtypeStruct(f.shape, f.dtype) for f in fs] + [jax.ShapeDtypeStruct((7, R, 128), f32)],
        scratch_shapes=[pltpu.SemaphoreType.DMA((na + 7,)), pltpu.SemaphoreType.DMA((na + 7,))],
        compiler_params=pltpu.CompilerParams(has_side_effects=True),
    )(*fs, pack)


def _adamw_halves(c_arr, mine, peer, w, m, v, name):
    _, R, Cc = w.shape
    tr = min(R, 256)

    def body(c_ref, mine_ref, peer_ref, w_ref, m_ref, v_ref, g_ref, d_ref, m2_ref, v2_ref):
        g = jnp.where(pl.program_id(0) == c_ref[0], mine_ref[...], peer_ref[...])
        delta, m2, v2 = _adam_math(g, w_ref[0], m_ref[0], v_ref[0])
        g_ref[0] = g
        d_ref[0] = delta
        m2_ref[0] = m2
        v2_ref[0] = v2

    half = pl.BlockSpec((tr, Cc), lambda hh, i, cr: (i, 0))
    full = pl.BlockSpec((1, tr, Cc), lambda hh, i, cr: (hh, i, 0))
    return pl.pallas_call(
        body, name=name,
        grid_spec=pltpu.PrefetchScalarGridSpec(
            num_scalar_prefetch=1, grid=(2, R // tr), in_specs=[half, half, full, full, full], out_specs=[full] * 4),
        out_shape=[jax.ShapeDtypeStruct((2, R, Cc), f32)] * 4,
        compiler_params=_cp(("arbitrary", "arbitrary")),
    )(c_arr, mine, peer, w, m, v)


def _rows8(a):
    flat = a.reshape(-1)
    n = flat.shape[0]
    rows = -(-n // 1024) * 8
    return jnp.pad(flat, (0, rows * 128 - n)).reshape(rows, 128)


def kernel(x, meta_tokens, norm_w, w_in, conv_w, hg_lb_logits, hg_norm_w, gdn_A_log, gdn_dt_bias, gdn_norm_w, w_out, final_norm_w, loss_target, m_meta_tokens, m_norm_w, m_w_in, m_conv_w, m_hg_lb_logits, m_hg_norm_w, m_gdn_A_log, m_gdn_dt_bias, m_gdn_norm_w, m_w_out, m_final_norm_w, v_meta_tokens, v_norm_w, v_w_in, v_conv_w, v_hg_lb_logits, v_hg_norm_w, v_gdn_A_log, v_gdn_dt_bias, v_gdn_norm_w, v_w_out, v_final_norm_w):
    me = 2 * lax.axis_index("x") + lax.axis_index("y")

    g_win, g_wout, g_conv, g_meta = _gather_weights(
        jnp.transpose(w_in, (2, 0, 1)), [w_out[0].astype(bf16).reshape(2, D // 8, D)], [conv_w[0], meta_tokens])
    wt_full = jnp.transpose(g_win, (0, 2, 1, 3)).reshape(IN_COLS, D)
    wbig = jnp.pad(wt_full, ((0, PC - IN_COLS), (0, 0)))
    wout_full = g_wout.reshape(D, D)
    conv4 = jnp.transpose(g_conv, (1, 0, 2)).reshape(4, 1, 3 * HD)
    meta_full = jnp.transpose(g_meta, (1, 0, 2)).reshape(N_META, D)

    c_arr = lax.axis_index("c").reshape(1).astype(jnp.int32)

    def chip_partials(gw, g_wout_part):
        g_in2 = gw.reshape(1, 2, D // 2, PC)
        g_out4 = g_wout_part.reshape(4, 2, D // 8, D)
        s_in, s_out = _swap_halves([g_in2, g_out4])
        pb_in, pf_in = _add_halves(c_arr, g_in2, s_in, "add_w_in")
        pb_out, pf_out = _add_halves(c_arr, g_out4, s_out, "add_w_out")
        pb_blocks = jnp.transpose(pb_in[0, :, 0:IN_COLS].reshape(D // 2, 4, SHARD_COLS), (1, 0, 2))
        own_in = lax.dynamic_slice(pf_in[0], (0, me * SHARD_COLS), (D // 2, SHARD_COLS))
        own_out = lax.dynamic_index_in_dim(pf_out, me, axis=0, keepdims=False)
        return [pb_blocks, pb_out], [own_in, own_out]

    (loss8, grad_x, d_meta, d_nw, d_conv, d_lb, d_hgw, d_alog, d_dtb, d_gdw, d_fw, pfs, rs) = _local_step(
        x, loss_target, wbig, wout_full, conv4, meta_full, norm_w, hg_lb_logits, hg_norm_w, gdn_A_log, gdn_dt_bias,
        gdn_norm_w, final_norm_w, chip_partials)

    pack = jnp.concatenate([
        loss8, d_nw[0].reshape(8, 128), d_lb.reshape(8, 128), d_hgw, _rows8(d_alog[0, :H]), _rows8(d_dtb[0, :H]),
        d_gdw, d_fw[0].reshape(8, 128), d_meta.reshape(128, 128), d_conv.reshape(48, 128)], axis=0)
    return _reduce_and_update(
        me, c_arr, grad_x, pfs, rs, pack, meta_tokens, norm_w, w_in, conv_w, hg_lb_logits, hg_norm_w, gdn_A_log,
        gdn_dt_bias, gdn_norm_w, w_out, final_norm_w, m_meta_tokens, m_norm_w, m_w_in, m_conv_w, m_hg_lb_logits,
        m_hg_norm_w, m_gdn_A_log, m_gdn_dt_bias, m_gdn_norm_w, m_w_out, m_final_norm_w, v_meta_tokens, v_norm_w, v_w_in,
        v_conv_w, v_hg_lb_logits, v_hg_norm_w, v_gdn_A_log, v_gdn_dt_bias, v_gdn_norm_w, v_w_out, v_final_norm_w)


def _local_step(x, loss_target, wbig, wout_full, conv4, meta_full, norm_w, hg_lb_logits, hg_norm_w, gdn_A_log, gdn_dt_bias,
                gdn_norm_w, final_norm_w, chip_partials):
    head = jnp.concatenate([jnp.zeros((PAD, D), f32), meta_full], axis=0)
    target = loss_target.reshape(NB * SEQ, D)
    l0, l1 = hg_lb_logits[0:1], hg_lb_logits[1:2]
    alog = jnp.pad(gdn_A_log, ((0, 0), (0, DK - H)))
    dtb = jnp.pad(gdn_dt_bias, ((0, 0), (0, DK - H)))
    fw = final_norm_w.reshape(1, D)

    hflat, proj, ut, cv2 = _in_proj(x.reshape(NB * SEQ, D), head, norm_w, wbig, conv4)
    proj3 = proj.reshape(NB, TP, PC)
    cv = cv2.reshape(NB, TP, 3 * HD)
    o_hg, s_hg, o_gd, s_gd, t_gd = _mix_fwd(proj3, cv, l0, l1, alog, dtb)
    (loss8, d_ohg, d_ogd, d_zhg, d_zgd, dh_res, g_wout_part, d_hgw, d_gdw, d_fw) = _out_loss(
        o_hg.reshape(N, HD), o_gd.reshape(N, HD), proj, hg_norm_w, gdn_norm_w, wout_full, hflat, fw, target)
    d_hg, d_l0, d_l1 = _hg_bwd(proj3, l0, l1, s_hg, d_ohg.reshape(NB, TP, HD))
    d_cv, d_ab, d_alog, d_dtb = _gd_bwd(cv, proj3, alog, dtb, s_gd, t_gd, d_ogd.reshape(NB, TP, HD))
    d_qkv, d_conv4 = _conv_bwd(proj3, conv4, d_cv)
    d_hg2, d_qkv2, d_ab2 = d_hg.reshape(N, 3 * HD), d_qkv.reshape(N, 3 * HD), d_ab.reshape(N, DK)
    pieces = [(d_hg2, COL_HG), (d_zhg, COL_ZHG), (d_qkv2, COL_QKV), (d_zgd, COL_ZGD), (d_ab2, COL_AB)]
    gw = _w_grad(ut, pieces)
    pbs, pfs = chip_partials(gw, g_wout_part) if chip_partials else ([], [gw, g_wout_part])
    dh, d_nw, *rs = _in_bwd(pieces, wbig, hflat, norm_w, dh_res, pbs)

    dh3 = dh.reshape(NB, TP, D)
    grad_x = dh3[:, PAD + N_META:, :]
    d_meta = jnp.sum(dh3[:, PAD:PAD + N_META, :], axis=0)
    d_conv = d_conv4[:, 0, :]
    d_lb = jnp.concatenate([d_l0[0:1], d_l1[0:1]], axis=0)
    return loss8, grad_x, d_meta, d_nw, d_conv, d_lb, d_hgw, d_alog, d_dtb, d_gdw, d_fw, pfs, rs


def _reduce_and_update(me, c_arr, grad_x, pfs, rs, pack, meta_tokens, norm_w, w_in, conv_w, hg_lb_logits, hg_norm_w,
                       gdn_A_log, gdn_dt_bias, gdn_norm_w, w_out, final_norm_w, m_meta_tokens, m_norm_w, m_w_in, m_conv_w,
                       m_hg_lb_logits, m_hg_norm_w, m_gdn_A_log, m_gdn_dt_bias, m_gdn_norm_w, m_w_out, m_final_norm_w,
                       v_meta_tokens, v_norm_w, v_w_in, v_conv_w, v_hg_lb_logits, v_hg_norm_w, v_gdn_A_log, v_gdn_dt_bias,
                       v_gdn_norm_w, v_w_out, v_final_norm_w):
    (own_in, own_out), (r_in, r_out) = pfs, rs
    f_in = _sum_blocks(own_in, r_in, "sum_w_in")
    f_out = _sum_blocks(own_out, r_out, "sum_w_out")
    o_in, o_out, r_pack = _swap_finished([f_in, f_out], pack)
    me8_arr = (2 * me + lax.axis_index("c")).reshape(1).astype(jnp.int32)
    small = _sum_packs(me8_arr, pack, r_pack)

    half_out = lambda a: a[0].reshape(2, D // 8, D)
    is0 = lax.axis_index("c") == 0
    g_in = jnp.concatenate([jnp.where(is0, f_in, o_in), jnp.where(is0, o_in, f_in)], axis=0)
    to_cm = lambda a: jnp.transpose(a, (2, 0, 1))
    gi, di, mi, vi = [jnp.transpose(a, (1, 2, 0))[0] for a in _adamw_rows(
        g_in.T.reshape(SHARD_COLS, 1, D), to_cm(w_in), to_cm(m_w_in), to_cm(v_w_in), "adamw_w_in")]
    go, do_, mo, vo = [a.reshape(D // 4, D) for a in _adamw_halves(
        c_arr, f_out, o_out, half_out(w_out), half_out(m_w_out), half_out(v_w_out), "adamw_w_out")]

    g_meta_full = small[64:192].reshape(N_META, D)
    g_meta_loc = lax.dynamic_slice(g_meta_full, (0, me * 256), (N_META, 256))
    gm, dm, mm_, vm = _adamw([g_meta_loc], meta_tokens, m_meta_tokens, v_meta_tokens, "adamw_meta")
    g_conv_full = small[192:240].reshape(4, 1536)
    g_conv_loc = lax.dynamic_slice(g_conv_full, (0, me * 384), (4, 384))
    gc, dc, mc, vc = _adamw([g_conv_loc], conv_w[0], m_conv_w[0], v_conv_w[0], "adamw_conv")

    reps = [(norm_w, m_norm_w, v_norm_w), (hg_lb_logits, m_hg_lb_logits, v_hg_lb_logits),
            (hg_norm_w, m_hg_norm_w, v_hg_norm_w), (gdn_A_log, m_gdn_A_log, v_gdn_A_log),
            (gdn_dt_bias, m_gdn_dt_bias, v_gdn_dt_bias), (gdn_norm_w, m_gdn_norm_w, v_gdn_norm_w),
            (final_norm_w, m_final_norm_w, v_final_norm_w)]
    wp = jnp.concatenate([_rows8(t[0]) for t in reps], axis=0)
    mp = jnp.concatenate([_rows8(t[1]) for t in reps], axis=0)
    vp = jnp.concatenate([_rows8(t[2]) for t in reps], axis=0)
    gr, dr, mr, vr = _adamw([small[8:64]], wp, mp, vp, "adamw_small")

    def unpack(p):
        outs = []
        for i, t in enumerate(reps):
            n = t[0].size
            outs.append(p[8 * i:8 * i + 8].reshape(-1)[:n].reshape(t[0].shape))
        return outs

    def leaves(meta_v, conv_v, in_v, out_v, rep_p):
        nw, lb, hgw, al, db, gdw, fwv = unpack(rep_p)
        return [meta_v, nw, in_v[None], conv_v[None], lb, hgw, al, db, gdw, out_v[None], fwv]

    loss = small[0, 0]
    return (loss, grad_x, *leaves(gm, gc, gi, go, gr), *leaves(dm, dc, di, do_, dr),
            *leaves(mm_, mc, mi, mo, mr), *leaves(vm, vc, vi, vo, vr))
```

```python
import functools

import jax
import jax.numpy as jnp
from jax import lax
from jax.experimental import pallas as pl
from jax.experimental.pallas import tpu as pltpu

f32 = jnp.float32
bf16 = jnp.bfloat16
MESH = pl.DeviceIdType.MESH
ANY = pl.BlockSpec(memory_space=pl.ANY)

D = 1024
NB = 2
N_META = 16
SEQ = 2048
PAD = 48
TP = PAD + N_META + SEQ
C = 64
NCH = TP // C
N = NB * TP
H = 4
DK = 128
HD = H * DK
PC = 4224
IN_COLS = 4104
SHARD_COLS = IN_COLS // 4
COL_HG, COL_ZHG, COL_QKV, COL_ZGD, COL_AB = 0, 3 * HD, 4 * HD, 7 * HD, 8 * HD
EPS = 1e-6
ADAM_LR, ADAM_B1, ADAM_B2, ADAM_EPS, ADAM_WD, ADAM_STEP = 0.001, 0.9, 0.999, 1e-08, 0.01, 10
VMEM_LIMIT = 56 * 1024 * 1024

P_HG = dict(lvl=1, av=1, qs=1, su=1)
P_GD = dict(kk=1, inv=1, sol=1, ws=1, qk=1, o=1, su=1)


def _cp(sem=None, **kw):
    return pltpu.CompilerParams(dimension_semantics=sem, vmem_limit_bytes=VMEM_LIMIT, **kw)


_DIMS = {"nn": (((1,), (0,)), ((), ())), "nt": (((1,), (1,)), ((), ())), "tn": (((0,), (0,)), ((), ()))}


def _split(x):
    hi = x.astype(bf16)
    return hi, (x - hi.astype(f32)).astype(bf16)


def _dg(a, b, kind, passes):
    d = lambda x, y: lax.dot_general(x, y, _DIMS[kind], preferred_element_type=f32)
    if passes == 1:
        return d(a.astype(bf16), b.astype(bf16))
    ah, al = _split(a)
    bh, bl = _split(b)
    return d(ah, bh) + d(ah, bl) + d(al, bh)


@functools.partial(jax.custom_vjp, nondiff_argnums=(2, 3))
def mmx(a, b, kind, passes):
    return _dg(a, b, kind, passes)


def _mmx_fwd(a, b, kind, passes):
    return _dg(a, b, kind, passes), (a, b)


def _mmx_bwd(kind, passes, res, g):
    a, b = res
    if kind == "nn":
        return _dg(g, b, "nt", passes), _dg(a, g, "tn", passes)
    if kind == "nt":
        return _dg(g, b, "nn", passes), _dg(g, a, "tn", passes)
    return _dg(b, g, "nt", passes), _dg(a, g, "nn", passes)


mmx.defvjp(_mmx_fwd, _mmx_bwd)


def _mask_dg(mask, x):
    xh, xl = _split(x)
    return jnp.dot(jnp.concatenate([mask, mask], axis=1), jnp.concatenate([xh, xl], axis=0), preferred_element_type=f32)


@functools.partial(jax.custom_vjp, nondiff_argnums=(2,))
def mask_mm(mask, x, bwd_passes):
    return _mask_dg(mask, x)


def _mask_fwd(mask, x, bwd_passes):
    return _mask_dg(mask, x), mask


def _mask_bwd(bwd_passes, mask, g):
    d = lambda y: lax.dot_general(mask, y, _DIMS["tn"], preferred_element_type=f32)
    if bwd_passes == 1:
        return None, d(g.astype(bf16))
    gh, gl = _split(g)
    return None, d(gh) + d(gl)


mask_mm.defvjp(_mask_fwd, _mask_bwd)


def bdot(a, b):
    return jnp.dot(a.astype(bf16), b.astype(bf16), preferred_element_type=f32)


def bdot_nt(a, b):
    return lax.dot_general(a.astype(bf16), b.astype(bf16), _DIMS["nt"], preferred_element_type=f32)


def bdot_tn(a, b):
    return lax.dot_general(a.astype(bf16), b.astype(bf16), _DIMS["tn"], preferred_element_type=f32)


def _iota2(n, m):
    return lax.broadcasted_iota(jnp.int32, (n, m), 0), lax.broadcasted_iota(jnp.int32, (n, m), 1)


sigmoid = jax.nn.sigmoid


def silu(x):
    return x * sigmoid(x)


def softplus(x):
    return jnp.maximum(x, 0.0) + jnp.log(1.0 + jnp.exp(-jnp.abs(x)))


def rmsnorm(x, w):
    return x * lax.rsqrt(jnp.mean(x * x, axis=-1, keepdims=True) + EPS) * w


def hg_masks():
    t, r = _iota2(C, C)
    mats = [r <= t, r > t]
    lvl = []
    for l in range(1, 7):
        sz = 1 << l
        half = sz >> 1
        seg_t = t >> l
        upper_t = (t & (sz - 1)) >= half
        mid_t = seg_t * sz + half - 1
        mats.append((upper_t & (r > mid_t) & (r <= t)) | ((~upper_t) & (r > t) & (r <= mid_t)))
        lvl.append(((seg_t == (r >> l)) & upper_t & ((r & (sz - 1)) < half)).astype(f32))
    stk = jnp.concatenate([m.astype(bf16) for m in mats], axis=0)
    return stk, lvl, (t == r).astype(f32)


def _head(a, h):
    return a[:, h * DK:(h + 1) * DK]


def _run(*gens):
    results = [None] * len(gens)
    live = list(range(len(gens)))
    while live:
        for i in list(live):
            try:
                next(gens[i])
            except StopIteration as e:
                results[i] = e.value
                live.remove(i)
    return results


def hg_chunk(St, ps, l0, l1):
    return _run(hg_stages(St, ps, l0, l1))[0]


def gd_chunk(S, cs, abs_, alog, dtb, t_saved=None):
    return _run(gd_stages(S, cs, abs_, alog, dtb, t_saved))[0]


def mix_chunk(St, ps, l0, l1, S, cs, abs_, alog, dtb):
    (sn_h, o_h), (sn_g, o_g, t_pack) = _run(hg_stages(St, ps, l0, l1), gd_stages(S, cs, abs_, alog, dtb))
    return sn_h, o_h, sn_g, o_g, t_pack


def hg_stages(St, ps, l0, l1):
    m = jnp.maximum(l0, l1)
    e0 = jnp.exp(l0 - m)
    e1 = jnp.exp(l1 - m)
    lb = e0 / (e0 + e1)
    stk, lvl, eye = hg_masks()
    msk = [eye] + lvl
    trow = lax.broadcasted_iota(jnp.int32, (C, 1), 0)
    upper = [(trow & ((1 << l) - 1)) >= (1 << (l - 1)) for l in range(1, 7)]
    qs, ks, vs, qG, kR, eGl = [], [], [], [], [], []
    for p in ps:
        pq, pf, v = p[:, 0:HD], p[:, HD:2 * HD], p[:, 2 * HD:3 * HD]
        q = silu(pq)
        f = lb + (1.0 - lb) * sigmoid(pf)
        k = 1.0 - f
        logf = jnp.log(f)
        Dm = mask_mm(stk, logf, 1)
        z = [jnp.where(up, q, k) * jnp.exp(Dm[(2 + i) * C:(3 + i) * C]) for i, up in enumerate(upper)]
        qs.append([q] + z)
        ks.append([k] + z)
        vs.append(v)
        qG.append(q * jnp.exp(Dm[0:C]))
        kR.append(k * jnp.exp(Dm[C:2 * C]))
        eGl.append(jnp.exp(jnp.sum(logf, axis=0, keepdims=True)))
    yield
    units = [(b, h) for b in range(len(ps)) for h in range(H)]
    parts = []
    for i in range(7):
        parts.append([msk[i] * mmx(_head(qs[b][i], h), _head(ks[b][i], h), "nt", P_HG["lvl"]) for b, h in units])
        yield
    A = [functools.reduce(lambda x, y: x + y, [parts[i][n] for i in range(7)]) for n in range(len(units))]
    qS = [mmx(_head(qG[b], h), St[n], "nt", P_HG["qs"]) for n, (b, h) in enumerate(units)]
    Sn = [St[n] * _head(eGl[b], h) + mmx(_head(vs[b], h), _head(kR[b], h), "tn", P_HG["su"])
          for n, (b, h) in enumerate(units)]
    yield
    outs = [mmx(A[n], _head(vs[b], h), "nn", P_HG["av"]) + qS[n] for n, (b, h) in enumerate(units)]
    return tuple(Sn), tuple(jnp.concatenate(outs[b * H:(b + 1) * H], axis=1) for b in range(len(ps)))


@jax.custom_vjp
def use_inverse(A, T):
    return T


def _use_inverse_fwd(A, T):
    return T, T


def _use_inverse_bwd(T, g):
    return -_dg(T, _dg(g, T, "nt", P_GD["inv"]), "tn", P_GD["inv"]), jnp.zeros_like(T)


use_inverse.defvjp(_use_inverse_fwd, _use_inverse_bwd)


def gd_stages(S, cs, abs_, alog, dtb, t_saved=None):
    t, r = _iota2(C, C)
    tri = (r <= t).astype(bf16)
    ups = (r > t).astype(bf16)
    lane = lax.broadcasted_iota(jnp.int32, (1, DK), 1)
    subl = lax.broadcasted_iota(jnp.int32, (8, 1), 0)
    eye = (t == r).astype(f32)
    strict = (r < t).astype(f32)
    bd = ((t >> 4) == (r >> 4)).astype(f32)
    qa, ka, va, b4, gam4, grev4, gam4T, glast4 = [], [], [], [], [], [], [], []
    for c, ab in zip(cs, abs_):
        qa.append(silu(c[:, 0:HD]))
        ka.append(silu(c[:, HD:2 * HD]))
        va.append(silu(c[:, 2 * HD:3 * HD]))
        g4 = -jnp.exp(alog) * softplus(ab + dtb)
        b4.append(sigmoid(ab))
        gam4.append(mask_mm(tri, g4, 2))
        grev4.append(mask_mm(ups, g4, 2))
        gam4T.append(gam4[-1].T)
        glast4.append(jnp.sum(g4, axis=0, keepdims=True))
    yield
    units = [(b, h) for b in range(len(cs)) for h in range(H)]
    nu = range(len(units))
    inv = lambda a, b: [mmx(a[n], b[n], "nn", P_GD["inv"]) for n in nu]
    v = [_head(va[b], h) for b, h in units]
    q = [_head(qa[b], h) for b, h in units]
    k = [_head(ka[b], h) for b, h in units]
    q = [x * lax.rsqrt(jnp.sum(x * x, -1, keepdims=True) + EPS) * (DK ** -0.5) for x in q]
    k = [x * lax.rsqrt(jnp.sum(x * x, -1, keepdims=True) + EPS) for x in k]
    oh = [(lane == h).astype(f32) for h in range(H)]
    gam_c = [jnp.sum(gam4[b] * oh[h], -1, keepdims=True) for b, h in units]
    grev_c = [jnp.sum(grev4[b] * oh[h], -1, keepdims=True) for b, h in units]
    beta = [jnp.sum(b4[b] * (lane == h + H).astype(f32), -1, keepdims=True) for b, h in units]
    glast = [jnp.sum(glast4[b] * oh[h], -1, keepdims=True) for b, h in units]
    gam_r = [jnp.sum(gam4T[b][0:8, :] * (subl == h).astype(f32), axis=0, keepdims=True) for b, h in units]
    dec = [jnp.exp(jnp.where(r <= t, gam_c[n] - gam_r[n], -1e30)) for n in nu]
    egam = [jnp.exp(gam_c[n]) for n in nu]
    kk = [mmx(k[n], k[n], "nt", P_GD["kk"]) for n in nu]
    qk = [mmx(q[n], k[n], "nt", P_GD["qk"]) * dec[n] for n in nu]
    yield
    A = [beta[n] * kk[n] * dec[n] * strict for n in nu]
    Dg = [A[n] * bd for n in nu]
    L = [A[n] - Dg[n] for n in nu]
    if t_saved is None:
        ImD = [eye - Dg[n] for n in nu]
        D2 = inv(Dg, Dg)
        yield
        P1 = inv(ImD, [eye + x for x in D2])
        D4 = inv(D2, D2)
        yield
        P2 = inv(P1, [eye + x for x in D4])
        D8 = inv(D4, D4)
        yield
        M = inv(P2, [eye + x for x in D8])
        yield
        Nn = inv(M, L)
        yield
        N2 = inv(Nn, Nn)
        yield
        T1 = inv([eye - x for x in Nn], [eye + x for x in N2])
        yield
        Tinv = inv(T1, M)
        yield
    else:
        Tinv = [use_inverse(A[n], t_saved[b][:, h * DK:h * DK + C]) for n, (b, h) in enumerate(units)]
    rhs = [jnp.concatenate([beta[n] * v[n], (beta[n] * egam[n]) * k[n]], axis=1) for n in nu]
    sol = [mmx(Tinv[n], rhs[n], "nn", P_GD["sol"]) for n in nu]
    yield
    qwS = [mmx(jnp.concatenate([q[n] * egam[n], sol[n][:, DK:2 * DK]], axis=0), S[n], "nn", P_GD["ws"]) for n in nu]
    yield
    u = [sol[n][:, 0:DK] - qwS[n][C:2 * C] for n in nu]
    outs = [qwS[n][0:C] + mmx(qk[n], u[n], "nn", P_GD["o"]) for n in nu]
    Sn = [jnp.exp(glast[n]) * S[n] + mmx(k[n] * jnp.exp(grev_c[n]), u[n], "tn", P_GD["su"]) for n in nu]
    zpad = jnp.zeros((C, DK - C), f32)
    t_pack = tuple(jnp.concatenate([x for n in range(b * H, (b + 1) * H) for x in (lax.stop_gradient(Tinv[n]), zpad)],
                                   axis=1) for b in range(len(cs)))
    return tuple(Sn), tuple(jnp.concatenate(outs[b * H:(b + 1) * H], axis=1) for b in range(len(cs))), t_pack


def _chunk_index(tile_chunks, k):
    def index(i):
        chunk = tile_chunks * i + k
        b = chunk // NCH
        return jnp.maximum((SEQ // C) * b + chunk - NCH * b - 1, 0), 0
    return index


def _in_proj(xflat, head, norm_w, wbig, conv4):
    tm = 384
    nck = tm // C
    W3 = 3 * HD

    def body(*refs):
        x_refs = refs[:nck]
        head_ref, nw_ref, w_ref, cw_ref, h_ref, p_ref, ut_ref, cv_ref, prev = refs[nck:]
        i = pl.program_id(0)

        @pl.when(i == 0)
        def _():
            prev[...] = jnp.zeros_like(prev)

        blocks = []
        for k in range(nck):
            chunk = nck * i + k
            blocks.append(jnp.where(chunk - NCH * (chunk // NCH) == 0, head_ref[...], x_refs[k][...]))
        hval = jnp.concatenate(blocks, axis=0)
        h_ref[...] = hval
        u = rmsnorm(hval, nw_ref[...])
        ut_ref[...] = u.T.astype(bf16)
        p = bdot_nt(u, w_ref[...])
        p_ref[...] = p
        x = p[:, COL_QKV:COL_QKV + W3]
        xx = jnp.concatenate([prev[...], x], axis=0)
        y = cw_ref[3] * x
        for s in (1, 2, 3):
            y = y + cw_ref[3 - s] * pltpu.roll(xx, s, 0)[8:]
        row = i * tm + lax.broadcasted_iota(jnp.int32, (tm, 1), 0)
        tok = jnp.where(row >= TP, row - TP, row)
        cv_ref[...] = jnp.where(tok >= 8, y, 0.0)
        prev[...] = x[tm - 8:tm]

    return pl.pallas_call(
        body, name="in_proj", grid=(N // tm,),
        in_specs=[pl.BlockSpec((C, D), _chunk_index(nck, k)) for k in range(nck)]
        + [pl.BlockSpec((C, D), lambda i: (0, 0)), pl.BlockSpec((1, D), lambda i: (0, 0)),
           pl.BlockSpec((PC, D), lambda i: (0, 0)), pl.BlockSpec((4, 1, W3), lambda i: (0, 0, 0))],
        out_specs=[pl.BlockSpec((tm, D), lambda i: (i, 0)), pl.BlockSpec((tm, PC), lambda i: (i, 0)),
                   pl.BlockSpec((D, tm), lambda i: (0, i)), pl.BlockSpec((tm, W3), lambda i: (i, 0))],
        out_shape=[jax.ShapeDtypeStruct((N, D), f32), jax.ShapeDtypeStruct((N, PC), f32),
                   jax.ShapeDtypeStruct((D, N), bf16), jax.ShapeDtypeStruct((N, W3), f32)],
        scratch_shapes=[pltpu.VMEM((8, W3), f32)],
        compiler_params=_cp(("arbitrary",)),
    )(*[xflat] * nck, head, norm_w, wbig, conv4)


NU = NB * H
_REV = lambda c: NCH - 1 - c
_FWD = lambda c: c


def _tok_spec(w, ix, col=0):
    return pl.BlockSpec((NB, C, w), lambda c: (0, ix(c), col))


def _state_spec(ix):
    return pl.BlockSpec((NB, 1, H, DK, DK), lambda c: (0, ix(c), 0, 0, 0))


def _row_spec(w):
    return pl.BlockSpec((1, w), lambda c: (0, 0))


def _rows(ref):
    return tuple(ref[b] for b in range(NB))


def _hg_extra_specs(ix):
    return [_row_spec(HD), _row_spec(HD)]


def _gd_extra_specs(ix):
    return [_tok_spec(DK, ix, COL_AB // DK), _row_spec(DK), _row_spec(DK)]


def _mix_fwd(proj3, cv, l0, l1, alog, dtb):
    def body(p_ref, c_ref, ab_ref, l0_ref, l1_ref, al_ref, db_ref, oh_ref, sh_ref, og_ref, sg_ref, t_ref, sth, stg):
        @pl.when(pl.program_id(0) == 0)
        def _():
            sth[...] = jnp.zeros_like(sth)
            stg[...] = jnp.zeros_like(stg)

        Sh = tuple(sth[n] for n in range(NU))
        Sg = tuple(stg[n] for n in range(NU))
        for n in range(NU):
            sh_ref[n // H, 0, n % H] = Sh[n]
            sg_ref[n // H, 0, n % H] = Sg[n]
        snh, oh, sng, og, tp = mix_chunk(Sh, _rows(p_ref), l0_ref[...], l1_ref[...],
                                         Sg, _rows(c_ref), _rows(ab_ref), al_ref[...], db_ref[...])
        for n in range(NU):
            sth[n] = snh[n]
            stg[n] = sng[n]
        for b in range(NB):
            oh_ref[b] = oh[b]
            og_ref[b] = og[b]
            t_ref[b] = tp[b]

    tok = jax.ShapeDtypeStruct((NB, TP, HD), f32)
    st = jax.ShapeDtypeStruct((NB, NCH, H, DK, DK), f32)
    return pl.pallas_call(
        body, name="mix_fwd", grid=(NCH,),
        in_specs=[_tok_spec(3 * HD, _FWD), _tok_spec(3 * HD, _FWD), _tok_spec(DK, _FWD, COL_AB // DK),
                  _row_spec(HD), _row_spec(HD), _row_spec(DK), _row_spec(DK)],
        out_specs=[_tok_spec(HD, _FWD), _state_spec(_FWD), _tok_spec(HD, _FWD), _state_spec(_FWD), _tok_spec(HD, _FWD)],
        out_shape=[tok, st, tok, st, tok],
        scratch_shapes=[pltpu.VMEM((NU, DK, DK), f32), pltpu.VMEM((NU, DK, DK), f32)],
        compiler_params=_cp(("arbitrary",)),
    )(proj3, cv, proj3, l0, l1, alog, dtb)


def _hg_bwd(proj3, l0, l1, s_saved, do):
    def body(p_ref, l0_ref, l1_ref, s_ref, do_ref, dp_ref, dl0_ref, dl1_ref, dst):
        @pl.when(pl.program_id(0) == 0)
        def _():
            dst[...] = jnp.zeros_like(dst)
            dl0_ref[...] = jnp.zeros_like(dl0_ref)
            dl1_ref[...] = jnp.zeros_like(dl1_ref)

        S = tuple(s_ref[n // H, 0, n % H] for n in range(NU))
        _, vjp = jax.vjp(hg_chunk, S, _rows(p_ref), l0_ref[...], l1_ref[...])
        dS, dp, dl0, dl1 = vjp((tuple(dst[n] for n in range(NU)), _rows(do_ref)))
        for n in range(NU):
            dst[n] = dS[n]
        for b in range(NB):
            dp_ref[b] = dp[b].astype(bf16)
        dl0_ref[...] += jnp.broadcast_to(dl0, (8, HD))
        dl1_ref[...] += jnp.broadcast_to(dl1, (8, HD))

    acc = pl.BlockSpec((8, HD), lambda c: (0, 0))
    return pl.pallas_call(
        body, name="hg_bwd", grid=(NCH,),
        in_specs=[_tok_spec(3 * HD, _REV)] + _hg_extra_specs(_REV) + [_state_spec(_REV), _tok_spec(HD, _REV)],
        out_specs=[_tok_spec(3 * HD, _REV), acc, acc],
        out_shape=[jax.ShapeDtypeStruct((NB, TP, 3 * HD), bf16), jax.ShapeDtypeStruct((8, HD), f32),
                   jax.ShapeDtypeStruct((8, HD), f32)],
        scratch_shapes=[pltpu.VMEM((NU, DK, DK), f32)],
        compiler_params=_cp(("arbitrary",)),
    )(proj3, l0, l1, s_saved, do)


def _gd_bwd(cv, proj3, alog, dtb, s_saved, t_saved, do):
    def body(c_ref, ab_ref, al_ref, db_ref, s_ref, t_ref, do_ref, dc_ref, dab_ref, dal_ref, ddb_ref, dst):
        @pl.when(pl.program_id(0) == 0)
        def _():
            dst[...] = jnp.zeros_like(dst)
            dal_ref[...] = jnp.zeros_like(dal_ref)
            ddb_ref[...] = jnp.zeros_like(ddb_ref)

        S = tuple(s_ref[n // H, 0, n % H] for n in range(NU))
        t_rows = _rows(t_ref)
        fn = lambda *a: gd_chunk(*a, t_saved=t_rows)[0:2]
        _, vjp = jax.vjp(fn, S, _rows(c_ref), _rows(ab_ref), al_ref[...], db_ref[...])
        dS, dc, dab, dal, ddb = vjp((tuple(dst[n] for n in range(NU)), _rows(do_ref)))
        for n in range(NU):
            dst[n] = dS[n]
        for b in range(NB):
            dc_ref[b] = dc[b]
            dab_ref[b] = dab[b].astype(bf16)
        dal_ref[...] += jnp.broadcast_to(dal, (8, DK))
        ddb_ref[...] += jnp.broadcast_to(ddb, (8, DK))

    acc = pl.BlockSpec((8, DK), lambda c: (0, 0))
    return pl.pallas_call(
        body, name="gd_bwd", grid=(NCH,),
        in_specs=[_tok_spec(3 * HD, _REV)] + _gd_extra_specs(_REV)
        + [_state_spec(_REV), _tok_spec(HD, _REV), _tok_spec(HD, _REV)],
        out_specs=[_tok_spec(3 * HD, _REV), _tok_spec(DK, _REV), acc, acc],
        out_shape=[jax.ShapeDtypeStruct((NB, TP, 3 * HD), f32), jax.ShapeDtypeStruct((NB, TP, DK), bf16),
                   jax.ShapeDtypeStruct((8, DK), f32), jax.ShapeDtypeStruct((8, DK), f32)],
        scratch_shapes=[pltpu.VMEM((NU, DK, DK), f32)],
        compiler_params=_cp(("arbitrary",)),
    )(cv, proj3, alog, dtb, s_saved, t_saved, do)


def _conv_bwd(proj3, conv4, dy):
    def body(x_ref, w_ref, dy_ref, dx_ref, dw_ref):
        @pl.when(pl.program_id(1) == 0)
        def _():
            dw_ref[...] = jnp.zeros_like(dw_ref)

        x = x_ref[0]
        row = lax.broadcasted_iota(jnp.int32, (TP, 1), 0)
        g = jnp.where(row >= 8, dy_ref[0], 0.0)
        dx = w_ref[3] * g
        dw_ref[3] += jnp.broadcast_to(jnp.sum(x * g, axis=0, keepdims=True), (8, HD))
        for s in (1, 2, 3):
            dx = dx + w_ref[3 - s] * pltpu.roll(g, TP - s, 0)
            dw_ref[3 - s] += jnp.broadcast_to(jnp.sum(pltpu.roll(x, s, 0) * g, axis=0, keepdims=True), (8, HD))
        dx_ref[0] = dx.astype(bf16)

    return pl.pallas_call(
        body, name="conv_bwd", grid=(3, NB),
        in_specs=[pl.BlockSpec((1, TP, HD), lambda j, b: (b, 0, COL_QKV // HD + j)),
                  pl.BlockSpec((4, 1, HD), lambda j, b: (0, 0, j)), pl.BlockSpec((1, TP, HD), lambda j, b: (b, 0, j))],
        out_specs=[pl.BlockSpec((1, TP, HD), lambda j, b: (b, 0, j)), pl.BlockSpec((4, 8, HD), lambda j, b: (0, 0, j))],
        out_shape=[jax.ShapeDtypeStruct((NB, TP, 3 * HD), bf16), jax.ShapeDtypeStruct((4, 8, 3 * HD), f32)],
        compiler_params=_cp(("arbitrary", "arbitrary")),
    )(proj3, conv4, dy)


def _out_loss(o_hg, o_gd, proj, hgw, gdw, wout, hflat, fw, target):
    tm = 384

    def body(ohg_ref, ogd_ref, zhg_ref, zgd_ref, hgw_ref, gdw_ref, wo_ref, h_ref, fw_ref, *refs):
        tg_refs = refs[:tm // C]
        (loss_ref, dohg_ref, dogd_ref, dzhg_ref, dzgd_ref, dh_ref, dwo_ref, dhgw_ref, dgdw_ref, dfw_ref) = refs[tm // C:]
        i = pl.program_id(0)

        @pl.when(i == 0)
        def _():
            for r in (loss_ref, dwo_ref, dhgw_ref, dgdw_ref, dfw_ref):
                r[...] = jnp.zeros_like(r)

        row = i * tm + lax.broadcasted_iota(jnp.int32, (tm, 1), 0)
        tok = jnp.where(row >= TP, row - TP, row)
        valid = (tok >= PAD + N_META).astype(f32)
        hval = h_ref[...]
        tgt = jnp.concatenate([r[...] for r in tg_refs], axis=0)

        mixers = ((ohg_ref, zhg_ref, hgw_ref[...]), (ogd_ref, zgd_ref, gdw_ref[...]))
        saved, ys = [], []
        for o_ref, z_ref, w in mixers:
            for hh in range(H):
                sl = slice(hh * DK, (hh + 1) * DK)
                o, z = o_ref[:, sl], z_ref[:, sl]
                r = lax.rsqrt(jnp.mean(o * o, axis=-1, keepdims=True) + EPS)
                n = o * r
                sg = sigmoid(z)
                ws = w * (z * sg)
                saved.append((r, n, sg, z, ws, w))
                ys.append(n * ws)
        y = jnp.concatenate(ys, axis=-1)
        h2 = hval + bdot(y, wo_ref[...])
        r2 = lax.rsqrt(jnp.mean(h2 * h2, axis=-1, keepdims=True) + EPS)
        n2 = h2 * r2
        fwv = fw_ref[...]
        err = (n2 * fwv - tgt) * valid
        loss = (0.5 / D) * jnp.sum(err * err)
        dyf = err * (1.0 / D)
        dn2 = dyf * fwv
        dout = r2 * (dn2 - n2 * jnp.mean(dn2 * n2, axis=-1, keepdims=True))
        dh_ref[...] = dout
        dy = bdot_nt(dout, wo_ref[...])
        dwo_ref[...] += bdot_tn(y, dout)
        dws = []
        for mi, (do_ref, dz_ref) in enumerate(((dohg_ref, dzhg_ref), (dogd_ref, dzgd_ref))):
            dw = jnp.zeros((1, DK), f32)
            for hh in range(H):
                sl = slice(hh * DK, (hh + 1) * DK)
                r, n, sg, z, ws, w = saved[mi * H + hh]
                dyh = dy[:, mi * HD + hh * DK:mi * HD + (hh + 1) * DK]
                t = dyh * n
                dw = dw + jnp.sum(t * (z * sg), axis=0, keepdims=True)
                dz_ref[:, sl] = (t * w * (sg * (1.0 + z * (1.0 - sg)))).astype(bf16)
                dn = dyh * ws
                do_ref[:, sl] = r * (dn - n * jnp.mean(dn * n, axis=-1, keepdims=True))
            dws.append(dw)
        loss_ref[...] += jnp.broadcast_to(loss, (8, DK))
        dhgw_ref[...] += jnp.broadcast_to(dws[0], (8, DK))
        dgdw_ref[...] += jnp.broadcast_to(dws[1], (8, DK))
        dfw_ref[...] += jnp.broadcast_to(jnp.sum(dyf * n2, axis=0, keepdims=True), (8, D))

    row = lambda w: pl.BlockSpec((tm, w), lambda i: (i, 0))
    whole = lambda r, w: pl.BlockSpec((r, w), lambda i: (0, 0))
    col = lambda c0: pl.BlockSpec((tm, HD), lambda i: (i, c0 // HD))

    tgt_spec = lambda k: pl.BlockSpec((C, D), _chunk_index(tm // C, k))
    return pl.pallas_call(
        body, name="out_loss", grid=(N // tm,),
        in_specs=[row(HD), row(HD), col(COL_ZHG), col(COL_ZGD),
                  whole(1, DK), whole(1, DK), whole(D, D), row(D), whole(1, D)] + [tgt_spec(k) for k in range(tm // C)],
        out_specs=[whole(8, DK), row(HD), row(HD), row(HD), row(HD), row(D), whole(D, D),
                   whole(8, DK), whole(8, DK), whole(8, D)],
        out_shape=[jax.ShapeDtypeStruct((8, DK), f32)] + [jax.ShapeDtypeStruct((N, HD), f32)] * 2
        + [jax.ShapeDtypeStruct((N, HD), bf16)] * 2
        + [jax.ShapeDtypeStruct((N, D), f32), jax.ShapeDtypeStruct((D, D), f32),
           jax.ShapeDtypeStruct((8, DK), f32), jax.ShapeDtypeStruct((8, DK), f32), jax.ShapeDtypeStruct((8, D), f32)],
        compiler_params=_cp(("arbitrary",)),
    )(o_hg, o_gd, proj, proj, hgw, gdw, wout, hflat, fw, *[target] * (tm // C))


def _in_bwd(pieces, wbig, hflat, norm_w, dh_res, pbs):
    tm = 384
    nsteps = N // tm
    np_ = len(pieces)
    na = len(pbs)
    offs = [c0 for _, c0 in pieces]
    widths = [d.shape[1] for d, _ in pieces]

    def body(*refs):
        d_refs = refs[:np_]
        w_ref, h_ref, nw_ref, dhr_ref = refs[np_:np_ + 4]
        srcs = refs[np_ + 4:np_ + 4 + na]
        dh_ref, dnw_ref = refs[np_ + 4 + na:np_ + 6 + na]
        dsts = refs[np_ + 6 + na:np_ + 6 + 2 * na]
        sems = refs[np_ + 6 + 2 * na:]
        i = pl.program_id(0)

        def copies():
            if not na:
                return []
            x, y, c, chips = _place()
            return [pltpu.make_async_remote_copy(
                src_ref=srcs[a].at[2 * px + py], dst_ref=dsts[a].at[j], send_sem=sems[0].at[na * j + a],
                recv_sem=sems[1].at[na * j + a], device_id=(px, py, c), device_id_type=MESH)
                for j, (px, py) in enumerate(chips) for a in range(na)]

        @pl.when(i == 0)
        def _():
            dnw_ref[...] = jnp.zeros_like(dnw_ref)
            for cp in copies():
                cp.start()

        du = jnp.zeros((tm, D), f32)
        for d_ref, off, wd in zip(d_refs, offs, widths):
            du = du + bdot(d_ref[...], w_ref[off:off + wd, :])
        _, vjp = jax.vjp(rmsnorm, h_ref[...], nw_ref[...])
        dh, dnw = vjp(du)
        dh_ref[...] = dh + dhr_ref[...]
        dnw_ref[...] += jnp.broadcast_to(dnw, (8, D))

        @pl.when(i == nsteps - 1)
        def _():
            for cp in copies():
                cp.wait()

    row = lambda w: pl.BlockSpec((tm, w), lambda i: (i, 0))
    return pl.pallas_call(
        body, name="in_bwd", grid=(nsteps,),
        in_specs=[row(w) for w in widths]
        + [pl.BlockSpec((PC, D), lambda i: (0, 0)), row(D), pl.BlockSpec((1, D), lambda i: (0, 0)), row(D)] + [ANY] * na,
        out_specs=[row(D), pl.BlockSpec((8, D), lambda i: (0, 0))] + [ANY] * na,
        out_shape=[jax.ShapeDtypeStruct((N, D), f32), jax.ShapeDtypeStruct((8, D), f32)]
        + [jax.ShapeDtypeStruct((3,) + p.shape[1:], p.dtype) for p in pbs],
        scratch_shapes=[pltpu.SemaphoreType.DMA((3 * na,)), pltpu.SemaphoreType.DMA((3 * na,))] if na else [],
        compiler_params=_cp(("arbitrary",)),
    )(*[d for d, _ in pieces], wbig, hflat, norm_w, dh_res, *pbs)


def _w_grad(ut, pieces):
    tk = N // 3
    offs = [c0 for _, c0 in pieces]
    widths = [d.shape[1] for d, _ in pieces]

    def body(u_ref, *refs):
        d_refs, o_ref = refs[:-1], refs[-1]

        @pl.when(pl.program_id(0) == 0)
        def _():
            o_ref[...] = jnp.zeros_like(o_ref)

        u = u_ref[...]
        for d_ref, off, wd in zip(d_refs, offs, widths):
            o_ref[:, off:off + wd] += jnp.dot(u, d_ref[...], preferred_element_type=f32)

    return pl.pallas_call(
        body, name="w_grad", grid=(N // tk,),
        in_specs=[pl.BlockSpec((D, tk), lambda k: (0, k))] + [pl.BlockSpec((tk, w), lambda k: (k, 0)) for w in widths],
        out_specs=pl.BlockSpec((D, PC), lambda k: (0, 0), pipeline_mode=pl.Buffered(1)),
        out_shape=jax.ShapeDtypeStruct((D, PC), f32),
        compiler_params=_cp(("arbitrary",)),
    )(ut, *[d for d, _ in pieces])


def _adam_math(g, w, m, v):
    m2 = ADAM_B1 * m + (1.0 - ADAM_B1) * g
    v2 = ADAM_B2 * v + (1.0 - ADAM_B2) * (g * g)
    m_hat = m2 / (1.0 - ADAM_B1 ** ADAM_STEP)
    v_hat = v2 / (1.0 - ADAM_B2 ** ADAM_STEP)
    delta = -ADAM_LR * (m_hat / (jnp.sqrt(v_hat) + ADAM_EPS) + ADAM_WD * w)
    return delta, m2, v2


def _adamw(gs, w, m, v, name):
    R, Cc = w.shape
    tr = 256 if R % 256 == 0 else R
    ng = len(gs)

    def body(*refs):
        g = refs[0][...]
        for r in refs[1:ng]:
            g = g + r[...]
        w_ref, m_ref, v_ref, g_ref, d_ref, m2_ref, v2_ref = refs[ng:]
        delta, m2, v2 = _adam_math(g, w_ref[...], m_ref[...], v_ref[...])
        g_ref[...] = g
        d_ref[...] = delta
        m2_ref[...] = m2
        v2_ref[...] = v2

    spec = pl.BlockSpec((tr, Cc), lambda i: (i, 0))
    return pl.pallas_call(
        body, name=name, grid=(R // tr,),
        in_specs=[spec] * (ng + 3), out_specs=[spec] * 4,
        out_shape=[jax.ShapeDtypeStruct((R, Cc), f32)] * 4,
        compiler_params=_cp(("arbitrary",)),
    )(*gs, w, m, v)


def _adamw_rows(g, w, m, v, name):
    R, _, Cc = w.shape
    tr = R // 9

    def body(g_ref, w_ref, m_ref, v_ref, go_ref, d_ref, m2_ref, v2_ref):
        g = g_ref[...]
        delta, m2, v2 = _adam_math(g, w_ref[...], m_ref[...], v_ref[...])
        go_ref[...] = g
        d_ref[...] = delta
        m2_ref[...] = m2
        v2_ref[...] = v2

    spec = pl.BlockSpec((tr, 1, Cc), lambda i: (i, 0, 0))
    return pl.pallas_call(
        body, name=name, grid=(R // tr,),
        in_specs=[spec] * 4, out_specs=[spec] * 4,
        out_shape=[jax.ShapeDtypeStruct((R, 1, Cc), f32)] * 4,
        compiler_params=_cp(("arbitrary",)),
    )(g, w, m, v)


def _place():
    x, y, c = lax.axis_index("x"), lax.axis_index("y"), lax.axis_index("c")
    return x, y, c, [(1 - x, y), (x, 1 - y), (1 - x, 1 - y)]


def _gather_weights(cm, halved, whole):
    R, _, Cc = cm.shape
    hw = Cc // 2
    shards = [jax.ShapeDtypeStruct((R, Cc), bf16)] + list(halved) + list(whole)
    nh = 1 + len(halved)
    na = len(shards)

    def body(*refs):
        srcs, dsts = refs[:na], refs[na:2 * na]
        send_sems, recv_sems, loc_sems = refs[2 * na:2 * na + 3]
        stage = refs[2 * na + 3:3 * na + 3]
        raw = refs[3 * na + 3]
        x, y, c, chips = _place()
        me = 2 * x + y
        loads = [pltpu.make_async_copy(srcs[0], raw, loc_sems.at[0])]
        loads += [pltpu.make_async_copy(srcs[i], stage[i], loc_sems.at[i]) for i in range(1, na)]
        locs = [pltpu.make_async_copy(v, d.at[me], loc_sems.at[i]) for i, (v, d) in enumerate(zip(stage, dsts))]
        for cp in loads:
            cp.start()

        def half_of(ref, i, half):
            return ref.at[:, pl.ds(pl.multiple_of(half * hw, hw), hw)] if i == 0 else ref.at[half]

        def ici(j, i, slot):
            px, py = chips[j]
            src = half_of(stage[0] if i == 0 else srcs[i], i, c) if i < nh else srcs[i]
            dst = half_of(dsts[i].at[slot], i, c) if i < nh else dsts[i].at[slot]
            return pltpu.make_async_remote_copy(
                src_ref=src, dst_ref=dst, send_sem=send_sems.at[na * j + i], recv_sem=recv_sems.at[na * j + i],
                device_id=(px, py, c), device_id_type=MESH)

        def d2d(j, i, half):
            px, py = chips[j]
            blk = half_of(dsts[i].at[2 * px + py], i, half)
            return pltpu.make_async_remote_copy(
                src_ref=blk, dst_ref=blk, send_sem=send_sems.at[3 * na + nh * j + i],
                recv_sem=recv_sems.at[3 * na + nh * j + i], device_id=(x, y, 1 - c), device_id_type=MESH)

        sends = [ici(j, i, me) for j in range(3) for i in range(1, na)]
        for cp in sends:
            cp.start()
        loads[0].wait()
        stage[0][...] = raw[:, 0, :].astype(bf16)
        first = [ici(j, 0, me) for j in range(3)]
        for cp in first:
            cp.start()
        sends += first
        locs[0].start()
        for ld, st in zip(loads[1:], locs[1:]):
            ld.wait()
            st.start()
        for j, (px, py) in enumerate(chips):
            for i in range(na):
                ici(j, i, 2 * px + py).wait_recv()
                if i < nh:
                    fwd = d2d(j, i, c)
                    fwd.start()
                    sends.append(fwd)
        for j in range(3):
            for i in range(nh):
                d2d(j, i, 1 - c).wait_recv()
        for cp in sends:
            cp.wait_send()
        for cp in locs:
            cp.wait()

    nsem = 3 * na + 3 * nh
    return pl.pallas_call(
        body, name="gather_weights",
        in_specs=[ANY] * na, out_specs=[ANY] * na,
        out_shape=[jax.ShapeDtypeStruct((4,) + s.shape, s.dtype) for s in shards],
        scratch_shapes=[pltpu.SemaphoreType.DMA((nsem,)), pltpu.SemaphoreType.DMA((nsem,)),
                        pltpu.SemaphoreType.DMA((na,))] + [pltpu.VMEM(s.shape, s.dtype) for s in shards]
        + [pltpu.VMEM(cm.shape, cm.dtype)],
        compiler_params=pltpu.CompilerParams(has_side_effects=True, vmem_limit_bytes=VMEM_LIMIT),
    )(cm, *halved, *whole)


def _swap_halves(gs):
    na = len(gs)
    jobs = [(i, q) for i in range(na) for q in range(gs[i].shape[0])]

    def body(*refs):
        srcs, dsts = refs[:na], refs[na:2 * na]
        send_sems, recv_sems = refs[2 * na:]
        x, y, c, _ = _place()
        cps = [pltpu.make_async_remote_copy(
            src_ref=srcs[i].at[q, 1 - c], dst_ref=dsts[i].at[q], send_sem=send_sems.at[k],
            recv_sem=recv_sems.at[k], device_id=(x, y, 1 - c), device_id_type=MESH)
            for k, (i, q) in enumerate(jobs)]
        for cp in cps:
            cp.start()
        for cp in cps:
            cp.wait()

    return pl.pallas_call(
        body, name="swap_halves",
        in_specs=[ANY] * na, out_specs=[ANY] * na,
        out_shape=[jax.ShapeDtypeStruct(g.shape[0:1] + g.shape[2:], g.dtype) for g in gs],
        scratch_shapes=[pltpu.SemaphoreType.DMA((len(jobs),)), pltpu.SemaphoreType.DMA((len(jobs),))],
        compiler_params=pltpu.CompilerParams(has_side_effects=True),
    )(*gs)


def _add_halves(c_arr, g, s, name):
    Q, _, R, Cc = g.shape
    tr = min(R, 128)

    def body(c_ref, g_ref, s_ref, b_ref, f_ref):
        p = g_ref[0, 0] + s_ref[0]
        f_ref[0] = p
        b_ref[0] = p.astype(bf16)

    blk = pl.BlockSpec((1, tr, Cc), lambda q, i, cr: (q, i, 0))
    return pl.pallas_call(
        body, name=name,
        grid_spec=pltpu.PrefetchScalarGridSpec(
            num_scalar_prefetch=1, grid=(Q, R // tr),
            in_specs=[pl.BlockSpec((1, 1, tr, Cc), lambda q, i, cr: (q, cr[0], i, 0)), blk], out_specs=[blk, blk]),
        out_shape=[jax.ShapeDtypeStruct((Q, R, Cc), bf16), jax.ShapeDtypeStruct((Q, R, Cc), f32)],
        compiler_params=_cp(("arbitrary", "arbitrary")),
    )(c_arr, g, s)


_FLIPS = [(fx, fy, fc) for fx in (0, 1) for fy in (0, 1) for fc in (0, 1)][1:]


def _sum_blocks(own, r, name):
    R, Cc = own.shape
    tr = min(R, 256)

    def body(own_ref, r_ref, o_ref):
        acc = own_ref[...]
        for j in range(3):
            acc = acc + r_ref[j].astype(f32)
        o_ref[...] = acc

    return pl.pallas_call(
        body, name=name, grid=(R // tr,),
        in_specs=[pl.BlockSpec((tr, Cc), lambda i: (i, 0)), pl.BlockSpec((3, tr, Cc), lambda i: (0, i, 0))],
        out_specs=pl.BlockSpec((tr, Cc), lambda i: (i, 0)),
        out_shape=jax.ShapeDtypeStruct((R, Cc), f32),
        compiler_params=_cp(("arbitrary",)),
    )(own, r)


def _sum_packs(me8_arr, pack, rp):
    R = pack.shape[0]

    def body(me_ref, pk_ref, rp_ref, o_ref):
        me8 = me_ref[0]
        acc = None
        for d in range(8):
            rel = d ^ me8
            term = jnp.where(rel == 0, pk_ref[...], rp_ref[jnp.maximum(rel - 1, 0)])
            acc = term if acc is None else acc + term
        o_ref[...] = acc

    return pl.pallas_call(
        body, name="sum_packs",
        grid_spec=pltpu.PrefetchScalarGridSpec(
            num_scalar_prefetch=1, grid=(1,),
            in_specs=[pl.BlockSpec((R, 128), lambda i, mr: (0, 0)), pl.BlockSpec((7, R, 128), lambda i, mr: (0, 0, 0))],
            out_specs=pl.BlockSpec((R, 128), lambda i, mr: (0, 0))),
        out_shape=jax.ShapeDtypeStruct((R, 128), f32),
        compiler_params=_cp(("arbitrary",)),
    )(me8_arr, pack, rp)


def _swap_finished(fs, pack):
    na = len(fs)
    R = pack.shape[0]

    def body(*refs):
        srcs, pk = refs[:na], refs[na]
        dsts, rp = refs[na + 1:2 * na + 1], refs[2 * na + 1]
        send_sems, recv_sems = refs[2 * na + 2:]
        x, y, c, _ = _place()
        cps = [pltpu.make_async_remote_copy(
            src_ref=srcs[i], dst_ref=dsts[i], send_sem=send_sems.at[i], recv_sem=recv_sems.at[i],
            device_id=(x, y, 1 - c), device_id_type=MESH) for i in range(na)]
        cps += [pltpu.make_async_remote_copy(
            src_ref=pk, dst_ref=rp.at[k], send_sem=send_sems.at[na + k], recv_sem=recv_sems.at[na + k],
            device_id=(x ^ fx, y ^ fy, c ^ fc), device_id_type=MESH) for k, (fx, fy, fc) in enumerate(_FLIPS)]
        for cp in cps:
            cp.start()
        for cp in cps:
            cp.wait()

    return pl.pallas_call(
        body, name="swap_finished",
        in_specs=[ANY] * (na + 1), out_specs=[ANY] * (na + 1),
        out_shape=[jax.ShapeDtypeStruct(f.shape, f.dtype) for f in fs] + [jax.ShapeDtypeStruct((7, R, 128), f32)],
        scratch_shapes=[pltpu.SemaphoreType.DMA((na + 7,)), pltpu.SemaphoreType.DMA((na + 7,))],
        compiler_params=pltpu.CompilerParams(has_side_effects=True),
    )(*fs, pack)


def _adamw_halves(c_arr, mine, peer, w, m, v, name):
    _, R, Cc = w.shape
    tr = min(R, 256)

    def body(c_ref, mine_ref, peer_ref, w_ref, m_ref, v_ref, g_ref, d_ref, m2_ref, v2_ref):
        g = jnp.where(pl.program_id(0) == c_ref[0], mine_ref[...], peer_ref[...])
        delta, m2, v2 = _adam_math(g, w_ref[0], m_ref[0], v_ref[0])
        g_ref[0] = g
        d_ref[0] = delta
        m2_ref[0] = m2
        v2_ref[0] = v2

    half = pl.BlockSpec((tr, Cc), lambda hh, i, cr: (i, 0))
    full = pl.BlockSpec((1, tr, Cc), lambda hh, i, cr: (hh, i, 0))
    return pl.pallas_call(
        body, name=name,
        grid_spec=pltpu.PrefetchScalarGridSpec(
            num_scalar_prefetch=1, grid=(2, R // tr), in_specs=[half, half, full, full, full], out_specs=[full] * 4),
        out_shape=[jax.ShapeDtypeStruct((2, R, Cc), f32)] * 4,
        compiler_params=_cp(("arbitrary", "arbitrary")),
    )(c_arr, mine, peer, w, m, v)


def _rows8(a):
    flat = a.reshape(-1)
    n = flat.shape[0]
    rows = -(-n // 1024) * 8
    return jnp.pad(flat, (0, rows * 128 - n)).reshape(rows, 128)


def kernel(x, meta_tokens, norm_w, w_in, conv_w, hg_lb_logits, hg_norm_w, gdn_A_log, gdn_dt_bias, gdn_norm_w, w_out, final_norm_w, loss_target, m_meta_tokens, m_norm_w, m_w_in, m_conv_w, m_hg_lb_logits, m_hg_norm_w, m_gdn_A_log, m_gdn_dt_bias, m_gdn_norm_w, m_w_out, m_final_norm_w, v_meta_tokens, v_norm_w, v_w_in, v_conv_w, v_hg_lb_logits, v_hg_norm_w, v_gdn_A_log, v_gdn_dt_bias, v_gdn_norm_w, v_w_out, v_final_norm_w):
    me = 2 * lax.axis_index("x") + lax.axis_index("y")

    g_win, g_wout, g_conv, g_meta = _gather_weights(
        jnp.transpose(w_in, (2, 0, 1)), [w_out[0].astype(bf16).reshape(2, D // 8, D)], [conv_w[0], meta_tokens])
    wbig = jnp.pad(g_win.reshape(IN_COLS, D), ((0, PC - IN_COLS), (0, 0)))
    wout_full = g_wout.reshape(D, D)
    conv4 = jnp.transpose(g_conv, (1, 0, 2)).reshape(4, 1, 3 * HD)
    meta_full = jnp.transpose(g_meta, (1, 0, 2)).reshape(N_META, D)

    c_arr = lax.axis_index("c").reshape(1).astype(jnp.int32)

    def chip_partials(gw, g_wout_part):
        g_in2 = gw.reshape(1, 2, D // 2, PC)
        g_out4 = g_wout_part.reshape(4, 2, D // 8, D)
        s_in, s_out = _swap_halves([g_in2, g_out4])
        pb_in, pf_in = _add_halves(c_arr, g_in2, s_in, "add_w_in")
        pb_out, pf_out = _add_halves(c_arr, g_out4, s_out, "add_w_out")
        pb_blocks = jnp.transpose(pb_in[0, :, 0:IN_COLS].reshape(D // 2, 4, SHARD_COLS), (1, 0, 2))
        own_in = lax.dynamic_slice(pf_in[0], (0, me * SHARD_COLS), (D // 2, SHARD_COLS))
        own_out = lax.dynamic_index_in_dim(pf_out, me, axis=0, keepdims=False)
        return [pb_blocks, pb_out], [own_in, own_out]

    (loss8, grad_x, d_meta, d_nw, d_conv, d_lb, d_hgw, d_alog, d_dtb, d_gdw, d_fw, pfs, rs) = _local_step(
        x, loss_target, wbig, wout_full, conv4, meta_full, norm_w, hg_lb_logits, hg_norm_w, gdn_A_log, gdn_dt_bias,
        gdn_norm_w, final_norm_w, chip_partials)

    pack = jnp.concatenate([
        loss8, d_nw[0].reshape(8, 128), d_lb.reshape(8, 128), d_hgw, _rows8(d_alog[0, :H]), _rows8(d_dtb[0, :H]),
        d_gdw, d_fw[0].reshape(8, 128), d_meta.reshape(128, 128), d_conv.reshape(48, 128)], axis=0)
    return _reduce_and_update(
        me, c_arr, grad_x, pfs, rs, pack, meta_tokens, norm_w, w_in, conv_w, hg_lb_logits, hg_norm_w, gdn_A_log,
        gdn_dt_bias, gdn_norm_w, w_out, final_norm_w, m_meta_tokens, m_norm_w, m_w_in, m_conv_w, m_hg_lb_logits,
        m_hg_norm_w, m_gdn_A_log, m_gdn_dt_bias, m_gdn_norm_w, m_w_out, m_final_norm_w, v_meta_tokens, v_norm_w, v_w_in,
        v_conv_w, v_hg_lb_logits, v_hg_norm_w, v_gdn_A_log, v_gdn_dt_bias, v_gdn_norm_w, v_w_out, v_final_norm_w)


def _local_step(x, loss_target, wbig, wout_full, conv4, meta_full, norm_w, hg_lb_logits, hg_norm_w, gdn_A_log, gdn_dt_bias,
                gdn_norm_w, final_norm_w, chip_partials):
    head = jnp.concatenate([jnp.zeros((PAD, D), f32), meta_full], axis=0)
    target = loss_target.reshape(NB * SEQ, D)
    l0, l1 = hg_lb_logits[0:1], hg_lb_logits[1:2]
    alog = jnp.pad(gdn_A_log, ((0, 0), (0, DK - H)))
    dtb = jnp.pad(gdn_dt_bias, ((0, 0), (0, DK - H)))
    fw = final_norm_w.reshape(1, D)

    hflat, proj, ut, cv2 = _in_proj(x.reshape(NB * SEQ, D), head, norm_w, wbig, conv4)
    proj3 = proj.reshape(NB, TP, PC)
    cv = cv2.reshape(NB, TP, 3 * HD)
    o_hg, s_hg, o_gd, s_gd, t_gd = _mix_fwd(proj3, cv, l0, l1, alog, dtb)
    (loss8, d_ohg, d_ogd, d_zhg, d_zgd, dh_res, g_wout_part, d_hgw, d_gdw, d_fw) = _out_loss(
        o_hg.reshape(N, HD), o_gd.reshape(N, HD), proj, hg_norm_w, gdn_norm_w, wout_full, hflat, fw, target)
    d_hg, d_l0, d_l1 = _hg_bwd(proj3, l0, l1, s_hg, d_ohg.reshape(NB, TP, HD))
    d_cv, d_ab, d_alog, d_dtb = _gd_bwd(cv, proj3, alog, dtb, s_gd, t_gd, d_ogd.reshape(NB, TP, HD))
    d_qkv, d_conv4 = _conv_bwd(proj3, conv4, d_cv)
    d_hg2, d_qkv2, d_ab2 = d_hg.reshape(N, 3 * HD), d_qkv.reshape(N, 3 * HD), d_ab.reshape(N, DK)
    pieces = [(d_hg2, COL_HG), (d_zhg, COL_ZHG), (d_qkv2, COL_QKV), (d_zgd, COL_ZGD), (d_ab2, COL_AB)]
    gw = _w_grad(ut, pieces)
    pbs, pfs = chip_partials(gw, g_wout_part) if chip_partials else ([], [gw, g_wout_part])
    dh, d_nw, *rs = _in_bwd(pieces, wbig, hflat, norm_w, dh_res, pbs)

    dh3 = dh.reshape(NB, TP, D)
    grad_x = dh3[:, PAD + N_META:, :]
    d_meta = jnp.sum(dh3[:, PAD:PAD + N_META, :], axis=0)
    d_conv = d_conv4[:, 0, :]
    d_lb = jnp.concatenate([d_l0[0:1], d_l1[0:1]], axis=0)
    return loss8, grad_x, d_meta, d_nw, d_conv, d_lb, d_hgw, d_alog, d_dtb, d_gdw, d_fw, pfs, rs


def _reduce_and_update(me, c_arr, grad_x, pfs, rs, pack, meta_tokens, norm_w, w_in, conv_w, hg_lb_logits, hg_norm_w,
                       gdn_A_log, gdn_dt_bias, gdn_norm_w, w_out, final_norm_w, m_meta_tokens, m_norm_w, m_w_in, m_conv_w,
                       m_hg_lb_logits, m_hg_norm_w, m_gdn_A_log, m_gdn_dt_bias, m_gdn_norm_w, m_w_out, m_final_norm_w,
                       v_meta_tokens, v_norm_w, v_w_in, v_conv_w, v_hg_lb_logits, v_hg_norm_w, v_gdn_A_log, v_gdn_dt_bias,
                       v_gdn_norm_w, v_w_out, v_final_norm_w):
    (own_in, own_out), (r_in, r_out) = pfs, rs
    f_in = _sum_blocks(own_in, r_in, "sum_w_in")
    f_out = _sum_blocks(own_out, r_out, "sum_w_out")
    o_in, o_out, r_pack = _swap_finished([f_in, f_out], pack)
    me8_arr = (2 * me + lax.axis_index("c")).reshape(1).astype(jnp.int32)
    small = _sum_packs(me8_arr, pack, r_pack)

    half_out = lambda a: a[0].reshape(2, D // 8, D)
    is0 = lax.axis_index("c") == 0
    g_in = jnp.concatenate([jnp.where(is0, f_in, o_in), jnp.where(is0, o_in, f_in)], axis=0)
    to_cm = lambda a: jnp.transpose(a, (2, 0, 1))
    gi, di, mi, vi = [jnp.transpose(a, (1, 2, 0))[0] for a in _adamw_rows(
        g_in.T.reshape(SHARD_COLS, 1, D), to_cm(w_in), to_cm(m_w_in), to_cm(v_w_in), "adamw_w_in")]
    go, do_, mo, vo = [a.reshape(D // 4, D) for a in _adamw_halves(
        c_arr, f_out, o_out, half_out(w_out), half_out(m_w_out), half_out(v_w_out), "adamw_w_out")]

    g_meta_full = small[64:192].reshape(N_META, D)
    g_meta_loc = lax.dynamic_slice(g_meta_full, (0, me * 256), (N_META, 256))
    gm, dm, mm_, vm = _adamw([g_meta_loc], meta_tokens, m_meta_tokens, v_meta_tokens, "adamw_meta")
    g_conv_full = small[192:240].reshape(4, 1536)
    g_conv_loc = lax.dynamic_slice(g_conv_full, (0, me * 384), (4, 384))
    gc, dc, mc, vc = _adamw([g_conv_loc], conv_w[0], m_conv_w[0], v_conv_w[0], "adamw_conv")

    reps = [(norm_w, m_norm_w, v_norm_w), (hg_lb_logits, m_hg_lb_logits, v_hg_lb_logits),
            (hg_norm_w, m_hg_norm_w, v_hg_norm_w), (gdn_A_log, m_gdn_A_log, v_gdn_A_log),
            (gdn_dt_bias, m_gdn_dt_bias, v_gdn_dt_bias), (gdn_norm_w, m_gdn_norm_w, v_gdn_norm_w),
            (final_norm_w, m_final_norm_w, v_final_norm_w)]
    wp = jnp.concatenate([_rows8(t[0]) for t in reps], axis=0)
    mp = jnp.concatenate([_rows8(t[1]) for t in reps], axis=0)
    vp = jnp.concatenate([_rows8(t[2]) for t in reps], axis=0)
    gr, dr, mr, vr = _adamw([small[8:64]], wp, mp, vp, "adamw_small")

    def unpack(p):
        outs = []
        for i, t in enumerate(reps):
            n = t[0].size
            outs.append(p[8 * i:8 * i + 8].reshape(-1)[:n].reshape(t[0].shape))
        return outs

    def leaves(meta_v, conv_v, in_v, out_v, rep_p):
        nw, lb, hgw, al, db, gdw, fwv = unpack(rep_p)
        return [meta_v, nw, in_v[None], conv_v[None], lb, hgw, al, db, gdw, out_v[None], fwv]

    loss = small[0, 0]
    return (loss, grad_x, *leaves(gm, gc, gi, go, gr), *leaves(dm, dc, di, do_, dr),
            *leaves(mm_, mc, mi, mo, mr), *leaves(vm, vc, vi, vo, vr))
```

```python
import functools

import jax
import jax.numpy as jnp
from jax import lax
from jax.experimental import pallas as pl
from jax.experimental.pallas import tpu as pltpu

f32 = jnp.float32
bf16 = jnp.bfloat16
MESH = pl.DeviceIdType.MESH
ANY = pl.BlockSpec(memory_space=pl.ANY)

D = 1024
NB = 2
N_META = 16
SEQ = 2048
PAD = 48
TP = PAD + N_META + SEQ
C = 64
NCH = TP // C
N = NB * TP
H = 4
DK = 128
HD = H * DK
PC = 4224
IN_COLS = 4104
SHARD_COLS = IN_COLS // 4
COL_HG, COL_ZHG, COL_QKV, COL_ZGD, COL_AB = 0, 3 * HD, 4 * HD, 7 * HD, 8 * HD
EPS = 1e-6
ADAM_LR, ADAM_B1, ADAM_B2, ADAM_EPS, ADAM_WD, ADAM_STEP = 0.001, 0.9, 0.999, 1e-08, 0.01, 10
VMEM_LIMIT = 56 * 1024 * 1024

P_HG = dict(lvl=1, av=1, qs=1, su=1)
P_GD = dict(kk=1, inv=1, sol=1, ws=1, qk=1, o=1, su=1)


def _cp(sem=None, **kw):
    return pltpu.CompilerParams(dimension_semantics=sem, vmem_limit_bytes=VMEM_LIMIT, **kw)


_DIMS = {"nn": (((1,), (0,)), ((), ())), "nt": (((1,), (1,)), ((), ())), "tn": (((0,), (0,)), ((), ()))}


def _split(x):
    hi = x.astype(bf16)
    return hi, (x - hi.astype(f32)).astype(bf16)


def _dg(a, b, kind, passes):
    d = lambda x, y: lax.dot_general(x, y, _DIMS[kind], preferred_element_type=f32)
    if passes == 1:
        return d(a.astype(bf16), b.astype(bf16))
    ah, al = _split(a)
    bh, bl = _split(b)
    return d(ah, bh) + d(ah, bl) + d(al, bh)


@functools.partial(jax.custom_vjp, nondiff_argnums=(2, 3))
def mmx(a, b, kind, passes):
    return _dg(a, b, kind, passes)


def _mmx_fwd(a, b, kind, passes):
    return _dg(a, b, kind, passes), (a, b)


def _mmx_bwd(kind, passes, res, g):
    a, b = res
    if kind == "nn":
        return _dg(g, b, "nt", passes), _dg(a, g, "tn", passes)
    if kind == "nt":
        return _dg(g, b, "nn", passes), _dg(g, a, "tn", passes)
    return _dg(b, g, "nt", passes), _dg(a, g, "nn", passes)


mmx.defvjp(_mmx_fwd, _mmx_bwd)


def _mask_dg(mask, x):
    xh, xl = _split(x)
    return jnp.dot(jnp.concatenate([mask, mask], axis=1), jnp.concatenate([xh, xl], axis=0), preferred_element_type=f32)


@functools.partial(jax.custom_vjp, nondiff_argnums=(2,))
def mask_mm(mask, x, bwd_passes):
    return _mask_dg(mask, x)


def _mask_fwd(mask, x, bwd_passes):
    return _mask_dg(mask, x), mask


def _mask_bwd(bwd_passes, mask, g):
    d = lambda y: lax.dot_general(mask, y, _DIMS["tn"], preferred_element_type=f32)
    if bwd_passes == 1:
        return None, d(g.astype(bf16))
    gh, gl = _split(g)
    return None, d(gh) + d(gl)


mask_mm.defvjp(_mask_fwd, _mask_bwd)


def bdot(a, b):
    return jnp.dot(a.astype(bf16), b.astype(bf16), preferred_element_type=f32)


def bdot_nt(a, b):
    return lax.dot_general(a.astype(bf16), b.astype(bf16), _DIMS["nt"], preferred_element_type=f32)


def bdot_tn(a, b):
    return lax.dot_general(a.astype(bf16), b.astype(bf16), _DIMS["tn"], preferred_element_type=f32)


def _iota2(n, m):
    return lax.broadcasted_iota(jnp.int32, (n, m), 0), lax.broadcasted_iota(jnp.int32, (n, m), 1)


sigmoid = jax.nn.sigmoid


def silu(x):
    return x * sigmoid(x)


def softplus(x):
    return jnp.maximum(x, 0.0) + jnp.log(1.0 + jnp.exp(-jnp.abs(x)))


def rmsnorm(x, w):
    return x * lax.rsqrt(jnp.mean(x * x, axis=-1, keepdims=True) + EPS) * w


def hg_masks():
    t, r = _iota2(C, C)
    mats = [r <= t, r > t]
    lvl = []
    for l in range(1, 7):
        sz = 1 << l
        half = sz >> 1
        seg_t = t >> l
        upper_t = (t & (sz - 1)) >= half
        mid_t = seg_t * sz + half - 1
        mats.append((upper_t & (r > mid_t) & (r <= t)) | ((~upper_t) & (r > t) & (r <= mid_t)))
        lvl.append(((seg_t == (r >> l)) & upper_t & ((r & (sz - 1)) < half)).astype(f32))
    stk = jnp.concatenate([m.astype(bf16) for m in mats], axis=0)
    return stk, lvl, (t == r).astype(f32)


def _head(a, h):
    return a[:, h * DK:(h + 1) * DK]


def _run(*gens):
    results = [None] * len(gens)
    live = list(range(len(gens)))
    while live:
        for i in list(live):
            try:
                next(gens[i])
            except StopIteration as e:
                results[i] = e.value
                live.remove(i)
    return results


def hg_chunk(St, ps, l0, l1):
    return _run(hg_stages(St, ps, l0, l1))[0]


def gd_chunk(S, cs, abs_, alog, dtb, t_saved=None):
    return _run(gd_stages(S, cs, abs_, alog, dtb, t_saved))[0]


def mix_chunk(St, ps, l0, l1, S, cs, abs_, alog, dtb):
    (sn_h, o_h), (sn_g, o_g, t_pack) = _run(hg_stages(St, ps, l0, l1), gd_stages(S, cs, abs_, alog, dtb))
    return sn_h, o_h, sn_g, o_g, t_pack


def hg_stages(St, ps, l0, l1):
    m = jnp.maximum(l0, l1)
    e0 = jnp.exp(l0 - m)
    e1 = jnp.exp(l1 - m)
    lb = e0 / (e0 + e1)
    stk, lvl, eye = hg_masks()
    msk = [eye] + lvl
    trow = lax.broadcasted_iota(jnp.int32, (C, 1), 0)
    upper = [(trow & ((1 << l) - 1)) >= (1 << (l - 1)) for l in range(1, 7)]
    qs, ks, vs, qG, kR, eGl = [], [], [], [], [], []
    for p in ps:
        pq, pf, v = p[:, 0:HD], p[:, HD:2 * HD], p[:, 2 * HD:3 * HD]
        q = silu(pq)
        f = lb + (1.0 - lb) * sigmoid(pf)
        k = 1.0 - f
        logf = jnp.log(f)
        Dm = mask_mm(stk, logf, 1)
        z = [jnp.where(up, q, k) * jnp.exp(Dm[(2 + i) * C:(3 + i) * C]) for i, up in enumerate(upper)]
        qs.append([q] + z)
        ks.append([k] + z)
        vs.append(v)
        qG.append(q * jnp.exp(Dm[0:C]))
        kR.append(k * jnp.exp(Dm[C:2 * C]))
        eGl.append(jnp.exp(jnp.sum(logf, axis=0, keepdims=True)))
    yield
    units = [(b, h) for b in range(len(ps)) for h in range(H)]
    parts = []
    for i in range(7):
        parts.append([msk[i] * mmx(_head(qs[b][i], h), _head(ks[b][i], h), "nt", P_HG["lvl"]) for b, h in units])
        yield
    A = [functools.reduce(lambda x, y: x + y, [parts[i][n] for i in range(7)]) for n in range(len(units))]
    qS = [mmx(_head(qG[b], h), St[n], "nt", P_HG["qs"]) for n, (b, h) in enumerate(units)]
    Sn = [St[n] * _head(eGl[b], h) + mmx(_head(vs[b], h), _head(kR[b], h), "tn", P_HG["su"])
          for n, (b, h) in enumerate(units)]
    yield
    outs = [mmx(A[n], _head(vs[b], h), "nn", P_HG["av"]) + qS[n] for n, (b, h) in enumerate(units)]
    return tuple(Sn), tuple(jnp.concatenate(outs[b * H:(b + 1) * H], axis=1) for b in range(len(ps)))


@jax.custom_vjp
def use_inverse(A, T):
    return T


def _use_inverse_fwd(A, T):
    return T, T


def _use_inverse_bwd(T, g):
    return -_dg(T, _dg(g, T, "nt", P_GD["inv"]), "tn", P_GD["inv"]), jnp.zeros_like(T)


use_inverse.defvjp(_use_inverse_fwd, _use_inverse_bwd)


def gd_stages(S, cs, abs_, alog, dtb, t_saved=None):
    t, r = _iota2(C, C)
    tri = (r <= t).astype(bf16)
    ups = (r > t).astype(bf16)
    lane = lax.broadcasted_iota(jnp.int32, (1, DK), 1)
    subl = lax.broadcasted_iota(jnp.int32, (8, 1), 0)
    eye = (t == r).astype(f32)
    strict = (r < t).astype(f32)
    bd = ((t >> 4) == (r >> 4)).astype(f32)
    qa, ka, va, b4, gam4, grev4, gam4T, glast4 = [], [], [], [], [], [], [], []
    for c, ab in zip(cs, abs_):
        qa.append(silu(c[:, 0:HD]))
        ka.append(silu(c[:, HD:2 * HD]))
        va.append(silu(c[:, 2 * HD:3 * HD]))
        g4 = -jnp.exp(alog) * softplus(ab + dtb)
        b4.append(sigmoid(ab))
        gam4.append(mask_mm(tri, g4, 2))
        grev4.append(mask_mm(ups, g4, 2))
        gam4T.append(gam4[-1].T)
        glast4.append(jnp.sum(g4, axis=0, keepdims=True))
    yield
    units = [(b, h) for b in range(len(cs)) for h in range(H)]
    nu = range(len(units))
    inv = lambda a, b: [mmx(a[n], b[n], "nn", P_GD["inv"]) for n in nu]
    v = [_head(va[b], h) for b, h in units]
    q = [_head(qa[b], h) for b, h in units]
    k = [_head(ka[b], h) for b, h in units]
    q = [x * lax.rsqrt(jnp.sum(x * x, -1, keepdims=True) + EPS) * (DK ** -0.5) for x in q]
    k = [x * lax.rsqrt(jnp.sum(x * x, -1, keepdims=True) + EPS) for x in k]
    oh = [(lane == h).astype(f32) for h in range(H)]
    gam_c = [jnp.sum(gam4[b] * oh[h], -1, keepdims=True) for b, h in units]
    grev_c = [jnp.sum(grev4[b] * oh[h], -1, keepdims=True) for b, h in units]
    beta = [jnp.sum(b4[b] * (lane == h + H).astype(f32), -1, keepdims=True) for b, h in units]
    glast = [jnp.sum(glast4[b] * oh[h], -1, keepdims=True) for b, h in units]
    gam_r = [jnp.sum(gam4T[b][0:8, :] * (subl == h).astype(f32), axis=0, keepdims=True) for b, h in units]
    dec = [jnp.exp(jnp.where(r <= t, gam_c[n] - gam_r[n], -1e30)) for n in nu]
    egam = [jnp.exp(gam_c[n]) for n in nu]
    kk = [mmx(k[n], k[n], "nt", P_GD["kk"]) for n in nu]
    qk = [mmx(q[n], k[n], "nt", P_GD["qk"]) * dec[n] for n in nu]
    yield
    A = [beta[n] * kk[n] * dec[n] * strict for n in nu]
    Dg = [A[n] * bd for n in nu]
    L = [A[n] - Dg[n] for n in nu]
    if t_saved is None:
        ImD = [eye - Dg[n] for n in nu]
        D2 = inv(Dg, Dg)
        yield
        P1 = inv(ImD, [eye + x for x in D2])
        D4 = inv(D2, D2)
        yield
        P2 = inv(P1, [eye + x for x in D4])
        D8 = inv(D4, D4)
        yield
        M = inv(P2, [eye + x for x in D8])
        yield
        Nn = inv(M, L)
        yield
        N2 = inv(Nn, Nn)
        yield
        T1 = inv([eye - x for x in Nn], [eye + x for x in N2])
        yield
        Tinv = inv(T1, M)
        yield
    else:
        Tinv = [use_inverse(A[n], t_saved[b][:, h * DK:h * DK + C]) for n, (b, h) in enumerate(units)]
    rhs = [jnp.concatenate([beta[n] * v[n], (beta[n] * egam[n]) * k[n]], axis=1) for n in nu]
    sol = [mmx(Tinv[n], rhs[n], "nn", P_GD["sol"]) for n in nu]
    yield
    qwS = [mmx(jnp.concatenate([q[n] * egam[n], sol[n][:, DK:2 * DK]], axis=0), S[n], "nn", P_GD["ws"]) for n in nu]
    yield
    u = [sol[n][:, 0:DK] - qwS[n][C:2 * C] for n in nu]
    outs = [qwS[n][0:C] + mmx(qk[n], u[n], "nn", P_GD["o"]) for n in nu]
    Sn = [jnp.exp(glast[n]) * S[n] + mmx(k[n] * jnp.exp(grev_c[n]), u[n], "tn", P_GD["su"]) for n in nu]
    zpad = jnp.zeros((C, DK - C), f32)
    t_pack = tuple(jnp.concatenate([x for n in range(b * H, (b + 1) * H) for x in (lax.stop_gradient(Tinv[n]), zpad)],
                                   axis=1) for b in range(len(cs)))
    return tuple(Sn), tuple(jnp.concatenate(outs[b * H:(b + 1) * H], axis=1) for b in range(len(cs))), t_pack


def _chunk_index(tile_chunks, k):
    def index(i):
        chunk = tile_chunks * i + k
        b = chunk // NCH
        return jnp.maximum((SEQ // C) * b + chunk - NCH * b - 1, 0), 0
    return index


def _in_proj(xflat, head, norm_w, wbig, conv4):
    tm = 384
    nck = tm // C
    W3 = 3 * HD

    def body(*refs):
        x_refs = refs[:nck]
        head_ref, nw_ref, w_ref, cw_ref, h_ref, p_ref, ut_ref, cv_ref, prev = refs[nck:]
        i = pl.program_id(0)

        @pl.when(i == 0)
        def _():
            prev[...] = jnp.zeros_like(prev)

        blocks = []
        for k in range(nck):
            chunk = nck * i + k
            blocks.append(jnp.where(chunk - NCH * (chunk // NCH) == 0, head_ref[...], x_refs[k][...]))
        hval = jnp.concatenate(blocks, axis=0)
        h_ref[...] = hval
        u = rmsnorm(hval, nw_ref[...])
        ut_ref[...] = u.T.astype(bf16)
        p = bdot_nt(u, w_ref[...])
        p_ref[...] = p
        x = p[:, COL_QKV:COL_QKV + W3]
        xx = jnp.concatenate([prev[...], x], axis=0)
        y = cw_ref[3] * x
        for s in (1, 2, 3):
            y = y + cw_ref[3 - s] * pltpu.roll(xx, s, 0)[8:]
        row = i * tm + lax.broadcasted_iota(jnp.int32, (tm, 1), 0)
        tok = jnp.where(row >= TP, row - TP, row)
        cv_ref[...] = jnp.where(tok >= 8, y, 0.0)
        prev[...] = x[tm - 8:tm]

    return pl.pallas_call(
        body, name="in_proj", grid=(N // tm,),
        in_specs=[pl.BlockSpec((C, D), _chunk_index(nck, k)) for k in range(nck)]
        + [pl.BlockSpec((C, D), lambda i: (0, 0)), pl.BlockSpec((1, D), lambda i: (0, 0)),
           pl.BlockSpec((PC, D), lambda i: (0, 0)), pl.BlockSpec((4, 1, W3), lambda i: (0, 0, 0))],
        out_specs=[pl.BlockSpec((tm, D), lambda i: (i, 0)), pl.BlockSpec((tm, PC), lambda i: (i, 0)),
                   pl.BlockSpec((D, tm), lambda i: (0, i)), pl.BlockSpec((tm, W3), lambda i: (i, 0))],
        out_shape=[jax.ShapeDtypeStruct((N, D), f32), jax.ShapeDtypeStruct((N, PC), f32),
                   jax.ShapeDtypeStruct((D, N), bf16), jax.ShapeDtypeStruct((N, W3), f32)],
        scratch_shapes=[pltpu.VMEM((8, W3), f32)],
        compiler_params=_cp(("arbitrary",)),
    )(*[xflat] * nck, head, norm_w, wbig, conv4)


NU = NB * H
_REV = lambda c: NCH - 1 - c
_FWD = lambda c: c


def _tok_spec(w, ix, col=0):
    return pl.BlockSpec((NB, C, w), lambda c: (0, ix(c), col))


def _state_spec(ix):
    return pl.BlockSpec((NB, 1, H, DK, DK), lambda c: (0, ix(c), 0, 0, 0))


def _row_spec(w):
    return pl.BlockSpec((1, w), lambda c: (0, 0))


def _rows(ref):
    return tuple(ref[b] for b in range(NB))


def _hg_extra_specs(ix):
    return [_row_spec(HD), _row_spec(HD)]


def _gd_extra_specs(ix):
    return [_tok_spec(DK, ix, COL_AB // DK), _row_spec(DK), _row_spec(DK)]


def _mix_fwd(proj3, cv, l0, l1, alog, dtb):
    def body(p_ref, c_ref, ab_ref, l0_ref, l1_ref, al_ref, db_ref, oh_ref, sh_ref, og_ref, sg_ref, t_ref, sth, stg):
        @pl.when(pl.program_id(0) == 0)
        def _():
            sth[...] = jnp.zeros_like(sth)
            stg[...] = jnp.zeros_like(stg)

        Sh = tuple(sth[n] for n in range(NU))
        Sg = tuple(stg[n] for n in range(NU))
        for n in range(NU):
            sh_ref[n // H, 0, n % H] = Sh[n]
            sg_ref[n // H, 0, n % H] = Sg[n]
        snh, oh, sng, og, tp = mix_chunk(Sh, _rows(p_ref), l0_ref[...], l1_ref[...],
                                         Sg, _rows(c_ref), _rows(ab_ref), al_ref[...], db_ref[...])
        for n in range(NU):
            sth[n] = snh[n]
            stg[n] = sng[n]
        for b in range(NB):
            oh_ref[b] = oh[b]
            og_ref[b] = og[b]
            t_ref[b] = tp[b]

    tok = jax.ShapeDtypeStruct((NB, TP, HD), f32)
    st = jax.ShapeDtypeStruct((NB, NCH, H, DK, DK), f32)
    return pl.pallas_call(
        body, name="mix_fwd", grid=(NCH,),
        in_specs=[_tok_spec(3 * HD, _FWD), _tok_spec(3 * HD, _FWD), _tok_spec(DK, _FWD, COL_AB // DK),
                  _row_spec(HD), _row_spec(HD), _row_spec(DK), _row_spec(DK)],
        out_specs=[_tok_spec(HD, _FWD), _state_spec(_FWD), _tok_spec(HD, _FWD), _state_spec(_FWD), _tok_spec(HD, _FWD)],
        out_shape=[tok, st, tok, st, tok],
        scratch_shapes=[pltpu.VMEM((NU, DK, DK), f32), pltpu.VMEM((NU, DK, DK), f32)],
        compiler_params=_cp(("arbitrary",)),
    )(proj3, cv, proj3, l0, l1, alog, dtb)


def _hg_bwd(proj3, l0, l1, s_saved, do):
    def body(p_ref, l0_ref, l1_ref, s_ref, do_ref, dp_ref, dl0_ref, dl1_ref, dst):
        @pl.when(pl.program_id(0) == 0)
        def _():
            dst[...] = jnp.zeros_like(dst)
            dl0_ref[...] = jnp.zeros_like(dl0_ref)
            dl1_ref[...] = jnp.zeros_like(dl1_ref)

        S = tuple(s_ref[n // H, 0, n % H] for n in range(NU))
        _, vjp = jax.vjp(hg_chunk, S, _rows(p_ref), l0_ref[...], l1_ref[...])
        dS, dp, dl0, dl1 = vjp((tuple(dst[n] for n in range(NU)), _rows(do_ref)))
        for n in range(NU):
            dst[n] = dS[n]
        for b in range(NB):
            dp_ref[b] = dp[b].astype(bf16)
        dl0_ref[...] += jnp.broadcast_to(dl0, (8, HD))
        dl1_ref[...] += jnp.broadcast_to(dl1, (8, HD))

    acc = pl.BlockSpec((8, HD), lambda c: (0, 0))
    return pl.pallas_call(
        body, name="hg_bwd", grid=(NCH,),
        in_specs=[_tok_spec(3 * HD, _REV)] + _hg_extra_specs(_REV) + [_state_spec(_REV), _tok_spec(HD, _REV)],
        out_specs=[_tok_spec(3 * HD, _REV), acc, acc],
        out_shape=[jax.ShapeDtypeStruct((NB, TP, 3 * HD), bf16), jax.ShapeDtypeStruct((8, HD), f32),
                   jax.ShapeDtypeStruct((8, HD), f32)],
        scratch_shapes=[pltpu.VMEM((NU, DK, DK), f32)],
        compiler_params=_cp(("arbitrary",)),
    )(proj3, l0, l1, s_saved, do)


def _gd_bwd(cv, proj3, alog, dtb, s_saved, t_saved, do):
    def body(c_ref, ab_ref, al_ref, db_ref, s_ref, t_ref, do_ref, dc_ref, dab_ref, dal_ref, ddb_ref, dst):
        @pl.when(pl.program_id(0) == 0)
        def _():
            dst[...] = jnp.zeros_like(dst)
            dal_ref[...] = jnp.zeros_like(dal_ref)
            ddb_ref[...] = jnp.zeros_like(ddb_ref)

        S = tuple(s_ref[n // H, 0, n % H] for n in range(NU))
        t_rows = _rows(t_ref)
        fn = lambda *a: gd_chunk(*a, t_saved=t_rows)[0:2]
        _, vjp = jax.vjp(fn, S, _rows(c_ref), _rows(ab_ref), al_ref[...], db_ref[...])
        dS, dc, dab, dal, ddb = vjp((tuple(dst[n] for n in range(NU)), _rows(do_ref)))
        for n in range(NU):
            dst[n] = dS[n]
        for b in range(NB):
            dc_ref[b] = dc[b]
            dab_ref[b] = dab[b].astype(bf16)
        dal_ref[...] += jnp.broadcast_to(dal, (8, DK))
        ddb_ref[...] += jnp.broadcast_to(ddb, (8, DK))

    acc = pl.BlockSpec((8, DK), lambda c: (0, 0))
    return pl.pallas_call(
        body, name="gd_bwd", grid=(NCH,),
        in_specs=[_tok_spec(3 * HD, _REV)] + _gd_extra_specs(_REV)
        + [_state_spec(_REV), _tok_spec(HD, _REV), _tok_spec(HD, _REV)],
        out_specs=[_tok_spec(3 * HD, _REV), _tok_spec(DK, _REV), acc, acc],
        out_shape=[jax.ShapeDtypeStruct((NB, TP, 3 * HD), f32), jax.ShapeDtypeStruct((NB, TP, DK), bf16),
                   jax.ShapeDtypeStruct((8, DK), f32), jax.ShapeDtypeStruct((8, DK), f32)],
        scratch_shapes=[pltpu.VMEM((NU, DK, DK), f32)],
        compiler_params=_cp(("arbitrary",)),
    )(cv, proj3, alog, dtb, s_saved, t_saved, do)


def _conv_bwd(proj3, conv4, dy):
    def body(x_ref, w_ref, dy_ref, dx_ref, dw_ref):
        @pl.when(pl.program_id(1) == 0)
        def _():
            dw_ref[...] = jnp.zeros_like(dw_ref)

        x = x_ref[0]
        row = lax.broadcasted_iota(jnp.int32, (TP, 1), 0)
        g = jnp.where(row >= 8, dy_ref[0], 0.0)
        dx = w_ref[3] * g
        dw_ref[3] += jnp.broadcast_to(jnp.sum(x * g, axis=0, keepdims=True), (8, HD))
        for s in (1, 2, 3):
            dx = dx + w_ref[3 - s] * pltpu.roll(g, TP - s, 0)
            dw_ref[3 - s] += jnp.broadcast_to(jnp.sum(pltpu.roll(x, s, 0) * g, axis=0, keepdims=True), (8, HD))
        dx_ref[0] = dx.astype(bf16)

    return pl.pallas_call(
        body, name="conv_bwd", grid=(3, NB),
        in_specs=[pl.BlockSpec((1, TP, HD), lambda j, b: (b, 0, COL_QKV // HD + j)),
                  pl.BlockSpec((4, 1, HD), lambda j, b: (0, 0, j)), pl.BlockSpec((1, TP, HD), lambda j, b: (b, 0, j))],
        out_specs=[pl.BlockSpec((1, TP, HD), lambda j, b: (b, 0, j)), pl.BlockSpec((4, 8, HD), lambda j, b: (0, 0, j))],
        out_shape=[jax.ShapeDtypeStruct((NB, TP, 3 * HD), bf16), jax.ShapeDtypeStruct((4, 8, 3 * HD), f32)],
        compiler_params=_cp(("arbitrary", "arbitrary")),
    )(proj3, conv4, dy)


def _out_loss(o_hg, o_gd, proj, hgw, gdw, wout, hflat, fw, target):
    tm = 384

    def body(ohg_ref, ogd_ref, zhg_ref, zgd_ref, hgw_ref, gdw_ref, wo_ref, h_ref, fw_ref, *refs):
        tg_refs = refs[:tm // C]
        (loss_ref, dohg_ref, dogd_ref, dzhg_ref, dzgd_ref, dh_ref, dwo_ref, dhgw_ref, dgdw_ref, dfw_ref) = refs[tm // C:]
        i = pl.program_id(0)

        @pl.when(i == 0)
        def _():
            for r in (loss_ref, dwo_ref, dhgw_ref, dgdw_ref, dfw_ref):
                r[...] = jnp.zeros_like(r)

        row = i * tm + lax.broadcasted_iota(jnp.int32, (tm, 1), 0)
        tok = jnp.where(row >= TP, row - TP, row)
        valid = (tok >= PAD + N_META).astype(f32)
        hval = h_ref[...]
        tgt = jnp.concatenate([r[...] for r in tg_refs], axis=0)

        mixers = ((ohg_ref, zhg_ref, hgw_ref[...]), (ogd_ref, zgd_ref, gdw_ref[...]))
        saved, ys = [], []
        for o_ref, z_ref, w in mixers:
            for hh in range(H):
                sl = slice(hh * DK, (hh + 1) * DK)
                o, z = o_ref[:, sl], z_ref[:, sl]
                r = lax.rsqrt(jnp.mean(o * o, axis=-1, keepdims=True) + EPS)
                n = o * r
                sg = sigmoid(z)
                ws = w * (z * sg)
                saved.append((r, n, sg, z, ws, w))
                ys.append(n * ws)
        y = jnp.concatenate(ys, axis=-1)
        h2 = hval + bdot(y, wo_ref[...])
        r2 = lax.rsqrt(jnp.mean(h2 * h2, axis=-1, keepdims=True) + EPS)
        n2 = h2 * r2
        fwv = fw_ref[...]
        err = (n2 * fwv - tgt) * valid
        loss = (0.5 / D) * jnp.sum(err * err)
        dyf = err * (1.0 / D)
        dn2 = dyf * fwv
        dout = r2 * (dn2 - n2 * jnp.mean(dn2 * n2, axis=-1, keepdims=True))
        dh_ref[...] = dout
        dy = bdot_nt(dout, wo_ref[...])
        dwo_ref[...] += bdot_tn(y, dout)
        dws = []
        for mi, (do_ref, dz_ref) in enumerate(((dohg_ref, dzhg_ref), (dogd_ref, dzgd_ref))):
            dw = jnp.zeros((1, DK), f32)
            for hh in range(H):
                sl = slice(hh * DK, (hh + 1) * DK)
                r, n, sg, z, ws, w = saved[mi * H + hh]
                dyh = dy[:, mi * HD + hh * DK:mi * HD + (hh + 1) * DK]
                t = dyh * n
                dw = dw + jnp.sum(t * (z * sg), axis=0, keepdims=True)
                dz_ref[:, sl] = (t * w * (sg * (1.0 + z * (1.0 - sg)))).astype(bf16)
                dn = dyh * ws
                do_ref[:, sl] = r * (dn - n * jnp.mean(dn * n, axis=-1, keepdims=True))
            dws.append(dw)
        loss_ref[...] += jnp.broadcast_to(loss, (8, DK))
        dhgw_ref[...] += jnp.broadcast_to(dws[0], (8, DK))
        dgdw_ref[...] += jnp.broadcast_to(dws[1], (8, DK))
        dfw_ref[...] += jnp.broadcast_to(jnp.sum(dyf * n2, axis=0, keepdims=True), (8, D))

    row = lambda w: pl.BlockSpec((tm, w), lambda i: (i, 0))
    whole = lambda r, w: pl.BlockSpec((r, w), lambda i: (0, 0))
    col = lambda c0: pl.BlockSpec((tm, HD), lambda i: (i, c0 // HD))

    tgt_spec = lambda k: pl.BlockSpec((C, D), _chunk_index(tm // C, k))
    return pl.pallas_call(
        body, name="out_loss", grid=(N // tm,),
        in_specs=[row(HD), row(HD), col(COL_ZHG), col(COL_ZGD),
                  whole(1, DK), whole(1, DK), whole(D, D), row(D), whole(1, D)] + [tgt_spec(k) for k in range(tm // C)],
        out_specs=[whole(8, DK), row(HD), row(HD), row(HD), row(HD), row(D), whole(D, D),
                   whole(8, DK), whole(8, DK), whole(8, D)],
        out_shape=[jax.ShapeDtypeStruct((8, DK), f32)] + [jax.ShapeDtypeStruct((N, HD), f32)] * 2
        + [jax.ShapeDtypeStruct((N, HD), bf16)] * 2
        + [jax.ShapeDtypeStruct((N, D), f32), jax.ShapeDtypeStruct((D, D), f32),
           jax.ShapeDtypeStruct((8, DK), f32), jax.ShapeDtypeStruct((8, DK), f32), jax.ShapeDtypeStruct((8, D), f32)],
        compiler_params=_cp(("arbitrary",)),
    )(o_hg, o_gd, proj, proj, hgw, gdw, wout, hflat, fw, *[target] * (tm // C))


def _in_bwd(pieces, wbig, hflat, norm_w, dh_res, pbs):
    tm = 384
    nsteps = N // tm
    np_ = len(pieces)
    na = len(pbs)
    offs = [c0 for _, c0 in pieces]
    widths = [d.shape[1] for d, _ in pieces]

    def body(*refs):
        d_refs = refs[:np_]
        w_ref, h_ref, nw_ref, dhr_ref = refs[np_:np_ + 4]
        srcs = refs[np_ + 4:np_ + 4 + na]
        dh_ref, dnw_ref = refs[np_ + 4 + na:np_ + 6 + na]
        dsts = refs[np_ + 6 + na:np_ + 6 + 2 * na]
        sems = refs[np_ + 6 + 2 * na:]
        i = pl.program_id(0)

        def copies():
            if not na:
                return []
            x, y, c, chips = _place()
            return [pltpu.make_async_remote_copy(
                src_ref=srcs[a].at[2 * px + py], dst_ref=dsts[a].at[j], send_sem=sems[0].at[na * j + a],
                recv_sem=sems[1].at[na * j + a], device_id=(px, py, c), device_id_type=MESH)
                for j, (px, py) in enumerate(chips) for a in range(na)]

        @pl.when(i == 0)
        def _():
            dnw_ref[...] = jnp.zeros_like(dnw_ref)
            for cp in copies():
                cp.start()

        du = jnp.zeros((tm, D), f32)
        for d_ref, off, wd in zip(d_refs, offs, widths):
            du = du + bdot(d_ref[...], w_ref[off:off + wd, :])
        _, vjp = jax.vjp(rmsnorm, h_ref[...], nw_ref[...])
        dh, dnw = vjp(du)
        dh_ref[...] = dh + dhr_ref[...]
        dnw_ref[...] += jnp.broadcast_to(dnw, (8, D))

        @pl.when(i == nsteps - 1)
        def _():
            for cp in copies():
                cp.wait()

    row = lambda w: pl.BlockSpec((tm, w), lambda i: (i, 0))
    return pl.pallas_call(
        body, name="in_bwd", grid=(nsteps,),
        in_specs=[row(w) for w in widths]
        + [pl.BlockSpec((PC, D), lambda i: (0, 0)), row(D), pl.BlockSpec((1, D), lambda i: (0, 0)), row(D)] + [ANY] * na,
        out_specs=[row(D), pl.BlockSpec((8, D), lambda i: (0, 0))] + [ANY] * na,
        out_shape=[jax.ShapeDtypeStruct((N, D), f32), jax.ShapeDtypeStruct((8, D), f32)]
        + [jax.ShapeDtypeStruct((3,) + p.shape[1:], p.dtype) for p in pbs],
        scratch_shapes=[pltpu.SemaphoreType.DMA((3 * na,)), pltpu.SemaphoreType.DMA((3 * na,))] if na else [],
        compiler_params=_cp(("arbitrary",)),
    )(*[d for d, _ in pieces], wbig, hflat, norm_w, dh_res, *pbs)


def _w_grad(ut, pieces):
    tk = N // 3
    offs = [c0 for _, c0 in pieces]
    widths = [d.shape[1] for d, _ in pieces]

    def body(u_ref, *refs):
        d_refs, o_ref = refs[:-1], refs[-1]

        @pl.when(pl.program_id(0) == 0)
        def _():
            o_ref[...] = jnp.zeros_like(o_ref)

        u = u_ref[...]
        for d_ref, off, wd in zip(d_refs, offs, widths):
            o_ref[:, off:off + wd] += jnp.dot(u, d_ref[...], preferred_element_type=f32)

    return pl.pallas_call(
        body, name="w_grad", grid=(N // tk,),
        in_specs=[pl.BlockSpec((D, tk), lambda k: (0, k))] + [pl.BlockSpec((tk, w), lambda k: (k, 0)) for w in widths],
        out_specs=pl.BlockSpec((D, PC), lambda k: (0, 0), pipeline_mode=pl.Buffered(1)),
        out_shape=jax.ShapeDtypeStruct((D, PC), f32),
        compiler_params=_cp(("arbitrary",)),
    )(ut, *[d for d, _ in pieces])


def _adam_math(g, w, m, v):
    m2 = ADAM_B1 * m + (1.0 - ADAM_B1) * g
    v2 = ADAM_B2 * v + (1.0 - ADAM_B2) * (g * g)
    m_hat = m2 / (1.0 - ADAM_B1 ** ADAM_STEP)
    v_hat = v2 / (1.0 - ADAM_B2 ** ADAM_STEP)
    delta = -ADAM_LR * (m_hat / (jnp.sqrt(v_hat) + ADAM_EPS) + ADAM_WD * w)
    return delta, m2, v2


def _adamw(gs, w, m, v, name):
    R, Cc = w.shape
    tr = 256 if R % 256 == 0 else R
    ng = len(gs)

    def body(*refs):
        g = refs[0][...]
        for r in refs[1:ng]:
            g = g + r[...]
        w_ref, m_ref, v_ref, g_ref, d_ref, m2_ref, v2_ref = refs[ng:]
        delta, m2, v2 = _adam_math(g, w_ref[...], m_ref[...], v_ref[...])
        g_ref[...] = g
        d_ref[...] = delta
        m2_ref[...] = m2
        v2_ref[...] = v2

    spec = pl.BlockSpec((tr, Cc), lambda i: (i, 0))
    return pl.pallas_call(
        body, name=name, grid=(R // tr,),
        in_specs=[spec] * (ng + 3), out_specs=[spec] * 4,
        out_shape=[jax.ShapeDtypeStruct((R, Cc), f32)] * 4,
        compiler_params=_cp(("arbitrary",)),
    )(*gs, w, m, v)


def _adamw_rows(g, w, m, v, name):
    R, _, Cc = w.shape
    tr = R // 9

    def body(g_ref, w_ref, m_ref, v_ref, go_ref, d_ref, m2_ref, v2_ref):
        g = g_ref[...]
        delta, m2, v2 = _adam_math(g, w_ref[...], m_ref[...], v_ref[...])
        go_ref[...] = g
        d_ref[...] = delta
        m2_ref[...] = m2
        v2_ref[...] = v2

    spec = pl.BlockSpec((tr, 1, Cc), lambda i: (i, 0, 0))
    return pl.pallas_call(
        body, name=name, grid=(R // tr,),
        in_specs=[spec] * 4, out_specs=[spec] * 4,
        out_shape=[jax.ShapeDtypeStruct((R, 1, Cc), f32)] * 4,
        compiler_params=_cp(("arbitrary",)),
    )(g, w, m, v)


def _place():
    x, y, c = lax.axis_index("x"), lax.axis_index("y"), lax.axis_index("c")
    return x, y, c, [(1 - x, y), (x, 1 - y), (1 - x, 1 - y)]


def _gather_weights(cm, halved, whole):
    R, _, Cc = cm.shape
    hw = Cc // 2
    shards = [jax.ShapeDtypeStruct((R, Cc), bf16)] + list(halved) + list(whole)
    nh = 1 + len(halved)
    na = len(shards)

    def body(*refs):
        srcs, dsts = refs[:na], refs[na:2 * na]
        send_sems, recv_sems, loc_sems = refs[2 * na:2 * na + 3]
        stage = refs[2 * na + 3:3 * na + 3]
        raw = refs[3 * na + 3]
        x, y, c, chips = _place()
        me = 2 * x + y
        loads = [pltpu.make_async_copy(srcs[0], raw, loc_sems.at[0])]
        loads += [pltpu.make_async_copy(srcs[i], stage[i], loc_sems.at[i]) for i in range(1, na)]
        locs = [pltpu.make_async_copy(v, d.at[me], loc_sems.at[i]) for i, (v, d) in enumerate(zip(stage, dsts))]
        for cp in loads:
            cp.start()

        def half_of(ref, i, half):
            return ref.at[:, pl.ds(pl.multiple_of(half * hw, hw), hw)] if i == 0 else ref.at[half]

        def ici(j, i, slot):
            px, py = chips[j]
            src = half_of(stage[0] if i == 0 else srcs[i], i, c) if i < nh else srcs[i]
            dst = half_of(dsts[i].at[slot], i, c) if i < nh else dsts[i].at[slot]
            return pltpu.make_async_remote_copy(
                src_ref=src, dst_ref=dst, send_sem=send_sems.at[na * j + i], recv_sem=recv_sems.at[na * j + i],
                device_id=(px, py, c), device_id_type=MESH)

        def d2d(j, i, half):
            px, py = chips[j]
            blk = half_of(dsts[i].at[2 * px + py], i, half)
            return pltpu.make_async_remote_copy(
                src_ref=blk, dst_ref=blk, send_sem=send_sems.at[3 * na + nh * j + i],
                recv_sem=recv_sems.at[3 * na + nh * j + i], device_id=(x, y, 1 - c), device_id_type=MESH)

        sends = [ici(j, i, me) for j in range(3) for i in range(1, na)]
        for cp in sends:
            cp.start()
        loads[0].wait()
        stage[0][...] = raw[:, 0, :].astype(bf16)
        first = [ici(j, 0, me) for j in range(3)]
        for cp in first:
            cp.start()
        sends += first
        locs[0].start()
        for ld, st in zip(loads[1:], locs[1:]):
            ld.wait()
            st.start()
        for j, (px, py) in enumerate(chips):
            for i in range(na):
                ici(j, i, 2 * px + py).wait_recv()
                if i < nh:
                    fwd = d2d(j, i, c)
                    fwd.start()
                    sends.append(fwd)
        for j in range(3):
            for i in range(nh):
                d2d(j, i, 1 - c).wait_recv()
        for cp in sends:
            cp.wait_send()
        for cp in locs:
            cp.wait()

    nsem = 3 * na + 3 * nh
    return pl.pallas_call(
        body, name="gather_weights",
        in_specs=[ANY] * na, out_specs=[ANY] * na,
        out_shape=[jax.ShapeDtypeStruct((4,) + s.shape, s.dtype) for s in shards],
        scratch_shapes=[pltpu.SemaphoreType.DMA((nsem,)), pltpu.SemaphoreType.DMA((nsem,)),
                        pltpu.SemaphoreType.DMA((na,))] + [pltpu.VMEM(s.shape, s.dtype) for s in shards]
        + [pltpu.VMEM(cm.shape, cm.dtype)],
        compiler_params=pltpu.CompilerParams(has_side_effects=True, vmem_limit_bytes=VMEM_LIMIT),
    )(cm, *halved, *whole)


def _swap_halves(gs):
    na = len(gs)
    jobs = [(i, q) for i in range(na) for q in range(gs[i].shape[0])]

    def body(*refs):
        srcs, dsts = refs[:na], refs[na:2 * na]
        send_sems, recv_sems = refs[2 * na:]
        x, y, c, _ = _place()
        cps = [pltpu.make_async_remote_copy(
            src_ref=srcs[i].at[q, 1 - c], dst_ref=dsts[i].at[q], send_sem=send_sems.at[k],
            recv_sem=recv_sems.at[k], device_id=(x, y, 1 - c), device_id_type=MESH)
            for k, (i, q) in enumerate(jobs)]
        for cp in cps:
            cp.start()
        for cp in cps:
            cp.wait()

    return pl.pallas_call(
        body, name="swap_halves",
        in_specs=[ANY] * na, out_specs=[ANY] * na,
        out_shape=[jax.ShapeDtypeStruct(g.shape[0:1] + g.shape[2:], g.dtype) for g in gs],
        scratch_shapes=[pltpu.SemaphoreType.DMA((len(jobs),)), pltpu.SemaphoreType.DMA((len(jobs),))],
        compiler_params=pltpu.CompilerParams(has_side_effects=True),
    )(*gs)


def _add_split(cm_arr, g, s):
    _, _, R, Cg = g.shape
    tr = 128

    def body(sc_ref, g_ref, s_ref, b_ref, o_ref):
        p = g_ref[0, 0] + s_ref[0]
        own = None
        for q in range(4):
            blk = p[:, SHARD_COLS * q:SHARD_COLS * (q + 1)]
            b_ref[q] = blk.astype(bf16)
            mine = jnp.where(sc_ref[1] == q, blk, 0.0)
            own = mine if own is None else own + mine
        o_ref[...] = own

    return pl.pallas_call(
        body, name="add_w_in",
        grid_spec=pltpu.PrefetchScalarGridSpec(
            num_scalar_prefetch=1, grid=(R // tr,),
            in_specs=[pl.BlockSpec((1, 1, tr, Cg), lambda i, sc: (0, sc[0], i, 0)),
                      pl.BlockSpec((1, tr, Cg), lambda i, sc: (0, i, 0))],
            out_specs=[pl.BlockSpec((4, tr, SHARD_COLS), lambda i, sc: (0, i, 0)),
                       pl.BlockSpec((tr, SHARD_COLS), lambda i, sc: (i, 0))]),
        out_shape=[jax.ShapeDtypeStruct((4, R, SHARD_COLS), bf16), jax.ShapeDtypeStruct((R, SHARD_COLS), f32)],
        compiler_params=_cp(("arbitrary",)),
    )(cm_arr, g, s)


def _add_halves(c_arr, g, s, name):
    Q, _, R, Cc = g.shape
    tr = min(R, 128)

    def body(c_ref, g_ref, s_ref, b_ref, f_ref):
        p = g_ref[0, 0] + s_ref[0]
        f_ref[0] = p
        b_ref[0] = p.astype(bf16)

    blk = pl.BlockSpec((1, tr, Cc), lambda q, i, cr: (q, i, 0))
    return pl.pallas_call(
        body, name=name,
        grid_spec=pltpu.PrefetchScalarGridSpec(
            num_scalar_prefetch=1, grid=(Q, R // tr),
            in_specs=[pl.BlockSpec((1, 1, tr, Cc), lambda q, i, cr: (q, cr[0], i, 0)), blk], out_specs=[blk, blk]),
        out_shape=[jax.ShapeDtypeStruct((Q, R, Cc), bf16), jax.ShapeDtypeStruct((Q, R, Cc), f32)],
        compiler_params=_cp(("arbitrary", "arbitrary")),
    )(c_arr, g, s)


_FLIPS = [(fx, fy, fc) for fx in (0, 1) for fy in (0, 1) for fc in (0, 1)][1:]


def _sum_blocks(own, r, name):
    R, Cc = own.shape
    tr = min(R, 256)

    def body(own_ref, r_ref, o_ref):
        acc = own_ref[...]
        for j in range(3):
            acc = acc + r_ref[j].astype(f32)
        o_ref[...] = acc

    return pl.pallas_call(
        body, name=name, grid=(R // tr,),
        in_specs=[pl.BlockSpec((tr, Cc), lambda i: (i, 0)), pl.BlockSpec((3, tr, Cc), lambda i: (0, i, 0))],
        out_specs=pl.BlockSpec((tr, Cc), lambda i: (i, 0)),
        out_shape=jax.ShapeDtypeStruct((R, Cc), f32),
        compiler_params=_cp(("arbitrary",)),
    )(own, r)


def _sum_packs(me8_arr, pack, rp):
    R = pack.shape[0]

    def body(me_ref, pk_ref, rp_ref, o_ref):
        me8 = me_ref[0]
        acc = None
        for d in range(8):
            rel = d ^ me8
            term = jnp.where(rel == 0, pk_ref[...], rp_ref[jnp.maximum(rel - 1, 0)])
            acc = term if acc is None else acc + term
        o_ref[...] = acc

    return pl.pallas_call(
        body, name="sum_packs",
        grid_spec=pltpu.PrefetchScalarGridSpec(
            num_scalar_prefetch=1, grid=(1,),
            in_specs=[pl.BlockSpec((R, 128), lambda i, mr: (0, 0)), pl.BlockSpec((7, R, 128), lambda i, mr: (0, 0, 0))],
            out_specs=pl.BlockSpec((R, 128), lambda i, mr: (0, 0))),
        out_shape=jax.ShapeDtypeStruct((R, 128), f32),
        compiler_params=_cp(("arbitrary",)),
    )(me8_arr, pack, rp)


def _swap_finished(fs, pack):
    na = len(fs)
    R = pack.shape[0]

    def body(*refs):
        srcs, pk = refs[:na], refs[na]
        dsts, rp = refs[na + 1:2 * na + 1], refs[2 * na + 1]
        send_sems, recv_sems = refs[2 * na + 2:]
        x, y, c, _ = _place()
        cps = [pltpu.make_async_remote_copy(
            src_ref=srcs[i], dst_ref=dsts[i], send_sem=send_sems.at[i], recv_sem=recv_sems.at[i],
            device_id=(x, y, 1 - c), device_id_type=MESH) for i in range(na)]
        cps += [pltpu.make_async_remote_copy(
            src_ref=pk, dst_ref=rp.at[k], send_sem=send_sems.at[na + k], recv_sem=recv_sems.at[na + k],
            device_id=(x ^ fx, y ^ fy, c ^ fc), device_id_type=MESH) for k, (fx, fy, fc) in enumerate(_FLIPS)]
        for cp in cps:
            cp.start()
        for cp in cps:
            cp.wait()

    return pl.pallas_call(
        body, name="swap_finished",
        in_specs=[ANY] * (na + 1), out_specs=[ANY] * (na + 1),
        out_shape=[jax.ShapeDtypeStruct(f.shape, f.dtype) for f in fs] + [jax.ShapeDtypeStruct((7, R, 128), f32)],
        scratch_shapes=[pltpu.SemaphoreType.DMA((na + 7,)), pltpu.SemaphoreType.DMA((na + 7,))],
        compiler_params=pltpu.CompilerParams(has_side_effects=True),
    )(*fs, pack)


def _adamw_halves(c_arr, mine, peer, w, m, v, name):
    _, R, Cc = w.shape
    tr = min(R, 256)

    def body(c_ref, mine_ref, peer_ref, w_ref, m_ref, v_ref, g_ref, d_ref, m2_ref, v2_ref):
        g = jnp.where(pl.program_id(0) == c_ref[0], mine_ref[...], peer_ref[...])
        delta, m2, v2 = _adam_math(g, w_ref[0], m_ref[0], v_ref[0])
        g_ref[0] = g
        d_ref[0] = delta
        m2_ref[0] = m2
        v2_ref[0] = v2

    half = pl.BlockSpec((tr, Cc), lambda hh, i, cr: (i, 0))
    full = pl.BlockSpec((1, tr, Cc), lambda hh, i, cr: (hh, i, 0))
    return pl.pallas_call(
        body, name=name,
        grid_spec=pltpu.PrefetchScalarGridSpec(
            num_scalar_prefetch=1, grid=(2, R // tr), in_specs=[half, half, full, full, full], out_specs=[full] * 4),
        out_shape=[jax.ShapeDtypeStruct((2, R, Cc), f32)] * 4,
        compiler_params=_cp(("arbitrary", "arbitrary")),
    )(c_arr, mine, peer, w, m, v)


def _rows8(a):
    flat = a.reshape(-1)
    n = flat.shape[0]
    rows = -(-n // 1024) * 8
    return jnp.pad(flat, (0, rows * 128 - n)).reshape(rows, 128)


def kernel(x, meta_tokens, norm_w, w_in, conv_w, hg_lb_logits, hg_norm_w, gdn_A_log, gdn_dt_bias, gdn_norm_w, w_out, final_norm_w, loss_target, m_meta_tokens, m_norm_w, m_w_in, m_conv_w, m_hg_lb_logits, m_hg_norm_w, m_gdn_A_log, m_gdn_dt_bias, m_gdn_norm_w, m_w_out, m_final_norm_w, v_meta_tokens, v_norm_w, v_w_in, v_conv_w, v_hg_lb_logits, v_hg_norm_w, v_gdn_A_log, v_gdn_dt_bias, v_gdn_norm_w, v_w_out, v_final_norm_w):
    me = 2 * lax.axis_index("x") + lax.axis_index("y")

    g_win, g_wout, g_conv, g_meta = _gather_weights(
        jnp.transpose(w_in, (2, 0, 1)), [w_out[0].astype(bf16).reshape(2, D // 8, D)], [conv_w[0], meta_tokens])
    wbig = jnp.pad(g_win.reshape(IN_COLS, D), ((0, PC - IN_COLS), (0, 0)))
    wout_full = g_wout.reshape(D, D)
    conv4 = jnp.transpose(g_conv, (1, 0, 2)).reshape(4, 1, 3 * HD)
    meta_full = jnp.transpose(g_meta, (1, 0, 2)).reshape(N_META, D)

    c_arr = lax.axis_index("c").reshape(1).astype(jnp.int32)

    def chip_partials(gw, g_wout_part):
        g_in2 = gw.reshape(1, 2, D // 2, PC)
        g_out4 = g_wout_part.reshape(4, 2, D // 8, D)
        s_in, s_out = _swap_halves([g_in2, g_out4])
        pb_blocks, own_in = _add_split(jnp.concatenate([c_arr, me.reshape(1).astype(jnp.int32)]), g_in2, s_in)
        pb_out, pf_out = _add_halves(c_arr, g_out4, s_out, "add_w_out")
        own_out = lax.dynamic_index_in_dim(pf_out, me, axis=0, keepdims=False)
        return [pb_blocks, pb_out], [own_in, own_out]

    (loss8, grad_x, d_meta, d_nw, d_conv, d_lb, d_hgw, d_alog, d_dtb, d_gdw, d_fw, pfs, rs) = _local_step(
        x, loss_target, wbig, wout_full, conv4, meta_full, norm_w, hg_lb_logits, hg_norm_w, gdn_A_log, gdn_dt_bias,
        gdn_norm_w, final_norm_w, chip_partials)

    pack = jnp.concatenate([
        loss8, d_nw[0].reshape(8, 128), d_lb.reshape(8, 128), d_hgw, _rows8(d_alog[0, :H]), _rows8(d_dtb[0, :H]),
        d_gdw, d_fw[0].reshape(8, 128), d_meta.reshape(128, 128), d_conv.reshape(48, 128)], axis=0)
    return _reduce_and_update(
        me, c_arr, grad_x, pfs, rs, pack, meta_tokens, norm_w, w_in, conv_w, hg_lb_logits, hg_norm_w, gdn_A_log,
        gdn_dt_bias, gdn_norm_w, w_out, final_norm_w, m_meta_tokens, m_norm_w, m_w_in, m_conv_w, m_hg_lb_logits,
        m_hg_norm_w, m_gdn_A_log, m_gdn_dt_bias, m_gdn_norm_w, m_w_out, m_final_norm_w, v_meta_tokens, v_norm_w, v_w_in,
        v_conv_w, v_hg_lb_logits, v_hg_norm_w, v_gdn_A_log, v_gdn_dt_bias, v_gdn_norm_w, v_w_out, v_final_norm_w)


def _local_step(x, loss_target, wbig, wout_full, conv4, meta_full, norm_w, hg_lb_logits, hg_norm_w, gdn_A_log, gdn_dt_bias,
                gdn_norm_w, final_norm_w, chip_partials):
    head = jnp.concatenate([jnp.zeros((PAD, D), f32), meta_full], axis=0)
    target = loss_target.reshape(NB * SEQ, D)
    l0, l1 = hg_lb_logits[0:1], hg_lb_logits[1:2]
    alog = jnp.pad(gdn_A_log, ((0, 0), (0, DK - H)))
    dtb = jnp.pad(gdn_dt_bias, ((0, 0), (0, DK - H)))
    fw = final_norm_w.reshape(1, D)

    hflat, proj, ut, cv2 = _in_proj(x.reshape(NB * SEQ, D), head, norm_w, wbig, conv4)
    proj3 = proj.reshape(NB, TP, PC)
    cv = cv2.reshape(NB, TP, 3 * HD)
    o_hg, s_hg, o_gd, s_gd, t_gd = _mix_fwd(proj3, cv, l0, l1, alog, dtb)
    (loss8, d_ohg, d_ogd, d_zhg, d_zgd, dh_res, g_wout_part, d_hgw, d_gdw, d_fw) = _out_loss(
        o_hg.reshape(N, HD), o_gd.reshape(N, HD), proj, hg_norm_w, gdn_norm_w, wout_full, hflat, fw, target)
    d_hg, d_l0, d_l1 = _hg_bwd(proj3, l0, l1, s_hg, d_ohg.reshape(NB, TP, HD))
    d_cv, d_ab, d_alog, d_dtb = _gd_bwd(cv, proj3, alog, dtb, s_gd, t_gd, d_ogd.reshape(NB, TP, HD))
    d_qkv, d_conv4 = _conv_bwd(proj3, conv4, d_cv)
    d_hg2, d_qkv2, d_ab2 = d_hg.reshape(N, 3 * HD), d_qkv.reshape(N, 3 * HD), d_ab.reshape(N, DK)
    pieces = [(d_hg2, COL_HG), (d_zhg, COL_ZHG), (d_qkv2, COL_QKV), (d_zgd, COL_ZGD), (d_ab2, COL_AB)]
    gw = _w_grad(ut, pieces)
    pbs, pfs = chip_partials(gw, g_wout_part) if chip_partials else ([], [gw, g_wout_part])
    dh, d_nw, *rs = _in_bwd(pieces, wbig, hflat, norm_w, dh_res, pbs)

    dh3 = dh.reshape(NB, TP, D)
    grad_x = dh3[:, PAD + N_META:, :]
    d_meta = jnp.sum(dh3[:, PAD:PAD + N_META, :], axis=0)
    d_conv = d_conv4[:, 0, :]
    d_lb = jnp.concatenate([d_l0[0:1], d_l1[0:1]], axis=0)
    return loss8, grad_x, d_meta, d_nw, d_conv, d_lb, d_hgw, d_alog, d_dtb, d_gdw, d_fw, pfs, rs


def _reduce_and_update(me, c_arr, grad_x, pfs, rs, pack, meta_tokens, norm_w, w_in, conv_w, hg_lb_logits, hg_norm_w,
                       gdn_A_log, gdn_dt_bias, gdn_norm_w, w_out, final_norm_w, m_meta_tokens, m_norm_w, m_w_in, m_conv_w,
                       m_hg_lb_logits, m_hg_norm_w, m_gdn_A_log, m_gdn_dt_bias, m_gdn_norm_w, m_w_out, m_final_norm_w,
                       v_meta_tokens, v_norm_w, v_w_in, v_conv_w, v_hg_lb_logits, v_hg_norm_w, v_gdn_A_log, v_gdn_dt_bias,
                       v_gdn_norm_w, v_w_out, v_final_norm_w):
    (own_in, own_out), (r_in, r_out) = pfs, rs
    f_in = _sum_blocks(own_in, r_in, "sum_w_in")
    f_out = _sum_blocks(own_out, r_out, "sum_w_out")
    o_in, o_out, r_pack = _swap_finished([f_in, f_out], pack)
    me8_arr = (2 * me + lax.axis_index("c")).reshape(1).astype(jnp.int32)
    small = _sum_packs(me8_arr, pack, r_pack)

    half_out = lambda a: a[0].reshape(2, D // 8, D)
    is0 = lax.axis_index("c") == 0
    g_in = jnp.concatenate([jnp.where(is0, f_in, o_in), jnp.where(is0, o_in, f_in)], axis=0)
    to_cm = lambda a: jnp.transpose(a, (2, 0, 1))
    gi, di, mi, vi = [jnp.transpose(a, (1, 2, 0))[0] for a in _adamw_rows(
        g_in.T.reshape(SHARD_COLS, 1, D), to_cm(w_in), to_cm(m_w_in), to_cm(v_w_in), "adamw_w_in")]
    go, do_, mo, vo = [a.reshape(D // 4, D) for a in _adamw_halves(
        c_arr, f_out, o_out, half_out(w_out), half_out(m_w_out), half_out(v_w_out), "adamw_w_out")]

    g_meta_full = small[64:192].reshape(N_META, D)
    g_meta_loc = lax.dynamic_slice(g_meta_full, (0, me * 256), (N_META, 256))
    gm, dm, mm_, vm = _adamw([g_meta_loc], meta_tokens, m_meta_tokens, v_meta_tokens, "adamw_meta")
    g_conv_full = small[192:240].reshape(4, 1536)
    g_conv_loc = lax.dynamic_slice(g_conv_full, (0, me * 384), (4, 384))
    gc, dc, mc, vc = _adamw([g_conv_loc], conv_w[0], m_conv_w[0], v_conv_w[0], "adamw_conv")

    reps = [(norm_w, m_norm_w, v_norm_w), (hg_lb_logits, m_hg_lb_logits, v_hg_lb_logits),
            (hg_norm_w, m_hg_norm_w, v_hg_norm_w), (gdn_A_log, m_gdn_A_log, v_gdn_A_log),
            (gdn_dt_bias, m_gdn_dt_bias, v_gdn_dt_bias), (gdn_norm_w, m_gdn_norm_w, v_gdn_norm_w),
            (final_norm_w, m_final_norm_w, v_final_norm_w)]
    wp = jnp.concatenate([_rows8(t[0]) for t in reps], axis=0)
    mp = jnp.concatenate([_rows8(t[1]) for t in reps], axis=0)
    vp = jnp.concatenate([_rows8(t[2]) for t in reps], axis=0)
    gr, dr, mr, vr = _adamw([small[8:64]], wp, mp, vp, "adamw_small")

    def unpack(p):
        outs = []
        for i, t in enumerate(reps):
            n = t[0].size
            outs.append(p[8 * i:8 * i + 8].reshape(-1)[:n].reshape(t[0].shape))
        return outs

    def leaves(meta_v, conv_v, in_v, out_v, rep_p):
        nw, lb, hgw, al, db, gdw, fwv = unpack(rep_p)
        return [meta_v, nw, in_v[None], conv_v[None], lb, hgw, al, db, gdw, out_v[None], fwv]

    loss = small[0, 0]
    return (loss, grad_x, *leaves(gm, gc, gi, go, gr), *leaves(dm, dc, di, do_, dr),
            *leaves(mm_, mc, mi, mo, mr), *leaves(vm, vc, vi, vo, vr))
```

```python
import functools

import jax
import jax.numpy as jnp
from jax import lax
from jax.experimental import pallas as pl
from jax.experimental.pallas import tpu as pltpu

f32 = jnp.float32
bf16 = jnp.bfloat16
MESH = pl.DeviceIdType.MESH
ANY = pl.BlockSpec(memory_space=pl.ANY)

D = 1024
NB = 2
N_META = 16
SEQ = 2048
PAD = 48
TP = PAD + N_META + SEQ
C = 64
NCH = TP // C
N = NB * TP
H = 4
DK = 128
HD = H * DK
PC = 4224
IN_COLS = 4104
SHARD_COLS = IN_COLS // 4
COL_HG, COL_ZHG, COL_QKV, COL_ZGD, COL_AB = 0, 3 * HD, 4 * HD, 7 * HD, 8 * HD
EPS = 1e-6
ADAM_LR, ADAM_B1, ADAM_B2, ADAM_EPS, ADAM_WD, ADAM_STEP = 0.001, 0.9, 0.999, 1e-08, 0.01, 10
VMEM_LIMIT = 56 * 1024 * 1024

P_HG = dict(lvl=1, av=1, qs=1, su=1)
P_GD = dict(kk=1, inv=1, sol=1, ws=1, qk=1, o=1, su=1)


def _cp(sem=None, **kw):
    return pltpu.CompilerParams(dimension_semantics=sem, vmem_limit_bytes=VMEM_LIMIT, **kw)


_DIMS = {"nn": (((1,), (0,)), ((), ())), "nt": (((1,), (1,)), ((), ())), "tn": (((0,), (0,)), ((), ()))}


def _split(x):
    hi = x.astype(bf16)
    return hi, (x - hi.astype(f32)).astype(bf16)


def _dg(a, b, kind, passes):
    d = lambda x, y: lax.dot_general(x, y, _DIMS[kind], preferred_element_type=f32)
    if passes == 1:
        return d(a.astype(bf16), b.astype(bf16))
    ah, al = _split(a)
    bh, bl = _split(b)
    return d(ah, bh) + d(ah, bl) + d(al, bh)


@functools.partial(jax.custom_vjp, nondiff_argnums=(2, 3))
def mmx(a, b, kind, passes):
    return _dg(a, b, kind, passes)


def _mmx_fwd(a, b, kind, passes):
    return _dg(a, b, kind, passes), (a, b)


def _mmx_bwd(kind, passes, res, g):
    a, b = res
    if kind == "nn":
        return _dg(g, b, "nt", passes), _dg(a, g, "tn", passes)
    if kind == "nt":
        return _dg(g, b, "nn", passes), _dg(g, a, "tn", passes)
    return _dg(b, g, "nt", passes), _dg(a, g, "nn", passes)


mmx.defvjp(_mmx_fwd, _mmx_bwd)


def _mask_dg(mask, x):
    xh, xl = _split(x)
    return jnp.dot(jnp.concatenate([mask, mask], axis=1), jnp.concatenate([xh, xl], axis=0), preferred_element_type=f32)


@functools.partial(jax.custom_vjp, nondiff_argnums=(2,))
def mask_mm(mask, x, bwd_passes):
    return _mask_dg(mask, x)


def _mask_fwd(mask, x, bwd_passes):
    return _mask_dg(mask, x), mask


def _mask_bwd(bwd_passes, mask, g):
    d = lambda y: lax.dot_general(mask, y, _DIMS["tn"], preferred_element_type=f32)
    if bwd_passes == 1:
        return None, d(g.astype(bf16))
    gh, gl = _split(g)
    return None, d(gh) + d(gl)


mask_mm.defvjp(_mask_fwd, _mask_bwd)


def bdot(a, b):
    return jnp.dot(a.astype(bf16), b.astype(bf16), preferred_element_type=f32)


def bdot_nt(a, b):
    return lax.dot_general(a.astype(bf16), b.astype(bf16), _DIMS["nt"], preferred_element_type=f32)


def bdot_tn(a, b):
    return lax.dot_general(a.astype(bf16), b.astype(bf16), _DIMS["tn"], preferred_element_type=f32)


def _iota2(n, m):
    return lax.broadcasted_iota(jnp.int32, (n, m), 0), lax.broadcasted_iota(jnp.int32, (n, m), 1)


sigmoid = jax.nn.sigmoid


def silu(x):
    return x * sigmoid(x)


def softplus(x):
    return jnp.maximum(x, 0.0) + jnp.log(1.0 + jnp.exp(-jnp.abs(x)))


def rmsnorm(x, w):
    return x * lax.rsqrt(jnp.mean(x * x, axis=-1, keepdims=True) + EPS) * w


def hg_masks():
    t, r = _iota2(C, C)
    mats = [r <= t, r > t]
    lvl = []
    for l in range(1, 7):
        sz = 1 << l
        half = sz >> 1
        seg_t = t >> l
        upper_t = (t & (sz - 1)) >= half
        mid_t = seg_t * sz + half - 1
        mats.append((upper_t & (r > mid_t) & (r <= t)) | ((~upper_t) & (r > t) & (r <= mid_t)))
        lvl.append(((seg_t == (r >> l)) & upper_t & ((r & (sz - 1)) < half)).astype(f32))
    stk = jnp.concatenate([m.astype(bf16) for m in mats], axis=0)
    return stk, lvl, (t == r).astype(f32)


def _head(a, h):
    return a[:, h * DK:(h + 1) * DK]


def _run(*gens):
    results = [None] * len(gens)
    live = list(range(len(gens)))
    while live:
        for i in list(live):
            try:
                next(gens[i])
            except StopIteration as e:
                results[i] = e.value
                live.remove(i)
    return results


def hg_chunk(St, ps, l0, l1):
    return _run(hg_stages(St, ps, l0, l1))[0]


def gd_chunk(S, cs, abs_, alog, dtb, t_saved=None):
    return _run(gd_stages(S, cs, abs_, alog, dtb, t_saved))[0]


def mix_chunk(St, ps, l0, l1, S, cs, abs_, alog, dtb):
    (sn_h, o_h), (sn_g, o_g, t_pack) = _run(hg_stages(St, ps, l0, l1), gd_stages(S, cs, abs_, alog, dtb))
    return sn_h, o_h, sn_g, o_g, t_pack


def hg_stages(St, ps, l0, l1):
    m = jnp.maximum(l0, l1)
    e0 = jnp.exp(l0 - m)
    e1 = jnp.exp(l1 - m)
    lb = e0 / (e0 + e1)
    stk, lvl, eye = hg_masks()
    msk = [eye] + lvl
    trow = lax.broadcasted_iota(jnp.int32, (C, 1), 0)
    upper = [(trow & ((1 << l) - 1)) >= (1 << (l - 1)) for l in range(1, 7)]
    qs, ks, vs, qG, kR, eGl = [], [], [], [], [], []
    for p in ps:
        pq, pf, v = p[:, 0:HD], p[:, HD:2 * HD], p[:, 2 * HD:3 * HD]
        q = silu(pq)
        f = lb + (1.0 - lb) * sigmoid(pf)
        k = 1.0 - f
        logf = jnp.log(f)
        Dm = mask_mm(stk, logf, 1)
        z = [jnp.where(up, q, k) * jnp.exp(Dm[(2 + i) * C:(3 + i) * C]) for i, up in enumerate(upper)]
        qs.append([q] + z)
        ks.append([k] + z)
        vs.append(v)
        qG.append(q * jnp.exp(Dm[0:C]))
        kR.append(k * jnp.exp(Dm[C:2 * C]))
        eGl.append(jnp.exp(jnp.sum(logf, axis=0, keepdims=True)))
    yield
    units = [(b, h) for b in range(len(ps)) for h in range(H)]
    parts = []
    for i in range(7):
        parts.append([msk[i] * mmx(_head(qs[b][i], h), _head(ks[b][i], h), "nt", P_HG["lvl"]) for b, h in units])
        yield
    A = [functools.reduce(lambda x, y: x + y, [parts[i][n] for i in range(7)]) for n in range(len(units))]
    qS = [mmx(_head(qG[b], h), St[n], "nt", P_HG["qs"]) for n, (b, h) in enumerate(units)]
    Sn = [St[n] * _head(eGl[b], h) + mmx(_head(vs[b], h), _head(kR[b], h), "tn", P_HG["su"])
          for n, (b, h) in enumerate(units)]
    yield
    outs = [mmx(A[n], _head(vs[b], h), "nn", P_HG["av"]) + qS[n] for n, (b, h) in enumerate(units)]
    return tuple(Sn), tuple(jnp.concatenate(outs[b * H:(b + 1) * H], axis=1) for b in range(len(ps)))


@jax.custom_vjp
def use_inverse(A, T):
    return T


def _use_inverse_fwd(A, T):
    return T, T


def _use_inverse_bwd(T, g):
    return -_dg(T, _dg(g, T, "nt", P_GD["inv"]), "tn", P_GD["inv"]), jnp.zeros_like(T)


use_inverse.defvjp(_use_inverse_fwd, _use_inverse_bwd)


def gd_stages(S, cs, abs_, alog, dtb, t_saved=None):
    t, r = _iota2(C, C)
    tri = (r <= t).astype(bf16)
    ups = (r > t).astype(bf16)
    lane = lax.broadcasted_iota(jnp.int32, (1, DK), 1)
    subl = lax.broadcasted_iota(jnp.int32, (8, 1), 0)
    eye = (t == r).astype(f32)
    strict = (r < t).astype(f32)
    bd = ((t >> 4) == (r >> 4)).astype(f32)
    qa, ka, va, b4, gam4, grev4, gam4T, glast4 = [], [], [], [], [], [], [], []
    for c, ab in zip(cs, abs_):
        qa.append(silu(c[:, 0:HD]))
        ka.append(silu(c[:, HD:2 * HD]))
        va.append(silu(c[:, 2 * HD:3 * HD]))
        g4 = -jnp.exp(alog) * softplus(ab + dtb)
        b4.append(sigmoid(ab))
        gam4.append(mask_mm(tri, g4, 2))
        grev4.append(mask_mm(ups, g4, 2))
        gam4T.append(gam4[-1].T)
        glast4.append(jnp.sum(g4, axis=0, keepdims=True))
    yield
    units = [(b, h) for b in range(len(cs)) for h in range(H)]
    nu = range(len(units))
    inv = lambda a, b: [mmx(a[n], b[n], "nn", P_GD["inv"]) for n in nu]
    v = [_head(va[b], h) for b, h in units]
    q = [_head(qa[b], h) for b, h in units]
    k = [_head(ka[b], h) for b, h in units]
    q = [x * lax.rsqrt(jnp.sum(x * x, -1, keepdims=True) + EPS) * (DK ** -0.5) for x in q]
    k = [x * lax.rsqrt(jnp.sum(x * x, -1, keepdims=True) + EPS) for x in k]
    oh = [(lane == h).astype(f32) for h in range(H)]
    gam_c = [jnp.sum(gam4[b] * oh[h], -1, keepdims=True) for b, h in units]
    grev_c = [jnp.sum(grev4[b] * oh[h], -1, keepdims=True) for b, h in units]
    beta = [jnp.sum(b4[b] * (lane == h + H).astype(f32), -1, keepdims=True) for b, h in units]
    glast = [jnp.sum(glast4[b] * oh[h], -1, keepdims=True) for b, h in units]
    gam_r = [jnp.sum(gam4T[b][0:8, :] * (subl == h).astype(f32), axis=0, keepdims=True) for b, h in units]
    dec = [jnp.exp(jnp.where(r <= t, gam_c[n] - gam_r[n], -1e30)) for n in nu]
    egam = [jnp.exp(gam_c[n]) for n in nu]
    kk = [mmx(k[n], k[n], "nt", P_GD["kk"]) for n in nu]
    qk = [mmx(q[n], k[n], "nt", P_GD["qk"]) * dec[n] for n in nu]
    yield
    A = [beta[n] * kk[n] * dec[n] * strict for n in nu]
    Dg = [A[n] * bd for n in nu]
    L = [A[n] - Dg[n] for n in nu]
    if t_saved is None:
        ImD = [eye - Dg[n] for n in nu]
        D2 = inv(Dg, Dg)
        yield
        P1 = inv(ImD, [eye + x for x in D2])
        D4 = inv(D2, D2)
        yield
        P2 = inv(P1, [eye + x for x in D4])
        D8 = inv(D4, D4)
        yield
        M = inv(P2, [eye + x for x in D8])
        yield
        Nn = inv(M, L)
        yield
        N2 = inv(Nn, Nn)
        yield
        T1 = inv([eye - x for x in Nn], [eye + x for x in N2])
        yield
        Tinv = inv(T1, M)
        yield
    else:
        Tinv = [use_inverse(A[n], t_saved[b][:, h * DK:h * DK + C]) for n, (b, h) in enumerate(units)]
    rhs = [jnp.concatenate([beta[n] * v[n], (beta[n] * egam[n]) * k[n]], axis=1) for n in nu]
    sol = [mmx(Tinv[n], rhs[n], "nn", P_GD["sol"]) for n in nu]
    yield
    qwS = [mmx(jnp.concatenate([q[n] * egam[n], sol[n][:, DK:2 * DK]], axis=0), S[n], "nn", P_GD["ws"]) for n in nu]
    yield
    u = [sol[n][:, 0:DK] - qwS[n][C:2 * C] for n in nu]
    outs = [qwS[n][0:C] + mmx(qk[n], u[n], "nn", P_GD["o"]) for n in nu]
    Sn = [jnp.exp(glast[n]) * S[n] + mmx(k[n] * jnp.exp(grev_c[n]), u[n], "tn", P_GD["su"]) for n in nu]
    zpad = jnp.zeros((C, DK - C), f32)
    t_pack = tuple(jnp.concatenate([x for n in range(b * H, (b + 1) * H) for x in (lax.stop_gradient(Tinv[n]), zpad)],
                                   axis=1) for b in range(len(cs)))
    return tuple(Sn), tuple(jnp.concatenate(outs[b * H:(b + 1) * H], axis=1) for b in range(len(cs))), t_pack


def _chunk_index(tile_chunks, k):
    def index(i):
        chunk = tile_chunks * i + k
        b = chunk // NCH
        return jnp.maximum((SEQ // C) * b + chunk - NCH * b - 1, 0), 0
    return index


def _in_proj(xflat, head, norm_w, w4, conv4):
    tm = 384
    nck = tm // C
    W3 = 3 * HD

    def body(*refs):
        x_refs = refs[:nck]
        head_ref, nw_ref, w4_ref, cw_ref, h_ref, p_ref, ut_ref, cv_ref, w_ref, prev = refs[nck:]
        i = pl.program_id(0)

        @pl.when(i == 0)
        def _():
            prev[...] = jnp.zeros_like(prev)
            w_ref[PC - DK:PC, :] = jnp.zeros((DK, D), bf16)
            for q in range(4):
                w_ref[SHARD_COLS * q:SHARD_COLS * (q + 1), :] = w4_ref[q]

        blocks = []
        for k in range(nck):
            chunk = nck * i + k
            blocks.append(jnp.where(chunk - NCH * (chunk // NCH) == 0, head_ref[...], x_refs[k][...]))
        hval = jnp.concatenate(blocks, axis=0)
        h_ref[...] = hval
        u = rmsnorm(hval, nw_ref[...])
        ut_ref[...] = u.T.astype(bf16)
        p = bdot_nt(u, w_ref[...])
        p_ref[...] = p
        x = p[:, COL_QKV:COL_QKV + W3]
        xx = jnp.concatenate([prev[...], x], axis=0)
        y = cw_ref[3] * x
        for s in (1, 2, 3):
            y = y + cw_ref[3 - s] * pltpu.roll(xx, s, 0)[8:]
        row = i * tm + lax.broadcasted_iota(jnp.int32, (tm, 1), 0)
        tok = jnp.where(row >= TP, row - TP, row)
        cv_ref[...] = jnp.where(tok >= 8, y, 0.0)
        prev[...] = x[tm - 8:tm]

    return pl.pallas_call(
        body, name="in_proj", grid=(N // tm,),
        in_specs=[pl.BlockSpec((C, D), _chunk_index(nck, k)) for k in range(nck)]
        + [pl.BlockSpec((C, D), lambda i: (0, 0)), pl.BlockSpec((1, D), lambda i: (0, 0)),
           pl.BlockSpec((4, SHARD_COLS, D), lambda i: (0, 0, 0), pipeline_mode=pl.Buffered(1)),
           pl.BlockSpec((4, 1, W3), lambda i: (0, 0, 0))],
        out_specs=[pl.BlockSpec((tm, D), lambda i: (i, 0)), pl.BlockSpec((tm, PC), lambda i: (i, 0)),
                   pl.BlockSpec((D, tm), lambda i: (0, i)), pl.BlockSpec((tm, W3), lambda i: (i, 0)),
                   pl.BlockSpec((PC, D), lambda i: (0, 0), pipeline_mode=pl.Buffered(1))],
        out_shape=[jax.ShapeDtypeStruct((N, D), f32), jax.ShapeDtypeStruct((N, PC), f32),
                   jax.ShapeDtypeStruct((D, N), bf16), jax.ShapeDtypeStruct((N, W3), f32),
                   jax.ShapeDtypeStruct((PC, D), bf16)],
        scratch_shapes=[pltpu.VMEM((8, W3), f32)],
        compiler_params=_cp(("arbitrary",)),
    )(*[xflat] * nck, head, norm_w, w4, conv4)


NU = NB * H
_REV = lambda c: NCH - 1 - c
_FWD = lambda c: c


def _tok_spec(w, ix, col=0):
    return pl.BlockSpec((NB, C, w), lambda c: (0, ix(c), col))


def _state_spec(ix):
    return pl.BlockSpec((NB, 1, H, DK, DK), lambda c: (0, ix(c), 0, 0, 0))


def _row_spec(w):
    return pl.BlockSpec((1, w), lambda c: (0, 0))


def _rows(ref):
    return tuple(ref[b] for b in range(NB))


def _hg_extra_specs(ix):
    return [_row_spec(HD), _row_spec(HD)]


def _gd_extra_specs(ix):
    return [_tok_spec(DK, ix, COL_AB // DK), _row_spec(DK), _row_spec(DK)]


def _mix_fwd(proj3, cv, l0, l1, alog, dtb):
    def body(p_ref, c_ref, ab_ref, l0_ref, l1_ref, al_ref, db_ref, oh_ref, sh_ref, og_ref, sg_ref, t_ref, sth, stg):
        @pl.when(pl.program_id(0) == 0)
        def _():
            sth[...] = jnp.zeros_like(sth)
            stg[...] = jnp.zeros_like(stg)

        Sh = tuple(sth[n] for n in range(NU))
        Sg = tuple(stg[n] for n in range(NU))
        for n in range(NU):
            sh_ref[n // H, 0, n % H] = Sh[n]
            sg_ref[n // H, 0, n % H] = Sg[n]
        snh, oh, sng, og, tp = mix_chunk(Sh, _rows(p_ref), l0_ref[...], l1_ref[...],
                                         Sg, _rows(c_ref), _rows(ab_ref), al_ref[...], db_ref[...])
        for n in range(NU):
            sth[n] = snh[n]
            stg[n] = sng[n]
        for b in range(NB):
            oh_ref[b] = oh[b]
            og_ref[b] = og[b]
            t_ref[b] = tp[b]

    tok = jax.ShapeDtypeStruct((NB, TP, HD), f32)
    st = jax.ShapeDtypeStruct((NB, NCH, H, DK, DK), f32)
    return pl.pallas_call(
        body, name="mix_fwd", grid=(NCH,),
        in_specs=[_tok_spec(3 * HD, _FWD), _tok_spec(3 * HD, _FWD), _tok_spec(DK, _FWD, COL_AB // DK),
                  _row_spec(HD), _row_spec(HD), _row_spec(DK), _row_spec(DK)],
        out_specs=[_tok_spec(HD, _FWD), _state_spec(_FWD), _tok_spec(HD, _FWD), _state_spec(_FWD), _tok_spec(HD, _FWD)],
        out_shape=[tok, st, tok, st, tok],
        scratch_shapes=[pltpu.VMEM((NU, DK, DK), f32), pltpu.VMEM((NU, DK, DK), f32)],
        compiler_params=_cp(("arbitrary",)),
    )(proj3, cv, proj3, l0, l1, alog, dtb)


def _hg_bwd(proj3, l0, l1, s_saved, do):
    def body(p_ref, l0_ref, l1_ref, s_ref, do_ref, dp_ref, dl0_ref, dl1_ref, dst):
        @pl.when(pl.program_id(0) == 0)
        def _():
            dst[...] = jnp.zeros_like(dst)
            dl0_ref[...] = jnp.zeros_like(dl0_ref)
            dl1_ref[...] = jnp.zeros_like(dl1_ref)

        S = tuple(s_ref[n // H, 0, n % H] for n in range(NU))
        _, vjp = jax.vjp(hg_chunk, S, _rows(p_ref), l0_ref[...], l1_ref[...])
        dS, dp, dl0, dl1 = vjp((tuple(dst[n] for n in range(NU)), _rows(do_ref)))
        for n in range(NU):
            dst[n] = dS[n]
        for b in range(NB):
            dp_ref[b] = dp[b].astype(bf16)
        dl0_ref[...] += jnp.broadcast_to(dl0, (8, HD))
        dl1_ref[...] += jnp.broadcast_to(dl1, (8, HD))

    acc = pl.BlockSpec((8, HD), lambda c: (0, 0))
    return pl.pallas_call(
        body, name="hg_bwd", grid=(NCH,),
        in_specs=[_tok_spec(3 * HD, _REV)] + _hg_extra_specs(_REV) + [_state_spec(_REV), _tok_spec(HD, _REV)],
        out_specs=[_tok_spec(3 * HD, _REV), acc, acc],
        out_shape=[jax.ShapeDtypeStruct((NB, TP, 3 * HD), bf16), jax.ShapeDtypeStruct((8, HD), f32),
                   jax.ShapeDtypeStruct((8, HD), f32)],
        scratch_shapes=[pltpu.VMEM((NU, DK, DK), f32)],
        compiler_params=_cp(("arbitrary",)),
    )(proj3, l0, l1, s_saved, do)


def _gd_bwd(cv, proj3, alog, dtb, s_saved, t_saved, do):
    def body(c_ref, ab_ref, al_ref, db_ref, s_ref, t_ref, do_ref, dc_ref, dab_ref, dal_ref, ddb_ref, dst):
        @pl.when(pl.program_id(0) == 0)
        def _():
            dst[...] = jnp.zeros_like(dst)
            dal_ref[...] = jnp.zeros_like(dal_ref)
            ddb_ref[...] = jnp.zeros_like(ddb_ref)

        S = tuple(s_ref[n // H, 0, n % H] for n in range(NU))
        t_rows = _rows(t_ref)
        fn = lambda *a: gd_chunk(*a, t_saved=t_rows)[0:2]
        _, vjp = jax.vjp(fn, S, _rows(c_ref), _rows(ab_ref), al_ref[...], db_ref[...])
        dS, dc, dab, dal, ddb = vjp((tuple(dst[n] for n in range(NU)), _rows(do_ref)))
        for n in range(NU):
            dst[n] = dS[n]
        for b in range(NB):
            dc_ref[b] = dc[b]
            dab_ref[b] = dab[b].astype(bf16)
        dal_ref[...] += jnp.broadcast_to(dal, (8, DK))
        ddb_ref[...] += jnp.broadcast_to(ddb, (8, DK))

    acc = pl.BlockSpec((8, DK), lambda c: (0, 0))
    return pl.pallas_call(
        body, name="gd_bwd", grid=(NCH,),
        in_specs=[_tok_spec(3 * HD, _REV)] + _gd_extra_specs(_REV)
        + [_state_spec(_REV), _tok_spec(HD, _REV), _tok_spec(HD, _REV)],
        out_specs=[_tok_spec(3 * HD, _REV), _tok_spec(DK, _REV), acc, acc],
        out_shape=[jax.ShapeDtypeStruct((NB, TP, 3 * HD), f32), jax.ShapeDtypeStruct((NB, TP, DK), bf16),
                   jax.ShapeDtypeStruct((8, DK), f32), jax.ShapeDtypeStruct((8, DK), f32)],
        scratch_shapes=[pltpu.VMEM((NU, DK, DK), f32)],
        compiler_params=_cp(("arbitrary",)),
    )(cv, proj3, alog, dtb, s_saved, t_saved, do)


def _conv_bwd(proj3, conv4, dy):
    def body(x_ref, w_ref, dy_ref, dx_ref, dw_ref):
        @pl.when(pl.program_id(1) == 0)
        def _():
            dw_ref[...] = jnp.zeros_like(dw_ref)

        x = x_ref[0]
        row = lax.broadcasted_iota(jnp.int32, (TP, 1), 0)
        g = jnp.where(row >= 8, dy_ref[0], 0.0)
        dx = w_ref[3] * g
        dw_ref[3] += jnp.broadcast_to(jnp.sum(x * g, axis=0, keepdims=True), (8, HD))
        for s in (1, 2, 3):
            dx = dx + w_ref[3 - s] * pltpu.roll(g, TP - s, 0)
            dw_ref[3 - s] += jnp.broadcast_to(jnp.sum(pltpu.roll(x, s, 0) * g, axis=0, keepdims=True), (8, HD))
        dx_ref[0] = dx.astype(bf16)

    return pl.pallas_call(
        body, name="conv_bwd", grid=(3, NB),
        in_specs=[pl.BlockSpec((1, TP, HD), lambda j, b: (b, 0, COL_QKV // HD + j)),
                  pl.BlockSpec((4, 1, HD), lambda j, b: (0, 0, j)), pl.BlockSpec((1, TP, HD), lambda j, b: (b, 0, j))],
        out_specs=[pl.BlockSpec((1, TP, HD), lambda j, b: (b, 0, j)), pl.BlockSpec((4, 8, HD), lambda j, b: (0, 0, j))],
        out_shape=[jax.ShapeDtypeStruct((NB, TP, 3 * HD), bf16), jax.ShapeDtypeStruct((4, 8, 3 * HD), f32)],
        compiler_params=_cp(("arbitrary", "arbitrary")),
    )(proj3, conv4, dy)


def _out_loss(o_hg, o_gd, proj, hgw, gdw, wout, hflat, fw, target):
    tm = 384

    def body(ohg_ref, ogd_ref, zhg_ref, zgd_ref, hgw_ref, gdw_ref, wo_ref, h_ref, fw_ref, *refs):
        tg_refs = refs[:tm // C]
        (loss_ref, dohg_ref, dogd_ref, dzhg_ref, dzgd_ref, dh_ref, dwo_ref, dhgw_ref, dgdw_ref, dfw_ref) = refs[tm // C:]
        i = pl.program_id(0)

        @pl.when(i == 0)
        def _():
            for r in (loss_ref, dwo_ref, dhgw_ref, dgdw_ref, dfw_ref):
                r[...] = jnp.zeros_like(r)

        row = i * tm + lax.broadcasted_iota(jnp.int32, (tm, 1), 0)
        tok = jnp.where(row >= TP, row - TP, row)
        valid = (tok >= PAD + N_META).astype(f32)
        hval = h_ref[...]
        tgt = jnp.concatenate([r[...] for r in tg_refs], axis=0)

        mixers = ((ohg_ref, zhg_ref, hgw_ref[...]), (ogd_ref, zgd_ref, gdw_ref[...]))
        saved, ys = [], []
        for o_ref, z_ref, w in mixers:
            for hh in range(H):
                sl = slice(hh * DK, (hh + 1) * DK)
                o, z = o_ref[:, sl], z_ref[:, sl]
                r = lax.rsqrt(jnp.mean(o * o, axis=-1, keepdims=True) + EPS)
                n = o * r
                sg = sigmoid(z)
                ws = w * (z * sg)
                saved.append((r, n, sg, z, ws, w))
                ys.append(n * ws)
        y = jnp.concatenate(ys, axis=-1)
        h2 = hval + bdot(y, wo_ref[...])
        r2 = lax.rsqrt(jnp.mean(h2 * h2, axis=-1, keepdims=True) + EPS)
        n2 = h2 * r2
        fwv = fw_ref[...]
        err = (n2 * fwv - tgt) * valid
        loss = (0.5 / D) * jnp.sum(err * err)
        dyf = err * (1.0 / D)
        dn2 = dyf * fwv
        dout = r2 * (dn2 - n2 * jnp.mean(dn2 * n2, axis=-1, keepdims=True))
        dh_ref[...] = dout
        dy = bdot_nt(dout, wo_ref[...])
        dwo_ref[...] += bdot_tn(y, dout)
        dws = []
        for mi, (do_ref, dz_ref) in enumerate(((dohg_ref, dzhg_ref), (dogd_ref, dzgd_ref))):
            dw = jnp.zeros((1, DK), f32)
            for hh in range(H):
                sl = slice(hh * DK, (hh + 1) * DK)
                r, n, sg, z, ws, w = saved[mi * H + hh]
                dyh = dy[:, mi * HD + hh * DK:mi * HD + (hh + 1) * DK]
                t = dyh * n
                dw = dw + jnp.sum(t * (z * sg), axis=0, keepdims=True)
                dz_ref[:, sl] = (t * w * (sg * (1.0 + z * (1.0 - sg)))).astype(bf16)
                dn = dyh * ws
                do_ref[:, sl] = r * (dn - n * jnp.mean(dn * n, axis=-1, keepdims=True))
            dws.append(dw)
        loss_ref[...] += jnp.broadcast_to(loss, (8, DK))
        dhgw_ref[...] += jnp.broadcast_to(dws[0], (8, DK))
        dgdw_ref[...] += jnp.broadcast_to(dws[1], (8, DK))
        dfw_ref[...] += jnp.broadcast_to(jnp.sum(dyf * n2, axis=0, keepdims=True), (8, D))

    row = lambda w: pl.BlockSpec((tm, w), lambda i: (i, 0))
    whole = lambda r, w: pl.BlockSpec((r, w), lambda i: (0, 0))
    col = lambda c0: pl.BlockSpec((tm, HD), lambda i: (i, c0 // HD))

    tgt_spec = lambda k: pl.BlockSpec((C, D), _chunk_index(tm // C, k))
    return pl.pallas_call(
        body, name="out_loss", grid=(N // tm,),
        in_specs=[row(HD), row(HD), col(COL_ZHG), col(COL_ZGD),
                  whole(1, DK), whole(1, DK), whole(D, D), row(D), whole(1, D)] + [tgt_spec(k) for k in range(tm // C)],
        out_specs=[whole(8, DK), row(HD), row(HD), row(HD), row(HD), row(D), whole(D, D),
                   whole(8, DK), whole(8, DK), whole(8, D)],
        out_shape=[jax.ShapeDtypeStruct((8, DK), f32)] + [jax.ShapeDtypeStruct((N, HD), f32)] * 2
        + [jax.ShapeDtypeStruct((N, HD), bf16)] * 2
        + [jax.ShapeDtypeStruct((N, D), f32), jax.ShapeDtypeStruct((D, D), f32),
           jax.ShapeDtypeStruct((8, DK), f32), jax.ShapeDtypeStruct((8, DK), f32), jax.ShapeDtypeStruct((8, D), f32)],
        compiler_params=_cp(("arbitrary",)),
    )(o_hg, o_gd, proj, proj, hgw, gdw, wout, hflat, fw, *[target] * (tm // C))


def _in_bwd(pieces, wbig, hflat, norm_w, dh_res, pbs):
    tm = 384
    nsteps = N // tm
    np_ = len(pieces)
    na = len(pbs)
    offs = [c0 for _, c0 in pieces]
    widths = [d.shape[1] for d, _ in pieces]

    def body(*refs):
        d_refs = refs[:np_]
        w_ref, h_ref, nw_ref, dhr_ref = refs[np_:np_ + 4]
        srcs = refs[np_ + 4:np_ + 4 + na]
        dh_ref, dnw_ref = refs[np_ + 4 + na:np_ + 6 + na]
        dsts = refs[np_ + 6 + na:np_ + 6 + 2 * na]
        sems = refs[np_ + 6 + 2 * na:]
        i = pl.program_id(0)

        def copies():
            if not na:
                return []
            x, y, c, chips = _place()
            return [pltpu.make_async_remote_copy(
                src_ref=srcs[a].at[2 * px + py], dst_ref=dsts[a].at[j], send_sem=sems[0].at[na * j + a],
                recv_sem=sems[1].at[na * j + a], device_id=(px, py, c), device_id_type=MESH)
                for j, (px, py) in enumerate(chips) for a in range(na)]

        @pl.when(i == 0)
        def _():
            dnw_ref[...] = jnp.zeros_like(dnw_ref)
            for cp in copies():
                cp.start()

        du = jnp.zeros((tm, D), f32)
        for d_ref, off, wd in zip(d_refs, offs, widths):
            du = du + bdot(d_ref[...], w_ref[off:off + wd, :])
        _, vjp = jax.vjp(rmsnorm, h_ref[...], nw_ref[...])
        dh, dnw = vjp(du)
        dh_ref[...] = dh + dhr_ref[...]
        dnw_ref[...] += jnp.broadcast_to(dnw, (8, D))

        @pl.when(i == nsteps - 1)
        def _():
            for cp in copies():
                cp.wait()

    row = lambda w: pl.BlockSpec((tm, w), lambda i: (i, 0))
    return pl.pallas_call(
        body, name="in_bwd", grid=(nsteps,),
        in_specs=[row(w) for w in widths]
        + [pl.BlockSpec((PC, D), lambda i: (0, 0)), row(D), pl.BlockSpec((1, D), lambda i: (0, 0)), row(D)] + [ANY] * na,
        out_specs=[row(D), pl.BlockSpec((8, D), lambda i: (0, 0))] + [ANY] * na,
        out_shape=[jax.ShapeDtypeStruct((N, D), f32), jax.ShapeDtypeStruct((8, D), f32)]
        + [jax.ShapeDtypeStruct((3,) + p.shape[1:], p.dtype) for p in pbs],
        scratch_shapes=[pltpu.SemaphoreType.DMA((3 * na,)), pltpu.SemaphoreType.DMA((3 * na,))] if na else [],
        compiler_params=_cp(("arbitrary",)),
    )(*[d for d, _ in pieces], wbig, hflat, norm_w, dh_res, *pbs)


def _w_grad(ut, pieces):
    tk = N // 3
    offs = [c0 for _, c0 in pieces]
    widths = [d.shape[1] for d, _ in pieces]

    def body(u_ref, *refs):
        d_refs, o_ref = refs[:-1], refs[-1]

        @pl.when(pl.program_id(0) == 0)
        def _():
            o_ref[...] = jnp.zeros_like(o_ref)

        u = u_ref[...]
        for d_ref, off, wd in zip(d_refs, offs, widths):
            o_ref[:, off:off + wd] += jnp.dot(u, d_ref[...], preferred_element_type=f32)

    return pl.pallas_call(
        body, name="w_grad", grid=(N // tk,),
        in_specs=[pl.BlockSpec((D, tk), lambda k: (0, k))] + [pl.BlockSpec((tk, w), lambda k: (k, 0)) for w in widths],
        out_specs=pl.BlockSpec((D, PC), lambda k: (0, 0), pipeline_mode=pl.Buffered(1)),
        out_shape=jax.ShapeDtypeStruct((D, PC), f32),
        compiler_params=_cp(("arbitrary",)),
    )(ut, *[d for d, _ in pieces])


def _adam_math(g, w, m, v):
    m2 = ADAM_B1 * m + (1.0 - ADAM_B1) * g
    v2 = ADAM_B2 * v + (1.0 - ADAM_B2) * (g * g)
    m_hat = m2 / (1.0 - ADAM_B1 ** ADAM_STEP)
    v_hat = v2 / (1.0 - ADAM_B2 ** ADAM_STEP)
    delta = -ADAM_LR * (m_hat / (jnp.sqrt(v_hat) + ADAM_EPS) + ADAM_WD * w)
    return delta, m2, v2


def _adamw(gs, w, m, v, name):
    R, Cc = w.shape
    tr = 256 if R % 256 == 0 else R
    ng = len(gs)

    def body(*refs):
        g = refs[0][...]
        for r in refs[1:ng]:
            g = g + r[...]
        w_ref, m_ref, v_ref, g_ref, d_ref, m2_ref, v2_ref = refs[ng:]
        delta, m2, v2 = _adam_math(g, w_ref[...], m_ref[...], v_ref[...])
        g_ref[...] = g
        d_ref[...] = delta
        m2_ref[...] = m2
        v2_ref[...] = v2

    spec = pl.BlockSpec((tr, Cc), lambda i: (i, 0))
    return pl.pallas_call(
        body, name=name, grid=(R // tr,),
        in_specs=[spec] * (ng + 3), out_specs=[spec] * 4,
        out_shape=[jax.ShapeDtypeStruct((R, Cc), f32)] * 4,
        compiler_params=_cp(("arbitrary",)),
    )(*gs, w, m, v)


def _adamw_rows(g, w, m, v, name):
    R, _, Cc = w.shape
    tr = R // 9

    def body(g_ref, w_ref, m_ref, v_ref, go_ref, d_ref, m2_ref, v2_ref):
        g = g_ref[...]
        delta, m2, v2 = _adam_math(g, w_ref[...], m_ref[...], v_ref[...])
        go_ref[...] = g
        d_ref[...] = delta
        m2_ref[...] = m2
        v2_ref[...] = v2

    spec = pl.BlockSpec((tr, 1, Cc), lambda i: (i, 0, 0))
    return pl.pallas_call(
        body, name=name, grid=(R // tr,),
        in_specs=[spec] * 4, out_specs=[spec] * 4,
        out_shape=[jax.ShapeDtypeStruct((R, 1, Cc), f32)] * 4,
        compiler_params=_cp(("arbitrary",)),
    )(g, w, m, v)


def _place():
    x, y, c = lax.axis_index("x"), lax.axis_index("y"), lax.axis_index("c")
    return x, y, c, [(1 - x, y), (x, 1 - y), (1 - x, 1 - y)]


def _gather_weights(cm, halved, whole):
    R, _, Cc = cm.shape
    hw = Cc // 2
    shards = [jax.ShapeDtypeStruct((R, Cc), bf16)] + list(halved) + list(whole)
    nh = 1 + len(halved)
    na = len(shards)

    def body(*refs):
        srcs, dsts = refs[:na], refs[na:2 * na]
        send_sems, recv_sems, loc_sems = refs[2 * na:2 * na + 3]
        stage = refs[2 * na + 3:3 * na + 3]
        raw = refs[3 * na + 3]
        x, y, c, chips = _place()
        me = 2 * x + y
        loads = [pltpu.make_async_copy(srcs[0], raw, loc_sems.at[0])]
        loads += [pltpu.make_async_copy(srcs[i], stage[i], loc_sems.at[i]) for i in range(1, na)]
        locs = [pltpu.make_async_copy(v, d.at[me], loc_sems.at[i]) for i, (v, d) in enumerate(zip(stage, dsts))]
        for cp in loads:
            cp.start()

        def half_of(ref, i, half):
            return ref.at[:, pl.ds(pl.multiple_of(half * hw, hw), hw)] if i == 0 else ref.at[half]

        def ici(j, i, slot):
            px, py = chips[j]
            src = half_of(stage[0] if i == 0 else srcs[i], i, c) if i < nh else srcs[i]
            dst = half_of(dsts[i].at[slot], i, c) if i < nh else dsts[i].at[slot]
            return pltpu.make_async_remote_copy(
                src_ref=src, dst_ref=dst, send_sem=send_sems.at[na * j + i], recv_sem=recv_sems.at[na * j + i],
                device_id=(px, py, c), device_id_type=MESH)

        def d2d(j, i, half):
            px, py = chips[j]
            blk = half_of(dsts[i].at[2 * px + py], i, half)
            return pltpu.make_async_remote_copy(
                src_ref=blk, dst_ref=blk, send_sem=send_sems.at[3 * na + nh * j + i],
                recv_sem=recv_sems.at[3 * na + nh * j + i], device_id=(x, y, 1 - c), device_id_type=MESH)

        sends = [ici(j, i, me) for j in range(3) for i in range(1, na)]
        for cp in sends:
            cp.start()
        loads[0].wait()
        stage[0][...] = raw[:, 0, :].astype(bf16)
        first = [ici(j, 0, me) for j in range(3)]
        for cp in first:
            cp.start()
        sends += first
        locs[0].start()
        for ld, st in zip(loads[1:], locs[1:]):
            ld.wait()
            st.start()
        for j, (px, py) in enumerate(chips):
            for i in range(na):
                ici(j, i, 2 * px + py).wait_recv()
                if i < nh:
                    fwd = d2d(j, i, c)
                    fwd.start()
                    sends.append(fwd)
        for j in range(3):
            for i in range(nh):
                d2d(j, i, 1 - c).wait_recv()
        for cp in sends:
            cp.wait_send()
        for cp in locs:
            cp.wait()

    nsem = 3 * na + 3 * nh
    return pl.pallas_call(
        body, name="gather_weights",
        in_specs=[ANY] * na, out_specs=[ANY] * na,
        out_shape=[jax.ShapeDtypeStruct((4,) + s.shape, s.dtype) for s in shards],
        scratch_shapes=[pltpu.SemaphoreType.DMA((nsem,)), pltpu.SemaphoreType.DMA((nsem,)),
                        pltpu.SemaphoreType.DMA((na,))] + [pltpu.VMEM(s.shape, s.dtype) for s in shards]
        + [pltpu.VMEM(cm.shape, cm.dtype)],
        compiler_params=pltpu.CompilerParams(has_side_effects=True, vmem_limit_bytes=VMEM_LIMIT),
    )(cm, *halved, *whole)


def _swap_halves(gs):
    na = len(gs)
    jobs = [(i, q) for i in range(na) for q in range(gs[i].shape[0])]

    def body(*refs):
        srcs, dsts = refs[:na], refs[na:2 * na]
        send_sems, recv_sems = refs[2 * na:]
        x, y, c, _ = _place()
        cps = [pltpu.make_async_remote_copy(
            src_ref=srcs[i].at[q, 1 - c], dst_ref=dsts[i].at[q], send_sem=send_sems.at[k],
            recv_sem=recv_sems.at[k], device_id=(x, y, 1 - c), device_id_type=MESH)
            for k, (i, q) in enumerate(jobs)]
        for cp in cps:
            cp.start()
        for cp in cps:
            cp.wait()

    return pl.pallas_call(
        body, name="swap_halves",
        in_specs=[ANY] * na, out_specs=[ANY] * na,
        out_shape=[jax.ShapeDtypeStruct(g.shape[0:1] + g.shape[2:], g.dtype) for g in gs],
        scratch_shapes=[pltpu.SemaphoreType.DMA((len(jobs),)), pltpu.SemaphoreType.DMA((len(jobs),))],
        compiler_params=pltpu.CompilerParams(has_side_effects=True),
    )(*gs)


def _add_split(cm_arr, g, s):
    _, _, R, Cg = g.shape
    tr = 128

    def body(sc_ref, g_ref, s_ref, b_ref, o_ref):
        p = g_ref[0, 0] + s_ref[0]
        own = None
        for q in range(4):
            blk = p[:, SHARD_COLS * q:SHARD_COLS * (q + 1)]
            b_ref[q] = blk.astype(bf16)
            mine = jnp.where(sc_ref[1] == q, blk, 0.0)
            own = mine if own is None else own + mine
        o_ref[...] = own

    return pl.pallas_call(
        body, name="add_w_in",
        grid_spec=pltpu.PrefetchScalarGridSpec(
            num_scalar_prefetch=1, grid=(R // tr,),
            in_specs=[pl.BlockSpec((1, 1, tr, Cg), lambda i, sc: (0, sc[0], i, 0)),
                      pl.BlockSpec((1, tr, Cg), lambda i, sc: (0, i, 0))],
            out_specs=[pl.BlockSpec((4, tr, SHARD_COLS), lambda i, sc: (0, i, 0)),
                       pl.BlockSpec((tr, SHARD_COLS), lambda i, sc: (i, 0))]),
        out_shape=[jax.ShapeDtypeStruct((4, R, SHARD_COLS), bf16), jax.ShapeDtypeStruct((R, SHARD_COLS), f32)],
        compiler_params=_cp(("arbitrary",)),
    )(cm_arr, g, s)


def _add_halves(c_arr, g, s, name):
    Q, _, R, Cc = g.shape
    tr = min(R, 128)

    def body(c_ref, g_ref, s_ref, b_ref, f_ref):
        p = g_ref[0, 0] + s_ref[0]
        f_ref[0] = p
        b_ref[0] = p.astype(bf16)

    blk = pl.BlockSpec((1, tr, Cc), lambda q, i, cr: (q, i, 0))
    return pl.pallas_call(
        body, name=name,
        grid_spec=pltpu.PrefetchScalarGridSpec(
            num_scalar_prefetch=1, grid=(Q, R // tr),
            in_specs=[pl.BlockSpec((1, 1, tr, Cc), lambda q, i, cr: (q, cr[0], i, 0)), blk], out_specs=[blk, blk]),
        out_shape=[jax.ShapeDtypeStruct((Q, R, Cc), bf16), jax.ShapeDtypeStruct((Q, R, Cc), f32)],
        compiler_params=_cp(("arbitrary", "arbitrary")),
    )(c_arr, g, s)


_FLIPS = [(fx, fy, fc) for fx in (0, 1) for fy in (0, 1) for fc in (0, 1)][1:]


def _sum_blocks(own, r, name):
    R, Cc = own.shape
    tr = min(R, 256)

    def body(own_ref, r_ref, o_ref):
        acc = own_ref[...]
        for j in range(3):
            acc = acc + r_ref[j].astype(f32)
        o_ref[...] = acc

    return pl.pallas_call(
        body, name=name, grid=(R // tr,),
        in_specs=[pl.BlockSpec((tr, Cc), lambda i: (i, 0)), pl.BlockSpec((3, tr, Cc), lambda i: (0, i, 0))],
        out_specs=pl.BlockSpec((tr, Cc), lambda i: (i, 0)),
        out_shape=jax.ShapeDtypeStruct((R, Cc), f32),
        compiler_params=_cp(("arbitrary",)),
    )(own, r)


def _sum_packs(me8_arr, pack, rp):
    R = pack.shape[0]

    def body(me_ref, pk_ref, rp_ref, o_ref):
        me8 = me_ref[0]
        acc = None
        for d in range(8):
            rel = d ^ me8
            term = jnp.where(rel == 0, pk_ref[...], rp_ref[jnp.maximum(rel - 1, 0)])
            acc = term if acc is None else acc + term
        o_ref[...] = acc

    return pl.pallas_call(
        body, name="sum_packs",
        grid_spec=pltpu.PrefetchScalarGridSpec(
            num_scalar_prefetch=1, grid=(1,),
            in_specs=[pl.BlockSpec((R, 128), lambda i, mr: (0, 0)), pl.BlockSpec((7, R, 128), lambda i, mr: (0, 0, 0))],
            out_specs=pl.BlockSpec((R, 128), lambda i, mr: (0, 0))),
        out_shape=jax.ShapeDtypeStruct((R, 128), f32),
        compiler_params=_cp(("arbitrary",)),
    )(me8_arr, pack, rp)


def _swap_finished(fs, pack):
    na = len(fs)
    R = pack.shape[0]

    def body(*refs):
        srcs, pk = refs[:na], refs[na]
        dsts, rp = refs[na + 1:2 * na + 1], refs[2 * na + 1]
        send_sems, recv_sems = refs[2 * na + 2:]
        x, y, c, _ = _place()
        cps = [pltpu.make_async_remote_copy(
            src_ref=srcs[i], dst_ref=dsts[i], send_sem=send_sems.at[i], recv_sem=recv_sems.at[i],
            device_id=(x, y, 1 - c), device_id_type=MESH) for i in range(na)]
        cps += [pltpu.make_async_remote_copy(
            src_ref=pk, dst_ref=rp.at[k], send_sem=send_sems.at[na + k], recv_sem=recv_sems.at[na + k],
            device_id=(x ^ fx, y ^ fy, c ^ fc), device_id_type=MESH) for k, (fx, fy, fc) in enumerate(_FLIPS)]
        for cp in cps:
            cp.start()
        for cp in cps:
            cp.wait()

    return pl.pallas_call(
        body, name="swap_finished",
        in_specs=[ANY] * (na + 1), out_specs=[ANY] * (na + 1),
        out_shape=[jax.ShapeDtypeStruct(f.shape, f.dtype) for f in fs] + [jax.ShapeDtypeStruct((7, R, 128), f32)],
        scratch_shapes=[pltpu.SemaphoreType.DMA((na + 7,)), pltpu.SemaphoreType.DMA((na + 7,))],
        compiler_params=pltpu.CompilerParams(has_side_effects=True),
    )(*fs, pack)


def _adamw_halves(c_arr, mine, peer, w, m, v, name):
    _, R, Cc = w.shape
    tr = min(R, 256)

    def body(c_ref, mine_ref, peer_ref, w_ref, m_ref, v_ref, g_ref, d_ref, m2_ref, v2_ref):
        g = jnp.where(pl.program_id(0) == c_ref[0], mine_ref[...], peer_ref[...])
        delta, m2, v2 = _adam_math(g, w_ref[0], m_ref[0], v_ref[0])
        g_ref[0] = g
        d_ref[0] = delta
        m2_ref[0] = m2
        v2_ref[0] = v2

    half = pl.BlockSpec((tr, Cc), lambda hh, i, cr: (i, 0))
    full = pl.BlockSpec((1, tr, Cc), lambda hh, i, cr: (hh, i, 0))
    return pl.pallas_call(
        body, name=name,
        grid_spec=pltpu.PrefetchScalarGridSpec(
            num_scalar_prefetch=1, grid=(2, R // tr), in_specs=[half, half, full, full, full], out_specs=[full] * 4),
        out_shape=[jax.ShapeDtypeStruct((2, R, Cc), f32)] * 4,
        compiler_params=_cp(("arbitrary", "arbitrary")),
    )(c_arr, mine, peer, w, m, v)


def _rows8(a):
    flat = a.reshape(-1)
    n = flat.shape[0]
    rows = -(-n // 1024) * 8
    return jnp.pad(flat, (0, rows * 128 - n)).reshape(rows, 128)


def kernel(x, meta_tokens, norm_w, w_in, conv_w, hg_lb_logits, hg_norm_w, gdn_A_log, gdn_dt_bias, gdn_norm_w, w_out, final_norm_w, loss_target, m_meta_tokens, m_norm_w, m_w_in, m_conv_w, m_hg_lb_logits, m_hg_norm_w, m_gdn_A_log, m_gdn_dt_bias, m_gdn_norm_w, m_w_out, m_final_norm_w, v_meta_tokens, v_norm_w, v_w_in, v_conv_w, v_hg_lb_logits, v_hg_norm_w, v_gdn_A_log, v_gdn_dt_bias, v_gdn_norm_w, v_w_out, v_final_norm_w):
    me = 2 * lax.axis_index("x") + lax.axis_index("y")

    g_win, g_wout, g_conv, g_meta = _gather_weights(
        jnp.transpose(w_in, (2, 0, 1)), [w_out[0].astype(bf16).reshape(2, D // 8, D)], [conv_w[0], meta_tokens])
    wout_full = g_wout.reshape(D, D)
    conv4 = jnp.transpose(g_conv, (1, 0, 2)).reshape(4, 1, 3 * HD)
    meta_full = jnp.transpose(g_meta, (1, 0, 2)).reshape(N_META, D)

    c_arr = lax.axis_index("c").reshape(1).astype(jnp.int32)

    def chip_partials(gw, g_wout_part):
        g_in2 = gw.reshape(1, 2, D // 2, PC)
        g_out4 = g_wout_part.reshape(4, 2, D // 8, D)
        s_in, s_out = _swap_halves([g_in2, g_out4])
        pb_blocks, own_in = _add_split(jnp.concatenate([c_arr, me.reshape(1).astype(jnp.int32)]), g_in2, s_in)
        pb_out, pf_out = _add_halves(c_arr, g_out4, s_out, "add_w_out")
        own_out = lax.dynamic_index_in_dim(pf_out, me, axis=0, keepdims=False)
        return [pb_blocks, pb_out], [own_in, own_out]

    (loss8, grad_x, d_meta, d_nw, d_conv, d_lb, d_hgw, d_alog, d_dtb, d_gdw, d_fw, pfs, rs) = _local_step(
        x, loss_target, g_win, wout_full, conv4, meta_full, norm_w, hg_lb_logits, hg_norm_w, gdn_A_log, gdn_dt_bias,
        gdn_norm_w, final_norm_w, chip_partials)

    pack = jnp.concatenate([
        loss8, d_nw[0].reshape(8, 128), d_lb.reshape(8, 128), d_hgw, _rows8(d_alog[0, :H]), _rows8(d_dtb[0, :H]),
        d_gdw, d_fw[0].reshape(8, 128), d_meta.reshape(128, 128), d_conv.reshape(48, 128)], axis=0)
    return _reduce_and_update(
        me, c_arr, grad_x, pfs, rs, pack, meta_tokens, norm_w, w_in, conv_w, hg_lb_logits, hg_norm_w, gdn_A_log,
        gdn_dt_bias, gdn_norm_w, w_out, final_norm_w, m_meta_tokens, m_norm_w, m_w_in, m_conv_w, m_hg_lb_logits,
        m_hg_norm_w, m_gdn_A_log, m_gdn_dt_bias, m_gdn_norm_w, m_w_out, m_final_norm_w, v_meta_tokens, v_norm_w, v_w_in,
        v_conv_w, v_hg_lb_logits, v_hg_norm_w, v_gdn_A_log, v_gdn_dt_bias, v_gdn_norm_w, v_w_out, v_final_norm_w)


def _local_step(x, loss_target, w4, wout_full, conv4, meta_full, norm_w, hg_lb_logits, hg_norm_w, gdn_A_log, gdn_dt_bias,
                gdn_norm_w, final_norm_w, chip_partials):
    head = jnp.concatenate([jnp.zeros((PAD, D), f32), meta_full], axis=0)
    target = loss_target.reshape(NB * SEQ, D)
    l0, l1 = hg_lb_logits[0:1], hg_lb_logits[1:2]
    alog = jnp.pad(gdn_A_log, ((0, 0), (0, DK - H)))
    dtb = jnp.pad(gdn_dt_bias, ((0, 0), (0, DK - H)))
    fw = final_norm_w.reshape(1, D)

    hflat, proj, ut, cv2, wbig = _in_proj(x.reshape(NB * SEQ, D), head, norm_w, w4, conv4)
    proj3 = proj.reshape(NB, TP, PC)
    cv = cv2.reshape(NB, TP, 3 * HD)
    o_hg, s_hg, o_gd, s_gd, t_gd = _mix_fwd(proj3, cv, l0, l1, alog, dtb)
    (loss8, d_ohg, d_ogd, d_zhg, d_zgd, dh_res, g_wout_part, d_hgw, d_gdw, d_fw) = _out_loss(
        o_hg.reshape(N, HD), o_gd.reshape(N, HD), proj, hg_norm_w, gdn_norm_w, wout_full, hflat, fw, target)
    d_hg, d_l0, d_l1 = _hg_bwd(proj3, l0, l1, s_hg, d_ohg.reshape(NB, TP, HD))
    d_cv, d_ab, d_alog, d_dtb = _gd_bwd(cv, proj3, alog, dtb, s_gd, t_gd, d_ogd.reshape(NB, TP, HD))
    d_qkv, d_conv4 = _conv_bwd(proj3, conv4, d_cv)
    d_hg2, d_qkv2, d_ab2 = d_hg.reshape(N, 3 * HD), d_qkv.reshape(N, 3 * HD), d_ab.reshape(N, DK)
    pieces = [(d_hg2, COL_HG), (d_zhg, COL_ZHG), (d_qkv2, COL_QKV), (d_zgd, COL_ZGD), (d_ab2, COL_AB)]
    gw = _w_grad(ut, pieces)
    pbs, pfs = chip_partials(gw, g_wout_part) if chip_partials else ([], [gw, g_wout_part])
    dh, d_nw, *rs = _in_bwd(pieces, wbig, hflat, norm_w, dh_res, pbs)

    dh3 = dh.reshape(NB, TP, D)
    grad_x = dh3[:, PAD + N_META:, :]
    d_meta = jnp.sum(dh3[:, PAD:PAD + N_META, :], axis=0)
    d_conv = d_conv4[:, 0, :]
    d_lb = jnp.concatenate([d_l0[0:1], d_l1[0:1]], axis=0)
    return loss8, grad_x, d_meta, d_nw, d_conv, d_lb, d_hgw, d_alog, d_dtb, d_gdw, d_fw, pfs, rs


def _reduce_and_update(me, c_arr, grad_x, pfs, rs, pack, meta_tokens, norm_w, w_in, conv_w, hg_lb_logits, hg_norm_w,
                       gdn_A_log, gdn_dt_bias, gdn_norm_w, w_out, final_norm_w, m_meta_tokens, m_norm_w, m_w_in, m_conv_w,
                       m_hg_lb_logits, m_hg_norm_w, m_gdn_A_log, m_gdn_dt_bias, m_gdn_norm_w, m_w_out, m_final_norm_w,
                       v_meta_tokens, v_norm_w, v_w_in, v_conv_w, v_hg_lb_logits, v_hg_norm_w, v_gdn_A_log, v_gdn_dt_bias,
                       v_gdn_norm_w, v_w_out, v_final_norm_w):
    (own_in, own_out), (r_in, r_out) = pfs, rs
    f_in = _sum_blocks(own_in, r_in, "sum_w_in")
    f_out = _sum_blocks(own_out, r_out, "sum_w_out")
    o_in, o_out, r_pack = _swap_finished([f_in, f_out], pack)
    me8_arr = (2 * me + lax.axis_index("c")).reshape(1).astype(jnp.int32)
    small = _sum_packs(me8_arr, pack, r_pack)

    half_out = lambda a: a[0].reshape(2, D // 8, D)
    is0 = lax.axis_index("c") == 0
    g_in = jnp.concatenate([jnp.where(is0, f_in, o_in), jnp.where(is0, o_in, f_in)], axis=0)
    to_cm = lambda a: jnp.transpose(a, (2, 0, 1))
    gi, di, mi, vi = [jnp.transpose(a, (1, 2, 0))[0] for a in _adamw_rows(
        g_in.T.reshape(SHARD_COLS, 1, D), to_cm(w_in), to_cm(m_w_in), to_cm(v_w_in), "adamw_w_in")]
    go, do_, mo, vo = [a.reshape(D // 4, D) for a in _adamw_halves(
        c_arr, f_out, o_out, half_out(w_out), half_out(m_w_out), half_out(v_w_out), "adamw_w_out")]

    g_meta_full = small[64:192].reshape(N_META, D)
    g_meta_loc = lax.dynamic_slice(g_meta_full, (0, me * 256), (N_META, 256))
    gm, dm, mm_, vm = _adamw([g_meta_loc], meta_tokens, m_meta_tokens, v_meta_tokens, "adamw_meta")
    g_conv_full = small[192:240].reshape(4, 1536)
    g_conv_loc = lax.dynamic_slice(g_conv_full, (0, me * 384), (4, 384))
    gc, dc, mc, vc = _adamw([g_conv_loc], conv_w[0], m_conv_w[0], v_conv_w[0], "adamw_conv")

    reps = [(norm_w, m_norm_w, v_norm_w), (hg_lb_logits, m_hg_lb_logits, v_hg_lb_logits),
            (hg_norm_w, m_hg_norm_w, v_hg_norm_w), (gdn_A_log, m_gdn_A_log, v_gdn_A_log),
            (gdn_dt_bias, m_gdn_dt_bias, v_gdn_dt_bias), (gdn_norm_w, m_gdn_norm_w, v_gdn_norm_w),
            (final_norm_w, m_final_norm_w, v_final_norm_w)]
    wp = jnp.concatenate([_rows8(t[0]) for t in reps], axis=0)
    mp = jnp.concatenate([_rows8(t[1]) for t in reps], axis=0)
    vp = jnp.concatenate([_rows8(t[2]) for t in reps], axis=0)
    gr, dr, mr, vr = _adamw([small[8:64]], wp, mp, vp, "adamw_small")

    def unpack(p):
        outs = []
        for i, t in enumerate(reps):
            n = t[0].size
            outs.append(p[8 * i:8 * i + 8].reshape(-1)[:n].reshape(t[0].shape))
        return outs

    def leaves(meta_v, conv_v, in_v, out_v, rep_p):
        nw, lb, hgw, al, db, gdw, fwv = unpack(rep_p)
        return [meta_v, nw, in_v[None], conv_v[None], lb, hgw, al, db, gdw, out_v[None], fwv]

    loss = small[0, 0]
    return (loss, grad_x, *leaves(gm, gc, gi, go, gr), *leaves(dm, dc, di, do_, dr),
            *leaves(mm_, mc, mi, mo, mr), *leaves(vm, vc, vi, vo, vr))
```

```python
import functools

import jax
import jax.numpy as jnp
from jax import lax
from jax.experimental import pallas as pl
from jax.experimental.pallas import tpu as pltpu

f32 = jnp.float32
bf16 = jnp.bfloat16
MESH = pl.DeviceIdType.MESH
ANY = pl.BlockSpec(memory_space=pl.ANY)

D = 1024
NB = 2
N_META = 16
SEQ = 2048
PAD = 48
TP = PAD + N_META + SEQ
C = 64
NCH = TP // C
N = NB * TP
H = 4
DK = 128
HD = H * DK
PC = 4224
IN_COLS = 4104
SHARD_COLS = IN_COLS // 4
COL_HG, COL_ZHG, COL_QKV, COL_ZGD, COL_AB = 0, 3 * HD, 4 * HD, 7 * HD, 8 * HD
EPS = 1e-6
ADAM_LR, ADAM_B1, ADAM_B2, ADAM_EPS, ADAM_WD, ADAM_STEP = 0.001, 0.9, 0.999, 1e-08, 0.01, 10
VMEM_LIMIT = 56 * 1024 * 1024

P_HG = dict(lvl=1, av=1, qs=1, su=1)
P_GD = dict(kk=1, inv=1, sol=1, ws=1, qk=1, o=1, su=1)


def _cp(sem=None, **kw):
    return pltpu.CompilerParams(dimension_semantics=sem, vmem_limit_bytes=VMEM_LIMIT, **kw)


_DIMS = {"nn": (((1,), (0,)), ((), ())), "nt": (((1,), (1,)), ((), ())), "tn": (((0,), (0,)), ((), ()))}


def _split(x):
    hi = x.astype(bf16)
    return hi, (x - hi.astype(f32)).astype(bf16)


def _dg(a, b, kind, passes):
    d = lambda x, y: lax.dot_general(x, y, _DIMS[kind], preferred_element_type=f32)
    if passes == 1:
        return d(a.astype(bf16), b.astype(bf16))
    ah, al = _split(a)
    bh, bl = _split(b)
    return d(ah, bh) + d(ah, bl) + d(al, bh)


@functools.partial(jax.custom_vjp, nondiff_argnums=(2, 3))
def mmx(a, b, kind, passes):
    return _dg(a, b, kind, passes)


def _mmx_fwd(a, b, kind, passes):
    return _dg(a, b, kind, passes), (a, b)


def _mmx_bwd(kind, passes, res, g):
    a, b = res
    if kind == "nn":
        return _dg(g, b, "nt", passes), _dg(a, g, "tn", passes)
    if kind == "nt":
        return _dg(g, b, "nn", passes), _dg(g, a, "tn", passes)
    return _dg(b, g, "nt", passes), _dg(a, g, "nn", passes)


mmx.defvjp(_mmx_fwd, _mmx_bwd)


def _mask_dg(mask, x):
    xh, xl = _split(x)
    return jnp.dot(jnp.concatenate([mask, mask], axis=1), jnp.concatenate([xh, xl], axis=0), preferred_element_type=f32)


@functools.partial(jax.custom_vjp, nondiff_argnums=(2,))
def mask_mm(mask, x, bwd_passes):
    return _mask_dg(mask, x)


def _mask_fwd(mask, x, bwd_passes):
    return _mask_dg(mask, x), mask


def _mask_bwd(bwd_passes, mask, g):
    d = lambda y: lax.dot_general(mask, y, _DIMS["tn"], preferred_element_type=f32)
    if bwd_passes == 1:
        return None, d(g.astype(bf16))
    gh, gl = _split(g)
    return None, d(gh) + d(gl)


mask_mm.defvjp(_mask_fwd, _mask_bwd)


def bdot(a, b):
    return jnp.dot(a.astype(bf16), b.astype(bf16), preferred_element_type=f32)


def bdot_nt(a, b):
    return lax.dot_general(a.astype(bf16), b.astype(bf16), _DIMS["nt"], preferred_element_type=f32)


def bdot_tn(a, b):
    return lax.dot_general(a.astype(bf16), b.astype(bf16), _DIMS["tn"], preferred_element_type=f32)


def _iota2(n, m):
    return lax.broadcasted_iota(jnp.int32, (n, m), 0), lax.broadcasted_iota(jnp.int32, (n, m), 1)


sigmoid = jax.nn.sigmoid


def silu(x):
    return x * sigmoid(x)


def softplus(x):
    return jnp.maximum(x, 0.0) + jnp.log(1.0 + jnp.exp(-jnp.abs(x)))


def rmsnorm(x, w):
    return x * lax.rsqrt(jnp.mean(x * x, axis=-1, keepdims=True) + EPS) * w


def hg_masks():
    t, r = _iota2(C, C)
    mats = [r <= t, r > t]
    lvl = []
    for l in range(1, 7):
        sz = 1 << l
        half = sz >> 1
        seg_t = t >> l
        upper_t = (t & (sz - 1)) >= half
        mid_t = seg_t * sz + half - 1
        mats.append((upper_t & (r > mid_t) & (r <= t)) | ((~upper_t) & (r > t) & (r <= mid_t)))
        lvl.append(((seg_t == (r >> l)) & upper_t & ((r & (sz - 1)) < half)).astype(f32))
    stk = jnp.concatenate([m.astype(bf16) for m in mats], axis=0)
    return stk, lvl, (t == r).astype(f32)


def _head(a, h):
    return a[:, h * DK:(h + 1) * DK]


def _run(*gens):
    results = [None] * len(gens)
    live = list(range(len(gens)))
    while live:
        for i in list(live):
            try:
                next(gens[i])
            except StopIteration as e:
                results[i] = e.value
                live.remove(i)
    return results


def hg_chunk(St, ps, l0, l1):
    return _run(hg_stages(St, ps, l0, l1))[0]


def gd_chunk(S, cs, abs_, alog, dtb, t_saved=None):
    return _run(gd_stages(S, cs, abs_, alog, dtb, t_saved))[0]


def mix_chunk(St, ps, l0, l1, S, cs, abs_, alog, dtb):
    (sn_h, o_h), (sn_g, o_g, t_pack) = _run(hg_stages(St, ps, l0, l1), gd_stages(S, cs, abs_, alog, dtb))
    return sn_h, o_h, sn_g, o_g, t_pack


def hg_stages(St, ps, l0, l1):
    m = jnp.maximum(l0, l1)
    e0 = jnp.exp(l0 - m)
    e1 = jnp.exp(l1 - m)
    lb = e0 / (e0 + e1)
    stk, lvl, eye = hg_masks()
    msk = [eye] + lvl
    trow = lax.broadcasted_iota(jnp.int32, (C, 1), 0)
    upper = [(trow & ((1 << l) - 1)) >= (1 << (l - 1)) for l in range(1, 7)]
    qs, ks, vs, qG, kR, eGl = [], [], [], [], [], []
    for p in ps:
        pq, pf, v = p[:, 0:HD], p[:, HD:2 * HD], p[:, 2 * HD:3 * HD]
        q = silu(pq)
        f = lb + (1.0 - lb) * sigmoid(pf)
        k = 1.0 - f
        logf = jnp.log(f)
        Dm = mask_mm(stk, logf, 1)
        z = [jnp.where(up, q, k) * jnp.exp(Dm[(2 + i) * C:(3 + i) * C]) for i, up in enumerate(upper)]
        qs.append([q] + z)
        ks.append([k] + z)
        vs.append(v)
        qG.append(q * jnp.exp(Dm[0:C]))
        kR.append(k * jnp.exp(Dm[C:2 * C]))
        eGl.append(jnp.exp(jnp.sum(logf, axis=0, keepdims=True)))
    yield
    units = [(b, h) for b in range(len(ps)) for h in range(H)]
    parts = []
    for i in range(7):
        parts.append([msk[i] * mmx(_head(qs[b][i], h), _head(ks[b][i], h), "nt", P_HG["lvl"]) for b, h in units])
        yield
    A = [functools.reduce(lambda x, y: x + y, [parts[i][n] for i in range(7)]) for n in range(len(units))]
    qS = [mmx(_head(qG[b], h), St[n], "nt", P_HG["qs"]) for n, (b, h) in enumerate(units)]
    Sn = [St[n] * _head(eGl[b], h) + mmx(_head(vs[b], h), _head(kR[b], h), "tn", P_HG["su"])
          for n, (b, h) in enumerate(units)]
    yield
    outs = [mmx(A[n], _head(vs[b], h), "nn", P_HG["av"]) + qS[n] for n, (b, h) in enumerate(units)]
    return tuple(Sn), tuple(jnp.concatenate(outs[b * H:(b + 1) * H], axis=1) for b in range(len(ps)))


@jax.custom_vjp
def use_inverse(A, T):
    return T


def _use_inverse_fwd(A, T):
    return T, T


def _use_inverse_bwd(T, g):
    return -_dg(T, _dg(g, T, "nt", P_GD["inv"]), "tn", P_GD["inv"]), jnp.zeros_like(T)


use_inverse.defvjp(_use_inverse_fwd, _use_inverse_bwd)


def gd_stages(S, cs, abs_, alog, dtb, t_saved=None):
    t, r = _iota2(C, C)
    tri = (r <= t).astype(bf16)
    ups = (r > t).astype(bf16)
    lane = lax.broadcasted_iota(jnp.int32, (1, DK), 1)
    subl = lax.broadcasted_iota(jnp.int32, (8, 1), 0)
    eye = (t == r).astype(f32)
    strict = (r < t).astype(f32)
    bd = ((t >> 4) == (r >> 4)).astype(f32)
    qa, ka, va, b4, gam4, grev4, gam4T, glast4 = [], [], [], [], [], [], [], []
    for c, ab in zip(cs, abs_):
        qa.append(silu(c[:, 0:HD]))
        ka.append(silu(c[:, HD:2 * HD]))
        va.append(silu(c[:, 2 * HD:3 * HD]))
        g4 = -jnp.exp(alog) * softplus(ab + dtb)
        b4.append(sigmoid(ab))
        gam4.append(mask_mm(tri, g4, 2))
        grev4.append(mask_mm(ups, g4, 2))
        gam4T.append(gam4[-1].T)
        glast4.append(jnp.sum(g4, axis=0, keepdims=True))
    yield
    units = [(b, h) for b in range(len(cs)) for h in range(H)]
    nu = range(len(units))
    inv = lambda a, b: [mmx(a[n], b[n], "nn", P_GD["inv"]) for n in nu]
    v = [_head(va[b], h) for b, h in units]
    q = [_head(qa[b], h) for b, h in units]
    k = [_head(ka[b], h) for b, h in units]
    q = [x * lax.rsqrt(jnp.sum(x * x, -1, keepdims=True) + EPS) * (DK ** -0.5) for x in q]
    k = [x * lax.rsqrt(jnp.sum(x * x, -1, keepdims=True) + EPS) for x in k]
    oh = [(lane == h).astype(f32) for h in range(H)]
    gam_c = [jnp.sum(gam4[b] * oh[h], -1, keepdims=True) for b, h in units]
    grev_c = [jnp.sum(grev4[b] * oh[h], -1, keepdims=True) for b, h in units]
    beta = [jnp.sum(b4[b] * (lane == h + H).astype(f32), -1, keepdims=True) for b, h in units]
    glast = [jnp.sum(glast4[b] * oh[h], -1, keepdims=True) for b, h in units]
    gam_r = [jnp.sum(gam4T[b][0:8, :] * (subl == h).astype(f32), axis=0, keepdims=True) for b, h in units]
    dec = [jnp.exp(jnp.where(r <= t, gam_c[n] - gam_r[n], -1e30)) for n in nu]
    egam = [jnp.exp(gam_c[n]) for n in nu]
    kk = [mmx(k[n], k[n], "nt", P_GD["kk"]) for n in nu]
    qk = [mmx(q[n], k[n], "nt", P_GD["qk"]) * dec[n] for n in nu]
    yield
    A = [beta[n] * kk[n] * dec[n] * strict for n in nu]
    Dg = [A[n] * bd for n in nu]
    L = [A[n] - Dg[n] for n in nu]
    if t_saved is None:
        ImD = [eye - Dg[n] for n in nu]
        D2 = inv(Dg, Dg)
        yield
        P1 = inv(ImD, [eye + x for x in D2])
        D4 = inv(D2, D2)
        yield
        P2 = inv(P1, [eye + x for x in D4])
        D8 = inv(D4, D4)
        yield
        M = inv(P2, [eye + x for x in D8])
        yield
        Nn = inv(M, L)
        yield
        N2 = inv(Nn, Nn)
        yield
        T1 = inv([eye - x for x in Nn], [eye + x for x in N2])
        yield
        Tinv = inv(T1, M)
        yield
    else:
        Tinv = [use_inverse(A[n], t_saved[b][:, h * DK:h * DK + C]) for n, (b, h) in enumerate(units)]
    rhs = [jnp.concatenate([beta[n] * v[n], (beta[n] * egam[n]) * k[n]], axis=1) for n in nu]
    sol = [mmx(Tinv[n], rhs[n], "nn", P_GD["sol"]) for n in nu]
    yield
    qwS = [mmx(jnp.concatenate([q[n] * egam[n], sol[n][:, DK:2 * DK]], axis=0), S[n], "nn", P_GD["ws"]) for n in nu]
    yield
    u = [sol[n][:, 0:DK] - qwS[n][C:2 * C] for n in nu]
    outs = [qwS[n][0:C] + mmx(qk[n], u[n], "nn", P_GD["o"]) for n in nu]
    Sn = [jnp.exp(glast[n]) * S[n] + mmx(k[n] * jnp.exp(grev_c[n]), u[n], "tn", P_GD["su"]) for n in nu]
    zpad = jnp.zeros((C, DK - C), f32)
    t_pack = tuple(jnp.concatenate([x for n in range(b * H, (b + 1) * H) for x in (lax.stop_gradient(Tinv[n]), zpad)],
                                   axis=1) for b in range(len(cs)))
    return tuple(Sn), tuple(jnp.concatenate(outs[b * H:(b + 1) * H], axis=1) for b in range(len(cs))), t_pack


def _chunk_index(tile_chunks, k):
    def index(i):
        chunk = tile_chunks * i + k
        b = chunk // NCH
        return jnp.maximum((SEQ // C) * b + chunk - NCH * b - 1, 0), 0
    return index


def _in_proj(xflat, head, norm_w, w4, conv4):
    tm = 384
    nck = tm // C
    W3 = 3 * HD

    def body(*refs):
        x_refs = refs[:nck]
        head_ref, nw_ref, w4_ref, cw_ref, h_ref, p_ref, ut_ref, cv_ref, w_ref, prev = refs[nck:]
        i = pl.program_id(0)

        @pl.when(i == 0)
        def _():
            prev[...] = jnp.zeros_like(prev)
            w_ref[PC - DK:PC, :] = jnp.zeros((DK, D), bf16)
            for q in range(4):
                w_ref[SHARD_COLS * q:SHARD_COLS * (q + 1), :] = w4_ref[q]

        blocks = []
        for k in range(nck):
            chunk = nck * i + k
            blocks.append(jnp.where(chunk - NCH * (chunk // NCH) == 0, head_ref[...], x_refs[k][...]))
        hval = jnp.concatenate(blocks, axis=0)
        h_ref[...] = hval
        u = rmsnorm(hval, nw_ref[...])
        ut_ref[...] = u.T.astype(bf16)
        p = bdot_nt(u, w_ref[...])
        p_ref[...] = p
        x = p[:, COL_QKV:COL_QKV + W3]
        xx = jnp.concatenate([prev[...], x], axis=0)
        y = cw_ref[3] * x
        for s in (1, 2, 3):
            y = y + cw_ref[3 - s] * pltpu.roll(xx, s, 0)[8:]
        row = i * tm + lax.broadcasted_iota(jnp.int32, (tm, 1), 0)
        tok = jnp.where(row >= TP, row - TP, row)
        cv_ref[...] = jnp.where(tok >= 8, y, 0.0)
        prev[...] = x[tm - 8:tm]

    return pl.pallas_call(
        body, name="in_proj", grid=(N // tm,),
        in_specs=[pl.BlockSpec((C, D), _chunk_index(nck, k)) for k in range(nck)]
        + [pl.BlockSpec((C, D), lambda i: (0, 0)), pl.BlockSpec((1, D), lambda i: (0, 0)),
           pl.BlockSpec((4, SHARD_COLS, D), lambda i: (0, 0, 0), pipeline_mode=pl.Buffered(1)),
           pl.BlockSpec((4, 1, W3), lambda i: (0, 0, 0))],
        out_specs=[pl.BlockSpec((tm, D), lambda i: (i, 0)), pl.BlockSpec((tm, PC), lambda i: (i, 0)),
                   pl.BlockSpec((D, tm), lambda i: (0, i)), pl.BlockSpec((tm, W3), lambda i: (i, 0)),
                   pl.BlockSpec((PC, D), lambda i: (0, 0), pipeline_mode=pl.Buffered(1))],
        out_shape=[jax.ShapeDtypeStruct((N, D), f32), jax.ShapeDtypeStruct((N, PC), f32),
                   jax.ShapeDtypeStruct((D, N), bf16), jax.ShapeDtypeStruct((N, W3), f32),
                   jax.ShapeDtypeStruct((PC, D), bf16)],
        scratch_shapes=[pltpu.VMEM((8, W3), f32)],
        compiler_params=_cp(("arbitrary",)),
    )(*[xflat] * nck, head, norm_w, w4, conv4)


NU = NB * H
_REV = lambda c: NCH - 1 - c
_FWD = lambda c: c


def _tok_spec(w, ix, col=0):
    return pl.BlockSpec((NB, C, w), lambda c: (0, ix(c), col))


def _state_spec(ix):
    return pl.BlockSpec((NB, 1, H, DK, DK), lambda c: (0, ix(c), 0, 0, 0))


def _row_spec(w):
    return pl.BlockSpec((1, w), lambda c: (0, 0))


def _rows(ref):
    return tuple(ref[b] for b in range(NB))


def _hg_extra_specs(ix):
    return [_row_spec(HD), _row_spec(HD)]


def _gd_extra_specs(ix):
    return [_tok_spec(DK, ix, COL_AB // DK), _row_spec(DK), _row_spec(DK)]


def _mix_fwd(proj3, cv, l0, l1, alog, dtb):
    def body(p_ref, c_ref, ab_ref, l0_ref, l1_ref, al_ref, db_ref, oh_ref, sh_ref, og_ref, sg_ref, t_ref, sth, stg):
        @pl.when(pl.program_id(0) == 0)
        def _():
            sth[...] = jnp.zeros_like(sth)
            stg[...] = jnp.zeros_like(stg)

        Sh = tuple(sth[n] for n in range(NU))
        Sg = tuple(stg[n] for n in range(NU))
        for n in range(NU):
            sh_ref[n // H, 0, n % H] = Sh[n]
            sg_ref[n // H, 0, n % H] = Sg[n]
        snh, oh, sng, og, tp = mix_chunk(Sh, _rows(p_ref), l0_ref[...], l1_ref[...],
                                         Sg, _rows(c_ref), _rows(ab_ref), al_ref[...], db_ref[...])
        for n in range(NU):
            sth[n] = snh[n]
            stg[n] = sng[n]
        for b in range(NB):
            oh_ref[b] = oh[b]
            og_ref[b] = og[b]
            t_ref[b] = tp[b]

    tok = jax.ShapeDtypeStruct((NB, TP, HD), f32)
    st = jax.ShapeDtypeStruct((NB, NCH, H, DK, DK), f32)
    return pl.pallas_call(
        body, name="mix_fwd", grid=(NCH,),
        in_specs=[_tok_spec(3 * HD, _FWD), _tok_spec(3 * HD, _FWD), _tok_spec(DK, _FWD, COL_AB // DK),
                  _row_spec(HD), _row_spec(HD), _row_spec(DK), _row_spec(DK)],
        out_specs=[_tok_spec(HD, _FWD), _state_spec(_FWD), _tok_spec(HD, _FWD), _state_spec(_FWD), _tok_spec(HD, _FWD)],
        out_shape=[tok, st, tok, st, tok],
        scratch_shapes=[pltpu.VMEM((NU, DK, DK), f32), pltpu.VMEM((NU, DK, DK), f32)],
        compiler_params=_cp(("arbitrary",)),
    )(proj3, cv, proj3, l0, l1, alog, dtb)


def _hg_bwd(proj3, l0, l1, s_saved, do):
    def body(p_ref, l0_ref, l1_ref, s_ref, do_ref, dp_ref, dl0_ref, dl1_ref, dst):
        @pl.when(pl.program_id(0) == 0)
        def _():
            dst[...] = jnp.zeros_like(dst)
            dl0_ref[...] = jnp.zeros_like(dl0_ref)
            dl1_ref[...] = jnp.zeros_like(dl1_ref)

        S = tuple(s_ref[n // H, 0, n % H] for n in range(NU))
        _, vjp = jax.vjp(hg_chunk, S, _rows(p_ref), l0_ref[...], l1_ref[...])
        dS, dp, dl0, dl1 = vjp((tuple(dst[n] for n in range(NU)), _rows(do_ref)))
        for n in range(NU):
            dst[n] = dS[n]
        for b in range(NB):
            dp_ref[b] = dp[b].astype(bf16)
        dl0_ref[...] += jnp.broadcast_to(dl0, (8, HD))
        dl1_ref[...] += jnp.broadcast_to(dl1, (8, HD))

    acc = pl.BlockSpec((8, HD), lambda c: (0, 0))
    return pl.pallas_call(
        body, name="hg_bwd", grid=(NCH,),
        in_specs=[_tok_spec(3 * HD, _REV)] + _hg_extra_specs(_REV) + [_state_spec(_REV), _tok_spec(HD, _REV)],
        out_specs=[_tok_spec(3 * HD, _REV), acc, acc],
        out_shape=[jax.ShapeDtypeStruct((NB, TP, 3 * HD), bf16), jax.ShapeDtypeStruct((8, HD), f32),
                   jax.ShapeDtypeStruct((8, HD), f32)],
        scratch_shapes=[pltpu.VMEM((NU, DK, DK), f32)],
        compiler_params=_cp(("arbitrary",)),
    )(proj3, l0, l1, s_saved, do)


def _gd_bwd(cv, proj3, alog, dtb, s_saved, t_saved, do):
    def body(c_ref, ab_ref, al_ref, db_ref, s_ref, t_ref, do_ref, dc_ref, dab_ref, dal_ref, ddb_ref, dst):
        @pl.when(pl.program_id(0) == 0)
        def _():
            dst[...] = jnp.zeros_like(dst)
            dal_ref[...] = jnp.zeros_like(dal_ref)
            ddb_ref[...] = jnp.zeros_like(ddb_ref)

        S = tuple(s_ref[n // H, 0, n % H] for n in range(NU))
        t_rows = _rows(t_ref)
        fn = lambda *a: gd_chunk(*a, t_saved=t_rows)[0:2]
        _, vjp = jax.vjp(fn, S, _rows(c_ref), _rows(ab_ref), al_ref[...], db_ref[...])
        dS, dc, dab, dal, ddb = vjp((tuple(dst[n] for n in range(NU)), _rows(do_ref)))
        for n in range(NU):
            dst[n] = dS[n]
        for b in range(NB):
            dc_ref[b] = dc[b]
            dab_ref[b] = dab[b].astype(bf16)
        dal_ref[...] += jnp.broadcast_to(dal, (8, DK))
        ddb_ref[...] += jnp.broadcast_to(ddb, (8, DK))

    acc = pl.BlockSpec((8, DK), lambda c: (0, 0))
    return pl.pallas_call(
        body, name="gd_bwd", grid=(NCH,),
        in_specs=[_tok_spec(3 * HD, _REV)] + _gd_extra_specs(_REV)
        + [_state_spec(_REV), _tok_spec(HD, _REV), _tok_spec(HD, _REV)],
        out_specs=[_tok_spec(3 * HD, _REV), _tok_spec(DK, _REV), acc, acc],
        out_shape=[jax.ShapeDtypeStruct((NB, TP, 3 * HD), f32), jax.ShapeDtypeStruct((NB, TP, DK), bf16),
                   jax.ShapeDtypeStruct((8, DK), f32), jax.ShapeDtypeStruct((8, DK), f32)],
        scratch_shapes=[pltpu.VMEM((NU, DK, DK), f32)],
        compiler_params=_cp(("arbitrary",)),
    )(cv, proj3, alog, dtb, s_saved, t_saved, do)


def _conv_bwd(proj3, conv4, dy):
    def body(x_ref, w_ref, dy_ref, dx_ref, dw_ref):
        @pl.when(pl.program_id(1) == 0)
        def _():
            dw_ref[...] = jnp.zeros_like(dw_ref)

        x = x_ref[0]
        row = lax.broadcasted_iota(jnp.int32, (TP, 1), 0)
        g = jnp.where(row >= 8, dy_ref[0], 0.0)
        dx = w_ref[3] * g
        dw_ref[3] += jnp.broadcast_to(jnp.sum(x * g, axis=0, keepdims=True), (8, HD))
        for s in (1, 2, 3):
            dx = dx + w_ref[3 - s] * pltpu.roll(g, TP - s, 0)
            dw_ref[3 - s] += jnp.broadcast_to(jnp.sum(pltpu.roll(x, s, 0) * g, axis=0, keepdims=True), (8, HD))
        dx_ref[0] = dx.astype(bf16)

    return pl.pallas_call(
        body, name="conv_bwd", grid=(3, NB),
        in_specs=[pl.BlockSpec((1, TP, HD), lambda j, b: (b, 0, COL_QKV // HD + j)),
                  pl.BlockSpec((4, 1, HD), lambda j, b: (0, 0, j)), pl.BlockSpec((1, TP, HD), lambda j, b: (b, 0, j))],
        out_specs=[pl.BlockSpec((1, TP, HD), lambda j, b: (b, 0, j)), pl.BlockSpec((4, 8, HD), lambda j, b: (0, 0, j))],
        out_shape=[jax.ShapeDtypeStruct((NB, TP, 3 * HD), bf16), jax.ShapeDtypeStruct((4, 8, 3 * HD), f32)],
        compiler_params=_cp(("arbitrary", "arbitrary")),
    )(proj3, conv4, dy)


def _out_loss(o_hg, o_gd, proj, hgw, gdw, wout, hflat, fw, target):
    tm = 384

    def body(ohg_ref, ogd_ref, zhg_ref, zgd_ref, hgw_ref, gdw_ref, wo_ref, h_ref, fw_ref, *refs):
        tg_refs = refs[:tm // C]
        (loss_ref, dohg_ref, dogd_ref, dzhg_ref, dzgd_ref, dh_ref, dwo_ref, dhgw_ref, dgdw_ref, dfw_ref) = refs[tm // C:]
        i = pl.program_id(0)

        @pl.when(i == 0)
        def _():
            for r in (loss_ref, dwo_ref, dhgw_ref, dgdw_ref, dfw_ref):
                r[...] = jnp.zeros_like(r)

        row = i * tm + lax.broadcasted_iota(jnp.int32, (tm, 1), 0)
        tok = jnp.where(row >= TP, row - TP, row)
        valid = (tok >= PAD + N_META).astype(f32)
        hval = h_ref[...]
        tgt = jnp.concatenate([r[...] for r in tg_refs], axis=0)

        mixers = ((ohg_ref, zhg_ref, hgw_ref[...]), (ogd_ref, zgd_ref, gdw_ref[...]))
        saved, ys = [], []
        for o_ref, z_ref, w in mixers:
            for hh in range(H):
                sl = slice(hh * DK, (hh + 1) * DK)
                o, z = o_ref[:, sl], z_ref[:, sl]
                r = lax.rsqrt(jnp.mean(o * o, axis=-1, keepdims=True) + EPS)
                n = o * r
                sg = sigmoid(z)
                ws = w * (z * sg)
                saved.append((r, n, sg, z, ws, w))
                ys.append(n * ws)
        y = jnp.concatenate(ys, axis=-1)
        h2 = hval + bdot(y, wo_ref[...])
        r2 = lax.rsqrt(jnp.mean(h2 * h2, axis=-1, keepdims=True) + EPS)
        n2 = h2 * r2
        fwv = fw_ref[...]
        err = (n2 * fwv - tgt) * valid
        loss = (0.5 / D) * jnp.sum(err * err)
        dyf = err * (1.0 / D)
        dn2 = dyf * fwv
        dout = r2 * (dn2 - n2 * jnp.mean(dn2 * n2, axis=-1, keepdims=True))
        dh_ref[...] = dout
        dy = bdot_nt(dout, wo_ref[...])
        dwo_ref[...] += bdot_tn(y, dout)
        dws = []
        for mi, (do_ref, dz_ref) in enumerate(((dohg_ref, dzhg_ref), (dogd_ref, dzgd_ref))):
            dw = jnp.zeros((1, DK), f32)
            for hh in range(H):
                sl = slice(hh * DK, (hh + 1) * DK)
                r, n, sg, z, ws, w = saved[mi * H + hh]
                dyh = dy[:, mi * HD + hh * DK:mi * HD + (hh + 1) * DK]
                t = dyh * n
                dw = dw + jnp.sum(t * (z * sg), axis=0, keepdims=True)
                dz_ref[:, sl] = (t * w * (sg * (1.0 + z * (1.0 - sg)))).astype(bf16)
                dn = dyh * ws
                do_ref[:, sl] = r * (dn - n * jnp.mean(dn * n, axis=-1, keepdims=True))
            dws.append(dw)
        loss_ref[...] += jnp.broadcast_to(loss, (8, DK))
        dhgw_ref[...] += jnp.broadcast_to(dws[0], (8, DK))
        dgdw_ref[...] += jnp.broadcast_to(dws[1], (8, DK))
        dfw_ref[...] += jnp.broadcast_to(jnp.sum(dyf * n2, axis=0, keepdims=True), (8, D))

    row = lambda w: pl.BlockSpec((tm, w), lambda i: (i, 0))
    whole = lambda r, w: pl.BlockSpec((r, w), lambda i: (0, 0))
    col = lambda c0: pl.BlockSpec((tm, HD), lambda i: (i, c0 // HD))

    tgt_spec = lambda k: pl.BlockSpec((C, D), _chunk_index(tm // C, k))
    return pl.pallas_call(
        body, name="out_loss", grid=(N // tm,),
        in_specs=[row(HD), row(HD), col(COL_ZHG), col(COL_ZGD),
                  whole(1, DK), whole(1, DK), whole(D, D), row(D), whole(1, D)] + [tgt_spec(k) for k in range(tm // C)],
        out_specs=[whole(8, DK), row(HD), row(HD), row(HD), row(HD), row(D), whole(D, D),
                   whole(8, DK), whole(8, DK), whole(8, D)],
        out_shape=[jax.ShapeDtypeStruct((8, DK), f32)] + [jax.ShapeDtypeStruct((N, HD), f32)] * 2
        + [jax.ShapeDtypeStruct((N, HD), bf16)] * 2
        + [jax.ShapeDtypeStruct((N, D), f32), jax.ShapeDtypeStruct((D, D), f32),
           jax.ShapeDtypeStruct((8, DK), f32), jax.ShapeDtypeStruct((8, DK), f32), jax.ShapeDtypeStruct((8, D), f32)],
        compiler_params=_cp(("arbitrary",)),
    )(o_hg, o_gd, proj, proj, hgw, gdw, wout, hflat, fw, *[target] * (tm // C))


def _in_bwd(pieces, wbig, hflat, norm_w, dh_res, pbs):
    tm = 384
    nsteps = N // tm
    np_ = len(pieces)
    na = len(pbs)
    offs = [c0 for _, c0 in pieces]
    widths = [d.shape[1] for d, _ in pieces]

    def body(*refs):
        d_refs = refs[:np_]
        w_ref, h_ref, nw_ref, dhr_ref = refs[np_:np_ + 4]
        srcs = refs[np_ + 4:np_ + 4 + na]
        dh_ref, dnw_ref = refs[np_ + 4 + na:np_ + 6 + na]
        dsts = refs[np_ + 6 + na:np_ + 6 + 2 * na]
        sems = refs[np_ + 6 + 2 * na:]
        i = pl.program_id(0)

        def copies():
            if not na:
                return []
            x, y, c, chips = _place()
            return [pltpu.make_async_remote_copy(
                src_ref=srcs[a].at[2 * px + py], dst_ref=dsts[a].at[j], send_sem=sems[0].at[na * j + a],
                recv_sem=sems[1].at[na * j + a], device_id=(px, py, c), device_id_type=MESH)
                for j, (px, py) in enumerate(chips) for a in range(na)]

        @pl.when(i == 0)
        def _():
            dnw_ref[...] = jnp.zeros_like(dnw_ref)
            for cp in copies():
                cp.start()

        du = jnp.zeros((tm, D), f32)
        for d_ref, off, wd in zip(d_refs, offs, widths):
            du = du + bdot(d_ref[...], w_ref[off:off + wd, :])
        _, vjp = jax.vjp(rmsnorm, h_ref[...], nw_ref[...])
        dh, dnw = vjp(du)
        dh_ref[...] = dh + dhr_ref[...]
        dnw_ref[...] += jnp.broadcast_to(dnw, (8, D))

        @pl.when(i == nsteps - 1)
        def _():
            for cp in copies():
                cp.wait()

    row = lambda w: pl.BlockSpec((tm, w), lambda i: (i, 0))
    return pl.pallas_call(
        body, name="in_bwd", grid=(nsteps,),
        in_specs=[row(w) for w in widths]
        + [pl.BlockSpec((PC, D), lambda i: (0, 0)), row(D), pl.BlockSpec((1, D), lambda i: (0, 0)), row(D)] + [ANY] * na,
        out_specs=[row(D), pl.BlockSpec((8, D), lambda i: (0, 0))] + [ANY] * na,
        out_shape=[jax.ShapeDtypeStruct((N, D), f32), jax.ShapeDtypeStruct((8, D), f32)]
        + [jax.ShapeDtypeStruct((3,) + p.shape[1:], p.dtype) for p in pbs],
        scratch_shapes=[pltpu.SemaphoreType.DMA((3 * na,)), pltpu.SemaphoreType.DMA((3 * na,))] if na else [],
        compiler_params=_cp(("arbitrary",)),
    )(*[d for d, _ in pieces], wbig, hflat, norm_w, dh_res, *pbs)


def _w_grad(ut, pieces):
    tk = N // 3
    offs = [c0 for _, c0 in pieces]
    widths = [d.shape[1] for d, _ in pieces]

    def body(u_ref, *refs):
        d_refs, o_ref = refs[:-1], refs[-1]

        @pl.when(pl.program_id(0) == 0)
        def _():
            o_ref[...] = jnp.zeros_like(o_ref)

        u = u_ref[...]
        for d_ref, off, wd in zip(d_refs, offs, widths):
            o_ref[:, off:off + wd] += jnp.dot(u, d_ref[...], preferred_element_type=f32)

    return pl.pallas_call(
        body, name="w_grad", grid=(N // tk,),
        in_specs=[pl.BlockSpec((D, tk), lambda k: (0, k))] + [pl.BlockSpec((tk, w), lambda k: (k, 0)) for w in widths],
        out_specs=pl.BlockSpec((D, PC), lambda k: (0, 0), pipeline_mode=pl.Buffered(1)),
        out_shape=jax.ShapeDtypeStruct((D, PC), f32),
        compiler_params=_cp(("arbitrary",)),
    )(ut, *[d for d, _ in pieces])


def _adam_math(g, w, m, v):
    m2 = ADAM_B1 * m + (1.0 - ADAM_B1) * g
    v2 = ADAM_B2 * v + (1.0 - ADAM_B2) * (g * g)
    m_hat = m2 / (1.0 - ADAM_B1 ** ADAM_STEP)
    v_hat = v2 / (1.0 - ADAM_B2 ** ADAM_STEP)
    delta = -ADAM_LR * (m_hat / (jnp.sqrt(v_hat) + ADAM_EPS) + ADAM_WD * w)
    return delta, m2, v2


def _adamw(gs, w, m, v, name):
    R, Cc = w.shape
    tr = 256 if R % 256 == 0 else R
    ng = len(gs)

    def body(*refs):
        g = refs[0][...]
        for r in refs[1:ng]:
            g = g + r[...]
        w_ref, m_ref, v_ref, g_ref, d_ref, m2_ref, v2_ref = refs[ng:]
        delta, m2, v2 = _adam_math(g, w_ref[...], m_ref[...], v_ref[...])
        g_ref[...] = g
        d_ref[...] = delta
        m2_ref[...] = m2
        v2_ref[...] = v2

    spec = pl.BlockSpec((tr, Cc), lambda i: (i, 0))
    return pl.pallas_call(
        body, name=name, grid=(R // tr,),
        in_specs=[spec] * (ng + 3), out_specs=[spec] * 4,
        out_shape=[jax.ShapeDtypeStruct((R, Cc), f32)] * 4,
        compiler_params=_cp(("arbitrary",)),
    )(*gs, w, m, v)


def _adamw_rows(g, w, m, v, name):
    R, _, Cc = w.shape
    tr = R // 9

    def body(g_ref, w_ref, m_ref, v_ref, go_ref, d_ref, m2_ref, v2_ref):
        g = g_ref[...]
        delta, m2, v2 = _adam_math(g, w_ref[...], m_ref[...], v_ref[...])
        go_ref[...] = g
        d_ref[...] = delta
        m2_ref[...] = m2
        v2_ref[...] = v2

    spec = pl.BlockSpec((tr, 1, Cc), lambda i: (i, 0, 0))
    return pl.pallas_call(
        body, name=name, grid=(R // tr,),
        in_specs=[spec] * 4, out_specs=[spec] * 4,
        out_shape=[jax.ShapeDtypeStruct((R, 1, Cc), f32)] * 4,
        compiler_params=_cp(("arbitrary",)),
    )(g, w, m, v)


def _place():
    x, y, c = lax.axis_index("x"), lax.axis_index("y"), lax.axis_index("c")
    return x, y, c, [(1 - x, y), (x, 1 - y), (1 - x, 1 - y)]


def _gather_weights(cm, halved, whole):
    R, _, Cc = cm.shape
    hw = Cc // 2
    shards = [jax.ShapeDtypeStruct((R, Cc), bf16)] + list(halved) + list(whole)
    nh = 1 + len(halved)
    na = len(shards)

    def body(*refs):
        srcs, dsts = refs[:na], refs[na:2 * na]
        send_sems, recv_sems, loc_sems = refs[2 * na:2 * na + 3]
        stage = refs[2 * na + 3:3 * na + 3]
        raw = refs[3 * na + 3]
        x, y, c, chips = _place()
        me = 2 * x + y
        loads = [pltpu.make_async_copy(srcs[0], raw, loc_sems.at[0])]
        loads += [pltpu.make_async_copy(srcs[i], stage[i], loc_sems.at[i]) for i in range(1, na)]
        locs = [pltpu.make_async_copy(v, d.at[me], loc_sems.at[i]) for i, (v, d) in enumerate(zip(stage, dsts))]
        for cp in loads:
            cp.start()

        def half_of(ref, i, half):
            return ref.at[:, pl.ds(pl.multiple_of(half * hw, hw), hw)] if i == 0 else ref.at[half]

        def ici(j, i, slot):
            px, py = chips[j]
            src = half_of(stage[0] if i == 0 else srcs[i], i, c) if i < nh else srcs[i]
            dst = half_of(dsts[i].at[slot], i, c) if i < nh else dsts[i].at[slot]
            return pltpu.make_async_remote_copy(
                src_ref=src, dst_ref=dst, send_sem=send_sems.at[na * j + i], recv_sem=recv_sems.at[na * j + i],
                device_id=(px, py, c), device_id_type=MESH)

        def d2d(j, i, half):
            px, py = chips[j]
            blk = half_of(dsts[i].at[2 * px + py], i, half)
            return pltpu.make_async_remote_copy(
                src_ref=blk, dst_ref=blk, send_sem=send_sems.at[3 * na + nh * j + i],
                recv_sem=recv_sems.at[3 * na + nh * j + i], device_id=(x, y, 1 - c), device_id_type=MESH)

        sends = [ici(j, i, me) for j in range(3) for i in range(1, na)]
        for cp in sends:
            cp.start()
        loads[0].wait()
        stage[0][...] = raw[:, 0, :].astype(bf16)
        first = [ici(j, 0, me) for j in range(3)]
        for cp in first:
            cp.start()
        sends += first
        locs[0].start()
        for ld, st in zip(loads[1:], locs[1:]):
            ld.wait()
            st.start()
        for j, (px, py) in enumerate(chips):
            for i in range(na):
                ici(j, i, 2 * px + py).wait_recv()
                if i < nh:
                    fwd = d2d(j, i, c)
                    fwd.start()
                    sends.append(fwd)
        for j in range(3):
            for i in range(nh):
                d2d(j, i, 1 - c).wait_recv()
        for cp in sends:
            cp.wait_send()
        for cp in locs:
            cp.wait()

    nsem = 3 * na + 3 * nh
    return pl.pallas_call(
        body, name="gather_weights",
        in_specs=[ANY] * na, out_specs=[ANY] * na,
        out_shape=[jax.ShapeDtypeStruct((4,) + s.shape, s.dtype) for s in shards],
        scratch_shapes=[pltpu.SemaphoreType.DMA((nsem,)), pltpu.SemaphoreType.DMA((nsem,)),
                        pltpu.SemaphoreType.DMA((na,))] + [pltpu.VMEM(s.shape, s.dtype) for s in shards]
        + [pltpu.VMEM(cm.shape, cm.dtype)],
        compiler_params=pltpu.CompilerParams(has_side_effects=True, vmem_limit_bytes=VMEM_LIMIT),
    )(cm, *halved, *whole)


def _swap_halves(gs):
    na = len(gs)
    jobs = [(i, q) for i in range(na) for q in range(gs[i].shape[0])]

    def body(*refs):
        srcs, dsts = refs[:na], refs[na:2 * na]
        send_sems, recv_sems = refs[2 * na:]
        x, y, c, _ = _place()
        cps = [pltpu.make_async_remote_copy(
            src_ref=srcs[i].at[q, 1 - c], dst_ref=dsts[i].at[q], send_sem=send_sems.at[k],
            recv_sem=recv_sems.at[k], device_id=(x, y, 1 - c), device_id_type=MESH)
            for k, (i, q) in enumerate(jobs)]
        for cp in cps:
            cp.start()
        for cp in cps:
            cp.wait()

    return pl.pallas_call(
        body, name="swap_halves",
        in_specs=[ANY] * na, out_specs=[ANY] * na,
        out_shape=[jax.ShapeDtypeStruct(g.shape[0:1] + g.shape[2:], g.dtype) for g in gs],
        scratch_shapes=[pltpu.SemaphoreType.DMA((len(jobs),)), pltpu.SemaphoreType.DMA((len(jobs),))],
        compiler_params=pltpu.CompilerParams(has_side_effects=True),
    )(*gs)


def _add_split(cm_arr, g, s):
    _, _, R, Cg = g.shape
    tr = 128

    def body(sc_ref, g_ref, s_ref, b_ref, o_ref):
        p = g_ref[0, 0] + s_ref[0]
        own = None
        for q in range(4):
            blk = p[:, SHARD_COLS * q:SHARD_COLS * (q + 1)]
            b_ref[q] = blk.astype(bf16)
            mine = jnp.where(sc_ref[1] == q, blk, 0.0)
            own = mine if own is None else own + mine
        o_ref[...] = own

    return pl.pallas_call(
        body, name="add_w_in",
        grid_spec=pltpu.PrefetchScalarGridSpec(
            num_scalar_prefetch=1, grid=(R // tr,),
            in_specs=[pl.BlockSpec((1, 1, tr, Cg), lambda i, sc: (0, sc[0], i, 0)),
                      pl.BlockSpec((1, tr, Cg), lambda i, sc: (0, i, 0))],
            out_specs=[pl.BlockSpec((4, tr, SHARD_COLS), lambda i, sc: (0, i, 0)),
                       pl.BlockSpec((tr, SHARD_COLS), lambda i, sc: (i, 0))]),
        out_shape=[jax.ShapeDtypeStruct((4, R, SHARD_COLS), bf16), jax.ShapeDtypeStruct((R, SHARD_COLS), f32)],
        compiler_params=_cp(("arbitrary",)),
    )(cm_arr, g, s)


def _add_halves(c_arr, g, s, name):
    Q, _, R, Cc = g.shape
    tr = min(R, 128)

    def body(c_ref, g_ref, s_ref, b_ref, f_ref):
        p = g_ref[0, 0] + s_ref[0]
        f_ref[0] = p
        b_ref[0] = p.astype(bf16)

    blk = pl.BlockSpec((1, tr, Cc), lambda q, i, cr: (q, i, 0))
    return pl.pallas_call(
        body, name=name,
        grid_spec=pltpu.PrefetchScalarGridSpec(
            num_scalar_prefetch=1, grid=(Q, R // tr),
            in_specs=[pl.BlockSpec((1, 1, tr, Cc), lambda q, i, cr: (q, cr[0], i, 0)), blk], out_specs=[blk, blk]),
        out_shape=[jax.ShapeDtypeStruct((Q, R, Cc), bf16), jax.ShapeDtypeStruct((Q, R, Cc), f32)],
        compiler_params=_cp(("arbitrary", "arbitrary")),
    )(c_arr, g, s)


_FLIPS = [(fx, fy, fc) for fx in (0, 1) for fy in (0, 1) for fc in (0, 1)][1:]


def _sum_blocks(own, r, name, transposed=False):
    R, Cc = own.shape
    tr = min(R, 256)

    def body(own_ref, r_ref, o_ref):
        acc = own_ref[...]
        for j in range(3):
            acc = acc + r_ref[j].astype(f32)
        o_ref[...] = acc.T if transposed else acc

    return pl.pallas_call(
        body, name=name, grid=(R // tr,),
        in_specs=[pl.BlockSpec((tr, Cc), lambda i: (i, 0)), pl.BlockSpec((3, tr, Cc), lambda i: (0, i, 0))],
        out_specs=pl.BlockSpec((Cc, tr), lambda i: (0, i)) if transposed else pl.BlockSpec((tr, Cc), lambda i: (i, 0)),
        out_shape=jax.ShapeDtypeStruct((Cc, R) if transposed else (R, Cc), f32),
        compiler_params=_cp(("arbitrary",)),
    )(own, r)


def _sum_packs(me8_arr, pack, rp):
    R = pack.shape[0]

    def body(me_ref, pk_ref, rp_ref, o_ref):
        me8 = me_ref[0]
        acc = None
        for d in range(8):
            rel = d ^ me8
            term = jnp.where(rel == 0, pk_ref[...], rp_ref[jnp.maximum(rel - 1, 0)])
            acc = term if acc is None else acc + term
        o_ref[...] = acc

    return pl.pallas_call(
        body, name="sum_packs",
        grid_spec=pltpu.PrefetchScalarGridSpec(
            num_scalar_prefetch=1, grid=(1,),
            in_specs=[pl.BlockSpec((R, 128), lambda i, mr: (0, 0)), pl.BlockSpec((7, R, 128), lambda i, mr: (0, 0, 0))],
            out_specs=pl.BlockSpec((R, 128), lambda i, mr: (0, 0))),
        out_shape=jax.ShapeDtypeStruct((R, 128), f32),
        compiler_params=_cp(("arbitrary",)),
    )(me8_arr, pack, rp)


def _swap_finished(fs, pack):
    na = len(fs)
    R = pack.shape[0]

    def body(*refs):
        srcs, pk = refs[:na], refs[na]
        dsts, rp = refs[na + 1:2 * na + 1], refs[2 * na + 1]
        send_sems, recv_sems = refs[2 * na + 2:]
        x, y, c, _ = _place()
        cps = [pltpu.make_async_remote_copy(
            src_ref=srcs[i], dst_ref=dsts[i], send_sem=send_sems.at[i], recv_sem=recv_sems.at[i],
            device_id=(x, y, 1 - c), device_id_type=MESH) for i in range(na)]
        cps += [pltpu.make_async_remote_copy(
            src_ref=pk, dst_ref=rp.at[k], send_sem=send_sems.at[na + k], recv_sem=recv_sems.at[na + k],
            device_id=(x ^ fx, y ^ fy, c ^ fc), device_id_type=MESH) for k, (fx, fy, fc) in enumerate(_FLIPS)]
        for cp in cps:
            cp.start()
        for cp in cps:
            cp.wait()

    return pl.pallas_call(
        body, name="swap_finished",
        in_specs=[ANY] * (na + 1), out_specs=[ANY] * (na + 1),
        out_shape=[jax.ShapeDtypeStruct(f.shape, f.dtype) for f in fs] + [jax.ShapeDtypeStruct((7, R, 128), f32)],
        scratch_shapes=[pltpu.SemaphoreType.DMA((na + 7,)), pltpu.SemaphoreType.DMA((na + 7,))],
        compiler_params=pltpu.CompilerParams(has_side_effects=True),
    )(*fs, pack)


def _adamw_halves(c_arr, mine, peer, w, m, v, name):
    _, R, Cc = w.shape
    tr = min(R, 256)

    def body(c_ref, mine_ref, peer_ref, w_ref, m_ref, v_ref, g_ref, d_ref, m2_ref, v2_ref):
        g = jnp.where(pl.program_id(0) == c_ref[0], mine_ref[...], peer_ref[...])
        delta, m2, v2 = _adam_math(g, w_ref[0], m_ref[0], v_ref[0])
        g_ref[0] = g
        d_ref[0] = delta
        m2_ref[0] = m2
        v2_ref[0] = v2

    half = pl.BlockSpec((tr, Cc), lambda hh, i, cr: (i, 0))
    full = pl.BlockSpec((1, tr, Cc), lambda hh, i, cr: (hh, i, 0))
    return pl.pallas_call(
        body, name=name,
        grid_spec=pltpu.PrefetchScalarGridSpec(
            num_scalar_prefetch=1, grid=(2, R // tr), in_specs=[half, half, full, full, full], out_specs=[full] * 4),
        out_shape=[jax.ShapeDtypeStruct((2, R, Cc), f32)] * 4,
        compiler_params=_cp(("arbitrary", "arbitrary")),
    )(c_arr, mine, peer, w, m, v)


def _rows8(a):
    flat = a.reshape(-1)
    n = flat.shape[0]
    rows = -(-n // 1024) * 8
    return jnp.pad(flat, (0, rows * 128 - n)).reshape(rows, 128)


def kernel(x, meta_tokens, norm_w, w_in, conv_w, hg_lb_logits, hg_norm_w, gdn_A_log, gdn_dt_bias, gdn_norm_w, w_out, final_norm_w, loss_target, m_meta_tokens, m_norm_w, m_w_in, m_conv_w, m_hg_lb_logits, m_hg_norm_w, m_gdn_A_log, m_gdn_dt_bias, m_gdn_norm_w, m_w_out, m_final_norm_w, v_meta_tokens, v_norm_w, v_w_in, v_conv_w, v_hg_lb_logits, v_hg_norm_w, v_gdn_A_log, v_gdn_dt_bias, v_gdn_norm_w, v_w_out, v_final_norm_w):
    me = 2 * lax.axis_index("x") + lax.axis_index("y")

    g_win, g_wout, g_conv, g_meta = _gather_weights(
        jnp.transpose(w_in, (2, 0, 1)), [w_out[0].astype(bf16).reshape(2, D // 8, D)], [conv_w[0], meta_tokens])
    wout_full = g_wout.reshape(D, D)
    conv4 = jnp.transpose(g_conv, (1, 0, 2)).reshape(4, 1, 3 * HD)
    meta_full = jnp.transpose(g_meta, (1, 0, 2)).reshape(N_META, D)

    c_arr = lax.axis_index("c").reshape(1).astype(jnp.int32)

    def chip_partials(gw, g_wout_part):
        g_in2 = gw.reshape(1, 2, D // 2, PC)
        g_out4 = g_wout_part.reshape(4, 2, D // 8, D)
        s_in, s_out = _swap_halves([g_in2, g_out4])
        pb_blocks, own_in = _add_split(jnp.concatenate([c_arr, me.reshape(1).astype(jnp.int32)]), g_in2, s_in)
        pb_out, pf_out = _add_halves(c_arr, g_out4, s_out, "add_w_out")
        own_out = lax.dynamic_index_in_dim(pf_out, me, axis=0, keepdims=False)
        return [pb_blocks, pb_out], [own_in, own_out]

    (loss8, grad_x, d_meta, d_nw, d_conv, d_lb, d_hgw, d_alog, d_dtb, d_gdw, d_fw, pfs, rs) = _local_step(
        x, loss_target, g_win, wout_full, conv4, meta_full, norm_w, hg_lb_logits, hg_norm_w, gdn_A_log, gdn_dt_bias,
        gdn_norm_w, final_norm_w, chip_partials)

    pack = jnp.concatenate([
        loss8, d_nw[0].reshape(8, 128), d_lb.reshape(8, 128), d_hgw, _rows8(d_alog[0, :H]), _rows8(d_dtb[0, :H]),
        d_gdw, d_fw[0].reshape(8, 128), d_meta.reshape(128, 128), d_conv.reshape(48, 128)], axis=0)
    return _reduce_and_update(
        me, c_arr, grad_x, pfs, rs, pack, meta_tokens, norm_w, w_in, conv_w, hg_lb_logits, hg_norm_w, gdn_A_log,
        gdn_dt_bias, gdn_norm_w, w_out, final_norm_w, m_meta_tokens, m_norm_w, m_w_in, m_conv_w, m_hg_lb_logits,
        m_hg_norm_w, m_gdn_A_log, m_gdn_dt_bias, m_gdn_norm_w, m_w_out, m_final_norm_w, v_meta_tokens, v_norm_w, v_w_in,
        v_conv_w, v_hg_lb_logits, v_hg_norm_w, v_gdn_A_log, v_gdn_dt_bias, v_gdn_norm_w, v_w_out, v_final_norm_w)


def _local_step(x, loss_target, w4, wout_full, conv4, meta_full, norm_w, hg_lb_logits, hg_norm_w, gdn_A_log, gdn_dt_bias,
                gdn_norm_w, final_norm_w, chip_partials):
    head = jnp.concatenate([jnp.zeros((PAD, D), f32), meta_full], axis=0)
    target = loss_target.reshape(NB * SEQ, D)
    l0, l1 = hg_lb_logits[0:1], hg_lb_logits[1:2]
    alog = jnp.pad(gdn_A_log, ((0, 0), (0, DK - H)))
    dtb = jnp.pad(gdn_dt_bias, ((0, 0), (0, DK - H)))
    fw = final_norm_w.reshape(1, D)

    hflat, proj, ut, cv2, wbig = _in_proj(x.reshape(NB * SEQ, D), head, norm_w, w4, conv4)
    proj3 = proj.reshape(NB, TP, PC)
    cv = cv2.reshape(NB, TP, 3 * HD)
    o_hg, s_hg, o_gd, s_gd, t_gd = _mix_fwd(proj3, cv, l0, l1, alog, dtb)
    (loss8, d_ohg, d_ogd, d_zhg, d_zgd, dh_res, g_wout_part, d_hgw, d_gdw, d_fw) = _out_loss(
        o_hg.reshape(N, HD), o_gd.reshape(N, HD), proj, hg_norm_w, gdn_norm_w, wout_full, hflat, fw, target)
    d_hg, d_l0, d_l1 = _hg_bwd(proj3, l0, l1, s_hg, d_ohg.reshape(NB, TP, HD))
    d_cv, d_ab, d_alog, d_dtb = _gd_bwd(cv, proj3, alog, dtb, s_gd, t_gd, d_ogd.reshape(NB, TP, HD))
    d_qkv, d_conv4 = _conv_bwd(proj3, conv4, d_cv)
    d_hg2, d_qkv2, d_ab2 = d_hg.reshape(N, 3 * HD), d_qkv.reshape(N, 3 * HD), d_ab.reshape(N, DK)
    pieces = [(d_hg2, COL_HG), (d_zhg, COL_ZHG), (d_qkv2, COL_QKV), (d_zgd, COL_ZGD), (d_ab2, COL_AB)]
    gw = _w_grad(ut, pieces)
    pbs, pfs = chip_partials(gw, g_wout_part) if chip_partials else ([], [gw, g_wout_part])
    dh, d_nw, *rs = _in_bwd(pieces, wbig, hflat, norm_w, dh_res, pbs)

    dh3 = dh.reshape(NB, TP, D)
    grad_x = dh3[:, PAD + N_META:, :]
    d_meta = jnp.sum(dh3[:, PAD:PAD + N_META, :], axis=0)
    d_conv = d_conv4[:, 0, :]
    d_lb = jnp.concatenate([d_l0[0:1], d_l1[0:1]], axis=0)
    return loss8, grad_x, d_meta, d_nw, d_conv, d_lb, d_hgw, d_alog, d_dtb, d_gdw, d_fw, pfs, rs


def _reduce_and_update(me, c_arr, grad_x, pfs, rs, pack, meta_tokens, norm_w, w_in, conv_w, hg_lb_logits, hg_norm_w,
                       gdn_A_log, gdn_dt_bias, gdn_norm_w, w_out, final_norm_w, m_meta_tokens, m_norm_w, m_w_in, m_conv_w,
                       m_hg_lb_logits, m_hg_norm_w, m_gdn_A_log, m_gdn_dt_bias, m_gdn_norm_w, m_w_out, m_final_norm_w,
                       v_meta_tokens, v_norm_w, v_w_in, v_conv_w, v_hg_lb_logits, v_hg_norm_w, v_gdn_A_log, v_gdn_dt_bias,
                       v_gdn_norm_w, v_w_out, v_final_norm_w):
    (own_in, own_out), (r_in, r_out) = pfs, rs
    f_in = _sum_blocks(own_in, r_in, "sum_w_in", transposed=True)
    f_out = _sum_blocks(own_out, r_out, "sum_w_out")
    o_in, o_out, r_pack = _swap_finished([f_in, f_out], pack)
    me8_arr = (2 * me + lax.axis_index("c")).reshape(1).astype(jnp.int32)
    small = _sum_packs(me8_arr, pack, r_pack)

    half_out = lambda a: a[0].reshape(2, D // 8, D)
    is0 = lax.axis_index("c") == 0
    g_in = jnp.concatenate([jnp.where(is0, f_in, o_in), jnp.where(is0, o_in, f_in)], axis=1)
    to_cm = lambda a: jnp.transpose(a, (2, 0, 1))
    gi, di, mi, vi = [jnp.transpose(a, (1, 2, 0))[0] for a in _adamw_rows(
        g_in.reshape(SHARD_COLS, 1, D), to_cm(w_in), to_cm(m_w_in), to_cm(v_w_in), "adamw_w_in")]
    go, do_, mo, vo = [a.reshape(D // 4, D) for a in _adamw_halves(
        c_arr, f_out, o_out, half_out(w_out), half_out(m_w_out), half_out(v_w_out), "adamw_w_out")]

    g_meta_full = small[64:192].reshape(N_META, D)
    g_meta_loc = lax.dynamic_slice(g_meta_full, (0, me * 256), (N_META, 256))
    gm, dm, mm_, vm = _adamw([g_meta_loc], meta_tokens, m_meta_tokens, v_meta_tokens, "adamw_meta")
    g_conv_full = small[192:240].reshape(4, 1536)
    g_conv_loc = lax.dynamic_slice(g_conv_full, (0, me * 384), (4, 384))
    gc, dc, mc, vc = _adamw([g_conv_loc], conv_w[0], m_conv_w[0], v_conv_w[0], "adamw_conv")

    reps = [(norm_w, m_norm_w, v_norm_w), (hg_lb_logits, m_hg_lb_logits, v_hg_lb_logits),
            (hg_norm_w, m_hg_norm_w, v_hg_norm_w), (gdn_A_log, m_gdn_A_log, v_gdn_A_log),
            (gdn_dt_bias, m_gdn_dt_bias, v_gdn_dt_bias), (gdn_norm_w, m_gdn_norm_w, v_gdn_norm_w),
            (final_norm_w, m_final_norm_w, v_final_norm_w)]
    wp = jnp.concatenate([_rows8(t[0]) for t in reps], axis=0)
    mp = jnp.concatenate([_rows8(t[1]) for t in reps], axis=0)
    vp = jnp.concatenate([_rows8(t[2]) for t in reps], axis=0)
    gr, dr, mr, vr = _adamw([small[8:64]], wp, mp, vp, "adamw_small")

    def unpack(p):
        outs = []
        for i, t in enumerate(reps):
            n = t[0].size
            outs.append(p[8 * i:8 * i + 8].reshape(-1)[:n].reshape(t[0].shape))
        return outs

    def leaves(meta_v, conv_v, in_v, out_v, rep_p):
        nw, lb, hgw, al, db, gdw, fwv = unpack(rep_p)
        return [meta_v, nw, in_v[None], conv_v[None], lb, hgw, al, db, gdw, out_v[None], fwv]

    loss = small[0, 0]
    return (loss, grad_x, *leaves(gm, gc, gi, go, gr), *leaves(dm, dc, di, do_, dr),
            *leaves(mm_, mc, mi, mo, mr), *leaves(vm, vc, vi, vo, vr))
```

```python
import functools

import jax
import jax.numpy as jnp
from jax import lax
from jax.experimental import pallas as pl
from jax.experimental.pallas import tpu as pltpu

f32 = jnp.float32
bf16 = jnp.bfloat16
MESH = pl.DeviceIdType.MESH
ANY = pl.BlockSpec(memory_space=pl.ANY)

D = 1024
NB = 2
N_META = 16
SEQ = 2048
PAD = 48
TP = PAD + N_META + SEQ
C = 64
NCH = TP // C
N = NB * TP
H = 4
DK = 128
HD = H * DK
PC = 4224
IN_COLS = 4104
SHARD_COLS = IN_COLS // 4
COL_HG, COL_ZHG, COL_QKV, COL_ZGD, COL_AB = 0, 3 * HD, 4 * HD, 7 * HD, 8 * HD
EPS = 1e-6
ADAM_LR, ADAM_B1, ADAM_B2, ADAM_EPS, ADAM_WD, ADAM_STEP = 0.001, 0.9, 0.999, 1e-08, 0.01, 10
VMEM_LIMIT = 56 * 1024 * 1024

P_HG = dict(lvl=1, av=1, qs=1, su=1)
P_GD = dict(kk=1, inv=1, sol=1, ws=1, qk=1, o=1, su=1)


def _cp(sem=None, **kw):
    return pltpu.CompilerParams(dimension_semantics=sem, vmem_limit_bytes=VMEM_LIMIT, **kw)


_DIMS = {"nn": (((1,), (0,)), ((), ())), "nt": (((1,), (1,)), ((), ())), "tn": (((0,), (0,)), ((), ()))}


def _split(x):
    hi = x.astype(bf16)
    return hi, (x - hi.astype(f32)).astype(bf16)


def _dg(a, b, kind, passes):
    d = lambda x, y: lax.dot_general(x, y, _DIMS[kind], preferred_element_type=f32)
    if passes == 1:
        return d(a.astype(bf16), b.astype(bf16))
    ah, al = _split(a)
    bh, bl = _split(b)
    return d(ah, bh) + d(ah, bl) + d(al, bh)


@functools.partial(jax.custom_vjp, nondiff_argnums=(2, 3))
def mmx(a, b, kind, passes):
    return _dg(a, b, kind, passes)


def _mmx_fwd(a, b, kind, passes):
    return _dg(a, b, kind, passes), (a, b)


def _mmx_bwd(kind, passes, res, g):
    a, b = res
    if kind == "nn":
        return _dg(g, b, "nt", passes), _dg(a, g, "tn", passes)
    if kind == "nt":
        return _dg(g, b, "nn", passes), _dg(g, a, "tn", passes)
    return _dg(b, g, "nt", passes), _dg(a, g, "nn", passes)


mmx.defvjp(_mmx_fwd, _mmx_bwd)


def _mask_dg(mask, x):
    xh, xl = _split(x)
    return jnp.dot(jnp.concatenate([mask, mask], axis=1), jnp.concatenate([xh, xl], axis=0), preferred_element_type=f32)


@functools.partial(jax.custom_vjp, nondiff_argnums=(2,))
def mask_mm(mask, x, bwd_passes):
    return _mask_dg(mask, x)


def _mask_fwd(mask, x, bwd_passes):
    return _mask_dg(mask, x), mask


def _mask_bwd(bwd_passes, mask, g):
    d = lambda y: lax.dot_general(mask, y, _DIMS["tn"], preferred_element_type=f32)
    if bwd_passes == 1:
        return None, d(g.astype(bf16))
    gh, gl = _split(g)
    return None, d(gh) + d(gl)


mask_mm.defvjp(_mask_fwd, _mask_bwd)


def bdot(a, b):
    return jnp.dot(a.astype(bf16), b.astype(bf16), preferred_element_type=f32)


def bdot_nt(a, b):
    return lax.dot_general(a.astype(bf16), b.astype(bf16), _DIMS["nt"], preferred_element_type=f32)


def bdot_tn(a, b):
    return lax.dot_general(a.astype(bf16), b.astype(bf16), _DIMS["tn"], preferred_element_type=f32)


def _iota2(n, m):
    return lax.broadcasted_iota(jnp.int32, (n, m), 0), lax.broadcasted_iota(jnp.int32, (n, m), 1)


sigmoid = jax.nn.sigmoid


def silu(x):
    return x * sigmoid(x)


def softplus(x):
    return jnp.maximum(x, 0.0) + jnp.log(1.0 + jnp.exp(-jnp.abs(x)))


def rmsnorm(x, w):
    return x * lax.rsqrt(jnp.mean(x * x, axis=-1, keepdims=True) + EPS) * w


def hg_masks():
    t, r = _iota2(C, C)
    mats = [r <= t, r > t]
    lvl = []
    for l in range(1, 7):
        sz = 1 << l
        half = sz >> 1
        seg_t = t >> l
        upper_t = (t & (sz - 1)) >= half
        mid_t = seg_t * sz + half - 1
        mats.append((upper_t & (r > mid_t) & (r <= t)) | ((~upper_t) & (r > t) & (r <= mid_t)))
        lvl.append(((seg_t == (r >> l)) & upper_t & ((r & (sz - 1)) < half)).astype(f32))
    stk = jnp.concatenate([m.astype(bf16) for m in mats], axis=0)
    return stk, lvl, (t == r).astype(f32)


def _head(a, h):
    return a[:, h * DK:(h + 1) * DK]


def _run(*gens):
    results = [None] * len(gens)
    live = list(range(len(gens)))
    while live:
        for i in list(live):
            try:
                next(gens[i])
            except StopIteration as e:
                results[i] = e.value
                live.remove(i)
    return results


def hg_chunk(St, ps, l0, l1):
    return _run(hg_stages(St, ps, l0, l1))[0]


def gd_chunk(S, cs, abs_, alog, dtb, t_saved=None):
    return _run(gd_stages(S, cs, abs_, alog, dtb, t_saved))[0]


def mix_chunk(St, ps, l0, l1, S, cs, abs_, alog, dtb):
    (sn_h, o_h), (sn_g, o_g, t_pack) = _run(hg_stages(St, ps, l0, l1), gd_stages(S, cs, abs_, alog, dtb))
    return sn_h, o_h, sn_g, o_g, t_pack


def hg_stages(St, ps, l0, l1):
    m = jnp.maximum(l0, l1)
    e0 = jnp.exp(l0 - m)
    e1 = jnp.exp(l1 - m)
    lb = e0 / (e0 + e1)
    stk, lvl, eye = hg_masks()
    msk = [eye] + lvl
    trow = lax.broadcasted_iota(jnp.int32, (C, 1), 0)
    upper = [(trow & ((1 << l) - 1)) >= (1 << (l - 1)) for l in range(1, 7)]
    qs, ks, vs, qG, kR, eGl = [], [], [], [], [], []
    for p in ps:
        pq, pf, v = p[:, 0:HD], p[:, HD:2 * HD], p[:, 2 * HD:3 * HD]
        q = silu(pq)
        f = lb + (1.0 - lb) * sigmoid(pf)
        k = 1.0 - f
        logf = jnp.log(f)
        Dm = mask_mm(stk, logf, 1)
        z = [jnp.where(up, q, k) * jnp.exp(Dm[(2 + i) * C:(3 + i) * C]) for i, up in enumerate(upper)]
        qs.append([q] + z)
        ks.append([k] + z)
        vs.append(v)
        qG.append(q * jnp.exp(Dm[0:C]))
        kR.append(k * jnp.exp(Dm[C:2 * C]))
        eGl.append(jnp.exp(jnp.sum(logf, axis=0, keepdims=True)))
    yield
    units = [(b, h) for b in range(len(ps)) for h in range(H)]
    parts = []
    for i in range(7):
        parts.append([msk[i] * mmx(_head(qs[b][i], h), _head(ks[b][i], h), "nt", P_HG["lvl"]) for b, h in units])
        yield
    A = [functools.reduce(lambda x, y: x + y, [parts[i][n] for i in range(7)]) for n in range(len(units))]
    qS = [mmx(_head(qG[b], h), St[n], "nt", P_HG["qs"]) for n, (b, h) in enumerate(units)]
    Sn = [St[n] * _head(eGl[b], h) + mmx(_head(vs[b], h), _head(kR[b], h), "tn", P_HG["su"])
          for n, (b, h) in enumerate(units)]
    yield
    outs = [mmx(A[n], _head(vs[b], h), "nn", P_HG["av"]) + qS[n] for n, (b, h) in enumerate(units)]
    return tuple(Sn), tuple(jnp.concatenate(outs[b * H:(b + 1) * H], axis=1) for b in range(len(ps)))


@jax.custom_vjp
def use_inverse(A, T):
    return T


def _use_inverse_fwd(A, T):
    return T, T


def _use_inverse_bwd(T, g):
    return -_dg(T, _dg(g, T, "nt", P_GD["inv"]), "tn", P_GD["inv"]), jnp.zeros_like(T)


use_inverse.defvjp(_use_inverse_fwd, _use_inverse_bwd)


def gd_stages(S, cs, abs_, alog, dtb, t_saved=None):
    t, r = _iota2(C, C)
    tri = (r <= t).astype(bf16)
    ups = (r > t).astype(bf16)
    lane = lax.broadcasted_iota(jnp.int32, (1, DK), 1)
    subl = lax.broadcasted_iota(jnp.int32, (8, 1), 0)
    eye = (t == r).astype(f32)
    strict = (r < t).astype(f32)
    bd = ((t >> 4) == (r >> 4)).astype(f32)
    qa, ka, va, b4, gam4, grev4, gam4T, glast4 = [], [], [], [], [], [], [], []
    for c, ab in zip(cs, abs_):
        qa.append(silu(c[:, 0:HD]))
        ka.append(silu(c[:, HD:2 * HD]))
        va.append(silu(c[:, 2 * HD:3 * HD]))
        g4 = -jnp.exp(alog) * softplus(ab + dtb)
        b4.append(sigmoid(ab))
        gam4.append(mask_mm(tri, g4, 2))
        grev4.append(mask_mm(ups, g4, 2))
        gam4T.append(gam4[-1].T)
        glast4.append(jnp.sum(g4, axis=0, keepdims=True))
    yield
    units = [(b, h) for b in range(len(cs)) for h in range(H)]
    nu = range(len(units))
    inv = lambda a, b: [mmx(a[n], b[n], "nn", P_GD["inv"]) for n in nu]
    v = [_head(va[b], h) for b, h in units]
    q = [_head(qa[b], h) for b, h in units]
    k = [_head(ka[b], h) for b, h in units]
    q = [x * lax.rsqrt(jnp.sum(x * x, -1, keepdims=True) + EPS) * (DK ** -0.5) for x in q]
    k = [x * lax.rsqrt(jnp.sum(x * x, -1, keepdims=True) + EPS) for x in k]
    oh = [(lane == h).astype(f32) for h in range(H)]
    gam_c = [jnp.sum(gam4[b] * oh[h], -1, keepdims=True) for b, h in units]
    grev_c = [jnp.sum(grev4[b] * oh[h], -1, keepdims=True) for b, h in units]
    beta = [jnp.sum(b4[b] * (lane == h + H).astype(f32), -1, keepdims=True) for b, h in units]
    glast = [jnp.sum(glast4[b] * oh[h], -1, keepdims=True) for b, h in units]
    gam_r = [jnp.sum(gam4T[b][0:8, :] * (subl == h).astype(f32), axis=0, keepdims=True) for b, h in units]
    dec = [jnp.exp(jnp.where(r <= t, gam_c[n] - gam_r[n], -1e30)) for n in nu]
    egam = [jnp.exp(gam_c[n]) for n in nu]
    kk = [mmx(k[n], k[n], "nt", P_GD["kk"]) for n in nu]
    qk = [mmx(q[n], k[n], "nt", P_GD["qk"]) * dec[n] for n in nu]
    yield
    A = [beta[n] * kk[n] * dec[n] * strict for n in nu]
    Dg = [A[n] * bd for n in nu]
    L = [A[n] - Dg[n] for n in nu]
    if t_saved is None:
        ImD = [eye - Dg[n] for n in nu]
        D2 = inv(Dg, Dg)
        yield
        P1 = inv(ImD, [eye + x for x in D2])
        D4 = inv(D2, D2)
        yield
        P2 = inv(P1, [eye + x for x in D4])
        D8 = inv(D4, D4)
        yield
        M = inv(P2, [eye + x for x in D8])
        yield
        Nn = inv(M, L)
        yield
        N2 = inv(Nn, Nn)
        yield
        T1 = inv([eye - x for x in Nn], [eye + x for x in N2])
        yield
        Tinv = inv(T1, M)
        yield
    else:
        Tinv = [use_inverse(A[n], t_saved[b][:, h * DK:h * DK + C]) for n, (b, h) in enumerate(units)]
    rhs = [jnp.concatenate([beta[n] * v[n], (beta[n] * egam[n]) * k[n]], axis=1) for n in nu]
    sol = [mmx(Tinv[n], rhs[n], "nn", P_GD["sol"]) for n in nu]
    yield
    qwS = [mmx(jnp.concatenate([q[n] * egam[n], sol[n][:, DK:2 * DK]], axis=0), S[n], "nn", P_GD["ws"]) for n in nu]
    yield
    u = [sol[n][:, 0:DK] - qwS[n][C:2 * C] for n in nu]
    outs = [qwS[n][0:C] + mmx(qk[n], u[n], "nn", P_GD["o"]) for n in nu]
    Sn = [jnp.exp(glast[n]) * S[n] + mmx(k[n] * jnp.exp(grev_c[n]), u[n], "tn", P_GD["su"]) for n in nu]
    zpad = jnp.zeros((C, DK - C), f32)
    t_pack = tuple(jnp.concatenate([x for n in range(b * H, (b + 1) * H) for x in (lax.stop_gradient(Tinv[n]), zpad)],
                                   axis=1) for b in range(len(cs)))
    return tuple(Sn), tuple(jnp.concatenate(outs[b * H:(b + 1) * H], axis=1) for b in range(len(cs))), t_pack


def _chunk_index(tile_chunks, k):
    def index(i):
        chunk = tile_chunks * i + k
        b = chunk // NCH
        return jnp.maximum((SEQ // C) * b + chunk - NCH * b - 1, 0), 0
    return index


def _in_proj(xflat, head, norm_w, w4, conv4):
    tm = 384
    nck = tm // C
    W3 = 3 * HD

    def body(*refs):
        x_refs = refs[:nck]
        head_ref, nw_ref, w4_ref, cw_ref, h_ref, p_ref, ut_ref, cv_ref, w_ref, prev = refs[nck:]
        i = pl.program_id(0)

        @pl.when(i == 0)
        def _():
            prev[...] = jnp.zeros_like(prev)
            w_ref[PC - DK:PC, :] = jnp.zeros((DK, D), bf16)
            for q in range(4):
                w_ref[SHARD_COLS * q:SHARD_COLS * (q + 1), :] = w4_ref[q]

        blocks = []
        for k in range(nck):
            chunk = nck * i + k
            blocks.append(jnp.where(chunk - NCH * (chunk // NCH) == 0, head_ref[...], x_refs[k][...]))
        hval = jnp.concatenate(blocks, axis=0)
        h_ref[...] = hval
        u = rmsnorm(hval, nw_ref[...])
        ut_ref[...] = u.T.astype(bf16)
        p = bdot_nt(u, w_ref[...])
        p_ref[...] = p
        x = p[:, COL_QKV:COL_QKV + W3]
        xx = jnp.concatenate([prev[...], x], axis=0)
        y = cw_ref[3] * x
        for s in (1, 2, 3):
            y = y + cw_ref[3 - s] * pltpu.roll(xx, s, 0)[8:]
        row = i * tm + lax.broadcasted_iota(jnp.int32, (tm, 1), 0)
        tok = jnp.where(row >= TP, row - TP, row)
        cv_ref[...] = jnp.where(tok >= 8, y, 0.0)
        prev[...] = x[tm - 8:tm]

    return pl.pallas_call(
        body, name="in_proj", grid=(N // tm,),
        in_specs=[pl.BlockSpec((C, D), _chunk_index(nck, k)) for k in range(nck)]
        + [pl.BlockSpec((C, D), lambda i: (0, 0)), pl.BlockSpec((1, D), lambda i: (0, 0)),
           pl.BlockSpec((4, SHARD_COLS, D), lambda i: (0, 0, 0), pipeline_mode=pl.Buffered(1)),
           pl.BlockSpec((4, 1, W3), lambda i: (0, 0, 0))],
        out_specs=[pl.BlockSpec((tm, D), lambda i: (i, 0)), pl.BlockSpec((tm, PC), lambda i: (i, 0)),
                   pl.BlockSpec((D, tm), lambda i: (0, i)), pl.BlockSpec((tm, W3), lambda i: (i, 0)),
                   pl.BlockSpec((PC, D), lambda i: (0, 0), pipeline_mode=pl.Buffered(1))],
        out_shape=[jax.ShapeDtypeStruct((N, D), f32), jax.ShapeDtypeStruct((N, PC), f32),
                   jax.ShapeDtypeStruct((D, N), bf16), jax.ShapeDtypeStruct((N, W3), f32),
                   jax.ShapeDtypeStruct((PC, D), bf16)],
        scratch_shapes=[pltpu.VMEM((8, W3), f32)],
        compiler_params=_cp(("arbitrary",)),
    )(*[xflat] * nck, head, norm_w, w4, conv4)


NU = NB * H
_REV = lambda c: NCH - 1 - c
_FWD = lambda c: c


def _tok_spec(w, ix, col=0):
    return pl.BlockSpec((NB, C, w), lambda c: (0, ix(c), col))


def _state_spec(ix):
    return pl.BlockSpec((NB, 1, H, DK, DK), lambda c: (0, ix(c), 0, 0, 0))


def _row_spec(w):
    return pl.BlockSpec((1, w), lambda c: (0, 0))


def _rows(ref):
    return tuple(ref[b] for b in range(NB))


def _hg_extra_specs(ix):
    return [_row_spec(HD), _row_spec(HD)]


def _gd_extra_specs(ix):
    return [_tok_spec(DK, ix, COL_AB // DK), _row_spec(DK), _row_spec(DK)]


def _mix_fwd(proj3, cv, l0, l1, alog, dtb):
    def body(p_ref, c_ref, ab_ref, l0_ref, l1_ref, al_ref, db_ref, oh_ref, sh_ref, og_ref, sg_ref, t_ref, sth, stg):
        @pl.when(pl.program_id(0) == 0)
        def _():
            sth[...] = jnp.zeros_like(sth)
            stg[...] = jnp.zeros_like(stg)

        Sh = tuple(sth[n] for n in range(NU))
        Sg = tuple(stg[n] for n in range(NU))
        for n in range(NU):
            sh_ref[n // H, 0, n % H] = Sh[n]
            sg_ref[n // H, 0, n % H] = Sg[n]
        snh, oh, sng, og, tp = mix_chunk(Sh, _rows(p_ref), l0_ref[...], l1_ref[...],
                                         Sg, _rows(c_ref), _rows(ab_ref), al_ref[...], db_ref[...])
        for n in range(NU):
            sth[n] = snh[n]
            stg[n] = sng[n]
        for b in range(NB):
            oh_ref[b] = oh[b]
            og_ref[b] = og[b]
            t_ref[b] = tp[b]

    tok = jax.ShapeDtypeStruct((NB, TP, HD), f32)
    st = jax.ShapeDtypeStruct((NB, NCH, H, DK, DK), f32)
    return pl.pallas_call(
        body, name="mix_fwd", grid=(NCH,),
        in_specs=[_tok_spec(3 * HD, _FWD), _tok_spec(3 * HD, _FWD), _tok_spec(DK, _FWD, COL_AB // DK),
                  _row_spec(HD), _row_spec(HD), _row_spec(DK), _row_spec(DK)],
        out_specs=[_tok_spec(HD, _FWD), _state_spec(_FWD), _tok_spec(HD, _FWD), _state_spec(_FWD), _tok_spec(HD, _FWD)],
        out_shape=[tok, st, tok, st, tok],
        scratch_shapes=[pltpu.VMEM((NU, DK, DK), f32), pltpu.VMEM((NU, DK, DK), f32)],
        compiler_params=_cp(("arbitrary",)),
    )(proj3, cv, proj3, l0, l1, alog, dtb)


def _hg_bwd(proj3, l0, l1, s_saved, do):
    def body(p_ref, l0_ref, l1_ref, s_ref, do_ref, dp_ref, dl0_ref, dl1_ref, dst):
        @pl.when(pl.program_id(0) == 0)
        def _():
            dst[...] = jnp.zeros_like(dst)
            dl0_ref[...] = jnp.zeros_like(dl0_ref)
            dl1_ref[...] = jnp.zeros_like(dl1_ref)

        S = tuple(s_ref[n // H, 0, n % H] for n in range(NU))
        _, vjp = jax.vjp(hg_chunk, S, _rows(p_ref), l0_ref[...], l1_ref[...])
        dS, dp, dl0, dl1 = vjp((tuple(dst[n] for n in range(NU)), _rows(do_ref)))
        for n in range(NU):
            dst[n] = dS[n]
        for b in range(NB):
            dp_ref[b] = dp[b].astype(bf16)
        dl0_ref[...] += jnp.broadcast_to(dl0, (8, HD))
        dl1_ref[...] += jnp.broadcast_to(dl1, (8, HD))

    acc = pl.BlockSpec((8, HD), lambda c: (0, 0))
    return pl.pallas_call(
        body, name="hg_bwd", grid=(NCH,),
        in_specs=[_tok_spec(3 * HD, _REV)] + _hg_extra_specs(_REV) + [_state_spec(_REV), _tok_spec(HD, _REV)],
        out_specs=[_tok_spec(3 * HD, _REV), acc, acc],
        out_shape=[jax.ShapeDtypeStruct((NB, TP, 3 * HD), bf16), jax.ShapeDtypeStruct((8, HD), f32),
                   jax.ShapeDtypeStruct((8, HD), f32)],
        scratch_shapes=[pltpu.VMEM((NU, DK, DK), f32)],
        compiler_params=_cp(("arbitrary",)),
    )(proj3, l0, l1, s_saved, do)


def _gd_bwd(cv, proj3, alog, dtb, s_saved, t_saved, do):
    def body(c_ref, ab_ref, al_ref, db_ref, s_ref, t_ref, do_ref, dc_ref, dab_ref, dal_ref, ddb_ref, dst):
        @pl.when(pl.program_id(0) == 0)
        def _():
            dst[...] = jnp.zeros_like(dst)
            dal_ref[...] = jnp.zeros_like(dal_ref)
            ddb_ref[...] = jnp.zeros_like(ddb_ref)

        S = tuple(s_ref[n // H, 0, n % H] for n in range(NU))
        t_rows = _rows(t_ref)
        fn = lambda *a: gd_chunk(*a, t_saved=t_rows)[0:2]
        _, vjp = jax.vjp(fn, S, _rows(c_ref), _rows(ab_ref), al_ref[...], db_ref[...])
        dS, dc, dab, dal, ddb = vjp((tuple(dst[n] for n in range(NU)), _rows(do_ref)))
        for n in range(NU):
            dst[n] = dS[n]
        for b in range(NB):
            dc_ref[b] = dc[b]
            dab_ref[b] = dab[b].astype(bf16)
        dal_ref[...] += jnp.broadcast_to(dal, (8, DK))
        ddb_ref[...] += jnp.broadcast_to(ddb, (8, DK))

    acc = pl.BlockSpec((8, DK), lambda c: (0, 0))
    return pl.pallas_call(
        body, name="gd_bwd", grid=(NCH,),
        in_specs=[_tok_spec(3 * HD, _REV)] + _gd_extra_specs(_REV)
        + [_state_spec(_REV), _tok_spec(HD, _REV), _tok_spec(HD, _REV)],
        out_specs=[_tok_spec(3 * HD, _REV), _tok_spec(DK, _REV), acc, acc],
        out_shape=[jax.ShapeDtypeStruct((NB, TP, 3 * HD), f32), jax.ShapeDtypeStruct((NB, TP, DK), bf16),
                   jax.ShapeDtypeStruct((8, DK), f32), jax.ShapeDtypeStruct((8, DK), f32)],
        scratch_shapes=[pltpu.VMEM((NU, DK, DK), f32)],
        compiler_params=_cp(("arbitrary",)),
    )(cv, proj3, alog, dtb, s_saved, t_saved, do)


def _conv_bwd(proj3, conv4, dy):
    def body(x_ref, w_ref, dy_ref, dx_ref, dw_ref):
        @pl.when(pl.program_id(1) == 0)
        def _():
            dw_ref[...] = jnp.zeros_like(dw_ref)

        x = x_ref[0]
        row = lax.broadcasted_iota(jnp.int32, (TP, 1), 0)
        g = jnp.where(row >= 8, dy_ref[0], 0.0)
        dx = w_ref[3] * g
        dw_ref[3] += jnp.broadcast_to(jnp.sum(x * g, axis=0, keepdims=True), (8, HD))
        for s in (1, 2, 3):
            dx = dx + w_ref[3 - s] * pltpu.roll(g, TP - s, 0)
            dw_ref[3 - s] += jnp.broadcast_to(jnp.sum(pltpu.roll(x, s, 0) * g, axis=0, keepdims=True), (8, HD))
        dx_ref[0] = dx.astype(bf16)

    return pl.pallas_call(
        body, name="conv_bwd", grid=(3, NB),
        in_specs=[pl.BlockSpec((1, TP, HD), lambda j, b: (b, 0, COL_QKV // HD + j)),
                  pl.BlockSpec((4, 1, HD), lambda j, b: (0, 0, j)), pl.BlockSpec((1, TP, HD), lambda j, b: (b, 0, j))],
        out_specs=[pl.BlockSpec((1, TP, HD), lambda j, b: (b, 0, j)), pl.BlockSpec((4, 8, HD), lambda j, b: (0, 0, j))],
        out_shape=[jax.ShapeDtypeStruct((NB, TP, 3 * HD), bf16), jax.ShapeDtypeStruct((4, 8, 3 * HD), f32)],
        compiler_params=_cp(("arbitrary", "arbitrary")),
    )(proj3, conv4, dy)


def _out_loss(o_hg, o_gd, proj, hgw, gdw, wout, hflat, fw, target):
    tm = 384

    def body(ohg_ref, ogd_ref, zhg_ref, zgd_ref, hgw_ref, gdw_ref, wo_ref, h_ref, fw_ref, *refs):
        tg_refs = refs[:tm // C]
        (loss_ref, dohg_ref, dogd_ref, dzhg_ref, dzgd_ref, dh_ref, dwo_ref, dhgw_ref, dgdw_ref, dfw_ref) = refs[tm // C:]
        i = pl.program_id(0)

        @pl.when(i == 0)
        def _():
            for r in (loss_ref, dwo_ref, dhgw_ref, dgdw_ref, dfw_ref):
                r[...] = jnp.zeros_like(r)

        row = i * tm + lax.broadcasted_iota(jnp.int32, (tm, 1), 0)
        tok = jnp.where(row >= TP, row - TP, row)
        valid = (tok >= PAD + N_META).astype(f32)
        hval = h_ref[...]
        tgt = jnp.concatenate([r[...] for r in tg_refs], axis=0)

        mixers = ((ohg_ref, zhg_ref, hgw_ref[...]), (ogd_ref, zgd_ref, gdw_ref[...]))
        saved, ys = [], []
        for o_ref, z_ref, w in mixers:
            for hh in range(H):
                sl = slice(hh * DK, (hh + 1) * DK)
                o, z = o_ref[:, sl], z_ref[:, sl]
                r = lax.rsqrt(jnp.mean(o * o, axis=-1, keepdims=True) + EPS)
                n = o * r
                sg = sigmoid(z)
                ws = w * (z * sg)
                saved.append((r, n, sg, z, ws, w))
                ys.append(n * ws)
        y = jnp.concatenate(ys, axis=-1)
        h2 = hval + bdot(y, wo_ref[...])
        r2 = lax.rsqrt(jnp.mean(h2 * h2, axis=-1, keepdims=True) + EPS)
        n2 = h2 * r2
        fwv = fw_ref[...]
        err = (n2 * fwv - tgt) * valid
        loss = (0.5 / D) * jnp.sum(err * err)
        dyf = err * (1.0 / D)
        dn2 = dyf * fwv
        dout = r2 * (dn2 - n2 * jnp.mean(dn2 * n2, axis=-1, keepdims=True))
        dh_ref[...] = dout
        dy = bdot_nt(dout, wo_ref[...])
        dwo_ref[...] += bdot_tn(y, dout)
        dws = []
        for mi, (do_ref, dz_ref) in enumerate(((dohg_ref, dzhg_ref), (dogd_ref, dzgd_ref))):
            dw = jnp.zeros((1, DK), f32)
            for hh in range(H):
                sl = slice(hh * DK, (hh + 1) * DK)
                r, n, sg, z, ws, w = saved[mi * H + hh]
                dyh = dy[:, mi * HD + hh * DK:mi * HD + (hh + 1) * DK]
                t = dyh * n
                dw = dw + jnp.sum(t * (z * sg), axis=0, keepdims=True)
                dz_ref[:, sl] = (t * w * (sg * (1.0 + z * (1.0 - sg)))).astype(bf16)
                dn = dyh * ws
                do_ref[:, sl] = r * (dn - n * jnp.mean(dn * n, axis=-1, keepdims=True))
            dws.append(dw)
        loss_ref[...] += jnp.broadcast_to(loss, (8, DK))
        dhgw_ref[...] += jnp.broadcast_to(dws[0], (8, DK))
        dgdw_ref[...] += jnp.broadcast_to(dws[1], (8, DK))
        dfw_ref[...] += jnp.broadcast_to(jnp.sum(dyf * n2, axis=0, keepdims=True), (8, D))

    row = lambda w: pl.BlockSpec((tm, w), lambda i: (i, 0))
    whole = lambda r, w: pl.BlockSpec((r, w), lambda i: (0, 0))
    col = lambda c0: pl.BlockSpec((tm, HD), lambda i: (i, c0 // HD))

    tgt_spec = lambda k: pl.BlockSpec((C, D), _chunk_index(tm // C, k))
    return pl.pallas_call(
        body, name="out_loss", grid=(N // tm,),
        in_specs=[row(HD), row(HD), col(COL_ZHG), col(COL_ZGD),
                  whole(1, DK), whole(1, DK), whole(D, D), row(D), whole(1, D)] + [tgt_spec(k) for k in range(tm // C)],
        out_specs=[whole(8, DK), row(HD), row(HD), row(HD), row(HD), row(D), whole(D, D),
                   whole(8, DK), whole(8, DK), whole(8, D)],
        out_shape=[jax.ShapeDtypeStruct((8, DK), f32)] + [jax.ShapeDtypeStruct((N, HD), f32)] * 2
        + [jax.ShapeDtypeStruct((N, HD), bf16)] * 2
        + [jax.ShapeDtypeStruct((N, D), f32), jax.ShapeDtypeStruct((D, D), f32),
           jax.ShapeDtypeStruct((8, DK), f32), jax.ShapeDtypeStruct((8, DK), f32), jax.ShapeDtypeStruct((8, D), f32)],
        compiler_params=_cp(("arbitrary",)),
    )(o_hg, o_gd, proj, proj, hgw, gdw, wout, hflat, fw, *[target] * (tm // C))


def _in_bwd(pieces, wbig, hflat, norm_w, dh_res, pbs):
    tm = 384
    nck = tm // C
    nsteps = N // tm
    np_ = len(pieces)
    na = len(pbs)
    offs = [c0 for _, c0 in pieces]
    widths = [d.shape[1] for d, _ in pieces]

    def body(*refs):
        d_refs = refs[:np_]
        w_ref, h_ref, nw_ref, dhr_ref = refs[np_:np_ + 4]
        srcs = refs[np_ + 4:np_ + 4 + na]
        hd_ref, dnw_ref, gx_ref = refs[np_ + 4 + na:np_ + 7 + na]
        dsts = refs[np_ + 7 + na:np_ + 7 + 2 * na]
        dhbuf, gsem = refs[np_ + 7 + 2 * na:np_ + 9 + 2 * na]
        sems = refs[np_ + 9 + 2 * na:]
        i = pl.program_id(0)

        def chunk_copy(step, k):
            chunk = nck * step + k
            b = chunk // NCH
            cidx = chunk - NCH * b
            row = pl.multiple_of(C * jnp.maximum((SEQ // C) * b + cidx - 1, 0), C)
            return cidx != 0, pltpu.make_async_copy(dhbuf.at[step % 2, pl.ds(C * k, C)], gx_ref.at[pl.ds(row, C)],
                                                    gsem.at[step % 2, k])

        def for_real_chunks(step, act):
            for k in range(nck):
                real, cp = chunk_copy(step, k)
                pl.when(real)(functools.partial(act, cp))

        def copies():
            if not na:
                return []
            x, y, c, chips = _place()
            return [pltpu.make_async_remote_copy(
                src_ref=srcs[a].at[2 * px + py], dst_ref=dsts[a].at[j], send_sem=sems[0].at[na * j + a],
                recv_sem=sems[1].at[na * j + a], device_id=(px, py, c), device_id_type=MESH)
                for j, (px, py) in enumerate(chips) for a in range(na)]

        @pl.when(i == 0)
        def _():
            dnw_ref[...] = jnp.zeros_like(dnw_ref)
            for cp in copies():
                cp.start()

        du = jnp.zeros((tm, D), f32)
        for d_ref, off, wd in zip(d_refs, offs, widths):
            du = du + bdot(d_ref[...], w_ref[off:off + wd, :])
        _, vjp = jax.vjp(rmsnorm, h_ref[...], nw_ref[...])
        dh, dnw = vjp(du)
        dh = dh + dhr_ref[...]
        dnw_ref[...] += jnp.broadcast_to(dnw, (8, D))

        @pl.when(i >= 2)
        def _():
            for_real_chunks(i - 2, lambda cp: cp.wait())

        dhbuf[i % 2] = dh
        for_real_chunks(i, lambda cp: cp.start())
        for k in range(nck):
            chunk = nck * i + k

            @pl.when(chunk - NCH * (chunk // NCH) == 0)
            def _(k=k):
                hd_ref[0] = dh[C * k:C * (k + 1)]

        @pl.when(i == nsteps - 1)
        def _():
            for_real_chunks(i - 1, lambda cp: cp.wait())
            for_real_chunks(i, lambda cp: cp.wait())
            for cp in copies():
                cp.wait()

    row = lambda w: pl.BlockSpec((tm, w), lambda i: (i, 0))
    return pl.pallas_call(
        body, name="in_bwd", grid=(nsteps,),
        in_specs=[row(w) for w in widths]
        + [pl.BlockSpec((PC, D), lambda i: (0, 0)), row(D), pl.BlockSpec((1, D), lambda i: (0, 0)), row(D)] + [ANY] * na,
        out_specs=[pl.BlockSpec((1, C, D), lambda i: ((nck * i + nck - 1) // NCH, 0, 0)),
                   pl.BlockSpec((8, D), lambda i: (0, 0)), ANY] + [ANY] * na,
        out_shape=[jax.ShapeDtypeStruct((NB, C, D), f32), jax.ShapeDtypeStruct((8, D), f32),
                   jax.ShapeDtypeStruct((NB * SEQ, D), f32)]
        + [jax.ShapeDtypeStruct((3,) + p.shape[1:], p.dtype) for p in pbs],
        scratch_shapes=[pltpu.VMEM((2, tm, D), f32), pltpu.SemaphoreType.DMA((2, nck))]
        + ([pltpu.SemaphoreType.DMA((3 * na,)), pltpu.SemaphoreType.DMA((3 * na,))] if na else []),
        compiler_params=_cp(("arbitrary",)),
    )(*[d for d, _ in pieces], wbig, hflat, norm_w, dh_res, *pbs)


def _w_grad(ut, pieces):
    tk = N // 3
    offs = [c0 for _, c0 in pieces]
    widths = [d.shape[1] for d, _ in pieces]

    def body(u_ref, *refs):
        d_refs, o_ref = refs[:-1], refs[-1]

        @pl.when(pl.program_id(0) == 0)
        def _():
            o_ref[...] = jnp.zeros_like(o_ref)

        u = u_ref[...]
        for d_ref, off, wd in zip(d_refs, offs, widths):
            o_ref[:, off:off + wd] += jnp.dot(u, d_ref[...], preferred_element_type=f32)

    return pl.pallas_call(
        body, name="w_grad", grid=(N // tk,),
        in_specs=[pl.BlockSpec((D, tk), lambda k: (0, k))] + [pl.BlockSpec((tk, w), lambda k: (k, 0)) for w in widths],
        out_specs=pl.BlockSpec((D, PC), lambda k: (0, 0), pipeline_mode=pl.Buffered(1)),
        out_shape=jax.ShapeDtypeStruct((D, PC), f32),
        compiler_params=_cp(("arbitrary",)),
    )(ut, *[d for d, _ in pieces])


def _adam_math(g, w, m, v):
    m2 = ADAM_B1 * m + (1.0 - ADAM_B1) * g
    v2 = ADAM_B2 * v + (1.0 - ADAM_B2) * (g * g)
    m_hat = m2 / (1.0 - ADAM_B1 ** ADAM_STEP)
    v_hat = v2 / (1.0 - ADAM_B2 ** ADAM_STEP)
    delta = -ADAM_LR * (m_hat / (jnp.sqrt(v_hat) + ADAM_EPS) + ADAM_WD * w)
    return delta, m2, v2


def _adamw(gs, w, m, v, name):
    R, Cc = w.shape
    tr = 256 if R % 256 == 0 else R
    ng = len(gs)

    def body(*refs):
        g = refs[0][...]
        for r in refs[1:ng]:
            g = g + r[...]
        w_ref, m_ref, v_ref, g_ref, d_ref, m2_ref, v2_ref = refs[ng:]
        delta, m2, v2 = _adam_math(g, w_ref[...], m_ref[...], v_ref[...])
        g_ref[...] = g
        d_ref[...] = delta
        m2_ref[...] = m2
        v2_ref[...] = v2

    spec = pl.BlockSpec((tr, Cc), lambda i: (i, 0))
    return pl.pallas_call(
        body, name=name, grid=(R // tr,),
        in_specs=[spec] * (ng + 3), out_specs=[spec] * 4,
        out_shape=[jax.ShapeDtypeStruct((R, Cc), f32)] * 4,
        compiler_params=_cp(("arbitrary",)),
    )(*gs, w, m, v)


def _adamw_rows(g, w, m, v, name):
    R, _, Cc = w.shape
    tr = R // 9

    def body(g_ref, w_ref, m_ref, v_ref, go_ref, d_ref, m2_ref, v2_ref):
        g = g_ref[...]
        delta, m2, v2 = _adam_math(g, w_ref[...], m_ref[...], v_ref[...])
        go_ref[...] = g
        d_ref[...] = delta
        m2_ref[...] = m2
        v2_ref[...] = v2

    spec = pl.BlockSpec((tr, 1, Cc), lambda i: (i, 0, 0))
    return pl.pallas_call(
        body, name=name, grid=(R // tr,),
        in_specs=[spec] * 4, out_specs=[spec] * 4,
        out_shape=[jax.ShapeDtypeStruct((R, 1, Cc), f32)] * 4,
        compiler_params=_cp(("arbitrary",)),
    )(g, w, m, v)


def _place():
    x, y, c = lax.axis_index("x"), lax.axis_index("y"), lax.axis_index("c")
    return x, y, c, [(1 - x, y), (x, 1 - y), (1 - x, 1 - y)]


def _gather_weights(cm, halved, whole):
    R, _, Cc = cm.shape
    hw = Cc // 2
    shards = [jax.ShapeDtypeStruct((R, Cc), bf16)] + list(halved) + list(whole)
    nh = 1 + len(halved)
    na = len(shards)

    def body(*refs):
        srcs, dsts = refs[:na], refs[na:2 * na]
        send_sems, recv_sems, loc_sems = refs[2 * na:2 * na + 3]
        stage = refs[2 * na + 3:3 * na + 3]
        raw = refs[3 * na + 3]
        x, y, c, chips = _place()
        me = 2 * x + y
        loads = [pltpu.make_async_copy(srcs[0], raw, loc_sems.at[0])]
        loads += [pltpu.make_async_copy(srcs[i], stage[i], loc_sems.at[i]) for i in range(1, na)]
        locs = [pltpu.make_async_copy(v, d.at[me], loc_sems.at[i]) for i, (v, d) in enumerate(zip(stage, dsts))]
        for cp in loads:
            cp.start()

        def half_of(ref, i, half):
            return ref.at[:, pl.ds(pl.multiple_of(half * hw, hw), hw)] if i == 0 else ref.at[half]

        def ici(j, i, slot):
            px, py = chips[j]
            src = half_of(stage[0] if i == 0 else srcs[i], i, c) if i < nh else srcs[i]
            dst = half_of(dsts[i].at[slot], i, c) if i < nh else dsts[i].at[slot]
            return pltpu.make_async_remote_copy(
                src_ref=src, dst_ref=dst, send_sem=send_sems.at[na * j + i], recv_sem=recv_sems.at[na * j + i],
                device_id=(px, py, c), device_id_type=MESH)

        def d2d(j, i, half):
            px, py = chips[j]
            blk = half_of(dsts[i].at[2 * px + py], i, half)
            return pltpu.make_async_remote_copy(
                src_ref=blk, dst_ref=blk, send_sem=send_sems.at[3 * na + nh * j + i],
                recv_sem=recv_sems.at[3 * na + nh * j + i], device_id=(x, y, 1 - c), device_id_type=MESH)

        sends = [ici(j, i, me) for j in range(3) for i in range(1, na)]
        for cp in sends:
            cp.start()
        loads[0].wait()
        stage[0][...] = raw[:, 0, :].astype(bf16)
        first = [ici(j, 0, me) for j in range(3)]
        for cp in first:
            cp.start()
        sends += first
        locs[0].start()
        for ld, st in zip(loads[1:], locs[1:]):
            ld.wait()
            st.start()
        for j, (px, py) in enumerate(chips):
            for i in range(na):
                ici(j, i, 2 * px + py).wait_recv()
                if i < nh:
                    fwd = d2d(j, i, c)
                    fwd.start()
                    sends.append(fwd)
        for j in range(3):
            for i in range(nh):
                d2d(j, i, 1 - c).wait_recv()
        for cp in sends:
            cp.wait_send()
        for cp in locs:
            cp.wait()

    nsem = 3 * na + 3 * nh
    return pl.pallas_call(
        body, name="gather_weights",
        in_specs=[ANY] * na, out_specs=[ANY] * na,
        out_shape=[jax.ShapeDtypeStruct((4,) + s.shape, s.dtype) for s in shards],
        scratch_shapes=[pltpu.SemaphoreType.DMA((nsem,)), pltpu.SemaphoreType.DMA((nsem,)),
                        pltpu.SemaphoreType.DMA((na,))] + [pltpu.VMEM(s.shape, s.dtype) for s in shards]
        + [pltpu.VMEM(cm.shape, cm.dtype)],
        compiler_params=pltpu.CompilerParams(has_side_effects=True, vmem_limit_bytes=VMEM_LIMIT),
    )(cm, *halved, *whole)


def _swap_halves(gs):
    na = len(gs)
    jobs = [(i, q) for i in range(na) for q in range(gs[i].shape[0])]

    def body(*refs):
        srcs, dsts = refs[:na], refs[na:2 * na]
        send_sems, recv_sems = refs[2 * na:]
        x, y, c, _ = _place()
        cps = [pltpu.make_async_remote_copy(
            src_ref=srcs[i].at[q, 1 - c], dst_ref=dsts[i].at[q], send_sem=send_sems.at[k],
            recv_sem=recv_sems.at[k], device_id=(x, y, 1 - c), device_id_type=MESH)
            for k, (i, q) in enumerate(jobs)]
        for cp in cps:
            cp.start()
        for cp in cps:
            cp.wait()

    return pl.pallas_call(
        body, name="swap_halves",
        in_specs=[ANY] * na, out_specs=[ANY] * na,
        out_shape=[jax.ShapeDtypeStruct(g.shape[0:1] + g.shape[2:], g.dtype) for g in gs],
        scratch_shapes=[pltpu.SemaphoreType.DMA((len(jobs),)), pltpu.SemaphoreType.DMA((len(jobs),))],
        compiler_params=pltpu.CompilerParams(has_side_effects=True),
    )(*gs)


def _add_split(cm_arr, g, s):
    _, _, R, Cg = g.shape
    tr = 128

    def body(sc_ref, g_ref, s_ref, b_ref, o_ref):
        p = g_ref[0, 0] + s_ref[0]
        own = None
        for q in range(4):
            blk = p[:, SHARD_COLS * q:SHARD_COLS * (q + 1)]
            b_ref[q] = blk.astype(bf16)
            mine = jnp.where(sc_ref[1] == q, blk, 0.0)
            own = mine if own is None else own + mine
        o_ref[...] = own

    return pl.pallas_call(
        body, name="add_w_in",
        grid_spec=pltpu.PrefetchScalarGridSpec(
            num_scalar_prefetch=1, grid=(R // tr,),
            in_specs=[pl.BlockSpec((1, 1, tr, Cg), lambda i, sc: (0, sc[0], i, 0)),
                      pl.BlockSpec((1, tr, Cg), lambda i, sc: (0, i, 0))],
            out_specs=[pl.BlockSpec((4, tr, SHARD_COLS), lambda i, sc: (0, i, 0)),
                       pl.BlockSpec((tr, SHARD_COLS), lambda i, sc: (i, 0))]),
        out_shape=[jax.ShapeDtypeStruct((4, R, SHARD_COLS), bf16), jax.ShapeDtypeStruct((R, SHARD_COLS), f32)],
        compiler_params=_cp(("arbitrary",)),
    )(cm_arr, g, s)


def _add_halves(c_arr, g, s, name):
    Q, _, R, Cc = g.shape
    tr = min(R, 128)

    def body(c_ref, g_ref, s_ref, b_ref, f_ref):
        p = g_ref[0, 0] + s_ref[0]
        f_ref[0] = p
        b_ref[0] = p.astype(bf16)

    blk = pl.BlockSpec((1, tr, Cc), lambda q, i, cr: (q, i, 0))
    return pl.pallas_call(
        body, name=name,
        grid_spec=pltpu.PrefetchScalarGridSpec(
            num_scalar_prefetch=1, grid=(Q, R // tr),
            in_specs=[pl.BlockSpec((1, 1, tr, Cc), lambda q, i, cr: (q, cr[0], i, 0)), blk], out_specs=[blk, blk]),
        out_shape=[jax.ShapeDtypeStruct((Q, R, Cc), bf16), jax.ShapeDtypeStruct((Q, R, Cc), f32)],
        compiler_params=_cp(("arbitrary", "arbitrary")),
    )(c_arr, g, s)


_FLIPS = [(fx, fy, fc) for fx in (0, 1) for fy in (0, 1) for fc in (0, 1)][1:]


def _sum_blocks(own, r, name, transposed=False):
    R, Cc = own.shape
    tr = min(R, 256)

    def body(own_ref, r_ref, o_ref):
        acc = own_ref[...]
        for j in range(3):
            acc = acc + r_ref[j].astype(f32)
        o_ref[...] = acc.T if transposed else acc

    return pl.pallas_call(
        body, name=name, grid=(R // tr,),
        in_specs=[pl.BlockSpec((tr, Cc), lambda i: (i, 0)), pl.BlockSpec((3, tr, Cc), lambda i: (0, i, 0))],
        out_specs=pl.BlockSpec((Cc, tr), lambda i: (0, i)) if transposed else pl.BlockSpec((tr, Cc), lambda i: (i, 0)),
        out_shape=jax.ShapeDtypeStruct((Cc, R) if transposed else (R, Cc), f32),
        compiler_params=_cp(("arbitrary",)),
    )(own, r)


def _sum_packs(me8_arr, pack, rp):
    R = pack.shape[0]

    def body(me_ref, pk_ref, rp_ref, o_ref):
        me8 = me_ref[0]
        acc = None
        for d in range(8):
            rel = d ^ me8
            term = jnp.where(rel == 0, pk_ref[...], rp_ref[jnp.maximum(rel - 1, 0)])
            acc = term if acc is None else acc + term
        o_ref[...] = acc

    return pl.pallas_call(
        body, name="sum_packs",
        grid_spec=pltpu.PrefetchScalarGridSpec(
            num_scalar_prefetch=1, grid=(1,),
            in_specs=[pl.BlockSpec((R, 128), lambda i, mr: (0, 0)), pl.BlockSpec((7, R, 128), lambda i, mr: (0, 0, 0))],
            out_specs=pl.BlockSpec((R, 128), lambda i, mr: (0, 0))),
        out_shape=jax.ShapeDtypeStruct((R, 128), f32),
        compiler_params=_cp(("arbitrary",)),
    )(me8_arr, pack, rp)


def _swap_finished(fs, pack):
    na = len(fs)
    R = pack.shape[0]

    def body(*refs):
        srcs, pk = refs[:na], refs[na]
        dsts, rp = refs[na + 1:2 * na + 1], refs[2 * na + 1]
        send_sems, recv_sems = refs[2 * na + 2:]
        x, y, c, _ = _place()
        cps = [pltpu.make_async_remote_copy(
            src_ref=srcs[i], dst_ref=dsts[i], send_sem=send_sems.at[i], recv_sem=recv_sems.at[i],
            device_id=(x, y, 1 - c), device_id_type=MESH) for i in range(na)]
        cps += [pltpu.make_async_remote_copy(
            src_ref=pk, dst_ref=rp.at[k], send_sem=send_sems.at[na + k], recv_sem=recv_sems.at[na + k],
            device_id=(x ^ fx, y ^ fy, c ^ fc), device_id_type=MESH) for k, (fx, fy, fc) in enumerate(_FLIPS)]
        for cp in cps:
            cp.start()
        for cp in cps:
            cp.wait()

    return pl.pallas_call(
        body, name="swap_finished",
        in_specs=[ANY] * (na + 1), out_specs=[ANY] * (na + 1),
        out_shape=[jax.ShapeDtypeStruct(f.shape, f.dtype) for f in fs] + [jax.ShapeDtypeStruct((7, R, 128), f32)],
        scratch_shapes=[pltpu.SemaphoreType.DMA((na + 7,)), pltpu.SemaphoreType.DMA((na + 7,))],
        compiler_params=pltpu.CompilerParams(has_side_effects=True),
    )(*fs, pack)


def _adamw_halves(c_arr, mine, peer, w, m, v, name):
    _, R, Cc = w.shape
    tr = min(R, 256)

    def body(c_ref, mine_ref, peer_ref, w_ref, m_ref, v_ref, g_ref, d_ref, m2_ref, v2_ref):
        g = jnp.where(pl.program_id(0) == c_ref[0], mine_ref[...], peer_ref[...])
        delta, m2, v2 = _adam_math(g, w_ref[0], m_ref[0], v_ref[0])
        g_ref[0] = g
        d_ref[0] = delta
        m2_ref[0] = m2
        v2_ref[0] = v2

    half = pl.BlockSpec((tr, Cc), lambda hh, i, cr: (i, 0))
    full = pl.BlockSpec((1, tr, Cc), lambda hh, i, cr: (hh, i, 0))
    return pl.pallas_call(
        body, name=name,
        grid_spec=pltpu.PrefetchScalarGridSpec(
            num_scalar_prefetch=1, grid=(2, R // tr), in_specs=[half, half, full, full, full], out_specs=[full] * 4),
        out_shape=[jax.ShapeDtypeStruct((2, R, Cc), f32)] * 4,
        compiler_params=_cp(("arbitrary", "arbitrary")),
    )(c_arr, mine, peer, w, m, v)


def _rows8(a):
    flat = a.reshape(-1)
    n = flat.shape[0]
    rows = -(-n // 1024) * 8
    return jnp.pad(flat, (0, rows * 128 - n)).reshape(rows, 128)


def kernel(x, meta_tokens, norm_w, w_in, conv_w, hg_lb_logits, hg_norm_w, gdn_A_log, gdn_dt_bias, gdn_norm_w, w_out, final_norm_w, loss_target, m_meta_tokens, m_norm_w, m_w_in, m_conv_w, m_hg_lb_logits, m_hg_norm_w, m_gdn_A_log, m_gdn_dt_bias, m_gdn_norm_w, m_w_out, m_final_norm_w, v_meta_tokens, v_norm_w, v_w_in, v_conv_w, v_hg_lb_logits, v_hg_norm_w, v_gdn_A_log, v_gdn_dt_bias, v_gdn_norm_w, v_w_out, v_final_norm_w):
    me = 2 * lax.axis_index("x") + lax.axis_index("y")

    g_win, g_wout, g_conv, g_meta = _gather_weights(
        jnp.transpose(w_in, (2, 0, 1)), [w_out[0].astype(bf16).reshape(2, D // 8, D)], [conv_w[0], meta_tokens])
    wout_full = g_wout.reshape(D, D)
    conv4 = jnp.transpose(g_conv, (1, 0, 2)).reshape(4, 1, 3 * HD)
    meta_full = jnp.transpose(g_meta, (1, 0, 2)).reshape(N_META, D)

    c_arr = lax.axis_index("c").reshape(1).astype(jnp.int32)

    def chip_partials(gw, g_wout_part):
        g_in2 = gw.reshape(1, 2, D // 2, PC)
        g_out4 = g_wout_part.reshape(4, 2, D // 8, D)
        s_in, s_out = _swap_halves([g_in2, g_out4])
        pb_blocks, own_in = _add_split(jnp.concatenate([c_arr, me.reshape(1).astype(jnp.int32)]), g_in2, s_in)
        pb_out, pf_out = _add_halves(c_arr, g_out4, s_out, "add_w_out")
        own_out = lax.dynamic_index_in_dim(pf_out, me, axis=0, keepdims=False)
        return [pb_blocks, pb_out], [own_in, own_out]

    (loss8, grad_x, d_meta, d_nw, d_conv, d_lb, d_hgw, d_alog, d_dtb, d_gdw, d_fw, pfs, rs) = _local_step(
        x, loss_target, g_win, wout_full, conv4, meta_full, norm_w, hg_lb_logits, hg_norm_w, gdn_A_log, gdn_dt_bias,
        gdn_norm_w, final_norm_w, chip_partials)

    pack = jnp.concatenate([
        loss8, d_nw[0].reshape(8, 128), d_lb.reshape(8, 128), d_hgw, _rows8(d_alog[0, :H]), _rows8(d_dtb[0, :H]),
        d_gdw, d_fw[0].reshape(8, 128), d_meta.reshape(128, 128), d_conv.reshape(48, 128)], axis=0)
    return _reduce_and_update(
        me, c_arr, grad_x, pfs, rs, pack, meta_tokens, norm_w, w_in, conv_w, hg_lb_logits, hg_norm_w, gdn_A_log,
        gdn_dt_bias, gdn_norm_w, w_out, final_norm_w, m_meta_tokens, m_norm_w, m_w_in, m_conv_w, m_hg_lb_logits,
        m_hg_norm_w, m_gdn_A_log, m_gdn_dt_bias, m_gdn_norm_w, m_w_out, m_final_norm_w, v_meta_tokens, v_norm_w, v_w_in,
        v_conv_w, v_hg_lb_logits, v_hg_norm_w, v_gdn_A_log, v_gdn_dt_bias, v_gdn_norm_w, v_w_out, v_final_norm_w)


def _local_step(x, loss_target, w4, wout_full, conv4, meta_full, norm_w, hg_lb_logits, hg_norm_w, gdn_A_log, gdn_dt_bias,
                gdn_norm_w, final_norm_w, chip_partials):
    head = jnp.concatenate([jnp.zeros((PAD, D), f32), meta_full], axis=0)
    target = loss_target.reshape(NB * SEQ, D)
    l0, l1 = hg_lb_logits[0:1], hg_lb_logits[1:2]
    alog = jnp.pad(gdn_A_log, ((0, 0), (0, DK - H)))
    dtb = jnp.pad(gdn_dt_bias, ((0, 0), (0, DK - H)))
    fw = final_norm_w.reshape(1, D)

    hflat, proj, ut, cv2, wbig = _in_proj(x.reshape(NB * SEQ, D), head, norm_w, w4, conv4)
    proj3 = proj.reshape(NB, TP, PC)
    cv = cv2.reshape(NB, TP, 3 * HD)
    o_hg, s_hg, o_gd, s_gd, t_gd = _mix_fwd(proj3, cv, l0, l1, alog, dtb)
    (loss8, d_ohg, d_ogd, d_zhg, d_zgd, dh_res, g_wout_part, d_hgw, d_gdw, d_fw) = _out_loss(
        o_hg.reshape(N, HD), o_gd.reshape(N, HD), proj, hg_norm_w, gdn_norm_w, wout_full, hflat, fw, target)
    d_hg, d_l0, d_l1 = _hg_bwd(proj3, l0, l1, s_hg, d_ohg.reshape(NB, TP, HD))
    d_cv, d_ab, d_alog, d_dtb = _gd_bwd(cv, proj3, alog, dtb, s_gd, t_gd, d_ogd.reshape(NB, TP, HD))
    d_qkv, d_conv4 = _conv_bwd(proj3, conv4, d_cv)
    d_hg2, d_qkv2, d_ab2 = d_hg.reshape(N, 3 * HD), d_qkv.reshape(N, 3 * HD), d_ab.reshape(N, DK)
    pieces = [(d_hg2, COL_HG), (d_zhg, COL_ZHG), (d_qkv2, COL_QKV), (d_zgd, COL_ZGD), (d_ab2, COL_AB)]
    gw = _w_grad(ut, pieces)
    pbs, pfs = chip_partials(gw, g_wout_part) if chip_partials else ([], [gw, g_wout_part])
    d_head, d_nw, gx, *rs = _in_bwd(pieces, wbig, hflat, norm_w, dh_res, pbs)

    grad_x = gx.reshape(NB, SEQ, D)
    d_meta = jnp.sum(d_head[:, PAD:PAD + N_META, :], axis=0)
    d_conv = d_conv4[:, 0, :]
    d_lb = jnp.concatenate([d_l0[0:1], d_l1[0:1]], axis=0)
    return loss8, grad_x, d_meta, d_nw, d_conv, d_lb, d_hgw, d_alog, d_dtb, d_gdw, d_fw, pfs, rs


def _reduce_and_update(me, c_arr, grad_x, pfs, rs, pack, meta_tokens, norm_w, w_in, conv_w, hg_lb_logits, hg_norm_w,
                       gdn_A_log, gdn_dt_bias, gdn_norm_w, w_out, final_norm_w, m_meta_tokens, m_norm_w, m_w_in, m_conv_w,
                       m_hg_lb_logits, m_hg_norm_w, m_gdn_A_log, m_gdn_dt_bias, m_gdn_norm_w, m_w_out, m_final_norm_w,
                       v_meta_tokens, v_norm_w, v_w_in, v_conv_w, v_hg_lb_logits, v_hg_norm_w, v_gdn_A_log, v_gdn_dt_bias,
                       v_gdn_norm_w, v_w_out, v_final_norm_w):
    (own_in, own_out), (r_in, r_out) = pfs, rs
    f_in = _sum_blocks(own_in, r_in, "sum_w_in", transposed=True)
    f_out = _sum_blocks(own_out, r_out, "sum_w_out")
    o_in, o_out, r_pack = _swap_finished([f_in, f_out], pack)
    me8_arr = (2 * me + lax.axis_index("c")).reshape(1).astype(jnp.int32)
    small = _sum_packs(me8_arr, pack, r_pack)

    half_out = lambda a: a[0].reshape(2, D // 8, D)
    is0 = lax.axis_index("c") == 0
    g_in = jnp.concatenate([jnp.where(is0, f_in, o_in), jnp.where(is0, o_in, f_in)], axis=1)
    to_cm = lambda a: jnp.transpose(a, (2, 0, 1))
    gi, di, mi, vi = [jnp.transpose(a, (1, 2, 0))[0] for a in _adamw_rows(
        g_in.reshape(SHARD_COLS, 1, D), to_cm(w_in), to_cm(m_w_in), to_cm(v_w_in), "adamw_w_in")]
    go, do_, mo, vo = [a.reshape(D // 4, D) for a in _adamw_halves(
        c_arr, f_out, o_out, half_out(w_out), half_out(m_w_out), half_out(v_w_out), "adamw_w_out")]

    g_meta_full = small[64:192].reshape(N_META, D)
    g_meta_loc = lax.dynamic_slice(g_meta_full, (0, me * 256), (N_META, 256))
    gm, dm, mm_, vm = _adamw([g_meta_loc], meta_tokens, m_meta_tokens, v_meta_tokens, "adamw_meta")
    g_conv_full = small[192:240].reshape(4, 1536)
    g_conv_loc = lax.dynamic_slice(g_conv_full, (0, me * 384), (4, 384))
    gc, dc, mc, vc = _adamw([g_conv_loc], conv_w[0], m_conv_w[0], v_conv_w[0], "adamw_conv")

    reps = [(norm_w, m_norm_w, v_norm_w), (hg_lb_logits, m_hg_lb_logits, v_hg_lb_logits),
            (hg_norm_w, m_hg_norm_w, v_hg_norm_w), (gdn_A_log, m_gdn_A_log, v_gdn_A_log),
            (gdn_dt_bias, m_gdn_dt_bias, v_gdn_dt_bias), (gdn_norm_w, m_gdn_norm_w, v_gdn_norm_w),
            (final_norm_w, m_final_norm_w, v_final_norm_w)]
    wp = jnp.concatenate([_rows8(t[0]) for t in reps], axis=0)
    mp = jnp.concatenate([_rows8(t[1]) for t in reps], axis=0)
    vp = jnp.concatenate([_rows8(t[2]) for t in reps], axis=0)
    gr, dr, mr, vr = _adamw([small[8:64]], wp, mp, vp, "adamw_small")

    def unpack(p):
        outs = []
        for i, t in enumerate(reps):
            n = t[0].size
            outs.append(p[8 * i:8 * i + 8].reshape(-1)[:n].reshape(t[0].shape))
        return outs

    def leaves(meta_v, conv_v, in_v, out_v, rep_p):
        nw, lb, hgw, al, db, gdw, fwv = unpack(rep_p)
        return [meta_v, nw, in_v[None], conv_v[None], lb, hgw, al, db, gdw, out_v[None], fwv]

    loss = small[0, 0]
    return (loss, grad_x, *leaves(gm, gc, gi, go, gr), *leaves(dm, dc, di, do_, dr),
            *leaves(mm_, mc, mi, mo, mr), *leaves(vm, vc, vi, vo, vr))
```

```python
import functools

import jax
import jax.numpy as jnp
from jax import lax
from jax.experimental import pallas as pl
from jax.experimental.pallas import tpu as pltpu

f32 = jnp.float32
bf16 = jnp.bfloat16
MESH = pl.DeviceIdType.MESH
ANY = pl.BlockSpec(memory_space=pl.ANY)

D = 1024
NB = 2
N_META = 16
SEQ = 2048
PAD = 48
TP = PAD + N_META + SEQ
C = 64
NCH = TP // C
N = NB * TP
H = 4
DK = 128
HD = H * DK
PC = 4224
IN_COLS = 4104
SHARD_COLS = IN_COLS // 4
COL_HG, COL_ZHG, COL_QKV, COL_ZGD, COL_AB = 0, 3 * HD, 4 * HD, 7 * HD, 8 * HD
EPS = 1e-6
ADAM_LR, ADAM_B1, ADAM_B2, ADAM_EPS, ADAM_WD, ADAM_STEP = 0.001, 0.9, 0.999, 1e-08, 0.01, 10
VMEM_LIMIT = 56 * 1024 * 1024

P_HG = dict(lvl=1, av=1, qs=1, su=1)
P_GD = dict(kk=1, inv=1, sol=1, ws=1, qk=1, o=1, su=1)


def _cp(sem=None, **kw):
    return pltpu.CompilerParams(dimension_semantics=sem, vmem_limit_bytes=VMEM_LIMIT, **kw)


_DIMS = {"nn": (((1,), (0,)), ((), ())), "nt": (((1,), (1,)), ((), ())), "tn": (((0,), (0,)), ((), ()))}


def _split(x):
    hi = x.astype(bf16)
    return hi, (x - hi.astype(f32)).astype(bf16)


def _dg(a, b, kind, passes):
    d = lambda x, y: lax.dot_general(x, y, _DIMS[kind], preferred_element_type=f32)
    if passes == 1:
        return d(a.astype(bf16), b.astype(bf16))
    ah, al = _split(a)
    bh, bl = _split(b)
    return d(ah, bh) + d(ah, bl) + d(al, bh)


@functools.partial(jax.custom_vjp, nondiff_argnums=(2, 3))
def mmx(a, b, kind, passes):
    return _dg(a, b, kind, passes)


def _mmx_fwd(a, b, kind, passes):
    return _dg(a, b, kind, passes), (a, b)


def _mmx_bwd(kind, passes, res, g):
    a, b = res
    if kind == "nn":
        return _dg(g, b, "nt", passes), _dg(a, g, "tn", passes)
    if kind == "nt":
        return _dg(g, b, "nn", passes), _dg(g, a, "tn", passes)
    return _dg(b, g, "nt", passes), _dg(a, g, "nn", passes)


mmx.defvjp(_mmx_fwd, _mmx_bwd)


def _mask_dg(mask, x):
    xh, xl = _split(x)
    return jnp.dot(jnp.concatenate([mask, mask], axis=1), jnp.concatenate([xh, xl], axis=0), preferred_element_type=f32)


@functools.partial(jax.custom_vjp, nondiff_argnums=(2,))
def mask_mm(mask, x, bwd_passes):
    return _mask_dg(mask, x)


def _mask_fwd(mask, x, bwd_passes):
    return _mask_dg(mask, x), mask


def _mask_bwd(bwd_passes, mask, g):
    d = lambda y: lax.dot_general(mask, y, _DIMS["tn"], preferred_element_type=f32)
    if bwd_passes == 1:
        return None, d(g.astype(bf16))
    gh, gl = _split(g)
    return None, d(gh) + d(gl)


mask_mm.defvjp(_mask_fwd, _mask_bwd)


def bdot(a, b):
    return jnp.dot(a.astype(bf16), b.astype(bf16), preferred_element_type=f32)


def bdot_nt(a, b):
    return lax.dot_general(a.astype(bf16), b.astype(bf16), _DIMS["nt"], preferred_element_type=f32)


def bdot_tn(a, b):
    return lax.dot_general(a.astype(bf16), b.astype(bf16), _DIMS["tn"], preferred_element_type=f32)


def _iota2(n, m):
    return lax.broadcasted_iota(jnp.int32, (n, m), 0), lax.broadcasted_iota(jnp.int32, (n, m), 1)


sigmoid = jax.nn.sigmoid


def silu(x):
    return x * sigmoid(x)


def softplus(x):
    return jnp.maximum(x, 0.0) + jnp.log(1.0 + jnp.exp(-jnp.abs(x)))


def rmsnorm(x, w):
    return x * lax.rsqrt(jnp.mean(x * x, axis=-1, keepdims=True) + EPS) * w


def hg_masks():
    t, r = _iota2(C, C)
    mats = [r <= t, r > t]
    lvl = []
    for l in range(1, 7):
        sz = 1 << l
        half = sz >> 1
        seg_t = t >> l
        upper_t = (t & (sz - 1)) >= half
        mid_t = seg_t * sz + half - 1
        mats.append((upper_t & (r > mid_t) & (r <= t)) | ((~upper_t) & (r > t) & (r <= mid_t)))
        lvl.append(((seg_t == (r >> l)) & upper_t & ((r & (sz - 1)) < half)).astype(f32))
    stk = jnp.concatenate([m.astype(bf16) for m in mats], axis=0)
    return stk, lvl, (t == r).astype(f32)


def _head(a, h):
    return a[:, h * DK:(h + 1) * DK]


def _run(*gens):
    results = [None] * len(gens)
    live = list(range(len(gens)))
    while live:
        for i in list(live):
            try:
                next(gens[i])
            except StopIteration as e:
                results[i] = e.value
                live.remove(i)
    return results


def hg_chunk(St, ps, l0, l1):
    return _run(hg_stages(St, ps, l0, l1))[0]


def gd_chunk(S, cs, abs_, alog, dtb, t_saved=None):
    return _run(gd_stages(S, cs, abs_, alog, dtb, t_saved))[0]


def mix_chunk(St, ps, l0, l1, S, cs, abs_, alog, dtb):
    (sn_h, o_h), (sn_g, o_g, t_pack) = _run(hg_stages(St, ps, l0, l1), gd_stages(S, cs, abs_, alog, dtb))
    return sn_h, o_h, sn_g, o_g, t_pack


def hg_stages(St, ps, l0, l1):
    m = jnp.maximum(l0, l1)
    e0 = jnp.exp(l0 - m)
    e1 = jnp.exp(l1 - m)
    lb = e0 / (e0 + e1)
    stk, lvl, eye = hg_masks()
    msk = [eye] + lvl
    trow = lax.broadcasted_iota(jnp.int32, (C, 1), 0)
    upper = [(trow & ((1 << l) - 1)) >= (1 << (l - 1)) for l in range(1, 7)]
    qs, ks, vs, qG, kR, eGl = [], [], [], [], [], []
    for p in ps:
        pq, pf, v = p[:, 0:HD], p[:, HD:2 * HD], p[:, 2 * HD:3 * HD]
        q = silu(pq)
        f = lb + (1.0 - lb) * sigmoid(pf)
        k = 1.0 - f
        logf = jnp.log(f)
        Dm = mask_mm(stk, logf, 1)
        z = [jnp.where(up, q, k) * jnp.exp(Dm[(2 + i) * C:(3 + i) * C]) for i, up in enumerate(upper)]
        qs.append([q] + z)
        ks.append([k] + z)
        vs.append(v)
        qG.append(q * jnp.exp(Dm[0:C]))
        kR.append(k * jnp.exp(Dm[C:2 * C]))
        eGl.append(jnp.exp(jnp.sum(logf, axis=0, keepdims=True)))
    yield
    units = [(b, h) for b in range(len(ps)) for h in range(H)]
    parts = []
    for i in range(7):
        parts.append([msk[i] * mmx(_head(qs[b][i], h), _head(ks[b][i], h), "nt", P_HG["lvl"]) for b, h in units])
        yield
    A = [functools.reduce(lambda x, y: x + y, [parts[i][n] for i in range(7)]) for n in range(len(units))]
    qS = [mmx(_head(qG[b], h), St[n], "nt", P_HG["qs"]) for n, (b, h) in enumerate(units)]
    Sn = [St[n] * _head(eGl[b], h) + mmx(_head(vs[b], h), _head(kR[b], h), "tn", P_HG["su"])
          for n, (b, h) in enumerate(units)]
    yield
    outs = [mmx(A[n], _head(vs[b], h), "nn", P_HG["av"]) + qS[n] for n, (b, h) in enumerate(units)]
    return tuple(Sn), tuple(jnp.concatenate(outs[b * H:(b + 1) * H], axis=1) for b in range(len(ps)))


@jax.custom_vjp
def use_inverse(A, T):
    return T


def _use_inverse_fwd(A, T):
    return T, T


def _use_inverse_bwd(T, g):
    return -_dg(T, _dg(g, T, "nt", P_GD["inv"]), "tn", P_GD["inv"]), jnp.zeros_like(T)


use_inverse.defvjp(_use_inverse_fwd, _use_inverse_bwd)


def gd_stages(S, cs, abs_, alog, dtb, t_saved=None):
    t, r = _iota2(C, C)
    tri = (r <= t).astype(bf16)
    ups = (r > t).astype(bf16)
    lane = lax.broadcasted_iota(jnp.int32, (1, DK), 1)
    subl = lax.broadcasted_iota(jnp.int32, (8, 1), 0)
    eye = (t == r).astype(f32)
    strict = (r < t).astype(f32)
    bd = ((t >> 4) == (r >> 4)).astype(f32)
    qa, ka, va, b4, gam4, grev4, gam4T, glast4 = [], [], [], [], [], [], [], []
    for c, ab in zip(cs, abs_):
        qa.append(silu(c[:, 0:HD]))
        ka.append(silu(c[:, HD:2 * HD]))
        va.append(silu(c[:, 2 * HD:3 * HD]))
        g4 = -jnp.exp(alog) * softplus(ab + dtb)
        b4.append(sigmoid(ab))
        gam4.append(mask_mm(tri, g4, 2))
        grev4.append(mask_mm(ups, g4, 2))
        gam4T.append(gam4[-1].T)
        glast4.append(jnp.sum(g4, axis=0, keepdims=True))
    yield
    units = [(b, h) for b in range(len(cs)) for h in range(H)]
    nu = range(len(units))
    inv = lambda a, b: [mmx(a[n], b[n], "nn", P_GD["inv"]) for n in nu]
    v = [_head(va[b], h) for b, h in units]
    q = [_head(qa[b], h) for b, h in units]
    k = [_head(ka[b], h) for b, h in units]
    q = [x * lax.rsqrt(jnp.sum(x * x, -1, keepdims=True) + EPS) * (DK ** -0.5) for x in q]
    k = [x * lax.rsqrt(jnp.sum(x * x, -1, keepdims=True) + EPS) for x in k]
    oh = [(lane == h).astype(f32) for h in range(H)]
    gam_c = [jnp.sum(gam4[b] * oh[h], -1, keepdims=True) for b, h in units]
    grev_c = [jnp.sum(grev4[b] * oh[h], -1, keepdims=True) for b, h in units]
    beta = [jnp.sum(b4[b] * (lane == h + H).astype(f32), -1, keepdims=True) for b, h in units]
    glast = [jnp.sum(glast4[b] * oh[h], -1, keepdims=True) for b, h in units]
    gam_r = [jnp.sum(gam4T[b][0:8, :] * (subl == h).astype(f32), axis=0, keepdims=True) for b, h in units]
    dec = [jnp.exp(jnp.where(r <= t, gam_c[n] - gam_r[n], -1e30)) for n in nu]
    egam = [jnp.exp(gam_c[n]) for n in nu]
    kk = [mmx(k[n], k[n], "nt", P_GD["kk"]) for n in nu]
    qk = [mmx(q[n], k[n], "nt", P_GD["qk"]) * dec[n] for n in nu]
    yield
    A = [beta[n] * kk[n] * dec[n] * strict for n in nu]
    Dg = [A[n] * bd for n in nu]
    L = [A[n] - Dg[n] for n in nu]
    if t_saved is None:
        ImD = [eye - Dg[n] for n in nu]
        D2 = inv(Dg, Dg)
        yield
        P1 = inv(ImD, [eye + x for x in D2])
        D4 = inv(D2, D2)
        yield
        P2 = inv(P1, [eye + x for x in D4])
        D8 = inv(D4, D4)
        yield
        M = inv(P2, [eye + x for x in D8])
        yield
        Nn = inv(M, L)
        yield
        N2 = inv(Nn, Nn)
        yield
        T1 = inv([eye - x for x in Nn], [eye + x for x in N2])
        yield
        Tinv = inv(T1, M)
        yield
    else:
        Tinv = [use_inverse(A[n], t_saved[b][:, h * DK:h * DK + C]) for n, (b, h) in enumerate(units)]
    rhs = [jnp.concatenate([beta[n] * v[n], (beta[n] * egam[n]) * k[n]], axis=1) for n in nu]
    sol = [mmx(Tinv[n], rhs[n], "nn", P_GD["sol"]) for n in nu]
    yield
    qwS = [mmx(jnp.concatenate([q[n] * egam[n], sol[n][:, DK:2 * DK]], axis=0), S[n], "nn", P_GD["ws"]) for n in nu]
    yield
    u = [sol[n][:, 0:DK] - qwS[n][C:2 * C] for n in nu]
    outs = [qwS[n][0:C] + mmx(qk[n], u[n], "nn", P_GD["o"]) for n in nu]
    Sn = [jnp.exp(glast[n]) * S[n] + mmx(k[n] * jnp.exp(grev_c[n]), u[n], "tn", P_GD["su"]) for n in nu]
    zpad = jnp.zeros((C, DK - C), f32)
    t_pack = tuple(jnp.concatenate([x for n in range(b * H, (b + 1) * H) for x in (lax.stop_gradient(Tinv[n]), zpad)],
                                   axis=1) for b in range(len(cs)))
    return tuple(Sn), tuple(jnp.concatenate(outs[b * H:(b + 1) * H], axis=1) for b in range(len(cs))), t_pack


def _chunk_index(tile_chunks, k):
    def index(i):
        chunk = tile_chunks * i + k
        b = chunk // NCH
        return jnp.maximum((SEQ // C) * b + chunk - NCH * b - 1, 0), 0
    return index


def _in_proj(xflat, head, norm_w, w4, conv4):
    tm = 384
    nck = tm // C
    W3 = 3 * HD

    def body(*refs):
        x_refs = refs[:nck]
        head_ref, nw_ref, w4_ref, cw_ref, h_ref, p_ref, ut_ref, cv_ref, w_ref, prev = refs[nck:]
        i = pl.program_id(0)

        @pl.when(i == 0)
        def _():
            prev[...] = jnp.zeros_like(prev)
            w_ref[PC - DK:PC, :] = jnp.zeros((DK, D), bf16)
            for q in range(4):
                w_ref[SHARD_COLS * q:SHARD_COLS * (q + 1), :] = w4_ref[q]

        blocks = []
        for k in range(nck):
            chunk = nck * i + k
            blocks.append(jnp.where(chunk - NCH * (chunk // NCH) == 0, head_ref[...], x_refs[k][...]))
        hval = jnp.concatenate(blocks, axis=0)
        h_ref[...] = hval
        u = rmsnorm(hval, nw_ref[...])
        ut_ref[...] = u.T.astype(bf16)
        p = bdot_nt(u, w_ref[...])
        p_ref[...] = p
        x = p[:, COL_QKV:COL_QKV + W3]
        xx = jnp.concatenate([prev[...], x], axis=0)
        y = cw_ref[3] * x
        for s in (1, 2, 3):
            y = y + cw_ref[3 - s] * pltpu.roll(xx, s, 0)[8:]
        row = i * tm + lax.broadcasted_iota(jnp.int32, (tm, 1), 0)
        tok = jnp.where(row >= TP, row - TP, row)
        cv_ref[...] = jnp.where(tok >= 8, y, 0.0)
        prev[...] = x[tm - 8:tm]

    return pl.pallas_call(
        body, name="in_proj", grid=(N // tm,),
        in_specs=[pl.BlockSpec((C, D), _chunk_index(nck, k)) for k in range(nck)]
        + [pl.BlockSpec((C, D), lambda i: (0, 0)), pl.BlockSpec((1, D), lambda i: (0, 0)),
           pl.BlockSpec((4, SHARD_COLS, D), lambda i: (0, 0, 0), pipeline_mode=pl.Buffered(1)),
           pl.BlockSpec((4, 1, W3), lambda i: (0, 0, 0))],
        out_specs=[pl.BlockSpec((tm, D), lambda i: (i, 0)), pl.BlockSpec((tm, PC), lambda i: (i, 0)),
                   pl.BlockSpec((D, tm), lambda i: (0, i)), pl.BlockSpec((tm, W3), lambda i: (i, 0)),
                   pl.BlockSpec((PC, D), lambda i: (0, 0), pipeline_mode=pl.Buffered(1))],
        out_shape=[jax.ShapeDtypeStruct((N, D), f32), jax.ShapeDtypeStruct((N, PC), f32),
                   jax.ShapeDtypeStruct((D, N), bf16), jax.ShapeDtypeStruct((N, W3), f32),
                   jax.ShapeDtypeStruct((PC, D), bf16)],
        scratch_shapes=[pltpu.VMEM((8, W3), f32)],
        compiler_params=_cp(("arbitrary",)),
    )(*[xflat] * nck, head, norm_w, w4, conv4)


NU = NB * H
_REV = lambda c: NCH - 1 - c
_FWD = lambda c: c


def _tok_spec(w, ix, col=0):
    return pl.BlockSpec((NB, C, w), lambda c: (0, ix(c), col))


def _state_spec(ix):
    return pl.BlockSpec((NB, 1, H, DK, DK), lambda c: (0, ix(c), 0, 0, 0))


def _row_spec(w):
    return pl.BlockSpec((1, w), lambda c: (0, 0))


def _rows(ref):
    return tuple(ref[b] for b in range(NB))


def _hg_extra_specs(ix):
    return [_row_spec(HD), _row_spec(HD)]


def _gd_extra_specs(ix):
    return [_tok_spec(DK, ix, COL_AB // DK), _row_spec(DK), _row_spec(DK)]


def _mix_fwd(proj3, cv, l0, l1, alog, dtb):
    def body(p_ref, c_ref, ab_ref, l0_ref, l1_ref, al_ref, db_ref, oh_ref, sh_ref, og_ref, sg_ref, t_ref, sth, stg):
        @pl.when(pl.program_id(0) == 0)
        def _():
            sth[...] = jnp.zeros_like(sth)
            stg[...] = jnp.zeros_like(stg)

        Sh = tuple(sth[n] for n in range(NU))
        Sg = tuple(stg[n] for n in range(NU))
        for n in range(NU):
            sh_ref[n // H, 0, n % H] = Sh[n]
            sg_ref[n // H, 0, n % H] = Sg[n]
        snh, oh, sng, og, tp = mix_chunk(Sh, _rows(p_ref), l0_ref[...], l1_ref[...],
                                         Sg, _rows(c_ref), _rows(ab_ref), al_ref[...], db_ref[...])
        for n in range(NU):
            sth[n] = snh[n]
            stg[n] = sng[n]
        for b in range(NB):
            oh_ref[b] = oh[b]
            og_ref[b] = og[b]
            t_ref[b] = tp[b]

    tok = jax.ShapeDtypeStruct((NB, TP, HD), f32)
    st = jax.ShapeDtypeStruct((NB, NCH, H, DK, DK), f32)
    return pl.pallas_call(
        body, name="mix_fwd", grid=(NCH,),
        in_specs=[_tok_spec(3 * HD, _FWD), _tok_spec(3 * HD, _FWD), _tok_spec(DK, _FWD, COL_AB // DK),
                  _row_spec(HD), _row_spec(HD), _row_spec(DK), _row_spec(DK)],
        out_specs=[_tok_spec(HD, _FWD), _state_spec(_FWD), _tok_spec(HD, _FWD), _state_spec(_FWD), _tok_spec(HD, _FWD)],
        out_shape=[tok, st, tok, st, tok],
        scratch_shapes=[pltpu.VMEM((NU, DK, DK), f32), pltpu.VMEM((NU, DK, DK), f32)],
        compiler_params=_cp(("arbitrary",)),
    )(proj3, cv, proj3, l0, l1, alog, dtb)


def _hg_bwd(proj3, l0, l1, s_saved, do):
    def body(p_ref, l0_ref, l1_ref, s_ref, do_ref, dp_ref, dl0_ref, dl1_ref, dst):
        @pl.when(pl.program_id(0) == 0)
        def _():
            dst[...] = jnp.zeros_like(dst)
            dl0_ref[...] = jnp.zeros_like(dl0_ref)
            dl1_ref[...] = jnp.zeros_like(dl1_ref)

        S = tuple(s_ref[n // H, 0, n % H] for n in range(NU))
        _, vjp = jax.vjp(hg_chunk, S, _rows(p_ref), l0_ref[...], l1_ref[...])
        dS, dp, dl0, dl1 = vjp((tuple(dst[n] for n in range(NU)), _rows(do_ref)))
        for n in range(NU):
            dst[n] = dS[n]
        for b in range(NB):
            dp_ref[b] = dp[b].astype(bf16)
        dl0_ref[...] += jnp.broadcast_to(dl0, (8, HD))
        dl1_ref[...] += jnp.broadcast_to(dl1, (8, HD))

    acc = pl.BlockSpec((8, HD), lambda c: (0, 0))
    return pl.pallas_call(
        body, name="hg_bwd", grid=(NCH,),
        in_specs=[_tok_spec(3 * HD, _REV)] + _hg_extra_specs(_REV) + [_state_spec(_REV), _tok_spec(HD, _REV)],
        out_specs=[_tok_spec(3 * HD, _REV), acc, acc],
        out_shape=[jax.ShapeDtypeStruct((NB, TP, 3 * HD), bf16), jax.ShapeDtypeStruct((8, HD), f32),
                   jax.ShapeDtypeStruct((8, HD), f32)],
        scratch_shapes=[pltpu.VMEM((NU, DK, DK), f32)],
        compiler_params=_cp(("arbitrary",)),
    )(proj3, l0, l1, s_saved, do)


def _gd_bwd(cv, proj3, alog, dtb, s_saved, t_saved, do):
    def body(c_ref, ab_ref, al_ref, db_ref, s_ref, t_ref, do_ref, dc_ref, dab_ref, dal_ref, ddb_ref, dst):
        @pl.when(pl.program_id(0) == 0)
        def _():
            dst[...] = jnp.zeros_like(dst)
            dal_ref[...] = jnp.zeros_like(dal_ref)
            ddb_ref[...] = jnp.zeros_like(ddb_ref)

        S = tuple(s_ref[n // H, 0, n % H] for n in range(NU))
        t_rows = _rows(t_ref)
        fn = lambda *a: gd_chunk(*a, t_saved=t_rows)[0:2]
        _, vjp = jax.vjp(fn, S, _rows(c_ref), _rows(ab_ref), al_ref[...], db_ref[...])
        dS, dc, dab, dal, ddb = vjp((tuple(dst[n] for n in range(NU)), _rows(do_ref)))
        for n in range(NU):
            dst[n] = dS[n]
        for b in range(NB):
            dc_ref[b] = dc[b]
            dab_ref[b] = dab[b].astype(bf16)
        dal_ref[...] += jnp.broadcast_to(dal, (8, DK))
        ddb_ref[...] += jnp.broadcast_to(ddb, (8, DK))

    acc = pl.BlockSpec((8, DK), lambda c: (0, 0))
    return pl.pallas_call(
        body, name="gd_bwd", grid=(NCH,),
        in_specs=[_tok_spec(3 * HD, _REV)] + _gd_extra_specs(_REV)
        + [_state_spec(_REV), _tok_spec(HD, _REV), _tok_spec(HD, _REV)],
        out_specs=[_tok_spec(3 * HD, _REV), _tok_spec(DK, _REV), acc, acc],
        out_shape=[jax.ShapeDtypeStruct((NB, TP, 3 * HD), f32), jax.ShapeDtypeStruct((NB, TP, DK), bf16),
                   jax.ShapeDtypeStruct((8, DK), f32), jax.ShapeDtypeStruct((8, DK), f32)],
        scratch_shapes=[pltpu.VMEM((NU, DK, DK), f32)],
        compiler_params=_cp(("arbitrary",)),
    )(cv, proj3, alog, dtb, s_saved, t_saved, do)


def _conv_bwd(proj3, conv4, dy):
    def body(x_ref, w_ref, dy_ref, dx_ref, dw_ref):
        @pl.when(pl.program_id(1) == 0)
        def _():
            dw_ref[...] = jnp.zeros_like(dw_ref)

        x = x_ref[0]
        row = lax.broadcasted_iota(jnp.int32, (TP, 1), 0)
        g = jnp.where(row >= 8, dy_ref[0], 0.0)
        dx = w_ref[3] * g
        dw_ref[3] += jnp.broadcast_to(jnp.sum(x * g, axis=0, keepdims=True), (8, HD))
        for s in (1, 2, 3):
            dx = dx + w_ref[3 - s] * pltpu.roll(g, TP - s, 0)
            dw_ref[3 - s] += jnp.broadcast_to(jnp.sum(pltpu.roll(x, s, 0) * g, axis=0, keepdims=True), (8, HD))
        dx_ref[0] = dx.astype(bf16)

    return pl.pallas_call(
        body, name="conv_bwd", grid=(3, NB),
        in_specs=[pl.BlockSpec((1, TP, HD), lambda j, b: (b, 0, COL_QKV // HD + j)),
                  pl.BlockSpec((4, 1, HD), lambda j, b: (0, 0, j)), pl.BlockSpec((1, TP, HD), lambda j, b: (b, 0, j))],
        out_specs=[pl.BlockSpec((1, TP, HD), lambda j, b: (b, 0, j)), pl.BlockSpec((4, 8, HD), lambda j, b: (0, 0, j))],
        out_shape=[jax.ShapeDtypeStruct((NB, TP, 3 * HD), bf16), jax.ShapeDtypeStruct((4, 8, 3 * HD), f32)],
        compiler_params=_cp(("arbitrary", "arbitrary")),
    )(proj3, conv4, dy)


def _out_loss(o_hg, o_gd, proj, hgw, gdw, wout, hflat, fw, target):
    tm = 384

    def body(ohg_ref, ogd_ref, zhg_ref, zgd_ref, hgw_ref, gdw_ref, wo_ref, h_ref, fw_ref, *refs):
        tg_refs = refs[:tm // C]
        (loss_ref, dohg_ref, dogd_ref, dzhg_ref, dzgd_ref, dh_ref, dwo_ref, dhgw_ref, dgdw_ref, dfw_ref) = refs[tm // C:]
        i = pl.program_id(0)

        @pl.when(i == 0)
        def _():
            for r in (loss_ref, dwo_ref, dhgw_ref, dgdw_ref, dfw_ref):
                r[...] = jnp.zeros_like(r)

        row = i * tm + lax.broadcasted_iota(jnp.int32, (tm, 1), 0)
        tok = jnp.where(row >= TP, row - TP, row)
        valid = (tok >= PAD + N_META).astype(f32)
        hval = h_ref[...]
        tgt = jnp.concatenate([r[...] for r in tg_refs], axis=0)

        mixers = ((ohg_ref, zhg_ref, hgw_ref[...]), (ogd_ref, zgd_ref, gdw_ref[...]))
        saved, ys = [], []
        for o_ref, z_ref, w in mixers:
            for hh in range(H):
                sl = slice(hh * DK, (hh + 1) * DK)
                o, z = o_ref[:, sl], z_ref[:, sl]
                r = lax.rsqrt(jnp.mean(o * o, axis=-1, keepdims=True) + EPS)
                n = o * r
                sg = sigmoid(z)
                ws = w * (z * sg)
                saved.append((r, n, sg, z, ws, w))
                ys.append(n * ws)
        y = jnp.concatenate(ys, axis=-1)
        h2 = hval + bdot(y, wo_ref[...])
        r2 = lax.rsqrt(jnp.mean(h2 * h2, axis=-1, keepdims=True) + EPS)
        n2 = h2 * r2
        fwv = fw_ref[...]
        err = (n2 * fwv - tgt) * valid
        loss = (0.5 / D) * jnp.sum(err * err)
        dyf = err * (1.0 / D)
        dn2 = dyf * fwv
        dout = r2 * (dn2 - n2 * jnp.mean(dn2 * n2, axis=-1, keepdims=True))
        dh_ref[...] = dout
        dy = bdot_nt(dout, wo_ref[...])
        dwo_ref[...] += bdot_tn(y, dout)
        dws = []
        for mi, (do_ref, dz_ref) in enumerate(((dohg_ref, dzhg_ref), (dogd_ref, dzgd_ref))):
            dw = jnp.zeros((1, DK), f32)
            for hh in range(H):
                sl = slice(hh * DK, (hh + 1) * DK)
                r, n, sg, z, ws, w = saved[mi * H + hh]
                dyh = dy[:, mi * HD + hh * DK:mi * HD + (hh + 1) * DK]
                t = dyh * n
                dw = dw + jnp.sum(t * (z * sg), axis=0, keepdims=True)
                dz_ref[:, sl] = (t * w * (sg * (1.0 + z * (1.0 - sg)))).astype(bf16)
                dn = dyh * ws
                do_ref[:, sl] = r * (dn - n * jnp.mean(dn * n, axis=-1, keepdims=True))
            dws.append(dw)
        loss_ref[...] += jnp.broadcast_to(loss, (8, DK))
        dhgw_ref[...] += jnp.broadcast_to(dws[0], (8, DK))
        dgdw_ref[...] += jnp.broadcast_to(dws[1], (8, DK))
        dfw_ref[...] += jnp.broadcast_to(jnp.sum(dyf * n2, axis=0, keepdims=True), (8, D))

    row = lambda w: pl.BlockSpec((tm, w), lambda i: (i, 0))
    whole = lambda r, w: pl.BlockSpec((r, w), lambda i: (0, 0))
    col = lambda c0: pl.BlockSpec((tm, HD), lambda i: (i, c0 // HD))

    tgt_spec = lambda k: pl.BlockSpec((C, D), _chunk_index(tm // C, k))
    return pl.pallas_call(
        body, name="out_loss", grid=(N // tm,),
        in_specs=[row(HD), row(HD), col(COL_ZHG), col(COL_ZGD),
                  whole(1, DK), whole(1, DK), whole(D, D), row(D), whole(1, D)] + [tgt_spec(k) for k in range(tm // C)],
        out_specs=[whole(8, DK), row(HD), row(HD), row(HD), row(HD), row(D), whole(D, D),
                   whole(8, DK), whole(8, DK), whole(8, D)],
        out_shape=[jax.ShapeDtypeStruct((8, DK), f32)] + [jax.ShapeDtypeStruct((N, HD), f32)] * 2
        + [jax.ShapeDtypeStruct((N, HD), bf16)] * 2
        + [jax.ShapeDtypeStruct((N, D), f32), jax.ShapeDtypeStruct((D, D), f32),
           jax.ShapeDtypeStruct((8, DK), f32), jax.ShapeDtypeStruct((8, DK), f32), jax.ShapeDtypeStruct((8, D), f32)],
        compiler_params=_cp(("arbitrary",)),
    )(o_hg, o_gd, proj, proj, hgw, gdw, wout, hflat, fw, *[target] * (tm // C))


def _in_bwd(pieces, wbig, hflat, norm_w, dh_res, pbs):
    tm = 384
    nsteps = N // tm
    np_ = len(pieces)
    na = len(pbs)
    offs = [c0 for _, c0 in pieces]
    widths = [d.shape[1] for d, _ in pieces]

    def body(*refs):
        d_refs = refs[:np_]
        w_ref, h_ref, nw_ref, dhr_ref = refs[np_:np_ + 4]
        srcs = refs[np_ + 4:np_ + 4 + na]
        dh_ref, dnw_ref = refs[np_ + 4 + na:np_ + 6 + na]
        dsts = refs[np_ + 6 + na:np_ + 6 + 2 * na]
        sems = refs[np_ + 6 + 2 * na:]
        i = pl.program_id(0)

        def copies():
            if not na:
                return []
            x, y, c, chips = _place()
            return [pltpu.make_async_remote_copy(
                src_ref=srcs[a].at[2 * px + py], dst_ref=dsts[a].at[j], send_sem=sems[0].at[na * j + a],
                recv_sem=sems[1].at[na * j + a], device_id=(px, py, c), device_id_type=MESH)
                for j, (px, py) in enumerate(chips) for a in range(na)]

        @pl.when(i == 0)
        def _():
            dnw_ref[...] = jnp.zeros_like(dnw_ref)
            for cp in copies():
                cp.start()

        du = jnp.zeros((tm, D), f32)
        for d_ref, off, wd in zip(d_refs, offs, widths):
            du = du + bdot(d_ref[...], w_ref[off:off + wd, :])
        _, vjp = jax.vjp(rmsnorm, h_ref[...], nw_ref[...])
        dh, dnw = vjp(du)
        dh_ref[...] = dh + dhr_ref[...]
        dnw_ref[...] += jnp.broadcast_to(dnw, (8, D))

        @pl.when(i == nsteps - 1)
        def _():
            for cp in copies():
                cp.wait()

    row = lambda w: pl.BlockSpec((tm, w), lambda i: (i, 0))
    return pl.pallas_call(
        body, name="in_bwd", grid=(nsteps,),
        in_specs=[row(w) for w in widths]
        + [pl.BlockSpec((PC, D), lambda i: (0, 0)), row(D), pl.BlockSpec((1, D), lambda i: (0, 0)), row(D)] + [ANY] * na,
        out_specs=[row(D), pl.BlockSpec((8, D), lambda i: (0, 0))] + [ANY] * na,
        out_shape=[jax.ShapeDtypeStruct((N, D), f32), jax.ShapeDtypeStruct((8, D), f32)]
        + [jax.ShapeDtypeStruct((3,) + p.shape[1:], p.dtype) for p in pbs],
        scratch_shapes=[pltpu.SemaphoreType.DMA((3 * na,)), pltpu.SemaphoreType.DMA((3 * na,))] if na else [],
        compiler_params=_cp(("arbitrary",)),
    )(*[d for d, _ in pieces], wbig, hflat, norm_w, dh_res, *pbs)


def _w_grad(ut, pieces, dy, proj, conv4):
    tk = 384
    nt = N // tk
    W3 = 3 * HD
    np_ = len(pieces)
    offs = [c0 for _, c0 in pieces]
    widths = [d.shape[1] for d, _ in pieces]

    def body(u_ref, *refs):
        d_refs = refs[:np_]
        dy_ref, x0_ref, x1_ref, x2_ref, w_ref, o_ref, dx_ref, dw_ref, nxt = refs[np_:]
        k = pl.program_id(0)

        @pl.when(k == 0)
        def _():
            o_ref[...] = jnp.zeros_like(o_ref)
            dw_ref[...] = jnp.zeros_like(dw_ref)
            nxt[...] = jnp.zeros_like(nxt)

        u = u_ref[...]
        for d_ref, off, wd in zip(d_refs, offs, widths):
            o_ref[:, off:off + wd] += jnp.dot(u, d_ref[...], preferred_element_type=f32)

        tok = (nt - 1 - k) * tk + lax.broadcasted_iota(jnp.int32, (tk, 1), 0)
        tok = jnp.where(tok >= TP, tok - TP, tok)
        for j, x_ref in enumerate((x0_ref, x1_ref, x2_ref)):
            sl = slice(j * HD, (j + 1) * HD)
            x = x_ref[...]
            g = jnp.where(tok >= 8, dy_ref[:, sl], 0.0)
            gg = jnp.concatenate([g, nxt[:, sl]], axis=0)
            dx = w_ref[3, :, sl] * g
            dw_ref[3, :, sl] += jnp.broadcast_to(jnp.sum(x * g, axis=0, keepdims=True), (8, HD))
            for s in (1, 2, 3):
                gs = pltpu.roll(gg, tk + 8 - s, 0)[:tk]
                dx = dx + w_ref[3 - s, :, sl] * gs
                dw_ref[3 - s, :, sl] += jnp.broadcast_to(jnp.sum(x * gs, axis=0, keepdims=True), (8, HD))
            nxt[:, sl] = g[:8]
            dxb = dx.astype(bf16)
            dx_ref[:, sl] = dxb
            o_ref[:, COL_QKV + j * HD:COL_QKV + (j + 1) * HD] += jnp.dot(u, dxb, preferred_element_type=f32)

    rev = lambda k: (nt - 1 - k, 0)
    return pl.pallas_call(
        body, name="w_grad", grid=(nt,),
        in_specs=[pl.BlockSpec((D, tk), lambda k: (0, nt - 1 - k))] + [pl.BlockSpec((tk, w), rev) for w in widths]
        + [pl.BlockSpec((tk, W3), rev)]
        + [pl.BlockSpec((tk, HD), lambda k, j=j: (nt - 1 - k, COL_QKV // HD + j)) for j in range(3)]
        + [pl.BlockSpec((4, 1, W3), lambda k: (0, 0, 0))],
        out_specs=[pl.BlockSpec((D, PC), lambda k: (0, 0), pipeline_mode=pl.Buffered(1)),
                   pl.BlockSpec((tk, W3), rev), pl.BlockSpec((4, 8, W3), lambda k: (0, 0, 0))],
        out_shape=[jax.ShapeDtypeStruct((D, PC), f32), jax.ShapeDtypeStruct((N, W3), bf16),
                   jax.ShapeDtypeStruct((4, 8, W3), f32)],
        scratch_shapes=[pltpu.VMEM((8, W3), f32)],
        compiler_params=_cp(("arbitrary",)),
    )(ut, *[d for d, _ in pieces], dy, proj, proj, proj, conv4)


def _adam_math(g, w, m, v):
    m2 = ADAM_B1 * m + (1.0 - ADAM_B1) * g
    v2 = ADAM_B2 * v + (1.0 - ADAM_B2) * (g * g)
    m_hat = m2 / (1.0 - ADAM_B1 ** ADAM_STEP)
    v_hat = v2 / (1.0 - ADAM_B2 ** ADAM_STEP)
    delta = -ADAM_LR * (m_hat / (jnp.sqrt(v_hat) + ADAM_EPS) + ADAM_WD * w)
    return delta, m2, v2


def _adamw(gs, w, m, v, name):
    R, Cc = w.shape
    tr = 256 if R % 256 == 0 else R
    ng = len(gs)

    def body(*refs):
        g = refs[0][...]
        for r in refs[1:ng]:
            g = g + r[...]
        w_ref, m_ref, v_ref, g_ref, d_ref, m2_ref, v2_ref = refs[ng:]
        delta, m2, v2 = _adam_math(g, w_ref[...], m_ref[...], v_ref[...])
        g_ref[...] = g
        d_ref[...] = delta
        m2_ref[...] = m2
        v2_ref[...] = v2

    spec = pl.BlockSpec((tr, Cc), lambda i: (i, 0))
    return pl.pallas_call(
        body, name=name, grid=(R // tr,),
        in_specs=[spec] * (ng + 3), out_specs=[spec] * 4,
        out_shape=[jax.ShapeDtypeStruct((R, Cc), f32)] * 4,
        compiler_params=_cp(("arbitrary",)),
    )(*gs, w, m, v)


def _adamw_rows(g, w, m, v, name):
    R, _, Cc = w.shape
    tr = R // 9

    def body(g_ref, w_ref, m_ref, v_ref, go_ref, d_ref, m2_ref, v2_ref):
        g = g_ref[...]
        delta, m2, v2 = _adam_math(g, w_ref[...], m_ref[...], v_ref[...])
        go_ref[...] = g
        d_ref[...] = delta
        m2_ref[...] = m2
        v2_ref[...] = v2

    spec = pl.BlockSpec((tr, 1, Cc), lambda i: (i, 0, 0))
    return pl.pallas_call(
        body, name=name, grid=(R // tr,),
        in_specs=[spec] * 4, out_specs=[spec] * 4,
        out_shape=[jax.ShapeDtypeStruct((R, 1, Cc), f32)] * 4,
        compiler_params=_cp(("arbitrary",)),
    )(g, w, m, v)


def _place():
    x, y, c = lax.axis_index("x"), lax.axis_index("y"), lax.axis_index("c")
    return x, y, c, [(1 - x, y), (x, 1 - y), (1 - x, 1 - y)]


def _gather_weights(cm, halved, whole):
    R, _, Cc = cm.shape
    hw = Cc // 2
    shards = [jax.ShapeDtypeStruct((R, Cc), bf16)] + list(halved) + list(whole)
    nh = 1 + len(halved)
    na = len(shards)

    def body(*refs):
        srcs, dsts = refs[:na], refs[na:2 * na]
        send_sems, recv_sems, loc_sems = refs[2 * na:2 * na + 3]
        stage = refs[2 * na + 3:3 * na + 3]
        raw = refs[3 * na + 3]
        x, y, c, chips = _place()
        me = 2 * x + y
        loads = [pltpu.make_async_copy(srcs[0], raw, loc_sems.at[0])]
        loads += [pltpu.make_async_copy(srcs[i], stage[i], loc_sems.at[i]) for i in range(1, na)]
        locs = [pltpu.make_async_copy(v, d.at[me], loc_sems.at[i]) for i, (v, d) in enumerate(zip(stage, dsts))]
        for cp in loads:
            cp.start()

        def half_of(ref, i, half):
            return ref.at[:, pl.ds(pl.multiple_of(half * hw, hw), hw)] if i == 0 else ref.at[half]

        def ici(j, i, slot):
            px, py = chips[j]
            src = half_of(stage[0] if i == 0 else srcs[i], i, c) if i < nh else srcs[i]
            dst = half_of(dsts[i].at[slot], i, c) if i < nh else dsts[i].at[slot]
            return pltpu.make_async_remote_copy(
                src_ref=src, dst_ref=dst, send_sem=send_sems.at[na * j + i], recv_sem=recv_sems.at[na * j + i],
                device_id=(px, py, c), device_id_type=MESH)

        def d2d(j, i, half):
            px, py = chips[j]
            blk = half_of(dsts[i].at[2 * px + py], i, half)
            return pltpu.make_async_remote_copy(
                src_ref=blk, dst_ref=blk, send_sem=send_sems.at[3 * na + nh * j + i],
                recv_sem=recv_sems.at[3 * na + nh * j + i], device_id=(x, y, 1 - c), device_id_type=MESH)

        sends = [ici(j, i, me) for j in range(3) for i in range(1, na)]
        for cp in sends:
            cp.start()
        loads[0].wait()
        stage[0][...] = raw[:, 0, :].astype(bf16)
        first = [ici(j, 0, me) for j in range(3)]
        for cp in first:
            cp.start()
        sends += first
        locs[0].start()
        for ld, st in zip(loads[1:], locs[1:]):
            ld.wait()
            st.start()
        for j, (px, py) in enumerate(chips):
            for i in range(na):
                ici(j, i, 2 * px + py).wait_recv()
                if i < nh:
                    fwd = d2d(j, i, c)
                    fwd.start()
                    sends.append(fwd)
        for j in range(3):
            for i in range(nh):
                d2d(j, i, 1 - c).wait_recv()
        for cp in sends:
            cp.wait_send()
        for cp in locs:
            cp.wait()

    nsem = 3 * na + 3 * nh
    return pl.pallas_call(
        body, name="gather_weights",
        in_specs=[ANY] * na, out_specs=[ANY] * na,
        out_shape=[jax.ShapeDtypeStruct((4,) + s.shape, s.dtype) for s in shards],
        scratch_shapes=[pltpu.SemaphoreType.DMA((nsem,)), pltpu.SemaphoreType.DMA((nsem,)),
                        pltpu.SemaphoreType.DMA((na,))] + [pltpu.VMEM(s.shape, s.dtype) for s in shards]
        + [pltpu.VMEM(cm.shape, cm.dtype)],
        compiler_params=pltpu.CompilerParams(has_side_effects=True, vmem_limit_bytes=VMEM_LIMIT),
    )(cm, *halved, *whole)


def _swap_halves(gs):
    na = len(gs)
    jobs = [(i, q) for i in range(na) for q in range(gs[i].shape[0])]

    def body(*refs):
        srcs, dsts = refs[:na], refs[na:2 * na]
        send_sems, recv_sems = refs[2 * na:]
        x, y, c, _ = _place()
        cps = [pltpu.make_async_remote_copy(
            src_ref=srcs[i].at[q, 1 - c], dst_ref=dsts[i].at[q], send_sem=send_sems.at[k],
            recv_sem=recv_sems.at[k], device_id=(x, y, 1 - c), device_id_type=MESH)
            for k, (i, q) in enumerate(jobs)]
        for cp in cps:
            cp.start()
        for cp in cps:
            cp.wait()

    return pl.pallas_call(
        body, name="swap_halves",
        in_specs=[ANY] * na, out_specs=[ANY] * na,
        out_shape=[jax.ShapeDtypeStruct(g.shape[0:1] + g.shape[2:], g.dtype) for g in gs],
        scratch_shapes=[pltpu.SemaphoreType.DMA((len(jobs),)), pltpu.SemaphoreType.DMA((len(jobs),))],
        compiler_params=pltpu.CompilerParams(has_side_effects=True),
    )(*gs)


def _add_split(cm_arr, g, s):
    _, _, R, Cg = g.shape
    tr = 128

    def body(sc_ref, g_ref, s_ref, b_ref, o_ref):
        p = g_ref[0, 0] + s_ref[0]
        own = None
        for q in range(4):
            blk = p[:, SHARD_COLS * q:SHARD_COLS * (q + 1)]
            b_ref[q] = blk.astype(bf16)
            mine = jnp.where(sc_ref[1] == q, blk, 0.0)
            own = mine if own is None else own + mine
        o_ref[...] = own

    return pl.pallas_call(
        body, name="add_w_in",
        grid_spec=pltpu.PrefetchScalarGridSpec(
            num_scalar_prefetch=1, grid=(R // tr,),
            in_specs=[pl.BlockSpec((1, 1, tr, Cg), lambda i, sc: (0, sc[0], i, 0)),
                      pl.BlockSpec((1, tr, Cg), lambda i, sc: (0, i, 0))],
            out_specs=[pl.BlockSpec((4, tr, SHARD_COLS), lambda i, sc: (0, i, 0)),
                       pl.BlockSpec((tr, SHARD_COLS), lambda i, sc: (i, 0))]),
        out_shape=[jax.ShapeDtypeStruct((4, R, SHARD_COLS), bf16), jax.ShapeDtypeStruct((R, SHARD_COLS), f32)],
        compiler_params=_cp(("arbitrary",)),
    )(cm_arr, g, s)


def _add_halves(c_arr, g, s, name):
    Q, _, R, Cc = g.shape
    tr = min(R, 128)

    def body(c_ref, g_ref, s_ref, b_ref, f_ref):
        p = g_ref[0, 0] + s_ref[0]
        f_ref[0] = p
        b_ref[0] = p.astype(bf16)

    blk = pl.BlockSpec((1, tr, Cc), lambda q, i, cr: (q, i, 0))
    return pl.pallas_call(
        body, name=name,
        grid_spec=pltpu.PrefetchScalarGridSpec(
            num_scalar_prefetch=1, grid=(Q, R // tr),
            in_specs=[pl.BlockSpec((1, 1, tr, Cc), lambda q, i, cr: (q, cr[0], i, 0)), blk], out_specs=[blk, blk]),
        out_shape=[jax.ShapeDtypeStruct((Q, R, Cc), bf16), jax.ShapeDtypeStruct((Q, R, Cc), f32)],
        compiler_params=_cp(("arbitrary", "arbitrary")),
    )(c_arr, g, s)


_FLIPS = [(fx, fy, fc) for fx in (0, 1) for fy in (0, 1) for fc in (0, 1)][1:]


def _sum_blocks(own, r, name, transposed=False):
    R, Cc = own.shape
    tr = min(R, 256)

    def body(own_ref, r_ref, o_ref):
        acc = own_ref[...]
        for j in range(3):
            acc = acc + r_ref[j].astype(f32)
        o_ref[...] = acc.T if transposed else acc

    return pl.pallas_call(
        body, name=name, grid=(R // tr,),
        in_specs=[pl.BlockSpec((tr, Cc), lambda i: (i, 0)), pl.BlockSpec((3, tr, Cc), lambda i: (0, i, 0))],
        out_specs=pl.BlockSpec((Cc, tr), lambda i: (0, i)) if transposed else pl.BlockSpec((tr, Cc), lambda i: (i, 0)),
        out_shape=jax.ShapeDtypeStruct((Cc, R) if transposed else (R, Cc), f32),
        compiler_params=_cp(("arbitrary",)),
    )(own, r)


def _sum_packs(me8_arr, pack, rp):
    R = pack.shape[0]

    def body(me_ref, pk_ref, rp_ref, o_ref):
        me8 = me_ref[0]
        acc = None
        for d in range(8):
            rel = d ^ me8
            term = jnp.where(rel == 0, pk_ref[...], rp_ref[jnp.maximum(rel - 1, 0)])
            acc = term if acc is None else acc + term
        o_ref[...] = acc

    return pl.pallas_call(
        body, name="sum_packs",
        grid_spec=pltpu.PrefetchScalarGridSpec(
            num_scalar_prefetch=1, grid=(1,),
            in_specs=[pl.BlockSpec((R, 128), lambda i, mr: (0, 0)), pl.BlockSpec((7, R, 128), lambda i, mr: (0, 0, 0))],
            out_specs=pl.BlockSpec((R, 128), lambda i, mr: (0, 0))),
        out_shape=jax.ShapeDtypeStruct((R, 128), f32),
        compiler_params=_cp(("arbitrary",)),
    )(me8_arr, pack, rp)


def _swap_finished(fs, pack):
    na = len(fs)
    R = pack.shape[0]

    def body(*refs):
        srcs, pk = refs[:na], refs[na]
        dsts, rp = refs[na + 1:2 * na + 1], refs[2 * na + 1]
        send_sems, recv_sems = refs[2 * na + 2:]
        x, y, c, _ = _place()
        cps = [pltpu.make_async_remote_copy(
            src_ref=srcs[i], dst_ref=dsts[i], send_sem=send_sems.at[i], recv_sem=recv_sems.at[i],
            device_id=(x, y, 1 - c), device_id_type=MESH) for i in range(na)]
        cps += [pltpu.make_async_remote_copy(
            src_ref=pk, dst_ref=rp.at[k], send_sem=send_sems.at[na + k], recv_sem=recv_sems.at[na + k],
            device_id=(x ^ fx, y ^ fy, c ^ fc), device_id_type=MESH) for k, (fx, fy, fc) in enumerate(_FLIPS)]
        for cp in cps:
            cp.start()
        for cp in cps:
            cp.wait()

    return pl.pallas_call(
        body, name="swap_finished",
        in_specs=[ANY] * (na + 1), out_specs=[ANY] * (na + 1),
        out_shape=[jax.ShapeDtypeStruct(f.shape, f.dtype) for f in fs] + [jax.ShapeDtypeStruct((7, R, 128), f32)],
        scratch_shapes=[pltpu.SemaphoreType.DMA((na + 7,)), pltpu.SemaphoreType.DMA((na + 7,))],
        compiler_params=pltpu.CompilerParams(has_side_effects=True),
    )(*fs, pack)


def _adamw_halves(c_arr, mine, peer, w, m, v, name):
    _, R, Cc = w.shape
    tr = min(R, 256)

    def body(c_ref, mine_ref, peer_ref, w_ref, m_ref, v_ref, g_ref, d_ref, m2_ref, v2_ref):
        g = jnp.where(pl.program_id(0) == c_ref[0], mine_ref[...], peer_ref[...])
        delta, m2, v2 = _adam_math(g, w_ref[0], m_ref[0], v_ref[0])
        g_ref[0] = g
        d_ref[0] = delta
        m2_ref[0] = m2
        v2_ref[0] = v2

    half = pl.BlockSpec((tr, Cc), lambda hh, i, cr: (i, 0))
    full = pl.BlockSpec((1, tr, Cc), lambda hh, i, cr: (hh, i, 0))
    return pl.pallas_call(
        body, name=name,
        grid_spec=pltpu.PrefetchScalarGridSpec(
            num_scalar_prefetch=1, grid=(2, R // tr), in_specs=[half, half, full, full, full], out_specs=[full] * 4),
        out_shape=[jax.ShapeDtypeStruct((2, R, Cc), f32)] * 4,
        compiler_params=_cp(("arbitrary", "arbitrary")),
    )(c_arr, mine, peer, w, m, v)


def _rows8(a):
    flat = a.reshape(-1)
    n = flat.shape[0]
    rows = -(-n // 1024) * 8
    return jnp.pad(flat, (0, rows * 128 - n)).reshape(rows, 128)


def kernel(x, meta_tokens, norm_w, w_in, conv_w, hg_lb_logits, hg_norm_w, gdn_A_log, gdn_dt_bias, gdn_norm_w, w_out, final_norm_w, loss_target, m_meta_tokens, m_norm_w, m_w_in, m_conv_w, m_hg_lb_logits, m_hg_norm_w, m_gdn_A_log, m_gdn_dt_bias, m_gdn_norm_w, m_w_out, m_final_norm_w, v_meta_tokens, v_norm_w, v_w_in, v_conv_w, v_hg_lb_logits, v_hg_norm_w, v_gdn_A_log, v_gdn_dt_bias, v_gdn_norm_w, v_w_out, v_final_norm_w):
    me = 2 * lax.axis_index("x") + lax.axis_index("y")

    g_win, g_wout, g_conv, g_meta = _gather_weights(
        jnp.transpose(w_in, (2, 0, 1)), [w_out[0].astype(bf16).reshape(2, D // 8, D)], [conv_w[0], meta_tokens])
    wout_full = g_wout.reshape(D, D)
    conv4 = jnp.transpose(g_conv, (1, 0, 2)).reshape(4, 1, 3 * HD)
    meta_full = jnp.transpose(g_meta, (1, 0, 2)).reshape(N_META, D)

    c_arr = lax.axis_index("c").reshape(1).astype(jnp.int32)

    def chip_partials(gw, g_wout_part):
        g_in2 = gw.reshape(1, 2, D // 2, PC)
        g_out4 = g_wout_part.reshape(4, 2, D // 8, D)
        s_in, s_out = _swap_halves([g_in2, g_out4])
        pb_blocks, own_in = _add_split(jnp.concatenate([c_arr, me.reshape(1).astype(jnp.int32)]), g_in2, s_in)
        pb_out, pf_out = _add_halves(c_arr, g_out4, s_out, "add_w_out")
        own_out = lax.dynamic_index_in_dim(pf_out, me, axis=0, keepdims=False)
        return [pb_blocks, pb_out], [own_in, own_out]

    (loss8, grad_x, d_meta, d_nw, d_conv, d_lb, d_hgw, d_alog, d_dtb, d_gdw, d_fw, pfs, rs) = _local_step(
        x, loss_target, g_win, wout_full, conv4, meta_full, norm_w, hg_lb_logits, hg_norm_w, gdn_A_log, gdn_dt_bias,
        gdn_norm_w, final_norm_w, chip_partials)

    pack = jnp.concatenate([
        loss8, d_nw[0].reshape(8, 128), d_lb.reshape(8, 128), d_hgw, _rows8(d_alog[0, :H]), _rows8(d_dtb[0, :H]),
        d_gdw, d_fw[0].reshape(8, 128), d_meta.reshape(128, 128), d_conv.reshape(48, 128)], axis=0)
    return _reduce_and_update(
        me, c_arr, grad_x, pfs, rs, pack, meta_tokens, norm_w, w_in, conv_w, hg_lb_logits, hg_norm_w, gdn_A_log,
        gdn_dt_bias, gdn_norm_w, w_out, final_norm_w, m_meta_tokens, m_norm_w, m_w_in, m_conv_w, m_hg_lb_logits,
        m_hg_norm_w, m_gdn_A_log, m_gdn_dt_bias, m_gdn_norm_w, m_w_out, m_final_norm_w, v_meta_tokens, v_norm_w, v_w_in,
        v_conv_w, v_hg_lb_logits, v_hg_norm_w, v_gdn_A_log, v_gdn_dt_bias, v_gdn_norm_w, v_w_out, v_final_norm_w)


def _local_step(x, loss_target, w4, wout_full, conv4, meta_full, norm_w, hg_lb_logits, hg_norm_w, gdn_A_log, gdn_dt_bias,
                gdn_norm_w, final_norm_w, chip_partials):
    head = jnp.concatenate([jnp.zeros((PAD, D), f32), meta_full], axis=0)
    target = loss_target.reshape(NB * SEQ, D)
    l0, l1 = hg_lb_logits[0:1], hg_lb_logits[1:2]
    alog = jnp.pad(gdn_A_log, ((0, 0), (0, DK - H)))
    dtb = jnp.pad(gdn_dt_bias, ((0, 0), (0, DK - H)))
    fw = final_norm_w.reshape(1, D)

    hflat, proj, ut, cv2, wbig = _in_proj(x.reshape(NB * SEQ, D), head, norm_w, w4, conv4)
    proj3 = proj.reshape(NB, TP, PC)
    cv = cv2.reshape(NB, TP, 3 * HD)
    o_hg, s_hg, o_gd, s_gd, t_gd = _mix_fwd(proj3, cv, l0, l1, alog, dtb)
    (loss8, d_ohg, d_ogd, d_zhg, d_zgd, dh_res, g_wout_part, d_hgw, d_gdw, d_fw) = _out_loss(
        o_hg.reshape(N, HD), o_gd.reshape(N, HD), proj, hg_norm_w, gdn_norm_w, wout_full, hflat, fw, target)
    d_hg, d_l0, d_l1 = _hg_bwd(proj3, l0, l1, s_hg, d_ohg.reshape(NB, TP, HD))
    d_cv, d_ab, d_alog, d_dtb = _gd_bwd(cv, proj3, alog, dtb, s_gd, t_gd, d_ogd.reshape(NB, TP, HD))
    d_hg2, d_ab2 = d_hg.reshape(N, 3 * HD), d_ab.reshape(N, DK)
    others = [(d_hg2, COL_HG), (d_zhg, COL_ZHG), (d_zgd, COL_ZGD), (d_ab2, COL_AB)]
    gw, d_qkv2, d_conv4 = _w_grad(ut, others, d_cv.reshape(N, 3 * HD), proj, conv4)
    pieces = [(d_hg2, COL_HG), (d_zhg, COL_ZHG), (d_qkv2, COL_QKV), (d_zgd, COL_ZGD), (d_ab2, COL_AB)]
    pbs, pfs = chip_partials(gw, g_wout_part) if chip_partials else ([], [gw, g_wout_part])
    dh, d_nw, *rs = _in_bwd(pieces, wbig, hflat, norm_w, dh_res, pbs)

    dh3 = dh.reshape(NB, TP, D)
    grad_x = dh3[:, PAD + N_META:, :]
    d_meta = jnp.sum(dh3[:, PAD:PAD + N_META, :], axis=0)
    d_conv = d_conv4[:, 0, :]
    d_lb = jnp.concatenate([d_l0[0:1], d_l1[0:1]], axis=0)
    return loss8, grad_x, d_meta, d_nw, d_conv, d_lb, d_hgw, d_alog, d_dtb, d_gdw, d_fw, pfs, rs


def _reduce_and_update(me, c_arr, grad_x, pfs, rs, pack, meta_tokens, norm_w, w_in, conv_w, hg_lb_logits, hg_norm_w,
                       gdn_A_log, gdn_dt_bias, gdn_norm_w, w_out, final_norm_w, m_meta_tokens, m_norm_w, m_w_in, m_conv_w,
                       m_hg_lb_logits, m_hg_norm_w, m_gdn_A_log, m_gdn_dt_bias, m_gdn_norm_w, m_w_out, m_final_norm_w,
                       v_meta_tokens, v_norm_w, v_w_in, v_conv_w, v_hg_lb_logits, v_hg_norm_w, v_gdn_A_log, v_gdn_dt_bias,
                       v_gdn_norm_w, v_w_out, v_final_norm_w):
    (own_in, own_out), (r_in, r_out) = pfs, rs
    f_in = _sum_blocks(own_in, r_in, "sum_w_in", transposed=True)
    f_out = _sum_blocks(own_out, r_out, "sum_w_out")
    o_in, o_out, r_pack = _swap_finished([f_in, f_out], pack)
    me8_arr = (2 * me + lax.axis_index("c")).reshape(1).astype(jnp.int32)
    small = _sum_packs(me8_arr, pack, r_pack)

    half_out = lambda a: a[0].reshape(2, D // 8, D)
    is0 = lax.axis_index("c") == 0
    g_in = jnp.concatenate([jnp.where(is0, f_in, o_in), jnp.where(is0, o_in, f_in)], axis=1)
    to_cm = lambda a: jnp.transpose(a, (2, 0, 1))
    gi, di, mi, vi = [jnp.transpose(a, (1, 2, 0))[0] for a in _adamw_rows(
        g_in.reshape(SHARD_COLS, 1, D), to_cm(w_in), to_cm(m_w_in), to_cm(v_w_in), "adamw_w_in")]
    go, do_, mo, vo = [a.reshape(D // 4, D) for a in _adamw_halves(
        c_arr, f_out, o_out, half_out(w_out), half_out(m_w_out), half_out(v_w_out), "adamw_w_out")]

    g_meta_full = small[64:192].reshape(N_META, D)
    g_meta_loc = lax.dynamic_slice(g_meta_full, (0, me * 256), (N_META, 256))
    gm, dm, mm_, vm = _adamw([g_meta_loc], meta_tokens, m_meta_tokens, v_meta_tokens, "adamw_meta")
    g_conv_full = small[192:240].reshape(4, 1536)
    g_conv_loc = lax.dynamic_slice(g_conv_full, (0, me * 384), (4, 384))
    gc, dc, mc, vc = _adamw([g_conv_loc], conv_w[0], m_conv_w[0], v_conv_w[0], "adamw_conv")

    reps = [(norm_w, m_norm_w, v_norm_w), (hg_lb_logits, m_hg_lb_logits, v_hg_lb_logits),
            (hg_norm_w, m_hg_norm_w, v_hg_norm_w), (gdn_A_log, m_gdn_A_log, v_gdn_A_log),
            (gdn_dt_bias, m_gdn_dt_bias, v_gdn_dt_bias), (gdn_norm_w, m_gdn_norm_w, v_gdn_norm_w),
            (final_norm_w, m_final_norm_w, v_final_norm_w)]
    wp = jnp.concatenate([_rows8(t[0]) for t in reps], axis=0)
    mp = jnp.concatenate([_rows8(t[1]) for t in reps], axis=0)
    vp = jnp.concatenate([_rows8(t[2]) for t in reps], axis=0)
    gr, dr, mr, vr = _adamw([small[8:64]], wp, mp, vp, "adamw_small")

    def unpack(p):
        outs = []
        for i, t in enumerate(reps):
            n = t[0].size
            outs.append(p[8 * i:8 * i + 8].reshape(-1)[:n].reshape(t[0].shape))
        return outs

    def leaves(meta_v, conv_v, in_v, out_v, rep_p):
        nw, lb, hgw, al, db, gdw, fwv = unpack(rep_p)
        return [meta_v, nw, in_v[None], conv_v[None], lb, hgw, al, db, gdw, out_v[None], fwv]

    loss = small[0, 0]
    return (loss, grad_x, *leaves(gm, gc, gi, go, gr), *leaves(dm, dc, di, do_, dr),
            *leaves(mm_, mc, mi, mo, mr), *leaves(vm, vc, vi, vo, vr))
```

```python
import functools

import jax
import jax.numpy as jnp
from jax import lax
from jax.experimental import pallas as pl
from jax.experimental.pallas import tpu as pltpu

f32 = jnp.float32
bf16 = jnp.bfloat16
MESH = pl.DeviceIdType.MESH
ANY = pl.BlockSpec(memory_space=pl.ANY)

D = 1024
NB = 2
N_META = 16
SEQ = 2048
PAD = 48
TP = PAD + N_META + SEQ
C = 64
NCH = TP // C
N = NB * TP
H = 4
DK = 128
HD = H * DK
PC = 4224
IN_COLS = 4104
SHARD_COLS = IN_COLS // 4
COL_HG, COL_ZHG, COL_QKV, COL_ZGD, COL_AB = 0, 3 * HD, 4 * HD, 7 * HD, 8 * HD
EPS = 1e-6
ADAM_LR, ADAM_B1, ADAM_B2, ADAM_EPS, ADAM_WD, ADAM_STEP = 0.001, 0.9, 0.999, 1e-08, 0.01, 10
VMEM_LIMIT = 56 * 1024 * 1024

P_HG = dict(lvl=1, av=1, qs=1, su=1)
P_GD = dict(kk=1, inv=1, sol=1, ws=1, qk=1, o=1, su=1)


def _cp(sem=None, **kw):
    return pltpu.CompilerParams(dimension_semantics=sem, vmem_limit_bytes=VMEM_LIMIT, **kw)


_DIMS = {"nn": (((1,), (0,)), ((), ())), "nt": (((1,), (1,)), ((), ())), "tn": (((0,), (0,)), ((), ()))}


def _split(x):
    hi = x.astype(bf16)
    return hi, (x - hi.astype(f32)).astype(bf16)


def _dg(a, b, kind, passes):
    d = lambda x, y: lax.dot_general(x, y, _DIMS[kind], preferred_element_type=f32)
    if passes == 1:
        return d(a.astype(bf16), b.astype(bf16))
    ah, al = _split(a)
    bh, bl = _split(b)
    return d(ah, bh) + d(ah, bl) + d(al, bh)


@functools.partial(jax.custom_vjp, nondiff_argnums=(2, 3))
def mmx(a, b, kind, passes):
    return _dg(a, b, kind, passes)


def _mmx_fwd(a, b, kind, passes):
    return _dg(a, b, kind, passes), (a, b)


def _mmx_bwd(kind, passes, res, g):
    a, b = res
    if kind == "nn":
        return _dg(g, b, "nt", passes), _dg(a, g, "tn", passes)
    if kind == "nt":
        return _dg(g, b, "nn", passes), _dg(g, a, "tn", passes)
    return _dg(b, g, "nt", passes), _dg(a, g, "nn", passes)


mmx.defvjp(_mmx_fwd, _mmx_bwd)


def _mask_dg(mask, x):
    xh, xl = _split(x)
    return jnp.dot(jnp.concatenate([mask, mask], axis=1), jnp.concatenate([xh, xl], axis=0), preferred_element_type=f32)


@functools.partial(jax.custom_vjp, nondiff_argnums=(2,))
def mask_mm(mask, x, bwd_passes):
    return _mask_dg(mask, x)


def _mask_fwd(mask, x, bwd_passes):
    return _mask_dg(mask, x), mask


def _mask_bwd(bwd_passes, mask, g):
    d = lambda y: lax.dot_general(mask, y, _DIMS["tn"], preferred_element_type=f32)
    if bwd_passes == 1:
        return None, d(g.astype(bf16))
    gh, gl = _split(g)
    return None, d(gh) + d(gl)


mask_mm.defvjp(_mask_fwd, _mask_bwd)


def bdot(a, b):
    return jnp.dot(a.astype(bf16), b.astype(bf16), preferred_element_type=f32)


def bdot_nt(a, b):
    return lax.dot_general(a.astype(bf16), b.astype(bf16), _DIMS["nt"], preferred_element_type=f32)


def bdot_tn(a, b):
    return lax.dot_general(a.astype(bf16), b.astype(bf16), _DIMS["tn"], preferred_element_type=f32)


def _iota2(n, m):
    return lax.broadcasted_iota(jnp.int32, (n, m), 0), lax.broadcasted_iota(jnp.int32, (n, m), 1)


sigmoid = jax.nn.sigmoid


def silu(x):
    return x * sigmoid(x)


def softplus(x):
    return jnp.maximum(x, 0.0) + jnp.log(1.0 + jnp.exp(-jnp.abs(x)))


def rmsnorm(x, w):
    return x * lax.rsqrt(jnp.mean(x * x, axis=-1, keepdims=True) + EPS) * w


def hg_masks():
    t, r = _iota2(C, C)
    mats = [r <= t, r > t]
    lvl = []
    for l in range(1, 7):
        sz = 1 << l
        half = sz >> 1
        seg_t = t >> l
        upper_t = (t & (sz - 1)) >= half
        mid_t = seg_t * sz + half - 1
        mats.append((upper_t & (r > mid_t) & (r <= t)) | ((~upper_t) & (r > t) & (r <= mid_t)))
        lvl.append(((seg_t == (r >> l)) & upper_t & ((r & (sz - 1)) < half)).astype(f32))
    stk = jnp.concatenate([m.astype(bf16) for m in mats], axis=0)
    return stk, lvl, (t == r).astype(f32)


def _head(a, h):
    return a[:, h * DK:(h + 1) * DK]


def _run(*gens):
    results = [None] * len(gens)
    live = list(range(len(gens)))
    while live:
        for i in list(live):
            try:
                next(gens[i])
            except StopIteration as e:
                results[i] = e.value
                live.remove(i)
    return results


def hg_chunk(St, ps, l0, l1):
    return _run(hg_stages(St, ps, l0, l1))[0]


def gd_chunk(S, cs, abs_, alog, dtb, t_saved=None):
    return _run(gd_stages(S, cs, abs_, alog, dtb, t_saved))[0]


def mix_chunk(St, ps, l0, l1, S, cs, abs_, alog, dtb):
    (sn_h, o_h), (sn_g, o_g, t_pack) = _run(hg_stages(St, ps, l0, l1), gd_stages(S, cs, abs_, alog, dtb))
    return sn_h, o_h, sn_g, o_g, t_pack


def hg_stages(St, ps, l0, l1):
    m = jnp.maximum(l0, l1)
    e0 = jnp.exp(l0 - m)
    e1 = jnp.exp(l1 - m)
    lb = e0 / (e0 + e1)
    stk, lvl, eye = hg_masks()
    msk = [eye] + lvl
    trow = lax.broadcasted_iota(jnp.int32, (C, 1), 0)
    upper = [(trow & ((1 << l) - 1)) >= (1 << (l - 1)) for l in range(1, 7)]
    qs, ks, vs, qG, kR, eGl = [], [], [], [], [], []
    for p in ps:
        pq, pf, v = p[:, 0:HD], p[:, HD:2 * HD], p[:, 2 * HD:3 * HD]
        q = silu(pq)
        f = lb + (1.0 - lb) * sigmoid(pf)
        k = 1.0 - f
        logf = jnp.log(f)
        Dm = mask_mm(stk, logf, 1)
        z = [jnp.where(up, q, k) * jnp.exp(Dm[(2 + i) * C:(3 + i) * C]) for i, up in enumerate(upper)]
        qs.append([q] + z)
        ks.append([k] + z)
        vs.append(v)
        qG.append(q * jnp.exp(Dm[0:C]))
        kR.append(k * jnp.exp(Dm[C:2 * C]))
        eGl.append(jnp.exp(jnp.sum(logf, axis=0, keepdims=True)))
    yield
    units = [(b, h) for b in range(len(ps)) for h in range(H)]
    parts = []
    for i in range(7):
        parts.append([msk[i] * mmx(_head(qs[b][i], h), _head(ks[b][i], h), "nt", P_HG["lvl"]) for b, h in units])
        yield
    A = [functools.reduce(lambda x, y: x + y, [parts[i][n] for i in range(7)]) for n in range(len(units))]
    qS = [mmx(_head(qG[b], h), St[n], "nt", P_HG["qs"]) for n, (b, h) in enumerate(units)]
    Sn = [St[n] * _head(eGl[b], h) + mmx(_head(vs[b], h), _head(kR[b], h), "tn", P_HG["su"])
          for n, (b, h) in enumerate(units)]
    yield
    outs = [mmx(A[n], _head(vs[b], h), "nn", P_HG["av"]) + qS[n] for n, (b, h) in enumerate(units)]
    return tuple(Sn), tuple(jnp.concatenate(outs[b * H:(b + 1) * H], axis=1) for b in range(len(ps)))


@jax.custom_vjp
def use_inverse(A, T):
    return T


def _use_inverse_fwd(A, T):
    return T, T


def _use_inverse_bwd(T, g):
    return -_dg(T, _dg(g, T, "nt", P_GD["inv"]), "tn", P_GD["inv"]), jnp.zeros_like(T)


use_inverse.defvjp(_use_inverse_fwd, _use_inverse_bwd)


def gd_stages(S, cs, abs_, alog, dtb, t_saved=None):
    t, r = _iota2(C, C)
    tri = (r <= t).astype(bf16)
    ups = (r > t).astype(bf16)
    lane = lax.broadcasted_iota(jnp.int32, (1, DK), 1)
    subl = lax.broadcasted_iota(jnp.int32, (8, 1), 0)
    eye = (t == r).astype(f32)
    strict = (r < t).astype(f32)
    bd = ((t >> 4) == (r >> 4)).astype(f32)
    qa, ka, va, b4, gam4, grev4, gam4T, glast4 = [], [], [], [], [], [], [], []
    for c, ab in zip(cs, abs_):
        qa.append(silu(c[:, 0:HD]))
        ka.append(silu(c[:, HD:2 * HD]))
        va.append(silu(c[:, 2 * HD:3 * HD]))
        g4 = -jnp.exp(alog) * softplus(ab + dtb)
        b4.append(sigmoid(ab))
        gam4.append(mask_mm(tri, g4, 2))
        grev4.append(mask_mm(ups, g4, 2))
        gam4T.append(gam4[-1].T)
        glast4.append(jnp.sum(g4, axis=0, keepdims=True))
    yield
    units = [(b, h) for b in range(len(cs)) for h in range(H)]
    nu = range(len(units))
    inv = lambda a, b: [mmx(a[n], b[n], "nn", P_GD["inv"]) for n in nu]
    v = [_head(va[b], h) for b, h in units]
    q = [_head(qa[b], h) for b, h in units]
    k = [_head(ka[b], h) for b, h in units]
    q = [x * lax.rsqrt(jnp.sum(x * x, -1, keepdims=True) + EPS) * (DK ** -0.5) for x in q]
    k = [x * lax.rsqrt(jnp.sum(x * x, -1, keepdims=True) + EPS) for x in k]
    oh = [(lane == h).astype(f32) for h in range(H)]
    gam_c = [jnp.sum(gam4[b] * oh[h], -1, keepdims=True) for b, h in units]
    grev_c = [jnp.sum(grev4[b] * oh[h], -1, keepdims=True) for b, h in units]
    beta = [jnp.sum(b4[b] * (lane == h + H).astype(f32), -1, keepdims=True) for b, h in units]
    glast = [jnp.sum(glast4[b] * oh[h], -1, keepdims=True) for b, h in units]
    gam_r = [jnp.sum(gam4T[b][0:8, :] * (subl == h).astype(f32), axis=0, keepdims=True) for b, h in units]
    dec = [jnp.exp(jnp.where(r <= t, gam_c[n] - gam_r[n], -1e30)) for n in nu]
    egam = [jnp.exp(gam_c[n]) for n in nu]
    kk = [mmx(k[n], k[n], "nt", P_GD["kk"]) for n in nu]
    qk = [mmx(q[n], k[n], "nt", P_GD["qk"]) * dec[n] for n in nu]
    yield
    A = [beta[n] * kk[n] * dec[n] * strict for n in nu]
    Dg = [A[n] * bd for n in nu]
    L = [A[n] - Dg[n] for n in nu]
    if t_saved is None:
        ImD = [eye - Dg[n] for n in nu]
        D2 = inv(Dg, Dg)
        yield
        P1 = inv(ImD, [eye + x for x in D2])
        D4 = inv(D2, D2)
        yield
        P2 = inv(P1, [eye + x for x in D4])
        D8 = inv(D4, D4)
        yield
        M = inv(P2, [eye + x for x in D8])
        yield
        Nn = inv(M, L)
        yield
        N2 = inv(Nn, Nn)
        yield
        T1 = inv([eye - x for x in Nn], [eye + x for x in N2])
        yield
        Tinv = inv(T1, M)
        yield
    else:
        Tinv = [use_inverse(A[n], t_saved[b][:, h * DK:h * DK + C]) for n, (b, h) in enumerate(units)]
    rhs = [jnp.concatenate([beta[n] * v[n], (beta[n] * egam[n]) * k[n]], axis=1) for n in nu]
    sol = [mmx(Tinv[n], rhs[n], "nn", P_GD["sol"]) for n in nu]
    yield
    qwS = [mmx(jnp.concatenate([q[n] * egam[n], sol[n][:, DK:2 * DK]], axis=0), S[n], "nn", P_GD["ws"]) for n in nu]
    yield
    u = [sol[n][:, 0:DK] - qwS[n][C:2 * C] for n in nu]
    outs = [qwS[n][0:C] + mmx(qk[n], u[n], "nn", P_GD["o"]) for n in nu]
    Sn = [jnp.exp(glast[n]) * S[n] + mmx(k[n] * jnp.exp(grev_c[n]), u[n], "tn", P_GD["su"]) for n in nu]
    zpad = jnp.zeros((C, DK - C), f32)
    t_pack = tuple(jnp.concatenate([x for n in range(b * H, (b + 1) * H) for x in (lax.stop_gradient(Tinv[n]), zpad)],
                                   axis=1) for b in range(len(cs)))
    return tuple(Sn), tuple(jnp.concatenate(outs[b * H:(b + 1) * H], axis=1) for b in range(len(cs))), t_pack


def _chunk_index(tile_chunks, k):
    def index(i):
        chunk = tile_chunks * i + k
        b = chunk // NCH
        return jnp.maximum((SEQ // C) * b + chunk - NCH * b - 1, 0), 0
    return index


def _in_proj(xflat, head, norm_w, w4, conv4):
    tm = 384
    nck = tm // C
    W3 = 3 * HD

    def body(*refs):
        x_refs = refs[:nck]
        head_ref, nw_ref, w4_ref, cw_ref, h_ref, p_ref, ut_ref, cv_ref, w_ref, prev = refs[nck:]
        i = pl.program_id(0)

        @pl.when(i == 0)
        def _():
            prev[...] = jnp.zeros_like(prev)
            w_ref[PC - DK:PC, :] = jnp.zeros((DK, D), bf16)
            for q in range(4):
                w_ref[SHARD_COLS * q:SHARD_COLS * (q + 1), :] = w4_ref[q]

        blocks = []
        for k in range(nck):
            chunk = nck * i + k
            blocks.append(jnp.where(chunk - NCH * (chunk // NCH) == 0, head_ref[...], x_refs[k][...]))
        hval = jnp.concatenate(blocks, axis=0)
        h_ref[...] = hval
        u = rmsnorm(hval, nw_ref[...])
        ut_ref[...] = u.T.astype(bf16)
        p = bdot_nt(u, w_ref[...])
        p_ref[...] = p
        x = p[:, COL_QKV:COL_QKV + W3]
        xx = jnp.concatenate([prev[...], x], axis=0)
        y = cw_ref[3] * x
        for s in (1, 2, 3):
            y = y + cw_ref[3 - s] * pltpu.roll(xx, s, 0)[8:]
        row = i * tm + lax.broadcasted_iota(jnp.int32, (tm, 1), 0)
        tok = jnp.where(row >= TP, row - TP, row)
        cv_ref[...] = jnp.where(tok >= 8, y, 0.0)
        prev[...] = x[tm - 8:tm]

    return pl.pallas_call(
        body, name="in_proj", grid=(N // tm,),
        in_specs=[pl.BlockSpec((C, D), _chunk_index(nck, k)) for k in range(nck)]
        + [pl.BlockSpec((C, D), lambda i: (0, 0)), pl.BlockSpec((1, D), lambda i: (0, 0)),
           pl.BlockSpec((4, SHARD_COLS, D), lambda i: (0, 0, 0), pipeline_mode=pl.Buffered(1)),
           pl.BlockSpec((4, 1, W3), lambda i: (0, 0, 0))],
        out_specs=[pl.BlockSpec((tm, D), lambda i: (i, 0)), pl.BlockSpec((tm, PC), lambda i: (i, 0)),
                   pl.BlockSpec((D, tm), lambda i: (0, i)), pl.BlockSpec((tm, W3), lambda i: (i, 0)),
                   pl.BlockSpec((PC, D), lambda i: (0, 0), pipeline_mode=pl.Buffered(1))],
        out_shape=[jax.ShapeDtypeStruct((N, D), f32), jax.ShapeDtypeStruct((N, PC), f32),
                   jax.ShapeDtypeStruct((D, N), bf16), jax.ShapeDtypeStruct((N, W3), f32),
                   jax.ShapeDtypeStruct((PC, D), bf16)],
        scratch_shapes=[pltpu.VMEM((8, W3), f32)],
        compiler_params=_cp(("arbitrary",)),
    )(*[xflat] * nck, head, norm_w, w4, conv4)


NU = NB * H
_REV = lambda c: NCH - 1 - c
_FWD = lambda c: c


def _tok_spec(w, ix, col=0):
    return pl.BlockSpec((NB, C, w), lambda c: (0, ix(c), col))


def _state_spec(ix):
    return pl.BlockSpec((NB, 1, H, DK, DK), lambda c: (0, ix(c), 0, 0, 0))


def _row_spec(w):
    return pl.BlockSpec((1, w), lambda c: (0, 0))


def _rows(ref):
    return tuple(ref[b] for b in range(NB))


def _hg_extra_specs(ix):
    return [_row_spec(HD), _row_spec(HD)]


def _gd_extra_specs(ix):
    return [_tok_spec(DK, ix, COL_AB // DK), _row_spec(DK), _row_spec(DK)]


def _mix_fwd(proj3, cv, l0, l1, alog, dtb):
    def body(p_ref, c_ref, ab_ref, l0_ref, l1_ref, al_ref, db_ref, oh_ref, sh_ref, og_ref, sg_ref, t_ref, sth, stg):
        @pl.when(pl.program_id(0) == 0)
        def _():
            sth[...] = jnp.zeros_like(sth)
            stg[...] = jnp.zeros_like(stg)

        Sh = tuple(sth[n] for n in range(NU))
        Sg = tuple(stg[n] for n in range(NU))
        for n in range(NU):
            sh_ref[n // H, 0, n % H] = Sh[n]
            sg_ref[n // H, 0, n % H] = Sg[n]
        snh, oh, sng, og, tp = mix_chunk(Sh, _rows(p_ref), l0_ref[...], l1_ref[...],
                                         Sg, _rows(c_ref), _rows(ab_ref), al_ref[...], db_ref[...])
        for n in range(NU):
            sth[n] = snh[n]
            stg[n] = sng[n]
        for b in range(NB):
            oh_ref[b] = oh[b]
            og_ref[b] = og[b]
            t_ref[b] = tp[b]

    tok = jax.ShapeDtypeStruct((NB, TP, HD), f32)
    st = jax.ShapeDtypeStruct((NB, NCH, H, DK, DK), f32)
    return pl.pallas_call(
        body, name="mix_fwd", grid=(NCH,),
        in_specs=[_tok_spec(3 * HD, _FWD), _tok_spec(3 * HD, _FWD), _tok_spec(DK, _FWD, COL_AB // DK),
                  _row_spec(HD), _row_spec(HD), _row_spec(DK), _row_spec(DK)],
        out_specs=[_tok_spec(HD, _FWD), _state_spec(_FWD), _tok_spec(HD, _FWD), _state_spec(_FWD), _tok_spec(HD, _FWD)],
        out_shape=[tok, st, tok, st, tok],
        scratch_shapes=[pltpu.VMEM((NU, DK, DK), f32), pltpu.VMEM((NU, DK, DK), f32)],
        compiler_params=_cp(("arbitrary",)),
    )(proj3, cv, proj3, l0, l1, alog, dtb)


def _hg_bwd(proj3, l0, l1, s_saved, do):
    def body(p_ref, l0_ref, l1_ref, s_ref, do_ref, dp_ref, dl0_ref, dl1_ref, dst):
        @pl.when(pl.program_id(0) == 0)
        def _():
            dst[...] = jnp.zeros_like(dst)
            dl0_ref[...] = jnp.zeros_like(dl0_ref)
            dl1_ref[...] = jnp.zeros_like(dl1_ref)

        S = tuple(s_ref[n // H, 0, n % H] for n in range(NU))
        _, vjp = jax.vjp(hg_chunk, S, _rows(p_ref), l0_ref[...], l1_ref[...])
        dS, dp, dl0, dl1 = vjp((tuple(dst[n] for n in range(NU)), _rows(do_ref)))
        for n in range(NU):
            dst[n] = dS[n]
        for b in range(NB):
            dp_ref[b] = dp[b].astype(bf16)
        dl0_ref[...] += jnp.broadcast_to(dl0, (8, HD))
        dl1_ref[...] += jnp.broadcast_to(dl1, (8, HD))

    acc = pl.BlockSpec((8, HD), lambda c: (0, 0))
    return pl.pallas_call(
        body, name="hg_bwd", grid=(NCH,),
        in_specs=[_tok_spec(3 * HD, _REV)] + _hg_extra_specs(_REV) + [_state_spec(_REV), _tok_spec(HD, _REV)],
        out_specs=[_tok_spec(3 * HD, _REV), acc, acc],
        out_shape=[jax.ShapeDtypeStruct((NB, TP, 3 * HD), bf16), jax.ShapeDtypeStruct((8, HD), f32),
                   jax.ShapeDtypeStruct((8, HD), f32)],
        scratch_shapes=[pltpu.VMEM((NU, DK, DK), f32)],
        compiler_params=_cp(("arbitrary",)),
    )(proj3, l0, l1, s_saved, do)


def _gd_bwd(cv, proj3, conv4, alog, dtb, s_saved, t_saved, do):
    def body(c_ref, ab_ref, al_ref, db_ref, s_ref, t_ref, do_ref, x0_ref, x1_ref, x2_ref, w_ref,
             dx_ref, dw_ref, dab_ref, dal_ref, ddb_ref, dst, nxt):
        @pl.when(pl.program_id(0) == 0)
        def _():
            dst[...] = jnp.zeros_like(dst)
            dal_ref[...] = jnp.zeros_like(dal_ref)
            ddb_ref[...] = jnp.zeros_like(ddb_ref)
            dw_ref[...] = jnp.zeros_like(dw_ref)
            nxt[...] = jnp.zeros_like(nxt)

        S = tuple(s_ref[n // H, 0, n % H] for n in range(NU))
        t_rows = _rows(t_ref)
        fn = lambda *a: gd_chunk(*a, t_saved=t_rows)[0:2]
        _, vjp = jax.vjp(fn, S, _rows(c_ref), _rows(ab_ref), al_ref[...], db_ref[...])
        dS, dc, dab, dal, ddb = vjp((tuple(dst[n] for n in range(NU)), _rows(do_ref)))
        for n in range(NU):
            dst[n] = dS[n]
        for b in range(NB):
            dab_ref[b] = dab[b].astype(bf16)
        dal_ref[...] += jnp.broadcast_to(dal, (8, DK))
        ddb_ref[...] += jnp.broadcast_to(ddb, (8, DK))

        tok = _REV(pl.program_id(0)) * C + lax.broadcasted_iota(jnp.int32, (C, 1), 0)
        for j, x_ref in enumerate((x0_ref, x1_ref, x2_ref)):
            sl = slice(j * HD, (j + 1) * HD)
            dws = [None] * 4
            for b in range(NB):
                x = x_ref[b]
                g = jnp.where(tok >= 8, dc[b][:, sl], 0.0)
                gg = jnp.concatenate([g, nxt[b, :, sl]], axis=0)
                dx = w_ref[3, :, sl] * g
                parts = [jnp.sum(x * g, axis=0, keepdims=True)]
                for s in (1, 2, 3):
                    gs = pltpu.roll(gg, C + 8 - s, 0)[:C]
                    dx = dx + w_ref[3 - s, :, sl] * gs
                    parts.append(jnp.sum(x * gs, axis=0, keepdims=True))
                nxt[b, :, sl] = g[:8]
                dx_ref[b, :, sl] = dx.astype(bf16)
                dws = [p if a is None else a + p for a, p in zip(dws, parts)]
            for s in range(4):
                dw_ref[3 - s, :, sl] += jnp.broadcast_to(dws[s], (8, HD))

    acc = pl.BlockSpec((8, DK), lambda c: (0, 0))
    return pl.pallas_call(
        body, name="gd_bwd", grid=(NCH,),
        in_specs=[_tok_spec(3 * HD, _REV)] + _gd_extra_specs(_REV)
        + [_state_spec(_REV), _tok_spec(HD, _REV), _tok_spec(HD, _REV)]
        + [_tok_spec(HD, _REV, COL_QKV // HD + j) for j in range(3)]
        + [pl.BlockSpec((4, 1, 3 * HD), lambda c: (0, 0, 0))],
        out_specs=[_tok_spec(3 * HD, _REV), pl.BlockSpec((4, 8, 3 * HD), lambda c: (0, 0, 0)),
                   _tok_spec(DK, _REV), acc, acc],
        out_shape=[jax.ShapeDtypeStruct((NB, TP, 3 * HD), bf16), jax.ShapeDtypeStruct((4, 8, 3 * HD), f32),
                   jax.ShapeDtypeStruct((NB, TP, DK), bf16),
                   jax.ShapeDtypeStruct((8, DK), f32), jax.ShapeDtypeStruct((8, DK), f32)],
        scratch_shapes=[pltpu.VMEM((NU, DK, DK), f32), pltpu.VMEM((NB, 8, 3 * HD), f32)],
        compiler_params=_cp(("arbitrary",)),
    )(cv, proj3, alog, dtb, s_saved, t_saved, do, proj3, proj3, proj3, conv4)


def _out_loss(o_hg, o_gd, proj, hgw, gdw, wout, hflat, fw, target):
    tm = 384

    def body(ohg_ref, ogd_ref, zhg_ref, zgd_ref, hgw_ref, gdw_ref, wo_ref, h_ref, fw_ref, *refs):
        tg_refs = refs[:tm // C]
        (loss_ref, dohg_ref, dogd_ref, dzhg_ref, dzgd_ref, dh_ref, dwo_ref, dhgw_ref, dgdw_ref, dfw_ref) = refs[tm // C:]
        i = pl.program_id(0)

        @pl.when(i == 0)
        def _():
            for r in (loss_ref, dwo_ref, dhgw_ref, dgdw_ref, dfw_ref):
                r[...] = jnp.zeros_like(r)

        row = i * tm + lax.broadcasted_iota(jnp.int32, (tm, 1), 0)
        tok = jnp.where(row >= TP, row - TP, row)
        valid = (tok >= PAD + N_META).astype(f32)
        hval = h_ref[...]
        tgt = jnp.concatenate([r[...] for r in tg_refs], axis=0)

        mixers = ((ohg_ref, zhg_ref, hgw_ref[...]), (ogd_ref, zgd_ref, gdw_ref[...]))
        saved, ys = [], []
        for o_ref, z_ref, w in mixers:
            for hh in range(H):
                sl = slice(hh * DK, (hh + 1) * DK)
                o, z = o_ref[:, sl], z_ref[:, sl]
                r = lax.rsqrt(jnp.mean(o * o, axis=-1, keepdims=True) + EPS)
                n = o * r
                sg = sigmoid(z)
                ws = w * (z * sg)
                saved.append((r, n, sg, z, ws, w))
                ys.append(n * ws)
        y = jnp.concatenate(ys, axis=-1)
        h2 = hval + bdot(y, wo_ref[...])
        r2 = lax.rsqrt(jnp.mean(h2 * h2, axis=-1, keepdims=True) + EPS)
        n2 = h2 * r2
        fwv = fw_ref[...]
        err = (n2 * fwv - tgt) * valid
        loss = (0.5 / D) * jnp.sum(err * err)
        dyf = err * (1.0 / D)
        dn2 = dyf * fwv
        dout = r2 * (dn2 - n2 * jnp.mean(dn2 * n2, axis=-1, keepdims=True))
        dh_ref[...] = dout
        dy = bdot_nt(dout, wo_ref[...])
        dwo_ref[...] += bdot_tn(y, dout)
        dws = []
        for mi, (do_ref, dz_ref) in enumerate(((dohg_ref, dzhg_ref), (dogd_ref, dzgd_ref))):
            dw = jnp.zeros((1, DK), f32)
            for hh in range(H):
                sl = slice(hh * DK, (hh + 1) * DK)
                r, n, sg, z, ws, w = saved[mi * H + hh]
                dyh = dy[:, mi * HD + hh * DK:mi * HD + (hh + 1) * DK]
                t = dyh * n
                dw = dw + jnp.sum(t * (z * sg), axis=0, keepdims=True)
                dz_ref[:, sl] = (t * w * (sg * (1.0 + z * (1.0 - sg)))).astype(bf16)
                dn = dyh * ws
                do_ref[:, sl] = r * (dn - n * jnp.mean(dn * n, axis=-1, keepdims=True))
            dws.append(dw)
        loss_ref[...] += jnp.broadcast_to(loss, (8, DK))
        dhgw_ref[...] += jnp.broadcast_to(dws[0], (8, DK))
        dgdw_ref[...] += jnp.broadcast_to(dws[1], (8, DK))
        dfw_ref[...] += jnp.broadcast_to(jnp.sum(dyf * n2, axis=0, keepdims=True), (8, D))

    row = lambda w: pl.BlockSpec((tm, w), lambda i: (i, 0))
    whole = lambda r, w: pl.BlockSpec((r, w), lambda i: (0, 0))
    col = lambda c0: pl.BlockSpec((tm, HD), lambda i: (i, c0 // HD))

    tgt_spec = lambda k: pl.BlockSpec((C, D), _chunk_index(tm // C, k))
    return pl.pallas_call(
        body, name="out_loss", grid=(N // tm,),
        in_specs=[row(HD), row(HD), col(COL_ZHG), col(COL_ZGD),
                  whole(1, DK), whole(1, DK), whole(D, D), row(D), whole(1, D)] + [tgt_spec(k) for k in range(tm // C)],
        out_specs=[whole(8, DK), row(HD), row(HD), row(HD), row(HD), row(D), whole(D, D),
                   whole(8, DK), whole(8, DK), whole(8, D)],
        out_shape=[jax.ShapeDtypeStruct((8, DK), f32)] + [jax.ShapeDtypeStruct((N, HD), f32)] * 2
        + [jax.ShapeDtypeStruct((N, HD), bf16)] * 2
        + [jax.ShapeDtypeStruct((N, D), f32), jax.ShapeDtypeStruct((D, D), f32),
           jax.ShapeDtypeStruct((8, DK), f32), jax.ShapeDtypeStruct((8, DK), f32), jax.ShapeDtypeStruct((8, D), f32)],
        compiler_params=_cp(("arbitrary",)),
    )(o_hg, o_gd, proj, proj, hgw, gdw, wout, hflat, fw, *[target] * (tm // C))


def _in_bwd(pieces, wbig, hflat, norm_w, dh_res, pbs):
    tm = 384
    nsteps = N // tm
    np_ = len(pieces)
    na = len(pbs)
    offs = [c0 for _, c0 in pieces]
    widths = [d.shape[1] for d, _ in pieces]

    def body(*refs):
        d_refs = refs[:np_]
        w_ref, h_ref, nw_ref, dhr_ref = refs[np_:np_ + 4]
        srcs = refs[np_ + 4:np_ + 4 + na]
        dh_ref, dnw_ref = refs[np_ + 4 + na:np_ + 6 + na]
        dsts = refs[np_ + 6 + na:np_ + 6 + 2 * na]
        sems = refs[np_ + 6 + 2 * na:]
        i = pl.program_id(0)

        def copies():
            if not na:
                return []
            x, y, c, chips = _place()
            return [pltpu.make_async_remote_copy(
                src_ref=srcs[a].at[2 * px + py], dst_ref=dsts[a].at[j], send_sem=sems[0].at[na * j + a],
                recv_sem=sems[1].at[na * j + a], device_id=(px, py, c), device_id_type=MESH)
                for j, (px, py) in enumerate(chips) for a in range(na)]

        @pl.when(i == 0)
        def _():
            dnw_ref[...] = jnp.zeros_like(dnw_ref)
            for cp in copies():
                cp.start()

        du = jnp.zeros((tm, D), f32)
        for d_ref, off, wd in zip(d_refs, offs, widths):
            du = du + bdot(d_ref[...], w_ref[off:off + wd, :])
        _, vjp = jax.vjp(rmsnorm, h_ref[...], nw_ref[...])
        dh, dnw = vjp(du)
        dh_ref[...] = dh + dhr_ref[...]
        dnw_ref[...] += jnp.broadcast_to(dnw, (8, D))

        @pl.when(i == nsteps - 1)
        def _():
            for cp in copies():
                cp.wait()

    row = lambda w: pl.BlockSpec((tm, w), lambda i: (i, 0))
    return pl.pallas_call(
        body, name="in_bwd", grid=(nsteps,),
        in_specs=[row(w) for w in widths]
        + [pl.BlockSpec((PC, D), lambda i: (0, 0)), row(D), pl.BlockSpec((1, D), lambda i: (0, 0)), row(D)] + [ANY] * na,
        out_specs=[row(D), pl.BlockSpec((8, D), lambda i: (0, 0))] + [ANY] * na,
        out_shape=[jax.ShapeDtypeStruct((N, D), f32), jax.ShapeDtypeStruct((8, D), f32)]
        + [jax.ShapeDtypeStruct((3,) + p.shape[1:], p.dtype) for p in pbs],
        scratch_shapes=[pltpu.SemaphoreType.DMA((3 * na,)), pltpu.SemaphoreType.DMA((3 * na,))] if na else [],
        compiler_params=_cp(("arbitrary",)),
    )(*[d for d, _ in pieces], wbig, hflat, norm_w, dh_res, *pbs)


def _w_grad(ut, pieces):
    tk = N // 3
    offs = [c0 for _, c0 in pieces]
    widths = [d.shape[1] for d, _ in pieces]

    def body(u_ref, *refs):
        d_refs, o_ref = refs[:-1], refs[-1]

        @pl.when(pl.program_id(0) == 0)
        def _():
            o_ref[...] = jnp.zeros_like(o_ref)

        u = u_ref[...]
        for d_ref, off, wd in zip(d_refs, offs, widths):
            o_ref[:, off:off + wd] += jnp.dot(u, d_ref[...], preferred_element_type=f32)

    return pl.pallas_call(
        body, name="w_grad", grid=(N // tk,),
        in_specs=[pl.BlockSpec((D, tk), lambda k: (0, k))] + [pl.BlockSpec((tk, w), lambda k: (k, 0)) for w in widths],
        out_specs=pl.BlockSpec((D, PC), lambda k: (0, 0), pipeline_mode=pl.Buffered(1)),
        out_shape=jax.ShapeDtypeStruct((D, PC), f32),
        compiler_params=_cp(("arbitrary",)),
    )(ut, *[d for d, _ in pieces])


def _adam_math(g, w, m, v):
    m2 = ADAM_B1 * m + (1.0 - ADAM_B1) * g
    v2 = ADAM_B2 * v + (1.0 - ADAM_B2) * (g * g)
    m_hat = m2 / (1.0 - ADAM_B1 ** ADAM_STEP)
    v_hat = v2 / (1.0 - ADAM_B2 ** ADAM_STEP)
    delta = -ADAM_LR * (m_hat / (jnp.sqrt(v_hat) + ADAM_EPS) + ADAM_WD * w)
    return delta, m2, v2


def _adamw(gs, w, m, v, name):
    R, Cc = w.shape
    tr = 256 if R % 256 == 0 else R
    ng = len(gs)

    def body(*refs):
        g = refs[0][...]
        for r in refs[1:ng]:
            g = g + r[...]
        w_ref, m_ref, v_ref, g_ref, d_ref, m2_ref, v2_ref = refs[ng:]
        delta, m2, v2 = _adam_math(g, w_ref[...], m_ref[...], v_ref[...])
        g_ref[...] = g
        d_ref[...] = delta
        m2_ref[...] = m2
        v2_ref[...] = v2

    spec = pl.BlockSpec((tr, Cc), lambda i: (i, 0))
    return pl.pallas_call(
        body, name=name, grid=(R // tr,),
        in_specs=[spec] * (ng + 3), out_specs=[spec] * 4,
        out_shape=[jax.ShapeDtypeStruct((R, Cc), f32)] * 4,
        compiler_params=_cp(("arbitrary",)),
    )(*gs, w, m, v)


def _adamw_rows(g, w, m, v, name):
    R, _, Cc = w.shape
    tr = R // 9

    def body(g_ref, w_ref, m_ref, v_ref, go_ref, d_ref, m2_ref, v2_ref):
        g = g_ref[...]
        delta, m2, v2 = _adam_math(g, w_ref[...], m_ref[...], v_ref[...])
        go_ref[...] = g
        d_ref[...] = delta
        m2_ref[...] = m2
        v2_ref[...] = v2

    spec = pl.BlockSpec((tr, 1, Cc), lambda i: (i, 0, 0))
    return pl.pallas_call(
        body, name=name, grid=(R // tr,),
        in_specs=[spec] * 4, out_specs=[spec] * 4,
        out_shape=[jax.ShapeDtypeStruct((R, 1, Cc), f32)] * 4,
        compiler_params=_cp(("arbitrary",)),
    )(g, w, m, v)


def _place():
    x, y, c = lax.axis_index("x"), lax.axis_index("y"), lax.axis_index("c")
    return x, y, c, [(1 - x, y), (x, 1 - y), (1 - x, 1 - y)]


def _gather_weights(cm, halved, whole):
    R, _, Cc = cm.shape
    hw = Cc // 2
    shards = [jax.ShapeDtypeStruct((R, Cc), bf16)] + list(halved) + list(whole)
    nh = 1 + len(halved)
    na = len(shards)

    def body(*refs):
        srcs, dsts = refs[:na], refs[na:2 * na]
        send_sems, recv_sems, loc_sems = refs[2 * na:2 * na + 3]
        stage = refs[2 * na + 3:3 * na + 3]
        raw = refs[3 * na + 3]
        x, y, c, chips = _place()
        me = 2 * x + y
        loads = [pltpu.make_async_copy(srcs[0], raw, loc_sems.at[0])]
        loads += [pltpu.make_async_copy(srcs[i], stage[i], loc_sems.at[i]) for i in range(1, na)]
        locs = [pltpu.make_async_copy(v, d.at[me], loc_sems.at[i]) for i, (v, d) in enumerate(zip(stage, dsts))]
        for cp in loads:
            cp.start()

        def half_of(ref, i, half):
            return ref.at[:, pl.ds(pl.multiple_of(half * hw, hw), hw)] if i == 0 else ref.at[half]

        def ici(j, i, slot):
            px, py = chips[j]
            src = half_of(stage[0] if i == 0 else srcs[i], i, c) if i < nh else srcs[i]
            dst = half_of(dsts[i].at[slot], i, c) if i < nh else dsts[i].at[slot]
            return pltpu.make_async_remote_copy(
                src_ref=src, dst_ref=dst, send_sem=send_sems.at[na * j + i], recv_sem=recv_sems.at[na * j + i],
                device_id=(px, py, c), device_id_type=MESH)

        def d2d(j, i, half):
            px, py = chips[j]
            blk = half_of(dsts[i].at[2 * px + py], i, half)
            return pltpu.make_async_remote_copy(
                src_ref=blk, dst_ref=blk, send_sem=send_sems.at[3 * na + nh * j + i],
                recv_sem=recv_sems.at[3 * na + nh * j + i], device_id=(x, y, 1 - c), device_id_type=MESH)

        sends = [ici(j, i, me) for j in range(3) for i in range(1, na)]
        for cp in sends:
            cp.start()
        loads[0].wait()
        stage[0][...] = raw[:, 0, :].astype(bf16)
        first = [ici(j, 0, me) for j in range(3)]
        for cp in first:
            cp.start()
        sends += first
        locs[0].start()
        for ld, st in zip(loads[1:], locs[1:]):
            ld.wait()
            st.start()
        for j, (px, py) in enumerate(chips):
            for i in range(na):
                ici(j, i, 2 * px + py).wait_recv()
                if i < nh:
                    fwd = d2d(j, i, c)
                    fwd.start()
                    sends.append(fwd)
        for j in range(3):
            for i in range(nh):
                d2d(j, i, 1 - c).wait_recv()
        for cp in sends:
            cp.wait_send()
        for cp in locs:
            cp.wait()

    nsem = 3 * na + 3 * nh
    return pl.pallas_call(
        body, name="gather_weights",
        in_specs=[ANY] * na, out_specs=[ANY] * na,
        out_shape=[jax.ShapeDtypeStruct((4,) + s.shape, s.dtype) for s in shards],
        scratch_shapes=[pltpu.SemaphoreType.DMA((nsem,)), pltpu.SemaphoreType.DMA((nsem,)),
                        pltpu.SemaphoreType.DMA((na,))] + [pltpu.VMEM(s.shape, s.dtype) for s in shards]
        + [pltpu.VMEM(cm.shape, cm.dtype)],
        compiler_params=pltpu.CompilerParams(has_side_effects=True, vmem_limit_bytes=VMEM_LIMIT),
    )(cm, *halved, *whole)


def _swap_halves(gs):
    na = len(gs)
    jobs = [(i, q) for i in range(na) for q in range(gs[i].shape[0])]

    def body(*refs):
        srcs, dsts = refs[:na], refs[na:2 * na]
        send_sems, recv_sems = refs[2 * na:]
        x, y, c, _ = _place()
        cps = [pltpu.make_async_remote_copy(
            src_ref=srcs[i].at[q, 1 - c], dst_ref=dsts[i].at[q], send_sem=send_sems.at[k],
            recv_sem=recv_sems.at[k], device_id=(x, y, 1 - c), device_id_type=MESH)
            for k, (i, q) in enumerate(jobs)]
        for cp in cps:
            cp.start()
        for cp in cps:
            cp.wait()

    return pl.pallas_call(
        body, name="swap_halves",
        in_specs=[ANY] * na, out_specs=[ANY] * na,
        out_shape=[jax.ShapeDtypeStruct(g.shape[0:1] + g.shape[2:], g.dtype) for g in gs],
        scratch_shapes=[pltpu.SemaphoreType.DMA((len(jobs),)), pltpu.SemaphoreType.DMA((len(jobs),))],
        compiler_params=pltpu.CompilerParams(has_side_effects=True),
    )(*gs)


def _add_split(cm_arr, g, s):
    _, _, R, Cg = g.shape
    tr = 128

    def body(sc_ref, g_ref, s_ref, b_ref, o_ref):
        p = g_ref[0, 0] + s_ref[0]
        own = None
        for q in range(4):
            blk = p[:, SHARD_COLS * q:SHARD_COLS * (q + 1)]
            b_ref[q] = blk.astype(bf16)
            mine = jnp.where(sc_ref[1] == q, blk, 0.0)
            own = mine if own is None else own + mine
        o_ref[...] = own

    return pl.pallas_call(
        body, name="add_w_in",
        grid_spec=pltpu.PrefetchScalarGridSpec(
            num_scalar_prefetch=1, grid=(R // tr,),
            in_specs=[pl.BlockSpec((1, 1, tr, Cg), lambda i, sc: (0, sc[0], i, 0)),
                      pl.BlockSpec((1, tr, Cg), lambda i, sc: (0, i, 0))],
            out_specs=[pl.BlockSpec((4, tr, SHARD_COLS), lambda i, sc: (0, i, 0)),
                       pl.BlockSpec((tr, SHARD_COLS), lambda i, sc: (i, 0))]),
        out_shape=[jax.ShapeDtypeStruct((4, R, SHARD_COLS), bf16), jax.ShapeDtypeStruct((R, SHARD_COLS), f32)],
        compiler_params=_cp(("arbitrary",)),
    )(cm_arr, g, s)


def _add_halves(c_arr, g, s, name):
    Q, _, R, Cc = g.shape
    tr = min(R, 128)

    def body(c_ref, g_ref, s_ref, b_ref, f_ref):
        p = g_ref[0, 0] + s_ref[0]
        f_ref[0] = p
        b_ref[0] = p.astype(bf16)

    blk = pl.BlockSpec((1, tr, Cc), lambda q, i, cr: (q, i, 0))
    return pl.pallas_call(
        body, name=name,
        grid_spec=pltpu.PrefetchScalarGridSpec(
            num_scalar_prefetch=1, grid=(Q, R // tr),
            in_specs=[pl.BlockSpec((1, 1, tr, Cc), lambda q, i, cr: (q, cr[0], i, 0)), blk], out_specs=[blk, blk]),
        out_shape=[jax.ShapeDtypeStruct((Q, R, Cc), bf16), jax.ShapeDtypeStruct((Q, R, Cc), f32)],
        compiler_params=_cp(("arbitrary", "arbitrary")),
    )(c_arr, g, s)


_FLIPS = [(fx, fy, fc) for fx in (0, 1) for fy in (0, 1) for fc in (0, 1)][1:]


def _sum_blocks(own, r, name, transposed=False):
    R, Cc = own.shape
    tr = min(R, 256)

    def body(own_ref, r_ref, o_ref):
        acc = own_ref[...]
        for j in range(3):
            acc = acc + r_ref[j].astype(f32)
        o_ref[...] = acc.T if transposed else acc

    return pl.pallas_call(
        body, name=name, grid=(R // tr,),
        in_specs=[pl.BlockSpec((tr, Cc), lambda i: (i, 0)), pl.BlockSpec((3, tr, Cc), lambda i: (0, i, 0))],
        out_specs=pl.BlockSpec((Cc, tr), lambda i: (0, i)) if transposed else pl.BlockSpec((tr, Cc), lambda i: (i, 0)),
        out_shape=jax.ShapeDtypeStruct((Cc, R) if transposed else (R, Cc), f32),
        compiler_params=_cp(("arbitrary",)),
    )(own, r)


def _sum_packs(me8_arr, pack, rp):
    R = pack.shape[0]

    def body(me_ref, pk_ref, rp_ref, o_ref):
        me8 = me_ref[0]
        acc = None
        for d in range(8):
            rel = d ^ me8
            term = jnp.where(rel == 0, pk_ref[...], rp_ref[jnp.maximum(rel - 1, 0)])
            acc = term if acc is None else acc + term
        o_ref[...] = acc

    return pl.pallas_call(
        body, name="sum_packs",
        grid_spec=pltpu.PrefetchScalarGridSpec(
            num_scalar_prefetch=1, grid=(1,),
            in_specs=[pl.BlockSpec((R, 128), lambda i, mr: (0, 0)), pl.BlockSpec((7, R, 128), lambda i, mr: (0, 0, 0))],
            out_specs=pl.BlockSpec((R, 128), lambda i, mr: (0, 0))),
        out_shape=jax.ShapeDtypeStruct((R, 128), f32),
        compiler_params=_cp(("arbitrary",)),
    )(me8_arr, pack, rp)


def _swap_finished(fs, pack):
    na = len(fs)
    R = pack.shape[0]

    def body(*refs):
        srcs, pk = refs[:na], refs[na]
        dsts, rp = refs[na + 1:2 * na + 1], refs[2 * na + 1]
        send_sems, recv_sems = refs[2 * na + 2:]
        x, y, c, _ = _place()
        cps = [pltpu.make_async_remote_copy(
            src_ref=srcs[i], dst_ref=dsts[i], send_sem=send_sems.at[i], recv_sem=recv_sems.at[i],
            device_id=(x, y, 1 - c), device_id_type=MESH) for i in range(na)]
        cps += [pltpu.make_async_remote_copy(
            src_ref=pk, dst_ref=rp.at[k], send_sem=send_sems.at[na + k], recv_sem=recv_sems.at[na + k],
            device_id=(x ^ fx, y ^ fy, c ^ fc), device_id_type=MESH) for k, (fx, fy, fc) in enumerate(_FLIPS)]
        for cp in cps:
            cp.start()
        for cp in cps:
            cp.wait()

    return pl.pallas_call(
        body, name="swap_finished",
        in_specs=[ANY] * (na + 1), out_specs=[ANY] * (na + 1),
        out_shape=[jax.ShapeDtypeStruct(f.shape, f.dtype) for f in fs] + [jax.ShapeDtypeStruct((7, R, 128), f32)],
        scratch_shapes=[pltpu.SemaphoreType.DMA((na + 7,)), pltpu.SemaphoreType.DMA((na + 7,))],
        compiler_params=pltpu.CompilerParams(has_side_effects=True),
    )(*fs, pack)


def _adamw_halves(c_arr, mine, peer, w, m, v, name):
    _, R, Cc = w.shape
    tr = min(R, 256)

    def body(c_ref, mine_ref, peer_ref, w_ref, m_ref, v_ref, g_ref, d_ref, m2_ref, v2_ref):
        g = jnp.where(pl.program_id(0) == c_ref[0], mine_ref[...], peer_ref[...])
        delta, m2, v2 = _adam_math(g, w_ref[0], m_ref[0], v_ref[0])
        g_ref[0] = g
        d_ref[0] = delta
        m2_ref[0] = m2
        v2_ref[0] = v2

    half = pl.BlockSpec((tr, Cc), lambda hh, i, cr: (i, 0))
    full = pl.BlockSpec((1, tr, Cc), lambda hh, i, cr: (hh, i, 0))
    return pl.pallas_call(
        body, name=name,
        grid_spec=pltpu.PrefetchScalarGridSpec(
            num_scalar_prefetch=1, grid=(2, R // tr), in_specs=[half, half, full, full, full], out_specs=[full] * 4),
        out_shape=[jax.ShapeDtypeStruct((2, R, Cc), f32)] * 4,
        compiler_params=_cp(("arbitrary", "arbitrary")),
    )(c_arr, mine, peer, w, m, v)


def _rows8(a):
    flat = a.reshape(-1)
    n = flat.shape[0]
    rows = -(-n // 1024) * 8
    return jnp.pad(flat, (0, rows * 128 - n)).reshape(rows, 128)


def kernel(x, meta_tokens, norm_w, w_in, conv_w, hg_lb_logits, hg_norm_w, gdn_A_log, gdn_dt_bias, gdn_norm_w, w_out, final_norm_w, loss_target, m_meta_tokens, m_norm_w, m_w_in, m_conv_w, m_hg_lb_logits, m_hg_norm_w, m_gdn_A_log, m_gdn_dt_bias, m_gdn_norm_w, m_w_out, m_final_norm_w, v_meta_tokens, v_norm_w, v_w_in, v_conv_w, v_hg_lb_logits, v_hg_norm_w, v_gdn_A_log, v_gdn_dt_bias, v_gdn_norm_w, v_w_out, v_final_norm_w):
    me = 2 * lax.axis_index("x") + lax.axis_index("y")

    g_win, g_wout, g_conv, g_meta = _gather_weights(
        jnp.transpose(w_in, (2, 0, 1)), [w_out[0].astype(bf16).reshape(2, D // 8, D)], [conv_w[0], meta_tokens])
    wout_full = g_wout.reshape(D, D)
    conv4 = jnp.transpose(g_conv, (1, 0, 2)).reshape(4, 1, 3 * HD)
    meta_full = jnp.transpose(g_meta, (1, 0, 2)).reshape(N_META, D)

    c_arr = lax.axis_index("c").reshape(1).astype(jnp.int32)

    def chip_partials(gw, g_wout_part):
        g_in2 = gw.reshape(1, 2, D // 2, PC)
        g_out4 = g_wout_part.reshape(4, 2, D // 8, D)
        s_in, s_out = _swap_halves([g_in2, g_out4])
        pb_blocks, own_in = _add_split(jnp.concatenate([c_arr, me.reshape(1).astype(jnp.int32)]), g_in2, s_in)
        pb_out, pf_out = _add_halves(c_arr, g_out4, s_out, "add_w_out")
        own_out = lax.dynamic_index_in_dim(pf_out, me, axis=0, keepdims=False)
        return [pb_blocks, pb_out], [own_in, own_out]

    (loss8, grad_x, d_meta, d_nw, d_conv, d_lb, d_hgw, d_alog, d_dtb, d_gdw, d_fw, pfs, rs) = _local_step(
        x, loss_target, g_win, wout_full, conv4, meta_full, norm_w, hg_lb_logits, hg_norm_w, gdn_A_log, gdn_dt_bias,
        gdn_norm_w, final_norm_w, chip_partials)

    pack = jnp.concatenate([
        loss8, d_nw[0].reshape(8, 128), d_lb.reshape(8, 128), d_hgw, _rows8(d_alog[0, :H]), _rows8(d_dtb[0, :H]),
        d_gdw, d_fw[0].reshape(8, 128), d_meta.reshape(128, 128), d_conv.reshape(48, 128)], axis=0)
    return _reduce_and_update(
        me, c_arr, grad_x, pfs, rs, pack, meta_tokens, norm_w, w_in, conv_w, hg_lb_logits, hg_norm_w, gdn_A_log,
        gdn_dt_bias, gdn_norm_w, w_out, final_norm_w, m_meta_tokens, m_norm_w, m_w_in, m_conv_w, m_hg_lb_logits,
        m_hg_norm_w, m_gdn_A_log, m_gdn_dt_bias, m_gdn_norm_w, m_w_out, m_final_norm_w, v_meta_tokens, v_norm_w, v_w_in,
        v_conv_w, v_hg_lb_logits, v_hg_norm_w, v_gdn_A_log, v_gdn_dt_bias, v_gdn_norm_w, v_w_out, v_final_norm_w)


def _local_step(x, loss_target, w4, wout_full, conv4, meta_full, norm_w, hg_lb_logits, hg_norm_w, gdn_A_log, gdn_dt_bias,
                gdn_norm_w, final_norm_w, chip_partials):
    head = jnp.concatenate([jnp.zeros((PAD, D), f32), meta_full], axis=0)
    target = loss_target.reshape(NB * SEQ, D)
    l0, l1 = hg_lb_logits[0:1], hg_lb_logits[1:2]
    alog = jnp.pad(gdn_A_log, ((0, 0), (0, DK - H)))
    dtb = jnp.pad(gdn_dt_bias, ((0, 0), (0, DK - H)))
    fw = final_norm_w.reshape(1, D)

    hflat, proj, ut, cv2, wbig = _in_proj(x.reshape(NB * SEQ, D), head, norm_w, w4, conv4)
    proj3 = proj.reshape(NB, TP, PC)
    cv = cv2.reshape(NB, TP, 3 * HD)
    o_hg, s_hg, o_gd, s_gd, t_gd = _mix_fwd(proj3, cv, l0, l1, alog, dtb)
    (loss8, d_ohg, d_ogd, d_zhg, d_zgd, dh_res, g_wout_part, d_hgw, d_gdw, d_fw) = _out_loss(
        o_hg.reshape(N, HD), o_gd.reshape(N, HD), proj, hg_norm_w, gdn_norm_w, wout_full, hflat, fw, target)
    d_hg, d_l0, d_l1 = _hg_bwd(proj3, l0, l1, s_hg, d_ohg.reshape(NB, TP, HD))
    d_qkv, d_conv4, d_ab, d_alog, d_dtb = _gd_bwd(cv, proj3, conv4, alog, dtb, s_gd, t_gd,
                                                  d_ogd.reshape(NB, TP, HD))
    d_hg2, d_qkv2, d_ab2 = d_hg.reshape(N, 3 * HD), d_qkv.reshape(N, 3 * HD), d_ab.reshape(N, DK)
    pieces = [(d_hg2, COL_HG), (d_zhg, COL_ZHG), (d_qkv2, COL_QKV), (d_zgd, COL_ZGD), (d_ab2, COL_AB)]
    gw = _w_grad(ut, pieces)
    pbs, pfs = chip_partials(gw, g_wout_part) if chip_partials else ([], [gw, g_wout_part])
    dh, d_nw, *rs = _in_bwd(pieces, wbig, hflat, norm_w, dh_res, pbs)

    dh3 = dh.reshape(NB, TP, D)
    grad_x = dh3[:, PAD + N_META:, :]
    d_meta = jnp.sum(dh3[:, PAD:PAD + N_META, :], axis=0)
    d_conv = d_conv4[:, 0, :]
    d_lb = jnp.concatenate([d_l0[0:1], d_l1[0:1]], axis=0)
    return loss8, grad_x, d_meta, d_nw, d_conv, d_lb, d_hgw, d_alog, d_dtb, d_gdw, d_fw, pfs, rs


def _reduce_and_update(me, c_arr, grad_x, pfs, rs, pack, meta_tokens, norm_w, w_in, conv_w, hg_lb_logits, hg_norm_w,
                       gdn_A_log, gdn_dt_bias, gdn_norm_w, w_out, final_norm_w, m_meta_tokens, m_norm_w, m_w_in, m_conv_w,
                       m_hg_lb_logits, m_hg_norm_w, m_gdn_A_log, m_gdn_dt_bias, m_gdn_norm_w, m_w_out, m_final_norm_w,
                       v_meta_tokens, v_norm_w, v_w_in, v_conv_w, v_hg_lb_logits, v_hg_norm_w, v_gdn_A_log, v_gdn_dt_bias,
                       v_gdn_norm_w, v_w_out, v_final_norm_w):
    (own_in, own_out), (r_in, r_out) = pfs, rs
    f_in = _sum_blocks(own_in, r_in, "sum_w_in", transposed=True)
    f_out = _sum_blocks(own_out, r_out, "sum_w_out")
    o_in, o_out, r_pack = _swap_finished([f_in, f_out], pack)
    me8_arr = (2 * me + lax.axis_index("c")).reshape(1).astype(jnp.int32)
    small = _sum_packs(me8_arr, pack, r_pack)

    half_out = lambda a: a[0].reshape(2, D // 8, D)
    is0 = lax.axis_index("c") == 0
    g_in = jnp.concatenate([jnp.where(is0, f_in, o_in), jnp.where(is0, o_in, f_in)], axis=1)
    to_cm = lambda a: jnp.transpose(a, (2, 0, 1))
    gi, di, mi, vi = [jnp.transpose(a, (1, 2, 0))[0] for a in _adamw_rows(
        g_in.reshape(SHARD_COLS, 1, D), to_cm(w_in), to_cm(m_w_in), to_cm(v_w_in), "adamw_w_in")]
    go, do_, mo, vo = [a.reshape(D // 4, D) for a in _adamw_halves(
        c_arr, f_out, o_out, half_out(w_out), half_out(m_w_out), half_out(v_w_out), "adamw_w_out")]

    g_meta_full = small[64:192].reshape(N_META, D)
    g_meta_loc = lax.dynamic_slice(g_meta_full, (0, me * 256), (N_META, 256))
    gm, dm, mm_, vm = _adamw([g_meta_loc], meta_tokens, m_meta_tokens, v_meta_tokens, "adamw_meta")
    g_conv_full = small[192:240].reshape(4, 1536)
    g_conv_loc = lax.dynamic_slice(g_conv_full, (0, me * 384), (4, 384))
    gc, dc, mc, vc = _adamw([g_conv_loc], conv_w[0], m_conv_w[0], v_conv_w[0], "adamw_conv")

    reps = [(norm_w, m_norm_w, v_norm_w), (hg_lb_logits, m_hg_lb_logits, v_hg_lb_logits),
            (hg_norm_w, m_hg_norm_w, v_hg_norm_w), (gdn_A_log, m_gdn_A_log, v_gdn_A_log),
            (gdn_dt_bias, m_gdn_dt_bias, v_gdn_dt_bias), (gdn_norm_w, m_gdn_norm_w, v_gdn_norm_w),
            (final_norm_w, m_final_norm_w, v_final_norm_w)]
    wp = jnp.concatenate([_rows8(t[0]) for t in reps], axis=0)
    mp = jnp.concatenate([_rows8(t[1]) for t in reps], axis=0)
    vp = jnp.concatenate([_rows8(t[2]) for t in reps], axis=0)
    gr, dr, mr, vr = _adamw([small[8:64]], wp, mp, vp, "adamw_small")

    def unpack(p):
        outs = []
        for i, t in enumerate(reps):
            n = t[0].size
            outs.append(p[8 * i:8 * i + 8].reshape(-1)[:n].reshape(t[0].shape))
        return outs

    def leaves(meta_v, conv_v, in_v, out_v, rep_p):
        nw, lb, hgw, al, db, gdw, fwv = unpack(rep_p)
        return [meta_v, nw, in_v[None], conv_v[None], lb, hgw, al, db, gdw, out_v[None], fwv]

    loss = small[0, 0]
    return (loss, grad_x, *leaves(gm, gc, gi, go, gr), *leaves(dm, dc, di, do_, dr),
            *leaves(mm_, mc, mi, mo, mr), *leaves(vm, vc, vi, vo, vr))
```

```python
import functools

import jax
import jax.numpy as jnp
from jax import lax
from jax.experimental import pallas as pl
from jax.experimental.pallas import tpu as pltpu

f32 = jnp.float32
bf16 = jnp.bfloat16
MESH = pl.DeviceIdType.MESH
ANY = pl.BlockSpec(memory_space=pl.ANY)

D = 1024
NB = 2
N_META = 16
SEQ = 2048
PAD = 48
TP = PAD + N_META + SEQ
C = 64
NCH = TP // C
N = NB * TP
H = 4
DK = 128
HD = H * DK
PC = 4224
IN_COLS = 4104
SHARD_COLS = IN_COLS // 4
COL_HG, COL_ZHG, COL_QKV, COL_ZGD, COL_AB = 0, 3 * HD, 4 * HD, 7 * HD, 8 * HD
EPS = 1e-6
ADAM_LR, ADAM_B1, ADAM_B2, ADAM_EPS, ADAM_WD, ADAM_STEP = 0.001, 0.9, 0.999, 1e-08, 0.01, 10
VMEM_LIMIT = 56 * 1024 * 1024

P_HG = dict(lvl=1, av=1, qs=1, su=1)
P_GD = dict(kk=1, inv=1, sol=1, ws=1, qk=1, o=1, su=1)


def _cp(sem=None, **kw):
    return pltpu.CompilerParams(dimension_semantics=sem, vmem_limit_bytes=VMEM_LIMIT, **kw)


_DIMS = {"nn": (((1,), (0,)), ((), ())), "nt": (((1,), (1,)), ((), ())), "tn": (((0,), (0,)), ((), ()))}


def _split(x):
    hi = x.astype(bf16)
    return hi, (x - hi.astype(f32)).astype(bf16)


def _dg(a, b, kind, passes):
    d = lambda x, y: lax.dot_general(x, y, _DIMS[kind], preferred_element_type=f32)
    if passes == 1:
        return d(a.astype(bf16), b.astype(bf16))
    ah, al = _split(a)
    bh, bl = _split(b)
    return d(ah, bh) + d(ah, bl) + d(al, bh)


@functools.partial(jax.custom_vjp, nondiff_argnums=(2, 3))
def mmx(a, b, kind, passes):
    return _dg(a, b, kind, passes)


def _mmx_fwd(a, b, kind, passes):
    return _dg(a, b, kind, passes), (a, b)


def _mmx_bwd(kind, passes, res, g):
    a, b = res
    if kind == "nn":
        return _dg(g, b, "nt", passes), _dg(a, g, "tn", passes)
    if kind == "nt":
        return _dg(g, b, "nn", passes), _dg(g, a, "tn", passes)
    return _dg(b, g, "nt", passes), _dg(a, g, "nn", passes)


mmx.defvjp(_mmx_fwd, _mmx_bwd)


def _mask_dg(mask, x):
    xh, xl = _split(x)
    return jnp.dot(jnp.concatenate([mask, mask], axis=1), jnp.concatenate([xh, xl], axis=0), preferred_element_type=f32)


@functools.partial(jax.custom_vjp, nondiff_argnums=(2,))
def mask_mm(mask, x, bwd_passes):
    return _mask_dg(mask, x)


def _mask_fwd(mask, x, bwd_passes):
    return _mask_dg(mask, x), mask


def _mask_bwd(bwd_passes, mask, g):
    d = lambda y: lax.dot_general(mask, y, _DIMS["tn"], preferred_element_type=f32)
    if bwd_passes == 1:
        return None, d(g.astype(bf16))
    gh, gl = _split(g)
    return None, d(gh) + d(gl)


mask_mm.defvjp(_mask_fwd, _mask_bwd)


def bdot(a, b):
    return jnp.dot(a.astype(bf16), b.astype(bf16), preferred_element_type=f32)


def bdot_nt(a, b):
    return lax.dot_general(a.astype(bf16), b.astype(bf16), _DIMS["nt"], preferred_element_type=f32)


def bdot_tn(a, b):
    return lax.dot_general(a.astype(bf16), b.astype(bf16), _DIMS["tn"], preferred_element_type=f32)


def _iota2(n, m):
    return lax.broadcasted_iota(jnp.int32, (n, m), 0), lax.broadcasted_iota(jnp.int32, (n, m), 1)


sigmoid = jax.nn.sigmoid


def silu(x):
    return x * sigmoid(x)


def softplus(x):
    return jnp.maximum(x, 0.0) + jnp.log(1.0 + jnp.exp(-jnp.abs(x)))


def rmsnorm(x, w):
    return x * lax.rsqrt(jnp.mean(x * x, axis=-1, keepdims=True) + EPS) * w


def hg_masks():
    t, r = _iota2(C, C)
    mats = [r <= t, r > t]
    lvl = []
    for l in range(1, 7):
        sz = 1 << l
        half = sz >> 1
        seg_t = t >> l
        upper_t = (t & (sz - 1)) >= half
        mid_t = seg_t * sz + half - 1
        mats.append((upper_t & (r > mid_t) & (r <= t)) | ((~upper_t) & (r > t) & (r <= mid_t)))
        lvl.append(((seg_t == (r >> l)) & upper_t & ((r & (sz - 1)) < half)).astype(f32))
    stk = jnp.concatenate([m.astype(bf16) for m in mats], axis=0)
    return stk, lvl, (t == r).astype(f32)


def _head(a, h):
    return a[:, h * DK:(h + 1) * DK]


def _run(*gens):
    results = [None] * len(gens)
    live = list(range(len(gens)))
    while live:
        for i in list(live):
            try:
                next(gens[i])
            except StopIteration as e:
                results[i] = e.value
                live.remove(i)
    return results


def hg_chunk(St, ps, l0, l1):
    return _run(hg_stages(St, ps, l0, l1))[0]


def gd_chunk(S, cs, abs_, alog, dtb, t_saved=None):
    return _run(gd_stages(S, cs, abs_, alog, dtb, t_saved))[0]


def mix_chunk(St, ps, l0, l1, S, cs, abs_, alog, dtb):
    (sn_h, o_h), (sn_g, o_g, t_pack) = _run(hg_stages(St, ps, l0, l1), gd_stages(S, cs, abs_, alog, dtb))
    return sn_h, o_h, sn_g, o_g, t_pack


def hg_stages(St, ps, l0, l1):
    m = jnp.maximum(l0, l1)
    e0 = jnp.exp(l0 - m)
    e1 = jnp.exp(l1 - m)
    lb = e0 / (e0 + e1)
    stk, lvl, eye = hg_masks()
    msk = [eye] + lvl
    trow = lax.broadcasted_iota(jnp.int32, (C, 1), 0)
    upper = [(trow & ((1 << l) - 1)) >= (1 << (l - 1)) for l in range(1, 7)]
    qs, ks, vs, qG, kR, eGl = [], [], [], [], [], []
    for p in ps:
        pq, pf, v = p[:, 0:HD], p[:, HD:2 * HD], p[:, 2 * HD:3 * HD]
        q = silu(pq)
        f = lb + (1.0 - lb) * sigmoid(pf)
        k = 1.0 - f
        logf = jnp.log(f)
        Dm = mask_mm(stk, logf, 1)
        z = [jnp.where(up, q, k) * jnp.exp(Dm[(2 + i) * C:(3 + i) * C]) for i, up in enumerate(upper)]
        qs.append([q] + z)
        ks.append([k] + z)
        vs.append(v)
        qG.append(q * jnp.exp(Dm[0:C]))
        kR.append(k * jnp.exp(Dm[C:2 * C]))
        eGl.append(jnp.exp(jnp.sum(logf, axis=0, keepdims=True)))
    yield
    units = [(b, h) for b in range(len(ps)) for h in range(H)]
    parts = []
    for i in range(7):
        parts.append([msk[i] * mmx(_head(qs[b][i], h), _head(ks[b][i], h), "nt", P_HG["lvl"]) for b, h in units])
        yield
    A = [functools.reduce(lambda x, y: x + y, [parts[i][n] for i in range(7)]) for n in range(len(units))]
    qS = [mmx(_head(qG[b], h), St[n], "nt", P_HG["qs"]) for n, (b, h) in enumerate(units)]
    Sn = [St[n] * _head(eGl[b], h) + mmx(_head(vs[b], h), _head(kR[b], h), "tn", P_HG["su"])
          for n, (b, h) in enumerate(units)]
    yield
    outs = [mmx(A[n], _head(vs[b], h), "nn", P_HG["av"]) + qS[n] for n, (b, h) in enumerate(units)]
    return tuple(Sn), tuple(jnp.concatenate(outs[b * H:(b + 1) * H], axis=1) for b in range(len(ps)))


@jax.custom_vjp
def use_inverse(A, T):
    return T


def _use_inverse_fwd(A, T):
    return T, T


def _use_inverse_bwd(T, g):
    return -_dg(T, _dg(g, T, "nt", P_GD["inv"]), "tn", P_GD["inv"]), jnp.zeros_like(T)


use_inverse.defvjp(_use_inverse_fwd, _use_inverse_bwd)


def gd_stages(S, cs, abs_, alog, dtb, t_saved=None):
    t, r = _iota2(C, C)
    tri = (r <= t).astype(bf16)
    ups = (r > t).astype(bf16)
    lane = lax.broadcasted_iota(jnp.int32, (1, DK), 1)
    subl = lax.broadcasted_iota(jnp.int32, (8, 1), 0)
    eye = (t == r).astype(f32)
    strict = (r < t).astype(f32)
    bd = ((t >> 4) == (r >> 4)).astype(f32)
    qa, ka, va, b4, gam4, grev4, gam4T, glast4 = [], [], [], [], [], [], [], []
    for c, ab in zip(cs, abs_):
        qa.append(silu(c[:, 0:HD]))
        ka.append(silu(c[:, HD:2 * HD]))
        va.append(silu(c[:, 2 * HD:3 * HD]))
        g4 = -jnp.exp(alog) * softplus(ab + dtb)
        b4.append(sigmoid(ab))
        gam4.append(mask_mm(tri, g4, 2))
        grev4.append(mask_mm(ups, g4, 2))
        gam4T.append(gam4[-1].T)
        glast4.append(jnp.sum(g4, axis=0, keepdims=True))
    yield
    units = [(b, h) for b in range(len(cs)) for h in range(H)]
    nu = range(len(units))
    inv = lambda a, b: [mmx(a[n], b[n], "nn", P_GD["inv"]) for n in nu]
    v = [_head(va[b], h) for b, h in units]
    q = [_head(qa[b], h) for b, h in units]
    k = [_head(ka[b], h) for b, h in units]
    q = [x * lax.rsqrt(jnp.sum(x * x, -1, keepdims=True) + EPS) * (DK ** -0.5) for x in q]
    k = [x * lax.rsqrt(jnp.sum(x * x, -1, keepdims=True) + EPS) for x in k]
    oh = [(lane == h).astype(f32) for h in range(H)]
    gam_c = [jnp.sum(gam4[b] * oh[h], -1, keepdims=True) for b, h in units]
    grev_c = [jnp.sum(grev4[b] * oh[h], -1, keepdims=True) for b, h in units]
    beta = [jnp.sum(b4[b] * (lane == h + H).astype(f32), -1, keepdims=True) for b, h in units]
    glast = [jnp.sum(glast4[b] * oh[h], -1, keepdims=True) for b, h in units]
    gam_r = [jnp.sum(gam4T[b][0:8, :] * (subl == h).astype(f32), axis=0, keepdims=True) for b, h in units]
    dec = [jnp.exp(jnp.where(r <= t, gam_c[n] - gam_r[n], -1e30)) for n in nu]
    egam = [jnp.exp(gam_c[n]) for n in nu]
    kk = [mmx(k[n], k[n], "nt", P_GD["kk"]) for n in nu]
    qk = [mmx(q[n], k[n], "nt", P_GD["qk"]) * dec[n] for n in nu]
    yield
    A = [beta[n] * kk[n] * dec[n] * strict for n in nu]
    Dg = [A[n] * bd for n in nu]
    L = [A[n] - Dg[n] for n in nu]
    if t_saved is None:
        ImD = [eye - Dg[n] for n in nu]
        D2 = inv(Dg, Dg)
        yield
        P1 = inv(ImD, [eye + x for x in D2])
        D4 = inv(D2, D2)
        yield
        P2 = inv(P1, [eye + x for x in D4])
        D8 = inv(D4, D4)
        yield
        M = inv(P2, [eye + x for x in D8])
        yield
        Nn = inv(M, L)
        yield
        N2 = inv(Nn, Nn)
        yield
        T1 = inv([eye - x for x in Nn], [eye + x for x in N2])
        yield
        Tinv = inv(T1, M)
        yield
    else:
        Tinv = [use_inverse(A[n], t_saved[b][:, h * DK:h * DK + C]) for n, (b, h) in enumerate(units)]
    rhs = [jnp.concatenate([beta[n] * v[n], (beta[n] * egam[n]) * k[n]], axis=1) for n in nu]
    sol = [mmx(Tinv[n], rhs[n], "nn", P_GD["sol"]) for n in nu]
    yield
    qwS = [mmx(jnp.concatenate([q[n] * egam[n], sol[n][:, DK:2 * DK]], axis=0), S[n], "nn", P_GD["ws"]) for n in nu]
    yield
    u = [sol[n][:, 0:DK] - qwS[n][C:2 * C] for n in nu]
    outs = [qwS[n][0:C] + mmx(qk[n], u[n], "nn", P_GD["o"]) for n in nu]
    Sn = [jnp.exp(glast[n]) * S[n] + mmx(k[n] * jnp.exp(grev_c[n]), u[n], "tn", P_GD["su"]) for n in nu]
    zpad = jnp.zeros((C, DK - C), f32)
    t_pack = tuple(jnp.concatenate([x for n in range(b * H, (b + 1) * H) for x in (lax.stop_gradient(Tinv[n]), zpad)],
                                   axis=1) for b in range(len(cs)))
    return tuple(Sn), tuple(jnp.concatenate(outs[b * H:(b + 1) * H], axis=1) for b in range(len(cs))), t_pack


def _chunk_index(tile_chunks, k):
    def index(i):
        chunk = tile_chunks * i + k
        b = chunk // NCH
        return jnp.maximum((SEQ // C) * b + chunk - NCH * b - 1, 0), 0
    return index


def _in_proj(xflat, head, norm_w, w4, conv4):
    tm = 384
    nck = tm // C
    W3 = 3 * HD

    def body(*refs):
        x_refs = refs[:nck]
        head_ref, nw_ref, w4_ref, cw_ref, h_ref, p_ref, ut_ref, cv_ref, w_ref, prev = refs[nck:]
        i = pl.program_id(0)

        @pl.when(i == 0)
        def _():
            prev[...] = jnp.zeros_like(prev)
            w_ref[PC - DK:PC, :] = jnp.zeros((DK, D), bf16)
            for q in range(4):
                w_ref[SHARD_COLS * q:SHARD_COLS * (q + 1), :] = w4_ref[q]

        blocks = []
        for k in range(nck):
            chunk = nck * i + k
            blocks.append(jnp.where(chunk - NCH * (chunk // NCH) == 0, head_ref[...], x_refs[k][...]))
        hval = jnp.concatenate(blocks, axis=0)
        h_ref[...] = hval
        u = rmsnorm(hval, nw_ref[...])
        ut_ref[...] = u.T.astype(bf16)
        p = bdot_nt(u, w_ref[...])
        p_ref[...] = p
        x = p[:, COL_QKV:COL_QKV + W3]
        xx = jnp.concatenate([prev[...], x], axis=0)
        y = cw_ref[3] * x
        for s in (1, 2, 3):
            y = y + cw_ref[3 - s] * pltpu.roll(xx, s, 0)[8:]
        row = i * tm + lax.broadcasted_iota(jnp.int32, (tm, 1), 0)
        tok = jnp.where(row >= TP, row - TP, row)
        cv_ref[...] = jnp.where(tok >= 8, y, 0.0)
        prev[...] = x[tm - 8:tm]

    return pl.pallas_call(
        body, name="in_proj", grid=(N // tm,),
        in_specs=[pl.BlockSpec((C, D), _chunk_index(nck, k)) for k in range(nck)]
        + [pl.BlockSpec((C, D), lambda i: (0, 0)), pl.BlockSpec((1, D), lambda i: (0, 0)),
           pl.BlockSpec((4, SHARD_COLS, D), lambda i: (0, 0, 0), pipeline_mode=pl.Buffered(1)),
           pl.BlockSpec((4, 1, W3), lambda i: (0, 0, 0))],
        out_specs=[pl.BlockSpec((tm, D), lambda i: (i, 0)), pl.BlockSpec((tm, PC), lambda i: (i, 0)),
                   pl.BlockSpec((D, tm), lambda i: (0, i)), pl.BlockSpec((tm, W3), lambda i: (i, 0)),
                   pl.BlockSpec((PC, D), lambda i: (0, 0), pipeline_mode=pl.Buffered(1))],
        out_shape=[jax.ShapeDtypeStruct((N, D), f32), jax.ShapeDtypeStruct((N, PC), f32),
                   jax.ShapeDtypeStruct((D, N), bf16), jax.ShapeDtypeStruct((N, W3), f32),
                   jax.ShapeDtypeStruct((PC, D), bf16)],
        scratch_shapes=[pltpu.VMEM((8, W3), f32)],
        compiler_params=_cp(("arbitrary",)),
    )(*[xflat] * nck, head, norm_w, w4, conv4)


NU = NB * H
_REV = lambda c: NCH - 1 - c
_FWD = lambda c: c


def _tok_spec(w, ix, col=0):
    return pl.BlockSpec((NB, C, w), lambda c: (0, ix(c), col))


def _state_spec(ix):
    return pl.BlockSpec((NB, 1, H, DK, DK), lambda c: (0, ix(c), 0, 0, 0))


def _row_spec(w):
    return pl.BlockSpec((1, w), lambda c: (0, 0))


def _rows(ref):
    return tuple(ref[b] for b in range(NB))


def _hg_extra_specs(ix):
    return [_row_spec(HD), _row_spec(HD)]


def _gd_extra_specs(ix):
    return [_tok_spec(DK, ix, COL_AB // DK), _row_spec(DK), _row_spec(DK)]


def _mix_fwd(proj3, cv, l0, l1, alog, dtb):
    def body(p_ref, c_ref, ab_ref, l0_ref, l1_ref, al_ref, db_ref, oh_ref, sh_ref, og_ref, sg_ref, t_ref, sth, stg):
        @pl.when(pl.program_id(0) == 0)
        def _():
            sth[...] = jnp.zeros_like(sth)
            stg[...] = jnp.zeros_like(stg)

        Sh = tuple(sth[n] for n in range(NU))
        Sg = tuple(stg[n] for n in range(NU))
        for n in range(NU):
            sh_ref[n // H, 0, n % H] = Sh[n]
            sg_ref[n // H, 0, n % H] = Sg[n]
        snh, oh, sng, og, tp = mix_chunk(Sh, _rows(p_ref), l0_ref[...], l1_ref[...],
                                         Sg, _rows(c_ref), _rows(ab_ref), al_ref[...], db_ref[...])
        for n in range(NU):
            sth[n] = snh[n]
            stg[n] = sng[n]
        for b in range(NB):
            oh_ref[b] = oh[b]
            og_ref[b] = og[b]
            t_ref[b] = tp[b]

    tok = jax.ShapeDtypeStruct((NB, TP, HD), f32)
    st = jax.ShapeDtypeStruct((NB, NCH, H, DK, DK), f32)
    return pl.pallas_call(
        body, name="mix_fwd", grid=(NCH,),
        in_specs=[_tok_spec(3 * HD, _FWD), _tok_spec(3 * HD, _FWD), _tok_spec(DK, _FWD, COL_AB // DK),
                  _row_spec(HD), _row_spec(HD), _row_spec(DK), _row_spec(DK)],
        out_specs=[_tok_spec(HD, _FWD), _state_spec(_FWD), _tok_spec(HD, _FWD), _state_spec(_FWD), _tok_spec(HD, _FWD)],
        out_shape=[tok, st, tok, st, tok],
        scratch_shapes=[pltpu.VMEM((NU, DK, DK), f32), pltpu.VMEM((NU, DK, DK), f32)],
        compiler_params=_cp(("arbitrary",)),
    )(proj3, cv, proj3, l0, l1, alog, dtb)


def _hg_bwd(proj3, l0, l1, s_saved, do):
    def body(p_ref, l0_ref, l1_ref, s_ref, do_ref, dp_ref, dl0_ref, dl1_ref, dst):
        @pl.when(pl.program_id(0) == 0)
        def _():
            dst[...] = jnp.zeros_like(dst)
            dl0_ref[...] = jnp.zeros_like(dl0_ref)
            dl1_ref[...] = jnp.zeros_like(dl1_ref)

        S = tuple(s_ref[n // H, 0, n % H] for n in range(NU))
        _, vjp = jax.vjp(hg_chunk, S, _rows(p_ref), l0_ref[...], l1_ref[...])
        dS, dp, dl0, dl1 = vjp((tuple(dst[n] for n in range(NU)), _rows(do_ref)))
        for n in range(NU):
            dst[n] = dS[n]
        for b in range(NB):
            dp_ref[b] = dp[b].astype(bf16)
        dl0_ref[...] += jnp.broadcast_to(dl0, (8, HD))
        dl1_ref[...] += jnp.broadcast_to(dl1, (8, HD))

    acc = pl.BlockSpec((8, HD), lambda c: (0, 0))
    return pl.pallas_call(
        body, name="hg_bwd", grid=(NCH,),
        in_specs=[_tok_spec(3 * HD, _REV)] + _hg_extra_specs(_REV) + [_state_spec(_REV), _tok_spec(HD, _REV)],
        out_specs=[_tok_spec(3 * HD, _REV), acc, acc],
        out_shape=[jax.ShapeDtypeStruct((NB, TP, 3 * HD), bf16), jax.ShapeDtypeStruct((8, HD), f32),
                   jax.ShapeDtypeStruct((8, HD), f32)],
        scratch_shapes=[pltpu.VMEM((NU, DK, DK), f32)],
        compiler_params=_cp(("arbitrary",)),
    )(proj3, l0, l1, s_saved, do)


def _gd_bwd(cv, proj3, conv4, alog, dtb, s_saved, t_saved, do):
    def body(c_ref, ab_ref, al_ref, db_ref, s_ref, t_ref, do_ref, x0_ref, x1_ref, x2_ref, w_ref,
             dx_ref, dw_ref, dab_ref, dal_ref, ddb_ref, dst, nxt):
        @pl.when(pl.program_id(0) == 0)
        def _():
            dst[...] = jnp.zeros_like(dst)
            dal_ref[...] = jnp.zeros_like(dal_ref)
            ddb_ref[...] = jnp.zeros_like(ddb_ref)
            dw_ref[...] = jnp.zeros_like(dw_ref)
            nxt[...] = jnp.zeros_like(nxt)

        S = tuple(s_ref[n // H, 0, n % H] for n in range(NU))
        t_rows = _rows(t_ref)
        fn = lambda *a: gd_chunk(*a, t_saved=t_rows)[0:2]
        _, vjp = jax.vjp(fn, S, _rows(c_ref), _rows(ab_ref), al_ref[...], db_ref[...])
        dS, dc, dab, dal, ddb = vjp((tuple(dst[n] for n in range(NU)), _rows(do_ref)))
        for n in range(NU):
            dst[n] = dS[n]
        for b in range(NB):
            dab_ref[b] = dab[b].astype(bf16)
        dal_ref[...] += jnp.broadcast_to(dal, (8, DK))
        ddb_ref[...] += jnp.broadcast_to(ddb, (8, DK))

        rows8 = lambda p: sum(p[8 * i:8 * i + 8] for i in range(1, C // 8)) + p[0:8]
        for j, x_ref in enumerate((x0_ref, x1_ref, x2_ref)):
            sl = slice(j * HD, (j + 1) * HD)
            dws = [None] * 4
            for b in range(NB):
                x = x_ref[b]
                g = dc[b][:, sl]
                gg = jnp.concatenate([g, nxt[b, :, sl]], axis=0)
                dx = w_ref[3, :, sl] * g
                parts = [x * g]
                for s in (1, 2, 3):
                    gs = pltpu.roll(gg, C + 8 - s, 0)[:C]
                    dx = dx + w_ref[3 - s, :, sl] * gs
                    parts.append(x * gs)
                nxt[b, :, sl] = g[:8]
                dx_ref[b, :, sl] = dx.astype(bf16)
                dws = [p if a is None else a + p for a, p in zip(dws, parts)]
            for s in range(4):
                dw_ref[3 - s, :, sl] += rows8(dws[s])

        @pl.when(pl.program_id(0) == NCH - 1)
        def _():
            for s in range(4):
                dw_ref[s] = jnp.broadcast_to(jnp.sum(dw_ref[s], axis=0, keepdims=True), (8, 3 * HD))

    acc = pl.BlockSpec((8, DK), lambda c: (0, 0))
    return pl.pallas_call(
        body, name="gd_bwd", grid=(NCH,),
        in_specs=[_tok_spec(3 * HD, _REV)] + _gd_extra_specs(_REV)
        + [_state_spec(_REV), _tok_spec(HD, _REV), _tok_spec(HD, _REV)]
        + [_tok_spec(HD, _REV, COL_QKV // HD + j) for j in range(3)]
        + [pl.BlockSpec((4, 1, 3 * HD), lambda c: (0, 0, 0))],
        out_specs=[_tok_spec(3 * HD, _REV), pl.BlockSpec((4, 8, 3 * HD), lambda c: (0, 0, 0)),
                   _tok_spec(DK, _REV), acc, acc],
        out_shape=[jax.ShapeDtypeStruct((NB, TP, 3 * HD), bf16), jax.ShapeDtypeStruct((4, 8, 3 * HD), f32),
                   jax.ShapeDtypeStruct((NB, TP, DK), bf16),
                   jax.ShapeDtypeStruct((8, DK), f32), jax.ShapeDtypeStruct((8, DK), f32)],
        scratch_shapes=[pltpu.VMEM((NU, DK, DK), f32), pltpu.VMEM((NB, 8, 3 * HD), f32)],
        compiler_params=_cp(("arbitrary",)),
    )(cv, proj3, alog, dtb, s_saved, t_saved, do, proj3, proj3, proj3, conv4)


def _out_loss(o_hg, o_gd, proj, hgw, gdw, wout, hflat, fw, target):
    tm = 384

    def body(ohg_ref, ogd_ref, zhg_ref, zgd_ref, hgw_ref, gdw_ref, wo_ref, h_ref, fw_ref, *refs):
        tg_refs = refs[:tm // C]
        (loss_ref, dohg_ref, dogd_ref, dzhg_ref, dzgd_ref, dh_ref, dwo_ref, dhgw_ref, dgdw_ref, dfw_ref) = refs[tm // C:]
        i = pl.program_id(0)

        @pl.when(i == 0)
        def _():
            for r in (loss_ref, dwo_ref, dhgw_ref, dgdw_ref, dfw_ref):
                r[...] = jnp.zeros_like(r)

        row = i * tm + lax.broadcasted_iota(jnp.int32, (tm, 1), 0)
        tok = jnp.where(row >= TP, row - TP, row)
        valid = (tok >= PAD + N_META).astype(f32)
        hval = h_ref[...]
        tgt = jnp.concatenate([r[...] for r in tg_refs], axis=0)

        mixers = ((ohg_ref, zhg_ref, hgw_ref[...]), (ogd_ref, zgd_ref, gdw_ref[...]))
        saved, ys = [], []
        for o_ref, z_ref, w in mixers:
            for hh in range(H):
                sl = slice(hh * DK, (hh + 1) * DK)
                o, z = o_ref[:, sl], z_ref[:, sl]
                r = lax.rsqrt(jnp.mean(o * o, axis=-1, keepdims=True) + EPS)
                n = o * r
                sg = sigmoid(z)
                ws = w * (z * sg)
                saved.append((r, n, sg, z, ws, w))
                ys.append(n * ws)
        y = jnp.concatenate(ys, axis=-1)
        h2 = hval + bdot(y, wo_ref[...])
        r2 = lax.rsqrt(jnp.mean(h2 * h2, axis=-1, keepdims=True) + EPS)
        n2 = h2 * r2
        fwv = fw_ref[...]
        err = (n2 * fwv - tgt) * valid
        loss = (0.5 / D) * jnp.sum(err * err)
        dyf = err * (1.0 / D)
        dn2 = dyf * fwv
        dout = r2 * (dn2 - n2 * jnp.mean(dn2 * n2, axis=-1, keepdims=True))
        dh_ref[...] = dout
        dy = bdot_nt(dout, wo_ref[...])
        dwo_ref[...] += bdot_tn(y, dout)
        dws = []
        for mi, (do_ref, dz_ref) in enumerate(((dohg_ref, dzhg_ref), (dogd_ref, dzgd_ref))):
            dw = jnp.zeros((1, DK), f32)
            for hh in range(H):
                sl = slice(hh * DK, (hh + 1) * DK)
                r, n, sg, z, ws, w = saved[mi * H + hh]
                dyh = dy[:, mi * HD + hh * DK:mi * HD + (hh + 1) * DK]
                t = dyh * n
                dw = dw + jnp.sum(t * (z * sg), axis=0, keepdims=True)
                dz_ref[:, sl] = (t * w * (sg * (1.0 + z * (1.0 - sg)))).astype(bf16)
                dn = dyh * ws
                do_ref[:, sl] = r * (dn - n * jnp.mean(dn * n, axis=-1, keepdims=True))
            dws.append(dw)
        loss_ref[...] += jnp.broadcast_to(loss, (8, DK))
        dhgw_ref[...] += jnp.broadcast_to(dws[0], (8, DK))
        dgdw_ref[...] += jnp.broadcast_to(dws[1], (8, DK))
        dfw_ref[...] += jnp.broadcast_to(jnp.sum(dyf * n2, axis=0, keepdims=True), (8, D))

    row = lambda w: pl.BlockSpec((tm, w), lambda i: (i, 0))
    whole = lambda r, w: pl.BlockSpec((r, w), lambda i: (0, 0))
    col = lambda c0: pl.BlockSpec((tm, HD), lambda i: (i, c0 // HD))

    tgt_spec = lambda k: pl.BlockSpec((C, D), _chunk_index(tm // C, k))
    return pl.pallas_call(
        body, name="out_loss", grid=(N // tm,),
        in_specs=[row(HD), row(HD), col(COL_ZHG), col(COL_ZGD),
                  whole(1, DK), whole(1, DK), whole(D, D), row(D), whole(1, D)] + [tgt_spec(k) for k in range(tm // C)],
        out_specs=[whole(8, DK), row(HD), row(HD), row(HD), row(HD), row(D), whole(D, D),
                   whole(8, DK), whole(8, DK), whole(8, D)],
        out_shape=[jax.ShapeDtypeStruct((8, DK), f32)] + [jax.ShapeDtypeStruct((N, HD), f32)] * 2
        + [jax.ShapeDtypeStruct((N, HD), bf16)] * 2
        + [jax.ShapeDtypeStruct((N, D), f32), jax.ShapeDtypeStruct((D, D), f32),
           jax.ShapeDtypeStruct((8, DK), f32), jax.ShapeDtypeStruct((8, DK), f32), jax.ShapeDtypeStruct((8, D), f32)],
        compiler_params=_cp(("arbitrary",)),
    )(o_hg, o_gd, proj, proj, hgw, gdw, wout, hflat, fw, *[target] * (tm // C))


def _in_bwd(pieces, wbig, hflat, norm_w, dh_res, pbs):
    tm = 384
    nsteps = N // tm
    np_ = len(pieces)
    na = len(pbs)
    offs = [c0 for _, c0 in pieces]
    widths = [d.shape[1] for d, _ in pieces]

    def body(*refs):
        d_refs = refs[:np_]
        w_ref, h_ref, nw_ref, dhr_ref = refs[np_:np_ + 4]
        srcs = refs[np_ + 4:np_ + 4 + na]
        dh_ref, dnw_ref = refs[np_ + 4 + na:np_ + 6 + na]
        dsts = refs[np_ + 6 + na:np_ + 6 + 2 * na]
        sems = refs[np_ + 6 + 2 * na:]
        i = pl.program_id(0)

        def copies():
            if not na:
                return []
            x, y, c, chips = _place()
            return [pltpu.make_async_remote_copy(
                src_ref=srcs[a].at[2 * px + py], dst_ref=dsts[a].at[j], send_sem=sems[0].at[na * j + a],
                recv_sem=sems[1].at[na * j + a], device_id=(px, py, c), device_id_type=MESH)
                for j, (px, py) in enumerate(chips) for a in range(na)]

        @pl.when(i == 0)
        def _():
            dnw_ref[...] = jnp.zeros_like(dnw_ref)
            for cp in copies():
                cp.start()

        du = jnp.zeros((tm, D), f32)
        for d_ref, off, wd in zip(d_refs, offs, widths):
            du = du + bdot(d_ref[...], w_ref[off:off + wd, :])
        _, vjp = jax.vjp(rmsnorm, h_ref[...], nw_ref[...])
        dh, dnw = vjp(du)
        dh_ref[...] = dh + dhr_ref[...]
        dnw_ref[...] += jnp.broadcast_to(dnw, (8, D))

        @pl.when(i == nsteps - 1)
        def _():
            for cp in copies():
                cp.wait()

    row = lambda w: pl.BlockSpec((tm, w), lambda i: (i, 0))
    return pl.pallas_call(
        body, name="in_bwd", grid=(nsteps,),
        in_specs=[row(w) for w in widths]
        + [pl.BlockSpec((PC, D), lambda i: (0, 0)), row(D), pl.BlockSpec((1, D), lambda i: (0, 0)), row(D)] + [ANY] * na,
        out_specs=[row(D), pl.BlockSpec((8, D), lambda i: (0, 0))] + [ANY] * na,
        out_shape=[jax.ShapeDtypeStruct((N, D), f32), jax.ShapeDtypeStruct((8, D), f32)]
        + [jax.ShapeDtypeStruct((3,) + p.shape[1:], p.dtype) for p in pbs],
        scratch_shapes=[pltpu.SemaphoreType.DMA((3 * na,)), pltpu.SemaphoreType.DMA((3 * na,))] if na else [],
        compiler_params=_cp(("arbitrary",)),
    )(*[d for d, _ in pieces], wbig, hflat, norm_w, dh_res, *pbs)


def _w_grad(ut, pieces):
    tk = N // 3
    offs = [c0 for _, c0 in pieces]
    widths = [d.shape[1] for d, _ in pieces]

    def body(u_ref, *refs):
        d_refs, o_ref = refs[:-1], refs[-1]

        @pl.when(pl.program_id(0) == 0)
        def _():
            o_ref[...] = jnp.zeros_like(o_ref)

        u = u_ref[...]
        for d_ref, off, wd in zip(d_refs, offs, widths):
            o_ref[:, off:off + wd] += jnp.dot(u, d_ref[...], preferred_element_type=f32)

    return pl.pallas_call(
        body, name="w_grad", grid=(N // tk,),
        in_specs=[pl.BlockSpec((D, tk), lambda k: (0, k))] + [pl.BlockSpec((tk, w), lambda k: (k, 0)) for w in widths],
        out_specs=pl.BlockSpec((D, PC), lambda k: (0, 0), pipeline_mode=pl.Buffered(1)),
        out_shape=jax.ShapeDtypeStruct((D, PC), f32),
        compiler_params=_cp(("arbitrary",)),
    )(ut, *[d for d, _ in pieces])


def _adam_math(g, w, m, v):
    m2 = ADAM_B1 * m + (1.0 - ADAM_B1) * g
    v2 = ADAM_B2 * v + (1.0 - ADAM_B2) * (g * g)
    m_hat = m2 / (1.0 - ADAM_B1 ** ADAM_STEP)
    v_hat = v2 / (1.0 - ADAM_B2 ** ADAM_STEP)
    delta = -ADAM_LR * (m_hat / (jnp.sqrt(v_hat) + ADAM_EPS) + ADAM_WD * w)
    return delta, m2, v2


def _adamw(gs, w, m, v, name):
    R, Cc = w.shape
    tr = 256 if R % 256 == 0 else R
    ng = len(gs)

    def body(*refs):
        g = refs[0][...]
        for r in refs[1:ng]:
            g = g + r[...]
        w_ref, m_ref, v_ref, g_ref, d_ref, m2_ref, v2_ref = refs[ng:]
        delta, m2, v2 = _adam_math(g, w_ref[...], m_ref[...], v_ref[...])
        g_ref[...] = g
        d_ref[...] = delta
        m2_ref[...] = m2
        v2_ref[...] = v2

    spec = pl.BlockSpec((tr, Cc), lambda i: (i, 0))
    return pl.pallas_call(
        body, name=name, grid=(R // tr,),
        in_specs=[spec] * (ng + 3), out_specs=[spec] * 4,
        out_shape=[jax.ShapeDtypeStruct((R, Cc), f32)] * 4,
        compiler_params=_cp(("arbitrary",)),
    )(*gs, w, m, v)


def _adamw_rows(g, w, m, v, name):
    R, _, Cc = w.shape
    tr = R // 9

    def body(g_ref, w_ref, m_ref, v_ref, go_ref, d_ref, m2_ref, v2_ref):
        g = g_ref[...]
        delta, m2, v2 = _adam_math(g, w_ref[...], m_ref[...], v_ref[...])
        go_ref[...] = g
        d_ref[...] = delta
        m2_ref[...] = m2
        v2_ref[...] = v2

    spec = pl.BlockSpec((tr, 1, Cc), lambda i: (i, 0, 0))
    return pl.pallas_call(
        body, name=name, grid=(R // tr,),
        in_specs=[spec] * 4, out_specs=[spec] * 4,
        out_shape=[jax.ShapeDtypeStruct((R, 1, Cc), f32)] * 4,
        compiler_params=_cp(("arbitrary",)),
    )(g, w, m, v)


def _place():
    x, y, c = lax.axis_index("x"), lax.axis_index("y"), lax.axis_index("c")
    return x, y, c, [(1 - x, y), (x, 1 - y), (1 - x, 1 - y)]


def _gather_weights(cm, halved, whole):
    R, _, Cc = cm.shape
    hw = Cc // 2
    shards = [jax.ShapeDtypeStruct((R, Cc), bf16)] + list(halved) + list(whole)
    nh = 1 + len(halved)
    na = len(shards)

    def body(*refs):
        srcs, dsts = refs[:na], refs[na:2 * na]
        send_sems, recv_sems, loc_sems = refs[2 * na:2 * na + 3]
        stage = refs[2 * na + 3:3 * na + 3]
        raw = refs[3 * na + 3]
        x, y, c, chips = _place()
        me = 2 * x + y
        loads = [pltpu.make_async_copy(srcs[0], raw, loc_sems.at[0])]
        loads += [pltpu.make_async_copy(srcs[i], stage[i], loc_sems.at[i]) for i in range(1, na)]
        locs = [pltpu.make_async_copy(v, d.at[me], loc_sems.at[i]) for i, (v, d) in enumerate(zip(stage, dsts))]
        for cp in loads:
            cp.start()

        def half_of(ref, i, half):
            return ref.at[:, pl.ds(pl.multiple_of(half * hw, hw), hw)] if i == 0 else ref.at[half]

        def ici(j, i, slot):
            px, py = chips[j]
            src = half_of(stage[0] if i == 0 else srcs[i], i, c) if i < nh else srcs[i]
            dst = half_of(dsts[i].at[slot], i, c) if i < nh else dsts[i].at[slot]
            return pltpu.make_async_remote_copy(
                src_ref=src, dst_ref=dst, send_sem=send_sems.at[na * j + i], recv_sem=recv_sems.at[na * j + i],
                device_id=(px, py, c), device_id_type=MESH)

        def d2d(j, i, half):
            px, py = chips[j]
            blk = half_of(dsts[i].at[2 * px + py], i, half)
            return pltpu.make_async_remote_copy(
                src_ref=blk, dst_ref=blk, send_sem=send_sems.at[3 * na + nh * j + i],
                recv_sem=recv_sems.at[3 * na + nh * j + i], device_id=(x, y, 1 - c), device_id_type=MESH)

        sends = [ici(j, i, me) for j in range(3) for i in range(1, na)]
        for cp in sends:
            cp.start()
        loads[0].wait()
        stage[0][...] = raw[:, 0, :].astype(bf16)
        first = [ici(j, 0, me) for j in range(3)]
        for cp in first:
            cp.start()
        sends += first
        locs[0].start()
        for ld, st in zip(loads[1:], locs[1:]):
            ld.wait()
            st.start()
        for j, (px, py) in enumerate(chips):
            for i in range(na):
                ici(j, i, 2 * px + py).wait_recv()
                if i < nh:
                    fwd = d2d(j, i, c)
                    fwd.start()
                    sends.append(fwd)
        for j in range(3):
            for i in range(nh):
                d2d(j, i, 1 - c).wait_recv()
        for cp in sends:
            cp.wait_send()
        for cp in locs:
            cp.wait()

    nsem = 3 * na + 3 * nh
    return pl.pallas_call(
        body, name="gather_weights",
        in_specs=[ANY] * na, out_specs=[ANY] * na,
        out_shape=[jax.ShapeDtypeStruct((4,) + s.shape, s.dtype) for s in shards],
        scratch_shapes=[pltpu.SemaphoreType.DMA((nsem,)), pltpu.SemaphoreType.DMA((nsem,)),
                        pltpu.SemaphoreType.DMA((na,))] + [pltpu.VMEM(s.shape, s.dtype) for s in shards]
        + [pltpu.VMEM(cm.shape, cm.dtype)],
        compiler_params=pltpu.CompilerParams(has_side_effects=True, vmem_limit_bytes=VMEM_LIMIT),
    )(cm, *halved, *whole)


def _swap_halves(gs):
    na = len(gs)
    jobs = [(i, q) for i in range(na) for q in range(gs[i].shape[0])]

    def body(*refs):
        srcs, dsts = refs[:na], refs[na:2 * na]
        send_sems, recv_sems = refs[2 * na:]
        x, y, c, _ = _place()
        cps = [pltpu.make_async_remote_copy(
            src_ref=srcs[i].at[q, 1 - c], dst_ref=dsts[i].at[q], send_sem=send_sems.at[k],
            recv_sem=recv_sems.at[k], device_id=(x, y, 1 - c), device_id_type=MESH)
            for k, (i, q) in enumerate(jobs)]
        for cp in cps:
            cp.start()
        for cp in cps:
            cp.wait()

    return pl.pallas_call(
        body, name="swap_halves",
        in_specs=[ANY] * na, out_specs=[ANY] * na,
        out_shape=[jax.ShapeDtypeStruct(g.shape[0:1] + g.shape[2:], g.dtype) for g in gs],
        scratch_shapes=[pltpu.SemaphoreType.DMA((len(jobs),)), pltpu.SemaphoreType.DMA((len(jobs),))],
        compiler_params=pltpu.CompilerParams(has_side_effects=True),
    )(*gs)


def _add_split(cm_arr, g, s):
    _, _, R, Cg = g.shape
    tr = 128

    def body(sc_ref, g_ref, s_ref, b_ref, o_ref):
        p = g_ref[0, 0] + s_ref[0]
        own = None
        for q in range(4):
            blk = p[:, SHARD_COLS * q:SHARD_COLS * (q + 1)]
            b_ref[q] = blk.astype(bf16)
            mine = jnp.where(sc_ref[1] == q, blk, 0.0)
            own = mine if own is None else own + mine
        o_ref[...] = own

    return pl.pallas_call(
        body, name="add_w_in",
        grid_spec=pltpu.PrefetchScalarGridSpec(
            num_scalar_prefetch=1, grid=(R // tr,),
            in_specs=[pl.BlockSpec((1, 1, tr, Cg), lambda i, sc: (0, sc[0], i, 0)),
                      pl.BlockSpec((1, tr, Cg), lambda i, sc: (0, i, 0))],
            out_specs=[pl.BlockSpec((4, tr, SHARD_COLS), lambda i, sc: (0, i, 0)),
                       pl.BlockSpec((tr, SHARD_COLS), lambda i, sc: (i, 0))]),
        out_shape=[jax.ShapeDtypeStruct((4, R, SHARD_COLS), bf16), jax.ShapeDtypeStruct((R, SHARD_COLS), f32)],
        compiler_params=_cp(("arbitrary",)),
    )(cm_arr, g, s)


def _add_halves(c_arr, g, s, name):
    Q, _, R, Cc = g.shape
    tr = min(R, 128)

    def body(c_ref, g_ref, s_ref, b_ref, f_ref):
        p = g_ref[0, 0] + s_ref[0]
        f_ref[0] = p
        b_ref[0] = p.astype(bf16)

    blk = pl.BlockSpec((1, tr, Cc), lambda q, i, cr: (q, i, 0))
    return pl.pallas_call(
        body, name=name,
        grid_spec=pltpu.PrefetchScalarGridSpec(
            num_scalar_prefetch=1, grid=(Q, R // tr),
            in_specs=[pl.BlockSpec((1, 1, tr, Cc), lambda q, i, cr: (q, cr[0], i, 0)), blk], out_specs=[blk, blk]),
        out_shape=[jax.ShapeDtypeStruct((Q, R, Cc), bf16), jax.ShapeDtypeStruct((Q, R, Cc), f32)],
        compiler_params=_cp(("arbitrary", "arbitrary")),
    )(c_arr, g, s)


_FLIPS = [(fx, fy, fc) for fx in (0, 1) for fy in (0, 1) for fc in (0, 1)][1:]


def _sum_blocks(own, r, name, transposed=False):
    R, Cc = own.shape
    tr = min(R, 256)

    def body(own_ref, r_ref, o_ref):
        acc = own_ref[...]
        for j in range(3):
            acc = acc + r_ref[j].astype(f32)
        o_ref[...] = acc.T if transposed else acc

    return pl.pallas_call(
        body, name=name, grid=(R // tr,),
        in_specs=[pl.BlockSpec((tr, Cc), lambda i: (i, 0)), pl.BlockSpec((3, tr, Cc), lambda i: (0, i, 0))],
        out_specs=pl.BlockSpec((Cc, tr), lambda i: (0, i)) if transposed else pl.BlockSpec((tr, Cc), lambda i: (i, 0)),
        out_shape=jax.ShapeDtypeStruct((Cc, R) if transposed else (R, Cc), f32),
        compiler_params=_cp(("arbitrary",)),
    )(own, r)


def _sum_packs(me8_arr, pack, rp):
    R = pack.shape[0]

    def body(me_ref, pk_ref, rp_ref, o_ref):
        me8 = me_ref[0]
        acc = None
        for d in range(8):
            rel = d ^ me8
            term = jnp.where(rel == 0, pk_ref[...], rp_ref[jnp.maximum(rel - 1, 0)])
            acc = term if acc is None else acc + term
        o_ref[...] = acc

    return pl.pallas_call(
        body, name="sum_packs",
        grid_spec=pltpu.PrefetchScalarGridSpec(
            num_scalar_prefetch=1, grid=(1,),
            in_specs=[pl.BlockSpec((R, 128), lambda i, mr: (0, 0)), pl.BlockSpec((7, R, 128), lambda i, mr: (0, 0, 0))],
            out_specs=pl.BlockSpec((R, 128), lambda i, mr: (0, 0))),
        out_shape=jax.ShapeDtypeStruct((R, 128), f32),
        compiler_params=_cp(("arbitrary",)),
    )(me8_arr, pack, rp)


def _swap_finished(fs, pack):
    na = len(fs)
    R = pack.shape[0]

    def body(*refs):
        srcs, pk = refs[:na], refs[na]
        dsts, rp = refs[na + 1:2 * na + 1], refs[2 * na + 1]
        send_sems, recv_sems = refs[2 * na + 2:]
        x, y, c, _ = _place()
        cps = [pltpu.make_async_remote_copy(
            src_ref=srcs[i], dst_ref=dsts[i], send_sem=send_sems.at[i], recv_sem=recv_sems.at[i],
            device_id=(x, y, 1 - c), device_id_type=MESH) for i in range(na)]
        cps += [pltpu.make_async_remote_copy(
            src_ref=pk, dst_ref=rp.at[k], send_sem=send_sems.at[na + k], recv_sem=recv_sems.at[na + k],
            device_id=(x ^ fx, y ^ fy, c ^ fc), device_id_type=MESH) for k, (fx, fy, fc) in enumerate(_FLIPS)]
        for cp in cps:
            cp.start()
        for cp in cps:
            cp.wait()

    return pl.pallas_call(
        body, name="swap_finished",
        in_specs=[ANY] * (na + 1), out_specs=[ANY] * (na + 1),
        out_shape=[jax.ShapeDtypeStruct(f.shape, f.dtype) for f in fs] + [jax.ShapeDtypeStruct((7, R, 128), f32)],
        scratch_shapes=[pltpu.SemaphoreType.DMA((na + 7,)), pltpu.SemaphoreType.DMA((na + 7,))],
        compiler_params=pltpu.CompilerParams(has_side_effects=True),
    )(*fs, pack)


def _adamw_halves(c_arr, mine, peer, w, m, v, name):
    _, R, Cc = w.shape
    tr = min(R, 256)

    def body(c_ref, mine_ref, peer_ref, w_ref, m_ref, v_ref, g_ref, d_ref, m2_ref, v2_ref):
        g = jnp.where(pl.program_id(0) == c_ref[0], mine_ref[...], peer_ref[...])
        delta, m2, v2 = _adam_math(g, w_ref[0], m_ref[0], v_ref[0])
        g_ref[0] = g
        d_ref[0] = delta
        m2_ref[0] = m2
        v2_ref[0] = v2

    half = pl.BlockSpec((tr, Cc), lambda hh, i, cr: (i, 0))
    full = pl.BlockSpec((1, tr, Cc), lambda hh, i, cr: (hh, i, 0))
    return pl.pallas_call(
        body, name=name,
        grid_spec=pltpu.PrefetchScalarGridSpec(
            num_scalar_prefetch=1, grid=(2, R // tr), in_specs=[half, half, full, full, full], out_specs=[full] * 4),
        out_shape=[jax.ShapeDtypeStruct((2, R, Cc), f32)] * 4,
        compiler_params=_cp(("arbitrary", "arbitrary")),
    )(c_arr, mine, peer, w, m, v)


def _rows8(a):
    flat = a.reshape(-1)
    n = flat.shape[0]
    rows = -(-n // 1024) * 8
    return jnp.pad(flat, (0, rows * 128 - n)).reshape(rows, 128)


def kernel(x, meta_tokens, norm_w, w_in, conv_w, hg_lb_logits, hg_norm_w, gdn_A_log, gdn_dt_bias, gdn_norm_w, w_out, final_norm_w, loss_target, m_meta_tokens, m_norm_w, m_w_in, m_conv_w, m_hg_lb_logits, m_hg_norm_w, m_gdn_A_log, m_gdn_dt_bias, m_gdn_norm_w, m_w_out, m_final_norm_w, v_meta_tokens, v_norm_w, v_w_in, v_conv_w, v_hg_lb_logits, v_hg_norm_w, v_gdn_A_log, v_gdn_dt_bias, v_gdn_norm_w, v_w_out, v_final_norm_w):
    me = 2 * lax.axis_index("x") + lax.axis_index("y")

    g_win, g_wout, g_conv, g_meta = _gather_weights(
        jnp.transpose(w_in, (2, 0, 1)), [w_out[0].astype(bf16).reshape(2, D // 8, D)], [conv_w[0], meta_tokens])
    wout_full = g_wout.reshape(D, D)
    conv4 = jnp.transpose(g_conv, (1, 0, 2)).reshape(4, 1, 3 * HD)
    meta_full = jnp.transpose(g_meta, (1, 0, 2)).reshape(N_META, D)

    c_arr = lax.axis_index("c").reshape(1).astype(jnp.int32)

    def chip_partials(gw, g_wout_part):
        g_in2 = gw.reshape(1, 2, D // 2, PC)
        g_out4 = g_wout_part.reshape(4, 2, D // 8, D)
        s_in, s_out = _swap_halves([g_in2, g_out4])
        pb_blocks, own_in = _add_split(jnp.concatenate([c_arr, me.reshape(1).astype(jnp.int32)]), g_in2, s_in)
        pb_out, pf_out = _add_halves(c_arr, g_out4, s_out, "add_w_out")
        own_out = lax.dynamic_index_in_dim(pf_out, me, axis=0, keepdims=False)
        return [pb_blocks, pb_out], [own_in, own_out]

    (loss8, grad_x, d_meta, d_nw, d_conv, d_lb, d_hgw, d_alog, d_dtb, d_gdw, d_fw, pfs, rs) = _local_step(
        x, loss_target, g_win, wout_full, conv4, meta_full, norm_w, hg_lb_logits, hg_norm_w, gdn_A_log, gdn_dt_bias,
        gdn_norm_w, final_norm_w, chip_partials)

    pack = jnp.concatenate([
        loss8, d_nw[0].reshape(8, 128), d_lb.reshape(8, 128), d_hgw, _rows8(d_alog[0, :H]), _rows8(d_dtb[0, :H]),
        d_gdw, d_fw[0].reshape(8, 128), d_meta.reshape(128, 128), d_conv.reshape(48, 128)], axis=0)
    return _reduce_and_update(
        me, c_arr, grad_x, pfs, rs, pack, meta_tokens, norm_w, w_in, conv_w, hg_lb_logits, hg_norm_w, gdn_A_log,
        gdn_dt_bias, gdn_norm_w, w_out, final_norm_w, m_meta_tokens, m_norm_w, m_w_in, m_conv_w, m_hg_lb_logits,
        m_hg_norm_w, m_gdn_A_log, m_gdn_dt_bias, m_gdn_norm_w, m_w_out, m_final_norm_w, v_meta_tokens, v_norm_w, v_w_in,
        v_conv_w, v_hg_lb_logits, v_hg_norm_w, v_gdn_A_log, v_gdn_dt_bias, v_gdn_norm_w, v_w_out, v_final_norm_w)


def _local_step(x, loss_target, w4, wout_full, conv4, meta_full, norm_w, hg_lb_logits, hg_norm_w, gdn_A_log, gdn_dt_bias,
                gdn_norm_w, final_norm_w, chip_partials):
    head = jnp.concatenate([jnp.zeros((PAD, D), f32), meta_full], axis=0)
    target = loss_target.reshape(NB * SEQ, D)
    l0, l1 = hg_lb_logits[0:1], hg_lb_logits[1:2]
    alog = jnp.pad(gdn_A_log, ((0, 0), (0, DK - H)))
    dtb = jnp.pad(gdn_dt_bias, ((0, 0), (0, DK - H)))
    fw = final_norm_w.reshape(1, D)

    hflat, proj, ut, cv2, wbig = _in_proj(x.reshape(NB * SEQ, D), head, norm_w, w4, conv4)
    proj3 = proj.reshape(NB, TP, PC)
    cv = cv2.reshape(NB, TP, 3 * HD)
    o_hg, s_hg, o_gd, s_gd, t_gd = _mix_fwd(proj3, cv, l0, l1, alog, dtb)
    (loss8, d_ohg, d_ogd, d_zhg, d_zgd, dh_res, g_wout_part, d_hgw, d_gdw, d_fw) = _out_loss(
        o_hg.reshape(N, HD), o_gd.reshape(N, HD), proj, hg_norm_w, gdn_norm_w, wout_full, hflat, fw, target)
    d_hg, d_l0, d_l1 = _hg_bwd(proj3, l0, l1, s_hg, d_ohg.reshape(NB, TP, HD))
    d_qkv, d_conv4, d_ab, d_alog, d_dtb = _gd_bwd(cv, proj3, conv4, alog, dtb, s_gd, t_gd,
                                                  d_ogd.reshape(NB, TP, HD))
    d_hg2, d_qkv2, d_ab2 = d_hg.reshape(N, 3 * HD), d_qkv.reshape(N, 3 * HD), d_ab.reshape(N, DK)
    pieces = [(d_hg2, COL_HG), (d_zhg, COL_ZHG), (d_qkv2, COL_QKV), (d_zgd, COL_ZGD), (d_ab2, COL_AB)]
    gw = _w_grad(ut, pieces)
    pbs, pfs = chip_partials(gw, g_wout_part) if chip_partials else ([], [gw, g_wout_part])
    dh, d_nw, *rs = _in_bwd(pieces, wbig, hflat, norm_w, dh_res, pbs)

    dh3 = dh.reshape(NB, TP, D)
    grad_x = dh3[:, PAD + N_META:, :]
    d_meta = jnp.sum(dh3[:, PAD:PAD + N_META, :], axis=0)
    d_conv = d_conv4[:, 0, :]
    d_lb = jnp.concatenate([d_l0[0:1], d_l1[0:1]], axis=0)
    return loss8, grad_x, d_meta, d_nw, d_conv, d_lb, d_hgw, d_alog, d_dtb, d_gdw, d_fw, pfs, rs


def _reduce_and_update(me, c_arr, grad_x, pfs, rs, pack, meta_tokens, norm_w, w_in, conv_w, hg_lb_logits, hg_norm_w,
                       gdn_A_log, gdn_dt_bias, gdn_norm_w, w_out, final_norm_w, m_meta_tokens, m_norm_w, m_w_in, m_conv_w,
                       m_hg_lb_logits, m_hg_norm_w, m_gdn_A_log, m_gdn_dt_bias, m_gdn_norm_w, m_w_out, m_final_norm_w,
                       v_meta_tokens, v_norm_w, v_w_in, v_conv_w, v_hg_lb_logits, v_hg_norm_w, v_gdn_A_log, v_gdn_dt_bias,
                       v_gdn_norm_w, v_w_out, v_final_norm_w):
    (own_in, own_out), (r_in, r_out) = pfs, rs
    f_in = _sum_blocks(own_in, r_in, "sum_w_in", transposed=True)
    f_out = _sum_blocks(own_out, r_out, "sum_w_out")
    o_in, o_out, r_pack = _swap_finished([f_in, f_out], pack)
    me8_arr = (2 * me + lax.axis_index("c")).reshape(1).astype(jnp.int32)
    small = _sum_packs(me8_arr, pack, r_pack)

    half_out = lambda a: a[0].reshape(2, D // 8, D)
    is0 = lax.axis_index("c") == 0
    g_in = jnp.concatenate([jnp.where(is0, f_in, o_in), jnp.where(is0, o_in, f_in)], axis=1)
    to_cm = lambda a: jnp.transpose(a, (2, 0, 1))
    gi, di, mi, vi = [jnp.transpose(a, (1, 2, 0))[0] for a in _adamw_rows(
        g_in.reshape(SHARD_COLS, 1, D), to_cm(w_in), to_cm(m_w_in), to_cm(v_w_in), "adamw_w_in")]
    go, do_, mo, vo = [a.reshape(D // 4, D) for a in _adamw_halves(
        c_arr, f_out, o_out, half_out(w_out), half_out(m_w_out), half_out(v_w_out), "adamw_w_out")]

    g_meta_full = small[64:192].reshape(N_META, D)
    g_meta_loc = lax.dynamic_slice(g_meta_full, (0, me * 256), (N_META, 256))
    gm, dm, mm_, vm = _adamw([g_meta_loc], meta_tokens, m_meta_tokens, v_meta_tokens, "adamw_meta")
    g_conv_full = small[192:240].reshape(4, 1536)
    g_conv_loc = lax.dynamic_slice(g_conv_full, (0, me * 384), (4, 384))
    gc, dc, mc, vc = _adamw([g_conv_loc], conv_w[0], m_conv_w[0], v_conv_w[0], "adamw_conv")

    reps = [(norm_w, m_norm_w, v_norm_w), (hg_lb_logits, m_hg_lb_logits, v_hg_lb_logits),
            (hg_norm_w, m_hg_norm_w, v_hg_norm_w), (gdn_A_log, m_gdn_A_log, v_gdn_A_log),
            (gdn_dt_bias, m_gdn_dt_bias, v_gdn_dt_bias), (gdn_norm_w, m_gdn_norm_w, v_gdn_norm_w),
            (final_norm_w, m_final_norm_w, v_final_norm_w)]
    wp = jnp.concatenate([_rows8(t[0]) for t in reps], axis=0)
    mp = jnp.concatenate([_rows8(t[1]) for t in reps], axis=0)
    vp = jnp.concatenate([_rows8(t[2]) for t in reps], axis=0)
    gr, dr, mr, vr = _adamw([small[8:64]], wp, mp, vp, "adamw_small")

    def unpack(p):
        outs = []
        for i, t in enumerate(reps):
            n = t[0].size
            outs.append(p[8 * i:8 * i + 8].reshape(-1)[:n].reshape(t[0].shape))
        return outs

    def leaves(meta_v, conv_v, in_v, out_v, rep_p):
        nw, lb, hgw, al, db, gdw, fwv = unpack(rep_p)
        return [meta_v, nw, in_v[None], conv_v[None], lb, hgw, al, db, gdw, out_v[None], fwv]

    loss = small[0, 0]
    return (loss, grad_x, *leaves(gm, gc, gi, go, gr), *leaves(dm, dc, di, do_, dr),
            *leaves(mm_, mc, mi, mo, mr), *leaves(vm, vc, vi, vo, vr))
```

```python
import functools

import jax
import jax.numpy as jnp
from jax import lax
from jax.experimental import pallas as pl
from jax.experimental.pallas import tpu as pltpu

f32 = jnp.float32
bf16 = jnp.bfloat16
MESH = pl.DeviceIdType.MESH
ANY = pl.BlockSpec(memory_space=pl.ANY)

D = 1024
NB = 2
N_META = 16
SEQ = 2048
PAD = 48
TP = PAD + N_META + SEQ
C = 64
NCH = TP // C
N = NB * TP
H = 4
DK = 128
HD = H * DK
PC = 4224
IN_COLS = 4104
SHARD_COLS = IN_COLS // 4
COL_HG, COL_ZHG, COL_QKV, COL_ZGD, COL_AB = 0, 3 * HD, 4 * HD, 7 * HD, 8 * HD
EPS = 1e-6
ADAM_LR, ADAM_B1, ADAM_B2, ADAM_EPS, ADAM_WD, ADAM_STEP = 0.001, 0.9, 0.999, 1e-08, 0.01, 10
VMEM_LIMIT = 56 * 1024 * 1024

P_HG = dict(lvl=1, av=1, qs=1, su=1)
P_GD = dict(kk=1, inv=1, sol=1, ws=1, qk=1, o=1, su=1)


def _cp(sem=None, **kw):
    return pltpu.CompilerParams(dimension_semantics=sem, vmem_limit_bytes=VMEM_LIMIT, **kw)


_DIMS = {"nn": (((1,), (0,)), ((), ())), "nt": (((1,), (1,)), ((), ())), "tn": (((0,), (0,)), ((), ()))}


def _split(x):
    hi = x.astype(bf16)
    return hi, (x - hi.astype(f32)).astype(bf16)


def _dg(a, b, kind, passes):
    d = lambda x, y: lax.dot_general(x, y, _DIMS[kind], preferred_element_type=f32)
    if passes == 1:
        return d(a.astype(bf16), b.astype(bf16))
    ah, al = _split(a)
    bh, bl = _split(b)
    return d(ah, bh) + d(ah, bl) + d(al, bh)


@functools.partial(jax.custom_vjp, nondiff_argnums=(2, 3))
def mmx(a, b, kind, passes):
    return _dg(a, b, kind, passes)


def _mmx_fwd(a, b, kind, passes):
    return _dg(a, b, kind, passes), (a, b)


def _mmx_bwd(kind, passes, res, g):
    a, b = res
    if kind == "nn":
        return _dg(g, b, "nt", passes), _dg(a, g, "tn", passes)
    if kind == "nt":
        return _dg(g, b, "nn", passes), _dg(g, a, "tn", passes)
    return _dg(b, g, "nt", passes), _dg(a, g, "nn", passes)


mmx.defvjp(_mmx_fwd, _mmx_bwd)


def _mask_dg(mask, x):
    xh, xl = _split(x)
    return jnp.dot(jnp.concatenate([mask, mask], axis=1), jnp.concatenate([xh, xl], axis=0), preferred_element_type=f32)


@functools.partial(jax.custom_vjp, nondiff_argnums=(2,))
def mask_mm(mask, x, bwd_passes):
    return _mask_dg(mask, x)


def _mask_fwd(mask, x, bwd_passes):
    return _mask_dg(mask, x), mask


def _mask_bwd(bwd_passes, mask, g):
    d = lambda y: lax.dot_general(mask, y, _DIMS["tn"], preferred_element_type=f32)
    if bwd_passes == 1:
        return None, d(g.astype(bf16))
    gh, gl = _split(g)
    return None, d(gh) + d(gl)


mask_mm.defvjp(_mask_fwd, _mask_bwd)


def bdot(a, b):
    return jnp.dot(a.astype(bf16), b.astype(bf16), preferred_element_type=f32)


def bdot_nt(a, b):
    return lax.dot_general(a.astype(bf16), b.astype(bf16), _DIMS["nt"], preferred_element_type=f32)


def bdot_tn(a, b):
    return lax.dot_general(a.astype(bf16), b.astype(bf16), _DIMS["tn"], preferred_element_type=f32)


def _iota2(n, m):
    return lax.broadcasted_iota(jnp.int32, (n, m), 0), lax.broadcasted_iota(jnp.int32, (n, m), 1)


sigmoid = jax.nn.sigmoid


def silu(x):
    return x * sigmoid(x)


def softplus(x):
    return jnp.maximum(x, 0.0) + jnp.log(1.0 + jnp.exp(-jnp.abs(x)))


def rmsnorm(x, w):
    return x * lax.rsqrt(jnp.mean(x * x, axis=-1, keepdims=True) + EPS) * w


def hg_masks():
    t, r = _iota2(C, C)
    mats = [r <= t, r > t]
    lvl = []
    for l in range(1, 7):
        sz = 1 << l
        half = sz >> 1
        seg_t = t >> l
        upper_t = (t & (sz - 1)) >= half
        mid_t = seg_t * sz + half - 1
        mats.append((upper_t & (r > mid_t) & (r <= t)) | ((~upper_t) & (r > t) & (r <= mid_t)))
        lvl.append(((seg_t == (r >> l)) & upper_t & ((r & (sz - 1)) < half)).astype(f32))
    stk = jnp.concatenate([m.astype(bf16) for m in mats], axis=0)
    return stk, lvl, (t == r).astype(f32)


def _head(a, h):
    return a[:, h * DK:(h + 1) * DK]


def _run(*gens):
    results = [None] * len(gens)
    live = list(range(len(gens)))
    while live:
        for i in list(live):
            try:
                next(gens[i])
            except StopIteration as e:
                results[i] = e.value
                live.remove(i)
    return results


def hg_chunk(St, ps, l0, l1):
    return _run(hg_stages(St, ps, l0, l1))[0]


def gd_chunk(S, cs, abs_, alog, dtb, t_saved=None):
    return _run(gd_stages(S, cs, abs_, alog, dtb, t_saved))[0]


def mix_chunk(St, ps, l0, l1, S, cs, abs_, alog, dtb):
    (sn_h, o_h), (sn_g, o_g, t_pack) = _run(hg_stages(St, ps, l0, l1), gd_stages(S, cs, abs_, alog, dtb))
    return sn_h, o_h, sn_g, o_g, t_pack


def hg_stages(St, ps, l0, l1):
    m = jnp.maximum(l0, l1)
    e0 = jnp.exp(l0 - m)
    e1 = jnp.exp(l1 - m)
    lb = e0 / (e0 + e1)
    stk, lvl, eye = hg_masks()
    msk = [eye] + lvl
    trow = lax.broadcasted_iota(jnp.int32, (C, 1), 0)
    upper = [(trow & ((1 << l) - 1)) >= (1 << (l - 1)) for l in range(1, 7)]
    qs, ks, vs, qG, kR, eGl = [], [], [], [], [], []
    for p in ps:
        pq, pf, v = p[:, 0:HD], p[:, HD:2 * HD], p[:, 2 * HD:3 * HD]
        q = silu(pq)
        f = lb + (1.0 - lb) * sigmoid(pf)
        k = 1.0 - f
        logf = jnp.log(f)
        Dm = mask_mm(stk, logf, 1)
        z = [jnp.where(up, q, k) * jnp.exp(Dm[(2 + i) * C:(3 + i) * C]) for i, up in enumerate(upper)]
        qs.append([q] + z)
        ks.append([k] + z)
        vs.append(v)
        qG.append(q * jnp.exp(Dm[0:C]))
        kR.append(k * jnp.exp(Dm[C:2 * C]))
        eGl.append(jnp.exp(jnp.sum(logf, axis=0, keepdims=True)))
    yield
    units = [(b, h) for b in range(len(ps)) for h in range(H)]
    parts = []
    for i in range(7):
        parts.append([msk[i] * mmx(_head(qs[b][i], h), _head(ks[b][i], h), "nt", P_HG["lvl"]) for b, h in units])
        yield
    A = [functools.reduce(lambda x, y: x + y, [parts[i][n] for i in range(7)]) for n in range(len(units))]
    qS = [mmx(_head(qG[b], h), St[n], "nt", P_HG["qs"]) for n, (b, h) in enumerate(units)]
    Sn = [St[n] * _head(eGl[b], h) + mmx(_head(vs[b], h), _head(kR[b], h), "tn", P_HG["su"])
          for n, (b, h) in enumerate(units)]
    yield
    outs = [mmx(A[n], _head(vs[b], h), "nn", P_HG["av"]) + qS[n] for n, (b, h) in enumerate(units)]
    return tuple(Sn), tuple(jnp.concatenate(outs[b * H:(b + 1) * H], axis=1) for b in range(len(ps)))


@jax.custom_vjp
def use_inverse(A, T):
    return T


def _use_inverse_fwd(A, T):
    return T, T


def _use_inverse_bwd(T, g):
    return -_dg(T, _dg(g, T, "nt", P_GD["inv"]), "tn", P_GD["inv"]), jnp.zeros_like(T)


use_inverse.defvjp(_use_inverse_fwd, _use_inverse_bwd)


def gd_stages(S, cs, abs_, alog, dtb, t_saved=None):
    t, r = _iota2(C, C)
    tri = (r <= t).astype(bf16)
    ups = (r > t).astype(bf16)
    lane = lax.broadcasted_iota(jnp.int32, (1, DK), 1)
    subl = lax.broadcasted_iota(jnp.int32, (8, 1), 0)
    eye = (t == r).astype(f32)
    strict = (r < t).astype(f32)
    bd = ((t >> 4) == (r >> 4)).astype(f32)
    qa, ka, va, b4, gam4, grev4, gam4T, glast4 = [], [], [], [], [], [], [], []
    for c, ab in zip(cs, abs_):
        qa.append(silu(c[:, 0:HD]))
        ka.append(silu(c[:, HD:2 * HD]))
        va.append(silu(c[:, 2 * HD:3 * HD]))
        g4 = -jnp.exp(alog) * softplus(ab + dtb)
        b4.append(sigmoid(ab))
        gam4.append(mask_mm(tri, g4, 2))
        grev4.append(mask_mm(ups, g4, 2))
        gam4T.append(gam4[-1].T)
        glast4.append(jnp.sum(g4, axis=0, keepdims=True))
    yield
    units = [(b, h) for b in range(len(cs)) for h in range(H)]
    nu = range(len(units))
    inv = lambda a, b: [mmx(a[n], b[n], "nn", P_GD["inv"]) for n in nu]
    v = [_head(va[b], h) for b, h in units]
    q = [_head(qa[b], h) for b, h in units]
    k = [_head(ka[b], h) for b, h in units]
    q = [x * lax.rsqrt(jnp.sum(x * x, -1, keepdims=True) + EPS) * (DK ** -0.5) for x in q]
    k = [x * lax.rsqrt(jnp.sum(x * x, -1, keepdims=True) + EPS) for x in k]
    oh = [(lane == h).astype(f32) for h in range(H)]
    gam_c = [jnp.sum(gam4[b] * oh[h], -1, keepdims=True) for b, h in units]
    grev_c = [jnp.sum(grev4[b] * oh[h], -1, keepdims=True) for b, h in units]
    beta = [jnp.sum(b4[b] * (lane == h + H).astype(f32), -1, keepdims=True) for b, h in units]
    glast = [jnp.sum(glast4[b] * oh[h], -1, keepdims=True) for b, h in units]
    gam_r = [jnp.sum(gam4T[b][0:8, :] * (subl == h).astype(f32), axis=0, keepdims=True) for b, h in units]
    dec = [jnp.exp(jnp.where(r <= t, gam_c[n] - gam_r[n], -1e30)) for n in nu]
    egam = [jnp.exp(gam_c[n]) for n in nu]
    kk = [mmx(k[n], k[n], "nt", P_GD["kk"]) for n in nu]
    qk = [mmx(q[n], k[n], "nt", P_GD["qk"]) * dec[n] for n in nu]
    yield
    A = [beta[n] * kk[n] * dec[n] * strict for n in nu]
    Dg = [A[n] * bd for n in nu]
    L = [A[n] - Dg[n] for n in nu]
    if t_saved is None:
        ImD = [eye - Dg[n] for n in nu]
        D2 = inv(Dg, Dg)
        yield
        P1 = inv(ImD, [eye + x for x in D2])
        D4 = inv(D2, D2)
        yield
        P2 = inv(P1, [eye + x for x in D4])
        D8 = inv(D4, D4)
        yield
        M = inv(P2, [eye + x for x in D8])
        yield
        Nn = inv(M, L)
        yield
        N2 = inv(Nn, Nn)
        yield
        T1 = inv([eye - x for x in Nn], [eye + x for x in N2])
        yield
        Tinv = inv(T1, M)
        yield
    else:
        Tinv = [use_inverse(A[n], t_saved[b][:, h * DK:h * DK + C]) for n, (b, h) in enumerate(units)]
    rhs = [jnp.concatenate([beta[n] * v[n], (beta[n] * egam[n]) * k[n]], axis=1) for n in nu]
    sol = [mmx(Tinv[n], rhs[n], "nn", P_GD["sol"]) for n in nu]
    yield
    qwS = [mmx(jnp.concatenate([q[n] * egam[n], sol[n][:, DK:2 * DK]], axis=0), S[n], "nn", P_GD["ws"]) for n in nu]
    yield
    u = [sol[n][:, 0:DK] - qwS[n][C:2 * C] for n in nu]
    outs = [qwS[n][0:C] + mmx(qk[n], u[n], "nn", P_GD["o"]) for n in nu]
    Sn = [jnp.exp(glast[n]) * S[n] + mmx(k[n] * jnp.exp(grev_c[n]), u[n], "tn", P_GD["su"]) for n in nu]
    zpad = jnp.zeros((C, DK - C), f32)
    t_pack = tuple(jnp.concatenate([x for n in range(b * H, (b + 1) * H) for x in (lax.stop_gradient(Tinv[n]), zpad)],
                                   axis=1) for b in range(len(cs)))
    return tuple(Sn), tuple(jnp.concatenate(outs[b * H:(b + 1) * H], axis=1) for b in range(len(cs))), t_pack


def _chunk_index(tile_chunks, k):
    def index(i):
        chunk = tile_chunks * i + k
        b = chunk // NCH
        return jnp.maximum((SEQ // C) * b + chunk - NCH * b - 1, 0), 0
    return index


def _in_proj(xflat, head, norm_w, w4, conv4):
    tm = 384
    nck = tm // C
    W3 = 3 * HD

    def body(*refs):
        x_refs = refs[:nck]
        head_ref, nw_ref, w4_ref, cw_ref, h_ref, p_ref, ut_ref, cv_ref, w_ref, prev = refs[nck:]
        i = pl.program_id(0)

        @pl.when(i == 0)
        def _():
            prev[...] = jnp.zeros_like(prev)
            w_ref[PC - DK:PC, :] = jnp.zeros((DK, D), bf16)
            for q in range(4):
                w_ref[SHARD_COLS * q:SHARD_COLS * (q + 1), :] = w4_ref[q]

        blocks = []
        for k in range(nck):
            chunk = nck * i + k
            blocks.append(jnp.where(chunk - NCH * (chunk // NCH) == 0, head_ref[...], x_refs[k][...]))
        hval = jnp.concatenate(blocks, axis=0)
        h_ref[...] = hval
        u = rmsnorm(hval, nw_ref[...])
        ut_ref[...] = u.T.astype(bf16)
        p = bdot_nt(u, w_ref[...])
        p_ref[...] = p
        x = p[:, COL_QKV:COL_QKV + W3]
        xx = jnp.concatenate([prev[...], x], axis=0)
        y = cw_ref[3] * x
        for s in (1, 2, 3):
            y = y + cw_ref[3 - s] * pltpu.roll(xx, s, 0)[8:]
        row = i * tm + lax.broadcasted_iota(jnp.int32, (tm, 1), 0)
        tok = jnp.where(row >= TP, row - TP, row)
        cv_ref[...] = jnp.where(tok >= 8, y, 0.0)
        prev[...] = x[tm - 8:tm]

    return pl.pallas_call(
        body, name="in_proj", grid=(N // tm,),
        in_specs=[pl.BlockSpec((C, D), _chunk_index(nck, k)) for k in range(nck)]
        + [pl.BlockSpec((C, D), lambda i: (0, 0)), pl.BlockSpec((1, D), lambda i: (0, 0)),
           pl.BlockSpec((4, SHARD_COLS, D), lambda i: (0, 0, 0), pipeline_mode=pl.Buffered(1)),
           pl.BlockSpec((4, 1, W3), lambda i: (0, 0, 0))],
        out_specs=[pl.BlockSpec((tm, D), lambda i: (i, 0)), pl.BlockSpec((tm, PC), lambda i: (i, 0)),
                   pl.BlockSpec((D, tm), lambda i: (0, i)), pl.BlockSpec((tm, W3), lambda i: (i, 0)),
                   pl.BlockSpec((PC, D), lambda i: (0, 0), pipeline_mode=pl.Buffered(1))],
        out_shape=[jax.ShapeDtypeStruct((N, D), f32), jax.ShapeDtypeStruct((N, PC), f32),
                   jax.ShapeDtypeStruct((D, N), bf16), jax.ShapeDtypeStruct((N, W3), f32),
                   jax.ShapeDtypeStruct((PC, D), bf16)],
        scratch_shapes=[pltpu.VMEM((8, W3), f32)],
        compiler_params=_cp(("arbitrary",)),
    )(*[xflat] * nck, head, norm_w, w4, conv4)


NU = NB * H
_REV = lambda c: NCH - 1 - c
_FWD = lambda c: c


def _tok_spec(w, ix, col=0):
    return pl.BlockSpec((NB, C, w), lambda c: (0, ix(c), col))


def _state_spec(ix):
    return pl.BlockSpec((NB, 1, H, DK, DK), lambda c: (0, ix(c), 0, 0, 0))


def _row_spec(w):
    return pl.BlockSpec((1, w), lambda c: (0, 0))


def _rows(ref):
    return tuple(ref[b] for b in range(NB))


def _hg_extra_specs(ix):
    return [_row_spec(HD), _row_spec(HD)]


def _gd_extra_specs(ix):
    return [_tok_spec(DK, ix, COL_AB // DK), _row_spec(DK), _row_spec(DK)]


def _mix_fwd(proj3, cv, l0, l1, alog, dtb):
    def body(p_ref, c_ref, ab_ref, l0_ref, l1_ref, al_ref, db_ref, oh_ref, sh_ref, og_ref, sg_ref, t_ref, sth, stg):
        @pl.when(pl.program_id(0) == 0)
        def _():
            sth[...] = jnp.zeros_like(sth)
            stg[...] = jnp.zeros_like(stg)

        Sh = tuple(sth[n] for n in range(NU))
        Sg = tuple(stg[n] for n in range(NU))
        for n in range(NU):
            sh_ref[n // H, 0, n % H] = Sh[n]
            sg_ref[n // H, 0, n % H] = Sg[n]
        snh, oh, sng, og, tp = mix_chunk(Sh, _rows(p_ref), l0_ref[...], l1_ref[...],
                                         Sg, _rows(c_ref), _rows(ab_ref), al_ref[...], db_ref[...])
        for n in range(NU):
            sth[n] = snh[n]
            stg[n] = sng[n]
        for b in range(NB):
            oh_ref[b] = oh[b]
            og_ref[b] = og[b]
            t_ref[b] = tp[b]

    tok = jax.ShapeDtypeStruct((NB, TP, HD), f32)
    st = jax.ShapeDtypeStruct((NB, NCH, H, DK, DK), f32)
    return pl.pallas_call(
        body, name="mix_fwd", grid=(NCH,),
        in_specs=[_tok_spec(3 * HD, _FWD), _tok_spec(3 * HD, _FWD), _tok_spec(DK, _FWD, COL_AB // DK),
                  _row_spec(HD), _row_spec(HD), _row_spec(DK), _row_spec(DK)],
        out_specs=[_tok_spec(HD, _FWD), _state_spec(_FWD), _tok_spec(HD, _FWD), _state_spec(_FWD), _tok_spec(HD, _FWD)],
        out_shape=[tok, st, tok, st, tok],
        scratch_shapes=[pltpu.VMEM((NU, DK, DK), f32), pltpu.VMEM((NU, DK, DK), f32)],
        compiler_params=_cp(("arbitrary",)),
    )(proj3, cv, proj3, l0, l1, alog, dtb)


def _hg_bwd(proj3, l0, l1, s_saved, do):
    def body(p_ref, l0_ref, l1_ref, s_ref, do_ref, dp_ref, dl0_ref, dl1_ref, dst):
        @pl.when(pl.program_id(0) == 0)
        def _():
            dst[...] = jnp.zeros_like(dst)
            dl0_ref[...] = jnp.zeros_like(dl0_ref)
            dl1_ref[...] = jnp.zeros_like(dl1_ref)

        S = tuple(s_ref[n // H, 0, n % H] for n in range(NU))
        _, vjp = jax.vjp(hg_chunk, S, _rows(p_ref), l0_ref[...], l1_ref[...])
        dS, dp, dl0, dl1 = vjp((tuple(dst[n] for n in range(NU)), _rows(do_ref)))
        for n in range(NU):
            dst[n] = dS[n]
        for b in range(NB):
            dp_ref[b] = dp[b].astype(bf16)
        dl0_ref[...] += jnp.broadcast_to(dl0, (8, HD))
        dl1_ref[...] += jnp.broadcast_to(dl1, (8, HD))

    acc = pl.BlockSpec((8, HD), lambda c: (0, 0))
    return pl.pallas_call(
        body, name="hg_bwd", grid=(NCH,),
        in_specs=[_tok_spec(3 * HD, _REV)] + _hg_extra_specs(_REV) + [_state_spec(_REV), _tok_spec(HD, _REV)],
        out_specs=[_tok_spec(3 * HD, _REV), acc, acc],
        out_shape=[jax.ShapeDtypeStruct((NB, TP, 3 * HD), bf16), jax.ShapeDtypeStruct((8, HD), f32),
                   jax.ShapeDtypeStruct((8, HD), f32)],
        scratch_shapes=[pltpu.VMEM((NU, DK, DK), f32)],
        compiler_params=_cp(("arbitrary",)),
    )(proj3, l0, l1, s_saved, do)


def _gd_bwd(cv, proj3, conv4, alog, dtb, s_saved, t_saved, do):
    def body(c_ref, ab_ref, al_ref, db_ref, s_ref, t_ref, do_ref, x0_ref, x1_ref, x2_ref, w_ref,
             dx_ref, dw_ref, dab_ref, dal_ref, ddb_ref, dst, nxt):
        @pl.when(pl.program_id(0) == 0)
        def _():
            dst[...] = jnp.zeros_like(dst)
            dal_ref[...] = jnp.zeros_like(dal_ref)
            ddb_ref[...] = jnp.zeros_like(ddb_ref)
            dw_ref[...] = jnp.zeros_like(dw_ref)
            nxt[...] = jnp.zeros_like(nxt)

        S = tuple(s_ref[n // H, 0, n % H] for n in range(NU))
        t_rows = _rows(t_ref)
        fn = lambda *a: gd_chunk(*a, t_saved=t_rows)[0:2]
        _, vjp = jax.vjp(fn, S, _rows(c_ref), _rows(ab_ref), al_ref[...], db_ref[...])
        dS, dc, dab, dal, ddb = vjp((tuple(dst[n] for n in range(NU)), _rows(do_ref)))
        for n in range(NU):
            dst[n] = dS[n]
        for b in range(NB):
            dab_ref[b] = dab[b].astype(bf16)
        dal_ref[...] += jnp.broadcast_to(dal, (8, DK))
        ddb_ref[...] += jnp.broadcast_to(ddb, (8, DK))

        rows8 = lambda p: sum(p[8 * i:8 * i + 8] for i in range(1, C // 8)) + p[0:8]
        for j, x_ref in enumerate((x0_ref, x1_ref, x2_ref)):
            sl = slice(j * HD, (j + 1) * HD)
            dws = [None] * 4
            for b in range(NB):
                x = x_ref[b]
                g = dc[b][:, sl]
                gg = jnp.concatenate([g, nxt[b, :, sl]], axis=0)
                dx = w_ref[3, :, sl] * g
                parts = [x * g]
                for s in (1, 2, 3):
                    gs = pltpu.roll(gg, C + 8 - s, 0)[:C]
                    dx = dx + w_ref[3 - s, :, sl] * gs
                    parts.append(x * gs)
                nxt[b, :, sl] = g[:8]
                dx_ref[b, :, sl] = dx.astype(bf16)
                dws = [p if a is None else a + p for a, p in zip(dws, parts)]
            for s in range(4):
                dw_ref[3 - s, :, sl] += rows8(dws[s])

        @pl.when(pl.program_id(0) == NCH - 1)
        def _():
            for s in range(4):
                dw_ref[s] = jnp.broadcast_to(jnp.sum(dw_ref[s], axis=0, keepdims=True), (8, 3 * HD))

    acc = pl.BlockSpec((8, DK), lambda c: (0, 0))
    return pl.pallas_call(
        body, name="gd_bwd", grid=(NCH,),
        in_specs=[_tok_spec(3 * HD, _REV)] + _gd_extra_specs(_REV)
        + [_state_spec(_REV), _tok_spec(HD, _REV), _tok_spec(HD, _REV)]
        + [_tok_spec(HD, _REV, COL_QKV // HD + j) for j in range(3)]
        + [pl.BlockSpec((4, 1, 3 * HD), lambda c: (0, 0, 0))],
        out_specs=[_tok_spec(3 * HD, _REV), pl.BlockSpec((4, 8, 3 * HD), lambda c: (0, 0, 0)),
                   _tok_spec(DK, _REV), acc, acc],
        out_shape=[jax.ShapeDtypeStruct((NB, TP, 3 * HD), bf16), jax.ShapeDtypeStruct((4, 8, 3 * HD), f32),
                   jax.ShapeDtypeStruct((NB, TP, DK), bf16),
                   jax.ShapeDtypeStruct((8, DK), f32), jax.ShapeDtypeStruct((8, DK), f32)],
        scratch_shapes=[pltpu.VMEM((NU, DK, DK), f32), pltpu.VMEM((NB, 8, 3 * HD), f32)],
        compiler_params=_cp(("arbitrary",)),
    )(cv, proj3, alog, dtb, s_saved, t_saved, do, proj3, proj3, proj3, conv4)


def _out_loss(o_hg, o_gd, proj, hgw, gdw, wout, hflat, fw, target):
    tm = 384

    def body(ohg_ref, ogd_ref, zhg_ref, zgd_ref, hgw_ref, gdw_ref, wo_ref, h_ref, fw_ref, *refs):
        tg_refs = refs[:tm // C]
        (loss_ref, dohg_ref, dogd_ref, dzhg_ref, dzgd_ref, dh_ref, dwo_ref, dhgw_ref, dgdw_ref, dfw_ref) = refs[tm // C:]
        i = pl.program_id(0)

        @pl.when(i == 0)
        def _():
            for r in (loss_ref, dwo_ref, dhgw_ref, dgdw_ref, dfw_ref):
                r[...] = jnp.zeros_like(r)

        row = i * tm + lax.broadcasted_iota(jnp.int32, (tm, 1), 0)
        tok = jnp.where(row >= TP, row - TP, row)
        valid = (tok >= PAD + N_META).astype(f32)
        hval = h_ref[...]
        tgt = jnp.concatenate([r[...] for r in tg_refs], axis=0)

        mixers = ((ohg_ref, zhg_ref, hgw_ref[...]), (ogd_ref, zgd_ref, gdw_ref[...]))
        saved, ys = [], []
        for o_ref, z_ref, w in mixers:
            for hh in range(H):
                sl = slice(hh * DK, (hh + 1) * DK)
                o, z = o_ref[:, sl], z_ref[:, sl]
                r = lax.rsqrt(jnp.mean(o * o, axis=-1, keepdims=True) + EPS)
                n = o * r
                sg = sigmoid(z)
                ws = w * (z * sg)
                saved.append((r, n, sg, z, ws, w))
                ys.append(n * ws)
        y = jnp.concatenate(ys, axis=-1)
        h2 = hval + bdot(y, wo_ref[...])
        r2 = lax.rsqrt(jnp.mean(h2 * h2, axis=-1, keepdims=True) + EPS)
        n2 = h2 * r2
        fwv = fw_ref[...]
        err = (n2 * fwv - tgt) * valid
        loss = (0.5 / D) * jnp.sum(err * err)
        dyf = err * (1.0 / D)
        dn2 = dyf * fwv
        dout = r2 * (dn2 - n2 * jnp.mean(dn2 * n2, axis=-1, keepdims=True))
        dh_ref[...] = dout
        dy = bdot_nt(dout, wo_ref[...])
        dwo_ref[...] += bdot_tn(y, dout)
        dws = []
        for mi, (do_ref, dz_ref) in enumerate(((dohg_ref, dzhg_ref), (dogd_ref, dzgd_ref))):
            dw = jnp.zeros((1, DK), f32)
            for hh in range(H):
                sl = slice(hh * DK, (hh + 1) * DK)
                r, n, sg, z, ws, w = saved[mi * H + hh]
                dyh = dy[:, mi * HD + hh * DK:mi * HD + (hh + 1) * DK]
                t = dyh * n
                dw = dw + jnp.sum(t * (z * sg), axis=0, keepdims=True)
                dz_ref[:, sl] = (t * w * (sg * (1.0 + z * (1.0 - sg)))).astype(bf16)
                dn = dyh * ws
                do_ref[:, sl] = r * (dn - n * jnp.mean(dn * n, axis=-1, keepdims=True))
            dws.append(dw)
        loss_ref[...] += jnp.broadcast_to(loss, (8, DK))
        dhgw_ref[...] += jnp.broadcast_to(dws[0], (8, DK))
        dgdw_ref[...] += jnp.broadcast_to(dws[1], (8, DK))
        dfw_ref[...] += jnp.broadcast_to(jnp.sum(dyf * n2, axis=0, keepdims=True), (8, D))

    row = lambda w: pl.BlockSpec((tm, w), lambda i: (i, 0))
    whole = lambda r, w: pl.BlockSpec((r, w), lambda i: (0, 0))
    col = lambda c0: pl.BlockSpec((tm, HD), lambda i: (i, c0 // HD))

    tgt_spec = lambda k: pl.BlockSpec((C, D), _chunk_index(tm // C, k))
    return pl.pallas_call(
        body, name="out_loss", grid=(N // tm,),
        in_specs=[row(HD), row(HD), col(COL_ZHG), col(COL_ZGD),
                  whole(1, DK), whole(1, DK), whole(D, D), row(D), whole(1, D)] + [tgt_spec(k) for k in range(tm // C)],
        out_specs=[whole(8, DK), row(HD), row(HD), row(HD), row(HD), row(D), whole(D, D),
                   whole(8, DK), whole(8, DK), whole(8, D)],
        out_shape=[jax.ShapeDtypeStruct((8, DK), f32)] + [jax.ShapeDtypeStruct((N, HD), f32)] * 2
        + [jax.ShapeDtypeStruct((N, HD), bf16)] * 2
        + [jax.ShapeDtypeStruct((N, D), f32), jax.ShapeDtypeStruct((D, D), f32),
           jax.ShapeDtypeStruct((8, DK), f32), jax.ShapeDtypeStruct((8, DK), f32), jax.ShapeDtypeStruct((8, D), f32)],
        compiler_params=_cp(("arbitrary",)),
    )(o_hg, o_gd, proj, proj, hgw, gdw, wout, hflat, fw, *[target] * (tm // C))


def _in_bwd(pieces, wbig, hflat, norm_w, dh_res, pbs):
    tm = 384
    nsteps = N // tm
    np_ = len(pieces)
    na = len(pbs)
    offs = [c0 for _, c0 in pieces]
    widths = [d.shape[1] for d, _ in pieces]

    def body(*refs):
        d_refs = refs[:np_]
        w_ref, h_ref, nw_ref, dhr_ref = refs[np_:np_ + 4]
        srcs = refs[np_ + 4:np_ + 4 + na]
        dh_ref, dnw_ref = refs[np_ + 4 + na:np_ + 6 + na]
        dsts = refs[np_ + 6 + na:np_ + 6 + 2 * na]
        sems = refs[np_ + 6 + 2 * na:]
        i = pl.program_id(0)

        def copies():
            if not na:
                return []
            x, y, c, chips = _place()
            return [pltpu.make_async_remote_copy(
                src_ref=srcs[a].at[2 * px + py], dst_ref=dsts[a].at[j], send_sem=sems[0].at[na * j + a],
                recv_sem=sems[1].at[na * j + a], device_id=(px, py, c), device_id_type=MESH)
                for j, (px, py) in enumerate(chips) for a in range(na)]

        @pl.when(i == 0)
        def _():
            dnw_ref[...] = jnp.zeros_like(dnw_ref)
            for cp in copies():
                cp.start()

        du = jnp.zeros((tm, D), f32)
        for d_ref, off, wd in zip(d_refs, offs, widths):
            du = du + bdot(d_ref[...], w_ref[off:off + wd, :])
        _, vjp = jax.vjp(rmsnorm, h_ref[...], nw_ref[...])
        dh, dnw = vjp(du)
        dh_ref[...] = dh + dhr_ref[...]
        dnw_ref[...] += jnp.broadcast_to(dnw, (8, D))

        @pl.when(i == nsteps - 1)
        def _():
            for cp in copies():
                cp.wait()

    row = lambda w: pl.BlockSpec((tm, w), lambda i: (i, 0))
    return pl.pallas_call(
        body, name="in_bwd", grid=(nsteps,),
        in_specs=[row(w) for w in widths]
        + [pl.BlockSpec((PC, D), lambda i: (0, 0)), row(D), pl.BlockSpec((1, D), lambda i: (0, 0)), row(D)] + [ANY] * na,
        out_specs=[row(D), pl.BlockSpec((8, D), lambda i: (0, 0))] + [ANY] * na,
        out_shape=[jax.ShapeDtypeStruct((N, D), f32), jax.ShapeDtypeStruct((8, D), f32)]
        + [jax.ShapeDtypeStruct((3,) + p.shape[1:], p.dtype) for p in pbs],
        scratch_shapes=[pltpu.SemaphoreType.DMA((3 * na,)), pltpu.SemaphoreType.DMA((3 * na,))] if na else [],
        compiler_params=_cp(("arbitrary",)),
    )(*[d for d, _ in pieces], wbig, hflat, norm_w, dh_res, *pbs)


def _w_grad(ut, pieces):
    bw = HD
    steps = PC // bw + (PC % bw > 0)
    plan = []
    for d, c0 in pieces:
        w = d.shape[1]
        plan.append((c0 // bw, max(1, w // bw), min(w, bw)))

    def body(u_ref, *refs):
        d_refs, o_ref = refs[:-1], refs[-1]
        j = pl.program_id(0)
        for d_ref, (s0, nb, w) in zip(d_refs, plan):
            @pl.when((j >= s0) & (j < s0 + nb))
            def _(d_ref=d_ref, w=w):
                r = jnp.dot(u_ref[...], d_ref[...], preferred_element_type=f32)
                if w == bw:
                    o_ref[...] = r
                else:
                    o_ref[:, 0:w] = r
                    o_ref[:, w:] = jnp.zeros((D, bw - w), f32)

    def d_spec(s0, nb, w):
        ix = lambda j: (0, jnp.clip(j - s0, 0, nb - 1))
        return pl.BlockSpec((N, w), ix) if nb > 1 else pl.BlockSpec((N, w), ix, pipeline_mode=pl.Buffered(1))

    return pl.pallas_call(
        body, name="w_grad", grid=(steps,),
        in_specs=[pl.BlockSpec((D, N), lambda j: (0, 0), pipeline_mode=pl.Buffered(1))]
        + [d_spec(*p) for p in plan],
        out_specs=pl.BlockSpec((D, bw), lambda j: (0, j)),
        out_shape=jax.ShapeDtypeStruct((D, PC), f32),
        compiler_params=_cp(("arbitrary",)),
    )(ut, *[d for d, _ in pieces])


def _adam_math(g, w, m, v):
    m2 = ADAM_B1 * m + (1.0 - ADAM_B1) * g
    v2 = ADAM_B2 * v + (1.0 - ADAM_B2) * (g * g)
    m_hat = m2 / (1.0 - ADAM_B1 ** ADAM_STEP)
    v_hat = v2 / (1.0 - ADAM_B2 ** ADAM_STEP)
    delta = -ADAM_LR * (m_hat / (jnp.sqrt(v_hat) + ADAM_EPS) + ADAM_WD * w)
    return delta, m2, v2


def _adamw(gs, w, m, v, name):
    R, Cc = w.shape
    tr = 256 if R % 256 == 0 else R
    ng = len(gs)

    def body(*refs):
        g = refs[0][...]
        for r in refs[1:ng]:
            g = g + r[...]
        w_ref, m_ref, v_ref, g_ref, d_ref, m2_ref, v2_ref = refs[ng:]
        delta, m2, v2 = _adam_math(g, w_ref[...], m_ref[...], v_ref[...])
        g_ref[...] = g
        d_ref[...] = delta
        m2_ref[...] = m2
        v2_ref[...] = v2

    spec = pl.BlockSpec((tr, Cc), lambda i: (i, 0))
    return pl.pallas_call(
        body, name=name, grid=(R // tr,),
        in_specs=[spec] * (ng + 3), out_specs=[spec] * 4,
        out_shape=[jax.ShapeDtypeStruct((R, Cc), f32)] * 4,
        compiler_params=_cp(("arbitrary",)),
    )(*gs, w, m, v)


def _adamw_rows(g, w, m, v, name):
    R, _, Cc = w.shape
    tr = R // 9

    def body(g_ref, w_ref, m_ref, v_ref, go_ref, d_ref, m2_ref, v2_ref):
        g = g_ref[...]
        delta, m2, v2 = _adam_math(g, w_ref[...], m_ref[...], v_ref[...])
        go_ref[...] = g
        d_ref[...] = delta
        m2_ref[...] = m2
        v2_ref[...] = v2

    spec = pl.BlockSpec((tr, 1, Cc), lambda i: (i, 0, 0))
    return pl.pallas_call(
        body, name=name, grid=(R // tr,),
        in_specs=[spec] * 4, out_specs=[spec] * 4,
        out_shape=[jax.ShapeDtypeStruct((R, 1, Cc), f32)] * 4,
        compiler_params=_cp(("arbitrary",)),
    )(g, w, m, v)


def _place():
    x, y, c = lax.axis_index("x"), lax.axis_index("y"), lax.axis_index("c")
    return x, y, c, [(1 - x, y), (x, 1 - y), (1 - x, 1 - y)]


def _gather_weights(cm, halved, whole):
    R, _, Cc = cm.shape
    hw = Cc // 2
    shards = [jax.ShapeDtypeStruct((R, Cc), bf16)] + list(halved) + list(whole)
    nh = 1 + len(halved)
    na = len(shards)

    def body(*refs):
        srcs, dsts = refs[:na], refs[na:2 * na]
        send_sems, recv_sems, loc_sems = refs[2 * na:2 * na + 3]
        stage = refs[2 * na + 3:3 * na + 3]
        raw = refs[3 * na + 3]
        x, y, c, chips = _place()
        me = 2 * x + y
        loads = [pltpu.make_async_copy(srcs[0], raw, loc_sems.at[0])]
        loads += [pltpu.make_async_copy(srcs[i], stage[i], loc_sems.at[i]) for i in range(1, na)]
        locs = [pltpu.make_async_copy(v, d.at[me], loc_sems.at[i]) for i, (v, d) in enumerate(zip(stage, dsts))]
        for cp in loads:
            cp.start()

        def half_of(ref, i, half):
            return ref.at[:, pl.ds(pl.multiple_of(half * hw, hw), hw)] if i == 0 else ref.at[half]

        def ici(j, i, slot):
            px, py = chips[j]
            src = half_of(stage[0] if i == 0 else srcs[i], i, c) if i < nh else srcs[i]
            dst = half_of(dsts[i].at[slot], i, c) if i < nh else dsts[i].at[slot]
            return pltpu.make_async_remote_copy(
                src_ref=src, dst_ref=dst, send_sem=send_sems.at[na * j + i], recv_sem=recv_sems.at[na * j + i],
                device_id=(px, py, c), device_id_type=MESH)

        def d2d(j, i, half):
            px, py = chips[j]
            blk = half_of(dsts[i].at[2 * px + py], i, half)
            return pltpu.make_async_remote_copy(
                src_ref=blk, dst_ref=blk, send_sem=send_sems.at[3 * na + nh * j + i],
                recv_sem=recv_sems.at[3 * na + nh * j + i], device_id=(x, y, 1 - c), device_id_type=MESH)

        sends = [ici(j, i, me) for j in range(3) for i in range(1, na)]
        for cp in sends:
            cp.start()
        loads[0].wait()
        stage[0][...] = raw[:, 0, :].astype(bf16)
        first = [ici(j, 0, me) for j in range(3)]
        for cp in first:
            cp.start()
        sends += first
        locs[0].start()
        for ld, st in zip(loads[1:], locs[1:]):
            ld.wait()
            st.start()
        for j, (px, py) in enumerate(chips):
            for i in range(na):
                ici(j, i, 2 * px + py).wait_recv()
                if i < nh:
                    fwd = d2d(j, i, c)
                    fwd.start()
                    sends.append(fwd)
        for j in range(3):
            for i in range(nh):
                d2d(j, i, 1 - c).wait_recv()
        for cp in sends:
            cp.wait_send()
        for cp in locs:
            cp.wait()

    nsem = 3 * na + 3 * nh
    return pl.pallas_call(
        body, name="gather_weights",
        in_specs=[ANY] * na, out_specs=[ANY] * na,
        out_shape=[jax.ShapeDtypeStruct((4,) + s.shape, s.dtype) for s in shards],
        scratch_shapes=[pltpu.SemaphoreType.DMA((nsem,)), pltpu.SemaphoreType.DMA((nsem,)),
                        pltpu.SemaphoreType.DMA((na,))] + [pltpu.VMEM(s.shape, s.dtype) for s in shards]
        + [pltpu.VMEM(cm.shape, cm.dtype)],
        compiler_params=pltpu.CompilerParams(has_side_effects=True, vmem_limit_bytes=VMEM_LIMIT),
    )(cm, *halved, *whole)


def _swap_halves(gs):
    na = len(gs)
    jobs = [(i, q) for i in range(na) for q in range(gs[i].shape[0])]

    def body(*refs):
        srcs, dsts = refs[:na], refs[na:2 * na]
        send_sems, recv_sems = refs[2 * na:]
        x, y, c, _ = _place()
        cps = [pltpu.make_async_remote_copy(
            src_ref=srcs[i].at[q, 1 - c], dst_ref=dsts[i].at[q], send_sem=send_sems.at[k],
            recv_sem=recv_sems.at[k], device_id=(x, y, 1 - c), device_id_type=MESH)
            for k, (i, q) in enumerate(jobs)]
        for cp in cps:
            cp.start()
        for cp in cps:
            cp.wait()

    return pl.pallas_call(
        body, name="swap_halves",
        in_specs=[ANY] * na, out_specs=[ANY] * na,
        out_shape=[jax.ShapeDtypeStruct(g.shape[0:1] + g.shape[2:], g.dtype) for g in gs],
        scratch_shapes=[pltpu.SemaphoreType.DMA((len(jobs),)), pltpu.SemaphoreType.DMA((len(jobs),))],
        compiler_params=pltpu.CompilerParams(has_side_effects=True),
    )(*gs)


def _add_split(cm_arr, g, s):
    _, _, R, Cg = g.shape
    tr = 128

    def body(sc_ref, g_ref, s_ref, b_ref, o_ref):
        p = g_ref[0, 0] + s_ref[0]
        own = None
        for q in range(4):
            blk = p[:, SHARD_COLS * q:SHARD_COLS * (q + 1)]
            b_ref[q] = blk.astype(bf16)
            mine = jnp.where(sc_ref[1] == q, blk, 0.0)
            own = mine if own is None else own + mine
        o_ref[...] = own

    return pl.pallas_call(
        body, name="add_w_in",
        grid_spec=pltpu.PrefetchScalarGridSpec(
            num_scalar_prefetch=1, grid=(R // tr,),
            in_specs=[pl.BlockSpec((1, 1, tr, Cg), lambda i, sc: (0, sc[0], i, 0)),
                      pl.BlockSpec((1, tr, Cg), lambda i, sc: (0, i, 0))],
            out_specs=[pl.BlockSpec((4, tr, SHARD_COLS), lambda i, sc: (0, i, 0)),
                       pl.BlockSpec((tr, SHARD_COLS), lambda i, sc: (i, 0))]),
        out_shape=[jax.ShapeDtypeStruct((4, R, SHARD_COLS), bf16), jax.ShapeDtypeStruct((R, SHARD_COLS), f32)],
        compiler_params=_cp(("arbitrary",)),
    )(cm_arr, g, s)


def _add_halves(c_arr, g, s, name):
    Q, _, R, Cc = g.shape
    tr = min(R, 128)

    def body(c_ref, g_ref, s_ref, b_ref, f_ref):
        p = g_ref[0, 0] + s_ref[0]
        f_ref[0] = p
        b_ref[0] = p.astype(bf16)

    blk = pl.BlockSpec((1, tr, Cc), lambda q, i, cr: (q, i, 0))
    return pl.pallas_call(
        body, name=name,
        grid_spec=pltpu.PrefetchScalarGridSpec(
            num_scalar_prefetch=1, grid=(Q, R // tr),
            in_specs=[pl.BlockSpec((1, 1, tr, Cc), lambda q, i, cr: (q, cr[0], i, 0)), blk], out_specs=[blk, blk]),
        out_shape=[jax.ShapeDtypeStruct((Q, R, Cc), bf16), jax.ShapeDtypeStruct((Q, R, Cc), f32)],
        compiler_params=_cp(("arbitrary", "arbitrary")),
    )(c_arr, g, s)


_FLIPS = [(fx, fy, fc) for fx in (0, 1) for fy in (0, 1) for fc in (0, 1)][1:]


def _sum_blocks(own, r, name, transposed=False):
    R, Cc = own.shape
    tr = min(R, 256)

    def body(own_ref, r_ref, o_ref):
        acc = own_ref[...]
        for j in range(3):
            acc = acc + r_ref[j].astype(f32)
        o_ref[...] = acc.T if transposed else acc

    return pl.pallas_call(
        body, name=name, grid=(R // tr,),
        in_specs=[pl.BlockSpec((tr, Cc), lambda i: (i, 0)), pl.BlockSpec((3, tr, Cc), lambda i: (0, i, 0))],
        out_specs=pl.BlockSpec((Cc, tr), lambda i: (0, i)) if transposed else pl.BlockSpec((tr, Cc), lambda i: (i, 0)),
        out_shape=jax.ShapeDtypeStruct((Cc, R) if transposed else (R, Cc), f32),
        compiler_params=_cp(("arbitrary",)),
    )(own, r)


def _sum_packs(me8_arr, pack, rp):
    R = pack.shape[0]

    def body(me_ref, pk_ref, rp_ref, o_ref):
        me8 = me_ref[0]
        acc = None
        for d in range(8):
            rel = d ^ me8
            term = jnp.where(rel == 0, pk_ref[...], rp_ref[jnp.maximum(rel - 1, 0)])
            acc = term if acc is None else acc + term
        o_ref[...] = acc

    return pl.pallas_call(
        body, name="sum_packs",
        grid_spec=pltpu.PrefetchScalarGridSpec(
            num_scalar_prefetch=1, grid=(1,),
            in_specs=[pl.BlockSpec((R, 128), lambda i, mr: (0, 0)), pl.BlockSpec((7, R, 128), lambda i, mr: (0, 0, 0))],
            out_specs=pl.BlockSpec((R, 128), lambda i, mr: (0, 0))),
        out_shape=jax.ShapeDtypeStruct((R, 128), f32),
        compiler_params=_cp(("arbitrary",)),
    )(me8_arr, pack, rp)


def _swap_finished(fs, pack):
    na = len(fs)
    R = pack.shape[0]

    def body(*refs):
        srcs, pk = refs[:na], refs[na]
        dsts, rp = refs[na + 1:2 * na + 1], refs[2 * na + 1]
        send_sems, recv_sems = refs[2 * na + 2:]
        x, y, c, _ = _place()
        cps = [pltpu.make_async_remote_copy(
            src_ref=srcs[i], dst_ref=dsts[i], send_sem=send_sems.at[i], recv_sem=recv_sems.at[i],
            device_id=(x, y, 1 - c), device_id_type=MESH) for i in range(na)]
        cps += [pltpu.make_async_remote_copy(
            src_ref=pk, dst_ref=rp.at[k], send_sem=send_sems.at[na + k], recv_sem=recv_sems.at[na + k],
            device_id=(x ^ fx, y ^ fy, c ^ fc), device_id_type=MESH) for k, (fx, fy, fc) in enumerate(_FLIPS)]
        for cp in cps:
            cp.start()
        for cp in cps:
            cp.wait()

    return pl.pallas_call(
        body, name="swap_finished",
        in_specs=[ANY] * (na + 1), out_specs=[ANY] * (na + 1),
        out_shape=[jax.ShapeDtypeStruct(f.shape, f.dtype) for f in fs] + [jax.ShapeDtypeStruct((7, R, 128), f32)],
        scratch_shapes=[pltpu.SemaphoreType.DMA((na + 7,)), pltpu.SemaphoreType.DMA((na + 7,))],
        compiler_params=pltpu.CompilerParams(has_side_effects=True),
    )(*fs, pack)


def _adamw_halves(c_arr, mine, peer, w, m, v, name):
    _, R, Cc = w.shape
    tr = min(R, 256)

    def body(c_ref, mine_ref, peer_ref, w_ref, m_ref, v_ref, g_ref, d_ref, m2_ref, v2_ref):
        g = jnp.where(pl.program_id(0) == c_ref[0], mine_ref[...], peer_ref[...])
        delta, m2, v2 = _adam_math(g, w_ref[0], m_ref[0], v_ref[0])
        g_ref[0] = g
        d_ref[0] = delta
        m2_ref[0] = m2
        v2_ref[0] = v2

    half = pl.BlockSpec((tr, Cc), lambda hh, i, cr: (i, 0))
    full = pl.BlockSpec((1, tr, Cc), lambda hh, i, cr: (hh, i, 0))
    return pl.pallas_call(
        body, name=name,
        grid_spec=pltpu.PrefetchScalarGridSpec(
            num_scalar_prefetch=1, grid=(2, R // tr), in_specs=[half, half, full, full, full], out_specs=[full] * 4),
        out_shape=[jax.ShapeDtypeStruct((2, R, Cc), f32)] * 4,
        compiler_params=_cp(("arbitrary", "arbitrary")),
    )(c_arr, mine, peer, w, m, v)


def _rows8(a):
    flat = a.reshape(-1)
    n = flat.shape[0]
    rows = -(-n // 1024) * 8
    return jnp.pad(flat, (0, rows * 128 - n)).reshape(rows, 128)


def kernel(x, meta_tokens, norm_w, w_in, conv_w, hg_lb_logits, hg_norm_w, gdn_A_log, gdn_dt_bias, gdn_norm_w, w_out, final_norm_w, loss_target, m_meta_tokens, m_norm_w, m_w_in, m_conv_w, m_hg_lb_logits, m_hg_norm_w, m_gdn_A_log, m_gdn_dt_bias, m_gdn_norm_w, m_w_out, m_final_norm_w, v_meta_tokens, v_norm_w, v_w_in, v_conv_w, v_hg_lb_logits, v_hg_norm_w, v_gdn_A_log, v_gdn_dt_bias, v_gdn_norm_w, v_w_out, v_final_norm_w):
    me = 2 * lax.axis_index("x") + lax.axis_index("y")

    g_win, g_wout, g_conv, g_meta = _gather_weights(
        jnp.transpose(w_in, (2, 0, 1)), [w_out[0].astype(bf16).reshape(2, D // 8, D)], [conv_w[0], meta_tokens])
    wout_full = g_wout.reshape(D, D)
    conv4 = jnp.transpose(g_conv, (1, 0, 2)).reshape(4, 1, 3 * HD)
    meta_full = jnp.transpose(g_meta, (1, 0, 2)).reshape(N_META, D)

    c_arr = lax.axis_index("c").reshape(1).astype(jnp.int32)

    def chip_partials(gw, g_wout_part):
        g_in2 = gw.reshape(1, 2, D // 2, PC)
        g_out4 = g_wout_part.reshape(4, 2, D // 8, D)
        s_in, s_out = _swap_halves([g_in2, g_out4])
        pb_blocks, own_in = _add_split(jnp.concatenate([c_arr, me.reshape(1).astype(jnp.int32)]), g_in2, s_in)
        pb_out, pf_out = _add_halves(c_arr, g_out4, s_out, "add_w_out")
        own_out = lax.dynamic_index_in_dim(pf_out, me, axis=0, keepdims=False)
        return [pb_blocks, pb_out], [own_in, own_out]

    (loss8, grad_x, d_meta, d_nw, d_conv, d_lb, d_hgw, d_alog, d_dtb, d_gdw, d_fw, pfs, rs) = _local_step(
        x, loss_target, g_win, wout_full, conv4, meta_full, norm_w, hg_lb_logits, hg_norm_w, gdn_A_log, gdn_dt_bias,
        gdn_norm_w, final_norm_w, chip_partials)

    pack = jnp.concatenate([
        loss8, d_nw[0].reshape(8, 128), d_lb.reshape(8, 128), d_hgw, _rows8(d_alog[0, :H]), _rows8(d_dtb[0, :H]),
        d_gdw, d_fw[0].reshape(8, 128), d_meta.reshape(128, 128), d_conv.reshape(48, 128)], axis=0)
    return _reduce_and_update(
        me, c_arr, grad_x, pfs, rs, pack, meta_tokens, norm_w, w_in, conv_w, hg_lb_logits, hg_norm_w, gdn_A_log,
        gdn_dt_bias, gdn_norm_w, w_out, final_norm_w, m_meta_tokens, m_norm_w, m_w_in, m_conv_w, m_hg_lb_logits,
        m_hg_norm_w, m_gdn_A_log, m_gdn_dt_bias, m_gdn_norm_w, m_w_out, m_final_norm_w, v_meta_tokens, v_norm_w, v_w_in,
        v_conv_w, v_hg_lb_logits, v_hg_norm_w, v_gdn_A_log, v_gdn_dt_bias, v_gdn_norm_w, v_w_out, v_final_norm_w)


def _local_step(x, loss_target, w4, wout_full, conv4, meta_full, norm_w, hg_lb_logits, hg_norm_w, gdn_A_log, gdn_dt_bias,
                gdn_norm_w, final_norm_w, chip_partials):
    head = jnp.concatenate([jnp.zeros((PAD, D), f32), meta_full], axis=0)
    target = loss_target.reshape(NB * SEQ, D)
    l0, l1 = hg_lb_logits[0:1], hg_lb_logits[1:2]
    alog = jnp.pad(gdn_A_log, ((0, 0), (0, DK - H)))
    dtb = jnp.pad(gdn_dt_bias, ((0, 0), (0, DK - H)))
    fw = final_norm_w.reshape(1, D)

    hflat, proj, ut, cv2, wbig = _in_proj(x.reshape(NB * SEQ, D), head, norm_w, w4, conv4)
    proj3 = proj.reshape(NB, TP, PC)
    cv = cv2.reshape(NB, TP, 3 * HD)
    o_hg, s_hg, o_gd, s_gd, t_gd = _mix_fwd(proj3, cv, l0, l1, alog, dtb)
    (loss8, d_ohg, d_ogd, d_zhg, d_zgd, dh_res, g_wout_part, d_hgw, d_gdw, d_fw) = _out_loss(
        o_hg.reshape(N, HD), o_gd.reshape(N, HD), proj, hg_norm_w, gdn_norm_w, wout_full, hflat, fw, target)
    d_hg, d_l0, d_l1 = _hg_bwd(proj3, l0, l1, s_hg, d_ohg.reshape(NB, TP, HD))
    d_qkv, d_conv4, d_ab, d_alog, d_dtb = _gd_bwd(cv, proj3, conv4, alog, dtb, s_gd, t_gd,
                                                  d_ogd.reshape(NB, TP, HD))
    d_hg2, d_qkv2, d_ab2 = d_hg.reshape(N, 3 * HD), d_qkv.reshape(N, 3 * HD), d_ab.reshape(N, DK)
    pieces = [(d_hg2, COL_HG), (d_zhg, COL_ZHG), (d_qkv2, COL_QKV), (d_zgd, COL_ZGD), (d_ab2, COL_AB)]
    gw = _w_grad(ut, pieces)
    pbs, pfs = chip_partials(gw, g_wout_part) if chip_partials else ([], [gw, g_wout_part])
    dh, d_nw, *rs = _in_bwd(pieces, wbig, hflat, norm_w, dh_res, pbs)

    dh3 = dh.reshape(NB, TP, D)
    grad_x = dh3[:, PAD + N_META:, :]
    d_meta = jnp.sum(dh3[:, PAD:PAD + N_META, :], axis=0)
    d_conv = d_conv4[:, 0, :]
    d_lb = jnp.concatenate([d_l0[0:1], d_l1[0:1]], axis=0)
    return loss8, grad_x, d_meta, d_nw, d_conv, d_lb, d_hgw, d_alog, d_dtb, d_gdw, d_fw, pfs, rs


def _reduce_and_update(me, c_arr, grad_x, pfs, rs, pack, meta_tokens, norm_w, w_in, conv_w, hg_lb_logits, hg_norm_w,
                       gdn_A_log, gdn_dt_bias, gdn_norm_w, w_out, final_norm_w, m_meta_tokens, m_norm_w, m_w_in, m_conv_w,
                       m_hg_lb_logits, m_hg_norm_w, m_gdn_A_log, m_gdn_dt_bias, m_gdn_norm_w, m_w_out, m_final_norm_w,
                       v_meta_tokens, v_norm_w, v_w_in, v_conv_w, v_hg_lb_logits, v_hg_norm_w, v_gdn_A_log, v_gdn_dt_bias,
                       v_gdn_norm_w, v_w_out, v_final_norm_w):
    (own_in, own_out), (r_in, r_out) = pfs, rs
    f_in = _sum_blocks(own_in, r_in, "sum_w_in", transposed=True)
    f_out = _sum_blocks(own_out, r_out, "sum_w_out")
    o_in, o_out, r_pack = _swap_finished([f_in, f_out], pack)
    me8_arr = (2 * me + lax.axis_index("c")).reshape(1).astype(jnp.int32)
    small = _sum_packs(me8_arr, pack, r_pack)

    half_out = lambda a: a[0].reshape(2, D // 8, D)
    is0 = lax.axis_index("c") == 0
    g_in = jnp.concatenate([jnp.where(is0, f_in, o_in), jnp.where(is0, o_in, f_in)], axis=1)
    to_cm = lambda a: jnp.transpose(a, (2, 0, 1))
    gi, di, mi, vi = [jnp.transpose(a, (1, 2, 0))[0] for a in _adamw_rows(
        g_in.reshape(SHARD_COLS, 1, D), to_cm(w_in), to_cm(m_w_in), to_cm(v_w_in), "adamw_w_in")]
    go, do_, mo, vo = [a.reshape(D // 4, D) for a in _adamw_halves(
        c_arr, f_out, o_out, half_out(w_out), half_out(m_w_out), half_out(v_w_out), "adamw_w_out")]

    g_meta_full = small[64:192].reshape(N_META, D)
    g_meta_loc = lax.dynamic_slice(g_meta_full, (0, me * 256), (N_META, 256))
    gm, dm, mm_, vm = _adamw([g_meta_loc], meta_tokens, m_meta_tokens, v_meta_tokens, "adamw_meta")
    g_conv_full = small[192:240].reshape(4, 1536)
    g_conv_loc = lax.dynamic_slice(g_conv_full, (0, me * 384), (4, 384))
    gc, dc, mc, vc = _adamw([g_conv_loc], conv_w[0], m_conv_w[0], v_conv_w[0], "adamw_conv")

    reps = [(norm_w, m_norm_w, v_norm_w), (hg_lb_logits, m_hg_lb_logits, v_hg_lb_logits),
            (hg_norm_w, m_hg_norm_w, v_hg_norm_w), (gdn_A_log, m_gdn_A_log, v_gdn_A_log),
            (gdn_dt_bias, m_gdn_dt_bias, v_gdn_dt_bias), (gdn_norm_w, m_gdn_norm_w, v_gdn_norm_w),
            (final_norm_w, m_final_norm_w, v_final_norm_w)]
    wp = jnp.concatenate([_rows8(t[0]) for t in reps], axis=0)
    mp = jnp.concatenate([_rows8(t[1]) for t in reps], axis=0)
    vp = jnp.concatenate([_rows8(t[2]) for t in reps], axis=0)
    gr, dr, mr, vr = _adamw([small[8:64]], wp, mp, vp, "adamw_small")

    def unpack(p):
        outs = []
        for i, t in enumerate(reps):
            n = t[0].size
            outs.append(p[8 * i:8 * i + 8].reshape(-1)[:n].reshape(t[0].shape))
        return outs

    def leaves(meta_v, conv_v, in_v, out_v, rep_p):
        nw, lb, hgw, al, db, gdw, fwv = unpack(rep_p)
        return [meta_v, nw, in_v[None], conv_v[None], lb, hgw, al, db, gdw, out_v[None], fwv]

    loss = small[0, 0]
    return (loss, grad_x, *leaves(gm, gc, gi, go, gr), *leaves(dm, dc, di, do_, dr),
            *leaves(mm_, mc, mi, mo, mr), *leaves(vm, vc, vi, vo, vr))
```

```python
import functools

import jax
import jax.numpy as jnp
from jax import lax
from jax.experimental import pallas as pl
from jax.experimental.pallas import tpu as pltpu

f32 = jnp.float32
bf16 = jnp.bfloat16
MESH = pl.DeviceIdType.MESH
ANY = pl.BlockSpec(memory_space=pl.ANY)

D = 1024
NB = 2
N_META = 16
SEQ = 2048
PAD = 48
TP = PAD + N_META + SEQ
C = 64
NCH = TP // C
N = NB * TP
H = 4
DK = 128
HD = H * DK
PC = 4224
IN_COLS = 4104
SHARD_COLS = IN_COLS // 4
COL_HG, COL_ZHG, COL_QKV, COL_ZGD, COL_AB = 0, 3 * HD, 4 * HD, 7 * HD, 8 * HD
EPS = 1e-6
ADAM_LR, ADAM_B1, ADAM_B2, ADAM_EPS, ADAM_WD, ADAM_STEP = 0.001, 0.9, 0.999, 1e-08, 0.01, 10
VMEM_LIMIT = 56 * 1024 * 1024

P_HG = dict(lvl=1, av=1, qs=1, su=1)
P_GD = dict(kk=1, inv=1, sol=1, ws=1, qk=1, o=1, su=1)


def _cp(sem=None, **kw):
    return pltpu.CompilerParams(dimension_semantics=sem, vmem_limit_bytes=VMEM_LIMIT, **kw)


_DIMS = {"nn": (((1,), (0,)), ((), ())), "nt": (((1,), (1,)), ((), ())), "tn": (((0,), (0,)), ((), ()))}


def _split(x):
    hi = x.astype(bf16)
    return hi, (x - hi.astype(f32)).astype(bf16)


def _dg(a, b, kind, passes):
    d = lambda x, y: lax.dot_general(x, y, _DIMS[kind], preferred_element_type=f32)
    if passes == 1:
        return d(a.astype(bf16), b.astype(bf16))
    ah, al = _split(a)
    bh, bl = _split(b)
    return d(ah, bh) + d(ah, bl) + d(al, bh)


@functools.partial(jax.custom_vjp, nondiff_argnums=(2, 3))
def mmx(a, b, kind, passes):
    return _dg(a, b, kind, passes)


def _mmx_fwd(a, b, kind, passes):
    return _dg(a, b, kind, passes), (a, b)


def _mmx_bwd(kind, passes, res, g):
    a, b = res
    if kind == "nn":
        return _dg(g, b, "nt", passes), _dg(a, g, "tn", passes)
    if kind == "nt":
        return _dg(g, b, "nn", passes), _dg(g, a, "tn", passes)
    return _dg(b, g, "nt", passes), _dg(a, g, "nn", passes)


mmx.defvjp(_mmx_fwd, _mmx_bwd)


def _mask_dg(mask, x):
    xh, xl = _split(x)
    return jnp.dot(jnp.concatenate([mask, mask], axis=1), jnp.concatenate([xh, xl], axis=0), preferred_element_type=f32)


@functools.partial(jax.custom_vjp, nondiff_argnums=(2,))
def mask_mm(mask, x, bwd_passes):
    return _mask_dg(mask, x)


def _mask_fwd(mask, x, bwd_passes):
    return _mask_dg(mask, x), mask


def _mask_bwd(bwd_passes, mask, g):
    d = lambda y: lax.dot_general(mask, y, _DIMS["tn"], preferred_element_type=f32)
    if bwd_passes == 1:
        return None, d(g.astype(bf16))
    gh, gl = _split(g)
    return None, d(gh) + d(gl)


mask_mm.defvjp(_mask_fwd, _mask_bwd)


def bdot(a, b):
    return jnp.dot(a.astype(bf16), b.astype(bf16), preferred_element_type=f32)


def bdot_nt(a, b):
    return lax.dot_general(a.astype(bf16), b.astype(bf16), _DIMS["nt"], preferred_element_type=f32)


def bdot_tn(a, b):
    return lax.dot_general(a.astype(bf16), b.astype(bf16), _DIMS["tn"], preferred_element_type=f32)


def _iota2(n, m):
    return lax.broadcasted_iota(jnp.int32, (n, m), 0), lax.broadcasted_iota(jnp.int32, (n, m), 1)


sigmoid = jax.nn.sigmoid


def silu(x):
    return x * sigmoid(x)


def softplus(x):
    return jnp.maximum(x, 0.0) + jnp.log(1.0 + jnp.exp(-jnp.abs(x)))


def rmsnorm(x, w):
    return x * lax.rsqrt(jnp.mean(x * x, axis=-1, keepdims=True) + EPS) * w


def hg_masks():
    t, r = _iota2(C, C)
    mats = [r <= t, r > t]
    lvl = []
    for l in range(1, 7):
        sz = 1 << l
        half = sz >> 1
        seg_t = t >> l
        upper_t = (t & (sz - 1)) >= half
        mid_t = seg_t * sz + half - 1
        mats.append((upper_t & (r > mid_t) & (r <= t)) | ((~upper_t) & (r > t) & (r <= mid_t)))
        lvl.append(((seg_t == (r >> l)) & upper_t & ((r & (sz - 1)) < half)).astype(f32))
    stk = jnp.concatenate([m.astype(bf16) for m in mats], axis=0)
    return stk, lvl, (t == r).astype(f32)


def _head(a, h):
    return a[:, h * DK:(h + 1) * DK]


def _run(*gens):
    results = [None] * len(gens)
    live = list(range(len(gens)))
    while live:
        for i in list(live):
            try:
                next(gens[i])
            except StopIteration as e:
                results[i] = e.value
                live.remove(i)
    return results


def hg_chunk(St, ps, l0, l1):
    return _run(hg_stages(St, ps, l0, l1))[0]


def gd_chunk(S, cs, abs_, alog, dtb, t_saved=None):
    return _run(gd_stages(S, cs, abs_, alog, dtb, t_saved))[0]


def mix_chunk(St, ps, l0, l1, S, cs, abs_, alog, dtb):
    (sn_h, o_h), (sn_g, o_g, t_pack) = _run(hg_stages(St, ps, l0, l1), gd_stages(S, cs, abs_, alog, dtb))
    return sn_h, o_h, sn_g, o_g, t_pack


def hg_stages(St, ps, l0, l1):
    m = jnp.maximum(l0, l1)
    e0 = jnp.exp(l0 - m)
    e1 = jnp.exp(l1 - m)
    lb = e0 / (e0 + e1)
    stk, lvl, eye = hg_masks()
    msk = [eye] + lvl
    trow = lax.broadcasted_iota(jnp.int32, (C, 1), 0)
    upper = [(trow & ((1 << l) - 1)) >= (1 << (l - 1)) for l in range(1, 7)]
    qs, ks, vs, qG, kR, eGl = [], [], [], [], [], []
    for p in ps:
        pq, pf, v = p[:, 0:HD], p[:, HD:2 * HD], p[:, 2 * HD:3 * HD]
        q = silu(pq)
        f = lb + (1.0 - lb) * sigmoid(pf)
        k = 1.0 - f
        logf = jnp.log(f)
        Dm = mask_mm(stk, logf, 1)
        z = [jnp.where(up, q, k) * jnp.exp(Dm[(2 + i) * C:(3 + i) * C]) for i, up in enumerate(upper)]
        qs.append([q] + z)
        ks.append([k] + z)
        vs.append(v)
        qG.append(q * jnp.exp(Dm[0:C]))
        kR.append(k * jnp.exp(Dm[C:2 * C]))
        eGl.append(jnp.exp(jnp.sum(logf, axis=0, keepdims=True)))
    yield
    units = [(b, h) for b in range(len(ps)) for h in range(H)]
    parts = []
    for i in range(7):
        parts.append([msk[i] * mmx(_head(qs[b][i], h), _head(ks[b][i], h), "nt", P_HG["lvl"]) for b, h in units])
        yield
    A = [functools.reduce(lambda x, y: x + y, [parts[i][n] for i in range(7)]) for n in range(len(units))]
    qS = [mmx(_head(qG[b], h), St[n], "nt", P_HG["qs"]) for n, (b, h) in enumerate(units)]
    Sn = [St[n] * _head(eGl[b], h) + mmx(_head(vs[b], h), _head(kR[b], h), "tn", P_HG["su"])
          for n, (b, h) in enumerate(units)]
    yield
    outs = [mmx(A[n], _head(vs[b], h), "nn", P_HG["av"]) + qS[n] for n, (b, h) in enumerate(units)]
    return tuple(Sn), tuple(jnp.concatenate(outs[b * H:(b + 1) * H], axis=1) for b in range(len(ps)))


@jax.custom_vjp
def use_inverse(A, T):
    return T


def _use_inverse_fwd(A, T):
    return T, T


def _use_inverse_bwd(T, g):
    return -_dg(T, _dg(g, T, "nt", P_GD["inv"]), "tn", P_GD["inv"]), jnp.zeros_like(T)


use_inverse.defvjp(_use_inverse_fwd, _use_inverse_bwd)


def gd_stages(S, cs, abs_, alog, dtb, t_saved=None):
    t, r = _iota2(C, C)
    tri = (r <= t).astype(bf16)
    ups = (r > t).astype(bf16)
    lane = lax.broadcasted_iota(jnp.int32, (1, DK), 1)
    subl = lax.broadcasted_iota(jnp.int32, (8, 1), 0)
    eye = (t == r).astype(f32)
    strict = (r < t).astype(f32)
    bd = ((t >> 4) == (r >> 4)).astype(f32)
    qa, ka, va, b4, gam4, grev4, gam4T, glast4 = [], [], [], [], [], [], [], []
    for c, ab in zip(cs, abs_):
        qa.append(silu(c[:, 0:HD]))
        ka.append(silu(c[:, HD:2 * HD]))
        va.append(silu(c[:, 2 * HD:3 * HD]))
        g4 = -jnp.exp(alog) * softplus(ab + dtb)
        b4.append(sigmoid(ab))
        gam4.append(mask_mm(tri, g4, 2))
        grev4.append(mask_mm(ups, g4, 2))
        gam4T.append(gam4[-1].T)
        glast4.append(jnp.sum(g4, axis=0, keepdims=True))
    yield
    units = [(b, h) for b in range(len(cs)) for h in range(H)]
    nu = range(len(units))
    inv = lambda a, b: [mmx(a[n], b[n], "nn", P_GD["inv"]) for n in nu]
    v = [_head(va[b], h) for b, h in units]
    q = [_head(qa[b], h) for b, h in units]
    k = [_head(ka[b], h) for b, h in units]
    q = [x * lax.rsqrt(jnp.sum(x * x, -1, keepdims=True) + EPS) * (DK ** -0.5) for x in q]
    k = [x * lax.rsqrt(jnp.sum(x * x, -1, keepdims=True) + EPS) for x in k]
    oh = [(lane == h).astype(f32) for h in range(H)]
    gam_c = [jnp.sum(gam4[b] * oh[h], -1, keepdims=True) for b, h in units]
    grev_c = [jnp.sum(grev4[b] * oh[h], -1, keepdims=True) for b, h in units]
    beta = [jnp.sum(b4[b] * (lane == h + H).astype(f32), -1, keepdims=True) for b, h in units]
    glast = [jnp.sum(glast4[b] * oh[h], -1, keepdims=True) for b, h in units]
    gam_r = [jnp.sum(gam4T[b][0:8, :] * (subl == h).astype(f32), axis=0, keepdims=True) for b, h in units]
    dec = [jnp.exp(jnp.where(r <= t, gam_c[n] - gam_r[n], -1e30)) for n in nu]
    egam = [jnp.exp(gam_c[n]) for n in nu]
    kk = [mmx(k[n], k[n], "nt", P_GD["kk"]) for n in nu]
    qk = [mmx(q[n], k[n], "nt", P_GD["qk"]) * dec[n] for n in nu]
    yield
    A = [beta[n] * kk[n] * dec[n] * strict for n in nu]
    Dg = [A[n] * bd for n in nu]
    L = [A[n] - Dg[n] for n in nu]
    if t_saved is None:
        ImD = [eye - Dg[n] for n in nu]
        D2 = inv(Dg, Dg)
        yield
        P1 = inv(ImD, [eye + x for x in D2])
        D4 = inv(D2, D2)
        yield
        P2 = inv(P1, [eye + x for x in D4])
        D8 = inv(D4, D4)
        yield
        M = inv(P2, [eye + x for x in D8])
        yield
        Nn = inv(M, L)
        yield
        N2 = inv(Nn, Nn)
        yield
        T1 = inv([eye - x for x in Nn], [eye + x for x in N2])
        yield
        Tinv = inv(T1, M)
        yield
    else:
        Tinv = [use_inverse(A[n], t_saved[b][:, h * DK:h * DK + C]) for n, (b, h) in enumerate(units)]
    rhs = [jnp.concatenate([beta[n] * v[n], (beta[n] * egam[n]) * k[n]], axis=1) for n in nu]
    sol = [mmx(Tinv[n], rhs[n], "nn", P_GD["sol"]) for n in nu]
    yield
    qwS = [mmx(jnp.concatenate([q[n] * egam[n], sol[n][:, DK:2 * DK]], axis=0), S[n], "nn", P_GD["ws"]) for n in nu]
    yield
    u = [sol[n][:, 0:DK] - qwS[n][C:2 * C] for n in nu]
    outs = [qwS[n][0:C] + mmx(qk[n], u[n], "nn", P_GD["o"]) for n in nu]
    Sn = [jnp.exp(glast[n]) * S[n] + mmx(k[n] * jnp.exp(grev_c[n]), u[n], "tn", P_GD["su"]) for n in nu]
    zpad = jnp.zeros((C, DK - C), f32)
    t_pack = tuple(jnp.concatenate([x for n in range(b * H, (b + 1) * H) for x in (lax.stop_gradient(Tinv[n]), zpad)],
                                   axis=1) for b in range(len(cs)))
    return tuple(Sn), tuple(jnp.concatenate(outs[b * H:(b + 1) * H], axis=1) for b in range(len(cs))), t_pack


def _chunk_index(tile_chunks, k):
    def index(i):
        chunk = tile_chunks * i + k
        b = chunk // NCH
        return jnp.maximum((SEQ // C) * b + chunk - NCH * b - 1, 0), 0
    return index


def _in_proj(xflat, head, norm_w, w4, conv4):
    tm = 384
    nck = tm // C
    W3 = 3 * HD

    def body(*refs):
        x_refs = refs[:nck]
        head_ref, nw_ref, w4_ref, cw_ref, h_ref, p_ref, ut_ref, cv_ref, w_ref, prev = refs[nck:]
        i = pl.program_id(0)

        @pl.when(i == 0)
        def _():
            prev[...] = jnp.zeros_like(prev)
            w_ref[PC - DK:PC, :] = jnp.zeros((DK, D), bf16)
            for q in range(4):
                w_ref[SHARD_COLS * q:SHARD_COLS * (q + 1), :] = w4_ref[q]

        blocks = []
        for k in range(nck):
            chunk = nck * i + k
            blocks.append(jnp.where(chunk - NCH * (chunk // NCH) == 0, head_ref[...], x_refs[k][...]))
        hval = jnp.concatenate(blocks, axis=0)
        h_ref[...] = hval
        u = rmsnorm(hval, nw_ref[...])
        ut_ref[...] = u.T.astype(bf16)
        p = bdot_nt(u, w_ref[...])
        p_ref[...] = p
        x = p[:, COL_QKV:COL_QKV + W3]
        xx = jnp.concatenate([prev[...], x], axis=0)
        y = cw_ref[3] * x
        for s in (1, 2, 3):
            y = y + cw_ref[3 - s] * pltpu.roll(xx, s, 0)[8:]
        row = i * tm + lax.broadcasted_iota(jnp.int32, (tm, 1), 0)
        tok = jnp.where(row >= TP, row - TP, row)
        cv_ref[...] = jnp.where(tok >= 8, y, 0.0)
        prev[...] = x[tm - 8:tm]

    return pl.pallas_call(
        body, name="in_proj", grid=(N // tm,),
        in_specs=[pl.BlockSpec((C, D), _chunk_index(nck, k)) for k in range(nck)]
        + [pl.BlockSpec((C, D), lambda i: (0, 0)), pl.BlockSpec((1, D), lambda i: (0, 0)),
           pl.BlockSpec((4, SHARD_COLS, D), lambda i: (0, 0, 0), pipeline_mode=pl.Buffered(1)),
           pl.BlockSpec((4, 1, W3), lambda i: (0, 0, 0))],
        out_specs=[pl.BlockSpec((tm, D), lambda i: (i, 0)), pl.BlockSpec((tm, PC), lambda i: (i, 0)),
                   pl.BlockSpec((D, tm), lambda i: (0, i)), pl.BlockSpec((tm, W3), lambda i: (i, 0)),
                   pl.BlockSpec((PC, D), lambda i: (0, 0), pipeline_mode=pl.Buffered(1))],
        out_shape=[jax.ShapeDtypeStruct((N, D), f32), jax.ShapeDtypeStruct((N, PC), f32),
                   jax.ShapeDtypeStruct((D, N), bf16), jax.ShapeDtypeStruct((N, W3), f32),
                   jax.ShapeDtypeStruct((PC, D), bf16)],
        scratch_shapes=[pltpu.VMEM((8, W3), f32)],
        compiler_params=_cp(("arbitrary",)),
    )(*[xflat] * nck, head, norm_w, w4, conv4)


NU = NB * H
_REV = lambda c: NCH - 1 - c
_FWD = lambda c: c


def _tok_spec(w, ix, col=0):
    return pl.BlockSpec((NB, C, w), lambda c: (0, ix(c), col))


def _state_spec(ix):
    return pl.BlockSpec((NB, 1, H, DK, DK), lambda c: (0, ix(c), 0, 0, 0))


def _row_spec(w):
    return pl.BlockSpec((1, w), lambda c: (0, 0))


def _rows(ref):
    return tuple(ref[b] for b in range(NB))


def _hg_extra_specs(ix):
    return [_row_spec(HD), _row_spec(HD)]


def _gd_extra_specs(ix):
    return [_tok_spec(DK, ix, COL_AB // DK), _row_spec(DK), _row_spec(DK)]


def _mix_fwd(proj3, cv, l0, l1, alog, dtb):
    def body(p_ref, c_ref, ab_ref, l0_ref, l1_ref, al_ref, db_ref, oh_ref, sh_ref, og_ref, sg_ref, t_ref, sth, stg):
        @pl.when(pl.program_id(0) == 0)
        def _():
            sth[...] = jnp.zeros_like(sth)
            stg[...] = jnp.zeros_like(stg)

        Sh = tuple(sth[n] for n in range(NU))
        Sg = tuple(stg[n] for n in range(NU))
        for n in range(NU):
            sh_ref[n // H, 0, n % H] = Sh[n]
            sg_ref[n // H, 0, n % H] = Sg[n]
        snh, oh, sng, og, tp = mix_chunk(Sh, _rows(p_ref), l0_ref[...], l1_ref[...],
                                         Sg, _rows(c_ref), _rows(ab_ref), al_ref[...], db_ref[...])
        for n in range(NU):
            sth[n] = snh[n]
            stg[n] = sng[n]
        for b in range(NB):
            oh_ref[b] = oh[b]
            og_ref[b] = og[b]
            t_ref[b] = tp[b]

    tok = jax.ShapeDtypeStruct((NB, TP, HD), f32)
    st = jax.ShapeDtypeStruct((NB, NCH, H, DK, DK), f32)
    return pl.pallas_call(
        body, name="mix_fwd", grid=(NCH,),
        in_specs=[_tok_spec(3 * HD, _FWD), _tok_spec(3 * HD, _FWD), _tok_spec(DK, _FWD, COL_AB // DK),
                  _row_spec(HD), _row_spec(HD), _row_spec(DK), _row_spec(DK)],
        out_specs=[_tok_spec(HD, _FWD), _state_spec(_FWD), _tok_spec(HD, _FWD), _state_spec(_FWD), _tok_spec(HD, _FWD)],
        out_shape=[tok, st, tok, st, tok],
        scratch_shapes=[pltpu.VMEM((NU, DK, DK), f32), pltpu.VMEM((NU, DK, DK), f32)],
        compiler_params=_cp(("arbitrary",)),
    )(proj3, cv, proj3, l0, l1, alog, dtb)


def _hg_bwd(proj3, l0, l1, s_saved, do):
    def body(p_ref, l0_ref, l1_ref, s_ref, do_ref, dp_ref, dl0_ref, dl1_ref, dst):
        @pl.when(pl.program_id(0) == 0)
        def _():
            dst[...] = jnp.zeros_like(dst)
            dl0_ref[...] = jnp.zeros_like(dl0_ref)
            dl1_ref[...] = jnp.zeros_like(dl1_ref)

        S = tuple(s_ref[n // H, 0, n % H] for n in range(NU))
        _, vjp = jax.vjp(hg_chunk, S, _rows(p_ref), l0_ref[...], l1_ref[...])
        dS, dp, dl0, dl1 = vjp((tuple(dst[n] for n in range(NU)), _rows(do_ref)))
        for n in range(NU):
            dst[n] = dS[n]
        for b in range(NB):
            dp_ref[b] = dp[b].astype(bf16)
        dl0_ref[...] += jnp.broadcast_to(dl0, (8, HD))
        dl1_ref[...] += jnp.broadcast_to(dl1, (8, HD))

    acc = pl.BlockSpec((8, HD), lambda c: (0, 0))
    return pl.pallas_call(
        body, name="hg_bwd", grid=(NCH,),
        in_specs=[_tok_spec(3 * HD, _REV)] + _hg_extra_specs(_REV) + [_state_spec(_REV), _tok_spec(HD, _REV)],
        out_specs=[_tok_spec(3 * HD, _REV), acc, acc],
        out_shape=[jax.ShapeDtypeStruct((NB, TP, 3 * HD), bf16), jax.ShapeDtypeStruct((8, HD), f32),
                   jax.ShapeDtypeStruct((8, HD), f32)],
        scratch_shapes=[pltpu.VMEM((NU, DK, DK), f32)],
        compiler_params=_cp(("arbitrary",)),
    )(proj3, l0, l1, s_saved, do)


def _gd_bwd(cv, proj3, conv4, alog, dtb, s_saved, t_saved, do):
    def body(c_ref, ab_ref, al_ref, db_ref, s_ref, t_ref, do_ref, x0_ref, x1_ref, x2_ref, w_ref,
             dx_ref, dw_ref, dab_ref, dal_ref, ddb_ref, dst, nxt):
        @pl.when(pl.program_id(0) == 0)
        def _():
            dst[...] = jnp.zeros_like(dst)
            dal_ref[...] = jnp.zeros_like(dal_ref)
            ddb_ref[...] = jnp.zeros_like(ddb_ref)
            dw_ref[...] = jnp.zeros_like(dw_ref)
            nxt[...] = jnp.zeros_like(nxt)

        S = tuple(s_ref[n // H, 0, n % H] for n in range(NU))
        t_rows = _rows(t_ref)
        fn = lambda *a: gd_chunk(*a, t_saved=t_rows)[0:2]
        _, vjp = jax.vjp(fn, S, _rows(c_ref), _rows(ab_ref), al_ref[...], db_ref[...])
        dS, dc, dab, dal, ddb = vjp((tuple(dst[n] for n in range(NU)), _rows(do_ref)))
        for n in range(NU):
            dst[n] = dS[n]
        for b in range(NB):
            dab_ref[b] = dab[b].astype(bf16)
        dal_ref[...] += jnp.broadcast_to(dal, (8, DK))
        ddb_ref[...] += jnp.broadcast_to(ddb, (8, DK))

        rows8 = lambda p: sum(p[8 * i:8 * i + 8] for i in range(1, C // 8)) + p[0:8]
        for j, x_ref in enumerate((x0_ref, x1_ref, x2_ref)):
            sl = slice(j * HD, (j + 1) * HD)
            dws = [None] * 4
            for b in range(NB):
                x = x_ref[b]
                g = dc[b][:, sl]
                gg = jnp.concatenate([g, nxt[b, :, sl]], axis=0)
                dx = w_ref[3, :, sl] * g
                parts = [x * g]
                for s in (1, 2, 3):
                    gs = pltpu.roll(gg, C + 8 - s, 0)[:C]
                    dx = dx + w_ref[3 - s, :, sl] * gs
                    parts.append(x * gs)
                nxt[b, :, sl] = g[:8]
                dx_ref[b, :, sl] = dx.astype(bf16)
                dws = [p if a is None else a + p for a, p in zip(dws, parts)]
            for s in range(4):
                dw_ref[3 - s, :, sl] += rows8(dws[s])

        @pl.when(pl.program_id(0) == NCH - 1)
        def _():
            for s in range(4):
                dw_ref[s] = jnp.broadcast_to(jnp.sum(dw_ref[s], axis=0, keepdims=True), (8, 3 * HD))

    acc = pl.BlockSpec((8, DK), lambda c: (0, 0))
    return pl.pallas_call(
        body, name="gd_bwd", grid=(NCH,),
        in_specs=[_tok_spec(3 * HD, _REV)] + _gd_extra_specs(_REV)
        + [_state_spec(_REV), _tok_spec(HD, _REV), _tok_spec(HD, _REV)]
        + [_tok_spec(HD, _REV, COL_QKV // HD + j) for j in range(3)]
        + [pl.BlockSpec((4, 1, 3 * HD), lambda c: (0, 0, 0))],
        out_specs=[_tok_spec(3 * HD, _REV), pl.BlockSpec((4, 8, 3 * HD), lambda c: (0, 0, 0)),
                   _tok_spec(DK, _REV), acc, acc],
        out_shape=[jax.ShapeDtypeStruct((NB, TP, 3 * HD), bf16), jax.ShapeDtypeStruct((4, 8, 3 * HD), f32),
                   jax.ShapeDtypeStruct((NB, TP, DK), bf16),
                   jax.ShapeDtypeStruct((8, DK), f32), jax.ShapeDtypeStruct((8, DK), f32)],
        scratch_shapes=[pltpu.VMEM((NU, DK, DK), f32), pltpu.VMEM((NB, 8, 3 * HD), f32)],
        compiler_params=_cp(("arbitrary",)),
    )(cv, proj3, alog, dtb, s_saved, t_saved, do, proj3, proj3, proj3, conv4)


def _out_loss(o_hg, o_gd, proj, hgw, gdw, wout, hflat, fw, target):
    tm = 384

    def body(ohg_ref, ogd_ref, zhg_ref, zgd_ref, hgw_ref, gdw_ref, wo_ref, h_ref, fw_ref, *refs):
        tg_refs = refs[:tm // C]
        (loss_ref, dohg_ref, dogd_ref, dzhg_ref, dzgd_ref, dh_ref, dwo_ref, dhgw_ref, dgdw_ref, dfw_ref) = refs[tm // C:]
        i = pl.program_id(0)

        @pl.when(i == 0)
        def _():
            for r in (loss_ref, dwo_ref, dhgw_ref, dgdw_ref, dfw_ref):
                r[...] = jnp.zeros_like(r)

        row = i * tm + lax.broadcasted_iota(jnp.int32, (tm, 1), 0)
        tok = jnp.where(row >= TP, row - TP, row)
        valid = (tok >= PAD + N_META).astype(f32)
        hval = h_ref[...]
        tgt = jnp.concatenate([r[...] for r in tg_refs], axis=0)

        mixers = ((ohg_ref, zhg_ref, hgw_ref[...]), (ogd_ref, zgd_ref, gdw_ref[...]))
        saved, ys = [], []
        for o_ref, z_ref, w in mixers:
            for hh in range(H):
                sl = slice(hh * DK, (hh + 1) * DK)
                o, z = o_ref[:, sl], z_ref[:, sl]
                r = lax.rsqrt(jnp.mean(o * o, axis=-1, keepdims=True) + EPS)
                n = o * r
                sg = sigmoid(z)
                ws = w * (z * sg)
                saved.append((r, n, sg, z, ws, w))
                ys.append(n * ws)
        y = jnp.concatenate(ys, axis=-1)
        h2 = hval + bdot(y, wo_ref[...])
        r2 = lax.rsqrt(jnp.mean(h2 * h2, axis=-1, keepdims=True) + EPS)
        n2 = h2 * r2
        fwv = fw_ref[...]
        err = (n2 * fwv - tgt) * valid
        loss = (0.5 / D) * jnp.sum(err * err)
        dyf = err * (1.0 / D)
        dn2 = dyf * fwv
        dout = r2 * (dn2 - n2 * jnp.mean(dn2 * n2, axis=-1, keepdims=True))
        dh_ref[...] = dout
        dy = bdot_nt(dout, wo_ref[...])
        dwo_ref[...] += bdot_tn(y, dout)
        dws = []
        for mi, (do_ref, dz_ref) in enumerate(((dohg_ref, dzhg_ref), (dogd_ref, dzgd_ref))):
            dw = jnp.zeros((1, DK), f32)
            for hh in range(H):
                sl = slice(hh * DK, (hh + 1) * DK)
                r, n, sg, z, ws, w = saved[mi * H + hh]
                dyh = dy[:, mi * HD + hh * DK:mi * HD + (hh + 1) * DK]
                t = dyh * n
                dw = dw + jnp.sum(t * (z * sg), axis=0, keepdims=True)
                dz_ref[:, sl] = (t * w * (sg * (1.0 + z * (1.0 - sg)))).astype(bf16)
                dn = dyh * ws
                do_ref[:, sl] = r * (dn - n * jnp.mean(dn * n, axis=-1, keepdims=True))
            dws.append(dw)
        loss_ref[...] += jnp.broadcast_to(loss, (8, DK))
        dhgw_ref[...] += jnp.broadcast_to(dws[0], (8, DK))
        dgdw_ref[...] += jnp.broadcast_to(dws[1], (8, DK))
        dfw_ref[...] += jnp.broadcast_to(jnp.sum(dyf * n2, axis=0, keepdims=True), (8, D))

    row = lambda w: pl.BlockSpec((tm, w), lambda i: (i, 0))
    whole = lambda r, w: pl.BlockSpec((r, w), lambda i: (0, 0))
    col = lambda c0: pl.BlockSpec((tm, HD), lambda i: (i, c0 // HD))

    tgt_spec = lambda k: pl.BlockSpec((C, D), _chunk_index(tm // C, k))
    return pl.pallas_call(
        body, name="out_loss", grid=(N // tm,),
        in_specs=[row(HD), row(HD), col(COL_ZHG), col(COL_ZGD),
                  whole(1, DK), whole(1, DK), whole(D, D), row(D), whole(1, D)] + [tgt_spec(k) for k in range(tm // C)],
        out_specs=[whole(8, DK), row(HD), row(HD), row(HD), row(HD), row(D), whole(D, D),
                   whole(8, DK), whole(8, DK), whole(8, D)],
        out_shape=[jax.ShapeDtypeStruct((8, DK), f32)] + [jax.ShapeDtypeStruct((N, HD), f32)] * 2
        + [jax.ShapeDtypeStruct((N, HD), bf16)] * 2
        + [jax.ShapeDtypeStruct((N, D), f32), jax.ShapeDtypeStruct((D, D), f32),
           jax.ShapeDtypeStruct((8, DK), f32), jax.ShapeDtypeStruct((8, DK), f32), jax.ShapeDtypeStruct((8, D), f32)],
        compiler_params=_cp(("arbitrary",)),
    )(o_hg, o_gd, proj, proj, hgw, gdw, wout, hflat, fw, *[target] * (tm // C))


def _in_bwd(pieces, wbig, hflat, norm_w, dh_res, pbs):
    tm = 384
    nsteps = N // tm
    np_ = len(pieces)
    na = len(pbs)
    offs = [c0 for _, c0 in pieces]
    widths = [d.shape[1] for d, _ in pieces]

    def body(*refs):
        d_refs = refs[:np_]
        w_ref, h_ref, nw_ref, dhr_ref = refs[np_:np_ + 4]
        srcs = refs[np_ + 4:np_ + 4 + na]
        dh_ref, dnw_ref = refs[np_ + 4 + na:np_ + 6 + na]
        dsts = refs[np_ + 6 + na:np_ + 6 + 2 * na]
        sems = refs[np_ + 6 + 2 * na:]
        i = pl.program_id(0)

        def copies():
            if not na:
                return []
            x, y, c, chips = _place()
            return [pltpu.make_async_remote_copy(
                src_ref=srcs[a].at[2 * px + py], dst_ref=dsts[a].at[j], send_sem=sems[0].at[na * j + a],
                recv_sem=sems[1].at[na * j + a], device_id=(px, py, c), device_id_type=MESH)
                for j, (px, py) in enumerate(chips) for a in range(na)]

        @pl.when(i == 0)
        def _():
            dnw_ref[...] = jnp.zeros_like(dnw_ref)
            for cp in copies():
                cp.start()

        du = jnp.zeros((tm, D), f32)
        for d_ref, off, wd in zip(d_refs, offs, widths):
            du = du + bdot(d_ref[...], w_ref[off:off + wd, :])
        _, vjp = jax.vjp(rmsnorm, h_ref[...], nw_ref[...])
        dh, dnw = vjp(du)
        dh_ref[...] = dh + dhr_ref[...]
        dnw_ref[...] += jnp.broadcast_to(dnw, (8, D))

        @pl.when(i == nsteps - 1)
        def _():
            for cp in copies():
                cp.wait()

    row = lambda w: pl.BlockSpec((tm, w), lambda i: (i, 0))
    return pl.pallas_call(
        body, name="in_bwd", grid=(nsteps,),
        in_specs=[row(w) for w in widths]
        + [pl.BlockSpec((PC, D), lambda i: (0, 0)), row(D), pl.BlockSpec((1, D), lambda i: (0, 0)), row(D)] + [ANY] * na,
        out_specs=[row(D), pl.BlockSpec((8, D), lambda i: (0, 0))] + [ANY] * na,
        out_shape=[jax.ShapeDtypeStruct((N, D), f32), jax.ShapeDtypeStruct((8, D), f32)]
        + [jax.ShapeDtypeStruct((3,) + p.shape[1:], p.dtype) for p in pbs],
        scratch_shapes=[pltpu.SemaphoreType.DMA((3 * na,)), pltpu.SemaphoreType.DMA((3 * na,))] if na else [],
        compiler_params=_cp(("arbitrary",)),
    )(*[d for d, _ in pieces], wbig, hflat, norm_w, dh_res, *pbs)


def _w_grad(ut, pieces):
    bw = HD
    steps = PC // bw + (PC % bw > 0)
    plan = []
    for d, c0 in pieces:
        w = d.shape[1]
        plan.append((c0 // bw, max(1, w // bw), min(w, bw)))
    late = [i for i, (s0, nb, _) in enumerate(plan) if nb == 1 and s0 > 0]
    npc = len(pieces)

    def body(u_ref, *refs):
        d_refs, o_ref = refs[:npc], refs[npc]
        bufs, sems = refs[npc + 1:-1], refs[-1]
        j = pl.program_id(0)
        cps = {i: pltpu.make_async_copy(d_refs[i], bufs[n], sems.at[n]) for n, i in enumerate(late)}

        @pl.when(j == 0)
        def _():
            for i in late:
                cps[i].start()

        for i, (d_ref, (s0, nb, w)) in enumerate(zip(d_refs, plan)):
            @pl.when((j >= s0) & (j < s0 + nb))
            def _(i=i, d_ref=d_ref, w=w):
                if i in cps:
                    cps[i].wait()
                    d_ref = bufs[late.index(i)]
                r = jnp.dot(u_ref[...], d_ref[...], preferred_element_type=f32)
                if w == bw:
                    o_ref[...] = r
                else:
                    o_ref[:, 0:w] = r
                    o_ref[:, w:] = jnp.zeros((D, bw - w), f32)

    def d_spec(i, s0, nb, w):
        if i in late:
            return ANY
        return pl.BlockSpec((N, w), lambda j: (0, jnp.clip(j - s0, 0, nb - 1)))

    return pl.pallas_call(
        body, name="w_grad", grid=(steps,),
        in_specs=[pl.BlockSpec((D, N), lambda j: (0, 0), pipeline_mode=pl.Buffered(1))]
        + [d_spec(i, *p) for i, p in enumerate(plan)],
        out_specs=pl.BlockSpec((D, bw), lambda j: (0, j)),
        out_shape=jax.ShapeDtypeStruct((D, PC), f32),
        scratch_shapes=[pltpu.VMEM((N, plan[i][2]), bf16) for i in late] + [pltpu.SemaphoreType.DMA((len(late),))],
        compiler_params=_cp(("arbitrary",)),
    )(ut, *[d for d, _ in pieces])


def _adam_math(g, w, m, v):
    m2 = ADAM_B1 * m + (1.0 - ADAM_B1) * g
    v2 = ADAM_B2 * v + (1.0 - ADAM_B2) * (g * g)
    m_hat = m2 / (1.0 - ADAM_B1 ** ADAM_STEP)
    v_hat = v2 / (1.0 - ADAM_B2 ** ADAM_STEP)
    delta = -ADAM_LR * (m_hat / (jnp.sqrt(v_hat) + ADAM_EPS) + ADAM_WD * w)
    return delta, m2, v2


def _adamw(gs, w, m, v, name):
    R, Cc = w.shape
    tr = 256 if R % 256 == 0 else R
    ng = len(gs)

    def body(*refs):
        g = refs[0][...]
        for r in refs[1:ng]:
            g = g + r[...]
        w_ref, m_ref, v_ref, g_ref, d_ref, m2_ref, v2_ref = refs[ng:]
        delta, m2, v2 = _adam_math(g, w_ref[...], m_ref[...], v_ref[...])
        g_ref[...] = g
        d_ref[...] = delta
        m2_ref[...] = m2
        v2_ref[...] = v2

    spec = pl.BlockSpec((tr, Cc), lambda i: (i, 0))
    return pl.pallas_call(
        body, name=name, grid=(R // tr,),
        in_specs=[spec] * (ng + 3), out_specs=[spec] * 4,
        out_shape=[jax.ShapeDtypeStruct((R, Cc), f32)] * 4,
        compiler_params=_cp(("arbitrary",)),
    )(*gs, w, m, v)


def _adamw_rows(g, w, m, v, name):
    R, _, Cc = w.shape
    tr = R // 9

    def body(g_ref, w_ref, m_ref, v_ref, go_ref, d_ref, m2_ref, v2_ref):
        g = g_ref[...]
        delta, m2, v2 = _adam_math(g, w_ref[...], m_ref[...], v_ref[...])
        go_ref[...] = g
        d_ref[...] = delta
        m2_ref[...] = m2
        v2_ref[...] = v2

    spec = pl.BlockSpec((tr, 1, Cc), lambda i: (i, 0, 0))
    return pl.pallas_call(
        body, name=name, grid=(R // tr,),
        in_specs=[spec] * 4, out_specs=[spec] * 4,
        out_shape=[jax.ShapeDtypeStruct((R, 1, Cc), f32)] * 4,
        compiler_params=_cp(("arbitrary",)),
    )(g, w, m, v)


def _place():
    x, y, c = lax.axis_index("x"), lax.axis_index("y"), lax.axis_index("c")
    return x, y, c, [(1 - x, y), (x, 1 - y), (1 - x, 1 - y)]


def _gather_weights(cm, halved, whole):
    R, _, Cc = cm.shape
    hw = Cc // 2
    shards = [jax.ShapeDtypeStruct((R, Cc), bf16)] + list(halved) + list(whole)
    nh = 1 + len(halved)
    na = len(shards)

    def body(*refs):
        srcs, dsts = refs[:na], refs[na:2 * na]
        send_sems, recv_sems, loc_sems = refs[2 * na:2 * na + 3]
        stage = refs[2 * na + 3:3 * na + 3]
        raw = refs[3 * na + 3]
        x, y, c, chips = _place()
        me = 2 * x + y
        loads = [pltpu.make_async_copy(srcs[0], raw, loc_sems.at[0])]
        loads += [pltpu.make_async_copy(srcs[i], stage[i], loc_sems.at[i]) for i in range(1, na)]
        locs = [pltpu.make_async_copy(v, d.at[me], loc_sems.at[i]) for i, (v, d) in enumerate(zip(stage, dsts))]
        for cp in loads:
            cp.start()

        def half_of(ref, i, half):
            return ref.at[:, pl.ds(pl.multiple_of(half * hw, hw), hw)] if i == 0 else ref.at[half]

        def ici(j, i, slot):
            px, py = chips[j]
            src = half_of(stage[0] if i == 0 else srcs[i], i, c) if i < nh else srcs[i]
            dst = half_of(dsts[i].at[slot], i, c) if i < nh else dsts[i].at[slot]
            return pltpu.make_async_remote_copy(
                src_ref=src, dst_ref=dst, send_sem=send_sems.at[na * j + i], recv_sem=recv_sems.at[na * j + i],
                device_id=(px, py, c), device_id_type=MESH)

        def d2d(j, i, half):
            px, py = chips[j]
            blk = half_of(dsts[i].at[2 * px + py], i, half)
            return pltpu.make_async_remote_copy(
                src_ref=blk, dst_ref=blk, send_sem=send_sems.at[3 * na + nh * j + i],
                recv_sem=recv_sems.at[3 * na + nh * j + i], device_id=(x, y, 1 - c), device_id_type=MESH)

        sends = [ici(j, i, me) for j in range(3) for i in range(1, na)]
        for cp in sends:
            cp.start()
        loads[0].wait()
        stage[0][...] = raw[:, 0, :].astype(bf16)
        first = [ici(j, 0, me) for j in range(3)]
        for cp in first:
            cp.start()
        sends += first
        locs[0].start()
        for ld, st in zip(loads[1:], locs[1:]):
            ld.wait()
            st.start()
        for j, (px, py) in enumerate(chips):
            for i in range(na):
                ici(j, i, 2 * px + py).wait_recv()
                if i < nh:
                    fwd = d2d(j, i, c)
                    fwd.start()
                    sends.append(fwd)
        for j in range(3):
            for i in range(nh):
                d2d(j, i, 1 - c).wait_recv()
        for cp in sends:
            cp.wait_send()
        for cp in locs:
            cp.wait()

    nsem = 3 * na + 3 * nh
    return pl.pallas_call(
        body, name="gather_weights",
        in_specs=[ANY] * na, out_specs=[ANY] * na,
        out_shape=[jax.ShapeDtypeStruct((4,) + s.shape, s.dtype) for s in shards],
        scratch_shapes=[pltpu.SemaphoreType.DMA((nsem,)), pltpu.SemaphoreType.DMA((nsem,)),
                        pltpu.SemaphoreType.DMA((na,))] + [pltpu.VMEM(s.shape, s.dtype) for s in shards]
        + [pltpu.VMEM(cm.shape, cm.dtype)],
        compiler_params=pltpu.CompilerParams(has_side_effects=True, vmem_limit_bytes=VMEM_LIMIT),
    )(cm, *halved, *whole)


def _swap_halves(gs):
    na = len(gs)
    jobs = [(i, q) for i in range(na) for q in range(gs[i].shape[0])]

    def body(*refs):
        srcs, dsts = refs[:na], refs[na:2 * na]
        send_sems, recv_sems = refs[2 * na:]
        x, y, c, _ = _place()
        cps = [pltpu.make_async_remote_copy(
            src_ref=srcs[i].at[q, 1 - c], dst_ref=dsts[i].at[q], send_sem=send_sems.at[k],
            recv_sem=recv_sems.at[k], device_id=(x, y, 1 - c), device_id_type=MESH)
            for k, (i, q) in enumerate(jobs)]
        for cp in cps:
            cp.start()
        for cp in cps:
            cp.wait()

    return pl.pallas_call(
        body, name="swap_halves",
        in_specs=[ANY] * na, out_specs=[ANY] * na,
        out_shape=[jax.ShapeDtypeStruct(g.shape[0:1] + g.shape[2:], g.dtype) for g in gs],
        scratch_shapes=[pltpu.SemaphoreType.DMA((len(jobs),)), pltpu.SemaphoreType.DMA((len(jobs),))],
        compiler_params=pltpu.CompilerParams(has_side_effects=True),
    )(*gs)


def _add_split(cm_arr, g, s):
    _, _, R, Cg = g.shape
    tr = 128

    def body(sc_ref, g_ref, s_ref, b_ref, o_ref):
        p = g_ref[0, 0] + s_ref[0]
        own = None
        for q in range(4):
            blk = p[:, SHARD_COLS * q:SHARD_COLS * (q + 1)]
            b_ref[q] = blk.astype(bf16)
            mine = jnp.where(sc_ref[1] == q, blk, 0.0)
            own = mine if own is None else own + mine
        o_ref[...] = own

    return pl.pallas_call(
        body, name="add_w_in",
        grid_spec=pltpu.PrefetchScalarGridSpec(
            num_scalar_prefetch=1, grid=(R // tr,),
            in_specs=[pl.BlockSpec((1, 1, tr, Cg), lambda i, sc: (0, sc[0], i, 0)),
                      pl.BlockSpec((1, tr, Cg), lambda i, sc: (0, i, 0))],
            out_specs=[pl.BlockSpec((4, tr, SHARD_COLS), lambda i, sc: (0, i, 0)),
                       pl.BlockSpec((tr, SHARD_COLS), lambda i, sc: (i, 0))]),
        out_shape=[jax.ShapeDtypeStruct((4, R, SHARD_COLS), bf16), jax.ShapeDtypeStruct((R, SHARD_COLS), f32)],
        compiler_params=_cp(("arbitrary",)),
    )(cm_arr, g, s)


def _add_halves(c_arr, g, s, name):
    Q, _, R, Cc = g.shape
    tr = min(R, 128)

    def body(c_ref, g_ref, s_ref, b_ref, f_ref):
        p = g_ref[0, 0] + s_ref[0]
        f_ref[0] = p
        b_ref[0] = p.astype(bf16)

    blk = pl.BlockSpec((1, tr, Cc), lambda q, i, cr: (q, i, 0))
    return pl.pallas_call(
        body, name=name,
        grid_spec=pltpu.PrefetchScalarGridSpec(
            num_scalar_prefetch=1, grid=(Q, R // tr),
            in_specs=[pl.BlockSpec((1, 1, tr, Cc), lambda q, i, cr: (q, cr[0], i, 0)), blk], out_specs=[blk, blk]),
        out_shape=[jax.ShapeDtypeStruct((Q, R, Cc), bf16), jax.ShapeDtypeStruct((Q, R, Cc), f32)],
        compiler_params=_cp(("arbitrary", "arbitrary")),
    )(c_arr, g, s)


_FLIPS = [(fx, fy, fc) for fx in (0, 1) for fy in (0, 1) for fc in (0, 1)][1:]


def _sum_blocks(own, r, name, transposed=False):
    R, Cc = own.shape
    tr = min(R, 256)

    def body(own_ref, r_ref, o_ref):
        acc = own_ref[...]
        for j in range(3):
            acc = acc + r_ref[j].astype(f32)
        o_ref[...] = acc.T if transposed else acc

    return pl.pallas_call(
        body, name=name, grid=(R // tr,),
        in_specs=[pl.BlockSpec((tr, Cc), lambda i: (i, 0)), pl.BlockSpec((3, tr, Cc), lambda i: (0, i, 0))],
        out_specs=pl.BlockSpec((Cc, tr), lambda i: (0, i)) if transposed else pl.BlockSpec((tr, Cc), lambda i: (i, 0)),
        out_shape=jax.ShapeDtypeStruct((Cc, R) if transposed else (R, Cc), f32),
        compiler_params=_cp(("arbitrary",)),
    )(own, r)


def _sum_packs(me8_arr, pack, rp):
    R = pack.shape[0]

    def body(me_ref, pk_ref, rp_ref, o_ref):
        me8 = me_ref[0]
        acc = None
        for d in range(8):
            rel = d ^ me8
            term = jnp.where(rel == 0, pk_ref[...], rp_ref[jnp.maximum(rel - 1, 0)])
            acc = term if acc is None else acc + term
        o_ref[...] = acc

    return pl.pallas_call(
        body, name="sum_packs",
        grid_spec=pltpu.PrefetchScalarGridSpec(
            num_scalar_prefetch=1, grid=(1,),
            in_specs=[pl.BlockSpec((R, 128), lambda i, mr: (0, 0)), pl.BlockSpec((7, R, 128), lambda i, mr: (0, 0, 0))],
            out_specs=pl.BlockSpec((R, 128), lambda i, mr: (0, 0))),
        out_shape=jax.ShapeDtypeStruct((R, 128), f32),
        compiler_params=_cp(("arbitrary",)),
    )(me8_arr, pack, rp)


def _swap_finished(fs, pack):
    na = len(fs)
    R = pack.shape[0]

    def body(*refs):
        srcs, pk = refs[:na], refs[na]
        dsts, rp = refs[na + 1:2 * na + 1], refs[2 * na + 1]
        send_sems, recv_sems = refs[2 * na + 2:]
        x, y, c, _ = _place()
        cps = [pltpu.make_async_remote_copy(
            src_ref=srcs[i], dst_ref=dsts[i], send_sem=send_sems.at[i], recv_sem=recv_sems.at[i],
            device_id=(x, y, 1 - c), device_id_type=MESH) for i in range(na)]
        cps += [pltpu.make_async_remote_copy(
            src_ref=pk, dst_ref=rp.at[k], send_sem=send_sems.at[na + k], recv_sem=recv_sems.at[na + k],
            device_id=(x ^ fx, y ^ fy, c ^ fc), device_id_type=MESH) for k, (fx, fy, fc) in enumerate(_FLIPS)]
        for cp in cps:
            cp.start()
        for cp in cps:
            cp.wait()

    return pl.pallas_call(
        body, name="swap_finished",
        in_specs=[ANY] * (na + 1), out_specs=[ANY] * (na + 1),
        out_shape=[jax.ShapeDtypeStruct(f.shape, f.dtype) for f in fs] + [jax.ShapeDtypeStruct((7, R, 128), f32)],
        scratch_shapes=[pltpu.SemaphoreType.DMA((na + 7,)), pltpu.SemaphoreType.DMA((na + 7,))],
        compiler_params=pltpu.CompilerParams(has_side_effects=True),
    )(*fs, pack)


def _adamw_halves(c_arr, mine, peer, w, m, v, name):
    _, R, Cc = w.shape
    tr = min(R, 256)

    def body(c_ref, mine_ref, peer_ref, w_ref, m_ref, v_ref, g_ref, d_ref, m2_ref, v2_ref):
        g = jnp.where(pl.program_id(0) == c_ref[0], mine_ref[...], peer_ref[...])
        delta, m2, v2 = _adam_math(g, w_ref[0], m_ref[0], v_ref[0])
        g_ref[0] = g
        d_ref[0] = delta
        m2_ref[0] = m2
        v2_ref[0] = v2

    half = pl.BlockSpec((tr, Cc), lambda hh, i, cr: (i, 0))
    full = pl.BlockSpec((1, tr, Cc), lambda hh, i, cr: (hh, i, 0))
    return pl.pallas_call(
        body, name=name,
        grid_spec=pltpu.PrefetchScalarGridSpec(
            num_scalar_prefetch=1, grid=(2, R // tr), in_specs=[half, half, full, full, full], out_specs=[full] * 4),
        out_shape=[jax.ShapeDtypeStruct((2, R, Cc), f32)] * 4,
        compiler_params=_cp(("arbitrary", "arbitrary")),
    )(c_arr, mine, peer, w, m, v)


def _rows8(a):
    flat = a.reshape(-1)
    n = flat.shape[0]
    rows = -(-n // 1024) * 8
    return jnp.pad(flat, (0, rows * 128 - n)).reshape(rows, 128)


def kernel(x, meta_tokens, norm_w, w_in, conv_w, hg_lb_logits, hg_norm_w, gdn_A_log, gdn_dt_bias, gdn_norm_w, w_out, final_norm_w, loss_target, m_meta_tokens, m_norm_w, m_w_in, m_conv_w, m_hg_lb_logits, m_hg_norm_w, m_gdn_A_log, m_gdn_dt_bias, m_gdn_norm_w, m_w_out, m_final_norm_w, v_meta_tokens, v_norm_w, v_w_in, v_conv_w, v_hg_lb_logits, v_hg_norm_w, v_gdn_A_log, v_gdn_dt_bias, v_gdn_norm_w, v_w_out, v_final_norm_w):
    me = 2 * lax.axis_index("x") + lax.axis_index("y")

    g_win, g_wout, g_conv, g_meta = _gather_weights(
        jnp.transpose(w_in, (2, 0, 1)), [w_out[0].astype(bf16).reshape(2, D // 8, D)], [conv_w[0], meta_tokens])
    wout_full = g_wout.reshape(D, D)
    conv4 = jnp.transpose(g_conv, (1, 0, 2)).reshape(4, 1, 3 * HD)
    meta_full = jnp.transpose(g_meta, (1, 0, 2)).reshape(N_META, D)

    c_arr = lax.axis_index("c").reshape(1).astype(jnp.int32)

    def chip_partials(gw, g_wout_part):
        g_in2 = gw.reshape(1, 2, D // 2, PC)
        g_out4 = g_wout_part.reshape(4, 2, D // 8, D)
        s_in, s_out = _swap_halves([g_in2, g_out4])
        pb_blocks, own_in = _add_split(jnp.concatenate([c_arr, me.reshape(1).astype(jnp.int32)]), g_in2, s_in)
        pb_out, pf_out = _add_halves(c_arr, g_out4, s_out, "add_w_out")
        own_out = lax.dynamic_index_in_dim(pf_out, me, axis=0, keepdims=False)
        return [pb_blocks, pb_out], [own_in, own_out]

    (loss8, grad_x, d_meta, d_nw, d_conv, d_lb, d_hgw, d_alog, d_dtb, d_gdw, d_fw, pfs, rs) = _local_step(
        x, loss_target, g_win, wout_full, conv4, meta_full, norm_w, hg_lb_logits, hg_norm_w, gdn_A_log, gdn_dt_bias,
        gdn_norm_w, final_norm_w, chip_partials)

    pack = jnp.concatenate([
        loss8, d_nw[0].reshape(8, 128), d_lb.reshape(8, 128), d_hgw, _rows8(d_alog[0, :H]), _rows8(d_dtb[0, :H]),
        d_gdw, d_fw[0].reshape(8, 128), d_meta.reshape(128, 128), d_conv.reshape(48, 128)], axis=0)
    return _reduce_and_update(
        me, c_arr, grad_x, pfs, rs, pack, meta_tokens, norm_w, w_in, conv_w, hg_lb_logits, hg_norm_w, gdn_A_log,
        gdn_dt_bias, gdn_norm_w, w_out, final_norm_w, m_meta_tokens, m_norm_w, m_w_in, m_conv_w, m_hg_lb_logits,
        m_hg_norm_w, m_gdn_A_log, m_gdn_dt_bias, m_gdn_norm_w, m_w_out, m_final_norm_w, v_meta_tokens, v_norm_w, v_w_in,
        v_conv_w, v_hg_lb_logits, v_hg_norm_w, v_gdn_A_log, v_gdn_dt_bias, v_gdn_norm_w, v_w_out, v_final_norm_w)


def _local_step(x, loss_target, w4, wout_full, conv4, meta_full, norm_w, hg_lb_logits, hg_norm_w, gdn_A_log, gdn_dt_bias,
                gdn_norm_w, final_norm_w, chip_partials):
    head = jnp.concatenate([jnp.zeros((PAD, D), f32), meta_full], axis=0)
    target = loss_target.reshape(NB * SEQ, D)
    l0, l1 = hg_lb_logits[0:1], hg_lb_logits[1:2]
    alog = jnp.pad(gdn_A_log, ((0, 0), (0, DK - H)))
    dtb = jnp.pad(gdn_dt_bias, ((0, 0), (0, DK - H)))
    fw = final_norm_w.reshape(1, D)

    hflat, proj, ut, cv2, wbig = _in_proj(x.reshape(NB * SEQ, D), head, norm_w, w4, conv4)
    proj3 = proj.reshape(NB, TP, PC)
    cv = cv2.reshape(NB, TP, 3 * HD)
    o_hg, s_hg, o_gd, s_gd, t_gd = _mix_fwd(proj3, cv, l0, l1, alog, dtb)
    (loss8, d_ohg, d_ogd, d_zhg, d_zgd, dh_res, g_wout_part, d_hgw, d_gdw, d_fw) = _out_loss(
        o_hg.reshape(N, HD), o_gd.reshape(N, HD), proj, hg_norm_w, gdn_norm_w, wout_full, hflat, fw, target)
    d_hg, d_l0, d_l1 = _hg_bwd(proj3, l0, l1, s_hg, d_ohg.reshape(NB, TP, HD))
    d_qkv, d_conv4, d_ab, d_alog, d_dtb = _gd_bwd(cv, proj3, conv4, alog, dtb, s_gd, t_gd,
                                                  d_ogd.reshape(NB, TP, HD))
    d_hg2, d_qkv2, d_ab2 = d_hg.reshape(N, 3 * HD), d_qkv.reshape(N, 3 * HD), d_ab.reshape(N, DK)
    pieces = [(d_hg2, COL_HG), (d_zhg, COL_ZHG), (d_qkv2, COL_QKV), (d_zgd, COL_ZGD), (d_ab2, COL_AB)]
    gw = _w_grad(ut, pieces)
    pbs, pfs = chip_partials(gw, g_wout_part) if chip_partials else ([], [gw, g_wout_part])
    dh, d_nw, *rs = _in_bwd(pieces, wbig, hflat, norm_w, dh_res, pbs)

    dh3 = dh.reshape(NB, TP, D)
    grad_x = dh3[:, PAD + N_META:, :]
    d_meta = jnp.sum(dh3[:, PAD:PAD + N_META, :], axis=0)
    d_conv = d_conv4[:, 0, :]
    d_lb = jnp.concatenate([d_l0[0:1], d_l1[0:1]], axis=0)
    return loss8, grad_x, d_meta, d_nw, d_conv, d_lb, d_hgw, d_alog, d_dtb, d_gdw, d_fw, pfs, rs


def _reduce_and_update(me, c_arr, grad_x, pfs, rs, pack, meta_tokens, norm_w, w_in, conv_w, hg_lb_logits, hg_norm_w,
                       gdn_A_log, gdn_dt_bias, gdn_norm_w, w_out, final_norm_w, m_meta_tokens, m_norm_w, m_w_in, m_conv_w,
                       m_hg_lb_logits, m_hg_norm_w, m_gdn_A_log, m_gdn_dt_bias, m_gdn_norm_w, m_w_out, m_final_norm_w,
                       v_meta_tokens, v_norm_w, v_w_in, v_conv_w, v_hg_lb_logits, v_hg_norm_w, v_gdn_A_log, v_gdn_dt_bias,
                       v_gdn_norm_w, v_w_out, v_final_norm_w):
    (own_in, own_out), (r_in, r_out) = pfs, rs
    f_in = _sum_blocks(own_in, r_in, "sum_w_in", transposed=True)
    f_out = _sum_blocks(own_out, r_out, "sum_w_out")
    o_in, o_out, r_pack = _swap_finished([f_in, f_out], pack)
    me8_arr = (2 * me + lax.axis_index("c")).reshape(1).astype(jnp.int32)
    small = _sum_packs(me8_arr, pack, r_pack)

    half_out = lambda a: a[0].reshape(2, D // 8, D)
    is0 = lax.axis_index("c") == 0
    g_in = jnp.concatenate([jnp.where(is0, f_in, o_in), jnp.where(is0, o_in, f_in)], axis=1)
    to_cm = lambda a: jnp.transpose(a, (2, 0, 1))
    gi, di, mi, vi = [jnp.transpose(a, (1, 2, 0))[0] for a in _adamw_rows(
        g_in.reshape(SHARD_COLS, 1, D), to_cm(w_in), to_cm(m_w_in), to_cm(v_w_in), "adamw_w_in")]
    go, do_, mo, vo = [a.reshape(D // 4, D) for a in _adamw_halves(
        c_arr, f_out, o_out, half_out(w_out), half_out(m_w_out), half_out(v_w_out), "adamw_w_out")]

    g_meta_full = small[64:192].reshape(N_META, D)
    g_meta_loc = lax.dynamic_slice(g_meta_full, (0, me * 256), (N_META, 256))
    gm, dm, mm_, vm = _adamw([g_meta_loc], meta_tokens, m_meta_tokens, v_meta_tokens, "adamw_meta")
    g_conv_full = small[192:240].reshape(4, 1536)
    g_conv_loc = lax.dynamic_slice(g_conv_full, (0, me * 384), (4, 384))
    gc, dc, mc, vc = _adamw([g_conv_loc], conv_w[0], m_conv_w[0], v_conv_w[0], "adamw_conv")

    reps = [(norm_w, m_norm_w, v_norm_w), (hg_lb_logits, m_hg_lb_logits, v_hg_lb_logits),
            (hg_norm_w, m_hg_norm_w, v_hg_norm_w), (gdn_A_log, m_gdn_A_log, v_gdn_A_log),
            (gdn_dt_bias, m_gdn_dt_bias, v_gdn_dt_bias), (gdn_norm_w, m_gdn_norm_w, v_gdn_norm_w),
            (final_norm_w, m_final_norm_w, v_final_norm_w)]
    wp = jnp.concatenate([_rows8(t[0]) for t in reps], axis=0)
    mp = jnp.concatenate([_rows8(t[1]) for t in reps], axis=0)
    vp = jnp.concatenate([_rows8(t[2]) for t in reps], axis=0)
    gr, dr, mr, vr = _adamw([small[8:64]], wp, mp, vp, "adamw_small")

    def unpack(p):
        outs = []
        for i, t in enumerate(reps):
            n = t[0].size
            outs.append(p[8 * i:8 * i + 8].reshape(-1)[:n].reshape(t[0].shape))
        return outs

    def leaves(meta_v, conv_v, in_v, out_v, rep_p):
        nw, lb, hgw, al, db, gdw, fwv = unpack(rep_p)
        return [meta_v, nw, in_v[None], conv_v[None], lb, hgw, al, db, gdw, out_v[None], fwv]

    loss = small[0, 0]
    return (loss, grad_x, *leaves(gm, gc, gi, go, gr), *leaves(dm, dc, di, do_, dr),
            *leaves(mm_, mc, mi, mo, mr), *leaves(vm, vc, vi, vo, vr))
```

```python
import functools

import jax
import jax.numpy as jnp
from jax import lax
from jax.experimental import pallas as pl
from jax.experimental.pallas import tpu as pltpu

f32 = jnp.float32
bf16 = jnp.bfloat16
MESH = pl.DeviceIdType.MESH
ANY = pl.BlockSpec(memory_space=pl.ANY)

D = 1024
NB = 2
N_META = 16
SEQ = 2048
PAD = 48
TP = PAD + N_META + SEQ
C = 64
NCH = TP // C
N = NB * TP
H = 4
DK = 128
HD = H * DK
PC = 4224
IN_COLS = 4104
SHARD_COLS = IN_COLS // 4
COL_HG, COL_ZHG, COL_QKV, COL_ZGD, COL_AB = 0, 3 * HD, 4 * HD, 7 * HD, 8 * HD
EPS = 1e-6
ADAM_LR, ADAM_B1, ADAM_B2, ADAM_EPS, ADAM_WD, ADAM_STEP = 0.001, 0.9, 0.999, 1e-08, 0.01, 10
VMEM_LIMIT = 56 * 1024 * 1024

P_HG = dict(lvl=1, av=1, qs=1, su=1)
P_GD = dict(kk=1, inv=1, sol=1, ws=1, qk=1, o=1, su=1)


def _cp(sem=None, **kw):
    return pltpu.CompilerParams(dimension_semantics=sem, vmem_limit_bytes=VMEM_LIMIT, **kw)


_DIMS = {"nn": (((1,), (0,)), ((), ())), "nt": (((1,), (1,)), ((), ())), "tn": (((0,), (0,)), ((), ()))}


def _split(x):
    hi = x.astype(bf16)
    return hi, (x - hi.astype(f32)).astype(bf16)


def _dg(a, b, kind, passes):
    d = lambda x, y: lax.dot_general(x, y, _DIMS[kind], preferred_element_type=f32)
    if passes == 1:
        return d(a.astype(bf16), b.astype(bf16))
    ah, al = _split(a)
    bh, bl = _split(b)
    return d(ah, bh) + d(ah, bl) + d(al, bh)


@functools.partial(jax.custom_vjp, nondiff_argnums=(2, 3))
def mmx(a, b, kind, passes):
    return _dg(a, b, kind, passes)


def _mmx_fwd(a, b, kind, passes):
    return _dg(a, b, kind, passes), (a, b)


def _mmx_bwd(kind, passes, res, g):
    a, b = res
    if kind == "nn":
        return _dg(g, b, "nt", passes), _dg(a, g, "tn", passes)
    if kind == "nt":
        return _dg(g, b, "nn", passes), _dg(g, a, "tn", passes)
    return _dg(b, g, "nt", passes), _dg(a, g, "nn", passes)


mmx.defvjp(_mmx_fwd, _mmx_bwd)


def _mask_dg(mask, x):
    xh, xl = _split(x)
    return jnp.dot(jnp.concatenate([mask, mask], axis=1), jnp.concatenate([xh, xl], axis=0), preferred_element_type=f32)


@functools.partial(jax.custom_vjp, nondiff_argnums=(2,))
def mask_mm(mask, x, bwd_passes):
    return _mask_dg(mask, x)


def _mask_fwd(mask, x, bwd_passes):
    return _mask_dg(mask, x), mask


def _mask_bwd(bwd_passes, mask, g):
    d = lambda y: lax.dot_general(mask, y, _DIMS["tn"], preferred_element_type=f32)
    if bwd_passes == 1:
        return None, d(g.astype(bf16))
    gh, gl = _split(g)
    return None, d(gh) + d(gl)


mask_mm.defvjp(_mask_fwd, _mask_bwd)


def bdot(a, b):
    return jnp.dot(a.astype(bf16), b.astype(bf16), preferred_element_type=f32)


def bdot_nt(a, b):
    return lax.dot_general(a.astype(bf16), b.astype(bf16), _DIMS["nt"], preferred_element_type=f32)


def bdot_tn(a, b):
    return lax.dot_general(a.astype(bf16), b.astype(bf16), _DIMS["tn"], preferred_element_type=f32)


def _iota2(n, m):
    return lax.broadcasted_iota(jnp.int32, (n, m), 0), lax.broadcasted_iota(jnp.int32, (n, m), 1)


sigmoid = jax.nn.sigmoid


def silu(x):
    return x * sigmoid(x)


def softplus(x):
    return jnp.maximum(x, 0.0) + jnp.log(1.0 + jnp.exp(-jnp.abs(x)))


def rmsnorm(x, w):
    return x * lax.rsqrt(jnp.mean(x * x, axis=-1, keepdims=True) + EPS) * w


def hg_masks():
    t, r = _iota2(C, C)
    mats = [r <= t, r > t]
    lvl = []
    for l in range(1, 7):
        sz = 1 << l
        half = sz >> 1
        seg_t = t >> l
        upper_t = (t & (sz - 1)) >= half
        mid_t = seg_t * sz + half - 1
        mats.append((upper_t & (r > mid_t) & (r <= t)) | ((~upper_t) & (r > t) & (r <= mid_t)))
        lvl.append(((seg_t == (r >> l)) & upper_t & ((r & (sz - 1)) < half)).astype(f32))
    stk = jnp.concatenate([m.astype(bf16) for m in mats], axis=0)
    return stk, lvl, (t == r).astype(f32)


def _head(a, h):
    return a[:, h * DK:(h + 1) * DK]


def _run(*gens):
    results = [None] * len(gens)
    live = list(range(len(gens)))
    while live:
        for i in list(live):
            try:
                next(gens[i])
            except StopIteration as e:
                results[i] = e.value
                live.remove(i)
    return results


def hg_chunk(St, ps, l0, l1):
    return _run(hg_stages(St, ps, l0, l1))[0]


def gd_chunk(S, cs, abs_, alog, dtb, t_saved=None):
    return _run(gd_stages(S, cs, abs_, alog, dtb, t_saved))[0]


def mix_chunk(St, ps, l0, l1, S, cs, abs_, alog, dtb):
    (sn_h, o_h), (sn_g, o_g, t_pack) = _run(hg_stages(St, ps, l0, l1), gd_stages(S, cs, abs_, alog, dtb))
    return sn_h, o_h, sn_g, o_g, t_pack


def hg_stages(St, ps, l0, l1):
    m = jnp.maximum(l0, l1)
    e0 = jnp.exp(l0 - m)
    e1 = jnp.exp(l1 - m)
    lb = e0 / (e0 + e1)
    stk, lvl, eye = hg_masks()
    msk = [eye] + lvl
    trow = lax.broadcasted_iota(jnp.int32, (C, 1), 0)
    upper = [(trow & ((1 << l) - 1)) >= (1 << (l - 1)) for l in range(1, 7)]
    qs, ks, vs, qG, kR, eGl = [], [], [], [], [], []
    for p in ps:
        pq, pf, v = p[:, 0:HD], p[:, HD:2 * HD], p[:, 2 * HD:3 * HD]
        q = silu(pq)
        f = lb + (1.0 - lb) * sigmoid(pf)
        k = 1.0 - f
        logf = jnp.log(f)
        Dm = mask_mm(stk, logf, 1)
        z = [jnp.where(up, q, k) * jnp.exp(Dm[(2 + i) * C:(3 + i) * C]) for i, up in enumerate(upper)]
        qs.append([q] + z)
        ks.append([k] + z)
        vs.append(v)
        qG.append(q * jnp.exp(Dm[0:C]))
        kR.append(k * jnp.exp(Dm[C:2 * C]))
        eGl.append(jnp.exp(jnp.sum(logf, axis=0, keepdims=True)))
    yield
    units = [(b, h) for b in range(len(ps)) for h in range(H)]
    parts = []
    for i in range(7):
        parts.append([msk[i] * mmx(_head(qs[b][i], h), _head(ks[b][i], h), "nt", P_HG["lvl"]) for b, h in units])
        yield
    A = [functools.reduce(lambda x, y: x + y, [parts[i][n] for i in range(7)]) for n in range(len(units))]
    qS = [mmx(_head(qG[b], h), St[n], "nt", P_HG["qs"]) for n, (b, h) in enumerate(units)]
    Sn = [St[n] * _head(eGl[b], h) + mmx(_head(vs[b], h), _head(kR[b], h), "tn", P_HG["su"])
          for n, (b, h) in enumerate(units)]
    yield
    outs = [mmx(A[n], _head(vs[b], h), "nn", P_HG["av"]) + qS[n] for n, (b, h) in enumerate(units)]
    return tuple(Sn), tuple(jnp.concatenate(outs[b * H:(b + 1) * H], axis=1) for b in range(len(ps)))


@jax.custom_vjp
def use_inverse(A, T):
    return T


def _use_inverse_fwd(A, T):
    return T, T


def _use_inverse_bwd(T, g):
    return -_dg(T, _dg(g, T, "nt", P_GD["inv"]), "tn", P_GD["inv"]), jnp.zeros_like(T)


use_inverse.defvjp(_use_inverse_fwd, _use_inverse_bwd)


def gd_stages(S, cs, abs_, alog, dtb, t_saved=None):
    t, r = _iota2(C, C)
    tri = (r <= t).astype(bf16)
    ups = (r > t).astype(bf16)
    lane = lax.broadcasted_iota(jnp.int32, (1, DK), 1)
    subl = lax.broadcasted_iota(jnp.int32, (8, 1), 0)
    eye = (t == r).astype(f32)
    strict = (r < t).astype(f32)
    bd = ((t >> 4) == (r >> 4)).astype(f32)
    qa, ka, va, b4, gam4, grev4, gam4T, glast4 = [], [], [], [], [], [], [], []
    for c, ab in zip(cs, abs_):
        qa.append(silu(c[:, 0:HD]))
        ka.append(silu(c[:, HD:2 * HD]))
        va.append(silu(c[:, 2 * HD:3 * HD]))
        g4 = -jnp.exp(alog) * softplus(ab + dtb)
        b4.append(sigmoid(ab))
        gam4.append(mask_mm(tri, g4, 2))
        grev4.append(mask_mm(ups, g4, 2))
        gam4T.append(gam4[-1].T)
        glast4.append(jnp.sum(g4, axis=0, keepdims=True))
    yield
    units = [(b, h) for b in range(len(cs)) for h in range(H)]
    nu = range(len(units))
    inv = lambda a, b: [mmx(a[n], b[n], "nn", P_GD["inv"]) for n in nu]
    v = [_head(va[b], h) for b, h in units]
    q = [_head(qa[b], h) for b, h in units]
    k = [_head(ka[b], h) for b, h in units]
    q = [x * lax.rsqrt(jnp.sum(x * x, -1, keepdims=True) + EPS) * (DK ** -0.5) for x in q]
    k = [x * lax.rsqrt(jnp.sum(x * x, -1, keepdims=True) + EPS) for x in k]
    oh = [(lane == h).astype(f32) for h in range(H)]
    gam_c = [jnp.sum(gam4[b] * oh[h], -1, keepdims=True) for b, h in units]
    grev_c = [jnp.sum(grev4[b] * oh[h], -1, keepdims=True) for b, h in units]
    beta = [jnp.sum(b4[b] * (lane == h + H).astype(f32), -1, keepdims=True) for b, h in units]
    glast = [jnp.sum(glast4[b] * oh[h], -1, keepdims=True) for b, h in units]
    gam_r = [jnp.sum(gam4T[b][0:8, :] * (subl == h).astype(f32), axis=0, keepdims=True) for b, h in units]
    dec = [jnp.exp(jnp.where(r <= t, gam_c[n] - gam_r[n], -1e30)) for n in nu]
    egam = [jnp.exp(gam_c[n]) for n in nu]
    kk = [mmx(k[n], k[n], "nt", P_GD["kk"]) for n in nu]
    qk = [mmx(q[n], k[n], "nt", P_GD["qk"]) * dec[n] for n in nu]
    yield
    A = [beta[n] * kk[n] * dec[n] * strict for n in nu]
    Dg = [A[n] * bd for n in nu]
    L = [A[n] - Dg[n] for n in nu]
    if t_saved is None:
        ImD = [eye - Dg[n] for n in nu]
        D2 = inv(Dg, Dg)
        yield
        P1 = inv(ImD, [eye + x for x in D2])
        D4 = inv(D2, D2)
        yield
        P2 = inv(P1, [eye + x for x in D4])
        D8 = inv(D4, D4)
        yield
        M = inv(P2, [eye + x for x in D8])
        yield
        Nn = inv(M, L)
        yield
        N2 = inv(Nn, Nn)
        yield
        T1 = inv([eye - x for x in Nn], [eye + x for x in N2])
        yield
        Tinv = inv(T1, M)
        yield
    else:
        Tinv = [use_inverse(A[n], t_saved[b][:, h * DK:h * DK + C]) for n, (b, h) in enumerate(units)]
    rhs = [jnp.concatenate([beta[n] * v[n], (beta[n] * egam[n]) * k[n]], axis=1) for n in nu]
    sol = [mmx(Tinv[n], rhs[n], "nn", P_GD["sol"]) for n in nu]
    yield
    qwS = [mmx(jnp.concatenate([q[n] * egam[n], sol[n][:, DK:2 * DK]], axis=0), S[n], "nn", P_GD["ws"]) for n in nu]
    yield
    u = [sol[n][:, 0:DK] - qwS[n][C:2 * C] for n in nu]
    outs = [qwS[n][0:C] + mmx(qk[n], u[n], "nn", P_GD["o"]) for n in nu]
    Sn = [jnp.exp(glast[n]) * S[n] + mmx(k[n] * jnp.exp(grev_c[n]), u[n], "tn", P_GD["su"]) for n in nu]
    zpad = jnp.zeros((C, DK - C), f32)
    t_pack = tuple(jnp.concatenate([x for n in range(b * H, (b + 1) * H) for x in (lax.stop_gradient(Tinv[n]), zpad)],
                                   axis=1) for b in range(len(cs)))
    return tuple(Sn), tuple(jnp.concatenate(outs[b * H:(b + 1) * H], axis=1) for b in range(len(cs))), t_pack


def _chunk_index(tile_chunks, k):
    def index(i):
        chunk = tile_chunks * i + k
        b = chunk // NCH
        return jnp.maximum((SEQ // C) * b + chunk - NCH * b - 1, 0), 0
    return index


def _in_proj(xflat, head, norm_w, w4, conv4):
    tm = 384
    nck = tm // C
    W3 = 3 * HD

    def body(*refs):
        x_refs = refs[:nck]
        head_ref, nw_ref, w4_ref, cw_ref, h_ref, p_ref, ut_ref, cv_ref, w_hbm, prev, w_ref, w_sem = refs[nck:]
        i = pl.program_id(0)
        w_out = pltpu.make_async_copy(w_ref, w_hbm, w_sem.at[0])

        @pl.when(i == 0)
        def _():
            prev[...] = jnp.zeros_like(prev)
            w_ref[PC - DK:PC, :] = jnp.zeros((DK, D), bf16)
            for q in range(4):
                w_ref[SHARD_COLS * q:SHARD_COLS * (q + 1), :] = w4_ref[q]
            w_out.start()

        blocks = []
        for k in range(nck):
            chunk = nck * i + k
            blocks.append(jnp.where(chunk - NCH * (chunk // NCH) == 0, head_ref[...], x_refs[k][...]))
        hval = jnp.concatenate(blocks, axis=0)
        h_ref[...] = hval
        u = rmsnorm(hval, nw_ref[...])
        ut_ref[...] = u.T.astype(bf16)
        p = bdot_nt(u, w_ref[...])
        p_ref[...] = p
        x = p[:, COL_QKV:COL_QKV + W3]
        xx = jnp.concatenate([prev[...], x], axis=0)
        y = cw_ref[3] * x
        for s in (1, 2, 3):
            y = y + cw_ref[3 - s] * pltpu.roll(xx, s, 0)[8:]
        row = i * tm + lax.broadcasted_iota(jnp.int32, (tm, 1), 0)
        tok = jnp.where(row >= TP, row - TP, row)
        cv_ref[...] = jnp.where(tok >= 8, y, 0.0)
        prev[...] = x[tm - 8:tm]

        @pl.when(i == N // tm - 1)
        def _():
            w_out.wait()

    return pl.pallas_call(
        body, name="in_proj", grid=(N // tm,),
        in_specs=[pl.BlockSpec((C, D), _chunk_index(nck, k)) for k in range(nck)]
        + [pl.BlockSpec((C, D), lambda i: (0, 0)), pl.BlockSpec((1, D), lambda i: (0, 0)),
           pl.BlockSpec((4, SHARD_COLS, D), lambda i: (0, 0, 0), pipeline_mode=pl.Buffered(1)),
           pl.BlockSpec((4, 1, W3), lambda i: (0, 0, 0))],
        out_specs=[pl.BlockSpec((tm, D), lambda i: (i, 0)), pl.BlockSpec((tm, PC), lambda i: (i, 0)),
                   pl.BlockSpec((D, tm), lambda i: (0, i)), pl.BlockSpec((tm, W3), lambda i: (i, 0)),
                   ANY],
        out_shape=[jax.ShapeDtypeStruct((N, D), f32), jax.ShapeDtypeStruct((N, PC), f32),
                   jax.ShapeDtypeStruct((D, N), bf16), jax.ShapeDtypeStruct((N, W3), f32),
                   jax.ShapeDtypeStruct((PC, D), bf16)],
        scratch_shapes=[pltpu.VMEM((8, W3), f32), pltpu.VMEM((PC, D), bf16), pltpu.SemaphoreType.DMA((1,))],
        compiler_params=_cp(("arbitrary",)),
    )(*[xflat] * nck, head, norm_w, w4, conv4)


NU = NB * H
_REV = lambda c: NCH - 1 - c
_FWD = lambda c: c


def _tok_spec(w, ix, col=0):
    return pl.BlockSpec((NB, C, w), lambda c: (0, ix(c), col))


def _state_spec(ix):
    return pl.BlockSpec((NB, 1, H, DK, DK), lambda c: (0, ix(c), 0, 0, 0))


def _row_spec(w):
    return pl.BlockSpec((1, w), lambda c: (0, 0))


def _rows(ref):
    return tuple(ref[b] for b in range(NB))


def _hg_extra_specs(ix):
    return [_row_spec(HD), _row_spec(HD)]


def _gd_extra_specs(ix):
    return [_tok_spec(DK, ix, COL_AB // DK), _row_spec(DK), _row_spec(DK)]


def _mix_fwd(proj3, cv, l0, l1, alog, dtb):
    def body(p_ref, c_ref, ab_ref, l0_ref, l1_ref, al_ref, db_ref, oh_ref, sh_ref, og_ref, sg_ref, t_ref, sth, stg):
        @pl.when(pl.program_id(0) == 0)
        def _():
            sth[...] = jnp.zeros_like(sth)
            stg[...] = jnp.zeros_like(stg)

        Sh = tuple(sth[n] for n in range(NU))
        Sg = tuple(stg[n] for n in range(NU))
        for n in range(NU):
            sh_ref[n // H, 0, n % H] = Sh[n]
            sg_ref[n // H, 0, n % H] = Sg[n]
        snh, oh, sng, og, tp = mix_chunk(Sh, _rows(p_ref), l0_ref[...], l1_ref[...],
                                         Sg, _rows(c_ref), _rows(ab_ref), al_ref[...], db_ref[...])
        for n in range(NU):
            sth[n] = snh[n]
            stg[n] = sng[n]
        for b in range(NB):
            oh_ref[b] = oh[b]
            og_ref[b] = og[b]
            t_ref[b] = tp[b]

    tok = jax.ShapeDtypeStruct((NB, TP, HD), f32)
    st = jax.ShapeDtypeStruct((NB, NCH, H, DK, DK), f32)
    return pl.pallas_call(
        body, name="mix_fwd", grid=(NCH,),
        in_specs=[_tok_spec(3 * HD, _FWD), _tok_spec(3 * HD, _FWD), _tok_spec(DK, _FWD, COL_AB // DK),
                  _row_spec(HD), _row_spec(HD), _row_spec(DK), _row_spec(DK)],
        out_specs=[_tok_spec(HD, _FWD), _state_spec(_FWD), _tok_spec(HD, _FWD), _state_spec(_FWD), _tok_spec(HD, _FWD)],
        out_shape=[tok, st, tok, st, tok],
        scratch_shapes=[pltpu.VMEM((NU, DK, DK), f32), pltpu.VMEM((NU, DK, DK), f32)],
        compiler_params=_cp(("arbitrary",)),
    )(proj3, cv, proj3, l0, l1, alog, dtb)


def _hg_bwd(proj3, l0, l1, s_saved, do):
    def body(p_ref, l0_ref, l1_ref, s_ref, do_ref, dp_ref, dl0_ref, dl1_ref, dst):
        @pl.when(pl.program_id(0) == 0)
        def _():
            dst[...] = jnp.zeros_like(dst)
            dl0_ref[...] = jnp.zeros_like(dl0_ref)
            dl1_ref[...] = jnp.zeros_like(dl1_ref)

        S = tuple(s_ref[n // H, 0, n % H] for n in range(NU))
        _, vjp = jax.vjp(hg_chunk, S, _rows(p_ref), l0_ref[...], l1_ref[...])
        dS, dp, dl0, dl1 = vjp((tuple(dst[n] for n in range(NU)), _rows(do_ref)))
        for n in range(NU):
            dst[n] = dS[n]
        for b in range(NB):
            dp_ref[b] = dp[b].astype(bf16)
        dl0_ref[...] += jnp.broadcast_to(dl0, (8, HD))
        dl1_ref[...] += jnp.broadcast_to(dl1, (8, HD))

    acc = pl.BlockSpec((8, HD), lambda c: (0, 0))
    return pl.pallas_call(
        body, name="hg_bwd", grid=(NCH,),
        in_specs=[_tok_spec(3 * HD, _REV)] + _hg_extra_specs(_REV) + [_state_spec(_REV), _tok_spec(HD, _REV)],
        out_specs=[_tok_spec(3 * HD, _REV), acc, acc],
        out_shape=[jax.ShapeDtypeStruct((NB, TP, 3 * HD), bf16), jax.ShapeDtypeStruct((8, HD), f32),
                   jax.ShapeDtypeStruct((8, HD), f32)],
        scratch_shapes=[pltpu.VMEM((NU, DK, DK), f32)],
        compiler_params=_cp(("arbitrary",)),
    )(proj3, l0, l1, s_saved, do)


def _gd_bwd(cv, proj3, conv4, alog, dtb, s_saved, t_saved, do):
    def body(c_ref, ab_ref, al_ref, db_ref, s_ref, t_ref, do_ref, x0_ref, x1_ref, x2_ref, w_ref,
             dx_ref, dw_ref, dab_ref, dal_ref, ddb_ref, dst, nxt):
        @pl.when(pl.program_id(0) == 0)
        def _():
            dst[...] = jnp.zeros_like(dst)
            dal_ref[...] = jnp.zeros_like(dal_ref)
            ddb_ref[...] = jnp.zeros_like(ddb_ref)
            dw_ref[...] = jnp.zeros_like(dw_ref)
            nxt[...] = jnp.zeros_like(nxt)

        S = tuple(s_ref[n // H, 0, n % H] for n in range(NU))
        t_rows = _rows(t_ref)
        fn = lambda *a: gd_chunk(*a, t_saved=t_rows)[0:2]
        _, vjp = jax.vjp(fn, S, _rows(c_ref), _rows(ab_ref), al_ref[...], db_ref[...])
        dS, dc, dab, dal, ddb = vjp((tuple(dst[n] for n in range(NU)), _rows(do_ref)))
        for n in range(NU):
            dst[n] = dS[n]
        for b in range(NB):
            dab_ref[b] = dab[b].astype(bf16)
        dal_ref[...] += jnp.broadcast_to(dal, (8, DK))
        ddb_ref[...] += jnp.broadcast_to(ddb, (8, DK))

        rows8 = lambda p: sum(p[8 * i:8 * i + 8] for i in range(1, C // 8)) + p[0:8]
        for j, x_ref in enumerate((x0_ref, x1_ref, x2_ref)):
            sl = slice(j * HD, (j + 1) * HD)
            dws = [None] * 4
            for b in range(NB):
                x = x_ref[b]
                g = dc[b][:, sl]
                gg = jnp.concatenate([g, nxt[b, :, sl]], axis=0)
                dx = w_ref[3, :, sl] * g
                parts = [x * g]
                for s in (1, 2, 3):
                    gs = pltpu.roll(gg, C + 8 - s, 0)[:C]
                    dx = dx + w_ref[3 - s, :, sl] * gs
                    parts.append(x * gs)
                nxt[b, :, sl] = g[:8]
                dx_ref[b, :, sl] = dx.astype(bf16)
                dws = [p if a is None else a + p for a, p in zip(dws, parts)]
            for s in range(4):
                dw_ref[3 - s, :, sl] += rows8(dws[s])

        @pl.when(pl.program_id(0) == NCH - 1)
        def _():
            for s in range(4):
                dw_ref[s] = jnp.broadcast_to(jnp.sum(dw_ref[s], axis=0, keepdims=True), (8, 3 * HD))

    acc = pl.BlockSpec((8, DK), lambda c: (0, 0))
    return pl.pallas_call(
        body, name="gd_bwd", grid=(NCH,),
        in_specs=[_tok_spec(3 * HD, _REV)] + _gd_extra_specs(_REV)
        + [_state_spec(_REV), _tok_spec(HD, _REV), _tok_spec(HD, _REV)]
        + [_tok_spec(HD, _REV, COL_QKV // HD + j) for j in range(3)]
        + [pl.BlockSpec((4, 1, 3 * HD), lambda c: (0, 0, 0))],
        out_specs=[_tok_spec(3 * HD, _REV), pl.BlockSpec((4, 8, 3 * HD), lambda c: (0, 0, 0)),
                   _tok_spec(DK, _REV), acc, acc],
        out_shape=[jax.ShapeDtypeStruct((NB, TP, 3 * HD), bf16), jax.ShapeDtypeStruct((4, 8, 3 * HD), f32),
                   jax.ShapeDtypeStruct((NB, TP, DK), bf16),
                   jax.ShapeDtypeStruct((8, DK), f32), jax.ShapeDtypeStruct((8, DK), f32)],
        scratch_shapes=[pltpu.VMEM((NU, DK, DK), f32), pltpu.VMEM((NB, 8, 3 * HD), f32)],
        compiler_params=_cp(("arbitrary",)),
    )(cv, proj3, alog, dtb, s_saved, t_saved, do, proj3, proj3, proj3, conv4)


def _out_loss(o_hg, o_gd, proj, hgw, gdw, wout, hflat, fw, target):
    tm = 384

    def body(ohg_ref, ogd_ref, zhg_ref, zgd_ref, hgw_ref, gdw_ref, wo_ref, h_ref, fw_ref, *refs):
        tg_refs = refs[:tm // C]
        (loss_ref, dohg_ref, dogd_ref, dzhg_ref, dzgd_ref, dh_ref, dwo_ref, dhgw_ref, dgdw_ref, dfw_ref) = refs[tm // C:]
        i = pl.program_id(0)

        @pl.when(i == 0)
        def _():
            for r in (loss_ref, dwo_ref, dhgw_ref, dgdw_ref, dfw_ref):
                r[...] = jnp.zeros_like(r)

        row = i * tm + lax.broadcasted_iota(jnp.int32, (tm, 1), 0)
        tok = jnp.where(row >= TP, row - TP, row)
        valid = (tok >= PAD + N_META).astype(f32)
        hval = h_ref[...]
        tgt = jnp.concatenate([r[...] for r in tg_refs], axis=0)

        mixers = ((ohg_ref, zhg_ref, hgw_ref[...]), (ogd_ref, zgd_ref, gdw_ref[...]))
        saved, ys = [], []
        for o_ref, z_ref, w in mixers:
            for hh in range(H):
                sl = slice(hh * DK, (hh + 1) * DK)
                o, z = o_ref[:, sl], z_ref[:, sl]
                r = lax.rsqrt(jnp.mean(o * o, axis=-1, keepdims=True) + EPS)
                n = o * r
                sg = sigmoid(z)
                ws = w * (z * sg)
                saved.append((r, n, sg, z, ws, w))
                ys.append(n * ws)
        y = jnp.concatenate(ys, axis=-1)
        h2 = hval + bdot(y, wo_ref[...])
        r2 = lax.rsqrt(jnp.mean(h2 * h2, axis=-1, keepdims=True) + EPS)
        n2 = h2 * r2
        fwv = fw_ref[...]
        err = (n2 * fwv - tgt) * valid
        loss = (0.5 / D) * jnp.sum(err * err)
        dyf = err * (1.0 / D)
        dn2 = dyf * fwv
        dout = r2 * (dn2 - n2 * jnp.mean(dn2 * n2, axis=-1, keepdims=True))
        dh_ref[...] = dout
        dy = bdot_nt(dout, wo_ref[...])
        dwo_ref[...] += bdot_tn(y, dout)
        dws = []
        for mi, (do_ref, dz_ref) in enumerate(((dohg_ref, dzhg_ref), (dogd_ref, dzgd_ref))):
            dw = jnp.zeros((1, DK), f32)
            for hh in range(H):
                sl = slice(hh * DK, (hh + 1) * DK)
                r, n, sg, z, ws, w = saved[mi * H + hh]
                dyh = dy[:, mi * HD + hh * DK:mi * HD + (hh + 1) * DK]
                t = dyh * n
                dw = dw + jnp.sum(t * (z * sg), axis=0, keepdims=True)
                dz_ref[:, sl] = (t * w * (sg * (1.0 + z * (1.0 - sg)))).astype(bf16)
                dn = dyh * ws
                do_ref[:, sl] = r * (dn - n * jnp.mean(dn * n, axis=-1, keepdims=True))
            dws.append(dw)
        loss_ref[...] += jnp.broadcast_to(loss, (8, DK))
        dhgw_ref[...] += jnp.broadcast_to(dws[0], (8, DK))
        dgdw_ref[...] += jnp.broadcast_to(dws[1], (8, DK))
        dfw_ref[...] += jnp.broadcast_to(jnp.sum(dyf * n2, axis=0, keepdims=True), (8, D))

    row = lambda w: pl.BlockSpec((tm, w), lambda i: (i, 0))
    whole = lambda r, w: pl.BlockSpec((r, w), lambda i: (0, 0))
    col = lambda c0: pl.BlockSpec((tm, HD), lambda i: (i, c0 // HD))

    tgt_spec = lambda k: pl.BlockSpec((C, D), _chunk_index(tm // C, k))
    return pl.pallas_call(
        body, name="out_loss", grid=(N // tm,),
        in_specs=[row(HD), row(HD), col(COL_ZHG), col(COL_ZGD),
                  whole(1, DK), whole(1, DK), whole(D, D), row(D), whole(1, D)] + [tgt_spec(k) for k in range(tm // C)],
        out_specs=[whole(8, DK), row(HD), row(HD), row(HD), row(HD), row(D), whole(D, D),
                   whole(8, DK), whole(8, DK), whole(8, D)],
        out_shape=[jax.ShapeDtypeStruct((8, DK), f32)] + [jax.ShapeDtypeStruct((N, HD), f32)] * 2
        + [jax.ShapeDtypeStruct((N, HD), bf16)] * 2
        + [jax.ShapeDtypeStruct((N, D), f32), jax.ShapeDtypeStruct((D, D), f32),
           jax.ShapeDtypeStruct((8, DK), f32), jax.ShapeDtypeStruct((8, DK), f32), jax.ShapeDtypeStruct((8, D), f32)],
        compiler_params=_cp(("arbitrary",)),
    )(o_hg, o_gd, proj, proj, hgw, gdw, wout, hflat, fw, *[target] * (tm // C))


def _in_bwd(pieces, wbig, hflat, norm_w, dh_res, pbs):
    tm = 384
    nsteps = N // tm
    np_ = len(pieces)
    na = len(pbs)
    offs = [c0 for _, c0 in pieces]
    widths = [d.shape[1] for d, _ in pieces]

    def body(*refs):
        d_refs = refs[:np_]
        w_ref, h_ref, nw_ref, dhr_ref = refs[np_:np_ + 4]
        srcs = refs[np_ + 4:np_ + 4 + na]
        dh_ref, dnw_ref = refs[np_ + 4 + na:np_ + 6 + na]
        dsts = refs[np_ + 6 + na:np_ + 6 + 2 * na]
        sems = refs[np_ + 6 + 2 * na:]
        i = pl.program_id(0)

        def copies():
            if not na:
                return []
            x, y, c, chips = _place()
            return [pltpu.make_async_remote_copy(
                src_ref=srcs[a].at[2 * px + py], dst_ref=dsts[a].at[j], send_sem=sems[0].at[na * j + a],
                recv_sem=sems[1].at[na * j + a], device_id=(px, py, c), device_id_type=MESH)
                for j, (px, py) in enumerate(chips) for a in range(na)]

        @pl.when(i == 0)
        def _():
            dnw_ref[...] = jnp.zeros_like(dnw_ref)
            for cp in copies():
                cp.start()

        du = jnp.zeros((tm, D), f32)
        for d_ref, off, wd in zip(d_refs, offs, widths):
            du = du + bdot(d_ref[...], w_ref[off:off + wd, :])
        _, vjp = jax.vjp(rmsnorm, h_ref[...], nw_ref[...])
        dh, dnw = vjp(du)
        dh_ref[...] = dh + dhr_ref[...]
        dnw_ref[...] += jnp.broadcast_to(dnw, (8, D))

        @pl.when(i == nsteps - 1)
        def _():
            for cp in copies():
                cp.wait()

    row = lambda w: pl.BlockSpec((tm, w), lambda i: (i, 0))
    return pl.pallas_call(
        body, name="in_bwd", grid=(nsteps,),
        in_specs=[row(w) for w in widths]
        + [pl.BlockSpec((PC, D), lambda i: (0, 0)), row(D), pl.BlockSpec((1, D), lambda i: (0, 0)), row(D)] + [ANY] * na,
        out_specs=[row(D), pl.BlockSpec((8, D), lambda i: (0, 0))] + [ANY] * na,
        out_shape=[jax.ShapeDtypeStruct((N, D), f32), jax.ShapeDtypeStruct((8, D), f32)]
        + [jax.ShapeDtypeStruct((3,) + p.shape[1:], p.dtype) for p in pbs],
        scratch_shapes=[pltpu.SemaphoreType.DMA((3 * na,)), pltpu.SemaphoreType.DMA((3 * na,))] if na else [],
        compiler_params=_cp(("arbitrary",)),
    )(*[d for d, _ in pieces], wbig, hflat, norm_w, dh_res, *pbs)


def _w_grad(ut, pieces):
    bw = HD
    steps = PC // bw + (PC % bw > 0)
    plan = []
    for d, c0 in pieces:
        w = d.shape[1]
        plan.append((c0 // bw, max(1, w // bw), min(w, bw)))
    late = [i for i, (s0, nb, _) in enumerate(plan) if nb == 1 and s0 > 0]
    npc = len(pieces)

    def body(u_ref, *refs):
        d_refs, o_ref = refs[:npc], refs[npc]
        bufs, sems = refs[npc + 1:-1], refs[-1]
        j = pl.program_id(0)
        cps = {i: pltpu.make_async_copy(d_refs[i], bufs[n], sems.at[n]) for n, i in enumerate(late)}

        @pl.when(j == 0)
        def _():
            for i in late:
                cps[i].start()

        for i, (d_ref, (s0, nb, w)) in enumerate(zip(d_refs, plan)):
            @pl.when((j >= s0) & (j < s0 + nb))
            def _(i=i, d_ref=d_ref, w=w):
                if i in cps:
                    cps[i].wait()
                    d_ref = bufs[late.index(i)]
                r = jnp.dot(u_ref[...], d_ref[...], preferred_element_type=f32)
                if w == bw:
                    o_ref[...] = r
                else:
                    o_ref[:, 0:w] = r
                    o_ref[:, w:] = jnp.zeros((D, bw - w), f32)

    def d_spec(i, s0, nb, w):
        if i in late:
            return ANY
        return pl.BlockSpec((N, w), lambda j: (0, jnp.clip(j - s0, 0, nb - 1)))

    return pl.pallas_call(
        body, name="w_grad", grid=(steps,),
        in_specs=[pl.BlockSpec((D, N), lambda j: (0, 0), pipeline_mode=pl.Buffered(1))]
        + [d_spec(i, *p) for i, p in enumerate(plan)],
        out_specs=pl.BlockSpec((D, bw), lambda j: (0, j)),
        out_shape=jax.ShapeDtypeStruct((D, PC), f32),
        scratch_shapes=[pltpu.VMEM((N, plan[i][2]), bf16) for i in late] + [pltpu.SemaphoreType.DMA((len(late),))],
        compiler_params=_cp(("arbitrary",)),
    )(ut, *[d for d, _ in pieces])


def _adam_math(g, w, m, v):
    m2 = ADAM_B1 * m + (1.0 - ADAM_B1) * g
    v2 = ADAM_B2 * v + (1.0 - ADAM_B2) * (g * g)
    m_hat = m2 / (1.0 - ADAM_B1 ** ADAM_STEP)
    v_hat = v2 / (1.0 - ADAM_B2 ** ADAM_STEP)
    delta = -ADAM_LR * (m_hat / (jnp.sqrt(v_hat) + ADAM_EPS) + ADAM_WD * w)
    return delta, m2, v2


def _adamw(gs, w, m, v, name):
    R, Cc = w.shape
    tr = 256 if R % 256 == 0 else R
    ng = len(gs)

    def body(*refs):
        g = refs[0][...]
        for r in refs[1:ng]:
            g = g + r[...]
        w_ref, m_ref, v_ref, g_ref, d_ref, m2_ref, v2_ref = refs[ng:]
        delta, m2, v2 = _adam_math(g, w_ref[...], m_ref[...], v_ref[...])
        g_ref[...] = g
        d_ref[...] = delta
        m2_ref[...] = m2
        v2_ref[...] = v2

    spec = pl.BlockSpec((tr, Cc), lambda i: (i, 0))
    return pl.pallas_call(
        body, name=name, grid=(R // tr,),
        in_specs=[spec] * (ng + 3), out_specs=[spec] * 4,
        out_shape=[jax.ShapeDtypeStruct((R, Cc), f32)] * 4,
        compiler_params=_cp(("arbitrary",)),
    )(*gs, w, m, v)


def _adamw_rows(g, w, m, v, name):
    R, _, Cc = w.shape
    tr = R // 9

    def body(g_ref, w_ref, m_ref, v_ref, go_ref, d_ref, m2_ref, v2_ref):
        g = g_ref[...]
        delta, m2, v2 = _adam_math(g, w_ref[...], m_ref[...], v_ref[...])
        go_ref[...] = g
        d_ref[...] = delta
        m2_ref[...] = m2
        v2_ref[...] = v2

    spec = pl.BlockSpec((tr, 1, Cc), lambda i: (i, 0, 0))
    return pl.pallas_call(
        body, name=name, grid=(R // tr,),
        in_specs=[spec] * 4, out_specs=[spec] * 4,
        out_shape=[jax.ShapeDtypeStruct((R, 1, Cc), f32)] * 4,
        compiler_params=_cp(("arbitrary",)),
    )(g, w, m, v)


def _place():
    x, y, c = lax.axis_index("x"), lax.axis_index("y"), lax.axis_index("c")
    return x, y, c, [(1 - x, y), (x, 1 - y), (1 - x, 1 - y)]


def _gather_weights(cm, halved, whole):
    R, _, Cc = cm.shape
    hw = Cc // 2
    shards = [jax.ShapeDtypeStruct((R, Cc), bf16)] + list(halved) + list(whole)
    nh = 1 + len(halved)
    na = len(shards)

    def body(*refs):
        srcs, dsts = refs[:na], refs[na:2 * na]
        send_sems, recv_sems, loc_sems = refs[2 * na:2 * na + 3]
        stage = refs[2 * na + 3:3 * na + 3]
        raw = refs[3 * na + 3]
        x, y, c, chips = _place()
        me = 2 * x + y
        loads = [pltpu.make_async_copy(srcs[0], raw, loc_sems.at[0])]
        loads += [pltpu.make_async_copy(srcs[i], stage[i], loc_sems.at[i]) for i in range(1, na)]
        locs = [pltpu.make_async_copy(v, d.at[me], loc_sems.at[i]) for i, (v, d) in enumerate(zip(stage, dsts))]
        for cp in loads:
            cp.start()

        def half_of(ref, i, half):
            return ref.at[:, pl.ds(pl.multiple_of(half * hw, hw), hw)] if i == 0 else ref.at[half]

        def ici(j, i, slot):
            px, py = chips[j]
            src = half_of(stage[0] if i == 0 else srcs[i], i, c) if i < nh else srcs[i]
            dst = half_of(dsts[i].at[slot], i, c) if i < nh else dsts[i].at[slot]
            return pltpu.make_async_remote_copy(
                src_ref=src, dst_ref=dst, send_sem=send_sems.at[na * j + i], recv_sem=recv_sems.at[na * j + i],
                device_id=(px, py, c), device_id_type=MESH)

        def d2d(j, i, half):
            px, py = chips[j]
            blk = half_of(dsts[i].at[2 * px + py], i, half)
            return pltpu.make_async_remote_copy(
                src_ref=blk, dst_ref=blk, send_sem=send_sems.at[3 * na + nh * j + i],
                recv_sem=recv_sems.at[3 * na + nh * j + i], device_id=(x, y, 1 - c), device_id_type=MESH)

        sends = [ici(j, i, me) for j in range(3) for i in range(1, na)]
        for cp in sends:
            cp.start()
        loads[0].wait()
        stage[0][...] = raw[:, 0, :].astype(bf16)
        first = [ici(j, 0, me) for j in range(3)]
        for cp in first:
            cp.start()
        sends += first
        locs[0].start()
        for ld, st in zip(loads[1:], locs[1:]):
            ld.wait()
            st.start()
        for j, (px, py) in enumerate(chips):
            for i in range(na):
                ici(j, i, 2 * px + py).wait_recv()
                if i < nh:
                    fwd = d2d(j, i, c)
                    fwd.start()
                    sends.append(fwd)
        for j in range(3):
            for i in range(nh):
                d2d(j, i, 1 - c).wait_recv()
        for cp in sends:
            cp.wait_send()
        for cp in locs:
            cp.wait()

    nsem = 3 * na + 3 * nh
    return pl.pallas_call(
        body, name="gather_weights",
        in_specs=[ANY] * na, out_specs=[ANY] * na,
        out_shape=[jax.ShapeDtypeStruct((4,) + s.shape, s.dtype) for s in shards],
        scratch_shapes=[pltpu.SemaphoreType.DMA((nsem,)), pltpu.SemaphoreType.DMA((nsem,)),
                        pltpu.SemaphoreType.DMA((na,))] + [pltpu.VMEM(s.shape, s.dtype) for s in shards]
        + [pltpu.VMEM(cm.shape, cm.dtype)],
        compiler_params=pltpu.CompilerParams(has_side_effects=True, vmem_limit_bytes=VMEM_LIMIT),
    )(cm, *halved, *whole)


def _swap_halves(gs):
    na = len(gs)
    jobs = [(i, q) for i in range(na) for q in range(gs[i].shape[0])]

    def body(*refs):
        srcs, dsts = refs[:na], refs[na:2 * na]
        send_sems, recv_sems = refs[2 * na:]
        x, y, c, _ = _place()
        cps = [pltpu.make_async_remote_copy(
            src_ref=srcs[i].at[q, 1 - c], dst_ref=dsts[i].at[q], send_sem=send_sems.at[k],
            recv_sem=recv_sems.at[k], device_id=(x, y, 1 - c), device_id_type=MESH)
            for k, (i, q) in enumerate(jobs)]
        for cp in cps:
            cp.start()
        for cp in cps:
            cp.wait()

    return pl.pallas_call(
        body, name="swap_halves",
        in_specs=[ANY] * na, out_specs=[ANY] * na,
        out_shape=[jax.ShapeDtypeStruct(g.shape[0:1] + g.shape[2:], g.dtype) for g in gs],
        scratch_shapes=[pltpu.SemaphoreType.DMA((len(jobs),)), pltpu.SemaphoreType.DMA((len(jobs),))],
        compiler_params=pltpu.CompilerParams(has_side_effects=True),
    )(*gs)


def _add_split(cm_arr, g, s):
    _, _, R, Cg = g.shape
    tr = 128

    def body(sc_ref, g_ref, s_ref, b_ref, o_ref):
        p = g_ref[0, 0] + s_ref[0]
        own = None
        for q in range(4):
            blk = p[:, SHARD_COLS * q:SHARD_COLS * (q + 1)]
            b_ref[q] = blk.astype(bf16)
            mine = jnp.where(sc_ref[1] == q, blk, 0.0)
            own = mine if own is None else own + mine
        o_ref[...] = own

    return pl.pallas_call(
        body, name="add_w_in",
        grid_spec=pltpu.PrefetchScalarGridSpec(
            num_scalar_prefetch=1, grid=(R // tr,),
            in_specs=[pl.BlockSpec((1, 1, tr, Cg), lambda i, sc: (0, sc[0], i, 0)),
                      pl.BlockSpec((1, tr, Cg), lambda i, sc: (0, i, 0))],
            out_specs=[pl.BlockSpec((4, tr, SHARD_COLS), lambda i, sc: (0, i, 0)),
                       pl.BlockSpec((tr, SHARD_COLS), lambda i, sc: (i, 0))]),
        out_shape=[jax.ShapeDtypeStruct((4, R, SHARD_COLS), bf16), jax.ShapeDtypeStruct((R, SHARD_COLS), f32)],
        compiler_params=_cp(("arbitrary",)),
    )(cm_arr, g, s)


def _add_halves(c_arr, g, s, name):
    Q, _, R, Cc = g.shape
    tr = min(R, 128)

    def body(c_ref, g_ref, s_ref, b_ref, f_ref):
        p = g_ref[0, 0] + s_ref[0]
        f_ref[0] = p
        b_ref[0] = p.astype(bf16)

    blk = pl.BlockSpec((1, tr, Cc), lambda q, i, cr: (q, i, 0))
    return pl.pallas_call(
        body, name=name,
        grid_spec=pltpu.PrefetchScalarGridSpec(
            num_scalar_prefetch=1, grid=(Q, R // tr),
            in_specs=[pl.BlockSpec((1, 1, tr, Cc), lambda q, i, cr: (q, cr[0], i, 0)), blk], out_specs=[blk, blk]),
        out_shape=[jax.ShapeDtypeStruct((Q, R, Cc), bf16), jax.ShapeDtypeStruct((Q, R, Cc), f32)],
        compiler_params=_cp(("arbitrary", "arbitrary")),
    )(c_arr, g, s)


_FLIPS = [(fx, fy, fc) for fx in (0, 1) for fy in (0, 1) for fc in (0, 1)][1:]


def _sum_blocks(own, r, name, transposed=False):
    R, Cc = own.shape
    tr = min(R, 256)

    def body(own_ref, r_ref, o_ref):
        acc = own_ref[...]
        for j in range(3):
            acc = acc + r_ref[j].astype(f32)
        o_ref[...] = acc.T if transposed else acc

    return pl.pallas_call(
        body, name=name, grid=(R // tr,),
        in_specs=[pl.BlockSpec((tr, Cc), lambda i: (i, 0)), pl.BlockSpec((3, tr, Cc), lambda i: (0, i, 0))],
        out_specs=pl.BlockSpec((Cc, tr), lambda i: (0, i)) if transposed else pl.BlockSpec((tr, Cc), lambda i: (i, 0)),
        out_shape=jax.ShapeDtypeStruct((Cc, R) if transposed else (R, Cc), f32),
        compiler_params=_cp(("arbitrary",)),
    )(own, r)


def _sum_packs(me8_arr, pack, rp):
    R = pack.shape[0]

    def body(me_ref, pk_ref, rp_ref, o_ref):
        me8 = me_ref[0]
        acc = None
        for d in range(8):
            rel = d ^ me8
            term = jnp.where(rel == 0, pk_ref[...], rp_ref[jnp.maximum(rel - 1, 0)])
            acc = term if acc is None else acc + term
        o_ref[...] = acc

    return pl.pallas_call(
        body, name="sum_packs",
        grid_spec=pltpu.PrefetchScalarGridSpec(
            num_scalar_prefetch=1, grid=(1,),
            in_specs=[pl.BlockSpec((R, 128), lambda i, mr: (0, 0)), pl.BlockSpec((7, R, 128), lambda i, mr: (0, 0, 0))],
            out_specs=pl.BlockSpec((R, 128), lambda i, mr: (0, 0))),
        out_shape=jax.ShapeDtypeStruct((R, 128), f32),
        compiler_params=_cp(("arbitrary",)),
    )(me8_arr, pack, rp)


def _swap_finished(fs, pack):
    na = len(fs)
    R = pack.shape[0]

    def body(*refs):
        srcs, pk = refs[:na], refs[na]
        dsts, rp = refs[na + 1:2 * na + 1], refs[2 * na + 1]
        send_sems, recv_sems = refs[2 * na + 2:]
        x, y, c, _ = _place()
        cps = [pltpu.make_async_remote_copy(
            src_ref=srcs[i], dst_ref=dsts[i], send_sem=send_sems.at[i], recv_sem=recv_sems.at[i],
            device_id=(x, y, 1 - c), device_id_type=MESH) for i in range(na)]
        cps += [pltpu.make_async_remote_copy(
            src_ref=pk, dst_ref=rp.at[k], send_sem=send_sems.at[na + k], recv_sem=recv_sems.at[na + k],
            device_id=(x ^ fx, y ^ fy, c ^ fc), device_id_type=MESH) for k, (fx, fy, fc) in enumerate(_FLIPS)]
        for cp in cps:
            cp.start()
        for cp in cps:
            cp.wait()

    return pl.pallas_call(
        body, name="swap_finished",
        in_specs=[ANY] * (na + 1), out_specs=[ANY] * (na + 1),
        out_shape=[jax.ShapeDtypeStruct(f.shape, f.dtype) for f in fs] + [jax.ShapeDtypeStruct((7, R, 128), f32)],
        scratch_shapes=[pltpu.SemaphoreType.DMA((na + 7,)), pltpu.SemaphoreType.DMA((na + 7,))],
        compiler_params=pltpu.CompilerParams(has_side_effects=True),
    )(*fs, pack)


def _adamw_halves(c_arr, mine, peer, w, m, v, name):
    _, R, Cc = w.shape
    tr = min(R, 256)

    def body(c_ref, mine_ref, peer_ref, w_ref, m_ref, v_ref, g_ref, d_ref, m2_ref, v2_ref):
        g = jnp.where(pl.program_id(0) == c_ref[0], mine_ref[...], peer_ref[...])
        delta, m2, v2 = _adam_math(g, w_ref[0], m_ref[0], v_ref[0])
        g_ref[0] = g
        d_ref[0] = delta
        m2_ref[0] = m2
        v2_ref[0] = v2

    half = pl.BlockSpec((tr, Cc), lambda hh, i, cr: (i, 0))
    full = pl.BlockSpec((1, tr, Cc), lambda hh, i, cr: (hh, i, 0))
    return pl.pallas_call(
        body, name=name,
        grid_spec=pltpu.PrefetchScalarGridSpec(
            num_scalar_prefetch=1, grid=(2, R // tr), in_specs=[half, half, full, full, full], out_specs=[full] * 4),
        out_shape=[jax.ShapeDtypeStruct((2, R, Cc), f32)] * 4,
        compiler_params=_cp(("arbitrary", "arbitrary")),
    )(c_arr, mine, peer, w, m, v)


def _rows8(a):
    flat = a.reshape(-1)
    n = flat.shape[0]
    rows = -(-n // 1024) * 8
    return jnp.pad(flat, (0, rows * 128 - n)).reshape(rows, 128)


def kernel(x, meta_tokens, norm_w, w_in, conv_w, hg_lb_logits, hg_norm_w, gdn_A_log, gdn_dt_bias, gdn_norm_w, w_out, final_norm_w, loss_target, m_meta_tokens, m_norm_w, m_w_in, m_conv_w, m_hg_lb_logits, m_hg_norm_w, m_gdn_A_log, m_gdn_dt_bias, m_gdn_norm_w, m_w_out, m_final_norm_w, v_meta_tokens, v_norm_w, v_w_in, v_conv_w, v_hg_lb_logits, v_hg_norm_w, v_gdn_A_log, v_gdn_dt_bias, v_gdn_norm_w, v_w_out, v_final_norm_w):
    me = 2 * lax.axis_index("x") + lax.axis_index("y")

    g_win, g_wout, g_conv, g_meta = _gather_weights(
        jnp.transpose(w_in, (2, 0, 1)), [w_out[0].astype(bf16).reshape(2, D // 8, D)], [conv_w[0], meta_tokens])
    wout_full = g_wout.reshape(D, D)
    conv4 = jnp.transpose(g_conv, (1, 0, 2)).reshape(4, 1, 3 * HD)
    meta_full = jnp.transpose(g_meta, (1, 0, 2)).reshape(N_META, D)

    c_arr = lax.axis_index("c").reshape(1).astype(jnp.int32)

    def chip_partials(gw, g_wout_part):
        g_in2 = gw.reshape(1, 2, D // 2, PC)
        g_out4 = g_wout_part.reshape(4, 2, D // 8, D)
        s_in, s_out = _swap_halves([g_in2, g_out4])
        pb_blocks, own_in = _add_split(jnp.concatenate([c_arr, me.reshape(1).astype(jnp.int32)]), g_in2, s_in)
        pb_out, pf_out = _add_halves(c_arr, g_out4, s_out, "add_w_out")
        own_out = lax.dynamic_index_in_dim(pf_out, me, axis=0, keepdims=False)
        return [pb_blocks, pb_out], [own_in, own_out]

    (loss8, grad_x, d_meta, d_nw, d_conv, d_lb, d_hgw, d_alog, d_dtb, d_gdw, d_fw, pfs, rs) = _local_step(
        x, loss_target, g_win, wout_full, conv4, meta_full, norm_w, hg_lb_logits, hg_norm_w, gdn_A_log, gdn_dt_bias,
        gdn_norm_w, final_norm_w, chip_partials)

    pack = jnp.concatenate([
        loss8, d_nw[0].reshape(8, 128), d_lb.reshape(8, 128), d_hgw, _rows8(d_alog[0, :H]), _rows8(d_dtb[0, :H]),
        d_gdw, d_fw[0].reshape(8, 128), d_meta.reshape(128, 128), d_conv.reshape(48, 128)], axis=0)
    return _reduce_and_update(
        me, c_arr, grad_x, pfs, rs, pack, meta_tokens, norm_w, w_in, conv_w, hg_lb_logits, hg_norm_w, gdn_A_log,
        gdn_dt_bias, gdn_norm_w, w_out, final_norm_w, m_meta_tokens, m_norm_w, m_w_in, m_conv_w, m_hg_lb_logits,
        m_hg_norm_w, m_gdn_A_log, m_gdn_dt_bias, m_gdn_norm_w, m_w_out, m_final_norm_w, v_meta_tokens, v_norm_w, v_w_in,
        v_conv_w, v_hg_lb_logits, v_hg_norm_w, v_gdn_A_log, v_gdn_dt_bias, v_gdn_norm_w, v_w_out, v_final_norm_w)


def _local_step(x, loss_target, w4, wout_full, conv4, meta_full, norm_w, hg_lb_logits, hg_norm_w, gdn_A_log, gdn_dt_bias,
                gdn_norm_w, final_norm_w, chip_partials):
    head = jnp.concatenate([jnp.zeros((PAD, D), f32), meta_full], axis=0)
    target = loss_target.reshape(NB * SEQ, D)
    l0, l1 = hg_lb_logits[0:1], hg_lb_logits[1:2]
    alog = jnp.pad(gdn_A_log, ((0, 0), (0, DK - H)))
    dtb = jnp.pad(gdn_dt_bias, ((0, 0), (0, DK - H)))
    fw = final_norm_w.reshape(1, D)

    hflat, proj, ut, cv2, wbig = _in_proj(x.reshape(NB * SEQ, D), head, norm_w, w4, conv4)
    proj3 = proj.reshape(NB, TP, PC)
    cv = cv2.reshape(NB, TP, 3 * HD)
    o_hg, s_hg, o_gd, s_gd, t_gd = _mix_fwd(proj3, cv, l0, l1, alog, dtb)
    (loss8, d_ohg, d_ogd, d_zhg, d_zgd, dh_res, g_wout_part, d_hgw, d_gdw, d_fw) = _out_loss(
        o_hg.reshape(N, HD), o_gd.reshape(N, HD), proj, hg_norm_w, gdn_norm_w, wout_full, hflat, fw, target)
    d_hg, d_l0, d_l1 = _hg_bwd(proj3, l0, l1, s_hg, d_ohg.reshape(NB, TP, HD))
    d_qkv, d_conv4, d_ab, d_alog, d_dtb = _gd_bwd(cv, proj3, conv4, alog, dtb, s_gd, t_gd,
                                                  d_ogd.reshape(NB, TP, HD))
    d_hg2, d_qkv2, d_ab2 = d_hg.reshape(N, 3 * HD), d_qkv.reshape(N, 3 * HD), d_ab.reshape(N, DK)
    pieces = [(d_hg2, COL_HG), (d_zhg, COL_ZHG), (d_qkv2, COL_QKV), (d_zgd, COL_ZGD), (d_ab2, COL_AB)]
    gw = _w_grad(ut, pieces)
    pbs, pfs = chip_partials(gw, g_wout_part) if chip_partials else ([], [gw, g_wout_part])
    dh, d_nw, *rs = _in_bwd(pieces, wbig, hflat, norm_w, dh_res, pbs)

    dh3 = dh.reshape(NB, TP, D)
    grad_x = dh3[:, PAD + N_META:, :]
    d_meta = jnp.sum(dh3[:, PAD:PAD + N_META, :], axis=0)
    d_conv = d_conv4[:, 0, :]
    d_lb = jnp.concatenate([d_l0[0:1], d_l1[0:1]], axis=0)
    return loss8, grad_x, d_meta, d_nw, d_conv, d_lb, d_hgw, d_alog, d_dtb, d_gdw, d_fw, pfs, rs


def _reduce_and_update(me, c_arr, grad_x, pfs, rs, pack, meta_tokens, norm_w, w_in, conv_w, hg_lb_logits, hg_norm_w,
                       gdn_A_log, gdn_dt_bias, gdn_norm_w, w_out, final_norm_w, m_meta_tokens, m_norm_w, m_w_in, m_conv_w,
                       m_hg_lb_logits, m_hg_norm_w, m_gdn_A_log, m_gdn_dt_bias, m_gdn_norm_w, m_w_out, m_final_norm_w,
                       v_meta_tokens, v_norm_w, v_w_in, v_conv_w, v_hg_lb_logits, v_hg_norm_w, v_gdn_A_log, v_gdn_dt_bias,
                       v_gdn_norm_w, v_w_out, v_final_norm_w):
    (own_in, own_out), (r_in, r_out) = pfs, rs
    f_in = _sum_blocks(own_in, r_in, "sum_w_in", transposed=True)
    f_out = _sum_blocks(own_out, r_out, "sum_w_out")
    o_in, o_out, r_pack = _swap_finished([f_in, f_out], pack)
    me8_arr = (2 * me + lax.axis_index("c")).reshape(1).astype(jnp.int32)
    small = _sum_packs(me8_arr, pack, r_pack)

    half_out = lambda a: a[0].reshape(2, D // 8, D)
    is0 = lax.axis_index("c") == 0
    g_in = jnp.concatenate([jnp.where(is0, f_in, o_in), jnp.where(is0, o_in, f_in)], axis=1)
    to_cm = lambda a: jnp.transpose(a, (2, 0, 1))
    gi, di, mi, vi = [jnp.transpose(a, (1, 2, 0))[0] for a in _adamw_rows(
        g_in.reshape(SHARD_COLS, 1, D), to_cm(w_in), to_cm(m_w_in), to_cm(v_w_in), "adamw_w_in")]
    go, do_, mo, vo = [a.reshape(D // 4, D) for a in _adamw_halves(
        c_arr, f_out, o_out, half_out(w_out), half_out(m_w_out), half_out(v_w_out), "adamw_w_out")]

    g_meta_full = small[64:192].reshape(N_META, D)
    g_meta_loc = lax.dynamic_slice(g_meta_full, (0, me * 256), (N_META, 256))
    gm, dm, mm_, vm = _adamw([g_meta_loc], meta_tokens, m_meta_tokens, v_meta_tokens, "adamw_meta")
    g_conv_full = small[192:240].reshape(4, 1536)
    g_conv_loc = lax.dynamic_slice(g_conv_full, (0, me * 384), (4, 384))
    gc, dc, mc, vc = _adamw([g_conv_loc], conv_w[0], m_conv_w[0], v_conv_w[0], "adamw_conv")

    reps = [(norm_w, m_norm_w, v_norm_w), (hg_lb_logits, m_hg_lb_logits, v_hg_lb_logits),
            (hg_norm_w, m_hg_norm_w, v_hg_norm_w), (gdn_A_log, m_gdn_A_log, v_gdn_A_log),
            (gdn_dt_bias, m_gdn_dt_bias, v_gdn_dt_bias), (gdn_norm_w, m_gdn_norm_w, v_gdn_norm_w),
            (final_norm_w, m_final_norm_w, v_final_norm_w)]
    wp = jnp.concatenate([_rows8(t[0]) for t in reps], axis=0)
    mp = jnp.concatenate([_rows8(t[1]) for t in reps], axis=0)
    vp = jnp.concatenate([_rows8(t[2]) for t in reps], axis=0)
    gr, dr, mr, vr = _adamw([small[8:64]], wp, mp, vp, "adamw_small")

    def unpack(p):
        outs = []
        for i, t in enumerate(reps):
            n = t[0].size
            outs.append(p[8 * i:8 * i + 8].reshape(-1)[:n].reshape(t[0].shape))
        return outs

    def leaves(meta_v, conv_v, in_v, out_v, rep_p):
        nw, lb, hgw, al, db, gdw, fwv = unpack(rep_p)
        return [meta_v, nw, in_v[None], conv_v[None], lb, hgw, al, db, gdw, out_v[None], fwv]

    loss = small[0, 0]
    return (loss, grad_x, *leaves(gm, gc, gi, go, gr), *leaves(dm, dc, di, do_, dr),
            *leaves(mm_, mc, mi, mo, mr), *leaves(vm, vc, vi, vo, vr))
```

```python
import functools

import jax
import jax.numpy as jnp
from jax import lax
from jax.experimental import pallas as pl
from jax.experimental.pallas import tpu as pltpu

f32 = jnp.float32
bf16 = jnp.bfloat16
MESH = pl.DeviceIdType.MESH
ANY = pl.BlockSpec(memory_space=pl.ANY)

D = 1024
NB = 2
N_META = 16
SEQ = 2048
PAD = 48
TP = PAD + N_META + SEQ
C = 64
NCH = TP // C
N = NB * TP
H = 4
DK = 128
HD = H * DK
PC = 4224
IN_COLS = 4104
SHARD_COLS = IN_COLS // 4
COL_HG, COL_ZHG, COL_QKV, COL_ZGD, COL_AB = 0, 3 * HD, 4 * HD, 7 * HD, 8 * HD
EPS = 1e-6
ADAM_LR, ADAM_B1, ADAM_B2, ADAM_EPS, ADAM_WD, ADAM_STEP = 0.001, 0.9, 0.999, 1e-08, 0.01, 10
VMEM_LIMIT = 56 * 1024 * 1024

P_HG = dict(lvl=1, av=1, qs=1, su=1)
P_GD = dict(kk=1, inv=1, sol=1, ws=1, qk=1, o=1, su=1)


def _cp(sem=None, **kw):
    return pltpu.CompilerParams(dimension_semantics=sem, vmem_limit_bytes=VMEM_LIMIT, **kw)


_DIMS = {"nn": (((1,), (0,)), ((), ())), "nt": (((1,), (1,)), ((), ())), "tn": (((0,), (0,)), ((), ()))}


def _split(x):
    hi = x.astype(bf16)
    return hi, (x - hi.astype(f32)).astype(bf16)


def _dg(a, b, kind, passes):
    d = lambda x, y: lax.dot_general(x, y, _DIMS[kind], preferred_element_type=f32)
    if passes == 1:
        return d(a.astype(bf16), b.astype(bf16))
    ah, al = _split(a)
    bh, bl = _split(b)
    return d(ah, bh) + d(ah, bl) + d(al, bh)


@functools.partial(jax.custom_vjp, nondiff_argnums=(2, 3))
def mmx(a, b, kind, passes):
    return _dg(a, b, kind, passes)


def _mmx_fwd(a, b, kind, passes):
    return _dg(a, b, kind, passes), (a, b)


def _mmx_bwd(kind, passes, res, g):
    a, b = res
    if kind == "nn":
        return _dg(g, b, "nt", passes), _dg(a, g, "tn", passes)
    if kind == "nt":
        return _dg(g, b, "nn", passes), _dg(g, a, "tn", passes)
    return _dg(b, g, "nt", passes), _dg(a, g, "nn", passes)


mmx.defvjp(_mmx_fwd, _mmx_bwd)


def _mask_dg(mask, x):
    xh, xl = _split(x)
    return jnp.dot(jnp.concatenate([mask, mask], axis=1), jnp.concatenate([xh, xl], axis=0), preferred_element_type=f32)


@functools.partial(jax.custom_vjp, nondiff_argnums=(2,))
def mask_mm(mask, x, bwd_passes):
    return _mask_dg(mask, x)


def _mask_fwd(mask, x, bwd_passes):
    return _mask_dg(mask, x), mask


def _mask_bwd(bwd_passes, mask, g):
    d = lambda y: lax.dot_general(mask, y, _DIMS["tn"], preferred_element_type=f32)
    if bwd_passes == 1:
        return None, d(g.astype(bf16))
    gh, gl = _split(g)
    return None, d(gh) + d(gl)


mask_mm.defvjp(_mask_fwd, _mask_bwd)


def bdot(a, b):
    return jnp.dot(a.astype(bf16), b.astype(bf16), preferred_element_type=f32)


def bdot_nt(a, b):
    return lax.dot_general(a.astype(bf16), b.astype(bf16), _DIMS["nt"], preferred_element_type=f32)


def bdot_tn(a, b):
    return lax.dot_general(a.astype(bf16), b.astype(bf16), _DIMS["tn"], preferred_element_type=f32)


def _iota2(n, m):
    return lax.broadcasted_iota(jnp.int32, (n, m), 0), lax.broadcasted_iota(jnp.int32, (n, m), 1)


sigmoid = jax.nn.sigmoid


def silu(x):
    return x * sigmoid(x)


def softplus(x):
    return jnp.maximum(x, 0.0) + jnp.log(1.0 + jnp.exp(-jnp.abs(x)))


def rmsnorm(x, w):
    return x * lax.rsqrt(jnp.mean(x * x, axis=-1, keepdims=True) + EPS) * w


def hg_masks():
    t, r = _iota2(C, C)
    mats = [r <= t, r > t]
    lvl = []
    for l in range(1, 7):
        sz = 1 << l
        half = sz >> 1
        seg_t = t >> l
        upper_t = (t & (sz - 1)) >= half
        mid_t = seg_t * sz + half - 1
        mats.append((upper_t & (r > mid_t) & (r <= t)) | ((~upper_t) & (r > t) & (r <= mid_t)))
        lvl.append(((seg_t == (r >> l)) & upper_t & ((r & (sz - 1)) < half)).astype(f32))
    stk = jnp.concatenate([m.astype(bf16) for m in mats], axis=0)
    return stk, lvl, (t == r).astype(f32)


def _head(a, h):
    return a[:, h * DK:(h + 1) * DK]


def _run(*gens):
    results = [None] * len(gens)
    live = list(range(len(gens)))
    while live:
        for i in list(live):
            try:
                next(gens[i])
            except StopIteration as e:
                results[i] = e.value
                live.remove(i)
    return results


def hg_chunk(St, ps, l0, l1):
    return _run(hg_stages(St, ps, l0, l1))[0]


def gd_chunk(S, cs, abs_, alog, dtb, t_saved=None):
    return _run(gd_stages(S, cs, abs_, alog, dtb, t_saved))[0]


def mix_chunk(St, ps, l0, l1, S, cs, abs_, alog, dtb):
    (sn_h, o_h), (sn_g, o_g, t_pack) = _run(hg_stages(St, ps, l0, l1), gd_stages(S, cs, abs_, alog, dtb))
    return sn_h, o_h, sn_g, o_g, t_pack


def hg_stages(St, ps, l0, l1):
    m = jnp.maximum(l0, l1)
    e0 = jnp.exp(l0 - m)
    e1 = jnp.exp(l1 - m)
    lb = e0 / (e0 + e1)
    stk, lvl, eye = hg_masks()
    msk = [eye] + lvl
    trow = lax.broadcasted_iota(jnp.int32, (C, 1), 0)
    upper = [(trow & ((1 << l) - 1)) >= (1 << (l - 1)) for l in range(1, 7)]
    qs, ks, vs, qG, kR, eGl = [], [], [], [], [], []
    for p in ps:
        pq, pf, v = p[:, 0:HD], p[:, HD:2 * HD], p[:, 2 * HD:3 * HD]
        q = silu(pq)
        f = lb + (1.0 - lb) * sigmoid(pf)
        k = 1.0 - f
        logf = jnp.log(f)
        Dm = mask_mm(stk, logf, 1)
        z = [jnp.where(up, q, k) * jnp.exp(Dm[(2 + i) * C:(3 + i) * C]) for i, up in enumerate(upper)]
        qs.append([q] + z)
        ks.append([k] + z)
        vs.append(v)
        qG.append(q * jnp.exp(Dm[0:C]))
        kR.append(k * jnp.exp(Dm[C:2 * C]))
        eGl.append(jnp.exp(jnp.sum(logf, axis=0, keepdims=True)))
    yield
    units = [(b, h) for b in range(len(ps)) for h in range(H)]
    parts = []
    for i in range(7):
        parts.append([msk[i] * mmx(_head(qs[b][i], h), _head(ks[b][i], h), "nt", P_HG["lvl"]) for b, h in units])
        yield
    A = [functools.reduce(lambda x, y: x + y, [parts[i][n] for i in range(7)]) for n in range(len(units))]
    qS = [mmx(_head(qG[b], h), St[n], "nt", P_HG["qs"]) for n, (b, h) in enumerate(units)]
    Sn = [St[n] * _head(eGl[b], h) + mmx(_head(vs[b], h), _head(kR[b], h), "tn", P_HG["su"])
          for n, (b, h) in enumerate(units)]
    yield
    outs = [mmx(A[n], _head(vs[b], h), "nn", P_HG["av"]) + qS[n] for n, (b, h) in enumerate(units)]
    return tuple(Sn), tuple(jnp.concatenate(outs[b * H:(b + 1) * H], axis=1) for b in range(len(ps)))


@jax.custom_vjp
def use_inverse(A, T):
    return T


def _use_inverse_fwd(A, T):
    return T, T


def _use_inverse_bwd(T, g):
    return -_dg(T, _dg(g, T, "nt", P_GD["inv"]), "tn", P_GD["inv"]), jnp.zeros_like(T)


use_inverse.defvjp(_use_inverse_fwd, _use_inverse_bwd)


def gd_stages(S, cs, abs_, alog, dtb, t_saved=None):
    t, r = _iota2(C, C)
    tri = (r <= t).astype(bf16)
    ups = (r > t).astype(bf16)
    lane = lax.broadcasted_iota(jnp.int32, (1, DK), 1)
    subl = lax.broadcasted_iota(jnp.int32, (8, 1), 0)
    eye = (t == r).astype(f32)
    strict = (r < t).astype(f32)
    bd = ((t >> 4) == (r >> 4)).astype(f32)
    qa, ka, va, b4, gam4, grev4, gam4T, glast4 = [], [], [], [], [], [], [], []
    for c, ab in zip(cs, abs_):
        qa.append(silu(c[:, 0:HD]))
        ka.append(silu(c[:, HD:2 * HD]))
        va.append(silu(c[:, 2 * HD:3 * HD]))
        g4 = -jnp.exp(alog) * softplus(ab + dtb)
        b4.append(sigmoid(ab))
        gam4.append(mask_mm(tri, g4, 2))
        grev4.append(mask_mm(ups, g4, 2))
        gam4T.append(gam4[-1].T)
        glast4.append(jnp.sum(g4, axis=0, keepdims=True))
    yield
    units = [(b, h) for b in range(len(cs)) for h in range(H)]
    nu = range(len(units))
    inv = lambda a, b: [mmx(a[n], b[n], "nn", P_GD["inv"]) for n in nu]
    v = [_head(va[b], h) for b, h in units]
    q = [_head(qa[b], h) for b, h in units]
    k = [_head(ka[b], h) for b, h in units]
    q = [x * lax.rsqrt(jnp.sum(x * x, -1, keepdims=True) + EPS) * (DK ** -0.5) for x in q]
    k = [x * lax.rsqrt(jnp.sum(x * x, -1, keepdims=True) + EPS) for x in k]
    oh = [(lane == h).astype(f32) for h in range(H)]
    gam_c = [jnp.sum(gam4[b] * oh[h], -1, keepdims=True) for b, h in units]
    grev_c = [jnp.sum(grev4[b] * oh[h], -1, keepdims=True) for b, h in units]
    beta = [jnp.sum(b4[b] * (lane == h + H).astype(f32), -1, keepdims=True) for b, h in units]
    glast = [jnp.sum(glast4[b] * oh[h], -1, keepdims=True) for b, h in units]
    gam_r = [jnp.sum(gam4T[b][0:8, :] * (subl == h).astype(f32), axis=0, keepdims=True) for b, h in units]
    dec = [jnp.exp(jnp.where(r <= t, gam_c[n] - gam_r[n], -1e30)) for n in nu]
    egam = [jnp.exp(gam_c[n]) for n in nu]
    kk = [mmx(k[n], k[n], "nt", P_GD["kk"]) for n in nu]
    qk = [mmx(q[n], k[n], "nt", P_GD["qk"]) * dec[n] for n in nu]
    yield
    A = [beta[n] * kk[n] * dec[n] * strict for n in nu]
    Dg = [A[n] * bd for n in nu]
    L = [A[n] - Dg[n] for n in nu]
    if t_saved is None:
        ImD = [eye - Dg[n] for n in nu]
        D2 = inv(Dg, Dg)
        yield
        P1 = inv(ImD, [eye + x for x in D2])
        D4 = inv(D2, D2)
        yield
        P2 = inv(P1, [eye + x for x in D4])
        D8 = inv(D4, D4)
        yield
        M = inv(P2, [eye + x for x in D8])
        yield
        Nn = inv(M, L)
        yield
        N2 = inv(Nn, Nn)
        yield
        T1 = inv([eye - x for x in Nn], [eye + x for x in N2])
        yield
        Tinv = inv(T1, M)
        yield
    else:
        Tinv = [use_inverse(A[n], t_saved[b][:, h * DK:h * DK + C]) for n, (b, h) in enumerate(units)]
    rhs = [jnp.concatenate([beta[n] * v[n], (beta[n] * egam[n]) * k[n]], axis=1) for n in nu]
    sol = [mmx(Tinv[n], rhs[n], "nn", P_GD["sol"]) for n in nu]
    yield
    qwS = [mmx(jnp.concatenate([q[n] * egam[n], sol[n][:, DK:2 * DK]], axis=0), S[n], "nn", P_GD["ws"]) for n in nu]
    yield
    u = [sol[n][:, 0:DK] - qwS[n][C:2 * C] for n in nu]
    outs = [qwS[n][0:C] + mmx(qk[n], u[n], "nn", P_GD["o"]) for n in nu]
    Sn = [jnp.exp(glast[n]) * S[n] + mmx(k[n] * jnp.exp(grev_c[n]), u[n], "tn", P_GD["su"]) for n in nu]
    zpad = jnp.zeros((C, DK - C), f32)
    t_pack = tuple(jnp.concatenate([x for n in range(b * H, (b + 1) * H) for x in (lax.stop_gradient(Tinv[n]), zpad)],
                                   axis=1) for b in range(len(cs)))
    return tuple(Sn), tuple(jnp.concatenate(outs[b * H:(b + 1) * H], axis=1) for b in range(len(cs))), t_pack


def _chunk_index(tile_chunks, k):
    def index(i):
        chunk = tile_chunks * i + k
        b = chunk // NCH
        return jnp.maximum((SEQ // C) * b + chunk - NCH * b - 1, 0), 0
    return index


def _in_proj(xflat, head, norm_w, w4, conv4):
    tm = 384
    nck = tm // C
    W3 = 3 * HD

    def body(*refs):
        x_refs = refs[:nck]
        head_ref, nw_ref, w4_ref, cw_ref, h_ref, p_ref, ut_ref, cv_ref, w_hbm, prev, w_ref, w_sem = refs[nck:]
        i = pl.program_id(0)
        w_out = pltpu.make_async_copy(w_ref, w_hbm, w_sem.at[0])

        @pl.when(i == 0)
        def _():
            prev[...] = jnp.zeros_like(prev)
            w_ref[PC - DK:PC, :] = jnp.zeros((DK, D), bf16)
            for q in range(4):
                w_ref[SHARD_COLS * q:SHARD_COLS * (q + 1), :] = w4_ref[q]
            w_out.start()

        blocks = []
        for k in range(nck):
            chunk = nck * i + k
            blocks.append(jnp.where(chunk - NCH * (chunk // NCH) == 0, head_ref[...], x_refs[k][...]))
        hval = jnp.concatenate(blocks, axis=0)
        h_ref[...] = hval
        u = rmsnorm(hval, nw_ref[...])
        ut_ref[...] = u.T.astype(bf16)
        p = bdot_nt(u, w_ref[...])
        p_ref[...] = p
        x = p[:, COL_QKV:COL_QKV + W3]
        xx = jnp.concatenate([prev[...], x], axis=0)
        y = cw_ref[3] * x
        for s in (1, 2, 3):
            y = y + cw_ref[3 - s] * pltpu.roll(xx, s, 0)[8:]
        row = i * tm + lax.broadcasted_iota(jnp.int32, (tm, 1), 0)
        tok = jnp.where(row >= TP, row - TP, row)
        cv_ref[...] = jnp.where(tok >= 8, y, 0.0)
        prev[...] = x[tm - 8:tm]

        @pl.when(i == N // tm - 1)
        def _():
            w_out.wait()

    return pl.pallas_call(
        body, name="in_proj", grid=(N // tm,),
        in_specs=[pl.BlockSpec((C, D), _chunk_index(nck, k)) for k in range(nck)]
        + [pl.BlockSpec((C, D), lambda i: (0, 0)), pl.BlockSpec((1, D), lambda i: (0, 0)),
           pl.BlockSpec((4, SHARD_COLS, D), lambda i: (0, 0, 0), pipeline_mode=pl.Buffered(1)),
           pl.BlockSpec((4, 1, W3), lambda i: (0, 0, 0))],
        out_specs=[pl.BlockSpec((tm, D), lambda i: (i, 0)), pl.BlockSpec((tm, PC), lambda i: (i, 0)),
                   pl.BlockSpec((D, tm), lambda i: (0, i)), pl.BlockSpec((tm, W3), lambda i: (i, 0)),
                   ANY],
        out_shape=[jax.ShapeDtypeStruct((N, D), f32), jax.ShapeDtypeStruct((N, PC), f32),
                   jax.ShapeDtypeStruct((D, N), bf16), jax.ShapeDtypeStruct((N, W3), f32),
                   jax.ShapeDtypeStruct((PC, D), bf16)],
        scratch_shapes=[pltpu.VMEM((8, W3), f32), pltpu.VMEM((PC, D), bf16), pltpu.SemaphoreType.DMA((1,))],
        compiler_params=_cp(("arbitrary",)),
    )(*[xflat] * nck, head, norm_w, w4, conv4)


NU = NB * H
_REV = lambda c: NCH - 1 - c
_FWD = lambda c: c


def _tok_spec(w, ix, col=0):
    return pl.BlockSpec((NB, C, w), lambda c: (0, ix(c), col))


def _state_spec(ix):
    return pl.BlockSpec((NB, 1, H, DK, DK), lambda c: (0, ix(c), 0, 0, 0))


def _row_spec(w):
    return pl.BlockSpec((1, w), lambda c: (0, 0))


def _rows(ref):
    return tuple(ref[b] for b in range(NB))


def _hg_extra_specs(ix):
    return [_row_spec(HD), _row_spec(HD)]


def _gd_extra_specs(ix):
    return [_tok_spec(DK, ix, COL_AB // DK), _row_spec(DK), _row_spec(DK)]


def _mix_fwd(proj3, cv, l0, l1, alog, dtb):
    def body(p_ref, c_ref, ab_ref, l0_ref, l1_ref, al_ref, db_ref, oh_ref, sh_ref, og_ref, sg_ref, t_ref, sth, stg):
        @pl.when(pl.program_id(0) == 0)
        def _():
            sth[...] = jnp.zeros_like(sth)
            stg[...] = jnp.zeros_like(stg)

        Sh = tuple(sth[n] for n in range(NU))
        Sg = tuple(stg[n] for n in range(NU))
        for n in range(NU):
            sh_ref[n // H, 0, n % H] = Sh[n]
            sg_ref[n // H, 0, n % H] = Sg[n]
        snh, oh, sng, og, tp = mix_chunk(Sh, _rows(p_ref), l0_ref[...], l1_ref[...],
                                         Sg, _rows(c_ref), _rows(ab_ref), al_ref[...], db_ref[...])
        for n in range(NU):
            sth[n] = snh[n]
            stg[n] = sng[n]
        for b in range(NB):
            oh_ref[b] = oh[b]
            og_ref[b] = og[b]
            t_ref[b] = tp[b]

    tok = jax.ShapeDtypeStruct((NB, TP, HD), f32)
    st = jax.ShapeDtypeStruct((NB, NCH, H, DK, DK), f32)
    return pl.pallas_call(
        body, name="mix_fwd", grid=(NCH,),
        in_specs=[_tok_spec(3 * HD, _FWD), _tok_spec(3 * HD, _FWD), _tok_spec(DK, _FWD, COL_AB // DK),
                  _row_spec(HD), _row_spec(HD), _row_spec(DK), _row_spec(DK)],
        out_specs=[_tok_spec(HD, _FWD), _state_spec(_FWD), _tok_spec(HD, _FWD), _state_spec(_FWD), _tok_spec(HD, _FWD)],
        out_shape=[tok, st, tok, st, tok],
        scratch_shapes=[pltpu.VMEM((NU, DK, DK), f32), pltpu.VMEM((NU, DK, DK), f32)],
        compiler_params=_cp(("arbitrary",)),
    )(proj3, cv, proj3, l0, l1, alog, dtb)


def _hg_bwd(proj3, l0, l1, s_saved, do):
    def body(p_ref, l0_ref, l1_ref, s_ref, do_ref, dp_ref, dl0_ref, dl1_ref, dst):
        @pl.when(pl.program_id(0) == 0)
        def _():
            dst[...] = jnp.zeros_like(dst)
            dl0_ref[...] = jnp.zeros_like(dl0_ref)
            dl1_ref[...] = jnp.zeros_like(dl1_ref)

        S = tuple(s_ref[n // H, 0, n % H] for n in range(NU))
        _, vjp = jax.vjp(hg_chunk, S, _rows(p_ref), l0_ref[...], l1_ref[...])
        dS, dp, dl0, dl1 = vjp((tuple(dst[n] for n in range(NU)), _rows(do_ref)))
        for n in range(NU):
            dst[n] = dS[n]
        for b in range(NB):
            dp_ref[b] = dp[b].astype(bf16)
        dl0_ref[...] += jnp.broadcast_to(dl0, (8, HD))
        dl1_ref[...] += jnp.broadcast_to(dl1, (8, HD))

    acc = pl.BlockSpec((8, HD), lambda c: (0, 0))
    return pl.pallas_call(
        body, name="hg_bwd", grid=(NCH,),
        in_specs=[_tok_spec(3 * HD, _REV)] + _hg_extra_specs(_REV) + [_state_spec(_REV), _tok_spec(HD, _REV)],
        out_specs=[_tok_spec(3 * HD, _REV), acc, acc],
        out_shape=[jax.ShapeDtypeStruct((NB, TP, 3 * HD), bf16), jax.ShapeDtypeStruct((8, HD), f32),
                   jax.ShapeDtypeStruct((8, HD), f32)],
        scratch_shapes=[pltpu.VMEM((NU, DK, DK), f32)],
        compiler_params=_cp(("arbitrary",)),
    )(proj3, l0, l1, s_saved, do)


def _gd_bwd(cv, proj3, conv4, alog, dtb, s_saved, t_saved, do):
    def body(c_ref, ab_ref, al_ref, db_ref, s_ref, t_ref, do_ref, x0_ref, x1_ref, x2_ref, w_ref,
             dx_ref, dw_ref, dab_ref, dal_ref, ddb_ref, dst, nxt):
        @pl.when(pl.program_id(0) == 0)
        def _():
            dst[...] = jnp.zeros_like(dst)
            dal_ref[...] = jnp.zeros_like(dal_ref)
            ddb_ref[...] = jnp.zeros_like(ddb_ref)
            dw_ref[...] = jnp.zeros_like(dw_ref)
            nxt[...] = jnp.zeros_like(nxt)

        S = tuple(s_ref[n // H, 0, n % H] for n in range(NU))
        t_rows = _rows(t_ref)
        fn = lambda *a: gd_chunk(*a, t_saved=t_rows)[0:2]
        _, vjp = jax.vjp(fn, S, _rows(c_ref), _rows(ab_ref), al_ref[...], db_ref[...])
        dS, dc, dab, dal, ddb = vjp((tuple(dst[n] for n in range(NU)), _rows(do_ref)))
        for n in range(NU):
            dst[n] = dS[n]
        for b in range(NB):
            dab_ref[b] = dab[b].astype(bf16)
        dal_ref[...] += jnp.broadcast_to(dal, (8, DK))
        ddb_ref[...] += jnp.broadcast_to(ddb, (8, DK))

        rows8 = lambda p: sum(p[8 * i:8 * i + 8] for i in range(1, C // 8)) + p[0:8]
        for j, x_ref in enumerate((x0_ref, x1_ref, x2_ref)):
            sl = slice(j * HD, (j + 1) * HD)
            dws = [None] * 4
            for b in range(NB):
                x = x_ref[b]
                g = dc[b][:, sl]
                gg = jnp.concatenate([g, nxt[b, :, sl]], axis=0)
                dx = w_ref[3, :, sl] * g
                parts = [x * g]
                for s in (1, 2, 3):
                    gs = pltpu.roll(gg, C + 8 - s, 0)[:C]
                    dx = dx + w_ref[3 - s, :, sl] * gs
                    parts.append(x * gs)
                nxt[b, :, sl] = g[:8]
                dx_ref[b, :, sl] = dx.astype(bf16)
                dws = [p if a is None else a + p for a, p in zip(dws, parts)]
            for s in range(4):
                dw_ref[3 - s, :, sl] += rows8(dws[s])

        @pl.when(pl.program_id(0) == NCH - 1)
        def _():
            for s in range(4):
                dw_ref[s] = jnp.broadcast_to(jnp.sum(dw_ref[s], axis=0, keepdims=True), (8, 3 * HD))

    acc = pl.BlockSpec((8, DK), lambda c: (0, 0))
    return pl.pallas_call(
        body, name="gd_bwd", grid=(NCH,),
        in_specs=[_tok_spec(3 * HD, _REV)] + _gd_extra_specs(_REV)
        + [_state_spec(_REV), _tok_spec(HD, _REV), _tok_spec(HD, _REV)]
        + [_tok_spec(HD, _REV, COL_QKV // HD + j) for j in range(3)]
        + [pl.BlockSpec((4, 1, 3 * HD), lambda c: (0, 0, 0))],
        out_specs=[_tok_spec(3 * HD, _REV), pl.BlockSpec((4, 8, 3 * HD), lambda c: (0, 0, 0)),
                   _tok_spec(DK, _REV), acc, acc],
        out_shape=[jax.ShapeDtypeStruct((NB, TP, 3 * HD), bf16), jax.ShapeDtypeStruct((4, 8, 3 * HD), f32),
                   jax.ShapeDtypeStruct((NB, TP, DK), bf16),
                   jax.ShapeDtypeStruct((8, DK), f32), jax.ShapeDtypeStruct((8, DK), f32)],
        scratch_shapes=[pltpu.VMEM((NU, DK, DK), f32), pltpu.VMEM((NB, 8, 3 * HD), f32)],
        compiler_params=_cp(("arbitrary",)),
    )(cv, proj3, alog, dtb, s_saved, t_saved, do, proj3, proj3, proj3, conv4)


def _out_loss(o_hg, o_gd, proj, hgw, gdw, wout, hflat, fw, target):
    tm = 384

    def body(ohg_ref, ogd_ref, zhg_ref, zgd_ref, hgw_ref, gdw_ref, wo_ref, h_ref, fw_ref, *refs):
        tg_refs = refs[:tm // C]
        (loss_ref, dohg_ref, dogd_ref, dzhg_ref, dzgd_ref, dh_ref, dwo_ref, dhgw_ref, dgdw_ref, dfw_ref) = refs[tm // C:]
        i = pl.program_id(0)

        @pl.when(i == 0)
        def _():
            for r in (loss_ref, dwo_ref, dhgw_ref, dgdw_ref, dfw_ref):
                r[...] = jnp.zeros_like(r)

        row = i * tm + lax.broadcasted_iota(jnp.int32, (tm, 1), 0)
        tok = jnp.where(row >= TP, row - TP, row)
        valid = (tok >= PAD + N_META).astype(f32)
        hval = h_ref[...]
        tgt = jnp.concatenate([r[...] for r in tg_refs], axis=0)

        mixers = ((ohg_ref, zhg_ref, hgw_ref[...]), (ogd_ref, zgd_ref, gdw_ref[...]))
        saved, ys = [], []
        for o_ref, z_ref, w in mixers:
            for hh in range(H):
                sl = slice(hh * DK, (hh + 1) * DK)
                o, z = o_ref[:, sl], z_ref[:, sl]
                r = lax.rsqrt(jnp.mean(o * o, axis=-1, keepdims=True) + EPS)
                n = o * r
                sg = sigmoid(z)
                ws = w * (z * sg)
                saved.append((r, n, sg, z, ws, w))
                ys.append(n * ws)
        y = jnp.concatenate(ys, axis=-1)
        h2 = hval + bdot(y, wo_ref[...])
        r2 = lax.rsqrt(jnp.mean(h2 * h2, axis=-1, keepdims=True) + EPS)
        n2 = h2 * r2
        fwv = fw_ref[...]
        err = (n2 * fwv - tgt) * valid
        loss = (0.5 / D) * jnp.sum(err * err)
        dyf = err * (1.0 / D)
        dn2 = dyf * fwv
        dout = r2 * (dn2 - n2 * jnp.mean(dn2 * n2, axis=-1, keepdims=True))
        dh_ref[...] = dout
        dy = bdot_nt(dout, wo_ref[...])
        dwo_ref[...] += bdot_tn(y, dout)
        dws = []
        for mi, (do_ref, dz_ref) in enumerate(((dohg_ref, dzhg_ref), (dogd_ref, dzgd_ref))):
            dw = jnp.zeros((1, DK), f32)
            for hh in range(H):
                sl = slice(hh * DK, (hh + 1) * DK)
                r, n, sg, z, ws, w = saved[mi * H + hh]
                dyh = dy[:, mi * HD + hh * DK:mi * HD + (hh + 1) * DK]
                t = dyh * n
                dw = dw + jnp.sum(t * (z * sg), axis=0, keepdims=True)
                dz_ref[:, sl] = (t * w * (sg * (1.0 + z * (1.0 - sg)))).astype(bf16)
                dn = dyh * ws
                do_ref[:, sl] = r * (dn - n * jnp.mean(dn * n, axis=-1, keepdims=True))
            dws.append(dw)
        loss_ref[...] += jnp.broadcast_to(loss, (8, DK))
        dhgw_ref[...] += jnp.broadcast_to(dws[0], (8, DK))
        dgdw_ref[...] += jnp.broadcast_to(dws[1], (8, DK))
        dfw_ref[...] += jnp.broadcast_to(jnp.sum(dyf * n2, axis=0, keepdims=True), (8, D))

    row = lambda w: pl.BlockSpec((tm, w), lambda i: (i, 0))
    whole = lambda r, w: pl.BlockSpec((r, w), lambda i: (0, 0))
    col = lambda c0: pl.BlockSpec((tm, HD), lambda i: (i, c0 // HD))

    tgt_spec = lambda k: pl.BlockSpec((C, D), _chunk_index(tm // C, k))
    return pl.pallas_call(
        body, name="out_loss", grid=(N // tm,),
        in_specs=[row(HD), row(HD), col(COL_ZHG), col(COL_ZGD),
                  whole(1, DK), whole(1, DK), whole(D, D), row(D), whole(1, D)] + [tgt_spec(k) for k in range(tm // C)],
        out_specs=[whole(8, DK), row(HD), row(HD), row(HD), row(HD), row(D), whole(D, D),
                   whole(8, DK), whole(8, DK), whole(8, D)],
        out_shape=[jax.ShapeDtypeStruct((8, DK), f32)] + [jax.ShapeDtypeStruct((N, HD), f32)] * 2
        + [jax.ShapeDtypeStruct((N, HD), bf16)] * 2
        + [jax.ShapeDtypeStruct((N, D), f32), jax.ShapeDtypeStruct((D, D), f32),
           jax.ShapeDtypeStruct((8, DK), f32), jax.ShapeDtypeStruct((8, DK), f32), jax.ShapeDtypeStruct((8, D), f32)],
        compiler_params=_cp(("arbitrary",)),
    )(o_hg, o_gd, proj, proj, hgw, gdw, wout, hflat, fw, *[target] * (tm // C))


def _in_bwd(pieces, wbig, hflat, norm_w, dh_res, pbs):
    tm = 384
    nsteps = N // tm
    np_ = len(pieces)
    na = len(pbs)
    offs = [c0 for _, c0 in pieces]
    widths = [d.shape[1] for d, _ in pieces]

    def body(*refs):
        d_refs = refs[:np_]
        w_hbm, h_ref, nw_ref, dhr_ref = refs[np_:np_ + 4]
        w_ref, w_sem = refs[-2], refs[-1]
        srcs = refs[np_ + 4:np_ + 4 + na]
        dh_ref, dnw_ref = refs[np_ + 4 + na:np_ + 6 + na]
        dsts = refs[np_ + 6 + na:np_ + 6 + 2 * na]
        sems = refs[np_ + 6 + 2 * na:]
        i = pl.program_id(0)

        def copies():
            if not na:
                return []
            x, y, c, chips = _place()
            return [pltpu.make_async_remote_copy(
                src_ref=srcs[a].at[2 * px + py], dst_ref=dsts[a].at[j], send_sem=sems[0].at[na * j + a],
                recv_sem=sems[1].at[na * j + a], device_id=(px, py, c), device_id_type=MESH)
                for j, (px, py) in enumerate(chips) for a in range(na)]

        @pl.when(i == 0)
        def _():
            dnw_ref[...] = jnp.zeros_like(dnw_ref)
            for cp in copies():
                cp.start()
            w_in = pltpu.make_async_copy(w_hbm, w_ref, w_sem.at[0])
            w_in.start()
            w_in.wait()

        du = jnp.zeros((tm, D), f32)
        for d_ref, off, wd in zip(d_refs, offs, widths):
            du = du + bdot(d_ref[...], w_ref[off:off + wd, :])
        _, vjp = jax.vjp(rmsnorm, h_ref[...], nw_ref[...])
        dh, dnw = vjp(du)
        dh_ref[...] = dh + dhr_ref[...]
        dnw_ref[...] += jnp.broadcast_to(dnw, (8, D))

        @pl.when(i == nsteps - 1)
        def _():
            for cp in copies():
                cp.wait()

    row = lambda w: pl.BlockSpec((tm, w), lambda i: (i, 0))
    return pl.pallas_call(
        body, name="in_bwd", grid=(nsteps,),
        in_specs=[row(w) for w in widths]
        + [ANY, row(D), pl.BlockSpec((1, D), lambda i: (0, 0)), row(D)] + [ANY] * na,
        out_specs=[row(D), pl.BlockSpec((8, D), lambda i: (0, 0))] + [ANY] * na,
        out_shape=[jax.ShapeDtypeStruct((N, D), f32), jax.ShapeDtypeStruct((8, D), f32)]
        + [jax.ShapeDtypeStruct((3,) + p.shape[1:], p.dtype) for p in pbs],
        scratch_shapes=([pltpu.SemaphoreType.DMA((3 * na,)), pltpu.SemaphoreType.DMA((3 * na,))] if na else [])
        + [pltpu.VMEM((PC, D), bf16), pltpu.SemaphoreType.DMA((1,))],
        compiler_params=_cp(("arbitrary",)),
    )(*[d for d, _ in pieces], wbig, hflat, norm_w, dh_res, *pbs)


def _w_grad(ut, pieces):
    bw = HD
    steps = PC // bw + (PC % bw > 0)
    plan = []
    for d, c0 in pieces:
        w = d.shape[1]
        plan.append((c0 // bw, max(1, w // bw), min(w, bw)))
    late = [i for i, (s0, nb, _) in enumerate(plan) if nb == 1 and s0 > 0]
    npc = len(pieces)

    def body(u_ref, *refs):
        d_refs, o_ref = refs[:npc], refs[npc]
        bufs, sems = refs[npc + 1:-1], refs[-1]
        j = pl.program_id(0)
        cps = {i: pltpu.make_async_copy(d_refs[i], bufs[n], sems.at[n]) for n, i in enumerate(late)}

        @pl.when(j == 0)
        def _():
            for i in late:
                cps[i].start()

        for i, (d_ref, (s0, nb, w)) in enumerate(zip(d_refs, plan)):
            @pl.when((j >= s0) & (j < s0 + nb))
            def _(i=i, d_ref=d_ref, w=w):
                if i in cps:
                    cps[i].wait()
                    d_ref = bufs[late.index(i)]
                r = jnp.dot(u_ref[...], d_ref[...], preferred_element_type=f32)
                if w == bw:
                    o_ref[...] = r
                else:
                    o_ref[:, 0:w] = r
                    o_ref[:, w:] = jnp.zeros((D, bw - w), f32)

    def d_spec(i, s0, nb, w):
        if i in late:
            return ANY
        return pl.BlockSpec((N, w), lambda j: (0, jnp.clip(j - s0, 0, nb - 1)))

    return pl.pallas_call(
        body, name="w_grad", grid=(steps,),
        in_specs=[pl.BlockSpec((D, N), lambda j: (0, 0), pipeline_mode=pl.Buffered(1))]
        + [d_spec(i, *p) for i, p in enumerate(plan)],
        out_specs=pl.BlockSpec((D, bw), lambda j: (0, j)),
        out_shape=jax.ShapeDtypeStruct((D, PC), f32),
        scratch_shapes=[pltpu.VMEM((N, plan[i][2]), bf16) for i in late] + [pltpu.SemaphoreType.DMA((len(late),))],
        compiler_params=_cp(("arbitrary",)),
    )(ut, *[d for d, _ in pieces])


def _adam_math(g, w, m, v):
    m2 = ADAM_B1 * m + (1.0 - ADAM_B1) * g
    v2 = ADAM_B2 * v + (1.0 - ADAM_B2) * (g * g)
    m_hat = m2 / (1.0 - ADAM_B1 ** ADAM_STEP)
    v_hat = v2 / (1.0 - ADAM_B2 ** ADAM_STEP)
    delta = -ADAM_LR * (m_hat / (jnp.sqrt(v_hat) + ADAM_EPS) + ADAM_WD * w)
    return delta, m2, v2


def _adamw(gs, w, m, v, name):
    R, Cc = w.shape
    tr = 256 if R % 256 == 0 else R
    ng = len(gs)

    def body(*refs):
        g = refs[0][...]
        for r in refs[1:ng]:
            g = g + r[...]
        w_ref, m_ref, v_ref, g_ref, d_ref, m2_ref, v2_ref = refs[ng:]
        delta, m2, v2 = _adam_math(g, w_ref[...], m_ref[...], v_ref[...])
        g_ref[...] = g
        d_ref[...] = delta
        m2_ref[...] = m2
        v2_ref[...] = v2

    spec = pl.BlockSpec((tr, Cc), lambda i: (i, 0))
    return pl.pallas_call(
        body, name=name, grid=(R // tr,),
        in_specs=[spec] * (ng + 3), out_specs=[spec] * 4,
        out_shape=[jax.ShapeDtypeStruct((R, Cc), f32)] * 4,
        compiler_params=_cp(("arbitrary",)),
    )(*gs, w, m, v)


def _adamw_rows(g, w, m, v, name):
    R, _, Cc = w.shape
    tr = R // 9

    def body(g_ref, w_ref, m_ref, v_ref, go_ref, d_ref, m2_ref, v2_ref):
        g = g_ref[...]
        delta, m2, v2 = _adam_math(g, w_ref[...], m_ref[...], v_ref[...])
        go_ref[...] = g
        d_ref[...] = delta
        m2_ref[...] = m2
        v2_ref[...] = v2

    spec = pl.BlockSpec((tr, 1, Cc), lambda i: (i, 0, 0))
    return pl.pallas_call(
        body, name=name, grid=(R // tr,),
        in_specs=[spec] * 4, out_specs=[spec] * 4,
        out_shape=[jax.ShapeDtypeStruct((R, 1, Cc), f32)] * 4,
        compiler_params=_cp(("arbitrary",)),
    )(g, w, m, v)


def _place():
    x, y, c = lax.axis_index("x"), lax.axis_index("y"), lax.axis_index("c")
    return x, y, c, [(1 - x, y), (x, 1 - y), (1 - x, 1 - y)]


def _gather_weights(cm, halved, whole):
    R, _, Cc = cm.shape
    hw = Cc // 2
    shards = [jax.ShapeDtypeStruct((R, Cc), bf16)] + list(halved) + list(whole)
    nh = 1 + len(halved)
    na = len(shards)

    def body(*refs):
        srcs, dsts = refs[:na], refs[na:2 * na]
        send_sems, recv_sems, loc_sems = refs[2 * na:2 * na + 3]
        stage = refs[2 * na + 3:3 * na + 3]
        raw = refs[3 * na + 3]
        x, y, c, chips = _place()
        me = 2 * x + y
        loads = [pltpu.make_async_copy(srcs[0], raw, loc_sems.at[0])]
        loads += [pltpu.make_async_copy(srcs[i], stage[i], loc_sems.at[i]) for i in range(1, na)]
        locs = [pltpu.make_async_copy(v, d.at[me], loc_sems.at[i]) for i, (v, d) in enumerate(zip(stage, dsts))]
        for cp in loads:
            cp.start()

        def half_of(ref, i, half):
            return ref.at[:, pl.ds(pl.multiple_of(half * hw, hw), hw)] if i == 0 else ref.at[half]

        def ici(j, i, slot):
            px, py = chips[j]
            src = half_of(stage[0] if i == 0 else srcs[i], i, c) if i < nh else srcs[i]
            dst = half_of(dsts[i].at[slot], i, c) if i < nh else dsts[i].at[slot]
            return pltpu.make_async_remote_copy(
                src_ref=src, dst_ref=dst, send_sem=send_sems.at[na * j + i], recv_sem=recv_sems.at[na * j + i],
                device_id=(px, py, c), device_id_type=MESH)

        def d2d(j, i, half):
            px, py = chips[j]
            blk = half_of(dsts[i].at[2 * px + py], i, half)
            return pltpu.make_async_remote_copy(
                src_ref=blk, dst_ref=blk, send_sem=send_sems.at[3 * na + nh * j + i],
                recv_sem=recv_sems.at[3 * na + nh * j + i], device_id=(x, y, 1 - c), device_id_type=MESH)

        sends = [ici(j, i, me) for j in range(3) for i in range(1, na)]
        for cp in sends:
            cp.start()
        loads[0].wait()
        stage[0][...] = raw[:, 0, :].astype(bf16)
        first = [ici(j, 0, me) for j in range(3)]
        for cp in first:
            cp.start()
        sends += first
        locs[0].start()
        for ld, st in zip(loads[1:], locs[1:]):
            ld.wait()
            st.start()
        for j, (px, py) in enumerate(chips):
            for i in range(na):
                ici(j, i, 2 * px + py).wait_recv()
                if i < nh:
                    fwd = d2d(j, i, c)
                    fwd.start()
                    sends.append(fwd)
        for j in range(3):
            for i in range(nh):
                d2d(j, i, 1 - c).wait_recv()
        for cp in sends:
            cp.wait_send()
        for cp in locs:
            cp.wait()

    nsem = 3 * na + 3 * nh
    return pl.pallas_call(
        body, name="gather_weights",
        in_specs=[ANY] * na, out_specs=[ANY] * na,
        out_shape=[jax.ShapeDtypeStruct((4,) + s.shape, s.dtype) for s in shards],
        scratch_shapes=[pltpu.SemaphoreType.DMA((nsem,)), pltpu.SemaphoreType.DMA((nsem,)),
                        pltpu.SemaphoreType.DMA((na,))] + [pltpu.VMEM(s.shape, s.dtype) for s in shards]
        + [pltpu.VMEM(cm.shape, cm.dtype)],
        compiler_params=pltpu.CompilerParams(has_side_effects=True, vmem_limit_bytes=VMEM_LIMIT),
    )(cm, *halved, *whole)


def _swap_halves(gs):
    na = len(gs)
    jobs = [(i, q) for i in range(na) for q in range(gs[i].shape[0])]

    def body(*refs):
        srcs, dsts = refs[:na], refs[na:2 * na]
        send_sems, recv_sems = refs[2 * na:]
        x, y, c, _ = _place()
        cps = [pltpu.make_async_remote_copy(
            src_ref=srcs[i].at[q, 1 - c], dst_ref=dsts[i].at[q], send_sem=send_sems.at[k],
            recv_sem=recv_sems.at[k], device_id=(x, y, 1 - c), device_id_type=MESH)
            for k, (i, q) in enumerate(jobs)]
        for cp in cps:
            cp.start()
        for cp in cps:
            cp.wait()

    return pl.pallas_call(
        body, name="swap_halves",
        in_specs=[ANY] * na, out_specs=[ANY] * na,
        out_shape=[jax.ShapeDtypeStruct(g.shape[0:1] + g.shape[2:], g.dtype) for g in gs],
        scratch_shapes=[pltpu.SemaphoreType.DMA((len(jobs),)), pltpu.SemaphoreType.DMA((len(jobs),))],
        compiler_params=pltpu.CompilerParams(has_side_effects=True),
    )(*gs)


def _add_split(cm_arr, g, s):
    _, _, R, Cg = g.shape
    tr = 128

    def body(sc_ref, g_ref, s_ref, b_ref, o_ref):
        p = g_ref[0, 0] + s_ref[0]
        own = None
        for q in range(4):
            blk = p[:, SHARD_COLS * q:SHARD_COLS * (q + 1)]
            b_ref[q] = blk.astype(bf16)
            mine = jnp.where(sc_ref[1] == q, blk, 0.0)
            own = mine if own is None else own + mine
        o_ref[...] = own

    return pl.pallas_call(
        body, name="add_w_in",
        grid_spec=pltpu.PrefetchScalarGridSpec(
            num_scalar_prefetch=1, grid=(R // tr,),
            in_specs=[pl.BlockSpec((1, 1, tr, Cg), lambda i, sc: (0, sc[0], i, 0)),
                      pl.BlockSpec((1, tr, Cg), lambda i, sc: (0, i, 0))],
            out_specs=[pl.BlockSpec((4, tr, SHARD_COLS), lambda i, sc: (0, i, 0)),
                       pl.BlockSpec((tr, SHARD_COLS), lambda i, sc: (i, 0))]),
        out_shape=[jax.ShapeDtypeStruct((4, R, SHARD_COLS), bf16), jax.ShapeDtypeStruct((R, SHARD_COLS), f32)],
        compiler_params=_cp(("arbitrary",)),
    )(cm_arr, g, s)


def _add_halves(c_arr, g, s, name):
    Q, _, R, Cc = g.shape
    tr = min(R, 128)

    def body(c_ref, g_ref, s_ref, b_ref, f_ref):
        p = g_ref[0, 0] + s_ref[0]
        f_ref[0] = p
        b_ref[0] = p.astype(bf16)

    blk = pl.BlockSpec((1, tr, Cc), lambda q, i, cr: (q, i, 0))
    return pl.pallas_call(
        body, name=name,
        grid_spec=pltpu.PrefetchScalarGridSpec(
            num_scalar_prefetch=1, grid=(Q, R // tr),
            in_specs=[pl.BlockSpec((1, 1, tr, Cc), lambda q, i, cr: (q, cr[0], i, 0)), blk], out_specs=[blk, blk]),
        out_shape=[jax.ShapeDtypeStruct((Q, R, Cc), bf16), jax.ShapeDtypeStruct((Q, R, Cc), f32)],
        compiler_params=_cp(("arbitrary", "arbitrary")),
    )(c_arr, g, s)


_FLIPS = [(fx, fy, fc) for fx in (0, 1) for fy in (0, 1) for fc in (0, 1)][1:]


def _sum_blocks(own, r, name, transposed=False):
    R, Cc = own.shape
    tr = min(R, 256)

    def body(own_ref, r_ref, o_ref):
        acc = own_ref[...]
        for j in range(3):
            acc = acc + r_ref[j].astype(f32)
        o_ref[...] = acc.T if transposed else acc

    return pl.pallas_call(
        body, name=name, grid=(R // tr,),
        in_specs=[pl.BlockSpec((tr, Cc), lambda i: (i, 0)), pl.BlockSpec((3, tr, Cc), lambda i: (0, i, 0))],
        out_specs=pl.BlockSpec((Cc, tr), lambda i: (0, i)) if transposed else pl.BlockSpec((tr, Cc), lambda i: (i, 0)),
        out_shape=jax.ShapeDtypeStruct((Cc, R) if transposed else (R, Cc), f32),
        compiler_params=_cp(("arbitrary",)),
    )(own, r)


def _sum_packs(me8_arr, pack, rp):
    R = pack.shape[0]

    def body(me_ref, pk_ref, rp_ref, o_ref):
        me8 = me_ref[0]
        acc = None
        for d in range(8):
            rel = d ^ me8
            term = jnp.where(rel == 0, pk_ref[...], rp_ref[jnp.maximum(rel - 1, 0)])
            acc = term if acc is None else acc + term
        o_ref[...] = acc

    return pl.pallas_call(
        body, name="sum_packs",
        grid_spec=pltpu.PrefetchScalarGridSpec(
            num_scalar_prefetch=1, grid=(1,),
            in_specs=[pl.BlockSpec((R, 128), lambda i, mr: (0, 0)), pl.BlockSpec((7, R, 128), lambda i, mr: (0, 0, 0))],
            out_specs=pl.BlockSpec((R, 128), lambda i, mr: (0, 0))),
        out_shape=jax.ShapeDtypeStruct((R, 128), f32),
        compiler_params=_cp(("arbitrary",)),
    )(me8_arr, pack, rp)


def _swap_finished(fs, pack):
    na = len(fs)
    R = pack.shape[0]

    def body(*refs):
        srcs, pk = refs[:na], refs[na]
        dsts, rp = refs[na + 1:2 * na + 1], refs[2 * na + 1]
        send_sems, recv_sems = refs[2 * na + 2:]
        x, y, c, _ = _place()
        cps = [pltpu.make_async_remote_copy(
            src_ref=srcs[i], dst_ref=dsts[i], send_sem=send_sems.at[i], recv_sem=recv_sems.at[i],
            device_id=(x, y, 1 - c), device_id_type=MESH) for i in range(na)]
        cps += [pltpu.make_async_remote_copy(
            src_ref=pk, dst_ref=rp.at[k], send_sem=send_sems.at[na + k], recv_sem=recv_sems.at[na + k],
            device_id=(x ^ fx, y ^ fy, c ^ fc), device_id_type=MESH) for k, (fx, fy, fc) in enumerate(_FLIPS)]
        for cp in cps:
            cp.start()
        for cp in cps:
            cp.wait()

    return pl.pallas_call(
        body, name="swap_finished",
        in_specs=[ANY] * (na + 1), out_specs=[ANY] * (na + 1),
        out_shape=[jax.ShapeDtypeStruct(f.shape, f.dtype) for f in fs] + [jax.ShapeDtypeStruct((7, R, 128), f32)],
        scratch_shapes=[pltpu.SemaphoreType.DMA((na + 7,)), pltpu.SemaphoreType.DMA((na + 7,))],
        compiler_params=pltpu.CompilerParams(has_side_effects=True),
    )(*fs, pack)


def _adamw_halves(c_arr, mine, peer, w, m, v, name):
    _, R, Cc = w.shape
    tr = min(R, 256)

    def body(c_ref, mine_ref, peer_ref, w_ref, m_ref, v_ref, g_ref, d_ref, m2_ref, v2_ref):
        g = jnp.where(pl.program_id(0) == c_ref[0], mine_ref[...], peer_ref[...])
        delta, m2, v2 = _adam_math(g, w_ref[0], m_ref[0], v_ref[0])
        g_ref[0] = g
        d_ref[0] = delta
        m2_ref[0] = m2
        v2_ref[0] = v2

    half = pl.BlockSpec((tr, Cc), lambda hh, i, cr: (i, 0))
    full = pl.BlockSpec((1, tr, Cc), lambda hh, i, cr: (hh, i, 0))
    return pl.pallas_call(
        body, name=name,
        grid_spec=pltpu.PrefetchScalarGridSpec(
            num_scalar_prefetch=1, grid=(2, R // tr), in_specs=[half, half, full, full, full], out_specs=[full] * 4),
        out_shape=[jax.ShapeDtypeStruct((2, R, Cc), f32)] * 4,
        compiler_params=_cp(("arbitrary", "arbitrary")),
    )(c_arr, mine, peer, w, m, v)


def _rows8(a):
    flat = a.reshape(-1)
    n = flat.shape[0]
    rows = -(-n // 1024) * 8
    return jnp.pad(flat, (0, rows * 128 - n)).reshape(rows, 128)


def kernel(x, meta_tokens, norm_w, w_in, conv_w, hg_lb_logits, hg_norm_w, gdn_A_log, gdn_dt_bias, gdn_norm_w, w_out, final_norm_w, loss_target, m_meta_tokens, m_norm_w, m_w_in, m_conv_w, m_hg_lb_logits, m_hg_norm_w, m_gdn_A_log, m_gdn_dt_bias, m_gdn_norm_w, m_w_out, m_final_norm_w, v_meta_tokens, v_norm_w, v_w_in, v_conv_w, v_hg_lb_logits, v_hg_norm_w, v_gdn_A_log, v_gdn_dt_bias, v_gdn_norm_w, v_w_out, v_final_norm_w):
    me = 2 * lax.axis_index("x") + lax.axis_index("y")

    g_win, g_wout, g_conv, g_meta = _gather_weights(
        jnp.transpose(w_in, (2, 0, 1)), [w_out[0].astype(bf16).reshape(2, D // 8, D)], [conv_w[0], meta_tokens])
    wout_full = g_wout.reshape(D, D)
    conv4 = jnp.transpose(g_conv, (1, 0, 2)).reshape(4, 1, 3 * HD)
    meta_full = jnp.transpose(g_meta, (1, 0, 2)).reshape(N_META, D)

    c_arr = lax.axis_index("c").reshape(1).astype(jnp.int32)

    def chip_partials(gw, g_wout_part):
        g_in2 = gw.reshape(1, 2, D // 2, PC)
        g_out4 = g_wout_part.reshape(4, 2, D // 8, D)
        s_in, s_out = _swap_halves([g_in2, g_out4])
        pb_blocks, own_in = _add_split(jnp.concatenate([c_arr, me.reshape(1).astype(jnp.int32)]), g_in2, s_in)
        pb_out, pf_out = _add_halves(c_arr, g_out4, s_out, "add_w_out")
        own_out = lax.dynamic_index_in_dim(pf_out, me, axis=0, keepdims=False)
        return [pb_blocks, pb_out], [own_in, own_out]

    (loss8, grad_x, d_meta, d_nw, d_conv, d_lb, d_hgw, d_alog, d_dtb, d_gdw, d_fw, pfs, rs) = _local_step(
        x, loss_target, g_win, wout_full, conv4, meta_full, norm_w, hg_lb_logits, hg_norm_w, gdn_A_log, gdn_dt_bias,
        gdn_norm_w, final_norm_w, chip_partials)

    pack = jnp.concatenate([
        loss8, d_nw[0].reshape(8, 128), d_lb.reshape(8, 128), d_hgw, _rows8(d_alog[0, :H]), _rows8(d_dtb[0, :H]),
        d_gdw, d_fw[0].reshape(8, 128), d_meta.reshape(128, 128), d_conv.reshape(48, 128)], axis=0)
    return _reduce_and_update(
        me, c_arr, grad_x, pfs, rs, pack, meta_tokens, norm_w, w_in, conv_w, hg_lb_logits, hg_norm_w, gdn_A_log,
        gdn_dt_bias, gdn_norm_w, w_out, final_norm_w, m_meta_tokens, m_norm_w, m_w_in, m_conv_w, m_hg_lb_logits,
        m_hg_norm_w, m_gdn_A_log, m_gdn_dt_bias, m_gdn_norm_w, m_w_out, m_final_norm_w, v_meta_tokens, v_norm_w, v_w_in,
        v_conv_w, v_hg_lb_logits, v_hg_norm_w, v_gdn_A_log, v_gdn_dt_bias, v_gdn_norm_w, v_w_out, v_final_norm_w)


def _local_step(x, loss_target, w4, wout_full, conv4, meta_full, norm_w, hg_lb_logits, hg_norm_w, gdn_A_log, gdn_dt_bias,
                gdn_norm_w, final_norm_w, chip_partials):
    head = jnp.concatenate([jnp.zeros((PAD, D), f32), meta_full], axis=0)
    target = loss_target.reshape(NB * SEQ, D)
    l0, l1 = hg_lb_logits[0:1], hg_lb_logits[1:2]
    alog = jnp.pad(gdn_A_log, ((0, 0), (0, DK - H)))
    dtb = jnp.pad(gdn_dt_bias, ((0, 0), (0, DK - H)))
    fw = final_norm_w.reshape(1, D)

    hflat, proj, ut, cv2, wbig = _in_proj(x.reshape(NB * SEQ, D), head, norm_w, w4, conv4)
    proj3 = proj.reshape(NB, TP, PC)
    cv = cv2.reshape(NB, TP, 3 * HD)
    o_hg, s_hg, o_gd, s_gd, t_gd = _mix_fwd(proj3, cv, l0, l1, alog, dtb)
    (loss8, d_ohg, d_ogd, d_zhg, d_zgd, dh_res, g_wout_part, d_hgw, d_gdw, d_fw) = _out_loss(
        o_hg.reshape(N, HD), o_gd.reshape(N, HD), proj, hg_norm_w, gdn_norm_w, wout_full, hflat, fw, target)
    d_hg, d_l0, d_l1 = _hg_bwd(proj3, l0, l1, s_hg, d_ohg.reshape(NB, TP, HD))
    d_qkv, d_conv4, d_ab, d_alog, d_dtb = _gd_bwd(cv, proj3, conv4, alog, dtb, s_gd, t_gd,
                                                  d_ogd.reshape(NB, TP, HD))
    d_hg2, d_qkv2, d_ab2 = d_hg.reshape(N, 3 * HD), d_qkv.reshape(N, 3 * HD), d_ab.reshape(N, DK)
    pieces = [(d_hg2, COL_HG), (d_zhg, COL_ZHG), (d_qkv2, COL_QKV), (d_zgd, COL_ZGD), (d_ab2, COL_AB)]
    gw = _w_grad(ut, pieces)
    pbs, pfs = chip_partials(gw, g_wout_part) if chip_partials else ([], [gw, g_wout_part])
    dh, d_nw, *rs = _in_bwd(pieces, wbig, hflat, norm_w, dh_res, pbs)

    dh3 = dh.reshape(NB, TP, D)
    grad_x = dh3[:, PAD + N_META:, :]
    d_meta = jnp.sum(dh3[:, PAD:PAD + N_META, :], axis=0)
    d_conv = d_conv4[:, 0, :]
    d_lb = jnp.concatenate([d_l0[0:1], d_l1[0:1]], axis=0)
    return loss8, grad_x, d_meta, d_nw, d_conv, d_lb, d_hgw, d_alog, d_dtb, d_gdw, d_fw, pfs, rs


def _reduce_and_update(me, c_arr, grad_x, pfs, rs, pack, meta_tokens, norm_w, w_in, conv_w, hg_lb_logits, hg_norm_w,
                       gdn_A_log, gdn_dt_bias, gdn_norm_w, w_out, final_norm_w, m_meta_tokens, m_norm_w, m_w_in, m_conv_w,
                       m_hg_lb_logits, m_hg_norm_w, m_gdn_A_log, m_gdn_dt_bias, m_gdn_norm_w, m_w_out, m_final_norm_w,
                       v_meta_tokens, v_norm_w, v_w_in, v_conv_w, v_hg_lb_logits, v_hg_norm_w, v_gdn_A_log, v_gdn_dt_bias,
                       v_gdn_norm_w, v_w_out, v_final_norm_w):
    (own_in, own_out), (r_in, r_out) = pfs, rs
    f_in = _sum_blocks(own_in, r_in, "sum_w_in", transposed=True)
    f_out = _sum_blocks(own_out, r_out, "sum_w_out")
    o_in, o_out, r_pack = _swap_finished([f_in, f_out], pack)
    me8_arr = (2 * me + lax.axis_index("c")).reshape(1).astype(jnp.int32)
    small = _sum_packs(me8_arr, pack, r_pack)

    half_out = lambda a: a[0].reshape(2, D // 8, D)
    is0 = lax.axis_index("c") == 0
    g_in = jnp.concatenate([jnp.where(is0, f_in, o_in), jnp.where(is0, o_in, f_in)], axis=1)
    to_cm = lambda a: jnp.transpose(a, (2, 0, 1))
    gi, di, mi, vi = [jnp.transpose(a, (1, 2, 0))[0] for a in _adamw_rows(
        g_in.reshape(SHARD_COLS, 1, D), to_cm(w_in), to_cm(m_w_in), to_cm(v_w_in), "adamw_w_in")]
    go, do_, mo, vo = [a.reshape(D // 4, D) for a in _adamw_halves(
        c_arr, f_out, o_out, half_out(w_out), half_out(m_w_out), half_out(v_w_out), "adamw_w_out")]

    g_meta_full = small[64:192].reshape(N_META, D)
    g_meta_loc = lax.dynamic_slice(g_meta_full, (0, me * 256), (N_META, 256))
    gm, dm, mm_, vm = _adamw([g_meta_loc], meta_tokens, m_meta_tokens, v_meta_tokens, "adamw_meta")
    g_conv_full = small[192:240].reshape(4, 1536)
    g_conv_loc = lax.dynamic_slice(g_conv_full, (0, me * 384), (4, 384))
    gc, dc, mc, vc = _adamw([g_conv_loc], conv_w[0], m_conv_w[0], v_conv_w[0], "adamw_conv")

    reps = [(norm_w, m_norm_w, v_norm_w), (hg_lb_logits, m_hg_lb_logits, v_hg_lb_logits),
            (hg_norm_w, m_hg_norm_w, v_hg_norm_w), (gdn_A_log, m_gdn_A_log, v_gdn_A_log),
            (gdn_dt_bias, m_gdn_dt_bias, v_gdn_dt_bias), (gdn_norm_w, m_gdn_norm_w, v_gdn_norm_w),
            (final_norm_w, m_final_norm_w, v_final_norm_w)]
    wp = jnp.concatenate([_rows8(t[0]) for t in reps], axis=0)
    mp = jnp.concatenate([_rows8(t[1]) for t in reps], axis=0)
    vp = jnp.concatenate([_rows8(t[2]) for t in reps], axis=0)
    gr, dr, mr, vr = _adamw([small[8:64]], wp, mp, vp, "adamw_small")

    def unpack(p):
        outs = []
        for i, t in enumerate(reps):
            n = t[0].size
            outs.append(p[8 * i:8 * i + 8].reshape(-1)[:n].reshape(t[0].shape))
        return outs

    def leaves(meta_v, conv_v, in_v, out_v, rep_p):
        nw, lb, hgw, al, db, gdw, fwv = unpack(rep_p)
        return [meta_v, nw, in_v[None], conv_v[None], lb, hgw, al, db, gdw, out_v[None], fwv]

    loss = small[0, 0]
    return (loss, grad_x, *leaves(gm, gc, gi, go, gr), *leaves(dm, dc, di, do_, dr),
            *leaves(mm_, mc, mi, mo, mr), *leaves(vm, vc, vi, vo, vr))
```
